```python
import jax, jax.numpy as jnp
from jax import lax
import numpy as np

D_MODEL = 1024
BATCH = 8
SEQ = 4096
DEPTH = 1

HEAD_DIM = 64
D_MIX = D_MODEL
GDN_HEADS = (D_MIX // 2) // HEAD_DIM
GDN_WIDTH = GDN_HEADS * HEAD_DIM
SWA_Q_HEADS = (D_MIX - GDN_WIDTH) // HEAD_DIM
SWA_KV_HEADS = 2
SWA_GROUP = SWA_Q_HEADS // SWA_KV_HEADS
SWA_WIDTH = SWA_Q_HEADS * HEAD_DIM
SWA_KV_WIDTH = SWA_KV_HEADS * HEAD_DIM
WINDOW = 128
CONV_WIDTH = 4
CHUNK = 64
D_FF = ((8 * D_MODEL // 3 + 255) // 256) * 256
PROJ_WIDTH = 4 * GDN_WIDTH + 2 * GDN_HEADS + SWA_WIDTH + 2 * SWA_KV_WIDTH
EPS = 1e-6

kernel_name = 'hymba_gdn_swa_adaln'


def rms_norm(x, w):
    xf = x.astype(jnp.float32)
    y = xf * lax.rsqrt(jnp.mean(xf * xf, axis=-1, keepdims=True) + EPS)
    return (y * w.astype(jnp.float32)).astype(x.dtype)


def l2_norm(x):
    xf = x.astype(jnp.float32)
    return xf * lax.rsqrt(jnp.sum(xf * xf, axis=-1, keepdims=True) + EPS)


def causal_depthwise_conv(x, w):
    return lax.conv_general_dilated(
        x, w.astype(x.dtype), window_strides=(1,), padding=[(CONV_WIDTH - 1, 0)],
        dimension_numbers=('NWC', 'WIO', 'NWC'), feature_group_count=x.shape[-1])


def gated_delta_rule_chunked(q, k, v, g, beta):
    B, T, H, Dk = q.shape
    Dv = v.shape[-1]
    N = T // CHUNK

    def chunks(t):
        t = t.reshape((B, N, CHUNK, H) + t.shape[3:])
        return jnp.moveaxis(t, 3, 1)

    q = chunks(q) * (Dk ** -0.5)
    k, v, g, beta = chunks(k), chunks(v), chunks(g), chunks(beta)
    G = jnp.cumsum(g, axis=-1)
    idx = jnp.arange(CHUNK)
    causal = idx[:, None] >= idx[None, :]
    strict = idx[:, None] > idx[None, :]
    decay = jnp.exp(jnp.where(causal, G[..., :, None] - G[..., None, :], -jnp.inf))
    kb = k * beta[..., None]
    A = jnp.where(strict, jnp.einsum('bhncd,bhnsd->bhncs', kb, k) * decay, 0.0)
    L = A + jnp.eye(CHUNK, dtype=A.dtype)
    rhs = jnp.concatenate([v * beta[..., None], kb * jnp.exp(G)[..., None]], axis=-1)
    sol = lax.linalg.triangular_solve(L, rhs, left_side=True, lower=True, unit_diagonal=True)
    u, w = sol[..., :Dv], sol[..., Dv:]
    qk = jnp.where(causal, jnp.einsum('bhncd,bhnsd->bhncs', q, k) * decay, 0.0)
    q_dec = q * jnp.exp(G)[..., None]
    k_dec = k * jnp.exp(G[..., -1:] - G)[..., None]
    chunk_decay = jnp.exp(G[..., -1])

    xs = tuple(jnp.moveaxis(t, 2, 0) for t in (u, w, qk, q_dec, k_dec, chunk_decay))

    def step(S, inp):
        u_c, w_c, qk_c, qd_c, kd_c, dec_c = inp
        v_new = u_c - jnp.einsum('bhcd,bhde->bhce', w_c, S)
        o = jnp.einsum('bhcd,bhde->bhce', qd_c, S) + jnp.einsum('bhcs,bhse->bhce', qk_c, v_new)
        S = S * dec_c[..., None, None] + jnp.einsum('bhcd,bhce->bhde', kd_c, v_new)
        return S, o

    S0 = jnp.zeros((B, H, Dk, Dv), jnp.float32)
    _, o = lax.scan(step, S0, xs)
    return jnp.transpose(o, (1, 0, 3, 2, 4)).reshape(B, T, H, Dv)


def sliding_window_attention(q, k, v, sinks):
    B, T, Hq, D = q.shape
    NB = T // WINDOW
    qb = q.reshape(B, NB, WINDOW, SWA_KV_HEADS, SWA_GROUP, D)

    def banded(t):
        t = t.reshape(B, NB, WINDOW, SWA_KV_HEADS, D)
        prev = jnp.pad(t, ((0, 0), (1, 0), (0, 0), (0, 0), (0, 0)))[:, :-1]
        return jnp.concatenate([prev, t], axis=2)

    kw, vw = banded(k), banded(v)
    s = jnp.einsum('bnqhgd,bnkhd->bhgnqk', qb, kw).astype(jnp.float32) * (D ** -0.5)
    qi = jnp.arange(WINDOW)[:, None] + WINDOW
    ki = jnp.arange(2 * WINDOW)[None, :]
    dist = (qi - ki).astype(jnp.float32)
    in_window = (qi - ki >= 0) & (qi - ki < WINDOW)
    key_exists = (jnp.arange(NB)[:, None] * WINDOW + ki - WINDOW) >= 0
    mask = in_window[None] & key_exists[:, None, :]
    slopes = 2.0 ** (-8.0 * (jnp.arange(Hq, dtype=jnp.float32) + 1.0) / Hq)
    slopes = slopes.reshape(SWA_KV_HEADS, SWA_GROUP)
    s = s - slopes[:, :, None, None, None] * dist
    s = jnp.where(mask, s, -jnp.inf)
    sink = sinks.astype(jnp.float32).reshape(SWA_KV_HEADS, SWA_GROUP)[None, :, :, None, None, None]
    m = jnp.maximum(jnp.max(s, axis=-1, keepdims=True), sink)
    p = jnp.exp(s - m)
    p = p / (jnp.sum(p, axis=-1, keepdims=True) + jnp.exp(sink - m))
    o = jnp.einsum('bhgnqk,bnkhd->bnqhgd', p.astype(v.dtype), vw)
    return o.reshape(B, T, Hq * D)


def _fwd_setup_inputs(seed: int = 0) -> dict:
    key = jax.random.key(seed)
    ks = jax.random.split(key, 20)
    f32 = jnp.float32
    nrm = lambda k, shape, scale: jax.random.normal(k, shape, f32) * scale
    gain = lambda k, shape: 1.0 + 0.1 * jax.random.normal(k, shape, f32)
    a_init = jax.random.uniform(ks[6], (DEPTH, GDN_HEADS), f32, 1.0, 16.0)
    dt = jnp.exp(jax.random.uniform(ks[7], (DEPTH, GDN_HEADS), f32, np.log(1e-3), np.log(1e-1)))
    return {
        'x': nrm(ks[0], (BATCH, SEQ, D_MODEL), 1.0),
        'c': nrm(ks[1], (BATCH, D_MODEL), 1.0),
        'w_ada': nrm(ks[2], (DEPTH, D_MODEL, 6 * D_MODEL), D_MODEL ** -0.5),
        'b_ada': nrm(ks[3], (DEPTH, 6 * D_MODEL), 0.1),
        'norm1_w': gain(ks[4], (DEPTH, D_MODEL)),
        'w_in': nrm(ks[5], (DEPTH, D_MODEL, PROJ_WIDTH), D_MODEL ** -0.5),
        'conv_w': nrm(ks[8], (DEPTH, CONV_WIDTH, 1, 3 * GDN_WIDTH), CONV_WIDTH ** -0.5),
        'a_log': jnp.log(a_init),
        'dt_bias': dt + jnp.log(-jnp.expm1(-dt)),
        'gdn_norm_w': gain(ks[9], (DEPTH, HEAD_DIM)),
        'q_norm_w': gain(ks[10], (DEPTH, HEAD_DIM)),
        'k_norm_w': gain(ks[11], (DEPTH, HEAD_DIM)),
        'sinks': nrm(ks[12], (DEPTH, SWA_Q_HEADS), 1.0),
        'w_out': nrm(ks[13], (DEPTH, D_MIX, D_MODEL), D_MIX ** -0.5),
        'norm2_w': gain(ks[14], (DEPTH, D_MODEL)),
        'w_gate': nrm(ks[15], (DEPTH, D_MODEL, D_FF), D_MODEL ** -0.5),
        'w_up': nrm(ks[16], (DEPTH, D_MODEL, D_FF), D_MODEL ** -0.5),
        'w_down': nrm(ks[17], (DEPTH, D_FF, D_MODEL), D_FF ** -0.5),
    }


def _fwd_reference(x, c, w_ada, b_ada, norm1_w, w_in, conv_w, a_log, dt_bias, gdn_norm_w,
              q_norm_w, k_norm_w, sinks, w_out, norm2_w, w_gate, w_up, w_down):
    B, T, _ = x.shape
    split_sizes = (GDN_WIDTH,) * 4 + (GDN_HEADS,) * 2 + (SWA_WIDTH, SWA_KV_WIDTH, SWA_KV_WIDTH)
    split_points = []
    acc = 0
    for sz in split_sizes[:-1]:
        acc += sz
        split_points.append(acc)
    c_act = jax.nn.silu(c)
    for l in range(DEPTH):
        mod = (c_act @ w_ada[l] + b_ada[l])[:, None, :]
        shift1, scale1, gate1, shift2, scale2, gate2 = jnp.split(mod, 6, axis=-1)

        h = rms_norm(x, norm1_w[l]) * (1.0 + scale1) + shift1
        proj = h @ w_in[l]
        gq, gk, gv, gz, ga, gb, sq, sk, sv = jnp.split(proj, split_points, axis=-1)

        qkv = jax.nn.silu(causal_depthwise_conv(jnp.concatenate([gq, gk, gv], axis=-1), conv_w[l]))
        gq, gk, gv = jnp.split(qkv, 3, axis=-1)
        heads = lambda t: t.reshape(B, T, -1, HEAD_DIM)
        q_g = l2_norm(heads(gq))
        k_g = l2_norm(heads(gk))
        v_g = heads(gv).astype(jnp.float32)
        beta = jax.nn.sigmoid(gb.astype(jnp.float32))
        g = -jnp.exp(a_log[l].astype(jnp.float32)) * jax.nn.softplus(
            ga.astype(jnp.float32) + dt_bias[l].astype(jnp.float32))
        o_g = gated_delta_rule_chunked(q_g, k_g, v_g, g, beta).astype(x.dtype)
        o_g = rms_norm(o_g, gdn_norm_w[l]) * jax.nn.silu(heads(gz))
        o_g = o_g.reshape(B, T, GDN_WIDTH)

        q_s = rms_norm(heads(sq), q_norm_w[l])
        k_s = rms_norm(heads(sk), k_norm_w[l])
        v_s = heads(sv)
        o_s = sliding_window_attention(q_s, k_s, v_s, sinks[l])

        mixed = jnp.concatenate([o_g, o_s], axis=-1) @ w_out[l]
        x = x + gate1 * mixed

        h2 = rms_norm(x, norm2_w[l]) * (1.0 + scale2) + shift2
        ffn = (jax.nn.silu(h2 @ w_gate[l]) * (h2 @ w_up[l])) @ w_down[l]
        x = x + gate2 * ffn
    return x


import jax as _jax
import jax.numpy as _jnp

TWIN_FORMAT = 'train_step'
FWD_PARAMS = ['x', 'c', 'w_ada', 'b_ada', 'norm1_w', 'w_in', 'conv_w', 'a_log', 'dt_bias', 'gdn_norm_w', 'q_norm_w', 'k_norm_w', 'sinks', 'w_out', 'norm2_w', 'w_gate', 'w_up', 'w_down']
TWIN_WEIGHTS = ['w_ada', 'b_ada', 'norm1_w', 'w_in', 'conv_w', 'a_log', 'dt_bias', 'gdn_norm_w', 'q_norm_w', 'k_norm_w', 'sinks', 'w_out', 'norm2_w', 'w_gate', 'w_up', 'w_down']
TWIN_DIFF_INPUT = 'x'
TWIN_INPUTS = ['x', 'c', 'w_ada', 'b_ada', 'norm1_w', 'w_in', 'conv_w', 'a_log', 'dt_bias', 'gdn_norm_w', 'q_norm_w', 'k_norm_w', 'sinks', 'w_out', 'norm2_w', 'w_gate', 'w_up', 'w_down', 'loss_target', 'm_w_ada', 'm_b_ada', 'm_norm1_w', 'm_w_in', 'm_conv_w', 'm_a_log', 'm_dt_bias', 'm_gdn_norm_w', 'm_q_norm_w', 'm_k_norm_w', 'm_sinks', 'm_w_out', 'm_norm2_w', 'm_w_gate', 'm_w_up', 'm_w_down', 'v_w_ada', 'v_b_ada', 'v_norm1_w', 'v_w_in', 'v_conv_w', 'v_a_log', 'v_dt_bias', 'v_gdn_norm_w', 'v_q_norm_w', 'v_k_norm_w', 'v_sinks', 'v_w_out', 'v_norm2_w', 'v_w_gate', 'v_w_up', 'v_w_down']
TWIN_OUTPUTS = ['loss', 'grad_x', 'grad_w_ada', 'grad_b_ada', 'grad_norm1_w', 'grad_w_in', 'grad_conv_w', 'grad_a_log', 'grad_dt_bias', 'grad_gdn_norm_w', 'grad_q_norm_w', 'grad_k_norm_w', 'grad_sinks', 'grad_w_out', 'grad_norm2_w', 'grad_w_gate', 'grad_w_up', 'grad_w_down', 'delta_w_ada', 'delta_b_ada', 'delta_norm1_w', 'delta_w_in', 'delta_conv_w', 'delta_a_log', 'delta_dt_bias', 'delta_gdn_norm_w', 'delta_q_norm_w', 'delta_k_norm_w', 'delta_sinks', 'delta_w_out', 'delta_norm2_w', 'delta_w_gate', 'delta_w_up', 'delta_w_down', 'new_m_w_ada', 'new_m_b_ada', 'new_m_norm1_w', 'new_m_w_in', 'new_m_conv_w', 'new_m_a_log', 'new_m_dt_bias', 'new_m_gdn_norm_w', 'new_m_q_norm_w', 'new_m_k_norm_w', 'new_m_sinks', 'new_m_w_out', 'new_m_norm2_w', 'new_m_w_gate', 'new_m_w_up', 'new_m_w_down', 'new_v_w_ada', 'new_v_b_ada', 'new_v_norm1_w', 'new_v_w_in', 'new_v_conv_w', 'new_v_a_log', 'new_v_dt_bias', 'new_v_gdn_norm_w', 'new_v_q_norm_w', 'new_v_k_norm_w', 'new_v_sinks', 'new_v_w_out', 'new_v_norm2_w', 'new_v_w_gate', 'new_v_w_up', 'new_v_w_down']
TWIN_LEAF_KINDS = {'loss': 'loss', 'grad_x': 'grad_x', 'grad_w_ada': 'grad_w', 'grad_b_ada': 'grad_w', 'grad_norm1_w': 'grad_w', 'grad_w_in': 'grad_w', 'grad_conv_w': 'grad_w', 'grad_a_log': 'grad_w', 'grad_dt_bias': 'grad_w', 'grad_gdn_norm_w': 'grad_w', 'grad_q_norm_w': 'grad_w', 'grad_k_norm_w': 'grad_w', 'grad_sinks': 'grad_w', 'grad_w_out': 'grad_w', 'grad_norm2_w': 'grad_w', 'grad_w_gate': 'grad_w', 'grad_w_up': 'grad_w', 'grad_w_down': 'grad_w', 'delta_w_ada': 'delta_w', 'delta_b_ada': 'delta_w', 'delta_norm1_w': 'delta_w', 'delta_w_in': 'delta_w', 'delta_conv_w': 'delta_w', 'delta_a_log': 'delta_w', 'delta_dt_bias': 'delta_w', 'delta_gdn_norm_w': 'delta_w', 'delta_q_norm_w': 'delta_w', 'delta_k_norm_w': 'delta_w', 'delta_sinks': 'delta_w', 'delta_w_out': 'delta_w', 'delta_norm2_w': 'delta_w', 'delta_w_gate': 'delta_w', 'delta_w_up': 'delta_w', 'delta_w_down': 'delta_w', 'new_m_w_ada': 'new_m', 'new_m_b_ada': 'new_m', 'new_m_norm1_w': 'new_m', 'new_m_w_in': 'new_m', 'new_m_conv_w': 'new_m', 'new_m_a_log': 'new_m', 'new_m_dt_bias': 'new_m', 'new_m_gdn_norm_w': 'new_m', 'new_m_q_norm_w': 'new_m', 'new_m_k_norm_w': 'new_m', 'new_m_sinks': 'new_m', 'new_m_w_out': 'new_m', 'new_m_norm2_w': 'new_m', 'new_m_w_gate': 'new_m', 'new_m_w_up': 'new_m', 'new_m_w_down': 'new_m', 'new_v_w_ada': 'new_v', 'new_v_b_ada': 'new_v', 'new_v_norm1_w': 'new_v', 'new_v_w_in': 'new_v', 'new_v_conv_w': 'new_v', 'new_v_a_log': 'new_v', 'new_v_dt_bias': 'new_v', 'new_v_gdn_norm_w': 'new_v', 'new_v_q_norm_w': 'new_v', 'new_v_k_norm_w': 'new_v', 'new_v_sinks': 'new_v', 'new_v_w_out': 'new_v', 'new_v_norm2_w': 'new_v', 'new_v_w_gate': 'new_v', 'new_v_w_up': 'new_v', 'new_v_w_down': 'new_v'}


def _forward(args):
    return _fwd_reference(*[args[k] for k in FWD_PARAMS])


def _output_shape():
    def fwd():
        inp = _fwd_setup_inputs(0)
        return _fwd_reference(*[inp[k] for k in FWD_PARAMS])
    out = _jax.eval_shape(fwd)
    return out.shape, out.dtype

N_MICROBATCH = 1
ADAM_LR = 0.001
ADAM_B1 = 0.9
ADAM_B2 = 0.999
ADAM_EPS = 1e-08
ADAM_WD = 0.01
ADAM_STEP = 10
PER_EXAMPLE_BATCH_AXIS = {'x': 0, 'c': 0, 'loss_target': 0}
SHARED_INPUTS = []
_WEIGHT_DTYPES = {'w_ada': _jnp.float32, 'b_ada': _jnp.float32, 'norm1_w': _jnp.float32, 'w_in': _jnp.float32, 'conv_w': _jnp.float32, 'a_log': _jnp.float32, 'dt_bias': _jnp.float32, 'gdn_norm_w': _jnp.float32, 'q_norm_w': _jnp.float32, 'k_norm_w': _jnp.float32, 'sinks': _jnp.float32, 'w_out': _jnp.float32, 'norm2_w': _jnp.float32, 'w_gate': _jnp.float32, 'w_up': _jnp.float32, 'w_down': _jnp.float32}
MOMENT_SCALE = {'w_ada': 4.599777e+00, 'b_ada': 1.173714e+01, 'norm1_w': 4.925281e+00, 'w_in': 2.582404e+00, 'conv_w': 2.162055e+00, 'a_log': 6.923039e+00, 'dt_bias': 6.641905e+00, 'gdn_norm_w': 7.132444e+01, 'q_norm_w': 4.424778e+00, 'k_norm_w': 4.542293e+00, 'sinks': 3.279932e+01, 'w_out': 2.849589e+00, 'norm2_w': 2.903128e+01, 'w_gate': 1.502964e+00, 'w_up': 1.024881e+00, 'w_down': 1.186275e+00}


def _to_microbatches(a, axis):
    t = _jnp.moveaxis(a, axis, 0)
    t = t.reshape((N_MICROBATCH, t.shape[0] // N_MICROBATCH) + t.shape[1:])
    return _jnp.moveaxis(t, 1, axis + 1)


def setup_inputs(seed: int = 0) -> dict:
    inp = _fwd_setup_inputs(seed)
    key = _jax.random.fold_in(_jax.random.key(seed), 7919)
    shape, _ = _output_shape()
    out = dict(inp)
    out["loss_target"] = _jax.random.normal(_jax.random.fold_in(key, 0), shape, _jnp.float32)
    for i, name in enumerate(TWIN_WEIGHTS):
        w = inp[name].astype(_jnp.float32)
        if MOMENT_SCALE is None:
            s = _jnp.sqrt(_jnp.mean(_jnp.square(w)) + 1e-30)
        else:
            s = MOMENT_SCALE[name]
        km, kv = _jax.random.split(_jax.random.fold_in(key, i + 1))
        out[name] = w
        out["m_" + name] = s * _jax.random.normal(km, w.shape, _jnp.float32)
        out["v_" + name] = (s * s) * _jax.random.uniform(kv, w.shape, _jnp.float32, 0.5, 1.5)
    if N_MICROBATCH > 1:
        for name, axis in PER_EXAMPLE_BATCH_AXIS.items():
            out[name] = _to_microbatches(out[name], axis)
    return {'x': out['x'], 'c': out['c'], 'w_ada': out['w_ada'], 'b_ada': out['b_ada'], 'norm1_w': out['norm1_w'], 'w_in': out['w_in'], 'conv_w': out['conv_w'], 'a_log': out['a_log'], 'dt_bias': out['dt_bias'], 'gdn_norm_w': out['gdn_norm_w'], 'q_norm_w': out['q_norm_w'], 'k_norm_w': out['k_norm_w'], 'sinks': out['sinks'], 'w_out': out['w_out'], 'norm2_w': out['norm2_w'], 'w_gate': out['w_gate'], 'w_up': out['w_up'], 'w_down': out['w_down'], 'loss_target': out['loss_target'], 'm_w_ada': out['m_w_ada'], 'm_b_ada': out['m_b_ada'], 'm_norm1_w': out['m_norm1_w'], 'm_w_in': out['m_w_in'], 'm_conv_w': out['m_conv_w'], 'm_a_log': out['m_a_log'], 'm_dt_bias': out['m_dt_bias'], 'm_gdn_norm_w': out['m_gdn_norm_w'], 'm_q_norm_w': out['m_q_norm_w'], 'm_k_norm_w': out['m_k_norm_w'], 'm_sinks': out['m_sinks'], 'm_w_out': out['m_w_out'], 'm_norm2_w': out['m_norm2_w'], 'm_w_gate': out['m_w_gate'], 'm_w_up': out['m_w_up'], 'm_w_down': out['m_w_down'], 'v_w_ada': out['v_w_ada'], 'v_b_ada': out['v_b_ada'], 'v_norm1_w': out['v_norm1_w'], 'v_w_in': out['v_w_in'], 'v_conv_w': out['v_conv_w'], 'v_a_log': out['v_a_log'], 'v_dt_bias': out['v_dt_bias'], 'v_gdn_norm_w': out['v_gdn_norm_w'], 'v_q_norm_w': out['v_q_norm_w'], 'v_k_norm_w': out['v_k_norm_w'], 'v_sinks': out['v_sinks'], 'v_w_out': out['v_w_out'], 'v_norm2_w': out['v_norm2_w'], 'v_w_gate': out['v_w_gate'], 'v_w_up': out['v_w_up'], 'v_w_down': out['v_w_down']}


def _loss(weights, diff, rest, loss_target):
    with _jax.named_scope("forward"):
        args = {**rest, TWIN_DIFF_INPUT: diff, **{k: w.astype(_WEIGHT_DTYPES[k]) for k, w in weights.items()}}
        y = _forward(args)
    with _jax.named_scope("loss_head"):
        err = _jnp.square(y.astype(_jnp.float32) - loss_target)
        return 0.5 * _jnp.sum(_jnp.mean(err, axis=-1)) if err.ndim else 0.5 * err


def _adamw(w, g, m, v):
    m = ADAM_B1 * m + (1.0 - ADAM_B1) * g
    v = ADAM_B2 * v + (1.0 - ADAM_B2) * _jnp.square(g)
    m_hat = m / (1.0 - ADAM_B1 ** ADAM_STEP)
    v_hat = v / (1.0 - ADAM_B2 ** ADAM_STEP)
    delta = -ADAM_LR * (m_hat / (_jnp.sqrt(v_hat) + ADAM_EPS) + ADAM_WD * w)
    return delta, m, v


def reference(x, c, w_ada, b_ada, norm1_w, w_in, conv_w, a_log, dt_bias, gdn_norm_w, q_norm_w, k_norm_w, sinks, w_out, norm2_w, w_gate, w_up, w_down, loss_target, m_w_ada, m_b_ada, m_norm1_w, m_w_in, m_conv_w, m_a_log, m_dt_bias, m_gdn_norm_w, m_q_norm_w, m_k_norm_w, m_sinks, m_w_out, m_norm2_w, m_w_gate, m_w_up, m_w_down, v_w_ada, v_b_ada, v_norm1_w, v_w_in, v_conv_w, v_a_log, v_dt_bias, v_gdn_norm_w, v_q_norm_w, v_k_norm_w, v_sinks, v_w_out, v_norm2_w, v_w_gate, v_w_up, v_w_down):
    given = dict(x=x, c=c, w_ada=w_ada, b_ada=b_ada, norm1_w=norm1_w, w_in=w_in, conv_w=conv_w, a_log=a_log, dt_bias=dt_bias, gdn_norm_w=gdn_norm_w, q_norm_w=q_norm_w, k_norm_w=k_norm_w, sinks=sinks, w_out=w_out, norm2_w=norm2_w, w_gate=w_gate, w_up=w_up, w_down=w_down, loss_target=loss_target, m_w_ada=m_w_ada, m_b_ada=m_b_ada, m_norm1_w=m_norm1_w, m_w_in=m_w_in, m_conv_w=m_conv_w, m_a_log=m_a_log, m_dt_bias=m_dt_bias, m_gdn_norm_w=m_gdn_norm_w, m_q_norm_w=m_q_norm_w, m_k_norm_w=m_k_norm_w, m_sinks=m_sinks, m_w_out=m_w_out, m_norm2_w=m_norm2_w, m_w_gate=m_w_gate, m_w_up=m_w_up, m_w_down=m_w_down, v_w_ada=v_w_ada, v_b_ada=v_b_ada, v_norm1_w=v_norm1_w, v_w_in=v_w_in, v_conv_w=v_conv_w, v_a_log=v_a_log, v_dt_bias=v_dt_bias, v_gdn_norm_w=v_gdn_norm_w, v_q_norm_w=v_q_norm_w, v_k_norm_w=v_k_norm_w, v_sinks=v_sinks, v_w_out=v_w_out, v_norm2_w=v_norm2_w, v_w_gate=v_w_gate, v_w_up=v_w_up, v_w_down=v_w_down)
    weights = {n: given[n] for n in TWIN_WEIGHTS}
    shared = {n: given[n] for n in SHARED_INPUTS}
    per_example = {n: given[n] for n in ['x', 'c']}
    grad_fn = _jax.value_and_grad(_loss, argnums=(0, 1))

    def one_microbatch(ex, loss_target):
        ex = dict(ex)
        diff = ex.pop(TWIN_DIFF_INPUT)
        return grad_fn(weights, diff, {**shared, **ex}, loss_target)

    if N_MICROBATCH == 1:
        loss, (grad_w, grad_x) = one_microbatch(per_example, given["loss_target"])
    else:
        def body(carry, xs):
            loss_sum, grad_sum = carry
            l_k, (gw_k, gx_k) = one_microbatch(xs[0], xs[1])
            with _jax.named_scope("update"):
                return (loss_sum + l_k, _jax.tree.map(_jnp.add, grad_sum, gw_k)), gx_k

        init = (_jnp.zeros((), _jnp.float32), _jax.tree.map(_jnp.zeros_like, weights))
        (loss, grad_w), grad_x = _jax.lax.scan(body, init, (per_example, given["loss_target"]))
    with _jax.named_scope("update"):
        delta_w, new_m, new_v = {}, {}, {}
        for n in TWIN_WEIGHTS:
            delta_w[n], new_m[n], new_v[n] = _adamw(weights[n], grad_w[n], given["m_" + n], given["v_" + n])
    return (loss, grad_x, *[grad_w[n] for n in TWIN_WEIGHTS], *[delta_w[n] for n in TWIN_WEIGHTS],
            *[new_m[n] for n in TWIN_WEIGHTS], *[new_v[n] for n in TWIN_WEIGHTS])
```

```python
import functools

import jax
import jax.numpy as jnp
from jax import lax
from jax.experimental import pallas as pl
from jax.experimental.pallas import tpu as pltpu

F32 = jnp.float32
BF16 = jnp.bfloat16
MESH = pl.DeviceIdType.MESH

D = 1024
HD = 64
GH = 8
GW = GH * HD
SQH = 8
SKVH = 2
SGRP = SQH // SKVH
WIN = 128
CONVW = 4
CHUNK = 64
DFF = 2816
PROJ = 2832
NP = 3072
EPS = 1e-6
N_DEV = 8
N_CHIP = 4

ADAM_LR = 0.001
ADAM_B1 = 0.9
ADAM_B2 = 0.999
ADAM_EPS = 1e-08
ADAM_WD = 0.01
ADAM_STEP = 10

VMEM_LIMIT = 48 * 1024 * 1024
LANE = 128

PACK_ROWS = (PROJ // N_CHIP, D // N_CHIP, DFF // N_CHIP, DFF // N_CHIP, DFF // N_CHIP)
PACK_P = 3104
PACK_H = PACK_P // 2


def _cparams(sem=None):
    return pltpu.CompilerParams(dimension_semantics=sem, vmem_limit_bytes=VMEM_LIMIT)


_NN = ((1,), (0,))
_NT = ((1,), (1,))
_TN = ((0,), (0,))


def _dot(a, b, dims):
    return lax.dot_general(a, b, (dims, ((), ())), preferred_element_type=F32)


def _raw1(a, b, dims):
    return _dot(a.astype(BF16), b.astype(BF16), dims)


def _raw3(a, b, dims):
    ah = a.astype(BF16)
    al = (a - ah.astype(F32)).astype(BF16)
    bh = b.astype(BF16)
    bl = (b - bh.astype(F32)).astype(BF16)
    return _dot(ah, bh, dims) + (_dot(al, bh, dims) + _dot(ah, bl, dims))


def _make_diff_mm(raw):
    @jax.custom_vjp
    def nn(a, b):
        return raw(a, b, _NN)

    @jax.custom_vjp
    def nt(a, b):
        return raw(a, b, _NT)

    @jax.custom_vjp
    def tn(a, b):
        return raw(a, b, _TN)

    nn.defvjp(lambda a, b: (raw(a, b, _NN), (a, b)), lambda r, g: (nt(g, r[1]), tn(r[0], g)))
    nt.defvjp(lambda a, b: (raw(a, b, _NT), (a, b)), lambda r, g: (nn(g, r[1]), tn(g, r[0])))
    tn.defvjp(lambda a, b: (raw(a, b, _TN), (a, b)), lambda r, g: (nt(r[1], g), nn(r[0], g)))
    return nn, nt, tn


def _tri_inv_raw(a, nn3):
    n = a.shape[0]
    ri = lax.broadcasted_iota(jnp.int32, (n, n), 0)
    ci = lax.broadcasted_iota(jnp.int32, (n, n), 1)
    t = (ri == ci).astype(F32)
    for lvl in range((n - 1).bit_length()):
        same_pair = (ri >> (lvl + 1)) == (ci >> (lvl + 1))
        lower_left = (((ri >> lvl) & 1) == 1) & (((ci >> lvl) & 1) == 0)
        y = jnp.where(same_pair & lower_left, a, 0.0)
        t = t - y if lvl == 0 else t - nn3(nn3(t, y), t)
    return t


class _Kit:
    def __init__(self, diff):
        if diff:
            self.nn, self.nt, self.tn = _make_diff_mm(_raw1)
            self.nn3, self.nt3, self.tn3 = _make_diff_mm(_raw3)
            nn3, nt3, tn3 = self.nn3, self.nt3, self.tn3

            @jax.custom_vjp
            def inv(a):
                return _tri_inv_raw(a, nn3)

            def inv_fwd(a):
                t = _tri_inv_raw(a, nn3)
                return t, t

            def inv_bwd(t, g):
                return (-tn3(t, nt3(g, t)),)

            inv.defvjp(inv_fwd, inv_bwd)
            self.inv = inv
        else:
            self.nn = lambda a, b: _raw1(a, b, _NN)
            self.nt = lambda a, b: _raw1(a, b, _NT)
            self.tn = lambda a, b: _raw1(a, b, _TN)
            self.nn3 = lambda a, b: _raw3(a, b, _NN)
            self.nt3 = lambda a, b: _raw3(a, b, _NT)
            self.tn3 = lambda a, b: _raw3(a, b, _TN)
            self.inv = lambda a: _tri_inv_raw(a, self.nn3)


def _sigmoid(x):
    return 1.0 / (1.0 + jnp.exp(-x))


def _silu(x):
    return x * _sigmoid(x)


def _rms(x, w):
    return x * lax.rsqrt(jnp.mean(x * x, axis=-1, keepdims=True) + EPS) * w


def _tile(dim, target):
    t = (min(dim, target) // LANE) * LANE
    while t >= LANE:
        if dim % t == 0:
            return t
        t -= LANE
    return dim


def _matmul(a, b, ta=False, tb=False, out_dtype=F32, name="matmul"):
    if ta:
        K, M = a.shape
    else:
        M, K = a.shape
    if tb:
        N, K2 = b.shape
    else:
        K2, N = b.shape
    assert K == K2, (a.shape, b.shape, ta, tb)
    tm, tn, tk = _tile(M, 1024), _tile(N, 1536), _tile(K, 1024)
    nk = K // tk
    dims = ((0,) if ta else (1,), (1,) if tb else (0,))

    def body(a_ref, b_ref, o_ref, acc_ref):
        k = pl.program_id(2)

        @pl.when(k == 0)
        def _():
            acc_ref[...] = jnp.zeros_like(acc_ref)

        acc_ref[...] += _dot(a_ref[...].astype(BF16), b_ref[...].astype(BF16), dims)

        @pl.when(k == nk - 1)
        def _():
            o_ref[...] = acc_ref[...].astype(o_ref.dtype)

    a_spec = (pl.BlockSpec((tk, tm), lambda i, j, k: (k, i)) if ta
              else pl.BlockSpec((tm, tk), lambda i, j, k: (i, k)))
    b_spec = (pl.BlockSpec((tn, tk), lambda i, j, k: (j, k)) if tb
              else pl.BlockSpec((tk, tn), lambda i, j, k: (k, j)))
    return pl.pallas_call(
        body, name=name,
        grid=(M // tm, N // tn, nk),
        in_specs=[a_spec, b_spec],
        out_specs=pl.BlockSpec((tm, tn), lambda i, j, k: (i, j)),
        out_shape=jax.ShapeDtypeStruct((M, N), out_dtype),
        scratch_shapes=[pltpu.VMEM((tm, tn), F32)],
        compiler_params=_cparams(("parallel", "parallel", "arbitrary")),
    )(a, b)


def _rowcall(fn, tiled, consts, out_tiled, out_acc, tm, name):
    T = tiled[0].shape[0]
    n_in = len(tiled) + len(consts)
    n_o = len(out_tiled)

    def body(*refs):
        vals = [r[...] for r in refs[:n_in]]
        outs = refs[n_in:]
        res = fn(*vals)
        for r, v in zip(outs[:n_o], res[:n_o]):
            r[...] = v.astype(r.dtype)
        if len(outs) > n_o:
            @pl.when(pl.program_id(0) == 0)
            def _():
                for r in outs[n_o:]:
                    r[...] = jnp.zeros_like(r)

            for r, v in zip(outs[n_o:], res[n_o:]):
                r[...] += v

    in_specs = [pl.BlockSpec((tm, a.shape[1]), lambda i: (i, 0)) for a in tiled]
    in_specs += [pl.BlockSpec(a.shape, lambda i, nd=a.ndim: (0,) * nd) for a in consts]
    out_specs = [pl.BlockSpec((tm, s.shape[1]), lambda i: (i, 0)) for s in out_tiled]
    out_specs += [pl.BlockSpec(s.shape, lambda i: (0, 0)) for s in out_acc]
    return pl.pallas_call(
        body, name=name, grid=(T // tm,),
        in_specs=in_specs, out_specs=out_specs,
        out_shape=list(out_tiled) + list(out_acc),
        compiler_params=_cparams(("arbitrary",)),
    )(*tiled, *consts)


def _sds(shape, dtype=F32):
    return jax.ShapeDtypeStruct(shape, dtype)


def _norm_mod(x, nw, scale, shift):
    return _rms(x, nw) * (1.0 + scale) + shift


def _norm_mod_fwd(x, nw, scale, shift):
    T = x.shape[0]
    (h,) = _rowcall(lambda *a: (_norm_mod(*a),), [x], [nw, scale, shift],
                    [_sds((T, D), BF16)], [], 512, "norm1_fwd")
    return h


def _norm_mod_bwd(x, dh, dres, nw, scale, shift):
    T = x.shape[0]

    def fn(x, dh, dres, nw, scale, shift):
        _, vjp = jax.vjp(_norm_mod, x, nw, scale, shift)
        dx, dnw, dsc, dsh = vjp(dh)
        return dx + dres, dnw, dsc, dsh

    return _rowcall(fn, [x, dh, dres], [nw, scale, shift], [_sds((T, D))],
                    [_sds((1, D))] * 3, 256, "norm1_bwd")


def _resid_norm(x, mixed, gate1, nw, scale, shift):
    x1 = x + gate1 * mixed
    return x1, _norm_mod(x1, nw, scale, shift)


def _resid_norm_fwd(x, mixed, gate1, nw, scale, shift):
    T = x.shape[0]
    return _rowcall(_resid_norm, [x, mixed], [gate1, nw, scale, shift],
                    [_sds((T, D)), _sds((T, D), BF16)], [], 512, "resid_norm2_fwd")


def _resid_norm_bwd(x, mixed, dy, dh2, gate1, nw, scale, shift):
    T = x.shape[0]

    def fn(x, mixed, dy, dh2, gate1, nw, scale, shift):
        _, vjp = jax.vjp(_resid_norm, x, mixed, gate1, nw, scale, shift)
        dx, dmixed, dg1, dnw, dsc, dsh = vjp((dy, dh2))
        return dx, dmixed, dg1, dnw, dsc, dsh

    return _rowcall(fn, [x, mixed, dy, dh2], [gate1, nw, scale, shift],
                    [_sds((T, D)), _sds((T, D), BF16)], [_sds((1, D))] * 4, 256, "resid_norm2_bwd")


def _ffn_act_fwd(ab):
    T = ab.shape[0]

    def fn(ab):
        a, b = ab[:, :DFF], ab[:, DFF:]
        return (_silu(a) * b,)

    (act,) = _rowcall(fn, [ab], [], [_sds((T, DFF), BF16)], [], 256, "ffn_act_fwd")
    return act


def _ffn_act_bwd(ab, dact):
    T = ab.shape[0]

    def fn(ab, dact):
        a, b = ab[:, :DFF], ab[:, DFF:]
        s = _sigmoid(a)
        da = dact * b * (s * (1.0 + a * (1.0 - s)))
        db = dact * (a * s)
        return (jnp.concatenate([da, db], axis=1),)

    (dab,) = _rowcall(fn, [ab, dact], [], [_sds((T, 2 * DFF), BF16)], [], 256, "ffn_act_bwd")
    return dab


def _loss_head(x1, ffn, target, gate2):
    T = x1.shape[0]

    def fn(x1, ffn, target, gate2):
        y = x1 + gate2 * ffn
        err = y - target
        loss = 0.5 * jnp.sum(jnp.sum(err * err, axis=1, keepdims=True), axis=0, keepdims=True) / D
        dy = err * (1.0 / D)
        dgate2 = jnp.sum(dy * ffn, axis=0, keepdims=True)
        return dy, gate2 * dy, dgate2, jnp.broadcast_to(loss, (1, LANE))

    return _rowcall(fn, [x1, ffn, target], [gate2], [_sds((T, D)), _sds((T, D), BF16)],
                    [_sds((1, D)), _sds((1, LANE))], 256, "loss_head")


def _round_bf16(x):
    return x.astype(BF16).astype(F32)


def _shift_down(x, s, rows):
    if s == 0:
        return x
    return jnp.where(rows >= s, pltpu.roll(x, s, 0), 0.0)


def _shift_up(x, s, rows, T):
    if s == 0:
        return x
    return jnp.where(rows < T - s, pltpu.roll(x, T - s, 0), 0.0)


def _conv_fwd(proj, conv_w):
    T = proj.shape[0]
    ncol = 3 * GW // LANE

    def body(x_ref, w_ref, o_ref):
        x = _round_bf16(x_ref[...])
        rows = lax.broadcasted_iota(jnp.int32, x.shape, 0)
        acc = jnp.zeros_like(x)
        for j in range(CONVW):
            acc = acc + _round_bf16(w_ref[pl.ds(j, 1), :]) * _shift_down(x, CONVW - 1 - j, rows)
        o_ref[...] = _silu(acc)

    return pl.pallas_call(
        body, name="conv_fwd", grid=(ncol,),
        in_specs=[pl.BlockSpec((T, LANE), lambda j: (0, j)), pl.BlockSpec((CONVW, LANE), lambda j: (0, j))],
        out_specs=pl.BlockSpec((T, LANE), lambda j: (0, j)),
        out_shape=_sds((T, 3 * GW)),
        compiler_params=_cparams(("parallel",)),
    )(proj, conv_w)


def _conv_bwd(proj, conv_w, dqc):
    T = proj.shape[0]
    ncol = 3 * GW // LANE

    def body(x_ref, w_ref, d_ref, dx_ref, dw_ref):
        x = _round_bf16(x_ref[...])
        rows = lax.broadcasted_iota(jnp.int32, x.shape, 0)
        xs = [_shift_down(x, CONVW - 1 - j, rows) for j in range(CONVW)]
        w = [_round_bf16(w_ref[pl.ds(j, 1), :]) for j in range(CONVW)]
        pre = jnp.zeros_like(x)
        for j in range(CONVW):
            pre = pre + w[j] * xs[j]
        s = _sigmoid(pre)
        dpre = _round_bf16(d_ref[...] * (s * (1.0 + pre * (1.0 - s))))
        dx = jnp.zeros_like(x)
        for j in range(CONVW):
            dx = dx + w[j] * _shift_up(dpre, CONVW - 1 - j, rows, T)
            dw_ref[pl.ds(j, 1), :] = jnp.sum(dpre * xs[j], axis=0, keepdims=True)
        dx_ref[...] = dx.astype(dx_ref.dtype)

    return pl.pallas_call(
        body, name="conv_bwd", grid=(ncol,),
        in_specs=[pl.BlockSpec((T, LANE), lambda j: (0, j)), pl.BlockSpec((CONVW, LANE), lambda j: (0, j)),
                  pl.BlockSpec((T, LANE), lambda j: (0, j))],
        out_specs=[pl.BlockSpec((T, LANE), lambda j: (0, j)), pl.BlockSpec((CONVW, LANE), lambda j: (0, j))],
        out_shape=[_sds((T, 3 * GW), BF16), _sds((CONVW, 3 * GW))],
        compiler_params=_cparams(("parallel",)),
    )(proj, conv_w, dqc)


def _gdn_chunk(kit, q, k, v, z, ga, gb, alog, dtb, nw, S):
    C = CHUNK
    ri = lax.broadcasted_iota(jnp.int32, (C, C), 0)
    ci = lax.broadcasted_iota(jnp.int32, (C, C), 1)
    causal = ri >= ci
    strict = ri > ci
    eye = (ri == ci).astype(F32)
    lower = causal.astype(F32)
    upper = (ri <= ci).astype(F32)

    a = ga + dtb
    softplus = jnp.maximum(a, 0.0) + jnp.log(1.0 + jnp.exp(-jnp.abs(a)))
    g_row = -jnp.exp(alog) * softplus
    beta_row = _sigmoid(gb)
    g_col = jnp.sum(eye * g_row, axis=1, keepdims=True)
    beta_col = jnp.sum(eye * beta_row, axis=1, keepdims=True)
    G_col = jnp.sum(lower * g_row, axis=1, keepdims=True)
    G_row = jnp.sum(upper * g_col, axis=0, keepdims=True)
    G_last = jnp.sum(g_row, axis=1, keepdims=True)
    decay = jnp.exp(jnp.where(causal, G_col - G_row, -1e30))

    qn = q * lax.rsqrt(jnp.sum(q * q, axis=-1, keepdims=True) + EPS) * (HD ** -0.5)
    kn = k * lax.rsqrt(jnp.sum(k * k, axis=-1, keepdims=True) + EPS)
    kb = kn * beta_col
    A = jnp.where(strict, kit.nt(kb, kn) * decay, 0.0)
    Tm = kit.inv(A)
    eG = jnp.exp(G_col)
    u = kit.nn3(Tm, v * beta_col)
    w = kit.nn3(Tm, kb * eG)
    qk = jnp.where(causal, kit.nt(qn, kn) * decay, 0.0)
    q_dec = qn * eG
    k_dec = kn * jnp.exp(G_last - G_col)
    dec = jnp.exp(G_last)
    v_new = u - kit.nn(w, S)
    o = kit.nn(q_dec, S) + kit.nn(qk, v_new)
    S_new = S * dec + kit.tn(k_dec, v_new)
    out = _rms(o, nw) * _silu(z)
    return out, S_new


GDN_CB = 4


def _gdn_specs(T):
    TB = GDN_CB * CHUNK
    seq = lambda off: pl.BlockSpec((None, TB, HD), lambda h, i, off=off: (h + off, i, 0))
    row = lambda off: pl.BlockSpec((None, GDN_CB, 1, CHUNK), lambda h, i, off=off: (h + off, i, 0, 0))
    per_head = pl.BlockSpec((None, 1, CHUNK), lambda h, i: (h, 0, 0))
    whole = pl.BlockSpec((1, HD), lambda h, i: (0, 0))
    state = pl.BlockSpec((None, GDN_CB, HD, HD), lambda h, i: (h, i, 0, 0))
    return seq, row, per_head, whole, state


def _gdn_fwd(qkv_hm, zs_hm, gab, alog_b, dtb_b, nw):
    T = qkv_hm.shape[1]
    N = T // CHUNK
    seq, row, per_head, whole, state = _gdn_specs(T)
    kit = _Kit(False)

    def body(q_ref, k_ref, v_ref, z_ref, ga_ref, gb_ref, al_ref, dt_ref, nw_ref, o_ref, S_ref, S_scr):
        @pl.when(pl.program_id(1) == 0)
        def _():
            S_scr[...] = jnp.zeros_like(S_scr)

        S = S_scr[...]
        for cb in range(GDN_CB):
            sl = pl.ds(cb * CHUNK, CHUNK)
            S_ref[cb] = S
            out, S = _gdn_chunk(kit, q_ref[sl, :], k_ref[sl, :], v_ref[sl, :], z_ref[sl, :], ga_ref[cb],
                                gb_ref[cb], al_ref[...], dt_ref[...], nw_ref[...], S)
            o_ref[sl, :] = out
        S_scr[...] = S

    return pl.pallas_call(
        body, name="gdn_fwd", grid=(GH, N // GDN_CB),
        in_specs=[seq(0), seq(GH), seq(2 * GH), seq(0), row(0), row(GH), per_head, per_head, whole],
        out_specs=[seq(0), state],
        out_shape=[_sds((GH, T, HD)), _sds((GH, N, HD, HD))],
        scratch_shapes=[pltpu.VMEM((HD, HD), F32)],
        compiler_params=_cparams(("arbitrary", "arbitrary")),
    )(qkv_hm, qkv_hm, qkv_hm, zs_hm, gab, gab, alog_b, dtb_b, nw)


def _gdn_bwd(qkv_hm, zs_hm, gab, alog_b, dtb_b, nw, S_all, do):
    T = qkv_hm.shape[1]
    N = T // CHUNK
    nblk = N // GDN_CB
    TB = GDN_CB * CHUNK
    kit = _Kit(True)
    rseq = lambda off: pl.BlockSpec((None, TB, HD), lambda h, i, off=off: (h + off, nblk - 1 - i, 0))
    rrow = lambda off: pl.BlockSpec((None, GDN_CB, 1, CHUNK), lambda h, i, off=off: (h + off, nblk - 1 - i, 0, 0))
    per_head = pl.BlockSpec((None, 1, CHUNK), lambda h, i: (h, 0, 0))
    whole = pl.BlockSpec((1, HD), lambda h, i: (0, 0))
    rstate = pl.BlockSpec((None, GDN_CB, HD, HD), lambda h, i: (h, nblk - 1 - i, 0, 0))

    def body(q_ref, k_ref, v_ref, z_ref, ga_ref, gb_ref, al_ref, dt_ref, nw_ref, S_ref, do_ref,
             dq_ref, dk_ref, dv_ref, dz_ref, dga_ref, dgb_ref, dal_ref, ddt_ref, dnw_ref, dS_scr):
        h = pl.program_id(0)
        i = pl.program_id(1)

        @pl.when(i == 0)
        def _():
            dS_scr[...] = jnp.zeros_like(dS_scr)
            dal_ref[...] = jnp.zeros_like(dal_ref)
            ddt_ref[...] = jnp.zeros_like(ddt_ref)

        @pl.when((i == 0) & (h == 0))
        def _():
            dnw_ref[...] = jnp.zeros_like(dnw_ref)

        dS = dS_scr[...]
        for cb in reversed(range(GDN_CB)):
            sl = pl.ds(cb * CHUNK, CHUNK)
            args = (q_ref[sl, :], k_ref[sl, :], v_ref[sl, :], z_ref[sl, :], ga_ref[cb], gb_ref[cb],
                    al_ref[...], dt_ref[...], nw_ref[...], S_ref[cb])
            _, vjp = jax.vjp(functools.partial(_gdn_chunk, kit), *args)
            dq, dk, dv, dz, dga, dgb, dal, ddt, dnw, dS = vjp((do_ref[sl, :], dS))
            dq_ref[sl, :] = dq
            dk_ref[sl, :] = dk
            dv_ref[sl, :] = dv
            dz_ref[sl, :] = dz
            dga_ref[cb] = dga
            dgb_ref[cb] = dgb
            dal_ref[...] += jnp.broadcast_to(jnp.sum(dal, axis=1, keepdims=True), dal.shape)
            ddt_ref[...] += jnp.broadcast_to(jnp.sum(ddt, axis=1, keepdims=True), ddt.shape)
            dnw_ref[...] += dnw
        dS_scr[...] = dS

    return pl.pallas_call(
        body, name="gdn_bwd", grid=(GH, nblk),
        in_specs=[rseq(0), rseq(GH), rseq(2 * GH), rseq(0), rrow(0), rrow(GH), per_head, per_head, whole,
                  rstate, rseq(0)],
        out_specs=[rseq(0), rseq(0), rseq(0), rseq(0), rrow(0), rrow(0), per_head, per_head, whole],
        out_shape=[_sds((GH, T, HD))] * 4 + [_sds((GH, N, 1, CHUNK))] * 2 + [_sds((GH, 1, CHUNK))] * 2
                  + [_sds((1, HD))],
        scratch_shapes=[pltpu.VMEM((HD, HD), F32)],
        compiler_params=_cparams(("arbitrary", "arbitrary")),
    )(qkv_hm, qkv_hm, qkv_hm, zs_hm, gab, gab, alog_b, dtb_b, nw, S_all, do)


def _swa_block(kit, first, q0, q1, q2, q3, kp, kc, vp, vc, qnw, knw, s0, s1, s2, s3, *, slopes):
    W = WIN
    ri = lax.broadcasted_iota(jnp.int32, (W, W), 0)
    ci = lax.broadcasted_iota(jnp.int32, (W, W), 1)
    mask_c = ri >= ci
    mask_p = ci > ri + first * W
    dist_c = (ri - ci).astype(F32)
    dist_p = (ri - ci + W).astype(F32)
    kpn = _rms(kp, knw)
    kcn = _rms(kc, knw)
    outs = []
    for q, sink, slope in zip((q0, q1, q2, q3), (s0, s1, s2, s3), slopes):
        qn = _rms(q, qnw)
        sc = jnp.where(mask_c, kit.nt(qn, kcn) * (HD ** -0.5) - slope * dist_c, -1e30)
        sp = jnp.where(mask_p, kit.nt(qn, kpn) * (HD ** -0.5) - slope * dist_p, -1e30)
        m = jnp.maximum(jnp.maximum(jnp.max(sc, axis=-1, keepdims=True), jnp.max(sp, axis=-1, keepdims=True)), sink)
        m = lax.stop_gradient(m)
        pc = jnp.exp(sc - m)
        pp = jnp.exp(sp - m)
        den = jnp.sum(pc, axis=-1, keepdims=True) + jnp.sum(pp, axis=-1, keepdims=True) + jnp.exp(sink - m)
        inv = 1.0 / den
        outs.append(kit.nn(pc * inv, vc) + kit.nn(pp * inv, vp))
    return tuple(outs)


def _swa_slopes(hk):
    return tuple(jnp.where(hk == 0, 2.0 ** (-8.0 * (g + 1.0) / SQH), 2.0 ** (-8.0 * (SGRP + g + 1.0) / SQH))
                 for g in range(SGRP))


def _swa_fwd(zs_hm, qnw, knw, sinks_col):
    T = zs_hm.shape[1]
    NB = T // WIN
    kit = _Kit(False)

    def body(q_ref, kp_ref, kc_ref, vp_ref, vc_ref, qnw_ref, knw_ref, s_ref, o_ref):
        hk = pl.program_id(0)
        first = (pl.program_id(1) == 0).astype(jnp.int32)
        args = ([q_ref[g] for g in range(SGRP)] + [kp_ref[...], kc_ref[...], vp_ref[...], vc_ref[...],
                                                     qnw_ref[...], knw_ref[...]] + [s_ref[g] for g in range(SGRP)])
        outs = _swa_block(kit, first, *args, slopes=_swa_slopes(hk))
        for g in range(SGRP):
            o_ref[g] = outs[g]

    qspec = pl.BlockSpec((SGRP, WIN, HD), lambda hk, n: (2 + hk, n, 0))
    cur = lambda off: pl.BlockSpec((None, WIN, HD), lambda hk, n, off=off: (off + hk, n, 0))
    prev = lambda off: pl.BlockSpec((None, WIN, HD), lambda hk, n, off=off: (off + hk, jnp.maximum(n - 1, 0), 0))
    whole = pl.BlockSpec((1, HD), lambda hk, n: (0, 0))
    sspec = pl.BlockSpec((SGRP, WIN, 1), lambda hk, n: (hk, 0, 0))
    return pl.pallas_call(
        body, name="swa_fwd", grid=(SKVH, NB),
        in_specs=[qspec, prev(16), cur(16), prev(18), cur(18), whole, whole, sspec],
        out_specs=pl.BlockSpec((SGRP, WIN, HD), lambda hk, n: (hk, n, 0)),
        out_shape=_sds((SQH, T, HD)),
        compiler_params=_cparams(("parallel", "arbitrary")),
    )(zs_hm, zs_hm, zs_hm, zs_hm, zs_hm, qnw, knw, sinks_col)


def _swa_bwd(zs_hm, qnw, knw, sinks_col, do):
    T = zs_hm.shape[1]
    NB = T // WIN
    kit = _Kit(True)

    def body(q_ref, kp_ref, kc_ref, vp_ref, vc_ref, qnw_ref, knw_ref, s_ref, do_ref,
             dq_ref, dk_ref, dv_ref, dqnw_ref, dknw_ref, ds_ref, ck_scr, cv_scr):
        hk = pl.program_id(0)
        i = pl.program_id(1)
        first = (i == NB - 1).astype(jnp.int32)

        @pl.when(i == 0)
        def _():
            ck_scr[...] = jnp.zeros_like(ck_scr)
            cv_scr[...] = jnp.zeros_like(cv_scr)
            ds_ref[...] = jnp.zeros_like(ds_ref)

        @pl.when((i == 0) & (hk == 0))
        def _():
            dqnw_ref[...] = jnp.zeros_like(dqnw_ref)
            dknw_ref[...] = jnp.zeros_like(dknw_ref)

        args = ([q_ref[g] for g in range(SGRP)] + [kp_ref[...], kc_ref[...], vp_ref[...], vc_ref[...],
                                                     qnw_ref[...], knw_ref[...]] + [s_ref[g] for g in range(SGRP)])
        dos = tuple(do_ref[g] for g in range(SGRP))
        _, vjp = jax.vjp(functools.partial(_swa_block, kit, first, slopes=_swa_slopes(hk)), *args)
        gr = vjp(dos)
        for g in range(SGRP):
            dq_ref[g] = gr[g]
            ds_ref[g] += jnp.broadcast_to(jnp.sum(gr[10 + g], axis=0, keepdims=True), (WIN, 1))
        dkp, dkc, dvp, dvc = gr[4:8]
        dk_ref[...] = dkc + ck_scr[...]
        dv_ref[...] = dvc + cv_scr[...]
        ck_scr[...] = dkp
        cv_scr[...] = dvp
        dqnw_ref[...] += gr[8]
        dknw_ref[...] += gr[9]

    rn = lambda n: NB - 1 - n
    qspec = pl.BlockSpec((SGRP, WIN, HD), lambda hk, i: (2 + hk, rn(i), 0))
    cur = lambda off: pl.BlockSpec((None, WIN, HD), lambda hk, i, off=off: (off + hk, rn(i), 0))
    prev = lambda off: pl.BlockSpec((None, WIN, HD), lambda hk, i, off=off: (off + hk, jnp.maximum(rn(i) - 1, 0), 0))
    whole = pl.BlockSpec((1, HD), lambda hk, i: (0, 0))
    sspec = pl.BlockSpec((SGRP, WIN, 1), lambda hk, i: (hk, 0, 0))
    ospec = pl.BlockSpec((SGRP, WIN, HD), lambda hk, i: (hk, rn(i), 0))
    return pl.pallas_call(
        body, name="swa_bwd", grid=(SKVH, NB),
        in_specs=[qspec, prev(16), cur(16), prev(18), cur(18), whole, whole, sspec, ospec],
        out_specs=[ospec, cur(0), cur(0), whole, whole, sspec],
        out_shape=[_sds((SQH, T, HD)), _sds((SKVH, T, HD)), _sds((SKVH, T, HD)),
                   _sds((1, HD)), _sds((1, HD)), _sds((SQH, WIN, 1))],
        scratch_shapes=[pltpu.VMEM((WIN, HD), F32), pltpu.VMEM((WIN, HD), F32)],
        compiler_params=_cparams(("arbitrary", "arbitrary")),
    )(zs_hm, zs_hm, zs_hm, zs_hm, zs_hm, qnw, knw, sinks_col, do)


GAB0 = 3 * GW + 1280


def _permute_w_in(w_in):
    return jnp.concatenate([w_in[:, :4 * GW], w_in[:, 4 * GW + 2 * GH:], w_in[:, 4 * GW:4 * GW + 2 * GH],
                            jnp.zeros((D, NP - PROJ), w_in.dtype)], axis=1)


def _unpermute_w_in(g):
    return jnp.concatenate([g[:, :4 * GW], g[:, GAB0:GAB0 + 2 * GH], g[:, 4 * GW:GAB0]], axis=1)


def _local_step(x, target, mod, n1w, w_in_p, conv_w, alog, dtb, gnw, qnw, knw, sinks, w_out, n2w, w_gu, w_down):
    T = x.shape[0]
    N = T // CHUNK
    shift1, scale1, gate1, shift2, scale2, gate2 = [mod[:, i * D:(i + 1) * D] for i in range(6)]

    h = _norm_mod_fwd(x, n1w, scale1, shift1)
    proj = _matmul(h, w_in_p, name="in_proj")
    qkv_c = _conv_fwd(proj, conv_w)
    qkv_hm = qkv_c.reshape(T, 3 * GH, HD).transpose(1, 0, 2)
    zs_hm = proj[:, 3 * GW:GAB0].reshape(T, 20, HD).transpose(1, 0, 2)
    gab = proj[:, GAB0:GAB0 + 2 * GH].T.reshape(2 * GH, N, 1, CHUNK)
    alog_b = jnp.broadcast_to(alog.reshape(GH, 1, 1), (GH, 1, CHUNK))
    dtb_b = jnp.broadcast_to(dtb.reshape(GH, 1, 1), (GH, 1, CHUNK))
    sinks_col = jnp.broadcast_to(sinks.reshape(SQH, 1, 1), (SQH, WIN, 1))
    o_g, S_all = _gdn_fwd(qkv_hm, zs_hm, gab, alog_b, dtb_b, gnw)
    o_s = _swa_fwd(zs_hm, qnw, knw, sinks_col)
    mixcat = jnp.concatenate([o_g, o_s], axis=0).transpose(1, 0, 2).reshape(T, D).astype(BF16)
    mixed = _matmul(mixcat, w_out, name="out_proj")
    x1, h2 = _resid_norm_fwd(x, mixed, gate1, n2w, scale2, shift2)
    ab = _matmul(h2, w_gu, name="ffn_up")
    act = _ffn_act_fwd(ab)
    ffn = _matmul(act, w_down, name="ffn_down")
    dy, dffn, dgate2, loss = _loss_head(x1, ffn, target, gate2)

    dact = _matmul(dffn, w_down, tb=True, name="ffn_down_dx")
    dab = _ffn_act_bwd(ab, dact)
    g_w_down = _matmul(act, dffn, ta=True, out_dtype=BF16, name="ffn_down_dw")
    g_w_gu = _matmul(h2, dab, ta=True, out_dtype=BF16, name="ffn_up_dw")
    dh2 = _matmul(dab, w_gu, tb=True, name="ffn_up_dx")
    dx1, dmixed, dgate1, dn2w, dscale2, dshift2 = _resid_norm_bwd(x, mixed, dy, dh2, gate1, n2w, scale2, shift2)
    g_w_out = _matmul(mixcat, dmixed, ta=True, out_dtype=BF16, name="out_proj_dw")
    dmix_hm = _matmul(dmixed, w_out, tb=True, name="out_proj_dx").reshape(T, 2 * GH, HD).transpose(1, 0, 2)
    dq, dk, dv, dz, dga, dgb, dalog, ddtb, dgnw = _gdn_bwd(qkv_hm, zs_hm, gab, alog_b, dtb_b, gnw, S_all,
                                                           dmix_hm[:GH])
    dqkv_hm = jnp.concatenate([dq, dk, dv], axis=0)
    dsq, dsk, dsv, dqnw, dknw, dsinks = _swa_bwd(zs_hm, qnw, knw, sinks_col, dmix_hm[GH:])
    dqkv_pre, dconv = _conv_bwd(proj, conv_w, dqkv_hm.transpose(1, 0, 2).reshape(T, 3 * GW))
    dzs = jnp.concatenate([dz, dsq, dsk, dsv], axis=0).transpose(1, 0, 2).reshape(T, 20 * HD).astype(BF16)
    dgab = jnp.concatenate([dga, dgb], axis=0).reshape(2 * GH, T).T.astype(BF16)
    dproj = jnp.concatenate([dqkv_pre, dzs, dgab, jnp.zeros((T, NP - PROJ), BF16)], axis=1)
    g_w_in_p = _matmul(h, dproj, ta=True, out_dtype=BF16, name="in_proj_dw")
    dh = _matmul(dproj, w_in_p, tb=True, name="in_proj_dx")
    grad_x, dn1w, dscale1, dshift1 = _norm_mod_bwd(x, dh, dx1, n1w, scale1, shift1)

    dmod = jnp.concatenate([dshift1, dscale1, dgate1, dshift2, dscale2, dgate2], axis=1)
    big = dict(w_in_p=g_w_in_p, w_out=g_w_out, w_gu=g_w_gu, w_down=g_w_down)
    small = dict(mod=dmod, norm1_w=dn1w, norm2_w=dn2w, conv_w=dconv, a_log=dalog[:, 0, 0], dt_bias=ddtb[:, 0, 0],
                 gdn_norm_w=dgnw, q_norm_w=dqnw, k_norm_w=dknw, sinks=dsinks[:, 0, 0])
    return loss, grad_x, big, small


def _adamw(w, g, m, v):
    m2 = ADAM_B1 * m + (1.0 - ADAM_B1) * g
    v2 = ADAM_B2 * v + (1.0 - ADAM_B2) * (g * g)
    m_hat = m2 / (1.0 - ADAM_B1 ** ADAM_STEP)
    v_hat = v2 / (1.0 - ADAM_B2 ** ADAM_STEP)
    delta = -ADAM_LR * (m_hat / (jnp.sqrt(v_hat) + ADAM_EPS) + ADAM_WD * w)
    return delta, m2, v2


def _reduce_adamw(recv, w, m, v, name):
    _, R, C = recv.shape
    tc = _tile(C, 256)

    def body(r_ref, w_ref, m_ref, v_ref, o_ref):
        g = r_ref[0].astype(F32)
        for s in range(1, N_DEV):
            g = g + r_ref[s].astype(F32)
        delta, m2, v2 = _adamw(w_ref[...], g, m_ref[...], v_ref[...])
        o_ref[0] = g
        o_ref[1] = delta
        o_ref[2] = m2
        o_ref[3] = v2

    col = pl.BlockSpec((R, tc), lambda j: (0, j))
    return pl.pallas_call(
        body, name=name, grid=(C // tc,),
        in_specs=[pl.BlockSpec((N_DEV, R, tc), lambda j: (0, 0, j)), col, col, col],
        out_specs=pl.BlockSpec((4, R, tc), lambda j: (0, 0, j)),
        out_shape=_sds((4, R, C)),
        compiler_params=_cparams(("parallel",)),
    )(recv, w, m, v)


def _adamw_call(g, w, m, v, name):
    def body(g_ref, w_ref, m_ref, v_ref, o_ref):
        delta, m2, v2 = _adamw(w_ref[...], g_ref[...], m_ref[...], v_ref[...])
        o_ref[0] = delta
        o_ref[1] = m2
        o_ref[2] = v2

    return pl.pallas_call(body, name=name, out_shape=_sds((3,) + g.shape))(g, w, m, v)


ADA_N = 6 * D // N_CHIP
KPAD = 128


def _mod_part(c8, w_ada, b_ada):
    tn = 512

    def body(c_ref, w_ref, b_ref, o_ref):
        o_ref[...] = _raw1(_silu(c_ref[...]), w_ref[...], _NN) + b_ref[...]

    return pl.pallas_call(
        body, name="ada_mod", grid=(ADA_N // tn,),
        in_specs=[pl.BlockSpec((16, D), lambda j: (0, 0)), pl.BlockSpec((D, tn), lambda j: (0, j)),
                  pl.BlockSpec((1, tn), lambda j: (0, j))],
        out_specs=pl.BlockSpec((16, tn), lambda j: (0, j)),
        out_shape=_sds((16, ADA_N)),
        compiler_params=_cparams(("parallel",)),
    )(c8, w_ada, b_ada)


def _w_ada_update(c8p, dm, w, m, v):
    tr = 256

    def body(c_ref, dm_ref, w_ref, m_ref, v_ref, o_ref):
        g = _raw1(_silu(c_ref[...]), dm_ref[...], _TN)
        delta, m2, v2 = _adamw(w_ref[...], g, m_ref[...], v_ref[...])
        o_ref[0] = g
        o_ref[1] = delta
        o_ref[2] = m2
        o_ref[3] = v2

    blk = pl.BlockSpec((tr, ADA_N), lambda i: (i, 0))
    return pl.pallas_call(
        body, name="w_ada_update", grid=(D // tr,),
        in_specs=[pl.BlockSpec((KPAD, tr), lambda i: (0, i)), pl.BlockSpec((KPAD, ADA_N), lambda i: (0, 0)),
                  blk, blk, blk],
        out_specs=pl.BlockSpec((4, tr, ADA_N), lambda i: (0, i, 0)),
        out_shape=_sds((4, D, ADA_N)),
        compiler_params=_cparams(("parallel",)),
    )(c8p, dm, w, m, v)


def _me():
    return lax.axis_index("x"), lax.axis_index("y"), lax.axis_index("c")


def _peer(k, me):
    mx, my, mc = me
    return (1 - mx if k & 4 else mx, 1 - my if k & 2 else my, 1 - mc if k & 1 else mc)


def _lin(p):
    return 4 * p[0] + 2 * p[1] + p[2]


def _remote(src, dst, ssem, rsem, dev):
    return pltpu.make_async_remote_copy(src_ref=src, dst_ref=dst, send_sem=ssem, recv_sem=rsem,
                                        device_id=dev, device_id_type=MESH)


def _all_gather8(x, name):
    def body(x_ref, out_ref, send_sems, recv_sems):
        me = _me()
        out_ref[_lin(me)] = x_ref[...]
        sends = []
        for k in range(1, N_DEV):
            cp = _remote(x_ref, out_ref.at[_lin(me)], send_sems.at[k - 1], recv_sems.at[k - 1], _peer(k, me))
            cp.start()
            sends.append(cp)
        for k in range(1, N_DEV):
            p = _peer(k, me)
            _remote(x_ref, out_ref.at[_lin(p)], send_sems.at[k - 1], recv_sems.at[k - 1], p).wait_recv()
        for cp in sends:
            cp.wait_send()

    return pl.pallas_call(
        body, name=name,
        out_shape=_sds((N_DEV,) + x.shape, x.dtype),
        in_specs=[pl.BlockSpec(memory_space=pltpu.VMEM)],
        out_specs=pl.BlockSpec(memory_space=pltpu.VMEM),
        scratch_shapes=[pltpu.SemaphoreType.DMA((N_DEV - 1,)), pltpu.SemaphoreType.DMA((N_DEV - 1,))],
    )(x)


def _gather_weights(wpk):
    def body(w_ref, out_ref, send_sems, recv_sems, local_sem):
        mx, my, mc = _me()
        mine = pltpu.make_async_copy(w_ref, out_ref.at[2 * mx + my], local_sem)
        mine.start()
        chips = [(1 - mx, my), (mx, 1 - my), (1 - mx, 1 - my)]
        sends = []
        for k, (px, py) in enumerate(chips):
            cp = _remote(w_ref, out_ref.at[2 * mx + my], send_sems.at[k], recv_sems.at[k], (px, py, mc))
            cp.start()
            sends.append(cp)
        for k, (px, py) in enumerate(chips):
            _remote(w_ref, out_ref.at[2 * px + py], send_sems.at[k], recv_sems.at[k], (px, py, mc)).wait_recv()
        for cp in sends:
            cp.wait_send()
        mine.wait()

    return pl.pallas_call(
        body, name="gather_weights",
        out_shape=_sds((N_CHIP,) + wpk.shape, wpk.dtype),
        in_specs=[pl.BlockSpec(memory_space=pl.ANY)],
        out_specs=pl.BlockSpec(memory_space=pl.ANY),
        scratch_shapes=[pltpu.SemaphoreType.DMA((N_CHIP - 1,)), pltpu.SemaphoreType.DMA((N_CHIP - 1,)),
                        pltpu.SemaphoreType.DMA],
    )(wpk)


def _grad_exchange(gpk):
    def body(g_ref, out_ref, send_sems, recv_sems, local_sem):
        me = _me()
        mx, my, mc = me
        mine = pltpu.make_async_copy(g_ref.at[2 * mx + my, mc], out_ref.at[_lin(me)], local_sem)
        mine.start()
        sends = []
        for k in range(1, N_DEV):
            p = _peer(k, me)
            cp = _remote(g_ref.at[2 * p[0] + p[1], p[2]], out_ref.at[_lin(me)], send_sems.at[k - 1],
                         recv_sems.at[k - 1], p)
            cp.start()
            sends.append(cp)
        for k in range(1, N_DEV):
            p = _peer(k, me)
            _remote(g_ref.at[2 * mx + my, mc], out_ref.at[_lin(p)], send_sems.at[k - 1], recv_sems.at[k - 1],
                    p).wait_recv()
        for cp in sends:
            cp.wait_send()
        mine.wait()

    return pl.pallas_call(
        body, name="grad_exchange",
        out_shape=_sds((N_DEV, PACK_H, D), gpk.dtype),
        in_specs=[pl.BlockSpec(memory_space=pl.ANY)],
        out_specs=pl.BlockSpec(memory_space=pl.ANY),
        scratch_shapes=[pltpu.SemaphoreType.DMA((N_DEV - 1,)), pltpu.SemaphoreType.DMA((N_DEV - 1,)),
                        pltpu.SemaphoreType.DMA],
    )(gpk)


def _sibling_exchange(halves):
    n = halves.shape[0]

    def body(h_ref, out_ref, send_sems, recv_sems, local_sems):
        mx, my, mc = _me()
        sib = (mx, my, 1 - mc)
        local, sends = [], []
        for a in range(n):
            lc = pltpu.make_async_copy(h_ref.at[a], out_ref.at[a, mc], local_sems.at[a])
            lc.start()
            local.append(lc)
            cp = _remote(h_ref.at[a], out_ref.at[a, mc], send_sems.at[a], recv_sems.at[a], sib)
            cp.start()
            sends.append(cp)
        for a in range(n):
            _remote(h_ref.at[a], out_ref.at[a, 1 - mc], send_sems.at[a], recv_sems.at[a], sib).wait_recv()
        for cp in sends:
            cp.wait_send()
        for lc in local:
            lc.wait()

    return pl.pallas_call(
        body, name="sibling_exchange",
        out_shape=_sds((n, 2) + halves.shape[1:], halves.dtype),
        in_specs=[pl.BlockSpec(memory_space=pl.ANY)],
        out_specs=pl.BlockSpec(memory_space=pl.ANY),
        scratch_shapes=[pltpu.SemaphoreType.DMA((n,)), pltpu.SemaphoreType.DMA((n,)), pltpu.SemaphoreType.DMA((n,))],
    )(halves)


def _pack_big(w_in, w_out, w_gate, w_up, w_down, dtype):
    parts = [t.reshape(-1, D).astype(dtype) for t in (w_in, w_out, w_gate, w_up, w_down)]
    parts.append(jnp.zeros((PACK_P - sum(PACK_ROWS), D), dtype))
    return jnp.concatenate(parts, axis=0)


def _unpack_big(pk):
    shapes = ((1, D, PROJ // N_CHIP), (1, D // N_CHIP, D), (1, D, DFF // N_CHIP), (1, D, DFF // N_CHIP),
              (1, DFF // N_CHIP, D))
    out, r = [], 0
    for rows, shp in zip(PACK_ROWS, shapes):
        out.append(pk[r:r + rows].reshape(shp))
        r += rows
    return out


SMALL_ORDER = (("mod", 6 * D), ("norm1_w", D), ("norm2_w", D), ("conv_w", CONVW * 3 * GW), ("a_log", GH),
               ("dt_bias", GH), ("gdn_norm_w", HD), ("q_norm_w", HD), ("k_norm_w", HD), ("sinks", SQH), ("loss", 1))
SMALL_R = 120


def _pack_small(d):
    parts = [d[k].reshape(-1).astype(F32) if k in d else jnp.zeros((n,), F32) for k, n in SMALL_ORDER]
    used = sum(n for _, n in SMALL_ORDER)
    parts.append(jnp.zeros((SMALL_R * LANE - used,), F32))
    return jnp.concatenate(parts).reshape(SMALL_R, LANE)


def _unpack_small(pk):
    flat = pk.reshape(-1)
    out, r = {}, 0
    for k, n in SMALL_ORDER:
        out[k] = flat[r:r + n]
        r += n
    return out


def kernel(x, c, w_ada, b_ada, norm1_w, w_in, conv_w, a_log, dt_bias, gdn_norm_w, q_norm_w, k_norm_w, sinks, w_out, norm2_w, w_gate, w_up, w_down, loss_target, m_w_ada, m_b_ada, m_norm1_w, m_w_in, m_conv_w, m_a_log, m_dt_bias, m_gdn_norm_w, m_q_norm_w, m_k_norm_w, m_sinks, m_w_out, m_norm2_w, m_w_gate, m_w_up, m_w_down, v_w_ada, v_b_ada, v_norm1_w, v_w_in, v_conv_w, v_a_log, v_dt_bias, v_gdn_norm_w, v_q_norm_w, v_k_norm_w, v_sinks, v_w_out, v_norm2_w, v_w_gate, v_w_up, v_w_down):
    mx, my, mc = _me()
    chip = 2 * mx + my
    dev = 4 * mx + 2 * my + mc
    T = x.shape[1]

    conv_sh = conv_w.reshape(CONVW, 3 * GW // N_CHIP)
    mine = jnp.concatenate([c.reshape(-1), conv_sh.reshape(-1), jnp.zeros((4 * LANE,), F32)]).reshape(24, LANE)
    got = _all_gather8(mine, "gather_c_conv")
    c8 = got[:, :8].reshape(N_DEV, D)
    conv_full = jnp.concatenate([got[2 * j, 8:20].reshape(CONVW, 3 * GW // N_CHIP) for j in range(N_CHIP)], axis=1)
    c16 = jnp.concatenate([c8, jnp.zeros((8, D), F32)], axis=0)
    b_sh = lax.dynamic_slice(b_ada, (0, chip * ADA_N), (1, ADA_N))
    mods = _all_gather8(_mod_part(c16, w_ada[0], b_sh), "gather_mod")
    mod = jnp.concatenate([lax.dynamic_slice(mods[2 * j], (dev, 0), (1, ADA_N)) for j in range(N_CHIP)], axis=1)

    wall = _gather_weights(_pack_big(w_in[0], w_out[0], w_gate[0], w_up[0], w_down[0], BF16))
    r0 = [0]
    for rows in PACK_ROWS:
        r0.append(r0[-1] + rows)
    piece = lambda i, shp: [wall[j, r0[i]:r0[i + 1]].reshape(shp) for j in range(N_CHIP)]
    w_in_f = jnp.concatenate(piece(0, (D, PROJ // N_CHIP)), axis=1)
    w_out_f = jnp.concatenate(piece(1, (D // N_CHIP, D)), axis=0)
    w_gu_f = jnp.concatenate(piece(2, (D, DFF // N_CHIP)) + piece(3, (D, DFF // N_CHIP)), axis=1)
    w_down_f = jnp.concatenate(piece(4, (DFF // N_CHIP, D)), axis=0)

    loss, grad_x, big, small = _local_step(
        x[0], loss_target[0], mod, norm1_w, _permute_w_in(w_in_f), conv_full, a_log, dt_bias, gdn_norm_w,
        q_norm_w, k_norm_w, sinks, w_out_f, norm2_w, w_gu_f, w_down_f)

    small["loss"] = loss[:, :1]
    sg = _all_gather8(_pack_small(small), "gather_small_grads")
    rep = dict(mod=(b_ada, m_b_ada, v_b_ada), norm1_w=(norm1_w, m_norm1_w, v_norm1_w),
               norm2_w=(norm2_w, m_norm2_w, v_norm2_w), a_log=(a_log, m_a_log, v_a_log),
               dt_bias=(dt_bias, m_dt_bias, v_dt_bias), gdn_norm_w=(gdn_norm_w, m_gdn_norm_w, v_gdn_norm_w),
               q_norm_w=(q_norm_w, m_q_norm_w, v_q_norm_w), k_norm_w=(k_norm_w, m_k_norm_w, v_k_norm_w),
               sinks=(sinks, m_sinks, v_sinks))
    wmv = [_pack_small({k: t[i] for k, t in rep.items()}) for i in range(3)]
    sres = _reduce_adamw(sg, wmv[0], wmv[1], wmv[2], "small_reduce_adamw")
    s_g, s_d, s_m, s_v = [_unpack_small(sres[i]) for i in range(4)]
    loss_out = s_g["loss"][0]

    g_conv = lax.dynamic_slice(s_g["conv_w"].reshape(CONVW, 3 * GW), (0, chip * (3 * GW // N_CHIP)),
                               (CONVW, 3 * GW // N_CHIP))
    pad16 = lambda t: jnp.concatenate([t.reshape(12, LANE), jnp.zeros((4, LANE), F32)], axis=0)
    cres = _adamw_call(pad16(g_conv), pad16(conv_w), pad16(m_conv_w), pad16(v_conv_w), "conv_adamw")
    conv_out = [g_conv.reshape(conv_w.shape)] + [cres[i, :12].reshape(conv_w.shape) for i in range(3)]

    dmod8 = sg[:, :6 * D // LANE].reshape(N_DEV, 6 * D)
    dm = lax.dynamic_slice(dmod8, (0, chip * ADA_N), (N_DEV, ADA_N))
    zpad = lambda t: jnp.concatenate([t, jnp.zeros((KPAD - N_DEV, t.shape[1]), F32)], axis=0)
    ares = _w_ada_update(zpad(c8), zpad(dm), w_ada[0], m_w_ada[0], v_w_ada[0])

    g_in = _unpermute_w_in(big["w_in_p"])
    g_gate, g_up = big["w_gu"][:, :DFF], big["w_gu"][:, DFF:]
    csl = lambda g, j, n: g[:, j * n:(j + 1) * n]
    rsl = lambda g, j, n: g[j * n:(j + 1) * n]
    gpk = jnp.stack([_pack_big(csl(g_in, j, PROJ // N_CHIP), rsl(big["w_out"], j, D // N_CHIP),
                               csl(g_gate, j, DFF // N_CHIP), csl(g_up, j, DFF // N_CHIP),
                               rsl(big["w_down"], j, DFF // N_CHIP), BF16) for j in range(N_CHIP)])
    recv = _grad_exchange(gpk.reshape(N_CHIP, 2, PACK_H, D))
    half = lambda a, b, c_, d_, e_: lax.dynamic_slice(_pack_big(a[0], b[0], c_[0], d_[0], e_[0], F32),
                                                      (mc * PACK_H, 0), (PACK_H, D))
    hres = _reduce_adamw(recv, half(w_in, w_out, w_gate, w_up, w_down),
                         half(m_w_in, m_w_out, m_w_gate, m_w_up, m_w_down),
                         half(v_w_in, v_w_out, v_w_gate, v_w_up, v_w_down), "big_reduce_adamw")
    full = _sibling_exchange(hres).reshape(4, PACK_P, D)
    bg, bd, bm, bv = [_unpack_big(full[i]) for i in range(4)]

    def group(a_i, small_d, conv_i, big_l):
        s = lambda k, ref: small_d[k].reshape(ref.shape)
        return [ares[a_i][None], s("mod", b_ada), s("norm1_w", norm1_w), big_l[0], conv_out[conv_i],
                s("a_log", a_log), s("dt_bias", dt_bias), s("gdn_norm_w", gdn_norm_w), s("q_norm_w", q_norm_w),
                s("k_norm_w", k_norm_w), s("sinks", sinks), big_l[1], s("norm2_w", norm2_w), big_l[2], big_l[3],
                big_l[4]]

    outs = [loss_out, grad_x[None]]
    outs += group(0, s_g, 0, bg) + group(1, s_d, 1, bd) + group(2, s_m, 2, bm) + group(3, s_v, 3, bv)
    return tuple(outs)
```

```python
import functools

import jax
import jax.numpy as jnp
from jax import lax
from jax.experimental import pallas as pl
from jax.experimental.pallas import tpu as pltpu

F32 = jnp.float32
BF16 = jnp.bfloat16
MESH = pl.DeviceIdType.MESH

D = 1024
HD = 64
GH = 8
GW = GH * HD
SQH = 8
SKVH = 2
SGRP = SQH // SKVH
WIN = 128
CONVW = 4
CHUNK = 64
DFF = 2816
PROJ = 2832
NP = 3072
EPS = 1e-6
N_DEV = 8
N_CHIP = 4

ADAM_LR = 0.001
ADAM_B1 = 0.9
ADAM_B2 = 0.999
ADAM_EPS = 1e-08
ADAM_WD = 0.01
ADAM_STEP = 10

VMEM_LIMIT = 48 * 1024 * 1024
LANE = 128

PACK_ROWS = (PROJ // N_CHIP, D // N_CHIP, DFF // N_CHIP, DFF // N_CHIP, DFF // N_CHIP)
PACK_P = 3104
PACK_H = PACK_P // 2


def _cparams(sem=None):
    return pltpu.CompilerParams(dimension_semantics=sem, vmem_limit_bytes=VMEM_LIMIT)


_NN = ((1,), (0,))
_NT = ((1,), (1,))
_TN = ((0,), (0,))


def _dot(a, b, dims):
    if a.ndim == 3:
        (ca,), (cb,) = dims
        return lax.dot_general(a, b, (((ca + 1,), (cb + 1,)), ((0,), (0,))), preferred_element_type=F32)
    return lax.dot_general(a, b, (dims, ((), ())), preferred_element_type=F32)


def _raw1(a, b, dims):
    return _dot(a.astype(BF16), b.astype(BF16), dims)


def _raw3(a, b, dims):
    ah = a.astype(BF16)
    al = (a - ah.astype(F32)).astype(BF16)
    bh = b.astype(BF16)
    bl = (b - bh.astype(F32)).astype(BF16)
    return _dot(ah, bh, dims) + (_dot(al, bh, dims) + _dot(ah, bl, dims))


def _make_diff_mm(raw):
    @jax.custom_vjp
    def nn(a, b):
        return raw(a, b, _NN)

    @jax.custom_vjp
    def nt(a, b):
        return raw(a, b, _NT)

    @jax.custom_vjp
    def tn(a, b):
        return raw(a, b, _TN)

    nn.defvjp(lambda a, b: (raw(a, b, _NN), (a, b)), lambda r, g: (nt(g, r[1]), tn(r[0], g)))
    nt.defvjp(lambda a, b: (raw(a, b, _NT), (a, b)), lambda r, g: (nn(g, r[1]), tn(g, r[0])))
    tn.defvjp(lambda a, b: (raw(a, b, _TN), (a, b)), lambda r, g: (nt(r[1], g), nn(r[0], g)))
    return nn, nt, tn


def _tri_inv_raw(a, nn3):
    n = a.shape[-1]
    ri = lax.broadcasted_iota(jnp.int32, (n, n), 0)
    ci = lax.broadcasted_iota(jnp.int32, (n, n), 1)
    t = (ri == ci).astype(F32)
    for lvl in range((n - 1).bit_length()):
        same_pair = (ri >> (lvl + 1)) == (ci >> (lvl + 1))
        lower_left = (((ri >> lvl) & 1) == 1) & (((ci >> lvl) & 1) == 0)
        y = jnp.where(same_pair & lower_left, a, 0.0)
        t = t - y if lvl == 0 else t - nn3(nn3(t, y), t)
    return t


class _Kit:
    def __init__(self, diff):
        if diff:
            self.nn, self.nt, self.tn = _make_diff_mm(_raw1)
            self.nn3, self.nt3, self.tn3 = _make_diff_mm(_raw3)
            nn3, nt3, tn3 = self.nn3, self.nt3, self.tn3

            @jax.custom_vjp
            def inv(a):
                return _tri_inv_raw(a, nn3)

            def inv_fwd(a):
                t = _tri_inv_raw(a, nn3)
                return t, t

            def inv_bwd(t, g):
                return (-tn3(t, nt3(g, t)),)

            inv.defvjp(inv_fwd, inv_bwd)
            self.inv = inv
        else:
            self.nn = lambda a, b: _raw1(a, b, _NN)
            self.nt = lambda a, b: _raw1(a, b, _NT)
            self.tn = lambda a, b: _raw1(a, b, _TN)
            self.nn3 = lambda a, b: _raw3(a, b, _NN)
            self.nt3 = lambda a, b: _raw3(a, b, _NT)
            self.tn3 = lambda a, b: _raw3(a, b, _TN)
            self.inv = lambda a: _tri_inv_raw(a, self.nn3)


def _sigmoid(x):
    return 1.0 / (1.0 + jnp.exp(-x))


def _silu(x):
    return x * _sigmoid(x)


def _rms(x, w):
    return x * lax.rsqrt(jnp.mean(x * x, axis=-1, keepdims=True) + EPS) * w


def _tile(dim, target):
    t = (min(dim, target) // LANE) * LANE
    while t >= LANE:
        if dim % t == 0:
            return t
        t -= LANE
    return dim


def _matmul(a, b, ta=False, tb=False, out_dtype=F32, name="matmul"):
    if ta:
        K, M = a.shape
    else:
        M, K = a.shape
    if tb:
        N, K2 = b.shape
    else:
        K2, N = b.shape
    assert K == K2, (a.shape, b.shape, ta, tb)
    tm, tn, tk = _tile(M, 1024), _tile(N, 1536), _tile(K, 1024)
    nk = K // tk
    dims = ((0,) if ta else (1,), (1,) if tb else (0,))

    def body(a_ref, b_ref, o_ref, acc_ref):
        k = pl.program_id(2)

        @pl.when(k == 0)
        def _():
            acc_ref[...] = jnp.zeros_like(acc_ref)

        acc_ref[...] += _dot(a_ref[...].astype(BF16), b_ref[...].astype(BF16), dims)

        @pl.when(k == nk - 1)
        def _():
            o_ref[...] = acc_ref[...].astype(o_ref.dtype)

    a_spec = (pl.BlockSpec((tk, tm), lambda i, j, k: (k, i)) if ta
              else pl.BlockSpec((tm, tk), lambda i, j, k: (i, k)))
    b_spec = (pl.BlockSpec((tn, tk), lambda i, j, k: (j, k)) if tb
              else pl.BlockSpec((tk, tn), lambda i, j, k: (k, j)))
    return pl.pallas_call(
        body, name=name,
        grid=(M // tm, N // tn, nk),
        in_specs=[a_spec, b_spec],
        out_specs=pl.BlockSpec((tm, tn), lambda i, j, k: (i, j)),
        out_shape=jax.ShapeDtypeStruct((M, N), out_dtype),
        scratch_shapes=[pltpu.VMEM((tm, tn), F32)],
        compiler_params=_cparams(("parallel", "parallel", "arbitrary")),
    )(a, b)


def _rowcall(fn, tiled, consts, out_tiled, out_acc, tm, name):
    T = tiled[0].shape[0]
    n_in = len(tiled) + len(consts)
    n_o = len(out_tiled)

    def body(*refs):
        vals = [r[...] for r in refs[:n_in]]
        outs = refs[n_in:]
        res = fn(*vals)
        for r, v in zip(outs[:n_o], res[:n_o]):
            r[...] = v.astype(r.dtype)
        if len(outs) > n_o:
            @pl.when(pl.program_id(0) == 0)
            def _():
                for r in outs[n_o:]:
                    r[...] = jnp.zeros_like(r)

            for r, v in zip(outs[n_o:], res[n_o:]):
                r[...] += v

    in_specs = [pl.BlockSpec((tm, a.shape[1]), lambda i: (i, 0)) for a in tiled]
    in_specs += [pl.BlockSpec(a.shape, lambda i, nd=a.ndim: (0,) * nd) for a in consts]
    out_specs = [pl.BlockSpec((tm, s.shape[1]), lambda i: (i, 0)) for s in out_tiled]
    out_specs += [pl.BlockSpec(s.shape, lambda i: (0, 0)) for s in out_acc]
    return pl.pallas_call(
        body, name=name, grid=(T // tm,),
        in_specs=in_specs, out_specs=out_specs,
        out_shape=list(out_tiled) + list(out_acc),
        compiler_params=_cparams(("arbitrary",)),
    )(*tiled, *consts)


def _sds(shape, dtype=F32):
    return jax.ShapeDtypeStruct(shape, dtype)


def _norm_mod(x, nw, scale, shift):
    return _rms(x, nw) * (1.0 + scale) + shift


def _norm_mod_fwd(x, nw, scale, shift):
    T = x.shape[0]
    (h,) = _rowcall(lambda *a: (_norm_mod(*a),), [x], [nw, scale, shift],
                    [_sds((T, D), BF16)], [], 512, "norm1_fwd")
    return h


def _norm_mod_bwd(x, dh, dres, nw, scale, shift):
    T = x.shape[0]

    def fn(x, dh, dres, nw, scale, shift):
        _, vjp = jax.vjp(_norm_mod, x, nw, scale, shift)
        dx, dnw, dsc, dsh = vjp(dh)
        return dx + dres, dnw, dsc, dsh

    return _rowcall(fn, [x, dh, dres], [nw, scale, shift], [_sds((T, D))],
                    [_sds((1, D))] * 3, 256, "norm1_bwd")


def _resid_norm(x, mixed, gate1, nw, scale, shift):
    x1 = x + gate1 * mixed
    return x1, _norm_mod(x1, nw, scale, shift)


def _resid_norm_fwd(x, mixed, gate1, nw, scale, shift):
    T = x.shape[0]
    return _rowcall(_resid_norm, [x, mixed], [gate1, nw, scale, shift],
                    [_sds((T, D)), _sds((T, D), BF16)], [], 512, "resid_norm2_fwd")


def _resid_norm_bwd(x, mixed, dy, dh2, gate1, nw, scale, shift):
    T = x.shape[0]

    def fn(x, mixed, dy, dh2, gate1, nw, scale, shift):
        _, vjp = jax.vjp(_resid_norm, x, mixed, gate1, nw, scale, shift)
        dx, dmixed, dg1, dnw, dsc, dsh = vjp((dy, dh2))
        return dx, dmixed, dg1, dnw, dsc, dsh

    return _rowcall(fn, [x, mixed, dy, dh2], [gate1, nw, scale, shift],
                    [_sds((T, D)), _sds((T, D), BF16)], [_sds((1, D))] * 4, 256, "resid_norm2_bwd")


def _ffn_act_fwd(ab):
    T = ab.shape[0]

    def fn(ab):
        a, b = ab[:, :DFF], ab[:, DFF:]
        return (_silu(a) * b,)

    (act,) = _rowcall(fn, [ab], [], [_sds((T, DFF), BF16)], [], 256, "ffn_act_fwd")
    return act


def _ffn_act_bwd(ab, dact):
    T = ab.shape[0]

    def fn(ab, dact):
        a, b = ab[:, :DFF], ab[:, DFF:]
        s = _sigmoid(a)
        da = dact * b * (s * (1.0 + a * (1.0 - s)))
        db = dact * (a * s)
        return (jnp.concatenate([da, db], axis=1),)

    (dab,) = _rowcall(fn, [ab, dact], [], [_sds((T, 2 * DFF), BF16)], [], 256, "ffn_act_bwd")
    return dab


def _loss_head(x1, ffn, target, gate2):
    T = x1.shape[0]

    def fn(x1, ffn, target, gate2):
        y = x1 + gate2 * ffn
        err = y - target
        loss = 0.5 * jnp.sum(jnp.sum(err * err, axis=1, keepdims=True), axis=0, keepdims=True) / D
        dy = err * (1.0 / D)
        dgate2 = jnp.sum(dy * ffn, axis=0, keepdims=True)
        return dy, gate2 * dy, dgate2, jnp.broadcast_to(loss, (1, LANE))

    return _rowcall(fn, [x1, ffn, target], [gate2], [_sds((T, D)), _sds((T, D), BF16)],
                    [_sds((1, D)), _sds((1, LANE))], 256, "loss_head")


def _round_bf16(x):
    return x.astype(BF16).astype(F32)


def _shift_down(x, s, rows):
    if s == 0:
        return x
    return jnp.where(rows >= s, pltpu.roll(x, s, 0), 0.0)


def _shift_up(x, s, rows, T):
    if s == 0:
        return x
    return jnp.where(rows < T - s, pltpu.roll(x, T - s, 0), 0.0)


def _conv_fwd(proj, conv_w):
    T = proj.shape[0]
    ncol = 3 * GW // LANE

    def body(x_ref, w_ref, o_ref):
        x = _round_bf16(x_ref[...])
        rows = lax.broadcasted_iota(jnp.int32, x.shape, 0)
        acc = jnp.zeros_like(x)
        for j in range(CONVW):
            acc = acc + _round_bf16(w_ref[pl.ds(j, 1), :]) * _shift_down(x, CONVW - 1 - j, rows)
        o_ref[...] = _silu(acc)

    return pl.pallas_call(
        body, name="conv_fwd", grid=(ncol,),
        in_specs=[pl.BlockSpec((T, LANE), lambda j: (0, j)), pl.BlockSpec((CONVW, LANE), lambda j: (0, j))],
        out_specs=pl.BlockSpec((T, LANE), lambda j: (0, j)),
        out_shape=_sds((T, 3 * GW)),
        compiler_params=_cparams(("parallel",)),
    )(proj, conv_w)


def _conv_bwd(proj, conv_w, dqc):
    T = proj.shape[0]
    ncol = 3 * GW // LANE

    def body(x_ref, w_ref, d_ref, dx_ref, dw_ref):
        x = _round_bf16(x_ref[...])
        rows = lax.broadcasted_iota(jnp.int32, x.shape, 0)
        xs = [_shift_down(x, CONVW - 1 - j, rows) for j in range(CONVW)]
        w = [_round_bf16(w_ref[pl.ds(j, 1), :]) for j in range(CONVW)]
        pre = jnp.zeros_like(x)
        for j in range(CONVW):
            pre = pre + w[j] * xs[j]
        s = _sigmoid(pre)
        dpre = _round_bf16(d_ref[...] * (s * (1.0 + pre * (1.0 - s))))
        dx = jnp.zeros_like(x)
        for j in range(CONVW):
            dx = dx + w[j] * _shift_up(dpre, CONVW - 1 - j, rows, T)
            dw_ref[pl.ds(j, 1), :] = jnp.sum(dpre * xs[j], axis=0, keepdims=True)
        dx_ref[...] = dx.astype(dx_ref.dtype)

    return pl.pallas_call(
        body, name="conv_bwd", grid=(ncol,),
        in_specs=[pl.BlockSpec((T, LANE), lambda j: (0, j)), pl.BlockSpec((CONVW, LANE), lambda j: (0, j)),
                  pl.BlockSpec((T, LANE), lambda j: (0, j))],
        out_specs=[pl.BlockSpec((T, LANE), lambda j: (0, j)), pl.BlockSpec((CONVW, LANE), lambda j: (0, j))],
        out_shape=[_sds((T, 3 * GW), BF16), _sds((CONVW, 3 * GW))],
        compiler_params=_cparams(("parallel",)),
    )(proj, conv_w, dqc)


def _gdn_prep(kit, q, k, v, ga, gb, alog, dtb):
    C = CHUNK
    ri = lax.broadcasted_iota(jnp.int32, (C, C), 0)
    ci = lax.broadcasted_iota(jnp.int32, (C, C), 1)
    causal = ri >= ci
    strict = ri > ci
    eye = (ri == ci).astype(F32)
    lower = causal.astype(F32)
    upper = (ri <= ci).astype(F32)

    a = ga + dtb
    softplus = jnp.maximum(a, 0.0) + jnp.log(1.0 + jnp.exp(-jnp.abs(a)))
    g_row = -jnp.exp(alog) * softplus
    beta_row = _sigmoid(gb)
    g_col = jnp.sum(eye * g_row, axis=2, keepdims=True)
    beta_col = jnp.sum(eye * beta_row, axis=2, keepdims=True)
    G_col = jnp.sum(lower * g_row, axis=2, keepdims=True)
    G_row = jnp.sum(upper * g_col, axis=1, keepdims=True)
    G_last = jnp.sum(g_row, axis=2, keepdims=True)
    decay = jnp.exp(jnp.where(causal, G_col - G_row, -1e30))

    qn = q * lax.rsqrt(jnp.sum(q * q, axis=-1, keepdims=True) + EPS) * (HD ** -0.5)
    kn = k * lax.rsqrt(jnp.sum(k * k, axis=-1, keepdims=True) + EPS)
    kb = kn * beta_col
    A = jnp.where(strict, kit.nt(kb, kn) * decay, 0.0)
    Tm = kit.inv(A)
    eG = jnp.exp(G_col)
    u = kit.nn3(Tm, v * beta_col)
    w = kit.nn3(Tm, kb * eG)
    qk = jnp.where(causal, kit.nt(qn, kn) * decay, 0.0)
    q_dec = qn * eG
    k_dec = kn * jnp.exp(G_last - G_col)
    dec = jnp.exp(G_last)
    return u, w, qk, q_dec, k_dec, dec


def _gdn_out(o, z, nw):
    return _rms(o, nw) * _silu(z)


GDN_CB = 4


def _gdn_specs(T, blk):
    TB = GDN_CB * CHUNK
    seq = lambda grp: pl.BlockSpec((GH, TB, HD), lambda i, grp=grp: (grp, blk(i), 0))
    row = lambda grp: pl.BlockSpec((GH, GDN_CB, 1, CHUNK), lambda i, grp=grp: (grp, blk(i), 0, 0))
    per_head = pl.BlockSpec((GH, 1, CHUNK), lambda i: (0, 0, 0))
    whole = pl.BlockSpec((1, HD), lambda i: (0, 0))
    state = pl.BlockSpec((GH, GDN_CB, HD, HD), lambda i: (0, blk(i), 0, 0))
    return seq, row, per_head, whole, state


def _gdn_load(seq_refs, row_refs, head_refs):
    chunks = lambda r: jnp.concatenate([r[:, pl.ds(cb * CHUNK, CHUNK), :] for cb in range(GDN_CB)], axis=0)
    rows = lambda r: jnp.concatenate([r[:, cb] for cb in range(GDN_CB)], axis=0)
    heads = lambda r: jnp.concatenate([r[...]] * GDN_CB, axis=0)
    return [chunks(r) for r in seq_refs], [rows(r) for r in row_refs], [heads(r) for r in head_refs]


def _gdn_fwd(qkv_hm, zs_hm, gab, alog_b, dtb_b, nw):
    T = qkv_hm.shape[1]
    N = T // CHUNK
    seq, row, per_head, whole, state = _gdn_specs(T, lambda i: i)
    kit = _Kit(False)

    def body(q_ref, k_ref, v_ref, z_ref, ga_ref, gb_ref, al_ref, dt_ref, nw_ref, o_ref, S_ref, S_scr):
        @pl.when(pl.program_id(0) == 0)
        def _():
            S_scr[...] = jnp.zeros_like(S_scr)

        (q, k, v, z), (ga, gb), (al, dt) = _gdn_load((q_ref, k_ref, v_ref, z_ref), (ga_ref, gb_ref), (al_ref, dt_ref))
        u, w, qk, q_dec, k_dec, dec = _gdn_prep(kit, q, k, v, ga, gb, al, dt)
        S = S_scr[...]
        for cb in range(GDN_CB):
            hs = slice(cb * GH, (cb + 1) * GH)
            S_ref[:, cb] = S
            v_new = u[hs] - kit.nn(w[hs], S)
            o = kit.nn(q_dec[hs], S) + kit.nn(qk[hs], v_new)
            S = S * dec[hs] + kit.tn(k_dec[hs], v_new)
            o_ref[:, pl.ds(cb * CHUNK, CHUNK), :] = _gdn_out(o, z[hs], nw_ref[...])
        S_scr[...] = S

    return pl.pallas_call(
        body, name="gdn_fwd", grid=(N // GDN_CB,),
        in_specs=[seq(0), seq(1), seq(2), seq(0), row(0), row(1), per_head, per_head, whole],
        out_specs=[seq(0), state],
        out_shape=[_sds((GH, T, HD)), _sds((GH, N, HD, HD))],
        scratch_shapes=[pltpu.VMEM((GH, HD, HD), F32)],
        compiler_params=_cparams(("arbitrary",)),
    )(qkv_hm, qkv_hm, qkv_hm, zs_hm, gab, gab, alog_b, dtb_b, nw)


def _gdn_bwd(qkv_hm, zs_hm, gab, alog_b, dtb_b, nw, S_all, do):
    T = qkv_hm.shape[1]
    N = T // CHUNK
    nblk = N // GDN_CB
    dkit, kit = _Kit(True), _Kit(False)
    rseq, rrow, per_head, whole, rstate = _gdn_specs(T, lambda i: nblk - 1 - i)

    def body(q_ref, k_ref, v_ref, z_ref, ga_ref, gb_ref, al_ref, dt_ref, nw_ref, S_ref, do_ref,
             dq_ref, dk_ref, dv_ref, dz_ref, dga_ref, dgb_ref, dal_ref, ddt_ref, dnw_ref, dS_scr):
        @pl.when(pl.program_id(0) == 0)
        def _():
            dS_scr[...] = jnp.zeros_like(dS_scr)
            dal_ref[...] = jnp.zeros_like(dal_ref)
            ddt_ref[...] = jnp.zeros_like(ddt_ref)
            dnw_ref[...] = jnp.zeros_like(dnw_ref)

        (q, k, v, z, dout), (ga, gb), (al, dt) = _gdn_load((q_ref, k_ref, v_ref, z_ref, do_ref), (ga_ref, gb_ref),
                                                          (al_ref, dt_ref))
        S_in = jnp.concatenate([S_ref[:, cb] for cb in range(GDN_CB)], axis=0)
        (u, w, qk, q_dec, k_dec, dec), prep_vjp = jax.vjp(functools.partial(_gdn_prep, dkit), q, k, v, ga, gb, al, dt)
        v_new = u - kit.nn(w, S_in)
        o = kit.nn(q_dec, S_in) + kit.nn(qk, v_new)
        _, out_vjp = jax.vjp(_gdn_out, o, z, nw_ref[...])
        do, dz, dnw = out_vjp(dout)
        dvn_part = kit.tn(qk, do)
        dS_part = kit.tn(q_dec, do)
        dS = dS_scr[...]
        dS_out, dvn = [None] * GDN_CB, [None] * GDN_CB
        for cb in reversed(range(GDN_CB)):
            hs = slice(cb * GH, (cb + 1) * GH)
            dS_out[cb] = dS
            dvn[cb] = dvn_part[hs] + kit.nn(k_dec[hs], dS)
            dS = dS * dec[hs] + dS_part[hs] - kit.tn(w[hs], dvn[cb])
        dS_scr[...] = dS
        dS_out = jnp.concatenate(dS_out, axis=0)
        dvn = jnp.concatenate(dvn, axis=0)
        ddec = jnp.sum(jnp.sum(S_in * dS_out, axis=2, keepdims=True), axis=1, keepdims=True)
        cts = (dvn, -kit.nt(dvn, S_in), kit.nt(do, v_new), kit.nt(do, S_in), kit.nt(v_new, dS_out), ddec)
        dq, dk, dv, dga, dgb, dal, ddt = prep_vjp(cts)
        lanesum = lambda t: jnp.broadcast_to(jnp.sum(t, axis=2, keepdims=True), t.shape)
        for cb in range(GDN_CB):
            hs = slice(cb * GH, (cb + 1) * GH)
            sl = pl.ds(cb * CHUNK, CHUNK)
            dq_ref[:, sl, :] = dq[hs]
            dk_ref[:, sl, :] = dk[hs]
            dv_ref[:, sl, :] = dv[hs]
            dz_ref[:, sl, :] = dz[hs]
            dga_ref[:, cb] = dga[hs]
            dgb_ref[:, cb] = dgb[hs]
            dal_ref[...] += lanesum(dal[hs])
            ddt_ref[...] += lanesum(ddt[hs])
        dnw_ref[...] += dnw

    return pl.pallas_call(
        body, name="gdn_bwd", grid=(nblk,),
        in_specs=[rseq(0), rseq(1), rseq(2), rseq(0), rrow(0), rrow(1), per_head, per_head, whole, rstate, rseq(0)],
        out_specs=[rseq(0), rseq(0), rseq(0), rseq(0), rrow(0), rrow(0), per_head, per_head, whole],
        out_shape=[_sds((GH, T, HD))] * 4 + [_sds((GH, N, 1, CHUNK))] * 2 + [_sds((GH, 1, CHUNK))] * 2
                  + [_sds((1, HD))],
        scratch_shapes=[pltpu.VMEM((GH, HD, HD), F32)],
        compiler_params=_cparams(("arbitrary",)),
    )(qkv_hm, qkv_hm, qkv_hm, zs_hm, gab, gab, alog_b, dtb_b, nw, S_all, do)


def _swa_block(kit, first, q0, q1, q2, q3, kp, kc, vp, vc, qnw, knw, s0, s1, s2, s3, *, slopes):
    W = WIN
    ri = lax.broadcasted_iota(jnp.int32, (W, W), 0)
    ci = lax.broadcasted_iota(jnp.int32, (W, W), 1)
    mask_c = ri >= ci
    mask_p = ci > ri + first * W
    dist_c = (ri - ci).astype(F32)
    dist_p = (ri - ci + W).astype(F32)
    kpn = _rms(kp, knw)
    kcn = _rms(kc, knw)
    outs = []
    for q, sink, slope in zip((q0, q1, q2, q3), (s0, s1, s2, s3), slopes):
        qn = _rms(q, qnw)
        sc = jnp.where(mask_c, kit.nt(qn, kcn) * (HD ** -0.5) - slope * dist_c, -1e30)
        sp = jnp.where(mask_p, kit.nt(qn, kpn) * (HD ** -0.5) - slope * dist_p, -1e30)
        m = jnp.maximum(jnp.maximum(jnp.max(sc, axis=-1, keepdims=True), jnp.max(sp, axis=-1, keepdims=True)), sink)
        m = lax.stop_gradient(m)
        pc = jnp.exp(sc - m)
        pp = jnp.exp(sp - m)
        den = jnp.sum(pc, axis=-1, keepdims=True) + jnp.sum(pp, axis=-1, keepdims=True) + jnp.exp(sink - m)
        inv = 1.0 / den
        outs.append(kit.nn(pc * inv, vc) + kit.nn(pp * inv, vp))
    return tuple(outs)


def _swa_slopes(hk):
    return tuple(jnp.where(hk == 0, 2.0 ** (-8.0 * (g + 1.0) / SQH), 2.0 ** (-8.0 * (SGRP + g + 1.0) / SQH))
                 for g in range(SGRP))


def _swa_fwd(zs_hm, qnw, knw, sinks_col):
    T = zs_hm.shape[1]
    NB = T // WIN
    kit = _Kit(False)

    def body(q_ref, kp_ref, kc_ref, vp_ref, vc_ref, qnw_ref, knw_ref, s_ref, o_ref):
        hk = pl.program_id(0)
        first = (pl.program_id(1) == 0).astype(jnp.int32)
        args = ([q_ref[g] for g in range(SGRP)] + [kp_ref[...], kc_ref[...], vp_ref[...], vc_ref[...],
                                                     qnw_ref[...], knw_ref[...]] + [s_ref[g] for g in range(SGRP)])
        outs = _swa_block(kit, first, *args, slopes=_swa_slopes(hk))
        for g in range(SGRP):
            o_ref[g] = outs[g]

    qspec = pl.BlockSpec((SGRP, WIN, HD), lambda hk, n: (2 + hk, n, 0))
    cur = lambda off: pl.BlockSpec((None, WIN, HD), lambda hk, n, off=off: (off + hk, n, 0))
    prev = lambda off: pl.BlockSpec((None, WIN, HD), lambda hk, n, off=off: (off + hk, jnp.maximum(n - 1, 0), 0))
    whole = pl.BlockSpec((1, HD), lambda hk, n: (0, 0))
    sspec = pl.BlockSpec((SGRP, WIN, 1), lambda hk, n: (hk, 0, 0))
    return pl.pallas_call(
        body, name="swa_fwd", grid=(SKVH, NB),
        in_specs=[qspec, prev(16), cur(16), prev(18), cur(18), whole, whole, sspec],
        out_specs=pl.BlockSpec((SGRP, WIN, HD), lambda hk, n: (hk, n, 0)),
        out_shape=_sds((SQH, T, HD)),
        compiler_params=_cparams(("parallel", "arbitrary")),
    )(zs_hm, zs_hm, zs_hm, zs_hm, zs_hm, qnw, knw, sinks_col)


def _swa_bwd(zs_hm, qnw, knw, sinks_col, do):
    T = zs_hm.shape[1]
    NB = T // WIN
    kit = _Kit(True)

    def body(q_ref, kp_ref, kc_ref, vp_ref, vc_ref, qnw_ref, knw_ref, s_ref, do_ref,
             dq_ref, dk_ref, dv_ref, dqnw_ref, dknw_ref, ds_ref, ck_scr, cv_scr):
        hk = pl.program_id(0)
        i = pl.program_id(1)
        first = (i == NB - 1).astype(jnp.int32)

        @pl.when(i == 0)
        def _():
            ck_scr[...] = jnp.zeros_like(ck_scr)
            cv_scr[...] = jnp.zeros_like(cv_scr)
            ds_ref[...] = jnp.zeros_like(ds_ref)

        @pl.when((i == 0) & (hk == 0))
        def _():
            dqnw_ref[...] = jnp.zeros_like(dqnw_ref)
            dknw_ref[...] = jnp.zeros_like(dknw_ref)

        args = ([q_ref[g] for g in range(SGRP)] + [kp_ref[...], kc_ref[...], vp_ref[...], vc_ref[...],
                                                     qnw_ref[...], knw_ref[...]] + [s_ref[g] for g in range(SGRP)])
        dos = tuple(do_ref[g] for g in range(SGRP))
        _, vjp = jax.vjp(functools.partial(_swa_block, kit, first, slopes=_swa_slopes(hk)), *args)
        gr = vjp(dos)
        for g in range(SGRP):
            dq_ref[g] = gr[g]
            ds_ref[g] += jnp.broadcast_to(jnp.sum(gr[10 + g], axis=0, keepdims=True), (WIN, 1))
        dkp, dkc, dvp, dvc = gr[4:8]
        dk_ref[...] = dkc + ck_scr[...]
        dv_ref[...] = dvc + cv_scr[...]
        ck_scr[...] = dkp
        cv_scr[...] = dvp
        dqnw_ref[...] += gr[8]
        dknw_ref[...] += gr[9]

    rn = lambda n: NB - 1 - n
    qspec = pl.BlockSpec((SGRP, WIN, HD), lambda hk, i: (2 + hk, rn(i), 0))
    cur = lambda off: pl.BlockSpec((None, WIN, HD), lambda hk, i, off=off: (off + hk, rn(i), 0))
    prev = lambda off: pl.BlockSpec((None, WIN, HD), lambda hk, i, off=off: (off + hk, jnp.maximum(rn(i) - 1, 0), 0))
    whole = pl.BlockSpec((1, HD), lambda hk, i: (0, 0))
    sspec = pl.BlockSpec((SGRP, WIN, 1), lambda hk, i: (hk, 0, 0))
    ospec = pl.BlockSpec((SGRP, WIN, HD), lambda hk, i: (hk, rn(i), 0))
    return pl.pallas_call(
        body, name="swa_bwd", grid=(SKVH, NB),
        in_specs=[qspec, prev(16), cur(16), prev(18), cur(18), whole, whole, sspec, ospec],
        out_specs=[ospec, cur(0), cur(0), whole, whole, sspec],
        out_shape=[_sds((SQH, T, HD)), _sds((SKVH, T, HD)), _sds((SKVH, T, HD)),
                   _sds((1, HD)), _sds((1, HD)), _sds((SQH, WIN, 1))],
        scratch_shapes=[pltpu.VMEM((WIN, HD), F32), pltpu.VMEM((WIN, HD), F32)],
        compiler_params=_cparams(("arbitrary", "arbitrary")),
    )(zs_hm, zs_hm, zs_hm, zs_hm, zs_hm, qnw, knw, sinks_col, do)


GAB0 = 3 * GW + 1280


def _permute_w_in(w_in):
    return jnp.concatenate([w_in[:, :4 * GW], w_in[:, 4 * GW + 2 * GH:], w_in[:, 4 * GW:4 * GW + 2 * GH],
                            jnp.zeros((D, NP - PROJ), w_in.dtype)], axis=1)


def _unpermute_w_in(g):
    return jnp.concatenate([g[:, :4 * GW], g[:, GAB0:GAB0 + 2 * GH], g[:, 4 * GW:GAB0]], axis=1)


def _local_step(x, target, mod, n1w, w_in_p, conv_w, alog, dtb, gnw, qnw, knw, sinks, w_out, n2w, w_gu, w_down):
    T = x.shape[0]
    N = T // CHUNK
    shift1, scale1, gate1, shift2, scale2, gate2 = [mod[:, i * D:(i + 1) * D] for i in range(6)]

    h = _norm_mod_fwd(x, n1w, scale1, shift1)
    proj = _matmul(h, w_in_p, name="in_proj")
    qkv_c = _conv_fwd(proj, conv_w)
    qkv_hm = qkv_c.reshape(T, 3 * GH, HD).transpose(1, 0, 2)
    zs_hm = proj[:, 3 * GW:GAB0].reshape(T, 20, HD).transpose(1, 0, 2)
    gab = proj[:, GAB0:GAB0 + 2 * GH].T.reshape(2 * GH, N, 1, CHUNK)
    alog_b = jnp.broadcast_to(alog.reshape(GH, 1, 1), (GH, 1, CHUNK))
    dtb_b = jnp.broadcast_to(dtb.reshape(GH, 1, 1), (GH, 1, CHUNK))
    sinks_col = jnp.broadcast_to(sinks.reshape(SQH, 1, 1), (SQH, WIN, 1))
    o_g, S_all = _gdn_fwd(qkv_hm, zs_hm, gab, alog_b, dtb_b, gnw)
    o_s = _swa_fwd(zs_hm, qnw, knw, sinks_col)
    mixcat = jnp.concatenate([o_g, o_s], axis=0).transpose(1, 0, 2).reshape(T, D).astype(BF16)
    mixed = _matmul(mixcat, w_out, name="out_proj")
    x1, h2 = _resid_norm_fwd(x, mixed, gate1, n2w, scale2, shift2)
    ab = _matmul(h2, w_gu, name="ffn_up")
    act = _ffn_act_fwd(ab)
    ffn = _matmul(act, w_down, name="ffn_down")
    dy, dffn, dgate2, loss = _loss_head(x1, ffn, target, gate2)

    dact = _matmul(dffn, w_down, tb=True, name="ffn_down_dx")
    dab = _ffn_act_bwd(ab, dact)
    g_w_down = _matmul(act, dffn, ta=True, out_dtype=BF16, name="ffn_down_dw")
    g_w_gu = _matmul(h2, dab, ta=True, out_dtype=BF16, name="ffn_up_dw")
    dh2 = _matmul(dab, w_gu, tb=True, name="ffn_up_dx")
    dx1, dmixed, dgate1, dn2w, dscale2, dshift2 = _resid_norm_bwd(x, mixed, dy, dh2, gate1, n2w, scale2, shift2)
    g_w_out = _matmul(mixcat, dmixed, ta=True, out_dtype=BF16, name="out_proj_dw")
    dmix_hm = _matmul(dmixed, w_out, tb=True, name="out_proj_dx").reshape(T, 2 * GH, HD).transpose(1, 0, 2)
    dq, dk, dv, dz, dga, dgb, dalog, ddtb, dgnw = _gdn_bwd(qkv_hm, zs_hm, gab, alog_b, dtb_b, gnw, S_all,
                                                           dmix_hm[:GH])
    dqkv_hm = jnp.concatenate([dq, dk, dv], axis=0)
    dsq, dsk, dsv, dqnw, dknw, dsinks = _swa_bwd(zs_hm, qnw, knw, sinks_col, dmix_hm[GH:])
    dqkv_pre, dconv = _conv_bwd(proj, conv_w, dqkv_hm.transpose(1, 0, 2).reshape(T, 3 * GW))
    dzs = jnp.concatenate([dz, dsq, dsk, dsv], axis=0).transpose(1, 0, 2).reshape(T, 20 * HD).astype(BF16)
    dgab = jnp.concatenate([dga, dgb], axis=0).reshape(2 * GH, T).T.astype(BF16)
    dproj = jnp.concatenate([dqkv_pre, dzs, dgab, jnp.zeros((T, NP - PROJ), BF16)], axis=1)
    g_w_in_p = _matmul(h, dproj, ta=True, out_dtype=BF16, name="in_proj_dw")
    dh = _matmul(dproj, w_in_p, tb=True, name="in_proj_dx")
    grad_x, dn1w, dscale1, dshift1 = _norm_mod_bwd(x, dh, dx1, n1w, scale1, shift1)

    dmod = jnp.concatenate([dshift1, dscale1, dgate1, dshift2, dscale2, dgate2], axis=1)
    big = dict(w_in_p=g_w_in_p, w_out=g_w_out, w_gu=g_w_gu, w_down=g_w_down)
    small = dict(mod=dmod, norm1_w=dn1w, norm2_w=dn2w, conv_w=dconv, a_log=dalog[:, 0, 0], dt_bias=ddtb[:, 0, 0],
                 gdn_norm_w=dgnw, q_norm_w=dqnw, k_norm_w=dknw, sinks=dsinks[:, 0, 0])
    return loss, grad_x, big, small


def _adamw(w, g, m, v):
    m2 = ADAM_B1 * m + (1.0 - ADAM_B1) * g
    v2 = ADAM_B2 * v + (1.0 - ADAM_B2) * (g * g)
    m_hat = m2 / (1.0 - ADAM_B1 ** ADAM_STEP)
    v_hat = v2 / (1.0 - ADAM_B2 ** ADAM_STEP)
    delta = -ADAM_LR * (m_hat / (jnp.sqrt(v_hat) + ADAM_EPS) + ADAM_WD * w)
    return delta, m2, v2


def _reduce_adamw(recv, w, m, v, name):
    _, R, C = recv.shape
    tc = _tile(C, 256)

    def body(r_ref, w_ref, m_ref, v_ref, o_ref):
        g = r_ref[0].astype(F32)
        for s in range(1, N_DEV):
            g = g + r_ref[s].astype(F32)
        delta, m2, v2 = _adamw(w_ref[...], g, m_ref[...], v_ref[...])
        o_ref[0] = g
        o_ref[1] = delta
        o_ref[2] = m2
        o_ref[3] = v2

    col = pl.BlockSpec((R, tc), lambda j: (0, j))
    return pl.pallas_call(
        body, name=name, grid=(C // tc,),
        in_specs=[pl.BlockSpec((N_DEV, R, tc), lambda j: (0, 0, j)), col, col, col],
        out_specs=pl.BlockSpec((4, R, tc), lambda j: (0, 0, j)),
        out_shape=_sds((4, R, C)),
        compiler_params=_cparams(("parallel",)),
    )(recv, w, m, v)


def _adamw_call(g, w, m, v, name):
    def body(g_ref, w_ref, m_ref, v_ref, o_ref):
        delta, m2, v2 = _adamw(w_ref[...], g_ref[...], m_ref[...], v_ref[...])
        o_ref[0] = delta
        o_ref[1] = m2
        o_ref[2] = v2

    return pl.pallas_call(body, name=name, out_shape=_sds((3,) + g.shape))(g, w, m, v)


ADA_N = 6 * D // N_CHIP
KPAD = 128


def _mod_part(c8, w_ada, b_ada):
    tn = 512

    def body(c_ref, w_ref, b_ref, o_ref):
        o_ref[...] = _raw1(_silu(c_ref[...]), w_ref[...], _NN) + b_ref[...]

    return pl.pallas_call(
        body, name="ada_mod", grid=(ADA_N // tn,),
        in_specs=[pl.BlockSpec((16, D), lambda j: (0, 0)), pl.BlockSpec((D, tn), lambda j: (0, j)),
                  pl.BlockSpec((1, tn), lambda j: (0, j))],
        out_specs=pl.BlockSpec((16, tn), lambda j: (0, j)),
        out_shape=_sds((16, ADA_N)),
        compiler_params=_cparams(("parallel",)),
    )(c8, w_ada, b_ada)


def _w_ada_update(c8p, dm, w, m, v):
    tr = 256

    def body(c_ref, dm_ref, w_ref, m_ref, v_ref, o_ref):
        g = _raw1(_silu(c_ref[...]), dm_ref[...], _TN)
        delta, m2, v2 = _adamw(w_ref[...], g, m_ref[...], v_ref[...])
        o_ref[0] = g
        o_ref[1] = delta
        o_ref[2] = m2
        o_ref[3] = v2

    blk = pl.BlockSpec((tr, ADA_N), lambda i: (i, 0))
    return pl.pallas_call(
        body, name="w_ada_update", grid=(D // tr,),
        in_specs=[pl.BlockSpec((KPAD, tr), lambda i: (0, i)), pl.BlockSpec((KPAD, ADA_N), lambda i: (0, 0)),
                  blk, blk, blk],
        out_specs=pl.BlockSpec((4, tr, ADA_N), lambda i: (0, i, 0)),
        out_shape=_sds((4, D, ADA_N)),
        compiler_params=_cparams(("parallel",)),
    )(c8p, dm, w, m, v)


def _me():
    return lax.axis_index("x"), lax.axis_index("y"), lax.axis_index("c")


def _peer(k, me):
    mx, my, mc = me
    return (1 - mx if k & 4 else mx, 1 - my if k & 2 else my, 1 - mc if k & 1 else mc)


def _lin(p):
    return 4 * p[0] + 2 * p[1] + p[2]


def _remote(src, dst, ssem, rsem, dev):
    return pltpu.make_async_remote_copy(src_ref=src, dst_ref=dst, send_sem=ssem, recv_sem=rsem,
                                        device_id=dev, device_id_type=MESH)


def _all_gather8(x, name):
    def body(x_ref, out_ref, send_sems, recv_sems):
        me = _me()
        out_ref[_lin(me)] = x_ref[...]
        sends = []
        for k in range(1, N_DEV):
            cp = _remote(x_ref, out_ref.at[_lin(me)], send_sems.at[k - 1], recv_sems.at[k - 1], _peer(k, me))
            cp.start()
            sends.append(cp)
        for k in range(1, N_DEV):
            p = _peer(k, me)
            _remote(x_ref, out_ref.at[_lin(p)], send_sems.at[k - 1], recv_sems.at[k - 1], p).wait_recv()
        for cp in sends:
            cp.wait_send()

    return pl.pallas_call(
        body, name=name,
        out_shape=_sds((N_DEV,) + x.shape, x.dtype),
        in_specs=[pl.BlockSpec(memory_space=pltpu.VMEM)],
        out_specs=pl.BlockSpec(memory_space=pltpu.VMEM),
        scratch_shapes=[pltpu.SemaphoreType.DMA((N_DEV - 1,)), pltpu.SemaphoreType.DMA((N_DEV - 1,))],
    )(x)


def _gather_weights(wpk):
    def body(w_ref, out_ref, send_sems, recv_sems, local_sem):
        mx, my, mc = _me()
        mine = pltpu.make_async_copy(w_ref, out_ref.at[2 * mx + my], local_sem)
        mine.start()
        chips = [(1 - mx, my), (mx, 1 - my), (1 - mx, 1 - my)]
        sends = []
        for k, (px, py) in enumerate(chips):
            cp = _remote(w_ref, out_ref.at[2 * mx + my], send_sems.at[k], recv_sems.at[k], (px, py, mc))
            cp.start()
            sends.append(cp)
        for k, (px, py) in enumerate(chips):
            _remote(w_ref, out_ref.at[2 * px + py], send_sems.at[k], recv_sems.at[k], (px, py, mc)).wait_recv()
        for cp in sends:
            cp.wait_send()
        mine.wait()

    return pl.pallas_call(
        body, name="gather_weights",
        out_shape=_sds((N_CHIP,) + wpk.shape, wpk.dtype),
        in_specs=[pl.BlockSpec(memory_space=pl.ANY)],
        out_specs=pl.BlockSpec(memory_space=pl.ANY),
        scratch_shapes=[pltpu.SemaphoreType.DMA((N_CHIP - 1,)), pltpu.SemaphoreType.DMA((N_CHIP - 1,)),
                        pltpu.SemaphoreType.DMA],
    )(wpk)


def _grad_exchange(gpk):
    def body(g_ref, out_ref, send_sems, recv_sems, local_sem):
        me = _me()
        mx, my, mc = me
        mine = pltpu.make_async_copy(g_ref.at[2 * mx + my, mc], out_ref.at[_lin(me)], local_sem)
        mine.start()
        sends = []
        for k in range(1, N_DEV):
            p = _peer(k, me)
            cp = _remote(g_ref.at[2 * p[0] + p[1], p[2]], out_ref.at[_lin(me)], send_sems.at[k - 1],
                         recv_sems.at[k - 1], p)
            cp.start()
            sends.append(cp)
        for k in range(1, N_DEV):
            p = _peer(k, me)
            _remote(g_ref.at[2 * mx + my, mc], out_ref.at[_lin(p)], send_sems.at[k - 1], recv_sems.at[k - 1],
                    p).wait_recv()
        for cp in sends:
            cp.wait_send()
        mine.wait()

    return pl.pallas_call(
        body, name="grad_exchange",
        out_shape=_sds((N_DEV, PACK_H, D), gpk.dtype),
        in_specs=[pl.BlockSpec(memory_space=pl.ANY)],
        out_specs=pl.BlockSpec(memory_space=pl.ANY),
        scratch_shapes=[pltpu.SemaphoreType.DMA((N_DEV - 1,)), pltpu.SemaphoreType.DMA((N_DEV - 1,)),
                        pltpu.SemaphoreType.DMA],
    )(gpk)


def _sibling_exchange(halves):
    n = halves.shape[0]

    def body(h_ref, out_ref, send_sems, recv_sems, local_sems):
        mx, my, mc = _me()
        sib = (mx, my, 1 - mc)
        local, sends = [], []
        for a in range(n):
            lc = pltpu.make_async_copy(h_ref.at[a], out_ref.at[a, mc], local_sems.at[a])
            lc.start()
            local.append(lc)
            cp = _remote(h_ref.at[a], out_ref.at[a, mc], send_sems.at[a], recv_sems.at[a], sib)
            cp.start()
            sends.append(cp)
        for a in range(n):
            _remote(h_ref.at[a], out_ref.at[a, 1 - mc], send_sems.at[a], recv_sems.at[a], sib).wait_recv()
        for cp in sends:
            cp.wait_send()
        for lc in local:
            lc.wait()

    return pl.pallas_call(
        body, name="sibling_exchange",
        out_shape=_sds((n, 2) + halves.shape[1:], halves.dtype),
        in_specs=[pl.BlockSpec(memory_space=pl.ANY)],
        out_specs=pl.BlockSpec(memory_space=pl.ANY),
        scratch_shapes=[pltpu.SemaphoreType.DMA((n,)), pltpu.SemaphoreType.DMA((n,)), pltpu.SemaphoreType.DMA((n,))],
    )(halves)


def _pack_big(w_in, w_out, w_gate, w_up, w_down, dtype):
    parts = [t.reshape(-1, D).astype(dtype) for t in (w_in, w_out, w_gate, w_up, w_down)]
    parts.append(jnp.zeros((PACK_P - sum(PACK_ROWS), D), dtype))
    return jnp.concatenate(parts, axis=0)


def _unpack_big(pk):
    shapes = ((1, D, PROJ // N_CHIP), (1, D // N_CHIP, D), (1, D, DFF // N_CHIP), (1, D, DFF // N_CHIP),
              (1, DFF // N_CHIP, D))
    out, r = [], 0
    for rows, shp in zip(PACK_ROWS, shapes):
        out.append(pk[r:r + rows].reshape(shp))
        r += rows
    return out


SMALL_ORDER = (("mod", 6 * D), ("norm1_w", D), ("norm2_w", D), ("conv_w", CONVW * 3 * GW), ("a_log", GH),
               ("dt_bias", GH), ("gdn_norm_w", HD), ("q_norm_w", HD), ("k_norm_w", HD), ("sinks", SQH), ("loss", 1))
SMALL_R = 120


def _pack_small(d):
    parts = [d[k].reshape(-1).astype(F32) if k in d else jnp.zeros((n,), F32) for k, n in SMALL_ORDER]
    used = sum(n for _, n in SMALL_ORDER)
    parts.append(jnp.zeros((SMALL_R * LANE - used,), F32))
    return jnp.concatenate(parts).reshape(SMALL_R, LANE)


def _unpack_small(pk):
    flat = pk.reshape(-1)
    out, r = {}, 0
    for k, n in SMALL_ORDER:
        out[k] = flat[r:r + n]
        r += n
    return out


def kernel(x, c, w_ada, b_ada, norm1_w, w_in, conv_w, a_log, dt_bias, gdn_norm_w, q_norm_w, k_norm_w, sinks, w_out, norm2_w, w_gate, w_up, w_down, loss_target, m_w_ada, m_b_ada, m_norm1_w, m_w_in, m_conv_w, m_a_log, m_dt_bias, m_gdn_norm_w, m_q_norm_w, m_k_norm_w, m_sinks, m_w_out, m_norm2_w, m_w_gate, m_w_up, m_w_down, v_w_ada, v_b_ada, v_norm1_w, v_w_in, v_conv_w, v_a_log, v_dt_bias, v_gdn_norm_w, v_q_norm_w, v_k_norm_w, v_sinks, v_w_out, v_norm2_w, v_w_gate, v_w_up, v_w_down):
    mx, my, mc = _me()
    chip = 2 * mx + my
    dev = 4 * mx + 2 * my + mc
    T = x.shape[1]

    conv_sh = conv_w.reshape(CONVW, 3 * GW // N_CHIP)
    mine = jnp.concatenate([c.reshape(-1), conv_sh.reshape(-1), jnp.zeros((4 * LANE,), F32)]).reshape(24, LANE)
    got = _all_gather8(mine, "gather_c_conv")
    c8 = got[:, :8].reshape(N_DEV, D)
    conv_full = jnp.concatenate([got[2 * j, 8:20].reshape(CONVW, 3 * GW // N_CHIP) for j in range(N_CHIP)], axis=1)
    c16 = jnp.concatenate([c8, jnp.zeros((8, D), F32)], axis=0)
    b_sh = lax.dynamic_slice(b_ada, (0, chip * ADA_N), (1, ADA_N))
    mods = _all_gather8(_mod_part(c16, w_ada[0], b_sh), "gather_mod")
    mod = jnp.concatenate([lax.dynamic_slice(mods[2 * j], (dev, 0), (1, ADA_N)) for j in range(N_CHIP)], axis=1)

    wall = _gather_weights(_pack_big(w_in[0], w_out[0], w_gate[0], w_up[0], w_down[0], BF16))
    r0 = [0]
    for rows in PACK_ROWS:
        r0.append(r0[-1] + rows)
    piece = lambda i, shp: [wall[j, r0[i]:r0[i + 1]].reshape(shp) for j in range(N_CHIP)]
    w_in_f = jnp.concatenate(piece(0, (D, PROJ // N_CHIP)), axis=1)
    w_out_f = jnp.concatenate(piece(1, (D // N_CHIP, D)), axis=0)
    w_gu_f = jnp.concatenate(piece(2, (D, DFF // N_CHIP)) + piece(3, (D, DFF // N_CHIP)), axis=1)
    w_down_f = jnp.concatenate(piece(4, (DFF // N_CHIP, D)), axis=0)

    loss, grad_x, big, small = _local_step(
        x[0], loss_target[0], mod, norm1_w, _permute_w_in(w_in_f), conv_full, a_log, dt_bias, gdn_norm_w,
        q_norm_w, k_norm_w, sinks, w_out_f, norm2_w, w_gu_f, w_down_f)

    small["loss"] = loss[:, :1]
    sg = _all_gather8(_pack_small(small), "gather_small_grads")
    rep = dict(mod=(b_ada, m_b_ada, v_b_ada), norm1_w=(norm1_w, m_norm1_w, v_norm1_w),
               norm2_w=(norm2_w, m_norm2_w, v_norm2_w), a_log=(a_log, m_a_log, v_a_log),
               dt_bias=(dt_bias, m_dt_bias, v_dt_bias), gdn_norm_w=(gdn_norm_w, m_gdn_norm_w, v_gdn_norm_w),
               q_norm_w=(q_norm_w, m_q_norm_w, v_q_norm_w), k_norm_w=(k_norm_w, m_k_norm_w, v_k_norm_w),
               sinks=(sinks, m_sinks, v_sinks))
    wmv = [_pack_small({k: t[i] for k, t in rep.items()}) for i in range(3)]
    sres = _reduce_adamw(sg, wmv[0], wmv[1], wmv[2], "small_reduce_adamw")
    s_g, s_d, s_m, s_v = [_unpack_small(sres[i]) for i in range(4)]
    loss_out = s_g["loss"][0]

    g_conv = lax.dynamic_slice(s_g["conv_w"].reshape(CONVW, 3 * GW), (0, chip * (3 * GW // N_CHIP)),
                               (CONVW, 3 * GW // N_CHIP))
    pad16 = lambda t: jnp.concatenate([t.reshape(12, LANE), jnp.zeros((4, LANE), F32)], axis=0)
    cres = _adamw_call(pad16(g_conv), pad16(conv_w), pad16(m_conv_w), pad16(v_conv_w), "conv_adamw")
    conv_out = [g_conv.reshape(conv_w.shape)] + [cres[i, :12].reshape(conv_w.shape) for i in range(3)]

    dmod8 = sg[:, :6 * D // LANE].reshape(N_DEV, 6 * D)
    dm = lax.dynamic_slice(dmod8, (0, chip * ADA_N), (N_DEV, ADA_N))
    zpad = lambda t: jnp.concatenate([t, jnp.zeros((KPAD - N_DEV, t.shape[1]), F32)], axis=0)
    ares = _w_ada_update(zpad(c8), zpad(dm), w_ada[0], m_w_ada[0], v_w_ada[0])

    g_in = _unpermute_w_in(big["w_in_p"])
    g_gate, g_up = big["w_gu"][:, :DFF], big["w_gu"][:, DFF:]
    csl = lambda g, j, n: g[:, j * n:(j + 1) * n]
    rsl = lambda g, j, n: g[j * n:(j + 1) * n]
    gpk = jnp.stack([_pack_big(csl(g_in, j, PROJ // N_CHIP), rsl(big["w_out"], j, D // N_CHIP),
                               csl(g_gate, j, DFF // N_CHIP), csl(g_up, j, DFF // N_CHIP),
                               rsl(big["w_down"], j, DFF // N_CHIP), BF16) for j in range(N_CHIP)])
    recv = _grad_exchange(gpk.reshape(N_CHIP, 2, PACK_H, D))
    half = lambda a, b, c_, d_, e_: lax.dynamic_slice(_pack_big(a[0], b[0], c_[0], d_[0], e_[0], F32),
                                                      (mc * PACK_H, 0), (PACK_H, D))
    hres = _reduce_adamw(recv, half(w_in, w_out, w_gate, w_up, w_down),
                         half(m_w_in, m_w_out, m_w_gate, m_w_up, m_w_down),
                         half(v_w_in, v_w_out, v_w_gate, v_w_up, v_w_down), "big_reduce_adamw")
    full = _sibling_exchange(hres).reshape(4, PACK_P, D)
    bg, bd, bm, bv = [_unpack_big(full[i]) for i in range(4)]

    def group(a_i, small_d, conv_i, big_l):
        s = lambda k, ref: small_d[k].reshape(ref.shape)
        return [ares[a_i][None], s("mod", b_ada), s("norm1_w", norm1_w), big_l[0], conv_out[conv_i],
                s("a_log", a_log), s("dt_bias", dt_bias), s("gdn_norm_w", gdn_norm_w), s("q_norm_w", q_norm_w),
                s("k_norm_w", k_norm_w), s("sinks", sinks), big_l[1], s("norm2_w", norm2_w), big_l[2], big_l[3],
                big_l[4]]

    outs = [loss_out, grad_x[None]]
    outs += group(0, s_g, 0, bg) + group(1, s_d, 1, bd) + group(2, s_m, 2, bm) + group(3, s_v, 3, bv)
    return tuple(outs)
```

```python
import functools

import jax
import jax.numpy as jnp
from jax import lax
from jax.experimental import pallas as pl
from jax.experimental.pallas import tpu as pltpu

F32 = jnp.float32
BF16 = jnp.bfloat16
MESH = pl.DeviceIdType.MESH

D = 1024
HD = 64
GH = 8
GW = GH * HD
SQH = 8
SKVH = 2
SGRP = SQH // SKVH
WIN = 128
CONVW = 4
CHUNK = 64
DFF = 2816
PROJ = 2832
NP = 3072
EPS = 1e-6
N_DEV = 8
N_CHIP = 4

ADAM_LR = 0.001
ADAM_B1 = 0.9
ADAM_B2 = 0.999
ADAM_EPS = 1e-08
ADAM_WD = 0.01
ADAM_STEP = 10

VMEM_LIMIT = 48 * 1024 * 1024
LANE = 128

PACK_ROWS = (PROJ // N_CHIP, D // N_CHIP, DFF // N_CHIP, DFF // N_CHIP, DFF // N_CHIP)
PACK_P = 3104
PACK_H = PACK_P // 2


def _cparams(sem=None):
    return pltpu.CompilerParams(dimension_semantics=sem, vmem_limit_bytes=VMEM_LIMIT)


_NN = ((1,), (0,))
_NT = ((1,), (1,))
_TN = ((0,), (0,))


def _dot(a, b, dims):
    if a.ndim == 3:
        (ca,), (cb,) = dims
        return lax.dot_general(a, b, (((ca + 1,), (cb + 1,)), ((0,), (0,))), preferred_element_type=F32)
    return lax.dot_general(a, b, (dims, ((), ())), preferred_element_type=F32)


def _raw1(a, b, dims):
    return _dot(a.astype(BF16), b.astype(BF16), dims)


def _raw3(a, b, dims):
    ah = a.astype(BF16)
    al = (a - ah.astype(F32)).astype(BF16)
    bh = b.astype(BF16)
    bl = (b - bh.astype(F32)).astype(BF16)
    return _dot(ah, bh, dims) + (_dot(al, bh, dims) + _dot(ah, bl, dims))


def _make_diff_mm(raw):
    @jax.custom_vjp
    def nn(a, b):
        return raw(a, b, _NN)

    @jax.custom_vjp
    def nt(a, b):
        return raw(a, b, _NT)

    @jax.custom_vjp
    def tn(a, b):
        return raw(a, b, _TN)

    nn.defvjp(lambda a, b: (raw(a, b, _NN), (a, b)), lambda r, g: (nt(g, r[1]), tn(r[0], g)))
    nt.defvjp(lambda a, b: (raw(a, b, _NT), (a, b)), lambda r, g: (nn(g, r[1]), tn(g, r[0])))
    tn.defvjp(lambda a, b: (raw(a, b, _TN), (a, b)), lambda r, g: (nt(r[1], g), nn(r[0], g)))
    return nn, nt, tn


def _tri_inv_raw(a, nn3):
    n = a.shape[-1]
    ri = lax.broadcasted_iota(jnp.int32, (n, n), 0)
    ci = lax.broadcasted_iota(jnp.int32, (n, n), 1)
    t = (ri == ci).astype(F32)
    for lvl in range((n - 1).bit_length()):
        same_pair = (ri >> (lvl + 1)) == (ci >> (lvl + 1))
        lower_left = (((ri >> lvl) & 1) == 1) & (((ci >> lvl) & 1) == 0)
        y = jnp.where(same_pair & lower_left, a, 0.0)
        t = t - y if lvl == 0 else t - nn3(nn3(t, y), t)
    return t


class _Kit:
    def __init__(self, diff):
        if diff:
            self.nn, self.nt, self.tn = _make_diff_mm(_raw1)
            self.nn3, self.nt3, self.tn3 = _make_diff_mm(_raw3)
            nn3, nt3, tn3 = self.nn3, self.nt3, self.tn3

            @jax.custom_vjp
            def inv(a):
                return _tri_inv_raw(a, nn3)

            def inv_fwd(a):
                t = _tri_inv_raw(a, nn3)
                return t, t

            def inv_bwd(t, g):
                return (-tn3(t, nt3(g, t)),)

            inv.defvjp(inv_fwd, inv_bwd)
            self.inv = inv
        else:
            self.nn = lambda a, b: _raw1(a, b, _NN)
            self.nt = lambda a, b: _raw1(a, b, _NT)
            self.tn = lambda a, b: _raw1(a, b, _TN)
            self.nn3 = lambda a, b: _raw3(a, b, _NN)
            self.nt3 = lambda a, b: _raw3(a, b, _NT)
            self.tn3 = lambda a, b: _raw3(a, b, _TN)
            self.inv = lambda a: _tri_inv_raw(a, self.nn3)


def _sigmoid(x):
    return 1.0 / (1.0 + jnp.exp(-x))


def _silu(x):
    return x * _sigmoid(x)


def _rms(x, w):
    return x * lax.rsqrt(jnp.mean(x * x, axis=-1, keepdims=True) + EPS) * w


def _tile(dim, target):
    t = (min(dim, target) // LANE) * LANE
    while t >= LANE:
        if dim % t == 0:
            return t
        t -= LANE
    return dim


def _matmul(a, b, ta=False, tb=False, out_dtype=F32, name="matmul"):
    if ta:
        K, M = a.shape
    else:
        M, K = a.shape
    if tb:
        N, K2 = b.shape
    else:
        K2, N = b.shape
    assert K == K2, (a.shape, b.shape, ta, tb)
    tm, tn, tk = _tile(M, 1024), _tile(N, 1536), _tile(K, 1024)
    nk = K // tk
    dims = ((0,) if ta else (1,), (1,) if tb else (0,))

    def body(a_ref, b_ref, o_ref, acc_ref):
        k = pl.program_id(2)

        @pl.when(k == 0)
        def _():
            acc_ref[...] = jnp.zeros_like(acc_ref)

        acc_ref[...] += _dot(a_ref[...].astype(BF16), b_ref[...].astype(BF16), dims)

        @pl.when(k == nk - 1)
        def _():
            o_ref[...] = acc_ref[...].astype(o_ref.dtype)

    a_spec = (pl.BlockSpec((tk, tm), lambda i, j, k: (k, i)) if ta
              else pl.BlockSpec((tm, tk), lambda i, j, k: (i, k)))
    b_spec = (pl.BlockSpec((tn, tk), lambda i, j, k: (j, k)) if tb
              else pl.BlockSpec((tk, tn), lambda i, j, k: (k, j)))
    return pl.pallas_call(
        body, name=name,
        grid=(M // tm, N // tn, nk),
        in_specs=[a_spec, b_spec],
        out_specs=pl.BlockSpec((tm, tn), lambda i, j, k: (i, j)),
        out_shape=jax.ShapeDtypeStruct((M, N), out_dtype),
        scratch_shapes=[pltpu.VMEM((tm, tn), F32)],
        compiler_params=_cparams(("parallel", "parallel", "arbitrary")),
    )(a, b)


def _rowcall(fn, tiled, consts, out_tiled, out_acc, tm, name):
    T = tiled[0].shape[0]
    n_in = len(tiled) + len(consts)
    n_o = len(out_tiled)

    def body(*refs):
        vals = [r[...] for r in refs[:n_in]]
        outs = refs[n_in:]
        res = fn(*vals)
        for r, v in zip(outs[:n_o], res[:n_o]):
            r[...] = v.astype(r.dtype)
        if len(outs) > n_o:
            @pl.when(pl.program_id(0) == 0)
            def _():
                for r in outs[n_o:]:
                    r[...] = jnp.zeros_like(r)

            for r, v in zip(outs[n_o:], res[n_o:]):
                r[...] += v

    in_specs = [pl.BlockSpec((tm, a.shape[1]), lambda i: (i, 0)) for a in tiled]
    in_specs += [pl.BlockSpec(a.shape, lambda i, nd=a.ndim: (0,) * nd) for a in consts]
    out_specs = [pl.BlockSpec((tm, s.shape[1]), lambda i: (i, 0)) for s in out_tiled]
    out_specs += [pl.BlockSpec(s.shape, lambda i: (0, 0)) for s in out_acc]
    return pl.pallas_call(
        body, name=name, grid=(T // tm,),
        in_specs=in_specs, out_specs=out_specs,
        out_shape=list(out_tiled) + list(out_acc),
        compiler_params=_cparams(("arbitrary",)),
    )(*tiled, *consts)


def _sds(shape, dtype=F32):
    return jax.ShapeDtypeStruct(shape, dtype)


def _norm_mod(x, nw, scale, shift):
    return _rms(x, nw) * (1.0 + scale) + shift


def _norm_mod_fwd(x, nw, scale, shift):
    T = x.shape[0]
    (h,) = _rowcall(lambda *a: (_norm_mod(*a),), [x], [nw, scale, shift],
                    [_sds((T, D), BF16)], [], 512, "norm1_fwd")
    return h


def _norm_mod_bwd(x, dh, dres, nw, scale, shift):
    T = x.shape[0]

    def fn(x, dh, dres, nw, scale, shift):
        _, vjp = jax.vjp(_norm_mod, x, nw, scale, shift)
        dx, dnw, dsc, dsh = vjp(dh)
        return dx + dres, dnw, dsc, dsh

    return _rowcall(fn, [x, dh, dres], [nw, scale, shift], [_sds((T, D))],
                    [_sds((1, D))] * 3, 256, "norm1_bwd")


def _resid_norm(x, mixed, gate1, nw, scale, shift):
    x1 = x + gate1 * mixed
    return x1, _norm_mod(x1, nw, scale, shift)


def _resid_norm_fwd(x, mixed, gate1, nw, scale, shift):
    T = x.shape[0]
    return _rowcall(_resid_norm, [x, mixed], [gate1, nw, scale, shift],
                    [_sds((T, D)), _sds((T, D), BF16)], [], 512, "resid_norm2_fwd")


def _resid_norm_bwd(x, mixed, dy, dh2, gate1, nw, scale, shift):
    T = x.shape[0]

    def fn(x, mixed, dy, dh2, gate1, nw, scale, shift):
        _, vjp = jax.vjp(_resid_norm, x, mixed, gate1, nw, scale, shift)
        dx, dmixed, dg1, dnw, dsc, dsh = vjp((dy, dh2))
        return dx, dmixed, dg1, dnw, dsc, dsh

    return _rowcall(fn, [x, mixed, dy, dh2], [gate1, nw, scale, shift],
                    [_sds((T, D)), _sds((T, D), BF16)], [_sds((1, D))] * 4, 256, "resid_norm2_bwd")


def _ffn_act_fwd(ab):
    T = ab.shape[0]

    def fn(ab):
        a, b = ab[:, :DFF], ab[:, DFF:]
        return (_silu(a) * b,)

    (act,) = _rowcall(fn, [ab], [], [_sds((T, DFF), BF16)], [], 256, "ffn_act_fwd")
    return act


def _ffn_act_bwd(ab, dact):
    T = ab.shape[0]

    def fn(ab, dact):
        a, b = ab[:, :DFF], ab[:, DFF:]
        s = _sigmoid(a)
        da = dact * b * (s * (1.0 + a * (1.0 - s)))
        db = dact * (a * s)
        return (jnp.concatenate([da, db], axis=1),)

    (dab,) = _rowcall(fn, [ab, dact], [], [_sds((T, 2 * DFF), BF16)], [], 256, "ffn_act_bwd")
    return dab


def _loss_head(x1, ffn, target, gate2):
    T = x1.shape[0]

    def fn(x1, ffn, target, gate2):
        y = x1 + gate2 * ffn
        err = y - target
        loss = 0.5 * jnp.sum(jnp.sum(err * err, axis=1, keepdims=True), axis=0, keepdims=True) / D
        dy = err * (1.0 / D)
        dgate2 = jnp.sum(dy * ffn, axis=0, keepdims=True)
        return dy, gate2 * dy, dgate2, jnp.broadcast_to(loss, (1, LANE))

    return _rowcall(fn, [x1, ffn, target], [gate2], [_sds((T, D)), _sds((T, D), BF16)],
                    [_sds((1, D)), _sds((1, LANE))], 256, "loss_head")


def _round_bf16(x):
    return x.astype(BF16).astype(F32)


def _shift_down(x, s, rows):
    if s == 0:
        return x
    return jnp.where(rows >= s, pltpu.roll(x, s, 0), 0.0)


def _shift_up(x, s, rows, T):
    if s == 0:
        return x
    return jnp.where(rows < T - s, pltpu.roll(x, T - s, 0), 0.0)


def _conv_fwd(proj, conv_w):
    T = proj.shape[0]
    ncol = 3 * GW // LANE

    def body(x_ref, w_ref, o_ref):
        x = _round_bf16(x_ref[...])
        rows = lax.broadcasted_iota(jnp.int32, x.shape, 0)
        acc = jnp.zeros_like(x)
        for j in range(CONVW):
            acc = acc + _round_bf16(w_ref[pl.ds(j, 1), :]) * _shift_down(x, CONVW - 1 - j, rows)
        o_ref[...] = _silu(acc)

    return pl.pallas_call(
        body, name="conv_fwd", grid=(ncol,),
        in_specs=[pl.BlockSpec((T, LANE), lambda j: (0, j)), pl.BlockSpec((CONVW, LANE), lambda j: (0, j))],
        out_specs=pl.BlockSpec((T, LANE), lambda j: (0, j)),
        out_shape=_sds((T, 3 * GW)),
        compiler_params=_cparams(("parallel",)),
    )(proj, conv_w)


def _conv_bwd(proj, conv_w, dqc):
    T = proj.shape[0]
    ncol = 3 * GW // LANE

    def body(x_ref, w_ref, d_ref, dx_ref, dw_ref):
        x = _round_bf16(x_ref[...])
        rows = lax.broadcasted_iota(jnp.int32, x.shape, 0)
        xs = [_shift_down(x, CONVW - 1 - j, rows) for j in range(CONVW)]
        w = [_round_bf16(w_ref[pl.ds(j, 1), :]) for j in range(CONVW)]
        pre = jnp.zeros_like(x)
        for j in range(CONVW):
            pre = pre + w[j] * xs[j]
        s = _sigmoid(pre)
        dpre = _round_bf16(d_ref[...] * (s * (1.0 + pre * (1.0 - s))))
        dx = jnp.zeros_like(x)
        for j in range(CONVW):
            dx = dx + w[j] * _shift_up(dpre, CONVW - 1 - j, rows, T)
            dw_ref[pl.ds(j, 1), :] = jnp.sum(dpre * xs[j], axis=0, keepdims=True)
        dx_ref[...] = dx.astype(dx_ref.dtype)

    return pl.pallas_call(
        body, name="conv_bwd", grid=(ncol,),
        in_specs=[pl.BlockSpec((T, LANE), lambda j: (0, j)), pl.BlockSpec((CONVW, LANE), lambda j: (0, j)),
                  pl.BlockSpec((T, LANE), lambda j: (0, j))],
        out_specs=[pl.BlockSpec((T, LANE), lambda j: (0, j)), pl.BlockSpec((CONVW, LANE), lambda j: (0, j))],
        out_shape=[_sds((T, 3 * GW), BF16), _sds((CONVW, 3 * GW))],
        compiler_params=_cparams(("parallel",)),
    )(proj, conv_w, dqc)


def _gdn_prep(kit, q, k, v, ga, gb, alog, dtb):
    C = CHUNK
    ri = lax.broadcasted_iota(jnp.int32, (C, C), 0)
    ci = lax.broadcasted_iota(jnp.int32, (C, C), 1)
    causal = ri >= ci
    strict = ri > ci
    eye = (ri == ci).astype(F32)
    lower = causal.astype(F32)
    upper = (ri <= ci).astype(F32)

    a = ga + dtb
    softplus = jnp.maximum(a, 0.0) + jnp.log(1.0 + jnp.exp(-jnp.abs(a)))
    g_row = -jnp.exp(alog) * softplus
    beta_row = _sigmoid(gb)
    g_col = jnp.sum(eye * g_row, axis=2, keepdims=True)
    beta_col = jnp.sum(eye * beta_row, axis=2, keepdims=True)
    G_col = jnp.sum(lower * g_row, axis=2, keepdims=True)
    G_row = jnp.sum(upper * g_col, axis=1, keepdims=True)
    G_last = jnp.sum(g_row, axis=2, keepdims=True)
    decay = jnp.exp(jnp.where(causal, G_col - G_row, -1e30))

    qn = q * lax.rsqrt(jnp.sum(q * q, axis=-1, keepdims=True) + EPS) * (HD ** -0.5)
    kn = k * lax.rsqrt(jnp.sum(k * k, axis=-1, keepdims=True) + EPS)
    kb = kn * beta_col
    A = jnp.where(strict, kit.nt(kb, kn) * decay, 0.0)
    Tm = kit.inv(A)
    eG = jnp.exp(G_col)
    u = kit.nn3(Tm, v * beta_col)
    w = kit.nn3(Tm, kb * eG)
    qk = jnp.where(causal, kit.nt(qn, kn) * decay, 0.0)
    q_dec = qn * eG
    k_dec = kn * jnp.exp(G_last - G_col)
    dec = jnp.exp(G_last)
    return u, w, qk, q_dec, k_dec, dec


def _gdn_out(o, z, nw):
    return _rms(o, nw) * _silu(z)


GDN_CB = 4


def _gdn_specs(T, blk):
    TB = GDN_CB * CHUNK
    seq = lambda grp: pl.BlockSpec((GH, TB, HD), lambda i, grp=grp: (grp, blk(i), 0))
    row = lambda grp: pl.BlockSpec((GH, GDN_CB, 1, CHUNK), lambda i, grp=grp: (grp, blk(i), 0, 0))
    per_head = pl.BlockSpec((GH, 1, CHUNK), lambda i: (0, 0, 0))
    whole = pl.BlockSpec((1, HD), lambda i: (0, 0))
    state = pl.BlockSpec((GH, GDN_CB, HD, HD), lambda i: (0, blk(i), 0, 0))
    return seq, row, per_head, whole, state


def _gdn_load(seq_refs, row_refs, head_refs):
    chunks = lambda r: jnp.concatenate([r[:, pl.ds(cb * CHUNK, CHUNK), :] for cb in range(GDN_CB)], axis=0)
    rows = lambda r: jnp.concatenate([r[:, cb] for cb in range(GDN_CB)], axis=0)
    heads = lambda r: jnp.concatenate([r[...]] * GDN_CB, axis=0)
    return [chunks(r) for r in seq_refs], [rows(r) for r in row_refs], [heads(r) for r in head_refs]


def _gdn_fwd(qkv_hm, zs_hm, gab, alog_b, dtb_b, nw):
    T = qkv_hm.shape[1]
    N = T // CHUNK
    seq, row, per_head, whole, state = _gdn_specs(T, lambda i: i)
    kit = _Kit(False)

    def body(q_ref, k_ref, v_ref, z_ref, ga_ref, gb_ref, al_ref, dt_ref, nw_ref, o_ref, S_ref, S_scr):
        @pl.when(pl.program_id(0) == 0)
        def _():
            S_scr[...] = jnp.zeros_like(S_scr)

        (q, k, v, z), (ga, gb), (al, dt) = _gdn_load((q_ref, k_ref, v_ref, z_ref), (ga_ref, gb_ref), (al_ref, dt_ref))
        u, w, qk, q_dec, k_dec, dec = _gdn_prep(kit, q, k, v, ga, gb, al, dt)
        S = S_scr[...]
        for cb in range(GDN_CB):
            hs = slice(cb * GH, (cb + 1) * GH)
            S_ref[:, cb] = S
            v_new = u[hs] - kit.nn(w[hs], S)
            o = kit.nn(q_dec[hs], S) + kit.nn(qk[hs], v_new)
            S = S * dec[hs] + kit.tn(k_dec[hs], v_new)
            o_ref[:, pl.ds(cb * CHUNK, CHUNK), :] = _gdn_out(o, z[hs], nw_ref[...])
        S_scr[...] = S

    return pl.pallas_call(
        body, name="gdn_fwd", grid=(N // GDN_CB,),
        in_specs=[seq(0), seq(1), seq(2), seq(0), row(0), row(1), per_head, per_head, whole],
        out_specs=[seq(0), state],
        out_shape=[_sds((GH, T, HD)), _sds((GH, N, HD, HD))],
        scratch_shapes=[pltpu.VMEM((GH, HD, HD), F32)],
        compiler_params=_cparams(("arbitrary",)),
    )(qkv_hm, qkv_hm, qkv_hm, zs_hm, gab, gab, alog_b, dtb_b, nw)


def _gdn_bwd(qkv_hm, zs_hm, gab, alog_b, dtb_b, nw, S_all, do):
    T = qkv_hm.shape[1]
    N = T // CHUNK
    nblk = N // GDN_CB
    dkit, kit = _Kit(True), _Kit(False)
    rseq, rrow, per_head, whole, rstate = _gdn_specs(T, lambda i: nblk - 1 - i)

    def body(q_ref, k_ref, v_ref, z_ref, ga_ref, gb_ref, al_ref, dt_ref, nw_ref, S_ref, do_ref,
             dq_ref, dk_ref, dv_ref, dz_ref, dga_ref, dgb_ref, dal_ref, ddt_ref, dnw_ref, dS_scr):
        @pl.when(pl.program_id(0) == 0)
        def _():
            dS_scr[...] = jnp.zeros_like(dS_scr)
            dal_ref[...] = jnp.zeros_like(dal_ref)
            ddt_ref[...] = jnp.zeros_like(ddt_ref)
            dnw_ref[...] = jnp.zeros_like(dnw_ref)

        (q, k, v, z, dout), (ga, gb), (al, dt) = _gdn_load((q_ref, k_ref, v_ref, z_ref, do_ref), (ga_ref, gb_ref),
                                                          (al_ref, dt_ref))
        S_in = jnp.concatenate([S_ref[:, cb] for cb in range(GDN_CB)], axis=0)
        (u, w, qk, q_dec, k_dec, dec), prep_vjp = jax.vjp(functools.partial(_gdn_prep, dkit), q, k, v, ga, gb, al, dt)
        v_new = u - kit.nn(w, S_in)
        o = kit.nn(q_dec, S_in) + kit.nn(qk, v_new)
        _, out_vjp = jax.vjp(_gdn_out, o, z, nw_ref[...])
        do, dz, dnw = out_vjp(dout)
        dvn_part = kit.tn(qk, do)
        dS_part = kit.tn(q_dec, do)
        dS = dS_scr[...]
        dS_out, dvn = [None] * GDN_CB, [None] * GDN_CB
        for cb in reversed(range(GDN_CB)):
            hs = slice(cb * GH, (cb + 1) * GH)
            dS_out[cb] = dS
            dvn[cb] = dvn_part[hs] + kit.nn(k_dec[hs], dS)
            dS = dS * dec[hs] + dS_part[hs] - kit.tn(w[hs], dvn[cb])
        dS_scr[...] = dS
        dS_out = jnp.concatenate(dS_out, axis=0)
        dvn = jnp.concatenate(dvn, axis=0)
        ddec = jnp.sum(jnp.sum(S_in * dS_out, axis=2, keepdims=True), axis=1, keepdims=True)
        cts = (dvn, -kit.nt(dvn, S_in), kit.nt(do, v_new), kit.nt(do, S_in), kit.nt(v_new, dS_out), ddec)
        dq, dk, dv, dga, dgb, dal, ddt = prep_vjp(cts)
        lanesum = lambda t: jnp.broadcast_to(jnp.sum(t, axis=2, keepdims=True), t.shape)
        for cb in range(GDN_CB):
            hs = slice(cb * GH, (cb + 1) * GH)
            sl = pl.ds(cb * CHUNK, CHUNK)
            dq_ref[:, sl, :] = dq[hs]
            dk_ref[:, sl, :] = dk[hs]
            dv_ref[:, sl, :] = dv[hs]
            dz_ref[:, sl, :] = dz[hs]
            dga_ref[:, cb] = dga[hs]
            dgb_ref[:, cb] = dgb[hs]
            dal_ref[...] += lanesum(dal[hs])
            ddt_ref[...] += lanesum(ddt[hs])
        dnw_ref[...] += dnw

    return pl.pallas_call(
        body, name="gdn_bwd", grid=(nblk,),
        in_specs=[rseq(0), rseq(1), rseq(2), rseq(0), rrow(0), rrow(1), per_head, per_head, whole, rstate, rseq(0)],
        out_specs=[rseq(0), rseq(0), rseq(0), rseq(0), rrow(0), rrow(0), per_head, per_head, whole],
        out_shape=[_sds((GH, T, HD))] * 4 + [_sds((GH, N, 1, CHUNK))] * 2 + [_sds((GH, 1, CHUNK))] * 2
                  + [_sds((1, HD))],
        scratch_shapes=[pltpu.VMEM((GH, HD, HD), F32)],
        compiler_params=_cparams(("arbitrary",)),
    )(qkv_hm, qkv_hm, qkv_hm, zs_hm, gab, gab, alog_b, dtb_b, nw, S_all, do)


def _swa_block(kit, first, q0, q1, q2, q3, kp, kc, vp, vc, qnw, knw, s0, s1, s2, s3, *, slopes):
    W = WIN
    ri = lax.broadcasted_iota(jnp.int32, (W, W), 0)
    ci = lax.broadcasted_iota(jnp.int32, (W, W), 1)
    mask_c = ri >= ci
    mask_p = ci > ri + first * W
    dist_c = (ri - ci).astype(F32)
    dist_p = (ri - ci + W).astype(F32)
    kpn = _rms(kp, knw)
    kcn = _rms(kc, knw)
    outs = []
    for q, sink, slope in zip((q0, q1, q2, q3), (s0, s1, s2, s3), slopes):
        qn = _rms(q, qnw)
        sc = jnp.where(mask_c, kit.nt(qn, kcn) * (HD ** -0.5) - slope * dist_c, -1e30)
        sp = jnp.where(mask_p, kit.nt(qn, kpn) * (HD ** -0.5) - slope * dist_p, -1e30)
        m = jnp.maximum(jnp.maximum(jnp.max(sc, axis=-1, keepdims=True), jnp.max(sp, axis=-1, keepdims=True)), sink)
        m = lax.stop_gradient(m)
        pc = jnp.exp(sc - m)
        pp = jnp.exp(sp - m)
        den = jnp.sum(pc, axis=-1, keepdims=True) + jnp.sum(pp, axis=-1, keepdims=True) + jnp.exp(sink - m)
        inv = 1.0 / den
        outs.append(kit.nn(pc * inv, vc) + kit.nn(pp * inv, vp))
    return tuple(outs)


def _swa_slopes(hk):
    return tuple(jnp.where(hk == 0, 2.0 ** (-8.0 * (g + 1.0) / SQH), 2.0 ** (-8.0 * (SGRP + g + 1.0) / SQH))
                 for g in range(SGRP))


def _swa_fwd(zs_hm, qnw, knw, sinks_col):
    T = zs_hm.shape[1]
    NB = T // WIN
    kit = _Kit(False)

    def body(q_ref, kp_ref, kc_ref, vp_ref, vc_ref, qnw_ref, knw_ref, s_ref, o_ref):
        hk = pl.program_id(0)
        first = (pl.program_id(1) == 0).astype(jnp.int32)
        args = ([q_ref[g] for g in range(SGRP)] + [kp_ref[...], kc_ref[...], vp_ref[...], vc_ref[...],
                                                     qnw_ref[...], knw_ref[...]] + [s_ref[g] for g in range(SGRP)])
        outs = _swa_block(kit, first, *args, slopes=_swa_slopes(hk))
        for g in range(SGRP):
            o_ref[g] = outs[g]

    qspec = pl.BlockSpec((SGRP, WIN, HD), lambda hk, n: (2 + hk, n, 0))
    cur = lambda off: pl.BlockSpec((None, WIN, HD), lambda hk, n, off=off: (off + hk, n, 0))
    prev = lambda off: pl.BlockSpec((None, WIN, HD), lambda hk, n, off=off: (off + hk, jnp.maximum(n - 1, 0), 0))
    whole = pl.BlockSpec((1, HD), lambda hk, n: (0, 0))
    sspec = pl.BlockSpec((SGRP, WIN, 1), lambda hk, n: (hk, 0, 0))
    return pl.pallas_call(
        body, name="swa_fwd", grid=(SKVH, NB),
        in_specs=[qspec, prev(16), cur(16), prev(18), cur(18), whole, whole, sspec],
        out_specs=pl.BlockSpec((SGRP, WIN, HD), lambda hk, n: (hk, n, 0)),
        out_shape=_sds((SQH, T, HD)),
        compiler_params=_cparams(("parallel", "arbitrary")),
    )(zs_hm, zs_hm, zs_hm, zs_hm, zs_hm, qnw, knw, sinks_col)


def _swa_bwd(zs_hm, qnw, knw, sinks_col, do):
    T = zs_hm.shape[1]
    NB = T // WIN
    kit = _Kit(True)

    def body(q_ref, kp_ref, kc_ref, vp_ref, vc_ref, qnw_ref, knw_ref, s_ref, do_ref,
             dq_ref, dk_ref, dv_ref, dqnw_ref, dknw_ref, ds_ref, ck_scr, cv_scr):
        hk = pl.program_id(0)
        i = pl.program_id(1)
        first = (i == NB - 1).astype(jnp.int32)

        @pl.when(i == 0)
        def _():
            ck_scr[...] = jnp.zeros_like(ck_scr)
            cv_scr[...] = jnp.zeros_like(cv_scr)
            ds_ref[...] = jnp.zeros_like(ds_ref)

        @pl.when((i == 0) & (hk == 0))
        def _():
            dqnw_ref[...] = jnp.zeros_like(dqnw_ref)
            dknw_ref[...] = jnp.zeros_like(dknw_ref)

        args = ([q_ref[g] for g in range(SGRP)] + [kp_ref[...], kc_ref[...], vp_ref[...], vc_ref[...],
                                                     qnw_ref[...], knw_ref[...]] + [s_ref[g] for g in range(SGRP)])
        dos = tuple(do_ref[g] for g in range(SGRP))
        _, vjp = jax.vjp(functools.partial(_swa_block, kit, first, slopes=_swa_slopes(hk)), *args)
        gr = vjp(dos)
        for g in range(SGRP):
            dq_ref[g] = gr[g]
            ds_ref[g] += jnp.broadcast_to(jnp.sum(gr[10 + g], axis=0, keepdims=True), (WIN, 1))
        dkp, dkc, dvp, dvc = gr[4:8]
        dk_ref[...] = dkc + ck_scr[...]
        dv_ref[...] = dvc + cv_scr[...]
        ck_scr[...] = dkp
        cv_scr[...] = dvp
        dqnw_ref[...] += gr[8]
        dknw_ref[...] += gr[9]

    rn = lambda n: NB - 1 - n
    qspec = pl.BlockSpec((SGRP, WIN, HD), lambda hk, i: (2 + hk, rn(i), 0))
    cur = lambda off: pl.BlockSpec((None, WIN, HD), lambda hk, i, off=off: (off + hk, rn(i), 0))
    prev = lambda off: pl.BlockSpec((None, WIN, HD), lambda hk, i, off=off: (off + hk, jnp.maximum(rn(i) - 1, 0), 0))
    whole = pl.BlockSpec((1, HD), lambda hk, i: (0, 0))
    sspec = pl.BlockSpec((SGRP, WIN, 1), lambda hk, i: (hk, 0, 0))
    ospec = pl.BlockSpec((SGRP, WIN, HD), lambda hk, i: (hk, rn(i), 0))
    return pl.pallas_call(
        body, name="swa_bwd", grid=(SKVH, NB),
        in_specs=[qspec, prev(16), cur(16), prev(18), cur(18), whole, whole, sspec, ospec],
        out_specs=[ospec, cur(0), cur(0), whole, whole, sspec],
        out_shape=[_sds((SQH, T, HD)), _sds((SKVH, T, HD)), _sds((SKVH, T, HD)),
                   _sds((1, HD)), _sds((1, HD)), _sds((SQH, WIN, 1))],
        scratch_shapes=[pltpu.VMEM((WIN, HD), F32), pltpu.VMEM((WIN, HD), F32)],
        compiler_params=_cparams(("arbitrary", "arbitrary")),
    )(zs_hm, zs_hm, zs_hm, zs_hm, zs_hm, qnw, knw, sinks_col, do)


def _swa_heads(kit, first, q, kp, kc, vp, vc, qnw, knw, sink, slope):
    W = WIN
    ri = lax.broadcasted_iota(jnp.int32, (W, W), 0)
    ci = lax.broadcasted_iota(jnp.int32, (W, W), 1)
    mask_c = ri >= ci
    mask_p = ci > ri + first * W
    dist_c = (ri - ci).astype(F32)
    dist_p = (ri - ci + W).astype(F32)
    kpn = _rms(kp, knw)
    kcn = _rms(kc, knw)
    qn = _rms(q, qnw)
    sc = jnp.where(mask_c, kit.nt(qn, kcn) * (HD ** -0.5) - slope * dist_c, -1e30)
    sp = jnp.where(mask_p, kit.nt(qn, kpn) * (HD ** -0.5) - slope * dist_p, -1e30)
    m = jnp.maximum(jnp.maximum(jnp.max(sc, axis=-1, keepdims=True), jnp.max(sp, axis=-1, keepdims=True)), sink)
    m = lax.stop_gradient(m)
    pc = jnp.exp(sc - m)
    pp = jnp.exp(sp - m)
    den = jnp.sum(pc, axis=-1, keepdims=True) + jnp.sum(pp, axis=-1, keepdims=True) + jnp.exp(sink - m)
    inv = 1.0 / den
    return kit.nn(pc * inv, vc) + kit.nn(pp * inv, vp)


def _per_query_head(kv_ref):
    return jnp.concatenate([kv_ref[pl.ds(h // SGRP, 1)] for h in range(SQH)], axis=0)


def _per_kv_head(d):
    return jnp.concatenate([jnp.sum(d[g * SGRP:(g + 1) * SGRP], axis=0, keepdims=True) for g in range(SKVH)], axis=0)


def _swa_specs(blk):
    qspec = pl.BlockSpec((SQH, WIN, HD), lambda i: (1, blk(i), 0))
    cur = lambda grp: pl.BlockSpec((SKVH, WIN, HD), lambda i, grp=grp: (grp, blk(i), 0))
    prev = lambda grp: pl.BlockSpec((SKVH, WIN, HD), lambda i, grp=grp: (grp, jnp.maximum(blk(i) - 1, 0), 0))
    whole = pl.BlockSpec((1, HD), lambda i: (0, 0))
    col = pl.BlockSpec((SQH, WIN, 1), lambda i: (0, 0, 0))
    ospec = pl.BlockSpec((SQH, WIN, HD), lambda i: (0, blk(i), 0))
    return qspec, cur, prev, whole, col, ospec


def _swa_fwd(zs_hm, qnw, knw, sinks_col, slopes_col):
    T = zs_hm.shape[1]
    kit = _Kit(False)
    qspec, cur, prev, whole, col, ospec = _swa_specs(lambda i: i)

    def body(q_ref, kp_ref, kc_ref, vp_ref, vc_ref, qnw_ref, knw_ref, s_ref, sl_ref, o_ref):
        first = (pl.program_id(0) == 0).astype(jnp.int32)
        o_ref[...] = _swa_heads(kit, first, q_ref[...], _per_query_head(kp_ref), _per_query_head(kc_ref),
                                _per_query_head(vp_ref), _per_query_head(vc_ref), qnw_ref[...], knw_ref[...],
                                s_ref[...], sl_ref[...])

    return pl.pallas_call(
        body, name="swa_fwd", grid=(T // WIN,),
        in_specs=[qspec, prev(8), cur(8), prev(9), cur(9), whole, whole, col, col],
        out_specs=ospec, out_shape=_sds((SQH, T, HD)),
        compiler_params=_cparams(("arbitrary",)),
    )(zs_hm, zs_hm, zs_hm, zs_hm, zs_hm, qnw, knw, sinks_col, slopes_col)


def _swa_bwd(zs_hm, qnw, knw, sinks_col, slopes_col, do):
    T = zs_hm.shape[1]
    NB = T // WIN
    kit = _Kit(True)
    qspec, cur, prev, whole, col, ospec = _swa_specs(lambda i: NB - 1 - i)

    def body(q_ref, kp_ref, kc_ref, vp_ref, vc_ref, qnw_ref, knw_ref, s_ref, sl_ref, do_ref,
             dq_ref, dk_ref, dv_ref, dqnw_ref, dknw_ref, ds_ref, ck_scr, cv_scr):
        i = pl.program_id(0)
        first = (i == NB - 1).astype(jnp.int32)

        @pl.when(i == 0)
        def _():
            ck_scr[...] = jnp.zeros_like(ck_scr)
            cv_scr[...] = jnp.zeros_like(cv_scr)
            ds_ref[...] = jnp.zeros_like(ds_ref)
            dqnw_ref[...] = jnp.zeros_like(dqnw_ref)
            dknw_ref[...] = jnp.zeros_like(dknw_ref)

        fn = lambda q, kp, kc, vp, vc, qnw, knw, sink: _swa_heads(kit, first, q, kp, kc, vp, vc, qnw, knw, sink,
                                                                  sl_ref[...])
        _, vjp = jax.vjp(fn, q_ref[...], _per_query_head(kp_ref), _per_query_head(kc_ref), _per_query_head(vp_ref),
                         _per_query_head(vc_ref), qnw_ref[...], knw_ref[...], s_ref[...])
        dq, dkp, dkc, dvp, dvc, dqnw, dknw, dsink = vjp(do_ref[...])
        dq_ref[...] = dq
        dk_ref[...] = _per_kv_head(dkc) + ck_scr[...]
        dv_ref[...] = _per_kv_head(dvc) + cv_scr[...]
        ck_scr[...] = _per_kv_head(dkp)
        cv_scr[...] = _per_kv_head(dvp)
        dqnw_ref[...] += dqnw
        dknw_ref[...] += dknw
        ds_ref[...] += jnp.broadcast_to(jnp.sum(dsink, axis=1, keepdims=True), dsink.shape)

    kvspec = pl.BlockSpec((SKVH, WIN, HD), lambda i: (0, NB - 1 - i, 0))
    return pl.pallas_call(
        body, name="swa_bwd", grid=(NB,),
        in_specs=[qspec, prev(8), cur(8), prev(9), cur(9), whole, whole, col, col, ospec],
        out_specs=[ospec, kvspec, kvspec, whole, whole, col],
        out_shape=[_sds((SQH, T, HD)), _sds((SKVH, T, HD)), _sds((SKVH, T, HD)),
                   _sds((1, HD)), _sds((1, HD)), _sds((SQH, WIN, 1))],
        scratch_shapes=[pltpu.VMEM((SKVH, WIN, HD), F32), pltpu.VMEM((SKVH, WIN, HD), F32)],
        compiler_params=_cparams(("arbitrary",)),
    )(zs_hm, zs_hm, zs_hm, zs_hm, zs_hm, qnw, knw, sinks_col, slopes_col, do)


GAB0 = 3 * GW + 1280


def _permute_w_in(w_in):
    return jnp.concatenate([w_in[:, :4 * GW], w_in[:, 4 * GW + 2 * GH:], w_in[:, 4 * GW:4 * GW + 2 * GH],
                            jnp.zeros((D, NP - PROJ), w_in.dtype)], axis=1)


def _unpermute_w_in(g):
    return jnp.concatenate([g[:, :4 * GW], g[:, GAB0:GAB0 + 2 * GH], g[:, 4 * GW:GAB0]], axis=1)


def _local_step(x, target, mod, n1w, w_in_p, conv_w, alog, dtb, gnw, qnw, knw, sinks, w_out, n2w, w_gu, w_down):
    T = x.shape[0]
    N = T // CHUNK
    shift1, scale1, gate1, shift2, scale2, gate2 = [mod[:, i * D:(i + 1) * D] for i in range(6)]

    h = _norm_mod_fwd(x, n1w, scale1, shift1)
    proj = _matmul(h, w_in_p, name="in_proj")
    qkv_c = _conv_fwd(proj, conv_w)
    qkv_hm = qkv_c.reshape(T, 3 * GH, HD).transpose(1, 0, 2)
    zs_hm = proj[:, 3 * GW:GAB0].reshape(T, 20, HD).transpose(1, 0, 2)
    gab = proj[:, GAB0:GAB0 + 2 * GH].T.reshape(2 * GH, N, 1, CHUNK)
    alog_b = jnp.broadcast_to(alog.reshape(GH, 1, 1), (GH, 1, CHUNK))
    dtb_b = jnp.broadcast_to(dtb.reshape(GH, 1, 1), (GH, 1, CHUNK))
    sinks_col = jnp.broadcast_to(sinks.reshape(SQH, 1, 1), (SQH, WIN, 1))
    o_g, S_all = _gdn_fwd(qkv_hm, zs_hm, gab, alog_b, dtb_b, gnw)
    slopes = 2.0 ** (-8.0 * (jnp.arange(SQH, dtype=F32) + 1.0) / SQH)
    slopes_col = jnp.broadcast_to(slopes.reshape(SQH, 1, 1), (SQH, WIN, 1))
    o_s = _swa_fwd(zs_hm, qnw, knw, sinks_col, slopes_col)
    mixcat = jnp.concatenate([o_g, o_s], axis=0).transpose(1, 0, 2).reshape(T, D).astype(BF16)
    mixed = _matmul(mixcat, w_out, name="out_proj")
    x1, h2 = _resid_norm_fwd(x, mixed, gate1, n2w, scale2, shift2)
    ab = _matmul(h2, w_gu, name="ffn_up")
    act = _ffn_act_fwd(ab)
    ffn = _matmul(act, w_down, name="ffn_down")
    dy, dffn, dgate2, loss = _loss_head(x1, ffn, target, gate2)

    dact = _matmul(dffn, w_down, tb=True, name="ffn_down_dx")
    dab = _ffn_act_bwd(ab, dact)
    g_w_down = _matmul(act, dffn, ta=True, out_dtype=BF16, name="ffn_down_dw")
    g_w_gu = _matmul(h2, dab, ta=True, out_dtype=BF16, name="ffn_up_dw")
    dh2 = _matmul(dab, w_gu, tb=True, name="ffn_up_dx")
    dx1, dmixed, dgate1, dn2w, dscale2, dshift2 = _resid_norm_bwd(x, mixed, dy, dh2, gate1, n2w, scale2, shift2)
    g_w_out = _matmul(mixcat, dmixed, ta=True, out_dtype=BF16, name="out_proj_dw")
    dmix_hm = _matmul(dmixed, w_out, tb=True, name="out_proj_dx").reshape(T, 2 * GH, HD).transpose(1, 0, 2)
    dq, dk, dv, dz, dga, dgb, dalog, ddtb, dgnw = _gdn_bwd(qkv_hm, zs_hm, gab, alog_b, dtb_b, gnw, S_all,
                                                           dmix_hm[:GH])
    dqkv_hm = jnp.concatenate([dq, dk, dv], axis=0)
    dsq, dsk, dsv, dqnw, dknw, dsinks = _swa_bwd(zs_hm, qnw, knw, sinks_col, slopes_col, dmix_hm[GH:])
    dqkv_pre, dconv = _conv_bwd(proj, conv_w, dqkv_hm.transpose(1, 0, 2).reshape(T, 3 * GW))
    dzs = jnp.concatenate([dz, dsq, dsk, dsv], axis=0).transpose(1, 0, 2).reshape(T, 20 * HD).astype(BF16)
    dgab = jnp.concatenate([dga, dgb], axis=0).reshape(2 * GH, T).T.astype(BF16)
    dproj = jnp.concatenate([dqkv_pre, dzs, dgab, jnp.zeros((T, NP - PROJ), BF16)], axis=1)
    g_w_in_p = _matmul(h, dproj, ta=True, out_dtype=BF16, name="in_proj_dw")
    dh = _matmul(dproj, w_in_p, tb=True, name="in_proj_dx")
    grad_x, dn1w, dscale1, dshift1 = _norm_mod_bwd(x, dh, dx1, n1w, scale1, shift1)

    dmod = jnp.concatenate([dshift1, dscale1, dgate1, dshift2, dscale2, dgate2], axis=1)
    big = dict(w_in_p=g_w_in_p, w_out=g_w_out, w_gu=g_w_gu, w_down=g_w_down)
    small = dict(mod=dmod, norm1_w=dn1w, norm2_w=dn2w, conv_w=dconv, a_log=dalog[:, 0, 0], dt_bias=ddtb[:, 0, 0],
                 gdn_norm_w=dgnw, q_norm_w=dqnw, k_norm_w=dknw, sinks=dsinks[:, 0, 0])
    return loss, grad_x, big, small


def _adamw(w, g, m, v):
    m2 = ADAM_B1 * m + (1.0 - ADAM_B1) * g
    v2 = ADAM_B2 * v + (1.0 - ADAM_B2) * (g * g)
    m_hat = m2 / (1.0 - ADAM_B1 ** ADAM_STEP)
    v_hat = v2 / (1.0 - ADAM_B2 ** ADAM_STEP)
    delta = -ADAM_LR * (m_hat / (jnp.sqrt(v_hat) + ADAM_EPS) + ADAM_WD * w)
    return delta, m2, v2


def _reduce_adamw(recv, w, m, v, name):
    _, R, C = recv.shape
    tc = _tile(C, 256)

    def body(r_ref, w_ref, m_ref, v_ref, o_ref):
        g = r_ref[0].astype(F32)
        for s in range(1, N_DEV):
            g = g + r_ref[s].astype(F32)
        delta, m2, v2 = _adamw(w_ref[...], g, m_ref[...], v_ref[...])
        o_ref[0] = g
        o_ref[1] = delta
        o_ref[2] = m2
        o_ref[3] = v2

    col = pl.BlockSpec((R, tc), lambda j: (0, j))
    return pl.pallas_call(
        body, name=name, grid=(C // tc,),
        in_specs=[pl.BlockSpec((N_DEV, R, tc), lambda j: (0, 0, j)), col, col, col],
        out_specs=pl.BlockSpec((4, R, tc), lambda j: (0, 0, j)),
        out_shape=_sds((4, R, C)),
        compiler_params=_cparams(("parallel",)),
    )(recv, w, m, v)


def _adamw_call(g, w, m, v, name):
    def body(g_ref, w_ref, m_ref, v_ref, o_ref):
        delta, m2, v2 = _adamw(w_ref[...], g_ref[...], m_ref[...], v_ref[...])
        o_ref[0] = delta
        o_ref[1] = m2
        o_ref[2] = v2

    return pl.pallas_call(body, name=name, out_shape=_sds((3,) + g.shape))(g, w, m, v)


ADA_N = 6 * D // N_CHIP
KPAD = 128


def _mod_part(c8, w_ada, b_ada):
    tn = 512

    def body(c_ref, w_ref, b_ref, o_ref):
        o_ref[...] = _raw1(_silu(c_ref[...]), w_ref[...], _NN) + b_ref[...]

    return pl.pallas_call(
        body, name="ada_mod", grid=(ADA_N // tn,),
        in_specs=[pl.BlockSpec((16, D), lambda j: (0, 0)), pl.BlockSpec((D, tn), lambda j: (0, j)),
                  pl.BlockSpec((1, tn), lambda j: (0, j))],
        out_specs=pl.BlockSpec((16, tn), lambda j: (0, j)),
        out_shape=_sds((16, ADA_N)),
        compiler_params=_cparams(("parallel",)),
    )(c8, w_ada, b_ada)


def _w_ada_update(c8p, dm, w, m, v):
    tr = 256

    def body(c_ref, dm_ref, w_ref, m_ref, v_ref, o_ref):
        g = _raw1(_silu(c_ref[...]), dm_ref[...], _TN)
        delta, m2, v2 = _adamw(w_ref[...], g, m_ref[...], v_ref[...])
        o_ref[0] = g
        o_ref[1] = delta
        o_ref[2] = m2
        o_ref[3] = v2

    blk = pl.BlockSpec((tr, ADA_N), lambda i: (i, 0))
    return pl.pallas_call(
        body, name="w_ada_update", grid=(D // tr,),
        in_specs=[pl.BlockSpec((KPAD, tr), lambda i: (0, i)), pl.BlockSpec((KPAD, ADA_N), lambda i: (0, 0)),
                  blk, blk, blk],
        out_specs=pl.BlockSpec((4, tr, ADA_N), lambda i: (0, i, 0)),
        out_shape=_sds((4, D, ADA_N)),
        compiler_params=_cparams(("parallel",)),
    )(c8p, dm, w, m, v)


def _me():
    return lax.axis_index("x"), lax.axis_index("y"), lax.axis_index("c")


def _peer(k, me):
    mx, my, mc = me
    return (1 - mx if k & 4 else mx, 1 - my if k & 2 else my, 1 - mc if k & 1 else mc)


def _lin(p):
    return 4 * p[0] + 2 * p[1] + p[2]


def _remote(src, dst, ssem, rsem, dev):
    return pltpu.make_async_remote_copy(src_ref=src, dst_ref=dst, send_sem=ssem, recv_sem=rsem,
                                        device_id=dev, device_id_type=MESH)


def _all_gather8(x, name):
    def body(x_ref, out_ref, send_sems, recv_sems):
        me = _me()
        out_ref[_lin(me)] = x_ref[...]
        sends = []
        for k in range(1, N_DEV):
            cp = _remote(x_ref, out_ref.at[_lin(me)], send_sems.at[k - 1], recv_sems.at[k - 1], _peer(k, me))
            cp.start()
            sends.append(cp)
        for k in range(1, N_DEV):
            p = _peer(k, me)
            _remote(x_ref, out_ref.at[_lin(p)], send_sems.at[k - 1], recv_sems.at[k - 1], p).wait_recv()
        for cp in sends:
            cp.wait_send()

    return pl.pallas_call(
        body, name=name,
        out_shape=_sds((N_DEV,) + x.shape, x.dtype),
        in_specs=[pl.BlockSpec(memory_space=pltpu.VMEM)],
        out_specs=pl.BlockSpec(memory_space=pltpu.VMEM),
        scratch_shapes=[pltpu.SemaphoreType.DMA((N_DEV - 1,)), pltpu.SemaphoreType.DMA((N_DEV - 1,))],
    )(x)


def _hbm_specs(n):
    return [pl.BlockSpec(memory_space=pl.ANY)] * n


def _gather_weights(shards):
    n = len(shards)

    def body(*refs):
        ins, outs = refs[:n], refs[n:2 * n]
        send_sems, recv_sems, local_sems = refs[2 * n:]
        mx, my, mc = _me()
        chips = [(1 - mx, my), (mx, 1 - my), (1 - mx, 1 - my)]
        started, local = [], []
        for a in range(n):
            lc = pltpu.make_async_copy(ins[a], outs[a].at[2 * mx + my], local_sems.at[a])
            lc.start()
            local.append(lc)
            for k, (px, py) in enumerate(chips):
                cp = _remote(ins[a], outs[a].at[2 * mx + my], send_sems.at[3 * a + k], recv_sems.at[3 * a + k],
                             (px, py, mc))
                cp.start()
                started.append(cp)
        for a in range(n):
            for k, (px, py) in enumerate(chips):
                _remote(ins[a], outs[a].at[2 * px + py], send_sems.at[3 * a + k], recv_sems.at[3 * a + k],
                        (px, py, mc)).wait_recv()
        for cp in started:
            cp.wait_send()
        for lc in local:
            lc.wait()

    return pl.pallas_call(
        body, name="gather_weights",
        out_shape=[_sds((N_CHIP,) + s.shape, s.dtype) for s in shards],
        in_specs=_hbm_specs(n), out_specs=_hbm_specs(n),
        scratch_shapes=[pltpu.SemaphoreType.DMA((3 * n,)), pltpu.SemaphoreType.DMA((3 * n,)),
                        pltpu.SemaphoreType.DMA((n,))],
    )(*shards)


def _grad_exchange(pieces):
    n = len(pieces)

    def body(*refs):
        ins, outs = refs[:n], refs[n:2 * n]
        send_sems, recv_sems, local_sems = refs[2 * n:]
        me = _me()
        mx, my, mc = me
        started, local = [], []
        for a in range(n):
            lc = pltpu.make_async_copy(ins[a].at[2 * mx + my, mc], outs[a].at[_lin(me)], local_sems.at[a])
            lc.start()
            local.append(lc)
            for k in range(1, N_DEV):
                p = _peer(k, me)
                s = (N_DEV - 1) * a + k - 1
                cp = _remote(ins[a].at[2 * p[0] + p[1], p[2]], outs[a].at[_lin(me)], send_sems.at[s],
                             recv_sems.at[s], p)
                cp.start()
                started.append(cp)
        for a in range(n):
            for k in range(1, N_DEV):
                p = _peer(k, me)
                s = (N_DEV - 1) * a + k - 1
                _remote(ins[a].at[2 * mx + my, mc], outs[a].at[_lin(p)], send_sems.at[s], recv_sems.at[s],
                        p).wait_recv()
        for cp in started:
            cp.wait_send()
        for lc in local:
            lc.wait()

    return pl.pallas_call(
        body, name="grad_exchange",
        out_shape=[_sds((N_DEV,) + p.shape[2:], p.dtype) for p in pieces],
        in_specs=_hbm_specs(n), out_specs=_hbm_specs(n),
        scratch_shapes=[pltpu.SemaphoreType.DMA(((N_DEV - 1) * n,)), pltpu.SemaphoreType.DMA(((N_DEV - 1) * n,)),
                        pltpu.SemaphoreType.DMA((n,))],
    )(*pieces)


SIB_CHUNKS = 4


def _sibling_exchange(halves):
    n = len(halves)

    def chunks(rows):
        step = rows // SIB_CHUNKS if rows % (8 * SIB_CHUNKS) == 0 else rows
        return [(r, step) for r in range(0, rows, step)]

    plan = [(a, r, sz) for a in range(n) for r, sz in chunks(halves[a].shape[0])]

    def body(*refs):
        ins, outs = refs[:n], refs[n:2 * n]
        send_sems, recv_sems, local_sems = refs[2 * n:]
        mx, my, mc = _me()
        sib = (mx, my, 1 - mc)
        started, local = [], []
        for s, (a, r, sz) in enumerate(plan):
            rows = halves[a].shape[0]
            src = ins[a].at[pl.ds(r, sz)]
            dst = outs[a].at[pl.ds(mc * rows + r, sz)]
            lc = pltpu.make_async_copy(src, dst, local_sems.at[s])
            lc.start()
            local.append(lc)
            cp = _remote(src, dst, send_sems.at[s], recv_sems.at[s], sib)
            cp.start()
            started.append(cp)
        for s, (a, r, sz) in enumerate(plan):
            rows = halves[a].shape[0]
            _remote(ins[a].at[pl.ds(r, sz)], outs[a].at[pl.ds((1 - mc) * rows + r, sz)], send_sems.at[s],
                    recv_sems.at[s], sib).wait_recv()
        for cp in started:
            cp.wait_send()
        for lc in local:
            lc.wait()

    return pl.pallas_call(
        body, name="sibling_exchange",
        out_shape=[_sds((2 * h.shape[0], h.shape[1]), h.dtype) for h in halves],
        in_specs=_hbm_specs(n), out_specs=_hbm_specs(n),
        scratch_shapes=[pltpu.SemaphoreType.DMA((len(plan),)), pltpu.SemaphoreType.DMA((len(plan),)),
                        pltpu.SemaphoreType.DMA((len(plan),))],
    )(*halves)


def _reduce8(recv, name):
    _, rows, cols = recv.shape

    def body(r_ref, o_ref):
        g = r_ref[0].astype(F32)
        for s in range(1, N_DEV):
            g = g + r_ref[s].astype(F32)
        o_ref[...] = g

    return pl.pallas_call(body, name=name, out_shape=_sds((rows, cols)),
                          compiler_params=_cparams())(recv)


def _adamw_big(g, w, m, v, name):
    rows, cols = g.shape
    tr = next(t for t in (256, 176, 128, 64, 8) if rows % t == 0)

    def body(g_ref, w_ref, m_ref, v_ref, d_ref, m2_ref, v2_ref):
        delta, m2, v2 = _adamw(w_ref[...], g_ref[...], m_ref[...], v_ref[...])
        d_ref[...] = delta
        m2_ref[...] = m2
        v2_ref[...] = v2

    blk = pl.BlockSpec((tr, cols), lambda i: (i, 0))
    return pl.pallas_call(
        body, name=name, grid=(rows // tr,),
        in_specs=[blk] * 4, out_specs=[blk] * 3, out_shape=[_sds((rows, cols))] * 3,
        compiler_params=_cparams(("parallel",)),
    )(g, w, m, v)


SMALL_ORDER = (("mod", 6 * D), ("norm1_w", D), ("norm2_w", D), ("conv_w", CONVW * 3 * GW), ("a_log", GH),
               ("dt_bias", GH), ("gdn_norm_w", HD), ("q_norm_w", HD), ("k_norm_w", HD), ("sinks", SQH), ("loss", 1))
SMALL_R = 120


def _pack_small(d):
    parts = [d[k].reshape(-1).astype(F32) if k in d else jnp.zeros((n,), F32) for k, n in SMALL_ORDER]
    used = sum(n for _, n in SMALL_ORDER)
    parts.append(jnp.zeros((SMALL_R * LANE - used,), F32))
    return jnp.concatenate(parts).reshape(SMALL_R, LANE)


def _unpack_small(pk):
    flat = pk.reshape(-1)
    out, r = {}, 0
    for k, n in SMALL_ORDER:
        out[k] = flat[r:r + n]
        r += n
    return out


def kernel(x, c, w_ada, b_ada, norm1_w, w_in, conv_w, a_log, dt_bias, gdn_norm_w, q_norm_w, k_norm_w, sinks, w_out, norm2_w, w_gate, w_up, w_down, loss_target, m_w_ada, m_b_ada, m_norm1_w, m_w_in, m_conv_w, m_a_log, m_dt_bias, m_gdn_norm_w, m_q_norm_w, m_k_norm_w, m_sinks, m_w_out, m_norm2_w, m_w_gate, m_w_up, m_w_down, v_w_ada, v_b_ada, v_norm1_w, v_w_in, v_conv_w, v_a_log, v_dt_bias, v_gdn_norm_w, v_q_norm_w, v_k_norm_w, v_sinks, v_w_out, v_norm2_w, v_w_gate, v_w_up, v_w_down):
    mx, my, mc = _me()
    chip = 2 * mx + my
    dev = 4 * mx + 2 * my + mc
    T = x.shape[1]

    conv_sh = conv_w.reshape(CONVW, 3 * GW // N_CHIP)
    mine = jnp.concatenate([c.reshape(-1), conv_sh.reshape(-1), jnp.zeros((4 * LANE,), F32)]).reshape(24, LANE)
    got = _all_gather8(mine, "gather_c_conv")
    c8 = got[:, :8].reshape(N_DEV, D)
    conv_full = jnp.concatenate([got[2 * j, 8:20].reshape(CONVW, 3 * GW // N_CHIP) for j in range(N_CHIP)], axis=1)
    c16 = jnp.concatenate([c8, jnp.zeros((8, D), F32)], axis=0)
    b_sh = lax.dynamic_slice(b_ada, (0, chip * ADA_N), (1, ADA_N))
    mods = _all_gather8(_mod_part(c16, w_ada[0], b_sh), "gather_mod")
    mod = jnp.concatenate([lax.dynamic_slice(mods[2 * j], (dev, 0), (1, ADA_N)) for j in range(N_CHIP)], axis=1)

    big_w = (w_in, w_out, w_gate, w_up, w_down)
    a_in, a_out, a_gate, a_up, a_down = _gather_weights([t[0].astype(BF16) for t in big_w])
    pieces = lambda a: [a[j] for j in range(N_CHIP)]
    w_in_f = jnp.concatenate(pieces(a_in), axis=1)
    w_out_f = a_out.reshape(D, D)
    w_gu_f = jnp.concatenate(pieces(a_gate) + pieces(a_up), axis=1)
    w_down_f = a_down.reshape(DFF, D)

    loss, grad_x, big, small = _local_step(
        x[0], loss_target[0], mod, norm1_w, _permute_w_in(w_in_f), conv_full, a_log, dt_bias, gdn_norm_w,
        q_norm_w, k_norm_w, sinks, w_out_f, norm2_w, w_gu_f, w_down_f)

    small["loss"] = loss[:, :1]
    sg = _all_gather8(_pack_small(small), "gather_small_grads")
    rep = dict(mod=(b_ada, m_b_ada, v_b_ada), norm1_w=(norm1_w, m_norm1_w, v_norm1_w),
               norm2_w=(norm2_w, m_norm2_w, v_norm2_w), a_log=(a_log, m_a_log, v_a_log),
               dt_bias=(dt_bias, m_dt_bias, v_dt_bias), gdn_norm_w=(gdn_norm_w, m_gdn_norm_w, v_gdn_norm_w),
               q_norm_w=(q_norm_w, m_q_norm_w, v_q_norm_w), k_norm_w=(k_norm_w, m_k_norm_w, v_k_norm_w),
               sinks=(sinks, m_sinks, v_sinks))
    wmv = [_pack_small({k: t[i] for k, t in rep.items()}) for i in range(3)]
    sres = _reduce_adamw(sg, wmv[0], wmv[1], wmv[2], "small_reduce_adamw")
    s_g, s_d, s_m, s_v = [_unpack_small(sres[i]) for i in range(4)]
    loss_out = s_g["loss"][0]

    g_conv = lax.dynamic_slice(s_g["conv_w"].reshape(CONVW, 3 * GW), (0, chip * (3 * GW // N_CHIP)),
                               (CONVW, 3 * GW // N_CHIP))
    pad16 = lambda t: jnp.concatenate([t.reshape(12, LANE), jnp.zeros((4, LANE), F32)], axis=0)
    cres = _adamw_call(pad16(g_conv), pad16(conv_w), pad16(m_conv_w), pad16(v_conv_w), "conv_adamw")
    conv_out = [g_conv.reshape(conv_w.shape)] + [cres[i, :12].reshape(conv_w.shape) for i in range(3)]

    dmod8 = sg[:, :6 * D // LANE].reshape(N_DEV, 6 * D)
    dm = lax.dynamic_slice(dmod8, (0, chip * ADA_N), (N_DEV, ADA_N))
    zpad = lambda t: jnp.concatenate([t, jnp.zeros((KPAD - N_DEV, t.shape[1]), F32)], axis=0)
    ares = _w_ada_update(zpad(c8), zpad(dm), w_ada[0], m_w_ada[0], v_w_ada[0])

    g_in = _unpermute_w_in(big["w_in_p"])
    g_gate, g_up = big["w_gu"][:, :DFF], big["w_gu"][:, DFF:]
    by_cols = lambda g: g.reshape(D, N_CHIP, -1).transpose(1, 0, 2).reshape(N_CHIP, 2, D // 2, -1)
    by_rows = lambda g: g.reshape(N_CHIP, 2, g.shape[0] // (2 * N_CHIP), D)
    recv = _grad_exchange([by_cols(g_in), by_rows(big["w_out"]), by_cols(g_gate), by_cols(g_up),
                           by_rows(big["w_down"])])
    names = ("w_in", "w_out", "w_gate", "w_up", "w_down")
    g_full = _sibling_exchange([_reduce8(r, "reduce_" + nm) for r, nm in zip(recv, names)])
    big_m = (m_w_in, m_w_out, m_w_gate, m_w_up, m_w_down)
    big_v = (v_w_in, v_w_out, v_w_gate, v_w_up, v_w_down)
    upd = [_adamw_big(g, w[0], m[0], v[0], "adamw_" + nm)
           for g, w, m, v, nm in zip(g_full, big_w, big_m, big_v, names)]
    bg = [g[None] for g in g_full]
    bd, bm, bv = [[u[i][None] for u in upd] for i in range(3)]

    def group(a_i, small_d, conv_i, big_l):
        s = lambda k, ref: small_d[k].reshape(ref.shape)
        return [ares[a_i][None], s("mod", b_ada), s("norm1_w", norm1_w), big_l[0], conv_out[conv_i],
                s("a_log", a_log), s("dt_bias", dt_bias), s("gdn_norm_w", gdn_norm_w), s("q_norm_w", q_norm_w),
                s("k_norm_w", k_norm_w), s("sinks", sinks), big_l[1], s("norm2_w", norm2_w), big_l[2], big_l[3],
                big_l[4]]

    outs = [loss_out, grad_x[None]]
    outs += group(0, s_g, 0, bg) + group(1, s_d, 1, bd) + group(2, s_m, 2, bm) + group(3, s_v, 3, bv)
    return tuple(outs)
```

```python
import functools

import jax
import jax.numpy as jnp
from jax import lax
from jax.experimental import pallas as pl
from jax.experimental.pallas import tpu as pltpu

F32 = jnp.float32
BF16 = jnp.bfloat16
MESH = pl.DeviceIdType.MESH

D = 1024
HD = 64
GH = 8
GW = GH * HD
SQH = 8
SKVH = 2
SGRP = SQH // SKVH
WIN = 128
CONVW = 4
CHUNK = 64
DFF = 2816
PROJ = 2832
NP = 3072
EPS = 1e-6
N_DEV = 8
N_CHIP = 4

ADAM_LR = 0.001
ADAM_B1 = 0.9
ADAM_B2 = 0.999
ADAM_EPS = 1e-08
ADAM_WD = 0.01
ADAM_STEP = 10

VMEM_LIMIT = 48 * 1024 * 1024
LANE = 128

PACK_ROWS = (PROJ // N_CHIP, D // N_CHIP, DFF // N_CHIP, DFF // N_CHIP, DFF // N_CHIP)
PACK_P = 3104
PACK_H = PACK_P // 2


def _cparams(sem=None):
    return pltpu.CompilerParams(dimension_semantics=sem, vmem_limit_bytes=VMEM_LIMIT)


_NN = ((1,), (0,))
_NT = ((1,), (1,))
_TN = ((0,), (0,))


def _dot(a, b, dims):
    if a.ndim == 3:
        (ca,), (cb,) = dims
        return lax.dot_general(a, b, (((ca + 1,), (cb + 1,)), ((0,), (0,))), preferred_element_type=F32)
    return lax.dot_general(a, b, (dims, ((), ())), preferred_element_type=F32)


def _raw1(a, b, dims):
    return _dot(a.astype(BF16), b.astype(BF16), dims)


def _raw3(a, b, dims):
    ah = a.astype(BF16)
    al = (a - ah.astype(F32)).astype(BF16)
    bh = b.astype(BF16)
    bl = (b - bh.astype(F32)).astype(BF16)
    return _dot(ah, bh, dims) + (_dot(al, bh, dims) + _dot(ah, bl, dims))


def _make_diff_mm(raw):
    @jax.custom_vjp
    def nn(a, b):
        return raw(a, b, _NN)

    @jax.custom_vjp
    def nt(a, b):
        return raw(a, b, _NT)

    @jax.custom_vjp
    def tn(a, b):
        return raw(a, b, _TN)

    nn.defvjp(lambda a, b: (raw(a, b, _NN), (a, b)), lambda r, g: (nt(g, r[1]), tn(r[0], g)))
    nt.defvjp(lambda a, b: (raw(a, b, _NT), (a, b)), lambda r, g: (nn(g, r[1]), tn(g, r[0])))
    tn.defvjp(lambda a, b: (raw(a, b, _TN), (a, b)), lambda r, g: (nt(r[1], g), nn(r[0], g)))
    return nn, nt, tn


def _tri_inv_raw(a, nn3):
    n = a.shape[-1]
    ri = lax.broadcasted_iota(jnp.int32, (n, n), 0)
    ci = lax.broadcasted_iota(jnp.int32, (n, n), 1)
    t = (ri == ci).astype(F32)
    for lvl in range((n - 1).bit_length()):
        same_pair = (ri >> (lvl + 1)) == (ci >> (lvl + 1))
        lower_left = (((ri >> lvl) & 1) == 1) & (((ci >> lvl) & 1) == 0)
        y = jnp.where(same_pair & lower_left, a, 0.0)
        t = t - y if lvl == 0 else t - nn3(nn3(t, y), t)
    return t


class _Kit:
    def __init__(self, diff):
        if diff:
            self.nn, self.nt, self.tn = _make_diff_mm(_raw1)
            self.nn3, self.nt3, self.tn3 = _make_diff_mm(_raw3)
            nn3, nt3, tn3 = self.nn3, self.nt3, self.tn3

            @jax.custom_vjp
            def inv(a):
                return _tri_inv_raw(a, nn3)

            def inv_fwd(a):
                t = _tri_inv_raw(a, nn3)
                return t, t

            def inv_bwd(t, g):
                return (-tn3(t, nt3(g, t)),)

            inv.defvjp(inv_fwd, inv_bwd)
            self.inv = inv
        else:
            self.nn = lambda a, b: _raw1(a, b, _NN)
            self.nt = lambda a, b: _raw1(a, b, _NT)
            self.tn = lambda a, b: _raw1(a, b, _TN)
            self.nn3 = lambda a, b: _raw3(a, b, _NN)
            self.nt3 = lambda a, b: _raw3(a, b, _NT)
            self.tn3 = lambda a, b: _raw3(a, b, _TN)
            self.inv = lambda a: _tri_inv_raw(a, self.nn3)


def _sigmoid(x):
    return 1.0 / (1.0 + jnp.exp(-x))


def _silu(x):
    return x * _sigmoid(x)


def _rms(x, w):
    return x * lax.rsqrt(jnp.mean(x * x, axis=-1, keepdims=True) + EPS) * w


def _tile(dim, target):
    t = (min(dim, target) // LANE) * LANE
    while t >= LANE:
        if dim % t == 0:
            return t
        t -= LANE
    return dim


def _matmul(a, b, ta=False, tb=False, out_dtype=F32, name="matmul"):
    if ta:
        K, M = a.shape
    else:
        M, K = a.shape
    if tb:
        N, K2 = b.shape
    else:
        K2, N = b.shape
    assert K == K2, (a.shape, b.shape, ta, tb)
    tm, tn, tk = _tile(M, 1024), _tile(N, 1536), _tile(K, 1024)
    nk = K // tk
    dims = ((0,) if ta else (1,), (1,) if tb else (0,))

    def body(a_ref, b_ref, o_ref, acc_ref):
        k = pl.program_id(2)

        @pl.when(k == 0)
        def _():
            acc_ref[...] = jnp.zeros_like(acc_ref)

        acc_ref[...] += _dot(a_ref[...].astype(BF16), b_ref[...].astype(BF16), dims)

        @pl.when(k == nk - 1)
        def _():
            o_ref[...] = acc_ref[...].astype(o_ref.dtype)

    a_spec = (pl.BlockSpec((tk, tm), lambda i, j, k: (k, i)) if ta
              else pl.BlockSpec((tm, tk), lambda i, j, k: (i, k)))
    b_spec = (pl.BlockSpec((tn, tk), lambda i, j, k: (j, k)) if tb
              else pl.BlockSpec((tk, tn), lambda i, j, k: (k, j)))
    return pl.pallas_call(
        body, name=name,
        grid=(M // tm, N // tn, nk),
        in_specs=[a_spec, b_spec],
        out_specs=pl.BlockSpec((tm, tn), lambda i, j, k: (i, j)),
        out_shape=jax.ShapeDtypeStruct((M, N), out_dtype),
        scratch_shapes=[pltpu.VMEM((tm, tn), F32)],
        compiler_params=_cparams(("parallel", "parallel", "arbitrary")),
    )(a, b)


def _rowcall(fn, tiled, consts, out_tiled, out_acc, tm, name):
    T = tiled[0].shape[0]
    n_in = len(tiled) + len(consts)
    n_o = len(out_tiled)

    def body(*refs):
        vals = [r[...] for r in refs[:n_in]]
        outs = refs[n_in:]
        res = fn(*vals)
        for r, v in zip(outs[:n_o], res[:n_o]):
            r[...] = v.astype(r.dtype)
        if len(outs) > n_o:
            @pl.when(pl.program_id(0) == 0)
            def _():
                for r in outs[n_o:]:
                    r[...] = jnp.zeros_like(r)

            for r, v in zip(outs[n_o:], res[n_o:]):
                r[...] += v

    in_specs = [pl.BlockSpec((tm, a.shape[1]), lambda i: (i, 0)) for a in tiled]
    in_specs += [pl.BlockSpec(a.shape, lambda i, nd=a.ndim: (0,) * nd) for a in consts]
    out_specs = [pl.BlockSpec((tm, s.shape[1]), lambda i: (i, 0)) for s in out_tiled]
    out_specs += [pl.BlockSpec(s.shape, lambda i: (0, 0)) for s in out_acc]
    return pl.pallas_call(
        body, name=name, grid=(T // tm,),
        in_specs=in_specs, out_specs=out_specs,
        out_shape=list(out_tiled) + list(out_acc),
        compiler_params=_cparams(("arbitrary",)),
    )(*tiled, *consts)


def _sds(shape, dtype=F32):
    return jax.ShapeDtypeStruct(shape, dtype)


def _norm_mod(x, nw, scale, shift):
    return _rms(x, nw) * (1.0 + scale) + shift


def _norm_mod_fwd(x, nw, scale, shift):
    T = x.shape[0]
    (h,) = _rowcall(lambda *a: (_norm_mod(*a),), [x], [nw, scale, shift],
                    [_sds((T, D), BF16)], [], 512, "norm1_fwd")
    return h


def _norm_mod_bwd(x, dh, dres, nw, scale, shift):
    T = x.shape[0]

    def fn(x, dh, dres, nw, scale, shift):
        _, vjp = jax.vjp(_norm_mod, x, nw, scale, shift)
        dx, dnw, dsc, dsh = vjp(dh)
        return dx + dres, dnw, dsc, dsh

    return _rowcall(fn, [x, dh, dres], [nw, scale, shift], [_sds((T, D))],
                    [_sds((1, D))] * 3, 256, "norm1_bwd")


def _resid_norm(x, mixed, gate1, nw, scale, shift):
    x1 = x + gate1 * mixed
    return x1, _norm_mod(x1, nw, scale, shift)


def _resid_norm_fwd(x, mixed, gate1, nw, scale, shift):
    T = x.shape[0]
    return _rowcall(_resid_norm, [x, mixed], [gate1, nw, scale, shift],
                    [_sds((T, D)), _sds((T, D), BF16)], [], 512, "resid_norm2_fwd")


def _resid_norm_bwd(x, mixed, dy, dh2, gate1, nw, scale, shift):
    T = x.shape[0]

    def fn(x, mixed, dy, dh2, gate1, nw, scale, shift):
        _, vjp = jax.vjp(_resid_norm, x, mixed, gate1, nw, scale, shift)
        dx, dmixed, dg1, dnw, dsc, dsh = vjp((dy, dh2))
        return dx, dmixed, dg1, dnw, dsc, dsh

    return _rowcall(fn, [x, mixed, dy, dh2], [gate1, nw, scale, shift],
                    [_sds((T, D)), _sds((T, D), BF16)], [_sds((1, D))] * 4, 256, "resid_norm2_bwd")


def _ffn_act_fwd(ab):
    T = ab.shape[0]

    def fn(ab):
        a, b = ab[:, :DFF], ab[:, DFF:]
        return (_silu(a) * b,)

    (act,) = _rowcall(fn, [ab], [], [_sds((T, DFF), BF16)], [], 256, "ffn_act_fwd")
    return act


def _ffn_act_bwd(ab, dact):
    T = ab.shape[0]

    def fn(ab, dact):
        a, b = ab[:, :DFF], ab[:, DFF:]
        s = _sigmoid(a)
        da = dact * b * (s * (1.0 + a * (1.0 - s)))
        db = dact * (a * s)
        return (jnp.concatenate([da, db], axis=1),)

    (dab,) = _rowcall(fn, [ab, dact], [], [_sds((T, 2 * DFF), BF16)], [], 256, "ffn_act_bwd")
    return dab


def _loss_head(x1, ffn, target, gate2):
    T = x1.shape[0]

    def fn(x1, ffn, target, gate2):
        y = x1 + gate2 * ffn
        err = y - target
        loss = 0.5 * jnp.sum(jnp.sum(err * err, axis=1, keepdims=True), axis=0, keepdims=True) / D
        dy = err * (1.0 / D)
        dgate2 = jnp.sum(dy * ffn, axis=0, keepdims=True)
        return dy, gate2 * dy, dgate2, jnp.broadcast_to(loss, (1, LANE))

    return _rowcall(fn, [x1, ffn, target], [gate2], [_sds((T, D)), _sds((T, D), BF16)],
                    [_sds((1, D)), _sds((1, LANE))], 256, "loss_head")


def _round_bf16(x):
    return x.astype(BF16).astype(F32)


def _shift_down(x, s, rows):
    if s == 0:
        return x
    return jnp.where(rows >= s, pltpu.roll(x, s, 0), 0.0)


def _shift_up(x, s, rows, T):
    if s == 0:
        return x
    return jnp.where(rows < T - s, pltpu.roll(x, T - s, 0), 0.0)


def _conv_fwd(proj, conv_w):
    T = proj.shape[0]
    ncol = 3 * GW // LANE

    def body(x_ref, w_ref, o_ref):
        x = _round_bf16(x_ref[...])
        rows = lax.broadcasted_iota(jnp.int32, x.shape, 0)
        acc = jnp.zeros_like(x)
        for j in range(CONVW):
            acc = acc + _round_bf16(w_ref[pl.ds(j, 1), :]) * _shift_down(x, CONVW - 1 - j, rows)
        o_ref[...] = _silu(acc)

    return pl.pallas_call(
        body, name="conv_fwd", grid=(ncol,),
        in_specs=[pl.BlockSpec((T, LANE), lambda j: (0, j)), pl.BlockSpec((CONVW, LANE), lambda j: (0, j))],
        out_specs=pl.BlockSpec((T, LANE), lambda j: (0, j)),
        out_shape=_sds((T, 3 * GW)),
        compiler_params=_cparams(("parallel",)),
    )(proj, conv_w)


def _conv_bwd(proj, conv_w, dqc):
    T = proj.shape[0]
    ncol = 3 * GW // LANE

    def body(x_ref, w_ref, d_ref, dx_ref, dw_ref):
        x = _round_bf16(x_ref[...])
        rows = lax.broadcasted_iota(jnp.int32, x.shape, 0)
        xs = [_shift_down(x, CONVW - 1 - j, rows) for j in range(CONVW)]
        w = [_round_bf16(w_ref[pl.ds(j, 1), :]) for j in range(CONVW)]
        pre = jnp.zeros_like(x)
        for j in range(CONVW):
            pre = pre + w[j] * xs[j]
        s = _sigmoid(pre)
        dpre = _round_bf16(d_ref[...] * (s * (1.0 + pre * (1.0 - s))))
        dx = jnp.zeros_like(x)
        for j in range(CONVW):
            dx = dx + w[j] * _shift_up(dpre, CONVW - 1 - j, rows, T)
            dw_ref[pl.ds(j, 1), :] = jnp.sum(dpre * xs[j], axis=0, keepdims=True)
        dx_ref[...] = dx.astype(dx_ref.dtype)

    return pl.pallas_call(
        body, name="conv_bwd", grid=(ncol,),
        in_specs=[pl.BlockSpec((T, LANE), lambda j: (0, j)), pl.BlockSpec((CONVW, LANE), lambda j: (0, j)),
                  pl.BlockSpec((T, LANE), lambda j: (0, j))],
        out_specs=[pl.BlockSpec((T, LANE), lambda j: (0, j)), pl.BlockSpec((CONVW, LANE), lambda j: (0, j))],
        out_shape=[_sds((T, 3 * GW), BF16), _sds((CONVW, 3 * GW))],
        compiler_params=_cparams(("parallel",)),
    )(proj, conv_w, dqc)


def _gdn_prep(kit, q, k, v, ga, gb, alog, dtb):
    C = CHUNK
    ri = lax.broadcasted_iota(jnp.int32, (C, C), 0)
    ci = lax.broadcasted_iota(jnp.int32, (C, C), 1)
    causal = ri >= ci
    strict = ri > ci
    eye = (ri == ci).astype(F32)
    lower = causal.astype(F32)
    upper = (ri <= ci).astype(F32)

    a = ga + dtb
    softplus = jnp.maximum(a, 0.0) + jnp.log(1.0 + jnp.exp(-jnp.abs(a)))
    g_row = -jnp.exp(alog) * softplus
    beta_row = _sigmoid(gb)
    g_col = jnp.sum(eye * g_row, axis=2, keepdims=True)
    beta_col = jnp.sum(eye * beta_row, axis=2, keepdims=True)
    G_col = jnp.sum(lower * g_row, axis=2, keepdims=True)
    G_row = jnp.sum(upper * g_col, axis=1, keepdims=True)
    G_last = jnp.sum(g_row, axis=2, keepdims=True)
    decay = jnp.exp(jnp.where(causal, G_col - G_row, -1e30))

    qn = q * lax.rsqrt(jnp.sum(q * q, axis=-1, keepdims=True) + EPS) * (HD ** -0.5)
    kn = k * lax.rsqrt(jnp.sum(k * k, axis=-1, keepdims=True) + EPS)
    kb = kn * beta_col
    A = jnp.where(strict, kit.nt(kb, kn) * decay, 0.0)
    Tm = kit.inv(A)
    eG = jnp.exp(G_col)
    u = kit.nn3(Tm, v * beta_col)
    w = kit.nn3(Tm, kb * eG)
    qk = jnp.where(causal, kit.nt(qn, kn) * decay, 0.0)
    q_dec = qn * eG
    k_dec = kn * jnp.exp(G_last - G_col)
    dec = jnp.exp(G_last)
    return u, w, qk, q_dec, k_dec, dec


def _gdn_out(o, z, nw):
    return _rms(o, nw) * _silu(z)


GDN_CB = 4


def _gdn_specs(T, blk):
    TB = GDN_CB * CHUNK
    seq = lambda grp: pl.BlockSpec((GH, TB, HD), lambda i, grp=grp: (grp, blk(i), 0))
    row = lambda grp: pl.BlockSpec((GH, GDN_CB, 1, CHUNK), lambda i, grp=grp: (grp, blk(i), 0, 0))
    per_head = pl.BlockSpec((GH, 1, CHUNK), lambda i: (0, 0, 0))
    whole = pl.BlockSpec((1, HD), lambda i: (0, 0))
    state = pl.BlockSpec((GH, GDN_CB, HD, HD), lambda i: (0, blk(i), 0, 0))
    return seq, row, per_head, whole, state


def _gdn_load(seq_refs, row_refs, head_refs):
    chunks = lambda r: jnp.concatenate([r[:, pl.ds(cb * CHUNK, CHUNK), :] for cb in range(GDN_CB)], axis=0)
    rows = lambda r: jnp.concatenate([r[:, cb] for cb in range(GDN_CB)], axis=0)
    heads = lambda r: jnp.concatenate([r[...]] * GDN_CB, axis=0)
    return [chunks(r) for r in seq_refs], [rows(r) for r in row_refs], [heads(r) for r in head_refs]


def _gdn_fwd(qkv_hm, zs_hm, gab, alog_b, dtb_b, nw, shards):
    T = qkv_hm.shape[1]
    N = T // CHUNK
    nblk = N // GDN_CB
    ns = len(shards)
    seq, row, per_head, whole, state = _gdn_specs(T, lambda i: i)
    kit = _Kit(False)

    def body(*refs):
        q_ref, k_ref, v_ref, z_ref, ga_ref, gb_ref, al_ref, dt_ref, nw_ref = refs[:9]
        o_ref, S_ref = refs[9 + ns:11 + ns]
        S_scr = refs[11 + 2 * ns]
        plan = _gather_plan(refs[9:9 + ns], refs[11 + ns:11 + 2 * ns], *refs[12 + 2 * ns:])

        @pl.when(pl.program_id(0) == 0)
        def _():
            S_scr[...] = jnp.zeros_like(S_scr)
            _start(plan)

        (q, k, v, z), (ga, gb), (al, dt) = _gdn_load((q_ref, k_ref, v_ref, z_ref), (ga_ref, gb_ref), (al_ref, dt_ref))
        u, w, qk, q_dec, k_dec, dec = _gdn_prep(kit, q, k, v, ga, gb, al, dt)
        S = S_scr[...]
        for cb in range(GDN_CB):
            hs = slice(cb * GH, (cb + 1) * GH)
            S_ref[:, cb] = S
            v_new = u[hs] - kit.nn(w[hs], S)
            o = kit.nn(q_dec[hs], S) + kit.nn(qk[hs], v_new)
            S = S * dec[hs] + kit.tn(k_dec[hs], v_new)
            o_ref[:, pl.ds(cb * CHUNK, CHUNK), :] = _gdn_out(o, z[hs], nw_ref[...])
        S_scr[...] = S

        @pl.when(pl.program_id(0) == nblk - 1)
        def _():
            _finish(plan)

    res = pl.pallas_call(
        body, name="gdn_fwd", grid=(nblk,),
        in_specs=[seq(0), seq(1), seq(2), seq(0), row(0), row(1), per_head, per_head, whole] + _hbm_specs(ns),
        out_specs=[seq(0), state] + _hbm_specs(ns),
        out_shape=[_sds((GH, T, HD)), _sds((GH, N, HD, HD))] + _gather_shapes(shards),
        scratch_shapes=[pltpu.VMEM((GH, HD, HD), F32)] + _gather_sems(ns),
        compiler_params=_cparams(("arbitrary",)),
    )(qkv_hm, qkv_hm, qkv_hm, zs_hm, gab, gab, alog_b, dtb_b, nw, *shards)
    return res[0], res[1], res[2:]


def _gdn_bwd(qkv_hm, zs_hm, gab, alog_b, dtb_b, nw, S_all, do, pieces):
    T = qkv_hm.shape[1]
    N = T // CHUNK
    nblk = N // GDN_CB
    npc = len(pieces)
    dkit, kit = _Kit(True), _Kit(False)
    rseq, rrow, per_head, whole, rstate = _gdn_specs(T, lambda i: nblk - 1 - i)

    def body(*refs):
        q_ref, k_ref, v_ref, z_ref, ga_ref, gb_ref, al_ref, dt_ref, nw_ref, S_ref, do_ref = refs[:11]
        dq_ref, dk_ref, dv_ref, dz_ref, dga_ref, dgb_ref, dal_ref, ddt_ref, dnw_ref = refs[11 + npc:20 + npc]
        dS_scr = refs[20 + 2 * npc]
        plan = _exchange_plan(refs[11:11 + npc], refs[20 + npc:20 + 2 * npc], *refs[21 + 2 * npc:])

        @pl.when(pl.program_id(0) == 0)
        def _():
            dS_scr[...] = jnp.zeros_like(dS_scr)
            dal_ref[...] = jnp.zeros_like(dal_ref)
            ddt_ref[...] = jnp.zeros_like(ddt_ref)
            dnw_ref[...] = jnp.zeros_like(dnw_ref)
            _start(plan)

        (q, k, v, z, dout), (ga, gb), (al, dt) = _gdn_load((q_ref, k_ref, v_ref, z_ref, do_ref), (ga_ref, gb_ref),
                                                          (al_ref, dt_ref))
        S_in = jnp.concatenate([S_ref[:, cb] for cb in range(GDN_CB)], axis=0)
        (u, w, qk, q_dec, k_dec, dec), prep_vjp = jax.vjp(functools.partial(_gdn_prep, dkit), q, k, v, ga, gb, al, dt)
        v_new = u - kit.nn(w, S_in)
        o = kit.nn(q_dec, S_in) + kit.nn(qk, v_new)
        _, out_vjp = jax.vjp(_gdn_out, o, z, nw_ref[...])
        do, dz, dnw = out_vjp(dout)
        dvn_part = kit.tn(qk, do)
        dS_part = kit.tn(q_dec, do)
        dS = dS_scr[...]
        dS_out, dvn = [None] * GDN_CB, [None] * GDN_CB
        for cb in reversed(range(GDN_CB)):
            hs = slice(cb * GH, (cb + 1) * GH)
            dS_out[cb] = dS
            dvn[cb] = dvn_part[hs] + kit.nn(k_dec[hs], dS)
            dS = dS * dec[hs] + dS_part[hs] - kit.tn(w[hs], dvn[cb])
        dS_scr[...] = dS
        dS_out = jnp.concatenate(dS_out, axis=0)
        dvn = jnp.concatenate(dvn, axis=0)
        ddec = jnp.sum(jnp.sum(S_in * dS_out, axis=2, keepdims=True), axis=1, keepdims=True)
        cts = (dvn, -kit.nt(dvn, S_in), kit.nt(do, v_new), kit.nt(do, S_in), kit.nt(v_new, dS_out), ddec)
        dq, dk, dv, dga, dgb, dal, ddt = prep_vjp(cts)
        lanesum = lambda t: jnp.broadcast_to(jnp.sum(t, axis=2, keepdims=True), t.shape)
        for cb in range(GDN_CB):
            hs = slice(cb * GH, (cb + 1) * GH)
            sl = pl.ds(cb * CHUNK, CHUNK)
            dq_ref[:, sl, :] = dq[hs]
            dk_ref[:, sl, :] = dk[hs]
            dv_ref[:, sl, :] = dv[hs]
            dz_ref[:, sl, :] = dz[hs]
            dga_ref[:, cb] = dga[hs]
            dgb_ref[:, cb] = dgb[hs]
            dal_ref[...] += lanesum(dal[hs])
            ddt_ref[...] += lanesum(ddt[hs])
        dnw_ref[...] += dnw

        @pl.when(pl.program_id(0) == nblk - 1)
        def _():
            _finish(plan)

    res = pl.pallas_call(
        body, name="gdn_bwd", grid=(nblk,),
        in_specs=[rseq(0), rseq(1), rseq(2), rseq(0), rrow(0), rrow(1), per_head, per_head, whole, rstate, rseq(0)]
                 + _hbm_specs(npc),
        out_specs=[rseq(0), rseq(0), rseq(0), rseq(0), rrow(0), rrow(0), per_head, per_head, whole] + _hbm_specs(npc),
        out_shape=[_sds((GH, T, HD))] * 4 + [_sds((GH, N, 1, CHUNK))] * 2 + [_sds((GH, 1, CHUNK))] * 2
                  + [_sds((1, HD))] + _exchange_shapes(pieces),
        scratch_shapes=[pltpu.VMEM((GH, HD, HD), F32)] + _exchange_sems(npc),
        compiler_params=_cparams(("arbitrary",)),
    )(qkv_hm, qkv_hm, qkv_hm, zs_hm, gab, gab, alog_b, dtb_b, nw, S_all, do, *pieces)
    return res[:9], res[9:]


def _swa_block(kit, first, q0, q1, q2, q3, kp, kc, vp, vc, qnw, knw, s0, s1, s2, s3, *, slopes):
    W = WIN
    ri = lax.broadcasted_iota(jnp.int32, (W, W), 0)
    ci = lax.broadcasted_iota(jnp.int32, (W, W), 1)
    mask_c = ri >= ci
    mask_p = ci > ri + first * W
    dist_c = (ri - ci).astype(F32)
    dist_p = (ri - ci + W).astype(F32)
    kpn = _rms(kp, knw)
    kcn = _rms(kc, knw)
    outs = []
    for q, sink, slope in zip((q0, q1, q2, q3), (s0, s1, s2, s3), slopes):
        qn = _rms(q, qnw)
        sc = jnp.where(mask_c, kit.nt(qn, kcn) * (HD ** -0.5) - slope * dist_c, -1e30)
        sp = jnp.where(mask_p, kit.nt(qn, kpn) * (HD ** -0.5) - slope * dist_p, -1e30)
        m = jnp.maximum(jnp.maximum(jnp.max(sc, axis=-1, keepdims=True), jnp.max(sp, axis=-1, keepdims=True)), sink)
        m = lax.stop_gradient(m)
        pc = jnp.exp(sc - m)
        pp = jnp.exp(sp - m)
        den = jnp.sum(pc, axis=-1, keepdims=True) + jnp.sum(pp, axis=-1, keepdims=True) + jnp.exp(sink - m)
        inv = 1.0 / den
        outs.append(kit.nn(pc * inv, vc) + kit.nn(pp * inv, vp))
    return tuple(outs)


def _swa_slopes(hk):
    return tuple(jnp.where(hk == 0, 2.0 ** (-8.0 * (g + 1.0) / SQH), 2.0 ** (-8.0 * (SGRP + g + 1.0) / SQH))
                 for g in range(SGRP))


def _swa_fwd(zs_hm, qnw, knw, sinks_col):
    T = zs_hm.shape[1]
    NB = T // WIN
    kit = _Kit(False)

    def body(q_ref, kp_ref, kc_ref, vp_ref, vc_ref, qnw_ref, knw_ref, s_ref, o_ref):
        hk = pl.program_id(0)
        first = (pl.program_id(1) == 0).astype(jnp.int32)
        args = ([q_ref[g] for g in range(SGRP)] + [kp_ref[...], kc_ref[...], vp_ref[...], vc_ref[...],
                                                     qnw_ref[...], knw_ref[...]] + [s_ref[g] for g in range(SGRP)])
        outs = _swa_block(kit, first, *args, slopes=_swa_slopes(hk))
        for g in range(SGRP):
            o_ref[g] = outs[g]

    qspec = pl.BlockSpec((SGRP, WIN, HD), lambda hk, n: (2 + hk, n, 0))
    cur = lambda off: pl.BlockSpec((None, WIN, HD), lambda hk, n, off=off: (off + hk, n, 0))
    prev = lambda off: pl.BlockSpec((None, WIN, HD), lambda hk, n, off=off: (off + hk, jnp.maximum(n - 1, 0), 0))
    whole = pl.BlockSpec((1, HD), lambda hk, n: (0, 0))
    sspec = pl.BlockSpec((SGRP, WIN, 1), lambda hk, n: (hk, 0, 0))
    return pl.pallas_call(
        body, name="swa_fwd", grid=(SKVH, NB),
        in_specs=[qspec, prev(16), cur(16), prev(18), cur(18), whole, whole, sspec],
        out_specs=pl.BlockSpec((SGRP, WIN, HD), lambda hk, n: (hk, n, 0)),
        out_shape=_sds((SQH, T, HD)),
        compiler_params=_cparams(("parallel", "arbitrary")),
    )(zs_hm, zs_hm, zs_hm, zs_hm, zs_hm, qnw, knw, sinks_col)


def _swa_bwd(zs_hm, qnw, knw, sinks_col, do):
    T = zs_hm.shape[1]
    NB = T // WIN
    kit = _Kit(True)

    def body(q_ref, kp_ref, kc_ref, vp_ref, vc_ref, qnw_ref, knw_ref, s_ref, do_ref,
             dq_ref, dk_ref, dv_ref, dqnw_ref, dknw_ref, ds_ref, ck_scr, cv_scr):
        hk = pl.program_id(0)
        i = pl.program_id(1)
        first = (i == NB - 1).astype(jnp.int32)

        @pl.when(i == 0)
        def _():
            ck_scr[...] = jnp.zeros_like(ck_scr)
            cv_scr[...] = jnp.zeros_like(cv_scr)
            ds_ref[...] = jnp.zeros_like(ds_ref)

        @pl.when((i == 0) & (hk == 0))
        def _():
            dqnw_ref[...] = jnp.zeros_like(dqnw_ref)
            dknw_ref[...] = jnp.zeros_like(dknw_ref)

        args = ([q_ref[g] for g in range(SGRP)] + [kp_ref[...], kc_ref[...], vp_ref[...], vc_ref[...],
                                                     qnw_ref[...], knw_ref[...]] + [s_ref[g] for g in range(SGRP)])
        dos = tuple(do_ref[g] for g in range(SGRP))
        _, vjp = jax.vjp(functools.partial(_swa_block, kit, first, slopes=_swa_slopes(hk)), *args)
        gr = vjp(dos)
        for g in range(SGRP):
            dq_ref[g] = gr[g]
            ds_ref[g] += jnp.broadcast_to(jnp.sum(gr[10 + g], axis=0, keepdims=True), (WIN, 1))
        dkp, dkc, dvp, dvc = gr[4:8]
        dk_ref[...] = dkc + ck_scr[...]
        dv_ref[...] = dvc + cv_scr[...]
        ck_scr[...] = dkp
        cv_scr[...] = dvp
        dqnw_ref[...] += gr[8]
        dknw_ref[...] += gr[9]

    rn = lambda n: NB - 1 - n
    qspec = pl.BlockSpec((SGRP, WIN, HD), lambda hk, i: (2 + hk, rn(i), 0))
    cur = lambda off: pl.BlockSpec((None, WIN, HD), lambda hk, i, off=off: (off + hk, rn(i), 0))
    prev = lambda off: pl.BlockSpec((None, WIN, HD), lambda hk, i, off=off: (off + hk, jnp.maximum(rn(i) - 1, 0), 0))
    whole = pl.BlockSpec((1, HD), lambda hk, i: (0, 0))
    sspec = pl.BlockSpec((SGRP, WIN, 1), lambda hk, i: (hk, 0, 0))
    ospec = pl.BlockSpec((SGRP, WIN, HD), lambda hk, i: (hk, rn(i), 0))
    return pl.pallas_call(
        body, name="swa_bwd", grid=(SKVH, NB),
        in_specs=[qspec, prev(16), cur(16), prev(18), cur(18), whole, whole, sspec, ospec],
        out_specs=[ospec, cur(0), cur(0), whole, whole, sspec],
        out_shape=[_sds((SQH, T, HD)), _sds((SKVH, T, HD)), _sds((SKVH, T, HD)),
                   _sds((1, HD)), _sds((1, HD)), _sds((SQH, WIN, 1))],
        scratch_shapes=[pltpu.VMEM((WIN, HD), F32), pltpu.VMEM((WIN, HD), F32)],
        compiler_params=_cparams(("arbitrary", "arbitrary")),
    )(zs_hm, zs_hm, zs_hm, zs_hm, zs_hm, qnw, knw, sinks_col, do)


def _swa_heads(kit, first, q, kp, kc, vp, vc, qnw, knw, sink, slope):
    W = WIN
    ri = lax.broadcasted_iota(jnp.int32, (W, W), 0)
    ci = lax.broadcasted_iota(jnp.int32, (W, W), 1)
    mask_c = ri >= ci
    mask_p = ci > ri + first * W
    dist_c = (ri - ci).astype(F32)
    dist_p = (ri - ci + W).astype(F32)
    kpn = _rms(kp, knw)
    kcn = _rms(kc, knw)
    qn = _rms(q, qnw)
    sc = jnp.where(mask_c, kit.nt(qn, kcn) * (HD ** -0.5) - slope * dist_c, -1e30)
    sp = jnp.where(mask_p, kit.nt(qn, kpn) * (HD ** -0.5) - slope * dist_p, -1e30)
    m = jnp.maximum(jnp.maximum(jnp.max(sc, axis=-1, keepdims=True), jnp.max(sp, axis=-1, keepdims=True)), sink)
    m = lax.stop_gradient(m)
    pc = jnp.exp(sc - m)
    pp = jnp.exp(sp - m)
    den = jnp.sum(pc, axis=-1, keepdims=True) + jnp.sum(pp, axis=-1, keepdims=True) + jnp.exp(sink - m)
    inv = 1.0 / den
    return kit.nn(pc * inv, vc) + kit.nn(pp * inv, vp)


def _per_query_head(kv_ref):
    return jnp.concatenate([kv_ref[pl.ds(h // SGRP, 1)] for h in range(SQH)], axis=0)


def _per_kv_head(d):
    return jnp.concatenate([jnp.sum(d[g * SGRP:(g + 1) * SGRP], axis=0, keepdims=True) for g in range(SKVH)], axis=0)


def _swa_specs(blk):
    qspec = pl.BlockSpec((SQH, WIN, HD), lambda i: (1, blk(i), 0))
    cur = lambda grp: pl.BlockSpec((SKVH, WIN, HD), lambda i, grp=grp: (grp, blk(i), 0))
    prev = lambda grp: pl.BlockSpec((SKVH, WIN, HD), lambda i, grp=grp: (grp, jnp.maximum(blk(i) - 1, 0), 0))
    whole = pl.BlockSpec((1, HD), lambda i: (0, 0))
    col = pl.BlockSpec((SQH, WIN, 1), lambda i: (0, 0, 0))
    ospec = pl.BlockSpec((SQH, WIN, HD), lambda i: (0, blk(i), 0))
    return qspec, cur, prev, whole, col, ospec


def _swa_fwd(zs_hm, qnw, knw, sinks_col, slopes_col):
    T = zs_hm.shape[1]
    kit = _Kit(False)
    qspec, cur, prev, whole, col, ospec = _swa_specs(lambda i: i)

    def body(q_ref, kp_ref, kc_ref, vp_ref, vc_ref, qnw_ref, knw_ref, s_ref, sl_ref, o_ref):
        first = (pl.program_id(0) == 0).astype(jnp.int32)
        o_ref[...] = _swa_heads(kit, first, q_ref[...], _per_query_head(kp_ref), _per_query_head(kc_ref),
                                _per_query_head(vp_ref), _per_query_head(vc_ref), qnw_ref[...], knw_ref[...],
                                s_ref[...], sl_ref[...])

    return pl.pallas_call(
        body, name="swa_fwd", grid=(T // WIN,),
        in_specs=[qspec, prev(8), cur(8), prev(9), cur(9), whole, whole, col, col],
        out_specs=ospec, out_shape=_sds((SQH, T, HD)),
        compiler_params=_cparams(("arbitrary",)),
    )(zs_hm, zs_hm, zs_hm, zs_hm, zs_hm, qnw, knw, sinks_col, slopes_col)


def _swa_bwd(zs_hm, qnw, knw, sinks_col, slopes_col, do):
    T = zs_hm.shape[1]
    NB = T // WIN
    kit = _Kit(True)
    qspec, cur, prev, whole, col, ospec = _swa_specs(lambda i: NB - 1 - i)

    def body(q_ref, kp_ref, kc_ref, vp_ref, vc_ref, qnw_ref, knw_ref, s_ref, sl_ref, do_ref,
             dq_ref, dk_ref, dv_ref, dqnw_ref, dknw_ref, ds_ref, ck_scr, cv_scr):
        i = pl.program_id(0)
        first = (i == NB - 1).astype(jnp.int32)

        @pl.when(i == 0)
        def _():
            ck_scr[...] = jnp.zeros_like(ck_scr)
            cv_scr[...] = jnp.zeros_like(cv_scr)
            ds_ref[...] = jnp.zeros_like(ds_ref)
            dqnw_ref[...] = jnp.zeros_like(dqnw_ref)
            dknw_ref[...] = jnp.zeros_like(dknw_ref)

        fn = lambda q, kp, kc, vp, vc, qnw, knw, sink: _swa_heads(kit, first, q, kp, kc, vp, vc, qnw, knw, sink,
                                                                  sl_ref[...])
        _, vjp = jax.vjp(fn, q_ref[...], _per_query_head(kp_ref), _per_query_head(kc_ref), _per_query_head(vp_ref),
                         _per_query_head(vc_ref), qnw_ref[...], knw_ref[...], s_ref[...])
        dq, dkp, dkc, dvp, dvc, dqnw, dknw, dsink = vjp(do_ref[...])
        dq_ref[...] = dq
        dk_ref[...] = _per_kv_head(dkc) + ck_scr[...]
        dv_ref[...] = _per_kv_head(dvc) + cv_scr[...]
        ck_scr[...] = _per_kv_head(dkp)
        cv_scr[...] = _per_kv_head(dvp)
        dqnw_ref[...] += dqnw
        dknw_ref[...] += dknw
        ds_ref[...] += jnp.broadcast_to(jnp.sum(dsink, axis=1, keepdims=True), dsink.shape)

    kvspec = pl.BlockSpec((SKVH, WIN, HD), lambda i: (0, NB - 1 - i, 0))
    return pl.pallas_call(
        body, name="swa_bwd", grid=(NB,),
        in_specs=[qspec, prev(8), cur(8), prev(9), cur(9), whole, whole, col, col, ospec],
        out_specs=[ospec, kvspec, kvspec, whole, whole, col],
        out_shape=[_sds((SQH, T, HD)), _sds((SKVH, T, HD)), _sds((SKVH, T, HD)),
                   _sds((1, HD)), _sds((1, HD)), _sds((SQH, WIN, 1))],
        scratch_shapes=[pltpu.VMEM((SKVH, WIN, HD), F32), pltpu.VMEM((SKVH, WIN, HD), F32)],
        compiler_params=_cparams(("arbitrary",)),
    )(zs_hm, zs_hm, zs_hm, zs_hm, zs_hm, qnw, knw, sinks_col, slopes_col, do)


GAB0 = 3 * GW + 1280


def _permute_w_in(w_in):
    return jnp.concatenate([w_in[:, :4 * GW], w_in[:, 4 * GW + 2 * GH:], w_in[:, 4 * GW:4 * GW + 2 * GH],
                            jnp.zeros((D, NP - PROJ), w_in.dtype)], axis=1)


def _unpermute_w_in(g):
    return jnp.concatenate([g[:, :4 * GW], g[:, GAB0:GAB0 + 2 * GH], g[:, 4 * GW:GAB0]], axis=1)


def _pieces_by_cols(g):
    return g.reshape(D, N_CHIP, -1).transpose(1, 0, 2).reshape(N_CHIP, 2, D // 2, -1)


def _pieces_by_rows(g):
    return g.reshape(N_CHIP, 2, g.shape[0] // (2 * N_CHIP), D)


def _local_step(x, target, mod, n1w, w_in_p, conv_w, alog, dtb, gnw, qnw, knw, sinks, w_out, n2w, ffn_shards):
    T = x.shape[0]
    N = T // CHUNK
    shift1, scale1, gate1, shift2, scale2, gate2 = [mod[:, i * D:(i + 1) * D] for i in range(6)]

    h = _norm_mod_fwd(x, n1w, scale1, shift1)
    proj = _matmul(h, w_in_p, name="in_proj")
    qkv_c = _conv_fwd(proj, conv_w)
    qkv_hm = qkv_c.reshape(T, 3 * GH, HD).transpose(1, 0, 2)
    zs_hm = proj[:, 3 * GW:GAB0].reshape(T, 20, HD).transpose(1, 0, 2)
    gab = proj[:, GAB0:GAB0 + 2 * GH].T.reshape(2 * GH, N, 1, CHUNK)
    alog_b = jnp.broadcast_to(alog.reshape(GH, 1, 1), (GH, 1, CHUNK))
    dtb_b = jnp.broadcast_to(dtb.reshape(GH, 1, 1), (GH, 1, CHUNK))
    sinks_col = jnp.broadcast_to(sinks.reshape(SQH, 1, 1), (SQH, WIN, 1))
    o_g, S_all, (a_gate, a_up, a_down) = _gdn_fwd(qkv_hm, zs_hm, gab, alog_b, dtb_b, gnw, ffn_shards)
    w_gu = jnp.concatenate([a_gate[j] for j in range(N_CHIP)] + [a_up[j] for j in range(N_CHIP)], axis=1)
    w_down = a_down.reshape(DFF, D)
    slopes = 2.0 ** (-8.0 * (jnp.arange(SQH, dtype=F32) + 1.0) / SQH)
    slopes_col = jnp.broadcast_to(slopes.reshape(SQH, 1, 1), (SQH, WIN, 1))
    o_s = _swa_fwd(zs_hm, qnw, knw, sinks_col, slopes_col)
    mixcat = jnp.concatenate([o_g, o_s], axis=0).transpose(1, 0, 2).reshape(T, D).astype(BF16)
    mixed = _matmul(mixcat, w_out, name="out_proj")
    x1, h2 = _resid_norm_fwd(x, mixed, gate1, n2w, scale2, shift2)
    ab = _matmul(h2, w_gu, name="ffn_up")
    act = _ffn_act_fwd(ab)
    ffn = _matmul(act, w_down, name="ffn_down")
    dy, dffn, dgate2, loss = _loss_head(x1, ffn, target, gate2)

    dact = _matmul(dffn, w_down, tb=True, name="ffn_down_dx")
    dab = _ffn_act_bwd(ab, dact)
    g_w_down = _matmul(act, dffn, ta=True, out_dtype=BF16, name="ffn_down_dw")
    g_w_gu = _matmul(h2, dab, ta=True, out_dtype=BF16, name="ffn_up_dw")
    dh2 = _matmul(dab, w_gu, tb=True, name="ffn_up_dx")
    dx1, dmixed, dgate1, dn2w, dscale2, dshift2 = _resid_norm_bwd(x, mixed, dy, dh2, gate1, n2w, scale2, shift2)
    g_w_out = _matmul(mixcat, dmixed, ta=True, out_dtype=BF16, name="out_proj_dw")
    dmix_hm = _matmul(dmixed, w_out, tb=True, name="out_proj_dx").reshape(T, 2 * GH, HD).transpose(1, 0, 2)
    ffn_pieces = [_pieces_by_cols(g_w_gu[:, :DFF]), _pieces_by_cols(g_w_gu[:, DFF:]), _pieces_by_rows(g_w_down)]
    (dq, dk, dv, dz, dga, dgb, dalog, ddtb, dgnw), recv_ffn = _gdn_bwd(qkv_hm, zs_hm, gab, alog_b, dtb_b, gnw, S_all,
                                                                       dmix_hm[:GH], ffn_pieces)
    dqkv_hm = jnp.concatenate([dq, dk, dv], axis=0)
    dsq, dsk, dsv, dqnw, dknw, dsinks = _swa_bwd(zs_hm, qnw, knw, sinks_col, slopes_col, dmix_hm[GH:])
    dqkv_pre, dconv = _conv_bwd(proj, conv_w, dqkv_hm.transpose(1, 0, 2).reshape(T, 3 * GW))
    dzs = jnp.concatenate([dz, dsq, dsk, dsv], axis=0).transpose(1, 0, 2).reshape(T, 20 * HD).astype(BF16)
    dgab = jnp.concatenate([dga, dgb], axis=0).reshape(2 * GH, T).T.astype(BF16)
    dproj = jnp.concatenate([dqkv_pre, dzs, dgab, jnp.zeros((T, NP - PROJ), BF16)], axis=1)
    g_w_in_p = _matmul(h, dproj, ta=True, out_dtype=BF16, name="in_proj_dw")
    dh = _matmul(dproj, w_in_p, tb=True, name="in_proj_dx")
    grad_x, dn1w, dscale1, dshift1 = _norm_mod_bwd(x, dh, dx1, n1w, scale1, shift1)

    dmod = jnp.concatenate([dshift1, dscale1, dgate1, dshift2, dscale2, dgate2], axis=1)
    big = dict(w_in_p=g_w_in_p, w_out=g_w_out, recv_ffn=recv_ffn)
    small = dict(mod=dmod, norm1_w=dn1w, norm2_w=dn2w, conv_w=dconv, a_log=dalog[:, 0, 0], dt_bias=ddtb[:, 0, 0],
                 gdn_norm_w=dgnw, q_norm_w=dqnw, k_norm_w=dknw, sinks=dsinks[:, 0, 0])
    return loss, grad_x, big, small


def _adamw(w, g, m, v):
    m2 = ADAM_B1 * m + (1.0 - ADAM_B1) * g
    v2 = ADAM_B2 * v + (1.0 - ADAM_B2) * (g * g)
    m_hat = m2 / (1.0 - ADAM_B1 ** ADAM_STEP)
    v_hat = v2 / (1.0 - ADAM_B2 ** ADAM_STEP)
    delta = -ADAM_LR * (m_hat / (jnp.sqrt(v_hat) + ADAM_EPS) + ADAM_WD * w)
    return delta, m2, v2


def _reduce_adamw(recv, w, m, v, name):
    _, R, C = recv.shape
    tc = _tile(C, 256)

    def body(r_ref, w_ref, m_ref, v_ref, o_ref):
        g = r_ref[0].astype(F32)
        for s in range(1, N_DEV):
            g = g + r_ref[s].astype(F32)
        delta, m2, v2 = _adamw(w_ref[...], g, m_ref[...], v_ref[...])
        o_ref[0] = g
        o_ref[1] = delta
        o_ref[2] = m2
        o_ref[3] = v2

    col = pl.BlockSpec((R, tc), lambda j: (0, j))
    return pl.pallas_call(
        body, name=name, grid=(C // tc,),
        in_specs=[pl.BlockSpec((N_DEV, R, tc), lambda j: (0, 0, j)), col, col, col],
        out_specs=pl.BlockSpec((4, R, tc), lambda j: (0, 0, j)),
        out_shape=_sds((4, R, C)),
        compiler_params=_cparams(("parallel",)),
    )(recv, w, m, v)


def _adamw_call(g, w, m, v, name):
    def body(g_ref, w_ref, m_ref, v_ref, o_ref):
        delta, m2, v2 = _adamw(w_ref[...], g_ref[...], m_ref[...], v_ref[...])
        o_ref[0] = delta
        o_ref[1] = m2
        o_ref[2] = v2

    return pl.pallas_call(body, name=name, out_shape=_sds((3,) + g.shape))(g, w, m, v)


ADA_N = 6 * D // N_CHIP
KPAD = 128


def _mod_part(c8, w_ada, b_ada):
    tn = 512

    def body(c_ref, w_ref, b_ref, o_ref):
        o_ref[...] = _raw1(_silu(c_ref[...]), w_ref[...], _NN) + b_ref[...]

    return pl.pallas_call(
        body, name="ada_mod", grid=(ADA_N // tn,),
        in_specs=[pl.BlockSpec((16, D), lambda j: (0, 0)), pl.BlockSpec((D, tn), lambda j: (0, j)),
                  pl.BlockSpec((1, tn), lambda j: (0, j))],
        out_specs=pl.BlockSpec((16, tn), lambda j: (0, j)),
        out_shape=_sds((16, ADA_N)),
        compiler_params=_cparams(("parallel",)),
    )(c8, w_ada, b_ada)


def _w_ada_update(c8p, dm, w, m, v):
    tr = 256

    def body(c_ref, dm_ref, w_ref, m_ref, v_ref, o_ref):
        g = _raw1(_silu(c_ref[...]), dm_ref[...], _TN)
        delta, m2, v2 = _adamw(w_ref[...], g, m_ref[...], v_ref[...])
        o_ref[0] = g
        o_ref[1] = delta
        o_ref[2] = m2
        o_ref[3] = v2

    blk = pl.BlockSpec((tr, ADA_N), lambda i: (i, 0))
    return pl.pallas_call(
        body, name="w_ada_update", grid=(D // tr,),
        in_specs=[pl.BlockSpec((KPAD, tr), lambda i: (0, i)), pl.BlockSpec((KPAD, ADA_N), lambda i: (0, 0)),
                  blk, blk, blk],
        out_specs=pl.BlockSpec((4, tr, ADA_N), lambda i: (0, i, 0)),
        out_shape=_sds((4, D, ADA_N)),
        compiler_params=_cparams(("parallel",)),
    )(c8p, dm, w, m, v)


def _me():
    return lax.axis_index("x"), lax.axis_index("y"), lax.axis_index("c")


def _peer(k, me):
    mx, my, mc = me
    return (1 - mx if k & 4 else mx, 1 - my if k & 2 else my, 1 - mc if k & 1 else mc)


def _lin(p):
    return 4 * p[0] + 2 * p[1] + p[2]


def _remote(src, dst, ssem, rsem, dev):
    return pltpu.make_async_remote_copy(src_ref=src, dst_ref=dst, send_sem=ssem, recv_sem=rsem,
                                        device_id=dev, device_id_type=MESH)


def _all_gather8(x, name):
    def body(x_ref, out_ref, send_sems, recv_sems):
        me = _me()
        out_ref[_lin(me)] = x_ref[...]
        sends = []
        for k in range(1, N_DEV):
            cp = _remote(x_ref, out_ref.at[_lin(me)], send_sems.at[k - 1], recv_sems.at[k - 1], _peer(k, me))
            cp.start()
            sends.append(cp)
        for k in range(1, N_DEV):
            p = _peer(k, me)
            _remote(x_ref, out_ref.at[_lin(p)], send_sems.at[k - 1], recv_sems.at[k - 1], p).wait_recv()
        for cp in sends:
            cp.wait_send()

    return pl.pallas_call(
        body, name=name,
        out_shape=_sds((N_DEV,) + x.shape, x.dtype),
        in_specs=[pl.BlockSpec(memory_space=pltpu.VMEM)],
        out_specs=pl.BlockSpec(memory_space=pltpu.VMEM),
        scratch_shapes=[pltpu.SemaphoreType.DMA((N_DEV - 1,)), pltpu.SemaphoreType.DMA((N_DEV - 1,))],
    )(x)


def _hbm_specs(n):
    return [pl.BlockSpec(memory_space=pl.ANY)] * n


def _gather_weights(shards):
    n = len(shards)

    def body(*refs):
        plan = _gather_plan(refs[:n], refs[n:2 * n], *refs[2 * n:])
        _start(plan)
        _finish(plan)

    return pl.pallas_call(
        body, name="gather_weights",
        out_shape=_gather_shapes(shards), in_specs=_hbm_specs(n), out_specs=_hbm_specs(n),
        scratch_shapes=_gather_sems(n),
    )(*shards)


def _gather_shapes(shards):
    return [_sds((N_CHIP,) + s.shape, s.dtype) for s in shards]


def _gather_sems(n):
    return [pltpu.SemaphoreType.DMA((3 * n,)), pltpu.SemaphoreType.DMA((3 * n,)), pltpu.SemaphoreType.DMA((n,))]


def _gather_plan(ins, outs, send_sems, recv_sems, local_sems):
    mx, my, mc = _me()
    chips = [(1 - mx, my), (mx, 1 - my), (1 - mx, 1 - my)]
    local, sends, recvs = [], [], []
    for a in range(len(ins)):
        local.append(pltpu.make_async_copy(ins[a], outs[a].at[2 * mx + my], local_sems.at[a]))
        for k, (px, py) in enumerate(chips):
            sems = (send_sems.at[3 * a + k], recv_sems.at[3 * a + k], (px, py, mc))
            sends.append(_remote(ins[a], outs[a].at[2 * mx + my], *sems))
            recvs.append(_remote(ins[a], outs[a].at[2 * px + py], *sems))
    return local, sends, recvs


def _start(plan):
    local, sends, _ = plan
    for cp in local + sends:
        cp.start()


def _finish(plan):
    local, sends, recvs = plan
    for cp in recvs:
        cp.wait_recv()
    for cp in sends:
        cp.wait_send()
    for cp in local:
        cp.wait()


def _grad_exchange(pieces):
    n = len(pieces)

    def body(*refs):
        plan = _exchange_plan(refs[:n], refs[n:2 * n], *refs[2 * n:])
        _start(plan)
        _finish(plan)

    return pl.pallas_call(
        body, name="grad_exchange",
        out_shape=_exchange_shapes(pieces), in_specs=_hbm_specs(n), out_specs=_hbm_specs(n),
        scratch_shapes=_exchange_sems(n),
    )(*pieces)


def _exchange_shapes(pieces):
    return [_sds((N_DEV,) + p.shape[2:], p.dtype) for p in pieces]


def _exchange_sems(n):
    return [pltpu.SemaphoreType.DMA(((N_DEV - 1) * n,)), pltpu.SemaphoreType.DMA(((N_DEV - 1) * n,)),
            pltpu.SemaphoreType.DMA((n,))]


def _exchange_plan(ins, outs, send_sems, recv_sems, local_sems):
    me = _me()
    mx, my, mc = me
    local, sends, recvs = [], [], []
    for a in range(len(ins)):
        local.append(pltpu.make_async_copy(ins[a].at[2 * mx + my, mc], outs[a].at[_lin(me)], local_sems.at[a]))
        for k in range(1, N_DEV):
            p = _peer(k, me)
            s = (N_DEV - 1) * a + k - 1
            sends.append(_remote(ins[a].at[2 * p[0] + p[1], p[2]], outs[a].at[_lin(me)], send_sems.at[s],
                                 recv_sems.at[s], p))
            recvs.append(_remote(ins[a].at[2 * mx + my, mc], outs[a].at[_lin(p)], send_sems.at[s],
                                 recv_sems.at[s], p))
    return local, sends, recvs


def _reduce_swap(recv, name):
    _, rows, cols = recv.shape

    def body(r_ref, o_ref, send_sem, recv_sem):
        mx, my, mc = _me()
        sib = (mx, my, 1 - mc)
        g = r_ref[0].astype(F32)
        for s in range(1, N_DEV):
            g = g + r_ref[s].astype(F32)
        mine = o_ref.at[pl.ds(pl.multiple_of(mc * rows, 8), rows)]
        theirs = o_ref.at[pl.ds(pl.multiple_of((1 - mc) * rows, 8), rows)]
        mine[...] = g
        cp = _remote(mine, mine, send_sem, recv_sem, sib)
        cp.start()
        _remote(mine, theirs, send_sem, recv_sem, sib).wait_recv()
        cp.wait_send()

    return pl.pallas_call(
        body, name=name, out_shape=_sds((2 * rows, cols)),
        in_specs=[pl.BlockSpec(memory_space=pltpu.VMEM)], out_specs=pl.BlockSpec(memory_space=pltpu.VMEM),
        scratch_shapes=[pltpu.SemaphoreType.DMA, pltpu.SemaphoreType.DMA],
        compiler_params=_cparams(),
    )(recv)


def _adamw_big(g, w, m, v, name):
    rows, cols = g.shape
    tr = next(t for t in (256, 176, 128, 64, 8) if rows % t == 0)

    def body(g_ref, w_ref, m_ref, v_ref, d_ref, m2_ref, v2_ref):
        delta, m2, v2 = _adamw(w_ref[...], g_ref[...], m_ref[...], v_ref[...])
        d_ref[...] = delta
        m2_ref[...] = m2
        v2_ref[...] = v2

    blk = pl.BlockSpec((tr, cols), lambda i: (i, 0))
    return pl.pallas_call(
        body, name=name, grid=(rows // tr,),
        in_specs=[blk] * 4, out_specs=[blk] * 3, out_shape=[_sds((rows, cols))] * 3,
        compiler_params=_cparams(("parallel",)),
    )(g, w, m, v)


SMALL_ORDER = (("mod", 6 * D), ("norm1_w", D), ("norm2_w", D), ("conv_w", CONVW * 3 * GW), ("a_log", GH),
               ("dt_bias", GH), ("gdn_norm_w", HD), ("q_norm_w", HD), ("k_norm_w", HD), ("sinks", SQH), ("loss", 1))
SMALL_R = 120


def _pack_small(d):
    parts = [d[k].reshape(-1).astype(F32) if k in d else jnp.zeros((n,), F32) for k, n in SMALL_ORDER]
    used = sum(n for _, n in SMALL_ORDER)
    parts.append(jnp.zeros((SMALL_R * LANE - used,), F32))
    return jnp.concatenate(parts).reshape(SMALL_R, LANE)


def _unpack_small(pk):
    flat = pk.reshape(-1)
    out, r = {}, 0
    for k, n in SMALL_ORDER:
        out[k] = flat[r:r + n]
        r += n
    return out


def kernel(x, c, w_ada, b_ada, norm1_w, w_in, conv_w, a_log, dt_bias, gdn_norm_w, q_norm_w, k_norm_w, sinks, w_out, norm2_w, w_gate, w_up, w_down, loss_target, m_w_ada, m_b_ada, m_norm1_w, m_w_in, m_conv_w, m_a_log, m_dt_bias, m_gdn_norm_w, m_q_norm_w, m_k_norm_w, m_sinks, m_w_out, m_norm2_w, m_w_gate, m_w_up, m_w_down, v_w_ada, v_b_ada, v_norm1_w, v_w_in, v_conv_w, v_a_log, v_dt_bias, v_gdn_norm_w, v_q_norm_w, v_k_norm_w, v_sinks, v_w_out, v_norm2_w, v_w_gate, v_w_up, v_w_down):
    mx, my, mc = _me()
    chip = 2 * mx + my
    dev = 4 * mx + 2 * my + mc
    T = x.shape[1]

    conv_sh = conv_w.reshape(CONVW, 3 * GW // N_CHIP)
    mine = jnp.concatenate([c.reshape(-1), conv_sh.reshape(-1), jnp.zeros((4 * LANE,), F32)]).reshape(24, LANE)
    got = _all_gather8(mine, "gather_c_conv")
    c8 = got[:, :8].reshape(N_DEV, D)
    conv_full = jnp.concatenate([got[2 * j, 8:20].reshape(CONVW, 3 * GW // N_CHIP) for j in range(N_CHIP)], axis=1)
    c16 = jnp.concatenate([c8, jnp.zeros((8, D), F32)], axis=0)
    b_sh = lax.dynamic_slice(b_ada, (0, chip * ADA_N), (1, ADA_N))
    mods = _all_gather8(_mod_part(c16, w_ada[0], b_sh), "gather_mod")
    mod = jnp.concatenate([lax.dynamic_slice(mods[2 * j], (dev, 0), (1, ADA_N)) for j in range(N_CHIP)], axis=1)

    big_w = (w_in, w_out, w_gate, w_up, w_down)
    shards = [t[0].astype(BF16) for t in big_w]
    a_in, a_out = _gather_weights(shards[:2])
    w_in_f = jnp.concatenate([a_in[j] for j in range(N_CHIP)], axis=1)
    w_out_f = a_out.reshape(D, D)

    loss, grad_x, big, small = _local_step(
        x[0], loss_target[0], mod, norm1_w, _permute_w_in(w_in_f), conv_full, a_log, dt_bias, gdn_norm_w,
        q_norm_w, k_norm_w, sinks, w_out_f, norm2_w, shards[2:])

    small["loss"] = loss[:, :1]
    sg = _all_gather8(_pack_small(small), "gather_small_grads")
    rep = dict(mod=(b_ada, m_b_ada, v_b_ada), norm1_w=(norm1_w, m_norm1_w, v_norm1_w),
               norm2_w=(norm2_w, m_norm2_w, v_norm2_w), a_log=(a_log, m_a_log, v_a_log),
               dt_bias=(dt_bias, m_dt_bias, v_dt_bias), gdn_norm_w=(gdn_norm_w, m_gdn_norm_w, v_gdn_norm_w),
               q_norm_w=(q_norm_w, m_q_norm_w, v_q_norm_w), k_norm_w=(k_norm_w, m_k_norm_w, v_k_norm_w),
               sinks=(sinks, m_sinks, v_sinks))
    wmv = [_pack_small({k: t[i] for k, t in rep.items()}) for i in range(3)]
    sres = _reduce_adamw(sg, wmv[0], wmv[1], wmv[2], "small_reduce_adamw")
    s_g, s_d, s_m, s_v = [_unpack_small(sres[i]) for i in range(4)]
    loss_out = s_g["loss"][0]

    g_conv = lax.dynamic_slice(s_g["conv_w"].reshape(CONVW, 3 * GW), (0, chip * (3 * GW // N_CHIP)),
                               (CONVW, 3 * GW // N_CHIP))
    pad16 = lambda t: jnp.concatenate([t.reshape(12, LANE), jnp.zeros((4, LANE), F32)], axis=0)
    cres = _adamw_call(pad16(g_conv), pad16(conv_w), pad16(m_conv_w), pad16(v_conv_w), "conv_adamw")
    conv_out = [g_conv.reshape(conv_w.shape)] + [cres[i, :12].reshape(conv_w.shape) for i in range(3)]

    dmod8 = sg[:, :6 * D // LANE].reshape(N_DEV, 6 * D)
    dm = lax.dynamic_slice(dmod8, (0, chip * ADA_N), (N_DEV, ADA_N))
    zpad = lambda t: jnp.concatenate([t, jnp.zeros((KPAD - N_DEV, t.shape[1]), F32)], axis=0)
    ares = _w_ada_update(zpad(c8), zpad(dm), w_ada[0], m_w_ada[0], v_w_ada[0])

    recv = _grad_exchange([_pieces_by_cols(_unpermute_w_in(big["w_in_p"])), _pieces_by_rows(big["w_out"])])
    names = ("w_in", "w_out", "w_gate", "w_up", "w_down")
    g_full = [_reduce_swap(r, "reduce_" + nm) for r, nm in zip(list(recv) + list(big["recv_ffn"]), names)]
    big_m = (m_w_in, m_w_out, m_w_gate, m_w_up, m_w_down)
    big_v = (v_w_in, v_w_out, v_w_gate, v_w_up, v_w_down)
    upd = [_adamw_big(g, w[0], m[0], v[0], "adamw_" + nm)
           for g, w, m, v, nm in zip(g_full, big_w, big_m, big_v, names)]
    bg = [g[None] for g in g_full]
    bd, bm, bv = [[u[i][None] for u in upd] for i in range(3)]

    def group(a_i, small_d, conv_i, big_l):
        s = lambda k, ref: small_d[k].reshape(ref.shape)
        return [ares[a_i][None], s("mod", b_ada), s("norm1_w", norm1_w), big_l[0], conv_out[conv_i],
                s("a_log", a_log), s("dt_bias", dt_bias), s("gdn_norm_w", gdn_norm_w), s("q_norm_w", q_norm_w),
                s("k_norm_w", k_norm_w), s("sinks", sinks), big_l[1], s("norm2_w", norm2_w), big_l[2], big_l[3],
                big_l[4]]

    outs = [loss_out, grad_x[None]]
    outs += group(0, s_g, 0, bg) + group(1, s_d, 1, bd) + group(2, s_m, 2, bm) + group(3, s_v, 3, bv)
    return tuple(outs)
```

```python
import functools

import jax
import jax.numpy as jnp
from jax import lax
from jax.experimental import pallas as pl
from jax.experimental.pallas import tpu as pltpu

F32 = jnp.float32
BF16 = jnp.bfloat16
MESH = pl.DeviceIdType.MESH

D = 1024
HD = 64
GH = 8
GW = GH * HD
SQH = 8
SKVH = 2
SGRP = SQH // SKVH
WIN = 128
CONVW = 4
CHUNK = 64
DFF = 2816
PROJ = 2832
NP = 3072
EPS = 1e-6
N_DEV = 8
N_CHIP = 4

ADAM_LR = 0.001
ADAM_B1 = 0.9
ADAM_B2 = 0.999
ADAM_EPS = 1e-08
ADAM_WD = 0.01
ADAM_STEP = 10

VMEM_LIMIT = 48 * 1024 * 1024
LANE = 128

PACK_ROWS = (PROJ // N_CHIP, D // N_CHIP, DFF // N_CHIP, DFF // N_CHIP, DFF // N_CHIP)
PACK_P = 3104
PACK_H = PACK_P // 2


def _cparams(sem=None):
    return pltpu.CompilerParams(dimension_semantics=sem, vmem_limit_bytes=VMEM_LIMIT)


_NN = ((1,), (0,))
_NT = ((1,), (1,))
_TN = ((0,), (0,))


def _dot(a, b, dims):
    if a.ndim == 3:
        (ca,), (cb,) = dims
        return lax.dot_general(a, b, (((ca + 1,), (cb + 1,)), ((0,), (0,))), preferred_element_type=F32)
    return lax.dot_general(a, b, (dims, ((), ())), preferred_element_type=F32)


def _raw1(a, b, dims):
    return _dot(a.astype(BF16), b.astype(BF16), dims)


def _raw3(a, b, dims):
    ah = a.astype(BF16)
    al = (a - ah.astype(F32)).astype(BF16)
    bh = b.astype(BF16)
    bl = (b - bh.astype(F32)).astype(BF16)
    return _dot(ah, bh, dims) + (_dot(al, bh, dims) + _dot(ah, bl, dims))


def _make_diff_mm(raw):
    @jax.custom_vjp
    def nn(a, b):
        return raw(a, b, _NN)

    @jax.custom_vjp
    def nt(a, b):
        return raw(a, b, _NT)

    @jax.custom_vjp
    def tn(a, b):
        return raw(a, b, _TN)

    nn.defvjp(lambda a, b: (raw(a, b, _NN), (a, b)), lambda r, g: (nt(g, r[1]), tn(r[0], g)))
    nt.defvjp(lambda a, b: (raw(a, b, _NT), (a, b)), lambda r, g: (nn(g, r[1]), tn(g, r[0])))
    tn.defvjp(lambda a, b: (raw(a, b, _TN), (a, b)), lambda r, g: (nt(r[1], g), nn(r[0], g)))
    return nn, nt, tn


def _tri_inv_raw(a, nn3):
    n = a.shape[-1]
    ri = lax.broadcasted_iota(jnp.int32, (n, n), 0)
    ci = lax.broadcasted_iota(jnp.int32, (n, n), 1)
    t = (ri == ci).astype(F32)
    for lvl in range((n - 1).bit_length()):
        same_pair = (ri >> (lvl + 1)) == (ci >> (lvl + 1))
        lower_left = (((ri >> lvl) & 1) == 1) & (((ci >> lvl) & 1) == 0)
        y = jnp.where(same_pair & lower_left, a, 0.0)
        t = t - y if lvl == 0 else t - nn3(nn3(t, y), t)
    return t


class _Kit:
    def __init__(self, diff):
        if diff:
            self.nn, self.nt, self.tn = _make_diff_mm(_raw1)
            self.nn3, self.nt3, self.tn3 = _make_diff_mm(_raw3)
            nn3, nt3, tn3 = self.nn3, self.nt3, self.tn3

            @jax.custom_vjp
            def inv(a):
                return _tri_inv_raw(a, nn3)

            def inv_fwd(a):
                t = _tri_inv_raw(a, nn3)
                return t, t

            def inv_bwd(t, g):
                return (-tn3(t, nt3(g, t)),)

            inv.defvjp(inv_fwd, inv_bwd)
            self.inv = inv
        else:
            self.nn = lambda a, b: _raw1(a, b, _NN)
            self.nt = lambda a, b: _raw1(a, b, _NT)
            self.tn = lambda a, b: _raw1(a, b, _TN)
            self.nn3 = lambda a, b: _raw3(a, b, _NN)
            self.nt3 = lambda a, b: _raw3(a, b, _NT)
            self.tn3 = lambda a, b: _raw3(a, b, _TN)
            self.inv = lambda a: _tri_inv_raw(a, self.nn3)


def _sigmoid(x):
    return 1.0 / (1.0 + jnp.exp(-x))


def _silu(x):
    return x * _sigmoid(x)


def _rms(x, w):
    return x * lax.rsqrt(jnp.mean(x * x, axis=-1, keepdims=True) + EPS) * w


def _tile(dim, target):
    t = (min(dim, target) // LANE) * LANE
    while t >= LANE:
        if dim % t == 0:
            return t
        t -= LANE
    return dim


def _matmul(a, b, ta=False, tb=False, out_dtype=F32, name="matmul"):
    if ta:
        K, M = a.shape
    else:
        M, K = a.shape
    if tb:
        N, K2 = b.shape
    else:
        K2, N = b.shape
    assert K == K2, (a.shape, b.shape, ta, tb)
    tm, tn, tk = _tile(M, 1024), _tile(N, 1536), _tile(K, 1024)
    nk = K // tk
    dims = ((0,) if ta else (1,), (1,) if tb else (0,))

    def body(a_ref, b_ref, o_ref, acc_ref):
        k = pl.program_id(2)

        @pl.when(k == 0)
        def _():
            acc_ref[...] = jnp.zeros_like(acc_ref)

        acc_ref[...] += _dot(a_ref[...].astype(BF16), b_ref[...].astype(BF16), dims)

        @pl.when(k == nk - 1)
        def _():
            o_ref[...] = acc_ref[...].astype(o_ref.dtype)

    a_spec = (pl.BlockSpec((tk, tm), lambda i, j, k: (k, i)) if ta
              else pl.BlockSpec((tm, tk), lambda i, j, k: (i, k)))
    b_spec = (pl.BlockSpec((tn, tk), lambda i, j, k: (j, k)) if tb
              else pl.BlockSpec((tk, tn), lambda i, j, k: (k, j)))
    return pl.pallas_call(
        body, name=name,
        grid=(M // tm, N // tn, nk),
        in_specs=[a_spec, b_spec],
        out_specs=pl.BlockSpec((tm, tn), lambda i, j, k: (i, j)),
        out_shape=jax.ShapeDtypeStruct((M, N), out_dtype),
        scratch_shapes=[pltpu.VMEM((tm, tn), F32)],
        compiler_params=_cparams(("parallel", "parallel", "arbitrary")),
    )(a, b)


def _rowcall(fn, tiled, consts, out_tiled, out_acc, tm, name):
    T = tiled[0].shape[0]
    n_in = len(tiled) + len(consts)
    n_o = len(out_tiled)

    def body(*refs):
        vals = [r[...] for r in refs[:n_in]]
        outs = refs[n_in:]
        res = fn(*vals)
        for r, v in zip(outs[:n_o], res[:n_o]):
            r[...] = v.astype(r.dtype)
        if len(outs) > n_o:
            @pl.when(pl.program_id(0) == 0)
            def _():
                for r in outs[n_o:]:
                    r[...] = jnp.zeros_like(r)

            for r, v in zip(outs[n_o:], res[n_o:]):
                r[...] += v

    in_specs = [pl.BlockSpec((tm, a.shape[1]), lambda i: (i, 0)) for a in tiled]
    in_specs += [pl.BlockSpec(a.shape, lambda i, nd=a.ndim: (0,) * nd) for a in consts]
    out_specs = [pl.BlockSpec((tm, s.shape[1]), lambda i: (i, 0)) for s in out_tiled]
    out_specs += [pl.BlockSpec(s.shape, lambda i: (0, 0)) for s in out_acc]
    return pl.pallas_call(
        body, name=name, grid=(T // tm,),
        in_specs=in_specs, out_specs=out_specs,
        out_shape=list(out_tiled) + list(out_acc),
        compiler_params=_cparams(("arbitrary",)),
    )(*tiled, *consts)


def _sds(shape, dtype=F32):
    return jax.ShapeDtypeStruct(shape, dtype)


def _norm_mod(x, nw, scale, shift):
    return _rms(x, nw) * (1.0 + scale) + shift


def _norm_mod_fwd(x, nw, scale, shift):
    T = x.shape[0]
    (h,) = _rowcall(lambda *a: (_norm_mod(*a),), [x], [nw, scale, shift],
                    [_sds((T, D), BF16)], [], 512, "norm1_fwd")
    return h


def _norm_mod_bwd(x, dh, dres, nw, scale, shift):
    T = x.shape[0]

    def fn(x, dh, dres, nw, scale, shift):
        _, vjp = jax.vjp(_norm_mod, x, nw, scale, shift)
        dx, dnw, dsc, dsh = vjp(dh)
        return dx + dres, dnw, dsc, dsh

    return _rowcall(fn, [x, dh, dres], [nw, scale, shift], [_sds((T, D))],
                    [_sds((1, D))] * 3, 256, "norm1_bwd")


def _resid_norm(x, mixed, gate1, nw, scale, shift):
    x1 = x + gate1 * mixed
    return x1, _norm_mod(x1, nw, scale, shift)


def _resid_norm_fwd(x, mixed, gate1, nw, scale, shift):
    T = x.shape[0]
    return _rowcall(_resid_norm, [x, mixed], [gate1, nw, scale, shift],
                    [_sds((T, D)), _sds((T, D), BF16)], [], 512, "resid_norm2_fwd")


def _resid_norm_bwd(x, mixed, dy, dh2, gate1, nw, scale, shift):
    T = x.shape[0]

    def fn(x, mixed, dy, dh2, gate1, nw, scale, shift):
        _, vjp = jax.vjp(_resid_norm, x, mixed, gate1, nw, scale, shift)
        dx, dmixed, dg1, dnw, dsc, dsh = vjp((dy, dh2))
        return dx, dmixed, dg1, dnw, dsc, dsh

    return _rowcall(fn, [x, mixed, dy, dh2], [gate1, nw, scale, shift],
                    [_sds((T, D)), _sds((T, D), BF16)], [_sds((1, D))] * 4, 256, "resid_norm2_bwd")


def _ffn_act_fwd(ab):
    T = ab.shape[0]

    def fn(ab):
        a, b = ab[:, :DFF], ab[:, DFF:]
        return (_silu(a) * b,)

    (act,) = _rowcall(fn, [ab], [], [_sds((T, DFF), BF16)], [], 256, "ffn_act_fwd")
    return act


def _ffn_act_bwd(ab, dact):
    T = ab.shape[0]

    def fn(ab, dact):
        a, b = ab[:, :DFF], ab[:, DFF:]
        s = _sigmoid(a)
        da = dact * b * (s * (1.0 + a * (1.0 - s)))
        db = dact * (a * s)
        return (jnp.concatenate([da, db], axis=1),)

    (dab,) = _rowcall(fn, [ab, dact], [], [_sds((T, 2 * DFF), BF16)], [], 256, "ffn_act_bwd")
    return dab


def _loss_head(x1, ffn, target, gate2):
    T = x1.shape[0]

    def fn(x1, ffn, target, gate2):
        y = x1 + gate2 * ffn
        err = y - target
        loss = 0.5 * jnp.sum(jnp.sum(err * err, axis=1, keepdims=True), axis=0, keepdims=True) / D
        dy = err * (1.0 / D)
        dgate2 = jnp.sum(dy * ffn, axis=0, keepdims=True)
        return dy, gate2 * dy, dgate2, jnp.broadcast_to(loss, (1, LANE))

    return _rowcall(fn, [x1, ffn, target], [gate2], [_sds((T, D)), _sds((T, D), BF16)],
                    [_sds((1, D)), _sds((1, LANE))], 256, "loss_head")


def _round_bf16(x):
    return x.astype(BF16).astype(F32)


def _shift_down(x, s, rows):
    if s == 0:
        return x
    return jnp.where(rows >= s, pltpu.roll(x, s, 0), 0.0)


def _shift_up(x, s, rows, T):
    if s == 0:
        return x
    return jnp.where(rows < T - s, pltpu.roll(x, T - s, 0), 0.0)


def _conv_fwd(proj, conv_w):
    T = proj.shape[0]
    ncol = 3 * GW // LANE

    def body(x_ref, w_ref, o_ref):
        x = _round_bf16(x_ref[...])
        rows = lax.broadcasted_iota(jnp.int32, x.shape, 0)
        acc = jnp.zeros_like(x)
        for j in range(CONVW):
            acc = acc + _round_bf16(w_ref[pl.ds(j, 1), :]) * _shift_down(x, CONVW - 1 - j, rows)
        o_ref[0], o_ref[1] = _split_pair(_silu(acc))

    return pl.pallas_call(
        body, name="conv_fwd", grid=(ncol,),
        in_specs=[pl.BlockSpec((T, LANE), lambda j: (0, j)), pl.BlockSpec((CONVW, LANE), lambda j: (0, j))],
        out_specs=pl.BlockSpec((2, T, HD), lambda j: (j, 0, 0)),
        out_shape=_sds((3 * GH, T, HD)),
        compiler_params=_cparams(("parallel",)),
    )(proj, conv_w)


RELAYOUT_TM = 1024


def _split_pair(y):
    return y[:, :HD], pltpu.roll(y, HD, 1)[:, :HD]


def _merge_pair(a, b):
    return jnp.concatenate([a, b], axis=1)


def _split_heads(x, col_block0, nheads, name):
    T = x.shape[0]
    tm = _tile(T, RELAYOUT_TM)

    def body(x_ref, o_ref):
        a, b = _split_pair(x_ref[...])
        o_ref[0] = a
        o_ref[1] = b

    return pl.pallas_call(
        body, name=name, grid=(nheads // 2, T // tm),
        in_specs=[pl.BlockSpec((tm, LANE), lambda j, i: (i, col_block0 + j))],
        out_specs=pl.BlockSpec((2, tm, HD), lambda j, i: (j, i, 0)),
        out_shape=_sds((nheads, T, HD), x.dtype),
        compiler_params=_cparams(("parallel", "parallel")),
    )(x)


def _merge_heads(hm, out_dtype, name, into=None, col_block0=0, head0=0, nheads=None):
    T = hm.shape[1]
    nheads = hm.shape[0] if nheads is None else nheads
    tm = _tile(T, RELAYOUT_TM)

    def body(*refs):
        h_ref, o_ref = refs[0], refs[-1]
        o_ref[...] = _merge_pair(h_ref[0], h_ref[1]).astype(o_ref.dtype)

    in_specs = [pl.BlockSpec((2, tm, HD), lambda j, i: (head0 // 2 + j, i, 0))]
    args = [hm]
    if into is None:
        out_shape = _sds((T, HD * nheads), out_dtype)
        aliases = {}
    else:
        out_shape = _sds(into.shape, into.dtype)
        in_specs.append(pl.BlockSpec(memory_space=pl.ANY))
        args.append(into)
        aliases = {1: 0}
    return pl.pallas_call(
        body, name=name, grid=(nheads // 2, T // tm),
        in_specs=in_specs,
        out_specs=pl.BlockSpec((tm, LANE), lambda j, i: (i, col_block0 + j)),
        out_shape=out_shape, input_output_aliases=aliases,
        compiler_params=_cparams(("parallel", "parallel")),
    )(*args)


def _conv_bwd(proj, conv_w, dqc):
    T = proj.shape[0]
    ncol = 3 * GW // LANE

    def body(x_ref, w_ref, d_ref, dx_ref, dw_ref):
        x = _round_bf16(x_ref[...])
        rows = lax.broadcasted_iota(jnp.int32, x.shape, 0)
        xs = [_shift_down(x, CONVW - 1 - j, rows) for j in range(CONVW)]
        w = [_round_bf16(w_ref[pl.ds(j, 1), :]) for j in range(CONVW)]
        pre = jnp.zeros_like(x)
        for j in range(CONVW):
            pre = pre + w[j] * xs[j]
        s = _sigmoid(pre)
        dpre = _round_bf16(_merge_pair(d_ref[0], d_ref[1]) * (s * (1.0 + pre * (1.0 - s))))
        dx = jnp.zeros_like(x)
        for j in range(CONVW):
            dx = dx + w[j] * _shift_up(dpre, CONVW - 1 - j, rows, T)
            dw_ref[pl.ds(j, 1), :] = jnp.sum(dpre * xs[j], axis=0, keepdims=True)
        dx_ref[...] = dx.astype(dx_ref.dtype)

    return pl.pallas_call(
        body, name="conv_bwd", grid=(ncol,),
        in_specs=[pl.BlockSpec((T, LANE), lambda j: (0, j)), pl.BlockSpec((CONVW, LANE), lambda j: (0, j)),
                  pl.BlockSpec((2, T, HD), lambda j: (j, 0, 0))],
        out_specs=[pl.BlockSpec((T, LANE), lambda j: (0, j)), pl.BlockSpec((CONVW, LANE), lambda j: (0, j))],
        out_shape=[_sds((T, NP), BF16), _sds((CONVW, 3 * GW))],
        compiler_params=_cparams(("parallel",)),
    )(proj, conv_w, dqc)


def _gdn_prep(kit, q, k, v, ga, gb, alog, dtb):
    C = CHUNK
    ri = lax.broadcasted_iota(jnp.int32, (C, C), 0)
    ci = lax.broadcasted_iota(jnp.int32, (C, C), 1)
    causal = ri >= ci
    strict = ri > ci
    eye = (ri == ci).astype(F32)
    lower = causal.astype(F32)
    upper = (ri <= ci).astype(F32)

    a = ga + dtb
    softplus = jnp.maximum(a, 0.0) + jnp.log(1.0 + jnp.exp(-jnp.abs(a)))
    g_row = -jnp.exp(alog) * softplus
    beta_row = _sigmoid(gb)
    g_col = jnp.sum(eye * g_row, axis=2, keepdims=True)
    beta_col = jnp.sum(eye * beta_row, axis=2, keepdims=True)
    G_col = jnp.sum(lower * g_row, axis=2, keepdims=True)
    G_row = jnp.sum(upper * g_col, axis=1, keepdims=True)
    G_last = jnp.sum(g_row, axis=2, keepdims=True)
    decay = jnp.exp(jnp.where(causal, G_col - G_row, -1e30))

    qn = q * lax.rsqrt(jnp.sum(q * q, axis=-1, keepdims=True) + EPS) * (HD ** -0.5)
    kn = k * lax.rsqrt(jnp.sum(k * k, axis=-1, keepdims=True) + EPS)
    kb = kn * beta_col
    A = jnp.where(strict, kit.nt(kb, kn) * decay, 0.0)
    Tm = kit.inv(A)
    eG = jnp.exp(G_col)
    u = kit.nn3(Tm, v * beta_col)
    w = kit.nn3(Tm, kb * eG)
    qk = jnp.where(causal, kit.nt(qn, kn) * decay, 0.0)
    q_dec = qn * eG
    k_dec = kn * jnp.exp(G_last - G_col)
    dec = jnp.exp(G_last)
    return u, w, qk, q_dec, k_dec, dec


def _gdn_out(o, z, nw):
    return _rms(o, nw) * _silu(z)


GDN_CB = 4


def _gdn_specs(T, blk):
    TB = GDN_CB * CHUNK
    seq = lambda grp: pl.BlockSpec((GH, TB, HD), lambda i, grp=grp: (grp, blk(i), 0))
    row = lambda grp: pl.BlockSpec((GH, GDN_CB, 1, CHUNK), lambda i, grp=grp: (grp, blk(i), 0, 0))
    per_head = pl.BlockSpec((GH, 1, CHUNK), lambda i: (0, 0, 0))
    whole = pl.BlockSpec((1, HD), lambda i: (0, 0))
    state = pl.BlockSpec((GH, GDN_CB, HD, HD), lambda i: (0, blk(i), 0, 0))
    return seq, row, per_head, whole, state


def _gdn_load(seq_refs, row_refs, head_refs):
    chunks = lambda r: jnp.concatenate([r[:, pl.ds(cb * CHUNK, CHUNK), :] for cb in range(GDN_CB)], axis=0)
    rows = lambda r: jnp.concatenate([r[:, cb] for cb in range(GDN_CB)], axis=0)
    heads = lambda r: jnp.concatenate([r[...]] * GDN_CB, axis=0)
    return [chunks(r) for r in seq_refs], [rows(r) for r in row_refs], [heads(r) for r in head_refs]


def _gdn_fwd(qkv_hm, zs_hm, gab, alog_b, dtb_b, nw, shards):
    T = qkv_hm.shape[1]
    N = T // CHUNK
    nblk = N // GDN_CB
    ns = len(shards)
    seq, row, per_head, whole, state = _gdn_specs(T, lambda i: i)
    kit = _Kit(False)

    def body(*refs):
        q_ref, k_ref, v_ref, z_ref, ga_ref, gb_ref, al_ref, dt_ref, nw_ref = refs[:9]
        o_ref, S_ref = refs[9 + ns:11 + ns]
        S_scr = refs[11 + 2 * ns]
        plan = _gather_plan(refs[9:9 + ns], refs[11 + ns:11 + 2 * ns], *refs[12 + 2 * ns:])

        @pl.when(pl.program_id(0) == 0)
        def _():
            S_scr[...] = jnp.zeros_like(S_scr)
            _start(plan)

        (q, k, v, z), (ga, gb), (al, dt) = _gdn_load((q_ref, k_ref, v_ref, z_ref), (ga_ref, gb_ref), (al_ref, dt_ref))
        u, w, qk, q_dec, k_dec, dec = _gdn_prep(kit, q, k, v, ga, gb, al, dt)
        S = S_scr[...]
        for cb in range(GDN_CB):
            hs = slice(cb * GH, (cb + 1) * GH)
            S_ref[:, cb] = S
            v_new = u[hs] - kit.nn(w[hs], S)
            o = kit.nn(q_dec[hs], S) + kit.nn(qk[hs], v_new)
            S = S * dec[hs] + kit.tn(k_dec[hs], v_new)
            o_ref[:, pl.ds(cb * CHUNK, CHUNK), :] = _gdn_out(o, z[hs], nw_ref[...])
        S_scr[...] = S

        @pl.when(pl.program_id(0) == nblk - 1)
        def _():
            _finish(plan)

    res = pl.pallas_call(
        body, name="gdn_fwd", grid=(nblk,),
        in_specs=[seq(0), seq(1), seq(2), seq(0), row(0), row(1), per_head, per_head, whole] + _hbm_specs(ns),
        out_specs=[seq(0), state] + _hbm_specs(ns),
        out_shape=[_sds((GH + SQH, T, HD)), _sds((GH, N, HD, HD))] + _gather_shapes(shards),
        scratch_shapes=[pltpu.VMEM((GH, HD, HD), F32)] + _gather_sems(ns),
        compiler_params=_cparams(("arbitrary",)),
    )(qkv_hm, qkv_hm, qkv_hm, zs_hm, gab, gab, alog_b, dtb_b, nw, *shards)
    return res[0], res[1], res[2:]


def _gdn_bwd(qkv_hm, zs_hm, gab, alog_b, dtb_b, nw, S_all, do, pieces):
    T = qkv_hm.shape[1]
    N = T // CHUNK
    nblk = N // GDN_CB
    npc = len(pieces)
    dkit, kit = _Kit(True), _Kit(False)
    rseq, rrow, per_head, whole, rstate = _gdn_specs(T, lambda i: nblk - 1 - i)

    def body(*refs):
        q_ref, k_ref, v_ref, z_ref, ga_ref, gb_ref, al_ref, dt_ref, nw_ref, S_ref, do_ref = refs[:11]
        dqkv_ref, dz_ref, dga_ref, dgb_ref, dal_ref, ddt_ref, dnw_ref = refs[11 + npc:18 + npc]
        dS_scr = refs[18 + 2 * npc]
        plan = _exchange_plan(refs[11:11 + npc], refs[18 + npc:18 + 2 * npc], *refs[19 + 2 * npc:])

        @pl.when(pl.program_id(0) == 0)
        def _():
            dS_scr[...] = jnp.zeros_like(dS_scr)
            dal_ref[...] = jnp.zeros_like(dal_ref)
            ddt_ref[...] = jnp.zeros_like(ddt_ref)
            dnw_ref[...] = jnp.zeros_like(dnw_ref)
            _start(plan)

        (q, k, v, z, dout), (ga, gb), (al, dt) = _gdn_load((q_ref, k_ref, v_ref, z_ref, do_ref), (ga_ref, gb_ref),
                                                          (al_ref, dt_ref))
        S_in = jnp.concatenate([S_ref[:, cb] for cb in range(GDN_CB)], axis=0)
        (u, w, qk, q_dec, k_dec, dec), prep_vjp = jax.vjp(functools.partial(_gdn_prep, dkit), q, k, v, ga, gb, al, dt)
        v_new = u - kit.nn(w, S_in)
        o = kit.nn(q_dec, S_in) + kit.nn(qk, v_new)
        _, out_vjp = jax.vjp(_gdn_out, o, z, nw_ref[...])
        do, dz, dnw = out_vjp(dout)
        dvn_part = kit.tn(qk, do)
        dS_part = kit.tn(q_dec, do)
        dS = dS_scr[...]
        dS_out, dvn = [None] * GDN_CB, [None] * GDN_CB
        for cb in reversed(range(GDN_CB)):
            hs = slice(cb * GH, (cb + 1) * GH)
            dS_out[cb] = dS
            dvn[cb] = dvn_part[hs] + kit.nn(k_dec[hs], dS)
            dS = dS * dec[hs] + dS_part[hs] - kit.tn(w[hs], dvn[cb])
        dS_scr[...] = dS
        dS_out = jnp.concatenate(dS_out, axis=0)
        dvn = jnp.concatenate(dvn, axis=0)
        ddec = jnp.sum(jnp.sum(S_in * dS_out, axis=2, keepdims=True), axis=1, keepdims=True)
        cts = (dvn, -kit.nt(dvn, S_in), kit.nt(do, v_new), kit.nt(do, S_in), kit.nt(v_new, dS_out), ddec)
        dq, dk, dv, dga, dgb, dal, ddt = prep_vjp(cts)
        lanesum = lambda t: jnp.broadcast_to(jnp.sum(t, axis=2, keepdims=True), t.shape)
        for cb in range(GDN_CB):
            hs = slice(cb * GH, (cb + 1) * GH)
            sl = pl.ds(cb * CHUNK, CHUNK)
            dqkv_ref[pl.ds(0, GH), sl, :] = dq[hs]
            dqkv_ref[pl.ds(GH, GH), sl, :] = dk[hs]
            dqkv_ref[pl.ds(2 * GH, GH), sl, :] = dv[hs]
            dz_ref[:, sl, :] = dz[hs]
            dga_ref[:, cb] = dga[hs]
            dgb_ref[:, cb] = dgb[hs]
            dal_ref[...] += lanesum(dal[hs])
            ddt_ref[...] += lanesum(ddt[hs])
        dnw_ref[...] += dnw

        @pl.when(pl.program_id(0) == nblk - 1)
        def _():
            _finish(plan)

    res = pl.pallas_call(
        body, name="gdn_bwd", grid=(nblk,),
        in_specs=[rseq(0), rseq(1), rseq(2), rseq(0), rrow(0), rrow(1), per_head, per_head, whole, rstate, rseq(0)]
                 + _hbm_specs(npc),
        out_specs=[pl.BlockSpec((3 * GH, GDN_CB * CHUNK, HD), lambda i: (0, nblk - 1 - i, 0)), rseq(0), rrow(0),
                   rrow(0), per_head, per_head, whole] + _hbm_specs(npc),
        out_shape=[_sds((3 * GH, T, HD)), _sds((GH + 4 + SWA_GRAD_HEADS, T, HD))] + [_sds((GH, N, 1, CHUNK))] * 2
                  + [_sds((GH, 1, CHUNK))] * 2 + [_sds((1, HD))] + _exchange_shapes(pieces),
        scratch_shapes=[pltpu.VMEM((GH, HD, HD), F32)] + _exchange_sems(npc),
        compiler_params=_cparams(("arbitrary",)),
    )(qkv_hm, qkv_hm, qkv_hm, zs_hm, gab, gab, alog_b, dtb_b, nw, S_all, do, *pieces)
    return res[:7], res[7:]


def _swa_block(kit, first, q0, q1, q2, q3, kp, kc, vp, vc, qnw, knw, s0, s1, s2, s3, *, slopes):
    W = WIN
    ri = lax.broadcasted_iota(jnp.int32, (W, W), 0)
    ci = lax.broadcasted_iota(jnp.int32, (W, W), 1)
    mask_c = ri >= ci
    mask_p = ci > ri + first * W
    dist_c = (ri - ci).astype(F32)
    dist_p = (ri - ci + W).astype(F32)
    kpn = _rms(kp, knw)
    kcn = _rms(kc, knw)
    outs = []
    for q, sink, slope in zip((q0, q1, q2, q3), (s0, s1, s2, s3), slopes):
        qn = _rms(q, qnw)
        sc = jnp.where(mask_c, kit.nt(qn, kcn) * (HD ** -0.5) - slope * dist_c, -1e30)
        sp = jnp.where(mask_p, kit.nt(qn, kpn) * (HD ** -0.5) - slope * dist_p, -1e30)
        m = jnp.maximum(jnp.maximum(jnp.max(sc, axis=-1, keepdims=True), jnp.max(sp, axis=-1, keepdims=True)), sink)
        m = lax.stop_gradient(m)
        pc = jnp.exp(sc - m)
        pp = jnp.exp(sp - m)
        den = jnp.sum(pc, axis=-1, keepdims=True) + jnp.sum(pp, axis=-1, keepdims=True) + jnp.exp(sink - m)
        inv = 1.0 / den
        outs.append(kit.nn(pc * inv, vc) + kit.nn(pp * inv, vp))
    return tuple(outs)


def _swa_slopes(hk):
    return tuple(jnp.where(hk == 0, 2.0 ** (-8.0 * (g + 1.0) / SQH), 2.0 ** (-8.0 * (SGRP + g + 1.0) / SQH))
                 for g in range(SGRP))


def _swa_fwd(zs_hm, qnw, knw, sinks_col):
    T = zs_hm.shape[1]
    NB = T // WIN
    kit = _Kit(False)

    def body(q_ref, kp_ref, kc_ref, vp_ref, vc_ref, qnw_ref, knw_ref, s_ref, o_ref):
        hk = pl.program_id(0)
        first = (pl.program_id(1) == 0).astype(jnp.int32)
        args = ([q_ref[g] for g in range(SGRP)] + [kp_ref[...], kc_ref[...], vp_ref[...], vc_ref[...],
                                                     qnw_ref[...], knw_ref[...]] + [s_ref[g] for g in range(SGRP)])
        outs = _swa_block(kit, first, *args, slopes=_swa_slopes(hk))
        for g in range(SGRP):
            o_ref[g] = outs[g]

    qspec = pl.BlockSpec((SGRP, WIN, HD), lambda hk, n: (2 + hk, n, 0))
    cur = lambda off: pl.BlockSpec((None, WIN, HD), lambda hk, n, off=off: (off + hk, n, 0))
    prev = lambda off: pl.BlockSpec((None, WIN, HD), lambda hk, n, off=off: (off + hk, jnp.maximum(n - 1, 0), 0))
    whole = pl.BlockSpec((1, HD), lambda hk, n: (0, 0))
    sspec = pl.BlockSpec((SGRP, WIN, 1), lambda hk, n: (hk, 0, 0))
    return pl.pallas_call(
        body, name="swa_fwd", grid=(SKVH, NB),
        in_specs=[qspec, prev(16), cur(16), prev(18), cur(18), whole, whole, sspec],
        out_specs=pl.BlockSpec((SGRP, WIN, HD), lambda hk, n: (hk, n, 0)),
        out_shape=_sds((SQH, T, HD)),
        compiler_params=_cparams(("parallel", "arbitrary")),
    )(zs_hm, zs_hm, zs_hm, zs_hm, zs_hm, qnw, knw, sinks_col)


def _swa_bwd(zs_hm, qnw, knw, sinks_col, do):
    T = zs_hm.shape[1]
    NB = T // WIN
    kit = _Kit(True)

    def body(q_ref, kp_ref, kc_ref, vp_ref, vc_ref, qnw_ref, knw_ref, s_ref, do_ref,
             dq_ref, dk_ref, dv_ref, dqnw_ref, dknw_ref, ds_ref, ck_scr, cv_scr):
        hk = pl.program_id(0)
        i = pl.program_id(1)
        first = (i == NB - 1).astype(jnp.int32)

        @pl.when(i == 0)
        def _():
            ck_scr[...] = jnp.zeros_like(ck_scr)
            cv_scr[...] = jnp.zeros_like(cv_scr)
            ds_ref[...] = jnp.zeros_like(ds_ref)

        @pl.when((i == 0) & (hk == 0))
        def _():
            dqnw_ref[...] = jnp.zeros_like(dqnw_ref)
            dknw_ref[...] = jnp.zeros_like(dknw_ref)

        args = ([q_ref[g] for g in range(SGRP)] + [kp_ref[...], kc_ref[...], vp_ref[...], vc_ref[...],
                                                     qnw_ref[...], knw_ref[...]] + [s_ref[g] for g in range(SGRP)])
        dos = tuple(do_ref[g] for g in range(SGRP))
        _, vjp = jax.vjp(functools.partial(_swa_block, kit, first, slopes=_swa_slopes(hk)), *args)
        gr = vjp(dos)
        for g in range(SGRP):
            dq_ref[g] = gr[g]
            ds_ref[g] += jnp.broadcast_to(jnp.sum(gr[10 + g], axis=0, keepdims=True), (WIN, 1))
        dkp, dkc, dvp, dvc = gr[4:8]
        dk_ref[...] = dkc + ck_scr[...]
        dv_ref[...] = dvc + cv_scr[...]
        ck_scr[...] = dkp
        cv_scr[...] = dvp
        dqnw_ref[...] += gr[8]
        dknw_ref[...] += gr[9]

    rn = lambda n: NB - 1 - n
    qspec = pl.BlockSpec((SGRP, WIN, HD), lambda hk, i: (2 + hk, rn(i), 0))
    cur = lambda off: pl.BlockSpec((None, WIN, HD), lambda hk, i, off=off: (off + hk, rn(i), 0))
    prev = lambda off: pl.BlockSpec((None, WIN, HD), lambda hk, i, off=off: (off + hk, jnp.maximum(rn(i) - 1, 0), 0))
    whole = pl.BlockSpec((1, HD), lambda hk, i: (0, 0))
    sspec = pl.BlockSpec((SGRP, WIN, 1), lambda hk, i: (hk, 0, 0))
    ospec = pl.BlockSpec((SGRP, WIN, HD), lambda hk, i: (hk, rn(i), 0))
    return pl.pallas_call(
        body, name="swa_bwd", grid=(SKVH, NB),
        in_specs=[qspec, prev(16), cur(16), prev(18), cur(18), whole, whole, sspec, ospec],
        out_specs=[ospec, cur(0), cur(0), whole, whole, sspec],
        out_shape=[_sds((SQH, T, HD)), _sds((SKVH, T, HD)), _sds((SKVH, T, HD)),
                   _sds((1, HD)), _sds((1, HD)), _sds((SQH, WIN, 1))],
        scratch_shapes=[pltpu.VMEM((WIN, HD), F32), pltpu.VMEM((WIN, HD), F32)],
        compiler_params=_cparams(("arbitrary", "arbitrary")),
    )(zs_hm, zs_hm, zs_hm, zs_hm, zs_hm, qnw, knw, sinks_col, do)


def _swa_heads(kit, first, q, kp, kc, vp, vc, qnw, knw, sink, slope):
    W = WIN
    ri = lax.broadcasted_iota(jnp.int32, (W, W), 0)
    ci = lax.broadcasted_iota(jnp.int32, (W, W), 1)
    mask_c = ri >= ci
    mask_p = ci > ri + first * W
    dist_c = (ri - ci).astype(F32)
    dist_p = (ri - ci + W).astype(F32)
    kpn = _rms(kp, knw)
    kcn = _rms(kc, knw)
    qn = _rms(q, qnw)
    sc = jnp.where(mask_c, kit.nt(qn, kcn) * (HD ** -0.5) - slope * dist_c, -1e30)
    sp = jnp.where(mask_p, kit.nt(qn, kpn) * (HD ** -0.5) - slope * dist_p, -1e30)
    m = jnp.maximum(jnp.maximum(jnp.max(sc, axis=-1, keepdims=True), jnp.max(sp, axis=-1, keepdims=True)), sink)
    m = lax.stop_gradient(m)
    pc = jnp.exp(sc - m)
    pp = jnp.exp(sp - m)
    den = jnp.sum(pc, axis=-1, keepdims=True) + jnp.sum(pp, axis=-1, keepdims=True) + jnp.exp(sink - m)
    inv = 1.0 / den
    return kit.nn(pc * inv, vc) + kit.nn(pp * inv, vp)


def _per_query_head(kv_ref):
    return jnp.concatenate([kv_ref[pl.ds(h // SGRP, 1)] for h in range(SQH)], axis=0)


def _per_kv_head(d):
    return jnp.concatenate([jnp.sum(d[g * SGRP:(g + 1) * SGRP], axis=0, keepdims=True) for g in range(SKVH)], axis=0)


def _swa_specs(blk):
    qspec = pl.BlockSpec((SQH, WIN, HD), lambda i: (1, blk(i), 0))
    cur = lambda grp: pl.BlockSpec((SKVH, WIN, HD), lambda i, grp=grp: (grp, blk(i), 0))
    prev = lambda grp: pl.BlockSpec((SKVH, WIN, HD), lambda i, grp=grp: (grp, jnp.maximum(blk(i) - 1, 0), 0))
    whole = pl.BlockSpec((1, HD), lambda i: (0, 0))
    col = pl.BlockSpec((SQH, WIN, 1), lambda i: (0, 0, 0))
    ospec = pl.BlockSpec((SQH, WIN, HD), lambda i: (0, blk(i), 0))
    return qspec, cur, prev, whole, col, ospec


def _swa_fwd(zs_hm, qnw, knw, sinks_col, slopes_col, o_buf):
    T = zs_hm.shape[1]
    kit = _Kit(False)
    qspec, cur, prev, whole, col, _ = _swa_specs(lambda i: i)

    def body(q_ref, kp_ref, kc_ref, vp_ref, vc_ref, qnw_ref, knw_ref, s_ref, sl_ref, buf_ref, o_ref):
        first = (pl.program_id(0) == 0).astype(jnp.int32)
        o_ref[...] = _swa_heads(kit, first, q_ref[...], _per_query_head(kp_ref), _per_query_head(kc_ref),
                                _per_query_head(vp_ref), _per_query_head(vc_ref), qnw_ref[...], knw_ref[...],
                                s_ref[...], sl_ref[...])

    return pl.pallas_call(
        body, name="swa_fwd", grid=(T // WIN,),
        in_specs=[qspec, prev(8), cur(8), prev(9), cur(9), whole, whole, col, col] + _hbm_specs(1),
        out_specs=pl.BlockSpec((SQH, WIN, HD), lambda i: (1, i, 0)), out_shape=_sds(o_buf.shape),
        input_output_aliases={9: 0},
        compiler_params=_cparams(("arbitrary",)),
    )(zs_hm, zs_hm, zs_hm, zs_hm, zs_hm, qnw, knw, sinks_col, slopes_col, o_buf)


SWA_GRAD_HEADS = SQH + 2 * SKVH


def _swa_bwd(zs_hm, qnw, knw, sinks_col, slopes_col, dmix_hm, d_buf):
    T = zs_hm.shape[1]
    NB = T // WIN
    kit = _Kit(True)
    qspec, cur, prev, whole, col, _ = _swa_specs(lambda i: NB - 1 - i)

    def body(q_ref, kp_ref, kc_ref, vp_ref, vc_ref, qnw_ref, knw_ref, s_ref, sl_ref, do_ref, buf_ref,
             d_ref, dqnw_ref, dknw_ref, ds_ref, ck_scr, cv_scr):
        dq_ref = d_ref.at[pl.ds(0, SQH)]
        dk_ref = d_ref.at[pl.ds(SQH, SKVH)]
        dv_ref = d_ref.at[pl.ds(SQH + SKVH, SKVH)]
        i = pl.program_id(0)
        first = (i == NB - 1).astype(jnp.int32)

        @pl.when(i == 0)
        def _():
            ck_scr[...] = jnp.zeros_like(ck_scr)
            cv_scr[...] = jnp.zeros_like(cv_scr)
            ds_ref[...] = jnp.zeros_like(ds_ref)
            dqnw_ref[...] = jnp.zeros_like(dqnw_ref)
            dknw_ref[...] = jnp.zeros_like(dknw_ref)

        fn = lambda q, kp, kc, vp, vc, qnw, knw, sink: _swa_heads(kit, first, q, kp, kc, vp, vc, qnw, knw, sink,
                                                                  sl_ref[...])
        _, vjp = jax.vjp(fn, q_ref[...], _per_query_head(kp_ref), _per_query_head(kc_ref), _per_query_head(vp_ref),
                         _per_query_head(vc_ref), qnw_ref[...], knw_ref[...], s_ref[...])
        dq, dkp, dkc, dvp, dvc, dqnw, dknw, dsink = vjp(do_ref[...])
        dq_ref[...] = dq
        dk_ref[...] = _per_kv_head(dkc) + ck_scr[...]
        dv_ref[...] = _per_kv_head(dvc) + cv_scr[...]
        ck_scr[...] = _per_kv_head(dkp)
        cv_scr[...] = _per_kv_head(dvp)
        dqnw_ref[...] += dqnw
        dknw_ref[...] += dknw
        ds_ref[...] += jnp.broadcast_to(jnp.sum(dsink, axis=1, keepdims=True), dsink.shape)

    dospec = pl.BlockSpec((SQH, WIN, HD), lambda i: (1, NB - 1 - i, 0))
    dspec = pl.BlockSpec((SWA_GRAD_HEADS, WIN, HD), lambda i: (1, NB - 1 - i, 0))
    res = pl.pallas_call(
        body, name="swa_bwd", grid=(NB,),
        in_specs=[qspec, prev(8), cur(8), prev(9), cur(9), whole, whole, col, col, dospec] + _hbm_specs(1),
        out_specs=[dspec, whole, whole, col],
        out_shape=[_sds(d_buf.shape), _sds((1, HD)), _sds((1, HD)), _sds((SQH, WIN, 1))],
        input_output_aliases={10: 0},
        scratch_shapes=[pltpu.VMEM((SKVH, WIN, HD), F32), pltpu.VMEM((SKVH, WIN, HD), F32)],
        compiler_params=_cparams(("arbitrary",)),
    )(zs_hm, zs_hm, zs_hm, zs_hm, zs_hm, qnw, knw, sinks_col, slopes_col, dmix_hm, d_buf)
    return res


GAB0 = 3 * GW + 1280


def _permute_w_in(w_in):
    return jnp.concatenate([w_in[:, :4 * GW], w_in[:, 4 * GW + 2 * GH:], w_in[:, 4 * GW:4 * GW + 2 * GH],
                            jnp.zeros((D, NP - PROJ), w_in.dtype)], axis=1)


def _unpermute_w_in(g):
    return jnp.concatenate([g[:, :4 * GW], g[:, GAB0:GAB0 + 2 * GH], g[:, 4 * GW:GAB0]], axis=1)


def _pieces_by_cols(g):
    return g.reshape(D, N_CHIP, -1).transpose(1, 0, 2).reshape(N_CHIP, 2, D // 2, -1)


def _pieces_by_rows(g):
    return g.reshape(N_CHIP, 2, g.shape[0] // (2 * N_CHIP), D)


def _local_step(x, target, mod, n1w, w_in_p, conv_w, alog, dtb, gnw, qnw, knw, sinks, w_out, n2w, ffn_shards):
    T = x.shape[0]
    N = T // CHUNK
    shift1, scale1, gate1, shift2, scale2, gate2 = [mod[:, i * D:(i + 1) * D] for i in range(6)]

    h = _norm_mod_fwd(x, n1w, scale1, shift1)
    proj = _matmul(h, w_in_p, name="in_proj")
    qkv_hm = _conv_fwd(proj, conv_w)
    zs_hm = _split_heads(proj, 3 * GW // LANE, 20, "split_zs")
    gab = proj[:, GAB0:GAB0 + 2 * GH].T.reshape(2 * GH, N, 1, CHUNK)
    alog_b = jnp.broadcast_to(alog.reshape(GH, 1, 1), (GH, 1, CHUNK))
    dtb_b = jnp.broadcast_to(dtb.reshape(GH, 1, 1), (GH, 1, CHUNK))
    sinks_col = jnp.broadcast_to(sinks.reshape(SQH, 1, 1), (SQH, WIN, 1))
    o_hm, S_all, (a_gate, a_up, a_down) = _gdn_fwd(qkv_hm, zs_hm, gab, alog_b, dtb_b, gnw, ffn_shards)
    w_gu = jnp.concatenate([a_gate[j] for j in range(N_CHIP)] + [a_up[j] for j in range(N_CHIP)], axis=1)
    w_down = a_down.reshape(DFF, D)
    slopes = 2.0 ** (-8.0 * (jnp.arange(SQH, dtype=F32) + 1.0) / SQH)
    slopes_col = jnp.broadcast_to(slopes.reshape(SQH, 1, 1), (SQH, WIN, 1))
    o_hm = _swa_fwd(zs_hm, qnw, knw, sinks_col, slopes_col, o_hm)
    mixcat = _merge_heads(o_hm, BF16, "merge_mix")
    mixed = _matmul(mixcat, w_out, name="out_proj")
    x1, h2 = _resid_norm_fwd(x, mixed, gate1, n2w, scale2, shift2)
    ab = _matmul(h2, w_gu, name="ffn_up")
    act = _ffn_act_fwd(ab)
    ffn = _matmul(act, w_down, name="ffn_down")
    dy, dffn, dgate2, loss = _loss_head(x1, ffn, target, gate2)

    dact = _matmul(dffn, w_down, tb=True, name="ffn_down_dx")
    dab = _ffn_act_bwd(ab, dact)
    g_w_down = _matmul(act, dffn, ta=True, out_dtype=BF16, name="ffn_down_dw")
    g_w_gu = _matmul(h2, dab, ta=True, out_dtype=BF16, name="ffn_up_dw")
    dh2 = _matmul(dab, w_gu, tb=True, name="ffn_up_dx")
    dx1, dmixed, dgate1, dn2w, dscale2, dshift2 = _resid_norm_bwd(x, mixed, dy, dh2, gate1, n2w, scale2, shift2)
    g_w_out = _matmul(mixcat, dmixed, ta=True, out_dtype=BF16, name="out_proj_dw")
    dmix_hm = _split_heads(_matmul(dmixed, w_out, tb=True, name="out_proj_dx"), 0, GH + SQH, "split_dmix")
    gu_pieces = g_w_gu.reshape(D, 2 * N_CHIP, -1).transpose(1, 0, 2).reshape(2, N_CHIP, 2, D // 2, -1)
    ffn_pieces = [gu_pieces[0], gu_pieces[1], _pieces_by_rows(g_w_down)]
    (dqkv_hm, d_hm, dga, dgb, dalog, ddtb, dgnw), recv_ffn = _gdn_bwd(qkv_hm, zs_hm, gab, alog_b, dtb_b, gnw, S_all,
                                                                      dmix_hm, ffn_pieces)
    d_hm, dqnw, dknw, dsinks = _swa_bwd(zs_hm, qnw, knw, sinks_col, slopes_col, dmix_hm, d_hm)
    dproj, dconv = _conv_bwd(proj, conv_w, dqkv_hm)
    dproj = _merge_heads(d_hm, BF16, "merge_dz", into=dproj, col_block0=3 * GW // LANE, head0=0, nheads=GH)
    dproj = _merge_heads(d_hm, BF16, "merge_dswa", into=dproj, col_block0=4 * GW // LANE, head0=GH + 4,
                         nheads=SWA_GRAD_HEADS)
    dgab = jnp.concatenate([dga, dgb], axis=0).reshape(2 * GH, T).T.astype(BF16)
    dproj = lax.dynamic_update_slice(dproj, jnp.concatenate([dgab, jnp.zeros((T, NP - PROJ), BF16)], axis=1),
                                     (0, GAB0))
    g_w_in_p = _matmul(h, dproj, ta=True, out_dtype=BF16, name="in_proj_dw")
    dh = _matmul(dproj, w_in_p, tb=True, name="in_proj_dx")
    grad_x, dn1w, dscale1, dshift1 = _norm_mod_bwd(x, dh, dx1, n1w, scale1, shift1)

    dmod = jnp.concatenate([dshift1, dscale1, dgate1, dshift2, dscale2, dgate2], axis=1)
    big = dict(w_in_p=g_w_in_p, w_out=g_w_out, recv_ffn=recv_ffn)
    small = dict(mod=dmod, norm1_w=dn1w, norm2_w=dn2w, conv_w=dconv, a_log=dalog[:, 0, 0], dt_bias=ddtb[:, 0, 0],
                 gdn_norm_w=dgnw, q_norm_w=dqnw, k_norm_w=dknw, sinks=dsinks[:, 0, 0])
    return loss, grad_x, big, small


def _adamw(w, g, m, v):
    m2 = ADAM_B1 * m + (1.0 - ADAM_B1) * g
    v2 = ADAM_B2 * v + (1.0 - ADAM_B2) * (g * g)
    m_hat = m2 / (1.0 - ADAM_B1 ** ADAM_STEP)
    v_hat = v2 / (1.0 - ADAM_B2 ** ADAM_STEP)
    delta = -ADAM_LR * (m_hat / (jnp.sqrt(v_hat) + ADAM_EPS) + ADAM_WD * w)
    return delta, m2, v2


def _reduce_adamw(recv, w, m, v, name):
    _, R, C = recv.shape
    tc = _tile(C, 256)

    def body(r_ref, w_ref, m_ref, v_ref, o_ref):
        g = r_ref[0].astype(F32)
        for s in range(1, N_DEV):
            g = g + r_ref[s].astype(F32)
        delta, m2, v2 = _adamw(w_ref[...], g, m_ref[...], v_ref[...])
        o_ref[0] = g
        o_ref[1] = delta
        o_ref[2] = m2
        o_ref[3] = v2

    col = pl.BlockSpec((R, tc), lambda j: (0, j))
    return pl.pallas_call(
        body, name=name, grid=(C // tc,),
        in_specs=[pl.BlockSpec((N_DEV, R, tc), lambda j: (0, 0, j)), col, col, col],
        out_specs=pl.BlockSpec((4, R, tc), lambda j: (0, 0, j)),
        out_shape=_sds((4, R, C)),
        compiler_params=_cparams(("parallel",)),
    )(recv, w, m, v)


def _adamw_call(g, w, m, v, name):
    def body(g_ref, w_ref, m_ref, v_ref, o_ref):
        delta, m2, v2 = _adamw(w_ref[...], g_ref[...], m_ref[...], v_ref[...])
        o_ref[0] = delta
        o_ref[1] = m2
        o_ref[2] = v2

    return pl.pallas_call(body, name=name, out_shape=_sds((3,) + g.shape))(g, w, m, v)


ADA_N = 6 * D // N_CHIP
KPAD = 128


def _mod_part(c8, w_ada, b_ada):
    tn = 512

    def body(c_ref, w_ref, b_ref, o_ref):
        o_ref[...] = _raw1(_silu(c_ref[...]), w_ref[...], _NN) + b_ref[...]

    return pl.pallas_call(
        body, name="ada_mod", grid=(ADA_N // tn,),
        in_specs=[pl.BlockSpec((16, D), lambda j: (0, 0)), pl.BlockSpec((D, tn), lambda j: (0, j)),
                  pl.BlockSpec((1, tn), lambda j: (0, j))],
        out_specs=pl.BlockSpec((16, tn), lambda j: (0, j)),
        out_shape=_sds((16, ADA_N)),
        compiler_params=_cparams(("parallel",)),
    )(c8, w_ada, b_ada)


def _w_ada_update(c8p, dm, w, m, v):
    tr = 256

    def body(c_ref, dm_ref, w_ref, m_ref, v_ref, g_ref, d_ref, m2_ref, v2_ref):
        g = _raw1(_silu(c_ref[...]), dm_ref[...], _TN)
        delta, m2, v2 = _adamw(w_ref[...], g, m_ref[...], v_ref[...])
        g_ref[...] = g
        d_ref[...] = delta
        m2_ref[...] = m2
        v2_ref[...] = v2

    blk = pl.BlockSpec((tr, ADA_N), lambda i: (i, 0))
    return pl.pallas_call(
        body, name="w_ada_update", grid=(D // tr,),
        in_specs=[pl.BlockSpec((KPAD, tr), lambda i: (0, i)), pl.BlockSpec((KPAD, ADA_N), lambda i: (0, 0)),
                  blk, blk, blk],
        out_specs=[blk] * 4, out_shape=[_sds((D, ADA_N))] * 4,
        compiler_params=_cparams(("parallel",)),
    )(c8p, dm, w, m, v)


def _me():
    return lax.axis_index("x"), lax.axis_index("y"), lax.axis_index("c")


def _peer(k, me):
    mx, my, mc = me
    return (1 - mx if k & 4 else mx, 1 - my if k & 2 else my, 1 - mc if k & 1 else mc)


def _lin(p):
    return 4 * p[0] + 2 * p[1] + p[2]


def _remote(src, dst, ssem, rsem, dev):
    return pltpu.make_async_remote_copy(src_ref=src, dst_ref=dst, send_sem=ssem, recv_sem=rsem,
                                        device_id=dev, device_id_type=MESH)


def _all_gather8(x, name):
    def body(x_ref, out_ref, send_sems, recv_sems):
        me = _me()
        out_ref[_lin(me)] = x_ref[...]
        sends = []
        for k in range(1, N_DEV):
            cp = _remote(x_ref, out_ref.at[_lin(me)], send_sems.at[k - 1], recv_sems.at[k - 1], _peer(k, me))
            cp.start()
            sends.append(cp)
        for k in range(1, N_DEV):
            p = _peer(k, me)
            _remote(x_ref, out_ref.at[_lin(p)], send_sems.at[k - 1], recv_sems.at[k - 1], p).wait_recv()
        for cp in sends:
            cp.wait_send()

    return pl.pallas_call(
        body, name=name,
        out_shape=_sds((N_DEV,) + x.shape, x.dtype),
        in_specs=[pl.BlockSpec(memory_space=pltpu.VMEM)],
        out_specs=pl.BlockSpec(memory_space=pltpu.VMEM),
        scratch_shapes=[pltpu.SemaphoreType.DMA((N_DEV - 1,)), pltpu.SemaphoreType.DMA((N_DEV - 1,))],
    )(x)


def _hbm_specs(n):
    return [pl.BlockSpec(memory_space=pl.ANY)] * n


def _gather_weights(shards):
    n = len(shards)

    def body(*refs):
        plan = _gather_plan(refs[:n], refs[n:2 * n], *refs[2 * n:])
        _start(plan)
        _finish(plan)

    return pl.pallas_call(
        body, name="gather_weights",
        out_shape=_gather_shapes(shards), in_specs=_hbm_specs(n), out_specs=_hbm_specs(n),
        scratch_shapes=_gather_sems(n),
    )(*shards)


def _gather_shapes(shards):
    return [_sds((N_CHIP,) + s.shape, s.dtype) for s in shards]


def _gather_sems(n):
    return [pltpu.SemaphoreType.DMA((3 * n,)), pltpu.SemaphoreType.DMA((3 * n,)), pltpu.SemaphoreType.DMA((n,))]


def _gather_plan(ins, outs, send_sems, recv_sems, local_sems):
    mx, my, mc = _me()
    chips = [(1 - mx, my), (mx, 1 - my), (1 - mx, 1 - my)]
    local, sends, recvs = [], [], []
    for a in range(len(ins)):
        local.append(pltpu.make_async_copy(ins[a], outs[a].at[2 * mx + my], local_sems.at[a]))
        for k, (px, py) in enumerate(chips):
            sems = (send_sems.at[3 * a + k], recv_sems.at[3 * a + k], (px, py, mc))
            sends.append(_remote(ins[a], outs[a].at[2 * mx + my], *sems))
            recvs.append(_remote(ins[a], outs[a].at[2 * px + py], *sems))
    return local, sends, recvs


def _start(plan):
    local, sends, _ = plan
    for cp in local + sends:
        cp.start()


def _finish(plan):
    local, sends, recvs = plan
    for cp in recvs:
        cp.wait_recv()
    for cp in sends:
        cp.wait_send()
    for cp in local:
        cp.wait()


def _grad_exchange(pieces):
    n = len(pieces)

    def body(*refs):
        plan = _exchange_plan(refs[:n], refs[n:2 * n], *refs[2 * n:])
        _start(plan)
        _finish(plan)

    return pl.pallas_call(
        body, name="grad_exchange",
        out_shape=_exchange_shapes(pieces), in_specs=_hbm_specs(n), out_specs=_hbm_specs(n),
        scratch_shapes=_exchange_sems(n),
    )(*pieces)


def _exchange_shapes(pieces):
    return [_sds((N_DEV,) + p.shape[2:], p.dtype) for p in pieces]


def _exchange_sems(n):
    return [pltpu.SemaphoreType.DMA(((N_DEV - 1) * n,)), pltpu.SemaphoreType.DMA(((N_DEV - 1) * n,)),
            pltpu.SemaphoreType.DMA((n,))]


def _exchange_plan(ins, outs, send_sems, recv_sems, local_sems):
    me = _me()
    mx, my, mc = me
    local, sends, recvs = [], [], []
    for a in range(len(ins)):
        local.append(pltpu.make_async_copy(ins[a].at[2 * mx + my, mc], outs[a].at[_lin(me)], local_sems.at[a]))
        for k in range(1, N_DEV):
            p = _peer(k, me)
            s = (N_DEV - 1) * a + k - 1
            sends.append(_remote(ins[a].at[2 * p[0] + p[1], p[2]], outs[a].at[_lin(me)], send_sems.at[s],
                                 recv_sems.at[s], p))
            recvs.append(_remote(ins[a].at[2 * mx + my, mc], outs[a].at[_lin(p)], send_sems.at[s],
                                 recv_sems.at[s], p))
    return local, sends, recvs


def _reduce_swap(recv, name):
    _, rows, cols = recv.shape

    def body(r_ref, o_ref, send_sem, recv_sem):
        mx, my, mc = _me()
        sib = (mx, my, 1 - mc)
        g = r_ref[0].astype(F32)
        for s in range(1, N_DEV):
            g = g + r_ref[s].astype(F32)
        mine = o_ref.at[pl.ds(pl.multiple_of(mc * rows, 8), rows)]
        theirs = o_ref.at[pl.ds(pl.multiple_of((1 - mc) * rows, 8), rows)]
        mine[...] = g
        cp = _remote(mine, mine, send_sem, recv_sem, sib)
        cp.start()
        _remote(mine, theirs, send_sem, recv_sem, sib).wait_recv()
        cp.wait_send()

    return pl.pallas_call(
        body, name=name, out_shape=_sds((2 * rows, cols)),
        in_specs=[pl.BlockSpec(memory_space=pltpu.VMEM)], out_specs=pl.BlockSpec(memory_space=pltpu.VMEM),
        scratch_shapes=[pltpu.SemaphoreType.DMA, pltpu.SemaphoreType.DMA],
        compiler_params=_cparams(),
    )(recv)


def _adamw_big(g, w, m, v, name):
    rows, cols = g.shape
    tr = next(t for t in (256, 176, 128, 64, 8) if rows % t == 0)

    def body(g_ref, w_ref, m_ref, v_ref, d_ref, m2_ref, v2_ref):
        delta, m2, v2 = _adamw(w_ref[...], g_ref[...], m_ref[...], v_ref[...])
        d_ref[...] = delta
        m2_ref[...] = m2
        v2_ref[...] = v2

    blk = pl.BlockSpec((tr, cols), lambda i: (i, 0))
    return pl.pallas_call(
        body, name=name, grid=(rows // tr,),
        in_specs=[blk] * 4, out_specs=[blk] * 3, out_shape=[_sds((rows, cols))] * 3,
        compiler_params=_cparams(("parallel",)),
    )(g, w, m, v)


SMALL_ORDER = (("mod", 6 * D), ("norm1_w", D), ("norm2_w", D), ("conv_w", CONVW * 3 * GW), ("a_log", GH),
               ("dt_bias", GH), ("gdn_norm_w", HD), ("q_norm_w", HD), ("k_norm_w", HD), ("sinks", SQH), ("loss", 1))
SMALL_R = 120


def _pack_small(d):
    parts = [d[k].reshape(-1).astype(F32) if k in d else jnp.zeros((n,), F32) for k, n in SMALL_ORDER]
    used = sum(n for _, n in SMALL_ORDER)
    parts.append(jnp.zeros((SMALL_R * LANE - used,), F32))
    return jnp.concatenate(parts).reshape(SMALL_R, LANE)


def _unpack_small(pk):
    flat = pk.reshape(-1)
    out, r = {}, 0
    for k, n in SMALL_ORDER:
        out[k] = flat[r:r + n]
        r += n
    return out


def kernel(x, c, w_ada, b_ada, norm1_w, w_in, conv_w, a_log, dt_bias, gdn_norm_w, q_norm_w, k_norm_w, sinks, w_out, norm2_w, w_gate, w_up, w_down, loss_target, m_w_ada, m_b_ada, m_norm1_w, m_w_in, m_conv_w, m_a_log, m_dt_bias, m_gdn_norm_w, m_q_norm_w, m_k_norm_w, m_sinks, m_w_out, m_norm2_w, m_w_gate, m_w_up, m_w_down, v_w_ada, v_b_ada, v_norm1_w, v_w_in, v_conv_w, v_a_log, v_dt_bias, v_gdn_norm_w, v_q_norm_w, v_k_norm_w, v_sinks, v_w_out, v_norm2_w, v_w_gate, v_w_up, v_w_down):
    mx, my, mc = _me()
    chip = 2 * mx + my
    dev = 4 * mx + 2 * my + mc
    T = x.shape[1]

    conv_sh = conv_w.reshape(CONVW, 3 * GW // N_CHIP)
    mine = jnp.concatenate([c.reshape(-1), conv_sh.reshape(-1), jnp.zeros((4 * LANE,), F32)]).reshape(24, LANE)
    got = _all_gather8(mine, "gather_c_conv")
    c8 = got[:, :8].reshape(N_DEV, D)
    conv_full = jnp.concatenate([got[2 * j, 8:20].reshape(CONVW, 3 * GW // N_CHIP) for j in range(N_CHIP)], axis=1)
    c16 = jnp.concatenate([c8, jnp.zeros((8, D), F32)], axis=0)
    b_sh = lax.dynamic_slice(b_ada, (0, chip * ADA_N), (1, ADA_N))
    mods = _all_gather8(_mod_part(c16, w_ada[0], b_sh), "gather_mod")
    mod = jnp.concatenate([lax.dynamic_slice(mods[2 * j], (dev, 0), (1, ADA_N)) for j in range(N_CHIP)], axis=1)

    big_w = (w_in, w_out, w_gate, w_up, w_down)
    shards = [t[0].astype(BF16) for t in big_w]
    a_in, a_out = _gather_weights(shards[:2])
    w_in_f = jnp.concatenate([a_in[j] for j in range(N_CHIP)], axis=1)
    w_out_f = a_out.reshape(D, D)

    loss, grad_x, big, small = _local_step(
        x[0], loss_target[0], mod, norm1_w, _permute_w_in(w_in_f), conv_full, a_log, dt_bias, gdn_norm_w,
        q_norm_w, k_norm_w, sinks, w_out_f, norm2_w, shards[2:])

    small["loss"] = loss[:, :1]
    sg = _all_gather8(_pack_small(small), "gather_small_grads")
    rep = dict(mod=(b_ada, m_b_ada, v_b_ada), norm1_w=(norm1_w, m_norm1_w, v_norm1_w),
               norm2_w=(norm2_w, m_norm2_w, v_norm2_w), a_log=(a_log, m_a_log, v_a_log),
               dt_bias=(dt_bias, m_dt_bias, v_dt_bias), gdn_norm_w=(gdn_norm_w, m_gdn_norm_w, v_gdn_norm_w),
               q_norm_w=(q_norm_w, m_q_norm_w, v_q_norm_w), k_norm_w=(k_norm_w, m_k_norm_w, v_k_norm_w),
               sinks=(sinks, m_sinks, v_sinks))
    wmv = [_pack_small({k: t[i] for k, t in rep.items()}) for i in range(3)]
    sres = _reduce_adamw(sg, wmv[0], wmv[1], wmv[2], "small_reduce_adamw")
    s_g, s_d, s_m, s_v = [_unpack_small(sres[i]) for i in range(4)]
    loss_out = s_g["loss"][0]

    g_conv = lax.dynamic_slice(s_g["conv_w"].reshape(CONVW, 3 * GW), (0, chip * (3 * GW // N_CHIP)),
                               (CONVW, 3 * GW // N_CHIP))
    pad16 = lambda t: jnp.concatenate([t.reshape(12, LANE), jnp.zeros((4, LANE), F32)], axis=0)
    cres = _adamw_call(pad16(g_conv), pad16(conv_w), pad16(m_conv_w), pad16(v_conv_w), "conv_adamw")
    conv_out = [g_conv.reshape(conv_w.shape)] + [cres[i, :12].reshape(conv_w.shape) for i in range(3)]

    dmod8 = sg[:, :6 * D // LANE].reshape(N_DEV, 6 * D)
    dm = lax.dynamic_slice(dmod8, (0, chip * ADA_N), (N_DEV, ADA_N))
    zpad = lambda t: jnp.concatenate([t, jnp.zeros((KPAD - N_DEV, t.shape[1]), F32)], axis=0)
    ares = _w_ada_update(zpad(c8), zpad(dm), w_ada[0], m_w_ada[0], v_w_ada[0])

    recv = _grad_exchange([_pieces_by_cols(_unpermute_w_in(big["w_in_p"])), _pieces_by_rows(big["w_out"])])
    names = ("w_in", "w_out", "w_gate", "w_up", "w_down")
    g_full = [_reduce_swap(r, "reduce_" + nm) for r, nm in zip(list(recv) + list(big["recv_ffn"]), names)]
    big_m = (m_w_in, m_w_out, m_w_gate, m_w_up, m_w_down)
    big_v = (v_w_in, v_w_out, v_w_gate, v_w_up, v_w_down)
    upd = [_adamw_big(g, w[0], m[0], v[0], "adamw_" + nm)
           for g, w, m, v, nm in zip(g_full, big_w, big_m, big_v, names)]
    bg = [g[None] for g in g_full]
    bd, bm, bv = [[u[i][None] for u in upd] for i in range(3)]

    def group(a_i, small_d, conv_i, big_l):
        s = lambda k, ref: small_d[k].reshape(ref.shape)
        return [ares[a_i][None], s("mod", b_ada), s("norm1_w", norm1_w), big_l[0], conv_out[conv_i],
                s("a_log", a_log), s("dt_bias", dt_bias), s("gdn_norm_w", gdn_norm_w), s("q_norm_w", q_norm_w),
                s("k_norm_w", k_norm_w), s("sinks", sinks), big_l[1], s("norm2_w", norm2_w), big_l[2], big_l[3],
                big_l[4]]

    outs = [loss_out, grad_x[None]]
    outs += group(0, s_g, 0, bg) + group(1, s_d, 1, bd) + group(2, s_m, 2, bm) + group(3, s_v, 3, bv)
    return tuple(outs)
```

```python
import functools

import jax
import jax.numpy as jnp
from jax import lax
from jax.experimental import pallas as pl
from jax.experimental.pallas import tpu as pltpu

F32 = jnp.float32
BF16 = jnp.bfloat16
MESH = pl.DeviceIdType.MESH

D = 1024
HD = 64
GH = 8
GW = GH * HD
SQH = 8
SKVH = 2
SGRP = SQH // SKVH
WIN = 128
CONVW = 4
CHUNK = 64
DFF = 2816
PROJ = 2832
NP = 3072
EPS = 1e-6
N_DEV = 8
N_CHIP = 4

ADAM_LR = 0.001
ADAM_B1 = 0.9
ADAM_B2 = 0.999
ADAM_EPS = 1e-08
ADAM_WD = 0.01
ADAM_STEP = 10

VMEM_LIMIT = 48 * 1024 * 1024
GDN_BWD_VMEM = 58 * 1024 * 1024
LANE = 128

PACK_ROWS = (PROJ // N_CHIP, D // N_CHIP, DFF // N_CHIP, DFF // N_CHIP, DFF // N_CHIP)
PACK_P = 3104
PACK_H = PACK_P // 2


def _cparams(sem=None, vmem=VMEM_LIMIT):
    return pltpu.CompilerParams(dimension_semantics=sem, vmem_limit_bytes=vmem)


_NN = ((1,), (0,))
_NT = ((1,), (1,))
_TN = ((0,), (0,))


def _dot(a, b, dims):
    if a.ndim == 3:
        (ca,), (cb,) = dims
        return lax.dot_general(a, b, (((ca + 1,), (cb + 1,)), ((0,), (0,))), preferred_element_type=F32)
    return lax.dot_general(a, b, (dims, ((), ())), preferred_element_type=F32)


def _raw1(a, b, dims):
    return _dot(a.astype(BF16), b.astype(BF16), dims)


def _raw3(a, b, dims):
    ah = a.astype(BF16)
    al = (a - ah.astype(F32)).astype(BF16)
    bh = b.astype(BF16)
    bl = (b - bh.astype(F32)).astype(BF16)
    return _dot(ah, bh, dims) + (_dot(al, bh, dims) + _dot(ah, bl, dims))


def _make_diff_mm(raw):
    @jax.custom_vjp
    def nn(a, b):
        return raw(a, b, _NN)

    @jax.custom_vjp
    def nt(a, b):
        return raw(a, b, _NT)

    @jax.custom_vjp
    def tn(a, b):
        return raw(a, b, _TN)

    nn.defvjp(lambda a, b: (raw(a, b, _NN), (a, b)), lambda r, g: (nt(g, r[1]), tn(r[0], g)))
    nt.defvjp(lambda a, b: (raw(a, b, _NT), (a, b)), lambda r, g: (nn(g, r[1]), tn(g, r[0])))
    tn.defvjp(lambda a, b: (raw(a, b, _TN), (a, b)), lambda r, g: (nt(r[1], g), nn(r[0], g)))
    return nn, nt, tn


def _tri_inv_raw(a, nn3):
    n = a.shape[-1]
    ri = lax.broadcasted_iota(jnp.int32, (n, n), 0)
    ci = lax.broadcasted_iota(jnp.int32, (n, n), 1)
    t = (ri == ci).astype(F32)
    for lvl in range((n - 1).bit_length()):
        same_pair = (ri >> (lvl + 1)) == (ci >> (lvl + 1))
        lower_left = (((ri >> lvl) & 1) == 1) & (((ci >> lvl) & 1) == 0)
        y = jnp.where(same_pair & lower_left, a, 0.0)
        t = t - y if lvl == 0 else t - nn3(nn3(t, y), t)
    return t


class _Kit:
    def __init__(self, diff):
        if diff:
            self.nn, self.nt, self.tn = _make_diff_mm(_raw1)
            self.nn3, self.nt3, self.tn3 = _make_diff_mm(_raw3)
            nn3, nt3, tn3 = self.nn3, self.nt3, self.tn3

            @jax.custom_vjp
            def inv(a, t):
                return t

            def inv_fwd(a, t):
                return t, t

            def inv_bwd(t, g):
                return -tn3(t, nt3(g, t)), jnp.zeros_like(t)

            inv.defvjp(inv_fwd, inv_bwd)
            self.inv = inv
        else:
            self.nn = lambda a, b: _raw1(a, b, _NN)
            self.nt = lambda a, b: _raw1(a, b, _NT)
            self.tn = lambda a, b: _raw1(a, b, _TN)
            self.nn3 = lambda a, b: _raw3(a, b, _NN)
            self.nt3 = lambda a, b: _raw3(a, b, _NT)
            self.tn3 = lambda a, b: _raw3(a, b, _TN)
            self.inv = lambda a, t: _tri_inv_raw(a, self.nn3) if t is None else t


def _sigmoid(x):
    return 1.0 / (1.0 + jnp.exp(-x))


def _silu(x):
    return x * _sigmoid(x)


def _rms(x, w):
    return x * lax.rsqrt(jnp.mean(x * x, axis=-1, keepdims=True) + EPS) * w


def _tile(dim, target):
    t = (min(dim, target) // LANE) * LANE
    while t >= LANE:
        if dim % t == 0:
            return t
        t -= LANE
    return dim


MM_TM, MM_TN, MM_TK = 1408, 1536, 1408


def _matmul(a, b, ta=False, tb=False, out_dtype=F32, name="matmul"):
    if ta:
        K, M = a.shape
    else:
        M, K = a.shape
    if tb:
        N, K2 = b.shape
    else:
        K2, N = b.shape
    assert K == K2, (a.shape, b.shape, ta, tb)
    tm, tn, tk = _tile(M, MM_TM), _tile(N, MM_TN), _tile(K, MM_TK)
    nk = K // tk
    dims = ((0,) if ta else (1,), (1,) if tb else (0,))

    def body(a_ref, b_ref, o_ref, *scratch):
        k = pl.program_id(2)
        part = _dot(a_ref[...].astype(BF16), b_ref[...].astype(BF16), dims)
        if nk == 1:
            o_ref[...] = part.astype(o_ref.dtype)
            return
        (acc_ref,) = scratch

        @pl.when(k == 0)
        def _():
            acc_ref[...] = part

        @pl.when((k > 0) & (k < nk - 1))
        def _():
            acc_ref[...] += part

        @pl.when(k == nk - 1)
        def _():
            o_ref[...] = (acc_ref[...] + part).astype(o_ref.dtype)

    a_spec = (pl.BlockSpec((tk, tm), lambda i, j, k: (k, i)) if ta
              else pl.BlockSpec((tm, tk), lambda i, j, k: (i, k)))
    b_spec = (pl.BlockSpec((tn, tk), lambda i, j, k: (j, k)) if tb
              else pl.BlockSpec((tk, tn), lambda i, j, k: (k, j)))
    return pl.pallas_call(
        body, name=name,
        grid=(M // tm, N // tn, nk),
        in_specs=[a_spec, b_spec],
        out_specs=pl.BlockSpec((tm, tn), lambda i, j, k: (i, j)),
        out_shape=jax.ShapeDtypeStruct((M, N), out_dtype),
        scratch_shapes=[pltpu.VMEM((tm, tn), F32)] if nk > 1 else [],
        compiler_params=_cparams(("parallel", "parallel", "arbitrary")),
    )(a, b)


def _rowcall(fn, tiled, consts, out_tiled, out_acc, tm, name):
    T = tiled[0].shape[0]
    n_in = len(tiled) + len(consts)
    n_o = len(out_tiled)

    def body(*refs):
        vals = [r[...] for r in refs[:n_in]]
        outs = refs[n_in:]
        res = fn(*vals)
        for r, v in zip(outs[:n_o], res[:n_o]):
            r[...] = v.astype(r.dtype)
        if len(outs) > n_o:
            @pl.when(pl.program_id(0) == 0)
            def _():
                for r in outs[n_o:]:
                    r[...] = jnp.zeros_like(r)

            for r, v in zip(outs[n_o:], res[n_o:]):
                r[...] += v

    in_specs = [pl.BlockSpec((tm, a.shape[1]), lambda i: (i, 0)) for a in tiled]
    in_specs += [pl.BlockSpec(a.shape, lambda i, nd=a.ndim: (0,) * nd) for a in consts]
    out_specs = [pl.BlockSpec((tm, s.shape[1]), lambda i: (i, 0)) for s in out_tiled]
    out_specs += [pl.BlockSpec(s.shape, lambda i: (0, 0)) for s in out_acc]
    return pl.pallas_call(
        body, name=name, grid=(T // tm,),
        in_specs=in_specs, out_specs=out_specs,
        out_shape=list(out_tiled) + list(out_acc),
        compiler_params=_cparams(("arbitrary",)),
    )(*tiled, *consts)


def _sds(shape, dtype=F32):
    return jax.ShapeDtypeStruct(shape, dtype)


def _norm_mod(x, nw, scale, shift):
    return _rms(x, nw) * (1.0 + scale) + shift


def _norm_mod_fwd(x, nw, scale, shift):
    T = x.shape[0]
    (h,) = _rowcall(lambda *a: (_norm_mod(*a),), [x], [nw, scale, shift],
                    [_sds((T, D), BF16)], [], 512, "norm1_fwd")
    return h


def _norm_mod_bwd(x, dh, dres, nw, scale, shift):
    T = x.shape[0]

    def fn(x, dh, dres, nw, scale, shift):
        _, vjp = jax.vjp(_norm_mod, x, nw, scale, shift)
        dx, dnw, dsc, dsh = vjp(dh)
        return dx + dres, dnw, dsc, dsh

    return _rowcall(fn, [x, dh, dres], [nw, scale, shift], [_sds((T, D))],
                    [_sds((1, D))] * 3, 256, "norm1_bwd")


def _resid_norm(x, mixed, gate1, nw, scale, shift):
    x1 = x + gate1 * mixed
    return x1, _norm_mod(x1, nw, scale, shift)


def _resid_norm_fwd(x, mixed, gate1, nw, scale, shift):
    T = x.shape[0]
    return _rowcall(_resid_norm, [x, mixed], [gate1, nw, scale, shift],
                    [_sds((T, D)), _sds((T, D), BF16)], [], 512, "resid_norm2_fwd")


def _resid_norm_bwd(x, mixed, dy, dh2, gate1, nw, scale, shift):
    T = x.shape[0]

    def fn(x, mixed, dy, dh2, gate1, nw, scale, shift):
        _, vjp = jax.vjp(_resid_norm, x, mixed, gate1, nw, scale, shift)
        dx, dmixed, dg1, dnw, dsc, dsh = vjp((dy, dh2))
        return dx, dmixed, dg1, dnw, dsc, dsh

    return _rowcall(fn, [x, mixed, dy, dh2], [gate1, nw, scale, shift],
                    [_sds((T, D)), _sds((T, D), BF16)], [_sds((1, D))] * 4, 256, "resid_norm2_bwd")


def _ffn_act_fwd(ab):
    T = ab.shape[0]

    def fn(ab):
        a, b = ab[:, :DFF], ab[:, DFF:]
        return (_silu(a) * b,)

    (act,) = _rowcall(fn, [ab], [], [_sds((T, DFF), BF16)], [], 256, "ffn_act_fwd")
    return act


def _ffn_act_bwd(ab, dact):
    T = ab.shape[0]

    def fn(ab, dact):
        a, b = ab[:, :DFF], ab[:, DFF:]
        s = _sigmoid(a)
        da = dact * b * (s * (1.0 + a * (1.0 - s)))
        db = dact * (a * s)
        return (jnp.concatenate([da, db], axis=1),)

    (dab,) = _rowcall(fn, [ab, dact], [], [_sds((T, 2 * DFF), BF16)], [], 256, "ffn_act_bwd")
    return dab


def _loss_head(x1, ffn, target, gate2):
    T = x1.shape[0]

    def fn(x1, ffn, target, gate2):
        y = x1 + gate2 * ffn
        err = y - target
        loss = 0.5 * jnp.sum(jnp.sum(err * err, axis=1, keepdims=True), axis=0, keepdims=True) / D
        dy = err * (1.0 / D)
        dgate2 = jnp.sum(dy * ffn, axis=0, keepdims=True)
        return dy, gate2 * dy, dgate2, jnp.broadcast_to(loss, (1, LANE))

    return _rowcall(fn, [x1, ffn, target], [gate2], [_sds((T, D)), _sds((T, D), BF16)],
                    [_sds((1, D)), _sds((1, LANE))], 256, "loss_head")


def _round_bf16(x):
    return x.astype(BF16).astype(F32)


def _shift_down(x, s, rows):
    if s == 0:
        return x
    return jnp.where(rows >= s, pltpu.roll(x, s, 0), 0.0)


def _shift_up(x, s, rows, T):
    if s == 0:
        return x
    return jnp.where(rows < T - s, pltpu.roll(x, T - s, 0), 0.0)


def _conv_fwd(proj, conv_w):
    T = proj.shape[0]
    ncol = 3 * GW // LANE

    def body(x_ref, w_ref, o_ref):
        x = _round_bf16(x_ref[...])
        rows = lax.broadcasted_iota(jnp.int32, x.shape, 0)
        acc = jnp.zeros_like(x)
        for j in range(CONVW):
            acc = acc + _round_bf16(w_ref[pl.ds(j, 1), :]) * _shift_down(x, CONVW - 1 - j, rows)
        o_ref[0], o_ref[1] = _split_pair(_silu(acc))

    return pl.pallas_call(
        body, name="conv_fwd", grid=(ncol,),
        in_specs=[pl.BlockSpec((T, LANE), lambda j: (0, j)), pl.BlockSpec((CONVW, LANE), lambda j: (0, j))],
        out_specs=pl.BlockSpec((2, T, HD), lambda j: (j, 0, 0)),
        out_shape=_sds((3 * GH, T, HD)),
        compiler_params=_cparams(("parallel",)),
    )(proj, conv_w)


RELAYOUT_TM = 4096


def _split_pair(y):
    return y[:, :HD], pltpu.roll(y, HD, 1)[:, :HD]


def _merge_pair(a, b):
    return jnp.concatenate([a, b], axis=1)


def _split_heads(x, col_block0, nheads, name):
    T = x.shape[0]
    tm = _tile(T, RELAYOUT_TM)

    def body(x_ref, o_ref):
        a, b = _split_pair(x_ref[...])
        o_ref[0] = a
        o_ref[1] = b

    return pl.pallas_call(
        body, name=name, grid=(nheads // 2, T // tm),
        in_specs=[pl.BlockSpec((tm, LANE), lambda j, i: (i, col_block0 + j))],
        out_specs=pl.BlockSpec((2, tm, HD), lambda j, i: (j, i, 0)),
        out_shape=_sds((nheads, T, HD), x.dtype),
        compiler_params=_cparams(("parallel", "parallel")),
    )(x)


def _merge_heads(hm, out_dtype, name, into=None, col_block0=0, head0=0, nheads=None):
    T = hm.shape[1]
    nheads = hm.shape[0] if nheads is None else nheads
    tm = _tile(T, RELAYOUT_TM)

    def body(*refs):
        h_ref, o_ref = refs[0], refs[-1]
        o_ref[...] = _merge_pair(h_ref[0], h_ref[1]).astype(o_ref.dtype)

    in_specs = [pl.BlockSpec((2, tm, HD), lambda j, i: (head0 // 2 + j, i, 0))]
    args = [hm]
    if into is None:
        out_shape = _sds((T, HD * nheads), out_dtype)
        aliases = {}
    else:
        out_shape = _sds(into.shape, into.dtype)
        in_specs.append(pl.BlockSpec(memory_space=pl.ANY))
        args.append(into)
        aliases = {1: 0}
    return pl.pallas_call(
        body, name=name, grid=(nheads // 2, T // tm),
        in_specs=in_specs,
        out_specs=pl.BlockSpec((tm, LANE), lambda j, i: (i, col_block0 + j)),
        out_shape=out_shape, input_output_aliases=aliases,
        compiler_params=_cparams(("parallel", "parallel")),
    )(*args)


def _conv_bwd(proj, conv_w, dqc):
    T = proj.shape[0]
    ncol = 3 * GW // LANE

    def body(x_ref, w_ref, d_ref, dx_ref, dw_ref):
        x = _round_bf16(x_ref[...])
        rows = lax.broadcasted_iota(jnp.int32, x.shape, 0)
        xs = [_shift_down(x, CONVW - 1 - j, rows) for j in range(CONVW)]
        w = [_round_bf16(w_ref[pl.ds(j, 1), :]) for j in range(CONVW)]
        pre = jnp.zeros_like(x)
        for j in range(CONVW):
            pre = pre + w[j] * xs[j]
        s = _sigmoid(pre)
        dpre = _round_bf16(_merge_pair(d_ref[0], d_ref[1]) * (s * (1.0 + pre * (1.0 - s))))
        dx = jnp.zeros_like(x)
        for j in range(CONVW):
            dx = dx + w[j] * _shift_up(dpre, CONVW - 1 - j, rows, T)
            dw_ref[pl.ds(j, 1), :] = jnp.sum(dpre * xs[j], axis=0, keepdims=True)
        dx_ref[...] = dx.astype(dx_ref.dtype)

    return pl.pallas_call(
        body, name="conv_bwd", grid=(ncol,),
        in_specs=[pl.BlockSpec((T, LANE), lambda j: (0, j)), pl.BlockSpec((CONVW, LANE), lambda j: (0, j)),
                  pl.BlockSpec((2, T, HD), lambda j: (j, 0, 0))],
        out_specs=[pl.BlockSpec((T, LANE), lambda j: (0, j)), pl.BlockSpec((CONVW, LANE), lambda j: (0, j))],
        out_shape=[_sds((T, NP), BF16), _sds((CONVW, 3 * GW))],
        compiler_params=_cparams(("parallel",)),
    )(proj, conv_w, dqc)


def _gdn_prep(kit, q, k, v, ga, gb, alog, dtb, t_inv=None):
    C = CHUNK
    ri = lax.broadcasted_iota(jnp.int32, (C, C), 0)
    ci = lax.broadcasted_iota(jnp.int32, (C, C), 1)
    causal = ri >= ci
    strict = ri > ci
    eye = (ri == ci).astype(F32)
    lower = causal.astype(F32)
    upper = (ri <= ci).astype(F32)

    a = ga + dtb
    softplus = jnp.maximum(a, 0.0) + jnp.log(1.0 + jnp.exp(-jnp.abs(a)))
    g_row = -jnp.exp(alog) * softplus
    beta_row = _sigmoid(gb)
    g_col = jnp.sum(eye * g_row, axis=2, keepdims=True)
    beta_col = jnp.sum(eye * beta_row, axis=2, keepdims=True)
    G_col = jnp.sum(lower * g_row, axis=2, keepdims=True)
    G_row = jnp.sum(upper * g_col, axis=1, keepdims=True)
    G_last = jnp.sum(g_row, axis=2, keepdims=True)
    decay = jnp.exp(jnp.where(causal, G_col - G_row, -1e30))

    qn = q * lax.rsqrt(jnp.sum(q * q, axis=-1, keepdims=True) + EPS) * (HD ** -0.5)
    kn = k * lax.rsqrt(jnp.sum(k * k, axis=-1, keepdims=True) + EPS)
    kb = kn * beta_col
    A = jnp.where(strict, kit.nt(kb, kn) * decay, 0.0)
    Tm = kit.inv(A, t_inv)
    eG = jnp.exp(G_col)
    u = kit.nn3(Tm, v * beta_col)
    w = kit.nn3(Tm, kb * eG)
    qk = jnp.where(causal, kit.nt(qn, kn) * decay, 0.0)
    q_dec = qn * eG
    k_dec = kn * jnp.exp(G_last - G_col)
    dec = jnp.exp(G_last)
    return u, w, qk, q_dec, k_dec, dec, Tm


def _gdn_out(o, z, nw):
    return _rms(o, nw) * _silu(z)


GDN_CB = 4


def _gdn_specs(T, blk):
    TB = GDN_CB * CHUNK
    seq = lambda grp: pl.BlockSpec((GH, TB, HD), lambda i, grp=grp: (grp, blk(i), 0))
    row = lambda grp: pl.BlockSpec((GH, GDN_CB, 1, CHUNK), lambda i, grp=grp: (grp, blk(i), 0, 0))
    per_head = pl.BlockSpec((GH, 1, CHUNK), lambda i: (0, 0, 0))
    whole = pl.BlockSpec((1, HD), lambda i: (0, 0))
    state = pl.BlockSpec((GH, GDN_CB, HD, HD), lambda i: (0, blk(i), 0, 0))
    return seq, row, per_head, whole, state


def _gdn_load(seq_refs, row_refs, head_refs):
    chunks = lambda r: jnp.concatenate([r[:, pl.ds(cb * CHUNK, CHUNK), :] for cb in range(GDN_CB)], axis=0)
    rows = lambda r: jnp.concatenate([r[:, cb] for cb in range(GDN_CB)], axis=0)
    heads = lambda r: jnp.concatenate([r[...]] * GDN_CB, axis=0)
    return [chunks(r) for r in seq_refs], [rows(r) for r in row_refs], [heads(r) for r in head_refs]


def _gdn_fwd(qkv_hm, zs_hm, gab, alog_b, dtb_b, nw, shards):
    T = qkv_hm.shape[1]
    N = T // CHUNK
    nblk = N // GDN_CB
    ns = len(shards)
    seq, row, per_head, whole, state = _gdn_specs(T, lambda i: i)
    kit = _Kit(False)

    def body(*refs):
        q_ref, k_ref, v_ref, z_ref, ga_ref, gb_ref, al_ref, dt_ref, nw_ref = refs[:9]
        o_ref, S_ref, T_ref = refs[9 + ns:12 + ns]
        S_scr = refs[12 + 2 * ns]
        plan = _gather_plan(refs[9:9 + ns], refs[12 + ns:12 + 2 * ns], *refs[13 + 2 * ns:])

        @pl.when(pl.program_id(0) == 0)
        def _():
            S_scr[...] = jnp.zeros_like(S_scr)
            _start(plan)

        (q, k, v, z), (ga, gb), (al, dt) = _gdn_load((q_ref, k_ref, v_ref, z_ref), (ga_ref, gb_ref), (al_ref, dt_ref))
        u, w, qk, q_dec, k_dec, dec, t_inv = _gdn_prep(kit, q, k, v, ga, gb, al, dt)
        S = S_scr[...]
        for cb in range(GDN_CB):
            hs = slice(cb * GH, (cb + 1) * GH)
            S_ref[:, cb] = S
            T_ref[:, cb] = t_inv[hs]
            v_new = u[hs] - kit.nn(w[hs], S)
            o = kit.nn(q_dec[hs], S) + kit.nn(qk[hs], v_new)
            S = S * dec[hs] + kit.tn(k_dec[hs], v_new)
            o_ref[:, pl.ds(cb * CHUNK, CHUNK), :] = _gdn_out(o, z[hs], nw_ref[...])
        S_scr[...] = S

        @pl.when(pl.program_id(0) == nblk - 1)
        def _():
            _finish(plan)

    res = pl.pallas_call(
        body, name="gdn_fwd", grid=(nblk,),
        in_specs=[seq(0), seq(1), seq(2), seq(0), row(0), row(1), per_head, per_head, whole] + _hbm_specs(ns),
        out_specs=[seq(0), state, state] + _hbm_specs(ns),
        out_shape=[_sds((GH + SQH, T, HD)), _sds((GH, N, HD, HD)), _sds((GH, N, CHUNK, CHUNK))]
                  + _gather_shapes(shards),
        scratch_shapes=[pltpu.VMEM((GH, HD, HD), F32)] + _gather_sems(ns),
        compiler_params=_cparams(("arbitrary",)),
    )(qkv_hm, qkv_hm, qkv_hm, zs_hm, gab, gab, alog_b, dtb_b, nw, *shards)
    return res[0], (res[1], res[2]), res[3:]


def _gdn_bwd(qkv_hm, zs_hm, gab, alog_b, dtb_b, nw, S_all, do, pieces):
    T = qkv_hm.shape[1]
    N = T // CHUNK
    nblk = N // GDN_CB
    npc = len(pieces)
    dkit, kit = _Kit(True), _Kit(False)
    rseq, rrow, per_head, whole, rstate = _gdn_specs(T, lambda i: nblk - 1 - i)

    def body(*refs):
        q_ref, k_ref, v_ref, z_ref, ga_ref, gb_ref, al_ref, dt_ref, nw_ref, S_ref, T_ref, do_ref = refs[:12]
        dqkv_ref, dz_ref, dga_ref, dgb_ref, dal_ref, ddt_ref, dnw_ref = refs[12 + npc:19 + npc]
        dS_scr = refs[19 + 2 * npc]
        plan = _exchange_plan(refs[12:12 + npc], refs[19 + npc:19 + 2 * npc], *refs[20 + 2 * npc:])

        @pl.when(pl.program_id(0) == 0)
        def _():
            dS_scr[...] = jnp.zeros_like(dS_scr)
            dal_ref[...] = jnp.zeros_like(dal_ref)
            ddt_ref[...] = jnp.zeros_like(ddt_ref)
            dnw_ref[...] = jnp.zeros_like(dnw_ref)
            _start(plan)

        (q, k, v, z, dout), (ga, gb), (al, dt) = _gdn_load((q_ref, k_ref, v_ref, z_ref, do_ref), (ga_ref, gb_ref),
                                                          (al_ref, dt_ref))
        S_in = jnp.concatenate([S_ref[:, cb] for cb in range(GDN_CB)], axis=0)
        t_inv = jnp.concatenate([T_ref[:, cb] for cb in range(GDN_CB)], axis=0)
        prep = lambda *a: _gdn_prep(dkit, *a, t_inv=t_inv)[:6]
        (u, w, qk, q_dec, k_dec, dec), prep_vjp = jax.vjp(prep, q, k, v, ga, gb, al, dt)
        v_new = u - kit.nn(w, S_in)
        o = kit.nn(q_dec, S_in) + kit.nn(qk, v_new)
        _, out_vjp = jax.vjp(_gdn_out, o, z, nw_ref[...])
        do, dz, dnw = out_vjp(dout)
        dvn_part = kit.tn(qk, do)
        dS_part = kit.tn(q_dec, do)
        dS = dS_scr[...]
        dS_out, dvn = [None] * GDN_CB, [None] * GDN_CB
        for cb in reversed(range(GDN_CB)):
            hs = slice(cb * GH, (cb + 1) * GH)
            dS_out[cb] = dS
            dvn[cb] = dvn_part[hs] + kit.nn(k_dec[hs], dS)
            dS = dS * dec[hs] + dS_part[hs] - kit.tn(w[hs], dvn[cb])
        dS_scr[...] = dS
        dS_out = jnp.concatenate(dS_out, axis=0)
        dvn = jnp.concatenate(dvn, axis=0)
        ddec = jnp.sum(jnp.sum(S_in * dS_out, axis=2, keepdims=True), axis=1, keepdims=True)
        cts = (dvn, -kit.nt(dvn, S_in), kit.nt(do, v_new), kit.nt(do, S_in), kit.nt(v_new, dS_out), ddec)
        dq, dk, dv, dga, dgb, dal, ddt = prep_vjp(cts)
        lanesum = lambda t: jnp.broadcast_to(jnp.sum(t, axis=2, keepdims=True), t.shape)
        for cb in range(GDN_CB):
            hs = slice(cb * GH, (cb + 1) * GH)
            sl = pl.ds(cb * CHUNK, CHUNK)
            dqkv_ref[pl.ds(0, GH), sl, :] = dq[hs]
            dqkv_ref[pl.ds(GH, GH), sl, :] = dk[hs]
            dqkv_ref[pl.ds(2 * GH, GH), sl, :] = dv[hs]
            dz_ref[:, sl, :] = dz[hs]
            dga_ref[:, cb] = dga[hs]
            dgb_ref[:, cb] = dgb[hs]
            dal_ref[...] += lanesum(dal[hs])
            ddt_ref[...] += lanesum(ddt[hs])
        dnw_ref[...] += dnw

        @pl.when(pl.program_id(0) == nblk - 1)
        def _():
            _finish(plan)

    res = pl.pallas_call(
        body, name="gdn_bwd", grid=(nblk,),
        in_specs=[rseq(0), rseq(1), rseq(2), rseq(0), rrow(0), rrow(1), per_head, per_head, whole, rstate, rstate,
                  rseq(0)] + _hbm_specs(npc),
        out_specs=[pl.BlockSpec((3 * GH, GDN_CB * CHUNK, HD), lambda i: (0, nblk - 1 - i, 0)), rseq(0), rrow(0),
                   rrow(0), per_head, per_head, whole] + _hbm_specs(npc),
        out_shape=[_sds((3 * GH, T, HD)), _sds((GH + 4 + SWA_GRAD_HEADS, T, HD))] + [_sds((GH, N, 1, CHUNK))] * 2
                  + [_sds((GH, 1, CHUNK))] * 2 + [_sds((1, HD))] + _exchange_shapes(pieces),
        scratch_shapes=[pltpu.VMEM((GH, HD, HD), F32)] + _exchange_sems(npc),
        compiler_params=_cparams(("arbitrary",), GDN_BWD_VMEM),
    )(qkv_hm, qkv_hm, qkv_hm, zs_hm, gab, gab, alog_b, dtb_b, nw, S_all[0], S_all[1], do, *pieces)
    return res[:7], res[7:]


def _swa_block(kit, first, q0, q1, q2, q3, kp, kc, vp, vc, qnw, knw, s0, s1, s2, s3, *, slopes):
    W = WIN
    ri = lax.broadcasted_iota(jnp.int32, (W, W), 0)
    ci = lax.broadcasted_iota(jnp.int32, (W, W), 1)
    mask_c = ri >= ci
    mask_p = ci > ri + first * W
    dist_c = (ri - ci).astype(F32)
    dist_p = (ri - ci + W).astype(F32)
    kpn = _rms(kp, knw)
    kcn = _rms(kc, knw)
    outs = []
    for q, sink, slope in zip((q0, q1, q2, q3), (s0, s1, s2, s3), slopes):
        qn = _rms(q, qnw)
        sc = jnp.where(mask_c, kit.nt(qn, kcn) * (HD ** -0.5) - slope * dist_c, -1e30)
        sp = jnp.where(mask_p, kit.nt(qn, kpn) * (HD ** -0.5) - slope * dist_p, -1e30)
        m = jnp.maximum(jnp.maximum(jnp.max(sc, axis=-1, keepdims=True), jnp.max(sp, axis=-1, keepdims=True)), sink)
        m = lax.stop_gradient(m)
        pc = jnp.exp(sc - m)
        pp = jnp.exp(sp - m)
        den = jnp.sum(pc, axis=-1, keepdims=True) + jnp.sum(pp, axis=-1, keepdims=True) + jnp.exp(sink - m)
        inv = 1.0 / den
        outs.append(kit.nn(pc * inv, vc) + kit.nn(pp * inv, vp))
    return tuple(outs)


def _swa_slopes(hk):
    return tuple(jnp.where(hk == 0, 2.0 ** (-8.0 * (g + 1.0) / SQH), 2.0 ** (-8.0 * (SGRP + g + 1.0) / SQH))
                 for g in range(SGRP))


def _swa_fwd(zs_hm, qnw, knw, sinks_col):
    T = zs_hm.shape[1]
    NB = T // WIN
    kit = _Kit(False)

    def body(q_ref, kp_ref, kc_ref, vp_ref, vc_ref, qnw_ref, knw_ref, s_ref, o_ref):
        hk = pl.program_id(0)
        first = (pl.program_id(1) == 0).astype(jnp.int32)
        args = ([q_ref[g] for g in range(SGRP)] + [kp_ref[...], kc_ref[...], vp_ref[...], vc_ref[...],
                                                     qnw_ref[...], knw_ref[...]] + [s_ref[g] for g in range(SGRP)])
        outs = _swa_block(kit, first, *args, slopes=_swa_slopes(hk))
        for g in range(SGRP):
            o_ref[g] = outs[g]

    qspec = pl.BlockSpec((SGRP, WIN, HD), lambda hk, n: (2 + hk, n, 0))
    cur = lambda off: pl.BlockSpec((None, WIN, HD), lambda hk, n, off=off: (off + hk, n, 0))
    prev = lambda off: pl.BlockSpec((None, WIN, HD), lambda hk, n, off=off: (off + hk, jnp.maximum(n - 1, 0), 0))
    whole = pl.BlockSpec((1, HD), lambda hk, n: (0, 0))
    sspec = pl.BlockSpec((SGRP, WIN, 1), lambda hk, n: (hk, 0, 0))
    return pl.pallas_call(
        body, name="swa_fwd", grid=(SKVH, NB),
        in_specs=[qspec, prev(16), cur(16), prev(18), cur(18), whole, whole, sspec],
        out_specs=pl.BlockSpec((SGRP, WIN, HD), lambda hk, n: (hk, n, 0)),
        out_shape=_sds((SQH, T, HD)),
        compiler_params=_cparams(("parallel", "arbitrary")),
    )(zs_hm, zs_hm, zs_hm, zs_hm, zs_hm, qnw, knw, sinks_col)


def _swa_bwd(zs_hm, qnw, knw, sinks_col, do):
    T = zs_hm.shape[1]
    NB = T // WIN
    kit = _Kit(True)

    def body(q_ref, kp_ref, kc_ref, vp_ref, vc_ref, qnw_ref, knw_ref, s_ref, do_ref,
             dq_ref, dk_ref, dv_ref, dqnw_ref, dknw_ref, ds_ref, ck_scr, cv_scr):
        hk = pl.program_id(0)
        i = pl.program_id(1)
        first = (i == NB - 1).astype(jnp.int32)

        @pl.when(i == 0)
        def _():
            ck_scr[...] = jnp.zeros_like(ck_scr)
            cv_scr[...] = jnp.zeros_like(cv_scr)
            ds_ref[...] = jnp.zeros_like(ds_ref)

        @pl.when((i == 0) & (hk == 0))
        def _():
            dqnw_ref[...] = jnp.zeros_like(dqnw_ref)
            dknw_ref[...] = jnp.zeros_like(dknw_ref)

        args = ([q_ref[g] for g in range(SGRP)] + [kp_ref[...], kc_ref[...], vp_ref[...], vc_ref[...],
                                                     qnw_ref[...], knw_ref[...]] + [s_ref[g] for g in range(SGRP)])
        dos = tuple(do_ref[g] for g in range(SGRP))
        _, vjp = jax.vjp(functools.partial(_swa_block, kit, first, slopes=_swa_slopes(hk)), *args)
        gr = vjp(dos)
        for g in range(SGRP):
            dq_ref[g] = gr[g]
            ds_ref[g] += jnp.broadcast_to(jnp.sum(gr[10 + g], axis=0, keepdims=True), (WIN, 1))
        dkp, dkc, dvp, dvc = gr[4:8]
        dk_ref[...] = dkc + ck_scr[...]
        dv_ref[...] = dvc + cv_scr[...]
        ck_scr[...] = dkp
        cv_scr[...] = dvp
        dqnw_ref[...] += gr[8]
        dknw_ref[...] += gr[9]

    rn = lambda n: NB - 1 - n
    qspec = pl.BlockSpec((SGRP, WIN, HD), lambda hk, i: (2 + hk, rn(i), 0))
    cur = lambda off: pl.BlockSpec((None, WIN, HD), lambda hk, i, off=off: (off + hk, rn(i), 0))
    prev = lambda off: pl.BlockSpec((None, WIN, HD), lambda hk, i, off=off: (off + hk, jnp.maximum(rn(i) - 1, 0), 0))
    whole = pl.BlockSpec((1, HD), lambda hk, i: (0, 0))
    sspec = pl.BlockSpec((SGRP, WIN, 1), lambda hk, i: (hk, 0, 0))
    ospec = pl.BlockSpec((SGRP, WIN, HD), lambda hk, i: (hk, rn(i), 0))
    return pl.pallas_call(
        body, name="swa_bwd", grid=(SKVH, NB),
        in_specs=[qspec, prev(16), cur(16), prev(18), cur(18), whole, whole, sspec, ospec],
        out_specs=[ospec, cur(0), cur(0), whole, whole, sspec],
        out_shape=[_sds((SQH, T, HD)), _sds((SKVH, T, HD)), _sds((SKVH, T, HD)),
                   _sds((1, HD)), _sds((1, HD)), _sds((SQH, WIN, 1))],
        scratch_shapes=[pltpu.VMEM((WIN, HD), F32), pltpu.VMEM((WIN, HD), F32)],
        compiler_params=_cparams(("arbitrary", "arbitrary")),
    )(zs_hm, zs_hm, zs_hm, zs_hm, zs_hm, qnw, knw, sinks_col, do)


def _swa_heads(kit, first, q, kp, kc, vp, vc, qnw, knw, sink, slope):
    W = WIN
    ri = lax.broadcasted_iota(jnp.int32, (W, W), 0)
    ci = lax.broadcasted_iota(jnp.int32, (W, W), 1)
    mask_c = ri >= ci
    mask_p = ci > ri + first * W
    dist_c = (ri - ci).astype(F32)
    dist_p = (ri - ci + W).astype(F32)
    kpn = _rms(kp, knw)
    kcn = _rms(kc, knw)
    qn = _rms(q, qnw)
    sc = jnp.where(mask_c, kit.nt(qn, kcn) * (HD ** -0.5) - slope * dist_c, -1e30)
    sp = jnp.where(mask_p, kit.nt(qn, kpn) * (HD ** -0.5) - slope * dist_p, -1e30)
    m = jnp.maximum(jnp.maximum(jnp.max(sc, axis=-1, keepdims=True), jnp.max(sp, axis=-1, keepdims=True)), sink)
    m = lax.stop_gradient(m)
    pc = jnp.exp(sc - m)
    pp = jnp.exp(sp - m)
    den = jnp.sum(pc, axis=-1, keepdims=True) + jnp.sum(pp, axis=-1, keepdims=True) + jnp.exp(sink - m)
    inv = 1.0 / den
    return kit.nn(pc * inv, vc) + kit.nn(pp * inv, vp)


def _per_query_head(kv_ref):
    return jnp.concatenate([kv_ref[pl.ds(h // SGRP, 1)] for h in range(SQH)], axis=0)


def _per_kv_head(d):
    return jnp.concatenate([jnp.sum(d[g * SGRP:(g + 1) * SGRP], axis=0, keepdims=True) for g in range(SKVH)], axis=0)


def _swa_specs(blk):
    qspec = pl.BlockSpec((SQH, WIN, HD), lambda i: (1, blk(i), 0))
    cur = lambda grp: pl.BlockSpec((SKVH, WIN, HD), lambda i, grp=grp: (grp, blk(i), 0))
    prev = lambda grp: pl.BlockSpec((SKVH, WIN, HD), lambda i, grp=grp: (grp, jnp.maximum(blk(i) - 1, 0), 0))
    whole = pl.BlockSpec((1, HD), lambda i: (0, 0))
    col = pl.BlockSpec((SQH, WIN, 1), lambda i: (0, 0, 0))
    ospec = pl.BlockSpec((SQH, WIN, HD), lambda i: (0, blk(i), 0))
    return qspec, cur, prev, whole, col, ospec


def _swa_fwd(zs_hm, qnw, knw, sinks_col, slopes_col, o_buf):
    T = zs_hm.shape[1]
    kit = _Kit(False)
    qspec, cur, prev, whole, col, _ = _swa_specs(lambda i: i)

    def body(q_ref, kp_ref, kc_ref, vp_ref, vc_ref, qnw_ref, knw_ref, s_ref, sl_ref, buf_ref, o_ref):
        first = (pl.program_id(0) == 0).astype(jnp.int32)
        o_ref[...] = _swa_heads(kit, first, q_ref[...], _per_query_head(kp_ref), _per_query_head(kc_ref),
                                _per_query_head(vp_ref), _per_query_head(vc_ref), qnw_ref[...], knw_ref[...],
                                s_ref[...], sl_ref[...])

    return pl.pallas_call(
        body, name="swa_fwd", grid=(T // WIN,),
        in_specs=[qspec, prev(8), cur(8), prev(9), cur(9), whole, whole, col, col] + _hbm_specs(1),
        out_specs=pl.BlockSpec((SQH, WIN, HD), lambda i: (1, i, 0)), out_shape=_sds(o_buf.shape),
        input_output_aliases={9: 0},
        compiler_params=_cparams(("arbitrary",)),
    )(zs_hm, zs_hm, zs_hm, zs_hm, zs_hm, qnw, knw, sinks_col, slopes_col, o_buf)


SWA_GRAD_HEADS = SQH + 2 * SKVH


def _swa_bwd(zs_hm, qnw, knw, sinks_col, slopes_col, dmix_hm, d_buf):
    T = zs_hm.shape[1]
    NB = T // WIN
    kit = _Kit(True)
    qspec, cur, prev, whole, col, _ = _swa_specs(lambda i: NB - 1 - i)

    def body(q_ref, kp_ref, kc_ref, vp_ref, vc_ref, qnw_ref, knw_ref, s_ref, sl_ref, do_ref, buf_ref,
             d_ref, dqnw_ref, dknw_ref, ds_ref, ck_scr, cv_scr):
        dq_ref = d_ref.at[pl.ds(0, SQH)]
        dk_ref = d_ref.at[pl.ds(SQH, SKVH)]
        dv_ref = d_ref.at[pl.ds(SQH + SKVH, SKVH)]
        i = pl.program_id(0)
        first = (i == NB - 1).astype(jnp.int32)

        @pl.when(i == 0)
        def _():
            ck_scr[...] = jnp.zeros_like(ck_scr)
            cv_scr[...] = jnp.zeros_like(cv_scr)
            ds_ref[...] = jnp.zeros_like(ds_ref)
            dqnw_ref[...] = jnp.zeros_like(dqnw_ref)
            dknw_ref[...] = jnp.zeros_like(dknw_ref)

        fn = lambda q, kp, kc, vp, vc, qnw, knw, sink: _swa_heads(kit, first, q, kp, kc, vp, vc, qnw, knw, sink,
                                                                  sl_ref[...])
        _, vjp = jax.vjp(fn, q_ref[...], _per_query_head(kp_ref), _per_query_head(kc_ref), _per_query_head(vp_ref),
                         _per_query_head(vc_ref), qnw_ref[...], knw_ref[...], s_ref[...])
        dq, dkp, dkc, dvp, dvc, dqnw, dknw, dsink = vjp(do_ref[...])
        dq_ref[...] = dq
        dk_ref[...] = _per_kv_head(dkc) + ck_scr[...]
        dv_ref[...] = _per_kv_head(dvc) + cv_scr[...]
        ck_scr[...] = _per_kv_head(dkp)
        cv_scr[...] = _per_kv_head(dvp)
        dqnw_ref[...] += dqnw
        dknw_ref[...] += dknw
        ds_ref[...] += jnp.broadcast_to(jnp.sum(dsink, axis=1, keepdims=True), dsink.shape)

    dospec = pl.BlockSpec((SQH, WIN, HD), lambda i: (1, NB - 1 - i, 0))
    dspec = pl.BlockSpec((SWA_GRAD_HEADS, WIN, HD), lambda i: (1, NB - 1 - i, 0))
    res = pl.pallas_call(
        body, name="swa_bwd", grid=(NB,),
        in_specs=[qspec, prev(8), cur(8), prev(9), cur(9), whole, whole, col, col, dospec] + _hbm_specs(1),
        out_specs=[dspec, whole, whole, col],
        out_shape=[_sds(d_buf.shape), _sds((1, HD)), _sds((1, HD)), _sds((SQH, WIN, 1))],
        input_output_aliases={10: 0},
        scratch_shapes=[pltpu.VMEM((SKVH, WIN, HD), F32), pltpu.VMEM((SKVH, WIN, HD), F32)],
        compiler_params=_cparams(("arbitrary",)),
    )(zs_hm, zs_hm, zs_hm, zs_hm, zs_hm, qnw, knw, sinks_col, slopes_col, dmix_hm, d_buf)
    return res


GAB0 = 3 * GW + 1280


def _permute_w_in(w_in):
    return jnp.concatenate([w_in[:, :4 * GW], w_in[:, 4 * GW + 2 * GH:], w_in[:, 4 * GW:4 * GW + 2 * GH],
                            jnp.zeros((D, NP - PROJ), w_in.dtype)], axis=1)


def _unpermute_w_in(g):
    return jnp.concatenate([g[:, :4 * GW], g[:, GAB0:GAB0 + 2 * GH], g[:, 4 * GW:GAB0]], axis=1)


def _pieces_by_cols(g):
    return g.reshape(D, N_CHIP, -1).transpose(1, 0, 2).reshape(N_CHIP, 2, D // 2, -1)


def _pieces_by_rows(g):
    return g.reshape(N_CHIP, 2, g.shape[0] // (2 * N_CHIP), D)


def _local_step(x, target, mod, n1w, w_in_p, conv_w, alog, dtb, gnw, qnw, knw, sinks, w_out, n2w, ffn_shards):
    T = x.shape[0]
    N = T // CHUNK
    shift1, scale1, gate1, shift2, scale2, gate2 = [mod[:, i * D:(i + 1) * D] for i in range(6)]

    h = _norm_mod_fwd(x, n1w, scale1, shift1)
    proj = _matmul(h, w_in_p, name="in_proj")
    qkv_hm = _conv_fwd(proj, conv_w)
    zs_hm = _split_heads(proj, 3 * GW // LANE, 20, "split_zs")
    gab = proj[:, GAB0:GAB0 + 2 * GH].T.reshape(2 * GH, N, 1, CHUNK)
    alog_b = jnp.broadcast_to(alog.reshape(GH, 1, 1), (GH, 1, CHUNK))
    dtb_b = jnp.broadcast_to(dtb.reshape(GH, 1, 1), (GH, 1, CHUNK))
    sinks_col = jnp.broadcast_to(sinks.reshape(SQH, 1, 1), (SQH, WIN, 1))
    o_hm, S_all, (a_gate, a_up, a_down) = _gdn_fwd(qkv_hm, zs_hm, gab, alog_b, dtb_b, gnw, ffn_shards)
    w_gu = jnp.concatenate([a_gate[j] for j in range(N_CHIP)] + [a_up[j] for j in range(N_CHIP)], axis=1)
    w_down = a_down.reshape(DFF, D)
    slopes = 2.0 ** (-8.0 * (jnp.arange(SQH, dtype=F32) + 1.0) / SQH)
    slopes_col = jnp.broadcast_to(slopes.reshape(SQH, 1, 1), (SQH, WIN, 1))
    o_hm = _swa_fwd(zs_hm, qnw, knw, sinks_col, slopes_col, o_hm)
    mixcat = _merge_heads(o_hm, BF16, "merge_mix")
    mixed = _matmul(mixcat, w_out, name="out_proj")
    x1, h2 = _resid_norm_fwd(x, mixed, gate1, n2w, scale2, shift2)
    ab = _matmul(h2, w_gu, name="ffn_up")
    act = _ffn_act_fwd(ab)
    ffn = _matmul(act, w_down, name="ffn_down")
    dy, dffn, dgate2, loss = _loss_head(x1, ffn, target, gate2)

    dact = _matmul(dffn, w_down, tb=True, name="ffn_down_dx")
    dab = _ffn_act_bwd(ab, dact)
    g_w_down = _matmul(act, dffn, ta=True, out_dtype=BF16, name="ffn_down_dw")
    g_w_gu = _matmul(h2, dab, ta=True, out_dtype=BF16, name="ffn_up_dw")
    dh2 = _matmul(dab, w_gu, tb=True, name="ffn_up_dx")
    dx1, dmixed, dgate1, dn2w, dscale2, dshift2 = _resid_norm_bwd(x, mixed, dy, dh2, gate1, n2w, scale2, shift2)
    g_w_out = _matmul(mixcat, dmixed, ta=True, out_dtype=BF16, name="out_proj_dw")
    dmix_hm = _split_heads(_matmul(dmixed, w_out, tb=True, name="out_proj_dx"), 0, GH + SQH, "split_dmix")
    gu_pieces = g_w_gu.reshape(D, 2 * N_CHIP, -1).transpose(1, 0, 2).reshape(2, N_CHIP, 2, D // 2, -1)
    ffn_pieces = [gu_pieces[0], gu_pieces[1], _pieces_by_rows(g_w_down)]
    (dqkv_hm, d_hm, dga, dgb, dalog, ddtb, dgnw), recv_ffn = _gdn_bwd(qkv_hm, zs_hm, gab, alog_b, dtb_b, gnw, S_all,
                                                                      dmix_hm, ffn_pieces)
    d_hm, dqnw, dknw, dsinks = _swa_bwd(zs_hm, qnw, knw, sinks_col, slopes_col, dmix_hm, d_hm)
    dproj, dconv = _conv_bwd(proj, conv_w, dqkv_hm)
    dproj = _merge_heads(d_hm, BF16, "merge_dz", into=dproj, col_block0=3 * GW // LANE, head0=0, nheads=GH)
    dproj = _merge_heads(d_hm, BF16, "merge_dswa", into=dproj, col_block0=4 * GW // LANE, head0=GH + 4,
                         nheads=SWA_GRAD_HEADS)
    dgab = jnp.concatenate([dga, dgb], axis=0).reshape(2 * GH, T).T.astype(BF16)
    dproj = lax.dynamic_update_slice(dproj, jnp.concatenate([dgab, jnp.zeros((T, NP - PROJ), BF16)], axis=1),
                                     (0, GAB0))
    g_w_in_p = _matmul(h, dproj, ta=True, out_dtype=BF16, name="in_proj_dw")
    dh = _matmul(dproj, w_in_p, tb=True, name="in_proj_dx")
    grad_x, dn1w, dscale1, dshift1 = _norm_mod_bwd(x, dh, dx1, n1w, scale1, shift1)

    dmod = jnp.concatenate([dshift1, dscale1, dgate1, dshift2, dscale2, dgate2], axis=1)
    big = dict(w_in_p=g_w_in_p, w_out=g_w_out, recv_ffn=recv_ffn)
    small = dict(mod=dmod, norm1_w=dn1w, norm2_w=dn2w, conv_w=dconv, a_log=dalog[:, 0, 0], dt_bias=ddtb[:, 0, 0],
                 gdn_norm_w=dgnw, q_norm_w=dqnw, k_norm_w=dknw, sinks=dsinks[:, 0, 0])
    return loss, grad_x, big, small


def _adamw(w, g, m, v):
    m2 = ADAM_B1 * m + (1.0 - ADAM_B1) * g
    v2 = ADAM_B2 * v + (1.0 - ADAM_B2) * (g * g)
    m_hat = m2 / (1.0 - ADAM_B1 ** ADAM_STEP)
    v_hat = v2 / (1.0 - ADAM_B2 ** ADAM_STEP)
    delta = -ADAM_LR * (m_hat / (jnp.sqrt(v_hat) + ADAM_EPS) + ADAM_WD * w)
    return delta, m2, v2


def _reduce_adamw(recv, w, m, v, name):
    _, R, C = recv.shape
    tc = _tile(C, 256)

    def body(r_ref, w_ref, m_ref, v_ref, o_ref):
        g = r_ref[0].astype(F32)
        for s in range(1, N_DEV):
            g = g + r_ref[s].astype(F32)
        delta, m2, v2 = _adamw(w_ref[...], g, m_ref[...], v_ref[...])
        o_ref[0] = g
        o_ref[1] = delta
        o_ref[2] = m2
        o_ref[3] = v2

    col = pl.BlockSpec((R, tc), lambda j: (0, j))
    return pl.pallas_call(
        body, name=name, grid=(C // tc,),
        in_specs=[pl.BlockSpec((N_DEV, R, tc), lambda j: (0, 0, j)), col, col, col],
        out_specs=pl.BlockSpec((4, R, tc), lambda j: (0, 0, j)),
        out_shape=_sds((4, R, C)),
        compiler_params=_cparams(("parallel",)),
    )(recv, w, m, v)


def _adamw_call(g, w, m, v, name):
    def body(g_ref, w_ref, m_ref, v_ref, o_ref):
        delta, m2, v2 = _adamw(w_ref[...], g_ref[...], m_ref[...], v_ref[...])
        o_ref[0] = delta
        o_ref[1] = m2
        o_ref[2] = v2

    return pl.pallas_call(body, name=name, out_shape=_sds((3,) + g.shape))(g, w, m, v)


ADA_N = 6 * D // N_CHIP
KPAD = 128


def _mod_part(c8, w_ada, b_ada):
    tn = 512

    def body(c_ref, w_ref, b_ref, o_ref):
        o_ref[...] = _raw1(_silu(c_ref[...]), w_ref[...], _NN) + b_ref[...]

    return pl.pallas_call(
        body, name="ada_mod", grid=(ADA_N // tn,),
        in_specs=[pl.BlockSpec((16, D), lambda j: (0, 0)), pl.BlockSpec((D, tn), lambda j: (0, j)),
                  pl.BlockSpec((1, tn), lambda j: (0, j))],
        out_specs=pl.BlockSpec((16, tn), lambda j: (0, j)),
        out_shape=_sds((16, ADA_N)),
        compiler_params=_cparams(("parallel",)),
    )(c8, w_ada, b_ada)


def _w_ada_update(c8p, dm, w, m, v):
    tr = 256

    def body(c_ref, dm_ref, w_ref, m_ref, v_ref, g_ref, d_ref, m2_ref, v2_ref):
        g = _raw1(_silu(c_ref[...]), dm_ref[...], _TN)
        delta, m2, v2 = _adamw(w_ref[...], g, m_ref[...], v_ref[...])
        g_ref[...] = g
        d_ref[...] = delta
        m2_ref[...] = m2
        v2_ref[...] = v2

    blk = pl.BlockSpec((tr, ADA_N), lambda i: (i, 0))
    return pl.pallas_call(
        body, name="w_ada_update", grid=(D // tr,),
        in_specs=[pl.BlockSpec((KPAD, tr), lambda i: (0, i)), pl.BlockSpec((KPAD, ADA_N), lambda i: (0, 0)),
                  blk, blk, blk],
        out_specs=[blk] * 4, out_shape=[_sds((D, ADA_N))] * 4,
        compiler_params=_cparams(("parallel",)),
    )(c8p, dm, w, m, v)


def _me():
    return lax.axis_index("x"), lax.axis_index("y"), lax.axis_index("c")


def _peer(k, me):
    mx, my, mc = me
    return (1 - mx if k & 4 else mx, 1 - my if k & 2 else my, 1 - mc if k & 1 else mc)


def _lin(p):
    return 4 * p[0] + 2 * p[1] + p[2]


def _remote(src, dst, ssem, rsem, dev):
    return pltpu.make_async_remote_copy(src_ref=src, dst_ref=dst, send_sem=ssem, recv_sem=rsem,
                                        device_id=dev, device_id_type=MESH)


def _all_gather8(x, name):
    def body(x_ref, out_ref, send_sems, recv_sems):
        me = _me()
        out_ref[_lin(me)] = x_ref[...]
        sends = []
        for k in range(1, N_DEV):
            cp = _remote(x_ref, out_ref.at[_lin(me)], send_sems.at[k - 1], recv_sems.at[k - 1], _peer(k, me))
            cp.start()
            sends.append(cp)
        for k in range(1, N_DEV):
            p = _peer(k, me)
            _remote(x_ref, out_ref.at[_lin(p)], send_sems.at[k - 1], recv_sems.at[k - 1], p).wait_recv()
        for cp in sends:
            cp.wait_send()

    return pl.pallas_call(
        body, name=name,
        out_shape=_sds((N_DEV,) + x.shape, x.dtype),
        in_specs=[pl.BlockSpec(memory_space=pltpu.VMEM)],
        out_specs=pl.BlockSpec(memory_space=pltpu.VMEM),
        scratch_shapes=[pltpu.SemaphoreType.DMA((N_DEV - 1,)), pltpu.SemaphoreType.DMA((N_DEV - 1,))],
    )(x)


def _hbm_specs(n):
    return [pl.BlockSpec(memory_space=pl.ANY)] * n


def _gather_weights(shards):
    n = len(shards)

    def body(*refs):
        plan = _gather_plan(refs[:n], refs[n:2 * n], *refs[2 * n:])
        _start(plan)
        _finish(plan)

    return pl.pallas_call(
        body, name="gather_weights",
        out_shape=_gather_shapes(shards), in_specs=_hbm_specs(n), out_specs=_hbm_specs(n),
        scratch_shapes=_gather_sems(n),
    )(*shards)


def _gather_shapes(shards):
    return [_sds((N_CHIP,) + s.shape, s.dtype) for s in shards]


def _gather_sems(n):
    return [pltpu.SemaphoreType.DMA((3 * n,)), pltpu.SemaphoreType.DMA((3 * n,)), pltpu.SemaphoreType.DMA((n,))]


def _gather_plan(ins, outs, send_sems, recv_sems, local_sems):
    mx, my, mc = _me()
    chips = [(1 - mx, my), (mx, 1 - my), (1 - mx, 1 - my)]
    local, sends, recvs = [], [], []
    for a in range(len(ins)):
        local.append(pltpu.make_async_copy(ins[a], outs[a].at[2 * mx + my], local_sems.at[a]))
        for k, (px, py) in enumerate(chips):
            sems = (send_sems.at[3 * a + k], recv_sems.at[3 * a + k], (px, py, mc))
            sends.append(_remote(ins[a], outs[a].at[2 * mx + my], *sems))
            recvs.append(_remote(ins[a], outs[a].at[2 * px + py], *sems))
    return local, sends, recvs


def _start(plan):
    local, sends, _ = plan
    for cp in local + sends:
        cp.start()


def _finish(plan):
    local, sends, recvs = plan
    for cp in recvs:
        cp.wait_recv()
    for cp in sends:
        cp.wait_send()
    for cp in local:
        cp.wait()


def _grad_exchange(pieces):
    n = len(pieces)

    def body(*refs):
        plan = _exchange_plan(refs[:n], refs[n:2 * n], *refs[2 * n:])
        _start(plan)
        _finish(plan)

    return pl.pallas_call(
        body, name="grad_exchange",
        out_shape=_exchange_shapes(pieces), in_specs=_hbm_specs(n), out_specs=_hbm_specs(n),
        scratch_shapes=_exchange_sems(n),
    )(*pieces)


def _exchange_shapes(pieces):
    return [_sds((N_DEV,) + p.shape[2:], p.dtype) for p in pieces]


def _exchange_sems(n):
    return [pltpu.SemaphoreType.DMA(((N_DEV - 1) * n,)), pltpu.SemaphoreType.DMA(((N_DEV - 1) * n,)),
            pltpu.SemaphoreType.DMA((n,))]


def _exchange_plan(ins, outs, send_sems, recv_sems, local_sems):
    me = _me()
    mx, my, mc = me
    local, sends, recvs = [], [], []
    for a in range(len(ins)):
        local.append(pltpu.make_async_copy(ins[a].at[2 * mx + my, mc], outs[a].at[_lin(me)], local_sems.at[a]))
        for k in range(1, N_DEV):
            p = _peer(k, me)
            s = (N_DEV - 1) * a + k - 1
            sends.append(_remote(ins[a].at[2 * p[0] + p[1], p[2]], outs[a].at[_lin(me)], send_sems.at[s],
                                 recv_sems.at[s], p))
            recvs.append(_remote(ins[a].at[2 * mx + my, mc], outs[a].at[_lin(p)], send_sems.at[s],
                                 recv_sems.at[s], p))
    return local, sends, recvs


def _reduce_swap(recv, name):
    _, rows, cols = recv.shape

    def body(r_ref, o_ref, send_sem, recv_sem):
        mx, my, mc = _me()
        sib = (mx, my, 1 - mc)
        g = r_ref[0].astype(F32)
        for s in range(1, N_DEV):
            g = g + r_ref[s].astype(F32)
        mine = o_ref.at[pl.ds(pl.multiple_of(mc * rows, 8), rows)]
        theirs = o_ref.at[pl.ds(pl.multiple_of((1 - mc) * rows, 8), rows)]
        mine[...] = g
        cp = _remote(mine, mine, send_sem, recv_sem, sib)
        cp.start()
        _remote(mine, theirs, send_sem, recv_sem, sib).wait_recv()
        cp.wait_send()

    return pl.pallas_call(
        body, name=name, out_shape=_sds((2 * rows, cols)),
        in_specs=[pl.BlockSpec(memory_space=pltpu.VMEM)], out_specs=pl.BlockSpec(memory_space=pltpu.VMEM),
        scratch_shapes=[pltpu.SemaphoreType.DMA, pltpu.SemaphoreType.DMA],
        compiler_params=_cparams(),
    )(recv)


def _adamw_big(g, w, m, v, name):
    rows, cols = g.shape
    tr = next(t for t in (256, 176, 128, 64, 8) if rows % t == 0)

    def body(g_ref, w_ref, m_ref, v_ref, d_ref, m2_ref, v2_ref):
        delta, m2, v2 = _adamw(w_ref[...], g_ref[...], m_ref[...], v_ref[...])
        d_ref[...] = delta
        m2_ref[...] = m2
        v2_ref[...] = v2

    blk = pl.BlockSpec((tr, cols), lambda i: (i, 0))
    return pl.pallas_call(
        body, name=name, grid=(rows // tr,),
        in_specs=[blk] * 4, out_specs=[blk] * 3, out_shape=[_sds((rows, cols))] * 3,
        compiler_params=_cparams(("parallel",)),
    )(g, w, m, v)


SMALL_ORDER = (("mod", 6 * D), ("norm1_w", D), ("norm2_w", D), ("conv_w", CONVW * 3 * GW), ("a_log", GH),
               ("dt_bias", GH), ("gdn_norm_w", HD), ("q_norm_w", HD), ("k_norm_w", HD), ("sinks", SQH), ("loss", 1))
SMALL_R = 120


def _pack_small(d):
    parts = [d[k].reshape(-1).astype(F32) if k in d else jnp.zeros((n,), F32) for k, n in SMALL_ORDER]
    used = sum(n for _, n in SMALL_ORDER)
    parts.append(jnp.zeros((SMALL_R * LANE - used,), F32))
    return jnp.concatenate(parts).reshape(SMALL_R, LANE)


def _unpack_small(pk):
    flat = pk.reshape(-1)
    out, r = {}, 0
    for k, n in SMALL_ORDER:
        out[k] = flat[r:r + n]
        r += n
    return out


def kernel(x, c, w_ada, b_ada, norm1_w, w_in, conv_w, a_log, dt_bias, gdn_norm_w, q_norm_w, k_norm_w, sinks, w_out, norm2_w, w_gate, w_up, w_down, loss_target, m_w_ada, m_b_ada, m_norm1_w, m_w_in, m_conv_w, m_a_log, m_dt_bias, m_gdn_norm_w, m_q_norm_w, m_k_norm_w, m_sinks, m_w_out, m_norm2_w, m_w_gate, m_w_up, m_w_down, v_w_ada, v_b_ada, v_norm1_w, v_w_in, v_conv_w, v_a_log, v_dt_bias, v_gdn_norm_w, v_q_norm_w, v_k_norm_w, v_sinks, v_w_out, v_norm2_w, v_w_gate, v_w_up, v_w_down):
    mx, my, mc = _me()
    chip = 2 * mx + my
    dev = 4 * mx + 2 * my + mc
    T = x.shape[1]

    conv_sh = conv_w.reshape(CONVW, 3 * GW // N_CHIP)
    mine = jnp.concatenate([c.reshape(-1), conv_sh.reshape(-1), jnp.zeros((4 * LANE,), F32)]).reshape(24, LANE)
    got = _all_gather8(mine, "gather_c_conv")
    c8 = got[:, :8].reshape(N_DEV, D)
    conv_full = jnp.concatenate([got[2 * j, 8:20].reshape(CONVW, 3 * GW // N_CHIP) for j in range(N_CHIP)], axis=1)
    c16 = jnp.concatenate([c8, jnp.zeros((8, D), F32)], axis=0)
    b_sh = lax.dynamic_slice(b_ada, (0, chip * ADA_N), (1, ADA_N))
    mods = _all_gather8(_mod_part(c16, w_ada[0], b_sh), "gather_mod")
    mod = jnp.concatenate([lax.dynamic_slice(mods[2 * j], (dev, 0), (1, ADA_N)) for j in range(N_CHIP)], axis=1)

    big_w = (w_in, w_out, w_gate, w_up, w_down)
    shards = [t[0].astype(BF16) for t in big_w]
    a_in, a_out = _gather_weights(shards[:2])
    w_in_f = jnp.concatenate([a_in[j] for j in range(N_CHIP)], axis=1)
    w_out_f = a_out.reshape(D, D)

    loss, grad_x, big, small = _local_step(
        x[0], loss_target[0], mod, norm1_w, _permute_w_in(w_in_f), conv_full, a_log, dt_bias, gdn_norm_w,
        q_norm_w, k_norm_w, sinks, w_out_f, norm2_w, shards[2:])

    small["loss"] = loss[:, :1]
    sg = _all_gather8(_pack_small(small), "gather_small_grads")
    rep = dict(mod=(b_ada, m_b_ada, v_b_ada), norm1_w=(norm1_w, m_norm1_w, v_norm1_w),
               norm2_w=(norm2_w, m_norm2_w, v_norm2_w), a_log=(a_log, m_a_log, v_a_log),
               dt_bias=(dt_bias, m_dt_bias, v_dt_bias), gdn_norm_w=(gdn_norm_w, m_gdn_norm_w, v_gdn_norm_w),
               q_norm_w=(q_norm_w, m_q_norm_w, v_q_norm_w), k_norm_w=(k_norm_w, m_k_norm_w, v_k_norm_w),
               sinks=(sinks, m_sinks, v_sinks))
    wmv = [_pack_small({k: t[i] for k, t in rep.items()}) for i in range(3)]
    sres = _reduce_adamw(sg, wmv[0], wmv[1], wmv[2], "small_reduce_adamw")
    s_g, s_d, s_m, s_v = [_unpack_small(sres[i]) for i in range(4)]
    loss_out = s_g["loss"][0]

    g_conv = lax.dynamic_slice(s_g["conv_w"].reshape(CONVW, 3 * GW), (0, chip * (3 * GW // N_CHIP)),
                               (CONVW, 3 * GW // N_CHIP))
    pad16 = lambda t: jnp.concatenate([t.reshape(12, LANE), jnp.zeros((4, LANE), F32)], axis=0)
    cres = _adamw_call(pad16(g_conv), pad16(conv_w), pad16(m_conv_w), pad16(v_conv_w), "conv_adamw")
    conv_out = [g_conv.reshape(conv_w.shape)] + [cres[i, :12].reshape(conv_w.shape) for i in range(3)]

    dmod8 = sg[:, :6 * D // LANE].reshape(N_DEV, 6 * D)
    dm = lax.dynamic_slice(dmod8, (0, chip * ADA_N), (N_DEV, ADA_N))
    zpad = lambda t: jnp.concatenate([t, jnp.zeros((KPAD - N_DEV, t.shape[1]), F32)], axis=0)
    ares = _w_ada_update(zpad(c8), zpad(dm), w_ada[0], m_w_ada[0], v_w_ada[0])

    recv = _grad_exchange([_pieces_by_cols(_unpermute_w_in(big["w_in_p"])), _pieces_by_rows(big["w_out"])])
    names = ("w_in", "w_out", "w_gate", "w_up", "w_down")
    g_full = [_reduce_swap(r, "reduce_" + nm) for r, nm in zip(list(recv) + list(big["recv_ffn"]), names)]
    big_m = (m_w_in, m_w_out, m_w_gate, m_w_up, m_w_down)
    big_v = (v_w_in, v_w_out, v_w_gate, v_w_up, v_w_down)
    upd = [_adamw_big(g, w[0], m[0], v[0], "adamw_" + nm)
           for g, w, m, v, nm in zip(g_full, big_w, big_m, big_v, names)]
    bg = [g[None] for g in g_full]
    bd, bm, bv = [[u[i][None] for u in upd] for i in range(3)]

    def group(a_i, small_d, conv_i, big_l):
        s = lambda k, ref: small_d[k].reshape(ref.shape)
        return [ares[a_i][None], s("mod", b_ada), s("norm1_w", norm1_w), big_l[0], conv_out[conv_i],
                s("a_log", a_log), s("dt_bias", dt_bias), s("gdn_norm_w", gdn_norm_w), s("q_norm_w", q_norm_w),
                s("k_norm_w", k_norm_w), s("sinks", sinks), big_l[1], s("norm2_w", norm2_w), big_l[2], big_l[3],
                big_l[4]]

    outs = [loss_out, grad_x[None]]
    outs += group(0, s_g, 0, bg) + group(1, s_d, 1, bd) + group(2, s_m, 2, bm) + group(3, s_v, 3, bv)
    return tuple(outs)
```

```python
import functools

import jax
import jax.numpy as jnp
from jax import lax
from jax.experimental import pallas as pl
from jax.experimental.pallas import tpu as pltpu

F32 = jnp.float32
BF16 = jnp.bfloat16
MESH = pl.DeviceIdType.MESH

D = 1024
HD = 64
GH = 8
GW = GH * HD
SQH = 8
SKVH = 2
SGRP = SQH // SKVH
WIN = 128
CONVW = 4
CHUNK = 64
DFF = 2816
PROJ = 2832
NP = 3072
EPS = 1e-6
N_DEV = 8
N_CHIP = 4

ADAM_LR = 0.001
ADAM_B1 = 0.9
ADAM_B2 = 0.999
ADAM_EPS = 1e-08
ADAM_WD = 0.01
ADAM_STEP = 10

VMEM_LIMIT = 48 * 1024 * 1024
GDN_BWD_VMEM = 58 * 1024 * 1024
LANE = 128

PACK_ROWS = (PROJ // N_CHIP, D // N_CHIP, DFF // N_CHIP, DFF // N_CHIP, DFF // N_CHIP)
PACK_P = 3104
PACK_H = PACK_P // 2


def _cparams(sem=None, vmem=VMEM_LIMIT):
    return pltpu.CompilerParams(dimension_semantics=sem, vmem_limit_bytes=vmem)


_NN = ((1,), (0,))
_NT = ((1,), (1,))
_TN = ((0,), (0,))


def _dot(a, b, dims):
    if a.ndim == 3:
        (ca,), (cb,) = dims
        return lax.dot_general(a, b, (((ca + 1,), (cb + 1,)), ((0,), (0,))), preferred_element_type=F32)
    return lax.dot_general(a, b, (dims, ((), ())), preferred_element_type=F32)


def _raw1(a, b, dims):
    return _dot(a.astype(BF16), b.astype(BF16), dims)


def _raw3(a, b, dims):
    ah = a.astype(BF16)
    al = (a - ah.astype(F32)).astype(BF16)
    bh = b.astype(BF16)
    bl = (b - bh.astype(F32)).astype(BF16)
    return _dot(ah, bh, dims) + (_dot(al, bh, dims) + _dot(ah, bl, dims))


def _make_diff_mm(raw):
    @jax.custom_vjp
    def nn(a, b):
        return raw(a, b, _NN)

    @jax.custom_vjp
    def nt(a, b):
        return raw(a, b, _NT)

    @jax.custom_vjp
    def tn(a, b):
        return raw(a, b, _TN)

    nn.defvjp(lambda a, b: (raw(a, b, _NN), (a, b)), lambda r, g: (nt(g, r[1]), tn(r[0], g)))
    nt.defvjp(lambda a, b: (raw(a, b, _NT), (a, b)), lambda r, g: (nn(g, r[1]), tn(g, r[0])))
    tn.defvjp(lambda a, b: (raw(a, b, _TN), (a, b)), lambda r, g: (nt(r[1], g), nn(r[0], g)))
    return nn, nt, tn


def _tri_inv_raw(a, nn3):
    n = a.shape[-1]
    ri = lax.broadcasted_iota(jnp.int32, (n, n), 0)
    ci = lax.broadcasted_iota(jnp.int32, (n, n), 1)
    t = (ri == ci).astype(F32)
    for lvl in range((n - 1).bit_length()):
        same_pair = (ri >> (lvl + 1)) == (ci >> (lvl + 1))
        lower_left = (((ri >> lvl) & 1) == 1) & (((ci >> lvl) & 1) == 0)
        y = jnp.where(same_pair & lower_left, a, 0.0)
        t = t - y if lvl == 0 else t - nn3(nn3(t, y), t)
    return t


class _Kit:
    def __init__(self, diff):
        if diff:
            self.nn, self.nt, self.tn = _make_diff_mm(_raw1)
            self.nn3, self.nt3, self.tn3 = _make_diff_mm(_raw3)
            nn3, nt3, tn3 = self.nn3, self.nt3, self.tn3

            @jax.custom_vjp
            def inv(a, t):
                return t

            def inv_fwd(a, t):
                return t, t

            def inv_bwd(t, g):
                return -tn3(t, nt3(g, t)), jnp.zeros_like(t)

            inv.defvjp(inv_fwd, inv_bwd)
            self.inv = inv
        else:
            self.nn = lambda a, b: _raw1(a, b, _NN)
            self.nt = lambda a, b: _raw1(a, b, _NT)
            self.tn = lambda a, b: _raw1(a, b, _TN)
            self.nn3 = lambda a, b: _raw3(a, b, _NN)
            self.nt3 = lambda a, b: _raw3(a, b, _NT)
            self.tn3 = lambda a, b: _raw3(a, b, _TN)
            self.inv = lambda a, t: _tri_inv_raw(a, self.nn3) if t is None else t


def _sigmoid(x):
    return 1.0 / (1.0 + jnp.exp(-x))


def _silu(x):
    return x * _sigmoid(x)


def _rms(x, w):
    return x * lax.rsqrt(jnp.mean(x * x, axis=-1, keepdims=True) + EPS) * w


def _tile(dim, target):
    t = (min(dim, target) // LANE) * LANE
    while t >= LANE:
        if dim % t == 0:
            return t
        t -= LANE
    return dim


MM_TM, MM_TN, MM_TK = 1408, 1536, 1408


def _matmul(a, b, ta=False, tb=False, out_dtype=F32, name="matmul", gather=None, exchange=None):
    carried = gather if gather is not None else exchange if exchange is not None else []
    nc = len(carried)
    if ta:
        K, M = a.shape
    else:
        M, K = a.shape
    if tb:
        N, K2 = b.shape
    else:
        K2, N = b.shape
    assert K == K2, (a.shape, b.shape, ta, tb)
    tm, tn, tk = _tile(M, MM_TM), _tile(N, MM_TN), _tile(K, MM_TK)
    nk = K // tk
    dims = ((0,) if ta else (1,), (1,) if tb else (0,))

    grid = (M // tm, N // tn, nk)

    def body(*refs):
        a_ref, b_ref = refs[:2]
        o_ref = refs[2 + nc]
        scratch = refs[3 + 2 * nc:]
        k = pl.program_id(2)
        if nc:
            make_plan = _gather_plan if gather is not None else _exchange_plan
            plan = make_plan(refs[2:2 + nc], refs[3 + nc:3 + 2 * nc], *scratch[-3:])
            at = lambda pos: ((pl.program_id(0) == pos[0]) & (pl.program_id(1) == pos[1]) & (k == pos[2]))

            @pl.when(at((0, 0, 0)))
            def _():
                _start(plan)

        part = _dot(a_ref[...].astype(BF16), b_ref[...].astype(BF16), dims)
        if nk == 1:
            o_ref[...] = part.astype(o_ref.dtype)
        else:
            acc_ref = scratch[0]

            @pl.when(k == 0)
            def _():
                acc_ref[...] = part

            @pl.when((k > 0) & (k < nk - 1))
            def _():
                acc_ref[...] += part

            @pl.when(k == nk - 1)
            def _():
                o_ref[...] = (acc_ref[...] + part).astype(o_ref.dtype)

        if nc:
            @pl.when(at((grid[0] - 1, grid[1] - 1, nk - 1)))
            def _():
                _finish(plan)

    a_spec = (pl.BlockSpec((tk, tm), lambda i, j, k: (k, i)) if ta
              else pl.BlockSpec((tm, tk), lambda i, j, k: (i, k)))
    b_spec = (pl.BlockSpec((tn, tk), lambda i, j, k: (j, k)) if tb
              else pl.BlockSpec((tk, tn), lambda i, j, k: (k, j)))
    if gather is not None:
        c_shapes, c_sems = _gather_shapes(carried), _gather_sems(nc)
    elif exchange is not None:
        c_shapes, c_sems = _exchange_shapes(carried), _exchange_sems(nc)
    else:
        c_shapes, c_sems = [], []
    res = pl.pallas_call(
        body, name=name, grid=grid,
        in_specs=[a_spec, b_spec] + _hbm_specs(nc),
        out_specs=[pl.BlockSpec((tm, tn), lambda i, j, k: (i, j))] + _hbm_specs(nc),
        out_shape=[jax.ShapeDtypeStruct((M, N), out_dtype)] + c_shapes,
        scratch_shapes=([pltpu.VMEM((tm, tn), F32)] if nk > 1 else []) + c_sems,
        compiler_params=_cparams(("arbitrary",) * 3 if nc else ("parallel", "parallel", "arbitrary")),
    )(a, b, *carried)
    return (res[0], res[1:]) if nc else res[0]


def _rowcall(fn, tiled, consts, out_tiled, out_acc, tm, name):
    T = tiled[0].shape[0]
    n_in = len(tiled) + len(consts)
    n_o = len(out_tiled)

    def body(*refs):
        vals = [r[...] for r in refs[:n_in]]
        outs = refs[n_in:]
        res = fn(*vals)
        for r, v in zip(outs[:n_o], res[:n_o]):
            r[...] = v.astype(r.dtype)
        if len(outs) > n_o:
            @pl.when(pl.program_id(0) == 0)
            def _():
                for r in outs[n_o:]:
                    r[...] = jnp.zeros_like(r)

            for r, v in zip(outs[n_o:], res[n_o:]):
                r[...] += v

    in_specs = [pl.BlockSpec((tm, a.shape[1]), lambda i: (i, 0)) for a in tiled]
    in_specs += [pl.BlockSpec(a.shape, lambda i, nd=a.ndim: (0,) * nd) for a in consts]
    out_specs = [pl.BlockSpec((tm, s.shape[1]), lambda i: (i, 0)) for s in out_tiled]
    out_specs += [pl.BlockSpec(s.shape, lambda i: (0, 0)) for s in out_acc]
    return pl.pallas_call(
        body, name=name, grid=(T // tm,),
        in_specs=in_specs, out_specs=out_specs,
        out_shape=list(out_tiled) + list(out_acc),
        compiler_params=_cparams(("arbitrary",)),
    )(*tiled, *consts)


def _sds(shape, dtype=F32):
    return jax.ShapeDtypeStruct(shape, dtype)


def _norm_mod(x, nw, scale, shift):
    return _rms(x, nw) * (1.0 + scale) + shift


def _norm_mod_fwd(x, nw, scale, shift):
    T = x.shape[0]
    (h,) = _rowcall(lambda *a: (_norm_mod(*a),), [x], [nw, scale, shift],
                    [_sds((T, D), BF16)], [], 512, "norm1_fwd")
    return h


def _norm_mod_bwd(x, dh, dres, nw, scale, shift):
    T = x.shape[0]

    def fn(x, dh, dres, nw, scale, shift):
        _, vjp = jax.vjp(_norm_mod, x, nw, scale, shift)
        dx, dnw, dsc, dsh = vjp(dh)
        return dx + dres, dnw, dsc, dsh

    return _rowcall(fn, [x, dh, dres], [nw, scale, shift], [_sds((T, D))],
                    [_sds((1, D))] * 3, 256, "norm1_bwd")


def _resid_norm(x, mixed, gate1, nw, scale, shift):
    x1 = x + gate1 * mixed
    return x1, _norm_mod(x1, nw, scale, shift)


def _resid_norm_fwd(x, mixed, gate1, nw, scale, shift):
    T = x.shape[0]
    return _rowcall(_resid_norm, [x, mixed], [gate1, nw, scale, shift],
                    [_sds((T, D)), _sds((T, D), BF16)], [], 512, "resid_norm2_fwd")


def _resid_norm_bwd(x, mixed, dy, dh2, gate1, nw, scale, shift):
    T = x.shape[0]

    def fn(x, mixed, dy, dh2, gate1, nw, scale, shift):
        _, vjp = jax.vjp(_resid_norm, x, mixed, gate1, nw, scale, shift)
        dx, dmixed, dg1, dnw, dsc, dsh = vjp((dy, dh2))
        return dx, dmixed, dg1, dnw, dsc, dsh

    return _rowcall(fn, [x, mixed, dy, dh2], [gate1, nw, scale, shift],
                    [_sds((T, D)), _sds((T, D), BF16)], [_sds((1, D))] * 4, 256, "resid_norm2_bwd")


def _ffn_act_fwd(ab):
    T = ab.shape[0]

    def fn(ab):
        a, b = ab[:, :DFF], ab[:, DFF:]
        return (_silu(a) * b,)

    (act,) = _rowcall(fn, [ab], [], [_sds((T, DFF), BF16)], [], 256, "ffn_act_fwd")
    return act


def _ffn_act_bwd(ab, dact):
    T = ab.shape[0]

    def fn(ab, dact):
        a, b = ab[:, :DFF], ab[:, DFF:]
        s = _sigmoid(a)
        da = dact * b * (s * (1.0 + a * (1.0 - s)))
        db = dact * (a * s)
        return (jnp.concatenate([da, db], axis=1),)

    (dab,) = _rowcall(fn, [ab, dact], [], [_sds((T, 2 * DFF), BF16)], [], 256, "ffn_act_bwd")
    return dab


def _loss_head(x1, ffn, target, gate2):
    T = x1.shape[0]

    def fn(x1, ffn, target, gate2):
        y = x1 + gate2 * ffn
        err = y - target
        loss = 0.5 * jnp.sum(jnp.sum(err * err, axis=1, keepdims=True), axis=0, keepdims=True) / D
        dy = err * (1.0 / D)
        dgate2 = jnp.sum(dy * ffn, axis=0, keepdims=True)
        return dy, gate2 * dy, dgate2, jnp.broadcast_to(loss, (1, LANE))

    return _rowcall(fn, [x1, ffn, target], [gate2], [_sds((T, D)), _sds((T, D), BF16)],
                    [_sds((1, D)), _sds((1, LANE))], 256, "loss_head")


def _round_bf16(x):
    return x.astype(BF16).astype(F32)


def _shift_down(x, s, rows):
    if s == 0:
        return x
    return jnp.where(rows >= s, pltpu.roll(x, s, 0), 0.0)


def _shift_up(x, s, rows, T):
    if s == 0:
        return x
    return jnp.where(rows < T - s, pltpu.roll(x, T - s, 0), 0.0)


def _conv_fwd(proj, conv_w):
    T = proj.shape[0]
    ncol = 3 * GW // LANE

    def body(x_ref, w_ref, o_ref):
        x = _round_bf16(x_ref[...])
        rows = lax.broadcasted_iota(jnp.int32, x.shape, 0)
        acc = jnp.zeros_like(x)
        for j in range(CONVW):
            acc = acc + _round_bf16(w_ref[pl.ds(j, 1), :]) * _shift_down(x, CONVW - 1 - j, rows)
        o_ref[0], o_ref[1] = _split_pair(_silu(acc))

    return pl.pallas_call(
        body, name="conv_fwd", grid=(ncol,),
        in_specs=[pl.BlockSpec((T, LANE), lambda j: (0, j)), pl.BlockSpec((CONVW, LANE), lambda j: (0, j))],
        out_specs=pl.BlockSpec((2, T, HD), lambda j: (j, 0, 0)),
        out_shape=_sds((3 * GH, T, HD)),
        compiler_params=_cparams(("parallel",)),
    )(proj, conv_w)


RELAYOUT_TM = 4096


def _split_pair(y):
    return y[:, :HD], pltpu.roll(y, HD, 1)[:, :HD]


def _merge_pair(a, b):
    return jnp.concatenate([a, b], axis=1)


def _split_heads(x, col_block0, nheads, name):
    T = x.shape[0]
    tm = _tile(T, RELAYOUT_TM)

    def body(x_ref, o_ref):
        a, b = _split_pair(x_ref[...])
        o_ref[0] = a
        o_ref[1] = b

    return pl.pallas_call(
        body, name=name, grid=(nheads // 2, T // tm),
        in_specs=[pl.BlockSpec((tm, LANE), lambda j, i: (i, col_block0 + j))],
        out_specs=pl.BlockSpec((2, tm, HD), lambda j, i: (j, i, 0)),
        out_shape=_sds((nheads, T, HD), x.dtype),
        compiler_params=_cparams(("parallel", "parallel")),
    )(x)


def _merge_heads(hm, out_dtype, name, into=None, col_block0=0, head0=0, nheads=None):
    T = hm.shape[1]
    nheads = hm.shape[0] if nheads is None else nheads
    tm = _tile(T, RELAYOUT_TM)

    def body(*refs):
        h_ref, o_ref = refs[0], refs[-1]
        o_ref[...] = _merge_pair(h_ref[0], h_ref[1]).astype(o_ref.dtype)

    in_specs = [pl.BlockSpec((2, tm, HD), lambda j, i: (head0 // 2 + j, i, 0))]
    args = [hm]
    if into is None:
        out_shape = _sds((T, HD * nheads), out_dtype)
        aliases = {}
    else:
        out_shape = _sds(into.shape, into.dtype)
        in_specs.append(pl.BlockSpec(memory_space=pl.ANY))
        args.append(into)
        aliases = {1: 0}
    return pl.pallas_call(
        body, name=name, grid=(nheads // 2, T // tm),
        in_specs=in_specs,
        out_specs=pl.BlockSpec((tm, LANE), lambda j, i: (i, col_block0 + j)),
        out_shape=out_shape, input_output_aliases=aliases,
        compiler_params=_cparams(("parallel", "parallel")),
    )(*args)


def _conv_bwd(proj, conv_w, dqc):
    T = proj.shape[0]
    ncol = 3 * GW // LANE

    def body(x_ref, w_ref, d_ref, dx_ref, dw_ref):
        x = _round_bf16(x_ref[...])
        rows = lax.broadcasted_iota(jnp.int32, x.shape, 0)
        xs = [_shift_down(x, CONVW - 1 - j, rows) for j in range(CONVW)]
        w = [_round_bf16(w_ref[pl.ds(j, 1), :]) for j in range(CONVW)]
        pre = jnp.zeros_like(x)
        for j in range(CONVW):
            pre = pre + w[j] * xs[j]
        s = _sigmoid(pre)
        dpre = _round_bf16(_merge_pair(d_ref[0], d_ref[1]) * (s * (1.0 + pre * (1.0 - s))))
        dx = jnp.zeros_like(x)
        for j in range(CONVW):
            dx = dx + w[j] * _shift_up(dpre, CONVW - 1 - j, rows, T)
            dw_ref[pl.ds(j, 1), :] = jnp.sum(dpre * xs[j], axis=0, keepdims=True)
        dx_ref[...] = dx.astype(dx_ref.dtype)

    return pl.pallas_call(
        body, name="conv_bwd", grid=(ncol,),
        in_specs=[pl.BlockSpec((T, LANE), lambda j: (0, j)), pl.BlockSpec((CONVW, LANE), lambda j: (0, j)),
                  pl.BlockSpec((2, T, HD), lambda j: (j, 0, 0))],
        out_specs=[pl.BlockSpec((T, LANE), lambda j: (0, j)), pl.BlockSpec((CONVW, LANE), lambda j: (0, j))],
        out_shape=[_sds((T, NP), BF16), _sds((CONVW, 3 * GW))],
        compiler_params=_cparams(("parallel",)),
    )(proj, conv_w, dqc)


def _gdn_prep(kit, q, k, v, ga, gb, alog, dtb, t_inv=None):
    C = CHUNK
    ri = lax.broadcasted_iota(jnp.int32, (C, C), 0)
    ci = lax.broadcasted_iota(jnp.int32, (C, C), 1)
    causal = ri >= ci
    strict = ri > ci
    eye = (ri == ci).astype(F32)
    lower = causal.astype(F32)
    upper = (ri <= ci).astype(F32)

    a = ga + dtb
    softplus = jnp.maximum(a, 0.0) + jnp.log(1.0 + jnp.exp(-jnp.abs(a)))
    g_row = -jnp.exp(alog) * softplus
    beta_row = _sigmoid(gb)
    g_col = jnp.sum(eye * g_row, axis=2, keepdims=True)
    beta_col = jnp.sum(eye * beta_row, axis=2, keepdims=True)
    G_col = jnp.sum(lower * g_row, axis=2, keepdims=True)
    G_row = jnp.sum(upper * g_col, axis=1, keepdims=True)
    G_last = jnp.sum(g_row, axis=2, keepdims=True)
    decay = jnp.exp(jnp.where(causal, G_col - G_row, -1e30))

    qn = q * lax.rsqrt(jnp.sum(q * q, axis=-1, keepdims=True) + EPS) * (HD ** -0.5)
    kn = k * lax.rsqrt(jnp.sum(k * k, axis=-1, keepdims=True) + EPS)
    kb = kn * beta_col
    A = jnp.where(strict, kit.nt(kb, kn) * decay, 0.0)
    Tm = kit.inv(A, t_inv)
    eG = jnp.exp(G_col)
    u = kit.nn3(Tm, v * beta_col)
    w = kit.nn3(Tm, kb * eG)
    qk = jnp.where(causal, kit.nt(qn, kn) * decay, 0.0)
    q_dec = qn * eG
    k_dec = kn * jnp.exp(G_last - G_col)
    dec = jnp.exp(G_last)
    return u, w, qk, q_dec, k_dec, dec, Tm


def _gdn_out(o, z, nw):
    return _rms(o, nw) * _silu(z)


GDN_CB = 4


def _gdn_specs(T, blk):
    TB = GDN_CB * CHUNK
    seq = lambda grp: pl.BlockSpec((GH, TB, HD), lambda i, grp=grp: (grp, blk(i), 0))
    row = lambda grp: pl.BlockSpec((GH, GDN_CB, 1, CHUNK), lambda i, grp=grp: (grp, blk(i), 0, 0))
    per_head = pl.BlockSpec((GH, 1, CHUNK), lambda i: (0, 0, 0))
    whole = pl.BlockSpec((1, HD), lambda i: (0, 0))
    state = pl.BlockSpec((GH, GDN_CB, HD, HD), lambda i: (0, blk(i), 0, 0))
    return seq, row, per_head, whole, state


def _gdn_load(seq_refs, row_refs, head_refs):
    chunks = lambda r: jnp.concatenate([r[:, pl.ds(cb * CHUNK, CHUNK), :] for cb in range(GDN_CB)], axis=0)
    rows = lambda r: jnp.concatenate([r[:, cb] for cb in range(GDN_CB)], axis=0)
    heads = lambda r: jnp.concatenate([r[...]] * GDN_CB, axis=0)
    return [chunks(r) for r in seq_refs], [rows(r) for r in row_refs], [heads(r) for r in head_refs]


def _gdn_fwd(qkv_hm, zs_hm, gab, alog_b, dtb_b, nw, shards):
    T = qkv_hm.shape[1]
    N = T // CHUNK
    nblk = N // GDN_CB
    ns = len(shards)
    seq, row, per_head, whole, state = _gdn_specs(T, lambda i: i)
    kit = _Kit(False)

    def body(*refs):
        q_ref, k_ref, v_ref, z_ref, ga_ref, gb_ref, al_ref, dt_ref, nw_ref = refs[:9]
        o_ref, S_ref, T_ref = refs[9 + ns:12 + ns]
        S_scr = refs[12 + 2 * ns]
        plan = _gather_plan(refs[9:9 + ns], refs[12 + ns:12 + 2 * ns], *refs[13 + 2 * ns:])

        @pl.when(pl.program_id(0) == 0)
        def _():
            S_scr[...] = jnp.zeros_like(S_scr)
            _start(plan)

        (q, k, v, z), (ga, gb), (al, dt) = _gdn_load((q_ref, k_ref, v_ref, z_ref), (ga_ref, gb_ref), (al_ref, dt_ref))
        u, w, qk, q_dec, k_dec, dec, t_inv = _gdn_prep(kit, q, k, v, ga, gb, al, dt)
        S = S_scr[...]
        for cb in range(GDN_CB):
            hs = slice(cb * GH, (cb + 1) * GH)
            S_ref[:, cb] = S
            T_ref[:, cb] = t_inv[hs]
            v_new = u[hs] - kit.nn(w[hs], S)
            o = kit.nn(q_dec[hs], S) + kit.nn(qk[hs], v_new)
            S = S * dec[hs] + kit.tn(k_dec[hs], v_new)
            o_ref[:, pl.ds(cb * CHUNK, CHUNK), :] = _gdn_out(o, z[hs], nw_ref[...])
        S_scr[...] = S

        @pl.when(pl.program_id(0) == nblk - 1)
        def _():
            _finish(plan)

    res = pl.pallas_call(
        body, name="gdn_fwd", grid=(nblk,),
        in_specs=[seq(0), seq(1), seq(2), seq(0), row(0), row(1), per_head, per_head, whole] + _hbm_specs(ns),
        out_specs=[seq(0), state, state] + _hbm_specs(ns),
        out_shape=[_sds((GH + SQH, T, HD)), _sds((GH, N, HD, HD)), _sds((GH, N, CHUNK, CHUNK))]
                  + _gather_shapes(shards),
        scratch_shapes=[pltpu.VMEM((GH, HD, HD), F32)] + _gather_sems(ns),
        compiler_params=_cparams(("arbitrary",)),
    )(qkv_hm, qkv_hm, qkv_hm, zs_hm, gab, gab, alog_b, dtb_b, nw, *shards)
    return res[0], (res[1], res[2]), res[3:]


def _gdn_bwd(qkv_hm, zs_hm, gab, alog_b, dtb_b, nw, S_all, do, pieces):
    T = qkv_hm.shape[1]
    N = T // CHUNK
    nblk = N // GDN_CB
    npc = len(pieces)
    dkit, kit = _Kit(True), _Kit(False)
    rseq, rrow, per_head, whole, rstate = _gdn_specs(T, lambda i: nblk - 1 - i)

    def body(*refs):
        q_ref, k_ref, v_ref, z_ref, ga_ref, gb_ref, al_ref, dt_ref, nw_ref, S_ref, T_ref, do_ref = refs[:12]
        dqkv_ref, dz_ref, dga_ref, dgb_ref, dal_ref, ddt_ref, dnw_ref = refs[12 + npc:19 + npc]
        dS_scr = refs[19 + 2 * npc]
        plan = _exchange_plan(refs[12:12 + npc], refs[19 + npc:19 + 2 * npc], *refs[20 + 2 * npc:])

        @pl.when(pl.program_id(0) == 0)
        def _():
            dS_scr[...] = jnp.zeros_like(dS_scr)
            dal_ref[...] = jnp.zeros_like(dal_ref)
            ddt_ref[...] = jnp.zeros_like(ddt_ref)
            dnw_ref[...] = jnp.zeros_like(dnw_ref)
            _start(plan)

        (q, k, v, z, dout), (ga, gb), (al, dt) = _gdn_load((q_ref, k_ref, v_ref, z_ref, do_ref), (ga_ref, gb_ref),
                                                          (al_ref, dt_ref))
        S_in = jnp.concatenate([S_ref[:, cb] for cb in range(GDN_CB)], axis=0)
        t_inv = jnp.concatenate([T_ref[:, cb] for cb in range(GDN_CB)], axis=0)
        prep = lambda *a: _gdn_prep(dkit, *a, t_inv=t_inv)[:6]
        (u, w, qk, q_dec, k_dec, dec), prep_vjp = jax.vjp(prep, q, k, v, ga, gb, al, dt)
        v_new = u - kit.nn(w, S_in)
        o = kit.nn(q_dec, S_in) + kit.nn(qk, v_new)
        _, out_vjp = jax.vjp(_gdn_out, o, z, nw_ref[...])
        do, dz, dnw = out_vjp(dout)
        dvn_part = kit.tn(qk, do)
        dS_part = kit.tn(q_dec, do)
        dS = dS_scr[...]
        dS_out, dvn = [None] * GDN_CB, [None] * GDN_CB
        for cb in reversed(range(GDN_CB)):
            hs = slice(cb * GH, (cb + 1) * GH)
            dS_out[cb] = dS
            dvn[cb] = dvn_part[hs] + kit.nn(k_dec[hs], dS)
            dS = dS * dec[hs] + dS_part[hs] - kit.tn(w[hs], dvn[cb])
        dS_scr[...] = dS
        dS_out = jnp.concatenate(dS_out, axis=0)
        dvn = jnp.concatenate(dvn, axis=0)
        ddec = jnp.sum(jnp.sum(S_in * dS_out, axis=2, keepdims=True), axis=1, keepdims=True)
        cts = (dvn, -kit.nt(dvn, S_in), kit.nt(do, v_new), kit.nt(do, S_in), kit.nt(v_new, dS_out), ddec)
        dq, dk, dv, dga, dgb, dal, ddt = prep_vjp(cts)
        lanesum = lambda t: jnp.broadcast_to(jnp.sum(t, axis=2, keepdims=True), t.shape)
        for cb in range(GDN_CB):
            hs = slice(cb * GH, (cb + 1) * GH)
            sl = pl.ds(cb * CHUNK, CHUNK)
            dqkv_ref[pl.ds(0, GH), sl, :] = dq[hs]
            dqkv_ref[pl.ds(GH, GH), sl, :] = dk[hs]
            dqkv_ref[pl.ds(2 * GH, GH), sl, :] = dv[hs]
            dz_ref[:, sl, :] = dz[hs]
            dga_ref[:, cb] = dga[hs]
            dgb_ref[:, cb] = dgb[hs]
            dal_ref[...] += lanesum(dal[hs])
            ddt_ref[...] += lanesum(ddt[hs])
        dnw_ref[...] += dnw

        @pl.when(pl.program_id(0) == nblk - 1)
        def _():
            _finish(plan)

    res = pl.pallas_call(
        body, name="gdn_bwd", grid=(nblk,),
        in_specs=[rseq(0), rseq(1), rseq(2), rseq(0), rrow(0), rrow(1), per_head, per_head, whole, rstate, rstate,
                  rseq(0)] + _hbm_specs(npc),
        out_specs=[pl.BlockSpec((3 * GH, GDN_CB * CHUNK, HD), lambda i: (0, nblk - 1 - i, 0)), rseq(0), rrow(0),
                   rrow(0), per_head, per_head, whole] + _hbm_specs(npc),
        out_shape=[_sds((3 * GH, T, HD)), _sds((GH + 4 + SWA_GRAD_HEADS, T, HD))] + [_sds((GH, N, 1, CHUNK))] * 2
                  + [_sds((GH, 1, CHUNK))] * 2 + [_sds((1, HD))] + _exchange_shapes(pieces),
        scratch_shapes=[pltpu.VMEM((GH, HD, HD), F32)] + _exchange_sems(npc),
        compiler_params=_cparams(("arbitrary",), GDN_BWD_VMEM),
    )(qkv_hm, qkv_hm, qkv_hm, zs_hm, gab, gab, alog_b, dtb_b, nw, S_all[0], S_all[1], do, *pieces)
    return res[:7], res[7:]


def _swa_block(kit, first, q0, q1, q2, q3, kp, kc, vp, vc, qnw, knw, s0, s1, s2, s3, *, slopes):
    W = WIN
    ri = lax.broadcasted_iota(jnp.int32, (W, W), 0)
    ci = lax.broadcasted_iota(jnp.int32, (W, W), 1)
    mask_c = ri >= ci
    mask_p = ci > ri + first * W
    dist_c = (ri - ci).astype(F32)
    dist_p = (ri - ci + W).astype(F32)
    kpn = _rms(kp, knw)
    kcn = _rms(kc, knw)
    outs = []
    for q, sink, slope in zip((q0, q1, q2, q3), (s0, s1, s2, s3), slopes):
        qn = _rms(q, qnw)
        sc = jnp.where(mask_c, kit.nt(qn, kcn) * (HD ** -0.5) - slope * dist_c, -1e30)
        sp = jnp.where(mask_p, kit.nt(qn, kpn) * (HD ** -0.5) - slope * dist_p, -1e30)
        m = jnp.maximum(jnp.maximum(jnp.max(sc, axis=-1, keepdims=True), jnp.max(sp, axis=-1, keepdims=True)), sink)
        m = lax.stop_gradient(m)
        pc = jnp.exp(sc - m)
        pp = jnp.exp(sp - m)
        den = jnp.sum(pc, axis=-1, keepdims=True) + jnp.sum(pp, axis=-1, keepdims=True) + jnp.exp(sink - m)
        inv = 1.0 / den
        outs.append(kit.nn(pc * inv, vc) + kit.nn(pp * inv, vp))
    return tuple(outs)


def _swa_slopes(hk):
    return tuple(jnp.where(hk == 0, 2.0 ** (-8.0 * (g + 1.0) / SQH), 2.0 ** (-8.0 * (SGRP + g + 1.0) / SQH))
                 for g in range(SGRP))


def _swa_fwd(zs_hm, qnw, knw, sinks_col):
    T = zs_hm.shape[1]
    NB = T // WIN
    kit = _Kit(False)

    def body(q_ref, kp_ref, kc_ref, vp_ref, vc_ref, qnw_ref, knw_ref, s_ref, o_ref):
        hk = pl.program_id(0)
        first = (pl.program_id(1) == 0).astype(jnp.int32)
        args = ([q_ref[g] for g in range(SGRP)] + [kp_ref[...], kc_ref[...], vp_ref[...], vc_ref[...],
                                                     qnw_ref[...], knw_ref[...]] + [s_ref[g] for g in range(SGRP)])
        outs = _swa_block(kit, first, *args, slopes=_swa_slopes(hk))
        for g in range(SGRP):
            o_ref[g] = outs[g]

    qspec = pl.BlockSpec((SGRP, WIN, HD), lambda hk, n: (2 + hk, n, 0))
    cur = lambda off: pl.BlockSpec((None, WIN, HD), lambda hk, n, off=off: (off + hk, n, 0))
    prev = lambda off: pl.BlockSpec((None, WIN, HD), lambda hk, n, off=off: (off + hk, jnp.maximum(n - 1, 0), 0))
    whole = pl.BlockSpec((1, HD), lambda hk, n: (0, 0))
    sspec = pl.BlockSpec((SGRP, WIN, 1), lambda hk, n: (hk, 0, 0))
    return pl.pallas_call(
        body, name="swa_fwd", grid=(SKVH, NB),
        in_specs=[qspec, prev(16), cur(16), prev(18), cur(18), whole, whole, sspec],
        out_specs=pl.BlockSpec((SGRP, WIN, HD), lambda hk, n: (hk, n, 0)),
        out_shape=_sds((SQH, T, HD)),
        compiler_params=_cparams(("parallel", "arbitrary")),
    )(zs_hm, zs_hm, zs_hm, zs_hm, zs_hm, qnw, knw, sinks_col)


def _swa_bwd(zs_hm, qnw, knw, sinks_col, do):
    T = zs_hm.shape[1]
    NB = T // WIN
    kit = _Kit(True)

    def body(q_ref, kp_ref, kc_ref, vp_ref, vc_ref, qnw_ref, knw_ref, s_ref, do_ref,
             dq_ref, dk_ref, dv_ref, dqnw_ref, dknw_ref, ds_ref, ck_scr, cv_scr):
        hk = pl.program_id(0)
        i = pl.program_id(1)
        first = (i == NB - 1).astype(jnp.int32)

        @pl.when(i == 0)
        def _():
            ck_scr[...] = jnp.zeros_like(ck_scr)
            cv_scr[...] = jnp.zeros_like(cv_scr)
            ds_ref[...] = jnp.zeros_like(ds_ref)

        @pl.when((i == 0) & (hk == 0))
        def _():
            dqnw_ref[...] = jnp.zeros_like(dqnw_ref)
            dknw_ref[...] = jnp.zeros_like(dknw_ref)

        args = ([q_ref[g] for g in range(SGRP)] + [kp_ref[...], kc_ref[...], vp_ref[...], vc_ref[...],
                                                     qnw_ref[...], knw_ref[...]] + [s_ref[g] for g in range(SGRP)])
        dos = tuple(do_ref[g] for g in range(SGRP))
        _, vjp = jax.vjp(functools.partial(_swa_block, kit, first, slopes=_swa_slopes(hk)), *args)
        gr = vjp(dos)
        for g in range(SGRP):
            dq_ref[g] = gr[g]
            ds_ref[g] += jnp.broadcast_to(jnp.sum(gr[10 + g], axis=0, keepdims=True), (WIN, 1))
        dkp, dkc, dvp, dvc = gr[4:8]
        dk_ref[...] = dkc + ck_scr[...]
        dv_ref[...] = dvc + cv_scr[...]
        ck_scr[...] = dkp
        cv_scr[...] = dvp
        dqnw_ref[...] += gr[8]
        dknw_ref[...] += gr[9]

    rn = lambda n: NB - 1 - n
    qspec = pl.BlockSpec((SGRP, WIN, HD), lambda hk, i: (2 + hk, rn(i), 0))
    cur = lambda off: pl.BlockSpec((None, WIN, HD), lambda hk, i, off=off: (off + hk, rn(i), 0))
    prev = lambda off: pl.BlockSpec((None, WIN, HD), lambda hk, i, off=off: (off + hk, jnp.maximum(rn(i) - 1, 0), 0))
    whole = pl.BlockSpec((1, HD), lambda hk, i: (0, 0))
    sspec = pl.BlockSpec((SGRP, WIN, 1), lambda hk, i: (hk, 0, 0))
    ospec = pl.BlockSpec((SGRP, WIN, HD), lambda hk, i: (hk, rn(i), 0))
    return pl.pallas_call(
        body, name="swa_bwd", grid=(SKVH, NB),
        in_specs=[qspec, prev(16), cur(16), prev(18), cur(18), whole, whole, sspec, ospec],
        out_specs=[ospec, cur(0), cur(0), whole, whole, sspec],
        out_shape=[_sds((SQH, T, HD)), _sds((SKVH, T, HD)), _sds((SKVH, T, HD)),
                   _sds((1, HD)), _sds((1, HD)), _sds((SQH, WIN, 1))],
        scratch_shapes=[pltpu.VMEM((WIN, HD), F32), pltpu.VMEM((WIN, HD), F32)],
        compiler_params=_cparams(("arbitrary", "arbitrary")),
    )(zs_hm, zs_hm, zs_hm, zs_hm, zs_hm, qnw, knw, sinks_col, do)


def _swa_heads(kit, first, q, kp, kc, vp, vc, qnw, knw, sink, slope):
    W = WIN
    ri = lax.broadcasted_iota(jnp.int32, (W, W), 0)
    ci = lax.broadcasted_iota(jnp.int32, (W, W), 1)
    mask_c = ri >= ci
    mask_p = ci > ri + first * W
    dist_c = (ri - ci).astype(F32)
    dist_p = (ri - ci + W).astype(F32)
    kpn = _rms(kp, knw)
    kcn = _rms(kc, knw)
    qn = _rms(q, qnw)
    sc = jnp.where(mask_c, kit.nt(qn, kcn) * (HD ** -0.5) - slope * dist_c, -1e30)
    sp = jnp.where(mask_p, kit.nt(qn, kpn) * (HD ** -0.5) - slope * dist_p, -1e30)
    m = jnp.maximum(jnp.maximum(jnp.max(sc, axis=-1, keepdims=True), jnp.max(sp, axis=-1, keepdims=True)), sink)
    m = lax.stop_gradient(m)
    pc = jnp.exp(sc - m)
    pp = jnp.exp(sp - m)
    den = jnp.sum(pc, axis=-1, keepdims=True) + jnp.sum(pp, axis=-1, keepdims=True) + jnp.exp(sink - m)
    inv = 1.0 / den
    return kit.nn(pc * inv, vc) + kit.nn(pp * inv, vp)


def _per_query_head(kv_ref):
    return jnp.concatenate([kv_ref[pl.ds(h // SGRP, 1)] for h in range(SQH)], axis=0)


def _per_kv_head(d):
    return jnp.concatenate([jnp.sum(d[g * SGRP:(g + 1) * SGRP], axis=0, keepdims=True) for g in range(SKVH)], axis=0)


def _swa_specs(blk):
    qspec = pl.BlockSpec((SQH, WIN, HD), lambda i: (1, blk(i), 0))
    cur = lambda grp: pl.BlockSpec((SKVH, WIN, HD), lambda i, grp=grp: (grp, blk(i), 0))
    prev = lambda grp: pl.BlockSpec((SKVH, WIN, HD), lambda i, grp=grp: (grp, jnp.maximum(blk(i) - 1, 0), 0))
    whole = pl.BlockSpec((1, HD), lambda i: (0, 0))
    col = pl.BlockSpec((SQH, WIN, 1), lambda i: (0, 0, 0))
    ospec = pl.BlockSpec((SQH, WIN, HD), lambda i: (0, blk(i), 0))
    return qspec, cur, prev, whole, col, ospec


def _swa_fwd(zs_hm, qnw, knw, sinks_col, slopes_col, o_buf, shards):
    T = zs_hm.shape[1]
    NB = T // WIN
    ns = len(shards)
    kit = _Kit(False)
    qspec, cur, prev, whole, col, _ = _swa_specs(lambda i: i)

    def body(*refs):
        q_ref, kp_ref, kc_ref, vp_ref, vc_ref, qnw_ref, knw_ref, s_ref, sl_ref = refs[:9]
        o_ref = refs[10 + ns]
        plan = _gather_plan(refs[10:10 + ns], refs[11 + ns:11 + 2 * ns], *refs[11 + 2 * ns:])

        @pl.when(pl.program_id(0) == 0)
        def _():
            _start(plan)

        first = (pl.program_id(0) == 0).astype(jnp.int32)
        o_ref[...] = _swa_heads(kit, first, q_ref[...], _per_query_head(kp_ref), _per_query_head(kc_ref),
                                _per_query_head(vp_ref), _per_query_head(vc_ref), qnw_ref[...], knw_ref[...],
                                s_ref[...], sl_ref[...])

        @pl.when(pl.program_id(0) == NB - 1)
        def _():
            _finish(plan)

    res = pl.pallas_call(
        body, name="swa_fwd", grid=(NB,),
        in_specs=[qspec, prev(8), cur(8), prev(9), cur(9), whole, whole, col, col] + _hbm_specs(1 + ns),
        out_specs=[pl.BlockSpec((SQH, WIN, HD), lambda i: (1, i, 0))] + _hbm_specs(ns),
        out_shape=[_sds(o_buf.shape)] + _gather_shapes(shards),
        input_output_aliases={9: 0},
        scratch_shapes=_gather_sems(ns),
        compiler_params=_cparams(("arbitrary",)),
    )(zs_hm, zs_hm, zs_hm, zs_hm, zs_hm, qnw, knw, sinks_col, slopes_col, o_buf, *shards)
    return res[0], res[1:]


SWA_GRAD_HEADS = SQH + 2 * SKVH


def _swa_bwd(zs_hm, qnw, knw, sinks_col, slopes_col, dmix_hm, d_buf):
    T = zs_hm.shape[1]
    NB = T // WIN
    kit = _Kit(True)
    qspec, cur, prev, whole, col, _ = _swa_specs(lambda i: NB - 1 - i)

    def body(q_ref, kp_ref, kc_ref, vp_ref, vc_ref, qnw_ref, knw_ref, s_ref, sl_ref, do_ref, buf_ref,
             d_ref, dqnw_ref, dknw_ref, ds_ref, ck_scr, cv_scr):
        dq_ref = d_ref.at[pl.ds(0, SQH)]
        dk_ref = d_ref.at[pl.ds(SQH, SKVH)]
        dv_ref = d_ref.at[pl.ds(SQH + SKVH, SKVH)]
        i = pl.program_id(0)
        first = (i == NB - 1).astype(jnp.int32)

        @pl.when(i == 0)
        def _():
            ck_scr[...] = jnp.zeros_like(ck_scr)
            cv_scr[...] = jnp.zeros_like(cv_scr)
            ds_ref[...] = jnp.zeros_like(ds_ref)
            dqnw_ref[...] = jnp.zeros_like(dqnw_ref)
            dknw_ref[...] = jnp.zeros_like(dknw_ref)

        fn = lambda q, kp, kc, vp, vc, qnw, knw, sink: _swa_heads(kit, first, q, kp, kc, vp, vc, qnw, knw, sink,
                                                                  sl_ref[...])
        _, vjp = jax.vjp(fn, q_ref[...], _per_query_head(kp_ref), _per_query_head(kc_ref), _per_query_head(vp_ref),
                         _per_query_head(vc_ref), qnw_ref[...], knw_ref[...], s_ref[...])
        dq, dkp, dkc, dvp, dvc, dqnw, dknw, dsink = vjp(do_ref[...])
        dq_ref[...] = dq
        dk_ref[...] = _per_kv_head(dkc) + ck_scr[...]
        dv_ref[...] = _per_kv_head(dvc) + cv_scr[...]
        ck_scr[...] = _per_kv_head(dkp)
        cv_scr[...] = _per_kv_head(dvp)
        dqnw_ref[...] += dqnw
        dknw_ref[...] += dknw
        ds_ref[...] += jnp.broadcast_to(jnp.sum(dsink, axis=1, keepdims=True), dsink.shape)

    dospec = pl.BlockSpec((SQH, WIN, HD), lambda i: (1, NB - 1 - i, 0))
    dspec = pl.BlockSpec((SWA_GRAD_HEADS, WIN, HD), lambda i: (1, NB - 1 - i, 0))
    res = pl.pallas_call(
        body, name="swa_bwd", grid=(NB,),
        in_specs=[qspec, prev(8), cur(8), prev(9), cur(9), whole, whole, col, col, dospec] + _hbm_specs(1),
        out_specs=[dspec, whole, whole, col],
        out_shape=[_sds(d_buf.shape), _sds((1, HD)), _sds((1, HD)), _sds((SQH, WIN, 1))],
        input_output_aliases={10: 0},
        scratch_shapes=[pltpu.VMEM((SKVH, WIN, HD), F32), pltpu.VMEM((SKVH, WIN, HD), F32)],
        compiler_params=_cparams(("arbitrary",)),
    )(zs_hm, zs_hm, zs_hm, zs_hm, zs_hm, qnw, knw, sinks_col, slopes_col, dmix_hm, d_buf)
    return res


GAB0 = 3 * GW + 1280


def _permute_w_in(w_in):
    return jnp.concatenate([w_in[:, :4 * GW], w_in[:, 4 * GW + 2 * GH:], w_in[:, 4 * GW:4 * GW + 2 * GH],
                            jnp.zeros((D, NP - PROJ), w_in.dtype)], axis=1)


def _unpermute_w_in(g):
    return jnp.concatenate([g[:, :4 * GW], g[:, GAB0:GAB0 + 2 * GH], g[:, 4 * GW:GAB0]], axis=1)


def _pieces_by_cols(g):
    return g.reshape(D, N_CHIP, -1).transpose(1, 0, 2).reshape(N_CHIP, 2, D // 2, -1)


def _pieces_by_rows(g):
    return g.reshape(N_CHIP, 2, g.shape[0] // (2 * N_CHIP), D)


def _local_step(x, target, mod, n1w, w_in_p, conv_w, alog, dtb, gnw, qnw, knw, sinks, n2w, shards):
    sh_out, sh_gate, sh_up, sh_down = shards
    T = x.shape[0]
    N = T // CHUNK
    shift1, scale1, gate1, shift2, scale2, gate2 = [mod[:, i * D:(i + 1) * D] for i in range(6)]

    h = _norm_mod_fwd(x, n1w, scale1, shift1)
    proj, (a_out,) = _matmul(h, w_in_p, name="in_proj", gather=[sh_out])
    w_out = a_out.reshape(D, D)
    qkv_hm = _conv_fwd(proj, conv_w)
    zs_hm = _split_heads(proj, 3 * GW // LANE, 20, "split_zs")
    gab = proj[:, GAB0:GAB0 + 2 * GH].T.reshape(2 * GH, N, 1, CHUNK)
    alog_b = jnp.broadcast_to(alog.reshape(GH, 1, 1), (GH, 1, CHUNK))
    dtb_b = jnp.broadcast_to(dtb.reshape(GH, 1, 1), (GH, 1, CHUNK))
    sinks_col = jnp.broadcast_to(sinks.reshape(SQH, 1, 1), (SQH, WIN, 1))
    o_hm, S_all, (a_gate, a_up) = _gdn_fwd(qkv_hm, zs_hm, gab, alog_b, dtb_b, gnw, [sh_gate, sh_up])
    w_gu = jnp.concatenate([a_gate[j] for j in range(N_CHIP)] + [a_up[j] for j in range(N_CHIP)], axis=1)
    slopes = 2.0 ** (-8.0 * (jnp.arange(SQH, dtype=F32) + 1.0) / SQH)
    slopes_col = jnp.broadcast_to(slopes.reshape(SQH, 1, 1), (SQH, WIN, 1))
    o_hm, (a_down,) = _swa_fwd(zs_hm, qnw, knw, sinks_col, slopes_col, o_hm, [sh_down])
    w_down = a_down.reshape(DFF, D)
    mixcat = _merge_heads(o_hm, BF16, "merge_mix")
    mixed = _matmul(mixcat, w_out, name="out_proj")
    x1, h2 = _resid_norm_fwd(x, mixed, gate1, n2w, scale2, shift2)
    ab = _matmul(h2, w_gu, name="ffn_up")
    act = _ffn_act_fwd(ab)
    ffn = _matmul(act, w_down, name="ffn_down")
    dy, dffn, dgate2, loss = _loss_head(x1, ffn, target, gate2)

    dact = _matmul(dffn, w_down, tb=True, name="ffn_down_dx")
    dab = _ffn_act_bwd(ab, dact)
    g_w_down = _matmul(act, dffn, ta=True, out_dtype=BF16, name="ffn_down_dw")
    g_w_gu = _matmul(h2, dab, ta=True, out_dtype=BF16, name="ffn_up_dw")
    dh2 = _matmul(dab, w_gu, tb=True, name="ffn_up_dx")
    dx1, dmixed, dgate1, dn2w, dscale2, dshift2 = _resid_norm_bwd(x, mixed, dy, dh2, gate1, n2w, scale2, shift2)
    g_w_out = _matmul(mixcat, dmixed, ta=True, out_dtype=BF16, name="out_proj_dw")
    dmix_hm = _split_heads(_matmul(dmixed, w_out, tb=True, name="out_proj_dx"), 0, GH + SQH, "split_dmix")
    gu_pieces = g_w_gu.reshape(D, 2 * N_CHIP, -1).transpose(1, 0, 2).reshape(2, N_CHIP, 2, D // 2, -1)
    pieces = [_pieces_by_rows(g_w_out), gu_pieces[0], gu_pieces[1], _pieces_by_rows(g_w_down)]
    (dqkv_hm, d_hm, dga, dgb, dalog, ddtb, dgnw), recv = _gdn_bwd(qkv_hm, zs_hm, gab, alog_b, dtb_b, gnw, S_all,
                                                                  dmix_hm, pieces)
    d_hm, dqnw, dknw, dsinks = _swa_bwd(zs_hm, qnw, knw, sinks_col, slopes_col, dmix_hm, d_hm)
    dproj, dconv = _conv_bwd(proj, conv_w, dqkv_hm)
    dproj = _merge_heads(d_hm, BF16, "merge_dz", into=dproj, col_block0=3 * GW // LANE, head0=0, nheads=GH)
    dproj = _merge_heads(d_hm, BF16, "merge_dswa", into=dproj, col_block0=4 * GW // LANE, head0=GH + 4,
                         nheads=SWA_GRAD_HEADS)
    dgab = jnp.concatenate([dga, dgb], axis=0).reshape(2 * GH, T).T.astype(BF16)
    dproj = lax.dynamic_update_slice(dproj, jnp.concatenate([dgab, jnp.zeros((T, NP - PROJ), BF16)], axis=1),
                                     (0, GAB0))
    g_w_in_p = _matmul(h, dproj, ta=True, out_dtype=BF16, name="in_proj_dw")
    dh, recv_in = _matmul(dproj, w_in_p, tb=True, name="in_proj_dx",
                          exchange=[_pieces_by_cols(_unpermute_w_in(g_w_in_p))])
    grad_x, dn1w, dscale1, dshift1 = _norm_mod_bwd(x, dh, dx1, n1w, scale1, shift1)

    dmod = jnp.concatenate([dshift1, dscale1, dgate1, dshift2, dscale2, dgate2], axis=1)
    big = list(recv_in) + list(recv)
    small = dict(mod=dmod, norm1_w=dn1w, norm2_w=dn2w, conv_w=dconv, a_log=dalog[:, 0, 0], dt_bias=ddtb[:, 0, 0],
                 gdn_norm_w=dgnw, q_norm_w=dqnw, k_norm_w=dknw, sinks=dsinks[:, 0, 0])
    return loss, grad_x, big, small


def _adamw(w, g, m, v):
    m2 = ADAM_B1 * m + (1.0 - ADAM_B1) * g
    v2 = ADAM_B2 * v + (1.0 - ADAM_B2) * (g * g)
    m_hat = m2 / (1.0 - ADAM_B1 ** ADAM_STEP)
    v_hat = v2 / (1.0 - ADAM_B2 ** ADAM_STEP)
    delta = -ADAM_LR * (m_hat / (jnp.sqrt(v_hat) + ADAM_EPS) + ADAM_WD * w)
    return delta, m2, v2


def _reduce_adamw(recv, w, m, v, name):
    _, R, C = recv.shape
    tc = _tile(C, 256)

    def body(r_ref, w_ref, m_ref, v_ref, o_ref):
        g = r_ref[0].astype(F32)
        for s in range(1, N_DEV):
            g = g + r_ref[s].astype(F32)
        delta, m2, v2 = _adamw(w_ref[...], g, m_ref[...], v_ref[...])
        o_ref[0] = g
        o_ref[1] = delta
        o_ref[2] = m2
        o_ref[3] = v2

    col = pl.BlockSpec((R, tc), lambda j: (0, j))
    return pl.pallas_call(
        body, name=name, grid=(C // tc,),
        in_specs=[pl.BlockSpec((N_DEV, R, tc), lambda j: (0, 0, j)), col, col, col],
        out_specs=pl.BlockSpec((4, R, tc), lambda j: (0, 0, j)),
        out_shape=_sds((4, R, C)),
        compiler_params=_cparams(("parallel",)),
    )(recv, w, m, v)


def _adamw_call(g, w, m, v, name):
    def body(g_ref, w_ref, m_ref, v_ref, o_ref):
        delta, m2, v2 = _adamw(w_ref[...], g_ref[...], m_ref[...], v_ref[...])
        o_ref[0] = delta
        o_ref[1] = m2
        o_ref[2] = v2

    return pl.pallas_call(body, name=name, out_shape=_sds((3,) + g.shape))(g, w, m, v)


ADA_N = 6 * D // N_CHIP
KPAD = 128


def _mod_part(c8, w_ada, b_ada):
    tn = 512

    def body(c_ref, w_ref, b_ref, o_ref):
        o_ref[...] = _raw1(_silu(c_ref[...]), w_ref[...], _NN) + b_ref[...]

    return pl.pallas_call(
        body, name="ada_mod", grid=(ADA_N // tn,),
        in_specs=[pl.BlockSpec((16, D), lambda j: (0, 0)), pl.BlockSpec((D, tn), lambda j: (0, j)),
                  pl.BlockSpec((1, tn), lambda j: (0, j))],
        out_specs=pl.BlockSpec((16, tn), lambda j: (0, j)),
        out_shape=_sds((16, ADA_N)),
        compiler_params=_cparams(("parallel",)),
    )(c8, w_ada, b_ada)


def _w_ada_update(c8p, dm, w, m, v):
    tr = 256

    def body(c_ref, dm_ref, w_ref, m_ref, v_ref, g_ref, d_ref, m2_ref, v2_ref):
        g = _raw1(_silu(c_ref[...]), dm_ref[...], _TN)
        delta, m2, v2 = _adamw(w_ref[...], g, m_ref[...], v_ref[...])
        g_ref[...] = g
        d_ref[...] = delta
        m2_ref[...] = m2
        v2_ref[...] = v2

    blk = pl.BlockSpec((tr, ADA_N), lambda i: (i, 0))
    return pl.pallas_call(
        body, name="w_ada_update", grid=(D // tr,),
        in_specs=[pl.BlockSpec((KPAD, tr), lambda i: (0, i)), pl.BlockSpec((KPAD, ADA_N), lambda i: (0, 0)),
                  blk, blk, blk],
        out_specs=[blk] * 4, out_shape=[_sds((D, ADA_N))] * 4,
        compiler_params=_cparams(("parallel",)),
    )(c8p, dm, w, m, v)


def _me():
    return lax.axis_index("x"), lax.axis_index("y"), lax.axis_index("c")


def _peer(k, me):
    mx, my, mc = me
    return (1 - mx if k & 4 else mx, 1 - my if k & 2 else my, 1 - mc if k & 1 else mc)


def _lin(p):
    return 4 * p[0] + 2 * p[1] + p[2]


def _remote(src, dst, ssem, rsem, dev):
    return pltpu.make_async_remote_copy(src_ref=src, dst_ref=dst, send_sem=ssem, recv_sem=rsem,
                                        device_id=dev, device_id_type=MESH)


def _all_gather8(x, name):
    def body(x_ref, out_ref, send_sems, recv_sems):
        me = _me()
        out_ref[_lin(me)] = x_ref[...]
        sends = []
        for k in range(1, N_DEV):
            cp = _remote(x_ref, out_ref.at[_lin(me)], send_sems.at[k - 1], recv_sems.at[k - 1], _peer(k, me))
            cp.start()
            sends.append(cp)
        for k in range(1, N_DEV):
            p = _peer(k, me)
            _remote(x_ref, out_ref.at[_lin(p)], send_sems.at[k - 1], recv_sems.at[k - 1], p).wait_recv()
        for cp in sends:
            cp.wait_send()

    return pl.pallas_call(
        body, name=name,
        out_shape=_sds((N_DEV,) + x.shape, x.dtype),
        in_specs=[pl.BlockSpec(memory_space=pltpu.VMEM)],
        out_specs=pl.BlockSpec(memory_space=pltpu.VMEM),
        scratch_shapes=[pltpu.SemaphoreType.DMA((N_DEV - 1,)), pltpu.SemaphoreType.DMA((N_DEV - 1,))],
    )(x)


def _hbm_specs(n):
    return [pl.BlockSpec(memory_space=pl.ANY)] * n


def _gather_weights(shards):
    n = len(shards)

    def body(*refs):
        plan = _gather_plan(refs[:n], refs[n:2 * n], *refs[2 * n:])
        _start(plan)
        _finish(plan)

    return pl.pallas_call(
        body, name="gather_weights",
        out_shape=_gather_shapes(shards), in_specs=_hbm_specs(n), out_specs=_hbm_specs(n),
        scratch_shapes=_gather_sems(n),
    )(*shards)


def _gather_shapes(shards):
    return [_sds((N_CHIP,) + s.shape, s.dtype) for s in shards]


def _gather_sems(n):
    return [pltpu.SemaphoreType.DMA((3 * n,)), pltpu.SemaphoreType.DMA((3 * n,)), pltpu.SemaphoreType.DMA((n,))]


def _gather_plan(ins, outs, send_sems, recv_sems, local_sems):
    mx, my, mc = _me()
    chips = [(1 - mx, my), (mx, 1 - my), (1 - mx, 1 - my)]
    local, sends, recvs = [], [], []
    for a in range(len(ins)):
        local.append(pltpu.make_async_copy(ins[a], outs[a].at[2 * mx + my], local_sems.at[a]))
        for k, (px, py) in enumerate(chips):
            sems = (send_sems.at[3 * a + k], recv_sems.at[3 * a + k], (px, py, mc))
            sends.append(_remote(ins[a], outs[a].at[2 * mx + my], *sems))
            recvs.append(_remote(ins[a], outs[a].at[2 * px + py], *sems))
    return local, sends, recvs


def _start(plan):
    local, sends, _ = plan
    for cp in local + sends:
        cp.start()


def _finish(plan):
    local, sends, recvs = plan
    for cp in recvs:
        cp.wait_recv()
    for cp in sends:
        cp.wait_send()
    for cp in local:
        cp.wait()


def _grad_exchange(pieces):
    n = len(pieces)

    def body(*refs):
        plan = _exchange_plan(refs[:n], refs[n:2 * n], *refs[2 * n:])
        _start(plan)
        _finish(plan)

    return pl.pallas_call(
        body, name="grad_exchange",
        out_shape=_exchange_shapes(pieces), in_specs=_hbm_specs(n), out_specs=_hbm_specs(n),
        scratch_shapes=_exchange_sems(n),
    )(*pieces)


def _exchange_shapes(pieces):
    return [_sds((N_DEV,) + p.shape[2:], p.dtype) for p in pieces]


def _exchange_sems(n):
    return [pltpu.SemaphoreType.DMA(((N_DEV - 1) * n,)), pltpu.SemaphoreType.DMA(((N_DEV - 1) * n,)),
            pltpu.SemaphoreType.DMA((n,))]


def _exchange_plan(ins, outs, send_sems, recv_sems, local_sems):
    me = _me()
    mx, my, mc = me
    local, sends, recvs = [], [], []
    for a in range(len(ins)):
        local.append(pltpu.make_async_copy(ins[a].at[2 * mx + my, mc], outs[a].at[_lin(me)], local_sems.at[a]))
        for k in range(1, N_DEV):
            p = _peer(k, me)
            s = (N_DEV - 1) * a + k - 1
            sends.append(_remote(ins[a].at[2 * p[0] + p[1], p[2]], outs[a].at[_lin(me)], send_sems.at[s],
                                 recv_sems.at[s], p))
            recvs.append(_remote(ins[a].at[2 * mx + my, mc], outs[a].at[_lin(p)], send_sems.at[s],
                                 recv_sems.at[s], p))
    return local, sends, recvs


def _reduce_swap(recv, name):
    _, rows, cols = recv.shape

    def body(r_ref, o_ref, send_sem, recv_sem):
        mx, my, mc = _me()
        sib = (mx, my, 1 - mc)
        g = r_ref[0].astype(F32)
        for s in range(1, N_DEV):
            g = g + r_ref[s].astype(F32)
        mine = o_ref.at[pl.ds(pl.multiple_of(mc * rows, 8), rows)]
        theirs = o_ref.at[pl.ds(pl.multiple_of((1 - mc) * rows, 8), rows)]
        mine[...] = g
        cp = _remote(mine, mine, send_sem, recv_sem, sib)
        cp.start()
        _remote(mine, theirs, send_sem, recv_sem, sib).wait_recv()
        cp.wait_send()

    return pl.pallas_call(
        body, name=name, out_shape=_sds((2 * rows, cols)),
        in_specs=[pl.BlockSpec(memory_space=pltpu.VMEM)], out_specs=pl.BlockSpec(memory_space=pltpu.VMEM),
        scratch_shapes=[pltpu.SemaphoreType.DMA, pltpu.SemaphoreType.DMA],
        compiler_params=_cparams(),
    )(recv)


def _adamw_big(g, w, m, v, name):
    rows, cols = g.shape
    tr = next(t for t in (256, 176, 128, 64, 8) if rows % t == 0)

    def body(g_ref, w_ref, m_ref, v_ref, go_ref, d_ref, m2_ref, v2_ref):
        g = g_ref[...]
        delta, m2, v2 = _adamw(w_ref[...], g, m_ref[...], v_ref[...])
        go_ref[...] = g
        d_ref[...] = delta
        m2_ref[...] = m2
        v2_ref[...] = v2

    blk = pl.BlockSpec((tr, cols), lambda i: (i, 0))
    return pl.pallas_call(
        body, name=name, grid=(rows // tr,),
        in_specs=[blk] * 4, out_specs=[blk] * 4, out_shape=[_sds((rows, cols))] * 4,
        compiler_params=_cparams(("parallel",)),
    )(g, w, m, v)


SMALL_ORDER = (("mod", 6 * D), ("norm1_w", D), ("norm2_w", D), ("conv_w", CONVW * 3 * GW), ("a_log", GH),
               ("dt_bias", GH), ("gdn_norm_w", HD), ("q_norm_w", HD), ("k_norm_w", HD), ("sinks", SQH), ("loss", 1))
SMALL_R = 120


def _pack_small(d):
    parts = [d[k].reshape(-1).astype(F32) if k in d else jnp.zeros((n,), F32) for k, n in SMALL_ORDER]
    used = sum(n for _, n in SMALL_ORDER)
    parts.append(jnp.zeros((SMALL_R * LANE - used,), F32))
    return jnp.concatenate(parts).reshape(SMALL_R, LANE)


def _unpack_small(pk):
    flat = pk.reshape(-1)
    out, r = {}, 0
    for k, n in SMALL_ORDER:
        out[k] = flat[r:r + n]
        r += n
    return out


def kernel(x, c, w_ada, b_ada, norm1_w, w_in, conv_w, a_log, dt_bias, gdn_norm_w, q_norm_w, k_norm_w, sinks, w_out, norm2_w, w_gate, w_up, w_down, loss_target, m_w_ada, m_b_ada, m_norm1_w, m_w_in, m_conv_w, m_a_log, m_dt_bias, m_gdn_norm_w, m_q_norm_w, m_k_norm_w, m_sinks, m_w_out, m_norm2_w, m_w_gate, m_w_up, m_w_down, v_w_ada, v_b_ada, v_norm1_w, v_w_in, v_conv_w, v_a_log, v_dt_bias, v_gdn_norm_w, v_q_norm_w, v_k_norm_w, v_sinks, v_w_out, v_norm2_w, v_w_gate, v_w_up, v_w_down):
    mx, my, mc = _me()
    chip = 2 * mx + my
    dev = 4 * mx + 2 * my + mc
    T = x.shape[1]

    conv_sh = conv_w.reshape(CONVW, 3 * GW // N_CHIP)
    mine = jnp.concatenate([c.reshape(-1), conv_sh.reshape(-1), jnp.zeros((4 * LANE,), F32)]).reshape(24, LANE)
    got = _all_gather8(mine, "gather_c_conv")
    c8 = got[:, :8].reshape(N_DEV, D)
    conv_full = jnp.concatenate([got[2 * j, 8:20].reshape(CONVW, 3 * GW // N_CHIP) for j in range(N_CHIP)], axis=1)
    c16 = jnp.concatenate([c8, jnp.zeros((8, D), F32)], axis=0)
    b_sh = lax.dynamic_slice(b_ada, (0, chip * ADA_N), (1, ADA_N))
    mods = _all_gather8(_mod_part(c16, w_ada[0], b_sh), "gather_mod")
    mod = jnp.concatenate([lax.dynamic_slice(mods[2 * j], (dev, 0), (1, ADA_N)) for j in range(N_CHIP)], axis=1)

    big_w = (w_in, w_out, w_gate, w_up, w_down)
    shards = [t[0].astype(BF16) for t in big_w]
    (a_in,) = _gather_weights(shards[:1])
    w_in_f = jnp.concatenate([a_in[j] for j in range(N_CHIP)], axis=1)

    loss, grad_x, big, small = _local_step(
        x[0], loss_target[0], mod, norm1_w, _permute_w_in(w_in_f), conv_full, a_log, dt_bias, gdn_norm_w,
        q_norm_w, k_norm_w, sinks, norm2_w, shards[1:])

    small["loss"] = loss[:, :1]
    sg = _all_gather8(_pack_small(small), "gather_small_grads")
    rep = dict(mod=(b_ada, m_b_ada, v_b_ada), norm1_w=(norm1_w, m_norm1_w, v_norm1_w),
               norm2_w=(norm2_w, m_norm2_w, v_norm2_w), a_log=(a_log, m_a_log, v_a_log),
               dt_bias=(dt_bias, m_dt_bias, v_dt_bias), gdn_norm_w=(gdn_norm_w, m_gdn_norm_w, v_gdn_norm_w),
               q_norm_w=(q_norm_w, m_q_norm_w, v_q_norm_w), k_norm_w=(k_norm_w, m_k_norm_w, v_k_norm_w),
               sinks=(sinks, m_sinks, v_sinks))
    wmv = [_pack_small({k: t[i] for k, t in rep.items()}) for i in range(3)]
    sres = _reduce_adamw(sg, wmv[0], wmv[1], wmv[2], "small_reduce_adamw")
    s_g, s_d, s_m, s_v = [_unpack_small(sres[i]) for i in range(4)]
    loss_out = s_g["loss"][0]

    g_conv = lax.dynamic_slice(s_g["conv_w"].reshape(CONVW, 3 * GW), (0, chip * (3 * GW // N_CHIP)),
                               (CONVW, 3 * GW // N_CHIP))
    pad16 = lambda t: jnp.concatenate([t.reshape(12, LANE), jnp.zeros((4, LANE), F32)], axis=0)
    cres = _adamw_call(pad16(g_conv), pad16(conv_w), pad16(m_conv_w), pad16(v_conv_w), "conv_adamw")
    conv_out = [g_conv.reshape(conv_w.shape)] + [cres[i, :12].reshape(conv_w.shape) for i in range(3)]

    dmod8 = sg[:, :6 * D // LANE].reshape(N_DEV, 6 * D)
    dm = lax.dynamic_slice(dmod8, (0, chip * ADA_N), (N_DEV, ADA_N))
    zpad = lambda t: jnp.concatenate([t, jnp.zeros((KPAD - N_DEV, t.shape[1]), F32)], axis=0)
    ares = _w_ada_update(zpad(c8), zpad(dm), w_ada[0], m_w_ada[0], v_w_ada[0])

    names = ("w_in", "w_out", "w_gate", "w_up", "w_down")
    g_full = [_reduce_swap(r, "reduce_" + nm) for r, nm in zip(big, names)]
    big_m = (m_w_in, m_w_out, m_w_gate, m_w_up, m_w_down)
    big_v = (v_w_in, v_w_out, v_w_gate, v_w_up, v_w_down)
    upd = [_adamw_big(g, w[0], m[0], v[0], "adamw_" + nm)
           for g, w, m, v, nm in zip(g_full, big_w, big_m, big_v, names)]
    bg, bd, bm, bv = [[u[i][None] for u in upd] for i in range(4)]

    def group(a_i, small_d, conv_i, big_l):
        s = lambda k, ref: small_d[k].reshape(ref.shape)
        return [ares[a_i][None], s("mod", b_ada), s("norm1_w", norm1_w), big_l[0], conv_out[conv_i],
                s("a_log", a_log), s("dt_bias", dt_bias), s("gdn_norm_w", gdn_norm_w), s("q_norm_w", q_norm_w),
                s("k_norm_w", k_norm_w), s("sinks", sinks), big_l[1], s("norm2_w", norm2_w), big_l[2], big_l[3],
                big_l[4]]

    outs = [loss_out, grad_x[None]]
    outs += group(0, s_g, 0, bg) + group(1, s_d, 1, bd) + group(2, s_m, 2, bm) + group(3, s_v, 3, bv)
    return tuple(outs)
```

```python
import functools

import jax
import jax.numpy as jnp
from jax import lax
from jax.experimental import pallas as pl
from jax.experimental.pallas import tpu as pltpu

F32 = jnp.float32
BF16 = jnp.bfloat16
MESH = pl.DeviceIdType.MESH

D = 1024
HD = 64
GH = 8
GW = GH * HD
SQH = 8
SKVH = 2
SGRP = SQH // SKVH
WIN = 128
CONVW = 4
CHUNK = 64
DFF = 2816
PROJ = 2832
NP = 3072
EPS = 1e-6
N_DEV = 8
N_CHIP = 4

ADAM_LR = 0.001
ADAM_B1 = 0.9
ADAM_B2 = 0.999
ADAM_EPS = 1e-08
ADAM_WD = 0.01
ADAM_STEP = 10

VMEM_LIMIT = 48 * 1024 * 1024
GDN_BWD_VMEM = 58 * 1024 * 1024
LANE = 128

PACK_ROWS = (PROJ // N_CHIP, D // N_CHIP, DFF // N_CHIP, DFF // N_CHIP, DFF // N_CHIP)
PACK_P = 3104
PACK_H = PACK_P // 2


def _cparams(sem=None, vmem=VMEM_LIMIT):
    return pltpu.CompilerParams(dimension_semantics=sem, vmem_limit_bytes=vmem)


_NN = ((1,), (0,))
_NT = ((1,), (1,))
_TN = ((0,), (0,))


def _dot(a, b, dims):
    if a.ndim == 3:
        (ca,), (cb,) = dims
        return lax.dot_general(a, b, (((ca + 1,), (cb + 1,)), ((0,), (0,))), preferred_element_type=F32)
    return lax.dot_general(a, b, (dims, ((), ())), preferred_element_type=F32)


def _raw1(a, b, dims):
    return _dot(a.astype(BF16), b.astype(BF16), dims)


def _raw3(a, b, dims):
    ah = a.astype(BF16)
    al = (a - ah.astype(F32)).astype(BF16)
    bh = b.astype(BF16)
    bl = (b - bh.astype(F32)).astype(BF16)
    return _dot(ah, bh, dims) + (_dot(al, bh, dims) + _dot(ah, bl, dims))


def _make_diff_mm(raw):
    @jax.custom_vjp
    def nn(a, b):
        return raw(a, b, _NN)

    @jax.custom_vjp
    def nt(a, b):
        return raw(a, b, _NT)

    @jax.custom_vjp
    def tn(a, b):
        return raw(a, b, _TN)

    nn.defvjp(lambda a, b: (raw(a, b, _NN), (a, b)), lambda r, g: (nt(g, r[1]), tn(r[0], g)))
    nt.defvjp(lambda a, b: (raw(a, b, _NT), (a, b)), lambda r, g: (nn(g, r[1]), tn(g, r[0])))
    tn.defvjp(lambda a, b: (raw(a, b, _TN), (a, b)), lambda r, g: (nt(r[1], g), nn(r[0], g)))
    return nn, nt, tn


def _tri_inv_raw(a, nn3):
    n = a.shape[-1]
    ri = lax.broadcasted_iota(jnp.int32, (n, n), 0)
    ci = lax.broadcasted_iota(jnp.int32, (n, n), 1)
    t = (ri == ci).astype(F32)
    for lvl in range((n - 1).bit_length()):
        same_pair = (ri >> (lvl + 1)) == (ci >> (lvl + 1))
        lower_left = (((ri >> lvl) & 1) == 1) & (((ci >> lvl) & 1) == 0)
        y = jnp.where(same_pair & lower_left, a, 0.0)
        t = t - y if lvl == 0 else t - nn3(nn3(t, y), t)
    return t


class _Kit:
    def __init__(self, diff):
        if diff:
            self.nn, self.nt, self.tn = _make_diff_mm(_raw1)
            self.nn3, self.nt3, self.tn3 = _make_diff_mm(_raw3)
            nn3, nt3, tn3 = self.nn3, self.nt3, self.tn3

            @jax.custom_vjp
            def inv(a, t):
                return t

            def inv_fwd(a, t):
                return t, t

            def inv_bwd(t, g):
                return -tn3(t, nt3(g, t)), jnp.zeros_like(t)

            inv.defvjp(inv_fwd, inv_bwd)
            self.inv = inv
        else:
            self.nn = lambda a, b: _raw1(a, b, _NN)
            self.nt = lambda a, b: _raw1(a, b, _NT)
            self.tn = lambda a, b: _raw1(a, b, _TN)
            self.nn3 = lambda a, b: _raw3(a, b, _NN)
            self.nt3 = lambda a, b: _raw3(a, b, _NT)
            self.tn3 = lambda a, b: _raw3(a, b, _TN)
            self.inv = lambda a, t: _tri_inv_raw(a, self.nn3) if t is None else t


def _sigmoid(x):
    return 1.0 / (1.0 + jnp.exp(-x))


def _silu(x):
    return x * _sigmoid(x)


def _rms(x, w):
    return x * lax.rsqrt(jnp.mean(x * x, axis=-1, keepdims=True) + EPS) * w


def _tile(dim, target):
    t = (min(dim, target) // LANE) * LANE
    while t >= LANE:
        if dim % t == 0:
            return t
        t -= LANE
    return dim


MM_TM, MM_TN, MM_TK = 1408, 1536, 1408


def _matmul(a, b, ta=False, tb=False, out_dtype=F32, name="matmul", gather=None, exchange=None):
    carried = gather if gather is not None else exchange if exchange is not None else []
    nc = len(carried)
    if ta:
        K, M = a.shape
    else:
        M, K = a.shape
    if tb:
        N, K2 = b.shape
    else:
        K2, N = b.shape
    assert K == K2, (a.shape, b.shape, ta, tb)
    tm, tn, tk = _tile(M, MM_TM), _tile(N, MM_TN), _tile(K, MM_TK)
    nk = K // tk
    dims = ((0,) if ta else (1,), (1,) if tb else (0,))

    grid = (M // tm, N // tn, nk)

    def body(*refs):
        a_ref, b_ref = refs[:2]
        o_ref = refs[2 + nc]
        scratch = refs[3 + 2 * nc:]
        k = pl.program_id(2)
        if nc:
            make_plan = _gather_plan if gather is not None else _exchange_plan
            plan = make_plan(refs[2:2 + nc], refs[3 + nc:3 + 2 * nc], *scratch[-3:])
            at = lambda pos: ((pl.program_id(0) == pos[0]) & (pl.program_id(1) == pos[1]) & (k == pos[2]))

            @pl.when(at((0, 0, 0)))
            def _():
                _start(plan)

        part = _dot(a_ref[...].astype(BF16), b_ref[...].astype(BF16), dims)
        if nk == 1:
            o_ref[...] = part.astype(o_ref.dtype)
        else:
            acc_ref = scratch[0]

            @pl.when(k == 0)
            def _():
                acc_ref[...] = part

            @pl.when((k > 0) & (k < nk - 1))
            def _():
                acc_ref[...] += part

            @pl.when(k == nk - 1)
            def _():
                o_ref[...] = (acc_ref[...] + part).astype(o_ref.dtype)

        if nc:
            @pl.when(at((grid[0] - 1, grid[1] - 1, nk - 1)))
            def _():
                _finish(plan)

    a_spec = (pl.BlockSpec((tk, tm), lambda i, j, k: (k, i)) if ta
              else pl.BlockSpec((tm, tk), lambda i, j, k: (i, k)))
    b_spec = (pl.BlockSpec((tn, tk), lambda i, j, k: (j, k)) if tb
              else pl.BlockSpec((tk, tn), lambda i, j, k: (k, j)))
    if gather is not None:
        c_shapes, c_sems = _gather_shapes(carried), _gather_sems(nc)
    elif exchange is not None:
        c_shapes, c_sems = _exchange_shapes(carried), _exchange_sems(nc)
    else:
        c_shapes, c_sems = [], []
    res = pl.pallas_call(
        body, name=name, grid=grid,
        in_specs=[a_spec, b_spec] + _hbm_specs(nc),
        out_specs=[pl.BlockSpec((tm, tn), lambda i, j, k: (i, j))] + _hbm_specs(nc),
        out_shape=[jax.ShapeDtypeStruct((M, N), out_dtype)] + c_shapes,
        scratch_shapes=([pltpu.VMEM((tm, tn), F32)] if nk > 1 else []) + c_sems,
        compiler_params=_cparams(("arbitrary",) * 3 if nc else ("parallel", "parallel", "arbitrary")),
    )(a, b, *carried)
    return (res[0], res[1:]) if nc else res[0]


def _rowcall(fn, tiled, consts, out_tiled, out_acc, tm, name):
    T = tiled[0].shape[0]
    n_in = len(tiled) + len(consts)
    n_o = len(out_tiled)

    def body(*refs):
        vals = [r[...] for r in refs[:n_in]]
        outs = refs[n_in:]
        res = fn(*vals)
        for r, v in zip(outs[:n_o], res[:n_o]):
            r[...] = v.astype(r.dtype)
        if len(outs) > n_o:
            @pl.when(pl.program_id(0) == 0)
            def _():
                for r in outs[n_o:]:
                    r[...] = jnp.zeros_like(r)

            for r, v in zip(outs[n_o:], res[n_o:]):
                r[...] += v

    in_specs = [pl.BlockSpec((tm, a.shape[1]), lambda i: (i, 0)) for a in tiled]
    in_specs += [pl.BlockSpec(a.shape, lambda i, nd=a.ndim: (0,) * nd) for a in consts]
    out_specs = [pl.BlockSpec((tm, s.shape[1]), lambda i: (i, 0)) for s in out_tiled]
    out_specs += [pl.BlockSpec(s.shape, lambda i: (0, 0)) for s in out_acc]
    return pl.pallas_call(
        body, name=name, grid=(T // tm,),
        in_specs=in_specs, out_specs=out_specs,
        out_shape=list(out_tiled) + list(out_acc),
        compiler_params=_cparams(("arbitrary",)),
    )(*tiled, *consts)


def _sds(shape, dtype=F32):
    return jax.ShapeDtypeStruct(shape, dtype)


def _norm_mod(x, nw, scale, shift):
    return _rms(x, nw) * (1.0 + scale) + shift


def _norm_mod_fwd(x, nw, scale, shift):
    T = x.shape[0]
    (h,) = _rowcall(lambda *a: (_norm_mod(*a),), [x], [nw, scale, shift],
                    [_sds((T, D), BF16)], [], 512, "norm1_fwd")
    return h


def _norm_mod_bwd(x, dh, dres, nw, scale, shift):
    T = x.shape[0]

    def fn(x, dh, dres, nw, scale, shift):
        _, vjp = jax.vjp(_norm_mod, x, nw, scale, shift)
        dx, dnw, dsc, dsh = vjp(dh)
        return dx + dres, dnw, dsc, dsh

    return _rowcall(fn, [x, dh, dres], [nw, scale, shift], [_sds((T, D))],
                    [_sds((1, D))] * 3, 256, "norm1_bwd")


def _resid_norm(x, mixed, gate1, nw, scale, shift):
    x1 = x + gate1 * mixed
    return x1, _norm_mod(x1, nw, scale, shift)


def _resid_norm_fwd(x, mixed, gate1, nw, scale, shift):
    T = x.shape[0]
    return _rowcall(_resid_norm, [x, mixed], [gate1, nw, scale, shift],
                    [_sds((T, D)), _sds((T, D), BF16)], [], 512, "resid_norm2_fwd")


def _resid_norm_bwd(x, mixed, dy, dh2, gate1, nw, scale, shift):
    T = x.shape[0]

    def fn(x, mixed, dy, dh2, gate1, nw, scale, shift):
        _, vjp = jax.vjp(_resid_norm, x, mixed, gate1, nw, scale, shift)
        dx, dmixed, dg1, dnw, dsc, dsh = vjp((dy, dh2))
        return dx, dmixed, dg1, dnw, dsc, dsh

    return _rowcall(fn, [x, mixed, dy, dh2], [gate1, nw, scale, shift],
                    [_sds((T, D)), _sds((T, D), BF16)], [_sds((1, D))] * 4, 256, "resid_norm2_bwd")


def _ffn_act_fwd(ab):
    T = ab.shape[0]

    def fn(ab):
        a, b = ab[:, :DFF], ab[:, DFF:]
        return (_silu(a) * b,)

    (act,) = _rowcall(fn, [ab], [], [_sds((T, DFF), BF16)], [], 256, "ffn_act_fwd")
    return act


def _ffn_act_bwd(ab, dact):
    T = ab.shape[0]

    def fn(ab, dact):
        a, b = ab[:, :DFF], ab[:, DFF:]
        s = _sigmoid(a)
        da = dact * b * (s * (1.0 + a * (1.0 - s)))
        db = dact * (a * s)
        return (jnp.concatenate([da, db], axis=1),)

    (dab,) = _rowcall(fn, [ab, dact], [], [_sds((T, 2 * DFF), BF16)], [], 256, "ffn_act_bwd")
    return dab


def _loss_head(x1, ffn, target, gate2):
    T = x1.shape[0]

    def fn(x1, ffn, target, gate2):
        y = x1 + gate2 * ffn
        err = y - target
        loss = 0.5 * jnp.sum(jnp.sum(err * err, axis=1, keepdims=True), axis=0, keepdims=True) / D
        dy = err * (1.0 / D)
        dgate2 = jnp.sum(dy * ffn, axis=0, keepdims=True)
        return dy, gate2 * dy, dgate2, jnp.broadcast_to(loss, (1, LANE))

    return _rowcall(fn, [x1, ffn, target], [gate2], [_sds((T, D)), _sds((T, D), BF16)],
                    [_sds((1, D)), _sds((1, LANE))], 256, "loss_head")


def _round_bf16(x):
    return x.astype(BF16).astype(F32)


def _shift_down(x, s, rows):
    if s == 0:
        return x
    return jnp.where(rows >= s, pltpu.roll(x, s, 0), 0.0)


def _shift_up(x, s, rows, T):
    if s == 0:
        return x
    return jnp.where(rows < T - s, pltpu.roll(x, T - s, 0), 0.0)


def _conv_fwd(proj, conv_w):
    T = proj.shape[0]
    ncol = 3 * GW // LANE

    def body(x_ref, w_ref, o_ref):
        x = _round_bf16(x_ref[...])
        rows = lax.broadcasted_iota(jnp.int32, x.shape, 0)
        acc = jnp.zeros_like(x)
        for j in range(CONVW):
            acc = acc + _round_bf16(w_ref[pl.ds(j, 1), :]) * _shift_down(x, CONVW - 1 - j, rows)
        o_ref[0], o_ref[1] = _split_pair(_silu(acc))

    return pl.pallas_call(
        body, name="conv_fwd", grid=(ncol,),
        in_specs=[pl.BlockSpec((T, LANE), lambda j: (0, j)), pl.BlockSpec((CONVW, LANE), lambda j: (0, j))],
        out_specs=pl.BlockSpec((2, T, HD), lambda j: (j, 0, 0)),
        out_shape=_sds((3 * GH, T, HD)),
        compiler_params=_cparams(("parallel",)),
    )(proj, conv_w)


RELAYOUT_TM = 4096


def _split_pair(y):
    return y[:, :HD], pltpu.roll(y, HD, 1)[:, :HD]


def _merge_pair(a, b):
    return jnp.concatenate([a, b], axis=1)


def _split_heads(x, col_block0, nheads, name):
    T = x.shape[0]
    tm = _tile(T, RELAYOUT_TM)

    def body(x_ref, o_ref):
        a, b = _split_pair(x_ref[...])
        o_ref[0] = a
        o_ref[1] = b

    return pl.pallas_call(
        body, name=name, grid=(nheads // 2, T // tm),
        in_specs=[pl.BlockSpec((tm, LANE), lambda j, i: (i, col_block0 + j))],
        out_specs=pl.BlockSpec((2, tm, HD), lambda j, i: (j, i, 0)),
        out_shape=_sds((nheads, T, HD), x.dtype),
        compiler_params=_cparams(("parallel", "parallel")),
    )(x)


def _merge_heads(hm, out_dtype, name, into=None, col_block0=0, head0=0, nheads=None):
    T = hm.shape[1]
    nheads = hm.shape[0] if nheads is None else nheads
    tm = _tile(T, RELAYOUT_TM)

    def body(*refs):
        h_ref, o_ref = refs[0], refs[-1]
        o_ref[...] = _merge_pair(h_ref[0], h_ref[1]).astype(o_ref.dtype)

    in_specs = [pl.BlockSpec((2, tm, HD), lambda j, i: (head0 // 2 + j, i, 0))]
    args = [hm]
    if into is None:
        out_shape = _sds((T, HD * nheads), out_dtype)
        aliases = {}
    else:
        out_shape = _sds(into.shape, into.dtype)
        in_specs.append(pl.BlockSpec(memory_space=pl.ANY))
        args.append(into)
        aliases = {1: 0}
    return pl.pallas_call(
        body, name=name, grid=(nheads // 2, T // tm),
        in_specs=in_specs,
        out_specs=pl.BlockSpec((tm, LANE), lambda j, i: (i, col_block0 + j)),
        out_shape=out_shape, input_output_aliases=aliases,
        compiler_params=_cparams(("parallel", "parallel")),
    )(*args)


def _conv_bwd(proj, conv_w, dqc):
    T = proj.shape[0]
    ncol = 3 * GW // LANE

    def body(x_ref, w_ref, d_ref, dx_ref, dw_ref):
        x = _round_bf16(x_ref[...])
        rows = lax.broadcasted_iota(jnp.int32, x.shape, 0)
        xs = [_shift_down(x, CONVW - 1 - j, rows) for j in range(CONVW)]
        w = [_round_bf16(w_ref[pl.ds(j, 1), :]) for j in range(CONVW)]
        pre = jnp.zeros_like(x)
        for j in range(CONVW):
            pre = pre + w[j] * xs[j]
        s = _sigmoid(pre)
        dpre = _round_bf16(_merge_pair(d_ref[0], d_ref[1]) * (s * (1.0 + pre * (1.0 - s))))
        dx = jnp.zeros_like(x)
        for j in range(CONVW):
            dx = dx + w[j] * _shift_up(dpre, CONVW - 1 - j, rows, T)
            dw_ref[pl.ds(j, 1), :] = jnp.sum(dpre * xs[j], axis=0, keepdims=True)
        dx_ref[...] = dx.astype(dx_ref.dtype)

    return pl.pallas_call(
        body, name="conv_bwd", grid=(ncol,),
        in_specs=[pl.BlockSpec((T, LANE), lambda j: (0, j)), pl.BlockSpec((CONVW, LANE), lambda j: (0, j)),
                  pl.BlockSpec((2, T, HD), lambda j: (j, 0, 0))],
        out_specs=[pl.BlockSpec((T, LANE), lambda j: (0, j)), pl.BlockSpec((CONVW, LANE), lambda j: (0, j))],
        out_shape=[_sds((T, NP), BF16), _sds((CONVW, 3 * GW))],
        compiler_params=_cparams(("parallel",)),
    )(proj, conv_w, dqc)


def _gdn_prep(kit, q, k, v, ga, gb, alog, dtb, t_inv=None):
    C = CHUNK
    ri = lax.broadcasted_iota(jnp.int32, (C, C), 0)
    ci = lax.broadcasted_iota(jnp.int32, (C, C), 1)
    causal = ri >= ci
    strict = ri > ci
    eye = (ri == ci).astype(F32)
    lower = causal.astype(F32)
    upper = (ri <= ci).astype(F32)

    a = ga + dtb
    softplus = jnp.maximum(a, 0.0) + jnp.log(1.0 + jnp.exp(-jnp.abs(a)))
    g_row = -jnp.exp(alog) * softplus
    beta_row = _sigmoid(gb)
    g_col = jnp.sum(eye * g_row, axis=2, keepdims=True)
    beta_col = jnp.sum(eye * beta_row, axis=2, keepdims=True)
    G_col = jnp.sum(lower * g_row, axis=2, keepdims=True)
    G_row = jnp.sum(upper * g_col, axis=1, keepdims=True)
    G_last = jnp.sum(g_row, axis=2, keepdims=True)
    decay = jnp.exp(jnp.where(causal, G_col - G_row, -1e30))

    qn = q * lax.rsqrt(jnp.sum(q * q, axis=-1, keepdims=True) + EPS) * (HD ** -0.5)
    kn = k * lax.rsqrt(jnp.sum(k * k, axis=-1, keepdims=True) + EPS)
    kb = kn * beta_col
    A = jnp.where(strict, kit.nt(kb, kn) * decay, 0.0)
    Tm = kit.inv(A, t_inv)
    eG = jnp.exp(G_col)
    u = kit.nn3(Tm, v * beta_col)
    w = kit.nn3(Tm, kb * eG)
    qk = jnp.where(causal, kit.nt(qn, kn) * decay, 0.0)
    q_dec = qn * eG
    k_dec = kn * jnp.exp(G_last - G_col)
    dec = jnp.exp(G_last)
    return u, w, qk, q_dec, k_dec, dec, Tm


def _gdn_out(o, z, nw):
    return _rms(o, nw) * _silu(z)


GDN_CB = 4


def _gdn_specs(T, blk):
    TB = GDN_CB * CHUNK
    seq = lambda grp: pl.BlockSpec((GH, TB, HD), lambda i, grp=grp: (grp, blk(i), 0))
    row = lambda grp: pl.BlockSpec((GH, GDN_CB, 1, CHUNK), lambda i, grp=grp: (grp, blk(i), 0, 0))
    per_head = pl.BlockSpec((GH, 1, CHUNK), lambda i: (0, 0, 0))
    whole = pl.BlockSpec((1, HD), lambda i: (0, 0))
    state = pl.BlockSpec((GH, GDN_CB, HD, HD), lambda i: (0, blk(i), 0, 0))
    return seq, row, per_head, whole, state


def _gdn_load(seq_refs, row_refs, head_refs):
    chunks = lambda r: jnp.concatenate([r[:, pl.ds(cb * CHUNK, CHUNK), :] for cb in range(GDN_CB)], axis=0)
    rows = lambda r: jnp.concatenate([r[:, cb] for cb in range(GDN_CB)], axis=0)
    heads = lambda r: jnp.concatenate([r[...]] * GDN_CB, axis=0)
    return [chunks(r) for r in seq_refs], [rows(r) for r in row_refs], [heads(r) for r in head_refs]


def _gdn_fwd(qkv_hm, zs_hm, gab, alog_b, dtb_b, nw, shards):
    T = qkv_hm.shape[1]
    N = T // CHUNK
    nblk = N // GDN_CB
    ns = len(shards)
    seq, row, per_head, whole, state = _gdn_specs(T, lambda i: i)
    kit = _Kit(False)

    def body(*refs):
        q_ref, k_ref, v_ref, z_ref, ga_ref, gb_ref, al_ref, dt_ref, nw_ref = refs[:9]
        o_ref, S_ref, T_ref = refs[9 + ns:12 + ns]
        S_scr = refs[12 + 2 * ns]
        plan = _gather_plan(refs[9:9 + ns], refs[12 + ns:12 + 2 * ns], *refs[13 + 2 * ns:])

        @pl.when(pl.program_id(0) == 0)
        def _():
            S_scr[...] = jnp.zeros_like(S_scr)
            _start(plan)

        (q, k, v, z), (ga, gb), (al, dt) = _gdn_load((q_ref, k_ref, v_ref, z_ref), (ga_ref, gb_ref), (al_ref, dt_ref))
        u, w, qk, q_dec, k_dec, dec, t_inv = _gdn_prep(kit, q, k, v, ga, gb, al, dt)
        S = S_scr[...]
        for cb in range(GDN_CB):
            hs = slice(cb * GH, (cb + 1) * GH)
            S_ref[:, cb] = S
            T_ref[:, cb] = t_inv[hs]
            v_new = u[hs] - kit.nn(w[hs], S)
            o = kit.nn(q_dec[hs], S) + kit.nn(qk[hs], v_new)
            S = S * dec[hs] + kit.tn(k_dec[hs], v_new)
            o_ref[:, pl.ds(cb * CHUNK, CHUNK), :] = _gdn_out(o, z[hs], nw_ref[...])
        S_scr[...] = S

        @pl.when(pl.program_id(0) == nblk - 1)
        def _():
            _finish(plan)

    res = pl.pallas_call(
        body, name="gdn_fwd", grid=(nblk,),
        in_specs=[seq(0), seq(1), seq(2), seq(0), row(0), row(1), per_head, per_head, whole] + _hbm_specs(ns),
        out_specs=[seq(0), state, state] + _hbm_specs(ns),
        out_shape=[_sds((GH + SQH, T, HD)), _sds((GH, N, HD, HD)), _sds((GH, N, CHUNK, CHUNK))]
                  + _gather_shapes(shards),
        scratch_shapes=[pltpu.VMEM((GH, HD, HD), F32)] + _gather_sems(ns),
        compiler_params=_cparams(("arbitrary",)),
    )(qkv_hm, qkv_hm, qkv_hm, zs_hm, gab, gab, alog_b, dtb_b, nw, *shards)
    return res[0], (res[1], res[2]), res[3:]


def _gdn_bwd(qkv_hm, zs_hm, gab, alog_b, dtb_b, nw, S_all, do, pieces):
    T = qkv_hm.shape[1]
    N = T // CHUNK
    nblk = N // GDN_CB
    npc = len(pieces)
    dkit, kit = _Kit(True), _Kit(False)
    rseq, rrow, per_head, whole, rstate = _gdn_specs(T, lambda i: nblk - 1 - i)

    def body(*refs):
        q_ref, k_ref, v_ref, z_ref, ga_ref, gb_ref, al_ref, dt_ref, nw_ref, S_ref, T_ref, do_ref = refs[:12]
        dqkv_ref, dz_ref, dga_ref, dgb_ref, dal_ref, ddt_ref, dnw_ref = refs[12 + npc:19 + npc]
        dS_scr = refs[19 + 2 * npc]
        plan = _exchange_plan(refs[12:12 + npc], refs[19 + npc:19 + 2 * npc], *refs[20 + 2 * npc:])

        @pl.when(pl.program_id(0) == 0)
        def _():
            dS_scr[...] = jnp.zeros_like(dS_scr)
            dal_ref[...] = jnp.zeros_like(dal_ref)
            ddt_ref[...] = jnp.zeros_like(ddt_ref)
            dnw_ref[...] = jnp.zeros_like(dnw_ref)
            _start(plan)

        (q, k, v, z, dout), (ga, gb), (al, dt) = _gdn_load((q_ref, k_ref, v_ref, z_ref, do_ref), (ga_ref, gb_ref),
                                                          (al_ref, dt_ref))
        S_in = jnp.concatenate([S_ref[:, cb] for cb in range(GDN_CB)], axis=0)
        t_inv = jnp.concatenate([T_ref[:, cb] for cb in range(GDN_CB)], axis=0)
        prep = lambda *a: _gdn_prep(dkit, *a, t_inv=t_inv)[:6]
        (u, w, qk, q_dec, k_dec, dec), prep_vjp = jax.vjp(prep, q, k, v, ga, gb, al, dt)
        v_new = u - kit.nn(w, S_in)
        o = kit.nn(q_dec, S_in) + kit.nn(qk, v_new)
        _, out_vjp = jax.vjp(_gdn_out, o, z, nw_ref[...])
        do, dz, dnw = out_vjp(dout)
        dvn_part = kit.tn(qk, do)
        dS_part = kit.tn(q_dec, do)
        dS = dS_scr[...]
        dS_out, dvn = [None] * GDN_CB, [None] * GDN_CB
        for cb in reversed(range(GDN_CB)):
            hs = slice(cb * GH, (cb + 1) * GH)
            dS_out[cb] = dS
            dvn[cb] = dvn_part[hs] + kit.nn(k_dec[hs], dS)
            dS = dS * dec[hs] + dS_part[hs] - kit.tn(w[hs], dvn[cb])
        dS_scr[...] = dS
        dS_out = jnp.concatenate(dS_out, axis=0)
        dvn = jnp.concatenate(dvn, axis=0)
        ddec = jnp.sum(jnp.sum(S_in * dS_out, axis=2, keepdims=True), axis=1, keepdims=True)
        cts = (dvn, -kit.nt(dvn, S_in), kit.nt(do, v_new), kit.nt(do, S_in), kit.nt(v_new, dS_out), ddec)
        dq, dk, dv, dga, dgb, dal, ddt = prep_vjp(cts)
        lanesum = lambda t: jnp.broadcast_to(jnp.sum(t, axis=2, keepdims=True), t.shape)
        for cb in range(GDN_CB):
            hs = slice(cb * GH, (cb + 1) * GH)
            sl = pl.ds(cb * CHUNK, CHUNK)
            dqkv_ref[pl.ds(0, GH), sl, :] = dq[hs]
            dqkv_ref[pl.ds(GH, GH), sl, :] = dk[hs]
            dqkv_ref[pl.ds(2 * GH, GH), sl, :] = dv[hs]
            dz_ref[:, sl, :] = dz[hs]
            dga_ref[:, cb] = dga[hs]
            dgb_ref[:, cb] = dgb[hs]
            dal_ref[...] += lanesum(dal[hs])
            ddt_ref[...] += lanesum(ddt[hs])
        dnw_ref[...] += dnw

        @pl.when(pl.program_id(0) == nblk - 1)
        def _():
            _finish(plan)

    res = pl.pallas_call(
        body, name="gdn_bwd", grid=(nblk,),
        in_specs=[rseq(0), rseq(1), rseq(2), rseq(0), rrow(0), rrow(1), per_head, per_head, whole, rstate, rstate,
                  rseq(0)] + _hbm_specs(npc),
        out_specs=[pl.BlockSpec((3 * GH, GDN_CB * CHUNK, HD), lambda i: (0, nblk - 1 - i, 0)), rseq(0), rrow(0),
                   rrow(0), per_head, per_head, whole] + _hbm_specs(npc),
        out_shape=[_sds((3 * GH, T, HD)), _sds((GH + 4 + SWA_GRAD_HEADS, T, HD))] + [_sds((GH, N, 1, CHUNK))] * 2
                  + [_sds((GH, 1, CHUNK))] * 2 + [_sds((1, HD))] + _exchange_shapes(pieces),
        scratch_shapes=[pltpu.VMEM((GH, HD, HD), F32)] + _exchange_sems(npc),
        compiler_params=_cparams(("arbitrary",), GDN_BWD_VMEM),
    )(qkv_hm, qkv_hm, qkv_hm, zs_hm, gab, gab, alog_b, dtb_b, nw, S_all[0], S_all[1], do, *pieces)
    return res[:7], res[7:]


def _swa_block(kit, first, q0, q1, q2, q3, kp, kc, vp, vc, qnw, knw, s0, s1, s2, s3, *, slopes):
    W = WIN
    ri = lax.broadcasted_iota(jnp.int32, (W, W), 0)
    ci = lax.broadcasted_iota(jnp.int32, (W, W), 1)
    mask_c = ri >= ci
    mask_p = ci > ri + first * W
    dist_c = (ri - ci).astype(F32)
    dist_p = (ri - ci + W).astype(F32)
    kpn = _rms(kp, knw)
    kcn = _rms(kc, knw)
    outs = []
    for q, sink, slope in zip((q0, q1, q2, q3), (s0, s1, s2, s3), slopes):
        qn = _rms(q, qnw)
        sc = jnp.where(mask_c, kit.nt(qn, kcn) * (HD ** -0.5) - slope * dist_c, -1e30)
        sp = jnp.where(mask_p, kit.nt(qn, kpn) * (HD ** -0.5) - slope * dist_p, -1e30)
        m = jnp.maximum(jnp.maximum(jnp.max(sc, axis=-1, keepdims=True), jnp.max(sp, axis=-1, keepdims=True)), sink)
        m = lax.stop_gradient(m)
        pc = jnp.exp(sc - m)
        pp = jnp.exp(sp - m)
        den = jnp.sum(pc, axis=-1, keepdims=True) + jnp.sum(pp, axis=-1, keepdims=True) + jnp.exp(sink - m)
        inv = 1.0 / den
        outs.append(kit.nn(pc * inv, vc) + kit.nn(pp * inv, vp))
    return tuple(outs)


def _swa_slopes(hk):
    return tuple(jnp.where(hk == 0, 2.0 ** (-8.0 * (g + 1.0) / SQH), 2.0 ** (-8.0 * (SGRP + g + 1.0) / SQH))
                 for g in range(SGRP))


def _swa_fwd(zs_hm, qnw, knw, sinks_col):
    T = zs_hm.shape[1]
    NB = T // WIN
    kit = _Kit(False)

    def body(q_ref, kp_ref, kc_ref, vp_ref, vc_ref, qnw_ref, knw_ref, s_ref, o_ref):
        hk = pl.program_id(0)
        first = (pl.program_id(1) == 0).astype(jnp.int32)
        args = ([q_ref[g] for g in range(SGRP)] + [kp_ref[...], kc_ref[...], vp_ref[...], vc_ref[...],
                                                     qnw_ref[...], knw_ref[...]] + [s_ref[g] for g in range(SGRP)])
        outs = _swa_block(kit, first, *args, slopes=_swa_slopes(hk))
        for g in range(SGRP):
            o_ref[g] = outs[g]

    qspec = pl.BlockSpec((SGRP, WIN, HD), lambda hk, n: (2 + hk, n, 0))
    cur = lambda off: pl.BlockSpec((None, WIN, HD), lambda hk, n, off=off: (off + hk, n, 0))
    prev = lambda off: pl.BlockSpec((None, WIN, HD), lambda hk, n, off=off: (off + hk, jnp.maximum(n - 1, 0), 0))
    whole = pl.BlockSpec((1, HD), lambda hk, n: (0, 0))
    sspec = pl.BlockSpec((SGRP, WIN, 1), lambda hk, n: (hk, 0, 0))
    return pl.pallas_call(
        body, name="swa_fwd", grid=(SKVH, NB),
        in_specs=[qspec, prev(16), cur(16), prev(18), cur(18), whole, whole, sspec],
        out_specs=pl.BlockSpec((SGRP, WIN, HD), lambda hk, n: (hk, n, 0)),
        out_shape=_sds((SQH, T, HD)),
        compiler_params=_cparams(("parallel", "arbitrary")),
    )(zs_hm, zs_hm, zs_hm, zs_hm, zs_hm, qnw, knw, sinks_col)


def _swa_bwd(zs_hm, qnw, knw, sinks_col, do):
    T = zs_hm.shape[1]
    NB = T // WIN
    kit = _Kit(True)

    def body(q_ref, kp_ref, kc_ref, vp_ref, vc_ref, qnw_ref, knw_ref, s_ref, do_ref,
             dq_ref, dk_ref, dv_ref, dqnw_ref, dknw_ref, ds_ref, ck_scr, cv_scr):
        hk = pl.program_id(0)
        i = pl.program_id(1)
        first = (i == NB - 1).astype(jnp.int32)

        @pl.when(i == 0)
        def _():
            ck_scr[...] = jnp.zeros_like(ck_scr)
            cv_scr[...] = jnp.zeros_like(cv_scr)
            ds_ref[...] = jnp.zeros_like(ds_ref)

        @pl.when((i == 0) & (hk == 0))
        def _():
            dqnw_ref[...] = jnp.zeros_like(dqnw_ref)
            dknw_ref[...] = jnp.zeros_like(dknw_ref)

        args = ([q_ref[g] for g in range(SGRP)] + [kp_ref[...], kc_ref[...], vp_ref[...], vc_ref[...],
                                                     qnw_ref[...], knw_ref[...]] + [s_ref[g] for g in range(SGRP)])
        dos = tuple(do_ref[g] for g in range(SGRP))
        _, vjp = jax.vjp(functools.partial(_swa_block, kit, first, slopes=_swa_slopes(hk)), *args)
        gr = vjp(dos)
        for g in range(SGRP):
            dq_ref[g] = gr[g]
            ds_ref[g] += jnp.broadcast_to(jnp.sum(gr[10 + g], axis=0, keepdims=True), (WIN, 1))
        dkp, dkc, dvp, dvc = gr[4:8]
        dk_ref[...] = dkc + ck_scr[...]
        dv_ref[...] = dvc + cv_scr[...]
        ck_scr[...] = dkp
        cv_scr[...] = dvp
        dqnw_ref[...] += gr[8]
        dknw_ref[...] += gr[9]

    rn = lambda n: NB - 1 - n
    qspec = pl.BlockSpec((SGRP, WIN, HD), lambda hk, i: (2 + hk, rn(i), 0))
    cur = lambda off: pl.BlockSpec((None, WIN, HD), lambda hk, i, off=off: (off + hk, rn(i), 0))
    prev = lambda off: pl.BlockSpec((None, WIN, HD), lambda hk, i, off=off: (off + hk, jnp.maximum(rn(i) - 1, 0), 0))
    whole = pl.BlockSpec((1, HD), lambda hk, i: (0, 0))
    sspec = pl.BlockSpec((SGRP, WIN, 1), lambda hk, i: (hk, 0, 0))
    ospec = pl.BlockSpec((SGRP, WIN, HD), lambda hk, i: (hk, rn(i), 0))
    return pl.pallas_call(
        body, name="swa_bwd", grid=(SKVH, NB),
        in_specs=[qspec, prev(16), cur(16), prev(18), cur(18), whole, whole, sspec, ospec],
        out_specs=[ospec, cur(0), cur(0), whole, whole, sspec],
        out_shape=[_sds((SQH, T, HD)), _sds((SKVH, T, HD)), _sds((SKVH, T, HD)),
                   _sds((1, HD)), _sds((1, HD)), _sds((SQH, WIN, 1))],
        scratch_shapes=[pltpu.VMEM((WIN, HD), F32), pltpu.VMEM((WIN, HD), F32)],
        compiler_params=_cparams(("arbitrary", "arbitrary")),
    )(zs_hm, zs_hm, zs_hm, zs_hm, zs_hm, qnw, knw, sinks_col, do)


def _swa_heads(kit, first, q, kp, kc, vp, vc, qnw, knw, sink, slope):
    W = WIN
    ri = lax.broadcasted_iota(jnp.int32, (W, W), 0)
    ci = lax.broadcasted_iota(jnp.int32, (W, W), 1)
    mask_c = ri >= ci
    mask_p = ci > ri + first * W
    dist_c = (ri - ci).astype(F32)
    dist_p = (ri - ci + W).astype(F32)
    kpn = _rms(kp, knw)
    kcn = _rms(kc, knw)
    qn = _rms(q, qnw)
    sc = jnp.where(mask_c, kit.nt(qn, kcn) * (HD ** -0.5) - slope * dist_c, -1e30)
    sp = jnp.where(mask_p, kit.nt(qn, kpn) * (HD ** -0.5) - slope * dist_p, -1e30)
    m = jnp.maximum(jnp.maximum(jnp.max(sc, axis=-1, keepdims=True), jnp.max(sp, axis=-1, keepdims=True)), sink)
    m = lax.stop_gradient(m)
    pc = jnp.exp(sc - m)
    pp = jnp.exp(sp - m)
    den = jnp.sum(pc, axis=-1, keepdims=True) + jnp.sum(pp, axis=-1, keepdims=True) + jnp.exp(sink - m)
    inv = 1.0 / den
    return kit.nn(pc * inv, vc) + kit.nn(pp * inv, vp)


def _per_query_head(kv_ref):
    return jnp.concatenate([kv_ref[pl.ds(h // SGRP, 1)] for h in range(SQH)], axis=0)


def _per_kv_head(d):
    return jnp.concatenate([jnp.sum(d[g * SGRP:(g + 1) * SGRP], axis=0, keepdims=True) for g in range(SKVH)], axis=0)


def _swa_specs(blk):
    qspec = pl.BlockSpec((SQH, WIN, HD), lambda i: (1, blk(i), 0))
    cur = lambda grp: pl.BlockSpec((SKVH, WIN, HD), lambda i, grp=grp: (grp, blk(i), 0))
    prev = lambda grp: pl.BlockSpec((SKVH, WIN, HD), lambda i, grp=grp: (grp, jnp.maximum(blk(i) - 1, 0), 0))
    whole = pl.BlockSpec((1, HD), lambda i: (0, 0))
    col = pl.BlockSpec((SQH, WIN, 1), lambda i: (0, 0, 0))
    ospec = pl.BlockSpec((SQH, WIN, HD), lambda i: (0, blk(i), 0))
    return qspec, cur, prev, whole, col, ospec


def _swa_fwd(zs_hm, qnw, knw, sinks_col, slopes_col, o_buf, shards):
    T = zs_hm.shape[1]
    NB = T // WIN
    ns = len(shards)
    kit = _Kit(False)
    qspec, cur, prev, whole, col, _ = _swa_specs(lambda i: i)

    def body(*refs):
        q_ref, kp_ref, kc_ref, vp_ref, vc_ref, qnw_ref, knw_ref, s_ref, sl_ref = refs[:9]
        o_ref = refs[10 + ns]
        plan = _gather_plan(refs[10:10 + ns], refs[11 + ns:11 + 2 * ns], *refs[11 + 2 * ns:])

        @pl.when(pl.program_id(0) == 0)
        def _():
            _start(plan)

        first = (pl.program_id(0) == 0).astype(jnp.int32)
        o_ref[...] = _swa_heads(kit, first, q_ref[...], _per_query_head(kp_ref), _per_query_head(kc_ref),
                                _per_query_head(vp_ref), _per_query_head(vc_ref), qnw_ref[...], knw_ref[...],
                                s_ref[...], sl_ref[...])

        @pl.when(pl.program_id(0) == NB - 1)
        def _():
            _finish(plan)

    res = pl.pallas_call(
        body, name="swa_fwd", grid=(NB,),
        in_specs=[qspec, prev(8), cur(8), prev(9), cur(9), whole, whole, col, col] + _hbm_specs(1 + ns),
        out_specs=[pl.BlockSpec((SQH, WIN, HD), lambda i: (1, i, 0))] + _hbm_specs(ns),
        out_shape=[_sds(o_buf.shape)] + _gather_shapes(shards),
        input_output_aliases={9: 0},
        scratch_shapes=_gather_sems(ns),
        compiler_params=_cparams(("arbitrary",)),
    )(zs_hm, zs_hm, zs_hm, zs_hm, zs_hm, qnw, knw, sinks_col, slopes_col, o_buf, *shards)
    return res[0], res[1:]


SWA_GRAD_HEADS = SQH + 2 * SKVH


def _swa_bwd(zs_hm, qnw, knw, sinks_col, slopes_col, dmix_hm, d_buf):
    T = zs_hm.shape[1]
    NB = T // WIN
    kit = _Kit(True)
    qspec, cur, prev, whole, col, _ = _swa_specs(lambda i: NB - 1 - i)

    def body(q_ref, kp_ref, kc_ref, vp_ref, vc_ref, qnw_ref, knw_ref, s_ref, sl_ref, do_ref, buf_ref,
             d_ref, dqnw_ref, dknw_ref, ds_ref, ck_scr, cv_scr):
        dq_ref = d_ref.at[pl.ds(0, SQH)]
        dk_ref = d_ref.at[pl.ds(SQH, SKVH)]
        dv_ref = d_ref.at[pl.ds(SQH + SKVH, SKVH)]
        i = pl.program_id(0)
        first = (i == NB - 1).astype(jnp.int32)

        @pl.when(i == 0)
        def _():
            ck_scr[...] = jnp.zeros_like(ck_scr)
            cv_scr[...] = jnp.zeros_like(cv_scr)
            ds_ref[...] = jnp.zeros_like(ds_ref)
            dqnw_ref[...] = jnp.zeros_like(dqnw_ref)
            dknw_ref[...] = jnp.zeros_like(dknw_ref)

        fn = lambda q, kp, kc, vp, vc, qnw, knw, sink: _swa_heads(kit, first, q, kp, kc, vp, vc, qnw, knw, sink,
                                                                  sl_ref[...])
        _, vjp = jax.vjp(fn, q_ref[...], _per_query_head(kp_ref), _per_query_head(kc_ref), _per_query_head(vp_ref),
                         _per_query_head(vc_ref), qnw_ref[...], knw_ref[...], s_ref[...])
        dq, dkp, dkc, dvp, dvc, dqnw, dknw, dsink = vjp(do_ref[...])
        dq_ref[...] = dq
        dk_ref[...] = _per_kv_head(dkc) + ck_scr[...]
        dv_ref[...] = _per_kv_head(dvc) + cv_scr[...]
        ck_scr[...] = _per_kv_head(dkp)
        cv_scr[...] = _per_kv_head(dvp)
        dqnw_ref[...] += dqnw
        dknw_ref[...] += dknw
        ds_ref[...] += jnp.broadcast_to(jnp.sum(dsink, axis=1, keepdims=True), dsink.shape)

    dospec = pl.BlockSpec((SQH, WIN, HD), lambda i: (1, NB - 1 - i, 0))
    dspec = pl.BlockSpec((SWA_GRAD_HEADS, WIN, HD), lambda i: (1, NB - 1 - i, 0))
    res = pl.pallas_call(
        body, name="swa_bwd", grid=(NB,),
        in_specs=[qspec, prev(8), cur(8), prev(9), cur(9), whole, whole, col, col, dospec] + _hbm_specs(1),
        out_specs=[dspec, whole, whole, col],
        out_shape=[_sds(d_buf.shape), _sds((1, HD)), _sds((1, HD)), _sds((SQH, WIN, 1))],
        input_output_aliases={10: 0},
        scratch_shapes=[pltpu.VMEM((SKVH, WIN, HD), F32), pltpu.VMEM((SKVH, WIN, HD), F32)],
        compiler_params=_cparams(("arbitrary",)),
    )(zs_hm, zs_hm, zs_hm, zs_hm, zs_hm, qnw, knw, sinks_col, slopes_col, dmix_hm, d_buf)
    return res


GAB0 = 3 * GW + 1280


W_IN_ROWS = PROJ // N_CHIP
W_IN_ROWS_PAD = 736


def _permute_w_in_t(w_in_t):
    return jnp.concatenate([w_in_t[:4 * GW], w_in_t[4 * GW + 2 * GH:], w_in_t[4 * GW:4 * GW + 2 * GH],
                            jnp.zeros((NP - PROJ, D), w_in_t.dtype)], axis=0)


def _w_in_grad_pieces(g_t):
    g = jnp.concatenate([g_t[:4 * GW], g_t[GAB0:GAB0 + 2 * GH], g_t[4 * GW:GAB0]], axis=0)
    g = jnp.pad(g.reshape(N_CHIP, W_IN_ROWS, D), ((0, 0), (0, W_IN_ROWS_PAD - W_IN_ROWS), (0, 0)))
    return g.reshape(N_CHIP, 2, W_IN_ROWS_PAD // 2, D)


def _pieces_by_rows(g):
    return g.reshape(N_CHIP, 2, g.shape[0] // (2 * N_CHIP), D)


def _local_step(x, target, mod, n1w, w_in_pt, conv_w, alog, dtb, gnw, qnw, knw, sinks, n2w, shards):
    sh_out, sh_gate, sh_up, sh_down = shards
    T = x.shape[0]
    N = T // CHUNK
    shift1, scale1, gate1, shift2, scale2, gate2 = [mod[:, i * D:(i + 1) * D] for i in range(6)]

    h = _norm_mod_fwd(x, n1w, scale1, shift1)
    proj, (a_out,) = _matmul(h, w_in_pt, tb=True, name="in_proj", gather=[sh_out])
    w_out = a_out.reshape(D, D)
    qkv_hm = _conv_fwd(proj, conv_w)
    zs_hm = _split_heads(proj, 3 * GW // LANE, 20, "split_zs")
    gab = proj[:, GAB0:GAB0 + 2 * GH].T.reshape(2 * GH, N, 1, CHUNK)
    alog_b = jnp.broadcast_to(alog.reshape(GH, 1, 1), (GH, 1, CHUNK))
    dtb_b = jnp.broadcast_to(dtb.reshape(GH, 1, 1), (GH, 1, CHUNK))
    sinks_col = jnp.broadcast_to(sinks.reshape(SQH, 1, 1), (SQH, WIN, 1))
    o_hm, S_all, (a_gate, a_up) = _gdn_fwd(qkv_hm, zs_hm, gab, alog_b, dtb_b, gnw, [sh_gate, sh_up])
    w_gut = jnp.concatenate([a_gate.reshape(DFF, D), a_up.reshape(DFF, D)], axis=0)
    slopes = 2.0 ** (-8.0 * (jnp.arange(SQH, dtype=F32) + 1.0) / SQH)
    slopes_col = jnp.broadcast_to(slopes.reshape(SQH, 1, 1), (SQH, WIN, 1))
    o_hm, (a_down,) = _swa_fwd(zs_hm, qnw, knw, sinks_col, slopes_col, o_hm, [sh_down])
    w_down = a_down.reshape(DFF, D)
    mixcat = _merge_heads(o_hm, BF16, "merge_mix")
    mixed = _matmul(mixcat, w_out, name="out_proj")
    x1, h2 = _resid_norm_fwd(x, mixed, gate1, n2w, scale2, shift2)
    ab = _matmul(h2, w_gut, tb=True, name="ffn_up")
    act = _ffn_act_fwd(ab)
    ffn = _matmul(act, w_down, name="ffn_down")
    dy, dffn, dgate2, loss = _loss_head(x1, ffn, target, gate2)

    dact = _matmul(dffn, w_down, tb=True, name="ffn_down_dx")
    dab = _ffn_act_bwd(ab, dact)
    g_w_down = _matmul(act, dffn, ta=True, out_dtype=BF16, name="ffn_down_dw")
    g_w_gut = _matmul(dab, h2, ta=True, out_dtype=BF16, name="ffn_up_dw")
    dh2 = _matmul(dab, w_gut, name="ffn_up_dx")
    dx1, dmixed, dgate1, dn2w, dscale2, dshift2 = _resid_norm_bwd(x, mixed, dy, dh2, gate1, n2w, scale2, shift2)
    g_w_out = _matmul(mixcat, dmixed, ta=True, out_dtype=BF16, name="out_proj_dw")
    dmix_hm = _split_heads(_matmul(dmixed, w_out, tb=True, name="out_proj_dx"), 0, GH + SQH, "split_dmix")
    pieces = [_pieces_by_rows(g_w_out), _pieces_by_rows(g_w_gut[:DFF]), _pieces_by_rows(g_w_gut[DFF:]),
              _pieces_by_rows(g_w_down)]
    (dqkv_hm, d_hm, dga, dgb, dalog, ddtb, dgnw), recv = _gdn_bwd(qkv_hm, zs_hm, gab, alog_b, dtb_b, gnw, S_all,
                                                                  dmix_hm, pieces)
    d_hm, dqnw, dknw, dsinks = _swa_bwd(zs_hm, qnw, knw, sinks_col, slopes_col, dmix_hm, d_hm)
    dproj, dconv = _conv_bwd(proj, conv_w, dqkv_hm)
    dproj = _merge_heads(d_hm, BF16, "merge_dz", into=dproj, col_block0=3 * GW // LANE, head0=0, nheads=GH)
    dproj = _merge_heads(d_hm, BF16, "merge_dswa", into=dproj, col_block0=4 * GW // LANE, head0=GH + 4,
                         nheads=SWA_GRAD_HEADS)
    dgab = jnp.concatenate([dga, dgb], axis=0).reshape(2 * GH, T).T.astype(BF16)
    dproj = lax.dynamic_update_slice(dproj, jnp.concatenate([dgab, jnp.zeros((T, NP - PROJ), BF16)], axis=1),
                                     (0, GAB0))
    g_w_in_pt = _matmul(dproj, h, ta=True, out_dtype=BF16, name="in_proj_dw")
    dh, recv_in = _matmul(dproj, w_in_pt, name="in_proj_dx", exchange=[_w_in_grad_pieces(g_w_in_pt)])
    grad_x, dn1w, dscale1, dshift1 = _norm_mod_bwd(x, dh, dx1, n1w, scale1, shift1)

    dmod = jnp.concatenate([dshift1, dscale1, dgate1, dshift2, dscale2, dgate2], axis=1)
    big = list(recv_in) + list(recv)
    small = dict(mod=dmod, norm1_w=dn1w, norm2_w=dn2w, conv_w=dconv, a_log=dalog[:, 0, 0], dt_bias=ddtb[:, 0, 0],
                 gdn_norm_w=dgnw, q_norm_w=dqnw, k_norm_w=dknw, sinks=dsinks[:, 0, 0])
    return loss, grad_x, big, small


def _adamw(w, g, m, v):
    m2 = ADAM_B1 * m + (1.0 - ADAM_B1) * g
    v2 = ADAM_B2 * v + (1.0 - ADAM_B2) * (g * g)
    m_hat = m2 / (1.0 - ADAM_B1 ** ADAM_STEP)
    v_hat = v2 / (1.0 - ADAM_B2 ** ADAM_STEP)
    delta = -ADAM_LR * (m_hat / (jnp.sqrt(v_hat) + ADAM_EPS) + ADAM_WD * w)
    return delta, m2, v2


def _reduce_adamw(recv, w, m, v, name):
    _, R, C = recv.shape
    tc = _tile(C, 256)

    def body(r_ref, w_ref, m_ref, v_ref, o_ref):
        g = r_ref[0].astype(F32)
        for s in range(1, N_DEV):
            g = g + r_ref[s].astype(F32)
        delta, m2, v2 = _adamw(w_ref[...], g, m_ref[...], v_ref[...])
        o_ref[0] = g
        o_ref[1] = delta
        o_ref[2] = m2
        o_ref[3] = v2

    col = pl.BlockSpec((R, tc), lambda j: (0, j))
    return pl.pallas_call(
        body, name=name, grid=(C // tc,),
        in_specs=[pl.BlockSpec((N_DEV, R, tc), lambda j: (0, 0, j)), col, col, col],
        out_specs=pl.BlockSpec((4, R, tc), lambda j: (0, 0, j)),
        out_shape=_sds((4, R, C)),
        compiler_params=_cparams(("parallel",)),
    )(recv, w, m, v)


def _adamw_call(g, w, m, v, name):
    def body(g_ref, w_ref, m_ref, v_ref, o_ref):
        delta, m2, v2 = _adamw(w_ref[...], g_ref[...], m_ref[...], v_ref[...])
        o_ref[0] = delta
        o_ref[1] = m2
        o_ref[2] = v2

    return pl.pallas_call(body, name=name, out_shape=_sds((3,) + g.shape))(g, w, m, v)


ADA_N = 6 * D // N_CHIP
KPAD = 128


def _mod_part(c8, w_ada, b_ada):
    tn = 512

    def body(c_ref, w_ref, b_ref, o_ref):
        o_ref[...] = _raw1(_silu(c_ref[...]), w_ref[...], _NN) + b_ref[...]

    return pl.pallas_call(
        body, name="ada_mod", grid=(ADA_N // tn,),
        in_specs=[pl.BlockSpec((16, D), lambda j: (0, 0)), pl.BlockSpec((D, tn), lambda j: (0, j)),
                  pl.BlockSpec((1, tn), lambda j: (0, j))],
        out_specs=pl.BlockSpec((16, tn), lambda j: (0, j)),
        out_shape=_sds((16, ADA_N)),
        compiler_params=_cparams(("parallel",)),
    )(c8, w_ada, b_ada)


def _w_ada_update(c8p, dm, w, m, v):
    tr = 256

    def body(c_ref, dm_ref, w_ref, m_ref, v_ref, g_ref, d_ref, m2_ref, v2_ref):
        g = _raw1(_silu(c_ref[...]), dm_ref[...], _TN)
        delta, m2, v2 = _adamw(w_ref[...], g, m_ref[...], v_ref[...])
        g_ref[...] = g
        d_ref[...] = delta
        m2_ref[...] = m2
        v2_ref[...] = v2

    blk = pl.BlockSpec((tr, ADA_N), lambda i: (i, 0))
    return pl.pallas_call(
        body, name="w_ada_update", grid=(D // tr,),
        in_specs=[pl.BlockSpec((KPAD, tr), lambda i: (0, i)), pl.BlockSpec((KPAD, ADA_N), lambda i: (0, 0)),
                  blk, blk, blk],
        out_specs=[blk] * 4, out_shape=[_sds((D, ADA_N))] * 4,
        compiler_params=_cparams(("parallel",)),
    )(c8p, dm, w, m, v)


def _me():
    return lax.axis_index("x"), lax.axis_index("y"), lax.axis_index("c")


def _peer(k, me):
    mx, my, mc = me
    return (1 - mx if k & 4 else mx, 1 - my if k & 2 else my, 1 - mc if k & 1 else mc)


def _lin(p):
    return 4 * p[0] + 2 * p[1] + p[2]


def _remote(src, dst, ssem, rsem, dev):
    return pltpu.make_async_remote_copy(src_ref=src, dst_ref=dst, send_sem=ssem, recv_sem=rsem,
                                        device_id=dev, device_id_type=MESH)


def _all_gather8(x, name):
    def body(x_ref, out_ref, send_sems, recv_sems):
        me = _me()
        out_ref[_lin(me)] = x_ref[...]
        sends = []
        for k in range(1, N_DEV):
            cp = _remote(x_ref, out_ref.at[_lin(me)], send_sems.at[k - 1], recv_sems.at[k - 1], _peer(k, me))
            cp.start()
            sends.append(cp)
        for k in range(1, N_DEV):
            p = _peer(k, me)
            _remote(x_ref, out_ref.at[_lin(p)], send_sems.at[k - 1], recv_sems.at[k - 1], p).wait_recv()
        for cp in sends:
            cp.wait_send()

    return pl.pallas_call(
        body, name=name,
        out_shape=_sds((N_DEV,) + x.shape, x.dtype),
        in_specs=[pl.BlockSpec(memory_space=pltpu.VMEM)],
        out_specs=pl.BlockSpec(memory_space=pltpu.VMEM),
        scratch_shapes=[pltpu.SemaphoreType.DMA((N_DEV - 1,)), pltpu.SemaphoreType.DMA((N_DEV - 1,))],
    )(x)


def _hbm_specs(n):
    return [pl.BlockSpec(memory_space=pl.ANY)] * n


def _gather_weights(shards):
    n = len(shards)

    def body(*refs):
        plan = _gather_plan(refs[:n], refs[n:2 * n], *refs[2 * n:])
        _start(plan)
        _finish(plan)

    return pl.pallas_call(
        body, name="gather_weights",
        out_shape=_gather_shapes(shards), in_specs=_hbm_specs(n), out_specs=_hbm_specs(n),
        scratch_shapes=_gather_sems(n),
    )(*shards)


def _gather_shapes(shards):
    return [_sds((N_CHIP,) + s.shape, s.dtype) for s in shards]


def _gather_sems(n):
    return [pltpu.SemaphoreType.DMA((3 * n,)), pltpu.SemaphoreType.DMA((3 * n,)), pltpu.SemaphoreType.DMA((n,))]


def _gather_plan(ins, outs, send_sems, recv_sems, local_sems):
    mx, my, mc = _me()
    chips = [(1 - mx, my), (mx, 1 - my), (1 - mx, 1 - my)]
    local, sends, recvs = [], [], []
    for a in range(len(ins)):
        local.append(pltpu.make_async_copy(ins[a], outs[a].at[2 * mx + my], local_sems.at[a]))
        for k, (px, py) in enumerate(chips):
            sems = (send_sems.at[3 * a + k], recv_sems.at[3 * a + k], (px, py, mc))
            sends.append(_remote(ins[a], outs[a].at[2 * mx + my], *sems))
            recvs.append(_remote(ins[a], outs[a].at[2 * px + py], *sems))
    return local, sends, recvs


def _start(plan):
    local, sends, _ = plan
    for cp in local + sends:
        cp.start()


def _finish(plan):
    local, sends, recvs = plan
    for cp in recvs:
        cp.wait_recv()
    for cp in sends:
        cp.wait_send()
    for cp in local:
        cp.wait()


def _grad_exchange(pieces):
    n = len(pieces)

    def body(*refs):
        plan = _exchange_plan(refs[:n], refs[n:2 * n], *refs[2 * n:])
        _start(plan)
        _finish(plan)

    return pl.pallas_call(
        body, name="grad_exchange",
        out_shape=_exchange_shapes(pieces), in_specs=_hbm_specs(n), out_specs=_hbm_specs(n),
        scratch_shapes=_exchange_sems(n),
    )(*pieces)


def _exchange_shapes(pieces):
    return [_sds((N_DEV,) + p.shape[2:], p.dtype) for p in pieces]


def _exchange_sems(n):
    return [pltpu.SemaphoreType.DMA(((N_DEV - 1) * n,)), pltpu.SemaphoreType.DMA(((N_DEV - 1) * n,)),
            pltpu.SemaphoreType.DMA((n,))]


def _exchange_plan(ins, outs, send_sems, recv_sems, local_sems):
    me = _me()
    mx, my, mc = me
    local, sends, recvs = [], [], []
    for a in range(len(ins)):
        local.append(pltpu.make_async_copy(ins[a].at[2 * mx + my, mc], outs[a].at[_lin(me)], local_sems.at[a]))
        for k in range(1, N_DEV):
            p = _peer(k, me)
            s = (N_DEV - 1) * a + k - 1
            sends.append(_remote(ins[a].at[2 * p[0] + p[1], p[2]], outs[a].at[_lin(me)], send_sems.at[s],
                                 recv_sems.at[s], p))
            recvs.append(_remote(ins[a].at[2 * mx + my, mc], outs[a].at[_lin(p)], send_sems.at[s],
                                 recv_sems.at[s], p))
    return local, sends, recvs


def _reduce_swap(recv, name):
    _, rows, cols = recv.shape

    def body(r_ref, o_ref, send_sem, recv_sem):
        mx, my, mc = _me()
        sib = (mx, my, 1 - mc)
        g = r_ref[0].astype(F32)
        for s in range(1, N_DEV):
            g = g + r_ref[s].astype(F32)
        mine = o_ref.at[pl.ds(pl.multiple_of(mc * rows, 8), rows)]
        theirs = o_ref.at[pl.ds(pl.multiple_of((1 - mc) * rows, 8), rows)]
        mine[...] = g
        cp = _remote(mine, mine, send_sem, recv_sem, sib)
        cp.start()
        _remote(mine, theirs, send_sem, recv_sem, sib).wait_recv()
        cp.wait_send()

    return pl.pallas_call(
        body, name=name, out_shape=_sds((2 * rows, cols)),
        in_specs=[pl.BlockSpec(memory_space=pltpu.VMEM)], out_specs=pl.BlockSpec(memory_space=pltpu.VMEM),
        scratch_shapes=[pltpu.SemaphoreType.DMA, pltpu.SemaphoreType.DMA],
        compiler_params=_cparams(),
    )(recv)


def _adamw_big(g, w, m, v, name):
    rows, cols = g.shape
    tr = next((t for t in (256, 176, 128, 64, 8) if rows % t == 0), None)
    if tr is None:
        tc = _tile(cols, 256)
        blk, grid = pl.BlockSpec((rows, tc), lambda i: (0, i)), (cols // tc,)
    else:
        blk, grid = pl.BlockSpec((tr, cols), lambda i: (i, 0)), (rows // tr,)

    def body(g_ref, w_ref, m_ref, v_ref, go_ref, d_ref, m2_ref, v2_ref):
        g = g_ref[...]
        delta, m2, v2 = _adamw(w_ref[...], g, m_ref[...], v_ref[...])
        go_ref[...] = g
        d_ref[...] = delta
        m2_ref[...] = m2
        v2_ref[...] = v2

    return pl.pallas_call(
        body, name=name, grid=grid,
        in_specs=[blk] * 4, out_specs=[blk] * 4, out_shape=[_sds((rows, cols))] * 4,
        compiler_params=_cparams(("parallel",)),
    )(g, w, m, v)


SMALL_ORDER = (("mod", 6 * D), ("norm1_w", D), ("norm2_w", D), ("conv_w", CONVW * 3 * GW), ("a_log", GH),
               ("dt_bias", GH), ("gdn_norm_w", HD), ("q_norm_w", HD), ("k_norm_w", HD), ("sinks", SQH), ("loss", 1))
SMALL_R = 120


def _pack_small(d):
    parts = [d[k].reshape(-1).astype(F32) if k in d else jnp.zeros((n,), F32) for k, n in SMALL_ORDER]
    used = sum(n for _, n in SMALL_ORDER)
    parts.append(jnp.zeros((SMALL_R * LANE - used,), F32))
    return jnp.concatenate(parts).reshape(SMALL_R, LANE)


def _unpack_small(pk):
    flat = pk.reshape(-1)
    out, r = {}, 0
    for k, n in SMALL_ORDER:
        out[k] = flat[r:r + n]
        r += n
    return out


def kernel(x, c, w_ada, b_ada, norm1_w, w_in, conv_w, a_log, dt_bias, gdn_norm_w, q_norm_w, k_norm_w, sinks, w_out, norm2_w, w_gate, w_up, w_down, loss_target, m_w_ada, m_b_ada, m_norm1_w, m_w_in, m_conv_w, m_a_log, m_dt_bias, m_gdn_norm_w, m_q_norm_w, m_k_norm_w, m_sinks, m_w_out, m_norm2_w, m_w_gate, m_w_up, m_w_down, v_w_ada, v_b_ada, v_norm1_w, v_w_in, v_conv_w, v_a_log, v_dt_bias, v_gdn_norm_w, v_q_norm_w, v_k_norm_w, v_sinks, v_w_out, v_norm2_w, v_w_gate, v_w_up, v_w_down):
    mx, my, mc = _me()
    chip = 2 * mx + my
    dev = 4 * mx + 2 * my + mc
    T = x.shape[1]

    conv_sh = conv_w.reshape(CONVW, 3 * GW // N_CHIP)
    mine = jnp.concatenate([c.reshape(-1), conv_sh.reshape(-1), jnp.zeros((4 * LANE,), F32)]).reshape(24, LANE)
    got = _all_gather8(mine, "gather_c_conv")
    c8 = got[:, :8].reshape(N_DEV, D)
    conv_full = jnp.concatenate([got[2 * j, 8:20].reshape(CONVW, 3 * GW // N_CHIP) for j in range(N_CHIP)], axis=1)
    c16 = jnp.concatenate([c8, jnp.zeros((8, D), F32)], axis=0)
    b_sh = lax.dynamic_slice(b_ada, (0, chip * ADA_N), (1, ADA_N))
    mods = _all_gather8(_mod_part(c16, w_ada[0], b_sh), "gather_mod")
    mod = jnp.concatenate([lax.dynamic_slice(mods[2 * j], (dev, 0), (1, ADA_N)) for j in range(N_CHIP)], axis=1)

    as_rows = lambda t, transposed: t[0].T if transposed else t[0]
    transposed = (True, False, True, True, False)
    big_w = [as_rows(t, tr) for t, tr in zip((w_in, w_out, w_gate, w_up, w_down), transposed)]
    shards = [t.astype(BF16) for t in big_w]
    (a_in,) = _gather_weights(shards[:1])
    w_in_pt = _permute_w_in_t(a_in.reshape(PROJ, D))

    loss, grad_x, big, small = _local_step(
        x[0], loss_target[0], mod, norm1_w, w_in_pt, conv_full, a_log, dt_bias, gdn_norm_w,
        q_norm_w, k_norm_w, sinks, norm2_w, shards[1:])

    small["loss"] = loss[:, :1]
    sg = _all_gather8(_pack_small(small), "gather_small_grads")
    rep = dict(mod=(b_ada, m_b_ada, v_b_ada), norm1_w=(norm1_w, m_norm1_w, v_norm1_w),
               norm2_w=(norm2_w, m_norm2_w, v_norm2_w), a_log=(a_log, m_a_log, v_a_log),
               dt_bias=(dt_bias, m_dt_bias, v_dt_bias), gdn_norm_w=(gdn_norm_w, m_gdn_norm_w, v_gdn_norm_w),
               q_norm_w=(q_norm_w, m_q_norm_w, v_q_norm_w), k_norm_w=(k_norm_w, m_k_norm_w, v_k_norm_w),
               sinks=(sinks, m_sinks, v_sinks))
    wmv = [_pack_small({k: t[i] for k, t in rep.items()}) for i in range(3)]
    sres = _reduce_adamw(sg, wmv[0], wmv[1], wmv[2], "small_reduce_adamw")
    s_g, s_d, s_m, s_v = [_unpack_small(sres[i]) for i in range(4)]
    loss_out = s_g["loss"][0]

    g_conv = lax.dynamic_slice(s_g["conv_w"].reshape(CONVW, 3 * GW), (0, chip * (3 * GW // N_CHIP)),
                               (CONVW, 3 * GW // N_CHIP))
    pad16 = lambda t: jnp.concatenate([t.reshape(12, LANE), jnp.zeros((4, LANE), F32)], axis=0)
    cres = _adamw_call(pad16(g_conv), pad16(conv_w), pad16(m_conv_w), pad16(v_conv_w), "conv_adamw")
    conv_out = [g_conv.reshape(conv_w.shape)] + [cres[i, :12].reshape(conv_w.shape) for i in range(3)]

    dmod8 = sg[:, :6 * D // LANE].reshape(N_DEV, 6 * D)
    dm = lax.dynamic_slice(dmod8, (0, chip * ADA_N), (N_DEV, ADA_N))
    zpad = lambda t: jnp.concatenate([t, jnp.zeros((KPAD - N_DEV, t.shape[1]), F32)], axis=0)
    ares = _w_ada_update(zpad(c8), zpad(dm), w_ada[0], m_w_ada[0], v_w_ada[0])

    names = ("w_in", "w_out", "w_gate", "w_up", "w_down")
    g_full = [_reduce_swap(r, "reduce_" + nm) for r, nm in zip(big, names)]
    g_full[0] = g_full[0][:W_IN_ROWS]
    big_m = [as_rows(t, tr) for t, tr in zip((m_w_in, m_w_out, m_w_gate, m_w_up, m_w_down), transposed)]
    big_v = [as_rows(t, tr) for t, tr in zip((v_w_in, v_w_out, v_w_gate, v_w_up, v_w_down), transposed)]
    upd = [_adamw_big(g, w, m, v, "adamw_" + nm) for g, w, m, v, nm in zip(g_full, big_w, big_m, big_v, names)]
    back = lambda t, tr: (t.T if tr else t)[None]
    bg, bd, bm, bv = [[back(u[i], tr) for u, tr in zip(upd, transposed)] for i in range(4)]

    def group(a_i, small_d, conv_i, big_l):
        s = lambda k, ref: small_d[k].reshape(ref.shape)
        return [ares[a_i][None], s("mod", b_ada), s("norm1_w", norm1_w), big_l[0], conv_out[conv_i],
                s("a_log", a_log), s("dt_bias", dt_bias), s("gdn_norm_w", gdn_norm_w), s("q_norm_w", q_norm_w),
                s("k_norm_w", k_norm_w), s("sinks", sinks), big_l[1], s("norm2_w", norm2_w), big_l[2], big_l[3],
                big_l[4]]

    outs = [loss_out, grad_x[None]]
    outs += group(0, s_g, 0, bg) + group(1, s_d, 1, bd) + group(2, s_m, 2, bm) + group(3, s_v, 3, bv)
    return tuple(outs)
```

```python
import functools

import jax
import jax.numpy as jnp
from jax import lax
from jax.experimental import pallas as pl
from jax.experimental.pallas import tpu as pltpu

F32 = jnp.float32
BF16 = jnp.bfloat16
MESH = pl.DeviceIdType.MESH

D = 1024
HD = 64
GH = 8
GW = GH * HD
SQH = 8
SKVH = 2
SGRP = SQH // SKVH
WIN = 128
CONVW = 4
CHUNK = 64
DFF = 2816
PROJ = 2832
NP = 3072
EPS = 1e-6
N_DEV = 8
N_CHIP = 4

ADAM_LR = 0.001
ADAM_B1 = 0.9
ADAM_B2 = 0.999
ADAM_EPS = 1e-08
ADAM_WD = 0.01
ADAM_STEP = 10

VMEM_LIMIT = 48 * 1024 * 1024
GDN_BWD_VMEM = 58 * 1024 * 1024
LANE = 128

PACK_ROWS = (PROJ // N_CHIP, D // N_CHIP, DFF // N_CHIP, DFF // N_CHIP, DFF // N_CHIP)
PACK_P = 3104
PACK_H = PACK_P // 2


def _cparams(sem=None, vmem=VMEM_LIMIT):
    return pltpu.CompilerParams(dimension_semantics=sem, vmem_limit_bytes=vmem)


_NN = ((1,), (0,))
_NT = ((1,), (1,))
_TN = ((0,), (0,))


def _dot(a, b, dims):
    if a.ndim == 3:
        (ca,), (cb,) = dims
        return lax.dot_general(a, b, (((ca + 1,), (cb + 1,)), ((0,), (0,))), preferred_element_type=F32)
    return lax.dot_general(a, b, (dims, ((), ())), preferred_element_type=F32)


def _raw1(a, b, dims):
    return _dot(a.astype(BF16), b.astype(BF16), dims)


def _raw3(a, b, dims):
    ah = a.astype(BF16)
    al = (a - ah.astype(F32)).astype(BF16)
    bh = b.astype(BF16)
    bl = (b - bh.astype(F32)).astype(BF16)
    return _dot(ah, bh, dims) + (_dot(al, bh, dims) + _dot(ah, bl, dims))


def _make_diff_mm(raw):
    @jax.custom_vjp
    def nn(a, b):
        return raw(a, b, _NN)

    @jax.custom_vjp
    def nt(a, b):
        return raw(a, b, _NT)

    @jax.custom_vjp
    def tn(a, b):
        return raw(a, b, _TN)

    nn.defvjp(lambda a, b: (raw(a, b, _NN), (a, b)), lambda r, g: (nt(g, r[1]), tn(r[0], g)))
    nt.defvjp(lambda a, b: (raw(a, b, _NT), (a, b)), lambda r, g: (nn(g, r[1]), tn(g, r[0])))
    tn.defvjp(lambda a, b: (raw(a, b, _TN), (a, b)), lambda r, g: (nt(r[1], g), nn(r[0], g)))
    return nn, nt, tn


def _tri_inv_raw(a, nn3):
    n = a.shape[-1]
    ri = lax.broadcasted_iota(jnp.int32, (n, n), 0)
    ci = lax.broadcasted_iota(jnp.int32, (n, n), 1)
    t = (ri == ci).astype(F32)
    for lvl in range((n - 1).bit_length()):
        same_pair = (ri >> (lvl + 1)) == (ci >> (lvl + 1))
        lower_left = (((ri >> lvl) & 1) == 1) & (((ci >> lvl) & 1) == 0)
        y = jnp.where(same_pair & lower_left, a, 0.0)
        t = t - y if lvl == 0 else t - nn3(nn3(t, y), t)
    return t


class _Kit:
    def __init__(self, diff):
        if diff:
            self.nn, self.nt, self.tn = _make_diff_mm(_raw1)
            self.nn3, self.nt3, self.tn3 = _make_diff_mm(_raw3)
            nn3, nt3, tn3 = self.nn3, self.nt3, self.tn3

            @jax.custom_vjp
            def inv(a, t):
                return t

            def inv_fwd(a, t):
                return t, t

            def inv_bwd(t, g):
                return -tn3(t, nt3(g, t)), jnp.zeros_like(t)

            inv.defvjp(inv_fwd, inv_bwd)
            self.inv = inv
        else:
            self.nn = lambda a, b: _raw1(a, b, _NN)
            self.nt = lambda a, b: _raw1(a, b, _NT)
            self.tn = lambda a, b: _raw1(a, b, _TN)
            self.nn3 = lambda a, b: _raw3(a, b, _NN)
            self.nt3 = lambda a, b: _raw3(a, b, _NT)
            self.tn3 = lambda a, b: _raw3(a, b, _TN)
            self.inv = lambda a, t: _tri_inv_raw(a, self.nn3) if t is None else t


def _sigmoid(x):
    return 1.0 / (1.0 + jnp.exp(-x))


def _silu(x):
    return x * _sigmoid(x)


def _rms(x, w):
    return x * lax.rsqrt(jnp.mean(x * x, axis=-1, keepdims=True) + EPS) * w


def _tile(dim, target):
    t = (min(dim, target) // LANE) * LANE
    while t >= LANE:
        if dim % t == 0:
            return t
        t -= LANE
    return dim


MM_TM, MM_TN, MM_TK = 1408, 1536, 1408


def _matmul(a, b, ta=False, tb=False, out_dtype=F32, name="matmul", gather=None, exchange=None):
    carried = gather if gather is not None else exchange if exchange is not None else []
    nc = len(carried)
    if ta:
        K, M = a.shape
    else:
        M, K = a.shape
    if tb:
        N, K2 = b.shape
    else:
        K2, N = b.shape
    assert K == K2, (a.shape, b.shape, ta, tb)
    tm, tn, tk = _tile(M, MM_TM), _tile(N, MM_TN), _tile(K, MM_TK)
    nk = K // tk
    dims = ((0,) if ta else (1,), (1,) if tb else (0,))

    grid = (M // tm, N // tn, nk)

    def body(*refs):
        a_ref, b_ref = refs[:2]
        o_ref = refs[2 + nc]
        scratch = refs[3 + 2 * nc:]
        k = pl.program_id(2)
        if nc:
            make_plan = _gather_plan if gather is not None else _exchange_plan
            plan = make_plan(refs[2:2 + nc], refs[3 + nc:3 + 2 * nc], *scratch[-3:])
            at = lambda pos: ((pl.program_id(0) == pos[0]) & (pl.program_id(1) == pos[1]) & (k == pos[2]))

            @pl.when(at((0, 0, 0)))
            def _():
                _start(plan)

        part = _dot(a_ref[...].astype(BF16), b_ref[...].astype(BF16), dims)
        if nk == 1:
            o_ref[...] = part.astype(o_ref.dtype)
        else:
            acc_ref = scratch[0]

            @pl.when(k == 0)
            def _():
                acc_ref[...] = part

            @pl.when((k > 0) & (k < nk - 1))
            def _():
                acc_ref[...] += part

            @pl.when(k == nk - 1)
            def _():
                o_ref[...] = (acc_ref[...] + part).astype(o_ref.dtype)

        if nc:
            @pl.when(at((grid[0] - 1, grid[1] - 1, nk - 1)))
            def _():
                _finish(plan)

    a_spec = (pl.BlockSpec((tk, tm), lambda i, j, k: (k, i)) if ta
              else pl.BlockSpec((tm, tk), lambda i, j, k: (i, k)))
    b_spec = (pl.BlockSpec((tn, tk), lambda i, j, k: (j, k)) if tb
              else pl.BlockSpec((tk, tn), lambda i, j, k: (k, j)))
    if gather is not None:
        c_shapes, c_sems = _gather_shapes(carried), _gather_sems(nc)
    elif exchange is not None:
        c_shapes, c_sems = _exchange_shapes(carried), _exchange_sems(nc)
    else:
        c_shapes, c_sems = [], []
    res = pl.pallas_call(
        body, name=name, grid=grid,
        in_specs=[a_spec, b_spec] + _hbm_specs(nc),
        out_specs=[pl.BlockSpec((tm, tn), lambda i, j, k: (i, j))] + _hbm_specs(nc),
        out_shape=[jax.ShapeDtypeStruct((M, N), out_dtype)] + c_shapes,
        scratch_shapes=([pltpu.VMEM((tm, tn), F32)] if nk > 1 else []) + c_sems,
        compiler_params=_cparams(("arbitrary",) * 3 if nc else ("parallel", "parallel", "arbitrary")),
    )(a, b, *carried)
    return (res[0], res[1:]) if nc else res[0]


def _rowcall(fn, tiled, consts, out_tiled, out_acc, tm, name):
    T = tiled[0].shape[0]
    n_in = len(tiled) + len(consts)
    n_o = len(out_tiled)

    def body(*refs):
        vals = [r[...] for r in refs[:n_in]]
        outs = refs[n_in:]
        res = fn(*vals)
        for r, v in zip(outs[:n_o], res[:n_o]):
            r[...] = v.astype(r.dtype)
        if len(outs) > n_o:
            @pl.when(pl.program_id(0) == 0)
            def _():
                for r in outs[n_o:]:
                    r[...] = jnp.zeros_like(r)

            for r, v in zip(outs[n_o:], res[n_o:]):
                r[...] += v

    in_specs = [pl.BlockSpec((tm, a.shape[1]), lambda i: (i, 0)) for a in tiled]
    in_specs += [pl.BlockSpec(a.shape, lambda i, nd=a.ndim: (0,) * nd) for a in consts]
    out_specs = [pl.BlockSpec((tm, s.shape[1]), lambda i: (i, 0)) for s in out_tiled]
    out_specs += [pl.BlockSpec(s.shape, lambda i: (0, 0)) for s in out_acc]
    return pl.pallas_call(
        body, name=name, grid=(T // tm,),
        in_specs=in_specs, out_specs=out_specs,
        out_shape=list(out_tiled) + list(out_acc),
        compiler_params=_cparams(("arbitrary",)),
    )(*tiled, *consts)


def _sds(shape, dtype=F32):
    return jax.ShapeDtypeStruct(shape, dtype)


def _norm_mod(x, nw, scale, shift):
    return _rms(x, nw) * (1.0 + scale) + shift


def _norm_mod_fwd(x, nw, scale, shift):
    T = x.shape[0]
    (h,) = _rowcall(lambda *a: (_norm_mod(*a),), [x], [nw, scale, shift],
                    [_sds((T, D), BF16)], [], 512, "norm1_fwd")
    return h


ROWS_TM = 512
ROWS_EPI = 256


def _matmul_rows(a, b, epi, tiled, consts, out_tiled, out_acc, name, pieces=()):
    T, K = a.shape
    tm, tk = _tile(T, ROWS_TM), _tile(K, MM_TK)
    nm, nk = T // tm, K // tk
    npc, nt, ncst, no, na = len(pieces), len(tiled), len(consts), len(out_tiled), len(out_acc)
    n_in = 2 + nt + ncst

    def body(*refs):
        a_ref, b_ref = refs[:2]
        t_refs, c_refs = refs[2:2 + nt], refs[2 + nt:n_in]
        o_refs = refs[n_in + npc:n_in + npc + no]
        acc_refs = refs[n_in + npc + no:n_in + npc + no + na]
        n_out = no + na + npc
        res_ref = refs[n_in + npc + n_out]
        plan = _exchange_plan(refs[n_in:n_in + npc], refs[n_in + npc + no + na:n_in + npc + n_out],
                              *refs[n_in + npc + n_out + 1:]) if npc else None
        i, k = pl.program_id(0), pl.program_id(1)

        @pl.when((i == 0) & (k == 0))
        def _():
            for r in acc_refs:
                r[...] = jnp.zeros_like(r)
            if npc:
                _start(plan)

        part = _dot(a_ref[...], b_ref[...], _NN)

        @pl.when(k == 0)
        def _():
            res_ref[...] = part

        @pl.when(k > 0)
        def _():
            res_ref[...] += part

        @pl.when(k == nk - 1)
        def _():
            for r0 in range(0, tm, ROWS_EPI):
                rows = pl.ds(r0, ROWS_EPI)
                outs = epi(res_ref[rows, :], *[r[rows, :] for r in t_refs], *[r[...] for r in c_refs])
                for r, v in zip(o_refs, outs[:no]):
                    r[rows, :] = v.astype(r.dtype)
                for r, v in zip(acc_refs, outs[no:]):
                    r[...] += v

        if npc:
            @pl.when((i == nm - 1) & (k == nk - 1))
            def _():
                _finish(plan)

    row = lambda w: pl.BlockSpec((tm, w), lambda i, k: (i, 0))
    whole = lambda s: pl.BlockSpec(s.shape, lambda i, k: (0, 0))
    res = pl.pallas_call(
        body, name=name, grid=(nm, nk),
        in_specs=[pl.BlockSpec((tm, tk), lambda i, k: (i, k)), pl.BlockSpec((tk, D), lambda i, k: (k, 0))]
                 + [row(t.shape[1]) for t in tiled] + [whole(c) for c in consts] + _hbm_specs(npc),
        out_specs=[row(s.shape[1]) for s in out_tiled] + [whole(s) for s in out_acc] + _hbm_specs(npc),
        out_shape=list(out_tiled) + list(out_acc) + (_exchange_shapes(pieces) if npc else []),
        scratch_shapes=[pltpu.VMEM((tm, D), F32)] + (_exchange_sems(npc) if npc else []),
        compiler_params=_cparams(("arbitrary", "arbitrary")),
    )(a, b, *tiled, *consts, *pieces)
    return res[:no + na], res[no + na:]


def _in_proj_dx_norm_bwd(dproj, w_in_pt, x, dres, nw, scale, shift, pieces):
    T = x.shape[0]

    def epi(dh, x, dres, nw, scale, shift):
        _, vjp = jax.vjp(_norm_mod, x, nw, scale, shift)
        dx, dnw, dsc, dsh = vjp(dh)
        return dx + dres, dnw, dsc, dsh

    return _matmul_rows(dproj, w_in_pt, epi, [x, dres], [nw, scale, shift], [_sds((T, D))], [_sds((1, D))] * 3,
                        "in_proj_dx_norm1_bwd", pieces)


def _ffn_up_dx_resid_bwd(dab, w_gut, x, mixed, dy, gate1, nw, scale, shift):
    T = x.shape[0]

    def epi(dh2, x, mixed, dy, gate1, nw, scale, shift):
        _, vjp = jax.vjp(_resid_norm, x, mixed, gate1, nw, scale, shift)
        return vjp((dy, dh2))

    outs, _ = _matmul_rows(dab, w_gut, epi, [x, mixed, dy], [gate1, nw, scale, shift],
                           [_sds((T, D)), _sds((T, D), BF16)], [_sds((1, D))] * 4, "ffn_up_dx_resid_norm2_bwd")
    return outs


def _ffn_down_loss(act, w_down, x1, target, gate2):
    T = x1.shape[0]

    def epi(ffn, x1, target, gate2):
        y = x1 + gate2 * ffn
        err = y - target
        loss = 0.5 * jnp.sum(jnp.sum(err * err, axis=1, keepdims=True), axis=0, keepdims=True) / D
        dy = err * (1.0 / D)
        return dy, gate2 * dy, jnp.sum(dy * ffn, axis=0, keepdims=True), jnp.broadcast_to(loss, (1, LANE))

    outs, _ = _matmul_rows(act, w_down, epi, [x1, target], [gate2], [_sds((T, D)), _sds((T, D), BF16)],
                           [_sds((1, D)), _sds((1, LANE))], "ffn_down_loss")
    return outs


def _resid_norm(x, mixed, gate1, nw, scale, shift):
    x1 = x + gate1 * mixed
    return x1, _norm_mod(x1, nw, scale, shift)


def _resid_norm_fwd(x, mixed, gate1, nw, scale, shift):
    T = x.shape[0]
    return _rowcall(_resid_norm, [x, mixed], [gate1, nw, scale, shift],
                    [_sds((T, D)), _sds((T, D), BF16)], [], 512, "resid_norm2_fwd")


def _resid_norm_bwd(x, mixed, dy, dh2, gate1, nw, scale, shift):
    T = x.shape[0]

    def fn(x, mixed, dy, dh2, gate1, nw, scale, shift):
        _, vjp = jax.vjp(_resid_norm, x, mixed, gate1, nw, scale, shift)
        dx, dmixed, dg1, dnw, dsc, dsh = vjp((dy, dh2))
        return dx, dmixed, dg1, dnw, dsc, dsh

    return _rowcall(fn, [x, mixed, dy, dh2], [gate1, nw, scale, shift],
                    [_sds((T, D)), _sds((T, D), BF16)], [_sds((1, D))] * 4, 256, "resid_norm2_bwd")


def _ffn_act_fwd(ab):
    T = ab.shape[0]

    def fn(ab):
        a, b = ab[:, :DFF], ab[:, DFF:]
        return (_silu(a) * b,)

    (act,) = _rowcall(fn, [ab], [], [_sds((T, DFF), BF16)], [], 256, "ffn_act_fwd")
    return act


def _ffn_act_bwd(ab, dact):
    T = ab.shape[0]

    def fn(ab, dact):
        a, b = ab[:, :DFF], ab[:, DFF:]
        s = _sigmoid(a)
        da = dact * b * (s * (1.0 + a * (1.0 - s)))
        db = dact * (a * s)
        return (jnp.concatenate([da, db], axis=1),)

    (dab,) = _rowcall(fn, [ab, dact], [], [_sds((T, 2 * DFF), BF16)], [], 256, "ffn_act_bwd")
    return dab


def _loss_head(x1, ffn, target, gate2):
    T = x1.shape[0]

    def fn(x1, ffn, target, gate2):
        y = x1 + gate2 * ffn
        err = y - target
        loss = 0.5 * jnp.sum(jnp.sum(err * err, axis=1, keepdims=True), axis=0, keepdims=True) / D
        dy = err * (1.0 / D)
        dgate2 = jnp.sum(dy * ffn, axis=0, keepdims=True)
        return dy, gate2 * dy, dgate2, jnp.broadcast_to(loss, (1, LANE))

    return _rowcall(fn, [x1, ffn, target], [gate2], [_sds((T, D)), _sds((T, D), BF16)],
                    [_sds((1, D)), _sds((1, LANE))], 256, "loss_head")


def _round_bf16(x):
    return x.astype(BF16).astype(F32)


def _shift_down(x, s, rows):
    if s == 0:
        return x
    return jnp.where(rows >= s, pltpu.roll(x, s, 0), 0.0)


def _shift_up(x, s, rows, T):
    if s == 0:
        return x
    return jnp.where(rows < T - s, pltpu.roll(x, T - s, 0), 0.0)


def _conv_fwd(proj, conv_w):
    T = proj.shape[0]
    ncol = 3 * GW // LANE

    def body(x_ref, w_ref, o_ref):
        x = _round_bf16(x_ref[...])
        rows = lax.broadcasted_iota(jnp.int32, x.shape, 0)
        acc = jnp.zeros_like(x)
        for j in range(CONVW):
            acc = acc + _round_bf16(w_ref[pl.ds(j, 1), :]) * _shift_down(x, CONVW - 1 - j, rows)
        o_ref[0], o_ref[1] = _split_pair(_silu(acc))

    return pl.pallas_call(
        body, name="conv_fwd", grid=(ncol,),
        in_specs=[pl.BlockSpec((T, LANE), lambda j: (0, j)), pl.BlockSpec((CONVW, LANE), lambda j: (0, j))],
        out_specs=pl.BlockSpec((2, T, HD), lambda j: (j, 0, 0)),
        out_shape=_sds((3 * GH, T, HD)),
        compiler_params=_cparams(("parallel",)),
    )(proj, conv_w)


RELAYOUT_TM = 4096


def _split_pair(y):
    return y[:, :HD], pltpu.roll(y, HD, 1)[:, :HD]


def _merge_pair(a, b):
    return jnp.concatenate([a, b], axis=1)


def _split_heads(x, col_block0, nheads, name):
    T = x.shape[0]
    tm = _tile(T, RELAYOUT_TM)

    def body(x_ref, o_ref):
        a, b = _split_pair(x_ref[...])
        o_ref[0] = a
        o_ref[1] = b

    return pl.pallas_call(
        body, name=name, grid=(nheads // 2, T // tm),
        in_specs=[pl.BlockSpec((tm, LANE), lambda j, i: (i, col_block0 + j))],
        out_specs=pl.BlockSpec((2, tm, HD), lambda j, i: (j, i, 0)),
        out_shape=_sds((nheads, T, HD), x.dtype),
        compiler_params=_cparams(("parallel", "parallel")),
    )(x)


def _merge_heads(hm, out_dtype, name, into=None, col_block0=0, head0=0, nheads=None):
    T = hm.shape[1]
    nheads = hm.shape[0] if nheads is None else nheads
    tm = _tile(T, RELAYOUT_TM)

    def body(*refs):
        h_ref, o_ref = refs[0], refs[-1]
        o_ref[...] = _merge_pair(h_ref[0], h_ref[1]).astype(o_ref.dtype)

    in_specs = [pl.BlockSpec((2, tm, HD), lambda j, i: (head0 // 2 + j, i, 0))]
    args = [hm]
    if into is None:
        out_shape = _sds((T, HD * nheads), out_dtype)
        aliases = {}
    else:
        out_shape = _sds(into.shape, into.dtype)
        in_specs.append(pl.BlockSpec(memory_space=pl.ANY))
        args.append(into)
        aliases = {1: 0}
    return pl.pallas_call(
        body, name=name, grid=(nheads // 2, T // tm),
        in_specs=in_specs,
        out_specs=pl.BlockSpec((tm, LANE), lambda j, i: (i, col_block0 + j)),
        out_shape=out_shape, input_output_aliases=aliases,
        compiler_params=_cparams(("parallel", "parallel")),
    )(*args)


def _conv_bwd(proj, conv_w, dqc):
    T = proj.shape[0]
    ncol = 3 * GW // LANE

    def body(x_ref, w_ref, d_ref, dx_ref, dw_ref):
        x = _round_bf16(x_ref[...])
        rows = lax.broadcasted_iota(jnp.int32, x.shape, 0)
        xs = [_shift_down(x, CONVW - 1 - j, rows) for j in range(CONVW)]
        w = [_round_bf16(w_ref[pl.ds(j, 1), :]) for j in range(CONVW)]
        pre = jnp.zeros_like(x)
        for j in range(CONVW):
            pre = pre + w[j] * xs[j]
        s = _sigmoid(pre)
        dpre = _round_bf16(_merge_pair(d_ref[0], d_ref[1]) * (s * (1.0 + pre * (1.0 - s))))
        dx = jnp.zeros_like(x)
        for j in range(CONVW):
            dx = dx + w[j] * _shift_up(dpre, CONVW - 1 - j, rows, T)
            dw_ref[pl.ds(j, 1), :] = jnp.sum(dpre * xs[j], axis=0, keepdims=True)
        dx_ref[...] = dx.astype(dx_ref.dtype)

    return pl.pallas_call(
        body, name="conv_bwd", grid=(ncol,),
        in_specs=[pl.BlockSpec((T, LANE), lambda j: (0, j)), pl.BlockSpec((CONVW, LANE), lambda j: (0, j)),
                  pl.BlockSpec((2, T, HD), lambda j: (j, 0, 0))],
        out_specs=[pl.BlockSpec((T, LANE), lambda j: (0, j)), pl.BlockSpec((CONVW, LANE), lambda j: (0, j))],
        out_shape=[_sds((T, NP), BF16), _sds((CONVW, 3 * GW))],
        compiler_params=_cparams(("parallel",)),
    )(proj, conv_w, dqc)


def _gdn_prep(kit, q, k, v, ga, gb, alog, dtb, t_inv=None):
    C = CHUNK
    ri = lax.broadcasted_iota(jnp.int32, (C, C), 0)
    ci = lax.broadcasted_iota(jnp.int32, (C, C), 1)
    causal = ri >= ci
    strict = ri > ci
    eye = (ri == ci).astype(F32)
    lower = causal.astype(F32)
    upper = (ri <= ci).astype(F32)

    a = ga + dtb
    softplus = jnp.maximum(a, 0.0) + jnp.log(1.0 + jnp.exp(-jnp.abs(a)))
    g_row = -jnp.exp(alog) * softplus
    beta_row = _sigmoid(gb)
    g_col = jnp.sum(eye * g_row, axis=2, keepdims=True)
    beta_col = jnp.sum(eye * beta_row, axis=2, keepdims=True)
    G_col = jnp.sum(lower * g_row, axis=2, keepdims=True)
    G_row = jnp.sum(upper * g_col, axis=1, keepdims=True)
    G_last = jnp.sum(g_row, axis=2, keepdims=True)
    decay = jnp.exp(jnp.where(causal, G_col - G_row, -1e30))

    qn = q * lax.rsqrt(jnp.sum(q * q, axis=-1, keepdims=True) + EPS) * (HD ** -0.5)
    kn = k * lax.rsqrt(jnp.sum(k * k, axis=-1, keepdims=True) + EPS)
    kb = kn * beta_col
    A = jnp.where(strict, kit.nt(kb, kn) * decay, 0.0)
    Tm = kit.inv(A, t_inv)
    eG = jnp.exp(G_col)
    u = kit.nn3(Tm, v * beta_col)
    w = kit.nn3(Tm, kb * eG)
    qk = jnp.where(causal, kit.nt(qn, kn) * decay, 0.0)
    q_dec = qn * eG
    k_dec = kn * jnp.exp(G_last - G_col)
    dec = jnp.exp(G_last)
    return u, w, qk, q_dec, k_dec, dec, Tm


def _gdn_out(o, z, nw):
    return _rms(o, nw) * _silu(z)


GDN_CB = 4


def _gdn_specs(T, blk):
    TB = GDN_CB * CHUNK
    seq = lambda grp: pl.BlockSpec((GH, TB, HD), lambda i, grp=grp: (grp, blk(i), 0))
    row = lambda grp: pl.BlockSpec((GH, GDN_CB, 1, CHUNK), lambda i, grp=grp: (grp, blk(i), 0, 0))
    per_head = pl.BlockSpec((GH, 1, CHUNK), lambda i: (0, 0, 0))
    whole = pl.BlockSpec((1, HD), lambda i: (0, 0))
    state = pl.BlockSpec((GH, GDN_CB, HD, HD), lambda i: (0, blk(i), 0, 0))
    return seq, row, per_head, whole, state


def _gdn_load(seq_refs, row_refs, head_refs):
    chunks = lambda r: jnp.concatenate([r[:, pl.ds(cb * CHUNK, CHUNK), :] for cb in range(GDN_CB)], axis=0)
    rows = lambda r: jnp.concatenate([r[:, cb] for cb in range(GDN_CB)], axis=0)
    heads = lambda r: jnp.concatenate([r[...]] * GDN_CB, axis=0)
    return [chunks(r) for r in seq_refs], [rows(r) for r in row_refs], [heads(r) for r in head_refs]


def _gdn_fwd(qkv_hm, zs_hm, gab, alog_b, dtb_b, nw, shards):
    T = qkv_hm.shape[1]
    N = T // CHUNK
    nblk = N // GDN_CB
    ns = len(shards)
    seq, row, per_head, whole, state = _gdn_specs(T, lambda i: i)
    kit = _Kit(False)

    def body(*refs):
        q_ref, k_ref, v_ref, z_ref, ga_ref, gb_ref, al_ref, dt_ref, nw_ref = refs[:9]
        o_ref, S_ref, T_ref = refs[9 + ns:12 + ns]
        S_scr = refs[12 + 2 * ns]
        plan = _gather_plan(refs[9:9 + ns], refs[12 + ns:12 + 2 * ns], *refs[13 + 2 * ns:])

        @pl.when(pl.program_id(0) == 0)
        def _():
            S_scr[...] = jnp.zeros_like(S_scr)
            _start(plan)

        (q, k, v, z), (ga, gb), (al, dt) = _gdn_load((q_ref, k_ref, v_ref, z_ref), (ga_ref, gb_ref), (al_ref, dt_ref))
        u, w, qk, q_dec, k_dec, dec, t_inv = _gdn_prep(kit, q, k, v, ga, gb, al, dt)
        S = S_scr[...]
        for cb in range(GDN_CB):
            hs = slice(cb * GH, (cb + 1) * GH)
            S_ref[:, cb] = S
            T_ref[:, cb] = t_inv[hs]
            v_new = u[hs] - kit.nn(w[hs], S)
            o = kit.nn(q_dec[hs], S) + kit.nn(qk[hs], v_new)
            S = S * dec[hs] + kit.tn(k_dec[hs], v_new)
            o_ref[:, pl.ds(cb * CHUNK, CHUNK), :] = _gdn_out(o, z[hs], nw_ref[...])
        S_scr[...] = S

        @pl.when(pl.program_id(0) == nblk - 1)
        def _():
            _finish(plan)

    res = pl.pallas_call(
        body, name="gdn_fwd", grid=(nblk,),
        in_specs=[seq(0), seq(1), seq(2), seq(0), row(0), row(1), per_head, per_head, whole] + _hbm_specs(ns),
        out_specs=[seq(0), state, state] + _hbm_specs(ns),
        out_shape=[_sds((GH + SQH, T, HD)), _sds((GH, N, HD, HD)), _sds((GH, N, CHUNK, CHUNK))]
                  + _gather_shapes(shards),
        scratch_shapes=[pltpu.VMEM((GH, HD, HD), F32)] + _gather_sems(ns),
        compiler_params=_cparams(("arbitrary",)),
    )(qkv_hm, qkv_hm, qkv_hm, zs_hm, gab, gab, alog_b, dtb_b, nw, *shards)
    return res[0], (res[1], res[2]), res[3:]


def _gdn_bwd(qkv_hm, zs_hm, gab, alog_b, dtb_b, nw, S_all, do, pieces):
    T = qkv_hm.shape[1]
    N = T // CHUNK
    nblk = N // GDN_CB
    npc = len(pieces)
    dkit, kit = _Kit(True), _Kit(False)
    rseq, rrow, per_head, whole, rstate = _gdn_specs(T, lambda i: nblk - 1 - i)

    def body(*refs):
        q_ref, k_ref, v_ref, z_ref, ga_ref, gb_ref, al_ref, dt_ref, nw_ref, S_ref, T_ref, do_ref = refs[:12]
        dqkv_ref, dz_ref, dga_ref, dgb_ref, dal_ref, ddt_ref, dnw_ref = refs[12 + npc:19 + npc]
        dS_scr = refs[19 + 2 * npc]
        plan = _exchange_plan(refs[12:12 + npc], refs[19 + npc:19 + 2 * npc], *refs[20 + 2 * npc:])

        @pl.when(pl.program_id(0) == 0)
        def _():
            dS_scr[...] = jnp.zeros_like(dS_scr)
            dal_ref[...] = jnp.zeros_like(dal_ref)
            ddt_ref[...] = jnp.zeros_like(ddt_ref)
            dnw_ref[...] = jnp.zeros_like(dnw_ref)
            _start(plan)

        (q, k, v, z, dout), (ga, gb), (al, dt) = _gdn_load((q_ref, k_ref, v_ref, z_ref, do_ref), (ga_ref, gb_ref),
                                                          (al_ref, dt_ref))
        S_in = jnp.concatenate([S_ref[:, cb] for cb in range(GDN_CB)], axis=0)
        t_inv = jnp.concatenate([T_ref[:, cb] for cb in range(GDN_CB)], axis=0)
        prep = lambda *a: _gdn_prep(dkit, *a, t_inv=t_inv)[:6]
        (u, w, qk, q_dec, k_dec, dec), prep_vjp = jax.vjp(prep, q, k, v, ga, gb, al, dt)
        v_new = u - kit.nn(w, S_in)
        o = kit.nn(q_dec, S_in) + kit.nn(qk, v_new)
        _, out_vjp = jax.vjp(_gdn_out, o, z, nw_ref[...])
        do, dz, dnw = out_vjp(dout)
        dvn_part = kit.tn(qk, do)
        dS_part = kit.tn(q_dec, do)
        dS = dS_scr[...]
        dS_out, dvn = [None] * GDN_CB, [None] * GDN_CB
        for cb in reversed(range(GDN_CB)):
            hs = slice(cb * GH, (cb + 1) * GH)
            dS_out[cb] = dS
            dvn[cb] = dvn_part[hs] + kit.nn(k_dec[hs], dS)
            dS = dS * dec[hs] + dS_part[hs] - kit.tn(w[hs], dvn[cb])
        dS_scr[...] = dS
        dS_out = jnp.concatenate(dS_out, axis=0)
        dvn = jnp.concatenate(dvn, axis=0)
        ddec = jnp.sum(jnp.sum(S_in * dS_out, axis=2, keepdims=True), axis=1, keepdims=True)
        cts = (dvn, -kit.nt(dvn, S_in), kit.nt(do, v_new), kit.nt(do, S_in), kit.nt(v_new, dS_out), ddec)
        dq, dk, dv, dga, dgb, dal, ddt = prep_vjp(cts)
        lanesum = lambda t: jnp.broadcast_to(jnp.sum(t, axis=2, keepdims=True), t.shape)
        for cb in range(GDN_CB):
            hs = slice(cb * GH, (cb + 1) * GH)
            sl = pl.ds(cb * CHUNK, CHUNK)
            dqkv_ref[pl.ds(0, GH), sl, :] = dq[hs]
            dqkv_ref[pl.ds(GH, GH), sl, :] = dk[hs]
            dqkv_ref[pl.ds(2 * GH, GH), sl, :] = dv[hs]
            dz_ref[:, sl, :] = dz[hs]
            dga_ref[:, cb] = dga[hs]
            dgb_ref[:, cb] = dgb[hs]
            dal_ref[...] += lanesum(dal[hs])
            ddt_ref[...] += lanesum(ddt[hs])
        dnw_ref[...] += dnw

        @pl.when(pl.program_id(0) == nblk - 1)
        def _():
            _finish(plan)

    res = pl.pallas_call(
        body, name="gdn_bwd", grid=(nblk,),
        in_specs=[rseq(0), rseq(1), rseq(2), rseq(0), rrow(0), rrow(1), per_head, per_head, whole, rstate, rstate,
                  rseq(0)] + _hbm_specs(npc),
        out_specs=[pl.BlockSpec((3 * GH, GDN_CB * CHUNK, HD), lambda i: (0, nblk - 1 - i, 0)), rseq(0), rrow(0),
                   rrow(0), per_head, per_head, whole] + _hbm_specs(npc),
        out_shape=[_sds((3 * GH, T, HD)), _sds((GH + 4 + SWA_GRAD_HEADS, T, HD))] + [_sds((GH, N, 1, CHUNK))] * 2
                  + [_sds((GH, 1, CHUNK))] * 2 + [_sds((1, HD))] + _exchange_shapes(pieces),
        scratch_shapes=[pltpu.VMEM((GH, HD, HD), F32)] + _exchange_sems(npc),
        compiler_params=_cparams(("arbitrary",), GDN_BWD_VMEM),
    )(qkv_hm, qkv_hm, qkv_hm, zs_hm, gab, gab, alog_b, dtb_b, nw, S_all[0], S_all[1], do, *pieces)
    return res[:7], res[7:]


def _swa_block(kit, first, q0, q1, q2, q3, kp, kc, vp, vc, qnw, knw, s0, s1, s2, s3, *, slopes):
    W = WIN
    ri = lax.broadcasted_iota(jnp.int32, (W, W), 0)
    ci = lax.broadcasted_iota(jnp.int32, (W, W), 1)
    mask_c = ri >= ci
    mask_p = ci > ri + first * W
    dist_c = (ri - ci).astype(F32)
    dist_p = (ri - ci + W).astype(F32)
    kpn = _rms(kp, knw)
    kcn = _rms(kc, knw)
    outs = []
    for q, sink, slope in zip((q0, q1, q2, q3), (s0, s1, s2, s3), slopes):
        qn = _rms(q, qnw)
        sc = jnp.where(mask_c, kit.nt(qn, kcn) * (HD ** -0.5) - slope * dist_c, -1e30)
        sp = jnp.where(mask_p, kit.nt(qn, kpn) * (HD ** -0.5) - slope * dist_p, -1e30)
        m = jnp.maximum(jnp.maximum(jnp.max(sc, axis=-1, keepdims=True), jnp.max(sp, axis=-1, keepdims=True)), sink)
        m = lax.stop_gradient(m)
        pc = jnp.exp(sc - m)
        pp = jnp.exp(sp - m)
        den = jnp.sum(pc, axis=-1, keepdims=True) + jnp.sum(pp, axis=-1, keepdims=True) + jnp.exp(sink - m)
        inv = 1.0 / den
        outs.append(kit.nn(pc * inv, vc) + kit.nn(pp * inv, vp))
    return tuple(outs)


def _swa_slopes(hk):
    return tuple(jnp.where(hk == 0, 2.0 ** (-8.0 * (g + 1.0) / SQH), 2.0 ** (-8.0 * (SGRP + g + 1.0) / SQH))
                 for g in range(SGRP))


def _swa_fwd(zs_hm, qnw, knw, sinks_col):
    T = zs_hm.shape[1]
    NB = T // WIN
    kit = _Kit(False)

    def body(q_ref, kp_ref, kc_ref, vp_ref, vc_ref, qnw_ref, knw_ref, s_ref, o_ref):
        hk = pl.program_id(0)
        first = (pl.program_id(1) == 0).astype(jnp.int32)
        args = ([q_ref[g] for g in range(SGRP)] + [kp_ref[...], kc_ref[...], vp_ref[...], vc_ref[...],
                                                     qnw_ref[...], knw_ref[...]] + [s_ref[g] for g in range(SGRP)])
        outs = _swa_block(kit, first, *args, slopes=_swa_slopes(hk))
        for g in range(SGRP):
            o_ref[g] = outs[g]

    qspec = pl.BlockSpec((SGRP, WIN, HD), lambda hk, n: (2 + hk, n, 0))
    cur = lambda off: pl.BlockSpec((None, WIN, HD), lambda hk, n, off=off: (off + hk, n, 0))
    prev = lambda off: pl.BlockSpec((None, WIN, HD), lambda hk, n, off=off: (off + hk, jnp.maximum(n - 1, 0), 0))
    whole = pl.BlockSpec((1, HD), lambda hk, n: (0, 0))
    sspec = pl.BlockSpec((SGRP, WIN, 1), lambda hk, n: (hk, 0, 0))
    return pl.pallas_call(
        body, name="swa_fwd", grid=(SKVH, NB),
        in_specs=[qspec, prev(16), cur(16), prev(18), cur(18), whole, whole, sspec],
        out_specs=pl.BlockSpec((SGRP, WIN, HD), lambda hk, n: (hk, n, 0)),
        out_shape=_sds((SQH, T, HD)),
        compiler_params=_cparams(("parallel", "arbitrary")),
    )(zs_hm, zs_hm, zs_hm, zs_hm, zs_hm, qnw, knw, sinks_col)


def _swa_bwd(zs_hm, qnw, knw, sinks_col, do):
    T = zs_hm.shape[1]
    NB = T // WIN
    kit = _Kit(True)

    def body(q_ref, kp_ref, kc_ref, vp_ref, vc_ref, qnw_ref, knw_ref, s_ref, do_ref,
             dq_ref, dk_ref, dv_ref, dqnw_ref, dknw_ref, ds_ref, ck_scr, cv_scr):
        hk = pl.program_id(0)
        i = pl.program_id(1)
        first = (i == NB - 1).astype(jnp.int32)

        @pl.when(i == 0)
        def _():
            ck_scr[...] = jnp.zeros_like(ck_scr)
            cv_scr[...] = jnp.zeros_like(cv_scr)
            ds_ref[...] = jnp.zeros_like(ds_ref)

        @pl.when((i == 0) & (hk == 0))
        def _():
            dqnw_ref[...] = jnp.zeros_like(dqnw_ref)
            dknw_ref[...] = jnp.zeros_like(dknw_ref)

        args = ([q_ref[g] for g in range(SGRP)] + [kp_ref[...], kc_ref[...], vp_ref[...], vc_ref[...],
                                                     qnw_ref[...], knw_ref[...]] + [s_ref[g] for g in range(SGRP)])
        dos = tuple(do_ref[g] for g in range(SGRP))
        _, vjp = jax.vjp(functools.partial(_swa_block, kit, first, slopes=_swa_slopes(hk)), *args)
        gr = vjp(dos)
        for g in range(SGRP):
            dq_ref[g] = gr[g]
            ds_ref[g] += jnp.broadcast_to(jnp.sum(gr[10 + g], axis=0, keepdims=True), (WIN, 1))
        dkp, dkc, dvp, dvc = gr[4:8]
        dk_ref[...] = dkc + ck_scr[...]
        dv_ref[...] = dvc + cv_scr[...]
        ck_scr[...] = dkp
        cv_scr[...] = dvp
        dqnw_ref[...] += gr[8]
        dknw_ref[...] += gr[9]

    rn = lambda n: NB - 1 - n
    qspec = pl.BlockSpec((SGRP, WIN, HD), lambda hk, i: (2 + hk, rn(i), 0))
    cur = lambda off: pl.BlockSpec((None, WIN, HD), lambda hk, i, off=off: (off + hk, rn(i), 0))
    prev = lambda off: pl.BlockSpec((None, WIN, HD), lambda hk, i, off=off: (off + hk, jnp.maximum(rn(i) - 1, 0), 0))
    whole = pl.BlockSpec((1, HD), lambda hk, i: (0, 0))
    sspec = pl.BlockSpec((SGRP, WIN, 1), lambda hk, i: (hk, 0, 0))
    ospec = pl.BlockSpec((SGRP, WIN, HD), lambda hk, i: (hk, rn(i), 0))
    return pl.pallas_call(
        body, name="swa_bwd", grid=(SKVH, NB),
        in_specs=[qspec, prev(16), cur(16), prev(18), cur(18), whole, whole, sspec, ospec],
        out_specs=[ospec, cur(0), cur(0), whole, whole, sspec],
        out_shape=[_sds((SQH, T, HD)), _sds((SKVH, T, HD)), _sds((SKVH, T, HD)),
                   _sds((1, HD)), _sds((1, HD)), _sds((SQH, WIN, 1))],
        scratch_shapes=[pltpu.VMEM((WIN, HD), F32), pltpu.VMEM((WIN, HD), F32)],
        compiler_params=_cparams(("arbitrary", "arbitrary")),
    )(zs_hm, zs_hm, zs_hm, zs_hm, zs_hm, qnw, knw, sinks_col, do)


def _swa_heads(kit, first, q, kp, kc, vp, vc, qnw, knw, sink, slope):
    W = WIN
    ri = lax.broadcasted_iota(jnp.int32, (W, W), 0)
    ci = lax.broadcasted_iota(jnp.int32, (W, W), 1)
    mask_c = ri >= ci
    mask_p = ci > ri + first * W
    dist_c = (ri - ci).astype(F32)
    dist_p = (ri - ci + W).astype(F32)
    kpn = _rms(kp, knw)
    kcn = _rms(kc, knw)
    qn = _rms(q, qnw)
    sc = jnp.where(mask_c, kit.nt(qn, kcn) * (HD ** -0.5) - slope * dist_c, -1e30)
    sp = jnp.where(mask_p, kit.nt(qn, kpn) * (HD ** -0.5) - slope * dist_p, -1e30)
    m = jnp.maximum(jnp.maximum(jnp.max(sc, axis=-1, keepdims=True), jnp.max(sp, axis=-1, keepdims=True)), sink)
    m = lax.stop_gradient(m)
    pc = jnp.exp(sc - m)
    pp = jnp.exp(sp - m)
    den = jnp.sum(pc, axis=-1, keepdims=True) + jnp.sum(pp, axis=-1, keepdims=True) + jnp.exp(sink - m)
    inv = 1.0 / den
    return kit.nn(pc * inv, vc) + kit.nn(pp * inv, vp)


def _per_query_head(kv_ref):
    return jnp.concatenate([kv_ref[pl.ds(h // SGRP, 1)] for h in range(SQH)], axis=0)


def _per_kv_head(d):
    return jnp.concatenate([jnp.sum(d[g * SGRP:(g + 1) * SGRP], axis=0, keepdims=True) for g in range(SKVH)], axis=0)


def _swa_specs(blk):
    qspec = pl.BlockSpec((SQH, WIN, HD), lambda i: (1, blk(i), 0))
    cur = lambda grp: pl.BlockSpec((SKVH, WIN, HD), lambda i, grp=grp: (grp, blk(i), 0))
    prev = lambda grp: pl.BlockSpec((SKVH, WIN, HD), lambda i, grp=grp: (grp, jnp.maximum(blk(i) - 1, 0), 0))
    whole = pl.BlockSpec((1, HD), lambda i: (0, 0))
    col = pl.BlockSpec((SQH, WIN, 1), lambda i: (0, 0, 0))
    ospec = pl.BlockSpec((SQH, WIN, HD), lambda i: (0, blk(i), 0))
    return qspec, cur, prev, whole, col, ospec


def _swa_fwd(zs_hm, qnw, knw, sinks_col, slopes_col, o_buf, shards):
    T = zs_hm.shape[1]
    NB = T // WIN
    ns = len(shards)
    kit = _Kit(False)
    qspec, cur, prev, whole, col, _ = _swa_specs(lambda i: i)

    def body(*refs):
        q_ref, kp_ref, kc_ref, vp_ref, vc_ref, qnw_ref, knw_ref, s_ref, sl_ref = refs[:9]
        o_ref = refs[10 + ns]
        plan = _gather_plan(refs[10:10 + ns], refs[11 + ns:11 + 2 * ns], *refs[11 + 2 * ns:])

        @pl.when(pl.program_id(0) == 0)
        def _():
            _start(plan)

        first = (pl.program_id(0) == 0).astype(jnp.int32)
        o_ref[...] = _swa_heads(kit, first, q_ref[...], _per_query_head(kp_ref), _per_query_head(kc_ref),
                                _per_query_head(vp_ref), _per_query_head(vc_ref), qnw_ref[...], knw_ref[...],
                                s_ref[...], sl_ref[...])

        @pl.when(pl.program_id(0) == NB - 1)
        def _():
            _finish(plan)

    res = pl.pallas_call(
        body, name="swa_fwd", grid=(NB,),
        in_specs=[qspec, prev(8), cur(8), prev(9), cur(9), whole, whole, col, col] + _hbm_specs(1 + ns),
        out_specs=[pl.BlockSpec((SQH, WIN, HD), lambda i: (1, i, 0))] + _hbm_specs(ns),
        out_shape=[_sds(o_buf.shape)] + _gather_shapes(shards),
        input_output_aliases={9: 0},
        scratch_shapes=_gather_sems(ns),
        compiler_params=_cparams(("arbitrary",)),
    )(zs_hm, zs_hm, zs_hm, zs_hm, zs_hm, qnw, knw, sinks_col, slopes_col, o_buf, *shards)
    return res[0], res[1:]


SWA_GRAD_HEADS = SQH + 2 * SKVH


def _swa_bwd(zs_hm, qnw, knw, sinks_col, slopes_col, dmix_hm, d_buf):
    T = zs_hm.shape[1]
    NB = T // WIN
    kit = _Kit(True)
    qspec, cur, prev, whole, col, _ = _swa_specs(lambda i: NB - 1 - i)

    def body(q_ref, kp_ref, kc_ref, vp_ref, vc_ref, qnw_ref, knw_ref, s_ref, sl_ref, do_ref, buf_ref,
             d_ref, dqnw_ref, dknw_ref, ds_ref, ck_scr, cv_scr):
        dq_ref = d_ref.at[pl.ds(0, SQH)]
        dk_ref = d_ref.at[pl.ds(SQH, SKVH)]
        dv_ref = d_ref.at[pl.ds(SQH + SKVH, SKVH)]
        i = pl.program_id(0)
        first = (i == NB - 1).astype(jnp.int32)

        @pl.when(i == 0)
        def _():
            ck_scr[...] = jnp.zeros_like(ck_scr)
            cv_scr[...] = jnp.zeros_like(cv_scr)
            ds_ref[...] = jnp.zeros_like(ds_ref)
            dqnw_ref[...] = jnp.zeros_like(dqnw_ref)
            dknw_ref[...] = jnp.zeros_like(dknw_ref)

        fn = lambda q, kp, kc, vp, vc, qnw, knw, sink: _swa_heads(kit, first, q, kp, kc, vp, vc, qnw, knw, sink,
                                                                  sl_ref[...])
        _, vjp = jax.vjp(fn, q_ref[...], _per_query_head(kp_ref), _per_query_head(kc_ref), _per_query_head(vp_ref),
                         _per_query_head(vc_ref), qnw_ref[...], knw_ref[...], s_ref[...])
        dq, dkp, dkc, dvp, dvc, dqnw, dknw, dsink = vjp(do_ref[...])
        dq_ref[...] = dq
        dk_ref[...] = _per_kv_head(dkc) + ck_scr[...]
        dv_ref[...] = _per_kv_head(dvc) + cv_scr[...]
        ck_scr[...] = _per_kv_head(dkp)
        cv_scr[...] = _per_kv_head(dvp)
        dqnw_ref[...] += dqnw
        dknw_ref[...] += dknw
        ds_ref[...] += jnp.broadcast_to(jnp.sum(dsink, axis=1, keepdims=True), dsink.shape)

    dospec = pl.BlockSpec((SQH, WIN, HD), lambda i: (1, NB - 1 - i, 0))
    dspec = pl.BlockSpec((SWA_GRAD_HEADS, WIN, HD), lambda i: (1, NB - 1 - i, 0))
    res = pl.pallas_call(
        body, name="swa_bwd", grid=(NB,),
        in_specs=[qspec, prev(8), cur(8), prev(9), cur(9), whole, whole, col, col, dospec] + _hbm_specs(1),
        out_specs=[dspec, whole, whole, col],
        out_shape=[_sds(d_buf.shape), _sds((1, HD)), _sds((1, HD)), _sds((SQH, WIN, 1))],
        input_output_aliases={10: 0},
        scratch_shapes=[pltpu.VMEM((SKVH, WIN, HD), F32), pltpu.VMEM((SKVH, WIN, HD), F32)],
        compiler_params=_cparams(("arbitrary",)),
    )(zs_hm, zs_hm, zs_hm, zs_hm, zs_hm, qnw, knw, sinks_col, slopes_col, dmix_hm, d_buf)
    return res


GAB0 = 3 * GW + 1280


W_IN_ROWS = PROJ // N_CHIP
W_IN_ROWS_PAD = 736


def _permute_w_in_t(w_in_t):
    return jnp.concatenate([w_in_t[:4 * GW], w_in_t[4 * GW + 2 * GH:], w_in_t[4 * GW:4 * GW + 2 * GH],
                            jnp.zeros((NP - PROJ, D), w_in_t.dtype)], axis=0)


def _w_in_grad_pieces(g_t):
    g = jnp.concatenate([g_t[:4 * GW], g_t[GAB0:GAB0 + 2 * GH], g_t[4 * GW:GAB0]], axis=0)
    g = jnp.pad(g.reshape(N_CHIP, W_IN_ROWS, D), ((0, 0), (0, W_IN_ROWS_PAD - W_IN_ROWS), (0, 0)))
    return g.reshape(N_CHIP, 2, W_IN_ROWS_PAD // 2, D)


def _pieces_by_rows(g):
    return g.reshape(N_CHIP, 2, g.shape[0] // (2 * N_CHIP), D)


def _local_step(x, target, mod, n1w, w_in_pt, conv_w, alog, dtb, gnw, qnw, knw, sinks, n2w, shards):
    sh_out, sh_gate, sh_up, sh_down = shards
    T = x.shape[0]
    N = T // CHUNK
    shift1, scale1, gate1, shift2, scale2, gate2 = [mod[:, i * D:(i + 1) * D] for i in range(6)]

    h = _norm_mod_fwd(x, n1w, scale1, shift1)
    proj, (a_out,) = _matmul(h, w_in_pt, tb=True, name="in_proj", gather=[sh_out])
    w_out = a_out.reshape(D, D)
    qkv_hm = _conv_fwd(proj, conv_w)
    zs_hm = _split_heads(proj, 3 * GW // LANE, 20, "split_zs")
    gab = proj[:, GAB0:GAB0 + 2 * GH].T.reshape(2 * GH, N, 1, CHUNK)
    alog_b = jnp.broadcast_to(alog.reshape(GH, 1, 1), (GH, 1, CHUNK))
    dtb_b = jnp.broadcast_to(dtb.reshape(GH, 1, 1), (GH, 1, CHUNK))
    sinks_col = jnp.broadcast_to(sinks.reshape(SQH, 1, 1), (SQH, WIN, 1))
    o_hm, S_all, (a_gate, a_up) = _gdn_fwd(qkv_hm, zs_hm, gab, alog_b, dtb_b, gnw, [sh_gate, sh_up])
    w_gut = jnp.concatenate([a_gate.reshape(DFF, D), a_up.reshape(DFF, D)], axis=0)
    slopes = 2.0 ** (-8.0 * (jnp.arange(SQH, dtype=F32) + 1.0) / SQH)
    slopes_col = jnp.broadcast_to(slopes.reshape(SQH, 1, 1), (SQH, WIN, 1))
    o_hm, (a_down,) = _swa_fwd(zs_hm, qnw, knw, sinks_col, slopes_col, o_hm, [sh_down])
    w_down = a_down.reshape(DFF, D)
    mixcat = _merge_heads(o_hm, BF16, "merge_mix")
    mixed = _matmul(mixcat, w_out, name="out_proj")
    x1, h2 = _resid_norm_fwd(x, mixed, gate1, n2w, scale2, shift2)
    ab = _matmul(h2, w_gut, tb=True, name="ffn_up")
    act = _ffn_act_fwd(ab)
    dy, dffn, dgate2, loss = _ffn_down_loss(act, w_down, x1, target, gate2)

    dact = _matmul(dffn, w_down, tb=True, name="ffn_down_dx")
    dab = _ffn_act_bwd(ab, dact)
    g_w_down = _matmul(act, dffn, ta=True, out_dtype=BF16, name="ffn_down_dw")
    g_w_gut = _matmul(dab, h2, ta=True, out_dtype=BF16, name="ffn_up_dw")
    dx1, dmixed, dgate1, dn2w, dscale2, dshift2 = _ffn_up_dx_resid_bwd(dab, w_gut, x, mixed, dy, gate1, n2w, scale2,
                                                                       shift2)
    g_w_out = _matmul(mixcat, dmixed, ta=True, out_dtype=BF16, name="out_proj_dw")
    dmix_hm = _split_heads(_matmul(dmixed, w_out, tb=True, name="out_proj_dx"), 0, GH + SQH, "split_dmix")
    pieces = [_pieces_by_rows(g_w_out), _pieces_by_rows(g_w_gut[:DFF]), _pieces_by_rows(g_w_gut[DFF:]),
              _pieces_by_rows(g_w_down)]
    (dqkv_hm, d_hm, dga, dgb, dalog, ddtb, dgnw), recv = _gdn_bwd(qkv_hm, zs_hm, gab, alog_b, dtb_b, gnw, S_all,
                                                                  dmix_hm, pieces)
    d_hm, dqnw, dknw, dsinks = _swa_bwd(zs_hm, qnw, knw, sinks_col, slopes_col, dmix_hm, d_hm)
    dproj, dconv = _conv_bwd(proj, conv_w, dqkv_hm)
    dproj = _merge_heads(d_hm, BF16, "merge_dz", into=dproj, col_block0=3 * GW // LANE, head0=0, nheads=GH)
    dproj = _merge_heads(d_hm, BF16, "merge_dswa", into=dproj, col_block0=4 * GW // LANE, head0=GH + 4,
                         nheads=SWA_GRAD_HEADS)
    dgab = jnp.concatenate([dga, dgb], axis=0).reshape(2 * GH, T).T.astype(BF16)
    dproj = lax.dynamic_update_slice(dproj, jnp.concatenate([dgab, jnp.zeros((T, NP - PROJ), BF16)], axis=1),
                                     (0, GAB0))
    g_w_in_pt = _matmul(dproj, h, ta=True, out_dtype=BF16, name="in_proj_dw")
    (grad_x, dn1w, dscale1, dshift1), recv_in = _in_proj_dx_norm_bwd(dproj, w_in_pt, x, dx1, n1w, scale1, shift1,
                                                                     [_w_in_grad_pieces(g_w_in_pt)])

    dmod = jnp.concatenate([dshift1, dscale1, dgate1, dshift2, dscale2, dgate2], axis=1)
    big = list(recv_in) + list(recv)
    small = dict(mod=dmod, norm1_w=dn1w, norm2_w=dn2w, conv_w=dconv, a_log=dalog[:, 0, 0], dt_bias=ddtb[:, 0, 0],
                 gdn_norm_w=dgnw, q_norm_w=dqnw, k_norm_w=dknw, sinks=dsinks[:, 0, 0])
    return loss, grad_x, big, small


def _adamw(w, g, m, v):
    m2 = ADAM_B1 * m + (1.0 - ADAM_B1) * g
    v2 = ADAM_B2 * v + (1.0 - ADAM_B2) * (g * g)
    m_hat = m2 / (1.0 - ADAM_B1 ** ADAM_STEP)
    v_hat = v2 / (1.0 - ADAM_B2 ** ADAM_STEP)
    delta = -ADAM_LR * (m_hat / (jnp.sqrt(v_hat) + ADAM_EPS) + ADAM_WD * w)
    return delta, m2, v2


def _reduce_adamw(recv, w, m, v, name):
    _, R, C = recv.shape
    tc = _tile(C, 256)

    def body(r_ref, w_ref, m_ref, v_ref, o_ref):
        g = r_ref[0].astype(F32)
        for s in range(1, N_DEV):
            g = g + r_ref[s].astype(F32)
        delta, m2, v2 = _adamw(w_ref[...], g, m_ref[...], v_ref[...])
        o_ref[0] = g
        o_ref[1] = delta
        o_ref[2] = m2
        o_ref[3] = v2

    col = pl.BlockSpec((R, tc), lambda j: (0, j))
    return pl.pallas_call(
        body, name=name, grid=(C // tc,),
        in_specs=[pl.BlockSpec((N_DEV, R, tc), lambda j: (0, 0, j)), col, col, col],
        out_specs=pl.BlockSpec((4, R, tc), lambda j: (0, 0, j)),
        out_shape=_sds((4, R, C)),
        compiler_params=_cparams(("parallel",)),
    )(recv, w, m, v)


def _adamw_call(g, w, m, v, name):
    def body(g_ref, w_ref, m_ref, v_ref, o_ref):
        delta, m2, v2 = _adamw(w_ref[...], g_ref[...], m_ref[...], v_ref[...])
        o_ref[0] = delta
        o_ref[1] = m2
        o_ref[2] = v2

    return pl.pallas_call(body, name=name, out_shape=_sds((3,) + g.shape))(g, w, m, v)


ADA_N = 6 * D // N_CHIP
KPAD = 128


def _mod_part(c8, w_ada, b_ada):
    tn = 512

    def body(c_ref, w_ref, b_ref, o_ref):
        o_ref[...] = _raw1(_silu(c_ref[...]), w_ref[...], _NN) + b_ref[...]

    return pl.pallas_call(
        body, name="ada_mod", grid=(ADA_N // tn,),
        in_specs=[pl.BlockSpec((16, D), lambda j: (0, 0)), pl.BlockSpec((D, tn), lambda j: (0, j)),
                  pl.BlockSpec((1, tn), lambda j: (0, j))],
        out_specs=pl.BlockSpec((16, tn), lambda j: (0, j)),
        out_shape=_sds((16, ADA_N)),
        compiler_params=_cparams(("parallel",)),
    )(c8, w_ada, b_ada)


def _w_ada_update(c8p, dm, w, m, v):
    tr = 256

    def body(c_ref, dm_ref, w_ref, m_ref, v_ref, g_ref, d_ref, m2_ref, v2_ref):
        g = _raw1(_silu(c_ref[...]), dm_ref[...], _TN)
        delta, m2, v2 = _adamw(w_ref[...], g, m_ref[...], v_ref[...])
        g_ref[...] = g
        d_ref[...] = delta
        m2_ref[...] = m2
        v2_ref[...] = v2

    blk = pl.BlockSpec((tr, ADA_N), lambda i: (i, 0))
    return pl.pallas_call(
        body, name="w_ada_update", grid=(D // tr,),
        in_specs=[pl.BlockSpec((KPAD, tr), lambda i: (0, i)), pl.BlockSpec((KPAD, ADA_N), lambda i: (0, 0)),
                  blk, blk, blk],
        out_specs=[blk] * 4, out_shape=[_sds((D, ADA_N))] * 4,
        compiler_params=_cparams(("parallel",)),
    )(c8p, dm, w, m, v)


def _me():
    return lax.axis_index("x"), lax.axis_index("y"), lax.axis_index("c")


def _peer(k, me):
    mx, my, mc = me
    return (1 - mx if k & 4 else mx, 1 - my if k & 2 else my, 1 - mc if k & 1 else mc)


def _lin(p):
    return 4 * p[0] + 2 * p[1] + p[2]


def _remote(src, dst, ssem, rsem, dev):
    return pltpu.make_async_remote_copy(src_ref=src, dst_ref=dst, send_sem=ssem, recv_sem=rsem,
                                        device_id=dev, device_id_type=MESH)


def _all_gather8(x, name):
    def body(x_ref, out_ref, send_sems, recv_sems):
        me = _me()
        out_ref[_lin(me)] = x_ref[...]
        sends = []
        for k in range(1, N_DEV):
            cp = _remote(x_ref, out_ref.at[_lin(me)], send_sems.at[k - 1], recv_sems.at[k - 1], _peer(k, me))
            cp.start()
            sends.append(cp)
        for k in range(1, N_DEV):
            p = _peer(k, me)
            _remote(x_ref, out_ref.at[_lin(p)], send_sems.at[k - 1], recv_sems.at[k - 1], p).wait_recv()
        for cp in sends:
            cp.wait_send()

    return pl.pallas_call(
        body, name=name,
        out_shape=_sds((N_DEV,) + x.shape, x.dtype),
        in_specs=[pl.BlockSpec(memory_space=pltpu.VMEM)],
        out_specs=pl.BlockSpec(memory_space=pltpu.VMEM),
        scratch_shapes=[pltpu.SemaphoreType.DMA((N_DEV - 1,)), pltpu.SemaphoreType.DMA((N_DEV - 1,))],
    )(x)


def _hbm_specs(n):
    return [pl.BlockSpec(memory_space=pl.ANY)] * n


def _gather_weights(shards):
    n = len(shards)

    def body(*refs):
        plan = _gather_plan(refs[:n], refs[n:2 * n], *refs[2 * n:])
        _start(plan)
        _finish(plan)

    return pl.pallas_call(
        body, name="gather_weights",
        out_shape=_gather_shapes(shards), in_specs=_hbm_specs(n), out_specs=_hbm_specs(n),
        scratch_shapes=_gather_sems(n),
    )(*shards)


def _gather_shapes(shards):
    return [_sds((N_CHIP,) + s.shape, s.dtype) for s in shards]


def _gather_sems(n):
    return [pltpu.SemaphoreType.DMA((3 * n,)), pltpu.SemaphoreType.DMA((3 * n,)), pltpu.SemaphoreType.DMA((n,))]


def _gather_plan(ins, outs, send_sems, recv_sems, local_sems):
    mx, my, mc = _me()
    chips = [(1 - mx, my), (mx, 1 - my), (1 - mx, 1 - my)]
    local, sends, recvs = [], [], []
    for a in range(len(ins)):
        local.append(pltpu.make_async_copy(ins[a], outs[a].at[2 * mx + my], local_sems.at[a]))
        for k, (px, py) in enumerate(chips):
            sems = (send_sems.at[3 * a + k], recv_sems.at[3 * a + k], (px, py, mc))
            sends.append(_remote(ins[a], outs[a].at[2 * mx + my], *sems))
            recvs.append(_remote(ins[a], outs[a].at[2 * px + py], *sems))
    return local, sends, recvs


def _start(plan):
    local, sends, _ = plan
    for cp in local + sends:
        cp.start()


def _finish(plan):
    local, sends, recvs = plan
    for cp in recvs:
        cp.wait_recv()
    for cp in sends:
        cp.wait_send()
    for cp in local:
        cp.wait()


def _grad_exchange(pieces):
    n = len(pieces)

    def body(*refs):
        plan = _exchange_plan(refs[:n], refs[n:2 * n], *refs[2 * n:])
        _start(plan)
        _finish(plan)

    return pl.pallas_call(
        body, name="grad_exchange",
        out_shape=_exchange_shapes(pieces), in_specs=_hbm_specs(n), out_specs=_hbm_specs(n),
        scratch_shapes=_exchange_sems(n),
    )(*pieces)


def _exchange_shapes(pieces):
    return [_sds((N_DEV,) + p.shape[2:], p.dtype) for p in pieces]


def _exchange_sems(n):
    return [pltpu.SemaphoreType.DMA(((N_DEV - 1) * n,)), pltpu.SemaphoreType.DMA(((N_DEV - 1) * n,)),
            pltpu.SemaphoreType.DMA((n,))]


def _exchange_plan(ins, outs, send_sems, recv_sems, local_sems):
    me = _me()
    mx, my, mc = me
    local, sends, recvs = [], [], []
    for a in range(len(ins)):
        local.append(pltpu.make_async_copy(ins[a].at[2 * mx + my, mc], outs[a].at[_lin(me)], local_sems.at[a]))
        for k in range(1, N_DEV):
            p = _peer(k, me)
            s = (N_DEV - 1) * a + k - 1
            sends.append(_remote(ins[a].at[2 * p[0] + p[1], p[2]], outs[a].at[_lin(me)], send_sems.at[s],
                                 recv_sems.at[s], p))
            recvs.append(_remote(ins[a].at[2 * mx + my, mc], outs[a].at[_lin(p)], send_sems.at[s],
                                 recv_sems.at[s], p))
    return local, sends, recvs


def _reduce_swap(recv, name):
    _, rows, cols = recv.shape

    def body(r_ref, o_ref, send_sem, recv_sem):
        mx, my, mc = _me()
        sib = (mx, my, 1 - mc)
        g = r_ref[0].astype(F32)
        for s in range(1, N_DEV):
            g = g + r_ref[s].astype(F32)
        mine = o_ref.at[pl.ds(pl.multiple_of(mc * rows, 8), rows)]
        theirs = o_ref.at[pl.ds(pl.multiple_of((1 - mc) * rows, 8), rows)]
        mine[...] = g
        cp = _remote(mine, mine, send_sem, recv_sem, sib)
        cp.start()
        _remote(mine, theirs, send_sem, recv_sem, sib).wait_recv()
        cp.wait_send()

    return pl.pallas_call(
        body, name=name, out_shape=_sds((2 * rows, cols)),
        in_specs=[pl.BlockSpec(memory_space=pltpu.VMEM)], out_specs=pl.BlockSpec(memory_space=pltpu.VMEM),
        scratch_shapes=[pltpu.SemaphoreType.DMA, pltpu.SemaphoreType.DMA],
        compiler_params=_cparams(),
    )(recv)


def _adamw_big(g, w, m, v, name):
    rows, cols = g.shape
    tr = next((t for t in (256, 176, 128, 64, 8) if rows % t == 0), None)
    if tr is None:
        tc = _tile(cols, 256)
        blk, grid = pl.BlockSpec((rows, tc), lambda i: (0, i)), (cols // tc,)
    else:
        blk, grid = pl.BlockSpec((tr, cols), lambda i: (i, 0)), (rows // tr,)

    def body(g_ref, w_ref, m_ref, v_ref, go_ref, d_ref, m2_ref, v2_ref):
        g = g_ref[...]
        delta, m2, v2 = _adamw(w_ref[...], g, m_ref[...], v_ref[...])
        go_ref[...] = g
        d_ref[...] = delta
        m2_ref[...] = m2
        v2_ref[...] = v2

    return pl.pallas_call(
        body, name=name, grid=grid,
        in_specs=[blk] * 4, out_specs=[blk] * 4, out_shape=[_sds((rows, cols))] * 4,
        compiler_params=_cparams(("parallel",)),
    )(g, w, m, v)


SMALL_ORDER = (("mod", 6 * D), ("norm1_w", D), ("norm2_w", D), ("conv_w", CONVW * 3 * GW), ("a_log", GH),
               ("dt_bias", GH), ("gdn_norm_w", HD), ("q_norm_w", HD), ("k_norm_w", HD), ("sinks", SQH), ("loss", 1))
SMALL_R = 120


def _pack_small(d):
    parts = [d[k].reshape(-1).astype(F32) if k in d else jnp.zeros((n,), F32) for k, n in SMALL_ORDER]
    used = sum(n for _, n in SMALL_ORDER)
    parts.append(jnp.zeros((SMALL_R * LANE - used,), F32))
    return jnp.concatenate(parts).reshape(SMALL_R, LANE)


def _unpack_small(pk):
    flat = pk.reshape(-1)
    out, r = {}, 0
    for k, n in SMALL_ORDER:
        out[k] = flat[r:r + n]
        r += n
    return out


def kernel(x, c, w_ada, b_ada, norm1_w, w_in, conv_w, a_log, dt_bias, gdn_norm_w, q_norm_w, k_norm_w, sinks, w_out, norm2_w, w_gate, w_up, w_down, loss_target, m_w_ada, m_b_ada, m_norm1_w, m_w_in, m_conv_w, m_a_log, m_dt_bias, m_gdn_norm_w, m_q_norm_w, m_k_norm_w, m_sinks, m_w_out, m_norm2_w, m_w_gate, m_w_up, m_w_down, v_w_ada, v_b_ada, v_norm1_w, v_w_in, v_conv_w, v_a_log, v_dt_bias, v_gdn_norm_w, v_q_norm_w, v_k_norm_w, v_sinks, v_w_out, v_norm2_w, v_w_gate, v_w_up, v_w_down):
    mx, my, mc = _me()
    chip = 2 * mx + my
    dev = 4 * mx + 2 * my + mc
    T = x.shape[1]

    conv_sh = conv_w.reshape(CONVW, 3 * GW // N_CHIP)
    mine = jnp.concatenate([c.reshape(-1), conv_sh.reshape(-1), jnp.zeros((4 * LANE,), F32)]).reshape(24, LANE)
    got = _all_gather8(mine, "gather_c_conv")
    c8 = got[:, :8].reshape(N_DEV, D)
    conv_full = jnp.concatenate([got[2 * j, 8:20].reshape(CONVW, 3 * GW // N_CHIP) for j in range(N_CHIP)], axis=1)
    c16 = jnp.concatenate([c8, jnp.zeros((8, D), F32)], axis=0)
    b_sh = lax.dynamic_slice(b_ada, (0, chip * ADA_N), (1, ADA_N))
    mods = _all_gather8(_mod_part(c16, w_ada[0], b_sh), "gather_mod")
    mod = jnp.concatenate([lax.dynamic_slice(mods[2 * j], (dev, 0), (1, ADA_N)) for j in range(N_CHIP)], axis=1)

    as_rows = lambda t, transposed: t[0].T if transposed else t[0]
    transposed = (True, False, True, True, False)
    big_w = [as_rows(t, tr) for t, tr in zip((w_in, w_out, w_gate, w_up, w_down), transposed)]
    shards = [t.astype(BF16) for t in big_w]
    (a_in,) = _gather_weights(shards[:1])
    w_in_pt = _permute_w_in_t(a_in.reshape(PROJ, D))

    loss, grad_x, big, small = _local_step(
        x[0], loss_target[0], mod, norm1_w, w_in_pt, conv_full, a_log, dt_bias, gdn_norm_w,
        q_norm_w, k_norm_w, sinks, norm2_w, shards[1:])

    small["loss"] = loss[:, :1]
    sg = _all_gather8(_pack_small(small), "gather_small_grads")
    rep = dict(mod=(b_ada, m_b_ada, v_b_ada), norm1_w=(norm1_w, m_norm1_w, v_norm1_w),
               norm2_w=(norm2_w, m_norm2_w, v_norm2_w), a_log=(a_log, m_a_log, v_a_log),
               dt_bias=(dt_bias, m_dt_bias, v_dt_bias), gdn_norm_w=(gdn_norm_w, m_gdn_norm_w, v_gdn_norm_w),
               q_norm_w=(q_norm_w, m_q_norm_w, v_q_norm_w), k_norm_w=(k_norm_w, m_k_norm_w, v_k_norm_w),
               sinks=(sinks, m_sinks, v_sinks))
    wmv = [_pack_small({k: t[i] for k, t in rep.items()}) for i in range(3)]
    sres = _reduce_adamw(sg, wmv[0], wmv[1], wmv[2], "small_reduce_adamw")
    s_g, s_d, s_m, s_v = [_unpack_small(sres[i]) for i in range(4)]
    loss_out = s_g["loss"][0]

    g_conv = lax.dynamic_slice(s_g["conv_w"].reshape(CONVW, 3 * GW), (0, chip * (3 * GW // N_CHIP)),
                               (CONVW, 3 * GW // N_CHIP))
    pad16 = lambda t: jnp.concatenate([t.reshape(12, LANE), jnp.zeros((4, LANE), F32)], axis=0)
    cres = _adamw_call(pad16(g_conv), pad16(conv_w), pad16(m_conv_w), pad16(v_conv_w), "conv_adamw")
    conv_out = [g_conv.reshape(conv_w.shape)] + [cres[i, :12].reshape(conv_w.shape) for i in range(3)]

    dmod8 = sg[:, :6 * D // LANE].reshape(N_DEV, 6 * D)
    dm = lax.dynamic_slice(dmod8, (0, chip * ADA_N), (N_DEV, ADA_N))
    zpad = lambda t: jnp.concatenate([t, jnp.zeros((KPAD - N_DEV, t.shape[1]), F32)], axis=0)
    ares = _w_ada_update(zpad(c8), zpad(dm), w_ada[0], m_w_ada[0], v_w_ada[0])

    names = ("w_in", "w_out", "w_gate", "w_up", "w_down")
    g_full = [_reduce_swap(r, "reduce_" + nm) for r, nm in zip(big, names)]
    g_full[0] = g_full[0][:W_IN_ROWS]
    big_m = [as_rows(t, tr) for t, tr in zip((m_w_in, m_w_out, m_w_gate, m_w_up, m_w_down), transposed)]
    big_v = [as_rows(t, tr) for t, tr in zip((v_w_in, v_w_out, v_w_gate, v_w_up, v_w_down), transposed)]
    upd = [_adamw_big(g, w, m, v, "adamw_" + nm) for g, w, m, v, nm in zip(g_full, big_w, big_m, big_v, names)]
    back = lambda t, tr: (t.T if tr else t)[None]
    bg, bd, bm, bv = [[back(u[i], tr) for u, tr in zip(upd, transposed)] for i in range(4)]

    def group(a_i, small_d, conv_i, big_l):
        s = lambda k, ref: small_d[k].reshape(ref.shape)
        return [ares[a_i][None], s("mod", b_ada), s("norm1_w", norm1_w), big_l[0], conv_out[conv_i],
                s("a_log", a_log), s("dt_bias", dt_bias), s("gdn_norm_w", gdn_norm_w), s("q_norm_w", q_norm_w),
                s("k_norm_w", k_norm_w), s("sinks", sinks), big_l[1], s("norm2_w", norm2_w), big_l[2], big_l[3],
                big_l[4]]

    outs = [loss_out, grad_x[None]]
    outs += group(0, s_g, 0, bg) + group(1, s_d, 1, bd) + group(2, s_m, 2, bm) + group(3, s_v, 3, bv)
    return tuple(outs)
```

```python
import functools

import jax
import jax.numpy as jnp
from jax import lax
from jax.experimental import pallas as pl
from jax.experimental.pallas import tpu as pltpu

F32 = jnp.float32
BF16 = jnp.bfloat16
MESH = pl.DeviceIdType.MESH

D = 1024
HD = 64
GH = 8
GW = GH * HD
SQH = 8
SKVH = 2
SGRP = SQH // SKVH
WIN = 128
CONVW = 4
CHUNK = 64
DFF = 2816
PROJ = 2832
NP = 3072
EPS = 1e-6
N_DEV = 8
N_CHIP = 4

ADAM_LR = 0.001
ADAM_B1 = 0.9
ADAM_B2 = 0.999
ADAM_EPS = 1e-08
ADAM_WD = 0.01
ADAM_STEP = 10

VMEM_LIMIT = 48 * 1024 * 1024
GDN_BWD_VMEM = 58 * 1024 * 1024
LANE = 128

PACK_ROWS = (PROJ // N_CHIP, D // N_CHIP, DFF // N_CHIP, DFF // N_CHIP, DFF // N_CHIP)
PACK_P = 3104
PACK_H = PACK_P // 2


def _cparams(sem=None, vmem=VMEM_LIMIT):
    return pltpu.CompilerParams(dimension_semantics=sem, vmem_limit_bytes=vmem)


_NN = ((1,), (0,))
_NT = ((1,), (1,))
_TN = ((0,), (0,))


def _dot(a, b, dims):
    if a.ndim == 3:
        (ca,), (cb,) = dims
        return lax.dot_general(a, b, (((ca + 1,), (cb + 1,)), ((0,), (0,))), preferred_element_type=F32)
    return lax.dot_general(a, b, (dims, ((), ())), preferred_element_type=F32)


def _raw1(a, b, dims):
    return _dot(a.astype(BF16), b.astype(BF16), dims)


def _raw3(a, b, dims):
    ah = a.astype(BF16)
    al = (a - ah.astype(F32)).astype(BF16)
    bh = b.astype(BF16)
    bl = (b - bh.astype(F32)).astype(BF16)
    return _dot(ah, bh, dims) + (_dot(al, bh, dims) + _dot(ah, bl, dims))


def _make_diff_mm(raw):
    @jax.custom_vjp
    def nn(a, b):
        return raw(a, b, _NN)

    @jax.custom_vjp
    def nt(a, b):
        return raw(a, b, _NT)

    @jax.custom_vjp
    def tn(a, b):
        return raw(a, b, _TN)

    nn.defvjp(lambda a, b: (raw(a, b, _NN), (a, b)), lambda r, g: (nt(g, r[1]), tn(r[0], g)))
    nt.defvjp(lambda a, b: (raw(a, b, _NT), (a, b)), lambda r, g: (nn(g, r[1]), tn(g, r[0])))
    tn.defvjp(lambda a, b: (raw(a, b, _TN), (a, b)), lambda r, g: (nt(r[1], g), nn(r[0], g)))
    return nn, nt, tn


def _tri_inv_raw(a, nn3):
    n = a.shape[-1]
    ri = lax.broadcasted_iota(jnp.int32, (n, n), 0)
    ci = lax.broadcasted_iota(jnp.int32, (n, n), 1)
    t = (ri == ci).astype(F32)
    for lvl in range((n - 1).bit_length()):
        same_pair = (ri >> (lvl + 1)) == (ci >> (lvl + 1))
        lower_left = (((ri >> lvl) & 1) == 1) & (((ci >> lvl) & 1) == 0)
        y = jnp.where(same_pair & lower_left, a, 0.0)
        t = t - y if lvl == 0 else t - nn3(nn3(t, y), t)
    return t


class _Kit:
    def __init__(self, diff):
        if diff:
            self.nn, self.nt, self.tn = _make_diff_mm(_raw1)
            self.nn3, self.nt3, self.tn3 = _make_diff_mm(_raw3)
            nn3, nt3, tn3 = self.nn3, self.nt3, self.tn3

            @jax.custom_vjp
            def inv(a, t):
                return t

            def inv_fwd(a, t):
                return t, t

            def inv_bwd(t, g):
                return -tn3(t, nt3(g, t)), jnp.zeros_like(t)

            inv.defvjp(inv_fwd, inv_bwd)
            self.inv = inv
        else:
            self.nn = lambda a, b: _raw1(a, b, _NN)
            self.nt = lambda a, b: _raw1(a, b, _NT)
            self.tn = lambda a, b: _raw1(a, b, _TN)
            self.nn3 = lambda a, b: _raw3(a, b, _NN)
            self.nt3 = lambda a, b: _raw3(a, b, _NT)
            self.tn3 = lambda a, b: _raw3(a, b, _TN)
            self.inv = lambda a, t: _tri_inv_raw(a, self.nn3) if t is None else t


def _sigmoid(x):
    return 1.0 / (1.0 + jnp.exp(-x))


def _silu(x):
    return x * _sigmoid(x)


def _rms(x, w):
    return x * lax.rsqrt(jnp.mean(x * x, axis=-1, keepdims=True) + EPS) * w


def _tile(dim, target):
    t = (min(dim, target) // LANE) * LANE
    while t >= LANE:
        if dim % t == 0:
            return t
        t -= LANE
    return dim


MM_TM, MM_TN, MM_TK = 1408, 1536, 1408


def _matmul(a, b, ta=False, tb=False, out_dtype=F32, name="matmul", gather=None, exchange=None):
    carried = gather if gather is not None else exchange if exchange is not None else []
    nc = len(carried)
    if ta:
        K, M = a.shape
    else:
        M, K = a.shape
    if tb:
        N, K2 = b.shape
    else:
        K2, N = b.shape
    assert K == K2, (a.shape, b.shape, ta, tb)
    tm, tn, tk = _tile(M, MM_TM), _tile(N, MM_TN), _tile(K, MM_TK)
    nk = K // tk
    dims = ((0,) if ta else (1,), (1,) if tb else (0,))

    grid = (M // tm, N // tn, nk)

    def body(*refs):
        a_ref, b_ref = refs[:2]
        o_ref = refs[2 + nc]
        scratch = refs[3 + 2 * nc:]
        k = pl.program_id(2)
        if nc:
            make_plan = _gather_plan if gather is not None else _exchange_plan
            plan = make_plan(refs[2:2 + nc], refs[3 + nc:3 + 2 * nc], *scratch[-3:])
            at = lambda pos: ((pl.program_id(0) == pos[0]) & (pl.program_id(1) == pos[1]) & (k == pos[2]))

            @pl.when(at((0, 0, 0)))
            def _():
                _start(plan)

        part = _dot(a_ref[...].astype(BF16), b_ref[...].astype(BF16), dims)
        if nk == 1:
            o_ref[...] = part.astype(o_ref.dtype)
        else:
            acc_ref = scratch[0]

            @pl.when(k == 0)
            def _():
                acc_ref[...] = part

            @pl.when((k > 0) & (k < nk - 1))
            def _():
                acc_ref[...] += part

            @pl.when(k == nk - 1)
            def _():
                o_ref[...] = (acc_ref[...] + part).astype(o_ref.dtype)

        if nc:
            @pl.when(at((grid[0] - 1, grid[1] - 1, nk - 1)))
            def _():
                _finish(plan)

    a_spec = (pl.BlockSpec((tk, tm), lambda i, j, k: (k, i)) if ta
              else pl.BlockSpec((tm, tk), lambda i, j, k: (i, k)))
    b_spec = (pl.BlockSpec((tn, tk), lambda i, j, k: (j, k)) if tb
              else pl.BlockSpec((tk, tn), lambda i, j, k: (k, j)))
    if gather is not None:
        c_shapes, c_sems = _gather_shapes(carried), _gather_sems(nc)
    elif exchange is not None:
        c_shapes, c_sems = _exchange_shapes(carried), _exchange_sems(nc)
    else:
        c_shapes, c_sems = [], []
    res = pl.pallas_call(
        body, name=name, grid=grid,
        in_specs=[a_spec, b_spec] + _hbm_specs(nc),
        out_specs=[pl.BlockSpec((tm, tn), lambda i, j, k: (i, j))] + _hbm_specs(nc),
        out_shape=[jax.ShapeDtypeStruct((M, N), out_dtype)] + c_shapes,
        scratch_shapes=([pltpu.VMEM((tm, tn), F32)] if nk > 1 else []) + c_sems,
        compiler_params=_cparams(("arbitrary",) * 3 if nc else ("parallel", "parallel", "arbitrary")),
    )(a, b, *carried)
    return (res[0], res[1:]) if nc else res[0]


def _rowcall(fn, tiled, consts, out_tiled, out_acc, tm, name):
    T = tiled[0].shape[0]
    n_in = len(tiled) + len(consts)
    n_o = len(out_tiled)

    def body(*refs):
        vals = [r[...] for r in refs[:n_in]]
        outs = refs[n_in:]
        res = fn(*vals)
        for r, v in zip(outs[:n_o], res[:n_o]):
            r[...] = v.astype(r.dtype)
        if len(outs) > n_o:
            @pl.when(pl.program_id(0) == 0)
            def _():
                for r in outs[n_o:]:
                    r[...] = jnp.zeros_like(r)

            for r, v in zip(outs[n_o:], res[n_o:]):
                r[...] += v

    in_specs = [pl.BlockSpec((tm, a.shape[1]), lambda i: (i, 0)) for a in tiled]
    in_specs += [pl.BlockSpec(a.shape, lambda i, nd=a.ndim: (0,) * nd) for a in consts]
    out_specs = [pl.BlockSpec((tm, s.shape[1]), lambda i: (i, 0)) for s in out_tiled]
    out_specs += [pl.BlockSpec(s.shape, lambda i: (0, 0)) for s in out_acc]
    return pl.pallas_call(
        body, name=name, grid=(T // tm,),
        in_specs=in_specs, out_specs=out_specs,
        out_shape=list(out_tiled) + list(out_acc),
        compiler_params=_cparams(("arbitrary",)),
    )(*tiled, *consts)


def _sds(shape, dtype=F32):
    return jax.ShapeDtypeStruct(shape, dtype)


def _norm_mod(x, nw, scale, shift):
    return _rms(x, nw) * (1.0 + scale) + shift


def _norm_mod_fwd(x, nw, scale, shift):
    T = x.shape[0]
    (h,) = _rowcall(lambda *a: (_norm_mod(*a),), [x], [nw, scale, shift],
                    [_sds((T, D), BF16)], [], 512, "norm1_fwd")
    return h


ROWS_TM = 512
ROWS_EPI = 256


def _matmul_rows(a, b, epi, tiled, consts, out_tiled, out_acc, name, pieces=()):
    T, K = a.shape
    tm, tk = _tile(T, ROWS_TM), _tile(K, MM_TK)
    nm, nk = T // tm, K // tk
    npc, nt, ncst, no, na = len(pieces), len(tiled), len(consts), len(out_tiled), len(out_acc)
    n_in = 2 + nt + ncst

    def body(*refs):
        a_ref, b_ref = refs[:2]
        t_refs, c_refs = refs[2:2 + nt], refs[2 + nt:n_in]
        o_refs = refs[n_in + npc:n_in + npc + no]
        acc_refs = refs[n_in + npc + no:n_in + npc + no + na]
        n_out = no + na + npc
        res_ref = refs[n_in + npc + n_out]
        plan = _exchange_plan(refs[n_in:n_in + npc], refs[n_in + npc + no + na:n_in + npc + n_out],
                              *refs[n_in + npc + n_out + 1:]) if npc else None
        i, k = pl.program_id(0), pl.program_id(1)

        @pl.when((i == 0) & (k == 0))
        def _():
            for r in acc_refs:
                r[...] = jnp.zeros_like(r)
            if npc:
                _start(plan)

        part = _dot(a_ref[...], b_ref[...], _NN)

        @pl.when(k == 0)
        def _():
            res_ref[...] = part

        @pl.when(k > 0)
        def _():
            res_ref[...] += part

        @pl.when(k == nk - 1)
        def _():
            for r0 in range(0, tm, ROWS_EPI):
                rows = pl.ds(r0, ROWS_EPI)
                outs = epi(res_ref[rows, :], *[r[rows, :] for r in t_refs], *[r[...] for r in c_refs])
                for r, v in zip(o_refs, outs[:no]):
                    r[rows, :] = v.astype(r.dtype)
                for r, v in zip(acc_refs, outs[no:]):
                    r[...] += v

        if npc:
            @pl.when((i == nm - 1) & (k == nk - 1))
            def _():
                _finish(plan)

    row = lambda w: pl.BlockSpec((tm, w), lambda i, k: (i, 0))
    whole = lambda s: pl.BlockSpec(s.shape, lambda i, k: (0, 0))
    res = pl.pallas_call(
        body, name=name, grid=(nm, nk),
        in_specs=[pl.BlockSpec((tm, tk), lambda i, k: (i, k)), pl.BlockSpec((tk, D), lambda i, k: (k, 0))]
                 + [row(t.shape[1]) for t in tiled] + [whole(c) for c in consts] + _hbm_specs(npc),
        out_specs=[row(s.shape[1]) for s in out_tiled] + [whole(s) for s in out_acc] + _hbm_specs(npc),
        out_shape=list(out_tiled) + list(out_acc) + (_exchange_shapes(pieces) if npc else []),
        scratch_shapes=[pltpu.VMEM((tm, D), F32)] + (_exchange_sems(npc) if npc else []),
        compiler_params=_cparams(("arbitrary", "arbitrary")),
    )(a, b, *tiled, *consts, *pieces)
    return res[:no + na], res[no + na:]


def _in_proj_dx_norm_bwd(dproj, w_in_pt, x, dres, nw, scale, shift, pieces):
    T = x.shape[0]

    def epi(dh, x, dres, nw, scale, shift):
        _, vjp = jax.vjp(_norm_mod, x, nw, scale, shift)
        dx, dnw, dsc, dsh = vjp(dh)
        return dx + dres, dnw, dsc, dsh

    return _matmul_rows(dproj, w_in_pt, epi, [x, dres], [nw, scale, shift], [_sds((T, D))], [_sds((1, D))] * 3,
                        "in_proj_dx_norm1_bwd", pieces)


def _ffn_up_dx_resid_bwd(dab, w_gut, x, mixed, dy, gate1, nw, scale, shift):
    T = x.shape[0]

    def epi(dh2, x, mixed, dy, gate1, nw, scale, shift):
        _, vjp = jax.vjp(_resid_norm, x, mixed, gate1, nw, scale, shift)
        return vjp((dy, dh2))

    outs, _ = _matmul_rows(dab, w_gut, epi, [x, mixed, dy], [gate1, nw, scale, shift],
                           [_sds((T, D)), _sds((T, D), BF16)], [_sds((1, D))] * 4, "ffn_up_dx_resid_norm2_bwd")
    return outs


def _ffn_down_loss(act, w_down, x1, target, gate2):
    T = x1.shape[0]

    def epi(ffn, x1, target, gate2):
        y = x1 + gate2 * ffn
        err = y - target
        loss = 0.5 * jnp.sum(jnp.sum(err * err, axis=1, keepdims=True), axis=0, keepdims=True) / D
        dy = err * (1.0 / D)
        return dy, gate2 * dy, jnp.sum(dy * ffn, axis=0, keepdims=True), jnp.broadcast_to(loss, (1, LANE))

    outs, _ = _matmul_rows(act, w_down, epi, [x1, target], [gate2], [_sds((T, D)), _sds((T, D), BF16)],
                           [_sds((1, D)), _sds((1, LANE))], "ffn_down_loss")
    return outs


def _resid_norm(x, mixed, gate1, nw, scale, shift):
    x1 = x + gate1 * mixed
    return x1, _norm_mod(x1, nw, scale, shift)


def _resid_norm_fwd(x, mixed, gate1, nw, scale, shift):
    T = x.shape[0]
    return _rowcall(_resid_norm, [x, mixed], [gate1, nw, scale, shift],
                    [_sds((T, D)), _sds((T, D), BF16)], [], 512, "resid_norm2_fwd")


def _resid_norm_bwd(x, mixed, dy, dh2, gate1, nw, scale, shift):
    T = x.shape[0]

    def fn(x, mixed, dy, dh2, gate1, nw, scale, shift):
        _, vjp = jax.vjp(_resid_norm, x, mixed, gate1, nw, scale, shift)
        dx, dmixed, dg1, dnw, dsc, dsh = vjp((dy, dh2))
        return dx, dmixed, dg1, dnw, dsc, dsh

    return _rowcall(fn, [x, mixed, dy, dh2], [gate1, nw, scale, shift],
                    [_sds((T, D)), _sds((T, D), BF16)], [_sds((1, D))] * 4, 256, "resid_norm2_bwd")


FFN_BLK = 256
FFN_TM = 1024


def _interleave_gate_up(gate_t, up_t):
    blocks = lambda t: t.reshape(DFF // FFN_BLK, 1, FFN_BLK, D)
    return jnp.concatenate([blocks(gate_t), blocks(up_t)], axis=1).reshape(2 * DFF, D)


def _split_gate_up(g):
    g = g.reshape(DFF // FFN_BLK, 2, FFN_BLK, D)
    return g[:, 0].reshape(DFF, D), g[:, 1].reshape(DFF, D)


def _ffn_up_act(h2, w_gut):
    T = h2.shape[0]
    tm = _tile(T, FFN_TM)

    def body(h_ref, w_ref, ab_ref, act_ref):
        ab = _dot(h_ref[...], w_ref[...], _NT)
        ab_ref[...] = ab
        act_ref[...] = (_silu(ab[:, :FFN_BLK]) * ab[:, FFN_BLK:]).astype(act_ref.dtype)

    return pl.pallas_call(
        body, name="ffn_up_act", grid=(T // tm, DFF // FFN_BLK),
        in_specs=[pl.BlockSpec((tm, D), lambda i, j: (i, 0)), pl.BlockSpec((2 * FFN_BLK, D), lambda i, j: (j, 0))],
        out_specs=[pl.BlockSpec((tm, 2 * FFN_BLK), lambda i, j: (i, j)), pl.BlockSpec((tm, FFN_BLK), lambda i, j: (i, j))],
        out_shape=[_sds((T, 2 * DFF)), _sds((T, DFF), BF16)],
        compiler_params=_cparams(("parallel", "parallel")),
    )(h2, w_gut)


def _ffn_down_dx_act(dffn, w_down, ab):
    T = dffn.shape[0]
    tm = _tile(T, FFN_TM)

    def body(d_ref, w_ref, ab_ref, o_ref):
        dact = _dot(d_ref[...], w_ref[...], _NT)
        a, b = ab_ref[:, :FFN_BLK], ab_ref[:, FFN_BLK:]
        s = _sigmoid(a)
        da = dact * b * (s * (1.0 + a * (1.0 - s)))
        db = dact * (a * s)
        o_ref[...] = jnp.concatenate([da, db], axis=1).astype(o_ref.dtype)

    return pl.pallas_call(
        body, name="ffn_down_dx_act", grid=(T // tm, DFF // FFN_BLK),
        in_specs=[pl.BlockSpec((tm, D), lambda i, j: (i, 0)), pl.BlockSpec((FFN_BLK, D), lambda i, j: (j, 0)),
                  pl.BlockSpec((tm, 2 * FFN_BLK), lambda i, j: (i, j))],
        out_specs=pl.BlockSpec((tm, 2 * FFN_BLK), lambda i, j: (i, j)),
        out_shape=_sds((T, 2 * DFF), BF16),
        compiler_params=_cparams(("parallel", "parallel")),
    )(dffn, w_down, ab)


def _loss_head(x1, ffn, target, gate2):
    T = x1.shape[0]

    def fn(x1, ffn, target, gate2):
        y = x1 + gate2 * ffn
        err = y - target
        loss = 0.5 * jnp.sum(jnp.sum(err * err, axis=1, keepdims=True), axis=0, keepdims=True) / D
        dy = err * (1.0 / D)
        dgate2 = jnp.sum(dy * ffn, axis=0, keepdims=True)
        return dy, gate2 * dy, dgate2, jnp.broadcast_to(loss, (1, LANE))

    return _rowcall(fn, [x1, ffn, target], [gate2], [_sds((T, D)), _sds((T, D), BF16)],
                    [_sds((1, D)), _sds((1, LANE))], 256, "loss_head")


def _round_bf16(x):
    return x.astype(BF16).astype(F32)


def _shift_down(x, s, rows):
    if s == 0:
        return x
    return jnp.where(rows >= s, pltpu.roll(x, s, 0), 0.0)


def _shift_up(x, s, rows, T):
    if s == 0:
        return x
    return jnp.where(rows < T - s, pltpu.roll(x, T - s, 0), 0.0)


def _conv_fwd(proj, conv_w):
    T = proj.shape[0]
    ncol = 3 * GW // LANE

    def body(x_ref, w_ref, o_ref):
        x = _round_bf16(x_ref[...])
        rows = lax.broadcasted_iota(jnp.int32, x.shape, 0)
        acc = jnp.zeros_like(x)
        for j in range(CONVW):
            acc = acc + _round_bf16(w_ref[pl.ds(j, 1), :]) * _shift_down(x, CONVW - 1 - j, rows)
        o_ref[0], o_ref[1] = _split_pair(_silu(acc))

    return pl.pallas_call(
        body, name="conv_fwd", grid=(ncol,),
        in_specs=[pl.BlockSpec((T, LANE), lambda j: (0, j)), pl.BlockSpec((CONVW, LANE), lambda j: (0, j))],
        out_specs=pl.BlockSpec((2, T, HD), lambda j: (j, 0, 0)),
        out_shape=_sds((3 * GH, T, HD)),
        compiler_params=_cparams(("parallel",)),
    )(proj, conv_w)


RELAYOUT_TM = 4096


def _split_pair(y):
    return y[:, :HD], pltpu.roll(y, HD, 1)[:, :HD]


def _merge_pair(a, b):
    return jnp.concatenate([a, b], axis=1)


def _split_heads(x, col_block0, nheads, name):
    T = x.shape[0]
    tm = _tile(T, RELAYOUT_TM)

    def body(x_ref, o_ref):
        a, b = _split_pair(x_ref[...])
        o_ref[0] = a
        o_ref[1] = b

    return pl.pallas_call(
        body, name=name, grid=(nheads // 2, T // tm),
        in_specs=[pl.BlockSpec((tm, LANE), lambda j, i: (i, col_block0 + j))],
        out_specs=pl.BlockSpec((2, tm, HD), lambda j, i: (j, i, 0)),
        out_shape=_sds((nheads, T, HD), x.dtype),
        compiler_params=_cparams(("parallel", "parallel")),
    )(x)


def _merge_heads(hm, out_dtype, name, into=None, col_block0=0, head0=0, nheads=None):
    T = hm.shape[1]
    nheads = hm.shape[0] if nheads is None else nheads
    tm = _tile(T, RELAYOUT_TM)

    def body(*refs):
        h_ref, o_ref = refs[0], refs[-1]
        o_ref[...] = _merge_pair(h_ref[0], h_ref[1]).astype(o_ref.dtype)

    in_specs = [pl.BlockSpec((2, tm, HD), lambda j, i: (head0 // 2 + j, i, 0))]
    args = [hm]
    if into is None:
        out_shape = _sds((T, HD * nheads), out_dtype)
        aliases = {}
    else:
        out_shape = _sds(into.shape, into.dtype)
        in_specs.append(pl.BlockSpec(memory_space=pl.ANY))
        args.append(into)
        aliases = {1: 0}
    return pl.pallas_call(
        body, name=name, grid=(nheads // 2, T // tm),
        in_specs=in_specs,
        out_specs=pl.BlockSpec((tm, LANE), lambda j, i: (i, col_block0 + j)),
        out_shape=out_shape, input_output_aliases=aliases,
        compiler_params=_cparams(("parallel", "parallel")),
    )(*args)


def _conv_bwd(proj, conv_w, dqc):
    T = proj.shape[0]
    ncol = 3 * GW // LANE

    def body(x_ref, w_ref, d_ref, dx_ref, dw_ref):
        x = _round_bf16(x_ref[...])
        rows = lax.broadcasted_iota(jnp.int32, x.shape, 0)
        xs = [_shift_down(x, CONVW - 1 - j, rows) for j in range(CONVW)]
        w = [_round_bf16(w_ref[pl.ds(j, 1), :]) for j in range(CONVW)]
        pre = jnp.zeros_like(x)
        for j in range(CONVW):
            pre = pre + w[j] * xs[j]
        s = _sigmoid(pre)
        dpre = _round_bf16(_merge_pair(d_ref[0], d_ref[1]) * (s * (1.0 + pre * (1.0 - s))))
        dx = jnp.zeros_like(x)
        for j in range(CONVW):
            dx = dx + w[j] * _shift_up(dpre, CONVW - 1 - j, rows, T)
            dw_ref[pl.ds(j, 1), :] = jnp.sum(dpre * xs[j], axis=0, keepdims=True)
        dx_ref[...] = dx.astype(dx_ref.dtype)

    return pl.pallas_call(
        body, name="conv_bwd", grid=(ncol,),
        in_specs=[pl.BlockSpec((T, LANE), lambda j: (0, j)), pl.BlockSpec((CONVW, LANE), lambda j: (0, j)),
                  pl.BlockSpec((2, T, HD), lambda j: (j, 0, 0))],
        out_specs=[pl.BlockSpec((T, LANE), lambda j: (0, j)), pl.BlockSpec((CONVW, LANE), lambda j: (0, j))],
        out_shape=[_sds((T, NP), BF16), _sds((CONVW, 3 * GW))],
        compiler_params=_cparams(("parallel",)),
    )(proj, conv_w, dqc)


def _gdn_prep(kit, q, k, v, ga, gb, alog, dtb, t_inv=None):
    C = CHUNK
    ri = lax.broadcasted_iota(jnp.int32, (C, C), 0)
    ci = lax.broadcasted_iota(jnp.int32, (C, C), 1)
    causal = ri >= ci
    strict = ri > ci
    eye = (ri == ci).astype(F32)
    lower = causal.astype(F32)
    upper = (ri <= ci).astype(F32)

    a = ga + dtb
    softplus = jnp.maximum(a, 0.0) + jnp.log(1.0 + jnp.exp(-jnp.abs(a)))
    g_row = -jnp.exp(alog) * softplus
    beta_row = _sigmoid(gb)
    g_col = jnp.sum(eye * g_row, axis=2, keepdims=True)
    beta_col = jnp.sum(eye * beta_row, axis=2, keepdims=True)
    G_col = jnp.sum(lower * g_row, axis=2, keepdims=True)
    G_row = jnp.sum(upper * g_col, axis=1, keepdims=True)
    G_last = jnp.sum(g_row, axis=2, keepdims=True)
    decay = jnp.exp(jnp.where(causal, G_col - G_row, -1e30))

    qn = q * lax.rsqrt(jnp.sum(q * q, axis=-1, keepdims=True) + EPS) * (HD ** -0.5)
    kn = k * lax.rsqrt(jnp.sum(k * k, axis=-1, keepdims=True) + EPS)
    kb = kn * beta_col
    A = jnp.where(strict, kit.nt(kb, kn) * decay, 0.0)
    Tm = kit.inv(A, t_inv)
    eG = jnp.exp(G_col)
    u = kit.nn3(Tm, v * beta_col)
    w = kit.nn3(Tm, kb * eG)
    qk = jnp.where(causal, kit.nt(qn, kn) * decay, 0.0)
    q_dec = qn * eG
    k_dec = kn * jnp.exp(G_last - G_col)
    dec = jnp.exp(G_last)
    return u, w, qk, q_dec, k_dec, dec, Tm


def _gdn_out(o, z, nw):
    return _rms(o, nw) * _silu(z)


GDN_CB = 4


def _gdn_specs(T, blk):
    TB = GDN_CB * CHUNK
    seq = lambda grp: pl.BlockSpec((GH, TB, HD), lambda i, grp=grp: (grp, blk(i), 0))
    row = lambda grp: pl.BlockSpec((GH, GDN_CB, 1, CHUNK), lambda i, grp=grp: (grp, blk(i), 0, 0))
    per_head = pl.BlockSpec((GH, 1, CHUNK), lambda i: (0, 0, 0))
    whole = pl.BlockSpec((1, HD), lambda i: (0, 0))
    state = pl.BlockSpec((GH, GDN_CB, HD, HD), lambda i: (0, blk(i), 0, 0))
    return seq, row, per_head, whole, state


def _gdn_load(seq_refs, row_refs, head_refs):
    chunks = lambda r: jnp.concatenate([r[:, pl.ds(cb * CHUNK, CHUNK), :] for cb in range(GDN_CB)], axis=0)
    rows = lambda r: jnp.concatenate([r[:, cb] for cb in range(GDN_CB)], axis=0)
    heads = lambda r: jnp.concatenate([r[...]] * GDN_CB, axis=0)
    return [chunks(r) for r in seq_refs], [rows(r) for r in row_refs], [heads(r) for r in head_refs]


def _gdn_fwd(qkv_hm, zs_hm, gab, alog_b, dtb_b, nw, shards):
    T = qkv_hm.shape[1]
    N = T // CHUNK
    nblk = N // GDN_CB
    ns = len(shards)
    seq, row, per_head, whole, state = _gdn_specs(T, lambda i: i)
    kit = _Kit(False)

    def body(*refs):
        q_ref, k_ref, v_ref, z_ref, ga_ref, gb_ref, al_ref, dt_ref, nw_ref = refs[:9]
        o_ref, S_ref, T_ref = refs[9 + ns:12 + ns]
        S_scr = refs[12 + 2 * ns]
        plan = _gather_plan(refs[9:9 + ns], refs[12 + ns:12 + 2 * ns], *refs[13 + 2 * ns:])

        @pl.when(pl.program_id(0) == 0)
        def _():
            S_scr[...] = jnp.zeros_like(S_scr)
            _start(plan)

        (q, k, v, z), (ga, gb), (al, dt) = _gdn_load((q_ref, k_ref, v_ref, z_ref), (ga_ref, gb_ref), (al_ref, dt_ref))
        u, w, qk, q_dec, k_dec, dec, t_inv = _gdn_prep(kit, q, k, v, ga, gb, al, dt)
        S = S_scr[...]
        for cb in range(GDN_CB):
            hs = slice(cb * GH, (cb + 1) * GH)
            S_ref[:, cb] = S
            T_ref[:, cb] = t_inv[hs]
            v_new = u[hs] - kit.nn(w[hs], S)
            o = kit.nn(q_dec[hs], S) + kit.nn(qk[hs], v_new)
            S = S * dec[hs] + kit.tn(k_dec[hs], v_new)
            o_ref[:, pl.ds(cb * CHUNK, CHUNK), :] = _gdn_out(o, z[hs], nw_ref[...])
        S_scr[...] = S

        @pl.when(pl.program_id(0) == nblk - 1)
        def _():
            _finish(plan)

    res = pl.pallas_call(
        body, name="gdn_fwd", grid=(nblk,),
        in_specs=[seq(0), seq(1), seq(2), seq(0), row(0), row(1), per_head, per_head, whole] + _hbm_specs(ns),
        out_specs=[seq(0), state, state] + _hbm_specs(ns),
        out_shape=[_sds((GH + SQH, T, HD)), _sds((GH, N, HD, HD)), _sds((GH, N, CHUNK, CHUNK))]
                  + _gather_shapes(shards),
        scratch_shapes=[pltpu.VMEM((GH, HD, HD), F32)] + _gather_sems(ns),
        compiler_params=_cparams(("arbitrary",)),
    )(qkv_hm, qkv_hm, qkv_hm, zs_hm, gab, gab, alog_b, dtb_b, nw, *shards)
    return res[0], (res[1], res[2]), res[3:]


def _gdn_bwd(qkv_hm, zs_hm, gab, alog_b, dtb_b, nw, S_all, do, pieces):
    T = qkv_hm.shape[1]
    N = T // CHUNK
    nblk = N // GDN_CB
    npc = len(pieces)
    dkit, kit = _Kit(True), _Kit(False)
    rseq, rrow, per_head, whole, rstate = _gdn_specs(T, lambda i: nblk - 1 - i)

    def body(*refs):
        q_ref, k_ref, v_ref, z_ref, ga_ref, gb_ref, al_ref, dt_ref, nw_ref, S_ref, T_ref, do_ref = refs[:12]
        dqkv_ref, dz_ref, dga_ref, dgb_ref, dal_ref, ddt_ref, dnw_ref = refs[12 + npc:19 + npc]
        dS_scr = refs[19 + 2 * npc]
        plan = _exchange_plan(refs[12:12 + npc], refs[19 + npc:19 + 2 * npc], *refs[20 + 2 * npc:])

        @pl.when(pl.program_id(0) == 0)
        def _():
            dS_scr[...] = jnp.zeros_like(dS_scr)
            dal_ref[...] = jnp.zeros_like(dal_ref)
            ddt_ref[...] = jnp.zeros_like(ddt_ref)
            dnw_ref[...] = jnp.zeros_like(dnw_ref)
            _start(plan)

        (q, k, v, z, dout), (ga, gb), (al, dt) = _gdn_load((q_ref, k_ref, v_ref, z_ref, do_ref), (ga_ref, gb_ref),
                                                          (al_ref, dt_ref))
        S_in = jnp.concatenate([S_ref[:, cb] for cb in range(GDN_CB)], axis=0)
        t_inv = jnp.concatenate([T_ref[:, cb] for cb in range(GDN_CB)], axis=0)
        prep = lambda *a: _gdn_prep(dkit, *a, t_inv=t_inv)[:6]
        (u, w, qk, q_dec, k_dec, dec), prep_vjp = jax.vjp(prep, q, k, v, ga, gb, al, dt)
        v_new = u - kit.nn(w, S_in)
        o = kit.nn(q_dec, S_in) + kit.nn(qk, v_new)
        _, out_vjp = jax.vjp(_gdn_out, o, z, nw_ref[...])
        do, dz, dnw = out_vjp(dout)
        dvn_part = kit.tn(qk, do)
        dS_part = kit.tn(q_dec, do)
        dS = dS_scr[...]
        dS_out, dvn = [None] * GDN_CB, [None] * GDN_CB
        for cb in reversed(range(GDN_CB)):
            hs = slice(cb * GH, (cb + 1) * GH)
            dS_out[cb] = dS
            dvn[cb] = dvn_part[hs] + kit.nn(k_dec[hs], dS)
            dS = dS * dec[hs] + dS_part[hs] - kit.tn(w[hs], dvn[cb])
        dS_scr[...] = dS
        dS_out = jnp.concatenate(dS_out, axis=0)
        dvn = jnp.concatenate(dvn, axis=0)
        ddec = jnp.sum(jnp.sum(S_in * dS_out, axis=2, keepdims=True), axis=1, keepdims=True)
        cts = (dvn, -kit.nt(dvn, S_in), kit.nt(do, v_new), kit.nt(do, S_in), kit.nt(v_new, dS_out), ddec)
        dq, dk, dv, dga, dgb, dal, ddt = prep_vjp(cts)
        lanesum = lambda t: jnp.broadcast_to(jnp.sum(t, axis=2, keepdims=True), t.shape)
        for cb in range(GDN_CB):
            hs = slice(cb * GH, (cb + 1) * GH)
            sl = pl.ds(cb * CHUNK, CHUNK)
            dqkv_ref[pl.ds(0, GH), sl, :] = dq[hs]
            dqkv_ref[pl.ds(GH, GH), sl, :] = dk[hs]
            dqkv_ref[pl.ds(2 * GH, GH), sl, :] = dv[hs]
            dz_ref[:, sl, :] = dz[hs]
            dga_ref[:, cb] = dga[hs]
            dgb_ref[:, cb] = dgb[hs]
            dal_ref[...] += lanesum(dal[hs])
            ddt_ref[...] += lanesum(ddt[hs])
        dnw_ref[...] += dnw

        @pl.when(pl.program_id(0) == nblk - 1)
        def _():
            _finish(plan)

    res = pl.pallas_call(
        body, name="gdn_bwd", grid=(nblk,),
        in_specs=[rseq(0), rseq(1), rseq(2), rseq(0), rrow(0), rrow(1), per_head, per_head, whole, rstate, rstate,
                  rseq(0)] + _hbm_specs(npc),
        out_specs=[pl.BlockSpec((3 * GH, GDN_CB * CHUNK, HD), lambda i: (0, nblk - 1 - i, 0)), rseq(0), rrow(0),
                   rrow(0), per_head, per_head, whole] + _hbm_specs(npc),
        out_shape=[_sds((3 * GH, T, HD)), _sds((GH + 4 + SWA_GRAD_HEADS, T, HD))] + [_sds((GH, N, 1, CHUNK))] * 2
                  + [_sds((GH, 1, CHUNK))] * 2 + [_sds((1, HD))] + _exchange_shapes(pieces),
        scratch_shapes=[pltpu.VMEM((GH, HD, HD), F32)] + _exchange_sems(npc),
        compiler_params=_cparams(("arbitrary",), GDN_BWD_VMEM),
    )(qkv_hm, qkv_hm, qkv_hm, zs_hm, gab, gab, alog_b, dtb_b, nw, S_all[0], S_all[1], do, *pieces)
    return res[:7], res[7:]


def _swa_block(kit, first, q0, q1, q2, q3, kp, kc, vp, vc, qnw, knw, s0, s1, s2, s3, *, slopes):
    W = WIN
    ri = lax.broadcasted_iota(jnp.int32, (W, W), 0)
    ci = lax.broadcasted_iota(jnp.int32, (W, W), 1)
    mask_c = ri >= ci
    mask_p = ci > ri + first * W
    dist_c = (ri - ci).astype(F32)
    dist_p = (ri - ci + W).astype(F32)
    kpn = _rms(kp, knw)
    kcn = _rms(kc, knw)
    outs = []
    for q, sink, slope in zip((q0, q1, q2, q3), (s0, s1, s2, s3), slopes):
        qn = _rms(q, qnw)
        sc = jnp.where(mask_c, kit.nt(qn, kcn) * (HD ** -0.5) - slope * dist_c, -1e30)
        sp = jnp.where(mask_p, kit.nt(qn, kpn) * (HD ** -0.5) - slope * dist_p, -1e30)
        m = jnp.maximum(jnp.maximum(jnp.max(sc, axis=-1, keepdims=True), jnp.max(sp, axis=-1, keepdims=True)), sink)
        m = lax.stop_gradient(m)
        pc = jnp.exp(sc - m)
        pp = jnp.exp(sp - m)
        den = jnp.sum(pc, axis=-1, keepdims=True) + jnp.sum(pp, axis=-1, keepdims=True) + jnp.exp(sink - m)
        inv = 1.0 / den
        outs.append(kit.nn(pc * inv, vc) + kit.nn(pp * inv, vp))
    return tuple(outs)


def _swa_slopes(hk):
    return tuple(jnp.where(hk == 0, 2.0 ** (-8.0 * (g + 1.0) / SQH), 2.0 ** (-8.0 * (SGRP + g + 1.0) / SQH))
                 for g in range(SGRP))


def _swa_fwd(zs_hm, qnw, knw, sinks_col):
    T = zs_hm.shape[1]
    NB = T // WIN
    kit = _Kit(False)

    def body(q_ref, kp_ref, kc_ref, vp_ref, vc_ref, qnw_ref, knw_ref, s_ref, o_ref):
        hk = pl.program_id(0)
        first = (pl.program_id(1) == 0).astype(jnp.int32)
        args = ([q_ref[g] for g in range(SGRP)] + [kp_ref[...], kc_ref[...], vp_ref[...], vc_ref[...],
                                                     qnw_ref[...], knw_ref[...]] + [s_ref[g] for g in range(SGRP)])
        outs = _swa_block(kit, first, *args, slopes=_swa_slopes(hk))
        for g in range(SGRP):
            o_ref[g] = outs[g]

    qspec = pl.BlockSpec((SGRP, WIN, HD), lambda hk, n: (2 + hk, n, 0))
    cur = lambda off: pl.BlockSpec((None, WIN, HD), lambda hk, n, off=off: (off + hk, n, 0))
    prev = lambda off: pl.BlockSpec((None, WIN, HD), lambda hk, n, off=off: (off + hk, jnp.maximum(n - 1, 0), 0))
    whole = pl.BlockSpec((1, HD), lambda hk, n: (0, 0))
    sspec = pl.BlockSpec((SGRP, WIN, 1), lambda hk, n: (hk, 0, 0))
    return pl.pallas_call(
        body, name="swa_fwd", grid=(SKVH, NB),
        in_specs=[qspec, prev(16), cur(16), prev(18), cur(18), whole, whole, sspec],
        out_specs=pl.BlockSpec((SGRP, WIN, HD), lambda hk, n: (hk, n, 0)),
        out_shape=_sds((SQH, T, HD)),
        compiler_params=_cparams(("parallel", "arbitrary")),
    )(zs_hm, zs_hm, zs_hm, zs_hm, zs_hm, qnw, knw, sinks_col)


def _swa_bwd(zs_hm, qnw, knw, sinks_col, do):
    T = zs_hm.shape[1]
    NB = T // WIN
    kit = _Kit(True)

    def body(q_ref, kp_ref, kc_ref, vp_ref, vc_ref, qnw_ref, knw_ref, s_ref, do_ref,
             dq_ref, dk_ref, dv_ref, dqnw_ref, dknw_ref, ds_ref, ck_scr, cv_scr):
        hk = pl.program_id(0)
        i = pl.program_id(1)
        first = (i == NB - 1).astype(jnp.int32)

        @pl.when(i == 0)
        def _():
            ck_scr[...] = jnp.zeros_like(ck_scr)
            cv_scr[...] = jnp.zeros_like(cv_scr)
            ds_ref[...] = jnp.zeros_like(ds_ref)

        @pl.when((i == 0) & (hk == 0))
        def _():
            dqnw_ref[...] = jnp.zeros_like(dqnw_ref)
            dknw_ref[...] = jnp.zeros_like(dknw_ref)

        args = ([q_ref[g] for g in range(SGRP)] + [kp_ref[...], kc_ref[...], vp_ref[...], vc_ref[...],
                                                     qnw_ref[...], knw_ref[...]] + [s_ref[g] for g in range(SGRP)])
        dos = tuple(do_ref[g] for g in range(SGRP))
        _, vjp = jax.vjp(functools.partial(_swa_block, kit, first, slopes=_swa_slopes(hk)), *args)
        gr = vjp(dos)
        for g in range(SGRP):
            dq_ref[g] = gr[g]
            ds_ref[g] += jnp.broadcast_to(jnp.sum(gr[10 + g], axis=0, keepdims=True), (WIN, 1))
        dkp, dkc, dvp, dvc = gr[4:8]
        dk_ref[...] = dkc + ck_scr[...]
        dv_ref[...] = dvc + cv_scr[...]
        ck_scr[...] = dkp
        cv_scr[...] = dvp
        dqnw_ref[...] += gr[8]
        dknw_ref[...] += gr[9]

    rn = lambda n: NB - 1 - n
    qspec = pl.BlockSpec((SGRP, WIN, HD), lambda hk, i: (2 + hk, rn(i), 0))
    cur = lambda off: pl.BlockSpec((None, WIN, HD), lambda hk, i, off=off: (off + hk, rn(i), 0))
    prev = lambda off: pl.BlockSpec((None, WIN, HD), lambda hk, i, off=off: (off + hk, jnp.maximum(rn(i) - 1, 0), 0))
    whole = pl.BlockSpec((1, HD), lambda hk, i: (0, 0))
    sspec = pl.BlockSpec((SGRP, WIN, 1), lambda hk, i: (hk, 0, 0))
    ospec = pl.BlockSpec((SGRP, WIN, HD), lambda hk, i: (hk, rn(i), 0))
    return pl.pallas_call(
        body, name="swa_bwd", grid=(SKVH, NB),
        in_specs=[qspec, prev(16), cur(16), prev(18), cur(18), whole, whole, sspec, ospec],
        out_specs=[ospec, cur(0), cur(0), whole, whole, sspec],
        out_shape=[_sds((SQH, T, HD)), _sds((SKVH, T, HD)), _sds((SKVH, T, HD)),
                   _sds((1, HD)), _sds((1, HD)), _sds((SQH, WIN, 1))],
        scratch_shapes=[pltpu.VMEM((WIN, HD), F32), pltpu.VMEM((WIN, HD), F32)],
        compiler_params=_cparams(("arbitrary", "arbitrary")),
    )(zs_hm, zs_hm, zs_hm, zs_hm, zs_hm, qnw, knw, sinks_col, do)


def _swa_heads(kit, first, q, kp, kc, vp, vc, qnw, knw, sink, slope):
    W = WIN
    ri = lax.broadcasted_iota(jnp.int32, (W, W), 0)
    ci = lax.broadcasted_iota(jnp.int32, (W, W), 1)
    mask_c = ri >= ci
    mask_p = ci > ri + first * W
    dist_c = (ri - ci).astype(F32)
    dist_p = (ri - ci + W).astype(F32)
    kpn = _rms(kp, knw)
    kcn = _rms(kc, knw)
    qn = _rms(q, qnw)
    sc = jnp.where(mask_c, kit.nt(qn, kcn) * (HD ** -0.5) - slope * dist_c, -1e30)
    sp = jnp.where(mask_p, kit.nt(qn, kpn) * (HD ** -0.5) - slope * dist_p, -1e30)
    m = jnp.maximum(jnp.maximum(jnp.max(sc, axis=-1, keepdims=True), jnp.max(sp, axis=-1, keepdims=True)), sink)
    m = lax.stop_gradient(m)
    pc = jnp.exp(sc - m)
    pp = jnp.exp(sp - m)
    den = jnp.sum(pc, axis=-1, keepdims=True) + jnp.sum(pp, axis=-1, keepdims=True) + jnp.exp(sink - m)
    inv = 1.0 / den
    return kit.nn(pc * inv, vc) + kit.nn(pp * inv, vp)


def _per_query_head(kv_ref):
    return jnp.concatenate([kv_ref[pl.ds(h // SGRP, 1)] for h in range(SQH)], axis=0)


def _per_kv_head(d):
    return jnp.concatenate([jnp.sum(d[g * SGRP:(g + 1) * SGRP], axis=0, keepdims=True) for g in range(SKVH)], axis=0)


def _swa_specs(blk):
    qspec = pl.BlockSpec((SQH, WIN, HD), lambda i: (1, blk(i), 0))
    cur = lambda grp: pl.BlockSpec((SKVH, WIN, HD), lambda i, grp=grp: (grp, blk(i), 0))
    prev = lambda grp: pl.BlockSpec((SKVH, WIN, HD), lambda i, grp=grp: (grp, jnp.maximum(blk(i) - 1, 0), 0))
    whole = pl.BlockSpec((1, HD), lambda i: (0, 0))
    col = pl.BlockSpec((SQH, WIN, 1), lambda i: (0, 0, 0))
    ospec = pl.BlockSpec((SQH, WIN, HD), lambda i: (0, blk(i), 0))
    return qspec, cur, prev, whole, col, ospec


def _swa_fwd(zs_hm, qnw, knw, sinks_col, slopes_col, o_buf, shards):
    T = zs_hm.shape[1]
    NB = T // WIN
    ns = len(shards)
    kit = _Kit(False)
    qspec, cur, prev, whole, col, _ = _swa_specs(lambda i: i)

    def body(*refs):
        q_ref, kp_ref, kc_ref, vp_ref, vc_ref, qnw_ref, knw_ref, s_ref, sl_ref = refs[:9]
        o_ref = refs[10 + ns]
        plan = _gather_plan(refs[10:10 + ns], refs[11 + ns:11 + 2 * ns], *refs[11 + 2 * ns:])

        @pl.when(pl.program_id(0) == 0)
        def _():
            _start(plan)

        first = (pl.program_id(0) == 0).astype(jnp.int32)
        o_ref[...] = _swa_heads(kit, first, q_ref[...], _per_query_head(kp_ref), _per_query_head(kc_ref),
                                _per_query_head(vp_ref), _per_query_head(vc_ref), qnw_ref[...], knw_ref[...],
                                s_ref[...], sl_ref[...])

        @pl.when(pl.program_id(0) == NB - 1)
        def _():
            _finish(plan)

    res = pl.pallas_call(
        body, name="swa_fwd", grid=(NB,),
        in_specs=[qspec, prev(8), cur(8), prev(9), cur(9), whole, whole, col, col] + _hbm_specs(1 + ns),
        out_specs=[pl.BlockSpec((SQH, WIN, HD), lambda i: (1, i, 0))] + _hbm_specs(ns),
        out_shape=[_sds(o_buf.shape)] + _gather_shapes(shards),
        input_output_aliases={9: 0},
        scratch_shapes=_gather_sems(ns),
        compiler_params=_cparams(("arbitrary",)),
    )(zs_hm, zs_hm, zs_hm, zs_hm, zs_hm, qnw, knw, sinks_col, slopes_col, o_buf, *shards)
    return res[0], res[1:]


SWA_GRAD_HEADS = SQH + 2 * SKVH


def _swa_bwd(zs_hm, qnw, knw, sinks_col, slopes_col, dmix_hm, d_buf):
    T = zs_hm.shape[1]
    NB = T // WIN
    kit = _Kit(True)
    qspec, cur, prev, whole, col, _ = _swa_specs(lambda i: NB - 1 - i)

    def body(q_ref, kp_ref, kc_ref, vp_ref, vc_ref, qnw_ref, knw_ref, s_ref, sl_ref, do_ref, buf_ref,
             d_ref, dqnw_ref, dknw_ref, ds_ref, ck_scr, cv_scr):
        dq_ref = d_ref.at[pl.ds(0, SQH)]
        dk_ref = d_ref.at[pl.ds(SQH, SKVH)]
        dv_ref = d_ref.at[pl.ds(SQH + SKVH, SKVH)]
        i = pl.program_id(0)
        first = (i == NB - 1).astype(jnp.int32)

        @pl.when(i == 0)
        def _():
            ck_scr[...] = jnp.zeros_like(ck_scr)
            cv_scr[...] = jnp.zeros_like(cv_scr)
            ds_ref[...] = jnp.zeros_like(ds_ref)
            dqnw_ref[...] = jnp.zeros_like(dqnw_ref)
            dknw_ref[...] = jnp.zeros_like(dknw_ref)

        fn = lambda q, kp, kc, vp, vc, qnw, knw, sink: _swa_heads(kit, first, q, kp, kc, vp, vc, qnw, knw, sink,
                                                                  sl_ref[...])
        _, vjp = jax.vjp(fn, q_ref[...], _per_query_head(kp_ref), _per_query_head(kc_ref), _per_query_head(vp_ref),
                         _per_query_head(vc_ref), qnw_ref[...], knw_ref[...], s_ref[...])
        dq, dkp, dkc, dvp, dvc, dqnw, dknw, dsink = vjp(do_ref[...])
        dq_ref[...] = dq
        dk_ref[...] = _per_kv_head(dkc) + ck_scr[...]
        dv_ref[...] = _per_kv_head(dvc) + cv_scr[...]
        ck_scr[...] = _per_kv_head(dkp)
        cv_scr[...] = _per_kv_head(dvp)
        dqnw_ref[...] += dqnw
        dknw_ref[...] += dknw
        ds_ref[...] += jnp.broadcast_to(jnp.sum(dsink, axis=1, keepdims=True), dsink.shape)

    dospec = pl.BlockSpec((SQH, WIN, HD), lambda i: (1, NB - 1 - i, 0))
    dspec = pl.BlockSpec((SWA_GRAD_HEADS, WIN, HD), lambda i: (1, NB - 1 - i, 0))
    res = pl.pallas_call(
        body, name="swa_bwd", grid=(NB,),
        in_specs=[qspec, prev(8), cur(8), prev(9), cur(9), whole, whole, col, col, dospec] + _hbm_specs(1),
        out_specs=[dspec, whole, whole, col],
        out_shape=[_sds(d_buf.shape), _sds((1, HD)), _sds((1, HD)), _sds((SQH, WIN, 1))],
        input_output_aliases={10: 0},
        scratch_shapes=[pltpu.VMEM((SKVH, WIN, HD), F32), pltpu.VMEM((SKVH, WIN, HD), F32)],
        compiler_params=_cparams(("arbitrary",)),
    )(zs_hm, zs_hm, zs_hm, zs_hm, zs_hm, qnw, knw, sinks_col, slopes_col, dmix_hm, d_buf)
    return res


GAB0 = 3 * GW + 1280


W_IN_ROWS = PROJ // N_CHIP
W_IN_ROWS_PAD = 736


def _permute_w_in_t(w_in_t):
    return jnp.concatenate([w_in_t[:4 * GW], w_in_t[4 * GW + 2 * GH:], w_in_t[4 * GW:4 * GW + 2 * GH],
                            jnp.zeros((NP - PROJ, D), w_in_t.dtype)], axis=0)


def _w_in_grad_pieces(g_t):
    g = jnp.concatenate([g_t[:4 * GW], g_t[GAB0:GAB0 + 2 * GH], g_t[4 * GW:GAB0]], axis=0)
    g = jnp.pad(g.reshape(N_CHIP, W_IN_ROWS, D), ((0, 0), (0, W_IN_ROWS_PAD - W_IN_ROWS), (0, 0)))
    return g.reshape(N_CHIP, 2, W_IN_ROWS_PAD // 2, D)


def _pieces_by_rows(g):
    return g.reshape(N_CHIP, 2, g.shape[0] // (2 * N_CHIP), D)


def _local_step(x, target, mod, n1w, w_in_pt, conv_w, alog, dtb, gnw, qnw, knw, sinks, n2w, shards):
    sh_out, sh_gate, sh_up, sh_down = shards
    T = x.shape[0]
    N = T // CHUNK
    shift1, scale1, gate1, shift2, scale2, gate2 = [mod[:, i * D:(i + 1) * D] for i in range(6)]

    h = _norm_mod_fwd(x, n1w, scale1, shift1)
    proj, (a_out,) = _matmul(h, w_in_pt, tb=True, name="in_proj", gather=[sh_out])
    w_out = a_out.reshape(D, D)
    qkv_hm = _conv_fwd(proj, conv_w)
    zs_hm = _split_heads(proj, 3 * GW // LANE, 20, "split_zs")
    gab = proj[:, GAB0:GAB0 + 2 * GH].T.reshape(2 * GH, N, 1, CHUNK)
    alog_b = jnp.broadcast_to(alog.reshape(GH, 1, 1), (GH, 1, CHUNK))
    dtb_b = jnp.broadcast_to(dtb.reshape(GH, 1, 1), (GH, 1, CHUNK))
    sinks_col = jnp.broadcast_to(sinks.reshape(SQH, 1, 1), (SQH, WIN, 1))
    o_hm, S_all, (a_gate, a_up) = _gdn_fwd(qkv_hm, zs_hm, gab, alog_b, dtb_b, gnw, [sh_gate, sh_up])
    w_gut = _interleave_gate_up(a_gate.reshape(DFF, D), a_up.reshape(DFF, D))
    slopes = 2.0 ** (-8.0 * (jnp.arange(SQH, dtype=F32) + 1.0) / SQH)
    slopes_col = jnp.broadcast_to(slopes.reshape(SQH, 1, 1), (SQH, WIN, 1))
    o_hm, (a_down,) = _swa_fwd(zs_hm, qnw, knw, sinks_col, slopes_col, o_hm, [sh_down])
    w_down = a_down.reshape(DFF, D)
    mixcat = _merge_heads(o_hm, BF16, "merge_mix")
    mixed = _matmul(mixcat, w_out, name="out_proj")
    x1, h2 = _resid_norm_fwd(x, mixed, gate1, n2w, scale2, shift2)
    ab, act = _ffn_up_act(h2, w_gut)
    dy, dffn, dgate2, loss = _ffn_down_loss(act, w_down, x1, target, gate2)

    dab = _ffn_down_dx_act(dffn, w_down, ab)
    g_w_down = _matmul(act, dffn, ta=True, out_dtype=BF16, name="ffn_down_dw")
    g_w_gut = _matmul(dab, h2, ta=True, out_dtype=BF16, name="ffn_up_dw")
    dx1, dmixed, dgate1, dn2w, dscale2, dshift2 = _ffn_up_dx_resid_bwd(dab, w_gut, x, mixed, dy, gate1, n2w, scale2,
                                                                       shift2)
    g_w_out = _matmul(mixcat, dmixed, ta=True, out_dtype=BF16, name="out_proj_dw")
    dmix_hm = _split_heads(_matmul(dmixed, w_out, tb=True, name="out_proj_dx"), 0, GH + SQH, "split_dmix")
    g_gate_t, g_up_t = _split_gate_up(g_w_gut)
    pieces = [_pieces_by_rows(g_w_out), _pieces_by_rows(g_gate_t), _pieces_by_rows(g_up_t),
              _pieces_by_rows(g_w_down)]
    (dqkv_hm, d_hm, dga, dgb, dalog, ddtb, dgnw), recv = _gdn_bwd(qkv_hm, zs_hm, gab, alog_b, dtb_b, gnw, S_all,
                                                                  dmix_hm, pieces)
    d_hm, dqnw, dknw, dsinks = _swa_bwd(zs_hm, qnw, knw, sinks_col, slopes_col, dmix_hm, d_hm)
    dproj, dconv = _conv_bwd(proj, conv_w, dqkv_hm)
    dproj = _merge_heads(d_hm, BF16, "merge_dz", into=dproj, col_block0=3 * GW // LANE, head0=0, nheads=GH)
    dproj = _merge_heads(d_hm, BF16, "merge_dswa", into=dproj, col_block0=4 * GW // LANE, head0=GH + 4,
                         nheads=SWA_GRAD_HEADS)
    dgab = jnp.concatenate([dga, dgb], axis=0).reshape(2 * GH, T).T.astype(BF16)
    dproj = lax.dynamic_update_slice(dproj, jnp.concatenate([dgab, jnp.zeros((T, NP - PROJ), BF16)], axis=1),
                                     (0, GAB0))
    g_w_in_pt = _matmul(dproj, h, ta=True, out_dtype=BF16, name="in_proj_dw")
    (grad_x, dn1w, dscale1, dshift1), recv_in = _in_proj_dx_norm_bwd(dproj, w_in_pt, x, dx1, n1w, scale1, shift1,
                                                                     [_w_in_grad_pieces(g_w_in_pt)])

    dmod = jnp.concatenate([dshift1, dscale1, dgate1, dshift2, dscale2, dgate2], axis=1)
    big = list(recv_in) + list(recv)
    small = dict(mod=dmod, norm1_w=dn1w, norm2_w=dn2w, conv_w=dconv, a_log=dalog[:, 0, 0], dt_bias=ddtb[:, 0, 0],
                 gdn_norm_w=dgnw, q_norm_w=dqnw, k_norm_w=dknw, sinks=dsinks[:, 0, 0])
    return loss, grad_x, big, small


def _adamw(w, g, m, v):
    m2 = ADAM_B1 * m + (1.0 - ADAM_B1) * g
    v2 = ADAM_B2 * v + (1.0 - ADAM_B2) * (g * g)
    m_hat = m2 / (1.0 - ADAM_B1 ** ADAM_STEP)
    v_hat = v2 / (1.0 - ADAM_B2 ** ADAM_STEP)
    delta = -ADAM_LR * (m_hat / (jnp.sqrt(v_hat) + ADAM_EPS) + ADAM_WD * w)
    return delta, m2, v2


def _reduce_adamw(recv, w, m, v, name):
    _, R, C = recv.shape
    tc = _tile(C, 256)

    def body(r_ref, w_ref, m_ref, v_ref, o_ref):
        g = r_ref[0].astype(F32)
        for s in range(1, N_DEV):
            g = g + r_ref[s].astype(F32)
        delta, m2, v2 = _adamw(w_ref[...], g, m_ref[...], v_ref[...])
        o_ref[0] = g
        o_ref[1] = delta
        o_ref[2] = m2
        o_ref[3] = v2

    col = pl.BlockSpec((R, tc), lambda j: (0, j))
    return pl.pallas_call(
        body, name=name, grid=(C // tc,),
        in_specs=[pl.BlockSpec((N_DEV, R, tc), lambda j: (0, 0, j)), col, col, col],
        out_specs=pl.BlockSpec((4, R, tc), lambda j: (0, 0, j)),
        out_shape=_sds((4, R, C)),
        compiler_params=_cparams(("parallel",)),
    )(recv, w, m, v)


def _adamw_call(g, w, m, v, name):
    def body(g_ref, w_ref, m_ref, v_ref, o_ref):
        delta, m2, v2 = _adamw(w_ref[...], g_ref[...], m_ref[...], v_ref[...])
        o_ref[0] = delta
        o_ref[1] = m2
        o_ref[2] = v2

    return pl.pallas_call(body, name=name, out_shape=_sds((3,) + g.shape))(g, w, m, v)


ADA_N = 6 * D // N_CHIP
KPAD = 128


def _mod_part(c8, w_ada, b_ada):
    tn = 512

    def body(c_ref, w_ref, b_ref, o_ref):
        o_ref[...] = _raw1(_silu(c_ref[...]), w_ref[...], _NN) + b_ref[...]

    return pl.pallas_call(
        body, name="ada_mod", grid=(ADA_N // tn,),
        in_specs=[pl.BlockSpec((16, D), lambda j: (0, 0)), pl.BlockSpec((D, tn), lambda j: (0, j)),
                  pl.BlockSpec((1, tn), lambda j: (0, j))],
        out_specs=pl.BlockSpec((16, tn), lambda j: (0, j)),
        out_shape=_sds((16, ADA_N)),
        compiler_params=_cparams(("parallel",)),
    )(c8, w_ada, b_ada)


def _w_ada_update(c8p, dm, w, m, v):
    tr = 256

    def body(c_ref, dm_ref, w_ref, m_ref, v_ref, g_ref, d_ref, m2_ref, v2_ref):
        g = _raw1(_silu(c_ref[...]), dm_ref[...], _TN)
        delta, m2, v2 = _adamw(w_ref[...], g, m_ref[...], v_ref[...])
        g_ref[...] = g
        d_ref[...] = delta
        m2_ref[...] = m2
        v2_ref[...] = v2

    blk = pl.BlockSpec((tr, ADA_N), lambda i: (i, 0))
    return pl.pallas_call(
        body, name="w_ada_update", grid=(D // tr,),
        in_specs=[pl.BlockSpec((KPAD, tr), lambda i: (0, i)), pl.BlockSpec((KPAD, ADA_N), lambda i: (0, 0)),
                  blk, blk, blk],
        out_specs=[blk] * 4, out_shape=[_sds((D, ADA_N))] * 4,
        compiler_params=_cparams(("parallel",)),
    )(c8p, dm, w, m, v)


def _me():
    return lax.axis_index("x"), lax.axis_index("y"), lax.axis_index("c")


def _peer(k, me):
    mx, my, mc = me
    return (1 - mx if k & 4 else mx, 1 - my if k & 2 else my, 1 - mc if k & 1 else mc)


def _lin(p):
    return 4 * p[0] + 2 * p[1] + p[2]


def _remote(src, dst, ssem, rsem, dev):
    return pltpu.make_async_remote_copy(src_ref=src, dst_ref=dst, send_sem=ssem, recv_sem=rsem,
                                        device_id=dev, device_id_type=MESH)


def _all_gather8(x, name):
    def body(x_ref, out_ref, send_sems, recv_sems):
        me = _me()
        out_ref[_lin(me)] = x_ref[...]
        sends = []
        for k in range(1, N_DEV):
            cp = _remote(x_ref, out_ref.at[_lin(me)], send_sems.at[k - 1], recv_sems.at[k - 1], _peer(k, me))
            cp.start()
            sends.append(cp)
        for k in range(1, N_DEV):
            p = _peer(k, me)
            _remote(x_ref, out_ref.at[_lin(p)], send_sems.at[k - 1], recv_sems.at[k - 1], p).wait_recv()
        for cp in sends:
            cp.wait_send()

    return pl.pallas_call(
        body, name=name,
        out_shape=_sds((N_DEV,) + x.shape, x.dtype),
        in_specs=[pl.BlockSpec(memory_space=pltpu.VMEM)],
        out_specs=pl.BlockSpec(memory_space=pltpu.VMEM),
        scratch_shapes=[pltpu.SemaphoreType.DMA((N_DEV - 1,)), pltpu.SemaphoreType.DMA((N_DEV - 1,))],
    )(x)


def _hbm_specs(n):
    return [pl.BlockSpec(memory_space=pl.ANY)] * n


def _gather_weights(shards):
    n = len(shards)

    def body(*refs):
        plan = _gather_plan(refs[:n], refs[n:2 * n], *refs[2 * n:])
        _start(plan)
        _finish(plan)

    return pl.pallas_call(
        body, name="gather_weights",
        out_shape=_gather_shapes(shards), in_specs=_hbm_specs(n), out_specs=_hbm_specs(n),
        scratch_shapes=_gather_sems(n),
    )(*shards)


def _gather_shapes(shards):
    return [_sds((N_CHIP,) + s.shape, s.dtype) for s in shards]


def _gather_sems(n):
    return [pltpu.SemaphoreType.DMA((3 * n,)), pltpu.SemaphoreType.DMA((3 * n,)), pltpu.SemaphoreType.DMA((n,))]


def _gather_plan(ins, outs, send_sems, recv_sems, local_sems):
    mx, my, mc = _me()
    chips = [(1 - mx, my), (mx, 1 - my), (1 - mx, 1 - my)]
    local, sends, recvs = [], [], []
    for a in range(len(ins)):
        local.append(pltpu.make_async_copy(ins[a], outs[a].at[2 * mx + my], local_sems.at[a]))
        for k, (px, py) in enumerate(chips):
            sems = (send_sems.at[3 * a + k], recv_sems.at[3 * a + k], (px, py, mc))
            sends.append(_remote(ins[a], outs[a].at[2 * mx + my], *sems))
            recvs.append(_remote(ins[a], outs[a].at[2 * px + py], *sems))
    return local, sends, recvs


def _start(plan):
    local, sends, _ = plan
    for cp in local + sends:
        cp.start()


def _finish(plan):
    local, sends, recvs = plan
    for cp in recvs:
        cp.wait_recv()
    for cp in sends:
        cp.wait_send()
    for cp in local:
        cp.wait()


def _grad_exchange(pieces):
    n = len(pieces)

    def body(*refs):
        plan = _exchange_plan(refs[:n], refs[n:2 * n], *refs[2 * n:])
        _start(plan)
        _finish(plan)

    return pl.pallas_call(
        body, name="grad_exchange",
        out_shape=_exchange_shapes(pieces), in_specs=_hbm_specs(n), out_specs=_hbm_specs(n),
        scratch_shapes=_exchange_sems(n),
    )(*pieces)


def _exchange_shapes(pieces):
    return [_sds((N_DEV,) + p.shape[2:], p.dtype) for p in pieces]


def _exchange_sems(n):
    return [pltpu.SemaphoreType.DMA(((N_DEV - 1) * n,)), pltpu.SemaphoreType.DMA(((N_DEV - 1) * n,)),
            pltpu.SemaphoreType.DMA((n,))]


def _exchange_plan(ins, outs, send_sems, recv_sems, local_sems):
    me = _me()
    mx, my, mc = me
    local, sends, recvs = [], [], []
    for a in range(len(ins)):
        local.append(pltpu.make_async_copy(ins[a].at[2 * mx + my, mc], outs[a].at[_lin(me)], local_sems.at[a]))
        for k in range(1, N_DEV):
            p = _peer(k, me)
            s = (N_DEV - 1) * a + k - 1
            sends.append(_remote(ins[a].at[2 * p[0] + p[1], p[2]], outs[a].at[_lin(me)], send_sems.at[s],
                                 recv_sems.at[s], p))
            recvs.append(_remote(ins[a].at[2 * mx + my, mc], outs[a].at[_lin(p)], send_sems.at[s],
                                 recv_sems.at[s], p))
    return local, sends, recvs


def _reduce_swap(recv, name):
    _, rows, cols = recv.shape

    def body(r_ref, o_ref, send_sem, recv_sem):
        mx, my, mc = _me()
        sib = (mx, my, 1 - mc)
        g = r_ref[0].astype(F32)
        for s in range(1, N_DEV):
            g = g + r_ref[s].astype(F32)
        mine = o_ref.at[pl.ds(pl.multiple_of(mc * rows, 8), rows)]
        theirs = o_ref.at[pl.ds(pl.multiple_of((1 - mc) * rows, 8), rows)]
        mine[...] = g
        cp = _remote(mine, mine, send_sem, recv_sem, sib)
        cp.start()
        _remote(mine, theirs, send_sem, recv_sem, sib).wait_recv()
        cp.wait_send()

    return pl.pallas_call(
        body, name=name, out_shape=_sds((2 * rows, cols)),
        in_specs=[pl.BlockSpec(memory_space=pltpu.VMEM)], out_specs=pl.BlockSpec(memory_space=pltpu.VMEM),
        scratch_shapes=[pltpu.SemaphoreType.DMA, pltpu.SemaphoreType.DMA],
        compiler_params=_cparams(),
    )(recv)


def _adamw_big(g, w, m, v, name):
    rows, cols = g.shape
    tr = next((t for t in (256, 176, 128, 64, 8) if rows % t == 0), None)
    if tr is None:
        tc = _tile(cols, 256)
        blk, grid = pl.BlockSpec((rows, tc), lambda i: (0, i)), (cols // tc,)
    else:
        blk, grid = pl.BlockSpec((tr, cols), lambda i: (i, 0)), (rows // tr,)

    def body(g_ref, w_ref, m_ref, v_ref, go_ref, d_ref, m2_ref, v2_ref):
        g = g_ref[...]
        delta, m2, v2 = _adamw(w_ref[...], g, m_ref[...], v_ref[...])
        go_ref[...] = g
        d_ref[...] = delta
        m2_ref[...] = m2
        v2_ref[...] = v2

    return pl.pallas_call(
        body, name=name, grid=grid,
        in_specs=[blk] * 4, out_specs=[blk] * 4, out_shape=[_sds((rows, cols))] * 4,
        compiler_params=_cparams(("parallel",)),
    )(g, w, m, v)


SMALL_ORDER = (("mod", 6 * D), ("norm1_w", D), ("norm2_w", D), ("conv_w", CONVW * 3 * GW), ("a_log", GH),
               ("dt_bias", GH), ("gdn_norm_w", HD), ("q_norm_w", HD), ("k_norm_w", HD), ("sinks", SQH), ("loss", 1))
SMALL_R = 120


def _pack_small(d):
    parts = [d[k].reshape(-1).astype(F32) if k in d else jnp.zeros((n,), F32) for k, n in SMALL_ORDER]
    used = sum(n for _, n in SMALL_ORDER)
    parts.append(jnp.zeros((SMALL_R * LANE - used,), F32))
    return jnp.concatenate(parts).reshape(SMALL_R, LANE)


def _unpack_small(pk):
    flat = pk.reshape(-1)
    out, r = {}, 0
    for k, n in SMALL_ORDER:
        out[k] = flat[r:r + n]
        r += n
    return out


def kernel(x, c, w_ada, b_ada, norm1_w, w_in, conv_w, a_log, dt_bias, gdn_norm_w, q_norm_w, k_norm_w, sinks, w_out, norm2_w, w_gate, w_up, w_down, loss_target, m_w_ada, m_b_ada, m_norm1_w, m_w_in, m_conv_w, m_a_log, m_dt_bias, m_gdn_norm_w, m_q_norm_w, m_k_norm_w, m_sinks, m_w_out, m_norm2_w, m_w_gate, m_w_up, m_w_down, v_w_ada, v_b_ada, v_norm1_w, v_w_in, v_conv_w, v_a_log, v_dt_bias, v_gdn_norm_w, v_q_norm_w, v_k_norm_w, v_sinks, v_w_out, v_norm2_w, v_w_gate, v_w_up, v_w_down):
    mx, my, mc = _me()
    chip = 2 * mx + my
    dev = 4 * mx + 2 * my + mc
    T = x.shape[1]

    conv_sh = conv_w.reshape(CONVW, 3 * GW // N_CHIP)
    mine = jnp.concatenate([c.reshape(-1), conv_sh.reshape(-1), jnp.zeros((4 * LANE,), F32)]).reshape(24, LANE)
    got = _all_gather8(mine, "gather_c_conv")
    c8 = got[:, :8].reshape(N_DEV, D)
    conv_full = jnp.concatenate([got[2 * j, 8:20].reshape(CONVW, 3 * GW // N_CHIP) for j in range(N_CHIP)], axis=1)
    c16 = jnp.concatenate([c8, jnp.zeros((8, D), F32)], axis=0)
    b_sh = lax.dynamic_slice(b_ada, (0, chip * ADA_N), (1, ADA_N))
    mods = _all_gather8(_mod_part(c16, w_ada[0], b_sh), "gather_mod")
    mod = jnp.concatenate([lax.dynamic_slice(mods[2 * j], (dev, 0), (1, ADA_N)) for j in range(N_CHIP)], axis=1)

    as_rows = lambda t, transposed: t[0].T if transposed else t[0]
    transposed = (True, False, True, True, False)
    big_w = [as_rows(t, tr) for t, tr in zip((w_in, w_out, w_gate, w_up, w_down), transposed)]
    shards = [t.astype(BF16) for t in big_w]
    (a_in,) = _gather_weights(shards[:1])
    w_in_pt = _permute_w_in_t(a_in.reshape(PROJ, D))

    loss, grad_x, big, small = _local_step(
        x[0], loss_target[0], mod, norm1_w, w_in_pt, conv_full, a_log, dt_bias, gdn_norm_w,
        q_norm_w, k_norm_w, sinks, norm2_w, shards[1:])

    small["loss"] = loss[:, :1]
    sg = _all_gather8(_pack_small(small), "gather_small_grads")
    rep = dict(mod=(b_ada, m_b_ada, v_b_ada), norm1_w=(norm1_w, m_norm1_w, v_norm1_w),
               norm2_w=(norm2_w, m_norm2_w, v_norm2_w), a_log=(a_log, m_a_log, v_a_log),
               dt_bias=(dt_bias, m_dt_bias, v_dt_bias), gdn_norm_w=(gdn_norm_w, m_gdn_norm_w, v_gdn_norm_w),
               q_norm_w=(q_norm_w, m_q_norm_w, v_q_norm_w), k_norm_w=(k_norm_w, m_k_norm_w, v_k_norm_w),
               sinks=(sinks, m_sinks, v_sinks))
    wmv = [_pack_small({k: t[i] for k, t in rep.items()}) for i in range(3)]
    sres = _reduce_adamw(sg, wmv[0], wmv[1], wmv[2], "small_reduce_adamw")
    s_g, s_d, s_m, s_v = [_unpack_small(sres[i]) for i in range(4)]
    loss_out = s_g["loss"][0]

    g_conv = lax.dynamic_slice(s_g["conv_w"].reshape(CONVW, 3 * GW), (0, chip * (3 * GW // N_CHIP)),
                               (CONVW, 3 * GW // N_CHIP))
    pad16 = lambda t: jnp.concatenate([t.reshape(12, LANE), jnp.zeros((4, LANE), F32)], axis=0)
    cres = _adamw_call(pad16(g_conv), pad16(conv_w), pad16(m_conv_w), pad16(v_conv_w), "conv_adamw")
    conv_out = [g_conv.reshape(conv_w.shape)] + [cres[i, :12].reshape(conv_w.shape) for i in range(3)]

    dmod8 = sg[:, :6 * D // LANE].reshape(N_DEV, 6 * D)
    dm = lax.dynamic_slice(dmod8, (0, chip * ADA_N), (N_DEV, ADA_N))
    zpad = lambda t: jnp.concatenate([t, jnp.zeros((KPAD - N_DEV, t.shape[1]), F32)], axis=0)
    ares = _w_ada_update(zpad(c8), zpad(dm), w_ada[0], m_w_ada[0], v_w_ada[0])

    names = ("w_in", "w_out", "w_gate", "w_up", "w_down")
    g_full = [_reduce_swap(r, "reduce_" + nm) for r, nm in zip(big, names)]
    g_full[0] = g_full[0][:W_IN_ROWS]
    big_m = [as_rows(t, tr) for t, tr in zip((m_w_in, m_w_out, m_w_gate, m_w_up, m_w_down), transposed)]
    big_v = [as_rows(t, tr) for t, tr in zip((v_w_in, v_w_out, v_w_gate, v_w_up, v_w_down), transposed)]
    upd = [_adamw_big(g, w, m, v, "adamw_" + nm) for g, w, m, v, nm in zip(g_full, big_w, big_m, big_v, names)]
    back = lambda t, tr: (t.T if tr else t)[None]
    bg, bd, bm, bv = [[back(u[i], tr) for u, tr in zip(upd, transposed)] for i in range(4)]

    def group(a_i, small_d, conv_i, big_l):
        s = lambda k, ref: small_d[k].reshape(ref.shape)
        return [ares[a_i][None], s("mod", b_ada), s("norm1_w", norm1_w), big_l[0], conv_out[conv_i],
                s("a_log", a_log), s("dt_bias", dt_bias), s("gdn_norm_w", gdn_norm_w), s("q_norm_w", q_norm_w),
                s("k_norm_w", k_norm_w), s("sinks", sinks), big_l[1], s("norm2_w", norm2_w), big_l[2], big_l[3],
                big_l[4]]

    outs = [loss_out, grad_x[None]]
    outs += group(0, s_g, 0, bg) + group(1, s_d, 1, bd) + group(2, s_m, 2, bm) + group(3, s_v, 3, bv)
    return tuple(outs)
```

```python
import functools

import jax
import jax.numpy as jnp
from jax import lax
from jax.experimental import pallas as pl
from jax.experimental.pallas import tpu as pltpu

F32 = jnp.float32
BF16 = jnp.bfloat16
MESH = pl.DeviceIdType.MESH

D = 1024
HD = 64
GH = 8
GW = GH * HD
SQH = 8
SKVH = 2
SGRP = SQH // SKVH
WIN = 128
CONVW = 4
CHUNK = 64
DFF = 2816
PROJ = 2832
NP = 3072
EPS = 1e-6
N_DEV = 8
N_CHIP = 4

ADAM_LR = 0.001
ADAM_B1 = 0.9
ADAM_B2 = 0.999
ADAM_EPS = 1e-08
ADAM_WD = 0.01
ADAM_STEP = 10

VMEM_LIMIT = 48 * 1024 * 1024
GDN_BWD_VMEM = 58 * 1024 * 1024
LANE = 128

PACK_ROWS = (PROJ // N_CHIP, D // N_CHIP, DFF // N_CHIP, DFF // N_CHIP, DFF // N_CHIP)
PACK_P = 3104
PACK_H = PACK_P // 2


def _cparams(sem=None, vmem=VMEM_LIMIT):
    return pltpu.CompilerParams(dimension_semantics=sem, vmem_limit_bytes=vmem)


_NN = ((1,), (0,))
_NT = ((1,), (1,))
_TN = ((0,), (0,))


def _dot(a, b, dims):
    if a.ndim == 3:
        (ca,), (cb,) = dims
        return lax.dot_general(a, b, (((ca + 1,), (cb + 1,)), ((0,), (0,))), preferred_element_type=F32)
    return lax.dot_general(a, b, (dims, ((), ())), preferred_element_type=F32)


def _raw1(a, b, dims):
    return _dot(a.astype(BF16), b.astype(BF16), dims)


def _raw3(a, b, dims):
    ah = a.astype(BF16)
    al = (a - ah.astype(F32)).astype(BF16)
    bh = b.astype(BF16)
    bl = (b - bh.astype(F32)).astype(BF16)
    return _dot(ah, bh, dims) + (_dot(al, bh, dims) + _dot(ah, bl, dims))


def _make_diff_mm(raw):
    @jax.custom_vjp
    def nn(a, b):
        return raw(a, b, _NN)

    @jax.custom_vjp
    def nt(a, b):
        return raw(a, b, _NT)

    @jax.custom_vjp
    def tn(a, b):
        return raw(a, b, _TN)

    nn.defvjp(lambda a, b: (raw(a, b, _NN), (a, b)), lambda r, g: (nt(g, r[1]), tn(r[0], g)))
    nt.defvjp(lambda a, b: (raw(a, b, _NT), (a, b)), lambda r, g: (nn(g, r[1]), tn(g, r[0])))
    tn.defvjp(lambda a, b: (raw(a, b, _TN), (a, b)), lambda r, g: (nt(r[1], g), nn(r[0], g)))
    return nn, nt, tn


def _tri_inv_raw(a, nn3):
    n = a.shape[-1]
    ri = lax.broadcasted_iota(jnp.int32, (n, n), 0)
    ci = lax.broadcasted_iota(jnp.int32, (n, n), 1)
    t = (ri == ci).astype(F32)
    for lvl in range((n - 1).bit_length()):
        same_pair = (ri >> (lvl + 1)) == (ci >> (lvl + 1))
        lower_left = (((ri >> lvl) & 1) == 1) & (((ci >> lvl) & 1) == 0)
        y = jnp.where(same_pair & lower_left, a, 0.0)
        t = t - y if lvl == 0 else t - nn3(nn3(t, y), t)
    return t


class _Kit:
    def __init__(self, diff):
        if diff:
            self.nn, self.nt, self.tn = _make_diff_mm(_raw1)
            self.nn3, self.nt3, self.tn3 = _make_diff_mm(_raw3)
            nn3, nt3, tn3 = self.nn3, self.nt3, self.tn3

            @jax.custom_vjp
            def inv(a, t):
                return t

            def inv_fwd(a, t):
                return t, t

            def inv_bwd(t, g):
                return -tn3(t, nt3(g, t)), jnp.zeros_like(t)

            inv.defvjp(inv_fwd, inv_bwd)
            self.inv = inv
        else:
            self.nn = lambda a, b: _raw1(a, b, _NN)
            self.nt = lambda a, b: _raw1(a, b, _NT)
            self.tn = lambda a, b: _raw1(a, b, _TN)
            self.nn3 = lambda a, b: _raw3(a, b, _NN)
            self.nt3 = lambda a, b: _raw3(a, b, _NT)
            self.tn3 = lambda a, b: _raw3(a, b, _TN)
            self.inv = lambda a, t: _tri_inv_raw(a, self.nn3) if t is None else t


def _sigmoid(x):
    return 1.0 / (1.0 + jnp.exp(-x))


def _silu(x):
    return x * _sigmoid(x)


def _rms(x, w):
    return x * lax.rsqrt(jnp.mean(x * x, axis=-1, keepdims=True) + EPS) * w


def _tile(dim, target):
    t = (min(dim, target) // LANE) * LANE
    while t >= LANE:
        if dim % t == 0:
            return t
        t -= LANE
    return dim


MM_TM, MM_TN, MM_TK = 1408, 1536, 1408


def _matmul(a, b, ta=False, tb=False, out_dtype=F32, name="matmul", gather=None, exchange=None):
    carried = gather if gather is not None else exchange if exchange is not None else []
    nc = len(carried)
    if ta:
        K, M = a.shape
    else:
        M, K = a.shape
    if tb:
        N, K2 = b.shape
    else:
        K2, N = b.shape
    assert K == K2, (a.shape, b.shape, ta, tb)
    tm, tn, tk = _tile(M, MM_TM), _tile(N, MM_TN), _tile(K, MM_TK)
    nk = K // tk
    dims = ((0,) if ta else (1,), (1,) if tb else (0,))

    grid = (M // tm, N // tn, nk)

    def body(*refs):
        a_ref, b_ref = refs[:2]
        o_ref = refs[2 + nc]
        scratch = refs[3 + 2 * nc:]
        k = pl.program_id(2)
        if nc:
            make_plan = _gather_plan if gather is not None else _exchange_plan
            plan = make_plan(refs[2:2 + nc], refs[3 + nc:3 + 2 * nc], *scratch[-3:])
            at = lambda pos: ((pl.program_id(0) == pos[0]) & (pl.program_id(1) == pos[1]) & (k == pos[2]))

            @pl.when(at((0, 0, 0)))
            def _():
                _start(plan)

        part = _dot(a_ref[...].astype(BF16), b_ref[...].astype(BF16), dims)
        if nk == 1:
            o_ref[...] = part.astype(o_ref.dtype)
        else:
            acc_ref = scratch[0]

            @pl.when(k == 0)
            def _():
                acc_ref[...] = part

            @pl.when((k > 0) & (k < nk - 1))
            def _():
                acc_ref[...] += part

            @pl.when(k == nk - 1)
            def _():
                o_ref[...] = (acc_ref[...] + part).astype(o_ref.dtype)

        if nc:
            @pl.when(at((grid[0] - 1, grid[1] - 1, nk - 1)))
            def _():
                _finish(plan)

    a_spec = (pl.BlockSpec((tk, tm), lambda i, j, k: (k, i)) if ta
              else pl.BlockSpec((tm, tk), lambda i, j, k: (i, k)))
    b_spec = (pl.BlockSpec((tn, tk), lambda i, j, k: (j, k)) if tb
              else pl.BlockSpec((tk, tn), lambda i, j, k: (k, j)))
    if gather is not None:
        c_shapes, c_sems = _gather_shapes(carried), _gather_sems(nc)
    elif exchange is not None:
        c_shapes, c_sems = _exchange_shapes(carried), _exchange_sems(nc)
    else:
        c_shapes, c_sems = [], []
    res = pl.pallas_call(
        body, name=name, grid=grid,
        in_specs=[a_spec, b_spec] + _hbm_specs(nc),
        out_specs=[pl.BlockSpec((tm, tn), lambda i, j, k: (i, j))] + _hbm_specs(nc),
        out_shape=[jax.ShapeDtypeStruct((M, N), out_dtype)] + c_shapes,
        scratch_shapes=([pltpu.VMEM((tm, tn), F32)] if nk > 1 else []) + c_sems,
        compiler_params=_cparams(("arbitrary",) * 3 if nc else ("parallel", "parallel", "arbitrary")),
    )(a, b, *carried)
    return (res[0], res[1:]) if nc else res[0]


def _rowcall(fn, tiled, consts, out_tiled, out_acc, tm, name):
    T = tiled[0].shape[0]
    n_in = len(tiled) + len(consts)
    n_o = len(out_tiled)

    def body(*refs):
        vals = [r[...] for r in refs[:n_in]]
        outs = refs[n_in:]
        res = fn(*vals)
        for r, v in zip(outs[:n_o], res[:n_o]):
            r[...] = v.astype(r.dtype)
        if len(outs) > n_o:
            @pl.when(pl.program_id(0) == 0)
            def _():
                for r in outs[n_o:]:
                    r[...] = jnp.zeros_like(r)

            for r, v in zip(outs[n_o:], res[n_o:]):
                r[...] += v

    in_specs = [pl.BlockSpec((tm, a.shape[1]), lambda i: (i, 0)) for a in tiled]
    in_specs += [pl.BlockSpec(a.shape, lambda i, nd=a.ndim: (0,) * nd) for a in consts]
    out_specs = [pl.BlockSpec((tm, s.shape[1]), lambda i: (i, 0)) for s in out_tiled]
    out_specs += [pl.BlockSpec(s.shape, lambda i: (0, 0)) for s in out_acc]
    return pl.pallas_call(
        body, name=name, grid=(T // tm,),
        in_specs=in_specs, out_specs=out_specs,
        out_shape=list(out_tiled) + list(out_acc),
        compiler_params=_cparams(("arbitrary",)),
    )(*tiled, *consts)


def _sds(shape, dtype=F32):
    return jax.ShapeDtypeStruct(shape, dtype)


def _norm_mod(x, nw, scale, shift):
    return _rms(x, nw) * (1.0 + scale) + shift


def _norm_mod_fwd(x, nw, scale, shift):
    T = x.shape[0]
    (h,) = _rowcall(lambda *a: (_norm_mod(*a),), [x], [nw, scale, shift],
                    [_sds((T, D), BF16)], [], 512, "norm1_fwd")
    return h


ROWS_TM = 512
ROWS_EPI = 256


def _matmul_rows(a, b, epi, tiled, consts, out_tiled, out_acc, name, pieces=()):
    T, K = a.shape
    tm, tk = _tile(T, ROWS_TM), _tile(K, MM_TK)
    nm, nk = T // tm, K // tk
    npc, nt, ncst, no, na = len(pieces), len(tiled), len(consts), len(out_tiled), len(out_acc)
    n_in = 2 + nt + ncst

    def body(*refs):
        a_ref, b_ref = refs[:2]
        t_refs, c_refs = refs[2:2 + nt], refs[2 + nt:n_in]
        o_refs = refs[n_in + npc:n_in + npc + no]
        acc_refs = refs[n_in + npc + no:n_in + npc + no + na]
        n_out = no + na + npc
        res_ref = refs[n_in + npc + n_out]
        plan = _exchange_plan(refs[n_in:n_in + npc], refs[n_in + npc + no + na:n_in + npc + n_out],
                              *refs[n_in + npc + n_out + 1:]) if npc else None
        i, k = pl.program_id(0), pl.program_id(1)

        @pl.when((i == 0) & (k == 0))
        def _():
            for r in acc_refs:
                r[...] = jnp.zeros_like(r)
            if npc:
                _start(plan)

        part = _dot(a_ref[...], b_ref[...], _NN)

        @pl.when(k == 0)
        def _():
            res_ref[...] = part

        @pl.when(k > 0)
        def _():
            res_ref[...] += part

        @pl.when(k == nk - 1)
        def _():
            for r0 in range(0, tm, ROWS_EPI):
                rows = pl.ds(r0, ROWS_EPI)
                outs = epi(res_ref[rows, :], *[r[rows, :] for r in t_refs], *[r[...] for r in c_refs])
                for r, v in zip(o_refs, outs[:no]):
                    r[rows, :] = v.astype(r.dtype)
                for r, v in zip(acc_refs, outs[no:]):
                    r[...] += v

        if npc:
            @pl.when((i == nm - 1) & (k == nk - 1))
            def _():
                _finish(plan)

    row = lambda w: pl.BlockSpec((tm, w), lambda i, k: (i, 0))
    whole = lambda s: pl.BlockSpec(s.shape, lambda i, k: (0, 0))
    res = pl.pallas_call(
        body, name=name, grid=(nm, nk),
        in_specs=[pl.BlockSpec((tm, tk), lambda i, k: (i, k)), pl.BlockSpec((tk, D), lambda i, k: (k, 0))]
                 + [row(t.shape[1]) for t in tiled] + [whole(c) for c in consts] + _hbm_specs(npc),
        out_specs=[row(s.shape[1]) for s in out_tiled] + [whole(s) for s in out_acc] + _hbm_specs(npc),
        out_shape=list(out_tiled) + list(out_acc) + (_exchange_shapes(pieces) if npc else []),
        scratch_shapes=[pltpu.VMEM((tm, D), F32)] + (_exchange_sems(npc) if npc else []),
        compiler_params=_cparams(("arbitrary", "arbitrary")),
    )(a, b, *tiled, *consts, *pieces)
    return res[:no + na], res[no + na:]


def _in_proj_dx_norm_bwd(dproj, w_in_pt, x, dres, nw, scale, shift, pieces):
    T = x.shape[0]

    def epi(dh, x, dres, nw, scale, shift):
        _, vjp = jax.vjp(_norm_mod, x, nw, scale, shift)
        dx, dnw, dsc, dsh = vjp(dh)
        return dx + dres, dnw, dsc, dsh

    return _matmul_rows(dproj, w_in_pt, epi, [x, dres], [nw, scale, shift], [_sds((T, D))], [_sds((1, D))] * 3,
                        "in_proj_dx_norm1_bwd", pieces)


def _out_proj_resid_norm(mixcat, w_out, x, gate1, nw, scale, shift):
    T = x.shape[0]

    def epi(mixed, x, gate1, nw, scale, shift):
        return (mixed,) + _resid_norm(x, mixed, gate1, nw, scale, shift)

    outs, _ = _matmul_rows(mixcat, w_out, epi, [x], [gate1, nw, scale, shift],
                           [_sds((T, D)), _sds((T, D)), _sds((T, D), BF16)], [], "out_proj_resid_norm2")
    return outs


def _ffn_up_dx_resid_bwd(dab, w_gut, x, mixed, dy, gate1, nw, scale, shift):
    T = x.shape[0]

    def epi(dh2, x, mixed, dy, gate1, nw, scale, shift):
        _, vjp = jax.vjp(_resid_norm, x, mixed, gate1, nw, scale, shift)
        return vjp((dy, dh2))

    outs, _ = _matmul_rows(dab, w_gut, epi, [x, mixed, dy], [gate1, nw, scale, shift],
                           [_sds((T, D)), _sds((T, D), BF16)], [_sds((1, D))] * 4, "ffn_up_dx_resid_norm2_bwd")
    return outs


def _ffn_down_loss(act, w_down, x1, target, gate2):
    T = x1.shape[0]

    def epi(ffn, x1, target, gate2):
        y = x1 + gate2 * ffn
        err = y - target
        loss = 0.5 * jnp.sum(jnp.sum(err * err, axis=1, keepdims=True), axis=0, keepdims=True) / D
        dy = err * (1.0 / D)
        return dy, gate2 * dy, jnp.sum(dy * ffn, axis=0, keepdims=True), jnp.broadcast_to(loss, (1, LANE))

    outs, _ = _matmul_rows(act, w_down, epi, [x1, target], [gate2], [_sds((T, D)), _sds((T, D), BF16)],
                           [_sds((1, D)), _sds((1, LANE))], "ffn_down_loss")
    return outs


def _resid_norm(x, mixed, gate1, nw, scale, shift):
    x1 = x + gate1 * mixed
    return x1, _norm_mod(x1, nw, scale, shift)


def _resid_norm_fwd(x, mixed, gate1, nw, scale, shift):
    T = x.shape[0]
    return _rowcall(_resid_norm, [x, mixed], [gate1, nw, scale, shift],
                    [_sds((T, D)), _sds((T, D), BF16)], [], 512, "resid_norm2_fwd")


def _resid_norm_bwd(x, mixed, dy, dh2, gate1, nw, scale, shift):
    T = x.shape[0]

    def fn(x, mixed, dy, dh2, gate1, nw, scale, shift):
        _, vjp = jax.vjp(_resid_norm, x, mixed, gate1, nw, scale, shift)
        dx, dmixed, dg1, dnw, dsc, dsh = vjp((dy, dh2))
        return dx, dmixed, dg1, dnw, dsc, dsh

    return _rowcall(fn, [x, mixed, dy, dh2], [gate1, nw, scale, shift],
                    [_sds((T, D)), _sds((T, D), BF16)], [_sds((1, D))] * 4, 256, "resid_norm2_bwd")


FFN_BLK = 256
FFN_TM = 2048


def _interleave_gate_up(gate_t, up_t):
    blocks = lambda t: t.reshape(DFF // FFN_BLK, 1, FFN_BLK, D)
    return jnp.concatenate([blocks(gate_t), blocks(up_t)], axis=1).reshape(2 * DFF, D)


def _split_gate_up(g):
    g = g.reshape(DFF // FFN_BLK, 2, FFN_BLK, D)
    return g[:, 0].reshape(DFF, D), g[:, 1].reshape(DFF, D)


def _ffn_up_act(h2, w_gut):
    T = h2.shape[0]
    tm = _tile(T, FFN_TM)

    def body(h_ref, w_ref, ab_ref, act_ref):
        ab = _dot(h_ref[...], w_ref[...], _NT)
        ab_ref[...] = ab
        act_ref[...] = (_silu(ab[:, :FFN_BLK]) * ab[:, FFN_BLK:]).astype(act_ref.dtype)

    return pl.pallas_call(
        body, name="ffn_up_act", grid=(T // tm, DFF // FFN_BLK),
        in_specs=[pl.BlockSpec((tm, D), lambda i, j: (i, 0)), pl.BlockSpec((2 * FFN_BLK, D), lambda i, j: (j, 0))],
        out_specs=[pl.BlockSpec((tm, 2 * FFN_BLK), lambda i, j: (i, j)), pl.BlockSpec((tm, FFN_BLK), lambda i, j: (i, j))],
        out_shape=[_sds((T, 2 * DFF)), _sds((T, DFF), BF16)],
        compiler_params=_cparams(("parallel", "parallel")),
    )(h2, w_gut)


def _ffn_down_dx_act(dffn, w_down, ab):
    T = dffn.shape[0]
    tm = _tile(T, FFN_TM)

    def body(d_ref, w_ref, ab_ref, o_ref):
        dact = _dot(d_ref[...], w_ref[...], _NT)
        a, b = ab_ref[:, :FFN_BLK], ab_ref[:, FFN_BLK:]
        s = _sigmoid(a)
        da = dact * b * (s * (1.0 + a * (1.0 - s)))
        db = dact * (a * s)
        o_ref[...] = jnp.concatenate([da, db], axis=1).astype(o_ref.dtype)

    return pl.pallas_call(
        body, name="ffn_down_dx_act", grid=(T // tm, DFF // FFN_BLK),
        in_specs=[pl.BlockSpec((tm, D), lambda i, j: (i, 0)), pl.BlockSpec((FFN_BLK, D), lambda i, j: (j, 0)),
                  pl.BlockSpec((tm, 2 * FFN_BLK), lambda i, j: (i, j))],
        out_specs=pl.BlockSpec((tm, 2 * FFN_BLK), lambda i, j: (i, j)),
        out_shape=_sds((T, 2 * DFF), BF16),
        compiler_params=_cparams(("parallel", "parallel")),
    )(dffn, w_down, ab)


def _loss_head(x1, ffn, target, gate2):
    T = x1.shape[0]

    def fn(x1, ffn, target, gate2):
        y = x1 + gate2 * ffn
        err = y - target
        loss = 0.5 * jnp.sum(jnp.sum(err * err, axis=1, keepdims=True), axis=0, keepdims=True) / D
        dy = err * (1.0 / D)
        dgate2 = jnp.sum(dy * ffn, axis=0, keepdims=True)
        return dy, gate2 * dy, dgate2, jnp.broadcast_to(loss, (1, LANE))

    return _rowcall(fn, [x1, ffn, target], [gate2], [_sds((T, D)), _sds((T, D), BF16)],
                    [_sds((1, D)), _sds((1, LANE))], 256, "loss_head")


def _round_bf16(x):
    return x.astype(BF16).astype(F32)


def _shift_down(x, s, rows):
    if s == 0:
        return x
    return jnp.where(rows >= s, pltpu.roll(x, s, 0), 0.0)


def _shift_up(x, s, rows, T):
    if s == 0:
        return x
    return jnp.where(rows < T - s, pltpu.roll(x, T - s, 0), 0.0)


def _conv_fwd(proj, conv_w):
    T = proj.shape[0]
    ncol = 3 * GW // LANE

    def body(x_ref, w_ref, o_ref):
        x = _round_bf16(x_ref[...])
        rows = lax.broadcasted_iota(jnp.int32, x.shape, 0)
        acc = jnp.zeros_like(x)
        for j in range(CONVW):
            acc = acc + _round_bf16(w_ref[pl.ds(j, 1), :]) * _shift_down(x, CONVW - 1 - j, rows)
        o_ref[0], o_ref[1] = _split_pair(_silu(acc))

    return pl.pallas_call(
        body, name="conv_fwd", grid=(ncol,),
        in_specs=[pl.BlockSpec((T, LANE), lambda j: (0, j)), pl.BlockSpec((CONVW, LANE), lambda j: (0, j))],
        out_specs=pl.BlockSpec((2, T, HD), lambda j: (j, 0, 0)),
        out_shape=_sds((3 * GH, T, HD)),
        compiler_params=_cparams(("parallel",)),
    )(proj, conv_w)


RELAYOUT_TM = 4096


def _split_pair(y):
    return y[:, :HD], pltpu.roll(y, HD, 1)[:, :HD]


def _merge_pair(a, b):
    return jnp.concatenate([a, b], axis=1)


def _split_heads(x, col_block0, nheads, name):
    T = x.shape[0]
    tm = _tile(T, RELAYOUT_TM)

    def body(x_ref, o_ref):
        a, b = _split_pair(x_ref[...])
        o_ref[0] = a
        o_ref[1] = b

    return pl.pallas_call(
        body, name=name, grid=(nheads // 2, T // tm),
        in_specs=[pl.BlockSpec((tm, LANE), lambda j, i: (i, col_block0 + j))],
        out_specs=pl.BlockSpec((2, tm, HD), lambda j, i: (j, i, 0)),
        out_shape=_sds((nheads, T, HD), x.dtype),
        compiler_params=_cparams(("parallel", "parallel")),
    )(x)


def _merge_heads(hm, out_dtype, name, into=None, col_block0=0, head0=0, nheads=None):
    T = hm.shape[1]
    nheads = hm.shape[0] if nheads is None else nheads
    tm = _tile(T, RELAYOUT_TM)

    def body(*refs):
        h_ref, o_ref = refs[0], refs[-1]
        o_ref[...] = _merge_pair(h_ref[0], h_ref[1]).astype(o_ref.dtype)

    in_specs = [pl.BlockSpec((2, tm, HD), lambda j, i: (head0 // 2 + j, i, 0))]
    args = [hm]
    if into is None:
        out_shape = _sds((T, HD * nheads), out_dtype)
        aliases = {}
    else:
        out_shape = _sds(into.shape, into.dtype)
        in_specs.append(pl.BlockSpec(memory_space=pl.ANY))
        args.append(into)
        aliases = {1: 0}
    return pl.pallas_call(
        body, name=name, grid=(nheads // 2, T // tm),
        in_specs=in_specs,
        out_specs=pl.BlockSpec((tm, LANE), lambda j, i: (i, col_block0 + j)),
        out_shape=out_shape, input_output_aliases=aliases,
        compiler_params=_cparams(("parallel", "parallel")),
    )(*args)


def _conv_bwd(proj, conv_w, dqc):
    T = proj.shape[0]
    ncol = 3 * GW // LANE

    def body(x_ref, w_ref, d_ref, dx_ref, dw_ref):
        x = _round_bf16(x_ref[...])
        rows = lax.broadcasted_iota(jnp.int32, x.shape, 0)
        xs = [_shift_down(x, CONVW - 1 - j, rows) for j in range(CONVW)]
        w = [_round_bf16(w_ref[pl.ds(j, 1), :]) for j in range(CONVW)]
        pre = jnp.zeros_like(x)
        for j in range(CONVW):
            pre = pre + w[j] * xs[j]
        s = _sigmoid(pre)
        dpre = _round_bf16(_merge_pair(d_ref[0], d_ref[1]) * (s * (1.0 + pre * (1.0 - s))))
        dx = jnp.zeros_like(x)
        for j in range(CONVW):
            dx = dx + w[j] * _shift_up(dpre, CONVW - 1 - j, rows, T)
            dw_ref[pl.ds(j, 1), :] = jnp.sum(dpre * xs[j], axis=0, keepdims=True)
        dx_ref[...] = dx.astype(dx_ref.dtype)

    return pl.pallas_call(
        body, name="conv_bwd", grid=(ncol,),
        in_specs=[pl.BlockSpec((T, LANE), lambda j: (0, j)), pl.BlockSpec((CONVW, LANE), lambda j: (0, j)),
                  pl.BlockSpec((2, T, HD), lambda j: (j, 0, 0))],
        out_specs=[pl.BlockSpec((T, LANE), lambda j: (0, j)), pl.BlockSpec((CONVW, LANE), lambda j: (0, j))],
        out_shape=[_sds((T, NP), BF16), _sds((CONVW, 3 * GW))],
        compiler_params=_cparams(("parallel",)),
    )(proj, conv_w, dqc)


def _gdn_prep(kit, q, k, v, ga, gb, alog, dtb, t_inv=None):
    C = CHUNK
    ri = lax.broadcasted_iota(jnp.int32, (C, C), 0)
    ci = lax.broadcasted_iota(jnp.int32, (C, C), 1)
    causal = ri >= ci
    strict = ri > ci
    eye = (ri == ci).astype(F32)
    lower = causal.astype(F32)
    upper = (ri <= ci).astype(F32)

    a = ga + dtb
    softplus = jnp.maximum(a, 0.0) + jnp.log(1.0 + jnp.exp(-jnp.abs(a)))
    g_row = -jnp.exp(alog) * softplus
    beta_row = _sigmoid(gb)
    g_col = jnp.sum(eye * g_row, axis=2, keepdims=True)
    beta_col = jnp.sum(eye * beta_row, axis=2, keepdims=True)
    G_col = jnp.sum(lower * g_row, axis=2, keepdims=True)
    G_row = jnp.sum(upper * g_col, axis=1, keepdims=True)
    G_last = jnp.sum(g_row, axis=2, keepdims=True)
    decay = jnp.exp(jnp.where(causal, G_col - G_row, -1e30))

    qn = q * lax.rsqrt(jnp.sum(q * q, axis=-1, keepdims=True) + EPS) * (HD ** -0.5)
    kn = k * lax.rsqrt(jnp.sum(k * k, axis=-1, keepdims=True) + EPS)
    kb = kn * beta_col
    A = jnp.where(strict, kit.nt(kb, kn) * decay, 0.0)
    Tm = kit.inv(A, t_inv)
    eG = jnp.exp(G_col)
    u = kit.nn3(Tm, v * beta_col)
    w = kit.nn3(Tm, kb * eG)
    qk = jnp.where(causal, kit.nt(qn, kn) * decay, 0.0)
    q_dec = qn * eG
    k_dec = kn * jnp.exp(G_last - G_col)
    dec = jnp.exp(G_last)
    return u, w, qk, q_dec, k_dec, dec, Tm


def _gdn_out(o, z, nw):
    return _rms(o, nw) * _silu(z)


GDN_CB = 4


def _gdn_specs(T, blk):
    TB = GDN_CB * CHUNK
    seq = lambda grp: pl.BlockSpec((GH, TB, HD), lambda i, grp=grp: (grp, blk(i), 0))
    row = lambda grp: pl.BlockSpec((GH, GDN_CB, 1, CHUNK), lambda i, grp=grp: (grp, blk(i), 0, 0))
    per_head = pl.BlockSpec((GH, 1, CHUNK), lambda i: (0, 0, 0))
    whole = pl.BlockSpec((1, HD), lambda i: (0, 0))
    state = pl.BlockSpec((GH, GDN_CB, HD, HD), lambda i: (0, blk(i), 0, 0))
    return seq, row, per_head, whole, state


def _gdn_load(seq_refs, row_refs, head_refs):
    chunks = lambda r: jnp.concatenate([r[:, pl.ds(cb * CHUNK, CHUNK), :] for cb in range(GDN_CB)], axis=0)
    rows = lambda r: jnp.concatenate([r[:, cb] for cb in range(GDN_CB)], axis=0)
    heads = lambda r: jnp.concatenate([r[...]] * GDN_CB, axis=0)
    return [chunks(r) for r in seq_refs], [rows(r) for r in row_refs], [heads(r) for r in head_refs]


def _gdn_fwd(qkv_hm, zs_hm, gab, alog_b, dtb_b, nw, shards):
    T = qkv_hm.shape[1]
    N = T // CHUNK
    nblk = N // GDN_CB
    ns = len(shards)
    seq, row, per_head, whole, state = _gdn_specs(T, lambda i: i)
    kit = _Kit(False)

    def body(*refs):
        q_ref, k_ref, v_ref, z_ref, ga_ref, gb_ref, al_ref, dt_ref, nw_ref = refs[:9]
        o_ref, S_ref, T_ref = refs[9 + ns:12 + ns]
        S_scr = refs[12 + 2 * ns]
        plan = _gather_plan(refs[9:9 + ns], refs[12 + ns:12 + 2 * ns], *refs[13 + 2 * ns:])

        @pl.when(pl.program_id(0) == 0)
        def _():
            S_scr[...] = jnp.zeros_like(S_scr)
            _start(plan)

        (q, k, v, z), (ga, gb), (al, dt) = _gdn_load((q_ref, k_ref, v_ref, z_ref), (ga_ref, gb_ref), (al_ref, dt_ref))
        u, w, qk, q_dec, k_dec, dec, t_inv = _gdn_prep(kit, q, k, v, ga, gb, al, dt)
        S = S_scr[...]
        for cb in range(GDN_CB):
            hs = slice(cb * GH, (cb + 1) * GH)
            S_ref[:, cb] = S
            T_ref[:, cb] = t_inv[hs]
            v_new = u[hs] - kit.nn(w[hs], S)
            o = kit.nn(q_dec[hs], S) + kit.nn(qk[hs], v_new)
            S = S * dec[hs] + kit.tn(k_dec[hs], v_new)
            o_ref[:, pl.ds(cb * CHUNK, CHUNK), :] = _gdn_out(o, z[hs], nw_ref[...])
        S_scr[...] = S

        @pl.when(pl.program_id(0) == nblk - 1)
        def _():
            _finish(plan)

    res = pl.pallas_call(
        body, name="gdn_fwd", grid=(nblk,),
        in_specs=[seq(0), seq(1), seq(2), seq(0), row(0), row(1), per_head, per_head, whole] + _hbm_specs(ns),
        out_specs=[seq(0), state, state] + _hbm_specs(ns),
        out_shape=[_sds((GH + SQH, T, HD)), _sds((GH, N, HD, HD)), _sds((GH, N, CHUNK, CHUNK))]
                  + _gather_shapes(shards),
        scratch_shapes=[pltpu.VMEM((GH, HD, HD), F32)] + _gather_sems(ns),
        compiler_params=_cparams(("arbitrary",)),
    )(qkv_hm, qkv_hm, qkv_hm, zs_hm, gab, gab, alog_b, dtb_b, nw, *shards)
    return res[0], (res[1], res[2]), res[3:]


def _gdn_bwd(qkv_hm, zs_hm, gab, alog_b, dtb_b, nw, S_all, do, pieces):
    T = qkv_hm.shape[1]
    N = T // CHUNK
    nblk = N // GDN_CB
    npc = len(pieces)
    dkit, kit = _Kit(True), _Kit(False)
    rseq, rrow, per_head, whole, rstate = _gdn_specs(T, lambda i: nblk - 1 - i)

    def body(*refs):
        q_ref, k_ref, v_ref, z_ref, ga_ref, gb_ref, al_ref, dt_ref, nw_ref, S_ref, T_ref, do_ref = refs[:12]
        dqkv_ref, dz_ref, dga_ref, dgb_ref, dal_ref, ddt_ref, dnw_ref = refs[12 + npc:19 + npc]
        dS_scr = refs[19 + 2 * npc]
        plan = _exchange_plan(refs[12:12 + npc], refs[19 + npc:19 + 2 * npc], *refs[20 + 2 * npc:])

        @pl.when(pl.program_id(0) == 0)
        def _():
            dS_scr[...] = jnp.zeros_like(dS_scr)
            dal_ref[...] = jnp.zeros_like(dal_ref)
            ddt_ref[...] = jnp.zeros_like(ddt_ref)
            dnw_ref[...] = jnp.zeros_like(dnw_ref)
            _start(plan)

        (q, k, v, z, dout), (ga, gb), (al, dt) = _gdn_load((q_ref, k_ref, v_ref, z_ref, do_ref), (ga_ref, gb_ref),
                                                          (al_ref, dt_ref))
        S_in = jnp.concatenate([S_ref[:, cb] for cb in range(GDN_CB)], axis=0)
        t_inv = jnp.concatenate([T_ref[:, cb] for cb in range(GDN_CB)], axis=0)
        prep = lambda *a: _gdn_prep(dkit, *a, t_inv=t_inv)[:6]
        (u, w, qk, q_dec, k_dec, dec), prep_vjp = jax.vjp(prep, q, k, v, ga, gb, al, dt)
        v_new = u - kit.nn(w, S_in)
        o = kit.nn(q_dec, S_in) + kit.nn(qk, v_new)
        _, out_vjp = jax.vjp(_gdn_out, o, z, nw_ref[...])
        do, dz, dnw = out_vjp(dout)
        dvn_part = kit.tn(qk, do)
        dS_part = kit.tn(q_dec, do)
        dS = dS_scr[...]
        dS_out, dvn = [None] * GDN_CB, [None] * GDN_CB
        for cb in reversed(range(GDN_CB)):
            hs = slice(cb * GH, (cb + 1) * GH)
            dS_out[cb] = dS
            dvn[cb] = dvn_part[hs] + kit.nn(k_dec[hs], dS)
            dS = dS * dec[hs] + dS_part[hs] - kit.tn(w[hs], dvn[cb])
        dS_scr[...] = dS
        dS_out = jnp.concatenate(dS_out, axis=0)
        dvn = jnp.concatenate(dvn, axis=0)
        ddec = jnp.sum(jnp.sum(S_in * dS_out, axis=2, keepdims=True), axis=1, keepdims=True)
        cts = (dvn, -kit.nt(dvn, S_in), kit.nt(do, v_new), kit.nt(do, S_in), kit.nt(v_new, dS_out), ddec)
        dq, dk, dv, dga, dgb, dal, ddt = prep_vjp(cts)
        lanesum = lambda t: jnp.broadcast_to(jnp.sum(t, axis=2, keepdims=True), t.shape)
        for cb in range(GDN_CB):
            hs = slice(cb * GH, (cb + 1) * GH)
            sl = pl.ds(cb * CHUNK, CHUNK)
            dqkv_ref[pl.ds(0, GH), sl, :] = dq[hs]
            dqkv_ref[pl.ds(GH, GH), sl, :] = dk[hs]
            dqkv_ref[pl.ds(2 * GH, GH), sl, :] = dv[hs]
            dz_ref[:, sl, :] = dz[hs]
            dga_ref[:, cb] = dga[hs]
            dgb_ref[:, cb] = dgb[hs]
            dal_ref[...] += lanesum(dal[hs])
            ddt_ref[...] += lanesum(ddt[hs])
        dnw_ref[...] += dnw

        @pl.when(pl.program_id(0) == nblk - 1)
        def _():
            _finish(plan)

    res = pl.pallas_call(
        body, name="gdn_bwd", grid=(nblk,),
        in_specs=[rseq(0), rseq(1), rseq(2), rseq(0), rrow(0), rrow(1), per_head, per_head, whole, rstate, rstate,
                  rseq(0)] + _hbm_specs(npc),
        out_specs=[pl.BlockSpec((3 * GH, GDN_CB * CHUNK, HD), lambda i: (0, nblk - 1 - i, 0)), rseq(0), rrow(0),
                   rrow(0), per_head, per_head, whole] + _hbm_specs(npc),
        out_shape=[_sds((3 * GH, T, HD)), _sds((GH + 4 + SWA_GRAD_HEADS, T, HD))] + [_sds((GH, N, 1, CHUNK))] * 2
                  + [_sds((GH, 1, CHUNK))] * 2 + [_sds((1, HD))] + _exchange_shapes(pieces),
        scratch_shapes=[pltpu.VMEM((GH, HD, HD), F32)] + _exchange_sems(npc),
        compiler_params=_cparams(("arbitrary",), GDN_BWD_VMEM),
    )(qkv_hm, qkv_hm, qkv_hm, zs_hm, gab, gab, alog_b, dtb_b, nw, S_all[0], S_all[1], do, *pieces)
    return res[:7], res[7:]


def _swa_block(kit, first, q0, q1, q2, q3, kp, kc, vp, vc, qnw, knw, s0, s1, s2, s3, *, slopes):
    W = WIN
    ri = lax.broadcasted_iota(jnp.int32, (W, W), 0)
    ci = lax.broadcasted_iota(jnp.int32, (W, W), 1)
    mask_c = ri >= ci
    mask_p = ci > ri + first * W
    dist_c = (ri - ci).astype(F32)
    dist_p = (ri - ci + W).astype(F32)
    kpn = _rms(kp, knw)
    kcn = _rms(kc, knw)
    outs = []
    for q, sink, slope in zip((q0, q1, q2, q3), (s0, s1, s2, s3), slopes):
        qn = _rms(q, qnw)
        sc = jnp.where(mask_c, kit.nt(qn, kcn) * (HD ** -0.5) - slope * dist_c, -1e30)
        sp = jnp.where(mask_p, kit.nt(qn, kpn) * (HD ** -0.5) - slope * dist_p, -1e30)
        m = jnp.maximum(jnp.maximum(jnp.max(sc, axis=-1, keepdims=True), jnp.max(sp, axis=-1, keepdims=True)), sink)
        m = lax.stop_gradient(m)
        pc = jnp.exp(sc - m)
        pp = jnp.exp(sp - m)
        den = jnp.sum(pc, axis=-1, keepdims=True) + jnp.sum(pp, axis=-1, keepdims=True) + jnp.exp(sink - m)
        inv = 1.0 / den
        outs.append(kit.nn(pc * inv, vc) + kit.nn(pp * inv, vp))
    return tuple(outs)


def _swa_slopes(hk):
    return tuple(jnp.where(hk == 0, 2.0 ** (-8.0 * (g + 1.0) / SQH), 2.0 ** (-8.0 * (SGRP + g + 1.0) / SQH))
                 for g in range(SGRP))


def _swa_fwd(zs_hm, qnw, knw, sinks_col):
    T = zs_hm.shape[1]
    NB = T // WIN
    kit = _Kit(False)

    def body(q_ref, kp_ref, kc_ref, vp_ref, vc_ref, qnw_ref, knw_ref, s_ref, o_ref):
        hk = pl.program_id(0)
        first = (pl.program_id(1) == 0).astype(jnp.int32)
        args = ([q_ref[g] for g in range(SGRP)] + [kp_ref[...], kc_ref[...], vp_ref[...], vc_ref[...],
                                                     qnw_ref[...], knw_ref[...]] + [s_ref[g] for g in range(SGRP)])
        outs = _swa_block(kit, first, *args, slopes=_swa_slopes(hk))
        for g in range(SGRP):
            o_ref[g] = outs[g]

    qspec = pl.BlockSpec((SGRP, WIN, HD), lambda hk, n: (2 + hk, n, 0))
    cur = lambda off: pl.BlockSpec((None, WIN, HD), lambda hk, n, off=off: (off + hk, n, 0))
    prev = lambda off: pl.BlockSpec((None, WIN, HD), lambda hk, n, off=off: (off + hk, jnp.maximum(n - 1, 0), 0))
    whole = pl.BlockSpec((1, HD), lambda hk, n: (0, 0))
    sspec = pl.BlockSpec((SGRP, WIN, 1), lambda hk, n: (hk, 0, 0))
    return pl.pallas_call(
        body, name="swa_fwd", grid=(SKVH, NB),
        in_specs=[qspec, prev(16), cur(16), prev(18), cur(18), whole, whole, sspec],
        out_specs=pl.BlockSpec((SGRP, WIN, HD), lambda hk, n: (hk, n, 0)),
        out_shape=_sds((SQH, T, HD)),
        compiler_params=_cparams(("parallel", "arbitrary")),
    )(zs_hm, zs_hm, zs_hm, zs_hm, zs_hm, qnw, knw, sinks_col)


def _swa_bwd(zs_hm, qnw, knw, sinks_col, do):
    T = zs_hm.shape[1]
    NB = T // WIN
    kit = _Kit(True)

    def body(q_ref, kp_ref, kc_ref, vp_ref, vc_ref, qnw_ref, knw_ref, s_ref, do_ref,
             dq_ref, dk_ref, dv_ref, dqnw_ref, dknw_ref, ds_ref, ck_scr, cv_scr):
        hk = pl.program_id(0)
        i = pl.program_id(1)
        first = (i == NB - 1).astype(jnp.int32)

        @pl.when(i == 0)
        def _():
            ck_scr[...] = jnp.zeros_like(ck_scr)
            cv_scr[...] = jnp.zeros_like(cv_scr)
            ds_ref[...] = jnp.zeros_like(ds_ref)

        @pl.when((i == 0) & (hk == 0))
        def _():
            dqnw_ref[...] = jnp.zeros_like(dqnw_ref)
            dknw_ref[...] = jnp.zeros_like(dknw_ref)

        args = ([q_ref[g] for g in range(SGRP)] + [kp_ref[...], kc_ref[...], vp_ref[...], vc_ref[...],
                                                     qnw_ref[...], knw_ref[...]] + [s_ref[g] for g in range(SGRP)])
        dos = tuple(do_ref[g] for g in range(SGRP))
        _, vjp = jax.vjp(functools.partial(_swa_block, kit, first, slopes=_swa_slopes(hk)), *args)
        gr = vjp(dos)
        for g in range(SGRP):
            dq_ref[g] = gr[g]
            ds_ref[g] += jnp.broadcast_to(jnp.sum(gr[10 + g], axis=0, keepdims=True), (WIN, 1))
        dkp, dkc, dvp, dvc = gr[4:8]
        dk_ref[...] = dkc + ck_scr[...]
        dv_ref[...] = dvc + cv_scr[...]
        ck_scr[...] = dkp
        cv_scr[...] = dvp
        dqnw_ref[...] += gr[8]
        dknw_ref[...] += gr[9]

    rn = lambda n: NB - 1 - n
    qspec = pl.BlockSpec((SGRP, WIN, HD), lambda hk, i: (2 + hk, rn(i), 0))
    cur = lambda off: pl.BlockSpec((None, WIN, HD), lambda hk, i, off=off: (off + hk, rn(i), 0))
    prev = lambda off: pl.BlockSpec((None, WIN, HD), lambda hk, i, off=off: (off + hk, jnp.maximum(rn(i) - 1, 0), 0))
    whole = pl.BlockSpec((1, HD), lambda hk, i: (0, 0))
    sspec = pl.BlockSpec((SGRP, WIN, 1), lambda hk, i: (hk, 0, 0))
    ospec = pl.BlockSpec((SGRP, WIN, HD), lambda hk, i: (hk, rn(i), 0))
    return pl.pallas_call(
        body, name="swa_bwd", grid=(SKVH, NB),
        in_specs=[qspec, prev(16), cur(16), prev(18), cur(18), whole, whole, sspec, ospec],
        out_specs=[ospec, cur(0), cur(0), whole, whole, sspec],
        out_shape=[_sds((SQH, T, HD)), _sds((SKVH, T, HD)), _sds((SKVH, T, HD)),
                   _sds((1, HD)), _sds((1, HD)), _sds((SQH, WIN, 1))],
        scratch_shapes=[pltpu.VMEM((WIN, HD), F32), pltpu.VMEM((WIN, HD), F32)],
        compiler_params=_cparams(("arbitrary", "arbitrary")),
    )(zs_hm, zs_hm, zs_hm, zs_hm, zs_hm, qnw, knw, sinks_col, do)


def _swa_heads(kit, first, q, kp, kc, vp, vc, qnw, knw, sink, slope):
    W = WIN
    ri = lax.broadcasted_iota(jnp.int32, (W, W), 0)
    ci = lax.broadcasted_iota(jnp.int32, (W, W), 1)
    mask_c = ri >= ci
    mask_p = ci > ri + first * W
    dist_c = (ri - ci).astype(F32)
    dist_p = (ri - ci + W).astype(F32)
    kpn = _rms(kp, knw)
    kcn = _rms(kc, knw)
    qn = _rms(q, qnw)
    sc = jnp.where(mask_c, kit.nt(qn, kcn) * (HD ** -0.5) - slope * dist_c, -1e30)
    sp = jnp.where(mask_p, kit.nt(qn, kpn) * (HD ** -0.5) - slope * dist_p, -1e30)
    m = jnp.maximum(jnp.maximum(jnp.max(sc, axis=-1, keepdims=True), jnp.max(sp, axis=-1, keepdims=True)), sink)
    m = lax.stop_gradient(m)
    pc = jnp.exp(sc - m)
    pp = jnp.exp(sp - m)
    den = jnp.sum(pc, axis=-1, keepdims=True) + jnp.sum(pp, axis=-1, keepdims=True) + jnp.exp(sink - m)
    inv = 1.0 / den
    return kit.nn(pc * inv, vc) + kit.nn(pp * inv, vp)


def _per_query_head(kv_ref):
    return jnp.concatenate([kv_ref[pl.ds(h // SGRP, 1)] for h in range(SQH)], axis=0)


def _per_kv_head(d):
    return jnp.concatenate([jnp.sum(d[g * SGRP:(g + 1) * SGRP], axis=0, keepdims=True) for g in range(SKVH)], axis=0)


def _swa_specs(blk):
    qspec = pl.BlockSpec((SQH, WIN, HD), lambda i: (1, blk(i), 0))
    cur = lambda grp: pl.BlockSpec((SKVH, WIN, HD), lambda i, grp=grp: (grp, blk(i), 0))
    prev = lambda grp: pl.BlockSpec((SKVH, WIN, HD), lambda i, grp=grp: (grp, jnp.maximum(blk(i) - 1, 0), 0))
    whole = pl.BlockSpec((1, HD), lambda i: (0, 0))
    col = pl.BlockSpec((SQH, WIN, 1), lambda i: (0, 0, 0))
    ospec = pl.BlockSpec((SQH, WIN, HD), lambda i: (0, blk(i), 0))
    return qspec, cur, prev, whole, col, ospec


def _swa_fwd(zs_hm, qnw, knw, sinks_col, slopes_col, o_buf, shards):
    T = zs_hm.shape[1]
    NB = T // WIN
    ns = len(shards)
    kit = _Kit(False)
    qspec, cur, prev, whole, col, _ = _swa_specs(lambda i: i)

    def body(*refs):
        q_ref, kp_ref, kc_ref, vp_ref, vc_ref, qnw_ref, knw_ref, s_ref, sl_ref = refs[:9]
        o_ref = refs[10 + ns]
        plan = _gather_plan(refs[10:10 + ns], refs[11 + ns:11 + 2 * ns], *refs[11 + 2 * ns:])

        @pl.when(pl.program_id(0) == 0)
        def _():
            _start(plan)

        first = (pl.program_id(0) == 0).astype(jnp.int32)
        o_ref[...] = _swa_heads(kit, first, q_ref[...], _per_query_head(kp_ref), _per_query_head(kc_ref),
                                _per_query_head(vp_ref), _per_query_head(vc_ref), qnw_ref[...], knw_ref[...],
                                s_ref[...], sl_ref[...])

        @pl.when(pl.program_id(0) == NB - 1)
        def _():
            _finish(plan)

    res = pl.pallas_call(
        body, name="swa_fwd", grid=(NB,),
        in_specs=[qspec, prev(8), cur(8), prev(9), cur(9), whole, whole, col, col] + _hbm_specs(1 + ns),
        out_specs=[pl.BlockSpec((SQH, WIN, HD), lambda i: (1, i, 0))] + _hbm_specs(ns),
        out_shape=[_sds(o_buf.shape)] + _gather_shapes(shards),
        input_output_aliases={9: 0},
        scratch_shapes=_gather_sems(ns),
        compiler_params=_cparams(("arbitrary",)),
    )(zs_hm, zs_hm, zs_hm, zs_hm, zs_hm, qnw, knw, sinks_col, slopes_col, o_buf, *shards)
    return res[0], res[1:]


SWA_GRAD_HEADS = SQH + 2 * SKVH


def _swa_bwd(zs_hm, qnw, knw, sinks_col, slopes_col, dmix_hm, d_buf):
    T = zs_hm.shape[1]
    NB = T // WIN
    kit = _Kit(True)
    qspec, cur, prev, whole, col, _ = _swa_specs(lambda i: NB - 1 - i)

    def body(q_ref, kp_ref, kc_ref, vp_ref, vc_ref, qnw_ref, knw_ref, s_ref, sl_ref, do_ref, buf_ref,
             d_ref, dqnw_ref, dknw_ref, ds_ref, ck_scr, cv_scr):
        dq_ref = d_ref.at[pl.ds(0, SQH)]
        dk_ref = d_ref.at[pl.ds(SQH, SKVH)]
        dv_ref = d_ref.at[pl.ds(SQH + SKVH, SKVH)]
        i = pl.program_id(0)
        first = (i == NB - 1).astype(jnp.int32)

        @pl.when(i == 0)
        def _():
            ck_scr[...] = jnp.zeros_like(ck_scr)
            cv_scr[...] = jnp.zeros_like(cv_scr)
            ds_ref[...] = jnp.zeros_like(ds_ref)
            dqnw_ref[...] = jnp.zeros_like(dqnw_ref)
            dknw_ref[...] = jnp.zeros_like(dknw_ref)

        fn = lambda q, kp, kc, vp, vc, qnw, knw, sink: _swa_heads(kit, first, q, kp, kc, vp, vc, qnw, knw, sink,
                                                                  sl_ref[...])
        _, vjp = jax.vjp(fn, q_ref[...], _per_query_head(kp_ref), _per_query_head(kc_ref), _per_query_head(vp_ref),
                         _per_query_head(vc_ref), qnw_ref[...], knw_ref[...], s_ref[...])
        dq, dkp, dkc, dvp, dvc, dqnw, dknw, dsink = vjp(do_ref[...])
        dq_ref[...] = dq
        dk_ref[...] = _per_kv_head(dkc) + ck_scr[...]
        dv_ref[...] = _per_kv_head(dvc) + cv_scr[...]
        ck_scr[...] = _per_kv_head(dkp)
        cv_scr[...] = _per_kv_head(dvp)
        dqnw_ref[...] += dqnw
        dknw_ref[...] += dknw
        ds_ref[...] += jnp.broadcast_to(jnp.sum(dsink, axis=1, keepdims=True), dsink.shape)

    dospec = pl.BlockSpec((SQH, WIN, HD), lambda i: (1, NB - 1 - i, 0))
    dspec = pl.BlockSpec((SWA_GRAD_HEADS, WIN, HD), lambda i: (1, NB - 1 - i, 0))
    res = pl.pallas_call(
        body, name="swa_bwd", grid=(NB,),
        in_specs=[qspec, prev(8), cur(8), prev(9), cur(9), whole, whole, col, col, dospec] + _hbm_specs(1),
        out_specs=[dspec, whole, whole, col],
        out_shape=[_sds(d_buf.shape), _sds((1, HD)), _sds((1, HD)), _sds((SQH, WIN, 1))],
        input_output_aliases={10: 0},
        scratch_shapes=[pltpu.VMEM((SKVH, WIN, HD), F32), pltpu.VMEM((SKVH, WIN, HD), F32)],
        compiler_params=_cparams(("arbitrary",)),
    )(zs_hm, zs_hm, zs_hm, zs_hm, zs_hm, qnw, knw, sinks_col, slopes_col, dmix_hm, d_buf)
    return res


GAB0 = 3 * GW + 1280


W_IN_ROWS = PROJ // N_CHIP
W_IN_ROWS_PAD = 736


def _permute_w_in_t(w_in_t):
    return jnp.concatenate([w_in_t[:4 * GW], w_in_t[4 * GW + 2 * GH:], w_in_t[4 * GW:4 * GW + 2 * GH],
                            jnp.zeros((NP - PROJ, D), w_in_t.dtype)], axis=0)


def _w_in_grad_pieces(g_t):
    g = jnp.concatenate([g_t[:4 * GW], g_t[GAB0:GAB0 + 2 * GH], g_t[4 * GW:GAB0]], axis=0)
    g = jnp.pad(g.reshape(N_CHIP, W_IN_ROWS, D), ((0, 0), (0, W_IN_ROWS_PAD - W_IN_ROWS), (0, 0)))
    return g.reshape(N_CHIP, 2, W_IN_ROWS_PAD // 2, D)


def _pieces_by_rows(g):
    return g.reshape(N_CHIP, 2, g.shape[0] // (2 * N_CHIP), D)


def _local_step(x, target, mod, n1w, w_in_pt, conv_w, alog, dtb, gnw, qnw, knw, sinks, n2w, shards):
    sh_out, sh_gate, sh_up, sh_down = shards
    T = x.shape[0]
    N = T // CHUNK
    shift1, scale1, gate1, shift2, scale2, gate2 = [mod[:, i * D:(i + 1) * D] for i in range(6)]

    h = _norm_mod_fwd(x, n1w, scale1, shift1)
    proj, (a_out,) = _matmul(h, w_in_pt, tb=True, name="in_proj", gather=[sh_out])
    w_out = a_out.reshape(D, D)
    qkv_hm = _conv_fwd(proj, conv_w)
    zs_hm = _split_heads(proj, 3 * GW // LANE, 20, "split_zs")
    gab = proj[:, GAB0:GAB0 + 2 * GH].T.reshape(2 * GH, N, 1, CHUNK)
    alog_b = jnp.broadcast_to(alog.reshape(GH, 1, 1), (GH, 1, CHUNK))
    dtb_b = jnp.broadcast_to(dtb.reshape(GH, 1, 1), (GH, 1, CHUNK))
    sinks_col = jnp.broadcast_to(sinks.reshape(SQH, 1, 1), (SQH, WIN, 1))
    o_hm, S_all, (a_gate, a_up) = _gdn_fwd(qkv_hm, zs_hm, gab, alog_b, dtb_b, gnw, [sh_gate, sh_up])
    w_gut = _interleave_gate_up(a_gate.reshape(DFF, D), a_up.reshape(DFF, D))
    slopes = 2.0 ** (-8.0 * (jnp.arange(SQH, dtype=F32) + 1.0) / SQH)
    slopes_col = jnp.broadcast_to(slopes.reshape(SQH, 1, 1), (SQH, WIN, 1))
    o_hm, (a_down,) = _swa_fwd(zs_hm, qnw, knw, sinks_col, slopes_col, o_hm, [sh_down])
    w_down = a_down.reshape(DFF, D)
    mixcat = _merge_heads(o_hm, BF16, "merge_mix")
    mixed, x1, h2 = _out_proj_resid_norm(mixcat, w_out, x, gate1, n2w, scale2, shift2)
    ab, act = _ffn_up_act(h2, w_gut)
    dy, dffn, dgate2, loss = _ffn_down_loss(act, w_down, x1, target, gate2)

    dab = _ffn_down_dx_act(dffn, w_down, ab)
    g_w_down = _matmul(act, dffn, ta=True, out_dtype=BF16, name="ffn_down_dw")
    g_w_gut = _matmul(dab, h2, ta=True, out_dtype=BF16, name="ffn_up_dw")
    dx1, dmixed, dgate1, dn2w, dscale2, dshift2 = _ffn_up_dx_resid_bwd(dab, w_gut, x, mixed, dy, gate1, n2w, scale2,
                                                                       shift2)
    g_w_out = _matmul(mixcat, dmixed, ta=True, out_dtype=BF16, name="out_proj_dw")
    dmix_hm = _split_heads(_matmul(dmixed, w_out, tb=True, name="out_proj_dx"), 0, GH + SQH, "split_dmix")
    g_gate_t, g_up_t = _split_gate_up(g_w_gut)
    pieces = [_pieces_by_rows(g_w_out), _pieces_by_rows(g_gate_t), _pieces_by_rows(g_up_t),
              _pieces_by_rows(g_w_down)]
    (dqkv_hm, d_hm, dga, dgb, dalog, ddtb, dgnw), recv = _gdn_bwd(qkv_hm, zs_hm, gab, alog_b, dtb_b, gnw, S_all,
                                                                  dmix_hm, pieces)
    d_hm, dqnw, dknw, dsinks = _swa_bwd(zs_hm, qnw, knw, sinks_col, slopes_col, dmix_hm, d_hm)
    dproj, dconv = _conv_bwd(proj, conv_w, dqkv_hm)
    dproj = _merge_heads(d_hm, BF16, "merge_dz", into=dproj, col_block0=3 * GW // LANE, head0=0, nheads=GH)
    dproj = _merge_heads(d_hm, BF16, "merge_dswa", into=dproj, col_block0=4 * GW // LANE, head0=GH + 4,
                         nheads=SWA_GRAD_HEADS)
    dgab = jnp.concatenate([dga, dgb], axis=0).reshape(2 * GH, T).T.astype(BF16)
    dproj = lax.dynamic_update_slice(dproj, jnp.concatenate([dgab, jnp.zeros((T, NP - PROJ), BF16)], axis=1),
                                     (0, GAB0))
    g_w_in_pt = _matmul(dproj, h, ta=True, out_dtype=BF16, name="in_proj_dw")
    (grad_x, dn1w, dscale1, dshift1), recv_in = _in_proj_dx_norm_bwd(dproj, w_in_pt, x, dx1, n1w, scale1, shift1,
                                                                     [_w_in_grad_pieces(g_w_in_pt)])

    dmod = jnp.concatenate([dshift1, dscale1, dgate1, dshift2, dscale2, dgate2], axis=1)
    big = list(recv_in) + list(recv)
    small = dict(mod=dmod, norm1_w=dn1w, norm2_w=dn2w, conv_w=dconv, a_log=dalog[:, 0, 0], dt_bias=ddtb[:, 0, 0],
                 gdn_norm_w=dgnw, q_norm_w=dqnw, k_norm_w=dknw, sinks=dsinks[:, 0, 0])
    return loss, grad_x, big, small


def _adamw(w, g, m, v):
    m2 = ADAM_B1 * m + (1.0 - ADAM_B1) * g
    v2 = ADAM_B2 * v + (1.0 - ADAM_B2) * (g * g)
    m_hat = m2 / (1.0 - ADAM_B1 ** ADAM_STEP)
    v_hat = v2 / (1.0 - ADAM_B2 ** ADAM_STEP)
    delta = -ADAM_LR * (m_hat / (jnp.sqrt(v_hat) + ADAM_EPS) + ADAM_WD * w)
    return delta, m2, v2


def _reduce_adamw(recv, w, m, v, name):
    _, R, C = recv.shape
    tc = _tile(C, 256)

    def body(r_ref, w_ref, m_ref, v_ref, o_ref):
        g = r_ref[0].astype(F32)
        for s in range(1, N_DEV):
            g = g + r_ref[s].astype(F32)
        delta, m2, v2 = _adamw(w_ref[...], g, m_ref[...], v_ref[...])
        o_ref[0] = g
        o_ref[1] = delta
        o_ref[2] = m2
        o_ref[3] = v2

    col = pl.BlockSpec((R, tc), lambda j: (0, j))
    return pl.pallas_call(
        body, name=name, grid=(C // tc,),
        in_specs=[pl.BlockSpec((N_DEV, R, tc), lambda j: (0, 0, j)), col, col, col],
        out_specs=pl.BlockSpec((4, R, tc), lambda j: (0, 0, j)),
        out_shape=_sds((4, R, C)),
        compiler_params=_cparams(("parallel",)),
    )(recv, w, m, v)


def _adamw_call(g, w, m, v, name):
    def body(g_ref, w_ref, m_ref, v_ref, o_ref):
        delta, m2, v2 = _adamw(w_ref[...], g_ref[...], m_ref[...], v_ref[...])
        o_ref[0] = delta
        o_ref[1] = m2
        o_ref[2] = v2

    return pl.pallas_call(body, name=name, out_shape=_sds((3,) + g.shape))(g, w, m, v)


ADA_N = 6 * D // N_CHIP
KPAD = 128


def _mod_part(c8, w_ada, b_ada):
    tn = 512

    def body(c_ref, w_ref, b_ref, o_ref):
        o_ref[...] = _raw1(_silu(c_ref[...]), w_ref[...], _NN) + b_ref[...]

    return pl.pallas_call(
        body, name="ada_mod", grid=(ADA_N // tn,),
        in_specs=[pl.BlockSpec((16, D), lambda j: (0, 0)), pl.BlockSpec((D, tn), lambda j: (0, j)),
                  pl.BlockSpec((1, tn), lambda j: (0, j))],
        out_specs=pl.BlockSpec((16, tn), lambda j: (0, j)),
        out_shape=_sds((16, ADA_N)),
        compiler_params=_cparams(("parallel",)),
    )(c8, w_ada, b_ada)


def _w_ada_update(c8p, dm, w, m, v):
    tr = 256

    def body(c_ref, dm_ref, w_ref, m_ref, v_ref, g_ref, d_ref, m2_ref, v2_ref):
        g = _raw1(_silu(c_ref[...]), dm_ref[...], _TN)
        delta, m2, v2 = _adamw(w_ref[...], g, m_ref[...], v_ref[...])
        g_ref[...] = g
        d_ref[...] = delta
        m2_ref[...] = m2
        v2_ref[...] = v2

    blk = pl.BlockSpec((tr, ADA_N), lambda i: (i, 0))
    return pl.pallas_call(
        body, name="w_ada_update", grid=(D // tr,),
        in_specs=[pl.BlockSpec((KPAD, tr), lambda i: (0, i)), pl.BlockSpec((KPAD, ADA_N), lambda i: (0, 0)),
                  blk, blk, blk],
        out_specs=[blk] * 4, out_shape=[_sds((D, ADA_N))] * 4,
        compiler_params=_cparams(("parallel",)),
    )(c8p, dm, w, m, v)


def _me():
    return lax.axis_index("x"), lax.axis_index("y"), lax.axis_index("c")


def _peer(k, me):
    mx, my, mc = me
    return (1 - mx if k & 4 else mx, 1 - my if k & 2 else my, 1 - mc if k & 1 else mc)


def _lin(p):
    return 4 * p[0] + 2 * p[1] + p[2]


def _remote(src, dst, ssem, rsem, dev):
    return pltpu.make_async_remote_copy(src_ref=src, dst_ref=dst, send_sem=ssem, recv_sem=rsem,
                                        device_id=dev, device_id_type=MESH)


def _all_gather8(x, name):
    def body(x_ref, out_ref, send_sems, recv_sems):
        me = _me()
        out_ref[_lin(me)] = x_ref[...]
        sends = []
        for k in range(1, N_DEV):
            cp = _remote(x_ref, out_ref.at[_lin(me)], send_sems.at[k - 1], recv_sems.at[k - 1], _peer(k, me))
            cp.start()
            sends.append(cp)
        for k in range(1, N_DEV):
            p = _peer(k, me)
            _remote(x_ref, out_ref.at[_lin(p)], send_sems.at[k - 1], recv_sems.at[k - 1], p).wait_recv()
        for cp in sends:
            cp.wait_send()

    return pl.pallas_call(
        body, name=name,
        out_shape=_sds((N_DEV,) + x.shape, x.dtype),
        in_specs=[pl.BlockSpec(memory_space=pltpu.VMEM)],
        out_specs=pl.BlockSpec(memory_space=pltpu.VMEM),
        scratch_shapes=[pltpu.SemaphoreType.DMA((N_DEV - 1,)), pltpu.SemaphoreType.DMA((N_DEV - 1,))],
    )(x)


def _hbm_specs(n):
    return [pl.BlockSpec(memory_space=pl.ANY)] * n


def _gather_weights(shards):
    n = len(shards)

    def body(*refs):
        plan = _gather_plan(refs[:n], refs[n:2 * n], *refs[2 * n:])
        _start(plan)
        _finish(plan)

    return pl.pallas_call(
        body, name="gather_weights",
        out_shape=_gather_shapes(shards), in_specs=_hbm_specs(n), out_specs=_hbm_specs(n),
        scratch_shapes=_gather_sems(n),
    )(*shards)


def _gather_shapes(shards):
    return [_sds((N_CHIP,) + s.shape, s.dtype) for s in shards]


def _gather_sems(n):
    return [pltpu.SemaphoreType.DMA((3 * n,)), pltpu.SemaphoreType.DMA((3 * n,)), pltpu.SemaphoreType.DMA((n,))]


def _gather_plan(ins, outs, send_sems, recv_sems, local_sems):
    mx, my, mc = _me()
    chips = [(1 - mx, my), (mx, 1 - my), (1 - mx, 1 - my)]
    local, sends, recvs = [], [], []
    for a in range(len(ins)):
        local.append(pltpu.make_async_copy(ins[a], outs[a].at[2 * mx + my], local_sems.at[a]))
        for k, (px, py) in enumerate(chips):
            sems = (send_sems.at[3 * a + k], recv_sems.at[3 * a + k], (px, py, mc))
            sends.append(_remote(ins[a], outs[a].at[2 * mx + my], *sems))
            recvs.append(_remote(ins[a], outs[a].at[2 * px + py], *sems))
    return local, sends, recvs


def _start(plan):
    local, sends, _ = plan
    for cp in local + sends:
        cp.start()


def _finish(plan):
    local, sends, recvs = plan
    for cp in recvs:
        cp.wait_recv()
    for cp in sends:
        cp.wait_send()
    for cp in local:
        cp.wait()


def _grad_exchange(pieces):
    n = len(pieces)

    def body(*refs):
        plan = _exchange_plan(refs[:n], refs[n:2 * n], *refs[2 * n:])
        _start(plan)
        _finish(plan)

    return pl.pallas_call(
        body, name="grad_exchange",
        out_shape=_exchange_shapes(pieces), in_specs=_hbm_specs(n), out_specs=_hbm_specs(n),
        scratch_shapes=_exchange_sems(n),
    )(*pieces)


def _exchange_shapes(pieces):
    return [_sds((N_DEV,) + p.shape[2:], p.dtype) for p in pieces]


def _exchange_sems(n):
    return [pltpu.SemaphoreType.DMA(((N_DEV - 1) * n,)), pltpu.SemaphoreType.DMA(((N_DEV - 1) * n,)),
            pltpu.SemaphoreType.DMA((n,))]


def _exchange_plan(ins, outs, send_sems, recv_sems, local_sems):
    me = _me()
    mx, my, mc = me
    local, sends, recvs = [], [], []
    for a in range(len(ins)):
        local.append(pltpu.make_async_copy(ins[a].at[2 * mx + my, mc], outs[a].at[_lin(me)], local_sems.at[a]))
        for k in range(1, N_DEV):
            p = _peer(k, me)
            s = (N_DEV - 1) * a + k - 1
            sends.append(_remote(ins[a].at[2 * p[0] + p[1], p[2]], outs[a].at[_lin(me)], send_sems.at[s],
                                 recv_sems.at[s], p))
            recvs.append(_remote(ins[a].at[2 * mx + my, mc], outs[a].at[_lin(p)], send_sems.at[s],
                                 recv_sems.at[s], p))
    return local, sends, recvs


def _reduce_swap(recv, name):
    _, rows, cols = recv.shape

    def body(r_ref, o_ref, send_sem, recv_sem):
        mx, my, mc = _me()
        sib = (mx, my, 1 - mc)
        g = r_ref[0].astype(F32)
        for s in range(1, N_DEV):
            g = g + r_ref[s].astype(F32)
        mine = o_ref.at[pl.ds(pl.multiple_of(mc * rows, 8), rows)]
        theirs = o_ref.at[pl.ds(pl.multiple_of((1 - mc) * rows, 8), rows)]
        mine[...] = g
        cp = _remote(mine, mine, send_sem, recv_sem, sib)
        cp.start()
        _remote(mine, theirs, send_sem, recv_sem, sib).wait_recv()
        cp.wait_send()

    return pl.pallas_call(
        body, name=name, out_shape=_sds((2 * rows, cols)),
        in_specs=[pl.BlockSpec(memory_space=pltpu.VMEM)], out_specs=pl.BlockSpec(memory_space=pltpu.VMEM),
        scratch_shapes=[pltpu.SemaphoreType.DMA, pltpu.SemaphoreType.DMA],
        compiler_params=_cparams(),
    )(recv)


def _adamw_big(g, w, m, v, name):
    rows, cols = g.shape
    tr = next((t for t in (256, 176, 128, 64, 8) if rows % t == 0), None)
    if tr is None:
        tc = _tile(cols, 256)
        blk, grid = pl.BlockSpec((rows, tc), lambda i: (0, i)), (cols // tc,)
    else:
        blk, grid = pl.BlockSpec((tr, cols), lambda i: (i, 0)), (rows // tr,)

    def body(g_ref, w_ref, m_ref, v_ref, go_ref, d_ref, m2_ref, v2_ref):
        g = g_ref[...]
        delta, m2, v2 = _adamw(w_ref[...], g, m_ref[...], v_ref[...])
        go_ref[...] = g
        d_ref[...] = delta
        m2_ref[...] = m2
        v2_ref[...] = v2

    return pl.pallas_call(
        body, name=name, grid=grid,
        in_specs=[blk] * 4, out_specs=[blk] * 4, out_shape=[_sds((rows, cols))] * 4,
        compiler_params=_cparams(("parallel",)),
    )(g, w, m, v)


SMALL_ORDER = (("mod", 6 * D), ("norm1_w", D), ("norm2_w", D), ("conv_w", CONVW * 3 * GW), ("a_log", GH),
               ("dt_bias", GH), ("gdn_norm_w", HD), ("q_norm_w", HD), ("k_norm_w", HD), ("sinks", SQH), ("loss", 1))
SMALL_R = 120


def _pack_small(d):
    parts = [d[k].reshape(-1).astype(F32) if k in d else jnp.zeros((n,), F32) for k, n in SMALL_ORDER]
    used = sum(n for _, n in SMALL_ORDER)
    parts.append(jnp.zeros((SMALL_R * LANE - used,), F32))
    return jnp.concatenate(parts).reshape(SMALL_R, LANE)


def _unpack_small(pk):
    flat = pk.reshape(-1)
    out, r = {}, 0
    for k, n in SMALL_ORDER:
        out[k] = flat[r:r + n]
        r += n
    return out


def kernel(x, c, w_ada, b_ada, norm1_w, w_in, conv_w, a_log, dt_bias, gdn_norm_w, q_norm_w, k_norm_w, sinks, w_out, norm2_w, w_gate, w_up, w_down, loss_target, m_w_ada, m_b_ada, m_norm1_w, m_w_in, m_conv_w, m_a_log, m_dt_bias, m_gdn_norm_w, m_q_norm_w, m_k_norm_w, m_sinks, m_w_out, m_norm2_w, m_w_gate, m_w_up, m_w_down, v_w_ada, v_b_ada, v_norm1_w, v_w_in, v_conv_w, v_a_log, v_dt_bias, v_gdn_norm_w, v_q_norm_w, v_k_norm_w, v_sinks, v_w_out, v_norm2_w, v_w_gate, v_w_up, v_w_down):
    mx, my, mc = _me()
    chip = 2 * mx + my
    dev = 4 * mx + 2 * my + mc
    T = x.shape[1]

    conv_sh = conv_w.reshape(CONVW, 3 * GW // N_CHIP)
    mine = jnp.concatenate([c.reshape(-1), conv_sh.reshape(-1), jnp.zeros((4 * LANE,), F32)]).reshape(24, LANE)
    got = _all_gather8(mine, "gather_c_conv")
    c8 = got[:, :8].reshape(N_DEV, D)
    conv_full = jnp.concatenate([got[2 * j, 8:20].reshape(CONVW, 3 * GW // N_CHIP) for j in range(N_CHIP)], axis=1)
    c16 = jnp.concatenate([c8, jnp.zeros((8, D), F32)], axis=0)
    b_sh = lax.dynamic_slice(b_ada, (0, chip * ADA_N), (1, ADA_N))
    mods = _all_gather8(_mod_part(c16, w_ada[0], b_sh), "gather_mod")
    mod = jnp.concatenate([lax.dynamic_slice(mods[2 * j], (dev, 0), (1, ADA_N)) for j in range(N_CHIP)], axis=1)

    as_rows = lambda t, transposed: t[0].T if transposed else t[0]
    transposed = (True, False, True, True, False)
    big_w = [as_rows(t, tr) for t, tr in zip((w_in, w_out, w_gate, w_up, w_down), transposed)]
    shards = [t.astype(BF16) for t in big_w]
    (a_in,) = _gather_weights(shards[:1])
    w_in_pt = _permute_w_in_t(a_in.reshape(PROJ, D))

    loss, grad_x, big, small = _local_step(
        x[0], loss_target[0], mod, norm1_w, w_in_pt, conv_full, a_log, dt_bias, gdn_norm_w,
        q_norm_w, k_norm_w, sinks, norm2_w, shards[1:])

    small["loss"] = loss[:, :1]
    sg = _all_gather8(_pack_small(small), "gather_small_grads")
    rep = dict(mod=(b_ada, m_b_ada, v_b_ada), norm1_w=(norm1_w, m_norm1_w, v_norm1_w),
               norm2_w=(norm2_w, m_norm2_w, v_norm2_w), a_log=(a_log, m_a_log, v_a_log),
               dt_bias=(dt_bias, m_dt_bias, v_dt_bias), gdn_norm_w=(gdn_norm_w, m_gdn_norm_w, v_gdn_norm_w),
               q_norm_w=(q_norm_w, m_q_norm_w, v_q_norm_w), k_norm_w=(k_norm_w, m_k_norm_w, v_k_norm_w),
               sinks=(sinks, m_sinks, v_sinks))
    wmv = [_pack_small({k: t[i] for k, t in rep.items()}) for i in range(3)]
    sres = _reduce_adamw(sg, wmv[0], wmv[1], wmv[2], "small_reduce_adamw")
    s_g, s_d, s_m, s_v = [_unpack_small(sres[i]) for i in range(4)]
    loss_out = s_g["loss"][0]

    g_conv = lax.dynamic_slice(s_g["conv_w"].reshape(CONVW, 3 * GW), (0, chip * (3 * GW // N_CHIP)),
                               (CONVW, 3 * GW // N_CHIP))
    pad16 = lambda t: jnp.concatenate([t.reshape(12, LANE), jnp.zeros((4, LANE), F32)], axis=0)
    cres = _adamw_call(pad16(g_conv), pad16(conv_w), pad16(m_conv_w), pad16(v_conv_w), "conv_adamw")
    conv_out = [g_conv.reshape(conv_w.shape)] + [cres[i, :12].reshape(conv_w.shape) for i in range(3)]

    dmod8 = sg[:, :6 * D // LANE].reshape(N_DEV, 6 * D)
    dm = lax.dynamic_slice(dmod8, (0, chip * ADA_N), (N_DEV, ADA_N))
    zpad = lambda t: jnp.concatenate([t, jnp.zeros((KPAD - N_DEV, t.shape[1]), F32)], axis=0)
    ares = _w_ada_update(zpad(c8), zpad(dm), w_ada[0], m_w_ada[0], v_w_ada[0])

    names = ("w_in", "w_out", "w_gate", "w_up", "w_down")
    g_full = [_reduce_swap(r, "reduce_" + nm) for r, nm in zip(big, names)]
    g_full[0] = g_full[0][:W_IN_ROWS]
    big_m = [as_rows(t, tr) for t, tr in zip((m_w_in, m_w_out, m_w_gate, m_w_up, m_w_down), transposed)]
    big_v = [as_rows(t, tr) for t, tr in zip((v_w_in, v_w_out, v_w_gate, v_w_up, v_w_down), transposed)]
    upd = [_adamw_big(g, w, m, v, "adamw_" + nm) for g, w, m, v, nm in zip(g_full, big_w, big_m, big_v, names)]
    back = lambda t, tr: (t.T if tr else t)[None]
    bg, bd, bm, bv = [[back(u[i], tr) for u, tr in zip(upd, transposed)] for i in range(4)]

    def group(a_i, small_d, conv_i, big_l):
        s = lambda k, ref: small_d[k].reshape(ref.shape)
        return [ares[a_i][None], s("mod", b_ada), s("norm1_w", norm1_w), big_l[0], conv_out[conv_i],
                s("a_log", a_log), s("dt_bias", dt_bias), s("gdn_norm_w", gdn_norm_w), s("q_norm_w", q_norm_w),
                s("k_norm_w", k_norm_w), s("sinks", sinks), big_l[1], s("norm2_w", norm2_w), big_l[2], big_l[3],
                big_l[4]]

    outs = [loss_out, grad_x[None]]
    outs += group(0, s_g, 0, bg) + group(1, s_d, 1, bd) + group(2, s_m, 2, bm) + group(3, s_v, 3, bv)
    return tuple(outs)
```

```python
import jax
import jax.numpy as jnp
from jax import lax
from jax.experimental import pallas as pl
from jax.experimental.pallas import tpu as pltpu

F32 = jnp.float32
BF16 = jnp.bfloat16
MESH = pl.DeviceIdType.MESH

D = 1024
HD = 64
GH = 8
GW = GH * HD
SQH = 8
SKVH = 2
SGRP = SQH // SKVH
WIN = 128
CONVW = 4
CHUNK = 64
DFF = 2816
PROJ = 2832
NP = 3072
EPS = 1e-6
N_DEV = 8
N_CHIP = 4

ADAM_LR = 0.001
ADAM_B1 = 0.9
ADAM_B2 = 0.999
ADAM_EPS = 1e-08
ADAM_WD = 0.01
ADAM_STEP = 10

VMEM_LIMIT = 48 * 1024 * 1024
GDN_BWD_VMEM = 58 * 1024 * 1024
LANE = 128


def _cparams(sem=None, vmem=VMEM_LIMIT):
    return pltpu.CompilerParams(dimension_semantics=sem, vmem_limit_bytes=vmem)


_NN = ((1,), (0,))
_NT = ((1,), (1,))
_TN = ((0,), (0,))


def _dot(a, b, dims):
    if a.ndim == 3:
        (ca,), (cb,) = dims
        return lax.dot_general(a, b, (((ca + 1,), (cb + 1,)), ((0,), (0,))), preferred_element_type=F32)
    return lax.dot_general(a, b, (dims, ((), ())), preferred_element_type=F32)


def _raw1(a, b, dims):
    return _dot(a.astype(BF16), b.astype(BF16), dims)


def _raw3(a, b, dims):
    ah = a.astype(BF16)
    al = (a - ah.astype(F32)).astype(BF16)
    bh = b.astype(BF16)
    bl = (b - bh.astype(F32)).astype(BF16)
    return _dot(ah, bh, dims) + (_dot(al, bh, dims) + _dot(ah, bl, dims))


def _make_diff_mm(raw):
    @jax.custom_vjp
    def nn(a, b):
        return raw(a, b, _NN)

    @jax.custom_vjp
    def nt(a, b):
        return raw(a, b, _NT)

    @jax.custom_vjp
    def tn(a, b):
        return raw(a, b, _TN)

    nn.defvjp(lambda a, b: (raw(a, b, _NN), (a, b)), lambda r, g: (nt(g, r[1]), tn(r[0], g)))
    nt.defvjp(lambda a, b: (raw(a, b, _NT), (a, b)), lambda r, g: (nn(g, r[1]), tn(g, r[0])))
    tn.defvjp(lambda a, b: (raw(a, b, _TN), (a, b)), lambda r, g: (nt(r[1], g), nn(r[0], g)))
    return nn, nt, tn


def _tri_inv_raw(a, nn3):
    n = a.shape[-1]
    ri = lax.broadcasted_iota(jnp.int32, (n, n), 0)
    ci = lax.broadcasted_iota(jnp.int32, (n, n), 1)
    t = (ri == ci).astype(F32)
    for lvl in range((n - 1).bit_length()):
        same_pair = (ri >> (lvl + 1)) == (ci >> (lvl + 1))
        lower_left = (((ri >> lvl) & 1) == 1) & (((ci >> lvl) & 1) == 0)
        y = jnp.where(same_pair & lower_left, a, 0.0)
        t = t - y if lvl == 0 else t - nn3(nn3(t, y), t)
    return t


class _Kit:
    def __init__(self, diff):
        if diff:
            self.nn, self.nt, self.tn = _make_diff_mm(_raw1)
            self.nn3, self.nt3, self.tn3 = _make_diff_mm(_raw3)
            nn3, nt3, tn3 = self.nn3, self.nt3, self.tn3

            @jax.custom_vjp
            def inv(a, t):
                return t

            def inv_fwd(a, t):
                return t, t

            def inv_bwd(t, g):
                return -tn3(t, nt3(g, t)), jnp.zeros_like(t)

            inv.defvjp(inv_fwd, inv_bwd)
            self.inv = inv
        else:
            self.nn = lambda a, b: _raw1(a, b, _NN)
            self.nt = lambda a, b: _raw1(a, b, _NT)
            self.tn = lambda a, b: _raw1(a, b, _TN)
            self.nn3 = lambda a, b: _raw3(a, b, _NN)
            self.nt3 = lambda a, b: _raw3(a, b, _NT)
            self.tn3 = lambda a, b: _raw3(a, b, _TN)
            self.inv = lambda a, t: _tri_inv_raw(a, self.nn3) if t is None else t


def _sigmoid(x):
    return 1.0 / (1.0 + jnp.exp(-x))


def _silu(x):
    return x * _sigmoid(x)


def _rms(x, w):
    return x * lax.rsqrt(jnp.mean(x * x, axis=-1, keepdims=True) + EPS) * w


def _tile(dim, target):
    t = (min(dim, target) // LANE) * LANE
    while t >= LANE:
        if dim % t == 0:
            return t
        t -= LANE
    return dim


MM_TM, MM_TN, MM_TK = 1408, 1536, 1408


def _matmul(a, b, ta=False, tb=False, out_dtype=F32, name="matmul", gather=None, exchange=None):
    carried = gather if gather is not None else exchange if exchange is not None else []
    nc = len(carried)
    if ta:
        K, M = a.shape
    else:
        M, K = a.shape
    if tb:
        N, K2 = b.shape
    else:
        K2, N = b.shape
    assert K == K2, (a.shape, b.shape, ta, tb)
    tm, tn, tk = _tile(M, MM_TM), _tile(N, MM_TN), _tile(K, MM_TK)
    nk = K // tk
    dims = ((0,) if ta else (1,), (1,) if tb else (0,))

    grid = (M // tm, N // tn, nk)

    def body(*refs):
        a_ref, b_ref = refs[:2]
        o_ref = refs[2 + nc]
        scratch = refs[3 + 2 * nc:]
        k = pl.program_id(2)
        if nc:
            make_plan = _gather_plan if gather is not None else _exchange_plan
            plan = make_plan(refs[2:2 + nc], refs[3 + nc:3 + 2 * nc], *scratch[-3:])
            at = lambda pos: ((pl.program_id(0) == pos[0]) & (pl.program_id(1) == pos[1]) & (k == pos[2]))

            @pl.when(at((0, 0, 0)))
            def _():
                _start(plan)

        part = _dot(a_ref[...].astype(BF16), b_ref[...].astype(BF16), dims)
        if nk == 1:
            o_ref[...] = part.astype(o_ref.dtype)
        else:
            acc_ref = scratch[0]

            @pl.when(k == 0)
            def _():
                acc_ref[...] = part

            @pl.when((k > 0) & (k < nk - 1))
            def _():
                acc_ref[...] += part

            @pl.when(k == nk - 1)
            def _():
                o_ref[...] = (acc_ref[...] + part).astype(o_ref.dtype)

        if nc:
            @pl.when(at((grid[0] - 1, grid[1] - 1, nk - 1)))
            def _():
                _finish(plan)

    a_spec = (pl.BlockSpec((tk, tm), lambda i, j, k: (k, i)) if ta
              else pl.BlockSpec((tm, tk), lambda i, j, k: (i, k)))
    b_spec = (pl.BlockSpec((tn, tk), lambda i, j, k: (j, k)) if tb
              else pl.BlockSpec((tk, tn), lambda i, j, k: (k, j)))
    if gather is not None:
        c_shapes, c_sems = _gather_shapes(carried), _gather_sems(nc)
    elif exchange is not None:
        c_shapes, c_sems = _exchange_shapes(carried), _exchange_sems(nc)
    else:
        c_shapes, c_sems = [], []
    res = pl.pallas_call(
        body, name=name, grid=grid,
        in_specs=[a_spec, b_spec] + _hbm_specs(nc),
        out_specs=[pl.BlockSpec((tm, tn), lambda i, j, k: (i, j))] + _hbm_specs(nc),
        out_shape=[jax.ShapeDtypeStruct((M, N), out_dtype)] + c_shapes,
        scratch_shapes=([pltpu.VMEM((tm, tn), F32)] if nk > 1 else []) + c_sems,
        compiler_params=_cparams(("arbitrary",) * 3 if nc else ("parallel", "parallel", "arbitrary")),
    )(a, b, *carried)
    return (res[0], res[1:]) if nc else res[0]


def _rowcall(fn, tiled, consts, out_tiled, out_acc, tm, name):
    T = tiled[0].shape[0]
    n_in = len(tiled) + len(consts)
    n_o = len(out_tiled)

    def body(*refs):
        vals = [r[...] for r in refs[:n_in]]
        outs = refs[n_in:]
        res = fn(*vals)
        for r, v in zip(outs[:n_o], res[:n_o]):
            r[...] = v.astype(r.dtype)
        if len(outs) > n_o:
            @pl.when(pl.program_id(0) == 0)
            def _():
                for r in outs[n_o:]:
                    r[...] = jnp.zeros_like(r)

            for r, v in zip(outs[n_o:], res[n_o:]):
                r[...] += v

    in_specs = [pl.BlockSpec((tm, a.shape[1]), lambda i: (i, 0)) for a in tiled]
    in_specs += [pl.BlockSpec(a.shape, lambda i, nd=a.ndim: (0,) * nd) for a in consts]
    out_specs = [pl.BlockSpec((tm, s.shape[1]), lambda i: (i, 0)) for s in out_tiled]
    out_specs += [pl.BlockSpec(s.shape, lambda i: (0, 0)) for s in out_acc]
    return pl.pallas_call(
        body, name=name, grid=(T // tm,),
        in_specs=in_specs, out_specs=out_specs,
        out_shape=list(out_tiled) + list(out_acc),
        compiler_params=_cparams(("arbitrary",)),
    )(*tiled, *consts)


def _sds(shape, dtype=F32):
    return jax.ShapeDtypeStruct(shape, dtype)


def _norm_mod(x, nw, scale, shift):
    return _rms(x, nw) * (1.0 + scale) + shift


def _norm_mod_fwd(x, nw, scale, shift):
    T = x.shape[0]
    (h,) = _rowcall(lambda *a: (_norm_mod(*a),), [x], [nw, scale, shift],
                    [_sds((T, D), BF16)], [], 512, "norm1_fwd")
    return h


ROWS_TM = 512
ROWS_EPI = 256


def _matmul_rows(a, b, epi, tiled, consts, out_tiled, out_acc, name, pieces=()):
    T, K = a.shape
    tm, tk = _tile(T, ROWS_TM), _tile(K, MM_TK)
    nm, nk = T // tm, K // tk
    npc, nt, ncst, no, na = len(pieces), len(tiled), len(consts), len(out_tiled), len(out_acc)
    n_in = 2 + nt + ncst

    def body(*refs):
        a_ref, b_ref = refs[:2]
        t_refs, c_refs = refs[2:2 + nt], refs[2 + nt:n_in]
        o_refs = refs[n_in + npc:n_in + npc + no]
        acc_refs = refs[n_in + npc + no:n_in + npc + no + na]
        n_out = no + na + npc
        res_ref = refs[n_in + npc + n_out]
        plan = _exchange_plan(refs[n_in:n_in + npc], refs[n_in + npc + no + na:n_in + npc + n_out],
                              *refs[n_in + npc + n_out + 1:]) if npc else None
        i, k = pl.program_id(0), pl.program_id(1)

        @pl.when((i == 0) & (k == 0))
        def _():
            for r in acc_refs:
                r[...] = jnp.zeros_like(r)
            if npc:
                _start(plan)

        part = _dot(a_ref[...], b_ref[...], _NN)

        @pl.when(k == 0)
        def _():
            res_ref[...] = part

        @pl.when(k > 0)
        def _():
            res_ref[...] += part

        @pl.when(k == nk - 1)
        def _():
            for r0 in range(0, tm, ROWS_EPI):
                rows = pl.ds(r0, ROWS_EPI)
                outs = epi(res_ref[rows, :], *[r[rows, :] for r in t_refs], *[r[...] for r in c_refs])
                for r, v in zip(o_refs, outs[:no]):
                    r[rows, :] = v.astype(r.dtype)
                for r, v in zip(acc_refs, outs[no:]):
                    r[...] += v

        if npc:
            @pl.when((i == nm - 1) & (k == nk - 1))
            def _():
                _finish(plan)

    row = lambda w: pl.BlockSpec((tm, w), lambda i, k: (i, 0))
    whole = lambda s: pl.BlockSpec(s.shape, lambda i, k: (0, 0))
    res = pl.pallas_call(
        body, name=name, grid=(nm, nk),
        in_specs=[pl.BlockSpec((tm, tk), lambda i, k: (i, k)), pl.BlockSpec((tk, D), lambda i, k: (k, 0))]
                 + [row(t.shape[1]) for t in tiled] + [whole(c) for c in consts] + _hbm_specs(npc),
        out_specs=[row(s.shape[1]) for s in out_tiled] + [whole(s) for s in out_acc] + _hbm_specs(npc),
        out_shape=list(out_tiled) + list(out_acc) + (_exchange_shapes(pieces) if npc else []),
        scratch_shapes=[pltpu.VMEM((tm, D), F32)] + (_exchange_sems(npc) if npc else []),
        compiler_params=_cparams(("arbitrary", "arbitrary")),
    )(a, b, *tiled, *consts, *pieces)
    return res[:no + na], res[no + na:]


def _in_proj_dx_norm_bwd(dproj, w_in_pt, x, dres, nw, scale, shift, pieces):
    T = x.shape[0]

    def epi(dh, x, dres, nw, scale, shift):
        _, vjp = jax.vjp(_norm_mod, x, nw, scale, shift)
        dx, dnw, dsc, dsh = vjp(dh)
        return dx + dres, dnw, dsc, dsh

    return _matmul_rows(dproj, w_in_pt, epi, [x, dres], [nw, scale, shift], [_sds((T, D))], [_sds((1, D))] * 3,
                        "in_proj_dx_norm1_bwd", pieces)


def _out_proj_resid_norm(mixcat, w_out, x, gate1, nw, scale, shift):
    T = x.shape[0]

    def epi(mixed, x, gate1, nw, scale, shift):
        return (mixed,) + _resid_norm(x, mixed, gate1, nw, scale, shift)

    outs, _ = _matmul_rows(mixcat, w_out, epi, [x], [gate1, nw, scale, shift],
                           [_sds((T, D)), _sds((T, D)), _sds((T, D), BF16)], [], "out_proj_resid_norm2")
    return outs


def _ffn_up_dx_resid_bwd(dab, w_gut, x, mixed, dy, gate1, nw, scale, shift):
    T = x.shape[0]

    def epi(dh2, x, mixed, dy, gate1, nw, scale, shift):
        _, vjp = jax.vjp(_resid_norm, x, mixed, gate1, nw, scale, shift)
        return vjp((dy, dh2))

    outs, _ = _matmul_rows(dab, w_gut, epi, [x, mixed, dy], [gate1, nw, scale, shift],
                           [_sds((T, D)), _sds((T, D), BF16)], [_sds((1, D))] * 4, "ffn_up_dx_resid_norm2_bwd")
    return outs


def _ffn_down_loss(act, w_down, x1, target, gate2):
    T = x1.shape[0]

    def epi(ffn, x1, target, gate2):
        y = x1 + gate2 * ffn
        err = y - target
        loss = 0.5 * jnp.sum(jnp.sum(err * err, axis=1, keepdims=True), axis=0, keepdims=True) / D
        dy = err * (1.0 / D)
        return dy, gate2 * dy, jnp.sum(dy * ffn, axis=0, keepdims=True), jnp.broadcast_to(loss, (1, LANE))

    outs, _ = _matmul_rows(act, w_down, epi, [x1, target], [gate2], [_sds((T, D)), _sds((T, D), BF16)],
                           [_sds((1, D)), _sds((1, LANE))], "ffn_down_loss")
    return outs


def _resid_norm(x, mixed, gate1, nw, scale, shift):
    x1 = x + gate1 * mixed
    return x1, _norm_mod(x1, nw, scale, shift)


FFN_BLK = 256
FFN_TM = 2048


def _interleave_gate_up(gate_t, up_t):
    blocks = lambda t: t.reshape(DFF // FFN_BLK, 1, FFN_BLK, D)
    return jnp.concatenate([blocks(gate_t), blocks(up_t)], axis=1).reshape(2 * DFF, D)


def _split_gate_up(g):
    g = g.reshape(DFF // FFN_BLK, 2, FFN_BLK, D)
    return g[:, 0].reshape(DFF, D), g[:, 1].reshape(DFF, D)


def _ffn_up_act(h2, w_gut):
    T = h2.shape[0]
    tm = _tile(T, FFN_TM)

    def body(h_ref, w_ref, ab_ref, act_ref):
        ab = _dot(h_ref[...], w_ref[...], _NT)
        ab_ref[...] = ab
        act_ref[...] = (_silu(ab[:, :FFN_BLK]) * ab[:, FFN_BLK:]).astype(act_ref.dtype)

    return pl.pallas_call(
        body, name="ffn_up_act", grid=(T // tm, DFF // FFN_BLK),
        in_specs=[pl.BlockSpec((tm, D), lambda i, j: (i, 0)), pl.BlockSpec((2 * FFN_BLK, D), lambda i, j: (j, 0))],
        out_specs=[pl.BlockSpec((tm, 2 * FFN_BLK), lambda i, j: (i, j)), pl.BlockSpec((tm, FFN_BLK), lambda i, j: (i, j))],
        out_shape=[_sds((T, 2 * DFF)), _sds((T, DFF), BF16)],
        compiler_params=_cparams(("parallel", "parallel")),
    )(h2, w_gut)


def _ffn_down_dx_act(dffn, w_down, ab):
    T = dffn.shape[0]
    tm = _tile(T, FFN_TM)

    def body(d_ref, w_ref, ab_ref, o_ref):
        dact = _dot(d_ref[...], w_ref[...], _NT)
        a, b = ab_ref[:, :FFN_BLK], ab_ref[:, FFN_BLK:]
        s = _sigmoid(a)
        da = dact * b * (s * (1.0 + a * (1.0 - s)))
        db = dact * (a * s)
        o_ref[...] = jnp.concatenate([da, db], axis=1).astype(o_ref.dtype)

    return pl.pallas_call(
        body, name="ffn_down_dx_act", grid=(T // tm, DFF // FFN_BLK),
        in_specs=[pl.BlockSpec((tm, D), lambda i, j: (i, 0)), pl.BlockSpec((FFN_BLK, D), lambda i, j: (j, 0)),
                  pl.BlockSpec((tm, 2 * FFN_BLK), lambda i, j: (i, j))],
        out_specs=pl.BlockSpec((tm, 2 * FFN_BLK), lambda i, j: (i, j)),
        out_shape=_sds((T, 2 * DFF), BF16),
        compiler_params=_cparams(("parallel", "parallel")),
    )(dffn, w_down, ab)


def _round_bf16(x):
    return x.astype(BF16).astype(F32)


def _shift_down(x, s, rows):
    if s == 0:
        return x
    return jnp.where(rows >= s, pltpu.roll(x, s, 0), 0.0)


def _shift_up(x, s, rows, T):
    if s == 0:
        return x
    return jnp.where(rows < T - s, pltpu.roll(x, T - s, 0), 0.0)


def _conv_fwd(proj, conv_w):
    T = proj.shape[0]
    ncol = 3 * GW // LANE

    def body(x_ref, w_ref, o_ref):
        x = _round_bf16(x_ref[...])
        rows = lax.broadcasted_iota(jnp.int32, x.shape, 0)
        acc = jnp.zeros_like(x)
        for j in range(CONVW):
            acc = acc + _round_bf16(w_ref[pl.ds(j, 1), :]) * _shift_down(x, CONVW - 1 - j, rows)
        o_ref[0], o_ref[1] = _split_pair(_silu(acc))

    return pl.pallas_call(
        body, name="conv_fwd", grid=(ncol,),
        in_specs=[pl.BlockSpec((T, LANE), lambda j: (0, j)), pl.BlockSpec((CONVW, LANE), lambda j: (0, j))],
        out_specs=pl.BlockSpec((2, T, HD), lambda j: (j, 0, 0)),
        out_shape=_sds((3 * GH, T, HD)),
        compiler_params=_cparams(("parallel",)),
    )(proj, conv_w)


RELAYOUT_TM = 4096


def _split_pair(y):
    return y[:, :HD], pltpu.roll(y, HD, 1)[:, :HD]


def _merge_pair(a, b):
    return jnp.concatenate([a, b], axis=1)


def _split_heads(x, col_block0, nheads, name):
    T = x.shape[0]
    tm = _tile(T, RELAYOUT_TM)

    def body(x_ref, o_ref):
        a, b = _split_pair(x_ref[...])
        o_ref[0] = a
        o_ref[1] = b

    return pl.pallas_call(
        body, name=name, grid=(nheads // 2, T // tm),
        in_specs=[pl.BlockSpec((tm, LANE), lambda j, i: (i, col_block0 + j))],
        out_specs=pl.BlockSpec((2, tm, HD), lambda j, i: (j, i, 0)),
        out_shape=_sds((nheads, T, HD), x.dtype),
        compiler_params=_cparams(("parallel", "parallel")),
    )(x)


def _merge_heads(hm, out_dtype, name, into=None, col_block0=0, head0=0, nheads=None):
    T = hm.shape[1]
    nheads = hm.shape[0] if nheads is None else nheads
    tm = _tile(T, RELAYOUT_TM)

    def body(*refs):
        h_ref, o_ref = refs[0], refs[-1]
        o_ref[...] = _merge_pair(h_ref[0], h_ref[1]).astype(o_ref.dtype)

    in_specs = [pl.BlockSpec((2, tm, HD), lambda j, i: (head0 // 2 + j, i, 0))]
    args = [hm]
    if into is None:
        out_shape = _sds((T, HD * nheads), out_dtype)
        aliases = {}
    else:
        out_shape = _sds(into.shape, into.dtype)
        in_specs.append(pl.BlockSpec(memory_space=pl.ANY))
        args.append(into)
        aliases = {1: 0}
    return pl.pallas_call(
        body, name=name, grid=(nheads // 2, T // tm),
        in_specs=in_specs,
        out_specs=pl.BlockSpec((tm, LANE), lambda j, i: (i, col_block0 + j)),
        out_shape=out_shape, input_output_aliases=aliases,
        compiler_params=_cparams(("parallel", "parallel")),
    )(*args)


def _conv_bwd(proj, conv_w, dqc):
    T = proj.shape[0]
    ncol = 3 * GW // LANE

    def body(x_ref, w_ref, d_ref, dx_ref, dw_ref):
        x = _round_bf16(x_ref[...])
        rows = lax.broadcasted_iota(jnp.int32, x.shape, 0)
        xs = [_shift_down(x, CONVW - 1 - j, rows) for j in range(CONVW)]
        w = [_round_bf16(w_ref[pl.ds(j, 1), :]) for j in range(CONVW)]
        pre = jnp.zeros_like(x)
        for j in range(CONVW):
            pre = pre + w[j] * xs[j]
        s = _sigmoid(pre)
        dpre = _round_bf16(_merge_pair(d_ref[0], d_ref[1]) * (s * (1.0 + pre * (1.0 - s))))
        dx = jnp.zeros_like(x)
        for j in range(CONVW):
            dx = dx + w[j] * _shift_up(dpre, CONVW - 1 - j, rows, T)
            dw_ref[pl.ds(j, 1), :] = jnp.sum(dpre * xs[j], axis=0, keepdims=True)
        dx_ref[...] = dx.astype(dx_ref.dtype)

    return pl.pallas_call(
        body, name="conv_bwd", grid=(ncol,),
        in_specs=[pl.BlockSpec((T, LANE), lambda j: (0, j)), pl.BlockSpec((CONVW, LANE), lambda j: (0, j)),
                  pl.BlockSpec((2, T, HD), lambda j: (j, 0, 0))],
        out_specs=[pl.BlockSpec((T, LANE), lambda j: (0, j)), pl.BlockSpec((CONVW, LANE), lambda j: (0, j))],
        out_shape=[_sds((T, NP), BF16), _sds((CONVW, 3 * GW))],
        compiler_params=_cparams(("parallel",)),
    )(proj, conv_w, dqc)


def _gdn_prep(kit, q, k, v, ga, gb, alog, dtb, t_inv=None):
    C = CHUNK
    ri = lax.broadcasted_iota(jnp.int32, (C, C), 0)
    ci = lax.broadcasted_iota(jnp.int32, (C, C), 1)
    causal = ri >= ci
    strict = ri > ci
    eye = (ri == ci).astype(F32)
    lower = causal.astype(F32)
    upper = (ri <= ci).astype(F32)

    a = ga + dtb
    softplus = jnp.maximum(a, 0.0) + jnp.log(1.0 + jnp.exp(-jnp.abs(a)))
    g_row = -jnp.exp(alog) * softplus
    beta_row = _sigmoid(gb)
    g_col = jnp.sum(eye * g_row, axis=2, keepdims=True)
    beta_col = jnp.sum(eye * beta_row, axis=2, keepdims=True)
    G_col = jnp.sum(lower * g_row, axis=2, keepdims=True)
    G_row = jnp.sum(upper * g_col, axis=1, keepdims=True)
    G_last = jnp.sum(g_row, axis=2, keepdims=True)
    decay = jnp.exp(jnp.where(causal, G_col - G_row, -1e30))

    qn = q * lax.rsqrt(jnp.sum(q * q, axis=-1, keepdims=True) + EPS) * (HD ** -0.5)
    kn = k * lax.rsqrt(jnp.sum(k * k, axis=-1, keepdims=True) + EPS)
    kb = kn * beta_col
    A = jnp.where(strict, kit.nt(kb, kn) * decay, 0.0)
    Tm = kit.inv(A, t_inv)
    eG = jnp.exp(G_col)
    u = kit.nn3(Tm, v * beta_col)
    w = kit.nn3(Tm, kb * eG)
    qk = jnp.where(causal, kit.nt(qn, kn) * decay, 0.0)
    q_dec = qn * eG
    k_dec = kn * jnp.exp(G_last - G_col)
    dec = jnp.exp(G_last)
    return u, w, qk, q_dec, k_dec, dec, Tm


def _gdn_out(o, z, nw):
    return _rms(o, nw) * _silu(z)


GDN_CB = 4


def _gdn_specs(T, blk):
    TB = GDN_CB * CHUNK
    seq = lambda grp: pl.BlockSpec((GH, TB, HD), lambda i, grp=grp: (grp, blk(i), 0))
    row = lambda grp: pl.BlockSpec((GH, GDN_CB, 1, CHUNK), lambda i, grp=grp: (grp, blk(i), 0, 0))
    per_head = pl.BlockSpec((GH, 1, CHUNK), lambda i: (0, 0, 0))
    whole = pl.BlockSpec((1, HD), lambda i: (0, 0))
    state = pl.BlockSpec((GH, GDN_CB, HD, HD), lambda i: (0, blk(i), 0, 0))
    return seq, row, per_head, whole, state


def _gdn_load(seq_refs, row_refs, head_refs):
    chunks = lambda r: jnp.concatenate([r[:, pl.ds(cb * CHUNK, CHUNK), :] for cb in range(GDN_CB)], axis=0)
    rows = lambda r: jnp.concatenate([r[:, cb] for cb in range(GDN_CB)], axis=0)
    heads = lambda r: jnp.concatenate([r[...]] * GDN_CB, axis=0)
    return [chunks(r) for r in seq_refs], [rows(r) for r in row_refs], [heads(r) for r in head_refs]


def _gdn_fwd(qkv_hm, zs_hm, gab, alog_b, dtb_b, nw, shards):
    T = qkv_hm.shape[1]
    N = T // CHUNK
    nblk = N // GDN_CB
    ns = len(shards)
    seq, row, per_head, whole, state = _gdn_specs(T, lambda i: i)
    kit = _Kit(False)

    def body(*refs):
        q_ref, k_ref, v_ref, z_ref, ga_ref, gb_ref, al_ref, dt_ref, nw_ref = refs[:9]
        o_ref, S_ref, T_ref = refs[9 + ns:12 + ns]
        S_scr = refs[12 + 2 * ns]
        plan = _gather_plan(refs[9:9 + ns], refs[12 + ns:12 + 2 * ns], *refs[13 + 2 * ns:])

        @pl.when(pl.program_id(0) == 0)
        def _():
            S_scr[...] = jnp.zeros_like(S_scr)
            _start(plan)

        (q, k, v, z), (ga, gb), (al, dt) = _gdn_load((q_ref, k_ref, v_ref, z_ref), (ga_ref, gb_ref), (al_ref, dt_ref))
        u, w, qk, q_dec, k_dec, dec, t_inv = _gdn_prep(kit, q, k, v, ga, gb, al, dt)
        S = S_scr[...]
        for cb in range(GDN_CB):
            hs = slice(cb * GH, (cb + 1) * GH)
            S_ref[:, cb] = S
            T_ref[:, cb] = t_inv[hs]
            v_new = u[hs] - kit.nn(w[hs], S)
            o = kit.nn(q_dec[hs], S) + kit.nn(qk[hs], v_new)
            S = S * dec[hs] + kit.tn(k_dec[hs], v_new)
            o_ref[:, pl.ds(cb * CHUNK, CHUNK), :] = _gdn_out(o, z[hs], nw_ref[...])
        S_scr[...] = S

        @pl.when(pl.program_id(0) == nblk - 1)
        def _():
            _finish(plan)

    res = pl.pallas_call(
        body, name="gdn_fwd", grid=(nblk,),
        in_specs=[seq(0), seq(1), seq(2), seq(0), row(0), row(1), per_head, per_head, whole] + _hbm_specs(ns),
        out_specs=[seq(0), state, state] + _hbm_specs(ns),
        out_shape=[_sds((GH + SQH, T, HD)), _sds((GH, N, HD, HD)), _sds((GH, N, CHUNK, CHUNK))]
                  + _gather_shapes(shards),
        scratch_shapes=[pltpu.VMEM((GH, HD, HD), F32)] + _gather_sems(ns),
        compiler_params=_cparams(("arbitrary",)),
    )(qkv_hm, qkv_hm, qkv_hm, zs_hm, gab, gab, alog_b, dtb_b, nw, *shards)
    return res[0], (res[1], res[2]), res[3:]


def _gdn_bwd(qkv_hm, zs_hm, gab, alog_b, dtb_b, nw, S_all, do, pieces):
    T = qkv_hm.shape[1]
    N = T // CHUNK
    nblk = N // GDN_CB
    npc = len(pieces)
    dkit, kit = _Kit(True), _Kit(False)
    rseq, rrow, per_head, whole, rstate = _gdn_specs(T, lambda i: nblk - 1 - i)

    def body(*refs):
        q_ref, k_ref, v_ref, z_ref, ga_ref, gb_ref, al_ref, dt_ref, nw_ref, S_ref, T_ref, do_ref = refs[:12]
        dqkv_ref, dz_ref, dga_ref, dgb_ref, dal_ref, ddt_ref, dnw_ref = refs[12 + npc:19 + npc]
        dS_scr = refs[19 + 2 * npc]
        plan = _exchange_plan(refs[12:12 + npc], refs[19 + npc:19 + 2 * npc], *refs[20 + 2 * npc:])

        @pl.when(pl.program_id(0) == 0)
        def _():
            dS_scr[...] = jnp.zeros_like(dS_scr)
            dal_ref[...] = jnp.zeros_like(dal_ref)
            ddt_ref[...] = jnp.zeros_like(ddt_ref)
            dnw_ref[...] = jnp.zeros_like(dnw_ref)
            _start(plan)

        (q, k, v, z, dout), (ga, gb), (al, dt) = _gdn_load((q_ref, k_ref, v_ref, z_ref, do_ref), (ga_ref, gb_ref),
                                                          (al_ref, dt_ref))
        S_in = jnp.concatenate([S_ref[:, cb] for cb in range(GDN_CB)], axis=0)
        t_inv = jnp.concatenate([T_ref[:, cb] for cb in range(GDN_CB)], axis=0)
        prep = lambda *a: _gdn_prep(dkit, *a, t_inv=t_inv)[:6]
        (u, w, qk, q_dec, k_dec, dec), prep_vjp = jax.vjp(prep, q, k, v, ga, gb, al, dt)
        v_new = u - kit.nn(w, S_in)
        o = kit.nn(q_dec, S_in) + kit.nn(qk, v_new)
        _, out_vjp = jax.vjp(_gdn_out, o, z, nw_ref[...])
        do, dz, dnw = out_vjp(dout)
        dvn_part = kit.tn(qk, do)
        dS_part = kit.tn(q_dec, do)
        dS = dS_scr[...]
        dS_out, dvn = [None] * GDN_CB, [None] * GDN_CB
        for cb in reversed(range(GDN_CB)):
            hs = slice(cb * GH, (cb + 1) * GH)
            dS_out[cb] = dS
            dvn[cb] = dvn_part[hs] + kit.nn(k_dec[hs], dS)
            dS = dS * dec[hs] + dS_part[hs] - kit.tn(w[hs], dvn[cb])
        dS_scr[...] = dS
        dS_out = jnp.concatenate(dS_out, axis=0)
        dvn = jnp.concatenate(dvn, axis=0)
        ddec = jnp.sum(jnp.sum(S_in * dS_out, axis=2, keepdims=True), axis=1, keepdims=True)
        cts = (dvn, -kit.nt(dvn, S_in), kit.nt(do, v_new), kit.nt(do, S_in), kit.nt(v_new, dS_out), ddec)
        dq, dk, dv, dga, dgb, dal, ddt = prep_vjp(cts)
        lanesum = lambda t: jnp.broadcast_to(jnp.sum(t, axis=2, keepdims=True), t.shape)
        for cb in range(GDN_CB):
            hs = slice(cb * GH, (cb + 1) * GH)
            sl = pl.ds(cb * CHUNK, CHUNK)
            dqkv_ref[pl.ds(0, GH), sl, :] = dq[hs]
            dqkv_ref[pl.ds(GH, GH), sl, :] = dk[hs]
            dqkv_ref[pl.ds(2 * GH, GH), sl, :] = dv[hs]
            dz_ref[:, sl, :] = dz[hs]
            dga_ref[:, cb] = dga[hs]
            dgb_ref[:, cb] = dgb[hs]
            dal_ref[...] += lanesum(dal[hs])
            ddt_ref[...] += lanesum(ddt[hs])
        dnw_ref[...] += dnw

        @pl.when(pl.program_id(0) == nblk - 1)
        def _():
            _finish(plan)

    res = pl.pallas_call(
        body, name="gdn_bwd", grid=(nblk,),
        in_specs=[rseq(0), rseq(1), rseq(2), rseq(0), rrow(0), rrow(1), per_head, per_head, whole, rstate, rstate,
                  rseq(0)] + _hbm_specs(npc),
        out_specs=[pl.BlockSpec((3 * GH, GDN_CB * CHUNK, HD), lambda i: (0, nblk - 1 - i, 0)), rseq(0), rrow(0),
                   rrow(0), per_head, per_head, whole] + _hbm_specs(npc),
        out_shape=[_sds((3 * GH, T, HD)), _sds((GH + 4 + SWA_GRAD_HEADS, T, HD))] + [_sds((GH, N, 1, CHUNK))] * 2
                  + [_sds((GH, 1, CHUNK))] * 2 + [_sds((1, HD))] + _exchange_shapes(pieces),
        scratch_shapes=[pltpu.VMEM((GH, HD, HD), F32)] + _exchange_sems(npc),
        compiler_params=_cparams(("arbitrary",), GDN_BWD_VMEM),
    )(qkv_hm, qkv_hm, qkv_hm, zs_hm, gab, gab, alog_b, dtb_b, nw, S_all[0], S_all[1], do, *pieces)
    return res[:7], res[7:]


def _swa_heads(kit, first, q, kp, kc, vp, vc, qnw, knw, sink, slope):
    W = WIN
    ri = lax.broadcasted_iota(jnp.int32, (W, W), 0)
    ci = lax.broadcasted_iota(jnp.int32, (W, W), 1)
    mask_c = ri >= ci
    mask_p = ci > ri + first * W
    dist_c = (ri - ci).astype(F32)
    dist_p = (ri - ci + W).astype(F32)
    kpn = _rms(kp, knw)
    kcn = _rms(kc, knw)
    qn = _rms(q, qnw)
    sc = jnp.where(mask_c, kit.nt(qn, kcn) * (HD ** -0.5) - slope * dist_c, -1e30)
    sp = jnp.where(mask_p, kit.nt(qn, kpn) * (HD ** -0.5) - slope * dist_p, -1e30)
    m = jnp.maximum(jnp.maximum(jnp.max(sc, axis=-1, keepdims=True), jnp.max(sp, axis=-1, keepdims=True)), sink)
    m = lax.stop_gradient(m)
    pc = jnp.exp(sc - m)
    pp = jnp.exp(sp - m)
    den = jnp.sum(pc, axis=-1, keepdims=True) + jnp.sum(pp, axis=-1, keepdims=True) + jnp.exp(sink - m)
    inv = 1.0 / den
    return kit.nn(pc * inv, vc) + kit.nn(pp * inv, vp)


def _swa_grads(kit, first, q, kp, kc, vp, vc, qnw, knw, sink, slope, do):
    W = WIN
    ri = lax.broadcasted_iota(jnp.int32, (W, W), 0)
    ci = lax.broadcasted_iota(jnp.int32, (W, W), 1)
    mask_c = ri >= ci
    mask_p = ci > ri + first * W
    dist_c = (ri - ci).astype(F32)
    dist_p = (ri - ci + W).astype(F32)
    scale = HD ** -0.5
    kpn, kp_vjp = jax.vjp(_rms, kp, knw)
    kcn, kc_vjp = jax.vjp(_rms, kc, knw)
    qn, q_vjp = jax.vjp(_rms, q, qnw)
    sc = jnp.where(mask_c, kit.nt(qn, kcn) * scale - slope * dist_c, -1e30)
    sp = jnp.where(mask_p, kit.nt(qn, kpn) * scale - slope * dist_p, -1e30)
    m = jnp.maximum(jnp.maximum(jnp.max(sc, axis=-1, keepdims=True), jnp.max(sp, axis=-1, keepdims=True)), sink)
    ec = jnp.exp(sc - m)
    ep = jnp.exp(sp - m)
    es = jnp.exp(sink - m)
    inv = 1.0 / (jnp.sum(ec, axis=-1, keepdims=True) + jnp.sum(ep, axis=-1, keepdims=True) + es)
    pc, pp = ec * inv, ep * inv
    dpc, dpp = kit.nt(do, vc), kit.nt(do, vp)
    delta = jnp.sum(dpc * pc, axis=-1, keepdims=True) + jnp.sum(dpp * pp, axis=-1, keepdims=True)
    dsc = pc * (dpc - delta) * scale
    dsp = pp * (dpp - delta) * scale
    dq, dqnw = q_vjp(kit.nn(dsc, kcn) + kit.nn(dsp, kpn))
    dkc, dknw_c = kc_vjp(kit.tn(dsc, qn))
    dkp, dknw_p = kp_vjp(kit.tn(dsp, qn))
    return dq, dkp, dkc, kit.tn(pp, do), kit.tn(pc, do), dqnw, dknw_c + dknw_p, -(es * inv) * delta


def _per_query_head(kv_ref):
    return jnp.concatenate([kv_ref[pl.ds(h // SGRP, 1)] for h in range(SQH)], axis=0)


def _per_kv_head(d):
    return jnp.concatenate([jnp.sum(d[g * SGRP:(g + 1) * SGRP], axis=0, keepdims=True) for g in range(SKVH)], axis=0)


def _swa_specs(blk):
    qspec = pl.BlockSpec((SQH, WIN, HD), lambda i: (1, blk(i), 0))
    cur = lambda grp: pl.BlockSpec((SKVH, WIN, HD), lambda i, grp=grp: (grp, blk(i), 0))
    prev = lambda grp: pl.BlockSpec((SKVH, WIN, HD), lambda i, grp=grp: (grp, jnp.maximum(blk(i) - 1, 0), 0))
    whole = pl.BlockSpec((1, HD), lambda i: (0, 0))
    col = pl.BlockSpec((SQH, WIN, 1), lambda i: (0, 0, 0))
    ospec = pl.BlockSpec((SQH, WIN, HD), lambda i: (0, blk(i), 0))
    return qspec, cur, prev, whole, col, ospec


def _swa_fwd(zs_hm, qnw, knw, sinks_col, slopes_col, o_buf, shards):
    T = zs_hm.shape[1]
    NB = T // WIN
    ns = len(shards)
    kit = _Kit(False)
    qspec, cur, prev, whole, col, _ = _swa_specs(lambda i: i)

    def body(*refs):
        q_ref, kp_ref, kc_ref, vp_ref, vc_ref, qnw_ref, knw_ref, s_ref, sl_ref = refs[:9]
        o_ref = refs[10 + ns]
        plan = _gather_plan(refs[10:10 + ns], refs[11 + ns:11 + 2 * ns], *refs[11 + 2 * ns:])

        @pl.when(pl.program_id(0) == 0)
        def _():
            _start(plan)

        first = (pl.program_id(0) == 0).astype(jnp.int32)
        o_ref[...] = _swa_heads(kit, first, q_ref[...], _per_query_head(kp_ref), _per_query_head(kc_ref),
                                _per_query_head(vp_ref), _per_query_head(vc_ref), qnw_ref[...], knw_ref[...],
                                s_ref[...], sl_ref[...])

        @pl.when(pl.program_id(0) == NB - 1)
        def _():
            _finish(plan)

    res = pl.pallas_call(
        body, name="swa_fwd", grid=(NB,),
        in_specs=[qspec, prev(8), cur(8), prev(9), cur(9), whole, whole, col, col] + _hbm_specs(1 + ns),
        out_specs=[pl.BlockSpec((SQH, WIN, HD), lambda i: (1, i, 0))] + _hbm_specs(ns),
        out_shape=[_sds(o_buf.shape)] + _gather_shapes(shards),
        input_output_aliases={9: 0},
        scratch_shapes=_gather_sems(ns),
        compiler_params=_cparams(("arbitrary",)),
    )(zs_hm, zs_hm, zs_hm, zs_hm, zs_hm, qnw, knw, sinks_col, slopes_col, o_buf, *shards)
    return res[0], res[1:]


SWA_GRAD_HEADS = SQH + 2 * SKVH


def _swa_bwd(zs_hm, qnw, knw, sinks_col, slopes_col, dmix_hm, d_buf):
    T = zs_hm.shape[1]
    NB = T // WIN
    kit = _Kit(False)
    qspec, cur, prev, whole, col, _ = _swa_specs(lambda i: NB - 1 - i)

    def body(q_ref, kp_ref, kc_ref, vp_ref, vc_ref, qnw_ref, knw_ref, s_ref, sl_ref, do_ref, buf_ref,
             d_ref, dqnw_ref, dknw_ref, ds_ref, ck_scr, cv_scr):
        dq_ref = d_ref.at[pl.ds(0, SQH)]
        dk_ref = d_ref.at[pl.ds(SQH, SKVH)]
        dv_ref = d_ref.at[pl.ds(SQH + SKVH, SKVH)]
        i = pl.program_id(0)
        first = (i == NB - 1).astype(jnp.int32)

        @pl.when(i == 0)
        def _():
            ck_scr[...] = jnp.zeros_like(ck_scr)
            cv_scr[...] = jnp.zeros_like(cv_scr)
            ds_ref[...] = jnp.zeros_like(ds_ref)
            dqnw_ref[...] = jnp.zeros_like(dqnw_ref)
            dknw_ref[...] = jnp.zeros_like(dknw_ref)

        dq, dkp, dkc, dvp, dvc, dqnw, dknw, dsink = _swa_grads(
            kit, first, q_ref[...], _per_query_head(kp_ref), _per_query_head(kc_ref), _per_query_head(vp_ref),
            _per_query_head(vc_ref), qnw_ref[...], knw_ref[...], s_ref[...], sl_ref[...], do_ref[...])
        dq_ref[...] = dq
        dk_ref[...] = _per_kv_head(dkc) + ck_scr[...]
        dv_ref[...] = _per_kv_head(dvc) + cv_scr[...]
        ck_scr[...] = _per_kv_head(dkp)
        cv_scr[...] = _per_kv_head(dvp)
        dqnw_ref[...] += dqnw
        dknw_ref[...] += dknw
        ds_ref[...] += jnp.broadcast_to(jnp.sum(dsink, axis=1, keepdims=True), dsink.shape)

    dospec = pl.BlockSpec((SQH, WIN, HD), lambda i: (1, NB - 1 - i, 0))
    dspec = pl.BlockSpec((SWA_GRAD_HEADS, WIN, HD), lambda i: (1, NB - 1 - i, 0))
    res = pl.pallas_call(
        body, name="swa_bwd", grid=(NB,),
        in_specs=[qspec, prev(8), cur(8), prev(9), cur(9), whole, whole, col, col, dospec] + _hbm_specs(1),
        out_specs=[dspec, whole, whole, col],
        out_shape=[_sds(d_buf.shape), _sds((1, HD)), _sds((1, HD)), _sds((SQH, WIN, 1))],
        input_output_aliases={10: 0},
        scratch_shapes=[pltpu.VMEM((SKVH, WIN, HD), F32), pltpu.VMEM((SKVH, WIN, HD), F32)],
        compiler_params=_cparams(("arbitrary",)),
    )(zs_hm, zs_hm, zs_hm, zs_hm, zs_hm, qnw, knw, sinks_col, slopes_col, dmix_hm, d_buf)
    return res


GAB0 = 3 * GW + 1280


W_IN_ROWS = PROJ // N_CHIP
W_IN_ROWS_PAD = 736


def _permute_w_in_t(w_in_t):
    return jnp.concatenate([w_in_t[:4 * GW], w_in_t[4 * GW + 2 * GH:], w_in_t[4 * GW:4 * GW + 2 * GH],
                            jnp.zeros((NP - PROJ, D), w_in_t.dtype)], axis=0)


def _w_in_grad_pieces(g_t):
    g = jnp.concatenate([g_t[:4 * GW], g_t[GAB0:GAB0 + 2 * GH], g_t[4 * GW:GAB0]], axis=0)
    g = jnp.pad(g.reshape(N_CHIP, W_IN_ROWS, D), ((0, 0), (0, W_IN_ROWS_PAD - W_IN_ROWS), (0, 0)))
    return g.reshape(N_CHIP, 2, W_IN_ROWS_PAD // 2, D)


def _pieces_by_rows(g):
    return g.reshape(N_CHIP, 2, g.shape[0] // (2 * N_CHIP), D)


def _local_step(x, target, mod, n1w, w_in_pt, conv_w, alog, dtb, gnw, qnw, knw, sinks, n2w, shards):
    sh_out, sh_gate, sh_up, sh_down = shards
    T = x.shape[0]
    N = T // CHUNK
    shift1, scale1, gate1, shift2, scale2, gate2 = [mod[:, i * D:(i + 1) * D] for i in range(6)]

    h = _norm_mod_fwd(x, n1w, scale1, shift1)
    proj, (a_out,) = _matmul(h, w_in_pt, tb=True, name="in_proj", gather=[sh_out])
    w_out = a_out.reshape(D, D)
    qkv_hm = _conv_fwd(proj, conv_w)
    zs_hm = _split_heads(proj, 3 * GW // LANE, 20, "split_zs")
    gab = proj[:, GAB0:GAB0 + 2 * GH].T.reshape(2 * GH, N, 1, CHUNK)
    alog_b = jnp.broadcast_to(alog.reshape(GH, 1, 1), (GH, 1, CHUNK))
    dtb_b = jnp.broadcast_to(dtb.reshape(GH, 1, 1), (GH, 1, CHUNK))
    sinks_col = jnp.broadcast_to(sinks.reshape(SQH, 1, 1), (SQH, WIN, 1))
    o_hm, S_all, (a_gate, a_up) = _gdn_fwd(qkv_hm, zs_hm, gab, alog_b, dtb_b, gnw, [sh_gate, sh_up])
    w_gut = _interleave_gate_up(a_gate.reshape(DFF, D), a_up.reshape(DFF, D))
    slopes = 2.0 ** (-8.0 * (jnp.arange(SQH, dtype=F32) + 1.0) / SQH)
    slopes_col = jnp.broadcast_to(slopes.reshape(SQH, 1, 1), (SQH, WIN, 1))
    o_hm, (a_down,) = _swa_fwd(zs_hm, qnw, knw, sinks_col, slopes_col, o_hm, [sh_down])
    w_down = a_down.reshape(DFF, D)
    mixcat = _merge_heads(o_hm, BF16, "merge_mix")
    mixed, x1, h2 = _out_proj_resid_norm(mixcat, w_out, x, gate1, n2w, scale2, shift2)
    ab, act = _ffn_up_act(h2, w_gut)
    dy, dffn, dgate2, loss = _ffn_down_loss(act, w_down, x1, target, gate2)

    dab = _ffn_down_dx_act(dffn, w_down, ab)
    g_w_down = _matmul(act, dffn, ta=True, out_dtype=BF16, name="ffn_down_dw")
    g_w_gut = _matmul(dab, h2, ta=True, out_dtype=BF16, name="ffn_up_dw")
    dx1, dmixed, dgate1, dn2w, dscale2, dshift2 = _ffn_up_dx_resid_bwd(dab, w_gut, x, mixed, dy, gate1, n2w, scale2,
                                                                       shift2)
    g_w_out = _matmul(mixcat, dmixed, ta=True, out_dtype=BF16, name="out_proj_dw")
    dmix_hm = _split_heads(_matmul(dmixed, w_out, tb=True, name="out_proj_dx"), 0, GH + SQH, "split_dmix")
    g_gate_t, g_up_t = _split_gate_up(g_w_gut)
    pieces = [_pieces_by_rows(g_w_out), _pieces_by_rows(g_gate_t), _pieces_by_rows(g_up_t),
              _pieces_by_rows(g_w_down)]
    (dqkv_hm, d_hm, dga, dgb, dalog, ddtb, dgnw), recv = _gdn_bwd(qkv_hm, zs_hm, gab, alog_b, dtb_b, gnw, S_all,
                                                                  dmix_hm, pieces)
    d_hm, dqnw, dknw, dsinks = _swa_bwd(zs_hm, qnw, knw, sinks_col, slopes_col, dmix_hm, d_hm)
    dproj, dconv = _conv_bwd(proj, conv_w, dqkv_hm)
    dproj = _merge_heads(d_hm, BF16, "merge_dz", into=dproj, col_block0=3 * GW // LANE, head0=0, nheads=GH)
    dproj = _merge_heads(d_hm, BF16, "merge_dswa", into=dproj, col_block0=4 * GW // LANE, head0=GH + 4,
                         nheads=SWA_GRAD_HEADS)
    dgab = jnp.concatenate([dga, dgb], axis=0).reshape(2 * GH, T).T.astype(BF16)
    dproj = lax.dynamic_update_slice(dproj, jnp.concatenate([dgab, jnp.zeros((T, NP - PROJ), BF16)], axis=1),
                                     (0, GAB0))
    g_w_in_pt = _matmul(dproj, h, ta=True, out_dtype=BF16, name="in_proj_dw")
    (grad_x, dn1w, dscale1, dshift1), recv_in = _in_proj_dx_norm_bwd(dproj, w_in_pt, x, dx1, n1w, scale1, shift1,
                                                                     [_w_in_grad_pieces(g_w_in_pt)])

    dmod = jnp.concatenate([dshift1, dscale1, dgate1, dshift2, dscale2, dgate2], axis=1)
    big = list(recv_in) + list(recv)
    small = dict(mod=dmod, norm1_w=dn1w, norm2_w=dn2w, conv_w=dconv, a_log=dalog[:, 0, 0], dt_bias=ddtb[:, 0, 0],
                 gdn_norm_w=dgnw, q_norm_w=dqnw, k_norm_w=dknw, sinks=dsinks[:, 0, 0])
    return loss, grad_x, big, small


def _adamw(w, g, m, v):
    m2 = ADAM_B1 * m + (1.0 - ADAM_B1) * g
    v2 = ADAM_B2 * v + (1.0 - ADAM_B2) * (g * g)
    m_hat = m2 / (1.0 - ADAM_B1 ** ADAM_STEP)
    v_hat = v2 / (1.0 - ADAM_B2 ** ADAM_STEP)
    delta = -ADAM_LR * (m_hat / (jnp.sqrt(v_hat) + ADAM_EPS) + ADAM_WD * w)
    return delta, m2, v2


def _reduce_adamw(recv, w, m, v, name):
    _, R, C = recv.shape
    tc = _tile(C, 256)

    def body(r_ref, w_ref, m_ref, v_ref, o_ref):
        g = r_ref[0].astype(F32)
        for s in range(1, N_DEV):
            g = g + r_ref[s].astype(F32)
        delta, m2, v2 = _adamw(w_ref[...], g, m_ref[...], v_ref[...])
        o_ref[0] = g
        o_ref[1] = delta
        o_ref[2] = m2
        o_ref[3] = v2

    col = pl.BlockSpec((R, tc), lambda j: (0, j))
    return pl.pallas_call(
        body, name=name, grid=(C // tc,),
        in_specs=[pl.BlockSpec((N_DEV, R, tc), lambda j: (0, 0, j)), col, col, col],
        out_specs=pl.BlockSpec((4, R, tc), lambda j: (0, 0, j)),
        out_shape=_sds((4, R, C)),
        compiler_params=_cparams(("parallel",)),
    )(recv, w, m, v)


def _adamw_call(g, w, m, v, name):
    def body(g_ref, w_ref, m_ref, v_ref, o_ref):
        delta, m2, v2 = _adamw(w_ref[...], g_ref[...], m_ref[...], v_ref[...])
        o_ref[0] = delta
        o_ref[1] = m2
        o_ref[2] = v2

    return pl.pallas_call(body, name=name, out_shape=_sds((3,) + g.shape))(g, w, m, v)


ADA_N = 6 * D // N_CHIP
KPAD = 128


def _mod_part(c8, w_ada, b_ada):
    tn = 512

    def body(c_ref, w_ref, b_ref, o_ref):
        o_ref[...] = _raw1(_silu(c_ref[...]), w_ref[...], _NN) + b_ref[...]

    return pl.pallas_call(
        body, name="ada_mod", grid=(ADA_N // tn,),
        in_specs=[pl.BlockSpec((16, D), lambda j: (0, 0)), pl.BlockSpec((D, tn), lambda j: (0, j)),
                  pl.BlockSpec((1, tn), lambda j: (0, j))],
        out_specs=pl.BlockSpec((16, tn), lambda j: (0, j)),
        out_shape=_sds((16, ADA_N)),
        compiler_params=_cparams(("parallel",)),
    )(c8, w_ada, b_ada)


def _w_ada_update(c8p, dm, w, m, v):
    tr = 256

    def body(c_ref, dm_ref, w_ref, m_ref, v_ref, g_ref, d_ref, m2_ref, v2_ref):
        g = _raw1(_silu(c_ref[...]), dm_ref[...], _TN)
        delta, m2, v2 = _adamw(w_ref[...], g, m_ref[...], v_ref[...])
        g_ref[...] = g
        d_ref[...] = delta
        m2_ref[...] = m2
        v2_ref[...] = v2

    blk = pl.BlockSpec((tr, ADA_N), lambda i: (i, 0))
    return pl.pallas_call(
        body, name="w_ada_update", grid=(D // tr,),
        in_specs=[pl.BlockSpec((KPAD, tr), lambda i: (0, i)), pl.BlockSpec((KPAD, ADA_N), lambda i: (0, 0)),
                  blk, blk, blk],
        out_specs=[blk] * 4, out_shape=[_sds((D, ADA_N))] * 4,
        compiler_params=_cparams(("parallel",)),
    )(c8p, dm, w, m, v)


def _me():
    return lax.axis_index("x"), lax.axis_index("y"), lax.axis_index("c")


def _peer(k, me):
    mx, my, mc = me
    return (1 - mx if k & 4 else mx, 1 - my if k & 2 else my, 1 - mc if k & 1 else mc)


def _lin(p):
    return 4 * p[0] + 2 * p[1] + p[2]


def _remote(src, dst, ssem, rsem, dev):
    return pltpu.make_async_remote_copy(src_ref=src, dst_ref=dst, send_sem=ssem, recv_sem=rsem,
                                        device_id=dev, device_id_type=MESH)


def _all_gather8(x, name):
    def body(x_ref, out_ref, send_sems, recv_sems):
        me = _me()
        out_ref[_lin(me)] = x_ref[...]
        sends = []
        for k in range(1, N_DEV):
            cp = _remote(x_ref, out_ref.at[_lin(me)], send_sems.at[k - 1], recv_sems.at[k - 1], _peer(k, me))
            cp.start()
            sends.append(cp)
        for k in range(1, N_DEV):
            p = _peer(k, me)
            _remote(x_ref, out_ref.at[_lin(p)], send_sems.at[k - 1], recv_sems.at[k - 1], p).wait_recv()
        for cp in sends:
            cp.wait_send()

    return pl.pallas_call(
        body, name=name,
        out_shape=_sds((N_DEV,) + x.shape, x.dtype),
        in_specs=[pl.BlockSpec(memory_space=pltpu.VMEM)],
        out_specs=pl.BlockSpec(memory_space=pltpu.VMEM),
        scratch_shapes=[pltpu.SemaphoreType.DMA((N_DEV - 1,)), pltpu.SemaphoreType.DMA((N_DEV - 1,))],
    )(x)


def _hbm_specs(n):
    return [pl.BlockSpec(memory_space=pl.ANY)] * n


def _gather_weights(shards):
    n = len(shards)

    def body(*refs):
        plan = _gather_plan(refs[:n], refs[n:2 * n], *refs[2 * n:])
        _start(plan)
        _finish(plan)

    return pl.pallas_call(
        body, name="gather_weights",
        out_shape=_gather_shapes(shards), in_specs=_hbm_specs(n), out_specs=_hbm_specs(n),
        scratch_shapes=_gather_sems(n),
    )(*shards)


def _gather_shapes(shards):
    return [_sds((N_CHIP,) + s.shape, s.dtype) for s in shards]


def _gather_sems(n):
    return [pltpu.SemaphoreType.DMA((3 * n,)), pltpu.SemaphoreType.DMA((3 * n,)), pltpu.SemaphoreType.DMA((n,))]


def _gather_plan(ins, outs, send_sems, recv_sems, local_sems):
    mx, my, mc = _me()
    chips = [(1 - mx, my), (mx, 1 - my), (1 - mx, 1 - my)]
    local, sends, recvs = [], [], []
    for a in range(len(ins)):
        local.append(pltpu.make_async_copy(ins[a], outs[a].at[2 * mx + my], local_sems.at[a]))
        for k, (px, py) in enumerate(chips):
            sems = (send_sems.at[3 * a + k], recv_sems.at[3 * a + k], (px, py, mc))
            sends.append(_remote(ins[a], outs[a].at[2 * mx + my], *sems))
            recvs.append(_remote(ins[a], outs[a].at[2 * px + py], *sems))
    return local, sends, recvs


def _start(plan):
    local, sends, _ = plan
    for cp in local + sends:
        cp.start()


def _finish(plan):
    local, sends, recvs = plan
    for cp in recvs:
        cp.wait_recv()
    for cp in sends:
        cp.wait_send()
    for cp in local:
        cp.wait()


def _exchange_shapes(pieces):
    return [_sds((N_DEV,) + p.shape[2:], p.dtype) for p in pieces]


def _exchange_sems(n):
    return [pltpu.SemaphoreType.DMA(((N_DEV - 1) * n,)), pltpu.SemaphoreType.DMA(((N_DEV - 1) * n,)),
            pltpu.SemaphoreType.DMA((n,))]


def _exchange_plan(ins, outs, send_sems, recv_sems, local_sems):
    me = _me()
    mx, my, mc = me
    local, sends, recvs = [], [], []
    for a in range(len(ins)):
        local.append(pltpu.make_async_copy(ins[a].at[2 * mx + my, mc], outs[a].at[_lin(me)], local_sems.at[a]))
        for k in range(1, N_DEV):
            p = _peer(k, me)
            s = (N_DEV - 1) * a + k - 1
            sends.append(_remote(ins[a].at[2 * p[0] + p[1], p[2]], outs[a].at[_lin(me)], send_sems.at[s],
                                 recv_sems.at[s], p))
            recvs.append(_remote(ins[a].at[2 * mx + my, mc], outs[a].at[_lin(p)], send_sems.at[s],
                                 recv_sems.at[s], p))
    return local, sends, recvs


REDUCE_VMEM = 56 * 1024 * 1024


def _reduce_swap(recvs):
    n = len(recvs)

    def body(*refs):
        r_refs, o_refs = refs[:n], refs[n:2 * n]
        send_sems, recv_sems = refs[2 * n:]
        mx, my, mc = _me()
        sib = (mx, my, 1 - mc)
        half = lambda a, c: o_refs[a].at[pl.ds(pl.multiple_of(c * recvs[a].shape[1], 8), recvs[a].shape[1])]
        sends = []
        for a in range(n):
            g = r_refs[a][0].astype(F32)
            for s in range(1, N_DEV):
                g = g + r_refs[a][s].astype(F32)
            half(a, mc)[...] = g
            cp = _remote(half(a, mc), half(a, mc), send_sems.at[a], recv_sems.at[a], sib)
            cp.start()
            sends.append(cp)
        for a in range(n):
            _remote(half(a, mc), half(a, 1 - mc), send_sems.at[a], recv_sems.at[a], sib).wait_recv()
        for cp in sends:
            cp.wait_send()

    vmem = pl.BlockSpec(memory_space=pltpu.VMEM)
    return pl.pallas_call(
        body, name="reduce_swap", out_shape=[_sds((2 * r.shape[1], r.shape[2])) for r in recvs],
        in_specs=[vmem] * n, out_specs=[vmem] * n,
        scratch_shapes=[pltpu.SemaphoreType.DMA((n,)), pltpu.SemaphoreType.DMA((n,))],
        compiler_params=_cparams(None, REDUCE_VMEM),
    )(*recvs)


def _adamw_big(g, w, m, v, name):
    rows, cols = g.shape
    tr = next((t for t in (256, 176, 128, 64, 8) if rows % t == 0), None)
    if tr is None:
        tc = _tile(cols, 256)
        blk, grid = pl.BlockSpec((rows, tc), lambda i: (0, i)), (cols // tc,)
    else:
        blk, grid = pl.BlockSpec((tr, cols), lambda i: (i, 0)), (rows // tr,)

    def body(g_ref, w_ref, m_ref, v_ref, go_ref, d_ref, m2_ref, v2_ref):
        g = g_ref[...]
        delta, m2, v2 = _adamw(w_ref[...], g, m_ref[...], v_ref[...])
        go_ref[...] = g
        d_ref[...] = delta
        m2_ref[...] = m2
        v2_ref[...] = v2

    return pl.pallas_call(
        body, name=name, grid=grid,
        in_specs=[blk] * 4, out_specs=[blk] * 4, out_shape=[_sds((rows, cols))] * 4,
        compiler_params=_cparams(("parallel",)),
    )(g, w, m, v)


SMALL_ORDER = (("mod", 6 * D), ("norm1_w", D), ("norm2_w", D), ("conv_w", CONVW * 3 * GW), ("a_log", GH),
               ("dt_bias", GH), ("gdn_norm_w", HD), ("q_norm_w", HD), ("k_norm_w", HD), ("sinks", SQH), ("loss", 1))
SMALL_R = 120


def _pack_small(d):
    parts = [d[k].reshape(-1).astype(F32) if k in d else jnp.zeros((n,), F32) for k, n in SMALL_ORDER]
    used = sum(n for _, n in SMALL_ORDER)
    parts.append(jnp.zeros((SMALL_R * LANE - used,), F32))
    return jnp.concatenate(parts).reshape(SMALL_R, LANE)


def _unpack_small(pk):
    flat = pk.reshape(-1)
    out, r = {}, 0
    for k, n in SMALL_ORDER:
        out[k] = flat[r:r + n]
        r += n
    return out


def kernel(x, c, w_ada, b_ada, norm1_w, w_in, conv_w, a_log, dt_bias, gdn_norm_w, q_norm_w, k_norm_w, sinks, w_out, norm2_w, w_gate, w_up, w_down, loss_target, m_w_ada, m_b_ada, m_norm1_w, m_w_in, m_conv_w, m_a_log, m_dt_bias, m_gdn_norm_w, m_q_norm_w, m_k_norm_w, m_sinks, m_w_out, m_norm2_w, m_w_gate, m_w_up, m_w_down, v_w_ada, v_b_ada, v_norm1_w, v_w_in, v_conv_w, v_a_log, v_dt_bias, v_gdn_norm_w, v_q_norm_w, v_k_norm_w, v_sinks, v_w_out, v_norm2_w, v_w_gate, v_w_up, v_w_down):
    mx, my, mc = _me()
    chip = 2 * mx + my
    dev = 4 * mx + 2 * my + mc
    T = x.shape[1]

    conv_sh = conv_w.reshape(CONVW, 3 * GW // N_CHIP)
    mine = jnp.concatenate([c.reshape(-1), conv_sh.reshape(-1), jnp.zeros((4 * LANE,), F32)]).reshape(24, LANE)
    got = _all_gather8(mine, "gather_c_conv")
    c8 = got[:, :8].reshape(N_DEV, D)
    conv_full = jnp.concatenate([got[2 * j, 8:20].reshape(CONVW, 3 * GW // N_CHIP) for j in range(N_CHIP)], axis=1)
    c16 = jnp.concatenate([c8, jnp.zeros((8, D), F32)], axis=0)
    b_sh = lax.dynamic_slice(b_ada, (0, chip * ADA_N), (1, ADA_N))
    mods = _all_gather8(_mod_part(c16, w_ada[0], b_sh), "gather_mod")
    mod = jnp.concatenate([lax.dynamic_slice(mods[2 * j], (dev, 0), (1, ADA_N)) for j in range(N_CHIP)], axis=1)

    as_rows = lambda t, transposed: t[0].T if transposed else t[0]
    transposed = (True, False, True, True, False)
    big_w = [as_rows(t, tr) for t, tr in zip((w_in, w_out, w_gate, w_up, w_down), transposed)]
    shards = [t.astype(BF16) for t in big_w]
    (a_in,) = _gather_weights(shards[:1])
    w_in_pt = _permute_w_in_t(a_in.reshape(PROJ, D))

    loss, grad_x, big, small = _local_step(
        x[0], loss_target[0], mod, norm1_w, w_in_pt, conv_full, a_log, dt_bias, gdn_norm_w,
        q_norm_w, k_norm_w, sinks, norm2_w, shards[1:])

    small["loss"] = loss[:, :1]
    sg = _all_gather8(_pack_small(small), "gather_small_grads")
    rep = dict(mod=(b_ada, m_b_ada, v_b_ada), norm1_w=(norm1_w, m_norm1_w, v_norm1_w),
               norm2_w=(norm2_w, m_norm2_w, v_norm2_w), a_log=(a_log, m_a_log, v_a_log),
               dt_bias=(dt_bias, m_dt_bias, v_dt_bias), gdn_norm_w=(gdn_norm_w, m_gdn_norm_w, v_gdn_norm_w),
               q_norm_w=(q_norm_w, m_q_norm_w, v_q_norm_w), k_norm_w=(k_norm_w, m_k_norm_w, v_k_norm_w),
               sinks=(sinks, m_sinks, v_sinks))
    wmv = [_pack_small({k: t[i] for k, t in rep.items()}) for i in range(3)]
    sres = _reduce_adamw(sg, wmv[0], wmv[1], wmv[2], "small_reduce_adamw")
    s_g, s_d, s_m, s_v = [_unpack_small(sres[i]) for i in range(4)]
    loss_out = s_g["loss"][0]

    g_conv = lax.dynamic_slice(s_g["conv_w"].reshape(CONVW, 3 * GW), (0, chip * (3 * GW // N_CHIP)),
                               (CONVW, 3 * GW // N_CHIP))
    pad16 = lambda t: jnp.concatenate([t.reshape(12, LANE), jnp.zeros((4, LANE), F32)], axis=0)
    cres = _adamw_call(pad16(g_conv), pad16(conv_w), pad16(m_conv_w), pad16(v_conv_w), "conv_adamw")
    conv_out = [g_conv.reshape(conv_w.shape)] + [cres[i, :12].reshape(conv_w.shape) for i in range(3)]

    dmod8 = sg[:, :6 * D // LANE].reshape(N_DEV, 6 * D)
    dm = lax.dynamic_slice(dmod8, (0, chip * ADA_N), (N_DEV, ADA_N))
    zpad = lambda t: jnp.concatenate([t, jnp.zeros((KPAD - N_DEV, t.shape[1]), F32)], axis=0)
    ares = _w_ada_update(zpad(c8), zpad(dm), w_ada[0], m_w_ada[0], v_w_ada[0])

    names = ("w_in", "w_out", "w_gate", "w_up", "w_down")
    g_full = list(_reduce_swap(big))
    g_full[0] = g_full[0][:W_IN_ROWS]
    big_m = [as_rows(t, tr) for t, tr in zip((m_w_in, m_w_out, m_w_gate, m_w_up, m_w_down), transposed)]
    big_v = [as_rows(t, tr) for t, tr in zip((v_w_in, v_w_out, v_w_gate, v_w_up, v_w_down), transposed)]
    upd = [_adamw_big(g, w, m, v, "adamw_" + nm) for g, w, m, v, nm in zip(g_full, big_w, big_m, big_v, names)]
    back = lambda t, tr: (t.T if tr else t)[None]
    bg, bd, bm, bv = [[back(u[i], tr) for u, tr in zip(upd, transposed)] for i in range(4)]

    def group(a_i, small_d, conv_i, big_l):
        s = lambda k, ref: small_d[k].reshape(ref.shape)
        return [ares[a_i][None], s("mod", b_ada), s("norm1_w", norm1_w), big_l[0], conv_out[conv_i],
                s("a_log", a_log), s("dt_bias", dt_bias), s("gdn_norm_w", gdn_norm_w), s("q_norm_w", q_norm_w),
                s("k_norm_w", k_norm_w), s("sinks", sinks), big_l[1], s("norm2_w", norm2_w), big_l[2], big_l[3],
                big_l[4]]

    outs = [loss_out, grad_x[None]]
    outs += group(0, s_g, 0, bg) + group(1, s_d, 1, bd) + group(2, s_m, 2, bm) + group(3, s_v, 3, bv)
    return tuple(outs)
```

```python
import jax
import jax.numpy as jnp
from jax import lax
from jax.experimental import pallas as pl
from jax.experimental.pallas import tpu as pltpu

F32 = jnp.float32
BF16 = jnp.bfloat16
MESH = pl.DeviceIdType.MESH

D = 1024
HD = 64
GH = 8
GW = GH * HD
SQH = 8
SKVH = 2
SGRP = SQH // SKVH
WIN = 128
CONVW = 4
CHUNK = 64
DFF = 2816
PROJ = 2832
NP = 3072
EPS = 1e-6
N_DEV = 8
N_CHIP = 4

ADAM_LR = 0.001
ADAM_B1 = 0.9
ADAM_B2 = 0.999
ADAM_EPS = 1e-08
ADAM_WD = 0.01
ADAM_STEP = 10

VMEM_LIMIT = 48 * 1024 * 1024
GDN_BWD_VMEM = 58 * 1024 * 1024
LANE = 128


def _cparams(sem=None, vmem=VMEM_LIMIT):
    return pltpu.CompilerParams(dimension_semantics=sem, vmem_limit_bytes=vmem)


_NN = ((1,), (0,))
_NT = ((1,), (1,))
_TN = ((0,), (0,))


def _dot(a, b, dims):
    if a.ndim == 3:
        (ca,), (cb,) = dims
        return lax.dot_general(a, b, (((ca + 1,), (cb + 1,)), ((0,), (0,))), preferred_element_type=F32)
    return lax.dot_general(a, b, (dims, ((), ())), preferred_element_type=F32)


def _raw1(a, b, dims):
    return _dot(a.astype(BF16), b.astype(BF16), dims)


def _raw3(a, b, dims):
    ah = a.astype(BF16)
    al = (a - ah.astype(F32)).astype(BF16)
    bh = b.astype(BF16)
    bl = (b - bh.astype(F32)).astype(BF16)
    return _dot(ah, bh, dims) + (_dot(al, bh, dims) + _dot(ah, bl, dims))


def _make_diff_mm(raw):
    @jax.custom_vjp
    def nn(a, b):
        return raw(a, b, _NN)

    @jax.custom_vjp
    def nt(a, b):
        return raw(a, b, _NT)

    @jax.custom_vjp
    def tn(a, b):
        return raw(a, b, _TN)

    nn.defvjp(lambda a, b: (raw(a, b, _NN), (a, b)), lambda r, g: (nt(g, r[1]), tn(r[0], g)))
    nt.defvjp(lambda a, b: (raw(a, b, _NT), (a, b)), lambda r, g: (nn(g, r[1]), tn(g, r[0])))
    tn.defvjp(lambda a, b: (raw(a, b, _TN), (a, b)), lambda r, g: (nt(r[1], g), nn(r[0], g)))
    return nn, nt, tn


def _tri_inv_raw(a, nn3):
    n = a.shape[-1]
    ri = lax.broadcasted_iota(jnp.int32, (n, n), 0)
    ci = lax.broadcasted_iota(jnp.int32, (n, n), 1)
    t = (ri == ci).astype(F32)
    for lvl in range((n - 1).bit_length()):
        same_pair = (ri >> (lvl + 1)) == (ci >> (lvl + 1))
        lower_left = (((ri >> lvl) & 1) == 1) & (((ci >> lvl) & 1) == 0)
        y = jnp.where(same_pair & lower_left, a, 0.0)
        t = t - y if lvl == 0 else t - nn3(nn3(t, y), t)
    return t


class _Kit:
    def __init__(self, diff):
        if diff:
            self.nn, self.nt, self.tn = _make_diff_mm(_raw1)
            self.nn3, self.nt3, self.tn3 = _make_diff_mm(_raw3)
            nn3, nt3, tn3 = self.nn3, self.nt3, self.tn3

            @jax.custom_vjp
            def inv(a, t):
                return t

            def inv_fwd(a, t):
                return t, t

            def inv_bwd(t, g):
                return -tn3(t, nt3(g, t)), jnp.zeros_like(t)

            inv.defvjp(inv_fwd, inv_bwd)
            self.inv = inv
        else:
            self.nn = lambda a, b: _raw1(a, b, _NN)
            self.nt = lambda a, b: _raw1(a, b, _NT)
            self.tn = lambda a, b: _raw1(a, b, _TN)
            self.nn3 = lambda a, b: _raw3(a, b, _NN)
            self.nt3 = lambda a, b: _raw3(a, b, _NT)
            self.tn3 = lambda a, b: _raw3(a, b, _TN)
            self.inv = lambda a, t: _tri_inv_raw(a, self.nn3) if t is None else t


def _sigmoid(x):
    return 1.0 / (1.0 + jnp.exp(-x))


def _silu(x):
    return x * _sigmoid(x)


def _rms(x, w):
    return x * lax.rsqrt(jnp.mean(x * x, axis=-1, keepdims=True) + EPS) * w


def _tile(dim, target):
    t = (min(dim, target) // LANE) * LANE
    while t >= LANE:
        if dim % t == 0:
            return t
        t -= LANE
    return dim


MM_TM, MM_TN, MM_TK = 1408, 1536, 1408


def _matmul(a, b, ta=False, tb=False, out_dtype=F32, name="matmul", gather=None, exchange=None):
    carried = gather if gather is not None else exchange if exchange is not None else []
    nc = len(carried)
    if ta:
        K, M = a.shape
    else:
        M, K = a.shape
    if tb:
        N, K2 = b.shape
    else:
        K2, N = b.shape
    assert K == K2, (a.shape, b.shape, ta, tb)
    tm, tn, tk = _tile(M, MM_TM), _tile(N, MM_TN), _tile(K, MM_TK)
    nk = K // tk
    dims = ((0,) if ta else (1,), (1,) if tb else (0,))

    grid = (M // tm, N // tn, nk)

    def body(*refs):
        a_ref, b_ref = refs[:2]
        o_ref = refs[2 + nc]
        scratch = refs[3 + 2 * nc:]
        k = pl.program_id(2)
        if nc:
            make_plan = _gather_plan if gather is not None else _exchange_plan
            plan = make_plan(refs[2:2 + nc], refs[3 + nc:3 + 2 * nc], *scratch[-3:])
            at = lambda pos: ((pl.program_id(0) == pos[0]) & (pl.program_id(1) == pos[1]) & (k == pos[2]))

            @pl.when(at((0, 0, 0)))
            def _():
                _start(plan)

        part = _dot(a_ref[...].astype(BF16), b_ref[...].astype(BF16), dims)
        if nk == 1:
            o_ref[...] = part.astype(o_ref.dtype)
        else:
            acc_ref = scratch[0]

            @pl.when(k == 0)
            def _():
                acc_ref[...] = part

            @pl.when((k > 0) & (k < nk - 1))
            def _():
                acc_ref[...] += part

            @pl.when(k == nk - 1)
            def _():
                o_ref[...] = (acc_ref[...] + part).astype(o_ref.dtype)

        if nc:
            @pl.when(at((grid[0] - 1, grid[1] - 1, nk - 1)))
            def _():
                _finish(plan)

    a_spec = (pl.BlockSpec((tk, tm), lambda i, j, k: (k, i)) if ta
              else pl.BlockSpec((tm, tk), lambda i, j, k: (i, k)))
    b_spec = (pl.BlockSpec((tn, tk), lambda i, j, k: (j, k)) if tb
              else pl.BlockSpec((tk, tn), lambda i, j, k: (k, j)))
    if gather is not None:
        c_shapes, c_sems = _gather_shapes(carried), _gather_sems(nc)
    elif exchange is not None:
        c_shapes, c_sems = _exchange_shapes(carried), _exchange_sems(nc)
    else:
        c_shapes, c_sems = [], []
    res = pl.pallas_call(
        body, name=name, grid=grid,
        in_specs=[a_spec, b_spec] + _hbm_specs(nc),
        out_specs=[pl.BlockSpec((tm, tn), lambda i, j, k: (i, j))] + _hbm_specs(nc),
        out_shape=[jax.ShapeDtypeStruct((M, N), out_dtype)] + c_shapes,
        scratch_shapes=([pltpu.VMEM((tm, tn), F32)] if nk > 1 else []) + c_sems,
        compiler_params=_cparams(("arbitrary",) * 3 if nc else ("parallel", "parallel", "arbitrary")),
    )(a, b, *carried)
    return (res[0], res[1:]) if nc else res[0]


def _rowcall(fn, tiled, consts, out_tiled, out_acc, tm, name):
    T = tiled[0].shape[0]
    n_in = len(tiled) + len(consts)
    n_o = len(out_tiled)

    def body(*refs):
        vals = [r[...] for r in refs[:n_in]]
        outs = refs[n_in:]
        res = fn(*vals)
        for r, v in zip(outs[:n_o], res[:n_o]):
            r[...] = v.astype(r.dtype)
        if len(outs) > n_o:
            @pl.when(pl.program_id(0) == 0)
            def _():
                for r in outs[n_o:]:
                    r[...] = jnp.zeros_like(r)

            for r, v in zip(outs[n_o:], res[n_o:]):
                r[...] += v

    in_specs = [pl.BlockSpec((tm, a.shape[1]), lambda i: (i, 0)) for a in tiled]
    in_specs += [pl.BlockSpec(a.shape, lambda i, nd=a.ndim: (0,) * nd) for a in consts]
    out_specs = [pl.BlockSpec((tm, s.shape[1]), lambda i: (i, 0)) for s in out_tiled]
    out_specs += [pl.BlockSpec(s.shape, lambda i: (0, 0)) for s in out_acc]
    return pl.pallas_call(
        body, name=name, grid=(T // tm,),
        in_specs=in_specs, out_specs=out_specs,
        out_shape=list(out_tiled) + list(out_acc),
        compiler_params=_cparams(("arbitrary",)),
    )(*tiled, *consts)


def _sds(shape, dtype=F32):
    return jax.ShapeDtypeStruct(shape, dtype)


def _norm_mod(x, nw, scale, shift):
    return _rms(x, nw) * (1.0 + scale) + shift


def _norm_mod_fwd(x, nw, scale, shift):
    T = x.shape[0]
    (h,) = _rowcall(lambda *a: (_norm_mod(*a),), [x], [nw, scale, shift],
                    [_sds((T, D), BF16)], [], 512, "norm1_fwd")
    return h


ROWS_TM = 512
ROWS_EPI = 256


def _matmul_rows(a, b, epi, tiled, consts, out_tiled, out_acc, name, pieces=()):
    T, K = a.shape
    tm, tk = _tile(T, ROWS_TM), _tile(K, MM_TK)
    nm, nk = T // tm, K // tk
    npc, nt, ncst, no, na = len(pieces), len(tiled), len(consts), len(out_tiled), len(out_acc)
    n_in = 2 + nt + ncst

    def body(*refs):
        a_ref, b_ref = refs[:2]
        t_refs, c_refs = refs[2:2 + nt], refs[2 + nt:n_in]
        o_refs = refs[n_in + npc:n_in + npc + no]
        acc_refs = refs[n_in + npc + no:n_in + npc + no + na]
        n_out = no + na + npc
        res_ref = refs[n_in + npc + n_out]
        plan = _exchange_plan(refs[n_in:n_in + npc], refs[n_in + npc + no + na:n_in + npc + n_out],
                              *refs[n_in + npc + n_out + 1:]) if npc else None
        i, k = pl.program_id(0), pl.program_id(1)

        @pl.when((i == 0) & (k == 0))
        def _():
            for r in acc_refs:
                r[...] = jnp.zeros_like(r)
            if npc:
                _start(plan)

        part = _dot(a_ref[...], b_ref[...], _NN)

        @pl.when(k == 0)
        def _():
            res_ref[...] = part

        @pl.when(k > 0)
        def _():
            res_ref[...] += part

        @pl.when(k == nk - 1)
        def _():
            for r0 in range(0, tm, ROWS_EPI):
                rows = pl.ds(r0, ROWS_EPI)
                outs = epi(res_ref[rows, :], *[r[rows, :] for r in t_refs], *[r[...] for r in c_refs])
                for r, v in zip(o_refs, outs[:no]):
                    r[rows, :] = v.astype(r.dtype)
                for r, v in zip(acc_refs, outs[no:]):
                    r[...] += v

        if npc:
            @pl.when((i == nm - 1) & (k == nk - 1))
            def _():
                _finish(plan)

    row = lambda w: pl.BlockSpec((tm, w), lambda i, k: (i, 0))
    whole = lambda s: pl.BlockSpec(s.shape, lambda i, k: (0, 0))
    res = pl.pallas_call(
        body, name=name, grid=(nm, nk),
        in_specs=[pl.BlockSpec((tm, tk), lambda i, k: (i, k)), pl.BlockSpec((tk, D), lambda i, k: (k, 0))]
                 + [row(t.shape[1]) for t in tiled] + [whole(c) for c in consts] + _hbm_specs(npc),
        out_specs=[row(s.shape[1]) for s in out_tiled] + [whole(s) for s in out_acc] + _hbm_specs(npc),
        out_shape=list(out_tiled) + list(out_acc) + (_exchange_shapes(pieces) if npc else []),
        scratch_shapes=[pltpu.VMEM((tm, D), F32)] + (_exchange_sems(npc) if npc else []),
        compiler_params=_cparams(("arbitrary", "arbitrary")),
    )(a, b, *tiled, *consts, *pieces)
    return res[:no + na], res[no + na:]


def _in_proj_dx_norm_bwd(dproj, w_in_pt, x, dres, nw, scale, shift, pieces):
    T = x.shape[0]

    def epi(dh, x, dres, nw, scale, shift):
        _, vjp = jax.vjp(_norm_mod, x, nw, scale, shift)
        dx, dnw, dsc, dsh = vjp(dh)
        return dx + dres, dnw, dsc, dsh

    return _matmul_rows(dproj, w_in_pt, epi, [x, dres], [nw, scale, shift], [_sds((T, D))], [_sds((1, D))] * 3,
                        "in_proj_dx_norm1_bwd", pieces)


def _out_proj_resid_norm(mixcat, w_out, x, gate1, nw, scale, shift):
    T = x.shape[0]

    def epi(mixed, x, gate1, nw, scale, shift):
        return (mixed,) + _resid_norm(x, mixed, gate1, nw, scale, shift)

    outs, _ = _matmul_rows(mixcat, w_out, epi, [x], [gate1, nw, scale, shift],
                           [_sds((T, D)), _sds((T, D)), _sds((T, D), BF16)], [], "out_proj_resid_norm2")
    return outs


def _ffn_up_dx_resid_bwd(dab, w_gut, x, mixed, dy, gate1, nw, scale, shift):
    T = x.shape[0]

    def epi(dh2, x, mixed, dy, gate1, nw, scale, shift):
        _, vjp = jax.vjp(_resid_norm, x, mixed, gate1, nw, scale, shift)
        return vjp((dy, dh2))

    outs, _ = _matmul_rows(dab, w_gut, epi, [x, mixed, dy], [gate1, nw, scale, shift],
                           [_sds((T, D)), _sds((T, D), BF16)], [_sds((1, D))] * 4, "ffn_up_dx_resid_norm2_bwd")
    return outs


def _ffn_down_loss(act, w_down, x1, target, gate2):
    T = x1.shape[0]

    def epi(ffn, x1, target, gate2):
        y = x1 + gate2 * ffn
        err = y - target
        loss = 0.5 * jnp.sum(jnp.sum(err * err, axis=1, keepdims=True), axis=0, keepdims=True) / D
        dy = err * (1.0 / D)
        return dy, gate2 * dy, jnp.sum(dy * ffn, axis=0, keepdims=True), jnp.broadcast_to(loss, (1, LANE))

    outs, _ = _matmul_rows(act, w_down, epi, [x1, target], [gate2], [_sds((T, D)), _sds((T, D), BF16)],
                           [_sds((1, D)), _sds((1, LANE))], "ffn_down_loss")
    return outs


def _resid_norm(x, mixed, gate1, nw, scale, shift):
    x1 = x + gate1 * mixed
    return x1, _norm_mod(x1, nw, scale, shift)


FFN_BLK = 256
FFN_TM = 2048


def _interleave_gate_up(gate_t, up_t):
    blocks = lambda t: t.reshape(DFF // FFN_BLK, 1, FFN_BLK, D)
    return jnp.concatenate([blocks(gate_t), blocks(up_t)], axis=1).reshape(2 * DFF, D)


def _split_gate_up(g):
    g = g.reshape(DFF // FFN_BLK, 2, FFN_BLK, D)
    return g[:, 0].reshape(DFF, D), g[:, 1].reshape(DFF, D)


def _ffn_up_act(h2, w_gut):
    T = h2.shape[0]
    tm = _tile(T, FFN_TM)

    def body(h_ref, w_ref, ab_ref, act_ref):
        ab = _dot(h_ref[...], w_ref[...], _NT)
        ab_ref[...] = ab
        act_ref[...] = (_silu(ab[:, :FFN_BLK]) * ab[:, FFN_BLK:]).astype(act_ref.dtype)

    return pl.pallas_call(
        body, name="ffn_up_act", grid=(T // tm, DFF // FFN_BLK),
        in_specs=[pl.BlockSpec((tm, D), lambda i, j: (i, 0)), pl.BlockSpec((2 * FFN_BLK, D), lambda i, j: (j, 0))],
        out_specs=[pl.BlockSpec((tm, 2 * FFN_BLK), lambda i, j: (i, j)), pl.BlockSpec((tm, FFN_BLK), lambda i, j: (i, j))],
        out_shape=[_sds((T, 2 * DFF)), _sds((T, DFF), BF16)],
        compiler_params=_cparams(("parallel", "parallel")),
    )(h2, w_gut)


def _ffn_down_dx_act(dffn, w_down, ab):
    T = dffn.shape[0]
    tm = _tile(T, FFN_TM)

    def body(d_ref, w_ref, ab_ref, o_ref):
        dact = _dot(d_ref[...], w_ref[...], _NT)
        a, b = ab_ref[:, :FFN_BLK], ab_ref[:, FFN_BLK:]
        s = _sigmoid(a)
        da = dact * b * (s * (1.0 + a * (1.0 - s)))
        db = dact * (a * s)
        o_ref[...] = jnp.concatenate([da, db], axis=1).astype(o_ref.dtype)

    return pl.pallas_call(
        body, name="ffn_down_dx_act", grid=(T // tm, DFF // FFN_BLK),
        in_specs=[pl.BlockSpec((tm, D), lambda i, j: (i, 0)), pl.BlockSpec((FFN_BLK, D), lambda i, j: (j, 0)),
                  pl.BlockSpec((tm, 2 * FFN_BLK), lambda i, j: (i, j))],
        out_specs=pl.BlockSpec((tm, 2 * FFN_BLK), lambda i, j: (i, j)),
        out_shape=_sds((T, 2 * DFF), BF16),
        compiler_params=_cparams(("parallel", "parallel")),
    )(dffn, w_down, ab)


def _round_bf16(x):
    return x.astype(BF16).astype(F32)


def _shift_down(x, s, rows):
    if s == 0:
        return x
    return jnp.where(rows >= s, pltpu.roll(x, s, 0), 0.0)


def _shift_up(x, s, rows, T):
    if s == 0:
        return x
    return jnp.where(rows < T - s, pltpu.roll(x, T - s, 0), 0.0)


def _conv_fwd(proj, conv_w, shards):
    T = proj.shape[0]
    ncol = 3 * GW // LANE
    ns = len(shards)

    def body(*refs):
        x_ref, w_ref = refs[:2]
        o_ref = refs[2 + ns]
        plan = _gather_plan(refs[2:2 + ns], refs[3 + ns:3 + 2 * ns], *refs[3 + 2 * ns:])

        @pl.when(pl.program_id(0) == 0)
        def _():
            _start(plan)

        x = _round_bf16(x_ref[...])
        rows = lax.broadcasted_iota(jnp.int32, x.shape, 0)
        acc = jnp.zeros_like(x)
        for j in range(CONVW):
            acc = acc + _round_bf16(w_ref[pl.ds(j, 1), :]) * _shift_down(x, CONVW - 1 - j, rows)
        o_ref[0], o_ref[1] = _split_pair(_silu(acc))

        @pl.when(pl.program_id(0) == ncol - 1)
        def _():
            _finish(plan)

    res = pl.pallas_call(
        body, name="conv_fwd", grid=(ncol,),
        in_specs=[pl.BlockSpec((T, LANE), lambda j: (0, j)), pl.BlockSpec((CONVW, LANE), lambda j: (0, j))]
                 + _hbm_specs(ns),
        out_specs=[pl.BlockSpec((2, T, HD), lambda j: (j, 0, 0))] + _hbm_specs(ns),
        out_shape=[_sds((3 * GH, T, HD))] + _gather_shapes(shards),
        scratch_shapes=_gather_sems(ns),
        compiler_params=_cparams(("arbitrary",)),
    )(proj, conv_w, *shards)
    return res[0], res[1:]


RELAYOUT_TM = 4096


def _split_pair(y):
    return y[:, :HD], pltpu.roll(y, HD, 1)[:, :HD]


def _merge_pair(a, b):
    return jnp.concatenate([a, b], axis=1)


def _split_heads(x, col_block0, nheads, name):
    T = x.shape[0]
    tm = _tile(T, RELAYOUT_TM)

    def body(x_ref, o_ref):
        a, b = _split_pair(x_ref[...])
        o_ref[0] = a
        o_ref[1] = b

    return pl.pallas_call(
        body, name=name, grid=(nheads // 2, T // tm),
        in_specs=[pl.BlockSpec((tm, LANE), lambda j, i: (i, col_block0 + j))],
        out_specs=pl.BlockSpec((2, tm, HD), lambda j, i: (j, i, 0)),
        out_shape=_sds((nheads, T, HD), x.dtype),
        compiler_params=_cparams(("parallel", "parallel")),
    )(x)


def _merge_heads(hm, out_dtype, name, into=None, col_block0=0, head0=0, nheads=None):
    T = hm.shape[1]
    nheads = hm.shape[0] if nheads is None else nheads
    tm = _tile(T, RELAYOUT_TM)

    def body(*refs):
        h_ref, o_ref = refs[0], refs[-1]
        o_ref[...] = _merge_pair(h_ref[0], h_ref[1]).astype(o_ref.dtype)

    in_specs = [pl.BlockSpec((2, tm, HD), lambda j, i: (head0 // 2 + j, i, 0))]
    args = [hm]
    if into is None:
        out_shape = _sds((T, HD * nheads), out_dtype)
        aliases = {}
    else:
        out_shape = _sds(into.shape, into.dtype)
        in_specs.append(pl.BlockSpec(memory_space=pl.ANY))
        args.append(into)
        aliases = {1: 0}
    return pl.pallas_call(
        body, name=name, grid=(nheads // 2, T // tm),
        in_specs=in_specs,
        out_specs=pl.BlockSpec((tm, LANE), lambda j, i: (i, col_block0 + j)),
        out_shape=out_shape, input_output_aliases=aliases,
        compiler_params=_cparams(("parallel", "parallel")),
    )(*args)


def _conv_bwd(proj, conv_w, dqc):
    T = proj.shape[0]
    ncol = 3 * GW // LANE

    def body(x_ref, w_ref, d_ref, dx_ref, dw_ref):
        x = _round_bf16(x_ref[...])
        rows = lax.broadcasted_iota(jnp.int32, x.shape, 0)
        xs = [_shift_down(x, CONVW - 1 - j, rows) for j in range(CONVW)]
        w = [_round_bf16(w_ref[pl.ds(j, 1), :]) for j in range(CONVW)]
        pre = jnp.zeros_like(x)
        for j in range(CONVW):
            pre = pre + w[j] * xs[j]
        s = _sigmoid(pre)
        dpre = _round_bf16(_merge_pair(d_ref[0], d_ref[1]) * (s * (1.0 + pre * (1.0 - s))))
        dx = jnp.zeros_like(x)
        for j in range(CONVW):
            dx = dx + w[j] * _shift_up(dpre, CONVW - 1 - j, rows, T)
            dw_ref[pl.ds(j, 1), :] = jnp.sum(dpre * xs[j], axis=0, keepdims=True)
        dx_ref[...] = dx.astype(dx_ref.dtype)

    return pl.pallas_call(
        body, name="conv_bwd", grid=(ncol,),
        in_specs=[pl.BlockSpec((T, LANE), lambda j: (0, j)), pl.BlockSpec((CONVW, LANE), lambda j: (0, j)),
                  pl.BlockSpec((2, T, HD), lambda j: (j, 0, 0))],
        out_specs=[pl.BlockSpec((T, LANE), lambda j: (0, j)), pl.BlockSpec((CONVW, LANE), lambda j: (0, j))],
        out_shape=[_sds((T, NP), BF16), _sds((CONVW, 3 * GW))],
        compiler_params=_cparams(("parallel",)),
    )(proj, conv_w, dqc)


def _gdn_prep(kit, q, k, v, ga, gb, alog, dtb, t_inv=None):
    C = CHUNK
    ri = lax.broadcasted_iota(jnp.int32, (C, C), 0)
    ci = lax.broadcasted_iota(jnp.int32, (C, C), 1)
    causal = ri >= ci
    strict = ri > ci
    eye = (ri == ci).astype(F32)
    lower = causal.astype(F32)
    upper = (ri <= ci).astype(F32)

    a = ga + dtb
    softplus = jnp.maximum(a, 0.0) + jnp.log(1.0 + jnp.exp(-jnp.abs(a)))
    g_row = -jnp.exp(alog) * softplus
    beta_row = _sigmoid(gb)
    g_col = jnp.sum(eye * g_row, axis=2, keepdims=True)
    beta_col = jnp.sum(eye * beta_row, axis=2, keepdims=True)
    G_col = jnp.sum(lower * g_row, axis=2, keepdims=True)
    G_row = jnp.sum(upper * g_col, axis=1, keepdims=True)
    G_last = jnp.sum(g_row, axis=2, keepdims=True)
    decay = jnp.exp(jnp.where(causal, G_col - G_row, -1e30))

    qn = q * lax.rsqrt(jnp.sum(q * q, axis=-1, keepdims=True) + EPS) * (HD ** -0.5)
    kn = k * lax.rsqrt(jnp.sum(k * k, axis=-1, keepdims=True) + EPS)
    kb = kn * beta_col
    A = jnp.where(strict, kit.nt(kb, kn) * decay, 0.0)
    Tm = kit.inv(A, t_inv)
    eG = jnp.exp(G_col)
    u = kit.nn3(Tm, v * beta_col)
    w = kit.nn3(Tm, kb * eG)
    qk = jnp.where(causal, kit.nt(qn, kn) * decay, 0.0)
    q_dec = qn * eG
    k_dec = kn * jnp.exp(G_last - G_col)
    dec = jnp.exp(G_last)
    return u, w, qk, q_dec, k_dec, dec, Tm


def _gdn_out(o, z, nw):
    return _rms(o, nw) * _silu(z)


GDN_CB = 4


def _gdn_specs(T, blk):
    TB = GDN_CB * CHUNK
    seq = lambda grp: pl.BlockSpec((GH, TB, HD), lambda i, grp=grp: (grp, blk(i), 0))
    row = lambda grp: pl.BlockSpec((GH, GDN_CB, 1, CHUNK), lambda i, grp=grp: (grp, blk(i), 0, 0))
    per_head = pl.BlockSpec((GH, 1, CHUNK), lambda i: (0, 0, 0))
    whole = pl.BlockSpec((1, HD), lambda i: (0, 0))
    state = pl.BlockSpec((GH, GDN_CB, HD, HD), lambda i: (0, blk(i), 0, 0))
    return seq, row, per_head, whole, state


def _gdn_load(seq_refs, row_refs, head_refs):
    chunks = lambda r: jnp.concatenate([r[:, pl.ds(cb * CHUNK, CHUNK), :] for cb in range(GDN_CB)], axis=0)
    rows = lambda r: jnp.concatenate([r[:, cb] for cb in range(GDN_CB)], axis=0)
    heads = lambda r: jnp.concatenate([r[...]] * GDN_CB, axis=0)
    return [chunks(r) for r in seq_refs], [rows(r) for r in row_refs], [heads(r) for r in head_refs]


def _gdn_fwd(qkv_hm, zs_hm, gab, alog_b, dtb_b, nw, shards):
    T = qkv_hm.shape[1]
    N = T // CHUNK
    nblk = N // GDN_CB
    ns = len(shards)
    seq, row, per_head, whole, state = _gdn_specs(T, lambda i: i)
    kit = _Kit(False)

    def body(*refs):
        q_ref, k_ref, v_ref, z_ref, ga_ref, gb_ref, al_ref, dt_ref, nw_ref = refs[:9]
        o_ref, S_ref, T_ref = refs[9 + ns:12 + ns]
        S_scr = refs[12 + 2 * ns]
        plan = _gather_plan(refs[9:9 + ns], refs[12 + ns:12 + 2 * ns], *refs[13 + 2 * ns:])

        @pl.when(pl.program_id(0) == 0)
        def _():
            S_scr[...] = jnp.zeros_like(S_scr)
            _start(plan)

        (q, k, v, z), (ga, gb), (al, dt) = _gdn_load((q_ref, k_ref, v_ref, z_ref), (ga_ref, gb_ref), (al_ref, dt_ref))
        u, w, qk, q_dec, k_dec, dec, t_inv = _gdn_prep(kit, q, k, v, ga, gb, al, dt)
        S = S_scr[...]
        for cb in range(GDN_CB):
            hs = slice(cb * GH, (cb + 1) * GH)
            S_ref[:, cb] = S
            T_ref[:, cb] = t_inv[hs]
            v_new = u[hs] - kit.nn(w[hs], S)
            o = kit.nn(q_dec[hs], S) + kit.nn(qk[hs], v_new)
            S = S * dec[hs] + kit.tn(k_dec[hs], v_new)
            o_ref[:, pl.ds(cb * CHUNK, CHUNK), :] = _gdn_out(o, z[hs], nw_ref[...])
        S_scr[...] = S

        @pl.when(pl.program_id(0) == nblk - 1)
        def _():
            _finish(plan)

    res = pl.pallas_call(
        body, name="gdn_fwd", grid=(nblk,),
        in_specs=[seq(0), seq(1), seq(2), seq(0), row(0), row(1), per_head, per_head, whole] + _hbm_specs(ns),
        out_specs=[seq(0), state, state] + _hbm_specs(ns),
        out_shape=[_sds((GH + SQH, T, HD)), _sds((GH, N, HD, HD)), _sds((GH, N, CHUNK, CHUNK))]
                  + _gather_shapes(shards),
        scratch_shapes=[pltpu.VMEM((GH, HD, HD), F32)] + _gather_sems(ns),
        compiler_params=_cparams(("arbitrary",)),
    )(qkv_hm, qkv_hm, qkv_hm, zs_hm, gab, gab, alog_b, dtb_b, nw, *shards)
    return res[0], (res[1], res[2]), res[3:]


def _gdn_bwd(qkv_hm, zs_hm, gab, alog_b, dtb_b, nw, S_all, do, pieces):
    T = qkv_hm.shape[1]
    N = T // CHUNK
    nblk = N // GDN_CB
    npc = len(pieces)
    dkit, kit = _Kit(True), _Kit(False)
    rseq, rrow, per_head, whole, rstate = _gdn_specs(T, lambda i: nblk - 1 - i)

    def body(*refs):
        q_ref, k_ref, v_ref, z_ref, ga_ref, gb_ref, al_ref, dt_ref, nw_ref, S_ref, T_ref, do_ref = refs[:12]
        dqkv_ref, dz_ref, dga_ref, dgb_ref, dal_ref, ddt_ref, dnw_ref = refs[12 + npc:19 + npc]
        dS_scr = refs[19 + 2 * npc]
        plan = _exchange_plan(refs[12:12 + npc], refs[19 + npc:19 + 2 * npc], *refs[20 + 2 * npc:])

        @pl.when(pl.program_id(0) == 0)
        def _():
            dS_scr[...] = jnp.zeros_like(dS_scr)
            dal_ref[...] = jnp.zeros_like(dal_ref)
            ddt_ref[...] = jnp.zeros_like(ddt_ref)
            dnw_ref[...] = jnp.zeros_like(dnw_ref)
            _start(plan)

        (q, k, v, z, dout), (ga, gb), (al, dt) = _gdn_load((q_ref, k_ref, v_ref, z_ref, do_ref), (ga_ref, gb_ref),
                                                          (al_ref, dt_ref))
        S_in = jnp.concatenate([S_ref[:, cb] for cb in range(GDN_CB)], axis=0)
        t_inv = jnp.concatenate([T_ref[:, cb] for cb in range(GDN_CB)], axis=0)
        prep = lambda *a: _gdn_prep(dkit, *a, t_inv=t_inv)[:6]
        (u, w, qk, q_dec, k_dec, dec), prep_vjp = jax.vjp(prep, q, k, v, ga, gb, al, dt)
        v_new = u - kit.nn(w, S_in)
        o = kit.nn(q_dec, S_in) + kit.nn(qk, v_new)
        _, out_vjp = jax.vjp(_gdn_out, o, z, nw_ref[...])
        do, dz, dnw = out_vjp(dout)
        dvn_part = kit.tn(qk, do)
        dS_part = kit.tn(q_dec, do)
        dS = dS_scr[...]
        dS_out, dvn = [None] * GDN_CB, [None] * GDN_CB
        for cb in reversed(range(GDN_CB)):
            hs = slice(cb * GH, (cb + 1) * GH)
            dS_out[cb] = dS
            dvn[cb] = dvn_part[hs] + kit.nn(k_dec[hs], dS)
            dS = dS * dec[hs] + dS_part[hs] - kit.tn(w[hs], dvn[cb])
        dS_scr[...] = dS
        dS_out = jnp.concatenate(dS_out, axis=0)
        dvn = jnp.concatenate(dvn, axis=0)
        ddec = jnp.sum(jnp.sum(S_in * dS_out, axis=2, keepdims=True), axis=1, keepdims=True)
        cts = (dvn, -kit.nt(dvn, S_in), kit.nt(do, v_new), kit.nt(do, S_in), kit.nt(v_new, dS_out), ddec)
        dq, dk, dv, dga, dgb, dal, ddt = prep_vjp(cts)
        lanesum = lambda t: jnp.broadcast_to(jnp.sum(t, axis=2, keepdims=True), t.shape)
        for cb in range(GDN_CB):
            hs = slice(cb * GH, (cb + 1) * GH)
            sl = pl.ds(cb * CHUNK, CHUNK)
            dqkv_ref[pl.ds(0, GH), sl, :] = dq[hs]
            dqkv_ref[pl.ds(GH, GH), sl, :] = dk[hs]
            dqkv_ref[pl.ds(2 * GH, GH), sl, :] = dv[hs]
            dz_ref[:, sl, :] = dz[hs]
            dga_ref[:, cb] = dga[hs]
            dgb_ref[:, cb] = dgb[hs]
            dal_ref[...] += lanesum(dal[hs])
            ddt_ref[...] += lanesum(ddt[hs])
        dnw_ref[...] += dnw

        @pl.when(pl.program_id(0) == nblk - 1)
        def _():
            _finish(plan)

    res = pl.pallas_call(
        body, name="gdn_bwd", grid=(nblk,),
        in_specs=[rseq(0), rseq(1), rseq(2), rseq(0), rrow(0), rrow(1), per_head, per_head, whole, rstate, rstate,
                  rseq(0)] + _hbm_specs(npc),
        out_specs=[pl.BlockSpec((3 * GH, GDN_CB * CHUNK, HD), lambda i: (0, nblk - 1 - i, 0)), rseq(0), rrow(0),
                   rrow(0), per_head, per_head, whole] + _hbm_specs(npc),
        out_shape=[_sds((3 * GH, T, HD)), _sds((GH + 4 + SWA_GRAD_HEADS, T, HD))] + [_sds((GH, N, 1, CHUNK))] * 2
                  + [_sds((GH, 1, CHUNK))] * 2 + [_sds((1, HD))] + _exchange_shapes(pieces),
        scratch_shapes=[pltpu.VMEM((GH, HD, HD), F32)] + _exchange_sems(npc),
        compiler_params=_cparams(("arbitrary",), GDN_BWD_VMEM),
    )(qkv_hm, qkv_hm, qkv_hm, zs_hm, gab, gab, alog_b, dtb_b, nw, S_all[0], S_all[1], do, *pieces)
    return res[:7], res[7:]


def _swa_heads(kit, first, q, kp, kc, vp, vc, qnw, knw, sink, slope):
    W = WIN
    ri = lax.broadcasted_iota(jnp.int32, (W, W), 0)
    ci = lax.broadcasted_iota(jnp.int32, (W, W), 1)
    mask_c = ri >= ci
    mask_p = ci > ri + first * W
    dist_c = (ri - ci).astype(F32)
    dist_p = (ri - ci + W).astype(F32)
    kpn = _rms(kp, knw)
    kcn = _rms(kc, knw)
    qn = _rms(q, qnw)
    sc = jnp.where(mask_c, kit.nt(qn, kcn) * (HD ** -0.5) - slope * dist_c, -1e30)
    sp = jnp.where(mask_p, kit.nt(qn, kpn) * (HD ** -0.5) - slope * dist_p, -1e30)
    m = jnp.maximum(jnp.maximum(jnp.max(sc, axis=-1, keepdims=True), jnp.max(sp, axis=-1, keepdims=True)), sink)
    m = lax.stop_gradient(m)
    pc = jnp.exp(sc - m)
    pp = jnp.exp(sp - m)
    den = jnp.sum(pc, axis=-1, keepdims=True) + jnp.sum(pp, axis=-1, keepdims=True) + jnp.exp(sink - m)
    inv = 1.0 / den
    return kit.nn(pc * inv, vc) + kit.nn(pp * inv, vp)


def _swa_grads(kit, first, q, kp, kc, vp, vc, qnw, knw, sink, slope, do):
    W = WIN
    ri = lax.broadcasted_iota(jnp.int32, (W, W), 0)
    ci = lax.broadcasted_iota(jnp.int32, (W, W), 1)
    mask_c = ri >= ci
    mask_p = ci > ri + first * W
    dist_c = (ri - ci).astype(F32)
    dist_p = (ri - ci + W).astype(F32)
    scale = HD ** -0.5
    kpn, kp_vjp = jax.vjp(_rms, kp, knw)
    kcn, kc_vjp = jax.vjp(_rms, kc, knw)
    qn, q_vjp = jax.vjp(_rms, q, qnw)
    sc = jnp.where(mask_c, kit.nt(qn, kcn) * scale - slope * dist_c, -1e30)
    sp = jnp.where(mask_p, kit.nt(qn, kpn) * scale - slope * dist_p, -1e30)
    m = jnp.maximum(jnp.maximum(jnp.max(sc, axis=-1, keepdims=True), jnp.max(sp, axis=-1, keepdims=True)), sink)
    ec = jnp.exp(sc - m)
    ep = jnp.exp(sp - m)
    es = jnp.exp(sink - m)
    inv = 1.0 / (jnp.sum(ec, axis=-1, keepdims=True) + jnp.sum(ep, axis=-1, keepdims=True) + es)
    pc, pp = ec * inv, ep * inv
    dpc, dpp = kit.nt(do, vc), kit.nt(do, vp)
    delta = jnp.sum(dpc * pc, axis=-1, keepdims=True) + jnp.sum(dpp * pp, axis=-1, keepdims=True)
    dsc = pc * (dpc - delta) * scale
    dsp = pp * (dpp - delta) * scale
    dq, dqnw = q_vjp(kit.nn(dsc, kcn) + kit.nn(dsp, kpn))
    dkc, dknw_c = kc_vjp(kit.tn(dsc, qn))
    dkp, dknw_p = kp_vjp(kit.tn(dsp, qn))
    return dq, dkp, dkc, kit.tn(pp, do), kit.tn(pc, do), dqnw, dknw_c + dknw_p, -(es * inv) * delta


def _per_query_head(kv_ref):
    return jnp.concatenate([kv_ref[pl.ds(h // SGRP, 1)] for h in range(SQH)], axis=0)


def _per_kv_head(d):
    return jnp.concatenate([jnp.sum(d[g * SGRP:(g + 1) * SGRP], axis=0, keepdims=True) for g in range(SKVH)], axis=0)


def _swa_specs(blk):
    qspec = pl.BlockSpec((SQH, WIN, HD), lambda i: (1, blk(i), 0))
    cur = lambda grp: pl.BlockSpec((SKVH, WIN, HD), lambda i, grp=grp: (grp, blk(i), 0))
    prev = lambda grp: pl.BlockSpec((SKVH, WIN, HD), lambda i, grp=grp: (grp, jnp.maximum(blk(i) - 1, 0), 0))
    whole = pl.BlockSpec((1, HD), lambda i: (0, 0))
    col = pl.BlockSpec((SQH, WIN, 1), lambda i: (0, 0, 0))
    ospec = pl.BlockSpec((SQH, WIN, HD), lambda i: (0, blk(i), 0))
    return qspec, cur, prev, whole, col, ospec


def _swa_fwd(zs_hm, qnw, knw, sinks_col, slopes_col, o_buf, shards):
    T = zs_hm.shape[1]
    NB = T // WIN
    ns = len(shards)
    kit = _Kit(False)
    qspec, cur, prev, whole, col, _ = _swa_specs(lambda i: i)

    def body(*refs):
        q_ref, kp_ref, kc_ref, vp_ref, vc_ref, qnw_ref, knw_ref, s_ref, sl_ref = refs[:9]
        o_ref = refs[10 + ns]
        plan = _gather_plan(refs[10:10 + ns], refs[11 + ns:11 + 2 * ns], *refs[11 + 2 * ns:])

        @pl.when(pl.program_id(0) == 0)
        def _():
            _start(plan)

        first = (pl.program_id(0) == 0).astype(jnp.int32)
        o_ref[...] = _swa_heads(kit, first, q_ref[...], _per_query_head(kp_ref), _per_query_head(kc_ref),
                                _per_query_head(vp_ref), _per_query_head(vc_ref), qnw_ref[...], knw_ref[...],
                                s_ref[...], sl_ref[...])

        @pl.when(pl.program_id(0) == NB - 1)
        def _():
            _finish(plan)

    res = pl.pallas_call(
        body, name="swa_fwd", grid=(NB,),
        in_specs=[qspec, prev(8), cur(8), prev(9), cur(9), whole, whole, col, col] + _hbm_specs(1 + ns),
        out_specs=[pl.BlockSpec((SQH, WIN, HD), lambda i: (1, i, 0))] + _hbm_specs(ns),
        out_shape=[_sds(o_buf.shape)] + _gather_shapes(shards),
        input_output_aliases={9: 0},
        scratch_shapes=_gather_sems(ns),
        compiler_params=_cparams(("arbitrary",)),
    )(zs_hm, zs_hm, zs_hm, zs_hm, zs_hm, qnw, knw, sinks_col, slopes_col, o_buf, *shards)
    return res[0], res[1:]


SWA_GRAD_HEADS = SQH + 2 * SKVH


def _swa_bwd(zs_hm, qnw, knw, sinks_col, slopes_col, dmix_hm, d_buf):
    T = zs_hm.shape[1]
    NB = T // WIN
    kit = _Kit(False)
    qspec, cur, prev, whole, col, _ = _swa_specs(lambda i: NB - 1 - i)

    def body(q_ref, kp_ref, kc_ref, vp_ref, vc_ref, qnw_ref, knw_ref, s_ref, sl_ref, do_ref, buf_ref,
             d_ref, dqnw_ref, dknw_ref, ds_ref, ck_scr, cv_scr):
        dq_ref = d_ref.at[pl.ds(0, SQH)]
        dk_ref = d_ref.at[pl.ds(SQH, SKVH)]
        dv_ref = d_ref.at[pl.ds(SQH + SKVH, SKVH)]
        i = pl.program_id(0)
        first = (i == NB - 1).astype(jnp.int32)

        @pl.when(i == 0)
        def _():
            ck_scr[...] = jnp.zeros_like(ck_scr)
            cv_scr[...] = jnp.zeros_like(cv_scr)
            ds_ref[...] = jnp.zeros_like(ds_ref)
            dqnw_ref[...] = jnp.zeros_like(dqnw_ref)
            dknw_ref[...] = jnp.zeros_like(dknw_ref)

        dq, dkp, dkc, dvp, dvc, dqnw, dknw, dsink = _swa_grads(
            kit, first, q_ref[...], _per_query_head(kp_ref), _per_query_head(kc_ref), _per_query_head(vp_ref),
            _per_query_head(vc_ref), qnw_ref[...], knw_ref[...], s_ref[...], sl_ref[...], do_ref[...])
        dq_ref[...] = dq
        dk_ref[...] = _per_kv_head(dkc) + ck_scr[...]
        dv_ref[...] = _per_kv_head(dvc) + cv_scr[...]
        ck_scr[...] = _per_kv_head(dkp)
        cv_scr[...] = _per_kv_head(dvp)
        dqnw_ref[...] += dqnw
        dknw_ref[...] += dknw
        ds_ref[...] += jnp.broadcast_to(jnp.sum(dsink, axis=1, keepdims=True), dsink.shape)

    dospec = pl.BlockSpec((SQH, WIN, HD), lambda i: (1, NB - 1 - i, 0))
    dspec = pl.BlockSpec((SWA_GRAD_HEADS, WIN, HD), lambda i: (1, NB - 1 - i, 0))
    res = pl.pallas_call(
        body, name="swa_bwd", grid=(NB,),
        in_specs=[qspec, prev(8), cur(8), prev(9), cur(9), whole, whole, col, col, dospec] + _hbm_specs(1),
        out_specs=[dspec, whole, whole, col],
        out_shape=[_sds(d_buf.shape), _sds((1, HD)), _sds((1, HD)), _sds((SQH, WIN, 1))],
        input_output_aliases={10: 0},
        scratch_shapes=[pltpu.VMEM((SKVH, WIN, HD), F32), pltpu.VMEM((SKVH, WIN, HD), F32)],
        compiler_params=_cparams(("arbitrary",)),
    )(zs_hm, zs_hm, zs_hm, zs_hm, zs_hm, qnw, knw, sinks_col, slopes_col, dmix_hm, d_buf)
    return res


GAB0 = 3 * GW + 1280


W_IN_ROWS = PROJ // N_CHIP
W_IN_ROWS_PAD = 736


def _permute_w_in_t(w_in_t):
    return jnp.concatenate([w_in_t[:4 * GW], w_in_t[4 * GW + 2 * GH:], w_in_t[4 * GW:4 * GW + 2 * GH],
                            jnp.zeros((NP - PROJ, D), w_in_t.dtype)], axis=0)


def _w_in_grad_pieces(g_t):
    g = jnp.concatenate([g_t[:4 * GW], g_t[GAB0:GAB0 + 2 * GH], g_t[4 * GW:GAB0]], axis=0)
    g = jnp.pad(g.reshape(N_CHIP, W_IN_ROWS, D), ((0, 0), (0, W_IN_ROWS_PAD - W_IN_ROWS), (0, 0)))
    return g.reshape(N_CHIP, 2, W_IN_ROWS_PAD // 2, D)


def _pieces_by_rows(g):
    return g.reshape(N_CHIP, 2, g.shape[0] // (2 * N_CHIP), D)


def _local_step(x, target, mod, n1w, w_in_pt, conv_w, alog, dtb, gnw, qnw, knw, sinks, n2w, shards):
    sh_out, sh_gate, sh_up, sh_down = shards
    T = x.shape[0]
    N = T // CHUNK
    shift1, scale1, gate1, shift2, scale2, gate2 = [mod[:, i * D:(i + 1) * D] for i in range(6)]

    h = _norm_mod_fwd(x, n1w, scale1, shift1)
    proj, (a_out,) = _matmul(h, w_in_pt, tb=True, name="in_proj", gather=[sh_out])
    w_out = a_out.reshape(D, D)
    half_down = sh_down.shape[0] // 2
    qkv_hm, (a_down0,) = _conv_fwd(proj, conv_w, [sh_down[:half_down]])
    zs_hm = _split_heads(proj, 3 * GW // LANE, 20, "split_zs")
    gab = proj[:, GAB0:GAB0 + 2 * GH].T.reshape(2 * GH, N, 1, CHUNK)
    alog_b = jnp.broadcast_to(alog.reshape(GH, 1, 1), (GH, 1, CHUNK))
    dtb_b = jnp.broadcast_to(dtb.reshape(GH, 1, 1), (GH, 1, CHUNK))
    sinks_col = jnp.broadcast_to(sinks.reshape(SQH, 1, 1), (SQH, WIN, 1))
    o_hm, S_all, (a_gate, a_up) = _gdn_fwd(qkv_hm, zs_hm, gab, alog_b, dtb_b, gnw, [sh_gate, sh_up])
    w_gut = _interleave_gate_up(a_gate.reshape(DFF, D), a_up.reshape(DFF, D))
    slopes = 2.0 ** (-8.0 * (jnp.arange(SQH, dtype=F32) + 1.0) / SQH)
    slopes_col = jnp.broadcast_to(slopes.reshape(SQH, 1, 1), (SQH, WIN, 1))
    o_hm, (a_down1,) = _swa_fwd(zs_hm, qnw, knw, sinks_col, slopes_col, o_hm, [sh_down[half_down:]])
    w_down = jnp.concatenate([a_down0, a_down1], axis=1).reshape(DFF, D)
    mixcat = _merge_heads(o_hm, BF16, "merge_mix")
    mixed, x1, h2 = _out_proj_resid_norm(mixcat, w_out, x, gate1, n2w, scale2, shift2)
    ab, act = _ffn_up_act(h2, w_gut)
    dy, dffn, dgate2, loss = _ffn_down_loss(act, w_down, x1, target, gate2)

    dab = _ffn_down_dx_act(dffn, w_down, ab)
    g_w_down = _matmul(act, dffn, ta=True, out_dtype=BF16, name="ffn_down_dw")
    g_w_gut = _matmul(dab, h2, ta=True, out_dtype=BF16, name="ffn_up_dw")
    dx1, dmixed, dgate1, dn2w, dscale2, dshift2 = _ffn_up_dx_resid_bwd(dab, w_gut, x, mixed, dy, gate1, n2w, scale2,
                                                                       shift2)
    g_w_out = _matmul(mixcat, dmixed, ta=True, out_dtype=BF16, name="out_proj_dw")
    dmix_hm = _split_heads(_matmul(dmixed, w_out, tb=True, name="out_proj_dx"), 0, GH + SQH, "split_dmix")
    g_gate_t, g_up_t = _split_gate_up(g_w_gut)
    pieces = [_pieces_by_rows(g_w_out), _pieces_by_rows(g_gate_t), _pieces_by_rows(g_up_t),
              _pieces_by_rows(g_w_down)]
    (dqkv_hm, d_hm, dga, dgb, dalog, ddtb, dgnw), recv = _gdn_bwd(qkv_hm, zs_hm, gab, alog_b, dtb_b, gnw, S_all,
                                                                  dmix_hm, pieces)
    d_hm, dqnw, dknw, dsinks = _swa_bwd(zs_hm, qnw, knw, sinks_col, slopes_col, dmix_hm, d_hm)
    dproj, dconv = _conv_bwd(proj, conv_w, dqkv_hm)
    dproj = _merge_heads(d_hm, BF16, "merge_dz", into=dproj, col_block0=3 * GW // LANE, head0=0, nheads=GH)
    dproj = _merge_heads(d_hm, BF16, "merge_dswa", into=dproj, col_block0=4 * GW // LANE, head0=GH + 4,
                         nheads=SWA_GRAD_HEADS)
    dgab = jnp.concatenate([dga, dgb], axis=0).reshape(2 * GH, T).T.astype(BF16)
    dproj = lax.dynamic_update_slice(dproj, jnp.concatenate([dgab, jnp.zeros((T, NP - PROJ), BF16)], axis=1),
                                     (0, GAB0))
    g_w_in_pt = _matmul(dproj, h, ta=True, out_dtype=BF16, name="in_proj_dw")
    (grad_x, dn1w, dscale1, dshift1), recv_in = _in_proj_dx_norm_bwd(dproj, w_in_pt, x, dx1, n1w, scale1, shift1,
                                                                     [_w_in_grad_pieces(g_w_in_pt)])

    dmod = jnp.concatenate([dshift1, dscale1, dgate1, dshift2, dscale2, dgate2], axis=1)
    big = list(recv_in) + list(recv)
    small = dict(mod=dmod, norm1_w=dn1w, norm2_w=dn2w, conv_w=dconv, a_log=dalog[:, 0, 0], dt_bias=ddtb[:, 0, 0],
                 gdn_norm_w=dgnw, q_norm_w=dqnw, k_norm_w=dknw, sinks=dsinks[:, 0, 0])
    return loss, grad_x, big, small


def _adamw(w, g, m, v):
    m2 = ADAM_B1 * m + (1.0 - ADAM_B1) * g
    v2 = ADAM_B2 * v + (1.0 - ADAM_B2) * (g * g)
    m_hat = m2 / (1.0 - ADAM_B1 ** ADAM_STEP)
    v_hat = v2 / (1.0 - ADAM_B2 ** ADAM_STEP)
    delta = -ADAM_LR * (m_hat / (jnp.sqrt(v_hat) + ADAM_EPS) + ADAM_WD * w)
    return delta, m2, v2


def _reduce_adamw(recv, w, m, v, name):
    _, R, C = recv.shape
    tc = _tile(C, 256)

    def body(r_ref, w_ref, m_ref, v_ref, o_ref):
        g = r_ref[0].astype(F32)
        for s in range(1, N_DEV):
            g = g + r_ref[s].astype(F32)
        delta, m2, v2 = _adamw(w_ref[...], g, m_ref[...], v_ref[...])
        o_ref[0] = g
        o_ref[1] = delta
        o_ref[2] = m2
        o_ref[3] = v2

    col = pl.BlockSpec((R, tc), lambda j: (0, j))
    return pl.pallas_call(
        body, name=name, grid=(C // tc,),
        in_specs=[pl.BlockSpec((N_DEV, R, tc), lambda j: (0, 0, j)), col, col, col],
        out_specs=pl.BlockSpec((4, R, tc), lambda j: (0, 0, j)),
        out_shape=_sds((4, R, C)),
        compiler_params=_cparams(("parallel",)),
    )(recv, w, m, v)


def _adamw_call(g, w, m, v, name):
    def body(g_ref, w_ref, m_ref, v_ref, o_ref):
        delta, m2, v2 = _adamw(w_ref[...], g_ref[...], m_ref[...], v_ref[...])
        o_ref[0] = delta
        o_ref[1] = m2
        o_ref[2] = v2

    return pl.pallas_call(body, name=name, out_shape=_sds((3,) + g.shape))(g, w, m, v)


ADA_N = 6 * D // N_CHIP
KPAD = 128


def _w_ada_update(c8p, dm, w, m, v):
    tr = 256

    def body(c_ref, dm_ref, w_ref, m_ref, v_ref, g_ref, d_ref, m2_ref, v2_ref):
        g = _raw1(_silu(c_ref[...]), dm_ref[...], _TN)
        delta, m2, v2 = _adamw(w_ref[...], g, m_ref[...], v_ref[...])
        g_ref[...] = g
        d_ref[...] = delta
        m2_ref[...] = m2
        v2_ref[...] = v2

    blk = pl.BlockSpec((tr, ADA_N), lambda i: (i, 0))
    return pl.pallas_call(
        body, name="w_ada_update", grid=(D // tr,),
        in_specs=[pl.BlockSpec((KPAD, tr), lambda i: (0, i)), pl.BlockSpec((KPAD, ADA_N), lambda i: (0, 0)),
                  blk, blk, blk],
        out_specs=[blk] * 4, out_shape=[_sds((D, ADA_N))] * 4,
        compiler_params=_cparams(("parallel",)),
    )(c8p, dm, w, m, v)


def _me():
    return lax.axis_index("x"), lax.axis_index("y"), lax.axis_index("c")


def _peer(k, me):
    mx, my, mc = me
    return (1 - mx if k & 4 else mx, 1 - my if k & 2 else my, 1 - mc if k & 1 else mc)


def _lin(p):
    return 4 * p[0] + 2 * p[1] + p[2]


def _remote(src, dst, ssem, rsem, dev):
    return pltpu.make_async_remote_copy(src_ref=src, dst_ref=dst, send_sem=ssem, recv_sem=rsem,
                                        device_id=dev, device_id_type=MESH)


def _all_gather8(x, name):
    def body(x_ref, out_ref, send_sems, recv_sems):
        me = _me()
        out_ref[_lin(me)] = x_ref[...]
        sends = []
        for k in range(1, N_DEV):
            cp = _remote(x_ref, out_ref.at[_lin(me)], send_sems.at[k - 1], recv_sems.at[k - 1], _peer(k, me))
            cp.start()
            sends.append(cp)
        for k in range(1, N_DEV):
            p = _peer(k, me)
            _remote(x_ref, out_ref.at[_lin(p)], send_sems.at[k - 1], recv_sems.at[k - 1], p).wait_recv()
        for cp in sends:
            cp.wait_send()

    return pl.pallas_call(
        body, name=name,
        out_shape=_sds((N_DEV,) + x.shape, x.dtype),
        in_specs=[pl.BlockSpec(memory_space=pltpu.VMEM)],
        out_specs=pl.BlockSpec(memory_space=pltpu.VMEM),
        scratch_shapes=[pltpu.SemaphoreType.DMA((N_DEV - 1,)), pltpu.SemaphoreType.DMA((N_DEV - 1,))],
    )(x)


def _ag8_plan(src, out, send_sems, recv_sems):
    me = _me()
    sends, recvs = [], []
    for k in range(1, N_DEV):
        p = _peer(k, me)
        sends.append(_remote(src, out.at[_lin(me)], send_sems.at[k - 1], recv_sems.at[k - 1], p))
        recvs.append(_remote(src, out.at[_lin(p)], send_sems.at[k - 1], recv_sems.at[k - 1], p))
    return [], sends, recvs


def _prologue(c_row, conv_sh, w_ada, b_sh, w_in_sh):
    def body(c_ref, cv_ref, wa_ref, b_ref, win_ref, call_ref, cvall_ref, mods_ref, ain_ref, c16_scr, mp_scr,
             c_send, c_recv, cv_send, cv_recv, m_send, m_recv, w_send, w_recv, w_local):
        me = _lin(_me())
        w_plan = _gather_plan([win_ref], [ain_ref], w_send, w_recv, w_local)
        _start(w_plan)
        c_plan = _ag8_plan(c_ref, call_ref, c_send, c_recv)
        cv_plan = _ag8_plan(cv_ref, cvall_ref, cv_send, cv_recv)
        call_ref[me] = c_ref[...]
        cvall_ref[me] = cv_ref[...]
        _start(c_plan)
        _start(cv_plan)
        _finish(c_plan)
        c16_scr[...] = jnp.zeros_like(c16_scr)
        for d in range(N_DEV):
            c16_scr[pl.ds(d, 1), :] = call_ref[d]
        mp_scr[...] = _raw1(_silu(c16_scr[...]), wa_ref[...], _NN) + b_ref[...]
        mods_ref[me] = mp_scr[...]
        m_plan = _ag8_plan(mp_scr, mods_ref, m_send, m_recv)
        _start(m_plan)
        _finish(cv_plan)
        _finish(m_plan)
        _finish(w_plan)

    vmem = pl.BlockSpec(memory_space=pltpu.VMEM)
    sems = lambda n: pltpu.SemaphoreType.DMA((n,))
    return pl.pallas_call(
        body, name="prologue",
        in_specs=[vmem] * 4 + _hbm_specs(1), out_specs=[vmem] * 3 + _hbm_specs(1),
        out_shape=[_sds((N_DEV,) + c_row.shape), _sds((N_DEV,) + conv_sh.shape), _sds((N_DEV, 16, ADA_N)),
                   _sds((N_CHIP,) + w_in_sh.shape, w_in_sh.dtype)],
        scratch_shapes=[pltpu.VMEM((16, D), F32), pltpu.VMEM((16, ADA_N), F32)] + [sems(N_DEV - 1)] * 6
                       + _gather_sems(1),
        compiler_params=_cparams(),
    )(c_row, conv_sh, w_ada, b_sh, w_in_sh)


def _hbm_specs(n):
    return [pl.BlockSpec(memory_space=pl.ANY)] * n


def _gather_shapes(shards):
    return [_sds((N_CHIP,) + s.shape, s.dtype) for s in shards]


def _gather_sems(n):
    return [pltpu.SemaphoreType.DMA((3 * n,)), pltpu.SemaphoreType.DMA((3 * n,)), pltpu.SemaphoreType.DMA((n,))]


def _gather_plan(ins, outs, send_sems, recv_sems, local_sems):
    mx, my, mc = _me()
    chips = [(1 - mx, my), (mx, 1 - my), (1 - mx, 1 - my)]
    local, sends, recvs = [], [], []
    for a in range(len(ins)):
        local.append(pltpu.make_async_copy(ins[a], outs[a].at[2 * mx + my], local_sems.at[a]))
        for k, (px, py) in enumerate(chips):
            sems = (send_sems.at[3 * a + k], recv_sems.at[3 * a + k], (px, py, mc))
            sends.append(_remote(ins[a], outs[a].at[2 * mx + my], *sems))
            recvs.append(_remote(ins[a], outs[a].at[2 * px + py], *sems))
    return local, sends, recvs


def _start(plan):
    local, sends, _ = plan
    for cp in local + sends:
        cp.start()


def _finish(plan):
    local, sends, recvs = plan
    for cp in recvs:
        cp.wait_recv()
    for cp in sends:
        cp.wait_send()
    for cp in local:
        cp.wait()


def _exchange_shapes(pieces):
    return [_sds((N_DEV,) + p.shape[2:], p.dtype) for p in pieces]


def _exchange_sems(n):
    return [pltpu.SemaphoreType.DMA(((N_DEV - 1) * n,)), pltpu.SemaphoreType.DMA(((N_DEV - 1) * n,)),
            pltpu.SemaphoreType.DMA((n,))]


def _exchange_plan(ins, outs, send_sems, recv_sems, local_sems):
    me = _me()
    mx, my, mc = me
    local, sends, recvs = [], [], []
    for a in range(len(ins)):
        local.append(pltpu.make_async_copy(ins[a].at[2 * mx + my, mc], outs[a].at[_lin(me)], local_sems.at[a]))
        for k in range(1, N_DEV):
            p = _peer(k, me)
            s = (N_DEV - 1) * a + k - 1
            sends.append(_remote(ins[a].at[2 * p[0] + p[1], p[2]], outs[a].at[_lin(me)], send_sems.at[s],
                                 recv_sems.at[s], p))
            recvs.append(_remote(ins[a].at[2 * mx + my, mc], outs[a].at[_lin(p)], send_sems.at[s],
                                 recv_sems.at[s], p))
    return local, sends, recvs


REDUCE_VMEM = 56 * 1024 * 1024


def _reduce_swap(recvs):
    n = len(recvs)

    def body(*refs):
        r_refs, o_refs = refs[:n], refs[n:2 * n]
        send_sems, recv_sems = refs[2 * n:]
        mx, my, mc = _me()
        sib = (mx, my, 1 - mc)
        half = lambda a, c: o_refs[a].at[pl.ds(pl.multiple_of(c * recvs[a].shape[1], 8), recvs[a].shape[1])]
        sends = []
        for a in range(n):
            g = r_refs[a][0].astype(F32)
            for s in range(1, N_DEV):
                g = g + r_refs[a][s].astype(F32)
            half(a, mc)[...] = g
            cp = _remote(half(a, mc), half(a, mc), send_sems.at[a], recv_sems.at[a], sib)
            cp.start()
            sends.append(cp)
        for a in range(n):
            _remote(half(a, mc), half(a, 1 - mc), send_sems.at[a], recv_sems.at[a], sib).wait_recv()
        for cp in sends:
            cp.wait_send()

    vmem = pl.BlockSpec(memory_space=pltpu.VMEM)
    return pl.pallas_call(
        body, name="reduce_swap", out_shape=[_sds((2 * r.shape[1], r.shape[2])) for r in recvs],
        in_specs=[vmem] * n, out_specs=[vmem] * n,
        scratch_shapes=[pltpu.SemaphoreType.DMA((n,)), pltpu.SemaphoreType.DMA((n,))],
        compiler_params=_cparams(None, REDUCE_VMEM),
    )(*recvs)


def _adamw_big(g, w, m, v, name):
    rows, cols = g.shape
    tr = next((t for t in (256, 176, 128, 64, 8) if rows % t == 0), None)
    if tr is None:
        tc = _tile(cols, 256)
        blk, grid = pl.BlockSpec((rows, tc), lambda i: (0, i)), (cols // tc,)
    else:
        blk, grid = pl.BlockSpec((tr, cols), lambda i: (i, 0)), (rows // tr,)

    def body(g_ref, w_ref, m_ref, v_ref, go_ref, d_ref, m2_ref, v2_ref):
        g = g_ref[...]
        delta, m2, v2 = _adamw(w_ref[...], g, m_ref[...], v_ref[...])
        go_ref[...] = g
        d_ref[...] = delta
        m2_ref[...] = m2
        v2_ref[...] = v2

    return pl.pallas_call(
        body, name=name, grid=grid,
        in_specs=[blk] * 4, out_specs=[blk] * 4, out_shape=[_sds((rows, cols))] * 4,
        compiler_params=_cparams(("parallel",)),
    )(g, w, m, v)


SMALL_ORDER = (("mod", 6 * D), ("norm1_w", D), ("norm2_w", D), ("conv_w", CONVW * 3 * GW), ("a_log", GH),
               ("dt_bias", GH), ("gdn_norm_w", HD), ("q_norm_w", HD), ("k_norm_w", HD), ("sinks", SQH), ("loss", 1))
SMALL_R = 120


def _pack_small(d):
    parts = [d[k].reshape(-1).astype(F32) if k in d else jnp.zeros((n,), F32) for k, n in SMALL_ORDER]
    used = sum(n for _, n in SMALL_ORDER)
    parts.append(jnp.zeros((SMALL_R * LANE - used,), F32))
    return jnp.concatenate(parts).reshape(SMALL_R, LANE)


def _unpack_small(pk):
    flat = pk.reshape(-1)
    out, r = {}, 0
    for k, n in SMALL_ORDER:
        out[k] = flat[r:r + n]
        r += n
    return out


def kernel(x, c, w_ada, b_ada, norm1_w, w_in, conv_w, a_log, dt_bias, gdn_norm_w, q_norm_w, k_norm_w, sinks, w_out, norm2_w, w_gate, w_up, w_down, loss_target, m_w_ada, m_b_ada, m_norm1_w, m_w_in, m_conv_w, m_a_log, m_dt_bias, m_gdn_norm_w, m_q_norm_w, m_k_norm_w, m_sinks, m_w_out, m_norm2_w, m_w_gate, m_w_up, m_w_down, v_w_ada, v_b_ada, v_norm1_w, v_w_in, v_conv_w, v_a_log, v_dt_bias, v_gdn_norm_w, v_q_norm_w, v_k_norm_w, v_sinks, v_w_out, v_norm2_w, v_w_gate, v_w_up, v_w_down):
    mx, my, mc = _me()
    chip = 2 * mx + my
    dev = 4 * mx + 2 * my + mc
    T = x.shape[1]

    as_rows = lambda t, transposed: t[0].T if transposed else t[0]
    transposed = (True, False, True, True, False)
    big_w = [as_rows(t, tr) for t, tr in zip((w_in, w_out, w_gate, w_up, w_down), transposed)]
    shards = [t.astype(BF16) for t in big_w]

    b_sh = lax.dynamic_slice(b_ada, (0, chip * ADA_N), (1, ADA_N))
    c_all, conv_all, mods, a_in = _prologue(c, conv_w.reshape(CONVW, 3 * GW // N_CHIP), w_ada[0], b_sh, shards[0])
    c8 = c_all.reshape(N_DEV, D)
    conv_full = jnp.concatenate([conv_all[2 * j] for j in range(N_CHIP)], axis=1)
    mod = jnp.concatenate([lax.dynamic_slice(mods[2 * j], (dev, 0), (1, ADA_N)) for j in range(N_CHIP)], axis=1)
    w_in_pt = _permute_w_in_t(a_in.reshape(PROJ, D))

    loss, grad_x, big, small = _local_step(
        x[0], loss_target[0], mod, norm1_w, w_in_pt, conv_full, a_log, dt_bias, gdn_norm_w,
        q_norm_w, k_norm_w, sinks, norm2_w, shards[1:])

    small["loss"] = loss[:, :1]
    sg = _all_gather8(_pack_small(small), "gather_small_grads")
    rep = dict(mod=(b_ada, m_b_ada, v_b_ada), norm1_w=(norm1_w, m_norm1_w, v_norm1_w),
               norm2_w=(norm2_w, m_norm2_w, v_norm2_w), a_log=(a_log, m_a_log, v_a_log),
               dt_bias=(dt_bias, m_dt_bias, v_dt_bias), gdn_norm_w=(gdn_norm_w, m_gdn_norm_w, v_gdn_norm_w),
               q_norm_w=(q_norm_w, m_q_norm_w, v_q_norm_w), k_norm_w=(k_norm_w, m_k_norm_w, v_k_norm_w),
               sinks=(sinks, m_sinks, v_sinks))
    wmv = [_pack_small({k: t[i] for k, t in rep.items()}) for i in range(3)]
    sres = _reduce_adamw(sg, wmv[0], wmv[1], wmv[2], "small_reduce_adamw")
    s_g, s_d, s_m, s_v = [_unpack_small(sres[i]) for i in range(4)]
    loss_out = s_g["loss"][0]

    g_conv = lax.dynamic_slice(s_g["conv_w"].reshape(CONVW, 3 * GW), (0, chip * (3 * GW // N_CHIP)),
                               (CONVW, 3 * GW // N_CHIP))
    pad16 = lambda t: jnp.concatenate([t.reshape(12, LANE), jnp.zeros((4, LANE), F32)], axis=0)
    cres = _adamw_call(pad16(g_conv), pad16(conv_w), pad16(m_conv_w), pad16(v_conv_w), "conv_adamw")
    conv_out = [g_conv.reshape(conv_w.shape)] + [cres[i, :12].reshape(conv_w.shape) for i in range(3)]

    dmod8 = sg[:, :6 * D // LANE].reshape(N_DEV, 6 * D)
    dm = lax.dynamic_slice(dmod8, (0, chip * ADA_N), (N_DEV, ADA_N))
    zpad = lambda t: jnp.concatenate([t, jnp.zeros((KPAD - N_DEV, t.shape[1]), F32)], axis=0)
    ares = _w_ada_update(zpad(c8), zpad(dm), w_ada[0], m_w_ada[0], v_w_ada[0])

    names = ("w_in", "w_out", "w_gate", "w_up", "w_down")
    g_full = list(_reduce_swap(big))
    g_full[0] = g_full[0][:W_IN_ROWS]
    big_m = [as_rows(t, tr) for t, tr in zip((m_w_in, m_w_out, m_w_gate, m_w_up, m_w_down), transposed)]
    big_v = [as_rows(t, tr) for t, tr in zip((v_w_in, v_w_out, v_w_gate, v_w_up, v_w_down), transposed)]
    upd = [_adamw_big(g, w, m, v, "adamw_" + nm) for g, w, m, v, nm in zip(g_full, big_w, big_m, big_v, names)]
    back = lambda t, tr: (t.T if tr else t)[None]
    bg, bd, bm, bv = [[back(u[i], tr) for u, tr in zip(upd, transposed)] for i in range(4)]

    def group(a_i, small_d, conv_i, big_l):
        s = lambda k, ref: small_d[k].reshape(ref.shape)
        return [ares[a_i][None], s("mod", b_ada), s("norm1_w", norm1_w), big_l[0], conv_out[conv_i],
                s("a_log", a_log), s("dt_bias", dt_bias), s("gdn_norm_w", gdn_norm_w), s("q_norm_w", q_norm_w),
                s("k_norm_w", k_norm_w), s("sinks", sinks), big_l[1], s("norm2_w", norm2_w), big_l[2], big_l[3],
                big_l[4]]

    outs = [loss_out, grad_x[None]]
    outs += group(0, s_g, 0, bg) + group(1, s_d, 1, bd) + group(2, s_m, 2, bm) + group(3, s_v, 3, bv)
    return tuple(outs)
```

```python
import jax
import jax.numpy as jnp
from jax import lax
from jax.experimental import pallas as pl
from jax.experimental.pallas import tpu as pltpu

F32 = jnp.float32
BF16 = jnp.bfloat16
MESH = pl.DeviceIdType.MESH

D = 1024
HD = 64
GH = 8
GW = GH * HD
SQH = 8
SKVH = 2
SGRP = SQH // SKVH
WIN = 128
CONVW = 4
CHUNK = 64
DFF = 2816
PROJ = 2832
NP = 3072
EPS = 1e-6
N_DEV = 8
N_CHIP = 4

ADAM_LR = 0.001
ADAM_B1 = 0.9
ADAM_B2 = 0.999
ADAM_EPS = 1e-08
ADAM_WD = 0.01
ADAM_STEP = 10

VMEM_LIMIT = 48 * 1024 * 1024
GDN_BWD_VMEM = 58 * 1024 * 1024
LANE = 128


def _cparams(sem=None, vmem=VMEM_LIMIT):
    return pltpu.CompilerParams(dimension_semantics=sem, vmem_limit_bytes=vmem)


_NN = ((1,), (0,))
_NT = ((1,), (1,))
_TN = ((0,), (0,))


def _dot(a, b, dims):
    if a.ndim == 3:
        (ca,), (cb,) = dims
        return lax.dot_general(a, b, (((ca + 1,), (cb + 1,)), ((0,), (0,))), preferred_element_type=F32)
    return lax.dot_general(a, b, (dims, ((), ())), preferred_element_type=F32)


def _raw1(a, b, dims):
    return _dot(a.astype(BF16), b.astype(BF16), dims)


def _raw3(a, b, dims):
    ah = a.astype(BF16)
    al = (a - ah.astype(F32)).astype(BF16)
    bh = b.astype(BF16)
    bl = (b - bh.astype(F32)).astype(BF16)
    return _dot(ah, bh, dims) + (_dot(al, bh, dims) + _dot(ah, bl, dims))


def _make_diff_mm(raw):
    @jax.custom_vjp
    def nn(a, b):
        return raw(a, b, _NN)

    @jax.custom_vjp
    def nt(a, b):
        return raw(a, b, _NT)

    @jax.custom_vjp
    def tn(a, b):
        return raw(a, b, _TN)

    nn.defvjp(lambda a, b: (raw(a, b, _NN), (a, b)), lambda r, g: (nt(g, r[1]), tn(r[0], g)))
    nt.defvjp(lambda a, b: (raw(a, b, _NT), (a, b)), lambda r, g: (nn(g, r[1]), tn(g, r[0])))
    tn.defvjp(lambda a, b: (raw(a, b, _TN), (a, b)), lambda r, g: (nt(r[1], g), nn(r[0], g)))
    return nn, nt, tn


def _tri_inv_raw(a, nn3):
    n = a.shape[-1]
    ri = lax.broadcasted_iota(jnp.int32, (n, n), 0)
    ci = lax.broadcasted_iota(jnp.int32, (n, n), 1)
    t = (ri == ci).astype(F32)
    for lvl in range((n - 1).bit_length()):
        same_pair = (ri >> (lvl + 1)) == (ci >> (lvl + 1))
        lower_left = (((ri >> lvl) & 1) == 1) & (((ci >> lvl) & 1) == 0)
        y = jnp.where(same_pair & lower_left, a, 0.0)
        t = t - y if lvl == 0 else t - nn3(nn3(t, y), t)
    return t


class _Kit:
    def __init__(self, diff):
        if diff:
            self.nn, self.nt, self.tn = _make_diff_mm(_raw1)
            self.nn3, self.nt3, self.tn3 = _make_diff_mm(_raw3)
            nn3, nt3, tn3 = self.nn3, self.nt3, self.tn3

            @jax.custom_vjp
            def inv(a, t):
                return t

            def inv_fwd(a, t):
                return t, t

            def inv_bwd(t, g):
                return -tn3(t, nt3(g, t)), jnp.zeros_like(t)

            inv.defvjp(inv_fwd, inv_bwd)
            self.inv = inv
        else:
            self.nn = lambda a, b: _raw1(a, b, _NN)
            self.nt = lambda a, b: _raw1(a, b, _NT)
            self.tn = lambda a, b: _raw1(a, b, _TN)
            self.nn3 = lambda a, b: _raw3(a, b, _NN)
            self.nt3 = lambda a, b: _raw3(a, b, _NT)
            self.tn3 = lambda a, b: _raw3(a, b, _TN)
            self.inv = lambda a, t: _tri_inv_raw(a, self.nn3) if t is None else t


def _sigmoid(x):
    return 1.0 / (1.0 + jnp.exp(-x))


def _silu(x):
    return x * _sigmoid(x)


def _rms(x, w):
    return x * lax.rsqrt(jnp.mean(x * x, axis=-1, keepdims=True) + EPS) * w


def _tile(dim, target):
    t = (min(dim, target) // LANE) * LANE
    while t >= LANE:
        if dim % t == 0:
            return t
        t -= LANE
    return dim


MM_TM, MM_TN, MM_TK = 1408, 1536, 1408


def _matmul(a, b, ta=False, tb=False, out_dtype=F32, name="matmul", gather=None, exchange=None):
    carried = gather if gather is not None else exchange if exchange is not None else []
    nc = len(carried)
    if ta:
        K, M = a.shape
    else:
        M, K = a.shape
    if tb:
        N, K2 = b.shape
    else:
        K2, N = b.shape
    assert K == K2, (a.shape, b.shape, ta, tb)
    tm, tn, tk = _tile(M, MM_TM), _tile(N, MM_TN), _tile(K, MM_TK)
    nk = K // tk
    dims = ((0,) if ta else (1,), (1,) if tb else (0,))

    grid = (M // tm, N // tn, nk)

    def body(*refs):
        a_ref, b_ref = refs[:2]
        o_ref = refs[2 + nc]
        scratch = refs[3 + 2 * nc:]
        k = pl.program_id(2)
        if nc:
            make_plan = _gather_plan if gather is not None else _exchange_plan
            plan = make_plan(refs[2:2 + nc], refs[3 + nc:3 + 2 * nc], *scratch[-3:])
            at = lambda pos: ((pl.program_id(0) == pos[0]) & (pl.program_id(1) == pos[1]) & (k == pos[2]))

            @pl.when(at((0, 0, 0)))
            def _():
                _start(plan)

        part = _dot(a_ref[...].astype(BF16), b_ref[...].astype(BF16), dims)
        if nk == 1:
            o_ref[...] = part.astype(o_ref.dtype)
        else:
            acc_ref = scratch[0]

            @pl.when(k == 0)
            def _():
                acc_ref[...] = part

            @pl.when((k > 0) & (k < nk - 1))
            def _():
                acc_ref[...] += part

            @pl.when(k == nk - 1)
            def _():
                o_ref[...] = (acc_ref[...] + part).astype(o_ref.dtype)

        if nc:
            @pl.when(at((grid[0] - 1, grid[1] - 1, nk - 1)))
            def _():
                _finish(plan)

    a_spec = (pl.BlockSpec((tk, tm), lambda i, j, k: (k, i)) if ta
              else pl.BlockSpec((tm, tk), lambda i, j, k: (i, k)))
    b_spec = (pl.BlockSpec((tn, tk), lambda i, j, k: (j, k)) if tb
              else pl.BlockSpec((tk, tn), lambda i, j, k: (k, j)))
    if gather is not None:
        c_shapes, c_sems = _gather_shapes(carried), _gather_sems(nc)
    elif exchange is not None:
        c_shapes, c_sems = _exchange_shapes(carried), _exchange_sems(nc)
    else:
        c_shapes, c_sems = [], []
    res = pl.pallas_call(
        body, name=name, grid=grid,
        in_specs=[a_spec, b_spec] + _hbm_specs(nc),
        out_specs=[pl.BlockSpec((tm, tn), lambda i, j, k: (i, j))] + _hbm_specs(nc),
        out_shape=[jax.ShapeDtypeStruct((M, N), out_dtype)] + c_shapes,
        scratch_shapes=([pltpu.VMEM((tm, tn), F32)] if nk > 1 else []) + c_sems,
        compiler_params=_cparams(("arbitrary",) * 3 if nc else ("parallel", "parallel", "arbitrary")),
    )(a, b, *carried)
    return (res[0], res[1:]) if nc else res[0]


def _sds(shape, dtype=F32):
    return jax.ShapeDtypeStruct(shape, dtype)


def _norm_mod(x, nw, scale, shift):
    return _rms(x, nw) * (1.0 + scale) + shift


def _norm_in_proj(x, nw, scale, shift, w_in_pt, shards):
    T = x.shape[0]
    N = w_in_pt.shape[0]
    tm, tn = _tile(T, 1024), _tile(N, MM_TN)
    nm, nn = T // tm, N // tn
    ns = len(shards)

    def body(*refs):
        x_ref, nw_ref, sc_ref, sh_ref, w_ref = refs[:5]
        h_ref, o_ref = refs[5 + ns:7 + ns]
        plan = _gather_plan(refs[5:5 + ns], refs[7 + ns:7 + 2 * ns], *refs[7 + 2 * ns:])
        i, j = pl.program_id(0), pl.program_id(1)

        @pl.when((i == 0) & (j == 0))
        def _():
            _start(plan)

        @pl.when(j == 0)
        def _():
            for r0 in range(0, tm, ROWS_EPI):
                rows = pl.ds(r0, ROWS_EPI)
                h_ref[rows, :] = _norm_mod(x_ref[rows, :], nw_ref[...], sc_ref[...], sh_ref[...]).astype(BF16)

        o_ref[...] = _dot(h_ref[...], w_ref[...], _NT)

        @pl.when((i == nm - 1) & (j == nn - 1))
        def _():
            _finish(plan)

    vec = pl.BlockSpec((1, D), lambda i, j: (0, 0))
    res = pl.pallas_call(
        body, name="norm1_in_proj", grid=(nm, nn),
        in_specs=[pl.BlockSpec((tm, D), lambda i, j: (i, 0)), vec, vec, vec,
                  pl.BlockSpec((tn, D), lambda i, j: (j, 0))] + _hbm_specs(ns),
        out_specs=[pl.BlockSpec((tm, D), lambda i, j: (i, 0)), pl.BlockSpec((tm, tn), lambda i, j: (i, j))]
                  + _hbm_specs(ns),
        out_shape=[_sds((T, D), BF16), _sds((T, N))] + _gather_shapes(shards),
        scratch_shapes=_gather_sems(ns),
        compiler_params=_cparams(("arbitrary", "arbitrary")),
    )(x, nw, scale, shift, w_in_pt, *shards)
    return res[0], res[1], res[2:]


ROWS_TM = 1024
ROWS_EPI = 256
ROWS_VMEM = 58 * 1024 * 1024


def _matmul_rows(a, b, epi, tiled, consts, out_tiled, out_acc, name, pieces=(), rows=ROWS_TM):
    T, K = a.shape
    tm, tk = _tile(T, rows), _tile(K, MM_TK)
    nm, nk = T // tm, K // tk
    npc, nt, ncst, no, na = len(pieces), len(tiled), len(consts), len(out_tiled), len(out_acc)
    n_in = 2 + nt + ncst

    def body(*refs):
        a_ref, b_ref = refs[:2]
        t_refs, c_refs = refs[2:2 + nt], refs[2 + nt:n_in]
        o_refs = refs[n_in + npc:n_in + npc + no]
        acc_refs = refs[n_in + npc + no:n_in + npc + no + na]
        n_out = no + na + npc
        res_ref = refs[n_in + npc + n_out]
        plan = _exchange_plan(refs[n_in:n_in + npc], refs[n_in + npc + no + na:n_in + npc + n_out],
                              *refs[n_in + npc + n_out + 1:]) if npc else None
        i, k = pl.program_id(0), pl.program_id(1)

        @pl.when((i == 0) & (k == 0))
        def _():
            for r in acc_refs:
                r[...] = jnp.zeros_like(r)
            if npc:
                _start(plan)

        part = _dot(a_ref[...], b_ref[...], _NN)

        @pl.when(k == 0)
        def _():
            res_ref[...] = part

        @pl.when(k > 0)
        def _():
            res_ref[...] += part

        @pl.when(k == nk - 1)
        def _():
            for r0 in range(0, tm, ROWS_EPI):
                rows = pl.ds(r0, ROWS_EPI)
                outs = epi(res_ref[rows, :], *[r[rows, :] for r in t_refs], *[r[...] for r in c_refs])
                for r, v in zip(o_refs, outs[:no]):
                    r[rows, :] = v.astype(r.dtype)
                for r, v in zip(acc_refs, outs[no:]):
                    r[...] += v

        if npc:
            @pl.when((i == nm - 1) & (k == nk - 1))
            def _():
                _finish(plan)

    row = lambda w: pl.BlockSpec((tm, w), lambda i, k: (i, 0))
    whole = lambda s: pl.BlockSpec(s.shape, lambda i, k: (0, 0))
    res = pl.pallas_call(
        body, name=name, grid=(nm, nk),
        in_specs=[pl.BlockSpec((tm, tk), lambda i, k: (i, k)), pl.BlockSpec((tk, D), lambda i, k: (k, 0))]
                 + [row(t.shape[1]) for t in tiled] + [whole(c) for c in consts] + _hbm_specs(npc),
        out_specs=[row(s.shape[1]) for s in out_tiled] + [whole(s) for s in out_acc] + _hbm_specs(npc),
        out_shape=list(out_tiled) + list(out_acc) + (_exchange_shapes(pieces) if npc else []),
        scratch_shapes=[pltpu.VMEM((tm, D), F32)] + (_exchange_sems(npc) if npc else []),
        compiler_params=_cparams(("arbitrary", "arbitrary"), ROWS_VMEM),
    )(a, b, *tiled, *consts, *pieces)
    return res[:no + na], res[no + na:]


def _in_proj_dx_norm_bwd(dproj, w_in_pt, x, dres, nw, scale, shift, pieces):
    T = x.shape[0]

    def epi(dh, x, dres, nw, scale, shift):
        _, vjp = jax.vjp(_norm_mod, x, nw, scale, shift)
        dx, dnw, dsc, dsh = vjp(dh)
        return dx + dres, dnw, dsc, dsh

    return _matmul_rows(dproj, w_in_pt, epi, [x, dres], [nw, scale, shift], [_sds((T, D))], [_sds((1, D))] * 3,
                        "in_proj_dx_norm1_bwd", pieces)


def _out_proj_resid_norm(mixcat, w_out, x, gate1, nw, scale, shift):
    T = x.shape[0]

    def epi(mixed, x, gate1, nw, scale, shift):
        return (mixed,) + _resid_norm(x, mixed, gate1, nw, scale, shift)

    outs, _ = _matmul_rows(mixcat, w_out, epi, [x], [gate1, nw, scale, shift],
                           [_sds((T, D)), _sds((T, D)), _sds((T, D), BF16)], [], "out_proj_resid_norm2")
    return outs


def _ffn_up_dx_resid_bwd(dab, w_gut, x, mixed, dy, gate1, nw, scale, shift):
    T = x.shape[0]

    def epi(dh2, x, mixed, dy, gate1, nw, scale, shift):
        _, vjp = jax.vjp(_resid_norm, x, mixed, gate1, nw, scale, shift)
        return vjp((dy, dh2))

    outs, _ = _matmul_rows(dab, w_gut, epi, [x, mixed, dy], [gate1, nw, scale, shift],
                           [_sds((T, D)), _sds((T, D), BF16)], [_sds((1, D))] * 4, "ffn_up_dx_resid_norm2_bwd",
                           rows=ROWS_TM // 2)
    return outs


def _ffn_down_loss(act, w_down, x1, target, gate2):
    T = x1.shape[0]

    def epi(ffn, x1, target, gate2):
        y = x1 + gate2 * ffn
        err = y - target
        loss = 0.5 * jnp.sum(jnp.sum(err * err, axis=1, keepdims=True), axis=0, keepdims=True) / D
        dy = err * (1.0 / D)
        return dy, gate2 * dy, jnp.sum(dy * ffn, axis=0, keepdims=True), jnp.broadcast_to(loss, (1, LANE))

    outs, _ = _matmul_rows(act, w_down, epi, [x1, target], [gate2], [_sds((T, D)), _sds((T, D), BF16)],
                           [_sds((1, D)), _sds((1, LANE))], "ffn_down_loss")
    return outs


def _resid_norm(x, mixed, gate1, nw, scale, shift):
    x1 = x + gate1 * mixed
    return x1, _norm_mod(x1, nw, scale, shift)


FFN_BLK = 256
FFN_TM = 2048


def _interleave_gate_up(gate_t, up_t):
    blocks = lambda t: t.reshape(DFF // FFN_BLK, 1, FFN_BLK, D)
    return jnp.concatenate([blocks(gate_t), blocks(up_t)], axis=1).reshape(2 * DFF, D)


def _split_gate_up(g):
    g = g.reshape(DFF // FFN_BLK, 2, FFN_BLK, D)
    return g[:, 0].reshape(DFF, D), g[:, 1].reshape(DFF, D)


def _ffn_up_act(h2, w_gut):
    T = h2.shape[0]
    tm = _tile(T, FFN_TM)

    def body(h_ref, w_ref, ab_ref, act_ref):
        ab = _dot(h_ref[...], w_ref[...], _NT)
        ab_ref[...] = ab
        act_ref[...] = (_silu(ab[:, :FFN_BLK]) * ab[:, FFN_BLK:]).astype(act_ref.dtype)

    return pl.pallas_call(
        body, name="ffn_up_act", grid=(T // tm, DFF // FFN_BLK),
        in_specs=[pl.BlockSpec((tm, D), lambda i, j: (i, 0)), pl.BlockSpec((2 * FFN_BLK, D), lambda i, j: (j, 0))],
        out_specs=[pl.BlockSpec((tm, 2 * FFN_BLK), lambda i, j: (i, j)), pl.BlockSpec((tm, FFN_BLK), lambda i, j: (i, j))],
        out_shape=[_sds((T, 2 * DFF)), _sds((T, DFF), BF16)],
        compiler_params=_cparams(("parallel", "parallel")),
    )(h2, w_gut)


def _ffn_down_dx_act(dffn, w_down, ab):
    T = dffn.shape[0]
    tm = _tile(T, FFN_TM)

    def body(d_ref, w_ref, ab_ref, o_ref):
        dact = _dot(d_ref[...], w_ref[...], _NT)
        a, b = ab_ref[:, :FFN_BLK], ab_ref[:, FFN_BLK:]
        s = _sigmoid(a)
        da = dact * b * (s * (1.0 + a * (1.0 - s)))
        db = dact * (a * s)
        o_ref[...] = jnp.concatenate([da, db], axis=1).astype(o_ref.dtype)

    return pl.pallas_call(
        body, name="ffn_down_dx_act", grid=(T // tm, DFF // FFN_BLK),
        in_specs=[pl.BlockSpec((tm, D), lambda i, j: (i, 0)), pl.BlockSpec((FFN_BLK, D), lambda i, j: (j, 0)),
                  pl.BlockSpec((tm, 2 * FFN_BLK), lambda i, j: (i, j))],
        out_specs=pl.BlockSpec((tm, 2 * FFN_BLK), lambda i, j: (i, j)),
        out_shape=_sds((T, 2 * DFF), BF16),
        compiler_params=_cparams(("parallel", "parallel")),
    )(dffn, w_down, ab)


def _round_bf16(x):
    return x.astype(BF16).astype(F32)


def _shift_down(x, s, rows):
    if s == 0:
        return x
    return jnp.where(rows >= s, pltpu.roll(x, s, 0), 0.0)


def _shift_up(x, s, rows, T):
    if s == 0:
        return x
    return jnp.where(rows < T - s, pltpu.roll(x, T - s, 0), 0.0)


def _conv_fwd(proj, conv_w, shards):
    T = proj.shape[0]
    ncol = 3 * GW // LANE
    ns = len(shards)

    def body(*refs):
        x_ref, w_ref = refs[:2]
        o_ref = refs[2 + ns]
        plan = _gather_plan(refs[2:2 + ns], refs[3 + ns:3 + 2 * ns], *refs[3 + 2 * ns:])

        @pl.when(pl.program_id(0) == 0)
        def _():
            _start(plan)

        x = _round_bf16(x_ref[...])
        rows = lax.broadcasted_iota(jnp.int32, x.shape, 0)
        acc = jnp.zeros_like(x)
        for j in range(CONVW):
            acc = acc + _round_bf16(w_ref[pl.ds(j, 1), :]) * _shift_down(x, CONVW - 1 - j, rows)
        o_ref[0], o_ref[1] = _split_pair(_silu(acc))

        @pl.when(pl.program_id(0) == ncol - 1)
        def _():
            _finish(plan)

    res = pl.pallas_call(
        body, name="conv_fwd", grid=(ncol,),
        in_specs=[pl.BlockSpec((T, LANE), lambda j: (0, j)), pl.BlockSpec((CONVW, LANE), lambda j: (0, j))]
                 + _hbm_specs(ns),
        out_specs=[pl.BlockSpec((2, T, HD), lambda j: (j, 0, 0))] + _hbm_specs(ns),
        out_shape=[_sds((3 * GH, T, HD))] + _gather_shapes(shards),
        scratch_shapes=_gather_sems(ns),
        compiler_params=_cparams(("arbitrary",)),
    )(proj, conv_w, *shards)
    return res[0], res[1:]


RELAYOUT_TM = 4096


def _split_pair(y):
    return y[:, :HD], pltpu.roll(y, HD, 1)[:, :HD]


def _merge_pair(a, b):
    return jnp.concatenate([a, b], axis=1)


def _split_heads(x, col_block0, nheads, name):
    T = x.shape[0]
    tm = _tile(T, RELAYOUT_TM)

    def body(x_ref, o_ref):
        a, b = _split_pair(x_ref[...])
        o_ref[0] = a
        o_ref[1] = b

    return pl.pallas_call(
        body, name=name, grid=(nheads // 2, T // tm),
        in_specs=[pl.BlockSpec((tm, LANE), lambda j, i: (i, col_block0 + j))],
        out_specs=pl.BlockSpec((2, tm, HD), lambda j, i: (j, i, 0)),
        out_shape=_sds((nheads, T, HD), x.dtype),
        compiler_params=_cparams(("parallel", "parallel")),
    )(x)


def _merge_heads(hm, out_dtype, name, into=None, col_block0=0, head0=0, nheads=None):
    T = hm.shape[1]
    nheads = hm.shape[0] if nheads is None else nheads
    tm = _tile(T, RELAYOUT_TM)

    def body(*refs):
        h_ref, o_ref = refs[0], refs[-1]
        o_ref[...] = _merge_pair(h_ref[0], h_ref[1]).astype(o_ref.dtype)

    in_specs = [pl.BlockSpec((2, tm, HD), lambda j, i: (head0 // 2 + j, i, 0))]
    args = [hm]
    if into is None:
        out_shape = _sds((T, HD * nheads), out_dtype)
        aliases = {}
    else:
        out_shape = _sds(into.shape, into.dtype)
        in_specs.append(pl.BlockSpec(memory_space=pl.ANY))
        args.append(into)
        aliases = {1: 0}
    return pl.pallas_call(
        body, name=name, grid=(nheads // 2, T // tm),
        in_specs=in_specs,
        out_specs=pl.BlockSpec((tm, LANE), lambda j, i: (i, col_block0 + j)),
        out_shape=out_shape, input_output_aliases=aliases,
        compiler_params=_cparams(("parallel", "parallel")),
    )(*args)


def _conv_bwd(proj, conv_w, dqc):
    T = proj.shape[0]
    ncol = 3 * GW // LANE

    def body(x_ref, w_ref, d_ref, dx_ref, dw_ref):
        x = _round_bf16(x_ref[...])
        rows = lax.broadcasted_iota(jnp.int32, x.shape, 0)
        xs = [_shift_down(x, CONVW - 1 - j, rows) for j in range(CONVW)]
        w = [_round_bf16(w_ref[pl.ds(j, 1), :]) for j in range(CONVW)]
        pre = jnp.zeros_like(x)
        for j in range(CONVW):
            pre = pre + w[j] * xs[j]
        s = _sigmoid(pre)
        dpre = _round_bf16(_merge_pair(d_ref[0], d_ref[1]) * (s * (1.0 + pre * (1.0 - s))))
        dx = jnp.zeros_like(x)
        for j in range(CONVW):
            dx = dx + w[j] * _shift_up(dpre, CONVW - 1 - j, rows, T)
            dw_ref[pl.ds(j, 1), :] = jnp.sum(dpre * xs[j], axis=0, keepdims=True)
        dx_ref[...] = dx.astype(dx_ref.dtype)

    return pl.pallas_call(
        body, name="conv_bwd", grid=(ncol,),
        in_specs=[pl.BlockSpec((T, LANE), lambda j: (0, j)), pl.BlockSpec((CONVW, LANE), lambda j: (0, j)),
                  pl.BlockSpec((2, T, HD), lambda j: (j, 0, 0))],
        out_specs=[pl.BlockSpec((T, LANE), lambda j: (0, j)), pl.BlockSpec((CONVW, LANE), lambda j: (0, j))],
        out_shape=[_sds((T, NP), BF16), _sds((CONVW, 3 * GW))],
        compiler_params=_cparams(("parallel",)),
    )(proj, conv_w, dqc)


def _gdn_prep(kit, q, k, v, ga, gb, alog, dtb, t_inv=None):
    C = CHUNK
    ri = lax.broadcasted_iota(jnp.int32, (C, C), 0)
    ci = lax.broadcasted_iota(jnp.int32, (C, C), 1)
    causal = ri >= ci
    strict = ri > ci
    eye = (ri == ci).astype(F32)
    lower = causal.astype(F32)
    upper = (ri <= ci).astype(F32)

    a = ga + dtb
    softplus = jnp.maximum(a, 0.0) + jnp.log(1.0 + jnp.exp(-jnp.abs(a)))
    g_row = -jnp.exp(alog) * softplus
    beta_row = _sigmoid(gb)
    g_col = jnp.sum(eye * g_row, axis=2, keepdims=True)
    beta_col = jnp.sum(eye * beta_row, axis=2, keepdims=True)
    G_col = jnp.sum(lower * g_row, axis=2, keepdims=True)
    G_row = jnp.sum(upper * g_col, axis=1, keepdims=True)
    G_last = jnp.sum(g_row, axis=2, keepdims=True)
    decay = jnp.exp(jnp.where(causal, G_col - G_row, -1e30))

    qn = q * lax.rsqrt(jnp.sum(q * q, axis=-1, keepdims=True) + EPS) * (HD ** -0.5)
    kn = k * lax.rsqrt(jnp.sum(k * k, axis=-1, keepdims=True) + EPS)
    kb = kn * beta_col
    A = jnp.where(strict, kit.nt(kb, kn) * decay, 0.0)
    Tm = kit.inv(A, t_inv)
    eG = jnp.exp(G_col)
    u = kit.nn3(Tm, v * beta_col)
    w = kit.nn3(Tm, kb * eG)
    qk = jnp.where(causal, kit.nt(qn, kn) * decay, 0.0)
    q_dec = qn * eG
    k_dec = kn * jnp.exp(G_last - G_col)
    dec = jnp.exp(G_last)
    return u, w, qk, q_dec, k_dec, dec, Tm


def _gdn_out(o, z, nw):
    return _rms(o, nw) * _silu(z)


GDN_CB = 4


def _gdn_specs(T, blk):
    TB = GDN_CB * CHUNK
    seq = lambda grp: pl.BlockSpec((GH, TB, HD), lambda i, grp=grp: (grp, blk(i), 0))
    row = lambda grp: pl.BlockSpec((GH, GDN_CB, 1, CHUNK), lambda i, grp=grp: (grp, blk(i), 0, 0))
    per_head = pl.BlockSpec((GH, 1, CHUNK), lambda i: (0, 0, 0))
    whole = pl.BlockSpec((1, HD), lambda i: (0, 0))
    state = pl.BlockSpec((GH, GDN_CB, HD, HD), lambda i: (0, blk(i), 0, 0))
    return seq, row, per_head, whole, state


def _gdn_load(seq_refs, row_refs, head_refs):
    chunks = lambda r: jnp.concatenate([r[:, pl.ds(cb * CHUNK, CHUNK), :] for cb in range(GDN_CB)], axis=0)
    rows = lambda r: jnp.concatenate([r[:, cb] for cb in range(GDN_CB)], axis=0)
    heads = lambda r: jnp.concatenate([r[...]] * GDN_CB, axis=0)
    return [chunks(r) for r in seq_refs], [rows(r) for r in row_refs], [heads(r) for r in head_refs]


def _gdn_fwd(qkv_hm, zs_hm, gab, alog_b, dtb_b, nw, shards):
    T = qkv_hm.shape[1]
    N = T // CHUNK
    nblk = N // GDN_CB
    ns = len(shards)
    seq, row, per_head, whole, state = _gdn_specs(T, lambda i: i)
    kit = _Kit(False)

    def body(*refs):
        q_ref, k_ref, v_ref, z_ref, ga_ref, gb_ref, al_ref, dt_ref, nw_ref = refs[:9]
        o_ref, S_ref, T_ref = refs[9 + ns:12 + ns]
        S_scr = refs[12 + 2 * ns]
        plan = _gather_plan(refs[9:9 + ns], refs[12 + ns:12 + 2 * ns], *refs[13 + 2 * ns:])

        @pl.when(pl.program_id(0) == 0)
        def _():
            S_scr[...] = jnp.zeros_like(S_scr)
            _start(plan)

        (q, k, v, z), (ga, gb), (al, dt) = _gdn_load((q_ref, k_ref, v_ref, z_ref), (ga_ref, gb_ref), (al_ref, dt_ref))
        u, w, qk, q_dec, k_dec, dec, t_inv = _gdn_prep(kit, q, k, v, ga, gb, al, dt)
        S = S_scr[...]
        for cb in range(GDN_CB):
            hs = slice(cb * GH, (cb + 1) * GH)
            S_ref[:, cb] = S
            T_ref[:, cb] = t_inv[hs]
            v_new = u[hs] - kit.nn(w[hs], S)
            o = kit.nn(q_dec[hs], S) + kit.nn(qk[hs], v_new)
            S = S * dec[hs] + kit.tn(k_dec[hs], v_new)
            o_ref[:, pl.ds(cb * CHUNK, CHUNK), :] = _gdn_out(o, z[hs], nw_ref[...])
        S_scr[...] = S

        @pl.when(pl.program_id(0) == nblk - 1)
        def _():
            _finish(plan)

    res = pl.pallas_call(
        body, name="gdn_fwd", grid=(nblk,),
        in_specs=[seq(0), seq(1), seq(2), seq(0), row(0), row(1), per_head, per_head, whole] + _hbm_specs(ns),
        out_specs=[seq(0), state, state] + _hbm_specs(ns),
        out_shape=[_sds((GH + SQH, T, HD)), _sds((GH, N, HD, HD)), _sds((GH, N, CHUNK, CHUNK))]
                  + _gather_shapes(shards),
        scratch_shapes=[pltpu.VMEM((GH, HD, HD), F32)] + _gather_sems(ns),
        compiler_params=_cparams(("arbitrary",)),
    )(qkv_hm, qkv_hm, qkv_hm, zs_hm, gab, gab, alog_b, dtb_b, nw, *shards)
    return res[0], (res[1], res[2]), res[3:]


def _gdn_bwd(qkv_hm, zs_hm, gab, alog_b, dtb_b, nw, S_all, do, pieces):
    T = qkv_hm.shape[1]
    N = T // CHUNK
    nblk = N // GDN_CB
    npc = len(pieces)
    dkit, kit = _Kit(True), _Kit(False)
    rseq, rrow, per_head, whole, rstate = _gdn_specs(T, lambda i: nblk - 1 - i)

    def body(*refs):
        q_ref, k_ref, v_ref, z_ref, ga_ref, gb_ref, al_ref, dt_ref, nw_ref, S_ref, T_ref, do_ref = refs[:12]
        dqkv_ref, dz_ref, dga_ref, dgb_ref, dal_ref, ddt_ref, dnw_ref = refs[12 + npc:19 + npc]
        dS_scr = refs[19 + 2 * npc]
        plan = _exchange_plan(refs[12:12 + npc], refs[19 + npc:19 + 2 * npc], *refs[20 + 2 * npc:])

        @pl.when(pl.program_id(0) == 0)
        def _():
            dS_scr[...] = jnp.zeros_like(dS_scr)
            dal_ref[...] = jnp.zeros_like(dal_ref)
            ddt_ref[...] = jnp.zeros_like(ddt_ref)
            dnw_ref[...] = jnp.zeros_like(dnw_ref)
            _start(plan)

        (q, k, v, z, dout), (ga, gb), (al, dt) = _gdn_load((q_ref, k_ref, v_ref, z_ref, do_ref), (ga_ref, gb_ref),
                                                          (al_ref, dt_ref))
        S_in = jnp.concatenate([S_ref[:, cb] for cb in range(GDN_CB)], axis=0)
        t_inv = jnp.concatenate([T_ref[:, cb] for cb in range(GDN_CB)], axis=0)
        prep = lambda *a: _gdn_prep(dkit, *a, t_inv=t_inv)[:6]
        (u, w, qk, q_dec, k_dec, dec), prep_vjp = jax.vjp(prep, q, k, v, ga, gb, al, dt)
        v_new = u - kit.nn(w, S_in)
        o = kit.nn(q_dec, S_in) + kit.nn(qk, v_new)
        _, out_vjp = jax.vjp(_gdn_out, o, z, nw_ref[...])
        do, dz, dnw = out_vjp(dout)
        dvn_part = kit.tn(qk, do)
        dS_part = kit.tn(q_dec, do)
        dS = dS_scr[...]
        dS_out, dvn = [None] * GDN_CB, [None] * GDN_CB
        for cb in reversed(range(GDN_CB)):
            hs = slice(cb * GH, (cb + 1) * GH)
            dS_out[cb] = dS
            dvn[cb] = dvn_part[hs] + kit.nn(k_dec[hs], dS)
            dS = dS * dec[hs] + dS_part[hs] - kit.tn(w[hs], dvn[cb])
        dS_scr[...] = dS
        dS_out = jnp.concatenate(dS_out, axis=0)
        dvn = jnp.concatenate(dvn, axis=0)
        ddec = jnp.sum(jnp.sum(S_in * dS_out, axis=2, keepdims=True), axis=1, keepdims=True)
        cts = (dvn, -kit.nt(dvn, S_in), kit.nt(do, v_new), kit.nt(do, S_in), kit.nt(v_new, dS_out), ddec)
        dq, dk, dv, dga, dgb, dal, ddt = prep_vjp(cts)
        lanesum = lambda t: jnp.broadcast_to(jnp.sum(t, axis=2, keepdims=True), t.shape)
        for cb in range(GDN_CB):
            hs = slice(cb * GH, (cb + 1) * GH)
            sl = pl.ds(cb * CHUNK, CHUNK)
            dqkv_ref[pl.ds(0, GH), sl, :] = dq[hs]
            dqkv_ref[pl.ds(GH, GH), sl, :] = dk[hs]
            dqkv_ref[pl.ds(2 * GH, GH), sl, :] = dv[hs]
            dz_ref[:, sl, :] = dz[hs]
            dga_ref[:, cb] = dga[hs]
            dgb_ref[:, cb] = dgb[hs]
            dal_ref[...] += lanesum(dal[hs])
            ddt_ref[...] += lanesum(ddt[hs])
        dnw_ref[...] += dnw

        @pl.when(pl.program_id(0) == nblk - 1)
        def _():
            _finish(plan)

    res = pl.pallas_call(
        body, name="gdn_bwd", grid=(nblk,),
        in_specs=[rseq(0), rseq(1), rseq(2), rseq(0), rrow(0), rrow(1), per_head, per_head, whole, rstate, rstate,
                  rseq(0)] + _hbm_specs(npc),
        out_specs=[pl.BlockSpec((3 * GH, GDN_CB * CHUNK, HD), lambda i: (0, nblk - 1 - i, 0)), rseq(0), rrow(0),
                   rrow(0), per_head, per_head, whole] + _hbm_specs(npc),
        out_shape=[_sds((3 * GH, T, HD)), _sds((GH + 4 + SWA_GRAD_HEADS, T, HD))] + [_sds((GH, N, 1, CHUNK))] * 2
                  + [_sds((GH, 1, CHUNK))] * 2 + [_sds((1, HD))] + _exchange_shapes(pieces),
        scratch_shapes=[pltpu.VMEM((GH, HD, HD), F32)] + _exchange_sems(npc),
        compiler_params=_cparams(("arbitrary",), GDN_BWD_VMEM),
    )(qkv_hm, qkv_hm, qkv_hm, zs_hm, gab, gab, alog_b, dtb_b, nw, S_all[0], S_all[1], do, *pieces)
    return res[:7], res[7:]


def _swa_heads(kit, first, q, kp, kc, vp, vc, qnw, knw, sink, slope):
    W = WIN
    ri = lax.broadcasted_iota(jnp.int32, (W, W), 0)
    ci = lax.broadcasted_iota(jnp.int32, (W, W), 1)
    mask_c = ri >= ci
    mask_p = ci > ri + first * W
    dist_c = (ri - ci).astype(F32)
    dist_p = (ri - ci + W).astype(F32)
    kpn = _rms(kp, knw)
    kcn = _rms(kc, knw)
    qn = _rms(q, qnw)
    sc = jnp.where(mask_c, kit.nt(qn, kcn) * (HD ** -0.5) - slope * dist_c, -1e30)
    sp = jnp.where(mask_p, kit.nt(qn, kpn) * (HD ** -0.5) - slope * dist_p, -1e30)
    m = jnp.maximum(jnp.maximum(jnp.max(sc, axis=-1, keepdims=True), jnp.max(sp, axis=-1, keepdims=True)), sink)
    m = lax.stop_gradient(m)
    pc = jnp.exp(sc - m)
    pp = jnp.exp(sp - m)
    den = jnp.sum(pc, axis=-1, keepdims=True) + jnp.sum(pp, axis=-1, keepdims=True) + jnp.exp(sink - m)
    inv = 1.0 / den
    return kit.nn(pc * inv, vc) + kit.nn(pp * inv, vp)


def _swa_grads(kit, first, q, kp, kc, vp, vc, qnw, knw, sink, slope, do):
    W = WIN
    ri = lax.broadcasted_iota(jnp.int32, (W, W), 0)
    ci = lax.broadcasted_iota(jnp.int32, (W, W), 1)
    mask_c = ri >= ci
    mask_p = ci > ri + first * W
    dist_c = (ri - ci).astype(F32)
    dist_p = (ri - ci + W).astype(F32)
    scale = HD ** -0.5
    kpn, kp_vjp = jax.vjp(_rms, kp, knw)
    kcn, kc_vjp = jax.vjp(_rms, kc, knw)
    qn, q_vjp = jax.vjp(_rms, q, qnw)
    sc = jnp.where(mask_c, kit.nt(qn, kcn) * scale - slope * dist_c, -1e30)
    sp = jnp.where(mask_p, kit.nt(qn, kpn) * scale - slope * dist_p, -1e30)
    m = jnp.maximum(jnp.maximum(jnp.max(sc, axis=-1, keepdims=True), jnp.max(sp, axis=-1, keepdims=True)), sink)
    ec = jnp.exp(sc - m)
    ep = jnp.exp(sp - m)
    es = jnp.exp(sink - m)
    inv = 1.0 / (jnp.sum(ec, axis=-1, keepdims=True) + jnp.sum(ep, axis=-1, keepdims=True) + es)
    pc, pp = ec * inv, ep * inv
    dpc, dpp = kit.nt(do, vc), kit.nt(do, vp)
    delta = jnp.sum(dpc * pc, axis=-1, keepdims=True) + jnp.sum(dpp * pp, axis=-1, keepdims=True)
    dsc = pc * (dpc - delta) * scale
    dsp = pp * (dpp - delta) * scale
    dq, dqnw = q_vjp(kit.nn(dsc, kcn) + kit.nn(dsp, kpn))
    dkc, dknw_c = kc_vjp(kit.tn(dsc, qn))
    dkp, dknw_p = kp_vjp(kit.tn(dsp, qn))
    return dq, dkp, dkc, kit.tn(pp, do), kit.tn(pc, do), dqnw, dknw_c + dknw_p, -(es * inv) * delta


def _per_query_head(kv_ref):
    return jnp.concatenate([kv_ref[pl.ds(h // SGRP, 1)] for h in range(SQH)], axis=0)


def _per_kv_head(d):
    return jnp.concatenate([jnp.sum(d[g * SGRP:(g + 1) * SGRP], axis=0, keepdims=True) for g in range(SKVH)], axis=0)


def _swa_specs(blk):
    qspec = pl.BlockSpec((SQH, WIN, HD), lambda i: (1, blk(i), 0))
    cur = lambda grp: pl.BlockSpec((SKVH, WIN, HD), lambda i, grp=grp: (grp, blk(i), 0))
    prev = lambda grp: pl.BlockSpec((SKVH, WIN, HD), lambda i, grp=grp: (grp, jnp.maximum(blk(i) - 1, 0), 0))
    whole = pl.BlockSpec((1, HD), lambda i: (0, 0))
    col = pl.BlockSpec((SQH, WIN, 1), lambda i: (0, 0, 0))
    ospec = pl.BlockSpec((SQH, WIN, HD), lambda i: (0, blk(i), 0))
    return qspec, cur, prev, whole, col, ospec


def _swa_fwd(zs_hm, qnw, knw, sinks_col, slopes_col, o_buf, shards):
    T = zs_hm.shape[1]
    NB = T // WIN
    ns = len(shards)
    kit = _Kit(False)
    qspec, cur, prev, whole, col, _ = _swa_specs(lambda i: i)

    def body(*refs):
        q_ref, kp_ref, kc_ref, vp_ref, vc_ref, qnw_ref, knw_ref, s_ref, sl_ref = refs[:9]
        o_ref = refs[10 + ns]
        plan = _gather_plan(refs[10:10 + ns], refs[11 + ns:11 + 2 * ns], *refs[11 + 2 * ns:])

        @pl.when(pl.program_id(0) == 0)
        def _():
            _start(plan)

        first = (pl.program_id(0) == 0).astype(jnp.int32)
        o_ref[...] = _swa_heads(kit, first, q_ref[...], _per_query_head(kp_ref), _per_query_head(kc_ref),
                                _per_query_head(vp_ref), _per_query_head(vc_ref), qnw_ref[...], knw_ref[...],
                                s_ref[...], sl_ref[...])

        @pl.when(pl.program_id(0) == NB - 1)
        def _():
            _finish(plan)

    res = pl.pallas_call(
        body, name="swa_fwd", grid=(NB,),
        in_specs=[qspec, prev(8), cur(8), prev(9), cur(9), whole, whole, col, col] + _hbm_specs(1 + ns),
        out_specs=[pl.BlockSpec((SQH, WIN, HD), lambda i: (1, i, 0))] + _hbm_specs(ns),
        out_shape=[_sds(o_buf.shape)] + _gather_shapes(shards),
        input_output_aliases={9: 0},
        scratch_shapes=_gather_sems(ns),
        compiler_params=_cparams(("arbitrary",)),
    )(zs_hm, zs_hm, zs_hm, zs_hm, zs_hm, qnw, knw, sinks_col, slopes_col, o_buf, *shards)
    return res[0], res[1:]


SWA_GRAD_HEADS = SQH + 2 * SKVH


def _swa_bwd(zs_hm, qnw, knw, sinks_col, slopes_col, dmix_hm, d_buf):
    T = zs_hm.shape[1]
    NB = T // WIN
    kit = _Kit(False)
    qspec, cur, prev, whole, col, _ = _swa_specs(lambda i: NB - 1 - i)

    def body(q_ref, kp_ref, kc_ref, vp_ref, vc_ref, qnw_ref, knw_ref, s_ref, sl_ref, do_ref, buf_ref,
             d_ref, dqnw_ref, dknw_ref, ds_ref, ck_scr, cv_scr):
        dq_ref = d_ref.at[pl.ds(0, SQH)]
        dk_ref = d_ref.at[pl.ds(SQH, SKVH)]
        dv_ref = d_ref.at[pl.ds(SQH + SKVH, SKVH)]
        i = pl.program_id(0)
        first = (i == NB - 1).astype(jnp.int32)

        @pl.when(i == 0)
        def _():
            ck_scr[...] = jnp.zeros_like(ck_scr)
            cv_scr[...] = jnp.zeros_like(cv_scr)
            ds_ref[...] = jnp.zeros_like(ds_ref)
            dqnw_ref[...] = jnp.zeros_like(dqnw_ref)
            dknw_ref[...] = jnp.zeros_like(dknw_ref)

        dq, dkp, dkc, dvp, dvc, dqnw, dknw, dsink = _swa_grads(
            kit, first, q_ref[...], _per_query_head(kp_ref), _per_query_head(kc_ref), _per_query_head(vp_ref),
            _per_query_head(vc_ref), qnw_ref[...], knw_ref[...], s_ref[...], sl_ref[...], do_ref[...])
        dq_ref[...] = dq
        dk_ref[...] = _per_kv_head(dkc) + ck_scr[...]
        dv_ref[...] = _per_kv_head(dvc) + cv_scr[...]
        ck_scr[...] = _per_kv_head(dkp)
        cv_scr[...] = _per_kv_head(dvp)
        dqnw_ref[...] += dqnw
        dknw_ref[...] += dknw
        ds_ref[...] += jnp.broadcast_to(jnp.sum(dsink, axis=1, keepdims=True), dsink.shape)

    dospec = pl.BlockSpec((SQH, WIN, HD), lambda i: (1, NB - 1 - i, 0))
    dspec = pl.BlockSpec((SWA_GRAD_HEADS, WIN, HD), lambda i: (1, NB - 1 - i, 0))
    res = pl.pallas_call(
        body, name="swa_bwd", grid=(NB,),
        in_specs=[qspec, prev(8), cur(8), prev(9), cur(9), whole, whole, col, col, dospec] + _hbm_specs(1),
        out_specs=[dspec, whole, whole, col],
        out_shape=[_sds(d_buf.shape), _sds((1, HD)), _sds((1, HD)), _sds((SQH, WIN, 1))],
        input_output_aliases={10: 0},
        scratch_shapes=[pltpu.VMEM((SKVH, WIN, HD), F32), pltpu.VMEM((SKVH, WIN, HD), F32)],
        compiler_params=_cparams(("arbitrary",)),
    )(zs_hm, zs_hm, zs_hm, zs_hm, zs_hm, qnw, knw, sinks_col, slopes_col, dmix_hm, d_buf)
    return res


GAB0 = 3 * GW + 1280


W_IN_ROWS = PROJ // N_CHIP
W_IN_ROWS_PAD = 736


def _permute_w_in_t(w_in_t):
    return jnp.concatenate([w_in_t[:4 * GW], w_in_t[4 * GW + 2 * GH:], w_in_t[4 * GW:4 * GW + 2 * GH],
                            jnp.zeros((NP - PROJ, D), w_in_t.dtype)], axis=0)


def _w_in_grad_pieces(g_t):
    g = jnp.concatenate([g_t[:4 * GW], g_t[GAB0:GAB0 + 2 * GH], g_t[4 * GW:GAB0]], axis=0)
    g = jnp.pad(g.reshape(N_CHIP, W_IN_ROWS, D), ((0, 0), (0, W_IN_ROWS_PAD - W_IN_ROWS), (0, 0)))
    return g.reshape(N_CHIP, 2, W_IN_ROWS_PAD // 2, D)


def _pieces_by_rows(g):
    return g.reshape(N_CHIP, 2, g.shape[0] // (2 * N_CHIP), D)


def _local_step(x, target, mod, n1w, w_in_pt, conv_w, alog, dtb, gnw, qnw, knw, sinks, n2w, shards):
    sh_out, sh_gate, sh_up, sh_down = shards
    T = x.shape[0]
    N = T // CHUNK
    shift1, scale1, gate1, shift2, scale2, gate2 = [mod[:, i * D:(i + 1) * D] for i in range(6)]

    h, proj, (a_out,) = _norm_in_proj(x, n1w, scale1, shift1, w_in_pt, [sh_out])
    w_out = a_out.reshape(D, D)
    half_down = sh_down.shape[0] // 2
    qkv_hm, (a_down0,) = _conv_fwd(proj, conv_w, [sh_down[:half_down]])
    zs_hm = _split_heads(proj, 3 * GW // LANE, 20, "split_zs")
    gab = proj[:, GAB0:GAB0 + 2 * GH].T.reshape(2 * GH, N, 1, CHUNK)
    alog_b = jnp.broadcast_to(alog.reshape(GH, 1, 1), (GH, 1, CHUNK))
    dtb_b = jnp.broadcast_to(dtb.reshape(GH, 1, 1), (GH, 1, CHUNK))
    sinks_col = jnp.broadcast_to(sinks.reshape(SQH, 1, 1), (SQH, WIN, 1))
    o_hm, S_all, (a_gate, a_up) = _gdn_fwd(qkv_hm, zs_hm, gab, alog_b, dtb_b, gnw, [sh_gate, sh_up])
    w_gut = _interleave_gate_up(a_gate.reshape(DFF, D), a_up.reshape(DFF, D))
    slopes = 2.0 ** (-8.0 * (jnp.arange(SQH, dtype=F32) + 1.0) / SQH)
    slopes_col = jnp.broadcast_to(slopes.reshape(SQH, 1, 1), (SQH, WIN, 1))
    o_hm, (a_down1,) = _swa_fwd(zs_hm, qnw, knw, sinks_col, slopes_col, o_hm, [sh_down[half_down:]])
    w_down = jnp.concatenate([a_down0, a_down1], axis=1).reshape(DFF, D)
    mixcat = _merge_heads(o_hm, BF16, "merge_mix")
    mixed, x1, h2 = _out_proj_resid_norm(mixcat, w_out, x, gate1, n2w, scale2, shift2)
    ab, act = _ffn_up_act(h2, w_gut)
    dy, dffn, dgate2, loss = _ffn_down_loss(act, w_down, x1, target, gate2)

    dab = _ffn_down_dx_act(dffn, w_down, ab)
    g_w_down = _matmul(act, dffn, ta=True, out_dtype=BF16, name="ffn_down_dw")
    g_w_gut = _matmul(dab, h2, ta=True, out_dtype=BF16, name="ffn_up_dw")
    dx1, dmixed, dgate1, dn2w, dscale2, dshift2 = _ffn_up_dx_resid_bwd(dab, w_gut, x, mixed, dy, gate1, n2w, scale2,
                                                                       shift2)
    g_w_out = _matmul(mixcat, dmixed, ta=True, out_dtype=BF16, name="out_proj_dw")
    dmix_hm = _split_heads(_matmul(dmixed, w_out, tb=True, name="out_proj_dx"), 0, GH + SQH, "split_dmix")
    g_gate_t, g_up_t = _split_gate_up(g_w_gut)
    pieces = [_pieces_by_rows(g_w_out), _pieces_by_rows(g_gate_t), _pieces_by_rows(g_up_t),
              _pieces_by_rows(g_w_down)]
    (dqkv_hm, d_hm, dga, dgb, dalog, ddtb, dgnw), recv = _gdn_bwd(qkv_hm, zs_hm, gab, alog_b, dtb_b, gnw, S_all,
                                                                  dmix_hm, pieces)
    d_hm, dqnw, dknw, dsinks = _swa_bwd(zs_hm, qnw, knw, sinks_col, slopes_col, dmix_hm, d_hm)
    dproj, dconv = _conv_bwd(proj, conv_w, dqkv_hm)
    dproj = _merge_heads(d_hm, BF16, "merge_dz", into=dproj, col_block0=3 * GW // LANE, head0=0, nheads=GH)
    dproj = _merge_heads(d_hm, BF16, "merge_dswa", into=dproj, col_block0=4 * GW // LANE, head0=GH + 4,
                         nheads=SWA_GRAD_HEADS)
    dgab = jnp.concatenate([dga, dgb], axis=0).reshape(2 * GH, T).T.astype(BF16)
    dproj = lax.dynamic_update_slice(dproj, jnp.concatenate([dgab, jnp.zeros((T, NP - PROJ), BF16)], axis=1),
                                     (0, GAB0))
    g_w_in_pt = _matmul(dproj, h, ta=True, out_dtype=BF16, name="in_proj_dw")
    (grad_x, dn1w, dscale1, dshift1), recv_in = _in_proj_dx_norm_bwd(dproj, w_in_pt, x, dx1, n1w, scale1, shift1,
                                                                     [_w_in_grad_pieces(g_w_in_pt)])

    dmod = jnp.concatenate([dshift1, dscale1, dgate1, dshift2, dscale2, dgate2], axis=1)
    big = list(recv_in) + list(recv)
    small = dict(mod=dmod, norm1_w=dn1w, norm2_w=dn2w, conv_w=dconv, a_log=dalog[:, 0, 0], dt_bias=ddtb[:, 0, 0],
                 gdn_norm_w=dgnw, q_norm_w=dqnw, k_norm_w=dknw, sinks=dsinks[:, 0, 0])
    return loss, grad_x, big, small


def _adamw(w, g, m, v):
    m2 = ADAM_B1 * m + (1.0 - ADAM_B1) * g
    v2 = ADAM_B2 * v + (1.0 - ADAM_B2) * (g * g)
    m_hat = m2 / (1.0 - ADAM_B1 ** ADAM_STEP)
    v_hat = v2 / (1.0 - ADAM_B2 ** ADAM_STEP)
    delta = -ADAM_LR * (m_hat / (jnp.sqrt(v_hat) + ADAM_EPS) + ADAM_WD * w)
    return delta, m2, v2


def _reduce_adamw(recv, w, m, v, name):
    _, R, C = recv.shape
    tc = _tile(C, 256)

    def body(r_ref, w_ref, m_ref, v_ref, o_ref):
        g = r_ref[0].astype(F32)
        for s in range(1, N_DEV):
            g = g + r_ref[s].astype(F32)
        delta, m2, v2 = _adamw(w_ref[...], g, m_ref[...], v_ref[...])
        o_ref[0] = g
        o_ref[1] = delta
        o_ref[2] = m2
        o_ref[3] = v2

    col = pl.BlockSpec((R, tc), lambda j: (0, j))
    return pl.pallas_call(
        body, name=name, grid=(C // tc,),
        in_specs=[pl.BlockSpec((N_DEV, R, tc), lambda j: (0, 0, j)), col, col, col],
        out_specs=pl.BlockSpec((4, R, tc), lambda j: (0, 0, j)),
        out_shape=_sds((4, R, C)),
        compiler_params=_cparams(("parallel",)),
    )(recv, w, m, v)


def _adamw_call(g, w, m, v, name):
    def body(g_ref, w_ref, m_ref, v_ref, o_ref):
        delta, m2, v2 = _adamw(w_ref[...], g_ref[...], m_ref[...], v_ref[...])
        o_ref[0] = delta
        o_ref[1] = m2
        o_ref[2] = v2

    return pl.pallas_call(body, name=name, out_shape=_sds((3,) + g.shape))(g, w, m, v)


ADA_N = 6 * D // N_CHIP
KPAD = 128


def _w_ada_update(c8p, dm, w, m, v):
    tr = 256

    def body(c_ref, dm_ref, w_ref, m_ref, v_ref, g_ref, d_ref, m2_ref, v2_ref):
        g = _raw1(_silu(c_ref[...]), dm_ref[...], _TN)
        delta, m2, v2 = _adamw(w_ref[...], g, m_ref[...], v_ref[...])
        g_ref[...] = g
        d_ref[...] = delta
        m2_ref[...] = m2
        v2_ref[...] = v2

    blk = pl.BlockSpec((tr, ADA_N), lambda i: (i, 0))
    return pl.pallas_call(
        body, name="w_ada_update", grid=(D // tr,),
        in_specs=[pl.BlockSpec((KPAD, tr), lambda i: (0, i)), pl.BlockSpec((KPAD, ADA_N), lambda i: (0, 0)),
                  blk, blk, blk],
        out_specs=[blk] * 4, out_shape=[_sds((D, ADA_N))] * 4,
        compiler_params=_cparams(("parallel",)),
    )(c8p, dm, w, m, v)


def _me():
    return lax.axis_index("x"), lax.axis_index("y"), lax.axis_index("c")


def _peer(k, me):
    mx, my, mc = me
    return (1 - mx if k & 4 else mx, 1 - my if k & 2 else my, 1 - mc if k & 1 else mc)


def _lin(p):
    return 4 * p[0] + 2 * p[1] + p[2]


def _remote(src, dst, ssem, rsem, dev):
    return pltpu.make_async_remote_copy(src_ref=src, dst_ref=dst, send_sem=ssem, recv_sem=rsem,
                                        device_id=dev, device_id_type=MESH)


def _all_gather8(x, name):
    def body(x_ref, out_ref, send_sems, recv_sems):
        me = _me()
        out_ref[_lin(me)] = x_ref[...]
        sends = []
        for k in range(1, N_DEV):
            cp = _remote(x_ref, out_ref.at[_lin(me)], send_sems.at[k - 1], recv_sems.at[k - 1], _peer(k, me))
            cp.start()
            sends.append(cp)
        for k in range(1, N_DEV):
            p = _peer(k, me)
            _remote(x_ref, out_ref.at[_lin(p)], send_sems.at[k - 1], recv_sems.at[k - 1], p).wait_recv()
        for cp in sends:
            cp.wait_send()

    return pl.pallas_call(
        body, name=name,
        out_shape=_sds((N_DEV,) + x.shape, x.dtype),
        in_specs=[pl.BlockSpec(memory_space=pltpu.VMEM)],
        out_specs=pl.BlockSpec(memory_space=pltpu.VMEM),
        scratch_shapes=[pltpu.SemaphoreType.DMA((N_DEV - 1,)), pltpu.SemaphoreType.DMA((N_DEV - 1,))],
    )(x)


def _ag8_plan(src, out, send_sems, recv_sems):
    me = _me()
    sends, recvs = [], []
    for k in range(1, N_DEV):
        p = _peer(k, me)
        sends.append(_remote(src, out.at[_lin(me)], send_sems.at[k - 1], recv_sems.at[k - 1], p))
        recvs.append(_remote(src, out.at[_lin(p)], send_sems.at[k - 1], recv_sems.at[k - 1], p))
    return [], sends, recvs


def _prologue(c_row, conv_sh, w_ada, b_sh, w_in_sh):
    def body(c_ref, cv_ref, wa_ref, b_ref, win_ref, call_ref, cvall_ref, mods_ref, ain_ref, c16_scr, mp_scr,
             c_send, c_recv, cv_send, cv_recv, m_send, m_recv, w_send, w_recv, w_local):
        me = _lin(_me())
        w_plan = _gather_plan([win_ref], [ain_ref], w_send, w_recv, w_local)
        _start(w_plan)
        c_plan = _ag8_plan(c_ref, call_ref, c_send, c_recv)
        cv_plan = _ag8_plan(cv_ref, cvall_ref, cv_send, cv_recv)
        call_ref[me] = c_ref[...]
        cvall_ref[me] = cv_ref[...]
        _start(c_plan)
        _start(cv_plan)
        _finish(c_plan)
        c16_scr[...] = jnp.zeros_like(c16_scr)
        for d in range(N_DEV):
            c16_scr[pl.ds(d, 1), :] = call_ref[d]
        mp_scr[...] = _raw1(_silu(c16_scr[...]), wa_ref[...], _NN) + b_ref[...]
        mods_ref[me] = mp_scr[...]
        m_plan = _ag8_plan(mp_scr, mods_ref, m_send, m_recv)
        _start(m_plan)
        _finish(cv_plan)
        _finish(m_plan)
        _finish(w_plan)

    vmem = pl.BlockSpec(memory_space=pltpu.VMEM)
    sems = lambda n: pltpu.SemaphoreType.DMA((n,))
    return pl.pallas_call(
        body, name="prologue",
        in_specs=[vmem] * 4 + _hbm_specs(1), out_specs=[vmem] * 3 + _hbm_specs(1),
        out_shape=[_sds((N_DEV,) + c_row.shape), _sds((N_DEV,) + conv_sh.shape), _sds((N_DEV, 16, ADA_N)),
                   _sds((N_CHIP,) + w_in_sh.shape, w_in_sh.dtype)],
        scratch_shapes=[pltpu.VMEM((16, D), F32), pltpu.VMEM((16, ADA_N), F32)] + [sems(N_DEV - 1)] * 6
                       + _gather_sems(1),
        compiler_params=_cparams(),
    )(c_row, conv_sh, w_ada, b_sh, w_in_sh)


def _hbm_specs(n):
    return [pl.BlockSpec(memory_space=pl.ANY)] * n


def _gather_shapes(shards):
    return [_sds((N_CHIP,) + s.shape, s.dtype) for s in shards]


def _gather_sems(n):
    return [pltpu.SemaphoreType.DMA((3 * n,)), pltpu.SemaphoreType.DMA((3 * n,)), pltpu.SemaphoreType.DMA((n,))]


def _gather_plan(ins, outs, send_sems, recv_sems, local_sems):
    mx, my, mc = _me()
    chips = [(1 - mx, my), (mx, 1 - my), (1 - mx, 1 - my)]
    local, sends, recvs = [], [], []
    for a in range(len(ins)):
        local.append(pltpu.make_async_copy(ins[a], outs[a].at[2 * mx + my], local_sems.at[a]))
        for k, (px, py) in enumerate(chips):
            sems = (send_sems.at[3 * a + k], recv_sems.at[3 * a + k], (px, py, mc))
            sends.append(_remote(ins[a], outs[a].at[2 * mx + my], *sems))
            recvs.append(_remote(ins[a], outs[a].at[2 * px + py], *sems))
    return local, sends, recvs


def _start(plan):
    local, sends, _ = plan
    for cp in local + sends:
        cp.start()


def _finish(plan):
    local, sends, recvs = plan
    for cp in recvs:
        cp.wait_recv()
    for cp in sends:
        cp.wait_send()
    for cp in local:
        cp.wait()


def _exchange_shapes(pieces):
    return [_sds((N_DEV,) + p.shape[2:], p.dtype) for p in pieces]


def _exchange_sems(n):
    return [pltpu.SemaphoreType.DMA(((N_DEV - 1) * n,)), pltpu.SemaphoreType.DMA(((N_DEV - 1) * n,)),
            pltpu.SemaphoreType.DMA((n,))]


def _exchange_plan(ins, outs, send_sems, recv_sems, local_sems):
    me = _me()
    mx, my, mc = me
    local, sends, recvs = [], [], []
    for a in range(len(ins)):
        local.append(pltpu.make_async_copy(ins[a].at[2 * mx + my, mc], outs[a].at[_lin(me)], local_sems.at[a]))
        for k in range(1, N_DEV):
            p = _peer(k, me)
            s = (N_DEV - 1) * a + k - 1
            sends.append(_remote(ins[a].at[2 * p[0] + p[1], p[2]], outs[a].at[_lin(me)], send_sems.at[s],
                                 recv_sems.at[s], p))
            recvs.append(_remote(ins[a].at[2 * mx + my, mc], outs[a].at[_lin(p)], send_sems.at[s],
                                 recv_sems.at[s], p))
    return local, sends, recvs


REDUCE_VMEM = 56 * 1024 * 1024


def _reduce_swap(recvs):
    n = len(recvs)

    def body(*refs):
        r_refs, o_refs = refs[:n], refs[n:2 * n]
        send_sems, recv_sems = refs[2 * n:]
        mx, my, mc = _me()
        sib = (mx, my, 1 - mc)
        half = lambda a, c: o_refs[a].at[pl.ds(pl.multiple_of(c * recvs[a].shape[1], 8), recvs[a].shape[1])]
        sends = []
        for a in range(n):
            g = r_refs[a][0].astype(F32)
            for s in range(1, N_DEV):
                g = g + r_refs[a][s].astype(F32)
            half(a, mc)[...] = g
            cp = _remote(half(a, mc), half(a, mc), send_sems.at[a], recv_sems.at[a], sib)
            cp.start()
            sends.append(cp)
        for a in range(n):
            _remote(half(a, mc), half(a, 1 - mc), send_sems.at[a], recv_sems.at[a], sib).wait_recv()
        for cp in sends:
            cp.wait_send()

    vmem = pl.BlockSpec(memory_space=pltpu.VMEM)
    return pl.pallas_call(
        body, name="reduce_swap", out_shape=[_sds((2 * r.shape[1], r.shape[2])) for r in recvs],
        in_specs=[vmem] * n, out_specs=[vmem] * n,
        scratch_shapes=[pltpu.SemaphoreType.DMA((n,)), pltpu.SemaphoreType.DMA((n,))],
        compiler_params=_cparams(None, REDUCE_VMEM),
    )(*recvs)


def _adamw_big(g, w, m, v, name):
    rows, cols = g.shape
    tr = next((t for t in (256, 176, 128, 64, 8) if rows % t == 0), None)
    if tr is None:
        tc = _tile(cols, 256)
        blk, grid = pl.BlockSpec((rows, tc), lambda i: (0, i)), (cols // tc,)
    else:
        blk, grid = pl.BlockSpec((tr, cols), lambda i: (i, 0)), (rows // tr,)

    def body(g_ref, w_ref, m_ref, v_ref, go_ref, d_ref, m2_ref, v2_ref):
        g = g_ref[...]
        delta, m2, v2 = _adamw(w_ref[...], g, m_ref[...], v_ref[...])
        go_ref[...] = g
        d_ref[...] = delta
        m2_ref[...] = m2
        v2_ref[...] = v2

    return pl.pallas_call(
        body, name=name, grid=grid,
        in_specs=[blk] * 4, out_specs=[blk] * 4, out_shape=[_sds((rows, cols))] * 4,
        compiler_params=_cparams(("parallel",)),
    )(g, w, m, v)


SMALL_ORDER = (("mod", 6 * D), ("norm1_w", D), ("norm2_w", D), ("conv_w", CONVW * 3 * GW), ("a_log", GH),
               ("dt_bias", GH), ("gdn_norm_w", HD), ("q_norm_w", HD), ("k_norm_w", HD), ("sinks", SQH), ("loss", 1))
SMALL_R = 120


def _pack_small(d):
    parts = [d[k].reshape(-1).astype(F32) if k in d else jnp.zeros((n,), F32) for k, n in SMALL_ORDER]
    used = sum(n for _, n in SMALL_ORDER)
    parts.append(jnp.zeros((SMALL_R * LANE - used,), F32))
    return jnp.concatenate(parts).reshape(SMALL_R, LANE)


def _unpack_small(pk):
    flat = pk.reshape(-1)
    out, r = {}, 0
    for k, n in SMALL_ORDER:
        out[k] = flat[r:r + n]
        r += n
    return out


def kernel(x, c, w_ada, b_ada, norm1_w, w_in, conv_w, a_log, dt_bias, gdn_norm_w, q_norm_w, k_norm_w, sinks, w_out, norm2_w, w_gate, w_up, w_down, loss_target, m_w_ada, m_b_ada, m_norm1_w, m_w_in, m_conv_w, m_a_log, m_dt_bias, m_gdn_norm_w, m_q_norm_w, m_k_norm_w, m_sinks, m_w_out, m_norm2_w, m_w_gate, m_w_up, m_w_down, v_w_ada, v_b_ada, v_norm1_w, v_w_in, v_conv_w, v_a_log, v_dt_bias, v_gdn_norm_w, v_q_norm_w, v_k_norm_w, v_sinks, v_w_out, v_norm2_w, v_w_gate, v_w_up, v_w_down):
    mx, my, mc = _me()
    chip = 2 * mx + my
    dev = 4 * mx + 2 * my + mc
    T = x.shape[1]

    as_rows = lambda t, transposed: t[0].T if transposed else t[0]
    transposed = (True, False, True, True, False)
    big_w = [as_rows(t, tr) for t, tr in zip((w_in, w_out, w_gate, w_up, w_down), transposed)]
    shards = [t.astype(BF16) for t in big_w]

    b_sh = lax.dynamic_slice(b_ada, (0, chip * ADA_N), (1, ADA_N))
    c_all, conv_all, mods, a_in = _prologue(c, conv_w.reshape(CONVW, 3 * GW // N_CHIP), w_ada[0], b_sh, shards[0])
    c8 = c_all.reshape(N_DEV, D)
    conv_full = jnp.concatenate([conv_all[2 * j] for j in range(N_CHIP)], axis=1)
    mod = jnp.concatenate([lax.dynamic_slice(mods[2 * j], (dev, 0), (1, ADA_N)) for j in range(N_CHIP)], axis=1)
    w_in_pt = _permute_w_in_t(a_in.reshape(PROJ, D))

    loss, grad_x, big, small = _local_step(
        x[0], loss_target[0], mod, norm1_w, w_in_pt, conv_full, a_log, dt_bias, gdn_norm_w,
        q_norm_w, k_norm_w, sinks, norm2_w, shards[1:])

    small["loss"] = loss[:, :1]
    sg = _all_gather8(_pack_small(small), "gather_small_grads")
    rep = dict(mod=(b_ada, m_b_ada, v_b_ada), norm1_w=(norm1_w, m_norm1_w, v_norm1_w),
               norm2_w=(norm2_w, m_norm2_w, v_norm2_w), a_log=(a_log, m_a_log, v_a_log),
               dt_bias=(dt_bias, m_dt_bias, v_dt_bias), gdn_norm_w=(gdn_norm_w, m_gdn_norm_w, v_gdn_norm_w),
               q_norm_w=(q_norm_w, m_q_norm_w, v_q_norm_w), k_norm_w=(k_norm_w, m_k_norm_w, v_k_norm_w),
               sinks=(sinks, m_sinks, v_sinks))
    wmv = [_pack_small({k: t[i] for k, t in rep.items()}) for i in range(3)]
    sres = _reduce_adamw(sg, wmv[0], wmv[1], wmv[2], "small_reduce_adamw")
    s_g, s_d, s_m, s_v = [_unpack_small(sres[i]) for i in range(4)]
    loss_out = s_g["loss"][0]

    g_conv = lax.dynamic_slice(s_g["conv_w"].reshape(CONVW, 3 * GW), (0, chip * (3 * GW // N_CHIP)),
                               (CONVW, 3 * GW // N_CHIP))
    pad16 = lambda t: jnp.concatenate([t.reshape(12, LANE), jnp.zeros((4, LANE), F32)], axis=0)
    cres = _adamw_call(pad16(g_conv), pad16(conv_w), pad16(m_conv_w), pad16(v_conv_w), "conv_adamw")
    conv_out = [g_conv.reshape(conv_w.shape)] + [cres[i, :12].reshape(conv_w.shape) for i in range(3)]

    dmod8 = sg[:, :6 * D // LANE].reshape(N_DEV, 6 * D)
    dm = lax.dynamic_slice(dmod8, (0, chip * ADA_N), (N_DEV, ADA_N))
    zpad = lambda t: jnp.concatenate([t, jnp.zeros((KPAD - N_DEV, t.shape[1]), F32)], axis=0)
    ares = _w_ada_update(zpad(c8), zpad(dm), w_ada[0], m_w_ada[0], v_w_ada[0])

    names = ("w_in", "w_out", "w_gate", "w_up", "w_down")
    g_full = list(_reduce_swap(big))
    g_full[0] = g_full[0][:W_IN_ROWS]
    big_m = [as_rows(t, tr) for t, tr in zip((m_w_in, m_w_out, m_w_gate, m_w_up, m_w_down), transposed)]
    big_v = [as_rows(t, tr) for t, tr in zip((v_w_in, v_w_out, v_w_gate, v_w_up, v_w_down), transposed)]
    upd = [_adamw_big(g, w, m, v, "adamw_" + nm) for g, w, m, v, nm in zip(g_full, big_w, big_m, big_v, names)]
    back = lambda t, tr: (t.T if tr else t)[None]
    bg, bd, bm, bv = [[back(u[i], tr) for u, tr in zip(upd, transposed)] for i in range(4)]

    def group(a_i, small_d, conv_i, big_l):
        s = lambda k, ref: small_d[k].reshape(ref.shape)
        return [ares[a_i][None], s("mod", b_ada), s("norm1_w", norm1_w), big_l[0], conv_out[conv_i],
                s("a_log", a_log), s("dt_bias", dt_bias), s("gdn_norm_w", gdn_norm_w), s("q_norm_w", q_norm_w),
                s("k_norm_w", k_norm_w), s("sinks", sinks), big_l[1], s("norm2_w", norm2_w), big_l[2], big_l[3],
                big_l[4]]

    outs = [loss_out, grad_x[None]]
    outs += group(0, s_g, 0, bg) + group(1, s_d, 1, bd) + group(2, s_m, 2, bm) + group(3, s_v, 3, bv)
    return tuple(outs)
```

```python
import jax
import jax.numpy as jnp
from jax import lax
from jax.experimental import pallas as pl
from jax.experimental.pallas import tpu as pltpu

F32 = jnp.float32
BF16 = jnp.bfloat16
MESH = pl.DeviceIdType.MESH

D = 1024
HD = 64
GH = 8
GW = GH * HD
SQH = 8
SKVH = 2
SGRP = SQH // SKVH
WIN = 128
CONVW = 4
CHUNK = 64
DFF = 2816
PROJ = 2832
NP = 3072
EPS = 1e-6
N_DEV = 8
N_CHIP = 4

ADAM_LR = 0.001
ADAM_B1 = 0.9
ADAM_B2 = 0.999
ADAM_EPS = 1e-08
ADAM_WD = 0.01
ADAM_STEP = 10

VMEM_LIMIT = 48 * 1024 * 1024
GDN_BWD_VMEM = 58 * 1024 * 1024
LANE = 128


def _cparams(sem=None, vmem=VMEM_LIMIT):
    return pltpu.CompilerParams(dimension_semantics=sem, vmem_limit_bytes=vmem)


_NN = ((1,), (0,))
_NT = ((1,), (1,))
_TN = ((0,), (0,))


def _dot(a, b, dims):
    if a.ndim == 3:
        (ca,), (cb,) = dims
        return lax.dot_general(a, b, (((ca + 1,), (cb + 1,)), ((0,), (0,))), preferred_element_type=F32)
    return lax.dot_general(a, b, (dims, ((), ())), preferred_element_type=F32)


def _raw1(a, b, dims):
    return _dot(a.astype(BF16), b.astype(BF16), dims)


def _raw3(a, b, dims):
    ah = a.astype(BF16)
    al = (a - ah.astype(F32)).astype(BF16)
    bh = b.astype(BF16)
    bl = (b - bh.astype(F32)).astype(BF16)
    return _dot(ah, bh, dims) + (_dot(al, bh, dims) + _dot(ah, bl, dims))


def _make_diff_mm(raw):
    @jax.custom_vjp
    def nn(a, b):
        return raw(a, b, _NN)

    @jax.custom_vjp
    def nt(a, b):
        return raw(a, b, _NT)

    @jax.custom_vjp
    def tn(a, b):
        return raw(a, b, _TN)

    nn.defvjp(lambda a, b: (raw(a, b, _NN), (a, b)), lambda r, g: (nt(g, r[1]), tn(r[0], g)))
    nt.defvjp(lambda a, b: (raw(a, b, _NT), (a, b)), lambda r, g: (nn(g, r[1]), tn(g, r[0])))
    tn.defvjp(lambda a, b: (raw(a, b, _TN), (a, b)), lambda r, g: (nt(r[1], g), nn(r[0], g)))
    return nn, nt, tn


def _tri_inv_raw(a, nn3):
    n = a.shape[-1]
    ri = lax.broadcasted_iota(jnp.int32, (n, n), 0)
    ci = lax.broadcasted_iota(jnp.int32, (n, n), 1)
    t = (ri == ci).astype(F32)
    for lvl in range((n - 1).bit_length()):
        same_pair = (ri >> (lvl + 1)) == (ci >> (lvl + 1))
        lower_left = (((ri >> lvl) & 1) == 1) & (((ci >> lvl) & 1) == 0)
        y = jnp.where(same_pair & lower_left, a, 0.0)
        t = t - y if lvl == 0 else t - nn3(nn3(t, y), t)
    return t


class _Kit:
    def __init__(self, diff):
        if diff:
            self.nn, self.nt, self.tn = _make_diff_mm(_raw1)
            self.nn3, self.nt3, self.tn3 = _make_diff_mm(_raw3)
            nn3, nt3, tn3 = self.nn3, self.nt3, self.tn3

            @jax.custom_vjp
            def inv(a, t):
                return t

            def inv_fwd(a, t):
                return t, t

            def inv_bwd(t, g):
                return -tn3(t, nt3(g, t)), jnp.zeros_like(t)

            inv.defvjp(inv_fwd, inv_bwd)
            self.inv = inv
        else:
            self.nn = lambda a, b: _raw1(a, b, _NN)
            self.nt = lambda a, b: _raw1(a, b, _NT)
            self.tn = lambda a, b: _raw1(a, b, _TN)
            self.nn3 = lambda a, b: _raw3(a, b, _NN)
            self.nt3 = lambda a, b: _raw3(a, b, _NT)
            self.tn3 = lambda a, b: _raw3(a, b, _TN)
            self.inv = lambda a, t: _tri_inv_raw(a, self.nn3) if t is None else t


def _sigmoid(x):
    return 1.0 / (1.0 + jnp.exp(-x))


def _silu(x):
    return x * _sigmoid(x)


def _rms(x, w):
    return x * lax.rsqrt(jnp.mean(x * x, axis=-1, keepdims=True) + EPS) * w


def _tile(dim, target):
    t = (min(dim, target) // LANE) * LANE
    while t >= LANE:
        if dim % t == 0:
            return t
        t -= LANE
    return dim


MM_TM, MM_TN, MM_TK = 1408, 1536, 1408


def _matmul(a, b, ta=False, tb=False, out_dtype=F32, name="matmul", gather=None, exchange=None):
    carried = gather if gather is not None else exchange if exchange is not None else []
    nc = len(carried)
    if ta:
        K, M = a.shape
    else:
        M, K = a.shape
    if tb:
        N, K2 = b.shape
    else:
        K2, N = b.shape
    assert K == K2, (a.shape, b.shape, ta, tb)
    tm, tn, tk = _tile(M, MM_TM), _tile(N, MM_TN), _tile(K, MM_TK)
    nk = K // tk
    dims = ((0,) if ta else (1,), (1,) if tb else (0,))

    grid = (M // tm, N // tn, nk)

    def body(*refs):
        a_ref, b_ref = refs[:2]
        o_ref = refs[2 + nc]
        scratch = refs[3 + 2 * nc:]
        k = pl.program_id(2)
        if nc:
            make_plan = _gather_plan if gather is not None else _exchange_plan
            plan = make_plan(refs[2:2 + nc], refs[3 + nc:3 + 2 * nc], *scratch[-3:])
            at = lambda pos: ((pl.program_id(0) == pos[0]) & (pl.program_id(1) == pos[1]) & (k == pos[2]))

            @pl.when(at((0, 0, 0)))
            def _():
                _start(plan)

        part = _dot(a_ref[...].astype(BF16), b_ref[...].astype(BF16), dims)
        if nk == 1:
            o_ref[...] = part.astype(o_ref.dtype)
        else:
            acc_ref = scratch[0]

            @pl.when(k == 0)
            def _():
                acc_ref[...] = part

            @pl.when((k > 0) & (k < nk - 1))
            def _():
                acc_ref[...] += part

            @pl.when(k == nk - 1)
            def _():
                o_ref[...] = (acc_ref[...] + part).astype(o_ref.dtype)

        if nc:
            @pl.when(at((grid[0] - 1, grid[1] - 1, nk - 1)))
            def _():
                _finish(plan)

    a_spec = (pl.BlockSpec((tk, tm), lambda i, j, k: (k, i)) if ta
              else pl.BlockSpec((tm, tk), lambda i, j, k: (i, k)))
    b_spec = (pl.BlockSpec((tn, tk), lambda i, j, k: (j, k)) if tb
              else pl.BlockSpec((tk, tn), lambda i, j, k: (k, j)))
    if gather is not None:
        c_shapes, c_sems = _gather_shapes(carried), _gather_sems(nc)
    elif exchange is not None:
        c_shapes, c_sems = _exchange_shapes(carried), _exchange_sems(nc)
    else:
        c_shapes, c_sems = [], []
    res = pl.pallas_call(
        body, name=name, grid=grid,
        in_specs=[a_spec, b_spec] + _hbm_specs(nc),
        out_specs=[pl.BlockSpec((tm, tn), lambda i, j, k: (i, j))] + _hbm_specs(nc),
        out_shape=[jax.ShapeDtypeStruct((M, N), out_dtype)] + c_shapes,
        scratch_shapes=([pltpu.VMEM((tm, tn), F32)] if nk > 1 else []) + c_sems,
        compiler_params=_cparams(("arbitrary",) * 3 if nc else ("parallel", "parallel", "arbitrary")),
    )(a, b, *carried)
    return (res[0], res[1:]) if nc else res[0]


def _sds(shape, dtype=F32):
    return jax.ShapeDtypeStruct(shape, dtype)


def _norm_mod(x, nw, scale, shift):
    return _rms(x, nw) * (1.0 + scale) + shift


def _norm_in_proj(x, nw, scale, shift, w_in_pt, shards):
    T = x.shape[0]
    N = w_in_pt.shape[0]
    tm, tn = _tile(T, 1024), _tile(N, MM_TN)
    nm, nn = T // tm, N // tn
    ns = len(shards)

    def body(*refs):
        x_ref, nw_ref, sc_ref, sh_ref, w_ref = refs[:5]
        h_ref, o_ref = refs[5 + ns:7 + ns]
        plan = _gather_plan(refs[5:5 + ns], refs[7 + ns:7 + 2 * ns], *refs[7 + 2 * ns:])
        i, j = pl.program_id(0), pl.program_id(1)

        @pl.when((i == 0) & (j == 0))
        def _():
            _start(plan)

        @pl.when(j == 0)
        def _():
            for r0 in range(0, tm, ROWS_EPI):
                rows = pl.ds(r0, ROWS_EPI)
                h_ref[rows, :] = _norm_mod(x_ref[rows, :], nw_ref[...], sc_ref[...], sh_ref[...]).astype(BF16)

        o_ref[...] = _dot(h_ref[...], w_ref[...], _NT)

        @pl.when((i == nm - 1) & (j == nn - 1))
        def _():
            _finish(plan)

    vec = pl.BlockSpec((1, D), lambda i, j: (0, 0))
    res = pl.pallas_call(
        body, name="norm1_in_proj", grid=(nm, nn),
        in_specs=[pl.BlockSpec((tm, D), lambda i, j: (i, 0)), vec, vec, vec,
                  pl.BlockSpec((tn, D), lambda i, j: (j, 0))] + _hbm_specs(ns),
        out_specs=[pl.BlockSpec((tm, D), lambda i, j: (i, 0)), pl.BlockSpec((tm, tn), lambda i, j: (i, j))]
                  + _hbm_specs(ns),
        out_shape=[_sds((T, D), BF16), _sds((T, N))] + _gather_shapes(shards),
        scratch_shapes=_gather_sems(ns),
        compiler_params=_cparams(("arbitrary", "arbitrary")),
    )(x, nw, scale, shift, w_in_pt, *shards)
    return res[0], res[1], res[2:]


ROWS_TM = 512
ROWS_EPI = 256


def _matmul_rows(a, b, epi, tiled, consts, out_tiled, out_acc, name, pieces=()):
    T, K = a.shape
    tm, tk = _tile(T, ROWS_TM), _tile(K, MM_TK)
    nm, nk = T // tm, K // tk
    npc, nt, ncst, no, na = len(pieces), len(tiled), len(consts), len(out_tiled), len(out_acc)
    n_in = 2 + nt + ncst

    def body(*refs):
        a_ref, b_ref = refs[:2]
        t_refs, c_refs = refs[2:2 + nt], refs[2 + nt:n_in]
        o_refs = refs[n_in + npc:n_in + npc + no]
        acc_refs = refs[n_in + npc + no:n_in + npc + no + na]
        n_out = no + na + npc
        res_ref = refs[n_in + npc + n_out]
        plan = _exchange_plan(refs[n_in:n_in + npc], refs[n_in + npc + no + na:n_in + npc + n_out],
                              *refs[n_in + npc + n_out + 1:]) if npc else None
        i, k = pl.program_id(0), pl.program_id(1)

        @pl.when((i == 0) & (k == 0))
        def _():
            for r in acc_refs:
                r[...] = jnp.zeros_like(r)
            if npc:
                _start(plan)

        part = _dot(a_ref[...], b_ref[...], _NN)

        @pl.when(k == 0)
        def _():
            res_ref[...] = part

        @pl.when(k > 0)
        def _():
            res_ref[...] += part

        @pl.when(k == nk - 1)
        def _():
            for r0 in range(0, tm, ROWS_EPI):
                rows = pl.ds(r0, ROWS_EPI)
                outs = epi(res_ref[rows, :], *[r[rows, :] for r in t_refs], *[r[...] for r in c_refs])
                for r, v in zip(o_refs, outs[:no]):
                    r[rows, :] = v.astype(r.dtype)
                for r, v in zip(acc_refs, outs[no:]):
                    r[...] += v

        if npc:
            @pl.when((i == nm - 1) & (k == nk - 1))
            def _():
                _finish(plan)

    row = lambda w: pl.BlockSpec((tm, w), lambda i, k: (i, 0))
    whole = lambda s: pl.BlockSpec(s.shape, lambda i, k: (0, 0))
    res = pl.pallas_call(
        body, name=name, grid=(nm, nk),
        in_specs=[pl.BlockSpec((tm, tk), lambda i, k: (i, k)), pl.BlockSpec((tk, D), lambda i, k: (k, 0))]
                 + [row(t.shape[1]) for t in tiled] + [whole(c) for c in consts] + _hbm_specs(npc),
        out_specs=[row(s.shape[1]) for s in out_tiled] + [whole(s) for s in out_acc] + _hbm_specs(npc),
        out_shape=list(out_tiled) + list(out_acc) + (_exchange_shapes(pieces) if npc else []),
        scratch_shapes=[pltpu.VMEM((tm, D), F32)] + (_exchange_sems(npc) if npc else []),
        compiler_params=_cparams(("arbitrary", "arbitrary")),
    )(a, b, *tiled, *consts, *pieces)
    return res[:no + na], res[no + na:]


def _in_proj_dx_norm_bwd(dproj, w_in_pt, x, dres, nw, scale, shift, pieces):
    T = x.shape[0]

    def epi(dh, x, dres, nw, scale, shift):
        _, vjp = jax.vjp(_norm_mod, x, nw, scale, shift)
        dx, dnw, dsc, dsh = vjp(dh)
        return dx + dres, dnw, dsc, dsh

    return _matmul_rows(dproj, w_in_pt, epi, [x, dres], [nw, scale, shift], [_sds((T, D))], [_sds((1, D))] * 3,
                        "in_proj_dx_norm1_bwd", pieces)


def _out_proj_resid_norm(mixcat, w_out, x, gate1, nw, scale, shift):
    T = x.shape[0]

    def epi(mixed, x, gate1, nw, scale, shift):
        return (mixed,) + _resid_norm(x, mixed, gate1, nw, scale, shift)

    outs, _ = _matmul_rows(mixcat, w_out, epi, [x], [gate1, nw, scale, shift],
                           [_sds((T, D)), _sds((T, D)), _sds((T, D), BF16)], [], "out_proj_resid_norm2")
    return outs


def _ffn_up_dx_resid_bwd(dab, w_gut, x, mixed, dy, gate1, nw, scale, shift):
    T = x.shape[0]

    def epi(dh2, x, mixed, dy, gate1, nw, scale, shift):
        _, vjp = jax.vjp(_resid_norm, x, mixed, gate1, nw, scale, shift)
        return vjp((dy, dh2))

    outs, _ = _matmul_rows(dab, w_gut, epi, [x, mixed, dy], [gate1, nw, scale, shift],
                           [_sds((T, D)), _sds((T, D), BF16)], [_sds((1, D))] * 4, "ffn_up_dx_resid_norm2_bwd")
    return outs


def _ffn_down_loss(act, w_down, x1, target, gate2):
    T = x1.shape[0]

    def epi(ffn, x1, target, gate2):
        y = x1 + gate2 * ffn
        err = y - target
        loss = 0.5 * jnp.sum(jnp.sum(err * err, axis=1, keepdims=True), axis=0, keepdims=True) / D
        dy = err * (1.0 / D)
        return dy, gate2 * dy, jnp.sum(dy * ffn, axis=0, keepdims=True), jnp.broadcast_to(loss, (1, LANE))

    outs, _ = _matmul_rows(act, w_down, epi, [x1, target], [gate2], [_sds((T, D)), _sds((T, D), BF16)],
                           [_sds((1, D)), _sds((1, LANE))], "ffn_down_loss")
    return outs


def _resid_norm(x, mixed, gate1, nw, scale, shift):
    x1 = x + gate1 * mixed
    return x1, _norm_mod(x1, nw, scale, shift)


FFN_BLK = 256
FFN_TM = 2048


def _interleave_gate_up(gate_t, up_t):
    blocks = lambda t: t.reshape(DFF // FFN_BLK, 1, FFN_BLK, D)
    return jnp.concatenate([blocks(gate_t), blocks(up_t)], axis=1).reshape(2 * DFF, D)


def _split_gate_up(g):
    g = g.reshape(DFF // FFN_BLK, 2, FFN_BLK, D)
    return g[:, 0].reshape(DFF, D), g[:, 1].reshape(DFF, D)


def _ffn_up_act(h2, w_gut):
    T = h2.shape[0]
    tm = _tile(T, FFN_TM)

    def body(h_ref, w_ref, ab_ref, act_ref):
        ab = _dot(h_ref[...], w_ref[...], _NT)
        ab_ref[...] = ab
        act_ref[...] = (_silu(ab[:, :FFN_BLK]) * ab[:, FFN_BLK:]).astype(act_ref.dtype)

    return pl.pallas_call(
        body, name="ffn_up_act", grid=(T // tm, DFF // FFN_BLK),
        in_specs=[pl.BlockSpec((tm, D), lambda i, j: (i, 0)), pl.BlockSpec((2 * FFN_BLK, D), lambda i, j: (j, 0))],
        out_specs=[pl.BlockSpec((tm, 2 * FFN_BLK), lambda i, j: (i, j)), pl.BlockSpec((tm, FFN_BLK), lambda i, j: (i, j))],
        out_shape=[_sds((T, 2 * DFF)), _sds((T, DFF), BF16)],
        compiler_params=_cparams(("parallel", "parallel")),
    )(h2, w_gut)


def _ffn_down_dx_act(dffn, w_down, ab):
    T = dffn.shape[0]
    tm = _tile(T, FFN_TM)

    def body(d_ref, w_ref, ab_ref, o_ref):
        dact = _dot(d_ref[...], w_ref[...], _NT)
        a, b = ab_ref[:, :FFN_BLK], ab_ref[:, FFN_BLK:]
        s = _sigmoid(a)
        da = dact * b * (s * (1.0 + a * (1.0 - s)))
        db = dact * (a * s)
        o_ref[...] = jnp.concatenate([da, db], axis=1).astype(o_ref.dtype)

    return pl.pallas_call(
        body, name="ffn_down_dx_act", grid=(T // tm, DFF // FFN_BLK),
        in_specs=[pl.BlockSpec((tm, D), lambda i, j: (i, 0)), pl.BlockSpec((FFN_BLK, D), lambda i, j: (j, 0)),
                  pl.BlockSpec((tm, 2 * FFN_BLK), lambda i, j: (i, j))],
        out_specs=pl.BlockSpec((tm, 2 * FFN_BLK), lambda i, j: (i, j)),
        out_shape=_sds((T, 2 * DFF), BF16),
        compiler_params=_cparams(("parallel", "parallel")),
    )(dffn, w_down, ab)


def _round_bf16(x):
    return x.astype(BF16).astype(F32)


def _shift_down(x, s, rows):
    if s == 0:
        return x
    return jnp.where(rows >= s, pltpu.roll(x, s, 0), 0.0)


def _shift_up(x, s, rows, T):
    if s == 0:
        return x
    return jnp.where(rows < T - s, pltpu.roll(x, T - s, 0), 0.0)


def _conv_fwd(proj, conv_w):
    T = proj.shape[0]
    ncol = 3 * GW // LANE

    def body(x_ref, w_ref, o_ref):
        x = _round_bf16(x_ref[...])
        rows = lax.broadcasted_iota(jnp.int32, x.shape, 0)
        acc = jnp.zeros_like(x)
        for j in range(CONVW):
            acc = acc + _round_bf16(w_ref[pl.ds(j, 1), :]) * _shift_down(x, CONVW - 1 - j, rows)
        o_ref[0], o_ref[1] = _split_pair(_silu(acc))

    return pl.pallas_call(
        body, name="conv_fwd", grid=(ncol,),
        in_specs=[pl.BlockSpec((T, LANE), lambda j: (0, j)), pl.BlockSpec((CONVW, LANE), lambda j: (0, j))],
        out_specs=pl.BlockSpec((2, T, HD), lambda j: (j, 0, 0)),
        out_shape=_sds((3 * GH, T, HD)),
        compiler_params=_cparams(("parallel",)),
    )(proj, conv_w)


RELAYOUT_TM = 4096


def _split_pair(y):
    return y[:, :HD], pltpu.roll(y, HD, 1)[:, :HD]


def _merge_pair(a, b):
    return jnp.concatenate([a, b], axis=1)


def _split_heads(x, col_block0, nheads, name):
    T = x.shape[0]
    tm = _tile(T, RELAYOUT_TM)

    def body(x_ref, o_ref):
        a, b = _split_pair(x_ref[...])
        o_ref[0] = a
        o_ref[1] = b

    return pl.pallas_call(
        body, name=name, grid=(nheads // 2, T // tm),
        in_specs=[pl.BlockSpec((tm, LANE), lambda j, i: (i, col_block0 + j))],
        out_specs=pl.BlockSpec((2, tm, HD), lambda j, i: (j, i, 0)),
        out_shape=_sds((nheads, T, HD), x.dtype),
        compiler_params=_cparams(("parallel", "parallel")),
    )(x)


def _merge_heads(hm, out_dtype, name, into=None, col_block0=0, head0=0, nheads=None):
    T = hm.shape[1]
    nheads = hm.shape[0] if nheads is None else nheads
    tm = _tile(T, RELAYOUT_TM)

    def body(*refs):
        h_ref, o_ref = refs[0], refs[-1]
        o_ref[...] = _merge_pair(h_ref[0], h_ref[1]).astype(o_ref.dtype)

    in_specs = [pl.BlockSpec((2, tm, HD), lambda j, i: (head0 // 2 + j, i, 0))]
    args = [hm]
    if into is None:
        out_shape = _sds((T, HD * nheads), out_dtype)
        aliases = {}
    else:
        out_shape = _sds(into.shape, into.dtype)
        in_specs.append(pl.BlockSpec(memory_space=pl.ANY))
        args.append(into)
        aliases = {1: 0}
    return pl.pallas_call(
        body, name=name, grid=(nheads // 2, T // tm),
        in_specs=in_specs,
        out_specs=pl.BlockSpec((tm, LANE), lambda j, i: (i, col_block0 + j)),
        out_shape=out_shape, input_output_aliases=aliases,
        compiler_params=_cparams(("parallel", "parallel")),
    )(*args)


def _conv_bwd(proj, conv_w, dqc):
    T = proj.shape[0]
    ncol = 3 * GW // LANE

    def body(x_ref, w_ref, d_ref, dx_ref, dw_ref):
        x = _round_bf16(x_ref[...])
        rows = lax.broadcasted_iota(jnp.int32, x.shape, 0)
        xs = [_shift_down(x, CONVW - 1 - j, rows) for j in range(CONVW)]
        w = [_round_bf16(w_ref[pl.ds(j, 1), :]) for j in range(CONVW)]
        pre = jnp.zeros_like(x)
        for j in range(CONVW):
            pre = pre + w[j] * xs[j]
        s = _sigmoid(pre)
        dpre = _round_bf16(_merge_pair(d_ref[0], d_ref[1]) * (s * (1.0 + pre * (1.0 - s))))
        dx = jnp.zeros_like(x)
        for j in range(CONVW):
            dx = dx + w[j] * _shift_up(dpre, CONVW - 1 - j, rows, T)
            dw_ref[pl.ds(j, 1), :] = jnp.sum(dpre * xs[j], axis=0, keepdims=True)
        dx_ref[...] = dx.astype(dx_ref.dtype)

    return pl.pallas_call(
        body, name="conv_bwd", grid=(ncol,),
        in_specs=[pl.BlockSpec((T, LANE), lambda j: (0, j)), pl.BlockSpec((CONVW, LANE), lambda j: (0, j)),
                  pl.BlockSpec((2, T, HD), lambda j: (j, 0, 0))],
        out_specs=[pl.BlockSpec((T, LANE), lambda j: (0, j)), pl.BlockSpec((CONVW, LANE), lambda j: (0, j))],
        out_shape=[_sds((T, NP), BF16), _sds((CONVW, 3 * GW))],
        compiler_params=_cparams(("parallel",)),
    )(proj, conv_w, dqc)


def _gdn_prep(kit, q, k, v, ga, gb, alog, dtb, t_inv=None):
    C = CHUNK
    ri = lax.broadcasted_iota(jnp.int32, (C, C), 0)
    ci = lax.broadcasted_iota(jnp.int32, (C, C), 1)
    causal = ri >= ci
    strict = ri > ci
    eye = (ri == ci).astype(F32)
    lower = causal.astype(F32)
    upper = (ri <= ci).astype(F32)

    a = ga + dtb
    softplus = jnp.maximum(a, 0.0) + jnp.log(1.0 + jnp.exp(-jnp.abs(a)))
    g_row = -jnp.exp(alog) * softplus
    beta_row = _sigmoid(gb)
    g_col = jnp.sum(eye * g_row, axis=2, keepdims=True)
    beta_col = jnp.sum(eye * beta_row, axis=2, keepdims=True)
    G_col = jnp.sum(lower * g_row, axis=2, keepdims=True)
    G_row = jnp.sum(upper * g_col, axis=1, keepdims=True)
    G_last = jnp.sum(g_row, axis=2, keepdims=True)
    decay = jnp.exp(jnp.where(causal, G_col - G_row, -1e30))

    qn = q * lax.rsqrt(jnp.sum(q * q, axis=-1, keepdims=True) + EPS) * (HD ** -0.5)
    kn = k * lax.rsqrt(jnp.sum(k * k, axis=-1, keepdims=True) + EPS)
    kb = kn * beta_col
    A = jnp.where(strict, kit.nt(kb, kn) * decay, 0.0)
    Tm = kit.inv(A, t_inv)
    eG = jnp.exp(G_col)
    u = kit.nn3(Tm, v * beta_col)
    w = kit.nn3(Tm, kb * eG)
    qk = jnp.where(causal, kit.nt(qn, kn) * decay, 0.0)
    q_dec = qn * eG
    k_dec = kn * jnp.exp(G_last - G_col)
    dec = jnp.exp(G_last)
    return u, w, qk, q_dec, k_dec, dec, Tm


def _gdn_out(o, z, nw):
    return _rms(o, nw) * _silu(z)


GDN_CB = 4


def _gdn_specs(T, blk):
    TB = GDN_CB * CHUNK
    seq = lambda grp: pl.BlockSpec((GH, TB, HD), lambda i, grp=grp: (grp, blk(i), 0))
    row = lambda grp: pl.BlockSpec((GH, GDN_CB, 1, CHUNK), lambda i, grp=grp: (grp, blk(i), 0, 0))
    per_head = pl.BlockSpec((GH, 1, CHUNK), lambda i: (0, 0, 0))
    whole = pl.BlockSpec((1, HD), lambda i: (0, 0))
    state = pl.BlockSpec((GH, GDN_CB, HD, HD), lambda i: (0, blk(i), 0, 0))
    return seq, row, per_head, whole, state


def _gdn_load(seq_refs, row_refs, head_refs):
    chunks = lambda r: jnp.concatenate([r[:, pl.ds(cb * CHUNK, CHUNK), :] for cb in range(GDN_CB)], axis=0)
    rows = lambda r: jnp.concatenate([r[:, cb] for cb in range(GDN_CB)], axis=0)
    heads = lambda r: jnp.concatenate([r[...]] * GDN_CB, axis=0)
    return [chunks(r) for r in seq_refs], [rows(r) for r in row_refs], [heads(r) for r in head_refs]


def _gdn_fwd(qkv_hm, zs_hm, gab, alog_b, dtb_b, nw, shards):
    T = qkv_hm.shape[1]
    N = T // CHUNK
    nblk = N // GDN_CB
    ns = len(shards)
    seq, row, per_head, whole, state = _gdn_specs(T, lambda i: i)
    kit = _Kit(False)

    def body(*refs):
        q_ref, k_ref, v_ref, z_ref, ga_ref, gb_ref, al_ref, dt_ref, nw_ref = refs[:9]
        o_ref, S_ref, T_ref = refs[9 + ns:12 + ns]
        S_scr = refs[12 + 2 * ns]
        plan = _gather_plan(refs[9:9 + ns], refs[12 + ns:12 + 2 * ns], *refs[13 + 2 * ns:])

        @pl.when(pl.program_id(0) == 0)
        def _():
            S_scr[...] = jnp.zeros_like(S_scr)
            _start(plan)

        (q, k, v, z), (ga, gb), (al, dt) = _gdn_load((q_ref, k_ref, v_ref, z_ref), (ga_ref, gb_ref), (al_ref, dt_ref))
        u, w, qk, q_dec, k_dec, dec, t_inv = _gdn_prep(kit, q, k, v, ga, gb, al, dt)
        S = S_scr[...]
        for cb in range(GDN_CB):
            hs = slice(cb * GH, (cb + 1) * GH)
            S_ref[:, cb] = S
            T_ref[:, cb] = t_inv[hs]
            v_new = u[hs] - kit.nn(w[hs], S)
            o = kit.nn(q_dec[hs], S) + kit.nn(qk[hs], v_new)
            S = S * dec[hs] + kit.tn(k_dec[hs], v_new)
            o_ref[:, pl.ds(cb * CHUNK, CHUNK), :] = _gdn_out(o, z[hs], nw_ref[...])
        S_scr[...] = S

        @pl.when(pl.program_id(0) == nblk - 1)
        def _():
            _finish(plan)

    res = pl.pallas_call(
        body, name="gdn_fwd", grid=(nblk,),
        in_specs=[seq(0), seq(1), seq(2), seq(0), row(0), row(1), per_head, per_head, whole] + _hbm_specs(ns),
        out_specs=[seq(0), state, state] + _hbm_specs(ns),
        out_shape=[_sds((GH + SQH, T, HD)), _sds((GH, N, HD, HD)), _sds((GH, N, CHUNK, CHUNK))]
                  + _gather_shapes(shards),
        scratch_shapes=[pltpu.VMEM((GH, HD, HD), F32)] + _gather_sems(ns),
        compiler_params=_cparams(("arbitrary",)),
    )(qkv_hm, qkv_hm, qkv_hm, zs_hm, gab, gab, alog_b, dtb_b, nw, *shards)
    return res[0], (res[1], res[2]), res[3:]


def _gdn_bwd(qkv_hm, zs_hm, gab, alog_b, dtb_b, nw, S_all, do, pieces):
    T = qkv_hm.shape[1]
    N = T // CHUNK
    nblk = N // GDN_CB
    npc = len(pieces)
    dkit, kit = _Kit(True), _Kit(False)
    rseq, rrow, per_head, whole, rstate = _gdn_specs(T, lambda i: nblk - 1 - i)

    def body(*refs):
        q_ref, k_ref, v_ref, z_ref, ga_ref, gb_ref, al_ref, dt_ref, nw_ref, S_ref, T_ref, do_ref = refs[:12]
        dqkv_ref, dz_ref, dga_ref, dgb_ref, dal_ref, ddt_ref, dnw_ref = refs[12 + npc:19 + npc]
        dS_scr = refs[19 + 2 * npc]
        plan = _exchange_plan(refs[12:12 + npc], refs[19 + npc:19 + 2 * npc], *refs[20 + 2 * npc:])

        @pl.when(pl.program_id(0) == 0)
        def _():
            dS_scr[...] = jnp.zeros_like(dS_scr)
            dal_ref[...] = jnp.zeros_like(dal_ref)
            ddt_ref[...] = jnp.zeros_like(ddt_ref)
            dnw_ref[...] = jnp.zeros_like(dnw_ref)
            _start(plan)

        (q, k, v, z, dout), (ga, gb), (al, dt) = _gdn_load((q_ref, k_ref, v_ref, z_ref, do_ref), (ga_ref, gb_ref),
                                                          (al_ref, dt_ref))
        S_in = jnp.concatenate([S_ref[:, cb] for cb in range(GDN_CB)], axis=0)
        t_inv = jnp.concatenate([T_ref[:, cb] for cb in range(GDN_CB)], axis=0)
        prep = lambda *a: _gdn_prep(dkit, *a, t_inv=t_inv)[:6]
        (u, w, qk, q_dec, k_dec, dec), prep_vjp = jax.vjp(prep, q, k, v, ga, gb, al, dt)
        v_new = u - kit.nn(w, S_in)
        o = kit.nn(q_dec, S_in) + kit.nn(qk, v_new)
        _, out_vjp = jax.vjp(_gdn_out, o, z, nw_ref[...])
        do, dz, dnw = out_vjp(dout)
        dvn_part = kit.tn(qk, do)
        dS_part = kit.tn(q_dec, do)
        dS = dS_scr[...]
        dS_out, dvn = [None] * GDN_CB, [None] * GDN_CB
        for cb in reversed(range(GDN_CB)):
            hs = slice(cb * GH, (cb + 1) * GH)
            dS_out[cb] = dS
            dvn[cb] = dvn_part[hs] + kit.nn(k_dec[hs], dS)
            dS = dS * dec[hs] + dS_part[hs] - kit.tn(w[hs], dvn[cb])
        dS_scr[...] = dS
        dS_out = jnp.concatenate(dS_out, axis=0)
        dvn = jnp.concatenate(dvn, axis=0)
        ddec = jnp.sum(jnp.sum(S_in * dS_out, axis=2, keepdims=True), axis=1, keepdims=True)
        cts = (dvn, -kit.nt(dvn, S_in), kit.nt(do, v_new), kit.nt(do, S_in), kit.nt(v_new, dS_out), ddec)
        dq, dk, dv, dga, dgb, dal, ddt = prep_vjp(cts)
        lanesum = lambda t: jnp.broadcast_to(jnp.sum(t, axis=2, keepdims=True), t.shape)
        for cb in range(GDN_CB):
            hs = slice(cb * GH, (cb + 1) * GH)
            sl = pl.ds(cb * CHUNK, CHUNK)
            dqkv_ref[pl.ds(0, GH), sl, :] = dq[hs]
            dqkv_ref[pl.ds(GH, GH), sl, :] = dk[hs]
            dqkv_ref[pl.ds(2 * GH, GH), sl, :] = dv[hs]
            dz_ref[:, sl, :] = dz[hs]
            dga_ref[:, cb] = dga[hs]
            dgb_ref[:, cb] = dgb[hs]
            dal_ref[...] += lanesum(dal[hs])
            ddt_ref[...] += lanesum(ddt[hs])
        dnw_ref[...] += dnw

        @pl.when(pl.program_id(0) == nblk - 1)
        def _():
            _finish(plan)

    res = pl.pallas_call(
        body, name="gdn_bwd", grid=(nblk,),
        in_specs=[rseq(0), rseq(1), rseq(2), rseq(0), rrow(0), rrow(1), per_head, per_head, whole, rstate, rstate,
                  rseq(0)] + _hbm_specs(npc),
        out_specs=[pl.BlockSpec((3 * GH, GDN_CB * CHUNK, HD), lambda i: (0, nblk - 1 - i, 0)), rseq(0), rrow(0),
                   rrow(0), per_head, per_head, whole] + _hbm_specs(npc),
        out_shape=[_sds((3 * GH, T, HD)), _sds((GH + 4 + SWA_GRAD_HEADS, T, HD))] + [_sds((GH, N, 1, CHUNK))] * 2
                  + [_sds((GH, 1, CHUNK))] * 2 + [_sds((1, HD))] + _exchange_shapes(pieces),
        scratch_shapes=[pltpu.VMEM((GH, HD, HD), F32)] + _exchange_sems(npc),
        compiler_params=_cparams(("arbitrary",), GDN_BWD_VMEM),
    )(qkv_hm, qkv_hm, qkv_hm, zs_hm, gab, gab, alog_b, dtb_b, nw, S_all[0], S_all[1], do, *pieces)
    return res[:7], res[7:]


def _swa_heads(kit, first, q, kp, kc, vp, vc, qnw, knw, sink, slope):
    W = WIN
    ri = lax.broadcasted_iota(jnp.int32, (W, W), 0)
    ci = lax.broadcasted_iota(jnp.int32, (W, W), 1)
    mask_c = ri >= ci
    mask_p = ci > ri + first * W
    dist_c = (ri - ci).astype(F32)
    dist_p = (ri - ci + W).astype(F32)
    kpn = _rms(kp, knw)
    kcn = _rms(kc, knw)
    qn = _rms(q, qnw)
    sc = jnp.where(mask_c, kit.nt(qn, kcn) * (HD ** -0.5) - slope * dist_c, -1e30)
    sp = jnp.where(mask_p, kit.nt(qn, kpn) * (HD ** -0.5) - slope * dist_p, -1e30)
    m = jnp.maximum(jnp.maximum(jnp.max(sc, axis=-1, keepdims=True), jnp.max(sp, axis=-1, keepdims=True)), sink)
    m = lax.stop_gradient(m)
    pc = jnp.exp(sc - m)
    pp = jnp.exp(sp - m)
    den = jnp.sum(pc, axis=-1, keepdims=True) + jnp.sum(pp, axis=-1, keepdims=True) + jnp.exp(sink - m)
    inv = 1.0 / den
    return kit.nn(pc * inv, vc) + kit.nn(pp * inv, vp)


def _swa_grads(kit, first, q, kp, kc, vp, vc, qnw, knw, sink, slope, do):
    W = WIN
    ri = lax.broadcasted_iota(jnp.int32, (W, W), 0)
    ci = lax.broadcasted_iota(jnp.int32, (W, W), 1)
    mask_c = ri >= ci
    mask_p = ci > ri + first * W
    dist_c = (ri - ci).astype(F32)
    dist_p = (ri - ci + W).astype(F32)
    scale = HD ** -0.5
    kpn, kp_vjp = jax.vjp(_rms, kp, knw)
    kcn, kc_vjp = jax.vjp(_rms, kc, knw)
    qn, q_vjp = jax.vjp(_rms, q, qnw)
    sc = jnp.where(mask_c, kit.nt(qn, kcn) * scale - slope * dist_c, -1e30)
    sp = jnp.where(mask_p, kit.nt(qn, kpn) * scale - slope * dist_p, -1e30)
    m = jnp.maximum(jnp.maximum(jnp.max(sc, axis=-1, keepdims=True), jnp.max(sp, axis=-1, keepdims=True)), sink)
    ec = jnp.exp(sc - m)
    ep = jnp.exp(sp - m)
    es = jnp.exp(sink - m)
    inv = 1.0 / (jnp.sum(ec, axis=-1, keepdims=True) + jnp.sum(ep, axis=-1, keepdims=True) + es)
    pc, pp = ec * inv, ep * inv
    dpc, dpp = kit.nt(do, vc), kit.nt(do, vp)
    delta = jnp.sum(dpc * pc, axis=-1, keepdims=True) + jnp.sum(dpp * pp, axis=-1, keepdims=True)
    dsc = pc * (dpc - delta) * scale
    dsp = pp * (dpp - delta) * scale
    dq, dqnw = q_vjp(kit.nn(dsc, kcn) + kit.nn(dsp, kpn))
    dkc, dknw_c = kc_vjp(kit.tn(dsc, qn))
    dkp, dknw_p = kp_vjp(kit.tn(dsp, qn))
    return dq, dkp, dkc, kit.tn(pp, do), kit.tn(pc, do), dqnw, dknw_c + dknw_p, -(es * inv) * delta


def _per_query_head(kv_ref):
    return jnp.concatenate([kv_ref[pl.ds(h // SGRP, 1)] for h in range(SQH)], axis=0)


def _per_kv_head(d):
    return jnp.concatenate([jnp.sum(d[g * SGRP:(g + 1) * SGRP], axis=0, keepdims=True) for g in range(SKVH)], axis=0)


def _swa_specs(blk):
    qspec = pl.BlockSpec((SQH, WIN, HD), lambda i: (1, blk(i), 0))
    cur = lambda grp: pl.BlockSpec((SKVH, WIN, HD), lambda i, grp=grp: (grp, blk(i), 0))
    prev = lambda grp: pl.BlockSpec((SKVH, WIN, HD), lambda i, grp=grp: (grp, jnp.maximum(blk(i) - 1, 0), 0))
    whole = pl.BlockSpec((1, HD), lambda i: (0, 0))
    col = pl.BlockSpec((SQH, WIN, 1), lambda i: (0, 0, 0))
    ospec = pl.BlockSpec((SQH, WIN, HD), lambda i: (0, blk(i), 0))
    return qspec, cur, prev, whole, col, ospec


def _swa_fwd(zs_hm, qnw, knw, sinks_col, slopes_col, o_buf, shards):
    T = zs_hm.shape[1]
    NB = T // WIN
    ns = len(shards)
    kit = _Kit(False)
    qspec, cur, prev, whole, col, _ = _swa_specs(lambda i: i)

    def body(*refs):
        q_ref, kp_ref, kc_ref, vp_ref, vc_ref, qnw_ref, knw_ref, s_ref, sl_ref = refs[:9]
        o_ref = refs[10 + ns]
        plan = _gather_plan(refs[10:10 + ns], refs[11 + ns:11 + 2 * ns], *refs[11 + 2 * ns:])

        @pl.when(pl.program_id(0) == 0)
        def _():
            _start(plan)

        first = (pl.program_id(0) == 0).astype(jnp.int32)
        o_ref[...] = _swa_heads(kit, first, q_ref[...], _per_query_head(kp_ref), _per_query_head(kc_ref),
                                _per_query_head(vp_ref), _per_query_head(vc_ref), qnw_ref[...], knw_ref[...],
                                s_ref[...], sl_ref[...])

        @pl.when(pl.program_id(0) == NB - 1)
        def _():
            _finish(plan)

    res = pl.pallas_call(
        body, name="swa_fwd", grid=(NB,),
        in_specs=[qspec, prev(8), cur(8), prev(9), cur(9), whole, whole, col, col] + _hbm_specs(1 + ns),
        out_specs=[pl.BlockSpec((SQH, WIN, HD), lambda i: (1, i, 0))] + _hbm_specs(ns),
        out_shape=[_sds(o_buf.shape)] + _gather_shapes(shards),
        input_output_aliases={9: 0},
        scratch_shapes=_gather_sems(ns),
        compiler_params=_cparams(("arbitrary",)),
    )(zs_hm, zs_hm, zs_hm, zs_hm, zs_hm, qnw, knw, sinks_col, slopes_col, o_buf, *shards)
    return res[0], res[1:]


SWA_GRAD_HEADS = SQH + 2 * SKVH


def _swa_bwd(zs_hm, qnw, knw, sinks_col, slopes_col, dmix_hm, d_buf):
    T = zs_hm.shape[1]
    NB = T // WIN
    kit = _Kit(False)
    qspec, cur, prev, whole, col, _ = _swa_specs(lambda i: NB - 1 - i)

    def body(q_ref, kp_ref, kc_ref, vp_ref, vc_ref, qnw_ref, knw_ref, s_ref, sl_ref, do_ref, buf_ref,
             d_ref, dqnw_ref, dknw_ref, ds_ref, ck_scr, cv_scr):
        dq_ref = d_ref.at[pl.ds(0, SQH)]
        dk_ref = d_ref.at[pl.ds(SQH, SKVH)]
        dv_ref = d_ref.at[pl.ds(SQH + SKVH, SKVH)]
        i = pl.program_id(0)
        first = (i == NB - 1).astype(jnp.int32)

        @pl.when(i == 0)
        def _():
            ck_scr[...] = jnp.zeros_like(ck_scr)
            cv_scr[...] = jnp.zeros_like(cv_scr)
            ds_ref[...] = jnp.zeros_like(ds_ref)
            dqnw_ref[...] = jnp.zeros_like(dqnw_ref)
            dknw_ref[...] = jnp.zeros_like(dknw_ref)

        dq, dkp, dkc, dvp, dvc, dqnw, dknw, dsink = _swa_grads(
            kit, first, q_ref[...], _per_query_head(kp_ref), _per_query_head(kc_ref), _per_query_head(vp_ref),
            _per_query_head(vc_ref), qnw_ref[...], knw_ref[...], s_ref[...], sl_ref[...], do_ref[...])
        dq_ref[...] = dq
        dk_ref[...] = _per_kv_head(dkc) + ck_scr[...]
        dv_ref[...] = _per_kv_head(dvc) + cv_scr[...]
        ck_scr[...] = _per_kv_head(dkp)
        cv_scr[...] = _per_kv_head(dvp)
        dqnw_ref[...] += dqnw
        dknw_ref[...] += dknw
        ds_ref[...] += jnp.broadcast_to(jnp.sum(dsink, axis=1, keepdims=True), dsink.shape)

    dospec = pl.BlockSpec((SQH, WIN, HD), lambda i: (1, NB - 1 - i, 0))
    dspec = pl.BlockSpec((SWA_GRAD_HEADS, WIN, HD), lambda i: (1, NB - 1 - i, 0))
    res = pl.pallas_call(
        body, name="swa_bwd", grid=(NB,),
        in_specs=[qspec, prev(8), cur(8), prev(9), cur(9), whole, whole, col, col, dospec] + _hbm_specs(1),
        out_specs=[dspec, whole, whole, col],
        out_shape=[_sds(d_buf.shape), _sds((1, HD)), _sds((1, HD)), _sds((SQH, WIN, 1))],
        input_output_aliases={10: 0},
        scratch_shapes=[pltpu.VMEM((SKVH, WIN, HD), F32), pltpu.VMEM((SKVH, WIN, HD), F32)],
        compiler_params=_cparams(("arbitrary",)),
    )(zs_hm, zs_hm, zs_hm, zs_hm, zs_hm, qnw, knw, sinks_col, slopes_col, dmix_hm, d_buf)
    return res


GAB0 = 3 * GW + 1280


W_IN_ROWS = PROJ // N_CHIP
W_IN_ROWS_PAD = 736


def _permute_w_in_t(w_in_t):
    return jnp.concatenate([w_in_t[:4 * GW], w_in_t[4 * GW + 2 * GH:], w_in_t[4 * GW:4 * GW + 2 * GH],
                            jnp.zeros((NP - PROJ, D), w_in_t.dtype)], axis=0)


def _w_in_grad_pieces(g_t):
    g = jnp.concatenate([g_t[:4 * GW], g_t[GAB0:GAB0 + 2 * GH], g_t[4 * GW:GAB0]], axis=0)
    g = jnp.pad(g.reshape(N_CHIP, W_IN_ROWS, D), ((0, 0), (0, W_IN_ROWS_PAD - W_IN_ROWS), (0, 0)))
    return g.reshape(N_CHIP, 2, W_IN_ROWS_PAD // 2, D)


def _pieces_by_rows(g):
    return g.reshape(N_CHIP, 2, g.shape[0] // (2 * N_CHIP), D)


def _local_step(x, target, mod, n1w, w_in_pt, conv_w, alog, dtb, gnw, qnw, knw, sinks, n2w, shards):
    sh_out, sh_gate, sh_up, sh_down = shards
    T = x.shape[0]
    N = T // CHUNK
    shift1, scale1, gate1, shift2, scale2, gate2 = [mod[:, i * D:(i + 1) * D] for i in range(6)]

    h, proj, (a_down,) = _norm_in_proj(x, n1w, scale1, shift1, w_in_pt, [sh_down])
    w_down = a_down.reshape(DFF, D)
    qkv_hm = _conv_fwd(proj, conv_w)
    zs_hm = _split_heads(proj, 3 * GW // LANE, 20, "split_zs")
    gab = proj[:, GAB0:GAB0 + 2 * GH].T.reshape(2 * GH, N, 1, CHUNK)
    alog_b = jnp.broadcast_to(alog.reshape(GH, 1, 1), (GH, 1, CHUNK))
    dtb_b = jnp.broadcast_to(dtb.reshape(GH, 1, 1), (GH, 1, CHUNK))
    sinks_col = jnp.broadcast_to(sinks.reshape(SQH, 1, 1), (SQH, WIN, 1))
    o_hm, S_all, (a_gate, a_up) = _gdn_fwd(qkv_hm, zs_hm, gab, alog_b, dtb_b, gnw, [sh_gate, sh_up])
    w_gut = _interleave_gate_up(a_gate.reshape(DFF, D), a_up.reshape(DFF, D))
    slopes = 2.0 ** (-8.0 * (jnp.arange(SQH, dtype=F32) + 1.0) / SQH)
    slopes_col = jnp.broadcast_to(slopes.reshape(SQH, 1, 1), (SQH, WIN, 1))
    o_hm, (a_out,) = _swa_fwd(zs_hm, qnw, knw, sinks_col, slopes_col, o_hm, [sh_out])
    w_out = a_out.reshape(D, D)
    mixcat = _merge_heads(o_hm, BF16, "merge_mix")
    mixed, x1, h2 = _out_proj_resid_norm(mixcat, w_out, x, gate1, n2w, scale2, shift2)
    ab, act = _ffn_up_act(h2, w_gut)
    dy, dffn, dgate2, loss = _ffn_down_loss(act, w_down, x1, target, gate2)

    dab = _ffn_down_dx_act(dffn, w_down, ab)
    g_w_down = _matmul(act, dffn, ta=True, out_dtype=BF16, name="ffn_down_dw")
    g_w_gut = _matmul(dab, h2, ta=True, out_dtype=BF16, name="ffn_up_dw")
    dx1, dmixed, dgate1, dn2w, dscale2, dshift2 = _ffn_up_dx_resid_bwd(dab, w_gut, x, mixed, dy, gate1, n2w, scale2,
                                                                       shift2)
    g_w_out = _matmul(mixcat, dmixed, ta=True, out_dtype=BF16, name="out_proj_dw")
    dmix_hm = _split_heads(_matmul(dmixed, w_out, tb=True, name="out_proj_dx"), 0, GH + SQH, "split_dmix")
    g_gate_t, g_up_t = _split_gate_up(g_w_gut)
    pieces = [_pieces_by_rows(g_w_out), _pieces_by_rows(g_gate_t), _pieces_by_rows(g_up_t),
              _pieces_by_rows(g_w_down)]
    (dqkv_hm, d_hm, dga, dgb, dalog, ddtb, dgnw), recv = _gdn_bwd(qkv_hm, zs_hm, gab, alog_b, dtb_b, gnw, S_all,
                                                                  dmix_hm, pieces)
    d_hm, dqnw, dknw, dsinks = _swa_bwd(zs_hm, qnw, knw, sinks_col, slopes_col, dmix_hm, d_hm)
    dproj, dconv = _conv_bwd(proj, conv_w, dqkv_hm)
    dproj = _merge_heads(d_hm, BF16, "merge_dz", into=dproj, col_block0=3 * GW // LANE, head0=0, nheads=GH)
    dproj = _merge_heads(d_hm, BF16, "merge_dswa", into=dproj, col_block0=4 * GW // LANE, head0=GH + 4,
                         nheads=SWA_GRAD_HEADS)
    dgab = jnp.concatenate([dga, dgb], axis=0).reshape(2 * GH, T).T.astype(BF16)
    dproj = lax.dynamic_update_slice(dproj, jnp.concatenate([dgab, jnp.zeros((T, NP - PROJ), BF16)], axis=1),
                                     (0, GAB0))
    g_w_in_pt = _matmul(dproj, h, ta=True, out_dtype=BF16, name="in_proj_dw")
    (grad_x, dn1w, dscale1, dshift1), recv_in = _in_proj_dx_norm_bwd(dproj, w_in_pt, x, dx1, n1w, scale1, shift1,
                                                                     [_w_in_grad_pieces(g_w_in_pt)])

    dmod = jnp.concatenate([dshift1, dscale1, dgate1, dshift2, dscale2, dgate2], axis=1)
    big = list(recv_in) + list(recv)
    small = dict(mod=dmod, norm1_w=dn1w, norm2_w=dn2w, conv_w=dconv, a_log=dalog[:, 0, 0], dt_bias=ddtb[:, 0, 0],
                 gdn_norm_w=dgnw, q_norm_w=dqnw, k_norm_w=dknw, sinks=dsinks[:, 0, 0])
    return loss, grad_x, big, small


def _adamw(w, g, m, v):
    m2 = ADAM_B1 * m + (1.0 - ADAM_B1) * g
    v2 = ADAM_B2 * v + (1.0 - ADAM_B2) * (g * g)
    m_hat = m2 / (1.0 - ADAM_B1 ** ADAM_STEP)
    v_hat = v2 / (1.0 - ADAM_B2 ** ADAM_STEP)
    delta = -ADAM_LR * (m_hat / (jnp.sqrt(v_hat) + ADAM_EPS) + ADAM_WD * w)
    return delta, m2, v2


def _reduce_adamw(recv, w, m, v, name):
    _, R, C = recv.shape
    tc = _tile(C, 256)

    def body(r_ref, w_ref, m_ref, v_ref, o_ref):
        g = r_ref[0].astype(F32)
        for s in range(1, N_DEV):
            g = g + r_ref[s].astype(F32)
        delta, m2, v2 = _adamw(w_ref[...], g, m_ref[...], v_ref[...])
        o_ref[0] = g
        o_ref[1] = delta
        o_ref[2] = m2
        o_ref[3] = v2

    col = pl.BlockSpec((R, tc), lambda j: (0, j))
    return pl.pallas_call(
        body, name=name, grid=(C // tc,),
        in_specs=[pl.BlockSpec((N_DEV, R, tc), lambda j: (0, 0, j)), col, col, col],
        out_specs=pl.BlockSpec((4, R, tc), lambda j: (0, 0, j)),
        out_shape=_sds((4, R, C)),
        compiler_params=_cparams(("parallel",)),
    )(recv, w, m, v)


def _adamw_call(g, w, m, v, name):
    def body(g_ref, w_ref, m_ref, v_ref, o_ref):
        delta, m2, v2 = _adamw(w_ref[...], g_ref[...], m_ref[...], v_ref[...])
        o_ref[0] = delta
        o_ref[1] = m2
        o_ref[2] = v2

    return pl.pallas_call(body, name=name, out_shape=_sds((3,) + g.shape))(g, w, m, v)


ADA_N = 6 * D // N_CHIP
KPAD = 128


def _w_ada_update(c8p, dm, w, m, v):
    tr = 256

    def body(c_ref, dm_ref, w_ref, m_ref, v_ref, g_ref, d_ref, m2_ref, v2_ref):
        g = _raw1(_silu(c_ref[...]), dm_ref[...], _TN)
        delta, m2, v2 = _adamw(w_ref[...], g, m_ref[...], v_ref[...])
        g_ref[...] = g
        d_ref[...] = delta
        m2_ref[...] = m2
        v2_ref[...] = v2

    blk = pl.BlockSpec((tr, ADA_N), lambda i: (i, 0))
    return pl.pallas_call(
        body, name="w_ada_update", grid=(D // tr,),
        in_specs=[pl.BlockSpec((KPAD, tr), lambda i: (0, i)), pl.BlockSpec((KPAD, ADA_N), lambda i: (0, 0)),
                  blk, blk, blk],
        out_specs=[blk] * 4, out_shape=[_sds((D, ADA_N))] * 4,
        compiler_params=_cparams(("parallel",)),
    )(c8p, dm, w, m, v)


def _me():
    return lax.axis_index("x"), lax.axis_index("y"), lax.axis_index("c")


def _peer(k, me):
    mx, my, mc = me
    return (1 - mx if k & 4 else mx, 1 - my if k & 2 else my, 1 - mc if k & 1 else mc)


def _lin(p):
    return 4 * p[0] + 2 * p[1] + p[2]


def _remote(src, dst, ssem, rsem, dev):
    return pltpu.make_async_remote_copy(src_ref=src, dst_ref=dst, send_sem=ssem, recv_sem=rsem,
                                        device_id=dev, device_id_type=MESH)


def _all_gather8(x, name):
    def body(x_ref, out_ref, send_sems, recv_sems):
        me = _me()
        out_ref[_lin(me)] = x_ref[...]
        sends = []
        for k in range(1, N_DEV):
            cp = _remote(x_ref, out_ref.at[_lin(me)], send_sems.at[k - 1], recv_sems.at[k - 1], _peer(k, me))
            cp.start()
            sends.append(cp)
        for k in range(1, N_DEV):
            p = _peer(k, me)
            _remote(x_ref, out_ref.at[_lin(p)], send_sems.at[k - 1], recv_sems.at[k - 1], p).wait_recv()
        for cp in sends:
            cp.wait_send()

    return pl.pallas_call(
        body, name=name,
        out_shape=_sds((N_DEV,) + x.shape, x.dtype),
        in_specs=[pl.BlockSpec(memory_space=pltpu.VMEM)],
        out_specs=pl.BlockSpec(memory_space=pltpu.VMEM),
        scratch_shapes=[pltpu.SemaphoreType.DMA((N_DEV - 1,)), pltpu.SemaphoreType.DMA((N_DEV - 1,))],
    )(x)


def _ag8_plan(src, out, send_sems, recv_sems):
    me = _me()
    sends, recvs = [], []
    for k in range(1, N_DEV):
        p = _peer(k, me)
        sends.append(_remote(src, out.at[_lin(me)], send_sems.at[k - 1], recv_sems.at[k - 1], p))
        recvs.append(_remote(src, out.at[_lin(p)], send_sems.at[k - 1], recv_sems.at[k - 1], p))
    return [], sends, recvs


def _prologue(c_row, conv_sh, w_ada, b_sh, w_in_sh):
    def body(c_ref, cv_ref, wa_ref, b_ref, win_ref, call_ref, cvall_ref, mods_ref, ain_ref, c16_scr, mp_scr,
             c_send, c_recv, cv_send, cv_recv, m_send, m_recv, w_send, w_recv, w_local):
        me = _lin(_me())
        w_plan = _gather_plan([win_ref], [ain_ref], w_send, w_recv, w_local)
        _start(w_plan)
        c_plan = _ag8_plan(c_ref, call_ref, c_send, c_recv)
        cv_plan = _ag8_plan(cv_ref, cvall_ref, cv_send, cv_recv)
        call_ref[me] = c_ref[...]
        cvall_ref[me] = cv_ref[...]
        _start(c_plan)
        _start(cv_plan)
        _finish(c_plan)
        c16_scr[...] = jnp.zeros_like(c16_scr)
        for d in range(N_DEV):
            c16_scr[pl.ds(d, 1), :] = call_ref[d]
        mp_scr[...] = _raw1(_silu(c16_scr[...]), wa_ref[...], _NN) + b_ref[...]
        mods_ref[me] = mp_scr[...]
        m_plan = _ag8_plan(mp_scr, mods_ref, m_send, m_recv)
        _start(m_plan)
        _finish(cv_plan)
        _finish(m_plan)
        _finish(w_plan)

    vmem = pl.BlockSpec(memory_space=pltpu.VMEM)
    sems = lambda n: pltpu.SemaphoreType.DMA((n,))
    return pl.pallas_call(
        body, name="prologue",
        in_specs=[vmem] * 4 + _hbm_specs(1), out_specs=[vmem] * 3 + _hbm_specs(1),
        out_shape=[_sds((N_DEV,) + c_row.shape), _sds((N_DEV,) + conv_sh.shape), _sds((N_DEV, 16, ADA_N)),
                   _sds((N_CHIP,) + w_in_sh.shape, w_in_sh.dtype)],
        scratch_shapes=[pltpu.VMEM((16, D), F32), pltpu.VMEM((16, ADA_N), F32)] + [sems(N_DEV - 1)] * 6
                       + _gather_sems(1),
        compiler_params=_cparams(),
    )(c_row, conv_sh, w_ada, b_sh, w_in_sh)


def _hbm_specs(n):
    return [pl.BlockSpec(memory_space=pl.ANY)] * n


def _gather_shapes(shards):
    return [_sds((N_CHIP,) + s.shape, s.dtype) for s in shards]


def _gather_sems(n):
    return [pltpu.SemaphoreType.DMA((3 * n,)), pltpu.SemaphoreType.DMA((3 * n,)), pltpu.SemaphoreType.DMA((n,))]


def _gather_plan(ins, outs, send_sems, recv_sems, local_sems):
    mx, my, mc = _me()
    chips = [(1 - mx, my), (mx, 1 - my), (1 - mx, 1 - my)]
    local, sends, recvs = [], [], []
    for a in range(len(ins)):
        local.append(pltpu.make_async_copy(ins[a], outs[a].at[2 * mx + my], local_sems.at[a]))
        for k, (px, py) in enumerate(chips):
            sems = (send_sems.at[3 * a + k], recv_sems.at[3 * a + k], (px, py, mc))
            sends.append(_remote(ins[a], outs[a].at[2 * mx + my], *sems))
            recvs.append(_remote(ins[a], outs[a].at[2 * px + py], *sems))
    return local, sends, recvs


def _start(plan):
    local, sends, _ = plan
    for cp in local + sends:
        cp.start()


def _finish(plan):
    local, sends, recvs = plan
    for cp in recvs:
        cp.wait_recv()
    for cp in sends:
        cp.wait_send()
    for cp in local:
        cp.wait()


def _exchange_shapes(pieces):
    return [_sds((N_DEV,) + p.shape[2:], p.dtype) for p in pieces]


def _exchange_sems(n):
    return [pltpu.SemaphoreType.DMA(((N_DEV - 1) * n,)), pltpu.SemaphoreType.DMA(((N_DEV - 1) * n,)),
            pltpu.SemaphoreType.DMA((n,))]


def _exchange_plan(ins, outs, send_sems, recv_sems, local_sems):
    me = _me()
    mx, my, mc = me
    local, sends, recvs = [], [], []
    for a in range(len(ins)):
        local.append(pltpu.make_async_copy(ins[a].at[2 * mx + my, mc], outs[a].at[_lin(me)], local_sems.at[a]))
        for k in range(1, N_DEV):
            p = _peer(k, me)
            s = (N_DEV - 1) * a + k - 1
            sends.append(_remote(ins[a].at[2 * p[0] + p[1], p[2]], outs[a].at[_lin(me)], send_sems.at[s],
                                 recv_sems.at[s], p))
            recvs.append(_remote(ins[a].at[2 * mx + my, mc], outs[a].at[_lin(p)], send_sems.at[s],
                                 recv_sems.at[s], p))
    return local, sends, recvs


REDUCE_VMEM = 56 * 1024 * 1024


def _reduce_swap(recvs):
    n = len(recvs)

    def body(*refs):
        r_refs, o_refs = refs[:n], refs[n:2 * n]
        send_sems, recv_sems = refs[2 * n:]
        mx, my, mc = _me()
        sib = (mx, my, 1 - mc)
        half = lambda a, c: o_refs[a].at[pl.ds(pl.multiple_of(c * recvs[a].shape[1], 8), recvs[a].shape[1])]
        sends = []
        for a in range(n):
            g = r_refs[a][0].astype(F32)
            for s in range(1, N_DEV):
                g = g + r_refs[a][s].astype(F32)
            half(a, mc)[...] = g
            cp = _remote(half(a, mc), half(a, mc), send_sems.at[a], recv_sems.at[a], sib)
            cp.start()
            sends.append(cp)
        for a in range(n):
            _remote(half(a, mc), half(a, 1 - mc), send_sems.at[a], recv_sems.at[a], sib).wait_recv()
        for cp in sends:
            cp.wait_send()

    vmem = pl.BlockSpec(memory_space=pltpu.VMEM)
    return pl.pallas_call(
        body, name="reduce_swap", out_shape=[_sds((2 * r.shape[1], r.shape[2])) for r in recvs],
        in_specs=[vmem] * n, out_specs=[vmem] * n,
        scratch_shapes=[pltpu.SemaphoreType.DMA((n,)), pltpu.SemaphoreType.DMA((n,))],
        compiler_params=_cparams(None, REDUCE_VMEM),
    )(*recvs)


def _adamw_big(g, w, m, v, name):
    rows, cols = g.shape
    tr = next((t for t in (256, 176, 128, 64, 8) if rows % t == 0), None)
    if tr is None:
        tc = _tile(cols, 256)
        blk, grid = pl.BlockSpec((rows, tc), lambda i: (0, i)), (cols // tc,)
    else:
        blk, grid = pl.BlockSpec((tr, cols), lambda i: (i, 0)), (rows // tr,)

    def body(g_ref, w_ref, m_ref, v_ref, go_ref, d_ref, m2_ref, v2_ref):
        g = g_ref[...]
        delta, m2, v2 = _adamw(w_ref[...], g, m_ref[...], v_ref[...])
        go_ref[...] = g
        d_ref[...] = delta
        m2_ref[...] = m2
        v2_ref[...] = v2

    return pl.pallas_call(
        body, name=name, grid=grid,
        in_specs=[blk] * 4, out_specs=[blk] * 4, out_shape=[_sds((rows, cols))] * 4,
        compiler_params=_cparams(("parallel",)),
    )(g, w, m, v)


SMALL_ORDER = (("mod", 6 * D), ("norm1_w", D), ("norm2_w", D), ("conv_w", CONVW * 3 * GW), ("a_log", GH),
               ("dt_bias", GH), ("gdn_norm_w", HD), ("q_norm_w", HD), ("k_norm_w", HD), ("sinks", SQH), ("loss", 1))
SMALL_R = 120


def _pack_small(d):
    parts = [d[k].reshape(-1).astype(F32) if k in d else jnp.zeros((n,), F32) for k, n in SMALL_ORDER]
    used = sum(n for _, n in SMALL_ORDER)
    parts.append(jnp.zeros((SMALL_R * LANE - used,), F32))
    return jnp.concatenate(parts).reshape(SMALL_R, LANE)


def _unpack_small(pk):
    flat = pk.reshape(-1)
    out, r = {}, 0
    for k, n in SMALL_ORDER:
        out[k] = flat[r:r + n]
        r += n
    return out


def kernel(x, c, w_ada, b_ada, norm1_w, w_in, conv_w, a_log, dt_bias, gdn_norm_w, q_norm_w, k_norm_w, sinks, w_out, norm2_w, w_gate, w_up, w_down, loss_target, m_w_ada, m_b_ada, m_norm1_w, m_w_in, m_conv_w, m_a_log, m_dt_bias, m_gdn_norm_w, m_q_norm_w, m_k_norm_w, m_sinks, m_w_out, m_norm2_w, m_w_gate, m_w_up, m_w_down, v_w_ada, v_b_ada, v_norm1_w, v_w_in, v_conv_w, v_a_log, v_dt_bias, v_gdn_norm_w, v_q_norm_w, v_k_norm_w, v_sinks, v_w_out, v_norm2_w, v_w_gate, v_w_up, v_w_down):
    mx, my, mc = _me()
    chip = 2 * mx + my
    dev = 4 * mx + 2 * my + mc
    T = x.shape[1]

    as_rows = lambda t, transposed: t[0].T if transposed else t[0]
    transposed = (True, False, True, True, False)
    big_w = [as_rows(t, tr) for t, tr in zip((w_in, w_out, w_gate, w_up, w_down), transposed)]
    shards = [t.astype(BF16) for t in big_w]

    b_sh = lax.dynamic_slice(b_ada, (0, chip * ADA_N), (1, ADA_N))
    c_all, conv_all, mods, a_in = _prologue(c, conv_w.reshape(CONVW, 3 * GW // N_CHIP), w_ada[0], b_sh, shards[0])
    c8 = c_all.reshape(N_DEV, D)
    conv_full = jnp.concatenate([conv_all[2 * j] for j in range(N_CHIP)], axis=1)
    mod = jnp.concatenate([lax.dynamic_slice(mods[2 * j], (dev, 0), (1, ADA_N)) for j in range(N_CHIP)], axis=1)
    w_in_pt = _permute_w_in_t(a_in.reshape(PROJ, D))

    loss, grad_x, big, small = _local_step(
        x[0], loss_target[0], mod, norm1_w, w_in_pt, conv_full, a_log, dt_bias, gdn_norm_w,
        q_norm_w, k_norm_w, sinks, norm2_w, shards[1:])

    small["loss"] = loss[:, :1]
    sg = _all_gather8(_pack_small(small), "gather_small_grads")
    rep = dict(mod=(b_ada, m_b_ada, v_b_ada), norm1_w=(norm1_w, m_norm1_w, v_norm1_w),
               norm2_w=(norm2_w, m_norm2_w, v_norm2_w), a_log=(a_log, m_a_log, v_a_log),
               dt_bias=(dt_bias, m_dt_bias, v_dt_bias), gdn_norm_w=(gdn_norm_w, m_gdn_norm_w, v_gdn_norm_w),
               q_norm_w=(q_norm_w, m_q_norm_w, v_q_norm_w), k_norm_w=(k_norm_w, m_k_norm_w, v_k_norm_w),
               sinks=(sinks, m_sinks, v_sinks))
    wmv = [_pack_small({k: t[i] for k, t in rep.items()}) for i in range(3)]
    sres = _reduce_adamw(sg, wmv[0], wmv[1], wmv[2], "small_reduce_adamw")
    s_g, s_d, s_m, s_v = [_unpack_small(sres[i]) for i in range(4)]
    loss_out = s_g["loss"][0]

    g_conv = lax.dynamic_slice(s_g["conv_w"].reshape(CONVW, 3 * GW), (0, chip * (3 * GW // N_CHIP)),
                               (CONVW, 3 * GW // N_CHIP))
    pad16 = lambda t: jnp.concatenate([t.reshape(12, LANE), jnp.zeros((4, LANE), F32)], axis=0)
    cres = _adamw_call(pad16(g_conv), pad16(conv_w), pad16(m_conv_w), pad16(v_conv_w), "conv_adamw")
    conv_out = [g_conv.reshape(conv_w.shape)] + [cres[i, :12].reshape(conv_w.shape) for i in range(3)]

    dmod8 = sg[:, :6 * D // LANE].reshape(N_DEV, 6 * D)
    dm = lax.dynamic_slice(dmod8, (0, chip * ADA_N), (N_DEV, ADA_N))
    zpad = lambda t: jnp.concatenate([t, jnp.zeros((KPAD - N_DEV, t.shape[1]), F32)], axis=0)
    ares = _w_ada_update(zpad(c8), zpad(dm), w_ada[0], m_w_ada[0], v_w_ada[0])

    names = ("w_in", "w_out", "w_gate", "w_up", "w_down")
    g_full = list(_reduce_swap(big))
    g_full[0] = g_full[0][:W_IN_ROWS]
    big_m = [as_rows(t, tr) for t, tr in zip((m_w_in, m_w_out, m_w_gate, m_w_up, m_w_down), transposed)]
    big_v = [as_rows(t, tr) for t, tr in zip((v_w_in, v_w_out, v_w_gate, v_w_up, v_w_down), transposed)]
    upd = [_adamw_big(g, w, m, v, "adamw_" + nm) for g, w, m, v, nm in zip(g_full, big_w, big_m, big_v, names)]
    back = lambda t, tr: (t.T if tr else t)[None]
    bg, bd, bm, bv = [[back(u[i], tr) for u, tr in zip(upd, transposed)] for i in range(4)]

    def group(a_i, small_d, conv_i, big_l):
        s = lambda k, ref: small_d[k].reshape(ref.shape)
        return [ares[a_i][None], s("mod", b_ada), s("norm1_w", norm1_w), big_l[0], conv_out[conv_i],
                s("a_log", a_log), s("dt_bias", dt_bias), s("gdn_norm_w", gdn_norm_w), s("q_norm_w", q_norm_w),
                s("k_norm_w", k_norm_w), s("sinks", sinks), big_l[1], s("norm2_w", norm2_w), big_l[2], big_l[3],
                big_l[4]]

    outs = [loss_out, grad_x[None]]
    outs += group(0, s_g, 0, bg) + group(1, s_d, 1, bd) + group(2, s_m, 2, bm) + group(3, s_v, 3, bv)
    return tuple(outs)
```

```python
import jax
import jax.numpy as jnp
from jax import lax
from jax.experimental import pallas as pl
from jax.experimental.pallas import tpu as pltpu

F32 = jnp.float32
BF16 = jnp.bfloat16
MESH = pl.DeviceIdType.MESH

D = 1024
HD = 64
GH = 8
GW = GH * HD
SQH = 8
SKVH = 2
SGRP = SQH // SKVH
WIN = 128
CONVW = 4
CHUNK = 64
DFF = 2816
PROJ = 2832
NP = 3072
EPS = 1e-6
N_DEV = 8
N_CHIP = 4

ADAM_LR = 0.001
ADAM_B1 = 0.9
ADAM_B2 = 0.999
ADAM_EPS = 1e-08
ADAM_WD = 0.01
ADAM_STEP = 10

VMEM_LIMIT = 48 * 1024 * 1024
GDN_BWD_VMEM = 58 * 1024 * 1024
LANE = 128


def _cparams(sem=None, vmem=VMEM_LIMIT):
    return pltpu.CompilerParams(dimension_semantics=sem, vmem_limit_bytes=vmem)


_NN = ((1,), (0,))
_NT = ((1,), (1,))
_TN = ((0,), (0,))


def _dot(a, b, dims):
    if a.ndim == 3:
        (ca,), (cb,) = dims
        return lax.dot_general(a, b, (((ca + 1,), (cb + 1,)), ((0,), (0,))), preferred_element_type=F32)
    return lax.dot_general(a, b, (dims, ((), ())), preferred_element_type=F32)


def _raw1(a, b, dims):
    return _dot(a.astype(BF16), b.astype(BF16), dims)


def _raw3(a, b, dims):
    ah = a.astype(BF16)
    al = (a - ah.astype(F32)).astype(BF16)
    bh = b.astype(BF16)
    bl = (b - bh.astype(F32)).astype(BF16)
    return _dot(ah, bh, dims) + (_dot(al, bh, dims) + _dot(ah, bl, dims))


def _make_diff_mm(raw):
    @jax.custom_vjp
    def nn(a, b):
        return raw(a, b, _NN)

    @jax.custom_vjp
    def nt(a, b):
        return raw(a, b, _NT)

    @jax.custom_vjp
    def tn(a, b):
        return raw(a, b, _TN)

    nn.defvjp(lambda a, b: (raw(a, b, _NN), (a, b)), lambda r, g: (nt(g, r[1]), tn(r[0], g)))
    nt.defvjp(lambda a, b: (raw(a, b, _NT), (a, b)), lambda r, g: (nn(g, r[1]), tn(g, r[0])))
    tn.defvjp(lambda a, b: (raw(a, b, _TN), (a, b)), lambda r, g: (nt(r[1], g), nn(r[0], g)))
    return nn, nt, tn


def _tri_inv_raw(a, nn3):
    n = a.shape[-1]
    ri = lax.broadcasted_iota(jnp.int32, (n, n), 0)
    ci = lax.broadcasted_iota(jnp.int32, (n, n), 1)
    t = (ri == ci).astype(F32)
    for lvl in range((n - 1).bit_length()):
        same_pair = (ri >> (lvl + 1)) == (ci >> (lvl + 1))
        lower_left = (((ri >> lvl) & 1) == 1) & (((ci >> lvl) & 1) == 0)
        y = jnp.where(same_pair & lower_left, a, 0.0)
        t = t - y if lvl == 0 else t - nn3(nn3(t, y), t)
    return t


class _Kit:
    def __init__(self, diff):
        if diff:
            self.nn, self.nt, self.tn = _make_diff_mm(_raw1)
            self.nn3, self.nt3, self.tn3 = _make_diff_mm(_raw3)
            nn3, nt3, tn3 = self.nn3, self.nt3, self.tn3

            @jax.custom_vjp
            def inv(a, t):
                return t

            def inv_fwd(a, t):
                return t, t

            def inv_bwd(t, g):
                return -tn3(t, nt3(g, t)), jnp.zeros_like(t)

            inv.defvjp(inv_fwd, inv_bwd)
            self.inv = inv
        else:
            self.nn = lambda a, b: _raw1(a, b, _NN)
            self.nt = lambda a, b: _raw1(a, b, _NT)
            self.tn = lambda a, b: _raw1(a, b, _TN)
            self.nn3 = lambda a, b: _raw3(a, b, _NN)
            self.nt3 = lambda a, b: _raw3(a, b, _NT)
            self.tn3 = lambda a, b: _raw3(a, b, _TN)
            self.inv = lambda a, t: _tri_inv_raw(a, self.nn3) if t is None else t


def _sigmoid(x):
    return 1.0 / (1.0 + jnp.exp(-x))


def _silu(x):
    return x * _sigmoid(x)


def _rms(x, w):
    return x * lax.rsqrt(jnp.mean(x * x, axis=-1, keepdims=True) + EPS) * w


def _tile(dim, target):
    t = (min(dim, target) // LANE) * LANE
    while t >= LANE:
        if dim % t == 0:
            return t
        t -= LANE
    return dim


MM_TM, MM_TN, MM_TK = 1408, 1536, 1408


def _matmul(a, b, ta=False, tb=False, out_dtype=F32, name="matmul", gather=None, exchange=None):
    carried = gather if gather is not None else exchange if exchange is not None else []
    nc = len(carried)
    if ta:
        K, M = a.shape
    else:
        M, K = a.shape
    if tb:
        N, K2 = b.shape
    else:
        K2, N = b.shape
    assert K == K2, (a.shape, b.shape, ta, tb)
    tm, tn, tk = _tile(M, MM_TM), _tile(N, MM_TN), _tile(K, MM_TK)
    nk = K // tk
    dims = ((0,) if ta else (1,), (1,) if tb else (0,))

    grid = (M // tm, N // tn, nk)

    def body(*refs):
        a_ref, b_ref = refs[:2]
        o_ref = refs[2 + nc]
        scratch = refs[3 + 2 * nc:]
        k = pl.program_id(2)
        if nc:
            make_plan = _gather_plan if gather is not None else _exchange_plan
            plan = make_plan(refs[2:2 + nc], refs[3 + nc:3 + 2 * nc], *scratch[-3:])
            at = lambda pos: ((pl.program_id(0) == pos[0]) & (pl.program_id(1) == pos[1]) & (k == pos[2]))

            @pl.when(at((0, 0, 0)))
            def _():
                _start(plan)

        part = _dot(a_ref[...].astype(BF16), b_ref[...].astype(BF16), dims)
        if nk == 1:
            o_ref[...] = part.astype(o_ref.dtype)
        else:
            acc_ref = scratch[0]

            @pl.when(k == 0)
            def _():
                acc_ref[...] = part

            @pl.when((k > 0) & (k < nk - 1))
            def _():
                acc_ref[...] += part

            @pl.when(k == nk - 1)
            def _():
                o_ref[...] = (acc_ref[...] + part).astype(o_ref.dtype)

        if nc:
            @pl.when(at((grid[0] - 1, grid[1] - 1, nk - 1)))
            def _():
                _finish(plan)

    a_spec = (pl.BlockSpec((tk, tm), lambda i, j, k: (k, i)) if ta
              else pl.BlockSpec((tm, tk), lambda i, j, k: (i, k)))
    b_spec = (pl.BlockSpec((tn, tk), lambda i, j, k: (j, k)) if tb
              else pl.BlockSpec((tk, tn), lambda i, j, k: (k, j)))
    if gather is not None:
        c_shapes, c_sems = _gather_shapes(carried), _gather_sems(nc)
    elif exchange is not None:
        c_shapes, c_sems = _exchange_shapes(carried), _exchange_sems(nc)
    else:
        c_shapes, c_sems = [], []
    res = pl.pallas_call(
        body, name=name, grid=grid,
        in_specs=[a_spec, b_spec] + _hbm_specs(nc),
        out_specs=[pl.BlockSpec((tm, tn), lambda i, j, k: (i, j))] + _hbm_specs(nc),
        out_shape=[jax.ShapeDtypeStruct((M, N), out_dtype)] + c_shapes,
        scratch_shapes=([pltpu.VMEM((tm, tn), F32)] if nk > 1 else []) + c_sems,
        compiler_params=_cparams(("arbitrary",) * 3 if nc else ("parallel", "parallel", "arbitrary")),
    )(a, b, *carried)
    return (res[0], res[1:]) if nc else res[0]


def _sds(shape, dtype=F32):
    return jax.ShapeDtypeStruct(shape, dtype)


def _norm_mod(x, nw, scale, shift):
    return _rms(x, nw) * (1.0 + scale) + shift


def _norm_in_proj(x, nw, scale, shift, w_in_pt, shards):
    T = x.shape[0]
    N = w_in_pt.shape[0]
    tm, tn = _tile(T, 1024), _tile(N, MM_TN)
    nm, nn = T // tm, N // tn
    ns = len(shards)

    def body(*refs):
        x_ref, nw_ref, sc_ref, sh_ref, w_ref = refs[:5]
        h_ref, o_ref = refs[5 + ns:7 + ns]
        plan = _gather_plan(refs[5:5 + ns], refs[7 + ns:7 + 2 * ns], *refs[7 + 2 * ns:])
        i, j = pl.program_id(0), pl.program_id(1)

        @pl.when((i == 0) & (j == 0))
        def _():
            _start(plan)

        @pl.when(j == 0)
        def _():
            for r0 in range(0, tm, ROWS_EPI):
                rows = pl.ds(r0, ROWS_EPI)
                h_ref[rows, :] = _norm_mod(x_ref[rows, :], nw_ref[...], sc_ref[...], sh_ref[...]).astype(BF16)

        o_ref[...] = _dot(h_ref[...], w_ref[...], _NT)

        @pl.when((i == nm - 1) & (j == nn - 1))
        def _():
            _finish(plan)

    vec = pl.BlockSpec((1, D), lambda i, j: (0, 0))
    res = pl.pallas_call(
        body, name="norm1_in_proj", grid=(nm, nn),
        in_specs=[pl.BlockSpec((tm, D), lambda i, j: (i, 0)), vec, vec, vec,
                  pl.BlockSpec((tn, D), lambda i, j: (j, 0))] + _hbm_specs(ns),
        out_specs=[pl.BlockSpec((tm, D), lambda i, j: (i, 0)), pl.BlockSpec((tm, tn), lambda i, j: (i, j))]
                  + _hbm_specs(ns),
        out_shape=[_sds((T, D), BF16), _sds((T, N))] + _gather_shapes(shards),
        scratch_shapes=_gather_sems(ns),
        compiler_params=_cparams(("arbitrary", "arbitrary")),
    )(x, nw, scale, shift, w_in_pt, *shards)
    return res[0], res[1], res[2:]


ROWS_TM = 512
ROWS_EPI = 256


def _matmul_rows(a, b, epi, tiled, consts, out_tiled, out_acc, name, pieces=()):
    T, K = a.shape
    tm, tk = _tile(T, ROWS_TM), _tile(K, MM_TK)
    nm, nk = T // tm, K // tk
    npc, nt, ncst, no, na = len(pieces), len(tiled), len(consts), len(out_tiled), len(out_acc)
    n_in = 2 + nt + ncst

    def body(*refs):
        a_ref, b_ref = refs[:2]
        t_refs, c_refs = refs[2:2 + nt], refs[2 + nt:n_in]
        o_refs = refs[n_in + npc:n_in + npc + no]
        acc_refs = refs[n_in + npc + no:n_in + npc + no + na]
        n_out = no + na + npc
        res_ref = refs[n_in + npc + n_out]
        plan = _exchange_plan(refs[n_in:n_in + npc], refs[n_in + npc + no + na:n_in + npc + n_out],
                              *refs[n_in + npc + n_out + 1:]) if npc else None
        i, k = pl.program_id(0), pl.program_id(1)

        @pl.when((i == 0) & (k == 0))
        def _():
            for r in acc_refs:
                r[...] = jnp.zeros_like(r)
            if npc:
                _start(plan)

        part = _dot(a_ref[...], b_ref[...], _NN)

        @pl.when(k == 0)
        def _():
            res_ref[...] = part

        @pl.when(k > 0)
        def _():
            res_ref[...] += part

        @pl.when(k == nk - 1)
        def _():
            for r0 in range(0, tm, ROWS_EPI):
                rows = pl.ds(r0, ROWS_EPI)
                outs = epi(res_ref[rows, :], *[r[rows, :] for r in t_refs], *[r[...] for r in c_refs])
                for r, v in zip(o_refs, outs[:no]):
                    r[rows, :] = v.astype(r.dtype)
                for r, v in zip(acc_refs, outs[no:]):
                    r[...] += v

        if npc:
            @pl.when((i == nm - 1) & (k == nk - 1))
            def _():
                _finish(plan)

    row = lambda w: pl.BlockSpec((tm, w), lambda i, k: (i, 0))
    whole = lambda s: pl.BlockSpec(s.shape, lambda i, k: (0, 0))
    res = pl.pallas_call(
        body, name=name, grid=(nm, nk),
        in_specs=[pl.BlockSpec((tm, tk), lambda i, k: (i, k)), pl.BlockSpec((tk, D), lambda i, k: (k, 0))]
                 + [row(t.shape[1]) for t in tiled] + [whole(c) for c in consts] + _hbm_specs(npc),
        out_specs=[row(s.shape[1]) for s in out_tiled] + [whole(s) for s in out_acc] + _hbm_specs(npc),
        out_shape=list(out_tiled) + list(out_acc) + (_exchange_shapes(pieces) if npc else []),
        scratch_shapes=[pltpu.VMEM((tm, D), F32)] + (_exchange_sems(npc) if npc else []),
        compiler_params=_cparams(("arbitrary", "arbitrary")),
    )(a, b, *tiled, *consts, *pieces)
    return res[:no + na], res[no + na:]


def _in_proj_dx_norm_bwd(dproj, w_in_pt, x, dres, nw, scale, shift, pieces):
    T = x.shape[0]

    def epi(dh, x, dres, nw, scale, shift):
        _, vjp = jax.vjp(_norm_mod, x, nw, scale, shift)
        dx, dnw, dsc, dsh = vjp(dh)
        return dx + dres, dnw, dsc, dsh

    return _matmul_rows(dproj, w_in_pt, epi, [x, dres], [nw, scale, shift], [_sds((T, D))], [_sds((1, D))] * 3,
                        "in_proj_dx_norm1_bwd", pieces)


def _out_proj_resid_norm(mixcat, w_out, x, gate1, nw, scale, shift):
    T = x.shape[0]

    def epi(mixed, x, gate1, nw, scale, shift):
        return (mixed,) + _resid_norm(x, mixed, gate1, nw, scale, shift)

    outs, _ = _matmul_rows(mixcat, w_out, epi, [x], [gate1, nw, scale, shift],
                           [_sds((T, D)), _sds((T, D)), _sds((T, D), BF16)], [], "out_proj_resid_norm2")
    return outs


def _ffn_up_dx_resid_bwd(dab, w_gut, x, mixed, dy, gate1, nw, scale, shift):
    T = x.shape[0]

    def epi(dh2, x, mixed, dy, gate1, nw, scale, shift):
        _, vjp = jax.vjp(_resid_norm, x, mixed, gate1, nw, scale, shift)
        return vjp((dy, dh2))

    outs, _ = _matmul_rows(dab, w_gut, epi, [x, mixed, dy], [gate1, nw, scale, shift],
                           [_sds((T, D)), _sds((T, D), BF16)], [_sds((1, D))] * 4, "ffn_up_dx_resid_norm2_bwd")
    return outs


def _ffn_down_loss(act, w_down, x1, target, gate2):
    T = x1.shape[0]

    def epi(ffn, x1, target, gate2):
        y = x1 + gate2 * ffn
        err = y - target
        loss = 0.5 * jnp.sum(jnp.sum(err * err, axis=1, keepdims=True), axis=0, keepdims=True) / D
        dy = err * (1.0 / D)
        return dy, gate2 * dy, jnp.sum(dy * ffn, axis=0, keepdims=True), jnp.broadcast_to(loss, (1, LANE))

    outs, _ = _matmul_rows(act, w_down, epi, [x1, target], [gate2], [_sds((T, D)), _sds((T, D), BF16)],
                           [_sds((1, D)), _sds((1, LANE))], "ffn_down_loss")
    return outs


def _resid_norm(x, mixed, gate1, nw, scale, shift):
    x1 = x + gate1 * mixed
    return x1, _norm_mod(x1, nw, scale, shift)


FFN_BLK = 256
FFN_TM = 2048


def _interleave_gate_up(gate_t, up_t):
    blocks = lambda t: t.reshape(DFF // FFN_BLK, 1, FFN_BLK, D)
    return jnp.concatenate([blocks(gate_t), blocks(up_t)], axis=1).reshape(2 * DFF, D)


def _split_gate_up(g):
    g = g.reshape(DFF // FFN_BLK, 2, FFN_BLK, D)
    return g[:, 0].reshape(DFF, D), g[:, 1].reshape(DFF, D)


def _ffn_up_act(h2, w_gut):
    T = h2.shape[0]
    tm = _tile(T, FFN_TM)

    def body(h_ref, w_ref, ab_ref, act_ref):
        ab = _dot(h_ref[...], w_ref[...], _NT)
        ab_ref[...] = ab
        act_ref[...] = (_silu(ab[:, :FFN_BLK]) * ab[:, FFN_BLK:]).astype(act_ref.dtype)

    return pl.pallas_call(
        body, name="ffn_up_act", grid=(T // tm, DFF // FFN_BLK),
        in_specs=[pl.BlockSpec((tm, D), lambda i, j: (i, 0)), pl.BlockSpec((2 * FFN_BLK, D), lambda i, j: (j, 0))],
        out_specs=[pl.BlockSpec((tm, 2 * FFN_BLK), lambda i, j: (i, j)), pl.BlockSpec((tm, FFN_BLK), lambda i, j: (i, j))],
        out_shape=[_sds((T, 2 * DFF)), _sds((T, DFF), BF16)],
        compiler_params=_cparams(("parallel", "parallel")),
    )(h2, w_gut)


def _ffn_down_dx_act(dffn, w_down, ab):
    T = dffn.shape[0]
    tm = _tile(T, FFN_TM)

    def body(d_ref, w_ref, ab_ref, o_ref):
        dact = _dot(d_ref[...], w_ref[...], _NT)
        a, b = ab_ref[:, :FFN_BLK], ab_ref[:, FFN_BLK:]
        s = _sigmoid(a)
        da = dact * b * (s * (1.0 + a * (1.0 - s)))
        db = dact * (a * s)
        o_ref[...] = jnp.concatenate([da, db], axis=1).astype(o_ref.dtype)

    return pl.pallas_call(
        body, name="ffn_down_dx_act", grid=(T // tm, DFF // FFN_BLK),
        in_specs=[pl.BlockSpec((tm, D), lambda i, j: (i, 0)), pl.BlockSpec((FFN_BLK, D), lambda i, j: (j, 0)),
                  pl.BlockSpec((tm, 2 * FFN_BLK), lambda i, j: (i, j))],
        out_specs=pl.BlockSpec((tm, 2 * FFN_BLK), lambda i, j: (i, j)),
        out_shape=_sds((T, 2 * DFF), BF16),
        compiler_params=_cparams(("parallel", "parallel")),
    )(dffn, w_down, ab)


def _round_bf16(x):
    return x.astype(BF16).astype(F32)


def _shift_down(x, s, rows):
    if s == 0:
        return x
    return jnp.where(rows >= s, pltpu.roll(x, s, 0), 0.0)


def _shift_up(x, s, rows, T):
    if s == 0:
        return x
    return jnp.where(rows < T - s, pltpu.roll(x, T - s, 0), 0.0)


def _conv_fwd(proj, conv_w):
    T = proj.shape[0]
    ncol = 3 * GW // LANE

    def body(x_ref, w_ref, o_ref):
        x = _round_bf16(x_ref[...])
        rows = lax.broadcasted_iota(jnp.int32, x.shape, 0)
        acc = jnp.zeros_like(x)
        for j in range(CONVW):
            acc = acc + _round_bf16(w_ref[pl.ds(j, 1), :]) * _shift_down(x, CONVW - 1 - j, rows)
        o_ref[0], o_ref[1] = _split_pair(_silu(acc))

    return pl.pallas_call(
        body, name="conv_fwd", grid=(ncol,),
        in_specs=[pl.BlockSpec((T, LANE), lambda j: (0, j)), pl.BlockSpec((CONVW, LANE), lambda j: (0, j))],
        out_specs=pl.BlockSpec((2, T, HD), lambda j: (j, 0, 0)),
        out_shape=_sds((3 * GH, T, HD)),
        compiler_params=_cparams(("parallel",)),
    )(proj, conv_w)


RELAYOUT_TM = 4096


def _split_pair(y):
    return y[:, :HD], pltpu.roll(y, HD, 1)[:, :HD]


def _merge_pair(a, b):
    return jnp.concatenate([a, b], axis=1)


def _split_heads(x, col_block0, nheads, name):
    T = x.shape[0]
    tm = _tile(T, RELAYOUT_TM)

    def body(x_ref, o_ref):
        a, b = _split_pair(x_ref[...])
        o_ref[0] = a
        o_ref[1] = b

    return pl.pallas_call(
        body, name=name, grid=(nheads // 2, T // tm),
        in_specs=[pl.BlockSpec((tm, LANE), lambda j, i: (i, col_block0 + j))],
        out_specs=pl.BlockSpec((2, tm, HD), lambda j, i: (j, i, 0)),
        out_shape=_sds((nheads, T, HD), x.dtype),
        compiler_params=_cparams(("parallel", "parallel")),
    )(x)


def _merge_heads(hm, out_dtype, name, into=None, col_block0=0, head0=0, nheads=None):
    T = hm.shape[1]
    nheads = hm.shape[0] if nheads is None else nheads
    tm = _tile(T, RELAYOUT_TM)

    def body(*refs):
        h_ref, o_ref = refs[0], refs[-1]
        o_ref[...] = _merge_pair(h_ref[0], h_ref[1]).astype(o_ref.dtype)

    in_specs = [pl.BlockSpec((2, tm, HD), lambda j, i: (head0 // 2 + j, i, 0))]
    args = [hm]
    if into is None:
        out_shape = _sds((T, HD * nheads), out_dtype)
        aliases = {}
    else:
        out_shape = _sds(into.shape, into.dtype)
        in_specs.append(pl.BlockSpec(memory_space=pl.ANY))
        args.append(into)
        aliases = {1: 0}
    return pl.pallas_call(
        body, name=name, grid=(nheads // 2, T // tm),
        in_specs=in_specs,
        out_specs=pl.BlockSpec((tm, LANE), lambda j, i: (i, col_block0 + j)),
        out_shape=out_shape, input_output_aliases=aliases,
        compiler_params=_cparams(("parallel", "parallel")),
    )(*args)


def _conv_bwd(proj, conv_w, dqc):
    T = proj.shape[0]
    ncol = 3 * GW // LANE

    def body(x_ref, w_ref, d_ref, dx_ref, dw_ref):
        x = _round_bf16(x_ref[...])
        rows = lax.broadcasted_iota(jnp.int32, x.shape, 0)
        xs = [_shift_down(x, CONVW - 1 - j, rows) for j in range(CONVW)]
        w = [_round_bf16(w_ref[pl.ds(j, 1), :]) for j in range(CONVW)]
        pre = jnp.zeros_like(x)
        for j in range(CONVW):
            pre = pre + w[j] * xs[j]
        s = _sigmoid(pre)
        dpre = _round_bf16(_merge_pair(d_ref[0], d_ref[1]) * (s * (1.0 + pre * (1.0 - s))))
        dx = jnp.zeros_like(x)
        for j in range(CONVW):
            dx = dx + w[j] * _shift_up(dpre, CONVW - 1 - j, rows, T)
            dw_ref[pl.ds(j, 1), :] = jnp.sum(dpre * xs[j], axis=0, keepdims=True)
        dx_ref[...] = dx.astype(dx_ref.dtype)

    return pl.pallas_call(
        body, name="conv_bwd", grid=(ncol,),
        in_specs=[pl.BlockSpec((T, LANE), lambda j: (0, j)), pl.BlockSpec((CONVW, LANE), lambda j: (0, j)),
                  pl.BlockSpec((2, T, HD), lambda j: (j, 0, 0))],
        out_specs=[pl.BlockSpec((T, LANE), lambda j: (0, j)), pl.BlockSpec((CONVW, LANE), lambda j: (0, j))],
        out_shape=[_sds((T, NP), BF16), _sds((CONVW, 3 * GW))],
        compiler_params=_cparams(("parallel",)),
    )(proj, conv_w, dqc)


def _gdn_prep(kit, q, k, v, ga, gb, alog, dtb, t_inv=None):
    C = CHUNK
    ri = lax.broadcasted_iota(jnp.int32, (C, C), 0)
    ci = lax.broadcasted_iota(jnp.int32, (C, C), 1)
    causal = ri >= ci
    strict = ri > ci
    eye = (ri == ci).astype(F32)
    lower = causal.astype(F32)
    upper = (ri <= ci).astype(F32)

    a = ga + dtb
    softplus = jnp.maximum(a, 0.0) + jnp.log(1.0 + jnp.exp(-jnp.abs(a)))
    g_row = -jnp.exp(alog) * softplus
    beta_row = _sigmoid(gb)
    g_col = jnp.sum(eye * g_row, axis=2, keepdims=True)
    beta_col = jnp.sum(eye * beta_row, axis=2, keepdims=True)
    G_col = jnp.sum(lower * g_row, axis=2, keepdims=True)
    G_row = jnp.sum(upper * g_col, axis=1, keepdims=True)
    G_last = jnp.sum(g_row, axis=2, keepdims=True)
    decay = jnp.exp(jnp.where(causal, G_col - G_row, -1e30))

    qn = q * lax.rsqrt(jnp.sum(q * q, axis=-1, keepdims=True) + EPS) * (HD ** -0.5)
    kn = k * lax.rsqrt(jnp.sum(k * k, axis=-1, keepdims=True) + EPS)
    kb = kn * beta_col
    A = jnp.where(strict, kit.nt(kb, kn) * decay, 0.0)
    Tm = kit.inv(A, t_inv)
    eG = jnp.exp(G_col)
    u = kit.nn3(Tm, v * beta_col)
    w = kit.nn3(Tm, kb * eG)
    qk = jnp.where(causal, kit.nt(qn, kn) * decay, 0.0)
    q_dec = qn * eG
    k_dec = kn * jnp.exp(G_last - G_col)
    dec = jnp.exp(G_last)
    return u, w, qk, q_dec, k_dec, dec, Tm


def _gdn_out(o, z, nw):
    return _rms(o, nw) * _silu(z)


GDN_CB = 4


def _gdn_specs(T, blk):
    TB = GDN_CB * CHUNK
    seq = lambda grp: pl.BlockSpec((GH, TB, HD), lambda i, grp=grp: (grp, blk(i), 0))
    row = lambda grp: pl.BlockSpec((GH, GDN_CB, 1, CHUNK), lambda i, grp=grp: (grp, blk(i), 0, 0))
    per_head = pl.BlockSpec((GH, 1, CHUNK), lambda i: (0, 0, 0))
    whole = pl.BlockSpec((1, HD), lambda i: (0, 0))
    state = pl.BlockSpec((GH, GDN_CB, HD, HD), lambda i: (0, blk(i), 0, 0))
    return seq, row, per_head, whole, state


def _gdn_load(seq_refs, row_refs, head_refs):
    chunks = lambda r: jnp.concatenate([r[:, pl.ds(cb * CHUNK, CHUNK), :] for cb in range(GDN_CB)], axis=0)
    rows = lambda r: jnp.concatenate([r[:, cb] for cb in range(GDN_CB)], axis=0)
    heads = lambda r: jnp.concatenate([r[...]] * GDN_CB, axis=0)
    return [chunks(r) for r in seq_refs], [rows(r) for r in row_refs], [heads(r) for r in head_refs]


def _gdn_fwd(qkv_hm, zs_hm, gab, alog_b, dtb_b, nw, shards):
    T = qkv_hm.shape[1]
    N = T // CHUNK
    nblk = N // GDN_CB
    ns = len(shards)
    seq, row, per_head, whole, state = _gdn_specs(T, lambda i: i)
    kit = _Kit(False)

    def body(*refs):
        q_ref, k_ref, v_ref, z_ref, ga_ref, gb_ref, al_ref, dt_ref, nw_ref = refs[:9]
        o_ref, S_ref, T_ref = refs[9 + ns:12 + ns]
        S_scr = refs[12 + 2 * ns]
        plan = _gather_plan(refs[9:9 + ns], refs[12 + ns:12 + 2 * ns], *refs[13 + 2 * ns:])

        @pl.when(pl.program_id(0) == 0)
        def _():
            S_scr[...] = jnp.zeros_like(S_scr)
            _start(plan)

        (q, k, v, z), (ga, gb), (al, dt) = _gdn_load((q_ref, k_ref, v_ref, z_ref), (ga_ref, gb_ref), (al_ref, dt_ref))
        u, w, qk, q_dec, k_dec, dec, t_inv = _gdn_prep(kit, q, k, v, ga, gb, al, dt)
        S = S_scr[...]
        for cb in range(GDN_CB):
            hs = slice(cb * GH, (cb + 1) * GH)
            S_ref[:, cb] = S
            T_ref[:, cb] = t_inv[hs]
            v_new = u[hs] - kit.nn(w[hs], S)
            o = kit.nn(q_dec[hs], S) + kit.nn(qk[hs], v_new)
            S = S * dec[hs] + kit.tn(k_dec[hs], v_new)
            o_ref[:, pl.ds(cb * CHUNK, CHUNK), :] = _gdn_out(o, z[hs], nw_ref[...])
        S_scr[...] = S

        @pl.when(pl.program_id(0) == nblk - 1)
        def _():
            _finish(plan)

    res = pl.pallas_call(
        body, name="gdn_fwd", grid=(nblk,),
        in_specs=[seq(0), seq(1), seq(2), seq(0), row(0), row(1), per_head, per_head, whole] + _hbm_specs(ns),
        out_specs=[seq(0), state, state] + _hbm_specs(ns),
        out_shape=[_sds((GH + SQH, T, HD)), _sds((GH, N, HD, HD)), _sds((GH, N, CHUNK, CHUNK))]
                  + _gather_shapes(shards),
        scratch_shapes=[pltpu.VMEM((GH, HD, HD), F32)] + _gather_sems(ns),
        compiler_params=_cparams(("arbitrary",)),
    )(qkv_hm, qkv_hm, qkv_hm, zs_hm, gab, gab, alog_b, dtb_b, nw, *shards)
    return res[0], (res[1], res[2]), res[3:]


def _gdn_bwd(qkv_hm, zs_hm, gab, alog_b, dtb_b, nw, S_all, do, pieces):
    T = qkv_hm.shape[1]
    N = T // CHUNK
    nblk = N // GDN_CB
    npc = len(pieces)
    dkit, kit = _Kit(True), _Kit(False)
    rseq, rrow, per_head, whole, rstate = _gdn_specs(T, lambda i: nblk - 1 - i)

    def body(*refs):
        q_ref, k_ref, v_ref, z_ref, ga_ref, gb_ref, al_ref, dt_ref, nw_ref, S_ref, T_ref, do_ref = refs[:12]
        dqkv_ref, dz_ref, dga_ref, dgb_ref, dal_ref, ddt_ref, dnw_ref = refs[12 + npc:19 + npc]
        dS_scr = refs[19 + 2 * npc]
        plan = _exchange_plan(refs[12:12 + npc], refs[19 + npc:19 + 2 * npc], *refs[20 + 2 * npc:])

        @pl.when(pl.program_id(0) == 0)
        def _():
            dS_scr[...] = jnp.zeros_like(dS_scr)
            dal_ref[...] = jnp.zeros_like(dal_ref)
            ddt_ref[...] = jnp.zeros_like(ddt_ref)
            dnw_ref[...] = jnp.zeros_like(dnw_ref)
            _start(plan)

        (q, k, v, z, dout), (ga, gb), (al, dt) = _gdn_load((q_ref, k_ref, v_ref, z_ref, do_ref), (ga_ref, gb_ref),
                                                          (al_ref, dt_ref))
        S_in = jnp.concatenate([S_ref[:, cb] for cb in range(GDN_CB)], axis=0)
        t_inv = jnp.concatenate([T_ref[:, cb] for cb in range(GDN_CB)], axis=0)
        prep = lambda *a: _gdn_prep(dkit, *a, t_inv=t_inv)[:6]
        (u, w, qk, q_dec, k_dec, dec), prep_vjp = jax.vjp(prep, q, k, v, ga, gb, al, dt)
        v_new = u - kit.nn(w, S_in)
        o = kit.nn(q_dec, S_in) + kit.nn(qk, v_new)
        _, out_vjp = jax.vjp(_gdn_out, o, z, nw_ref[...])
        do, dz, dnw = out_vjp(dout)
        dvn_part = kit.tn(qk, do)
        dS_part = kit.tn(q_dec, do)
        dS = dS_scr[...]
        dS_out, dvn = [None] * GDN_CB, [None] * GDN_CB
        for cb in reversed(range(GDN_CB)):
            hs = slice(cb * GH, (cb + 1) * GH)
            dS_out[cb] = dS
            dvn[cb] = dvn_part[hs] + kit.nn(k_dec[hs], dS)
            dS = dS * dec[hs] + dS_part[hs] - kit.tn(w[hs], dvn[cb])
        dS_scr[...] = dS
        dS_out = jnp.concatenate(dS_out, axis=0)
        dvn = jnp.concatenate(dvn, axis=0)
        ddec = jnp.sum(jnp.sum(S_in * dS_out, axis=2, keepdims=True), axis=1, keepdims=True)
        cts = (dvn, -kit.nt(dvn, S_in), kit.nt(do, v_new), kit.nt(do, S_in), kit.nt(v_new, dS_out), ddec)
        dq, dk, dv, dga, dgb, dal, ddt = prep_vjp(cts)
        lanesum = lambda t: jnp.broadcast_to(jnp.sum(t, axis=2, keepdims=True), t.shape)
        for cb in range(GDN_CB):
            hs = slice(cb * GH, (cb + 1) * GH)
            sl = pl.ds(cb * CHUNK, CHUNK)
            dqkv_ref[pl.ds(0, GH), sl, :] = dq[hs]
            dqkv_ref[pl.ds(GH, GH), sl, :] = dk[hs]
            dqkv_ref[pl.ds(2 * GH, GH), sl, :] = dv[hs]
            dz_ref[:, sl, :] = dz[hs]
            dga_ref[:, cb] = dga[hs]
            dgb_ref[:, cb] = dgb[hs]
            dal_ref[...] += lanesum(dal[hs])
            ddt_ref[...] += lanesum(ddt[hs])
        dnw_ref[...] += dnw

        @pl.when(pl.program_id(0) == nblk - 1)
        def _():
            _finish(plan)

    res = pl.pallas_call(
        body, name="gdn_bwd", grid=(nblk,),
        in_specs=[rseq(0), rseq(1), rseq(2), rseq(0), rrow(0), rrow(1), per_head, per_head, whole, rstate, rstate,
                  rseq(0)] + _hbm_specs(npc),
        out_specs=[pl.BlockSpec((3 * GH, GDN_CB * CHUNK, HD), lambda i: (0, nblk - 1 - i, 0)), rseq(0), rrow(0),
                   rrow(0), per_head, per_head, whole] + _hbm_specs(npc),
        out_shape=[_sds((3 * GH, T, HD)), _sds((GH + 4 + SWA_GRAD_HEADS, T, HD))] + [_sds((GH, N, 1, CHUNK))] * 2
                  + [_sds((GH, 1, CHUNK))] * 2 + [_sds((1, HD))] + _exchange_shapes(pieces),
        scratch_shapes=[pltpu.VMEM((GH, HD, HD), F32)] + _exchange_sems(npc),
        compiler_params=_cparams(("arbitrary",), GDN_BWD_VMEM),
    )(qkv_hm, qkv_hm, qkv_hm, zs_hm, gab, gab, alog_b, dtb_b, nw, S_all[0], S_all[1], do, *pieces)
    return res[:7], res[7:]


def _swa_heads(kit, first, q, kp, kc, vp, vc, qnw, knw, sink, slope):
    W = WIN
    ri = lax.broadcasted_iota(jnp.int32, (W, W), 0)
    ci = lax.broadcasted_iota(jnp.int32, (W, W), 1)
    mask_c = ri >= ci
    mask_p = ci > ri + first * W
    dist_c = (ri - ci).astype(F32)
    dist_p = (ri - ci + W).astype(F32)
    kpn = _rms(kp, knw)
    kcn = _rms(kc, knw)
    qn = _rms(q, qnw)
    sc = jnp.where(mask_c, kit.nt(qn, kcn) * (HD ** -0.5) - slope * dist_c, -1e30)
    sp = jnp.where(mask_p, kit.nt(qn, kpn) * (HD ** -0.5) - slope * dist_p, -1e30)
    m = jnp.maximum(jnp.maximum(jnp.max(sc, axis=-1, keepdims=True), jnp.max(sp, axis=-1, keepdims=True)), sink)
    m = lax.stop_gradient(m)
    pc = jnp.exp(sc - m)
    pp = jnp.exp(sp - m)
    den = jnp.sum(pc, axis=-1, keepdims=True) + jnp.sum(pp, axis=-1, keepdims=True) + jnp.exp(sink - m)
    inv = 1.0 / den
    return kit.nn(pc * inv, vc) + kit.nn(pp * inv, vp)


def _swa_grads(kit, first, q, kp, kc, vp, vc, qnw, knw, sink, slope, do):
    W = WIN
    ri = lax.broadcasted_iota(jnp.int32, (W, W), 0)
    ci = lax.broadcasted_iota(jnp.int32, (W, W), 1)
    mask_c = ri >= ci
    mask_p = ci > ri + first * W
    dist_c = (ri - ci).astype(F32)
    dist_p = (ri - ci + W).astype(F32)
    scale = HD ** -0.5
    kpn, kp_vjp = jax.vjp(_rms, kp, knw)
    kcn, kc_vjp = jax.vjp(_rms, kc, knw)
    qn, q_vjp = jax.vjp(_rms, q, qnw)
    sc = jnp.where(mask_c, kit.nt(qn, kcn) * scale - slope * dist_c, -1e30)
    sp = jnp.where(mask_p, kit.nt(qn, kpn) * scale - slope * dist_p, -1e30)
    m = jnp.maximum(jnp.maximum(jnp.max(sc, axis=-1, keepdims=True), jnp.max(sp, axis=-1, keepdims=True)), sink)
    ec = jnp.exp(sc - m)
    ep = jnp.exp(sp - m)
    es = jnp.exp(sink - m)
    inv = 1.0 / (jnp.sum(ec, axis=-1, keepdims=True) + jnp.sum(ep, axis=-1, keepdims=True) + es)
    pc, pp = ec * inv, ep * inv
    dpc, dpp = kit.nt(do, vc), kit.nt(do, vp)
    delta = jnp.sum(dpc * pc, axis=-1, keepdims=True) + jnp.sum(dpp * pp, axis=-1, keepdims=True)
    dsc = pc * (dpc - delta) * scale
    dsp = pp * (dpp - delta) * scale
    dq, dqnw = q_vjp(kit.nn(dsc, kcn) + kit.nn(dsp, kpn))
    dkc, dknw_c = kc_vjp(kit.tn(dsc, qn))
    dkp, dknw_p = kp_vjp(kit.tn(dsp, qn))
    return dq, dkp, dkc, kit.tn(pp, do), kit.tn(pc, do), dqnw, dknw_c + dknw_p, -(es * inv) * delta


def _per_query_head(kv_ref):
    return jnp.concatenate([kv_ref[pl.ds(h // SGRP, 1)] for h in range(SQH)], axis=0)


def _per_kv_head(d):
    return jnp.concatenate([jnp.sum(d[g * SGRP:(g + 1) * SGRP], axis=0, keepdims=True) for g in range(SKVH)], axis=0)


def _swa_specs(blk):
    qspec = pl.BlockSpec((SQH, WIN, HD), lambda i: (1, blk(i), 0))
    cur = lambda grp: pl.BlockSpec((SKVH, WIN, HD), lambda i, grp=grp: (grp, blk(i), 0))
    prev = lambda grp: pl.BlockSpec((SKVH, WIN, HD), lambda i, grp=grp: (grp, jnp.maximum(blk(i) - 1, 0), 0))
    whole = pl.BlockSpec((1, HD), lambda i: (0, 0))
    col = pl.BlockSpec((SQH, WIN, 1), lambda i: (0, 0, 0))
    ospec = pl.BlockSpec((SQH, WIN, HD), lambda i: (0, blk(i), 0))
    return qspec, cur, prev, whole, col, ospec


def _swa_fwd(zs_hm, qnw, knw, sinks_col, slopes_col, o_buf, shards):
    T = zs_hm.shape[1]
    NB = T // WIN
    ns = len(shards)
    kit = _Kit(False)
    qspec, cur, prev, whole, col, _ = _swa_specs(lambda i: i)

    def body(*refs):
        q_ref, kp_ref, kc_ref, vp_ref, vc_ref, qnw_ref, knw_ref, s_ref, sl_ref = refs[:9]
        o_ref = refs[10 + ns]
        plan = _gather_plan(refs[10:10 + ns], refs[11 + ns:11 + 2 * ns], *refs[11 + 2 * ns:])

        @pl.when(pl.program_id(0) == 0)
        def _():
            _start(plan)

        first = (pl.program_id(0) == 0).astype(jnp.int32)
        o_ref[...] = _swa_heads(kit, first, q_ref[...], _per_query_head(kp_ref), _per_query_head(kc_ref),
                                _per_query_head(vp_ref), _per_query_head(vc_ref), qnw_ref[...], knw_ref[...],
                                s_ref[...], sl_ref[...])

        @pl.when(pl.program_id(0) == NB - 1)
        def _():
            _finish(plan)

    res = pl.pallas_call(
        body, name="swa_fwd", grid=(NB,),
        in_specs=[qspec, prev(8), cur(8), prev(9), cur(9), whole, whole, col, col] + _hbm_specs(1 + ns),
        out_specs=[pl.BlockSpec((SQH, WIN, HD), lambda i: (1, i, 0))] + _hbm_specs(ns),
        out_shape=[_sds(o_buf.shape)] + _gather_shapes(shards),
        input_output_aliases={9: 0},
        scratch_shapes=_gather_sems(ns),
        compiler_params=_cparams(("arbitrary",)),
    )(zs_hm, zs_hm, zs_hm, zs_hm, zs_hm, qnw, knw, sinks_col, slopes_col, o_buf, *shards)
    return res[0], res[1:]


SWA_GRAD_HEADS = SQH + 2 * SKVH


def _swa_bwd(zs_hm, qnw, knw, sinks_col, slopes_col, dmix_hm, d_buf):
    T = zs_hm.shape[1]
    NB = T // WIN
    kit = _Kit(False)
    qspec, cur, prev, whole, col, _ = _swa_specs(lambda i: NB - 1 - i)

    def body(q_ref, kp_ref, kc_ref, vp_ref, vc_ref, qnw_ref, knw_ref, s_ref, sl_ref, do_ref, buf_ref,
             d_ref, dqnw_ref, dknw_ref, ds_ref, ck_scr, cv_scr):
        dq_ref = d_ref.at[pl.ds(0, SQH)]
        dk_ref = d_ref.at[pl.ds(SQH, SKVH)]
        dv_ref = d_ref.at[pl.ds(SQH + SKVH, SKVH)]
        i = pl.program_id(0)
        first = (i == NB - 1).astype(jnp.int32)

        @pl.when(i == 0)
        def _():
            ck_scr[...] = jnp.zeros_like(ck_scr)
            cv_scr[...] = jnp.zeros_like(cv_scr)
            ds_ref[...] = jnp.zeros_like(ds_ref)
            dqnw_ref[...] = jnp.zeros_like(dqnw_ref)
            dknw_ref[...] = jnp.zeros_like(dknw_ref)

        dq, dkp, dkc, dvp, dvc, dqnw, dknw, dsink = _swa_grads(
            kit, first, q_ref[...], _per_query_head(kp_ref), _per_query_head(kc_ref), _per_query_head(vp_ref),
            _per_query_head(vc_ref), qnw_ref[...], knw_ref[...], s_ref[...], sl_ref[...], do_ref[...])
        dq_ref[...] = dq
        dk_ref[...] = _per_kv_head(dkc) + ck_scr[...]
        dv_ref[...] = _per_kv_head(dvc) + cv_scr[...]
        ck_scr[...] = _per_kv_head(dkp)
        cv_scr[...] = _per_kv_head(dvp)
        dqnw_ref[...] += dqnw
        dknw_ref[...] += dknw
        ds_ref[...] += jnp.broadcast_to(jnp.sum(dsink, axis=1, keepdims=True), dsink.shape)

    dospec = pl.BlockSpec((SQH, WIN, HD), lambda i: (1, NB - 1 - i, 0))
    dspec = pl.BlockSpec((SWA_GRAD_HEADS, WIN, HD), lambda i: (1, NB - 1 - i, 0))
    res = pl.pallas_call(
        body, name="swa_bwd", grid=(NB,),
        in_specs=[qspec, prev(8), cur(8), prev(9), cur(9), whole, whole, col, col, dospec] + _hbm_specs(1),
        out_specs=[dspec, whole, whole, col],
        out_shape=[_sds(d_buf.shape), _sds((1, HD)), _sds((1, HD)), _sds((SQH, WIN, 1))],
        input_output_aliases={10: 0},
        scratch_shapes=[pltpu.VMEM((SKVH, WIN, HD), F32), pltpu.VMEM((SKVH, WIN, HD), F32)],
        compiler_params=_cparams(("arbitrary",)),
    )(zs_hm, zs_hm, zs_hm, zs_hm, zs_hm, qnw, knw, sinks_col, slopes_col, dmix_hm, d_buf)
    return res


GAB0 = 3 * GW + 1280


W_IN_ROWS = PROJ // N_CHIP
W_IN_ROWS_PAD = 736


def _permute_w_in_t(w_in_t):
    return jnp.concatenate([w_in_t[:4 * GW], w_in_t[4 * GW + 2 * GH:], w_in_t[4 * GW:4 * GW + 2 * GH],
                            jnp.zeros((NP - PROJ, D), w_in_t.dtype)], axis=0)


def _w_in_grad_pieces(g_t):
    g = jnp.concatenate([g_t[:4 * GW], g_t[GAB0:GAB0 + 2 * GH], g_t[4 * GW:GAB0]], axis=0)
    g = jnp.pad(g.reshape(N_CHIP, W_IN_ROWS, D), ((0, 0), (0, W_IN_ROWS_PAD - W_IN_ROWS), (0, 0)))
    return g.reshape(N_CHIP, 2, W_IN_ROWS_PAD // 2, D)


def _pieces_by_rows(g):
    return g.reshape(N_CHIP, 2, g.shape[0] // (2 * N_CHIP), D)


def _local_step(x, target, mod, n1w, w_in_pt, conv_w, alog, dtb, gnw, qnw, knw, sinks, n2w, shards):
    sh_out, sh_gate, sh_up, sh_down = shards
    T = x.shape[0]
    N = T // CHUNK
    shift1, scale1, gate1, shift2, scale2, gate2 = [mod[:, i * D:(i + 1) * D] for i in range(6)]

    h, proj, (a_out,) = _norm_in_proj(x, n1w, scale1, shift1, w_in_pt, [sh_out])
    w_out = a_out.reshape(D, D)
    qkv_hm = _conv_fwd(proj, conv_w)
    zs_hm = _split_heads(proj, 3 * GW // LANE, 20, "split_zs")
    gab = proj[:, GAB0:GAB0 + 2 * GH].T.reshape(2 * GH, N, 1, CHUNK)
    alog_b = jnp.broadcast_to(alog.reshape(GH, 1, 1), (GH, 1, CHUNK))
    dtb_b = jnp.broadcast_to(dtb.reshape(GH, 1, 1), (GH, 1, CHUNK))
    sinks_col = jnp.broadcast_to(sinks.reshape(SQH, 1, 1), (SQH, WIN, 1))
    o_hm, S_all, (a_gate, a_up) = _gdn_fwd(qkv_hm, zs_hm, gab, alog_b, dtb_b, gnw, [sh_gate, sh_up])
    w_gut = _interleave_gate_up(a_gate.reshape(DFF, D), a_up.reshape(DFF, D))
    slopes = 2.0 ** (-8.0 * (jnp.arange(SQH, dtype=F32) + 1.0) / SQH)
    slopes_col = jnp.broadcast_to(slopes.reshape(SQH, 1, 1), (SQH, WIN, 1))
    o_hm, (a_down,) = _swa_fwd(zs_hm, qnw, knw, sinks_col, slopes_col, o_hm, [sh_down])
    w_down = a_down.reshape(DFF, D)
    mixcat = _merge_heads(o_hm, BF16, "merge_mix")
    mixed, x1, h2 = _out_proj_resid_norm(mixcat, w_out, x, gate1, n2w, scale2, shift2)
    ab, act = _ffn_up_act(h2, w_gut)
    dy, dffn, dgate2, loss = _ffn_down_loss(act, w_down, x1, target, gate2)

    dab = _ffn_down_dx_act(dffn, w_down, ab)
    g_w_down = _matmul(act, dffn, ta=True, out_dtype=BF16, name="ffn_down_dw")
    g_w_gut = _matmul(dab, h2, ta=True, out_dtype=BF16, name="ffn_up_dw")
    dx1, dmixed, dgate1, dn2w, dscale2, dshift2 = _ffn_up_dx_resid_bwd(dab, w_gut, x, mixed, dy, gate1, n2w, scale2,
                                                                       shift2)
    g_w_out = _matmul(mixcat, dmixed, ta=True, out_dtype=BF16, name="out_proj_dw")
    dmix_hm = _split_heads(_matmul(dmixed, w_out, tb=True, name="out_proj_dx"), 0, GH + SQH, "split_dmix")
    g_gate_t, g_up_t = _split_gate_up(g_w_gut)
    pieces = [_pieces_by_rows(g_w_out), _pieces_by_rows(g_gate_t), _pieces_by_rows(g_up_t),
              _pieces_by_rows(g_w_down)]
    (dqkv_hm, d_hm, dga, dgb, dalog, ddtb, dgnw), recv = _gdn_bwd(qkv_hm, zs_hm, gab, alog_b, dtb_b, gnw, S_all,
                                                                  dmix_hm, pieces)
    d_hm, dqnw, dknw, dsinks = _swa_bwd(zs_hm, qnw, knw, sinks_col, slopes_col, dmix_hm, d_hm)
    dproj, dconv = _conv_bwd(proj, conv_w, dqkv_hm)
    dproj = _merge_heads(d_hm, BF16, "merge_dz", into=dproj, col_block0=3 * GW // LANE, head0=0, nheads=GH)
    dproj = _merge_heads(d_hm, BF16, "merge_dswa", into=dproj, col_block0=4 * GW // LANE, head0=GH + 4,
                         nheads=SWA_GRAD_HEADS)
    dgab = jnp.concatenate([dga, dgb], axis=0).reshape(2 * GH, T).T.astype(BF16)
    dproj = lax.dynamic_update_slice(dproj, jnp.concatenate([dgab, jnp.zeros((T, NP - PROJ), BF16)], axis=1),
                                     (0, GAB0))
    g_w_in_pt = _matmul(dproj, h, ta=True, out_dtype=BF16, name="in_proj_dw")
    (grad_x, dn1w, dscale1, dshift1), recv_in = _in_proj_dx_norm_bwd(dproj, w_in_pt, x, dx1, n1w, scale1, shift1,
                                                                     [_w_in_grad_pieces(g_w_in_pt)])

    dmod = jnp.concatenate([dshift1, dscale1, dgate1, dshift2, dscale2, dgate2], axis=1)
    big = list(recv_in) + list(recv)
    small = dict(mod=dmod, norm1_w=dn1w, norm2_w=dn2w, conv_w=dconv, a_log=dalog[:, 0, 0], dt_bias=ddtb[:, 0, 0],
                 gdn_norm_w=dgnw, q_norm_w=dqnw, k_norm_w=dknw, sinks=dsinks[:, 0, 0])
    return loss, grad_x, big, small


def _adamw(w, g, m, v):
    m2 = ADAM_B1 * m + (1.0 - ADAM_B1) * g
    v2 = ADAM_B2 * v + (1.0 - ADAM_B2) * (g * g)
    m_hat = m2 / (1.0 - ADAM_B1 ** ADAM_STEP)
    v_hat = v2 / (1.0 - ADAM_B2 ** ADAM_STEP)
    delta = -ADAM_LR * (m_hat / (jnp.sqrt(v_hat) + ADAM_EPS) + ADAM_WD * w)
    return delta, m2, v2


def _reduce_adamw(recv, w, m, v, name):
    _, R, C = recv.shape
    tc = _tile(C, 256)

    def body(r_ref, w_ref, m_ref, v_ref, o_ref):
        g = r_ref[0].astype(F32)
        for s in range(1, N_DEV):
            g = g + r_ref[s].astype(F32)
        delta, m2, v2 = _adamw(w_ref[...], g, m_ref[...], v_ref[...])
        o_ref[0] = g
        o_ref[1] = delta
        o_ref[2] = m2
        o_ref[3] = v2

    col = pl.BlockSpec((R, tc), lambda j: (0, j))
    return pl.pallas_call(
        body, name=name, grid=(C // tc,),
        in_specs=[pl.BlockSpec((N_DEV, R, tc), lambda j: (0, 0, j)), col, col, col],
        out_specs=pl.BlockSpec((4, R, tc), lambda j: (0, 0, j)),
        out_shape=_sds((4, R, C)),
        compiler_params=_cparams(("parallel",)),
    )(recv, w, m, v)


def _adamw_call(g, w, m, v, name):
    def body(g_ref, w_ref, m_ref, v_ref, o_ref):
        delta, m2, v2 = _adamw(w_ref[...], g_ref[...], m_ref[...], v_ref[...])
        o_ref[0] = delta
        o_ref[1] = m2
        o_ref[2] = v2

    return pl.pallas_call(body, name=name, out_shape=_sds((3,) + g.shape))(g, w, m, v)


ADA_N = 6 * D // N_CHIP
KPAD = 128


def _w_ada_update(c8p, dm, w, m, v):
    tr = 256

    def body(c_ref, dm_ref, w_ref, m_ref, v_ref, g_ref, d_ref, m2_ref, v2_ref):
        g = _raw1(_silu(c_ref[...]), dm_ref[...], _TN)
        delta, m2, v2 = _adamw(w_ref[...], g, m_ref[...], v_ref[...])
        g_ref[...] = g
        d_ref[...] = delta
        m2_ref[...] = m2
        v2_ref[...] = v2

    blk = pl.BlockSpec((tr, ADA_N), lambda i: (i, 0))
    return pl.pallas_call(
        body, name="w_ada_update", grid=(D // tr,),
        in_specs=[pl.BlockSpec((KPAD, tr), lambda i: (0, i)), pl.BlockSpec((KPAD, ADA_N), lambda i: (0, 0)),
                  blk, blk, blk],
        out_specs=[blk] * 4, out_shape=[_sds((D, ADA_N))] * 4,
        compiler_params=_cparams(("parallel",)),
    )(c8p, dm, w, m, v)


def _me():
    return lax.axis_index("x"), lax.axis_index("y"), lax.axis_index("c")


def _peer(k, me):
    mx, my, mc = me
    return (1 - mx if k & 4 else mx, 1 - my if k & 2 else my, 1 - mc if k & 1 else mc)


def _lin(p):
    return 4 * p[0] + 2 * p[1] + p[2]


def _remote(src, dst, ssem, rsem, dev):
    return pltpu.make_async_remote_copy(src_ref=src, dst_ref=dst, send_sem=ssem, recv_sem=rsem,
                                        device_id=dev, device_id_type=MESH)


def _all_gather8(x, name):
    def body(x_ref, out_ref, send_sems, recv_sems):
        me = _me()
        out_ref[_lin(me)] = x_ref[...]
        sends = []
        for k in range(1, N_DEV):
            cp = _remote(x_ref, out_ref.at[_lin(me)], send_sems.at[k - 1], recv_sems.at[k - 1], _peer(k, me))
            cp.start()
            sends.append(cp)
        for k in range(1, N_DEV):
            p = _peer(k, me)
            _remote(x_ref, out_ref.at[_lin(p)], send_sems.at[k - 1], recv_sems.at[k - 1], p).wait_recv()
        for cp in sends:
            cp.wait_send()

    return pl.pallas_call(
        body, name=name,
        out_shape=_sds((N_DEV,) + x.shape, x.dtype),
        in_specs=[pl.BlockSpec(memory_space=pltpu.VMEM)],
        out_specs=pl.BlockSpec(memory_space=pltpu.VMEM),
        scratch_shapes=[pltpu.SemaphoreType.DMA((N_DEV - 1,)), pltpu.SemaphoreType.DMA((N_DEV - 1,))],
    )(x)


def _ag8_plan(src, out, send_sems, recv_sems):
    me = _me()
    sends, recvs = [], []
    for k in range(1, N_DEV):
        p = _peer(k, me)
        sends.append(_remote(src, out.at[_lin(me)], send_sems.at[k - 1], recv_sems.at[k - 1], p))
        recvs.append(_remote(src, out.at[_lin(p)], send_sems.at[k - 1], recv_sems.at[k - 1], p))
    return [], sends, recvs


def _prologue(c_row, conv_sh, w_ada, b_sh, w_in_sh):
    def body(c_ref, cv_ref, wa_ref, b_ref, win_ref, call_ref, cvall_ref, mods_ref, ain_ref, c16_scr, mp_scr,
             c_send, c_recv, cv_send, cv_recv, m_send, m_recv, w_send, w_recv, w_local):
        me = _lin(_me())
        w_plan = _gather_plan([win_ref], [ain_ref], w_send, w_recv, w_local)
        _start(w_plan)
        c_plan = _ag8_plan(c_ref, call_ref, c_send, c_recv)
        cv_plan = _ag8_plan(cv_ref, cvall_ref, cv_send, cv_recv)
        call_ref[me] = c_ref[...]
        cvall_ref[me] = cv_ref[...]
        _start(c_plan)
        _start(cv_plan)
        _finish(c_plan)
        c16_scr[...] = jnp.zeros_like(c16_scr)
        for d in range(N_DEV):
            c16_scr[pl.ds(d, 1), :] = call_ref[d]
        mp_scr[...] = _raw1(_silu(c16_scr[...]), wa_ref[...], _NN) + b_ref[...]
        mods_ref[me] = mp_scr[...]
        m_plan = _ag8_plan(mp_scr, mods_ref, m_send, m_recv)
        _start(m_plan)
        _finish(cv_plan)
        _finish(m_plan)
        _finish(w_plan)

    vmem = pl.BlockSpec(memory_space=pltpu.VMEM)
    sems = lambda n: pltpu.SemaphoreType.DMA((n,))
    return pl.pallas_call(
        body, name="prologue",
        in_specs=[vmem] * 4 + _hbm_specs(1), out_specs=[vmem] * 3 + _hbm_specs(1),
        out_shape=[_sds((N_DEV,) + c_row.shape), _sds((N_DEV,) + conv_sh.shape), _sds((N_DEV, 16, ADA_N)),
                   _sds((N_CHIP,) + w_in_sh.shape, w_in_sh.dtype)],
        scratch_shapes=[pltpu.VMEM((16, D), F32), pltpu.VMEM((16, ADA_N), F32)] + [sems(N_DEV - 1)] * 6
                       + _gather_sems(1),
        compiler_params=_cparams(),
    )(c_row, conv_sh, w_ada, b_sh, w_in_sh)


def _hbm_specs(n):
    return [pl.BlockSpec(memory_space=pl.ANY)] * n


def _gather_shapes(shards):
    return [_sds((N_CHIP,) + s.shape, s.dtype) for s in shards]


def _gather_sems(n):
    return [pltpu.SemaphoreType.DMA((3 * n,)), pltpu.SemaphoreType.DMA((3 * n,)), pltpu.SemaphoreType.DMA((n,))]


def _gather_plan(ins, outs, send_sems, recv_sems, local_sems):
    mx, my, mc = _me()
    chips = [(1 - mx, my), (mx, 1 - my), (1 - mx, 1 - my)]
    local, sends, recvs = [], [], []
    for a in range(len(ins)):
        local.append(pltpu.make_async_copy(ins[a], outs[a].at[2 * mx + my], local_sems.at[a]))
        for k, (px, py) in enumerate(chips):
            sems = (send_sems.at[3 * a + k], recv_sems.at[3 * a + k], (px, py, mc))
            sends.append(_remote(ins[a], outs[a].at[2 * mx + my], *sems))
            recvs.append(_remote(ins[a], outs[a].at[2 * px + py], *sems))
    return local, sends, recvs


def _start(plan):
    local, sends, _ = plan
    for cp in local + sends:
        cp.start()


def _finish(plan):
    local, sends, recvs = plan
    for cp in recvs:
        cp.wait_recv()
    for cp in sends:
        cp.wait_send()
    for cp in local:
        cp.wait()


def _exchange_shapes(pieces):
    return [_sds((N_DEV,) + p.shape[2:], p.dtype) for p in pieces]


def _exchange_sems(n):
    return [pltpu.SemaphoreType.DMA(((N_DEV - 1) * n,)), pltpu.SemaphoreType.DMA(((N_DEV - 1) * n,)),
            pltpu.SemaphoreType.DMA((n,))]


def _exchange_plan(ins, outs, send_sems, recv_sems, local_sems):
    me = _me()
    mx, my, mc = me
    local, sends, recvs = [], [], []
    for a in range(len(ins)):
        local.append(pltpu.make_async_copy(ins[a].at[2 * mx + my, mc], outs[a].at[_lin(me)], local_sems.at[a]))
        for k in range(1, N_DEV):
            p = _peer(k, me)
            s = (N_DEV - 1) * a + k - 1
            sends.append(_remote(ins[a].at[2 * p[0] + p[1], p[2]], outs[a].at[_lin(me)], send_sems.at[s],
                                 recv_sems.at[s], p))
            recvs.append(_remote(ins[a].at[2 * mx + my, mc], outs[a].at[_lin(p)], send_sems.at[s],
                                 recv_sems.at[s], p))
    return local, sends, recvs


REDUCE_VMEM = 56 * 1024 * 1024


def _reduce_swap(recvs):
    n = len(recvs)

    def body(*refs):
        r_refs, o_refs = refs[:n], refs[n:2 * n]
        send_sems, recv_sems = refs[2 * n:]
        mx, my, mc = _me()
        sib = (mx, my, 1 - mc)
        half = lambda a, c: o_refs[a].at[pl.ds(pl.multiple_of(c * recvs[a].shape[1], 8), recvs[a].shape[1])]
        sends = []
        for a in range(n):
            g = r_refs[a][0].astype(F32)
            for s in range(1, N_DEV):
                g = g + r_refs[a][s].astype(F32)
            half(a, mc)[...] = g
            cp = _remote(half(a, mc), half(a, mc), send_sems.at[a], recv_sems.at[a], sib)
            cp.start()
            sends.append(cp)
        for a in range(n):
            _remote(half(a, mc), half(a, 1 - mc), send_sems.at[a], recv_sems.at[a], sib).wait_recv()
        for cp in sends:
            cp.wait_send()

    vmem = pl.BlockSpec(memory_space=pltpu.VMEM)
    return pl.pallas_call(
        body, name="reduce_swap", out_shape=[_sds((2 * r.shape[1], r.shape[2])) for r in recvs],
        in_specs=[vmem] * n, out_specs=[vmem] * n,
        scratch_shapes=[pltpu.SemaphoreType.DMA((n,)), pltpu.SemaphoreType.DMA((n,))],
        compiler_params=_cparams(None, REDUCE_VMEM),
    )(*recvs)


def _adamw_big(g, w, m, v, name):
    rows, cols = g.shape
    tr = next((t for t in (256, 176, 128, 64, 8) if rows % t == 0), None)
    if tr is None:
        tc = _tile(cols, 256)
        blk, grid = pl.BlockSpec((rows, tc), lambda i: (0, i)), (cols // tc,)
    else:
        blk, grid = pl.BlockSpec((tr, cols), lambda i: (i, 0)), (rows // tr,)

    def body(g_ref, w_ref, m_ref, v_ref, go_ref, d_ref, m2_ref, v2_ref):
        g = g_ref[...]
        delta, m2, v2 = _adamw(w_ref[...], g, m_ref[...], v_ref[...])
        go_ref[...] = g
        d_ref[...] = delta
        m2_ref[...] = m2
        v2_ref[...] = v2

    return pl.pallas_call(
        body, name=name, grid=grid,
        in_specs=[blk] * 4, out_specs=[blk] * 4, out_shape=[_sds((rows, cols))] * 4,
        compiler_params=_cparams(("parallel",)),
    )(g, w, m, v)


SMALL_ORDER = (("mod", 6 * D), ("norm1_w", D), ("norm2_w", D), ("conv_w", CONVW * 3 * GW), ("a_log", GH),
               ("dt_bias", GH), ("gdn_norm_w", HD), ("q_norm_w", HD), ("k_norm_w", HD), ("sinks", SQH), ("loss", 1))
SMALL_R = 120


def _pack_small(d):
    parts = [d[k].reshape(-1).astype(F32) if k in d else jnp.zeros((n,), F32) for k, n in SMALL_ORDER]
    used = sum(n for _, n in SMALL_ORDER)
    parts.append(jnp.zeros((SMALL_R * LANE - used,), F32))
    return jnp.concatenate(parts).reshape(SMALL_R, LANE)


def _unpack_small(pk):
    flat = pk.reshape(-1)
    out, r = {}, 0
    for k, n in SMALL_ORDER:
        out[k] = flat[r:r + n]
        r += n
    return out


def kernel(x, c, w_ada, b_ada, norm1_w, w_in, conv_w, a_log, dt_bias, gdn_norm_w, q_norm_w, k_norm_w, sinks, w_out, norm2_w, w_gate, w_up, w_down, loss_target, m_w_ada, m_b_ada, m_norm1_w, m_w_in, m_conv_w, m_a_log, m_dt_bias, m_gdn_norm_w, m_q_norm_w, m_k_norm_w, m_sinks, m_w_out, m_norm2_w, m_w_gate, m_w_up, m_w_down, v_w_ada, v_b_ada, v_norm1_w, v_w_in, v_conv_w, v_a_log, v_dt_bias, v_gdn_norm_w, v_q_norm_w, v_k_norm_w, v_sinks, v_w_out, v_norm2_w, v_w_gate, v_w_up, v_w_down):
    mx, my, mc = _me()
    chip = 2 * mx + my
    dev = 4 * mx + 2 * my + mc
    T = x.shape[1]

    as_rows = lambda t, transposed: t[0].T if transposed else t[0]
    transposed = (True, False, True, True, False)
    big_w = [as_rows(t, tr) for t, tr in zip((w_in, w_out, w_gate, w_up, w_down), transposed)]
    shards = [t.astype(BF16) for t in big_w]

    b_sh = lax.dynamic_slice(b_ada, (0, chip * ADA_N), (1, ADA_N))
    c_all, conv_all, mods, a_in = _prologue(c, conv_w.reshape(CONVW, 3 * GW // N_CHIP), w_ada[0], b_sh, shards[0])
    c8 = c_all.reshape(N_DEV, D)
    conv_full = jnp.concatenate([conv_all[2 * j] for j in range(N_CHIP)], axis=1)
    mod = jnp.concatenate([lax.dynamic_slice(mods[2 * j], (dev, 0), (1, ADA_N)) for j in range(N_CHIP)], axis=1)
    w_in_pt = _permute_w_in_t(a_in.reshape(PROJ, D))

    loss, grad_x, big, small = _local_step(
        x[0], loss_target[0], mod, norm1_w, w_in_pt, conv_full, a_log, dt_bias, gdn_norm_w,
        q_norm_w, k_norm_w, sinks, norm2_w, shards[1:])

    small["loss"] = loss[:, :1]
    sg = _all_gather8(_pack_small(small), "gather_small_grads")
    rep = dict(mod=(b_ada, m_b_ada, v_b_ada), norm1_w=(norm1_w, m_norm1_w, v_norm1_w),
               norm2_w=(norm2_w, m_norm2_w, v_norm2_w), a_log=(a_log, m_a_log, v_a_log),
               dt_bias=(dt_bias, m_dt_bias, v_dt_bias), gdn_norm_w=(gdn_norm_w, m_gdn_norm_w, v_gdn_norm_w),
               q_norm_w=(q_norm_w, m_q_norm_w, v_q_norm_w), k_norm_w=(k_norm_w, m_k_norm_w, v_k_norm_w),
               sinks=(sinks, m_sinks, v_sinks))
    wmv = [_pack_small({k: t[i] for k, t in rep.items()}) for i in range(3)]
    sres = _reduce_adamw(sg, wmv[0], wmv[1], wmv[2], "small_reduce_adamw")
    s_g, s_d, s_m, s_v = [_unpack_small(sres[i]) for i in range(4)]
    loss_out = s_g["loss"][0]

    g_conv = lax.dynamic_slice(s_g["conv_w"].reshape(CONVW, 3 * GW), (0, chip * (3 * GW // N_CHIP)),
                               (CONVW, 3 * GW // N_CHIP))
    pad16 = lambda t: jnp.concatenate([t.reshape(12, LANE), jnp.zeros((4, LANE), F32)], axis=0)
    cres = _adamw_call(pad16(g_conv), pad16(conv_w), pad16(m_conv_w), pad16(v_conv_w), "conv_adamw")
    conv_out = [g_conv.reshape(conv_w.shape)] + [cres[i, :12].reshape(conv_w.shape) for i in range(3)]

    dmod8 = sg[:, :6 * D // LANE].reshape(N_DEV, 6 * D)
    dm = lax.dynamic_slice(dmod8, (0, chip * ADA_N), (N_DEV, ADA_N))
    zpad = lambda t: jnp.concatenate([t, jnp.zeros((KPAD - N_DEV, t.shape[1]), F32)], axis=0)
    ares = _w_ada_update(zpad(c8), zpad(dm), w_ada[0], m_w_ada[0], v_w_ada[0])

    names = ("w_in", "w_out", "w_gate", "w_up", "w_down")
    g_full = list(_reduce_swap(big))
    g_full[0] = g_full[0][:W_IN_ROWS]
    big_m = [as_rows(t, tr) for t, tr in zip((m_w_in, m_w_out, m_w_gate, m_w_up, m_w_down), transposed)]
    big_v = [as_rows(t, tr) for t, tr in zip((v_w_in, v_w_out, v_w_gate, v_w_up, v_w_down), transposed)]
    upd = [_adamw_big(g, w, m, v, "adamw_" + nm) for g, w, m, v, nm in zip(g_full, big_w, big_m, big_v, names)]
    back = lambda t, tr: (t.T if tr else t)[None]
    bg, bd, bm, bv = [[back(u[i], tr) for u, tr in zip(upd, transposed)] for i in range(4)]

    def group(a_i, small_d, conv_i, big_l):
        s = lambda k, ref: small_d[k].reshape(ref.shape)
        return [ares[a_i][None], s("mod", b_ada), s("norm1_w", norm1_w), big_l[0], conv_out[conv_i],
                s("a_log", a_log), s("dt_bias", dt_bias), s("gdn_norm_w", gdn_norm_w), s("q_norm_w", q_norm_w),
                s("k_norm_w", k_norm_w), s("sinks", sinks), big_l[1], s("norm2_w", norm2_w), big_l[2], big_l[3],
                big_l[4]]

    outs = [loss_out, grad_x[None]]
    outs += group(0, s_g, 0, bg) + group(1, s_d, 1, bd) + group(2, s_m, 2, bm) + group(3, s_v, 3, bv)
    return tuple(outs)
```

```python
import jax
import jax.numpy as jnp
from jax import lax
from jax.experimental import pallas as pl
from jax.experimental.pallas import tpu as pltpu

F32 = jnp.float32
BF16 = jnp.bfloat16
MESH = pl.DeviceIdType.MESH

D = 1024
HD = 64
GH = 8
GW = GH * HD
SQH = 8
SKVH = 2
SGRP = SQH // SKVH
WIN = 128
CONVW = 4
CHUNK = 64
DFF = 2816
PROJ = 2832
NP = 3072
EPS = 1e-6
N_DEV = 8
N_CHIP = 4

ADAM_LR = 0.001
ADAM_B1 = 0.9
ADAM_B2 = 0.999
ADAM_EPS = 1e-08
ADAM_WD = 0.01
ADAM_STEP = 10

VMEM_LIMIT = 48 * 1024 * 1024
GDN_BWD_VMEM = 58 * 1024 * 1024
LANE = 128


def _cparams(sem=None, vmem=VMEM_LIMIT):
    return pltpu.CompilerParams(dimension_semantics=sem, vmem_limit_bytes=vmem)


_NN = ((1,), (0,))
_NT = ((1,), (1,))
_TN = ((0,), (0,))


def _dot(a, b, dims):
    if a.ndim == 3:
        (ca,), (cb,) = dims
        return lax.dot_general(a, b, (((ca + 1,), (cb + 1,)), ((0,), (0,))), preferred_element_type=F32)
    return lax.dot_general(a, b, (dims, ((), ())), preferred_element_type=F32)


def _raw1(a, b, dims):
    return _dot(a.astype(BF16), b.astype(BF16), dims)


def _raw3(a, b, dims):
    ah = a.astype(BF16)
    al = (a - ah.astype(F32)).astype(BF16)
    bh = b.astype(BF16)
    bl = (b - bh.astype(F32)).astype(BF16)
    return _dot(ah, bh, dims) + (_dot(al, bh, dims) + _dot(ah, bl, dims))


def _make_diff_mm(raw):
    @jax.custom_vjp
    def nn(a, b):
        return raw(a, b, _NN)

    @jax.custom_vjp
    def nt(a, b):
        return raw(a, b, _NT)

    @jax.custom_vjp
    def tn(a, b):
        return raw(a, b, _TN)

    nn.defvjp(lambda a, b: (raw(a, b, _NN), (a, b)), lambda r, g: (nt(g, r[1]), tn(r[0], g)))
    nt.defvjp(lambda a, b: (raw(a, b, _NT), (a, b)), lambda r, g: (nn(g, r[1]), tn(g, r[0])))
    tn.defvjp(lambda a, b: (raw(a, b, _TN), (a, b)), lambda r, g: (nt(r[1], g), nn(r[0], g)))
    return nn, nt, tn


def _tri_inv_raw(a, nn3):
    n = a.shape[-1]
    ri = lax.broadcasted_iota(jnp.int32, (n, n), 0)
    ci = lax.broadcasted_iota(jnp.int32, (n, n), 1)
    t = (ri == ci).astype(F32)
    for lvl in range((n - 1).bit_length()):
        same_pair = (ri >> (lvl + 1)) == (ci >> (lvl + 1))
        lower_left = (((ri >> lvl) & 1) == 1) & (((ci >> lvl) & 1) == 0)
        y = jnp.where(same_pair & lower_left, a, 0.0)
        t = t - y if lvl == 0 else t - nn3(nn3(t, y), t)
    return t


class _Kit:
    def __init__(self, diff):
        if diff:
            self.nn, self.nt, self.tn = _make_diff_mm(_raw1)
            self.nn3, self.nt3, self.tn3 = _make_diff_mm(_raw3)
            nn3, nt3, tn3 = self.nn3, self.nt3, self.tn3

            @jax.custom_vjp
            def inv(a, t):
                return t

            def inv_fwd(a, t):
                return t, t

            def inv_bwd(t, g):
                return -tn3(t, nt3(g, t)), jnp.zeros_like(t)

            inv.defvjp(inv_fwd, inv_bwd)
            self.inv = inv
        else:
            self.nn = lambda a, b: _raw1(a, b, _NN)
            self.nt = lambda a, b: _raw1(a, b, _NT)
            self.tn = lambda a, b: _raw1(a, b, _TN)
            self.nn3 = lambda a, b: _raw3(a, b, _NN)
            self.nt3 = lambda a, b: _raw3(a, b, _NT)
            self.tn3 = lambda a, b: _raw3(a, b, _TN)
            self.inv = lambda a, t: _tri_inv_raw(a, self.nn3) if t is None else t


def _sigmoid(x):
    return 1.0 / (1.0 + jnp.exp(-x))


def _silu(x):
    return x * _sigmoid(x)


def _rms(x, w):
    return x * lax.rsqrt(jnp.mean(x * x, axis=-1, keepdims=True) + EPS) * w


def _tile(dim, target):
    t = (min(dim, target) // LANE) * LANE
    while t >= LANE:
        if dim % t == 0:
            return t
        t -= LANE
    return dim


MM_TM, MM_TN, MM_TK = 1408, 1536, 1408


def _matmul(a, b, ta=False, tb=False, out_dtype=F32, name="matmul", gather=None, exchange=None):
    carried = gather if gather is not None else exchange if exchange is not None else []
    nc = len(carried)
    if ta:
        K, M = a.shape
    else:
        M, K = a.shape
    if tb:
        N, K2 = b.shape
    else:
        K2, N = b.shape
    assert K == K2, (a.shape, b.shape, ta, tb)
    tm, tn, tk = _tile(M, MM_TM), _tile(N, MM_TN), _tile(K, MM_TK)
    nk = K // tk
    dims = ((0,) if ta else (1,), (1,) if tb else (0,))

    grid = (M // tm, N // tn, nk)

    def body(*refs):
        a_ref, b_ref = refs[:2]
        o_ref = refs[2 + nc]
        scratch = refs[3 + 2 * nc:]
        k = pl.program_id(2)
        if nc:
            make_plan = _gather_plan if gather is not None else _exchange_plan
            plan = make_plan(refs[2:2 + nc], refs[3 + nc:3 + 2 * nc], *scratch[-3:])
            at = lambda pos: ((pl.program_id(0) == pos[0]) & (pl.program_id(1) == pos[1]) & (k == pos[2]))

            @pl.when(at((0, 0, 0)))
            def _():
                _start(plan)

        part = _dot(a_ref[...].astype(BF16), b_ref[...].astype(BF16), dims)
        if nk == 1:
            o_ref[...] = part.astype(o_ref.dtype)
        else:
            acc_ref = scratch[0]

            @pl.when(k == 0)
            def _():
                acc_ref[...] = part

            @pl.when((k > 0) & (k < nk - 1))
            def _():
                acc_ref[...] += part

            @pl.when(k == nk - 1)
            def _():
                o_ref[...] = (acc_ref[...] + part).astype(o_ref.dtype)

        if nc:
            @pl.when(at((grid[0] - 1, grid[1] - 1, nk - 1)))
            def _():
                _finish(plan)

    a_spec = (pl.BlockSpec((tk, tm), lambda i, j, k: (k, i)) if ta
              else pl.BlockSpec((tm, tk), lambda i, j, k: (i, k)))
    b_spec = (pl.BlockSpec((tn, tk), lambda i, j, k: (j, k)) if tb
              else pl.BlockSpec((tk, tn), lambda i, j, k: (k, j)))
    if gather is not None:
        c_shapes, c_sems = _gather_shapes(carried), _gather_sems(nc)
    elif exchange is not None:
        c_shapes, c_sems = _exchange_shapes(carried), _exchange_sems(nc)
    else:
        c_shapes, c_sems = [], []
    res = pl.pallas_call(
        body, name=name, grid=grid,
        in_specs=[a_spec, b_spec] + _hbm_specs(nc),
        out_specs=[pl.BlockSpec((tm, tn), lambda i, j, k: (i, j))] + _hbm_specs(nc),
        out_shape=[jax.ShapeDtypeStruct((M, N), out_dtype)] + c_shapes,
        scratch_shapes=([pltpu.VMEM((tm, tn), F32)] if nk > 1 else []) + c_sems,
        compiler_params=_cparams(("arbitrary",) * 3 if nc else ("parallel", "parallel", "arbitrary")),
    )(a, b, *carried)
    return (res[0], res[1:]) if nc else res[0]


def _sds(shape, dtype=F32):
    return jax.ShapeDtypeStruct(shape, dtype)


def _norm_mod(x, nw, scale, shift):
    return _rms(x, nw) * (1.0 + scale) + shift


def _norm_in_proj(x, nw, scale, shift, w_in_pt, shards):
    T = x.shape[0]
    N = w_in_pt.shape[0]
    tm, tn = _tile(T, 1024), _tile(N, MM_TN)
    nm, nn = T // tm, N // tn
    ns = len(shards)

    def body(*refs):
        x_ref, nw_ref, sc_ref, sh_ref, w_ref = refs[:5]
        h_ref, o_ref = refs[5 + ns:7 + ns]
        plan = _gather_plan(refs[5:5 + ns], refs[7 + ns:7 + 2 * ns], *refs[7 + 2 * ns:])
        i, j = pl.program_id(0), pl.program_id(1)

        @pl.when((i == 0) & (j == 0))
        def _():
            _start(plan)

        @pl.when(j == 0)
        def _():
            for r0 in range(0, tm, ROWS_EPI):
                rows = pl.ds(r0, ROWS_EPI)
                h_ref[rows, :] = _norm_mod(x_ref[rows, :], nw_ref[...], sc_ref[...], sh_ref[...]).astype(BF16)

        o_ref[...] = _dot(h_ref[...], w_ref[...], _NT)

        @pl.when((i == nm - 1) & (j == nn - 1))
        def _():
            _finish(plan)

    vec = pl.BlockSpec((1, D), lambda i, j: (0, 0))
    res = pl.pallas_call(
        body, name="norm1_in_proj", grid=(nm, nn),
        in_specs=[pl.BlockSpec((tm, D), lambda i, j: (i, 0)), vec, vec, vec,
                  pl.BlockSpec((tn, D), lambda i, j: (j, 0))] + _hbm_specs(ns),
        out_specs=[pl.BlockSpec((tm, D), lambda i, j: (i, 0)), pl.BlockSpec((tm, tn), lambda i, j: (i, j))]
                  + _hbm_specs(ns),
        out_shape=[_sds((T, D), BF16), _sds((T, N))] + _gather_shapes(shards),
        scratch_shapes=_gather_sems(ns),
        compiler_params=_cparams(("arbitrary", "arbitrary")),
    )(x, nw, scale, shift, w_in_pt, *shards)
    return res[0], res[1], res[2:]


ROWS_TM = 512
ROWS_EPI = 256


def _matmul_rows(a, b, epi, tiled, consts, out_tiled, out_acc, name, pieces=()):
    T, K = a.shape
    tm, tk = _tile(T, ROWS_TM), _tile(K, MM_TK)
    nm, nk = T // tm, K // tk
    npc, nt, ncst, no, na = len(pieces), len(tiled), len(consts), len(out_tiled), len(out_acc)
    n_in = 2 + nt + ncst

    def body(*refs):
        a_ref, b_ref = refs[:2]
        t_refs, c_refs = refs[2:2 + nt], refs[2 + nt:n_in]
        o_refs = refs[n_in + npc:n_in + npc + no]
        acc_refs = refs[n_in + npc + no:n_in + npc + no + na]
        n_out = no + na + npc
        res_ref = refs[n_in + npc + n_out]
        plan = _exchange_plan(refs[n_in:n_in + npc], refs[n_in + npc + no + na:n_in + npc + n_out],
                              *refs[n_in + npc + n_out + 1:]) if npc else None
        i, k = pl.program_id(0), pl.program_id(1)

        @pl.when((i == 0) & (k == 0))
        def _():
            for r in acc_refs:
                r[...] = jnp.zeros_like(r)
            if npc:
                _start(plan)

        part = _dot(a_ref[...], b_ref[...], _NN)

        @pl.when(k == 0)
        def _():
            res_ref[...] = part

        @pl.when(k > 0)
        def _():
            res_ref[...] += part

        @pl.when(k == nk - 1)
        def _():
            for r0 in range(0, tm, ROWS_EPI):
                rows = pl.ds(r0, ROWS_EPI)
                outs = epi(res_ref[rows, :], *[r[rows, :] for r in t_refs], *[r[...] for r in c_refs])
                for r, v in zip(o_refs, outs[:no]):
                    r[rows, :] = v.astype(r.dtype)
                for r, v in zip(acc_refs, outs[no:]):
                    r[...] += v

        if npc:
            @pl.when((i == nm - 1) & (k == nk - 1))
            def _():
                _finish(plan)

    row = lambda w: pl.BlockSpec((tm, w), lambda i, k: (i, 0))
    whole = lambda s: pl.BlockSpec(s.shape, lambda i, k: (0, 0))
    res = pl.pallas_call(
        body, name=name, grid=(nm, nk),
        in_specs=[pl.BlockSpec((tm, tk), lambda i, k: (i, k)), pl.BlockSpec((tk, D), lambda i, k: (k, 0))]
                 + [row(t.shape[1]) for t in tiled] + [whole(c) for c in consts] + _hbm_specs(npc),
        out_specs=[row(s.shape[1]) for s in out_tiled] + [whole(s) for s in out_acc] + _hbm_specs(npc),
        out_shape=list(out_tiled) + list(out_acc) + (_exchange_shapes(pieces) if npc else []),
        scratch_shapes=[pltpu.VMEM((tm, D), F32)] + (_exchange_sems(npc) if npc else []),
        compiler_params=_cparams(("arbitrary", "arbitrary")),
    )(a, b, *tiled, *consts, *pieces)
    return res[:no + na], res[no + na:]


def _in_proj_dx_norm_bwd(dproj, w_in_pt, x, dres, nw, scale, shift, pieces):
    T = x.shape[0]

    def epi(dh, x, dres, nw, scale, shift):
        _, vjp = jax.vjp(_norm_mod, x, nw, scale, shift)
        dx, dnw, dsc, dsh = vjp(dh)
        return dx + dres, dnw, dsc, dsh

    return _matmul_rows(dproj, w_in_pt, epi, [x, dres], [nw, scale, shift], [_sds((T, D))], [_sds((1, D))] * 3,
                        "in_proj_dx_norm1_bwd", pieces)


def _out_proj_resid_norm(mixcat, w_out, x, gate1, nw, scale, shift):
    T = x.shape[0]

    def epi(mixed, x, gate1, nw, scale, shift):
        return (mixed,) + _resid_norm(x, mixed, gate1, nw, scale, shift)

    outs, _ = _matmul_rows(mixcat, w_out, epi, [x], [gate1, nw, scale, shift],
                           [_sds((T, D)), _sds((T, D)), _sds((T, D), BF16)], [], "out_proj_resid_norm2")
    return outs


def _ffn_up_dx_resid_bwd(dab, w_gut, x, mixed, dy, gate1, nw, scale, shift):
    T = x.shape[0]

    def epi(dh2, x, mixed, dy, gate1, nw, scale, shift):
        _, vjp = jax.vjp(_resid_norm, x, mixed, gate1, nw, scale, shift)
        return vjp((dy, dh2))

    outs, _ = _matmul_rows(dab, w_gut, epi, [x, mixed, dy], [gate1, nw, scale, shift],
                           [_sds((T, D)), _sds((T, D), BF16)], [_sds((1, D))] * 4, "ffn_up_dx_resid_norm2_bwd")
    return outs


def _ffn_down_loss(act, w_down, x1, target, gate2):
    T = x1.shape[0]

    def epi(ffn, x1, target, gate2):
        y = x1 + gate2 * ffn
        err = y - target
        loss = 0.5 * jnp.sum(jnp.sum(err * err, axis=1, keepdims=True), axis=0, keepdims=True) / D
        dy = err * (1.0 / D)
        return dy, gate2 * dy, jnp.sum(dy * ffn, axis=0, keepdims=True), jnp.broadcast_to(loss, (1, LANE))

    outs, _ = _matmul_rows(act, w_down, epi, [x1, target], [gate2], [_sds((T, D)), _sds((T, D), BF16)],
                           [_sds((1, D)), _sds((1, LANE))], "ffn_down_loss")
    return outs


def _resid_norm(x, mixed, gate1, nw, scale, shift):
    x1 = x + gate1 * mixed
    return x1, _norm_mod(x1, nw, scale, shift)


FFN_BLK = 256
FFN_TM = 2048


def _interleave_gate_up(gate_t, up_t):
    blocks = lambda t: t.reshape(DFF // FFN_BLK, 1, FFN_BLK, D)
    return jnp.concatenate([blocks(gate_t), blocks(up_t)], axis=1).reshape(2 * DFF, D)


def _split_gate_up(g):
    g = g.reshape(DFF // FFN_BLK, 2, FFN_BLK, D)
    return g[:, 0].reshape(DFF, D), g[:, 1].reshape(DFF, D)


def _ffn_up_act(h2, w_gut):
    T = h2.shape[0]
    tm = _tile(T, FFN_TM)

    def body(h_ref, w_ref, ab_ref, act_ref):
        ab = _dot(h_ref[...], w_ref[...], _NT)
        ab_ref[...] = ab
        act_ref[...] = (_silu(ab[:, :FFN_BLK]) * ab[:, FFN_BLK:]).astype(act_ref.dtype)

    return pl.pallas_call(
        body, name="ffn_up_act", grid=(T // tm, DFF // FFN_BLK),
        in_specs=[pl.BlockSpec((tm, D), lambda i, j: (i, 0)), pl.BlockSpec((2 * FFN_BLK, D), lambda i, j: (j, 0))],
        out_specs=[pl.BlockSpec((tm, 2 * FFN_BLK), lambda i, j: (i, j)), pl.BlockSpec((tm, FFN_BLK), lambda i, j: (i, j))],
        out_shape=[_sds((T, 2 * DFF)), _sds((T, DFF), BF16)],
        compiler_params=_cparams(("parallel", "parallel")),
    )(h2, w_gut)


def _ffn_down_dx_act(dffn, w_down, ab):
    T = dffn.shape[0]
    tm = _tile(T, FFN_TM)

    def body(d_ref, w_ref, ab_ref, o_ref):
        dact = _dot(d_ref[...], w_ref[...], _NT)
        a, b = ab_ref[:, :FFN_BLK], ab_ref[:, FFN_BLK:]
        s = _sigmoid(a)
        da = dact * b * (s * (1.0 + a * (1.0 - s)))
        db = dact * (a * s)
        o_ref[...] = jnp.concatenate([da, db], axis=1).astype(o_ref.dtype)

    return pl.pallas_call(
        body, name="ffn_down_dx_act", grid=(T // tm, DFF // FFN_BLK),
        in_specs=[pl.BlockSpec((tm, D), lambda i, j: (i, 0)), pl.BlockSpec((FFN_BLK, D), lambda i, j: (j, 0)),
                  pl.BlockSpec((tm, 2 * FFN_BLK), lambda i, j: (i, j))],
        out_specs=pl.BlockSpec((tm, 2 * FFN_BLK), lambda i, j: (i, j)),
        out_shape=_sds((T, 2 * DFF), BF16),
        compiler_params=_cparams(("parallel", "parallel")),
    )(dffn, w_down, ab)


def _round_bf16(x):
    return x.astype(BF16).astype(F32)


def _shift_down(x, s, rows):
    if s == 0:
        return x
    return jnp.where(rows >= s, pltpu.roll(x, s, 0), 0.0)


def _shift_up(x, s, rows, T):
    if s == 0:
        return x
    return jnp.where(rows < T - s, pltpu.roll(x, T - s, 0), 0.0)


def _conv_fwd(proj, conv_w):
    T = proj.shape[0]
    ncol = 3 * GW // LANE

    def body(x_ref, w_ref, o_ref):
        x = _round_bf16(x_ref[...])
        rows = lax.broadcasted_iota(jnp.int32, x.shape, 0)
        acc = jnp.zeros_like(x)
        for j in range(CONVW):
            acc = acc + _round_bf16(w_ref[pl.ds(j, 1), :]) * _shift_down(x, CONVW - 1 - j, rows)
        o_ref[0], o_ref[1] = _split_pair(_silu(acc))

    return pl.pallas_call(
        body, name="conv_fwd", grid=(ncol,),
        in_specs=[pl.BlockSpec((T, LANE), lambda j: (0, j)), pl.BlockSpec((CONVW, LANE), lambda j: (0, j))],
        out_specs=pl.BlockSpec((2, T, HD), lambda j: (j, 0, 0)),
        out_shape=_sds((3 * GH, T, HD)),
        compiler_params=_cparams(("parallel",)),
    )(proj, conv_w)


RELAYOUT_TM = 4096


def _split_pair(y):
    return y[:, :HD], pltpu.roll(y, HD, 1)[:, :HD]


def _merge_pair(a, b):
    return jnp.concatenate([a, b], axis=1)


def _split_heads(x, col_block0, nheads, name):
    T = x.shape[0]
    tm = _tile(T, RELAYOUT_TM)

    def body(x_ref, o_ref):
        a, b = _split_pair(x_ref[...])
        o_ref[0] = a
        o_ref[1] = b

    return pl.pallas_call(
        body, name=name, grid=(nheads // 2, T // tm),
        in_specs=[pl.BlockSpec((tm, LANE), lambda j, i: (i, col_block0 + j))],
        out_specs=pl.BlockSpec((2, tm, HD), lambda j, i: (j, i, 0)),
        out_shape=_sds((nheads, T, HD), x.dtype),
        compiler_params=_cparams(("parallel", "parallel")),
    )(x)


def _merge_heads(hm, out_dtype, name, into=None, col_block0=0, head0=0, nheads=None):
    T = hm.shape[1]
    nheads = hm.shape[0] if nheads is None else nheads
    tm = _tile(T, RELAYOUT_TM)

    def body(*refs):
        h_ref, o_ref = refs[0], refs[-1]
        o_ref[...] = _merge_pair(h_ref[0], h_ref[1]).astype(o_ref.dtype)

    in_specs = [pl.BlockSpec((2, tm, HD), lambda j, i: (head0 // 2 + j, i, 0))]
    args = [hm]
    if into is None:
        out_shape = _sds((T, HD * nheads), out_dtype)
        aliases = {}
    else:
        out_shape = _sds(into.shape, into.dtype)
        in_specs.append(pl.BlockSpec(memory_space=pl.ANY))
        args.append(into)
        aliases = {1: 0}
    return pl.pallas_call(
        body, name=name, grid=(nheads // 2, T // tm),
        in_specs=in_specs,
        out_specs=pl.BlockSpec((tm, LANE), lambda j, i: (i, col_block0 + j)),
        out_shape=out_shape, input_output_aliases=aliases,
        compiler_params=_cparams(("parallel", "parallel")),
    )(*args)


def _conv_bwd(proj, conv_w, dqc):
    T = proj.shape[0]
    ncol = 3 * GW // LANE

    def body(x_ref, w_ref, d_ref, dx_ref, dw_ref):
        x = _round_bf16(x_ref[...])
        rows = lax.broadcasted_iota(jnp.int32, x.shape, 0)
        xs = [_shift_down(x, CONVW - 1 - j, rows) for j in range(CONVW)]
        w = [_round_bf16(w_ref[pl.ds(j, 1), :]) for j in range(CONVW)]
        pre = jnp.zeros_like(x)
        for j in range(CONVW):
            pre = pre + w[j] * xs[j]
        s = _sigmoid(pre)
        dpre = _round_bf16(_merge_pair(d_ref[0], d_ref[1]) * (s * (1.0 + pre * (1.0 - s))))
        dx = jnp.zeros_like(x)
        for j in range(CONVW):
            dx = dx + w[j] * _shift_up(dpre, CONVW - 1 - j, rows, T)
            dw_ref[pl.ds(j, 1), :] = jnp.sum(dpre * xs[j], axis=0, keepdims=True)
        dx_ref[...] = dx.astype(dx_ref.dtype)

    return pl.pallas_call(
        body, name="conv_bwd", grid=(ncol,),
        in_specs=[pl.BlockSpec((T, LANE), lambda j: (0, j)), pl.BlockSpec((CONVW, LANE), lambda j: (0, j)),
                  pl.BlockSpec((2, T, HD), lambda j: (j, 0, 0))],
        out_specs=[pl.BlockSpec((T, LANE), lambda j: (0, j)), pl.BlockSpec((CONVW, LANE), lambda j: (0, j))],
        out_shape=[_sds((T, NP), BF16), _sds((CONVW, 3 * GW))],
        compiler_params=_cparams(("parallel",)),
    )(proj, conv_w, dqc)


def _gdn_prep(kit, q, k, v, ga, gb, alog, dtb, t_inv=None):
    C = CHUNK
    ri = lax.broadcasted_iota(jnp.int32, (C, C), 0)
    ci = lax.broadcasted_iota(jnp.int32, (C, C), 1)
    causal = ri >= ci
    strict = ri > ci
    eye = (ri == ci).astype(F32)
    lower = causal.astype(F32)
    upper = (ri <= ci).astype(F32)

    a = ga + dtb
    softplus = jnp.maximum(a, 0.0) + jnp.log(1.0 + jnp.exp(-jnp.abs(a)))
    g_row = -jnp.exp(alog) * softplus
    beta_row = _sigmoid(gb)
    g_col = jnp.sum(eye * g_row, axis=2, keepdims=True)
    beta_col = jnp.sum(eye * beta_row, axis=2, keepdims=True)
    G_col = jnp.sum(lower * g_row, axis=2, keepdims=True)
    G_row = jnp.sum(upper * g_col, axis=1, keepdims=True)
    G_last = jnp.sum(g_row, axis=2, keepdims=True)
    decay = jnp.exp(jnp.where(causal, G_col - G_row, -1e30))

    qn = q * lax.rsqrt(jnp.sum(q * q, axis=-1, keepdims=True) + EPS) * (HD ** -0.5)
    kn = k * lax.rsqrt(jnp.sum(k * k, axis=-1, keepdims=True) + EPS)
    kb = kn * beta_col
    A = jnp.where(strict, kit.nt(kb, kn) * decay, 0.0)
    Tm = kit.inv(A, t_inv)
    eG = jnp.exp(G_col)
    u = kit.nn3(Tm, v * beta_col)
    w = kit.nn3(Tm, kb * eG)
    qk = jnp.where(causal, kit.nt(qn, kn) * decay, 0.0)
    q_dec = qn * eG
    k_dec = kn * jnp.exp(G_last - G_col)
    dec = jnp.exp(G_last)
    return u, w, qk, q_dec, k_dec, dec, Tm


def _gdn_out(o, z, nw):
    return _rms(o, nw) * _silu(z)


GDN_CB = 4


def _gdn_specs(T, blk):
    TB = GDN_CB * CHUNK
    seq = lambda grp: pl.BlockSpec((GH, TB, HD), lambda i, grp=grp: (grp, blk(i), 0))
    row = lambda grp: pl.BlockSpec((GH, GDN_CB, 1, CHUNK), lambda i, grp=grp: (grp, blk(i), 0, 0))
    per_head = pl.BlockSpec((GH, 1, CHUNK), lambda i: (0, 0, 0))
    whole = pl.BlockSpec((1, HD), lambda i: (0, 0))
    state = pl.BlockSpec((GH, GDN_CB, HD, HD), lambda i: (0, blk(i), 0, 0))
    return seq, row, per_head, whole, state


def _gdn_load(seq_refs, row_refs, head_refs):
    chunks = lambda r: jnp.concatenate([r[:, pl.ds(cb * CHUNK, CHUNK), :] for cb in range(GDN_CB)], axis=0)
    rows = lambda r: jnp.concatenate([r[:, cb] for cb in range(GDN_CB)], axis=0)
    heads = lambda r: jnp.concatenate([r[...]] * GDN_CB, axis=0)
    return [chunks(r) for r in seq_refs], [rows(r) for r in row_refs], [heads(r) for r in head_refs]


def _gdn_fwd(qkv_hm, zs_hm, gab, alog_b, dtb_b, nw, shards):
    T = qkv_hm.shape[1]
    N = T // CHUNK
    nblk = N // GDN_CB
    ns = len(shards)
    seq, row, per_head, whole, state = _gdn_specs(T, lambda i: i)
    kit = _Kit(False)

    def body(*refs):
        q_ref, k_ref, v_ref, z_ref, ga_ref, gb_ref, al_ref, dt_ref, nw_ref = refs[:9]
        o_ref, S_ref, T_ref = refs[9 + ns:12 + ns]
        S_scr = refs[12 + 2 * ns]
        plan = _gather_plan(refs[9:9 + ns], refs[12 + ns:12 + 2 * ns], *refs[13 + 2 * ns:])

        @pl.when(pl.program_id(0) == 0)
        def _():
            S_scr[...] = jnp.zeros_like(S_scr)
            _start(plan)

        (q, k, v, z), (ga, gb), (al, dt) = _gdn_load((q_ref, k_ref, v_ref, z_ref), (ga_ref, gb_ref), (al_ref, dt_ref))
        u, w, qk, q_dec, k_dec, dec, t_inv = _gdn_prep(kit, q, k, v, ga, gb, al, dt)
        S = S_scr[...]
        for cb in range(GDN_CB):
            hs = slice(cb * GH, (cb + 1) * GH)
            S_ref[:, cb] = S
            T_ref[:, cb] = t_inv[hs]
            v_new = u[hs] - kit.nn(w[hs], S)
            o = kit.nn(q_dec[hs], S) + kit.nn(qk[hs], v_new)
            S = S * dec[hs] + kit.tn(k_dec[hs], v_new)
            o_ref[:, pl.ds(cb * CHUNK, CHUNK), :] = _gdn_out(o, z[hs], nw_ref[...])
        S_scr[...] = S

        @pl.when(pl.program_id(0) == nblk - 1)
        def _():
            _finish(plan)

    res = pl.pallas_call(
        body, name="gdn_fwd", grid=(nblk,),
        in_specs=[seq(0), seq(1), seq(2), seq(0), row(0), row(1), per_head, per_head, whole] + _hbm_specs(ns),
        out_specs=[seq(0), state, state] + _hbm_specs(ns),
        out_shape=[_sds((GH + SQH, T, HD)), _sds((GH, N, HD, HD)), _sds((GH, N, CHUNK, CHUNK))]
                  + _gather_shapes(shards),
        scratch_shapes=[pltpu.VMEM((GH, HD, HD), F32)] + _gather_sems(ns),
        compiler_params=_cparams(("arbitrary",)),
    )(qkv_hm, qkv_hm, qkv_hm, zs_hm, gab, gab, alog_b, dtb_b, nw, *shards)
    return res[0], (res[1], res[2]), res[3:]


def _gdn_bwd(qkv_hm, zs_hm, gab, alog_b, dtb_b, nw, S_all, do, pieces):
    T = qkv_hm.shape[1]
    N = T // CHUNK
    nblk = N // GDN_CB
    npc = len(pieces)
    dkit, kit = _Kit(True), _Kit(False)
    rseq, rrow, per_head, whole, rstate = _gdn_specs(T, lambda i: nblk - 1 - i)

    def body(*refs):
        q_ref, k_ref, v_ref, z_ref, ga_ref, gb_ref, al_ref, dt_ref, nw_ref, S_ref, T_ref, do_ref = refs[:12]
        dqkv_ref, dz_ref, dga_ref, dgb_ref, dal_ref, ddt_ref, dnw_ref = refs[12 + npc:19 + npc]
        dS_scr = refs[19 + 2 * npc]
        plan = _exchange_plan(refs[12:12 + npc], refs[19 + npc:19 + 2 * npc], *refs[20 + 2 * npc:])

        @pl.when(pl.program_id(0) == 0)
        def _():
            dS_scr[...] = jnp.zeros_like(dS_scr)
            dal_ref[...] = jnp.zeros_like(dal_ref)
            ddt_ref[...] = jnp.zeros_like(ddt_ref)
            dnw_ref[...] = jnp.zeros_like(dnw_ref)
            _start(plan)

        (q, k, v, z, dout), (ga, gb), (al, dt) = _gdn_load((q_ref, k_ref, v_ref, z_ref, do_ref), (ga_ref, gb_ref),
                                                          (al_ref, dt_ref))
        S_in = jnp.concatenate([S_ref[:, cb] for cb in range(GDN_CB)], axis=0)
        t_inv = jnp.concatenate([T_ref[:, cb] for cb in range(GDN_CB)], axis=0)
        prep = lambda *a: _gdn_prep(dkit, *a, t_inv=t_inv)[:6]
        (u, w, qk, q_dec, k_dec, dec), prep_vjp = jax.vjp(prep, q, k, v, ga, gb, al, dt)
        v_new = u - kit.nn(w, S_in)
        o = kit.nn(q_dec, S_in) + kit.nn(qk, v_new)
        _, out_vjp = jax.vjp(_gdn_out, o, z, nw_ref[...])
        do, dz, dnw = out_vjp(dout)
        dvn_part = kit.tn(qk, do)
        dS_part = kit.tn(q_dec, do)
        dS = dS_scr[...]
        dS_out, dvn = [None] * GDN_CB, [None] * GDN_CB
        for cb in reversed(range(GDN_CB)):
            hs = slice(cb * GH, (cb + 1) * GH)
            dS_out[cb] = dS
            dvn[cb] = dvn_part[hs] + kit.nn(k_dec[hs], dS)
            dS = dS * dec[hs] + dS_part[hs] - kit.tn(w[hs], dvn[cb])
        dS_scr[...] = dS
        dS_out = jnp.concatenate(dS_out, axis=0)
        dvn = jnp.concatenate(dvn, axis=0)
        ddec = jnp.sum(jnp.sum(S_in * dS_out, axis=2, keepdims=True), axis=1, keepdims=True)
        cts = (dvn, -kit.nt(dvn, S_in), kit.nt(do, v_new), kit.nt(do, S_in), kit.nt(v_new, dS_out), ddec)
        dq, dk, dv, dga, dgb, dal, ddt = prep_vjp(cts)
        lanesum = lambda t: jnp.broadcast_to(jnp.sum(t, axis=2, keepdims=True), t.shape)
        for cb in range(GDN_CB):
            hs = slice(cb * GH, (cb + 1) * GH)
            sl = pl.ds(cb * CHUNK, CHUNK)
            dqkv_ref[pl.ds(0, GH), sl, :] = dq[hs]
            dqkv_ref[pl.ds(GH, GH), sl, :] = dk[hs]
            dqkv_ref[pl.ds(2 * GH, GH), sl, :] = dv[hs]
            dz_ref[:, sl, :] = dz[hs]
            dga_ref[:, cb] = dga[hs]
            dgb_ref[:, cb] = dgb[hs]
            dal_ref[...] += lanesum(dal[hs])
            ddt_ref[...] += lanesum(ddt[hs])
        dnw_ref[...] += dnw

        @pl.when(pl.program_id(0) == nblk - 1)
        def _():
            _finish(plan)

    res = pl.pallas_call(
        body, name="gdn_bwd", grid=(nblk,),
        in_specs=[rseq(0), rseq(1), rseq(2), rseq(0), rrow(0), rrow(1), per_head, per_head, whole, rstate, rstate,
                  rseq(0)] + _hbm_specs(npc),
        out_specs=[pl.BlockSpec((3 * GH, GDN_CB * CHUNK, HD), lambda i: (0, nblk - 1 - i, 0)), rseq(0), rrow(0),
                   rrow(0), per_head, per_head, whole] + _hbm_specs(npc),
        out_shape=[_sds((3 * GH, T, HD)), _sds((GH + 4 + SWA_GRAD_HEADS, T, HD))] + [_sds((GH, N, 1, CHUNK))] * 2
                  + [_sds((GH, 1, CHUNK))] * 2 + [_sds((1, HD))] + _exchange_shapes(pieces),
        scratch_shapes=[pltpu.VMEM((GH, HD, HD), F32)] + _exchange_sems(npc),
        compiler_params=_cparams(("arbitrary",), GDN_BWD_VMEM),
    )(qkv_hm, qkv_hm, qkv_hm, zs_hm, gab, gab, alog_b, dtb_b, nw, S_all[0], S_all[1], do, *pieces)
    return res[:7], res[7:]


def _swa_heads(kit, first, q, kp, kc, vp, vc, qnw, knw, sink, slope):
    W = WIN
    ri = lax.broadcasted_iota(jnp.int32, (W, W), 0)
    ci = lax.broadcasted_iota(jnp.int32, (W, W), 1)
    mask_c = ri >= ci
    mask_p = ci > ri + first * W
    dist_c = (ri - ci).astype(F32)
    dist_p = (ri - ci + W).astype(F32)
    kpn = _rms(kp, knw)
    kcn = _rms(kc, knw)
    qn = _rms(q, qnw)
    sc = jnp.where(mask_c, kit.nt(qn, kcn) * (HD ** -0.5) - slope * dist_c, -1e30)
    sp = jnp.where(mask_p, kit.nt(qn, kpn) * (HD ** -0.5) - slope * dist_p, -1e30)
    m = jnp.maximum(jnp.maximum(jnp.max(sc, axis=-1, keepdims=True), jnp.max(sp, axis=-1, keepdims=True)), sink)
    m = lax.stop_gradient(m)
    pc = jnp.exp(sc - m)
    pp = jnp.exp(sp - m)
    den = jnp.sum(pc, axis=-1, keepdims=True) + jnp.sum(pp, axis=-1, keepdims=True) + jnp.exp(sink - m)
    inv = 1.0 / den
    return kit.nn(pc * inv, vc) + kit.nn(pp * inv, vp)


def _swa_grads(kit, first, q, kp, kc, vp, vc, qnw, knw, sink, slope, do):
    W = WIN
    ri = lax.broadcasted_iota(jnp.int32, (W, W), 0)
    ci = lax.broadcasted_iota(jnp.int32, (W, W), 1)
    mask_c = ri >= ci
    mask_p = ci > ri + first * W
    dist_c = (ri - ci).astype(F32)
    dist_p = (ri - ci + W).astype(F32)
    scale = HD ** -0.5
    kpn, kp_vjp = jax.vjp(_rms, kp, knw)
    kcn, kc_vjp = jax.vjp(_rms, kc, knw)
    qn, q_vjp = jax.vjp(_rms, q, qnw)
    sc = jnp.where(mask_c, kit.nt(qn, kcn) * scale - slope * dist_c, -1e30)
    sp = jnp.where(mask_p, kit.nt(qn, kpn) * scale - slope * dist_p, -1e30)
    m = jnp.maximum(jnp.maximum(jnp.max(sc, axis=-1, keepdims=True), jnp.max(sp, axis=-1, keepdims=True)), sink)
    ec = jnp.exp(sc - m)
    ep = jnp.exp(sp - m)
    es = jnp.exp(sink - m)
    inv = 1.0 / (jnp.sum(ec, axis=-1, keepdims=True) + jnp.sum(ep, axis=-1, keepdims=True) + es)
    pc, pp = ec * inv, ep * inv
    dpc, dpp = kit.nt(do, vc), kit.nt(do, vp)
    delta = jnp.sum(dpc * pc, axis=-1, keepdims=True) + jnp.sum(dpp * pp, axis=-1, keepdims=True)
    dsc = pc * (dpc - delta) * scale
    dsp = pp * (dpp - delta) * scale
    dq, dqnw = q_vjp(kit.nn(dsc, kcn) + kit.nn(dsp, kpn))
    dkc, dknw_c = kc_vjp(kit.tn(dsc, qn))
    dkp, dknw_p = kp_vjp(kit.tn(dsp, qn))
    return dq, dkp, dkc, kit.tn(pp, do), kit.tn(pc, do), dqnw, dknw_c + dknw_p, -(es * inv) * delta


def _per_query_head(kv_ref):
    return jnp.concatenate([kv_ref[pl.ds(h // SGRP, 1)] for h in range(SQH)], axis=0)


def _per_kv_head(d):
    return jnp.concatenate([jnp.sum(d[g * SGRP:(g + 1) * SGRP], axis=0, keepdims=True) for g in range(SKVH)], axis=0)


def _swa_specs(blk):
    qspec = pl.BlockSpec((SQH, WIN, HD), lambda i: (1, blk(i), 0))
    cur = lambda grp: pl.BlockSpec((SKVH, WIN, HD), lambda i, grp=grp: (grp, blk(i), 0))
    prev = lambda grp: pl.BlockSpec((SKVH, WIN, HD), lambda i, grp=grp: (grp, jnp.maximum(blk(i) - 1, 0), 0))
    whole = pl.BlockSpec((1, HD), lambda i: (0, 0))
    col = pl.BlockSpec((SQH, WIN, 1), lambda i: (0, 0, 0))
    ospec = pl.BlockSpec((SQH, WIN, HD), lambda i: (0, blk(i), 0))
    return qspec, cur, prev, whole, col, ospec


def _swa_fwd(zs_hm, qnw, knw, sinks_col, slopes_col, o_buf, shards):
    T = zs_hm.shape[1]
    NB = T // WIN
    ns = len(shards)
    kit = _Kit(False)
    qspec, cur, prev, whole, col, _ = _swa_specs(lambda i: i)

    def body(*refs):
        q_ref, kp_ref, kc_ref, vp_ref, vc_ref, qnw_ref, knw_ref, s_ref, sl_ref = refs[:9]
        o_ref = refs[10 + ns]
        plan = _gather_plan(refs[10:10 + ns], refs[11 + ns:11 + 2 * ns], *refs[11 + 2 * ns:])

        @pl.when(pl.program_id(0) == 0)
        def _():
            _start(plan)

        first = (pl.program_id(0) == 0).astype(jnp.int32)
        o_ref[...] = _swa_heads(kit, first, q_ref[...], _per_query_head(kp_ref), _per_query_head(kc_ref),
                                _per_query_head(vp_ref), _per_query_head(vc_ref), qnw_ref[...], knw_ref[...],
                                s_ref[...], sl_ref[...])

        @pl.when(pl.program_id(0) == NB - 1)
        def _():
            _finish(plan)

    res = pl.pallas_call(
        body, name="swa_fwd", grid=(NB,),
        in_specs=[qspec, prev(8), cur(8), prev(9), cur(9), whole, whole, col, col] + _hbm_specs(1 + ns),
        out_specs=[pl.BlockSpec((SQH, WIN, HD), lambda i: (1, i, 0))] + _hbm_specs(ns),
        out_shape=[_sds(o_buf.shape)] + _gather_shapes(shards),
        input_output_aliases={9: 0},
        scratch_shapes=_gather_sems(ns),
        compiler_params=_cparams(("arbitrary",)),
    )(zs_hm, zs_hm, zs_hm, zs_hm, zs_hm, qnw, knw, sinks_col, slopes_col, o_buf, *shards)
    return res[0], res[1:]


SWA_GRAD_HEADS = SQH + 2 * SKVH


def _swa_bwd(zs_hm, qnw, knw, sinks_col, slopes_col, dmix_hm, d_buf):
    T = zs_hm.shape[1]
    NB = T // WIN
    kit = _Kit(False)
    qspec, cur, prev, whole, col, _ = _swa_specs(lambda i: NB - 1 - i)

    def body(q_ref, kp_ref, kc_ref, vp_ref, vc_ref, qnw_ref, knw_ref, s_ref, sl_ref, do_ref, buf_ref,
             d_ref, dqnw_ref, dknw_ref, ds_ref, ck_scr, cv_scr):
        dq_ref = d_ref.at[pl.ds(0, SQH)]
        dk_ref = d_ref.at[pl.ds(SQH, SKVH)]
        dv_ref = d_ref.at[pl.ds(SQH + SKVH, SKVH)]
        i = pl.program_id(0)
        first = (i == NB - 1).astype(jnp.int32)

        @pl.when(i == 0)
        def _():
            ck_scr[...] = jnp.zeros_like(ck_scr)
            cv_scr[...] = jnp.zeros_like(cv_scr)
            ds_ref[...] = jnp.zeros_like(ds_ref)
            dqnw_ref[...] = jnp.zeros_like(dqnw_ref)
            dknw_ref[...] = jnp.zeros_like(dknw_ref)

        dq, dkp, dkc, dvp, dvc, dqnw, dknw, dsink = _swa_grads(
            kit, first, q_ref[...], _per_query_head(kp_ref), _per_query_head(kc_ref), _per_query_head(vp_ref),
            _per_query_head(vc_ref), qnw_ref[...], knw_ref[...], s_ref[...], sl_ref[...], do_ref[...])
        dq_ref[...] = dq
        dk_ref[...] = _per_kv_head(dkc) + ck_scr[...]
        dv_ref[...] = _per_kv_head(dvc) + cv_scr[...]
        ck_scr[...] = _per_kv_head(dkp)
        cv_scr[...] = _per_kv_head(dvp)
        dqnw_ref[...] += dqnw
        dknw_ref[...] += dknw
        ds_ref[...] += jnp.broadcast_to(jnp.sum(dsink, axis=1, keepdims=True), dsink.shape)

    dospec = pl.BlockSpec((SQH, WIN, HD), lambda i: (1, NB - 1 - i, 0))
    dspec = pl.BlockSpec((SWA_GRAD_HEADS, WIN, HD), lambda i: (1, NB - 1 - i, 0))
    res = pl.pallas_call(
        body, name="swa_bwd", grid=(NB,),
        in_specs=[qspec, prev(8), cur(8), prev(9), cur(9), whole, whole, col, col, dospec] + _hbm_specs(1),
        out_specs=[dspec, whole, whole, col],
        out_shape=[_sds(d_buf.shape), _sds((1, HD)), _sds((1, HD)), _sds((SQH, WIN, 1))],
        input_output_aliases={10: 0},
        scratch_shapes=[pltpu.VMEM((SKVH, WIN, HD), F32), pltpu.VMEM((SKVH, WIN, HD), F32)],
        compiler_params=_cparams(("arbitrary",)),
    )(zs_hm, zs_hm, zs_hm, zs_hm, zs_hm, qnw, knw, sinks_col, slopes_col, dmix_hm, d_buf)
    return res


GAB0 = 3 * GW + 1280


W_IN_ROWS = PROJ // N_CHIP
W_IN_ROWS_PAD = 736


def _permute_w_in_t(w_in_t):
    return jnp.concatenate([w_in_t[:4 * GW], w_in_t[4 * GW + 2 * GH:], w_in_t[4 * GW:4 * GW + 2 * GH],
                            jnp.zeros((NP - PROJ, D), w_in_t.dtype)], axis=0)


def _w_in_grad_pieces(g_t):
    g = jnp.concatenate([g_t[:4 * GW], g_t[GAB0:GAB0 + 2 * GH], g_t[4 * GW:GAB0]], axis=0)
    g = jnp.pad(g.reshape(N_CHIP, W_IN_ROWS, D), ((0, 0), (0, W_IN_ROWS_PAD - W_IN_ROWS), (0, 0)))
    return g.reshape(N_CHIP, 2, W_IN_ROWS_PAD // 2, D)


def _pieces_by_rows(g):
    return g.reshape(N_CHIP, 2, g.shape[0] // (2 * N_CHIP), D)


def _local_step(x, target, mod, n1w, w_in_pt, conv_w, alog, dtb, gnw, qnw, knw, sinks, n2w, shards):
    sh_out, sh_gate, sh_up, sh_down = shards
    T = x.shape[0]
    N = T // CHUNK
    shift1, scale1, gate1, shift2, scale2, gate2 = [mod[:, i * D:(i + 1) * D] for i in range(6)]

    h, proj, (a_out,) = _norm_in_proj(x, n1w, scale1, shift1, w_in_pt, [sh_out])
    w_out = a_out.reshape(D, D)
    qkv_hm = _conv_fwd(proj, conv_w)
    zs_hm = _split_heads(proj, 3 * GW // LANE, 20, "split_zs")
    gab = proj[:, GAB0:GAB0 + 2 * GH].T.reshape(2 * GH, N, 1, CHUNK)
    alog_b = jnp.broadcast_to(alog.reshape(GH, 1, 1), (GH, 1, CHUNK))
    dtb_b = jnp.broadcast_to(dtb.reshape(GH, 1, 1), (GH, 1, CHUNK))
    sinks_col = jnp.broadcast_to(sinks.reshape(SQH, 1, 1), (SQH, WIN, 1))
    o_hm, S_all, (a_gate, a_up) = _gdn_fwd(qkv_hm, zs_hm, gab, alog_b, dtb_b, gnw, [sh_gate, sh_up])
    w_gut = _interleave_gate_up(a_gate.reshape(DFF, D), a_up.reshape(DFF, D))
    slopes = 2.0 ** (-8.0 * (jnp.arange(SQH, dtype=F32) + 1.0) / SQH)
    slopes_col = jnp.broadcast_to(slopes.reshape(SQH, 1, 1), (SQH, WIN, 1))
    o_hm, (a_down,) = _swa_fwd(zs_hm, qnw, knw, sinks_col, slopes_col, o_hm, [sh_down])
    w_down = a_down.reshape(DFF, D)
    mixcat = _merge_heads(o_hm, BF16, "merge_mix")
    mixed, x1, h2 = _out_proj_resid_norm(mixcat, w_out, x, gate1, n2w, scale2, shift2)
    ab, act = _ffn_up_act(h2, w_gut)
    dy, dffn, dgate2, loss = _ffn_down_loss(act, w_down, x1, target, gate2)

    dab = _ffn_down_dx_act(dffn, w_down, ab)
    g_w_down = _matmul(act, dffn, ta=True, out_dtype=BF16, name="ffn_down_dw")
    g_w_gut = _matmul(dab, h2, ta=True, out_dtype=BF16, name="ffn_up_dw")
    dx1, dmixed, dgate1, dn2w, dscale2, dshift2 = _ffn_up_dx_resid_bwd(dab, w_gut, x, mixed, dy, gate1, n2w, scale2,
                                                                       shift2)
    g_w_out = _matmul(mixcat, dmixed, ta=True, out_dtype=BF16, name="out_proj_dw")
    dmix_hm = _split_heads(_matmul(dmixed, w_out, tb=True, name="out_proj_dx"), 0, GH + SQH, "split_dmix")
    g_gate_t, g_up_t = _split_gate_up(g_w_gut)
    pieces = [_pieces_by_rows(g_w_out), _pieces_by_rows(g_gate_t), _pieces_by_rows(g_up_t),
              _pieces_by_rows(g_w_down)]
    (dqkv_hm, d_hm, dga, dgb, dalog, ddtb, dgnw), recv = _gdn_bwd(qkv_hm, zs_hm, gab, alog_b, dtb_b, gnw, S_all,
                                                                  dmix_hm, pieces)
    d_hm, dqnw, dknw, dsinks = _swa_bwd(zs_hm, qnw, knw, sinks_col, slopes_col, dmix_hm, d_hm)
    dproj, dconv = _conv_bwd(proj, conv_w, dqkv_hm)
    dproj = _merge_heads(d_hm, BF16, "merge_dz", into=dproj, col_block0=3 * GW // LANE, head0=0, nheads=GH)
    dproj = _merge_heads(d_hm, BF16, "merge_dswa", into=dproj, col_block0=4 * GW // LANE, head0=GH + 4,
                         nheads=SWA_GRAD_HEADS)
    dgab = jnp.concatenate([dga, dgb], axis=0).reshape(2 * GH, T).T.astype(BF16)
    dproj = lax.dynamic_update_slice(dproj, jnp.concatenate([dgab, jnp.zeros((T, NP - PROJ), BF16)], axis=1),
                                     (0, GAB0))
    g_w_in_pt = _matmul(dproj, h, ta=True, out_dtype=BF16, name="in_proj_dw")
    (grad_x, dn1w, dscale1, dshift1), recv_in = _in_proj_dx_norm_bwd(dproj, w_in_pt, x, dx1, n1w, scale1, shift1,
                                                                     [_w_in_grad_pieces(g_w_in_pt)])

    dmod = jnp.concatenate([dshift1, dscale1, dgate1, dshift2, dscale2, dgate2], axis=1)
    big = list(recv_in) + list(recv)
    small = dict(mod=dmod, norm1_w=dn1w, norm2_w=dn2w, conv_w=dconv, a_log=dalog[:, 0, 0], dt_bias=ddtb[:, 0, 0],
                 gdn_norm_w=dgnw, q_norm_w=dqnw, k_norm_w=dknw, sinks=dsinks[:, 0, 0])
    return loss, grad_x, big, small


def _adamw(w, g, m, v):
    m2 = ADAM_B1 * m + (1.0 - ADAM_B1) * g
    v2 = ADAM_B2 * v + (1.0 - ADAM_B2) * (g * g)
    m_hat = m2 / (1.0 - ADAM_B1 ** ADAM_STEP)
    v_hat = v2 / (1.0 - ADAM_B2 ** ADAM_STEP)
    delta = -ADAM_LR * (m_hat / (jnp.sqrt(v_hat) + ADAM_EPS) + ADAM_WD * w)
    return delta, m2, v2


def _reduce_adamw(recv, w, m, v, name):
    _, R, C = recv.shape
    tc = _tile(C, 256)

    def body(r_ref, w_ref, m_ref, v_ref, o_ref):
        g = r_ref[0].astype(F32)
        for s in range(1, N_DEV):
            g = g + r_ref[s].astype(F32)
        delta, m2, v2 = _adamw(w_ref[...], g, m_ref[...], v_ref[...])
        o_ref[0] = g
        o_ref[1] = delta
        o_ref[2] = m2
        o_ref[3] = v2

    col = pl.BlockSpec((R, tc), lambda j: (0, j))
    return pl.pallas_call(
        body, name=name, grid=(C // tc,),
        in_specs=[pl.BlockSpec((N_DEV, R, tc), lambda j: (0, 0, j)), col, col, col],
        out_specs=pl.BlockSpec((4, R, tc), lambda j: (0, 0, j)),
        out_shape=_sds((4, R, C)),
        compiler_params=_cparams(("parallel",)),
    )(recv, w, m, v)


def _adamw_call(g, w, m, v, name):
    def body(g_ref, w_ref, m_ref, v_ref, o_ref):
        delta, m2, v2 = _adamw(w_ref[...], g_ref[...], m_ref[...], v_ref[...])
        o_ref[0] = delta
        o_ref[1] = m2
        o_ref[2] = v2

    return pl.pallas_call(body, name=name, out_shape=_sds((3,) + g.shape))(g, w, m, v)


ADA_N = 6 * D // N_CHIP
KPAD = 128


def _w_ada_update(c8p, dm, w, m, v):
    tr = 256

    def body(c_ref, dm_ref, w_ref, m_ref, v_ref, g_ref, d_ref, m2_ref, v2_ref):
        g = _raw1(_silu(c_ref[...]), dm_ref[...], _TN)
        delta, m2, v2 = _adamw(w_ref[...], g, m_ref[...], v_ref[...])
        g_ref[...] = g
        d_ref[...] = delta
        m2_ref[...] = m2
        v2_ref[...] = v2

    blk = pl.BlockSpec((tr, ADA_N), lambda i: (i, 0))
    return pl.pallas_call(
        body, name="w_ada_update", grid=(D // tr,),
        in_specs=[pl.BlockSpec((KPAD, tr), lambda i: (0, i)), pl.BlockSpec((KPAD, ADA_N), lambda i: (0, 0)),
                  blk, blk, blk],
        out_specs=[blk] * 4, out_shape=[_sds((D, ADA_N))] * 4,
        compiler_params=_cparams(("parallel",)),
    )(c8p, dm, w, m, v)


def _me():
    return lax.axis_index("x"), lax.axis_index("y"), lax.axis_index("c")


def _peer(k, me):
    mx, my, mc = me
    return (1 - mx if k & 4 else mx, 1 - my if k & 2 else my, 1 - mc if k & 1 else mc)


def _lin(p):
    return 4 * p[0] + 2 * p[1] + p[2]


def _remote(src, dst, ssem, rsem, dev):
    return pltpu.make_async_remote_copy(src_ref=src, dst_ref=dst, send_sem=ssem, recv_sem=rsem,
                                        device_id=dev, device_id_type=MESH)


def _all_gather8(x, name):
    def body(x_ref, out_ref, send_sems, recv_sems):
        me = _me()
        out_ref[_lin(me)] = x_ref[...]
        sends = []
        for k in range(1, N_DEV):
            cp = _remote(x_ref, out_ref.at[_lin(me)], send_sems.at[k - 1], recv_sems.at[k - 1], _peer(k, me))
            cp.start()
            sends.append(cp)
        for k in range(1, N_DEV):
            p = _peer(k, me)
            _remote(x_ref, out_ref.at[_lin(p)], send_sems.at[k - 1], recv_sems.at[k - 1], p).wait_recv()
        for cp in sends:
            cp.wait_send()

    return pl.pallas_call(
        body, name=name,
        out_shape=_sds((N_DEV,) + x.shape, x.dtype),
        in_specs=[pl.BlockSpec(memory_space=pltpu.VMEM)],
        out_specs=pl.BlockSpec(memory_space=pltpu.VMEM),
        scratch_shapes=[pltpu.SemaphoreType.DMA((N_DEV - 1,)), pltpu.SemaphoreType.DMA((N_DEV - 1,))],
    )(x)


def _ag8_plan(src, out, send_sems, recv_sems):
    me = _me()
    sends, recvs = [], []
    for k in range(1, N_DEV):
        p = _peer(k, me)
        sends.append(_remote(src, out.at[_lin(me)], send_sems.at[k - 1], recv_sems.at[k - 1], p))
        recvs.append(_remote(src, out.at[_lin(p)], send_sems.at[k - 1], recv_sems.at[k - 1], p))
    return [], sends, recvs


def _prologue(c_row, conv_sh, w_ada, b_sh, w_in_sh):
    def body(c_ref, cv_ref, wa_ref, b_ref, win_ref, call_ref, cvall_ref, mods_ref, ain_ref, c16_scr, mp_scr,
             c_send, c_recv, cv_send, cv_recv, m_send, m_recv, w_send, w_recv, w_local):
        me = _lin(_me())
        w_plan = _gather_half_plan([win_ref], [ain_ref], w_send, w_recv, w_local)
        _start(w_plan)
        c_plan = _ag8_plan(c_ref, call_ref, c_send, c_recv)
        cv_plan = _ag8_plan(cv_ref, cvall_ref, cv_send, cv_recv)
        call_ref[me] = c_ref[...]
        cvall_ref[me] = cv_ref[...]
        _start(c_plan)
        _start(cv_plan)
        _finish(c_plan)
        c16_scr[...] = jnp.zeros_like(c16_scr)
        for d in range(N_DEV):
            c16_scr[pl.ds(d, 1), :] = call_ref[d]
        mp_scr[...] = _raw1(_silu(c16_scr[...]), wa_ref[...], _NN) + b_ref[...]
        mods_ref[me] = mp_scr[...]
        m_plan = _ag8_plan(mp_scr, mods_ref, m_send, m_recv)
        _start(m_plan)
        _finish(cv_plan)
        _finish(m_plan)
        _finish(w_plan)

    vmem = pl.BlockSpec(memory_space=pltpu.VMEM)
    sems = lambda n: pltpu.SemaphoreType.DMA((n,))
    return pl.pallas_call(
        body, name="prologue",
        in_specs=[vmem] * 4 + _hbm_specs(1), out_specs=[vmem] * 3 + _hbm_specs(1),
        out_shape=[_sds((N_DEV,) + c_row.shape), _sds((N_DEV,) + conv_sh.shape), _sds((N_DEV, 16, ADA_N)),
                   _sds((N_CHIP,) + w_in_sh.shape, w_in_sh.dtype)],
        scratch_shapes=[pltpu.VMEM((16, D), F32), pltpu.VMEM((16, ADA_N), F32)] + [sems(N_DEV - 1)] * 6
                       + _gather_sems(1),
        compiler_params=_cparams(),
    )(c_row, conv_sh, w_ada, b_sh, w_in_sh)


def _hbm_specs(n):
    return [pl.BlockSpec(memory_space=pl.ANY)] * n


def _gather_shapes(shards):
    return [_sds((N_CHIP,) + s.shape, s.dtype) for s in shards]


def _gather_sems(n):
    return [pltpu.SemaphoreType.DMA((3 * n,)), pltpu.SemaphoreType.DMA((3 * n,)), pltpu.SemaphoreType.DMA((n,))]


def _gather_plan(ins, outs, send_sems, recv_sems, local_sems):
    mx, my, mc = _me()
    chips = [(1 - mx, my), (mx, 1 - my), (1 - mx, 1 - my)]
    local, sends, recvs = [], [], []
    for a in range(len(ins)):
        local.append(pltpu.make_async_copy(ins[a], outs[a].at[2 * mx + my], local_sems.at[a]))
        for k, (px, py) in enumerate(chips):
            sems = (send_sems.at[3 * a + k], recv_sems.at[3 * a + k], (px, py, mc))
            sends.append(_remote(ins[a], outs[a].at[2 * mx + my], *sems))
            recvs.append(_remote(ins[a], outs[a].at[2 * px + py], *sems))
    return local, sends, recvs


def _gather_half_plan(ins, outs, send_sems, recv_sems, local_sems):
    mx, my, mc = _me()
    chips = [(1 - mx, my), (mx, 1 - my), (1 - mx, 1 - my)]
    local, sends, recvs = [], [], []
    for a in range(len(ins)):
        h = ins[a].shape[0] // 2
        mine = pl.ds(pl.multiple_of(mc * h, 16), h)
        local.append(pltpu.make_async_copy(ins[a], outs[a].at[2 * mx + my], local_sems.at[a]))
        for k, (px, py) in enumerate(chips):
            sems = (send_sems.at[3 * a + k], recv_sems.at[3 * a + k], (px, py, mc))
            sends.append(_remote(ins[a].at[mine], outs[a].at[2 * mx + my, mine], *sems))
            recvs.append(_remote(ins[a].at[mine], outs[a].at[2 * px + py, mine], *sems))
    return local, sends, recvs


def _sibling_fill(pieces):
    h = pieces.shape[1] // 2

    def body(p_ref, o_ref, send_sems, recv_sems):
        mx, my, mc = _me()
        sib = (mx, my, 1 - mc)
        chips = [(1 - mx, my), (mx, 1 - my), (1 - mx, 1 - my)]
        half = lambda c: pl.ds(pl.multiple_of(c * h, 16), h)
        o_ref[2 * mx + my] = p_ref[2 * mx + my]
        sends = []
        for k, (px, py) in enumerate(chips):
            j = 2 * px + py
            o_ref[j, half(mc), :] = p_ref[j, half(mc), :]
            cp = _remote(p_ref.at[j, half(mc)], o_ref.at[j, half(mc)], send_sems.at[k], recv_sems.at[k], sib)
            cp.start()
            sends.append(cp)
        for k, (px, py) in enumerate(chips):
            j = 2 * px + py
            _remote(p_ref.at[j, half(mc)], o_ref.at[j, half(1 - mc)], send_sems.at[k], recv_sems.at[k],
                    sib).wait_recv()
        for cp in sends:
            cp.wait_send()

    vmem = pl.BlockSpec(memory_space=pltpu.VMEM)
    return pl.pallas_call(
        body, name="sibling_fill", out_shape=_sds(pieces.shape, pieces.dtype),
        in_specs=[vmem], out_specs=vmem,
        scratch_shapes=[pltpu.SemaphoreType.DMA((N_CHIP - 1,)), pltpu.SemaphoreType.DMA((N_CHIP - 1,))],
        compiler_params=_cparams(),
    )(pieces)


def _start(plan):
    local, sends, _ = plan
    for cp in local + sends:
        cp.start()


def _finish(plan):
    local, sends, recvs = plan
    for cp in recvs:
        cp.wait_recv()
    for cp in sends:
        cp.wait_send()
    for cp in local:
        cp.wait()


def _exchange_shapes(pieces):
    return [_sds((N_DEV,) + p.shape[2:], p.dtype) for p in pieces]


def _exchange_sems(n):
    return [pltpu.SemaphoreType.DMA(((N_DEV - 1) * n,)), pltpu.SemaphoreType.DMA(((N_DEV - 1) * n,)),
            pltpu.SemaphoreType.DMA((n,))]


def _exchange_plan(ins, outs, send_sems, recv_sems, local_sems):
    me = _me()
    mx, my, mc = me
    local, sends, recvs = [], [], []
    for a in range(len(ins)):
        local.append(pltpu.make_async_copy(ins[a].at[2 * mx + my, mc], outs[a].at[_lin(me)], local_sems.at[a]))
        for k in range(1, N_DEV):
            p = _peer(k, me)
            s = (N_DEV - 1) * a + k - 1
            sends.append(_remote(ins[a].at[2 * p[0] + p[1], p[2]], outs[a].at[_lin(me)], send_sems.at[s],
                                 recv_sems.at[s], p))
            recvs.append(_remote(ins[a].at[2 * mx + my, mc], outs[a].at[_lin(p)], send_sems.at[s],
                                 recv_sems.at[s], p))
    return local, sends, recvs


REDUCE_VMEM = 56 * 1024 * 1024


def _reduce_swap(recvs):
    n = len(recvs)

    def body(*refs):
        r_refs, o_refs = refs[:n], refs[n:2 * n]
        send_sems, recv_sems = refs[2 * n:]
        mx, my, mc = _me()
        sib = (mx, my, 1 - mc)
        half = lambda a, c: o_refs[a].at[pl.ds(pl.multiple_of(c * recvs[a].shape[1], 8), recvs[a].shape[1])]
        sends = []
        for a in range(n):
            g = r_refs[a][0].astype(F32)
            for s in range(1, N_DEV):
                g = g + r_refs[a][s].astype(F32)
            half(a, mc)[...] = g
            cp = _remote(half(a, mc), half(a, mc), send_sems.at[a], recv_sems.at[a], sib)
            cp.start()
            sends.append(cp)
        for a in range(n):
            _remote(half(a, mc), half(a, 1 - mc), send_sems.at[a], recv_sems.at[a], sib).wait_recv()
        for cp in sends:
            cp.wait_send()

    vmem = pl.BlockSpec(memory_space=pltpu.VMEM)
    return pl.pallas_call(
        body, name="reduce_swap", out_shape=[_sds((2 * r.shape[1], r.shape[2])) for r in recvs],
        in_specs=[vmem] * n, out_specs=[vmem] * n,
        scratch_shapes=[pltpu.SemaphoreType.DMA((n,)), pltpu.SemaphoreType.DMA((n,))],
        compiler_params=_cparams(None, REDUCE_VMEM),
    )(*recvs)


def _adamw_big(g, w, m, v, name):
    rows, cols = g.shape
    tr = next((t for t in (256, 176, 128, 64, 8) if rows % t == 0), None)
    if tr is None:
        tc = _tile(cols, 256)
        blk, grid = pl.BlockSpec((rows, tc), lambda i: (0, i)), (cols // tc,)
    else:
        blk, grid = pl.BlockSpec((tr, cols), lambda i: (i, 0)), (rows // tr,)

    def body(g_ref, w_ref, m_ref, v_ref, go_ref, d_ref, m2_ref, v2_ref):
        g = g_ref[...]
        delta, m2, v2 = _adamw(w_ref[...], g, m_ref[...], v_ref[...])
        go_ref[...] = g
        d_ref[...] = delta
        m2_ref[...] = m2
        v2_ref[...] = v2

    return pl.pallas_call(
        body, name=name, grid=grid,
        in_specs=[blk] * 4, out_specs=[blk] * 4, out_shape=[_sds((rows, cols))] * 4,
        compiler_params=_cparams(("parallel",)),
    )(g, w, m, v)


SMALL_ORDER = (("mod", 6 * D), ("norm1_w", D), ("norm2_w", D), ("conv_w", CONVW * 3 * GW), ("a_log", GH),
               ("dt_bias", GH), ("gdn_norm_w", HD), ("q_norm_w", HD), ("k_norm_w", HD), ("sinks", SQH), ("loss", 1))
SMALL_R = 120


def _pack_small(d):
    parts = [d[k].reshape(-1).astype(F32) if k in d else jnp.zeros((n,), F32) for k, n in SMALL_ORDER]
    used = sum(n for _, n in SMALL_ORDER)
    parts.append(jnp.zeros((SMALL_R * LANE - used,), F32))
    return jnp.concatenate(parts).reshape(SMALL_R, LANE)


def _unpack_small(pk):
    flat = pk.reshape(-1)
    out, r = {}, 0
    for k, n in SMALL_ORDER:
        out[k] = flat[r:r + n]
        r += n
    return out


def kernel(x, c, w_ada, b_ada, norm1_w, w_in, conv_w, a_log, dt_bias, gdn_norm_w, q_norm_w, k_norm_w, sinks, w_out, norm2_w, w_gate, w_up, w_down, loss_target, m_w_ada, m_b_ada, m_norm1_w, m_w_in, m_conv_w, m_a_log, m_dt_bias, m_gdn_norm_w, m_q_norm_w, m_k_norm_w, m_sinks, m_w_out, m_norm2_w, m_w_gate, m_w_up, m_w_down, v_w_ada, v_b_ada, v_norm1_w, v_w_in, v_conv_w, v_a_log, v_dt_bias, v_gdn_norm_w, v_q_norm_w, v_k_norm_w, v_sinks, v_w_out, v_norm2_w, v_w_gate, v_w_up, v_w_down):
    mx, my, mc = _me()
    chip = 2 * mx + my
    dev = 4 * mx + 2 * my + mc
    T = x.shape[1]

    as_rows = lambda t, transposed: t[0].T if transposed else t[0]
    transposed = (True, False, True, True, False)
    big_w = [as_rows(t, tr) for t, tr in zip((w_in, w_out, w_gate, w_up, w_down), transposed)]
    shards = [t.astype(BF16) for t in big_w]

    b_sh = lax.dynamic_slice(b_ada, (0, chip * ADA_N), (1, ADA_N))
    w_in_sh = jnp.pad(shards[0], ((0, W_IN_ROWS_PAD - W_IN_ROWS), (0, 0)))
    c_all, conv_all, mods, a_in = _prologue(c, conv_w.reshape(CONVW, 3 * GW // N_CHIP), w_ada[0], b_sh, w_in_sh)
    c8 = c_all.reshape(N_DEV, D)
    conv_full = jnp.concatenate([conv_all[2 * j] for j in range(N_CHIP)], axis=1)
    mod = jnp.concatenate([lax.dynamic_slice(mods[2 * j], (dev, 0), (1, ADA_N)) for j in range(N_CHIP)], axis=1)
    w_in_pt = _permute_w_in_t(_sibling_fill(a_in)[:, :W_IN_ROWS].reshape(PROJ, D))

    loss, grad_x, big, small = _local_step(
        x[0], loss_target[0], mod, norm1_w, w_in_pt, conv_full, a_log, dt_bias, gdn_norm_w,
        q_norm_w, k_norm_w, sinks, norm2_w, shards[1:])

    small["loss"] = loss[:, :1]
    sg = _all_gather8(_pack_small(small), "gather_small_grads")
    rep = dict(mod=(b_ada, m_b_ada, v_b_ada), norm1_w=(norm1_w, m_norm1_w, v_norm1_w),
               norm2_w=(norm2_w, m_norm2_w, v_norm2_w), a_log=(a_log, m_a_log, v_a_log),
               dt_bias=(dt_bias, m_dt_bias, v_dt_bias), gdn_norm_w=(gdn_norm_w, m_gdn_norm_w, v_gdn_norm_w),
               q_norm_w=(q_norm_w, m_q_norm_w, v_q_norm_w), k_norm_w=(k_norm_w, m_k_norm_w, v_k_norm_w),
               sinks=(sinks, m_sinks, v_sinks))
    wmv = [_pack_small({k: t[i] for k, t in rep.items()}) for i in range(3)]
    sres = _reduce_adamw(sg, wmv[0], wmv[1], wmv[2], "small_reduce_adamw")
    s_g, s_d, s_m, s_v = [_unpack_small(sres[i]) for i in range(4)]
    loss_out = s_g["loss"][0]

    g_conv = lax.dynamic_slice(s_g["conv_w"].reshape(CONVW, 3 * GW), (0, chip * (3 * GW // N_CHIP)),
                               (CONVW, 3 * GW // N_CHIP))
    pad16 = lambda t: jnp.concatenate([t.reshape(12, LANE), jnp.zeros((4, LANE), F32)], axis=0)
    cres = _adamw_call(pad16(g_conv), pad16(conv_w), pad16(m_conv_w), pad16(v_conv_w), "conv_adamw")
    conv_out = [g_conv.reshape(conv_w.shape)] + [cres[i, :12].reshape(conv_w.shape) for i in range(3)]

    dmod8 = sg[:, :6 * D // LANE].reshape(N_DEV, 6 * D)
    dm = lax.dynamic_slice(dmod8, (0, chip * ADA_N), (N_DEV, ADA_N))
    zpad = lambda t: jnp.concatenate([t, jnp.zeros((KPAD - N_DEV, t.shape[1]), F32)], axis=0)
    ares = _w_ada_update(zpad(c8), zpad(dm), w_ada[0], m_w_ada[0], v_w_ada[0])

    names = ("w_in", "w_out", "w_gate", "w_up", "w_down")
    g_full = list(_reduce_swap(big))
    g_full[0] = g_full[0][:W_IN_ROWS]
    big_m = [as_rows(t, tr) for t, tr in zip((m_w_in, m_w_out, m_w_gate, m_w_up, m_w_down), transposed)]
    big_v = [as_rows(t, tr) for t, tr in zip((v_w_in, v_w_out, v_w_gate, v_w_up, v_w_down), transposed)]
    upd = [_adamw_big(g, w, m, v, "adamw_" + nm) for g, w, m, v, nm in zip(g_full, big_w, big_m, big_v, names)]
    back = lambda t, tr: (t.T if tr else t)[None]
    bg, bd, bm, bv = [[back(u[i], tr) for u, tr in zip(upd, transposed)] for i in range(4)]

    def group(a_i, small_d, conv_i, big_l):
        s = lambda k, ref: small_d[k].reshape(ref.shape)
        return [ares[a_i][None], s("mod", b_ada), s("norm1_w", norm1_w), big_l[0], conv_out[conv_i],
                s("a_log", a_log), s("dt_bias", dt_bias), s("gdn_norm_w", gdn_norm_w), s("q_norm_w", q_norm_w),
                s("k_norm_w", k_norm_w), s("sinks", sinks), big_l[1], s("norm2_w", norm2_w), big_l[2], big_l[3],
                big_l[4]]

    outs = [loss_out, grad_x[None]]
    outs += group(0, s_g, 0, bg) + group(1, s_d, 1, bd) + group(2, s_m, 2, bm) + group(3, s_v, 3, bv)
    return tuple(outs)
```

```python
import jax
import jax.numpy as jnp
from jax import lax
from jax.experimental import pallas as pl
from jax.experimental.pallas import tpu as pltpu

F32 = jnp.float32
BF16 = jnp.bfloat16
MESH = pl.DeviceIdType.MESH

D = 1024
HD = 64
GH = 8
GW = GH * HD
SQH = 8
SKVH = 2
SGRP = SQH // SKVH
WIN = 128
CONVW = 4
CHUNK = 64
DFF = 2816
PROJ = 2832
NP = 3072
EPS = 1e-6
N_DEV = 8
N_CHIP = 4

ADAM_LR = 0.001
ADAM_B1 = 0.9
ADAM_B2 = 0.999
ADAM_EPS = 1e-08
ADAM_WD = 0.01
ADAM_STEP = 10

VMEM_LIMIT = 48 * 1024 * 1024
GDN_BWD_VMEM = 58 * 1024 * 1024
LANE = 128


def _cparams(sem=None, vmem=VMEM_LIMIT):
    return pltpu.CompilerParams(dimension_semantics=sem, vmem_limit_bytes=vmem)


_NN = ((1,), (0,))
_NT = ((1,), (1,))
_TN = ((0,), (0,))


def _dot(a, b, dims):
    if a.ndim == 3:
        (ca,), (cb,) = dims
        return lax.dot_general(a, b, (((ca + 1,), (cb + 1,)), ((0,), (0,))), preferred_element_type=F32)
    return lax.dot_general(a, b, (dims, ((), ())), preferred_element_type=F32)


def _raw1(a, b, dims):
    return _dot(a.astype(BF16), b.astype(BF16), dims)


def _raw3(a, b, dims):
    ah = a.astype(BF16)
    al = (a - ah.astype(F32)).astype(BF16)
    bh = b.astype(BF16)
    bl = (b - bh.astype(F32)).astype(BF16)
    return _dot(ah, bh, dims) + (_dot(al, bh, dims) + _dot(ah, bl, dims))


def _make_diff_mm(raw):
    @jax.custom_vjp
    def nn(a, b):
        return raw(a, b, _NN)

    @jax.custom_vjp
    def nt(a, b):
        return raw(a, b, _NT)

    @jax.custom_vjp
    def tn(a, b):
        return raw(a, b, _TN)

    nn.defvjp(lambda a, b: (raw(a, b, _NN), (a, b)), lambda r, g: (nt(g, r[1]), tn(r[0], g)))
    nt.defvjp(lambda a, b: (raw(a, b, _NT), (a, b)), lambda r, g: (nn(g, r[1]), tn(g, r[0])))
    tn.defvjp(lambda a, b: (raw(a, b, _TN), (a, b)), lambda r, g: (nt(r[1], g), nn(r[0], g)))
    return nn, nt, tn


def _tri_inv_raw(a, nn3):
    n = a.shape[-1]
    ri = lax.broadcasted_iota(jnp.int32, (n, n), 0)
    ci = lax.broadcasted_iota(jnp.int32, (n, n), 1)
    t = (ri == ci).astype(F32)
    for lvl in range((n - 1).bit_length()):
        same_pair = (ri >> (lvl + 1)) == (ci >> (lvl + 1))
        lower_left = (((ri >> lvl) & 1) == 1) & (((ci >> lvl) & 1) == 0)
        y = jnp.where(same_pair & lower_left, a, 0.0)
        t = t - y if lvl == 0 else t - nn3(nn3(t, y), t)
    return t


class _Kit:
    def __init__(self, diff):
        if diff:
            self.nn, self.nt, self.tn = _make_diff_mm(_raw1)
            self.nn3, self.nt3, self.tn3 = _make_diff_mm(_raw3)
            nn3, nt3, tn3 = self.nn3, self.nt3, self.tn3

            @jax.custom_vjp
            def inv(a, t):
                return t

            def inv_fwd(a, t):
                return t, t

            def inv_bwd(t, g):
                return -tn3(t, nt3(g, t)), jnp.zeros_like(t)

            inv.defvjp(inv_fwd, inv_bwd)
            self.inv = inv
        else:
            self.nn = lambda a, b: _raw1(a, b, _NN)
            self.nt = lambda a, b: _raw1(a, b, _NT)
            self.tn = lambda a, b: _raw1(a, b, _TN)
            self.nn3 = lambda a, b: _raw3(a, b, _NN)
            self.nt3 = lambda a, b: _raw3(a, b, _NT)
            self.tn3 = lambda a, b: _raw3(a, b, _TN)
            self.inv = lambda a, t: _tri_inv_raw(a, self.nn3) if t is None else t


def _sigmoid(x):
    return 1.0 / (1.0 + jnp.exp(-x))


def _silu(x):
    return x * _sigmoid(x)


def _rms(x, w):
    return x * lax.rsqrt(jnp.mean(x * x, axis=-1, keepdims=True) + EPS) * w


def _tile(dim, target):
    t = (min(dim, target) // LANE) * LANE
    while t >= LANE:
        if dim % t == 0:
            return t
        t -= LANE
    return dim


MM_TM, MM_TN, MM_TK = 1408, 1536, 1408


def _matmul(a, b, ta=False, tb=False, out_dtype=F32, name="matmul", gather=None, exchange=None):
    carried = gather if gather is not None else exchange if exchange is not None else []
    nc = len(carried)
    if ta:
        K, M = a.shape
    else:
        M, K = a.shape
    if tb:
        N, K2 = b.shape
    else:
        K2, N = b.shape
    assert K == K2, (a.shape, b.shape, ta, tb)
    tm, tn, tk = _tile(M, MM_TM), _tile(N, MM_TN), _tile(K, MM_TK)
    nk = K // tk
    dims = ((0,) if ta else (1,), (1,) if tb else (0,))

    grid = (M // tm, N // tn, nk)

    def body(*refs):
        a_ref, b_ref = refs[:2]
        o_ref = refs[2 + nc]
        scratch = refs[3 + 2 * nc:]
        k = pl.program_id(2)
        if nc:
            make_plan = _gather_plan if gather is not None else _exchange_plan
            plan = make_plan(refs[2:2 + nc], refs[3 + nc:3 + 2 * nc], *scratch[-3:])
            at = lambda pos: ((pl.program_id(0) == pos[0]) & (pl.program_id(1) == pos[1]) & (k == pos[2]))

            @pl.when(at((0, 0, 0)))
            def _():
                _start(plan)

        part = _dot(a_ref[...].astype(BF16), b_ref[...].astype(BF16), dims)
        if nk == 1:
            o_ref[...] = part.astype(o_ref.dtype)
        else:
            acc_ref = scratch[0]

            @pl.when(k == 0)
            def _():
                acc_ref[...] = part

            @pl.when((k > 0) & (k < nk - 1))
            def _():
                acc_ref[...] += part

            @pl.when(k == nk - 1)
            def _():
                o_ref[...] = (acc_ref[...] + part).astype(o_ref.dtype)

        if nc:
            @pl.when(at((grid[0] - 1, grid[1] - 1, nk - 1)))
            def _():
                _finish(plan)

    a_spec = (pl.BlockSpec((tk, tm), lambda i, j, k: (k, i)) if ta
              else pl.BlockSpec((tm, tk), lambda i, j, k: (i, k)))
    b_spec = (pl.BlockSpec((tn, tk), lambda i, j, k: (j, k)) if tb
              else pl.BlockSpec((tk, tn), lambda i, j, k: (k, j)))
    if gather is not None:
        c_shapes, c_sems = _gather_shapes(carried), _gather_sems(nc)
    elif exchange is not None:
        c_shapes, c_sems = _exchange_shapes(carried), _exchange_sems(nc)
    else:
        c_shapes, c_sems = [], []
    res = pl.pallas_call(
        body, name=name, grid=grid,
        in_specs=[a_spec, b_spec] + _hbm_specs(nc),
        out_specs=[pl.BlockSpec((tm, tn), lambda i, j, k: (i, j))] + _hbm_specs(nc),
        out_shape=[jax.ShapeDtypeStruct((M, N), out_dtype)] + c_shapes,
        scratch_shapes=([pltpu.VMEM((tm, tn), F32)] if nk > 1 else []) + c_sems,
        compiler_params=_cparams(("arbitrary",) * 3 if nc else ("parallel", "parallel", "arbitrary")),
    )(a, b, *carried)
    return (res[0], res[1:]) if nc else res[0]


def _sds(shape, dtype=F32):
    return jax.ShapeDtypeStruct(shape, dtype)


def _norm_mod(x, nw, scale, shift):
    return _rms(x, nw) * (1.0 + scale) + shift


def _norm_in_proj(x, nw, scale, shift, w_in_pt, shards):
    T = x.shape[0]
    N = w_in_pt.shape[0]
    tm, tn = _tile(T, 1024), _tile(N, MM_TN)
    nm, nn = T // tm, N // tn
    ns = len(shards)

    def body(*refs):
        x_ref, nw_ref, sc_ref, sh_ref, w_ref = refs[:5]
        h_ref, o_ref = refs[5 + ns:7 + ns]
        plan = _gather_plan(refs[5:5 + ns], refs[7 + ns:7 + 2 * ns], *refs[7 + 2 * ns:])
        i, j = pl.program_id(0), pl.program_id(1)

        @pl.when((i == 0) & (j == 0))
        def _():
            _start(plan)

        @pl.when(j == 0)
        def _():
            for r0 in range(0, tm, ROWS_EPI):
                rows = pl.ds(r0, ROWS_EPI)
                h_ref[rows, :] = _norm_mod(x_ref[rows, :], nw_ref[...], sc_ref[...], sh_ref[...]).astype(BF16)

        o_ref[...] = _dot(h_ref[...], w_ref[...], _NT)

        @pl.when((i == nm - 1) & (j == nn - 1))
        def _():
            _finish(plan)

    vec = pl.BlockSpec((1, D), lambda i, j: (0, 0))
    res = pl.pallas_call(
        body, name="norm1_in_proj", grid=(nm, nn),
        in_specs=[pl.BlockSpec((tm, D), lambda i, j: (i, 0)), vec, vec, vec,
                  pl.BlockSpec((tn, D), lambda i, j: (j, 0))] + _hbm_specs(ns),
        out_specs=[pl.BlockSpec((tm, D), lambda i, j: (i, 0)), pl.BlockSpec((tm, tn), lambda i, j: (i, j))]
                  + _hbm_specs(ns),
        out_shape=[_sds((T, D), BF16), _sds((T, N))] + _gather_shapes(shards),
        scratch_shapes=_gather_sems(ns),
        compiler_params=_cparams(("arbitrary", "arbitrary")),
    )(x, nw, scale, shift, w_in_pt, *shards)
    return res[0], res[1], res[2:]


ROWS_TM = 512
ROWS_EPI = 256


def _matmul_rows(a, b, epi, tiled, consts, out_tiled, out_acc, name, pieces=()):
    T, K = a.shape
    tm, tk = _tile(T, ROWS_TM), _tile(K, MM_TK)
    nm, nk = T // tm, K // tk
    npc, nt, ncst, no, na = len(pieces), len(tiled), len(consts), len(out_tiled), len(out_acc)
    n_in = 2 + nt + ncst

    def body(*refs):
        a_ref, b_ref = refs[:2]
        t_refs, c_refs = refs[2:2 + nt], refs[2 + nt:n_in]
        o_refs = refs[n_in + npc:n_in + npc + no]
        acc_refs = refs[n_in + npc + no:n_in + npc + no + na]
        n_out = no + na + npc
        res_ref = refs[n_in + npc + n_out]
        plan = _exchange_plan(refs[n_in:n_in + npc], refs[n_in + npc + no + na:n_in + npc + n_out],
                              *refs[n_in + npc + n_out + 1:]) if npc else None
        i, k = pl.program_id(0), pl.program_id(1)

        @pl.when((i == 0) & (k == 0))
        def _():
            for r in acc_refs:
                r[...] = jnp.zeros_like(r)
            if npc:
                _start(plan)

        part = _dot(a_ref[...], b_ref[...], _NN)

        @pl.when(k == 0)
        def _():
            res_ref[...] = part

        @pl.when(k > 0)
        def _():
            res_ref[...] += part

        @pl.when(k == nk - 1)
        def _():
            for r0 in range(0, tm, ROWS_EPI):
                rows = pl.ds(r0, ROWS_EPI)
                outs = epi(res_ref[rows, :], *[r[rows, :] for r in t_refs], *[r[...] for r in c_refs])
                for r, v in zip(o_refs, outs[:no]):
                    r[rows, :] = v.astype(r.dtype)
                for r, v in zip(acc_refs, outs[no:]):
                    r[...] += v

        if npc:
            @pl.when((i == nm - 1) & (k == nk - 1))
            def _():
                _finish(plan)

    row = lambda w: pl.BlockSpec((tm, w), lambda i, k: (i, 0))
    whole = lambda s: pl.BlockSpec(s.shape, lambda i, k: (0, 0))
    res = pl.pallas_call(
        body, name=name, grid=(nm, nk),
        in_specs=[pl.BlockSpec((tm, tk), lambda i, k: (i, k)), pl.BlockSpec((tk, D), lambda i, k: (k, 0))]
                 + [row(t.shape[1]) for t in tiled] + [whole(c) for c in consts] + _hbm_specs(npc),
        out_specs=[row(s.shape[1]) for s in out_tiled] + [whole(s) for s in out_acc] + _hbm_specs(npc),
        out_shape=list(out_tiled) + list(out_acc) + (_exchange_shapes(pieces) if npc else []),
        scratch_shapes=[pltpu.VMEM((tm, D), F32)] + (_exchange_sems(npc) if npc else []),
        compiler_params=_cparams(("arbitrary", "arbitrary")),
    )(a, b, *tiled, *consts, *pieces)
    return res[:no + na], res[no + na:]


def _in_proj_dx_norm_bwd(dproj, w_in_pt, x, dres, nw, scale, shift, pieces):
    T = x.shape[0]

    def epi(dh, x, dres, nw, scale, shift):
        _, vjp = jax.vjp(_norm_mod, x, nw, scale, shift)
        dx, dnw, dsc, dsh = vjp(dh)
        return dx + dres, dnw, dsc, dsh

    return _matmul_rows(dproj, w_in_pt, epi, [x, dres], [nw, scale, shift], [_sds((T, D))], [_sds((1, D))] * 3,
                        "in_proj_dx_norm1_bwd", pieces)


def _out_proj_resid_norm(mixcat, w_out, x, gate1, nw, scale, shift):
    T = x.shape[0]

    def epi(mixed, x, gate1, nw, scale, shift):
        return (mixed,) + _resid_norm(x, mixed, gate1, nw, scale, shift)

    outs, _ = _matmul_rows(mixcat, w_out, epi, [x], [gate1, nw, scale, shift],
                           [_sds((T, D)), _sds((T, D)), _sds((T, D), BF16)], [], "out_proj_resid_norm2")
    return outs


def _ffn_up_dx_resid_bwd(dab, w_gut, x, mixed, dy, gate1, nw, scale, shift):
    T = x.shape[0]

    def epi(dh2, x, mixed, dy, gate1, nw, scale, shift):
        _, vjp = jax.vjp(_resid_norm, x, mixed, gate1, nw, scale, shift)
        return vjp((dy, dh2))

    outs, _ = _matmul_rows(dab, w_gut, epi, [x, mixed, dy], [gate1, nw, scale, shift],
                           [_sds((T, D)), _sds((T, D), BF16)], [_sds((1, D))] * 4, "ffn_up_dx_resid_norm2_bwd")
    return outs


def _ffn_down_loss(act, w_down, x1, target, gate2):
    T = x1.shape[0]

    def epi(ffn, x1, target, gate2):
        y = x1 + gate2 * ffn
        err = y - target
        loss = 0.5 * jnp.sum(jnp.sum(err * err, axis=1, keepdims=True), axis=0, keepdims=True) / D
        dy = err * (1.0 / D)
        return dy, gate2 * dy, jnp.sum(dy * ffn, axis=0, keepdims=True), jnp.broadcast_to(loss, (1, LANE))

    outs, _ = _matmul_rows(act, w_down, epi, [x1, target], [gate2], [_sds((T, D)), _sds((T, D), BF16)],
                           [_sds((1, D)), _sds((1, LANE))], "ffn_down_loss")
    return outs


def _resid_norm(x, mixed, gate1, nw, scale, shift):
    x1 = x + gate1 * mixed
    return x1, _norm_mod(x1, nw, scale, shift)


FFN_BLK = 256
FFN_TM = 2048


def _interleave_gate_up(gate_t, up_t):
    blocks = lambda t: t.reshape(DFF // FFN_BLK, 1, FFN_BLK, D)
    return jnp.concatenate([blocks(gate_t), blocks(up_t)], axis=1).reshape(2 * DFF, D)


def _split_gate_up(g):
    g = g.reshape(DFF // FFN_BLK, 2, FFN_BLK, D)
    return g[:, 0].reshape(DFF, D), g[:, 1].reshape(DFF, D)


def _ffn_up_act(h2, w_gut):
    T = h2.shape[0]
    tm = _tile(T, FFN_TM)

    def body(h_ref, w_ref, ab_ref, act_ref):
        ab = _dot(h_ref[...], w_ref[...], _NT)
        ab_ref[...] = ab
        act_ref[...] = (_silu(ab[:, :FFN_BLK]) * ab[:, FFN_BLK:]).astype(act_ref.dtype)

    return pl.pallas_call(
        body, name="ffn_up_act", grid=(T // tm, DFF // FFN_BLK),
        in_specs=[pl.BlockSpec((tm, D), lambda i, j: (i, 0)), pl.BlockSpec((2 * FFN_BLK, D), lambda i, j: (j, 0))],
        out_specs=[pl.BlockSpec((tm, 2 * FFN_BLK), lambda i, j: (i, j)), pl.BlockSpec((tm, FFN_BLK), lambda i, j: (i, j))],
        out_shape=[_sds((T, 2 * DFF)), _sds((T, DFF), BF16)],
        compiler_params=_cparams(("parallel", "parallel")),
    )(h2, w_gut)


def _ffn_down_dx_act(dffn, w_down, ab):
    T = dffn.shape[0]
    tm = _tile(T, FFN_TM)

    def body(d_ref, w_ref, ab_ref, o_ref):
        dact = _dot(d_ref[...], w_ref[...], _NT)
        a, b = ab_ref[:, :FFN_BLK], ab_ref[:, FFN_BLK:]
        s = _sigmoid(a)
        da = dact * b * (s * (1.0 + a * (1.0 - s)))
        db = dact * (a * s)
        o_ref[...] = jnp.concatenate([da, db], axis=1).astype(o_ref.dtype)

    return pl.pallas_call(
        body, name="ffn_down_dx_act", grid=(T // tm, DFF // FFN_BLK),
        in_specs=[pl.BlockSpec((tm, D), lambda i, j: (i, 0)), pl.BlockSpec((FFN_BLK, D), lambda i, j: (j, 0)),
                  pl.BlockSpec((tm, 2 * FFN_BLK), lambda i, j: (i, j))],
        out_specs=pl.BlockSpec((tm, 2 * FFN_BLK), lambda i, j: (i, j)),
        out_shape=_sds((T, 2 * DFF), BF16),
        compiler_params=_cparams(("parallel", "parallel")),
    )(dffn, w_down, ab)


def _round_bf16(x):
    return x.astype(BF16).astype(F32)


def _shift_down(x, s, rows):
    if s == 0:
        return x
    return jnp.where(rows >= s, pltpu.roll(x, s, 0), 0.0)


def _shift_up(x, s, rows, T):
    if s == 0:
        return x
    return jnp.where(rows < T - s, pltpu.roll(x, T - s, 0), 0.0)


def _conv_fwd(proj, conv_w):
    T = proj.shape[0]
    ncol = 3 * GW // LANE

    def body(x_ref, w_ref, o_ref):
        x = _round_bf16(x_ref[...])
        rows = lax.broadcasted_iota(jnp.int32, x.shape, 0)
        acc = jnp.zeros_like(x)
        for j in range(CONVW):
            acc = acc + _round_bf16(w_ref[pl.ds(j, 1), :]) * _shift_down(x, CONVW - 1 - j, rows)
        o_ref[0], o_ref[1] = _split_pair(_silu(acc))

    return pl.pallas_call(
        body, name="conv_fwd", grid=(ncol,),
        in_specs=[pl.BlockSpec((T, LANE), lambda j: (0, j)), pl.BlockSpec((CONVW, LANE), lambda j: (0, j))],
        out_specs=pl.BlockSpec((2, T, HD), lambda j: (j, 0, 0)),
        out_shape=_sds((3 * GH, T, HD)),
        compiler_params=_cparams(("parallel",)),
    )(proj, conv_w)


RELAYOUT_TM = 4096


def _split_pair(y):
    return y[:, :HD], pltpu.roll(y, HD, 1)[:, :HD]


def _merge_pair(a, b):
    return jnp.concatenate([a, b], axis=1)


def _split_heads(x, col_block0, nheads, name):
    T = x.shape[0]
    tm = _tile(T, RELAYOUT_TM)

    def body(x_ref, o_ref):
        a, b = _split_pair(x_ref[...])
        o_ref[0] = a
        o_ref[1] = b

    return pl.pallas_call(
        body, name=name, grid=(nheads // 2, T // tm),
        in_specs=[pl.BlockSpec((tm, LANE), lambda j, i: (i, col_block0 + j))],
        out_specs=pl.BlockSpec((2, tm, HD), lambda j, i: (j, i, 0)),
        out_shape=_sds((nheads, T, HD), x.dtype),
        compiler_params=_cparams(("parallel", "parallel")),
    )(x)


def _merge_heads(hm, out_dtype, name, into=None, col_block0=0, head0=0, nheads=None):
    T = hm.shape[1]
    nheads = hm.shape[0] if nheads is None else nheads
    tm = _tile(T, RELAYOUT_TM)

    def body(*refs):
        h_ref, o_ref = refs[0], refs[-1]
        o_ref[...] = _merge_pair(h_ref[0], h_ref[1]).astype(o_ref.dtype)

    in_specs = [pl.BlockSpec((2, tm, HD), lambda j, i: (head0 // 2 + j, i, 0))]
    args = [hm]
    if into is None:
        out_shape = _sds((T, HD * nheads), out_dtype)
        aliases = {}
    else:
        out_shape = _sds(into.shape, into.dtype)
        in_specs.append(pl.BlockSpec(memory_space=pl.ANY))
        args.append(into)
        aliases = {1: 0}
    return pl.pallas_call(
        body, name=name, grid=(nheads // 2, T // tm),
        in_specs=in_specs,
        out_specs=pl.BlockSpec((tm, LANE), lambda j, i: (i, col_block0 + j)),
        out_shape=out_shape, input_output_aliases=aliases,
        compiler_params=_cparams(("parallel", "parallel")),
    )(*args)


def _conv_bwd(proj, conv_w, dqc):
    T = proj.shape[0]
    ncol = 3 * GW // LANE

    def body(x_ref, w_ref, d_ref, dx_ref, dw_ref):
        x = _round_bf16(x_ref[...])
        rows = lax.broadcasted_iota(jnp.int32, x.shape, 0)
        xs = [_shift_down(x, CONVW - 1 - j, rows) for j in range(CONVW)]
        w = [_round_bf16(w_ref[pl.ds(j, 1), :]) for j in range(CONVW)]
        pre = jnp.zeros_like(x)
        for j in range(CONVW):
            pre = pre + w[j] * xs[j]
        s = _sigmoid(pre)
        dpre = _round_bf16(_merge_pair(d_ref[0], d_ref[1]) * (s * (1.0 + pre * (1.0 - s))))
        dx = jnp.zeros_like(x)
        for j in range(CONVW):
            dx = dx + w[j] * _shift_up(dpre, CONVW - 1 - j, rows, T)
            dw_ref[pl.ds(j, 1), :] = jnp.sum(dpre * xs[j], axis=0, keepdims=True)
        dx_ref[...] = dx.astype(dx_ref.dtype)

    return pl.pallas_call(
        body, name="conv_bwd", grid=(ncol,),
        in_specs=[pl.BlockSpec((T, LANE), lambda j: (0, j)), pl.BlockSpec((CONVW, LANE), lambda j: (0, j)),
                  pl.BlockSpec((2, T, HD), lambda j: (j, 0, 0))],
        out_specs=[pl.BlockSpec((T, LANE), lambda j: (0, j)), pl.BlockSpec((CONVW, LANE), lambda j: (0, j))],
        out_shape=[_sds((T, NP), BF16), _sds((CONVW, 3 * GW))],
        compiler_params=_cparams(("parallel",)),
    )(proj, conv_w, dqc)


def _gdn_prep(kit, q, k, v, ga, gb, alog, dtb, t_inv=None):
    C = CHUNK
    ri = lax.broadcasted_iota(jnp.int32, (C, C), 0)
    ci = lax.broadcasted_iota(jnp.int32, (C, C), 1)
    causal = ri >= ci
    strict = ri > ci
    eye = (ri == ci).astype(F32)
    lower = causal.astype(F32)
    upper = (ri <= ci).astype(F32)

    a = ga + dtb
    softplus = jnp.maximum(a, 0.0) + jnp.log(1.0 + jnp.exp(-jnp.abs(a)))
    g_row = -jnp.exp(alog) * softplus
    beta_row = _sigmoid(gb)
    g_col = jnp.sum(eye * g_row, axis=2, keepdims=True)
    beta_col = jnp.sum(eye * beta_row, axis=2, keepdims=True)
    G_col = jnp.sum(lower * g_row, axis=2, keepdims=True)
    G_row = jnp.sum(upper * g_col, axis=1, keepdims=True)
    G_last = jnp.sum(g_row, axis=2, keepdims=True)
    decay = jnp.exp(jnp.where(causal, G_col - G_row, -1e30))

    qn = q * lax.rsqrt(jnp.sum(q * q, axis=-1, keepdims=True) + EPS) * (HD ** -0.5)
    kn = k * lax.rsqrt(jnp.sum(k * k, axis=-1, keepdims=True) + EPS)
    kb = kn * beta_col
    A = jnp.where(strict, kit.nt(kb, kn) * decay, 0.0)
    Tm = kit.inv(A, t_inv)
    eG = jnp.exp(G_col)
    u = kit.nn3(Tm, v * beta_col)
    w = kit.nn3(Tm, kb * eG)
    qk = jnp.where(causal, kit.nt(qn, kn) * decay, 0.0)
    q_dec = qn * eG
    k_dec = kn * jnp.exp(G_last - G_col)
    dec = jnp.exp(G_last)
    return u, w, qk, q_dec, k_dec, dec, Tm


def _gdn_out(o, z, nw):
    return _rms(o, nw) * _silu(z)


GDN_CB = 4


def _gdn_specs(T, blk):
    TB = GDN_CB * CHUNK
    seq = lambda grp: pl.BlockSpec((GH, TB, HD), lambda i, grp=grp: (grp, blk(i), 0))
    row = lambda grp: pl.BlockSpec((GH, GDN_CB, 1, CHUNK), lambda i, grp=grp: (grp, blk(i), 0, 0))
    per_head = pl.BlockSpec((GH, 1, CHUNK), lambda i: (0, 0, 0))
    whole = pl.BlockSpec((1, HD), lambda i: (0, 0))
    state = pl.BlockSpec((GH, GDN_CB, HD, HD), lambda i: (0, blk(i), 0, 0))
    return seq, row, per_head, whole, state


def _gdn_load(seq_refs, row_refs, head_refs):
    chunks = lambda r: jnp.concatenate([r[:, pl.ds(cb * CHUNK, CHUNK), :] for cb in range(GDN_CB)], axis=0)
    rows = lambda r: jnp.concatenate([r[:, cb] for cb in range(GDN_CB)], axis=0)
    heads = lambda r: jnp.concatenate([r[...]] * GDN_CB, axis=0)
    return [chunks(r) for r in seq_refs], [rows(r) for r in row_refs], [heads(r) for r in head_refs]


def _gdn_fwd(qkv_hm, zs_hm, gab, alog_b, dtb_b, nw, shards):
    T = qkv_hm.shape[1]
    N = T // CHUNK
    nblk = N // GDN_CB
    ns = len(shards)
    seq, row, per_head, whole, state = _gdn_specs(T, lambda i: i)
    kit = _Kit(False)

    def body(*refs):
        q_ref, k_ref, v_ref, z_ref, ga_ref, gb_ref, al_ref, dt_ref, nw_ref = refs[:9]
        o_ref, S_ref, T_ref = refs[9 + ns:12 + ns]
        S_scr = refs[12 + 2 * ns]
        plan = _gather_plan(refs[9:9 + ns], refs[12 + ns:12 + 2 * ns], *refs[13 + 2 * ns:])

        @pl.when(pl.program_id(0) == 0)
        def _():
            S_scr[...] = jnp.zeros_like(S_scr)
            _start(plan)

        (q, k, v, z), (ga, gb), (al, dt) = _gdn_load((q_ref, k_ref, v_ref, z_ref), (ga_ref, gb_ref), (al_ref, dt_ref))
        u, w, qk, q_dec, k_dec, dec, t_inv = _gdn_prep(kit, q, k, v, ga, gb, al, dt)
        S = S_scr[...]
        for cb in range(GDN_CB):
            hs = slice(cb * GH, (cb + 1) * GH)
            S_ref[:, cb] = S
            T_ref[:, cb] = t_inv[hs]
            v_new = u[hs] - kit.nn(w[hs], S)
            o = kit.nn(q_dec[hs], S) + kit.nn(qk[hs], v_new)
            S = S * dec[hs] + kit.tn(k_dec[hs], v_new)
            o_ref[:, pl.ds(cb * CHUNK, CHUNK), :] = _gdn_out(o, z[hs], nw_ref[...])
        S_scr[...] = S

        @pl.when(pl.program_id(0) == nblk - 1)
        def _():
            _finish(plan)

    res = pl.pallas_call(
        body, name="gdn_fwd", grid=(nblk,),
        in_specs=[seq(0), seq(1), seq(2), seq(0), row(0), row(1), per_head, per_head, whole] + _hbm_specs(ns),
        out_specs=[seq(0), state, state] + _hbm_specs(ns),
        out_shape=[_sds((GH + SQH, T, HD)), _sds((GH, N, HD, HD)), _sds((GH, N, CHUNK, CHUNK))]
                  + _gather_shapes(shards),
        scratch_shapes=[pltpu.VMEM((GH, HD, HD), F32)] + _gather_sems(ns),
        compiler_params=_cparams(("arbitrary",)),
    )(qkv_hm, qkv_hm, qkv_hm, zs_hm, gab, gab, alog_b, dtb_b, nw, *shards)
    return res[0], (res[1], res[2]), res[3:]


def _gdn_bwd(qkv_hm, zs_hm, gab, alog_b, dtb_b, nw, S_all, do, pieces):
    T = qkv_hm.shape[1]
    N = T // CHUNK
    nblk = N // GDN_CB
    npc = len(pieces)
    dkit, kit = _Kit(True), _Kit(False)
    rseq, rrow, per_head, whole, rstate = _gdn_specs(T, lambda i: nblk - 1 - i)

    def body(*refs):
        q_ref, k_ref, v_ref, z_ref, ga_ref, gb_ref, al_ref, dt_ref, nw_ref, S_ref, T_ref, do_ref = refs[:12]
        dqkv_ref, dz_ref, dga_ref, dgb_ref, dal_ref, ddt_ref, dnw_ref = refs[12 + npc:19 + npc]
        dS_scr = refs[19 + 2 * npc]
        plan = _exchange_plan(refs[12:12 + npc], refs[19 + npc:19 + 2 * npc], *refs[20 + 2 * npc:])

        @pl.when(pl.program_id(0) == 0)
        def _():
            dS_scr[...] = jnp.zeros_like(dS_scr)
            dal_ref[...] = jnp.zeros_like(dal_ref)
            ddt_ref[...] = jnp.zeros_like(ddt_ref)
            dnw_ref[...] = jnp.zeros_like(dnw_ref)
            _start(plan)

        (q, k, v, z, dout), (ga, gb), (al, dt) = _gdn_load((q_ref, k_ref, v_ref, z_ref, do_ref), (ga_ref, gb_ref),
                                                          (al_ref, dt_ref))
        S_in = jnp.concatenate([S_ref[:, cb] for cb in range(GDN_CB)], axis=0)
        t_inv = jnp.concatenate([T_ref[:, cb] for cb in range(GDN_CB)], axis=0)
        prep = lambda *a: _gdn_prep(dkit, *a, t_inv=t_inv)[:6]
        (u, w, qk, q_dec, k_dec, dec), prep_vjp = jax.vjp(prep, q, k, v, ga, gb, al, dt)
        v_new = u - kit.nn(w, S_in)
        o = kit.nn(q_dec, S_in) + kit.nn(qk, v_new)
        _, out_vjp = jax.vjp(_gdn_out, o, z, nw_ref[...])
        do, dz, dnw = out_vjp(dout)
        dvn_part = kit.tn(qk, do)
        dS_part = kit.tn(q_dec, do)
        dS = dS_scr[...]
        dS_out, dvn = [None] * GDN_CB, [None] * GDN_CB
        for cb in reversed(range(GDN_CB)):
            hs = slice(cb * GH, (cb + 1) * GH)
            dS_out[cb] = dS
            dvn[cb] = dvn_part[hs] + kit.nn(k_dec[hs], dS)
            dS = dS * dec[hs] + dS_part[hs] - kit.tn(w[hs], dvn[cb])
        dS_scr[...] = dS
        dS_out = jnp.concatenate(dS_out, axis=0)
        dvn = jnp.concatenate(dvn, axis=0)
        ddec = jnp.sum(jnp.sum(S_in * dS_out, axis=2, keepdims=True), axis=1, keepdims=True)
        cts = (dvn, -kit.nt(dvn, S_in), kit.nt(do, v_new), kit.nt(do, S_in), kit.nt(v_new, dS_out), ddec)
        dq, dk, dv, dga, dgb, dal, ddt = prep_vjp(cts)
        lanesum = lambda t: jnp.broadcast_to(jnp.sum(t, axis=2, keepdims=True), t.shape)
        for cb in range(GDN_CB):
            hs = slice(cb * GH, (cb + 1) * GH)
            sl = pl.ds(cb * CHUNK, CHUNK)
            dqkv_ref[pl.ds(0, GH), sl, :] = dq[hs]
            dqkv_ref[pl.ds(GH, GH), sl, :] = dk[hs]
            dqkv_ref[pl.ds(2 * GH, GH), sl, :] = dv[hs]
            dz_ref[:, sl, :] = dz[hs]
            dga_ref[:, cb] = dga[hs]
            dgb_ref[:, cb] = dgb[hs]
            dal_ref[...] += lanesum(dal[hs])
            ddt_ref[...] += lanesum(ddt[hs])
        dnw_ref[...] += dnw

        @pl.when(pl.program_id(0) == nblk - 1)
        def _():
            _finish(plan)

    res = pl.pallas_call(
        body, name="gdn_bwd", grid=(nblk,),
        in_specs=[rseq(0), rseq(1), rseq(2), rseq(0), rrow(0), rrow(1), per_head, per_head, whole, rstate, rstate,
                  rseq(0)] + _hbm_specs(npc),
        out_specs=[pl.BlockSpec((3 * GH, GDN_CB * CHUNK, HD), lambda i: (0, nblk - 1 - i, 0)), rseq(0), rrow(0),
                   rrow(0), per_head, per_head, whole] + _hbm_specs(npc),
        out_shape=[_sds((3 * GH, T, HD)), _sds((GH + 4 + SWA_GRAD_HEADS, T, HD))] + [_sds((GH, N, 1, CHUNK))] * 2
                  + [_sds((GH, 1, CHUNK))] * 2 + [_sds((1, HD))] + _exchange_shapes(pieces),
        scratch_shapes=[pltpu.VMEM((GH, HD, HD), F32)] + _exchange_sems(npc),
        compiler_params=_cparams(("arbitrary",), GDN_BWD_VMEM),
    )(qkv_hm, qkv_hm, qkv_hm, zs_hm, gab, gab, alog_b, dtb_b, nw, S_all[0], S_all[1], do, *pieces)
    return res[:7], res[7:]


def _swa_heads(kit, first, q, kp, kc, vp, vc, qnw, knw, sink, slope):
    W = WIN
    ri = lax.broadcasted_iota(jnp.int32, (W, W), 0)
    ci = lax.broadcasted_iota(jnp.int32, (W, W), 1)
    mask_c = ri >= ci
    mask_p = ci > ri + first * W
    dist_c = (ri - ci).astype(F32)
    dist_p = (ri - ci + W).astype(F32)
    kpn = _rms(kp, knw)
    kcn = _rms(kc, knw)
    qn = _rms(q, qnw)
    sc = jnp.where(mask_c, kit.nt(qn, kcn) * (HD ** -0.5) - slope * dist_c, -1e30)
    sp = jnp.where(mask_p, kit.nt(qn, kpn) * (HD ** -0.5) - slope * dist_p, -1e30)
    m = jnp.maximum(jnp.maximum(jnp.max(sc, axis=-1, keepdims=True), jnp.max(sp, axis=-1, keepdims=True)), sink)
    m = lax.stop_gradient(m)
    pc = jnp.exp(sc - m)
    pp = jnp.exp(sp - m)
    den = jnp.sum(pc, axis=-1, keepdims=True) + jnp.sum(pp, axis=-1, keepdims=True) + jnp.exp(sink - m)
    inv = 1.0 / den
    return kit.nn(pc * inv, vc) + kit.nn(pp * inv, vp)


def _swa_grads(kit, first, q, kp, kc, vp, vc, qnw, knw, sink, slope, do):
    W = WIN
    ri = lax.broadcasted_iota(jnp.int32, (W, W), 0)
    ci = lax.broadcasted_iota(jnp.int32, (W, W), 1)
    mask_c = ri >= ci
    mask_p = ci > ri + first * W
    dist_c = (ri - ci).astype(F32)
    dist_p = (ri - ci + W).astype(F32)
    scale = HD ** -0.5
    kpn, kp_vjp = jax.vjp(_rms, kp, knw)
    kcn, kc_vjp = jax.vjp(_rms, kc, knw)
    qn, q_vjp = jax.vjp(_rms, q, qnw)
    sc = jnp.where(mask_c, kit.nt(qn, kcn) * scale - slope * dist_c, -1e30)
    sp = jnp.where(mask_p, kit.nt(qn, kpn) * scale - slope * dist_p, -1e30)
    m = jnp.maximum(jnp.maximum(jnp.max(sc, axis=-1, keepdims=True), jnp.max(sp, axis=-1, keepdims=True)), sink)
    ec = jnp.exp(sc - m)
    ep = jnp.exp(sp - m)
    es = jnp.exp(sink - m)
    inv = 1.0 / (jnp.sum(ec, axis=-1, keepdims=True) + jnp.sum(ep, axis=-1, keepdims=True) + es)
    pc, pp = ec * inv, ep * inv
    dpc, dpp = kit.nt(do, vc), kit.nt(do, vp)
    delta = jnp.sum(dpc * pc, axis=-1, keepdims=True) + jnp.sum(dpp * pp, axis=-1, keepdims=True)
    dsc = pc * (dpc - delta) * scale
    dsp = pp * (dpp - delta) * scale
    dq, dqnw = q_vjp(kit.nn(dsc, kcn) + kit.nn(dsp, kpn))
    dkc, dknw_c = kc_vjp(kit.tn(dsc, qn))
    dkp, dknw_p = kp_vjp(kit.tn(dsp, qn))
    return dq, dkp, dkc, kit.tn(pp, do), kit.tn(pc, do), dqnw, dknw_c + dknw_p, -(es * inv) * delta


def _per_query_head(kv_ref):
    return jnp.concatenate([kv_ref[pl.ds(h // SGRP, 1)] for h in range(SQH)], axis=0)


def _per_kv_head(d):
    return jnp.concatenate([jnp.sum(d[g * SGRP:(g + 1) * SGRP], axis=0, keepdims=True) for g in range(SKVH)], axis=0)


def _swa_specs(blk):
    qspec = pl.BlockSpec((SQH, WIN, HD), lambda i: (1, blk(i), 0))
    cur = lambda grp: pl.BlockSpec((SKVH, WIN, HD), lambda i, grp=grp: (grp, blk(i), 0))
    prev = lambda grp: pl.BlockSpec((SKVH, WIN, HD), lambda i, grp=grp: (grp, jnp.maximum(blk(i) - 1, 0), 0))
    whole = pl.BlockSpec((1, HD), lambda i: (0, 0))
    col = pl.BlockSpec((SQH, WIN, 1), lambda i: (0, 0, 0))
    ospec = pl.BlockSpec((SQH, WIN, HD), lambda i: (0, blk(i), 0))
    return qspec, cur, prev, whole, col, ospec


def _swa_fwd(zs_hm, qnw, knw, sinks_col, slopes_col, o_buf, shards):
    T = zs_hm.shape[1]
    NB = T // WIN
    ns = len(shards)
    kit = _Kit(False)
    qspec, cur, prev, whole, col, _ = _swa_specs(lambda i: i)

    def body(*refs):
        q_ref, kp_ref, kc_ref, vp_ref, vc_ref, qnw_ref, knw_ref, s_ref, sl_ref = refs[:9]
        o_ref = refs[10 + ns]
        plan = _gather_half_plan(refs[10:10 + ns], refs[11 + ns:11 + 2 * ns], *refs[11 + 2 * ns:])

        @pl.when(pl.program_id(0) == 0)
        def _():
            _start(plan)

        first = (pl.program_id(0) == 0).astype(jnp.int32)
        o_ref[...] = _swa_heads(kit, first, q_ref[...], _per_query_head(kp_ref), _per_query_head(kc_ref),
                                _per_query_head(vp_ref), _per_query_head(vc_ref), qnw_ref[...], knw_ref[...],
                                s_ref[...], sl_ref[...])

        @pl.when(pl.program_id(0) == NB - 1)
        def _():
            _finish(plan)

    res = pl.pallas_call(
        body, name="swa_fwd", grid=(NB,),
        in_specs=[qspec, prev(8), cur(8), prev(9), cur(9), whole, whole, col, col] + _hbm_specs(1 + ns),
        out_specs=[pl.BlockSpec((SQH, WIN, HD), lambda i: (1, i, 0))] + _hbm_specs(ns),
        out_shape=[_sds(o_buf.shape)] + _gather_shapes(shards),
        input_output_aliases={9: 0},
        scratch_shapes=_gather_sems(ns),
        compiler_params=_cparams(("arbitrary",)),
    )(zs_hm, zs_hm, zs_hm, zs_hm, zs_hm, qnw, knw, sinks_col, slopes_col, o_buf, *shards)
    return res[0], res[1:]


SWA_GRAD_HEADS = SQH + 2 * SKVH


def _swa_bwd(zs_hm, qnw, knw, sinks_col, slopes_col, dmix_hm, d_buf):
    T = zs_hm.shape[1]
    NB = T // WIN
    kit = _Kit(False)
    qspec, cur, prev, whole, col, _ = _swa_specs(lambda i: NB - 1 - i)

    def body(q_ref, kp_ref, kc_ref, vp_ref, vc_ref, qnw_ref, knw_ref, s_ref, sl_ref, do_ref, buf_ref,
             d_ref, dqnw_ref, dknw_ref, ds_ref, ck_scr, cv_scr):
        dq_ref = d_ref.at[pl.ds(0, SQH)]
        dk_ref = d_ref.at[pl.ds(SQH, SKVH)]
        dv_ref = d_ref.at[pl.ds(SQH + SKVH, SKVH)]
        i = pl.program_id(0)
        first = (i == NB - 1).astype(jnp.int32)

        @pl.when(i == 0)
        def _():
            ck_scr[...] = jnp.zeros_like(ck_scr)
            cv_scr[...] = jnp.zeros_like(cv_scr)
            ds_ref[...] = jnp.zeros_like(ds_ref)
            dqnw_ref[...] = jnp.zeros_like(dqnw_ref)
            dknw_ref[...] = jnp.zeros_like(dknw_ref)

        dq, dkp, dkc, dvp, dvc, dqnw, dknw, dsink = _swa_grads(
            kit, first, q_ref[...], _per_query_head(kp_ref), _per_query_head(kc_ref), _per_query_head(vp_ref),
            _per_query_head(vc_ref), qnw_ref[...], knw_ref[...], s_ref[...], sl_ref[...], do_ref[...])
        dq_ref[...] = dq
        dk_ref[...] = _per_kv_head(dkc) + ck_scr[...]
        dv_ref[...] = _per_kv_head(dvc) + cv_scr[...]
        ck_scr[...] = _per_kv_head(dkp)
        cv_scr[...] = _per_kv_head(dvp)
        dqnw_ref[...] += dqnw
        dknw_ref[...] += dknw
        ds_ref[...] += jnp.broadcast_to(jnp.sum(dsink, axis=1, keepdims=True), dsink.shape)

    dospec = pl.BlockSpec((SQH, WIN, HD), lambda i: (1, NB - 1 - i, 0))
    dspec = pl.BlockSpec((SWA_GRAD_HEADS, WIN, HD), lambda i: (1, NB - 1 - i, 0))
    res = pl.pallas_call(
        body, name="swa_bwd", grid=(NB,),
        in_specs=[qspec, prev(8), cur(8), prev(9), cur(9), whole, whole, col, col, dospec] + _hbm_specs(1),
        out_specs=[dspec, whole, whole, col],
        out_shape=[_sds(d_buf.shape), _sds((1, HD)), _sds((1, HD)), _sds((SQH, WIN, 1))],
        input_output_aliases={10: 0},
        scratch_shapes=[pltpu.VMEM((SKVH, WIN, HD), F32), pltpu.VMEM((SKVH, WIN, HD), F32)],
        compiler_params=_cparams(("arbitrary",)),
    )(zs_hm, zs_hm, zs_hm, zs_hm, zs_hm, qnw, knw, sinks_col, slopes_col, dmix_hm, d_buf)
    return res


GAB0 = 3 * GW + 1280


W_IN_ROWS = PROJ // N_CHIP
W_IN_ROWS_PAD = 736


def _permute_w_in_t(w_in_t):
    return jnp.concatenate([w_in_t[:4 * GW], w_in_t[4 * GW + 2 * GH:], w_in_t[4 * GW:4 * GW + 2 * GH],
                            jnp.zeros((NP - PROJ, D), w_in_t.dtype)], axis=0)


def _w_in_grad_pieces(g_t):
    g = jnp.concatenate([g_t[:4 * GW], g_t[GAB0:GAB0 + 2 * GH], g_t[4 * GW:GAB0]], axis=0)
    g = jnp.pad(g.reshape(N_CHIP, W_IN_ROWS, D), ((0, 0), (0, W_IN_ROWS_PAD - W_IN_ROWS), (0, 0)))
    return g.reshape(N_CHIP, 2, W_IN_ROWS_PAD // 2, D)


def _pieces_by_rows(g):
    return g.reshape(N_CHIP, 2, g.shape[0] // (2 * N_CHIP), D)


def _local_step(x, target, mod, n1w, w_in_pt, conv_w, alog, dtb, gnw, qnw, knw, sinks, n2w, shards):
    sh_out, sh_gate, sh_up, sh_down = shards
    T = x.shape[0]
    N = T // CHUNK
    shift1, scale1, gate1, shift2, scale2, gate2 = [mod[:, i * D:(i + 1) * D] for i in range(6)]

    h, proj, (a_out,) = _norm_in_proj(x, n1w, scale1, shift1, w_in_pt, [sh_out])
    w_out = a_out.reshape(D, D)
    qkv_hm = _conv_fwd(proj, conv_w)
    zs_hm = _split_heads(proj, 3 * GW // LANE, 20, "split_zs")
    gab = proj[:, GAB0:GAB0 + 2 * GH].T.reshape(2 * GH, N, 1, CHUNK)
    alog_b = jnp.broadcast_to(alog.reshape(GH, 1, 1), (GH, 1, CHUNK))
    dtb_b = jnp.broadcast_to(dtb.reshape(GH, 1, 1), (GH, 1, CHUNK))
    sinks_col = jnp.broadcast_to(sinks.reshape(SQH, 1, 1), (SQH, WIN, 1))
    o_hm, S_all, (a_gate, a_up) = _gdn_fwd(qkv_hm, zs_hm, gab, alog_b, dtb_b, gnw, [sh_gate, sh_up])
    w_gut = _interleave_gate_up(a_gate.reshape(DFF, D), a_up.reshape(DFF, D))
    slopes = 2.0 ** (-8.0 * (jnp.arange(SQH, dtype=F32) + 1.0) / SQH)
    slopes_col = jnp.broadcast_to(slopes.reshape(SQH, 1, 1), (SQH, WIN, 1))
    o_hm, (a_down,) = _swa_fwd(zs_hm, qnw, knw, sinks_col, slopes_col, o_hm, [sh_down])
    w_down = _sibling_fill(a_down).reshape(DFF, D)
    mixcat = _merge_heads(o_hm, BF16, "merge_mix")
    mixed, x1, h2 = _out_proj_resid_norm(mixcat, w_out, x, gate1, n2w, scale2, shift2)
    ab, act = _ffn_up_act(h2, w_gut)
    dy, dffn, dgate2, loss = _ffn_down_loss(act, w_down, x1, target, gate2)

    dab = _ffn_down_dx_act(dffn, w_down, ab)
    g_w_down = _matmul(act, dffn, ta=True, out_dtype=BF16, name="ffn_down_dw")
    g_w_gut = _matmul(dab, h2, ta=True, out_dtype=BF16, name="ffn_up_dw")
    dx1, dmixed, dgate1, dn2w, dscale2, dshift2 = _ffn_up_dx_resid_bwd(dab, w_gut, x, mixed, dy, gate1, n2w, scale2,
                                                                       shift2)
    g_w_out = _matmul(mixcat, dmixed, ta=True, out_dtype=BF16, name="out_proj_dw")
    dmix_hm = _split_heads(_matmul(dmixed, w_out, tb=True, name="out_proj_dx"), 0, GH + SQH, "split_dmix")
    g_gate_t, g_up_t = _split_gate_up(g_w_gut)
    pieces = [_pieces_by_rows(g_w_out), _pieces_by_rows(g_gate_t), _pieces_by_rows(g_up_t),
              _pieces_by_rows(g_w_down)]
    (dqkv_hm, d_hm, dga, dgb, dalog, ddtb, dgnw), recv = _gdn_bwd(qkv_hm, zs_hm, gab, alog_b, dtb_b, gnw, S_all,
                                                                  dmix_hm, pieces)
    d_hm, dqnw, dknw, dsinks = _swa_bwd(zs_hm, qnw, knw, sinks_col, slopes_col, dmix_hm, d_hm)
    dproj, dconv = _conv_bwd(proj, conv_w, dqkv_hm)
    dproj = _merge_heads(d_hm, BF16, "merge_dz", into=dproj, col_block0=3 * GW // LANE, head0=0, nheads=GH)
    dproj = _merge_heads(d_hm, BF16, "merge_dswa", into=dproj, col_block0=4 * GW // LANE, head0=GH + 4,
                         nheads=SWA_GRAD_HEADS)
    dgab = jnp.concatenate([dga, dgb], axis=0).reshape(2 * GH, T).T.astype(BF16)
    dproj = lax.dynamic_update_slice(dproj, jnp.concatenate([dgab, jnp.zeros((T, NP - PROJ), BF16)], axis=1),
                                     (0, GAB0))
    g_w_in_pt = _matmul(dproj, h, ta=True, out_dtype=BF16, name="in_proj_dw")
    (grad_x, dn1w, dscale1, dshift1), recv_in = _in_proj_dx_norm_bwd(dproj, w_in_pt, x, dx1, n1w, scale1, shift1,
                                                                     [_w_in_grad_pieces(g_w_in_pt)])

    dmod = jnp.concatenate([dshift1, dscale1, dgate1, dshift2, dscale2, dgate2], axis=1)
    big = list(recv_in) + list(recv)
    small = dict(mod=dmod, norm1_w=dn1w, norm2_w=dn2w, conv_w=dconv, a_log=dalog[:, 0, 0], dt_bias=ddtb[:, 0, 0],
                 gdn_norm_w=dgnw, q_norm_w=dqnw, k_norm_w=dknw, sinks=dsinks[:, 0, 0])
    return loss, grad_x, big, small


def _adamw(w, g, m, v):
    m2 = ADAM_B1 * m + (1.0 - ADAM_B1) * g
    v2 = ADAM_B2 * v + (1.0 - ADAM_B2) * (g * g)
    m_hat = m2 / (1.0 - ADAM_B1 ** ADAM_STEP)
    v_hat = v2 / (1.0 - ADAM_B2 ** ADAM_STEP)
    delta = -ADAM_LR * (m_hat / (jnp.sqrt(v_hat) + ADAM_EPS) + ADAM_WD * w)
    return delta, m2, v2


def _reduce_adamw(recv, w, m, v, name):
    _, R, C = recv.shape
    tc = _tile(C, 256)

    def body(r_ref, w_ref, m_ref, v_ref, o_ref):
        g = r_ref[0].astype(F32)
        for s in range(1, N_DEV):
            g = g + r_ref[s].astype(F32)
        delta, m2, v2 = _adamw(w_ref[...], g, m_ref[...], v_ref[...])
        o_ref[0] = g
        o_ref[1] = delta
        o_ref[2] = m2
        o_ref[3] = v2

    col = pl.BlockSpec((R, tc), lambda j: (0, j))
    return pl.pallas_call(
        body, name=name, grid=(C // tc,),
        in_specs=[pl.BlockSpec((N_DEV, R, tc), lambda j: (0, 0, j)), col, col, col],
        out_specs=pl.BlockSpec((4, R, tc), lambda j: (0, 0, j)),
        out_shape=_sds((4, R, C)),
        compiler_params=_cparams(("parallel",)),
    )(recv, w, m, v)


def _adamw_call(g, w, m, v, name):
    def body(g_ref, w_ref, m_ref, v_ref, o_ref):
        delta, m2, v2 = _adamw(w_ref[...], g_ref[...], m_ref[...], v_ref[...])
        o_ref[0] = delta
        o_ref[1] = m2
        o_ref[2] = v2

    return pl.pallas_call(body, name=name, out_shape=_sds((3,) + g.shape))(g, w, m, v)


ADA_N = 6 * D // N_CHIP
KPAD = 128


def _w_ada_update(c8p, dm, w, m, v):
    tr = 256

    def body(c_ref, dm_ref, w_ref, m_ref, v_ref, g_ref, d_ref, m2_ref, v2_ref):
        g = _raw1(_silu(c_ref[...]), dm_ref[...], _TN)
        delta, m2, v2 = _adamw(w_ref[...], g, m_ref[...], v_ref[...])
        g_ref[...] = g
        d_ref[...] = delta
        m2_ref[...] = m2
        v2_ref[...] = v2

    blk = pl.BlockSpec((tr, ADA_N), lambda i: (i, 0))
    return pl.pallas_call(
        body, name="w_ada_update", grid=(D // tr,),
        in_specs=[pl.BlockSpec((KPAD, tr), lambda i: (0, i)), pl.BlockSpec((KPAD, ADA_N), lambda i: (0, 0)),
                  blk, blk, blk],
        out_specs=[blk] * 4, out_shape=[_sds((D, ADA_N))] * 4,
        compiler_params=_cparams(("parallel",)),
    )(c8p, dm, w, m, v)


def _me():
    return lax.axis_index("x"), lax.axis_index("y"), lax.axis_index("c")


def _peer(k, me):
    mx, my, mc = me
    return (1 - mx if k & 4 else mx, 1 - my if k & 2 else my, 1 - mc if k & 1 else mc)


def _lin(p):
    return 4 * p[0] + 2 * p[1] + p[2]


def _remote(src, dst, ssem, rsem, dev):
    return pltpu.make_async_remote_copy(src_ref=src, dst_ref=dst, send_sem=ssem, recv_sem=rsem,
                                        device_id=dev, device_id_type=MESH)


def _all_gather8(x, name):
    def body(x_ref, out_ref, send_sems, recv_sems):
        me = _me()
        out_ref[_lin(me)] = x_ref[...]
        sends = []
        for k in range(1, N_DEV):
            cp = _remote(x_ref, out_ref.at[_lin(me)], send_sems.at[k - 1], recv_sems.at[k - 1], _peer(k, me))
            cp.start()
            sends.append(cp)
        for k in range(1, N_DEV):
            p = _peer(k, me)
            _remote(x_ref, out_ref.at[_lin(p)], send_sems.at[k - 1], recv_sems.at[k - 1], p).wait_recv()
        for cp in sends:
            cp.wait_send()

    return pl.pallas_call(
        body, name=name,
        out_shape=_sds((N_DEV,) + x.shape, x.dtype),
        in_specs=[pl.BlockSpec(memory_space=pltpu.VMEM)],
        out_specs=pl.BlockSpec(memory_space=pltpu.VMEM),
        scratch_shapes=[pltpu.SemaphoreType.DMA((N_DEV - 1,)), pltpu.SemaphoreType.DMA((N_DEV - 1,))],
    )(x)


def _ag8_plan(src, out, send_sems, recv_sems):
    me = _me()
    sends, recvs = [], []
    for k in range(1, N_DEV):
        p = _peer(k, me)
        sends.append(_remote(src, out.at[_lin(me)], send_sems.at[k - 1], recv_sems.at[k - 1], p))
        recvs.append(_remote(src, out.at[_lin(p)], send_sems.at[k - 1], recv_sems.at[k - 1], p))
    return [], sends, recvs


def _prologue(c_row, conv_sh, w_ada, b_sh, w_in_sh):
    def body(c_ref, cv_ref, wa_ref, b_ref, win_ref, call_ref, cvall_ref, mods_ref, ain_ref, c16_scr, mp_scr,
             c_send, c_recv, cv_send, cv_recv, m_send, m_recv, w_send, w_recv, w_local):
        me = _lin(_me())
        w_plan = _gather_half_plan([win_ref], [ain_ref], w_send, w_recv, w_local)
        _start(w_plan)
        c_plan = _ag8_plan(c_ref, call_ref, c_send, c_recv)
        cv_plan = _ag8_plan(cv_ref, cvall_ref, cv_send, cv_recv)
        call_ref[me] = c_ref[...]
        cvall_ref[me] = cv_ref[...]
        _start(c_plan)
        _start(cv_plan)
        _finish(c_plan)
        c16_scr[...] = jnp.zeros_like(c16_scr)
        for d in range(N_DEV):
            c16_scr[pl.ds(d, 1), :] = call_ref[d]
        mp_scr[...] = _raw1(_silu(c16_scr[...]), wa_ref[...], _NN) + b_ref[...]
        mods_ref[me] = mp_scr[...]
        m_plan = _ag8_plan(mp_scr, mods_ref, m_send, m_recv)
        _start(m_plan)
        _finish(cv_plan)
        _finish(m_plan)
        _finish(w_plan)

    vmem = pl.BlockSpec(memory_space=pltpu.VMEM)
    sems = lambda n: pltpu.SemaphoreType.DMA((n,))
    return pl.pallas_call(
        body, name="prologue",
        in_specs=[vmem] * 4 + _hbm_specs(1), out_specs=[vmem] * 3 + _hbm_specs(1),
        out_shape=[_sds((N_DEV,) + c_row.shape), _sds((N_DEV,) + conv_sh.shape), _sds((N_DEV, 16, ADA_N)),
                   _sds((N_CHIP,) + w_in_sh.shape, w_in_sh.dtype)],
        scratch_shapes=[pltpu.VMEM((16, D), F32), pltpu.VMEM((16, ADA_N), F32)] + [sems(N_DEV - 1)] * 6
                       + _gather_sems(1),
        compiler_params=_cparams(),
    )(c_row, conv_sh, w_ada, b_sh, w_in_sh)


def _hbm_specs(n):
    return [pl.BlockSpec(memory_space=pl.ANY)] * n


def _gather_shapes(shards):
    return [_sds((N_CHIP,) + s.shape, s.dtype) for s in shards]


def _gather_sems(n):
    return [pltpu.SemaphoreType.DMA((3 * n,)), pltpu.SemaphoreType.DMA((3 * n,)), pltpu.SemaphoreType.DMA((n,))]


def _gather_plan(ins, outs, send_sems, recv_sems, local_sems):
    mx, my, mc = _me()
    chips = [(1 - mx, my), (mx, 1 - my), (1 - mx, 1 - my)]
    local, sends, recvs = [], [], []
    for a in range(len(ins)):
        local.append(pltpu.make_async_copy(ins[a], outs[a].at[2 * mx + my], local_sems.at[a]))
        for k, (px, py) in enumerate(chips):
            sems = (send_sems.at[3 * a + k], recv_sems.at[3 * a + k], (px, py, mc))
            sends.append(_remote(ins[a], outs[a].at[2 * mx + my], *sems))
            recvs.append(_remote(ins[a], outs[a].at[2 * px + py], *sems))
    return local, sends, recvs


def _gather_half_plan(ins, outs, send_sems, recv_sems, local_sems):
    mx, my, mc = _me()
    chips = [(1 - mx, my), (mx, 1 - my), (1 - mx, 1 - my)]
    local, sends, recvs = [], [], []
    for a in range(len(ins)):
        h = ins[a].shape[0] // 2
        mine = pl.ds(pl.multiple_of(mc * h, 16), h)
        local.append(pltpu.make_async_copy(ins[a], outs[a].at[2 * mx + my], local_sems.at[a]))
        for k, (px, py) in enumerate(chips):
            sems = (send_sems.at[3 * a + k], recv_sems.at[3 * a + k], (px, py, mc))
            sends.append(_remote(ins[a].at[mine], outs[a].at[2 * mx + my, mine], *sems))
            recvs.append(_remote(ins[a].at[mine], outs[a].at[2 * px + py, mine], *sems))
    return local, sends, recvs


def _sibling_fill(pieces):
    h = pieces.shape[1] // 2

    def body(p_ref, o_ref, send_sems, recv_sems):
        mx, my, mc = _me()
        sib = (mx, my, 1 - mc)
        chips = [(1 - mx, my), (mx, 1 - my), (1 - mx, 1 - my)]
        half = lambda c: pl.ds(pl.multiple_of(c * h, 16), h)
        o_ref[2 * mx + my] = p_ref[2 * mx + my]
        sends = []
        for k, (px, py) in enumerate(chips):
            j = 2 * px + py
            o_ref[j, half(mc), :] = p_ref[j, half(mc), :]
            cp = _remote(p_ref.at[j, half(mc)], o_ref.at[j, half(mc)], send_sems.at[k], recv_sems.at[k], sib)
            cp.start()
            sends.append(cp)
        for k, (px, py) in enumerate(chips):
            j = 2 * px + py
            _remote(p_ref.at[j, half(mc)], o_ref.at[j, half(1 - mc)], send_sems.at[k], recv_sems.at[k],
                    sib).wait_recv()
        for cp in sends:
            cp.wait_send()

    vmem = pl.BlockSpec(memory_space=pltpu.VMEM)
    return pl.pallas_call(
        body, name="sibling_fill", out_shape=_sds(pieces.shape, pieces.dtype),
        in_specs=[vmem], out_specs=vmem,
        scratch_shapes=[pltpu.SemaphoreType.DMA((N_CHIP - 1,)), pltpu.SemaphoreType.DMA((N_CHIP - 1,))],
        compiler_params=_cparams(),
    )(pieces)


def _start(plan):
    local, sends, _ = plan
    for cp in local + sends:
        cp.start()


def _finish(plan):
    local, sends, recvs = plan
    for cp in recvs:
        cp.wait_recv()
    for cp in sends:
        cp.wait_send()
    for cp in local:
        cp.wait()


def _exchange_shapes(pieces):
    return [_sds((N_DEV,) + p.shape[2:], p.dtype) for p in pieces]


def _exchange_sems(n):
    return [pltpu.SemaphoreType.DMA(((N_DEV - 1) * n,)), pltpu.SemaphoreType.DMA(((N_DEV - 1) * n,)),
            pltpu.SemaphoreType.DMA((n,))]


def _exchange_plan(ins, outs, send_sems, recv_sems, local_sems):
    me = _me()
    mx, my, mc = me
    local, sends, recvs = [], [], []
    for a in range(len(ins)):
        local.append(pltpu.make_async_copy(ins[a].at[2 * mx + my, mc], outs[a].at[_lin(me)], local_sems.at[a]))
        for k in range(1, N_DEV):
            p = _peer(k, me)
            s = (N_DEV - 1) * a + k - 1
            sends.append(_remote(ins[a].at[2 * p[0] + p[1], p[2]], outs[a].at[_lin(me)], send_sems.at[s],
                                 recv_sems.at[s], p))
            recvs.append(_remote(ins[a].at[2 * mx + my, mc], outs[a].at[_lin(p)], send_sems.at[s],
                                 recv_sems.at[s], p))
    return local, sends, recvs


REDUCE_VMEM = 56 * 1024 * 1024


def _reduce_swap(recvs):
    n = len(recvs)

    def body(*refs):
        r_refs, o_refs = refs[:n], refs[n:2 * n]
        send_sems, recv_sems = refs[2 * n:]
        mx, my, mc = _me()
        sib = (mx, my, 1 - mc)
        half = lambda a, c: o_refs[a].at[pl.ds(pl.multiple_of(c * recvs[a].shape[1], 8), recvs[a].shape[1])]
        sends = []
        for a in range(n):
            g = r_refs[a][0].astype(F32)
            for s in range(1, N_DEV):
                g = g + r_refs[a][s].astype(F32)
            half(a, mc)[...] = g
            cp = _remote(half(a, mc), half(a, mc), send_sems.at[a], recv_sems.at[a], sib)
            cp.start()
            sends.append(cp)
        for a in range(n):
            _remote(half(a, mc), half(a, 1 - mc), send_sems.at[a], recv_sems.at[a], sib).wait_recv()
        for cp in sends:
            cp.wait_send()

    vmem = pl.BlockSpec(memory_space=pltpu.VMEM)
    return pl.pallas_call(
        body, name="reduce_swap", out_shape=[_sds((2 * r.shape[1], r.shape[2])) for r in recvs],
        in_specs=[vmem] * n, out_specs=[vmem] * n,
        scratch_shapes=[pltpu.SemaphoreType.DMA((n,)), pltpu.SemaphoreType.DMA((n,))],
        compiler_params=_cparams(None, REDUCE_VMEM),
    )(*recvs)


def _adamw_big(g, w, m, v, name):
    rows, cols = g.shape
    tr = next((t for t in (256, 176, 128, 64, 8) if rows % t == 0), None)
    if tr is None:
        tc = _tile(cols, 256)
        blk, grid = pl.BlockSpec((rows, tc), lambda i: (0, i)), (cols // tc,)
    else:
        blk, grid = pl.BlockSpec((tr, cols), lambda i: (i, 0)), (rows // tr,)

    def body(g_ref, w_ref, m_ref, v_ref, go_ref, d_ref, m2_ref, v2_ref):
        g = g_ref[...]
        delta, m2, v2 = _adamw(w_ref[...], g, m_ref[...], v_ref[...])
        go_ref[...] = g
        d_ref[...] = delta
        m2_ref[...] = m2
        v2_ref[...] = v2

    return pl.pallas_call(
        body, name=name, grid=grid,
        in_specs=[blk] * 4, out_specs=[blk] * 4, out_shape=[_sds((rows, cols))] * 4,
        compiler_params=_cparams(("parallel",)),
    )(g, w, m, v)


SMALL_ORDER = (("mod", 6 * D), ("norm1_w", D), ("norm2_w", D), ("conv_w", CONVW * 3 * GW), ("a_log", GH),
               ("dt_bias", GH), ("gdn_norm_w", HD), ("q_norm_w", HD), ("k_norm_w", HD), ("sinks", SQH), ("loss", 1))
SMALL_R = 120


def _pack_small(d):
    parts = [d[k].reshape(-1).astype(F32) if k in d else jnp.zeros((n,), F32) for k, n in SMALL_ORDER]
    used = sum(n for _, n in SMALL_ORDER)
    parts.append(jnp.zeros((SMALL_R * LANE - used,), F32))
    return jnp.concatenate(parts).reshape(SMALL_R, LANE)


def _unpack_small(pk):
    flat = pk.reshape(-1)
    out, r = {}, 0
    for k, n in SMALL_ORDER:
        out[k] = flat[r:r + n]
        r += n
    return out


def kernel(x, c, w_ada, b_ada, norm1_w, w_in, conv_w, a_log, dt_bias, gdn_norm_w, q_norm_w, k_norm_w, sinks, w_out, norm2_w, w_gate, w_up, w_down, loss_target, m_w_ada, m_b_ada, m_norm1_w, m_w_in, m_conv_w, m_a_log, m_dt_bias, m_gdn_norm_w, m_q_norm_w, m_k_norm_w, m_sinks, m_w_out, m_norm2_w, m_w_gate, m_w_up, m_w_down, v_w_ada, v_b_ada, v_norm1_w, v_w_in, v_conv_w, v_a_log, v_dt_bias, v_gdn_norm_w, v_q_norm_w, v_k_norm_w, v_sinks, v_w_out, v_norm2_w, v_w_gate, v_w_up, v_w_down):
    mx, my, mc = _me()
    chip = 2 * mx + my
    dev = 4 * mx + 2 * my + mc
    T = x.shape[1]

    as_rows = lambda t, transposed: t[0].T if transposed else t[0]
    transposed = (True, False, True, True, False)
    big_w = [as_rows(t, tr) for t, tr in zip((w_in, w_out, w_gate, w_up, w_down), transposed)]
    shards = [t.astype(BF16) for t in big_w]

    b_sh = lax.dynamic_slice(b_ada, (0, chip * ADA_N), (1, ADA_N))
    w_in_sh = jnp.pad(shards[0], ((0, W_IN_ROWS_PAD - W_IN_ROWS), (0, 0)))
    c_all, conv_all, mods, a_in = _prologue(c, conv_w.reshape(CONVW, 3 * GW // N_CHIP), w_ada[0], b_sh, w_in_sh)
    c8 = c_all.reshape(N_DEV, D)
    conv_full = jnp.concatenate([conv_all[2 * j] for j in range(N_CHIP)], axis=1)
    mod = jnp.concatenate([lax.dynamic_slice(mods[2 * j], (dev, 0), (1, ADA_N)) for j in range(N_CHIP)], axis=1)
    w_in_pt = _permute_w_in_t(_sibling_fill(a_in)[:, :W_IN_ROWS].reshape(PROJ, D))

    loss, grad_x, big, small = _local_step(
        x[0], loss_target[0], mod, norm1_w, w_in_pt, conv_full, a_log, dt_bias, gdn_norm_w,
        q_norm_w, k_norm_w, sinks, norm2_w, shards[1:])

    small["loss"] = loss[:, :1]
    sg = _all_gather8(_pack_small(small), "gather_small_grads")
    rep = dict(mod=(b_ada, m_b_ada, v_b_ada), norm1_w=(norm1_w, m_norm1_w, v_norm1_w),
               norm2_w=(norm2_w, m_norm2_w, v_norm2_w), a_log=(a_log, m_a_log, v_a_log),
               dt_bias=(dt_bias, m_dt_bias, v_dt_bias), gdn_norm_w=(gdn_norm_w, m_gdn_norm_w, v_gdn_norm_w),
               q_norm_w=(q_norm_w, m_q_norm_w, v_q_norm_w), k_norm_w=(k_norm_w, m_k_norm_w, v_k_norm_w),
               sinks=(sinks, m_sinks, v_sinks))
    wmv = [_pack_small({k: t[i] for k, t in rep.items()}) for i in range(3)]
    sres = _reduce_adamw(sg, wmv[0], wmv[1], wmv[2], "small_reduce_adamw")
    s_g, s_d, s_m, s_v = [_unpack_small(sres[i]) for i in range(4)]
    loss_out = s_g["loss"][0]

    g_conv = lax.dynamic_slice(s_g["conv_w"].reshape(CONVW, 3 * GW), (0, chip * (3 * GW // N_CHIP)),
                               (CONVW, 3 * GW // N_CHIP))
    pad16 = lambda t: jnp.concatenate([t.reshape(12, LANE), jnp.zeros((4, LANE), F32)], axis=0)
    cres = _adamw_call(pad16(g_conv), pad16(conv_w), pad16(m_conv_w), pad16(v_conv_w), "conv_adamw")
    conv_out = [g_conv.reshape(conv_w.shape)] + [cres[i, :12].reshape(conv_w.shape) for i in range(3)]

    dmod8 = sg[:, :6 * D // LANE].reshape(N_DEV, 6 * D)
    dm = lax.dynamic_slice(dmod8, (0, chip * ADA_N), (N_DEV, ADA_N))
    zpad = lambda t: jnp.concatenate([t, jnp.zeros((KPAD - N_DEV, t.shape[1]), F32)], axis=0)
    ares = _w_ada_update(zpad(c8), zpad(dm), w_ada[0], m_w_ada[0], v_w_ada[0])

    names = ("w_in", "w_out", "w_gate", "w_up", "w_down")
    g_full = list(_reduce_swap(big))
    g_full[0] = g_full[0][:W_IN_ROWS]
    big_m = [as_rows(t, tr) for t, tr in zip((m_w_in, m_w_out, m_w_gate, m_w_up, m_w_down), transposed)]
    big_v = [as_rows(t, tr) for t, tr in zip((v_w_in, v_w_out, v_w_gate, v_w_up, v_w_down), transposed)]
    upd = [_adamw_big(g, w, m, v, "adamw_" + nm) for g, w, m, v, nm in zip(g_full, big_w, big_m, big_v, names)]
    back = lambda t, tr: (t.T if tr else t)[None]
    bg, bd, bm, bv = [[back(u[i], tr) for u, tr in zip(upd, transposed)] for i in range(4)]

    def group(a_i, small_d, conv_i, big_l):
        s = lambda k, ref: small_d[k].reshape(ref.shape)
        return [ares[a_i][None], s("mod", b_ada), s("norm1_w", norm1_w), big_l[0], conv_out[conv_i],
                s("a_log", a_log), s("dt_bias", dt_bias), s("gdn_norm_w", gdn_norm_w), s("q_norm_w", q_norm_w),
                s("k_norm_w", k_norm_w), s("sinks", sinks), big_l[1], s("norm2_w", norm2_w), big_l[2], big_l[3],
                big_l[4]]

    outs = [loss_out, grad_x[None]]
    outs += group(0, s_g, 0, bg) + group(1, s_d, 1, bd) + group(2, s_m, 2, bm) + group(3, s_v, 3, bv)
    return tuple(outs)
```

```python
import jax
import jax.numpy as jnp
from jax import lax
from jax.experimental import pallas as pl
from jax.experimental.pallas import tpu as pltpu

F32 = jnp.float32
BF16 = jnp.bfloat16
MESH = pl.DeviceIdType.MESH

D = 1024
HD = 64
GH = 8
GW = GH * HD
SQH = 8
SKVH = 2
SGRP = SQH // SKVH
WIN = 128
CONVW = 4
CHUNK = 64
DFF = 2816
PROJ = 2832
NP = 3072
EPS = 1e-6
N_DEV = 8
N_CHIP = 4

ADAM_LR = 0.001
ADAM_B1 = 0.9
ADAM_B2 = 0.999
ADAM_EPS = 1e-08
ADAM_WD = 0.01
ADAM_STEP = 10

VMEM_LIMIT = 48 * 1024 * 1024
GDN_BWD_VMEM = 58 * 1024 * 1024
LANE = 128


def _cparams(sem=None, vmem=VMEM_LIMIT):
    return pltpu.CompilerParams(dimension_semantics=sem, vmem_limit_bytes=vmem)


_NN = ((1,), (0,))
_NT = ((1,), (1,))
_TN = ((0,), (0,))


def _dot(a, b, dims):
    if a.ndim == 3:
        (ca,), (cb,) = dims
        return lax.dot_general(a, b, (((ca + 1,), (cb + 1,)), ((0,), (0,))), preferred_element_type=F32)
    return lax.dot_general(a, b, (dims, ((), ())), preferred_element_type=F32)


def _raw1(a, b, dims):
    return _dot(a.astype(BF16), b.astype(BF16), dims)


def _raw3(a, b, dims):
    ah = a.astype(BF16)
    al = (a - ah.astype(F32)).astype(BF16)
    bh = b.astype(BF16)
    bl = (b - bh.astype(F32)).astype(BF16)
    return _dot(ah, bh, dims) + (_dot(al, bh, dims) + _dot(ah, bl, dims))


def _make_diff_mm(raw):
    @jax.custom_vjp
    def nn(a, b):
        return raw(a, b, _NN)

    @jax.custom_vjp
    def nt(a, b):
        return raw(a, b, _NT)

    @jax.custom_vjp
    def tn(a, b):
        return raw(a, b, _TN)

    nn.defvjp(lambda a, b: (raw(a, b, _NN), (a, b)), lambda r, g: (nt(g, r[1]), tn(r[0], g)))
    nt.defvjp(lambda a, b: (raw(a, b, _NT), (a, b)), lambda r, g: (nn(g, r[1]), tn(g, r[0])))
    tn.defvjp(lambda a, b: (raw(a, b, _TN), (a, b)), lambda r, g: (nt(r[1], g), nn(r[0], g)))
    return nn, nt, tn


def _tri_inv_raw(a, nn3):
    n = a.shape[-1]
    ri = lax.broadcasted_iota(jnp.int32, (n, n), 0)
    ci = lax.broadcasted_iota(jnp.int32, (n, n), 1)
    t = (ri == ci).astype(F32)
    for lvl in range((n - 1).bit_length()):
        same_pair = (ri >> (lvl + 1)) == (ci >> (lvl + 1))
        lower_left = (((ri >> lvl) & 1) == 1) & (((ci >> lvl) & 1) == 0)
        y = jnp.where(same_pair & lower_left, a, 0.0)
        t = t - y if lvl == 0 else t - nn3(nn3(t, y), t)
    return t


class _Kit:
    def __init__(self, diff):
        if diff:
            self.nn, self.nt, self.tn = _make_diff_mm(_raw1)
            self.nn3, self.nt3, self.tn3 = _make_diff_mm(_raw3)
            nn3, nt3, tn3 = self.nn3, self.nt3, self.tn3

            @jax.custom_vjp
            def inv(a, t):
                return t

            def inv_fwd(a, t):
                return t, t

            def inv_bwd(t, g):
                return -tn3(t, nt3(g, t)), jnp.zeros_like(t)

            inv.defvjp(inv_fwd, inv_bwd)
            self.inv = inv
        else:
            self.nn = lambda a, b: _raw1(a, b, _NN)
            self.nt = lambda a, b: _raw1(a, b, _NT)
            self.tn = lambda a, b: _raw1(a, b, _TN)
            self.nn3 = lambda a, b: _raw3(a, b, _NN)
            self.nt3 = lambda a, b: _raw3(a, b, _NT)
            self.tn3 = lambda a, b: _raw3(a, b, _TN)
            self.inv = lambda a, t: _tri_inv_raw(a, self.nn3) if t is None else t


def _sigmoid(x):
    return 1.0 / (1.0 + jnp.exp(-x))


def _silu(x):
    return x * _sigmoid(x)


def _rms(x, w):
    return x * lax.rsqrt(jnp.mean(x * x, axis=-1, keepdims=True) + EPS) * w


def _tile(dim, target):
    t = (min(dim, target) // LANE) * LANE
    while t >= LANE:
        if dim % t == 0:
            return t
        t -= LANE
    return dim


MM_TM, MM_TN, MM_TK = 1408, 1536, 1408


def _matmul(a, b, ta=False, tb=False, out_dtype=F32, name="matmul", gather=None, exchange=None):
    carried = gather if gather is not None else exchange if exchange is not None else []
    nc = len(carried)
    if ta:
        K, M = a.shape
    else:
        M, K = a.shape
    if tb:
        N, K2 = b.shape
    else:
        K2, N = b.shape
    assert K == K2, (a.shape, b.shape, ta, tb)
    tm, tn, tk = _tile(M, MM_TM), _tile(N, MM_TN), _tile(K, MM_TK)
    nk = K // tk
    dims = ((0,) if ta else (1,), (1,) if tb else (0,))

    grid = (M // tm, N // tn, nk)

    def body(*refs):
        a_ref, b_ref = refs[:2]
        o_ref = refs[2 + nc]
        scratch = refs[3 + 2 * nc:]
        k = pl.program_id(2)
        if nc:
            make_plan = _gather_plan if gather is not None else _exchange_plan
            plan = make_plan(refs[2:2 + nc], refs[3 + nc:3 + 2 * nc], *scratch[-3:])
            at = lambda pos: ((pl.program_id(0) == pos[0]) & (pl.program_id(1) == pos[1]) & (k == pos[2]))

            @pl.when(at((0, 0, 0)))
            def _():
                _start(plan)

        part = _dot(a_ref[...].astype(BF16), b_ref[...].astype(BF16), dims)
        if nk == 1:
            o_ref[...] = part.astype(o_ref.dtype)
        else:
            acc_ref = scratch[0]

            @pl.when(k == 0)
            def _():
                acc_ref[...] = part

            @pl.when((k > 0) & (k < nk - 1))
            def _():
                acc_ref[...] += part

            @pl.when(k == nk - 1)
            def _():
                o_ref[...] = (acc_ref[...] + part).astype(o_ref.dtype)

        if nc:
            @pl.when(at((grid[0] - 1, grid[1] - 1, nk - 1)))
            def _():
                _finish(plan)

    a_spec = (pl.BlockSpec((tk, tm), lambda i, j, k: (k, i)) if ta
              else pl.BlockSpec((tm, tk), lambda i, j, k: (i, k)))
    b_spec = (pl.BlockSpec((tn, tk), lambda i, j, k: (j, k)) if tb
              else pl.BlockSpec((tk, tn), lambda i, j, k: (k, j)))
    if gather is not None:
        c_shapes, c_sems = _gather_shapes(carried), _gather_sems(nc)
    elif exchange is not None:
        c_shapes, c_sems = _exchange_shapes(carried), _exchange_sems(nc)
    else:
        c_shapes, c_sems = [], []
    res = pl.pallas_call(
        body, name=name, grid=grid,
        in_specs=[a_spec, b_spec] + _hbm_specs(nc),
        out_specs=[pl.BlockSpec((tm, tn), lambda i, j, k: (i, j))] + _hbm_specs(nc),
        out_shape=[jax.ShapeDtypeStruct((M, N), out_dtype)] + c_shapes,
        scratch_shapes=([pltpu.VMEM((tm, tn), F32)] if nk > 1 else []) + c_sems,
        compiler_params=_cparams(("arbitrary",) * 3 if nc else ("parallel", "parallel", "arbitrary")),
    )(a, b, *carried)
    return (res[0], res[1:]) if nc else res[0]


def _sds(shape, dtype=F32):
    return jax.ShapeDtypeStruct(shape, dtype)


def _norm_mod(x, nw, scale, shift):
    return _rms(x, nw) * (1.0 + scale) + shift


def _norm_in_proj(x, nw, scale, shift, w_in_pt, shards):
    T = x.shape[0]
    N = w_in_pt.shape[0]
    tm, tn = _tile(T, 1024), _tile(N, MM_TN)
    nm, nn = T // tm, N // tn
    ns = len(shards)

    def body(*refs):
        x_ref, nw_ref, sc_ref, sh_ref, w_ref = refs[:5]
        h_ref, o_ref = refs[5 + ns:7 + ns]
        plan = _gather_plan(refs[5:5 + ns], refs[7 + ns:7 + 2 * ns], *refs[7 + 2 * ns:])
        i, j = pl.program_id(0), pl.program_id(1)

        @pl.when((i == 0) & (j == 0))
        def _():
            _start(plan)

        @pl.when(j == 0)
        def _():
            for r0 in range(0, tm, ROWS_EPI):
                rows = pl.ds(r0, ROWS_EPI)
                h_ref[rows, :] = _norm_mod(x_ref[rows, :], nw_ref[...], sc_ref[...], sh_ref[...]).astype(BF16)

        o_ref[...] = _dot(h_ref[...], w_ref[...], _NT)

        @pl.when((i == nm - 1) & (j == nn - 1))
        def _():
            _finish(plan)

    vec = pl.BlockSpec((1, D), lambda i, j: (0, 0))
    res = pl.pallas_call(
        body, name="norm1_in_proj", grid=(nm, nn),
        in_specs=[pl.BlockSpec((tm, D), lambda i, j: (i, 0)), vec, vec, vec,
                  pl.BlockSpec((tn, D), lambda i, j: (j, 0))] + _hbm_specs(ns),
        out_specs=[pl.BlockSpec((tm, D), lambda i, j: (i, 0)), pl.BlockSpec((tm, tn), lambda i, j: (i, j))]
                  + _hbm_specs(ns),
        out_shape=[_sds((T, D), BF16), _sds((T, N))] + _gather_shapes(shards),
        scratch_shapes=_gather_sems(ns),
        compiler_params=_cparams(("arbitrary", "arbitrary")),
    )(x, nw, scale, shift, w_in_pt, *shards)
    return res[0], res[1], res[2:]


ROWS_TM = 512
ROWS_EPI = 256


def _matmul_rows(a, b, epi, tiled, consts, out_tiled, out_acc, name, pieces=()):
    T, K = a.shape
    tm, tk = _tile(T, ROWS_TM), _tile(K, MM_TK)
    nm, nk = T // tm, K // tk
    npc, nt, ncst, no, na = len(pieces), len(tiled), len(consts), len(out_tiled), len(out_acc)
    n_in = 2 + nt + ncst

    def body(*refs):
        a_ref, b_ref = refs[:2]
        t_refs, c_refs = refs[2:2 + nt], refs[2 + nt:n_in]
        o_refs = refs[n_in + npc:n_in + npc + no]
        acc_refs = refs[n_in + npc + no:n_in + npc + no + na]
        n_out = no + na + npc
        res_ref = refs[n_in + npc + n_out]
        plan = _exchange_plan(refs[n_in:n_in + npc], refs[n_in + npc + no + na:n_in + npc + n_out],
                              *refs[n_in + npc + n_out + 1:]) if npc else None
        i, k = pl.program_id(0), pl.program_id(1)

        @pl.when((i == 0) & (k == 0))
        def _():
            for r in acc_refs:
                r[...] = jnp.zeros_like(r)
            if npc:
                _start(plan)

        part = _dot(a_ref[...], b_ref[...], _NN)

        @pl.when(k == 0)
        def _():
            res_ref[...] = part

        @pl.when(k > 0)
        def _():
            res_ref[...] += part

        @pl.when(k == nk - 1)
        def _():
            for r0 in range(0, tm, ROWS_EPI):
                rows = pl.ds(r0, ROWS_EPI)
                outs = epi(res_ref[rows, :], *[r[rows, :] for r in t_refs], *[r[...] for r in c_refs])
                for r, v in zip(o_refs, outs[:no]):
                    r[rows, :] = v.astype(r.dtype)
                for r, v in zip(acc_refs, outs[no:]):
                    r[...] += v

        if npc:
            @pl.when((i == nm - 1) & (k == nk - 1))
            def _():
                _finish(plan)

    row = lambda w: pl.BlockSpec((tm, w), lambda i, k: (i, 0))
    whole = lambda s: pl.BlockSpec(s.shape, lambda i, k: (0, 0))
    res = pl.pallas_call(
        body, name=name, grid=(nm, nk),
        in_specs=[pl.BlockSpec((tm, tk), lambda i, k: (i, k)), pl.BlockSpec((tk, D), lambda i, k: (k, 0))]
                 + [row(t.shape[1]) for t in tiled] + [whole(c) for c in consts] + _hbm_specs(npc),
        out_specs=[row(s.shape[1]) for s in out_tiled] + [whole(s) for s in out_acc] + _hbm_specs(npc),
        out_shape=list(out_tiled) + list(out_acc) + (_exchange_shapes(pieces) if npc else []),
        scratch_shapes=[pltpu.VMEM((tm, D), F32)] + (_exchange_sems(npc) if npc else []),
        compiler_params=_cparams(("arbitrary", "arbitrary")),
    )(a, b, *tiled, *consts, *pieces)
    return res[:no + na], res[no + na:]


def _norm_mod_grads(x, nw, scale, dh):
    r = lax.rsqrt(jnp.mean(x * x, axis=-1, keepdims=True) + EPS)
    y = x * r
    t = dh * (nw * (1.0 + scale))
    dx = r * (t - y * jnp.mean(t * y, axis=-1, keepdims=True))
    dhy = dh * y
    rows = lambda v: jnp.sum(v, axis=0, keepdims=True)
    return dx, rows(dhy) * (1.0 + scale), rows(dhy) * nw, rows(dh)


def _in_proj_dx_norm_bwd(dproj, w_in_pt, x, dres, nw, scale, shift, pieces):
    T = x.shape[0]

    def epi(dh, x, dres, nw, scale, shift):
        dx, dnw, dsc, dsh = _norm_mod_grads(x, nw, scale, dh)
        return dx + dres, dnw, dsc, dsh

    return _matmul_rows(dproj, w_in_pt, epi, [x, dres], [nw, scale, shift], [_sds((T, D))], [_sds((1, D))] * 3,
                        "in_proj_dx_norm1_bwd", pieces)


def _out_proj_resid_norm(mixcat, w_out, x, gate1, nw, scale, shift):
    T = x.shape[0]

    def epi(mixed, x, gate1, nw, scale, shift):
        return (mixed,) + _resid_norm(x, mixed, gate1, nw, scale, shift)

    outs, _ = _matmul_rows(mixcat, w_out, epi, [x], [gate1, nw, scale, shift],
                           [_sds((T, D)), _sds((T, D)), _sds((T, D), BF16)], [], "out_proj_resid_norm2")
    return outs


def _ffn_up_dx_resid_bwd(dab, w_gut, x, mixed, dy, gate1, nw, scale, shift):
    T = x.shape[0]

    def epi(dh2, x, mixed, dy, gate1, nw, scale, shift):
        dx1, dnw, dsc, dsh = _norm_mod_grads(x + gate1 * mixed, nw, scale, dh2)
        dx1 = dx1 + dy
        return dx1, gate1 * dx1, jnp.sum(dx1 * mixed, axis=0, keepdims=True), dnw, dsc, dsh

    outs, _ = _matmul_rows(dab, w_gut, epi, [x, mixed, dy], [gate1, nw, scale, shift],
                           [_sds((T, D)), _sds((T, D), BF16)], [_sds((1, D))] * 4, "ffn_up_dx_resid_norm2_bwd")
    return outs


def _ffn_down_loss(act, w_down, x1, target, gate2):
    T = x1.shape[0]

    def epi(ffn, x1, target, gate2):
        y = x1 + gate2 * ffn
        err = y - target
        loss = 0.5 * jnp.sum(jnp.sum(err * err, axis=1, keepdims=True), axis=0, keepdims=True) / D
        dy = err * (1.0 / D)
        return dy, gate2 * dy, jnp.sum(dy * ffn, axis=0, keepdims=True), jnp.broadcast_to(loss, (1, LANE))

    outs, _ = _matmul_rows(act, w_down, epi, [x1, target], [gate2], [_sds((T, D)), _sds((T, D), BF16)],
                           [_sds((1, D)), _sds((1, LANE))], "ffn_down_loss")
    return outs


def _resid_norm(x, mixed, gate1, nw, scale, shift):
    x1 = x + gate1 * mixed
    return x1, _norm_mod(x1, nw, scale, shift)


FFN_BLK = 256
FFN_TM = 2048


def _interleave_gate_up(gate_t, up_t):
    blocks = lambda t: t.reshape(DFF // FFN_BLK, 1, FFN_BLK, D)
    return jnp.concatenate([blocks(gate_t), blocks(up_t)], axis=1).reshape(2 * DFF, D)


def _split_gate_up(g):
    g = g.reshape(DFF // FFN_BLK, 2, FFN_BLK, D)
    return g[:, 0].reshape(DFF, D), g[:, 1].reshape(DFF, D)


def _ffn_up_act(h2, w_gut):
    T = h2.shape[0]
    tm = _tile(T, FFN_TM)

    def body(h_ref, w_ref, ab_ref, act_ref):
        ab = _dot(h_ref[...], w_ref[...], _NT)
        ab_ref[...] = ab
        act_ref[...] = (_silu(ab[:, :FFN_BLK]) * ab[:, FFN_BLK:]).astype(act_ref.dtype)

    return pl.pallas_call(
        body, name="ffn_up_act", grid=(T // tm, DFF // FFN_BLK),
        in_specs=[pl.BlockSpec((tm, D), lambda i, j: (i, 0)), pl.BlockSpec((2 * FFN_BLK, D), lambda i, j: (j, 0))],
        out_specs=[pl.BlockSpec((tm, 2 * FFN_BLK), lambda i, j: (i, j)), pl.BlockSpec((tm, FFN_BLK), lambda i, j: (i, j))],
        out_shape=[_sds((T, 2 * DFF)), _sds((T, DFF), BF16)],
        compiler_params=_cparams(("parallel", "parallel")),
    )(h2, w_gut)


def _ffn_down_dx_act(dffn, w_down, ab):
    T = dffn.shape[0]
    tm = _tile(T, FFN_TM)

    def body(d_ref, w_ref, ab_ref, o_ref):
        dact = _dot(d_ref[...], w_ref[...], _NT)
        a, b = ab_ref[:, :FFN_BLK], ab_ref[:, FFN_BLK:]
        s = _sigmoid(a)
        da = dact * b * (s * (1.0 + a * (1.0 - s)))
        db = dact * (a * s)
        o_ref[...] = jnp.concatenate([da, db], axis=1).astype(o_ref.dtype)

    return pl.pallas_call(
        body, name="ffn_down_dx_act", grid=(T // tm, DFF // FFN_BLK),
        in_specs=[pl.BlockSpec((tm, D), lambda i, j: (i, 0)), pl.BlockSpec((FFN_BLK, D), lambda i, j: (j, 0)),
                  pl.BlockSpec((tm, 2 * FFN_BLK), lambda i, j: (i, j))],
        out_specs=pl.BlockSpec((tm, 2 * FFN_BLK), lambda i, j: (i, j)),
        out_shape=_sds((T, 2 * DFF), BF16),
        compiler_params=_cparams(("parallel", "parallel")),
    )(dffn, w_down, ab)


def _round_bf16(x):
    return x.astype(BF16).astype(F32)


def _shift_down(x, s, rows):
    if s == 0:
        return x
    return jnp.where(rows >= s, pltpu.roll(x, s, 0), 0.0)


def _shift_up(x, s, rows, T):
    if s == 0:
        return x
    return jnp.where(rows < T - s, pltpu.roll(x, T - s, 0), 0.0)


def _conv_fwd(proj, conv_w):
    T = proj.shape[0]
    ncol = 3 * GW // LANE

    def body(x_ref, w_ref, o_ref):
        x = _round_bf16(x_ref[...])
        rows = lax.broadcasted_iota(jnp.int32, x.shape, 0)
        acc = jnp.zeros_like(x)
        for j in range(CONVW):
            acc = acc + _round_bf16(w_ref[pl.ds(j, 1), :]) * _shift_down(x, CONVW - 1 - j, rows)
        o_ref[0], o_ref[1] = _split_pair(_silu(acc))

    return pl.pallas_call(
        body, name="conv_fwd", grid=(ncol,),
        in_specs=[pl.BlockSpec((T, LANE), lambda j: (0, j)), pl.BlockSpec((CONVW, LANE), lambda j: (0, j))],
        out_specs=pl.BlockSpec((2, T, HD), lambda j: (j, 0, 0)),
        out_shape=_sds((3 * GH, T, HD)),
        compiler_params=_cparams(("parallel",)),
    )(proj, conv_w)


RELAYOUT_TM = 4096


def _split_pair(y):
    return y[:, :HD], pltpu.roll(y, HD, 1)[:, :HD]


def _merge_pair(a, b):
    return jnp.concatenate([a, b], axis=1)


def _split_heads(x, col_block0, nheads, name):
    T = x.shape[0]
    tm = _tile(T, RELAYOUT_TM)

    def body(x_ref, o_ref):
        a, b = _split_pair(x_ref[...])
        o_ref[0] = a
        o_ref[1] = b

    return pl.pallas_call(
        body, name=name, grid=(nheads // 2, T // tm),
        in_specs=[pl.BlockSpec((tm, LANE), lambda j, i: (i, col_block0 + j))],
        out_specs=pl.BlockSpec((2, tm, HD), lambda j, i: (j, i, 0)),
        out_shape=_sds((nheads, T, HD), x.dtype),
        compiler_params=_cparams(("parallel", "parallel")),
    )(x)


def _merge_heads(hm, out_dtype, name, into=None, col_block0=0, head0=0, nheads=None):
    T = hm.shape[1]
    nheads = hm.shape[0] if nheads is None else nheads
    tm = _tile(T, RELAYOUT_TM)

    def body(*refs):
        h_ref, o_ref = refs[0], refs[-1]
        o_ref[...] = _merge_pair(h_ref[0], h_ref[1]).astype(o_ref.dtype)

    in_specs = [pl.BlockSpec((2, tm, HD), lambda j, i: (head0 // 2 + j, i, 0))]
    args = [hm]
    if into is None:
        out_shape = _sds((T, HD * nheads), out_dtype)
        aliases = {}
    else:
        out_shape = _sds(into.shape, into.dtype)
        in_specs.append(pl.BlockSpec(memory_space=pl.ANY))
        args.append(into)
        aliases = {1: 0}
    return pl.pallas_call(
        body, name=name, grid=(nheads // 2, T // tm),
        in_specs=in_specs,
        out_specs=pl.BlockSpec((tm, LANE), lambda j, i: (i, col_block0 + j)),
        out_shape=out_shape, input_output_aliases=aliases,
        compiler_params=_cparams(("parallel", "parallel")),
    )(*args)


def _conv_bwd(proj, conv_w, dqc):
    T = proj.shape[0]
    ncol = 3 * GW // LANE

    def body(x_ref, w_ref, d_ref, dx_ref, dw_ref):
        x = _round_bf16(x_ref[...])
        rows = lax.broadcasted_iota(jnp.int32, x.shape, 0)
        xs = [_shift_down(x, CONVW - 1 - j, rows) for j in range(CONVW)]
        w = [_round_bf16(w_ref[pl.ds(j, 1), :]) for j in range(CONVW)]
        pre = jnp.zeros_like(x)
        for j in range(CONVW):
            pre = pre + w[j] * xs[j]
        s = _sigmoid(pre)
        dpre = _round_bf16(_merge_pair(d_ref[0], d_ref[1]) * (s * (1.0 + pre * (1.0 - s))))
        dx = jnp.zeros_like(x)
        for j in range(CONVW):
            dx = dx + w[j] * _shift_up(dpre, CONVW - 1 - j, rows, T)
            dw_ref[pl.ds(j, 1), :] = jnp.sum(dpre * xs[j], axis=0, keepdims=True)
        dx_ref[...] = dx.astype(dx_ref.dtype)

    return pl.pallas_call(
        body, name="conv_bwd", grid=(ncol,),
        in_specs=[pl.BlockSpec((T, LANE), lambda j: (0, j)), pl.BlockSpec((CONVW, LANE), lambda j: (0, j)),
                  pl.BlockSpec((2, T, HD), lambda j: (j, 0, 0))],
        out_specs=[pl.BlockSpec((T, LANE), lambda j: (0, j)), pl.BlockSpec((CONVW, LANE), lambda j: (0, j))],
        out_shape=[_sds((T, NP), BF16), _sds((CONVW, 3 * GW))],
        compiler_params=_cparams(("parallel",)),
    )(proj, conv_w, dqc)


def _gdn_prep(kit, q, k, v, ga, gb, alog, dtb, t_inv=None):
    C = CHUNK
    ri = lax.broadcasted_iota(jnp.int32, (C, C), 0)
    ci = lax.broadcasted_iota(jnp.int32, (C, C), 1)
    causal = ri >= ci
    strict = ri > ci
    eye = (ri == ci).astype(F32)
    lower = causal.astype(F32)
    upper = (ri <= ci).astype(F32)

    a = ga + dtb
    softplus = jnp.maximum(a, 0.0) + jnp.log(1.0 + jnp.exp(-jnp.abs(a)))
    g_row = -jnp.exp(alog) * softplus
    beta_row = _sigmoid(gb)
    g_col = jnp.sum(eye * g_row, axis=2, keepdims=True)
    beta_col = jnp.sum(eye * beta_row, axis=2, keepdims=True)
    G_col = jnp.sum(lower * g_row, axis=2, keepdims=True)
    G_row = jnp.sum(upper * g_col, axis=1, keepdims=True)
    G_last = jnp.sum(g_row, axis=2, keepdims=True)
    decay = jnp.exp(jnp.where(causal, G_col - G_row, -1e30))

    qn = q * lax.rsqrt(jnp.sum(q * q, axis=-1, keepdims=True) + EPS) * (HD ** -0.5)
    kn = k * lax.rsqrt(jnp.sum(k * k, axis=-1, keepdims=True) + EPS)
    kb = kn * beta_col
    A = jnp.where(strict, kit.nt(kb, kn) * decay, 0.0)
    Tm = kit.inv(A, t_inv)
    eG = jnp.exp(G_col)
    u = kit.nn3(Tm, v * beta_col)
    w = kit.nn3(Tm, kb * eG)
    qk = jnp.where(causal, kit.nt(qn, kn) * decay, 0.0)
    q_dec = qn * eG
    k_dec = kn * jnp.exp(G_last - G_col)
    dec = jnp.exp(G_last)
    return u, w, qk, q_dec, k_dec, dec, Tm


def _gdn_out(o, z, nw):
    return _rms(o, nw) * _silu(z)


GDN_CB = 4


def _gdn_specs(T, blk):
    TB = GDN_CB * CHUNK
    seq = lambda grp: pl.BlockSpec((GH, TB, HD), lambda i, grp=grp: (grp, blk(i), 0))
    row = lambda grp: pl.BlockSpec((GH, GDN_CB, 1, CHUNK), lambda i, grp=grp: (grp, blk(i), 0, 0))
    per_head = pl.BlockSpec((GH, 1, CHUNK), lambda i: (0, 0, 0))
    whole = pl.BlockSpec((1, HD), lambda i: (0, 0))
    state = pl.BlockSpec((GH, GDN_CB, HD, HD), lambda i: (0, blk(i), 0, 0))
    return seq, row, per_head, whole, state


def _gdn_load(seq_refs, row_refs, head_refs):
    chunks = lambda r: jnp.concatenate([r[:, pl.ds(cb * CHUNK, CHUNK), :] for cb in range(GDN_CB)], axis=0)
    rows = lambda r: jnp.concatenate([r[:, cb] for cb in range(GDN_CB)], axis=0)
    heads = lambda r: jnp.concatenate([r[...]] * GDN_CB, axis=0)
    return [chunks(r) for r in seq_refs], [rows(r) for r in row_refs], [heads(r) for r in head_refs]


def _gdn_fwd(qkv_hm, zs_hm, gab, alog_b, dtb_b, nw, shards):
    T = qkv_hm.shape[1]
    N = T // CHUNK
    nblk = N // GDN_CB
    ns = len(shards)
    seq, row, per_head, whole, state = _gdn_specs(T, lambda i: i)
    kit = _Kit(False)

    def body(*refs):
        q_ref, k_ref, v_ref, z_ref, ga_ref, gb_ref, al_ref, dt_ref, nw_ref = refs[:9]
        o_ref, S_ref, T_ref = refs[9 + ns:12 + ns]
        S_scr = refs[12 + 2 * ns]
        plan = _gather_plan(refs[9:9 + ns], refs[12 + ns:12 + 2 * ns], *refs[13 + 2 * ns:])

        @pl.when(pl.program_id(0) == 0)
        def _():
            S_scr[...] = jnp.zeros_like(S_scr)
            _start(plan)

        (q, k, v, z), (ga, gb), (al, dt) = _gdn_load((q_ref, k_ref, v_ref, z_ref), (ga_ref, gb_ref), (al_ref, dt_ref))
        u, w, qk, q_dec, k_dec, dec, t_inv = _gdn_prep(kit, q, k, v, ga, gb, al, dt)
        S = S_scr[...]
        for cb in range(GDN_CB):
            hs = slice(cb * GH, (cb + 1) * GH)
            S_ref[:, cb] = S
            T_ref[:, cb] = t_inv[hs]
            v_new = u[hs] - kit.nn(w[hs], S)
            o = kit.nn(q_dec[hs], S) + kit.nn(qk[hs], v_new)
            S = S * dec[hs] + kit.tn(k_dec[hs], v_new)
            o_ref[:, pl.ds(cb * CHUNK, CHUNK), :] = _gdn_out(o, z[hs], nw_ref[...])
        S_scr[...] = S

        @pl.when(pl.program_id(0) == nblk - 1)
        def _():
            _finish(plan)

    res = pl.pallas_call(
        body, name="gdn_fwd", grid=(nblk,),
        in_specs=[seq(0), seq(1), seq(2), seq(0), row(0), row(1), per_head, per_head, whole] + _hbm_specs(ns),
        out_specs=[seq(0), state, state] + _hbm_specs(ns),
        out_shape=[_sds((GH + SQH, T, HD)), _sds((GH, N, HD, HD)), _sds((GH, N, CHUNK, CHUNK))]
                  + _gather_shapes(shards),
        scratch_shapes=[pltpu.VMEM((GH, HD, HD), F32)] + _gather_sems(ns),
        compiler_params=_cparams(("arbitrary",)),
    )(qkv_hm, qkv_hm, qkv_hm, zs_hm, gab, gab, alog_b, dtb_b, nw, *shards)
    return res[0], (res[1], res[2]), res[3:]


def _gdn_bwd(qkv_hm, zs_hm, gab, alog_b, dtb_b, nw, S_all, do, pieces):
    T = qkv_hm.shape[1]
    N = T // CHUNK
    nblk = N // GDN_CB
    npc = len(pieces)
    dkit, kit = _Kit(True), _Kit(False)
    rseq, rrow, per_head, whole, rstate = _gdn_specs(T, lambda i: nblk - 1 - i)

    def body(*refs):
        q_ref, k_ref, v_ref, z_ref, ga_ref, gb_ref, al_ref, dt_ref, nw_ref, S_ref, T_ref, do_ref = refs[:12]
        dqkv_ref, dz_ref, dga_ref, dgb_ref, dal_ref, ddt_ref, dnw_ref = refs[12 + npc:19 + npc]
        dS_scr = refs[19 + 2 * npc]
        plan = _exchange_plan(refs[12:12 + npc], refs[19 + npc:19 + 2 * npc], *refs[20 + 2 * npc:])

        @pl.when(pl.program_id(0) == 0)
        def _():
            dS_scr[...] = jnp.zeros_like(dS_scr)
            dal_ref[...] = jnp.zeros_like(dal_ref)
            ddt_ref[...] = jnp.zeros_like(ddt_ref)
            dnw_ref[...] = jnp.zeros_like(dnw_ref)
            _start(plan)

        (q, k, v, z, dout), (ga, gb), (al, dt) = _gdn_load((q_ref, k_ref, v_ref, z_ref, do_ref), (ga_ref, gb_ref),
                                                          (al_ref, dt_ref))
        S_in = jnp.concatenate([S_ref[:, cb] for cb in range(GDN_CB)], axis=0)
        t_inv = jnp.concatenate([T_ref[:, cb] for cb in range(GDN_CB)], axis=0)
        prep = lambda *a: _gdn_prep(dkit, *a, t_inv=t_inv)[:6]
        (u, w, qk, q_dec, k_dec, dec), prep_vjp = jax.vjp(prep, q, k, v, ga, gb, al, dt)
        v_new = u - kit.nn(w, S_in)
        o = kit.nn(q_dec, S_in) + kit.nn(qk, v_new)
        _, out_vjp = jax.vjp(_gdn_out, o, z, nw_ref[...])
        do, dz, dnw = out_vjp(dout)
        dvn_part = kit.tn(qk, do)
        dS_part = kit.tn(q_dec, do)
        dS = dS_scr[...]
        dS_out, dvn = [None] * GDN_CB, [None] * GDN_CB
        for cb in reversed(range(GDN_CB)):
            hs = slice(cb * GH, (cb + 1) * GH)
            dS_out[cb] = dS
            dvn[cb] = dvn_part[hs] + kit.nn(k_dec[hs], dS)
            dS = dS * dec[hs] + dS_part[hs] - kit.tn(w[hs], dvn[cb])
        dS_scr[...] = dS
        dS_out = jnp.concatenate(dS_out, axis=0)
        dvn = jnp.concatenate(dvn, axis=0)
        ddec = jnp.sum(jnp.sum(S_in * dS_out, axis=2, keepdims=True), axis=1, keepdims=True)
        cts = (dvn, -kit.nt(dvn, S_in), kit.nt(do, v_new), kit.nt(do, S_in), kit.nt(v_new, dS_out), ddec)
        dq, dk, dv, dga, dgb, dal, ddt = prep_vjp(cts)
        lanesum = lambda t: jnp.broadcast_to(jnp.sum(t, axis=2, keepdims=True), t.shape)
        for cb in range(GDN_CB):
            hs = slice(cb * GH, (cb + 1) * GH)
            sl = pl.ds(cb * CHUNK, CHUNK)
            dqkv_ref[pl.ds(0, GH), sl, :] = dq[hs]
            dqkv_ref[pl.ds(GH, GH), sl, :] = dk[hs]
            dqkv_ref[pl.ds(2 * GH, GH), sl, :] = dv[hs]
            dz_ref[:, sl, :] = dz[hs]
            dga_ref[:, cb] = dga[hs]
            dgb_ref[:, cb] = dgb[hs]
            dal_ref[...] += lanesum(dal[hs])
            ddt_ref[...] += lanesum(ddt[hs])
        dnw_ref[...] += dnw

        @pl.when(pl.program_id(0) == nblk - 1)
        def _():
            _finish(plan)

    res = pl.pallas_call(
        body, name="gdn_bwd", grid=(nblk,),
        in_specs=[rseq(0), rseq(1), rseq(2), rseq(0), rrow(0), rrow(1), per_head, per_head, whole, rstate, rstate,
                  rseq(0)] + _hbm_specs(npc),
        out_specs=[pl.BlockSpec((3 * GH, GDN_CB * CHUNK, HD), lambda i: (0, nblk - 1 - i, 0)), rseq(0), rrow(0),
                   rrow(0), per_head, per_head, whole] + _hbm_specs(npc),
        out_shape=[_sds((3 * GH, T, HD)), _sds((GH + 4 + SWA_GRAD_HEADS, T, HD))] + [_sds((GH, N, 1, CHUNK))] * 2
                  + [_sds((GH, 1, CHUNK))] * 2 + [_sds((1, HD))] + _exchange_shapes(pieces),
        scratch_shapes=[pltpu.VMEM((GH, HD, HD), F32)] + _exchange_sems(npc),
        compiler_params=_cparams(("arbitrary",), GDN_BWD_VMEM),
    )(qkv_hm, qkv_hm, qkv_hm, zs_hm, gab, gab, alog_b, dtb_b, nw, S_all[0], S_all[1], do, *pieces)
    return res[:7], res[7:]


def _swa_heads(kit, first, q, kp, kc, vp, vc, qnw, knw, sink, slope):
    W = WIN
    ri = lax.broadcasted_iota(jnp.int32, (W, W), 0)
    ci = lax.broadcasted_iota(jnp.int32, (W, W), 1)
    mask_c = ri >= ci
    mask_p = ci > ri + first * W
    dist_c = (ri - ci).astype(F32)
    dist_p = (ri - ci + W).astype(F32)
    kpn = _rms(kp, knw)
    kcn = _rms(kc, knw)
    qn = _rms(q, qnw)
    sc = jnp.where(mask_c, kit.nt(qn, kcn) * (HD ** -0.5) - slope * dist_c, -1e30)
    sp = jnp.where(mask_p, kit.nt(qn, kpn) * (HD ** -0.5) - slope * dist_p, -1e30)
    m = jnp.maximum(jnp.maximum(jnp.max(sc, axis=-1, keepdims=True), jnp.max(sp, axis=-1, keepdims=True)), sink)
    m = lax.stop_gradient(m)
    pc = jnp.exp(sc - m)
    pp = jnp.exp(sp - m)
    den = jnp.sum(pc, axis=-1, keepdims=True) + jnp.sum(pp, axis=-1, keepdims=True) + jnp.exp(sink - m)
    inv = 1.0 / den
    return kit.nn(pc * inv, vc) + kit.nn(pp * inv, vp)


def _swa_grads(kit, first, q, kp, kc, vp, vc, qnw, knw, sink, slope, do):
    W = WIN
    ri = lax.broadcasted_iota(jnp.int32, (W, W), 0)
    ci = lax.broadcasted_iota(jnp.int32, (W, W), 1)
    mask_c = ri >= ci
    mask_p = ci > ri + first * W
    dist_c = (ri - ci).astype(F32)
    dist_p = (ri - ci + W).astype(F32)
    scale = HD ** -0.5
    kpn, kp_vjp = jax.vjp(_rms, kp, knw)
    kcn, kc_vjp = jax.vjp(_rms, kc, knw)
    qn, q_vjp = jax.vjp(_rms, q, qnw)
    sc = jnp.where(mask_c, kit.nt(qn, kcn) * scale - slope * dist_c, -1e30)
    sp = jnp.where(mask_p, kit.nt(qn, kpn) * scale - slope * dist_p, -1e30)
    m = jnp.maximum(jnp.maximum(jnp.max(sc, axis=-1, keepdims=True), jnp.max(sp, axis=-1, keepdims=True)), sink)
    ec = jnp.exp(sc - m)
    ep = jnp.exp(sp - m)
    es = jnp.exp(sink - m)
    inv = 1.0 / (jnp.sum(ec, axis=-1, keepdims=True) + jnp.sum(ep, axis=-1, keepdims=True) + es)
    pc, pp = ec * inv, ep * inv
    dpc, dpp = kit.nt(do, vc), kit.nt(do, vp)
    delta = jnp.sum(dpc * pc, axis=-1, keepdims=True) + jnp.sum(dpp * pp, axis=-1, keepdims=True)
    dsc = pc * (dpc - delta) * scale
    dsp = pp * (dpp - delta) * scale
    dq, dqnw = q_vjp(kit.nn(dsc, kcn) + kit.nn(dsp, kpn))
    dkc, dknw_c = kc_vjp(kit.tn(dsc, qn))
    dkp, dknw_p = kp_vjp(kit.tn(dsp, qn))
    return dq, dkp, dkc, kit.tn(pp, do), kit.tn(pc, do), dqnw, dknw_c + dknw_p, -(es * inv) * delta


def _per_query_head(kv_ref):
    return jnp.concatenate([kv_ref[pl.ds(h // SGRP, 1)] for h in range(SQH)], axis=0)


def _per_kv_head(d):
    return jnp.concatenate([jnp.sum(d[g * SGRP:(g + 1) * SGRP], axis=0, keepdims=True) for g in range(SKVH)], axis=0)


def _swa_specs(blk):
    qspec = pl.BlockSpec((SQH, WIN, HD), lambda i: (1, blk(i), 0))
    cur = lambda grp: pl.BlockSpec((SKVH, WIN, HD), lambda i, grp=grp: (grp, blk(i), 0))
    prev = lambda grp: pl.BlockSpec((SKVH, WIN, HD), lambda i, grp=grp: (grp, jnp.maximum(blk(i) - 1, 0), 0))
    whole = pl.BlockSpec((1, HD), lambda i: (0, 0))
    col = pl.BlockSpec((SQH, WIN, 1), lambda i: (0, 0, 0))
    ospec = pl.BlockSpec((SQH, WIN, HD), lambda i: (0, blk(i), 0))
    return qspec, cur, prev, whole, col, ospec


def _swa_fwd(zs_hm, qnw, knw, sinks_col, slopes_col, o_buf, shards):
    T = zs_hm.shape[1]
    NB = T // WIN
    ns = len(shards)
    kit = _Kit(False)
    qspec, cur, prev, whole, col, _ = _swa_specs(lambda i: i)

    def body(*refs):
        q_ref, kp_ref, kc_ref, vp_ref, vc_ref, qnw_ref, knw_ref, s_ref, sl_ref = refs[:9]
        o_ref = refs[10 + ns]
        plan = _gather_plan(refs[10:10 + ns], refs[11 + ns:11 + 2 * ns], *refs[11 + 2 * ns:])

        @pl.when(pl.program_id(0) == 0)
        def _():
            _start(plan)

        first = (pl.program_id(0) == 0).astype(jnp.int32)
        o_ref[...] = _swa_heads(kit, first, q_ref[...], _per_query_head(kp_ref), _per_query_head(kc_ref),
                                _per_query_head(vp_ref), _per_query_head(vc_ref), qnw_ref[...], knw_ref[...],
                                s_ref[...], sl_ref[...])

        @pl.when(pl.program_id(0) == NB - 1)
        def _():
            _finish(plan)

    res = pl.pallas_call(
        body, name="swa_fwd", grid=(NB,),
        in_specs=[qspec, prev(8), cur(8), prev(9), cur(9), whole, whole, col, col] + _hbm_specs(1 + ns),
        out_specs=[pl.BlockSpec((SQH, WIN, HD), lambda i: (1, i, 0))] + _hbm_specs(ns),
        out_shape=[_sds(o_buf.shape)] + _gather_shapes(shards),
        input_output_aliases={9: 0},
        scratch_shapes=_gather_sems(ns),
        compiler_params=_cparams(("arbitrary",)),
    )(zs_hm, zs_hm, zs_hm, zs_hm, zs_hm, qnw, knw, sinks_col, slopes_col, o_buf, *shards)
    return res[0], res[1:]


SWA_GRAD_HEADS = SQH + 2 * SKVH


def _swa_bwd(zs_hm, qnw, knw, sinks_col, slopes_col, dmix_hm, d_buf):
    T = zs_hm.shape[1]
    NB = T // WIN
    kit = _Kit(False)
    qspec, cur, prev, whole, col, _ = _swa_specs(lambda i: NB - 1 - i)

    def body(q_ref, kp_ref, kc_ref, vp_ref, vc_ref, qnw_ref, knw_ref, s_ref, sl_ref, do_ref, buf_ref,
             d_ref, dqnw_ref, dknw_ref, ds_ref, ck_scr, cv_scr):
        dq_ref = d_ref.at[pl.ds(0, SQH)]
        dk_ref = d_ref.at[pl.ds(SQH, SKVH)]
        dv_ref = d_ref.at[pl.ds(SQH + SKVH, SKVH)]
        i = pl.program_id(0)
        first = (i == NB - 1).astype(jnp.int32)

        @pl.when(i == 0)
        def _():
            ck_scr[...] = jnp.zeros_like(ck_scr)
            cv_scr[...] = jnp.zeros_like(cv_scr)
            ds_ref[...] = jnp.zeros_like(ds_ref)
            dqnw_ref[...] = jnp.zeros_like(dqnw_ref)
            dknw_ref[...] = jnp.zeros_like(dknw_ref)

        dq, dkp, dkc, dvp, dvc, dqnw, dknw, dsink = _swa_grads(
            kit, first, q_ref[...], _per_query_head(kp_ref), _per_query_head(kc_ref), _per_query_head(vp_ref),
            _per_query_head(vc_ref), qnw_ref[...], knw_ref[...], s_ref[...], sl_ref[...], do_ref[...])
        dq_ref[...] = dq
        dk_ref[...] = _per_kv_head(dkc) + ck_scr[...]
        dv_ref[...] = _per_kv_head(dvc) + cv_scr[...]
        ck_scr[...] = _per_kv_head(dkp)
        cv_scr[...] = _per_kv_head(dvp)
        dqnw_ref[...] += dqnw
        dknw_ref[...] += dknw
        ds_ref[...] += jnp.broadcast_to(jnp.sum(dsink, axis=1, keepdims=True), dsink.shape)

    dospec = pl.BlockSpec((SQH, WIN, HD), lambda i: (1, NB - 1 - i, 0))
    dspec = pl.BlockSpec((SWA_GRAD_HEADS, WIN, HD), lambda i: (1, NB - 1 - i, 0))
    res = pl.pallas_call(
        body, name="swa_bwd", grid=(NB,),
        in_specs=[qspec, prev(8), cur(8), prev(9), cur(9), whole, whole, col, col, dospec] + _hbm_specs(1),
        out_specs=[dspec, whole, whole, col],
        out_shape=[_sds(d_buf.shape), _sds((1, HD)), _sds((1, HD)), _sds((SQH, WIN, 1))],
        input_output_aliases={10: 0},
        scratch_shapes=[pltpu.VMEM((SKVH, WIN, HD), F32), pltpu.VMEM((SKVH, WIN, HD), F32)],
        compiler_params=_cparams(("arbitrary",)),
    )(zs_hm, zs_hm, zs_hm, zs_hm, zs_hm, qnw, knw, sinks_col, slopes_col, dmix_hm, d_buf)
    return res


GAB0 = 3 * GW + 1280


W_IN_ROWS = PROJ // N_CHIP
W_IN_ROWS_PAD = 736


def _permute_w_in_t(w_in_t):
    return jnp.concatenate([w_in_t[:4 * GW], w_in_t[4 * GW + 2 * GH:], w_in_t[4 * GW:4 * GW + 2 * GH],
                            jnp.zeros((NP - PROJ, D), w_in_t.dtype)], axis=0)


def _w_in_grad_pieces(g_t):
    g = jnp.concatenate([g_t[:4 * GW], g_t[GAB0:GAB0 + 2 * GH], g_t[4 * GW:GAB0]], axis=0)
    g = jnp.pad(g.reshape(N_CHIP, W_IN_ROWS, D), ((0, 0), (0, W_IN_ROWS_PAD - W_IN_ROWS), (0, 0)))
    return g.reshape(N_CHIP, 2, W_IN_ROWS_PAD // 2, D)


def _pieces_by_rows(g):
    return g.reshape(N_CHIP, 2, g.shape[0] // (2 * N_CHIP), D)


def _local_step(x, target, mod, n1w, w_in_pt, conv_w, alog, dtb, gnw, qnw, knw, sinks, n2w, shards):
    sh_out, sh_gate, sh_up, sh_down = shards
    T = x.shape[0]
    N = T // CHUNK
    shift1, scale1, gate1, shift2, scale2, gate2 = [mod[:, i * D:(i + 1) * D] for i in range(6)]

    h, proj, (a_out,) = _norm_in_proj(x, n1w, scale1, shift1, w_in_pt, [sh_out])
    w_out = a_out.reshape(D, D)
    qkv_hm = _conv_fwd(proj, conv_w)
    zs_hm = _split_heads(proj, 3 * GW // LANE, 20, "split_zs")
    gab = proj[:, GAB0:GAB0 + 2 * GH].T.reshape(2 * GH, N, 1, CHUNK)
    alog_b = jnp.broadcast_to(alog.reshape(GH, 1, 1), (GH, 1, CHUNK))
    dtb_b = jnp.broadcast_to(dtb.reshape(GH, 1, 1), (GH, 1, CHUNK))
    sinks_col = jnp.broadcast_to(sinks.reshape(SQH, 1, 1), (SQH, WIN, 1))
    o_hm, S_all, (a_gate, a_up) = _gdn_fwd(qkv_hm, zs_hm, gab, alog_b, dtb_b, gnw, [sh_gate, sh_up])
    w_gut = _interleave_gate_up(a_gate.reshape(DFF, D), a_up.reshape(DFF, D))
    slopes = 2.0 ** (-8.0 * (jnp.arange(SQH, dtype=F32) + 1.0) / SQH)
    slopes_col = jnp.broadcast_to(slopes.reshape(SQH, 1, 1), (SQH, WIN, 1))
    o_hm, (a_down,) = _swa_fwd(zs_hm, qnw, knw, sinks_col, slopes_col, o_hm, [sh_down])
    w_down = a_down.reshape(DFF, D)
    mixcat = _merge_heads(o_hm, BF16, "merge_mix")
    mixed, x1, h2 = _out_proj_resid_norm(mixcat, w_out, x, gate1, n2w, scale2, shift2)
    ab, act = _ffn_up_act(h2, w_gut)
    dy, dffn, dgate2, loss = _ffn_down_loss(act, w_down, x1, target, gate2)

    dab = _ffn_down_dx_act(dffn, w_down, ab)
    g_w_down = _matmul(act, dffn, ta=True, out_dtype=BF16, name="ffn_down_dw")
    g_w_gut = _matmul(dab, h2, ta=True, out_dtype=BF16, name="ffn_up_dw")
    dx1, dmixed, dgate1, dn2w, dscale2, dshift2 = _ffn_up_dx_resid_bwd(dab, w_gut, x, mixed, dy, gate1, n2w, scale2,
                                                                       shift2)
    g_w_out = _matmul(mixcat, dmixed, ta=True, out_dtype=BF16, name="out_proj_dw")
    dmix_hm = _split_heads(_matmul(dmixed, w_out, tb=True, name="out_proj_dx"), 0, GH + SQH, "split_dmix")
    g_gate_t, g_up_t = _split_gate_up(g_w_gut)
    pieces = [_pieces_by_rows(g_w_out), _pieces_by_rows(g_gate_t), _pieces_by_rows(g_up_t),
              _pieces_by_rows(g_w_down)]
    (dqkv_hm, d_hm, dga, dgb, dalog, ddtb, dgnw), recv = _gdn_bwd(qkv_hm, zs_hm, gab, alog_b, dtb_b, gnw, S_all,
                                                                  dmix_hm, pieces)
    d_hm, dqnw, dknw, dsinks = _swa_bwd(zs_hm, qnw, knw, sinks_col, slopes_col, dmix_hm, d_hm)
    dproj, dconv = _conv_bwd(proj, conv_w, dqkv_hm)
    dproj = _merge_heads(d_hm, BF16, "merge_dz", into=dproj, col_block0=3 * GW // LANE, head0=0, nheads=GH)
    dproj = _merge_heads(d_hm, BF16, "merge_dswa", into=dproj, col_block0=4 * GW // LANE, head0=GH + 4,
                         nheads=SWA_GRAD_HEADS)
    dgab = jnp.concatenate([dga, dgb], axis=0).reshape(2 * GH, T).T.astype(BF16)
    dproj = lax.dynamic_update_slice(dproj, jnp.concatenate([dgab, jnp.zeros((T, NP - PROJ), BF16)], axis=1),
                                     (0, GAB0))
    g_w_in_pt = _matmul(dproj, h, ta=True, out_dtype=BF16, name="in_proj_dw")
    (grad_x, dn1w, dscale1, dshift1), recv_in = _in_proj_dx_norm_bwd(dproj, w_in_pt, x, dx1, n1w, scale1, shift1,
                                                                     [_w_in_grad_pieces(g_w_in_pt)])

    dmod = jnp.concatenate([dshift1, dscale1, dgate1, dshift2, dscale2, dgate2], axis=1)
    big = list(recv_in) + list(recv)
    small = dict(mod=dmod, norm1_w=dn1w, norm2_w=dn2w, conv_w=dconv, a_log=dalog[:, 0, 0], dt_bias=ddtb[:, 0, 0],
                 gdn_norm_w=dgnw, q_norm_w=dqnw, k_norm_w=dknw, sinks=dsinks[:, 0, 0])
    return loss, grad_x, big, small


def _adamw(w, g, m, v):
    m2 = ADAM_B1 * m + (1.0 - ADAM_B1) * g
    v2 = ADAM_B2 * v + (1.0 - ADAM_B2) * (g * g)
    m_hat = m2 / (1.0 - ADAM_B1 ** ADAM_STEP)
    v_hat = v2 / (1.0 - ADAM_B2 ** ADAM_STEP)
    delta = -ADAM_LR * (m_hat / (jnp.sqrt(v_hat) + ADAM_EPS) + ADAM_WD * w)
    return delta, m2, v2


def _reduce_adamw(recv, w, m, v, name):
    _, R, C = recv.shape
    tc = _tile(C, 256)

    def body(r_ref, w_ref, m_ref, v_ref, o_ref):
        g = r_ref[0].astype(F32)
        for s in range(1, N_DEV):
            g = g + r_ref[s].astype(F32)
        delta, m2, v2 = _adamw(w_ref[...], g, m_ref[...], v_ref[...])
        o_ref[0] = g
        o_ref[1] = delta
        o_ref[2] = m2
        o_ref[3] = v2

    col = pl.BlockSpec((R, tc), lambda j: (0, j))
    return pl.pallas_call(
        body, name=name, grid=(C // tc,),
        in_specs=[pl.BlockSpec((N_DEV, R, tc), lambda j: (0, 0, j)), col, col, col],
        out_specs=pl.BlockSpec((4, R, tc), lambda j: (0, 0, j)),
        out_shape=_sds((4, R, C)),
        compiler_params=_cparams(("parallel",)),
    )(recv, w, m, v)


def _adamw_call(g, w, m, v, name):
    def body(g_ref, w_ref, m_ref, v_ref, o_ref):
        delta, m2, v2 = _adamw(w_ref[...], g_ref[...], m_ref[...], v_ref[...])
        o_ref[0] = delta
        o_ref[1] = m2
        o_ref[2] = v2

    return pl.pallas_call(body, name=name, out_shape=_sds((3,) + g.shape))(g, w, m, v)


ADA_N = 6 * D // N_CHIP
KPAD = 128


def _w_ada_update(c8p, dm, w, m, v):
    tr = 256

    def body(c_ref, dm_ref, w_ref, m_ref, v_ref, g_ref, d_ref, m2_ref, v2_ref):
        g = _raw1(_silu(c_ref[...]), dm_ref[...], _TN)
        delta, m2, v2 = _adamw(w_ref[...], g, m_ref[...], v_ref[...])
        g_ref[...] = g
        d_ref[...] = delta
        m2_ref[...] = m2
        v2_ref[...] = v2

    blk = pl.BlockSpec((tr, ADA_N), lambda i: (i, 0))
    return pl.pallas_call(
        body, name="w_ada_update", grid=(D // tr,),
        in_specs=[pl.BlockSpec((KPAD, tr), lambda i: (0, i)), pl.BlockSpec((KPAD, ADA_N), lambda i: (0, 0)),
                  blk, blk, blk],
        out_specs=[blk] * 4, out_shape=[_sds((D, ADA_N))] * 4,
        compiler_params=_cparams(("parallel",)),
    )(c8p, dm, w, m, v)


def _me():
    return lax.axis_index("x"), lax.axis_index("y"), lax.axis_index("c")


def _peer(k, me):
    mx, my, mc = me
    return (1 - mx if k & 4 else mx, 1 - my if k & 2 else my, 1 - mc if k & 1 else mc)


def _lin(p):
    return 4 * p[0] + 2 * p[1] + p[2]


def _remote(src, dst, ssem, rsem, dev):
    return pltpu.make_async_remote_copy(src_ref=src, dst_ref=dst, send_sem=ssem, recv_sem=rsem,
                                        device_id=dev, device_id_type=MESH)


def _all_gather8(x, name):
    def body(x_ref, out_ref, send_sems, recv_sems):
        me = _me()
        out_ref[_lin(me)] = x_ref[...]
        sends = []
        for k in range(1, N_DEV):
            cp = _remote(x_ref, out_ref.at[_lin(me)], send_sems.at[k - 1], recv_sems.at[k - 1], _peer(k, me))
            cp.start()
            sends.append(cp)
        for k in range(1, N_DEV):
            p = _peer(k, me)
            _remote(x_ref, out_ref.at[_lin(p)], send_sems.at[k - 1], recv_sems.at[k - 1], p).wait_recv()
        for cp in sends:
            cp.wait_send()

    return pl.pallas_call(
        body, name=name,
        out_shape=_sds((N_DEV,) + x.shape, x.dtype),
        in_specs=[pl.BlockSpec(memory_space=pltpu.VMEM)],
        out_specs=pl.BlockSpec(memory_space=pltpu.VMEM),
        scratch_shapes=[pltpu.SemaphoreType.DMA((N_DEV - 1,)), pltpu.SemaphoreType.DMA((N_DEV - 1,))],
    )(x)


def _ag8_plan(src, out, send_sems, recv_sems):
    me = _me()
    sends, recvs = [], []
    for k in range(1, N_DEV):
        p = _peer(k, me)
        sends.append(_remote(src, out.at[_lin(me)], send_sems.at[k - 1], recv_sems.at[k - 1], p))
        recvs.append(_remote(src, out.at[_lin(p)], send_sems.at[k - 1], recv_sems.at[k - 1], p))
    return [], sends, recvs


def _prologue(c_row, conv_sh, w_ada, b_sh, w_in_sh):
    def body(c_ref, cv_ref, wa_ref, b_ref, win_ref, call_ref, cvall_ref, mods_ref, ain_ref, c16_scr, mp_scr,
             c_send, c_recv, cv_send, cv_recv, m_send, m_recv, w_send, w_recv, w_local):
        me = _lin(_me())
        w_plan = _gather_half_plan([win_ref], [ain_ref], w_send, w_recv, w_local)
        _start(w_plan)
        c_plan = _ag8_plan(c_ref, call_ref, c_send, c_recv)
        cv_plan = _ag8_plan(cv_ref, cvall_ref, cv_send, cv_recv)
        call_ref[me] = c_ref[...]
        cvall_ref[me] = cv_ref[...]
        _start(c_plan)
        _start(cv_plan)
        _finish(c_plan)
        c16_scr[...] = jnp.zeros_like(c16_scr)
        for d in range(N_DEV):
            c16_scr[pl.ds(d, 1), :] = call_ref[d]
        mp_scr[...] = _raw1(_silu(c16_scr[...]), wa_ref[...], _NN) + b_ref[...]
        mods_ref[me] = mp_scr[...]
        m_plan = _ag8_plan(mp_scr, mods_ref, m_send, m_recv)
        _start(m_plan)
        _finish(cv_plan)
        _finish(m_plan)
        _finish(w_plan)

    vmem = pl.BlockSpec(memory_space=pltpu.VMEM)
    sems = lambda n: pltpu.SemaphoreType.DMA((n,))
    return pl.pallas_call(
        body, name="prologue",
        in_specs=[vmem] * 4 + _hbm_specs(1), out_specs=[vmem] * 3 + _hbm_specs(1),
        out_shape=[_sds((N_DEV,) + c_row.shape), _sds((N_DEV,) + conv_sh.shape), _sds((N_DEV, 16, ADA_N)),
                   _sds((N_CHIP,) + w_in_sh.shape, w_in_sh.dtype)],
        scratch_shapes=[pltpu.VMEM((16, D), F32), pltpu.VMEM((16, ADA_N), F32)] + [sems(N_DEV - 1)] * 6
                       + _gather_sems(1),
        compiler_params=_cparams(),
    )(c_row, conv_sh, w_ada, b_sh, w_in_sh)


def _hbm_specs(n):
    return [pl.BlockSpec(memory_space=pl.ANY)] * n


def _gather_shapes(shards):
    return [_sds((N_CHIP,) + s.shape, s.dtype) for s in shards]


def _gather_sems(n):
    return [pltpu.SemaphoreType.DMA((3 * n,)), pltpu.SemaphoreType.DMA((3 * n,)), pltpu.SemaphoreType.DMA((n,))]


def _gather_plan(ins, outs, send_sems, recv_sems, local_sems):
    mx, my, mc = _me()
    chips = [(1 - mx, my), (mx, 1 - my), (1 - mx, 1 - my)]
    local, sends, recvs = [], [], []
    for a in range(len(ins)):
        local.append(pltpu.make_async_copy(ins[a], outs[a].at[2 * mx + my], local_sems.at[a]))
        for k, (px, py) in enumerate(chips):
            sems = (send_sems.at[3 * a + k], recv_sems.at[3 * a + k], (px, py, mc))
            sends.append(_remote(ins[a], outs[a].at[2 * mx + my], *sems))
            recvs.append(_remote(ins[a], outs[a].at[2 * px + py], *sems))
    return local, sends, recvs


def _gather_half_plan(ins, outs, send_sems, recv_sems, local_sems):
    mx, my, mc = _me()
    chips = [(1 - mx, my), (mx, 1 - my), (1 - mx, 1 - my)]
    local, sends, recvs = [], [], []
    for a in range(len(ins)):
        h = ins[a].shape[0] // 2
        mine = pl.ds(pl.multiple_of(mc * h, 16), h)
        local.append(pltpu.make_async_copy(ins[a], outs[a].at[2 * mx + my], local_sems.at[a]))
        for k, (px, py) in enumerate(chips):
            sems = (send_sems.at[3 * a + k], recv_sems.at[3 * a + k], (px, py, mc))
            sends.append(_remote(ins[a].at[mine], outs[a].at[2 * mx + my, mine], *sems))
            recvs.append(_remote(ins[a].at[mine], outs[a].at[2 * px + py, mine], *sems))
    return local, sends, recvs


def _sibling_fill(pieces):
    h = pieces.shape[1] // 2

    def body(p_ref, o_ref, send_sems, recv_sems):
        mx, my, mc = _me()
        sib = (mx, my, 1 - mc)
        chips = [(1 - mx, my), (mx, 1 - my), (1 - mx, 1 - my)]
        half = lambda c: pl.ds(pl.multiple_of(c * h, 16), h)
        o_ref[2 * mx + my] = p_ref[2 * mx + my]
        sends = []
        for k, (px, py) in enumerate(chips):
            j = 2 * px + py
            o_ref[j, half(mc), :] = p_ref[j, half(mc), :]
            cp = _remote(p_ref.at[j, half(mc)], o_ref.at[j, half(mc)], send_sems.at[k], recv_sems.at[k], sib)
            cp.start()
            sends.append(cp)
        for k, (px, py) in enumerate(chips):
            j = 2 * px + py
            _remote(p_ref.at[j, half(mc)], o_ref.at[j, half(1 - mc)], send_sems.at[k], recv_sems.at[k],
                    sib).wait_recv()
        for cp in sends:
            cp.wait_send()

    vmem = pl.BlockSpec(memory_space=pltpu.VMEM)
    return pl.pallas_call(
        body, name="sibling_fill", out_shape=_sds(pieces.shape, pieces.dtype),
        in_specs=[vmem], out_specs=vmem,
        scratch_shapes=[pltpu.SemaphoreType.DMA((N_CHIP - 1,)), pltpu.SemaphoreType.DMA((N_CHIP - 1,))],
        compiler_params=_cparams(),
    )(pieces)


def _start(plan):
    local, sends, _ = plan
    for cp in local + sends:
        cp.start()


def _finish(plan):
    local, sends, recvs = plan
    for cp in recvs:
        cp.wait_recv()
    for cp in sends:
        cp.wait_send()
    for cp in local:
        cp.wait()


def _exchange_shapes(pieces):
    return [_sds((N_DEV,) + p.shape[2:], p.dtype) for p in pieces]


def _exchange_sems(n):
    return [pltpu.SemaphoreType.DMA(((N_DEV - 1) * n,)), pltpu.SemaphoreType.DMA(((N_DEV - 1) * n,)),
            pltpu.SemaphoreType.DMA((n,))]


def _exchange_plan(ins, outs, send_sems, recv_sems, local_sems):
    me = _me()
    mx, my, mc = me
    local, sends, recvs = [], [], []
    for a in range(len(ins)):
        local.append(pltpu.make_async_copy(ins[a].at[2 * mx + my, mc], outs[a].at[_lin(me)], local_sems.at[a]))
        for k in range(1, N_DEV):
            p = _peer(k, me)
            s = (N_DEV - 1) * a + k - 1
            sends.append(_remote(ins[a].at[2 * p[0] + p[1], p[2]], outs[a].at[_lin(me)], send_sems.at[s],
                                 recv_sems.at[s], p))
            recvs.append(_remote(ins[a].at[2 * mx + my, mc], outs[a].at[_lin(p)], send_sems.at[s],
                                 recv_sems.at[s], p))
    return local, sends, recvs


REDUCE_VMEM = 56 * 1024 * 1024


def _reduce_swap(recvs):
    n = len(recvs)

    def body(*refs):
        r_refs, o_refs = refs[:n], refs[n:2 * n]
        send_sems, recv_sems = refs[2 * n:]
        mx, my, mc = _me()
        sib = (mx, my, 1 - mc)
        half = lambda a, c: o_refs[a].at[pl.ds(pl.multiple_of(c * recvs[a].shape[1], 8), recvs[a].shape[1])]
        sends = []
        for a in range(n):
            g = r_refs[a][0].astype(F32)
            for s in range(1, N_DEV):
                g = g + r_refs[a][s].astype(F32)
            half(a, mc)[...] = g
            cp = _remote(half(a, mc), half(a, mc), send_sems.at[a], recv_sems.at[a], sib)
            cp.start()
            sends.append(cp)
        for a in range(n):
            _remote(half(a, mc), half(a, 1 - mc), send_sems.at[a], recv_sems.at[a], sib).wait_recv()
        for cp in sends:
            cp.wait_send()

    vmem = pl.BlockSpec(memory_space=pltpu.VMEM)
    return pl.pallas_call(
        body, name="reduce_swap", out_shape=[_sds((2 * r.shape[1], r.shape[2])) for r in recvs],
        in_specs=[vmem] * n, out_specs=[vmem] * n,
        scratch_shapes=[pltpu.SemaphoreType.DMA((n,)), pltpu.SemaphoreType.DMA((n,))],
        compiler_params=_cparams(None, REDUCE_VMEM),
    )(*recvs)


def _adamw_big(g, w, m, v, name):
    rows, cols = g.shape
    tr = next((t for t in (256, 176, 128, 64, 8) if rows % t == 0), None)
    if tr is None:
        tc = _tile(cols, 256)
        blk, grid = pl.BlockSpec((rows, tc), lambda i: (0, i)), (cols // tc,)
    else:
        blk, grid = pl.BlockSpec((tr, cols), lambda i: (i, 0)), (rows // tr,)

    def body(g_ref, w_ref, m_ref, v_ref, go_ref, d_ref, m2_ref, v2_ref):
        g = g_ref[...]
        delta, m2, v2 = _adamw(w_ref[...], g, m_ref[...], v_ref[...])
        go_ref[...] = g
        d_ref[...] = delta
        m2_ref[...] = m2
        v2_ref[...] = v2

    return pl.pallas_call(
        body, name=name, grid=grid,
        in_specs=[blk] * 4, out_specs=[blk] * 4, out_shape=[_sds((rows, cols))] * 4,
        compiler_params=_cparams(("parallel",)),
    )(g, w, m, v)


SMALL_ORDER = (("mod", 6 * D), ("norm1_w", D), ("norm2_w", D), ("conv_w", CONVW * 3 * GW), ("a_log", GH),
               ("dt_bias", GH), ("gdn_norm_w", HD), ("q_norm_w", HD), ("k_norm_w", HD), ("sinks", SQH), ("loss", 1))
SMALL_R = 120


def _pack_small(d):
    parts = [d[k].reshape(-1).astype(F32) if k in d else jnp.zeros((n,), F32) for k, n in SMALL_ORDER]
    used = sum(n for _, n in SMALL_ORDER)
    parts.append(jnp.zeros((SMALL_R * LANE - used,), F32))
    return jnp.concatenate(parts).reshape(SMALL_R, LANE)


def _unpack_small(pk):
    flat = pk.reshape(-1)
    out, r = {}, 0
    for k, n in SMALL_ORDER:
        out[k] = flat[r:r + n]
        r += n
    return out


def kernel(x, c, w_ada, b_ada, norm1_w, w_in, conv_w, a_log, dt_bias, gdn_norm_w, q_norm_w, k_norm_w, sinks, w_out, norm2_w, w_gate, w_up, w_down, loss_target, m_w_ada, m_b_ada, m_norm1_w, m_w_in, m_conv_w, m_a_log, m_dt_bias, m_gdn_norm_w, m_q_norm_w, m_k_norm_w, m_sinks, m_w_out, m_norm2_w, m_w_gate, m_w_up, m_w_down, v_w_ada, v_b_ada, v_norm1_w, v_w_in, v_conv_w, v_a_log, v_dt_bias, v_gdn_norm_w, v_q_norm_w, v_k_norm_w, v_sinks, v_w_out, v_norm2_w, v_w_gate, v_w_up, v_w_down):
    mx, my, mc = _me()
    chip = 2 * mx + my
    dev = 4 * mx + 2 * my + mc
    T = x.shape[1]

    as_rows = lambda t, transposed: t[0].T if transposed else t[0]
    transposed = (True, False, True, True, False)
    big_w = [as_rows(t, tr) for t, tr in zip((w_in, w_out, w_gate, w_up, w_down), transposed)]
    shards = [t.astype(BF16) for t in big_w]

    b_sh = lax.dynamic_slice(b_ada, (0, chip * ADA_N), (1, ADA_N))
    w_in_sh = jnp.pad(shards[0], ((0, W_IN_ROWS_PAD - W_IN_ROWS), (0, 0)))
    c_all, conv_all, mods, a_in = _prologue(c, conv_w.reshape(CONVW, 3 * GW // N_CHIP), w_ada[0], b_sh, w_in_sh)
    c8 = c_all.reshape(N_DEV, D)
    conv_full = jnp.concatenate([conv_all[2 * j] for j in range(N_CHIP)], axis=1)
    mod = jnp.concatenate([lax.dynamic_slice(mods[2 * j], (dev, 0), (1, ADA_N)) for j in range(N_CHIP)], axis=1)
    w_in_pt = _permute_w_in_t(_sibling_fill(a_in)[:, :W_IN_ROWS].reshape(PROJ, D))

    loss, grad_x, big, small = _local_step(
        x[0], loss_target[0], mod, norm1_w, w_in_pt, conv_full, a_log, dt_bias, gdn_norm_w,
        q_norm_w, k_norm_w, sinks, norm2_w, shards[1:])

    small["loss"] = loss[:, :1]
    sg = _all_gather8(_pack_small(small), "gather_small_grads")
    rep = dict(mod=(b_ada, m_b_ada, v_b_ada), norm1_w=(norm1_w, m_norm1_w, v_norm1_w),
               norm2_w=(norm2_w, m_norm2_w, v_norm2_w), a_log=(a_log, m_a_log, v_a_log),
               dt_bias=(dt_bias, m_dt_bias, v_dt_bias), gdn_norm_w=(gdn_norm_w, m_gdn_norm_w, v_gdn_norm_w),
               q_norm_w=(q_norm_w, m_q_norm_w, v_q_norm_w), k_norm_w=(k_norm_w, m_k_norm_w, v_k_norm_w),
               sinks=(sinks, m_sinks, v_sinks))
    wmv = [_pack_small({k: t[i] for k, t in rep.items()}) for i in range(3)]
    sres = _reduce_adamw(sg, wmv[0], wmv[1], wmv[2], "small_reduce_adamw")
    s_g, s_d, s_m, s_v = [_unpack_small(sres[i]) for i in range(4)]
    loss_out = s_g["loss"][0]

    g_conv = lax.dynamic_slice(s_g["conv_w"].reshape(CONVW, 3 * GW), (0, chip * (3 * GW // N_CHIP)),
                               (CONVW, 3 * GW // N_CHIP))
    pad16 = lambda t: jnp.concatenate([t.reshape(12, LANE), jnp.zeros((4, LANE), F32)], axis=0)
    cres = _adamw_call(pad16(g_conv), pad16(conv_w), pad16(m_conv_w), pad16(v_conv_w), "conv_adamw")
    conv_out = [g_conv.reshape(conv_w.shape)] + [cres[i, :12].reshape(conv_w.shape) for i in range(3)]

    dmod8 = sg[:, :6 * D // LANE].reshape(N_DEV, 6 * D)
    dm = lax.dynamic_slice(dmod8, (0, chip * ADA_N), (N_DEV, ADA_N))
    zpad = lambda t: jnp.concatenate([t, jnp.zeros((KPAD - N_DEV, t.shape[1]), F32)], axis=0)
    ares = _w_ada_update(zpad(c8), zpad(dm), w_ada[0], m_w_ada[0], v_w_ada[0])

    names = ("w_in", "w_out", "w_gate", "w_up", "w_down")
    g_full = list(_reduce_swap(big))
    g_full[0] = g_full[0][:W_IN_ROWS]
    big_m = [as_rows(t, tr) for t, tr in zip((m_w_in, m_w_out, m_w_gate, m_w_up, m_w_down), transposed)]
    big_v = [as_rows(t, tr) for t, tr in zip((v_w_in, v_w_out, v_w_gate, v_w_up, v_w_down), transposed)]
    upd = [_adamw_big(g, w, m, v, "adamw_" + nm) for g, w, m, v, nm in zip(g_full, big_w, big_m, big_v, names)]
    back = lambda t, tr: (t.T if tr else t)[None]
    bg, bd, bm, bv = [[back(u[i], tr) for u, tr in zip(upd, transposed)] for i in range(4)]

    def group(a_i, small_d, conv_i, big_l):
        s = lambda k, ref: small_d[k].reshape(ref.shape)
        return [ares[a_i][None], s("mod", b_ada), s("norm1_w", norm1_w), big_l[0], conv_out[conv_i],
                s("a_log", a_log), s("dt_bias", dt_bias), s("gdn_norm_w", gdn_norm_w), s("q_norm_w", q_norm_w),
                s("k_norm_w", k_norm_w), s("sinks", sinks), big_l[1], s("norm2_w", norm2_w), big_l[2], big_l[3],
                big_l[4]]

    outs = [loss_out, grad_x[None]]
    outs += group(0, s_g, 0, bg) + group(1, s_d, 1, bd) + group(2, s_m, 2, bm) + group(3, s_v, 3, bv)
    return tuple(outs)
```

```python
import jax
import jax.numpy as jnp
from jax import lax
from jax.experimental import pallas as pl
from jax.experimental.pallas import tpu as pltpu

F32 = jnp.float32
BF16 = jnp.bfloat16
MESH = pl.DeviceIdType.MESH

D = 1024
HD = 64
GH = 8
GW = GH * HD
SQH = 8
SKVH = 2
SGRP = SQH // SKVH
WIN = 128
CONVW = 4
CHUNK = 64
DFF = 2816
PROJ = 2832
NP = 3072
EPS = 1e-6
N_DEV = 8
N_CHIP = 4

ADAM_LR = 0.001
ADAM_B1 = 0.9
ADAM_B2 = 0.999
ADAM_EPS = 1e-08
ADAM_WD = 0.01
ADAM_STEP = 10

VMEM_LIMIT = 48 * 1024 * 1024
GDN_BWD_VMEM = 58 * 1024 * 1024
LANE = 128


def _cparams(sem=None, vmem=VMEM_LIMIT):
    return pltpu.CompilerParams(dimension_semantics=sem, vmem_limit_bytes=vmem)


_NN = ((1,), (0,))
_NT = ((1,), (1,))
_TN = ((0,), (0,))


def _dot(a, b, dims):
    if a.ndim == 3:
        (ca,), (cb,) = dims
        return lax.dot_general(a, b, (((ca + 1,), (cb + 1,)), ((0,), (0,))), preferred_element_type=F32)
    return lax.dot_general(a, b, (dims, ((), ())), preferred_element_type=F32)


def _raw1(a, b, dims):
    return _dot(a.astype(BF16), b.astype(BF16), dims)


def _raw3(a, b, dims):
    ah = a.astype(BF16)
    al = (a - ah.astype(F32)).astype(BF16)
    bh = b.astype(BF16)
    bl = (b - bh.astype(F32)).astype(BF16)
    return _dot(ah, bh, dims) + (_dot(al, bh, dims) + _dot(ah, bl, dims))


def _make_diff_mm(raw):
    @jax.custom_vjp
    def nn(a, b):
        return raw(a, b, _NN)

    @jax.custom_vjp
    def nt(a, b):
        return raw(a, b, _NT)

    @jax.custom_vjp
    def tn(a, b):
        return raw(a, b, _TN)

    nn.defvjp(lambda a, b: (raw(a, b, _NN), (a, b)), lambda r, g: (nt(g, r[1]), tn(r[0], g)))
    nt.defvjp(lambda a, b: (raw(a, b, _NT), (a, b)), lambda r, g: (nn(g, r[1]), tn(g, r[0])))
    tn.defvjp(lambda a, b: (raw(a, b, _TN), (a, b)), lambda r, g: (nt(r[1], g), nn(r[0], g)))
    return nn, nt, tn


def _tri_inv_raw(a, nn3):
    n = a.shape[-1]
    ri = lax.broadcasted_iota(jnp.int32, (n, n), 0)
    ci = lax.broadcasted_iota(jnp.int32, (n, n), 1)
    t = (ri == ci).astype(F32)
    for lvl in range((n - 1).bit_length()):
        same_pair = (ri >> (lvl + 1)) == (ci >> (lvl + 1))
        lower_left = (((ri >> lvl) & 1) == 1) & (((ci >> lvl) & 1) == 0)
        y = jnp.where(same_pair & lower_left, a, 0.0)
        t = t - y if lvl == 0 else t - nn3(nn3(t, y), t)
    return t


class _Kit:
    def __init__(self, diff):
        if diff:
            self.nn, self.nt, self.tn = _make_diff_mm(_raw1)
            self.nn3, self.nt3, self.tn3 = _make_diff_mm(_raw3)
            nn3, nt3, tn3 = self.nn3, self.nt3, self.tn3

            @jax.custom_vjp
            def inv(a, t):
                return t

            def inv_fwd(a, t):
                return t, t

            def inv_bwd(t, g):
                return -tn3(t, nt3(g, t)), jnp.zeros_like(t)

            inv.defvjp(inv_fwd, inv_bwd)
            self.inv = inv
        else:
            self.nn = lambda a, b: _raw1(a, b, _NN)
            self.nt = lambda a, b: _raw1(a, b, _NT)
            self.tn = lambda a, b: _raw1(a, b, _TN)
            self.nn3 = lambda a, b: _raw3(a, b, _NN)
            self.nt3 = lambda a, b: _raw3(a, b, _NT)
            self.tn3 = lambda a, b: _raw3(a, b, _TN)
            self.inv = lambda a, t: _tri_inv_raw(a, self.nn3) if t is None else t


def _sigmoid(x):
    return 1.0 / (1.0 + jnp.exp(-x))


def _silu(x):
    return x * _sigmoid(x)


def _rms(x, w):
    return x * lax.rsqrt(jnp.mean(x * x, axis=-1, keepdims=True) + EPS) * w


def _tile(dim, target):
    t = (min(dim, target) // LANE) * LANE
    while t >= LANE:
        if dim % t == 0:
            return t
        t -= LANE
    return dim


MM_TM, MM_TN, MM_TK = 1408, 1536, 1408


def _matmul(a, b, ta=False, tb=False, out_dtype=F32, name="matmul", gather=None, exchange=None):
    carried = gather if gather is not None else exchange if exchange is not None else []
    nc = len(carried)
    if ta:
        K, M = a.shape
    else:
        M, K = a.shape
    if tb:
        N, K2 = b.shape
    else:
        K2, N = b.shape
    assert K == K2, (a.shape, b.shape, ta, tb)
    tm, tn, tk = _tile(M, MM_TM), _tile(N, MM_TN), _tile(K, MM_TK)
    nk = K // tk
    dims = ((0,) if ta else (1,), (1,) if tb else (0,))

    grid = (M // tm, N // tn, nk)

    def body(*refs):
        a_ref, b_ref = refs[:2]
        o_ref = refs[2 + nc]
        scratch = refs[3 + 2 * nc:]
        k = pl.program_id(2)
        if nc:
            make_plan = _gather_plan if gather is not None else _exchange_plan
            plan = make_plan(refs[2:2 + nc], refs[3 + nc:3 + 2 * nc], *scratch[-3:])
            at = lambda pos: ((pl.program_id(0) == pos[0]) & (pl.program_id(1) == pos[1]) & (k == pos[2]))

            @pl.when(at((0, 0, 0)))
            def _():
                _start(plan)

        part = _dot(a_ref[...].astype(BF16), b_ref[...].astype(BF16), dims)
        if nk == 1:
            o_ref[...] = part.astype(o_ref.dtype)
        else:
            acc_ref = scratch[0]

            @pl.when(k == 0)
            def _():
                acc_ref[...] = part

            @pl.when((k > 0) & (k < nk - 1))
            def _():
                acc_ref[...] += part

            @pl.when(k == nk - 1)
            def _():
                o_ref[...] = (acc_ref[...] + part).astype(o_ref.dtype)

        if nc:
            @pl.when(at((grid[0] - 1, grid[1] - 1, nk - 1)))
            def _():
                _finish(plan)

    a_spec = (pl.BlockSpec((tk, tm), lambda i, j, k: (k, i)) if ta
              else pl.BlockSpec((tm, tk), lambda i, j, k: (i, k)))
    b_spec = (pl.BlockSpec((tn, tk), lambda i, j, k: (j, k)) if tb
              else pl.BlockSpec((tk, tn), lambda i, j, k: (k, j)))
    if gather is not None:
        c_shapes, c_sems = _gather_shapes(carried), _gather_sems(nc)
    elif exchange is not None:
        c_shapes, c_sems = _exchange_shapes(carried), _exchange_sems(nc)
    else:
        c_shapes, c_sems = [], []
    res = pl.pallas_call(
        body, name=name, grid=grid,
        in_specs=[a_spec, b_spec] + _hbm_specs(nc),
        out_specs=[pl.BlockSpec((tm, tn), lambda i, j, k: (i, j))] + _hbm_specs(nc),
        out_shape=[jax.ShapeDtypeStruct((M, N), out_dtype)] + c_shapes,
        scratch_shapes=([pltpu.VMEM((tm, tn), F32)] if nk > 1 else []) + c_sems,
        compiler_params=_cparams(("arbitrary",) * 3 if nc else ("parallel", "parallel", "arbitrary")),
    )(a, b, *carried)
    return (res[0], res[1:]) if nc else res[0]


def _sds(shape, dtype=F32):
    return jax.ShapeDtypeStruct(shape, dtype)


def _norm_mod(x, nw, scale, shift):
    return _rms(x, nw) * (1.0 + scale) + shift


def _norm_in_proj(x, nw, scale, shift, w_in_pt, shards):
    T = x.shape[0]
    N = w_in_pt.shape[0]
    tm, tn = _tile(T, 1024), _tile(N, MM_TN)
    nm, nn = T // tm, N // tn
    ns = len(shards)

    def body(*refs):
        x_ref, nw_ref, sc_ref, sh_ref, w_ref = refs[:5]
        h_ref, o_ref = refs[5 + ns:7 + ns]
        plan = _gather_plan(refs[5:5 + ns], refs[7 + ns:7 + 2 * ns], *refs[7 + 2 * ns:])
        i, j = pl.program_id(0), pl.program_id(1)

        @pl.when((i == 0) & (j == 0))
        def _():
            _start(plan)

        @pl.when(j == 0)
        def _():
            for r0 in range(0, tm, ROWS_EPI):
                rows = pl.ds(r0, ROWS_EPI)
                h_ref[rows, :] = _norm_mod(x_ref[rows, :], nw_ref[...], sc_ref[...], sh_ref[...]).astype(BF16)

        o_ref[...] = _dot(h_ref[...], w_ref[...], _NT)

        @pl.when((i == nm - 1) & (j == nn - 1))
        def _():
            _finish(plan)

    vec = pl.BlockSpec((1, D), lambda i, j: (0, 0))
    res = pl.pallas_call(
        body, name="norm1_in_proj", grid=(nm, nn),
        in_specs=[pl.BlockSpec((tm, D), lambda i, j: (i, 0)), vec, vec, vec,
                  pl.BlockSpec((tn, D), lambda i, j: (j, 0))] + _hbm_specs(ns),
        out_specs=[pl.BlockSpec((tm, D), lambda i, j: (i, 0)), pl.BlockSpec((tm, tn), lambda i, j: (i, j))]
                  + _hbm_specs(ns),
        out_shape=[_sds((T, D), BF16), _sds((T, N))] + _gather_shapes(shards),
        scratch_shapes=_gather_sems(ns),
        compiler_params=_cparams(("arbitrary", "arbitrary")),
    )(x, nw, scale, shift, w_in_pt, *shards)
    return res[0], res[1], res[2:]


ROWS_TM = 512
ROWS_EPI = 256


def _matmul_rows(a, b, epi, tiled, consts, out_tiled, out_acc, name, pieces=()):
    T, K = a.shape
    tm, tk = _tile(T, ROWS_TM), _tile(K, MM_TK)
    nm, nk = T // tm, K // tk
    npc, nt, ncst, no, na = len(pieces), len(tiled), len(consts), len(out_tiled), len(out_acc)
    n_in = 2 + nt + ncst

    def body(*refs):
        a_ref, b_ref = refs[:2]
        t_refs, c_refs = refs[2:2 + nt], refs[2 + nt:n_in]
        o_refs = refs[n_in + npc:n_in + npc + no]
        acc_refs = refs[n_in + npc + no:n_in + npc + no + na]
        n_out = no + na + npc
        res_ref = refs[n_in + npc + n_out]
        plan = _exchange_plan(refs[n_in:n_in + npc], refs[n_in + npc + no + na:n_in + npc + n_out],
                              *refs[n_in + npc + n_out + 1:]) if npc else None
        i, k = pl.program_id(0), pl.program_id(1)

        @pl.when((i == 0) & (k == 0))
        def _():
            for r in acc_refs:
                r[...] = jnp.zeros_like(r)
            if npc:
                _start(plan)

        part = _dot(a_ref[...], b_ref[...], _NN)

        @pl.when(k == 0)
        def _():
            res_ref[...] = part

        @pl.when(k > 0)
        def _():
            res_ref[...] += part

        @pl.when(k == nk - 1)
        def _():
            for r0 in range(0, tm, ROWS_EPI):
                rows = pl.ds(r0, ROWS_EPI)
                outs = epi(res_ref[rows, :], *[r[rows, :] for r in t_refs], *[r[...] for r in c_refs])
                for r, v in zip(o_refs, outs[:no]):
                    r[rows, :] = v.astype(r.dtype)
                for r, v in zip(acc_refs, outs[no:]):
                    r[...] += v

        if npc:
            @pl.when((i == nm - 1) & (k == nk - 1))
            def _():
                _finish(plan)

    row = lambda w: pl.BlockSpec((tm, w), lambda i, k: (i, 0))
    whole = lambda s: pl.BlockSpec(s.shape, lambda i, k: (0, 0))
    res = pl.pallas_call(
        body, name=name, grid=(nm, nk),
        in_specs=[pl.BlockSpec((tm, tk), lambda i, k: (i, k)), pl.BlockSpec((tk, D), lambda i, k: (k, 0))]
                 + [row(t.shape[1]) for t in tiled] + [whole(c) for c in consts] + _hbm_specs(npc),
        out_specs=[row(s.shape[1]) for s in out_tiled] + [whole(s) for s in out_acc] + _hbm_specs(npc),
        out_shape=list(out_tiled) + list(out_acc) + (_exchange_shapes(pieces) if npc else []),
        scratch_shapes=[pltpu.VMEM((tm, D), F32)] + (_exchange_sems(npc) if npc else []),
        compiler_params=_cparams(("arbitrary", "arbitrary")),
    )(a, b, *tiled, *consts, *pieces)
    return res[:no + na], res[no + na:]


def _in_proj_dx_norm_bwd(dproj, w_in_pt, x, dres, nw, scale, shift, pieces):
    T = x.shape[0]

    def epi(dh, x, dres, nw, scale, shift):
        _, vjp = jax.vjp(_norm_mod, x, nw, scale, shift)
        dx, dnw, dsc, dsh = vjp(dh)
        return dx + dres, dnw, dsc, dsh

    return _matmul_rows(dproj, w_in_pt, epi, [x, dres], [nw, scale, shift], [_sds((T, D))], [_sds((1, D))] * 3,
                        "in_proj_dx_norm1_bwd", pieces)


def _out_proj_resid_norm(o_hm, w_out, x, gate1, nw, scale, shift):
    T = x.shape[0]
    nheads = o_hm.shape[0]
    tm = _tile(T, ROWS_TM)

    def body(o_ref, w_ref, x_ref, g_ref, nw_ref, sc_ref, sh_ref, cat_ref, mixed_ref, x1_ref, h2_ref):
        cat = jnp.concatenate([_merge_pair(o_ref[2 * p], o_ref[2 * p + 1]) for p in range(nheads // 2)], axis=1)
        cat_ref[...] = cat.astype(BF16)
        mixed_ref[...] = _dot(cat_ref[...], w_ref[...], _NN)
        for r0 in range(0, tm, ROWS_EPI):
            rows = pl.ds(r0, ROWS_EPI)
            x1, h2 = _resid_norm(x_ref[rows, :], mixed_ref[rows, :], g_ref[...], nw_ref[...], sc_ref[...], sh_ref[...])
            x1_ref[rows, :] = x1
            h2_ref[rows, :] = h2.astype(BF16)

    row = pl.BlockSpec((tm, D), lambda i: (i, 0))
    vec = pl.BlockSpec((1, D), lambda i: (0, 0))
    return pl.pallas_call(
        body, name="out_proj_resid_norm2", grid=(T // tm,),
        in_specs=[pl.BlockSpec((nheads, tm, HD), lambda i: (0, i, 0)), pl.BlockSpec((D, D), lambda i: (0, 0)), row,
                  vec, vec, vec, vec],
        out_specs=[row, row, row, row],
        out_shape=[_sds((T, D), BF16), _sds((T, D)), _sds((T, D)), _sds((T, D), BF16)],
        compiler_params=_cparams(("parallel",)),
    )(o_hm, w_out, x, gate1, nw, scale, shift)


def _ffn_up_dx_resid_bwd(dab, w_gut, x, mixed, dy, gate1, nw, scale, shift):
    T = x.shape[0]

    def epi(dh2, x, mixed, dy, gate1, nw, scale, shift):
        _, vjp = jax.vjp(_resid_norm, x, mixed, gate1, nw, scale, shift)
        return vjp((dy, dh2))

    outs, _ = _matmul_rows(dab, w_gut, epi, [x, mixed, dy], [gate1, nw, scale, shift],
                           [_sds((T, D)), _sds((T, D), BF16)], [_sds((1, D))] * 4, "ffn_up_dx_resid_norm2_bwd")
    return outs


def _ffn_down_loss(act, w_down, x1, target, gate2):
    T = x1.shape[0]

    def epi(ffn, x1, target, gate2):
        y = x1 + gate2 * ffn
        err = y - target
        loss = 0.5 * jnp.sum(jnp.sum(err * err, axis=1, keepdims=True), axis=0, keepdims=True) / D
        dy = err * (1.0 / D)
        return dy, gate2 * dy, jnp.sum(dy * ffn, axis=0, keepdims=True), jnp.broadcast_to(loss, (1, LANE))

    outs, _ = _matmul_rows(act, w_down, epi, [x1, target], [gate2], [_sds((T, D)), _sds((T, D), BF16)],
                           [_sds((1, D)), _sds((1, LANE))], "ffn_down_loss")
    return outs


def _resid_norm(x, mixed, gate1, nw, scale, shift):
    x1 = x + gate1 * mixed
    return x1, _norm_mod(x1, nw, scale, shift)


FFN_BLK = 256
FFN_TM = 2048


def _interleave_gate_up(gate_t, up_t):
    blocks = lambda t: t.reshape(DFF // FFN_BLK, 1, FFN_BLK, D)
    return jnp.concatenate([blocks(gate_t), blocks(up_t)], axis=1).reshape(2 * DFF, D)


def _split_gate_up(g):
    g = g.reshape(DFF // FFN_BLK, 2, FFN_BLK, D)
    return g[:, 0].reshape(DFF, D), g[:, 1].reshape(DFF, D)


def _ffn_up_act(h2, w_gut):
    T = h2.shape[0]
    tm = _tile(T, FFN_TM)

    def body(h_ref, w_ref, ab_ref, act_ref):
        ab = _dot(h_ref[...], w_ref[...], _NT)
        ab_ref[...] = ab
        act_ref[...] = (_silu(ab[:, :FFN_BLK]) * ab[:, FFN_BLK:]).astype(act_ref.dtype)

    return pl.pallas_call(
        body, name="ffn_up_act", grid=(T // tm, DFF // FFN_BLK),
        in_specs=[pl.BlockSpec((tm, D), lambda i, j: (i, 0)), pl.BlockSpec((2 * FFN_BLK, D), lambda i, j: (j, 0))],
        out_specs=[pl.BlockSpec((tm, 2 * FFN_BLK), lambda i, j: (i, j)), pl.BlockSpec((tm, FFN_BLK), lambda i, j: (i, j))],
        out_shape=[_sds((T, 2 * DFF)), _sds((T, DFF), BF16)],
        compiler_params=_cparams(("parallel", "parallel")),
    )(h2, w_gut)


def _ffn_down_dx_act(dffn, w_down, ab):
    T = dffn.shape[0]
    tm = _tile(T, FFN_TM)

    def body(d_ref, w_ref, ab_ref, o_ref):
        dact = _dot(d_ref[...], w_ref[...], _NT)
        a, b = ab_ref[:, :FFN_BLK], ab_ref[:, FFN_BLK:]
        s = _sigmoid(a)
        da = dact * b * (s * (1.0 + a * (1.0 - s)))
        db = dact * (a * s)
        o_ref[...] = jnp.concatenate([da, db], axis=1).astype(o_ref.dtype)

    return pl.pallas_call(
        body, name="ffn_down_dx_act", grid=(T // tm, DFF // FFN_BLK),
        in_specs=[pl.BlockSpec((tm, D), lambda i, j: (i, 0)), pl.BlockSpec((FFN_BLK, D), lambda i, j: (j, 0)),
                  pl.BlockSpec((tm, 2 * FFN_BLK), lambda i, j: (i, j))],
        out_specs=pl.BlockSpec((tm, 2 * FFN_BLK), lambda i, j: (i, j)),
        out_shape=_sds((T, 2 * DFF), BF16),
        compiler_params=_cparams(("parallel", "parallel")),
    )(dffn, w_down, ab)


def _round_bf16(x):
    return x.astype(BF16).astype(F32)


def _shift_down(x, s, rows):
    if s == 0:
        return x
    return jnp.where(rows >= s, pltpu.roll(x, s, 0), 0.0)


def _shift_up(x, s, rows, T):
    if s == 0:
        return x
    return jnp.where(rows < T - s, pltpu.roll(x, T - s, 0), 0.0)


def _conv_fwd(proj, conv_w):
    T = proj.shape[0]
    ncol = 3 * GW // LANE

    def body(x_ref, w_ref, o_ref):
        x = _round_bf16(x_ref[...])
        rows = lax.broadcasted_iota(jnp.int32, x.shape, 0)
        acc = jnp.zeros_like(x)
        for j in range(CONVW):
            acc = acc + _round_bf16(w_ref[pl.ds(j, 1), :]) * _shift_down(x, CONVW - 1 - j, rows)
        o_ref[0], o_ref[1] = _split_pair(_silu(acc))

    return pl.pallas_call(
        body, name="conv_fwd", grid=(ncol,),
        in_specs=[pl.BlockSpec((T, LANE), lambda j: (0, j)), pl.BlockSpec((CONVW, LANE), lambda j: (0, j))],
        out_specs=pl.BlockSpec((2, T, HD), lambda j: (j, 0, 0)),
        out_shape=_sds((3 * GH, T, HD)),
        compiler_params=_cparams(("parallel",)),
    )(proj, conv_w)


RELAYOUT_TM = 4096


def _split_pair(y):
    return y[:, :HD], pltpu.roll(y, HD, 1)[:, :HD]


def _merge_pair(a, b):
    return jnp.concatenate([a, b], axis=1)


def _split_heads(x, col_block0, nheads, name):
    T = x.shape[0]
    tm = _tile(T, RELAYOUT_TM)

    def body(x_ref, o_ref):
        a, b = _split_pair(x_ref[...])
        o_ref[0] = a
        o_ref[1] = b

    return pl.pallas_call(
        body, name=name, grid=(nheads // 2, T // tm),
        in_specs=[pl.BlockSpec((tm, LANE), lambda j, i: (i, col_block0 + j))],
        out_specs=pl.BlockSpec((2, tm, HD), lambda j, i: (j, i, 0)),
        out_shape=_sds((nheads, T, HD), x.dtype),
        compiler_params=_cparams(("parallel", "parallel")),
    )(x)


def _merge_heads(hm, out_dtype, name, into=None, col_block0=0, head0=0, nheads=None):
    T = hm.shape[1]
    nheads = hm.shape[0] if nheads is None else nheads
    tm = _tile(T, RELAYOUT_TM)

    def body(*refs):
        h_ref, o_ref = refs[0], refs[-1]
        o_ref[...] = _merge_pair(h_ref[0], h_ref[1]).astype(o_ref.dtype)

    in_specs = [pl.BlockSpec((2, tm, HD), lambda j, i: (head0 // 2 + j, i, 0))]
    args = [hm]
    if into is None:
        out_shape = _sds((T, HD * nheads), out_dtype)
        aliases = {}
    else:
        out_shape = _sds(into.shape, into.dtype)
        in_specs.append(pl.BlockSpec(memory_space=pl.ANY))
        args.append(into)
        aliases = {1: 0}
    return pl.pallas_call(
        body, name=name, grid=(nheads // 2, T // tm),
        in_specs=in_specs,
        out_specs=pl.BlockSpec((tm, LANE), lambda j, i: (i, col_block0 + j)),
        out_shape=out_shape, input_output_aliases=aliases,
        compiler_params=_cparams(("parallel", "parallel")),
    )(*args)


def _matmul_nt_heads(a, b, name):
    T, K = a.shape
    N = b.shape[0]
    tm = _tile(T, 1024)

    def body(a_ref, b_ref, o_ref):
        res = _dot(a_ref[...], b_ref[...], _NT)
        for p in range(N // LANE):
            o_ref[2 * p], o_ref[2 * p + 1] = _split_pair(res[:, p * LANE:(p + 1) * LANE])

    return pl.pallas_call(
        body, name=name, grid=(T // tm,),
        in_specs=[pl.BlockSpec((tm, K), lambda i: (i, 0)), pl.BlockSpec((N, K), lambda i: (0, 0))],
        out_specs=pl.BlockSpec((N // HD, tm, HD), lambda i: (0, i, 0)),
        out_shape=_sds((N // HD, T, HD)),
        compiler_params=_cparams(("parallel",)),
    )(a, b)


def _conv_bwd(proj, conv_w, dqc):
    T = proj.shape[0]
    ncol = 3 * GW // LANE

    def body(x_ref, w_ref, d_ref, dx_ref, dw_ref):
        x = _round_bf16(x_ref[...])
        rows = lax.broadcasted_iota(jnp.int32, x.shape, 0)
        xs = [_shift_down(x, CONVW - 1 - j, rows) for j in range(CONVW)]
        w = [_round_bf16(w_ref[pl.ds(j, 1), :]) for j in range(CONVW)]
        pre = jnp.zeros_like(x)
        for j in range(CONVW):
            pre = pre + w[j] * xs[j]
        s = _sigmoid(pre)
        dpre = _round_bf16(_merge_pair(d_ref[0], d_ref[1]) * (s * (1.0 + pre * (1.0 - s))))
        dx = jnp.zeros_like(x)
        for j in range(CONVW):
            dx = dx + w[j] * _shift_up(dpre, CONVW - 1 - j, rows, T)
            dw_ref[pl.ds(j, 1), :] = jnp.sum(dpre * xs[j], axis=0, keepdims=True)
        dx_ref[...] = dx.astype(dx_ref.dtype)

    return pl.pallas_call(
        body, name="conv_bwd", grid=(ncol,),
        in_specs=[pl.BlockSpec((T, LANE), lambda j: (0, j)), pl.BlockSpec((CONVW, LANE), lambda j: (0, j)),
                  pl.BlockSpec((2, T, HD), lambda j: (j, 0, 0))],
        out_specs=[pl.BlockSpec((T, LANE), lambda j: (0, j)), pl.BlockSpec((CONVW, LANE), lambda j: (0, j))],
        out_shape=[_sds((T, NP), BF16), _sds((CONVW, 3 * GW))],
        compiler_params=_cparams(("parallel",)),
    )(proj, conv_w, dqc)


def _gdn_prep(kit, q, k, v, ga, gb, alog, dtb, t_inv=None):
    C = CHUNK
    ri = lax.broadcasted_iota(jnp.int32, (C, C), 0)
    ci = lax.broadcasted_iota(jnp.int32, (C, C), 1)
    causal = ri >= ci
    strict = ri > ci
    eye = (ri == ci).astype(F32)
    lower = causal.astype(F32)
    upper = (ri <= ci).astype(F32)

    a = ga + dtb
    softplus = jnp.maximum(a, 0.0) + jnp.log(1.0 + jnp.exp(-jnp.abs(a)))
    g_row = -jnp.exp(alog) * softplus
    beta_row = _sigmoid(gb)
    g_col = jnp.sum(eye * g_row, axis=2, keepdims=True)
    beta_col = jnp.sum(eye * beta_row, axis=2, keepdims=True)
    G_col = jnp.sum(lower * g_row, axis=2, keepdims=True)
    G_row = jnp.sum(upper * g_col, axis=1, keepdims=True)
    G_last = jnp.sum(g_row, axis=2, keepdims=True)
    decay = jnp.exp(jnp.where(causal, G_col - G_row, -1e30))

    qn = q * lax.rsqrt(jnp.sum(q * q, axis=-1, keepdims=True) + EPS) * (HD ** -0.5)
    kn = k * lax.rsqrt(jnp.sum(k * k, axis=-1, keepdims=True) + EPS)
    kb = kn * beta_col
    A = jnp.where(strict, kit.nt(kb, kn) * decay, 0.0)
    Tm = kit.inv(A, t_inv)
    eG = jnp.exp(G_col)
    u = kit.nn3(Tm, v * beta_col)
    w = kit.nn3(Tm, kb * eG)
    qk = jnp.where(causal, kit.nt(qn, kn) * decay, 0.0)
    q_dec = qn * eG
    k_dec = kn * jnp.exp(G_last - G_col)
    dec = jnp.exp(G_last)
    return u, w, qk, q_dec, k_dec, dec, Tm


def _gdn_out(o, z, nw):
    return _rms(o, nw) * _silu(z)


GDN_CB = 4


def _gdn_specs(T, blk):
    TB = GDN_CB * CHUNK
    seq = lambda grp: pl.BlockSpec((GH, TB, HD), lambda i, grp=grp: (grp, blk(i), 0))
    row = lambda grp: pl.BlockSpec((GH, GDN_CB, 1, CHUNK), lambda i, grp=grp: (grp, blk(i), 0, 0))
    per_head = pl.BlockSpec((GH, 1, CHUNK), lambda i: (0, 0, 0))
    whole = pl.BlockSpec((1, HD), lambda i: (0, 0))
    state = pl.BlockSpec((GH, GDN_CB, HD, HD), lambda i: (0, blk(i), 0, 0))
    return seq, row, per_head, whole, state


def _gdn_load(seq_refs, row_refs, head_refs):
    chunks = lambda r: jnp.concatenate([r[:, pl.ds(cb * CHUNK, CHUNK), :] for cb in range(GDN_CB)], axis=0)
    rows = lambda r: jnp.concatenate([r[:, cb] for cb in range(GDN_CB)], axis=0)
    heads = lambda r: jnp.concatenate([r[...]] * GDN_CB, axis=0)
    return [chunks(r) for r in seq_refs], [rows(r) for r in row_refs], [heads(r) for r in head_refs]


def _gdn_fwd(qkv_hm, zs_hm, gab, alog_b, dtb_b, nw, shards):
    T = qkv_hm.shape[1]
    N = T // CHUNK
    nblk = N // GDN_CB
    ns = len(shards)
    seq, row, per_head, whole, state = _gdn_specs(T, lambda i: i)
    kit = _Kit(False)

    def body(*refs):
        q_ref, k_ref, v_ref, z_ref, ga_ref, gb_ref, al_ref, dt_ref, nw_ref = refs[:9]
        o_ref, S_ref, T_ref = refs[9 + ns:12 + ns]
        S_scr = refs[12 + 2 * ns]
        plan = _gather_plan(refs[9:9 + ns], refs[12 + ns:12 + 2 * ns], *refs[13 + 2 * ns:])

        @pl.when(pl.program_id(0) == 0)
        def _():
            S_scr[...] = jnp.zeros_like(S_scr)
            _start(plan)

        (q, k, v, z), (ga, gb), (al, dt) = _gdn_load((q_ref, k_ref, v_ref, z_ref), (ga_ref, gb_ref), (al_ref, dt_ref))
        u, w, qk, q_dec, k_dec, dec, t_inv = _gdn_prep(kit, q, k, v, ga, gb, al, dt)
        S = S_scr[...]
        for cb in range(GDN_CB):
            hs = slice(cb * GH, (cb + 1) * GH)
            S_ref[:, cb] = S
            T_ref[:, cb] = t_inv[hs]
            v_new = u[hs] - kit.nn(w[hs], S)
            o = kit.nn(q_dec[hs], S) + kit.nn(qk[hs], v_new)
            S = S * dec[hs] + kit.tn(k_dec[hs], v_new)
            o_ref[:, pl.ds(cb * CHUNK, CHUNK), :] = _gdn_out(o, z[hs], nw_ref[...])
        S_scr[...] = S

        @pl.when(pl.program_id(0) == nblk - 1)
        def _():
            _finish(plan)

    res = pl.pallas_call(
        body, name="gdn_fwd", grid=(nblk,),
        in_specs=[seq(0), seq(1), seq(2), seq(0), row(0), row(1), per_head, per_head, whole] + _hbm_specs(ns),
        out_specs=[seq(0), state, state] + _hbm_specs(ns),
        out_shape=[_sds((GH + SQH, T, HD)), _sds((GH, N, HD, HD)), _sds((GH, N, CHUNK, CHUNK))]
                  + _gather_shapes(shards),
        scratch_shapes=[pltpu.VMEM((GH, HD, HD), F32)] + _gather_sems(ns),
        compiler_params=_cparams(("arbitrary",)),
    )(qkv_hm, qkv_hm, qkv_hm, zs_hm, gab, gab, alog_b, dtb_b, nw, *shards)
    return res[0], (res[1], res[2]), res[3:]


def _gdn_bwd(qkv_hm, zs_hm, gab, alog_b, dtb_b, nw, S_all, do, pieces):
    T = qkv_hm.shape[1]
    N = T // CHUNK
    nblk = N // GDN_CB
    npc = len(pieces)
    dkit, kit = _Kit(True), _Kit(False)
    rseq, rrow, per_head, whole, rstate = _gdn_specs(T, lambda i: nblk - 1 - i)

    def body(*refs):
        q_ref, k_ref, v_ref, z_ref, ga_ref, gb_ref, al_ref, dt_ref, nw_ref, S_ref, T_ref, do_ref = refs[:12]
        dqkv_ref, dz_ref, dga_ref, dgb_ref, dal_ref, ddt_ref, dnw_ref = refs[12 + npc:19 + npc]
        dS_scr = refs[19 + 2 * npc]
        plan = _exchange_plan(refs[12:12 + npc], refs[19 + npc:19 + 2 * npc], *refs[20 + 2 * npc:])

        @pl.when(pl.program_id(0) == 0)
        def _():
            dS_scr[...] = jnp.zeros_like(dS_scr)
            dal_ref[...] = jnp.zeros_like(dal_ref)
            ddt_ref[...] = jnp.zeros_like(ddt_ref)
            dnw_ref[...] = jnp.zeros_like(dnw_ref)
            _start(plan)

        (q, k, v, z, dout), (ga, gb), (al, dt) = _gdn_load((q_ref, k_ref, v_ref, z_ref, do_ref), (ga_ref, gb_ref),
                                                          (al_ref, dt_ref))
        S_in = jnp.concatenate([S_ref[:, cb] for cb in range(GDN_CB)], axis=0)
        t_inv = jnp.concatenate([T_ref[:, cb] for cb in range(GDN_CB)], axis=0)
        prep = lambda *a: _gdn_prep(dkit, *a, t_inv=t_inv)[:6]
        (u, w, qk, q_dec, k_dec, dec), prep_vjp = jax.vjp(prep, q, k, v, ga, gb, al, dt)
        v_new = u - kit.nn(w, S_in)
        o = kit.nn(q_dec, S_in) + kit.nn(qk, v_new)
        _, out_vjp = jax.vjp(_gdn_out, o, z, nw_ref[...])
        do, dz, dnw = out_vjp(dout)
        dvn_part = kit.tn(qk, do)
        dS_part = kit.tn(q_dec, do)
        dS = dS_scr[...]
        dS_out, dvn = [None] * GDN_CB, [None] * GDN_CB
        for cb in reversed(range(GDN_CB)):
            hs = slice(cb * GH, (cb + 1) * GH)
            dS_out[cb] = dS
            dvn[cb] = dvn_part[hs] + kit.nn(k_dec[hs], dS)
            dS = dS * dec[hs] + dS_part[hs] - kit.tn(w[hs], dvn[cb])
        dS_scr[...] = dS
        dS_out = jnp.concatenate(dS_out, axis=0)
        dvn = jnp.concatenate(dvn, axis=0)
        ddec = jnp.sum(jnp.sum(S_in * dS_out, axis=2, keepdims=True), axis=1, keepdims=True)
        cts = (dvn, -kit.nt(dvn, S_in), kit.nt(do, v_new), kit.nt(do, S_in), kit.nt(v_new, dS_out), ddec)
        dq, dk, dv, dga, dgb, dal, ddt = prep_vjp(cts)
        lanesum = lambda t: jnp.broadcast_to(jnp.sum(t, axis=2, keepdims=True), t.shape)
        for cb in range(GDN_CB):
            hs = slice(cb * GH, (cb + 1) * GH)
            sl = pl.ds(cb * CHUNK, CHUNK)
            dqkv_ref[pl.ds(0, GH), sl, :] = dq[hs]
            dqkv_ref[pl.ds(GH, GH), sl, :] = dk[hs]
            dqkv_ref[pl.ds(2 * GH, GH), sl, :] = dv[hs]
            dz_ref[:, sl, :] = dz[hs]
            dga_ref[:, cb] = dga[hs]
            dgb_ref[:, cb] = dgb[hs]
            dal_ref[...] += lanesum(dal[hs])
            ddt_ref[...] += lanesum(ddt[hs])
        dnw_ref[...] += dnw

        @pl.when(pl.program_id(0) == nblk - 1)
        def _():
            _finish(plan)

    res = pl.pallas_call(
        body, name="gdn_bwd", grid=(nblk,),
        in_specs=[rseq(0), rseq(1), rseq(2), rseq(0), rrow(0), rrow(1), per_head, per_head, whole, rstate, rstate,
                  rseq(0)] + _hbm_specs(npc),
        out_specs=[pl.BlockSpec((3 * GH, GDN_CB * CHUNK, HD), lambda i: (0, nblk - 1 - i, 0)), rseq(0), rrow(0),
                   rrow(0), per_head, per_head, whole] + _hbm_specs(npc),
        out_shape=[_sds((3 * GH, T, HD)), _sds((GH + 4 + SWA_GRAD_HEADS, T, HD))] + [_sds((GH, N, 1, CHUNK))] * 2
                  + [_sds((GH, 1, CHUNK))] * 2 + [_sds((1, HD))] + _exchange_shapes(pieces),
        scratch_shapes=[pltpu.VMEM((GH, HD, HD), F32)] + _exchange_sems(npc),
        compiler_params=_cparams(("arbitrary",), GDN_BWD_VMEM),
    )(qkv_hm, qkv_hm, qkv_hm, zs_hm, gab, gab, alog_b, dtb_b, nw, S_all[0], S_all[1], do, *pieces)
    return res[:7], res[7:]


def _swa_heads(kit, first, q, kp, kc, vp, vc, qnw, knw, sink, slope):
    W = WIN
    ri = lax.broadcasted_iota(jnp.int32, (W, W), 0)
    ci = lax.broadcasted_iota(jnp.int32, (W, W), 1)
    mask_c = ri >= ci
    mask_p = ci > ri + first * W
    dist_c = (ri - ci).astype(F32)
    dist_p = (ri - ci + W).astype(F32)
    kpn = _rms(kp, knw)
    kcn = _rms(kc, knw)
    qn = _rms(q, qnw)
    sc = jnp.where(mask_c, kit.nt(qn, kcn) * (HD ** -0.5) - slope * dist_c, -1e30)
    sp = jnp.where(mask_p, kit.nt(qn, kpn) * (HD ** -0.5) - slope * dist_p, -1e30)
    m = jnp.maximum(jnp.maximum(jnp.max(sc, axis=-1, keepdims=True), jnp.max(sp, axis=-1, keepdims=True)), sink)
    m = lax.stop_gradient(m)
    pc = jnp.exp(sc - m)
    pp = jnp.exp(sp - m)
    den = jnp.sum(pc, axis=-1, keepdims=True) + jnp.sum(pp, axis=-1, keepdims=True) + jnp.exp(sink - m)
    inv = 1.0 / den
    return kit.nn(pc * inv, vc) + kit.nn(pp * inv, vp)


def _swa_grads(kit, first, q, kp, kc, vp, vc, qnw, knw, sink, slope, do):
    W = WIN
    ri = lax.broadcasted_iota(jnp.int32, (W, W), 0)
    ci = lax.broadcasted_iota(jnp.int32, (W, W), 1)
    mask_c = ri >= ci
    mask_p = ci > ri + first * W
    dist_c = (ri - ci).astype(F32)
    dist_p = (ri - ci + W).astype(F32)
    scale = HD ** -0.5
    kpn, kp_vjp = jax.vjp(_rms, kp, knw)
    kcn, kc_vjp = jax.vjp(_rms, kc, knw)
    qn, q_vjp = jax.vjp(_rms, q, qnw)
    sc = jnp.where(mask_c, kit.nt(qn, kcn) * scale - slope * dist_c, -1e30)
    sp = jnp.where(mask_p, kit.nt(qn, kpn) * scale - slope * dist_p, -1e30)
    m = jnp.maximum(jnp.maximum(jnp.max(sc, axis=-1, keepdims=True), jnp.max(sp, axis=-1, keepdims=True)), sink)
    ec = jnp.exp(sc - m)
    ep = jnp.exp(sp - m)
    es = jnp.exp(sink - m)
    inv = 1.0 / (jnp.sum(ec, axis=-1, keepdims=True) + jnp.sum(ep, axis=-1, keepdims=True) + es)
    pc, pp = ec * inv, ep * inv
    dpc, dpp = kit.nt(do, vc), kit.nt(do, vp)
    delta = jnp.sum(dpc * pc, axis=-1, keepdims=True) + jnp.sum(dpp * pp, axis=-1, keepdims=True)
    dsc = pc * (dpc - delta) * scale
    dsp = pp * (dpp - delta) * scale
    dq, dqnw = q_vjp(kit.nn(dsc, kcn) + kit.nn(dsp, kpn))
    dkc, dknw_c = kc_vjp(kit.tn(dsc, qn))
    dkp, dknw_p = kp_vjp(kit.tn(dsp, qn))
    return dq, dkp, dkc, kit.tn(pp, do), kit.tn(pc, do), dqnw, dknw_c + dknw_p, -(es * inv) * delta


def _per_query_head(kv_ref):
    return jnp.concatenate([kv_ref[pl.ds(h // SGRP, 1)] for h in range(SQH)], axis=0)


def _per_kv_head(d):
    return jnp.concatenate([jnp.sum(d[g * SGRP:(g + 1) * SGRP], axis=0, keepdims=True) for g in range(SKVH)], axis=0)


def _swa_specs(blk):
    qspec = pl.BlockSpec((SQH, WIN, HD), lambda i: (1, blk(i), 0))
    cur = lambda grp: pl.BlockSpec((SKVH, WIN, HD), lambda i, grp=grp: (grp, blk(i), 0))
    prev = lambda grp: pl.BlockSpec((SKVH, WIN, HD), lambda i, grp=grp: (grp, jnp.maximum(blk(i) - 1, 0), 0))
    whole = pl.BlockSpec((1, HD), lambda i: (0, 0))
    col = pl.BlockSpec((SQH, WIN, 1), lambda i: (0, 0, 0))
    ospec = pl.BlockSpec((SQH, WIN, HD), lambda i: (0, blk(i), 0))
    return qspec, cur, prev, whole, col, ospec


def _swa_fwd(zs_hm, qnw, knw, sinks_col, slopes_col, o_buf, shards):
    T = zs_hm.shape[1]
    NB = T // WIN
    ns = len(shards)
    kit = _Kit(False)
    qspec, cur, prev, whole, col, _ = _swa_specs(lambda i: i)

    def body(*refs):
        q_ref, kp_ref, kc_ref, vp_ref, vc_ref, qnw_ref, knw_ref, s_ref, sl_ref = refs[:9]
        o_ref = refs[10 + ns]
        plan = _gather_plan(refs[10:10 + ns], refs[11 + ns:11 + 2 * ns], *refs[11 + 2 * ns:])

        @pl.when(pl.program_id(0) == 0)
        def _():
            _start(plan)

        first = (pl.program_id(0) == 0).astype(jnp.int32)
        o_ref[...] = _swa_heads(kit, first, q_ref[...], _per_query_head(kp_ref), _per_query_head(kc_ref),
                                _per_query_head(vp_ref), _per_query_head(vc_ref), qnw_ref[...], knw_ref[...],
                                s_ref[...], sl_ref[...])

        @pl.when(pl.program_id(0) == NB - 1)
        def _():
            _finish(plan)

    res = pl.pallas_call(
        body, name="swa_fwd", grid=(NB,),
        in_specs=[qspec, prev(8), cur(8), prev(9), cur(9), whole, whole, col, col] + _hbm_specs(1 + ns),
        out_specs=[pl.BlockSpec((SQH, WIN, HD), lambda i: (1, i, 0))] + _hbm_specs(ns),
        out_shape=[_sds(o_buf.shape)] + _gather_shapes(shards),
        input_output_aliases={9: 0},
        scratch_shapes=_gather_sems(ns),
        compiler_params=_cparams(("arbitrary",)),
    )(zs_hm, zs_hm, zs_hm, zs_hm, zs_hm, qnw, knw, sinks_col, slopes_col, o_buf, *shards)
    return res[0], res[1:]


SWA_GRAD_HEADS = SQH + 2 * SKVH


def _swa_bwd(zs_hm, qnw, knw, sinks_col, slopes_col, dmix_hm, d_buf):
    T = zs_hm.shape[1]
    NB = T // WIN
    kit = _Kit(False)
    qspec, cur, prev, whole, col, _ = _swa_specs(lambda i: NB - 1 - i)

    def body(q_ref, kp_ref, kc_ref, vp_ref, vc_ref, qnw_ref, knw_ref, s_ref, sl_ref, do_ref, buf_ref,
             d_ref, dqnw_ref, dknw_ref, ds_ref, ck_scr, cv_scr):
        dq_ref = d_ref.at[pl.ds(0, SQH)]
        dk_ref = d_ref.at[pl.ds(SQH, SKVH)]
        dv_ref = d_ref.at[pl.ds(SQH + SKVH, SKVH)]
        i = pl.program_id(0)
        first = (i == NB - 1).astype(jnp.int32)

        @pl.when(i == 0)
        def _():
            ck_scr[...] = jnp.zeros_like(ck_scr)
            cv_scr[...] = jnp.zeros_like(cv_scr)
            ds_ref[...] = jnp.zeros_like(ds_ref)
            dqnw_ref[...] = jnp.zeros_like(dqnw_ref)
            dknw_ref[...] = jnp.zeros_like(dknw_ref)

        dq, dkp, dkc, dvp, dvc, dqnw, dknw, dsink = _swa_grads(
            kit, first, q_ref[...], _per_query_head(kp_ref), _per_query_head(kc_ref), _per_query_head(vp_ref),
            _per_query_head(vc_ref), qnw_ref[...], knw_ref[...], s_ref[...], sl_ref[...], do_ref[...])
        dq_ref[...] = dq
        dk_ref[...] = _per_kv_head(dkc) + ck_scr[...]
        dv_ref[...] = _per_kv_head(dvc) + cv_scr[...]
        ck_scr[...] = _per_kv_head(dkp)
        cv_scr[...] = _per_kv_head(dvp)
        dqnw_ref[...] += dqnw
        dknw_ref[...] += dknw
        ds_ref[...] += jnp.broadcast_to(jnp.sum(dsink, axis=1, keepdims=True), dsink.shape)

    dospec = pl.BlockSpec((SQH, WIN, HD), lambda i: (1, NB - 1 - i, 0))
    dspec = pl.BlockSpec((SWA_GRAD_HEADS, WIN, HD), lambda i: (1, NB - 1 - i, 0))
    res = pl.pallas_call(
        body, name="swa_bwd", grid=(NB,),
        in_specs=[qspec, prev(8), cur(8), prev(9), cur(9), whole, whole, col, col, dospec] + _hbm_specs(1),
        out_specs=[dspec, whole, whole, col],
        out_shape=[_sds(d_buf.shape), _sds((1, HD)), _sds((1, HD)), _sds((SQH, WIN, 1))],
        input_output_aliases={10: 0},
        scratch_shapes=[pltpu.VMEM((SKVH, WIN, HD), F32), pltpu.VMEM((SKVH, WIN, HD), F32)],
        compiler_params=_cparams(("arbitrary",)),
    )(zs_hm, zs_hm, zs_hm, zs_hm, zs_hm, qnw, knw, sinks_col, slopes_col, dmix_hm, d_buf)
    return res


GAB0 = 3 * GW + 1280


W_IN_ROWS = PROJ // N_CHIP
W_IN_ROWS_PAD = 736


def _permute_w_in_t(w_in_t):
    return jnp.concatenate([w_in_t[:4 * GW], w_in_t[4 * GW + 2 * GH:], w_in_t[4 * GW:4 * GW + 2 * GH],
                            jnp.zeros((NP - PROJ, D), w_in_t.dtype)], axis=0)


def _w_in_grad_pieces(g_t):
    g = jnp.concatenate([g_t[:4 * GW], g_t[GAB0:GAB0 + 2 * GH], g_t[4 * GW:GAB0]], axis=0)
    g = jnp.pad(g.reshape(N_CHIP, W_IN_ROWS, D), ((0, 0), (0, W_IN_ROWS_PAD - W_IN_ROWS), (0, 0)))
    return g.reshape(N_CHIP, 2, W_IN_ROWS_PAD // 2, D)


def _pieces_by_rows(g):
    return g.reshape(N_CHIP, 2, g.shape[0] // (2 * N_CHIP), D)


def _local_step(x, target, mod, n1w, w_in_pt, conv_w, alog, dtb, gnw, qnw, knw, sinks, n2w, shards):
    sh_out, sh_gate, sh_up, sh_down = shards
    T = x.shape[0]
    N = T // CHUNK
    shift1, scale1, gate1, shift2, scale2, gate2 = [mod[:, i * D:(i + 1) * D] for i in range(6)]

    h, proj, (a_out,) = _norm_in_proj(x, n1w, scale1, shift1, w_in_pt, [sh_out])
    w_out = a_out.reshape(D, D)
    qkv_hm = _conv_fwd(proj, conv_w)
    zs_hm = _split_heads(proj, 3 * GW // LANE, 20, "split_zs")
    gab = proj[:, GAB0:GAB0 + 2 * GH].T.reshape(2 * GH, N, 1, CHUNK)
    alog_b = jnp.broadcast_to(alog.reshape(GH, 1, 1), (GH, 1, CHUNK))
    dtb_b = jnp.broadcast_to(dtb.reshape(GH, 1, 1), (GH, 1, CHUNK))
    sinks_col = jnp.broadcast_to(sinks.reshape(SQH, 1, 1), (SQH, WIN, 1))
    o_hm, S_all, (a_gate, a_up) = _gdn_fwd(qkv_hm, zs_hm, gab, alog_b, dtb_b, gnw, [sh_gate, sh_up])
    w_gut = _interleave_gate_up(a_gate.reshape(DFF, D), a_up.reshape(DFF, D))
    slopes = 2.0 ** (-8.0 * (jnp.arange(SQH, dtype=F32) + 1.0) / SQH)
    slopes_col = jnp.broadcast_to(slopes.reshape(SQH, 1, 1), (SQH, WIN, 1))
    o_hm, (a_down,) = _swa_fwd(zs_hm, qnw, knw, sinks_col, slopes_col, o_hm, [sh_down])
    w_down = a_down.reshape(DFF, D)
    mixcat, mixed, x1, h2 = _out_proj_resid_norm(o_hm, w_out, x, gate1, n2w, scale2, shift2)
    ab, act = _ffn_up_act(h2, w_gut)
    dy, dffn, dgate2, loss = _ffn_down_loss(act, w_down, x1, target, gate2)

    dab = _ffn_down_dx_act(dffn, w_down, ab)
    g_w_down = _matmul(act, dffn, ta=True, out_dtype=BF16, name="ffn_down_dw")
    g_w_gut = _matmul(dab, h2, ta=True, out_dtype=BF16, name="ffn_up_dw")
    dx1, dmixed, dgate1, dn2w, dscale2, dshift2 = _ffn_up_dx_resid_bwd(dab, w_gut, x, mixed, dy, gate1, n2w, scale2,
                                                                       shift2)
    g_w_out = _matmul(mixcat, dmixed, ta=True, out_dtype=BF16, name="out_proj_dw")
    dmix_hm = _matmul_nt_heads(dmixed, w_out, "out_proj_dx")
    g_gate_t, g_up_t = _split_gate_up(g_w_gut)
    pieces = [_pieces_by_rows(g_w_out), _pieces_by_rows(g_gate_t), _pieces_by_rows(g_up_t),
              _pieces_by_rows(g_w_down)]
    (dqkv_hm, d_hm, dga, dgb, dalog, ddtb, dgnw), recv = _gdn_bwd(qkv_hm, zs_hm, gab, alog_b, dtb_b, gnw, S_all,
                                                                  dmix_hm, pieces)
    d_hm, dqnw, dknw, dsinks = _swa_bwd(zs_hm, qnw, knw, sinks_col, slopes_col, dmix_hm, d_hm)
    dproj, dconv = _conv_bwd(proj, conv_w, dqkv_hm)
    dproj = _merge_heads(d_hm, BF16, "merge_dz", into=dproj, col_block0=3 * GW // LANE, head0=0, nheads=GH)
    dproj = _merge_heads(d_hm, BF16, "merge_dswa", into=dproj, col_block0=4 * GW // LANE, head0=GH + 4,
                         nheads=SWA_GRAD_HEADS)
    dgab = jnp.concatenate([dga, dgb], axis=0).reshape(2 * GH, T).T.astype(BF16)
    dproj = lax.dynamic_update_slice(dproj, jnp.concatenate([dgab, jnp.zeros((T, NP - PROJ), BF16)], axis=1),
                                     (0, GAB0))
    g_w_in_pt = _matmul(dproj, h, ta=True, out_dtype=BF16, name="in_proj_dw")
    (grad_x, dn1w, dscale1, dshift1), recv_in = _in_proj_dx_norm_bwd(dproj, w_in_pt, x, dx1, n1w, scale1, shift1,
                                                                     [_w_in_grad_pieces(g_w_in_pt)])

    dmod = jnp.concatenate([dshift1, dscale1, dgate1, dshift2, dscale2, dgate2], axis=1)
    big = list(recv_in) + list(recv)
    small = dict(mod=dmod, norm1_w=dn1w, norm2_w=dn2w, conv_w=dconv, a_log=dalog[:, 0, 0], dt_bias=ddtb[:, 0, 0],
                 gdn_norm_w=dgnw, q_norm_w=dqnw, k_norm_w=dknw, sinks=dsinks[:, 0, 0])
    return loss, grad_x, big, small


def _adamw(w, g, m, v):
    m2 = ADAM_B1 * m + (1.0 - ADAM_B1) * g
    v2 = ADAM_B2 * v + (1.0 - ADAM_B2) * (g * g)
    m_hat = m2 / (1.0 - ADAM_B1 ** ADAM_STEP)
    v_hat = v2 / (1.0 - ADAM_B2 ** ADAM_STEP)
    delta = -ADAM_LR * (m_hat / (jnp.sqrt(v_hat) + ADAM_EPS) + ADAM_WD * w)
    return delta, m2, v2


def _reduce_adamw(recv, w, m, v, name):
    _, R, C = recv.shape
    tc = _tile(C, 256)

    def body(r_ref, w_ref, m_ref, v_ref, o_ref):
        g = r_ref[0].astype(F32)
        for s in range(1, N_DEV):
            g = g + r_ref[s].astype(F32)
        delta, m2, v2 = _adamw(w_ref[...], g, m_ref[...], v_ref[...])
        o_ref[0] = g
        o_ref[1] = delta
        o_ref[2] = m2
        o_ref[3] = v2

    col = pl.BlockSpec((R, tc), lambda j: (0, j))
    return pl.pallas_call(
        body, name=name, grid=(C // tc,),
        in_specs=[pl.BlockSpec((N_DEV, R, tc), lambda j: (0, 0, j)), col, col, col],
        out_specs=pl.BlockSpec((4, R, tc), lambda j: (0, 0, j)),
        out_shape=_sds((4, R, C)),
        compiler_params=_cparams(("parallel",)),
    )(recv, w, m, v)


def _adamw_call(g, w, m, v, name):
    def body(g_ref, w_ref, m_ref, v_ref, o_ref):
        delta, m2, v2 = _adamw(w_ref[...], g_ref[...], m_ref[...], v_ref[...])
        o_ref[0] = delta
        o_ref[1] = m2
        o_ref[2] = v2

    return pl.pallas_call(body, name=name, out_shape=_sds((3,) + g.shape))(g, w, m, v)


ADA_N = 6 * D // N_CHIP
KPAD = 128


def _w_ada_update(c8p, dm, w, m, v):
    tr = 256

    def body(c_ref, dm_ref, w_ref, m_ref, v_ref, g_ref, d_ref, m2_ref, v2_ref):
        g = _raw1(_silu(c_ref[...]), dm_ref[...], _TN)
        delta, m2, v2 = _adamw(w_ref[...], g, m_ref[...], v_ref[...])
        g_ref[...] = g
        d_ref[...] = delta
        m2_ref[...] = m2
        v2_ref[...] = v2

    blk = pl.BlockSpec((tr, ADA_N), lambda i: (i, 0))
    return pl.pallas_call(
        body, name="w_ada_update", grid=(D // tr,),
        in_specs=[pl.BlockSpec((KPAD, tr), lambda i: (0, i)), pl.BlockSpec((KPAD, ADA_N), lambda i: (0, 0)),
                  blk, blk, blk],
        out_specs=[blk] * 4, out_shape=[_sds((D, ADA_N))] * 4,
        compiler_params=_cparams(("parallel",)),
    )(c8p, dm, w, m, v)


def _me():
    return lax.axis_index("x"), lax.axis_index("y"), lax.axis_index("c")


def _peer(k, me):
    mx, my, mc = me
    return (1 - mx if k & 4 else mx, 1 - my if k & 2 else my, 1 - mc if k & 1 else mc)


def _lin(p):
    return 4 * p[0] + 2 * p[1] + p[2]


def _remote(src, dst, ssem, rsem, dev):
    return pltpu.make_async_remote_copy(src_ref=src, dst_ref=dst, send_sem=ssem, recv_sem=rsem,
                                        device_id=dev, device_id_type=MESH)


def _all_gather8(x, name):
    def body(x_ref, out_ref, send_sems, recv_sems):
        me = _me()
        out_ref[_lin(me)] = x_ref[...]
        sends = []
        for k in range(1, N_DEV):
            cp = _remote(x_ref, out_ref.at[_lin(me)], send_sems.at[k - 1], recv_sems.at[k - 1], _peer(k, me))
            cp.start()
            sends.append(cp)
        for k in range(1, N_DEV):
            p = _peer(k, me)
            _remote(x_ref, out_ref.at[_lin(p)], send_sems.at[k - 1], recv_sems.at[k - 1], p).wait_recv()
        for cp in sends:
            cp.wait_send()

    return pl.pallas_call(
        body, name=name,
        out_shape=_sds((N_DEV,) + x.shape, x.dtype),
        in_specs=[pl.BlockSpec(memory_space=pltpu.VMEM)],
        out_specs=pl.BlockSpec(memory_space=pltpu.VMEM),
        scratch_shapes=[pltpu.SemaphoreType.DMA((N_DEV - 1,)), pltpu.SemaphoreType.DMA((N_DEV - 1,))],
    )(x)


def _ag8_plan(src, out, send_sems, recv_sems):
    me = _me()
    sends, recvs = [], []
    for k in range(1, N_DEV):
        p = _peer(k, me)
        sends.append(_remote(src, out.at[_lin(me)], send_sems.at[k - 1], recv_sems.at[k - 1], p))
        recvs.append(_remote(src, out.at[_lin(p)], send_sems.at[k - 1], recv_sems.at[k - 1], p))
    return [], sends, recvs


def _prologue(c_row, conv_sh, w_ada, b_sh, w_in_sh):
    def body(c_ref, cv_ref, wa_ref, b_ref, win_ref, call_ref, cvall_ref, mods_ref, ain_ref, c16_scr, mp_scr,
             c_send, c_recv, cv_send, cv_recv, m_send, m_recv, w_send, w_recv, w_local):
        me = _lin(_me())
        w_plan = _gather_half_plan([win_ref], [ain_ref], w_send, w_recv, w_local)
        _start(w_plan)
        c_plan = _ag8_plan(c_ref, call_ref, c_send, c_recv)
        cv_plan = _ag8_plan(cv_ref, cvall_ref, cv_send, cv_recv)
        call_ref[me] = c_ref[...]
        cvall_ref[me] = cv_ref[...]
        _start(c_plan)
        _start(cv_plan)
        _finish(c_plan)
        c16_scr[...] = jnp.zeros_like(c16_scr)
        for d in range(N_DEV):
            c16_scr[pl.ds(d, 1), :] = call_ref[d]
        mp_scr[...] = _raw1(_silu(c16_scr[...]), wa_ref[...], _NN) + b_ref[...]
        mods_ref[me] = mp_scr[...]
        m_plan = _ag8_plan(mp_scr, mods_ref, m_send, m_recv)
        _start(m_plan)
        _finish(cv_plan)
        _finish(m_plan)
        _finish(w_plan)

    vmem = pl.BlockSpec(memory_space=pltpu.VMEM)
    sems = lambda n: pltpu.SemaphoreType.DMA((n,))
    return pl.pallas_call(
        body, name="prologue",
        in_specs=[vmem] * 4 + _hbm_specs(1), out_specs=[vmem] * 3 + _hbm_specs(1),
        out_shape=[_sds((N_DEV,) + c_row.shape), _sds((N_DEV,) + conv_sh.shape), _sds((N_DEV, 16, ADA_N)),
                   _sds((N_CHIP,) + w_in_sh.shape, w_in_sh.dtype)],
        scratch_shapes=[pltpu.VMEM((16, D), F32), pltpu.VMEM((16, ADA_N), F32)] + [sems(N_DEV - 1)] * 6
                       + _gather_sems(1),
        compiler_params=_cparams(),
    )(c_row, conv_sh, w_ada, b_sh, w_in_sh)


def _hbm_specs(n):
    return [pl.BlockSpec(memory_space=pl.ANY)] * n


def _gather_shapes(shards):
    return [_sds((N_CHIP,) + s.shape, s.dtype) for s in shards]


def _gather_sems(n):
    return [pltpu.SemaphoreType.DMA((3 * n,)), pltpu.SemaphoreType.DMA((3 * n,)), pltpu.SemaphoreType.DMA((n,))]


def _gather_plan(ins, outs, send_sems, recv_sems, local_sems):
    mx, my, mc = _me()
    chips = [(1 - mx, my), (mx, 1 - my), (1 - mx, 1 - my)]
    local, sends, recvs = [], [], []
    for a in range(len(ins)):
        local.append(pltpu.make_async_copy(ins[a], outs[a].at[2 * mx + my], local_sems.at[a]))
        for k, (px, py) in enumerate(chips):
            sems = (send_sems.at[3 * a + k], recv_sems.at[3 * a + k], (px, py, mc))
            sends.append(_remote(ins[a], outs[a].at[2 * mx + my], *sems))
            recvs.append(_remote(ins[a], outs[a].at[2 * px + py], *sems))
    return local, sends, recvs


def _gather_half_plan(ins, outs, send_sems, recv_sems, local_sems):
    mx, my, mc = _me()
    chips = [(1 - mx, my), (mx, 1 - my), (1 - mx, 1 - my)]
    local, sends, recvs = [], [], []
    for a in range(len(ins)):
        h = ins[a].shape[0] // 2
        mine = pl.ds(pl.multiple_of(mc * h, 16), h)
        local.append(pltpu.make_async_copy(ins[a], outs[a].at[2 * mx + my], local_sems.at[a]))
        for k, (px, py) in enumerate(chips):
            sems = (send_sems.at[3 * a + k], recv_sems.at[3 * a + k], (px, py, mc))
            sends.append(_remote(ins[a].at[mine], outs[a].at[2 * mx + my, mine], *sems))
            recvs.append(_remote(ins[a].at[mine], outs[a].at[2 * px + py, mine], *sems))
    return local, sends, recvs


def _sibling_fill(pieces):
    h = pieces.shape[1] // 2

    def body(p_ref, o_ref, send_sems, recv_sems):
        mx, my, mc = _me()
        sib = (mx, my, 1 - mc)
        chips = [(1 - mx, my), (mx, 1 - my), (1 - mx, 1 - my)]
        half = lambda c: pl.ds(pl.multiple_of(c * h, 16), h)
        o_ref[2 * mx + my] = p_ref[2 * mx + my]
        sends = []
        for k, (px, py) in enumerate(chips):
            j = 2 * px + py
            o_ref[j, half(mc), :] = p_ref[j, half(mc), :]
            cp = _remote(p_ref.at[j, half(mc)], o_ref.at[j, half(mc)], send_sems.at[k], recv_sems.at[k], sib)
            cp.start()
            sends.append(cp)
        for k, (px, py) in enumerate(chips):
            j = 2 * px + py
            _remote(p_ref.at[j, half(mc)], o_ref.at[j, half(1 - mc)], send_sems.at[k], recv_sems.at[k],
                    sib).wait_recv()
        for cp in sends:
            cp.wait_send()

    vmem = pl.BlockSpec(memory_space=pltpu.VMEM)
    return pl.pallas_call(
        body, name="sibling_fill", out_shape=_sds(pieces.shape, pieces.dtype),
        in_specs=[vmem], out_specs=vmem,
        scratch_shapes=[pltpu.SemaphoreType.DMA((N_CHIP - 1,)), pltpu.SemaphoreType.DMA((N_CHIP - 1,))],
        compiler_params=_cparams(),
    )(pieces)


def _start(plan):
    local, sends, _ = plan
    for cp in local + sends:
        cp.start()


def _finish(plan):
    local, sends, recvs = plan
    for cp in recvs:
        cp.wait_recv()
    for cp in sends:
        cp.wait_send()
    for cp in local:
        cp.wait()


def _exchange_shapes(pieces):
    return [_sds((N_DEV,) + p.shape[2:], p.dtype) for p in pieces]


def _exchange_sems(n):
    return [pltpu.SemaphoreType.DMA(((N_DEV - 1) * n,)), pltpu.SemaphoreType.DMA(((N_DEV - 1) * n,)),
            pltpu.SemaphoreType.DMA((n,))]


def _exchange_plan(ins, outs, send_sems, recv_sems, local_sems):
    me = _me()
    mx, my, mc = me
    local, sends, recvs = [], [], []
    for a in range(len(ins)):
        local.append(pltpu.make_async_copy(ins[a].at[2 * mx + my, mc], outs[a].at[_lin(me)], local_sems.at[a]))
        for k in range(1, N_DEV):
            p = _peer(k, me)
            s = (N_DEV - 1) * a + k - 1
            sends.append(_remote(ins[a].at[2 * p[0] + p[1], p[2]], outs[a].at[_lin(me)], send_sems.at[s],
                                 recv_sems.at[s], p))
            recvs.append(_remote(ins[a].at[2 * mx + my, mc], outs[a].at[_lin(p)], send_sems.at[s],
                                 recv_sems.at[s], p))
    return local, sends, recvs


REDUCE_VMEM = 56 * 1024 * 1024


def _reduce_swap(recvs):
    n = len(recvs)

    def body(*refs):
        r_refs, o_refs = refs[:n], refs[n:2 * n]
        send_sems, recv_sems = refs[2 * n:]
        mx, my, mc = _me()
        sib = (mx, my, 1 - mc)
        half = lambda a, c: o_refs[a].at[pl.ds(pl.multiple_of(c * recvs[a].shape[1], 8), recvs[a].shape[1])]
        sends = []
        for a in range(n):
            g = r_refs[a][0].astype(F32)
            for s in range(1, N_DEV):
                g = g + r_refs[a][s].astype(F32)
            half(a, mc)[...] = g
            cp = _remote(half(a, mc), half(a, mc), send_sems.at[a], recv_sems.at[a], sib)
            cp.start()
            sends.append(cp)
        for a in range(n):
            _remote(half(a, mc), half(a, 1 - mc), send_sems.at[a], recv_sems.at[a], sib).wait_recv()
        for cp in sends:
            cp.wait_send()

    vmem = pl.BlockSpec(memory_space=pltpu.VMEM)
    return pl.pallas_call(
        body, name="reduce_swap", out_shape=[_sds((2 * r.shape[1], r.shape[2])) for r in recvs],
        in_specs=[vmem] * n, out_specs=[vmem] * n,
        scratch_shapes=[pltpu.SemaphoreType.DMA((n,)), pltpu.SemaphoreType.DMA((n,))],
        compiler_params=_cparams(None, REDUCE_VMEM),
    )(*recvs)


def _adamw_big(g, w, m, v, name):
    rows, cols = g.shape
    tr = next((t for t in (256, 176, 128, 64, 8) if rows % t == 0), None)
    if tr is None:
        tc = _tile(cols, 256)
        blk, grid = pl.BlockSpec((rows, tc), lambda i: (0, i)), (cols // tc,)
    else:
        blk, grid = pl.BlockSpec((tr, cols), lambda i: (i, 0)), (rows // tr,)

    def body(g_ref, w_ref, m_ref, v_ref, go_ref, d_ref, m2_ref, v2_ref):
        g = g_ref[...]
        delta, m2, v2 = _adamw(w_ref[...], g, m_ref[...], v_ref[...])
        go_ref[...] = g
        d_ref[...] = delta
        m2_ref[...] = m2
        v2_ref[...] = v2

    return pl.pallas_call(
        body, name=name, grid=grid,
        in_specs=[blk] * 4, out_specs=[blk] * 4, out_shape=[_sds((rows, cols))] * 4,
        compiler_params=_cparams(("parallel",)),
    )(g, w, m, v)


SMALL_ORDER = (("mod", 6 * D), ("norm1_w", D), ("norm2_w", D), ("conv_w", CONVW * 3 * GW), ("a_log", GH),
               ("dt_bias", GH), ("gdn_norm_w", HD), ("q_norm_w", HD), ("k_norm_w", HD), ("sinks", SQH), ("loss", 1))
SMALL_R = 120


def _pack_small(d):
    parts = [d[k].reshape(-1).astype(F32) if k in d else jnp.zeros((n,), F32) for k, n in SMALL_ORDER]
    used = sum(n for _, n in SMALL_ORDER)
    parts.append(jnp.zeros((SMALL_R * LANE - used,), F32))
    return jnp.concatenate(parts).reshape(SMALL_R, LANE)


def _unpack_small(pk):
    flat = pk.reshape(-1)
    out, r = {}, 0
    for k, n in SMALL_ORDER:
        out[k] = flat[r:r + n]
        r += n
    return out


def kernel(x, c, w_ada, b_ada, norm1_w, w_in, conv_w, a_log, dt_bias, gdn_norm_w, q_norm_w, k_norm_w, sinks, w_out, norm2_w, w_gate, w_up, w_down, loss_target, m_w_ada, m_b_ada, m_norm1_w, m_w_in, m_conv_w, m_a_log, m_dt_bias, m_gdn_norm_w, m_q_norm_w, m_k_norm_w, m_sinks, m_w_out, m_norm2_w, m_w_gate, m_w_up, m_w_down, v_w_ada, v_b_ada, v_norm1_w, v_w_in, v_conv_w, v_a_log, v_dt_bias, v_gdn_norm_w, v_q_norm_w, v_k_norm_w, v_sinks, v_w_out, v_norm2_w, v_w_gate, v_w_up, v_w_down):
    mx, my, mc = _me()
    chip = 2 * mx + my
    dev = 4 * mx + 2 * my + mc
    T = x.shape[1]

    as_rows = lambda t, transposed: t[0].T if transposed else t[0]
    transposed = (True, False, True, True, False)
    big_w = [as_rows(t, tr) for t, tr in zip((w_in, w_out, w_gate, w_up, w_down), transposed)]
    shards = [t.astype(BF16) for t in big_w]

    b_sh = lax.dynamic_slice(b_ada, (0, chip * ADA_N), (1, ADA_N))
    w_in_sh = jnp.pad(shards[0], ((0, W_IN_ROWS_PAD - W_IN_ROWS), (0, 0)))
    c_all, conv_all, mods, a_in = _prologue(c, conv_w.reshape(CONVW, 3 * GW // N_CHIP), w_ada[0], b_sh, w_in_sh)
    c8 = c_all.reshape(N_DEV, D)
    conv_full = jnp.concatenate([conv_all[2 * j] for j in range(N_CHIP)], axis=1)
    mod = jnp.concatenate([lax.dynamic_slice(mods[2 * j], (dev, 0), (1, ADA_N)) for j in range(N_CHIP)], axis=1)
    w_in_pt = _permute_w_in_t(_sibling_fill(a_in)[:, :W_IN_ROWS].reshape(PROJ, D))

    loss, grad_x, big, small = _local_step(
        x[0], loss_target[0], mod, norm1_w, w_in_pt, conv_full, a_log, dt_bias, gdn_norm_w,
        q_norm_w, k_norm_w, sinks, norm2_w, shards[1:])

    small["loss"] = loss[:, :1]
    sg = _all_gather8(_pack_small(small), "gather_small_grads")
    rep = dict(mod=(b_ada, m_b_ada, v_b_ada), norm1_w=(norm1_w, m_norm1_w, v_norm1_w),
               norm2_w=(norm2_w, m_norm2_w, v_norm2_w), a_log=(a_log, m_a_log, v_a_log),
               dt_bias=(dt_bias, m_dt_bias, v_dt_bias), gdn_norm_w=(gdn_norm_w, m_gdn_norm_w, v_gdn_norm_w),
               q_norm_w=(q_norm_w, m_q_norm_w, v_q_norm_w), k_norm_w=(k_norm_w, m_k_norm_w, v_k_norm_w),
               sinks=(sinks, m_sinks, v_sinks))
    wmv = [_pack_small({k: t[i] for k, t in rep.items()}) for i in range(3)]
    sres = _reduce_adamw(sg, wmv[0], wmv[1], wmv[2], "small_reduce_adamw")
    s_g, s_d, s_m, s_v = [_unpack_small(sres[i]) for i in range(4)]
    loss_out = s_g["loss"][0]

    g_conv = lax.dynamic_slice(s_g["conv_w"].reshape(CONVW, 3 * GW), (0, chip * (3 * GW // N_CHIP)),
                               (CONVW, 3 * GW // N_CHIP))
    pad16 = lambda t: jnp.concatenate([t.reshape(12, LANE), jnp.zeros((4, LANE), F32)], axis=0)
    cres = _adamw_call(pad16(g_conv), pad16(conv_w), pad16(m_conv_w), pad16(v_conv_w), "conv_adamw")
    conv_out = [g_conv.reshape(conv_w.shape)] + [cres[i, :12].reshape(conv_w.shape) for i in range(3)]

    dmod8 = sg[:, :6 * D // LANE].reshape(N_DEV, 6 * D)
    dm = lax.dynamic_slice(dmod8, (0, chip * ADA_N), (N_DEV, ADA_N))
    zpad = lambda t: jnp.concatenate([t, jnp.zeros((KPAD - N_DEV, t.shape[1]), F32)], axis=0)
    ares = _w_ada_update(zpad(c8), zpad(dm), w_ada[0], m_w_ada[0], v_w_ada[0])

    names = ("w_in", "w_out", "w_gate", "w_up", "w_down")
    g_full = list(_reduce_swap(big))
    g_full[0] = g_full[0][:W_IN_ROWS]
    big_m = [as_rows(t, tr) for t, tr in zip((m_w_in, m_w_out, m_w_gate, m_w_up, m_w_down), transposed)]
    big_v = [as_rows(t, tr) for t, tr in zip((v_w_in, v_w_out, v_w_gate, v_w_up, v_w_down), transposed)]
    upd = [_adamw_big(g, w, m, v, "adamw_" + nm) for g, w, m, v, nm in zip(g_full, big_w, big_m, big_v, names)]
    back = lambda t, tr: (t.T if tr else t)[None]
    bg, bd, bm, bv = [[back(u[i], tr) for u, tr in zip(upd, transposed)] for i in range(4)]

    def group(a_i, small_d, conv_i, big_l):
        s = lambda k, ref: small_d[k].reshape(ref.shape)
        return [ares[a_i][None], s("mod", b_ada), s("norm1_w", norm1_w), big_l[0], conv_out[conv_i],
                s("a_log", a_log), s("dt_bias", dt_bias), s("gdn_norm_w", gdn_norm_w), s("q_norm_w", q_norm_w),
                s("k_norm_w", k_norm_w), s("sinks", sinks), big_l[1], s("norm2_w", norm2_w), big_l[2], big_l[3],
                big_l[4]]

    outs = [loss_out, grad_x[None]]
    outs += group(0, s_g, 0, bg) + group(1, s_d, 1, bd) + group(2, s_m, 2, bm) + group(3, s_v, 3, bv)
    return tuple(outs)
```

```python
import jax
import jax.numpy as jnp
from jax import lax
from jax.experimental import pallas as pl
from jax.experimental.pallas import tpu as pltpu

F32 = jnp.float32
BF16 = jnp.bfloat16
MESH = pl.DeviceIdType.MESH

D = 1024
HD = 64
GH = 8
GW = GH * HD
SQH = 8
SKVH = 2
SGRP = SQH // SKVH
WIN = 128
CONVW = 4
CHUNK = 64
DFF = 2816
PROJ = 2832
NP = 3072
EPS = 1e-6
N_DEV = 8
N_CHIP = 4

ADAM_LR = 0.001
ADAM_B1 = 0.9
ADAM_B2 = 0.999
ADAM_EPS = 1e-08
ADAM_WD = 0.01
ADAM_STEP = 10

VMEM_LIMIT = 48 * 1024 * 1024
GDN_BWD_VMEM = 58 * 1024 * 1024
LANE = 128


def _cparams(sem=None, vmem=VMEM_LIMIT):
    return pltpu.CompilerParams(dimension_semantics=sem, vmem_limit_bytes=vmem)


_NN = ((1,), (0,))
_NT = ((1,), (1,))
_TN = ((0,), (0,))


def _dot(a, b, dims):
    if a.ndim == 3:
        (ca,), (cb,) = dims
        return lax.dot_general(a, b, (((ca + 1,), (cb + 1,)), ((0,), (0,))), preferred_element_type=F32)
    return lax.dot_general(a, b, (dims, ((), ())), preferred_element_type=F32)


def _raw1(a, b, dims):
    return _dot(a.astype(BF16), b.astype(BF16), dims)


def _raw3(a, b, dims):
    ah = a.astype(BF16)
    al = (a - ah.astype(F32)).astype(BF16)
    bh = b.astype(BF16)
    bl = (b - bh.astype(F32)).astype(BF16)
    return _dot(ah, bh, dims) + (_dot(al, bh, dims) + _dot(ah, bl, dims))


def _make_diff_mm(raw):
    @jax.custom_vjp
    def nn(a, b):
        return raw(a, b, _NN)

    @jax.custom_vjp
    def nt(a, b):
        return raw(a, b, _NT)

    @jax.custom_vjp
    def tn(a, b):
        return raw(a, b, _TN)

    nn.defvjp(lambda a, b: (raw(a, b, _NN), (a, b)), lambda r, g: (nt(g, r[1]), tn(r[0], g)))
    nt.defvjp(lambda a, b: (raw(a, b, _NT), (a, b)), lambda r, g: (nn(g, r[1]), tn(g, r[0])))
    tn.defvjp(lambda a, b: (raw(a, b, _TN), (a, b)), lambda r, g: (nt(r[1], g), nn(r[0], g)))
    return nn, nt, tn


def _tri_inv_raw(a, nn3):
    n = a.shape[-1]
    ri = lax.broadcasted_iota(jnp.int32, (n, n), 0)
    ci = lax.broadcasted_iota(jnp.int32, (n, n), 1)
    t = (ri == ci).astype(F32)
    for lvl in range((n - 1).bit_length()):
        same_pair = (ri >> (lvl + 1)) == (ci >> (lvl + 1))
        lower_left = (((ri >> lvl) & 1) == 1) & (((ci >> lvl) & 1) == 0)
        y = jnp.where(same_pair & lower_left, a, 0.0)
        t = t - y if lvl == 0 else t - nn3(nn3(t, y), t)
    return t


class _Kit:
    def __init__(self, diff):
        if diff:
            self.nn, self.nt, self.tn = _make_diff_mm(_raw1)
            self.nn3, self.nt3, self.tn3 = _make_diff_mm(_raw3)
            nn3, nt3, tn3 = self.nn3, self.nt3, self.tn3

            @jax.custom_vjp
            def inv(a, t):
                return t

            def inv_fwd(a, t):
                return t, t

            def inv_bwd(t, g):
                return -tn3(t, nt3(g, t)), jnp.zeros_like(t)

            inv.defvjp(inv_fwd, inv_bwd)
            self.inv = inv
        else:
            self.nn = lambda a, b: _raw1(a, b, _NN)
            self.nt = lambda a, b: _raw1(a, b, _NT)
            self.tn = lambda a, b: _raw1(a, b, _TN)
            self.nn3 = lambda a, b: _raw3(a, b, _NN)
            self.nt3 = lambda a, b: _raw3(a, b, _NT)
            self.tn3 = lambda a, b: _raw3(a, b, _TN)
            self.inv = lambda a, t: _tri_inv_raw(a, self.nn3) if t is None else t


def _sigmoid(x):
    return 1.0 / (1.0 + jnp.exp(-x))


def _silu(x):
    return x * _sigmoid(x)


def _rms(x, w):
    return x * lax.rsqrt(jnp.mean(x * x, axis=-1, keepdims=True) + EPS) * w


def _tile(dim, target):
    t = (min(dim, target) // LANE) * LANE
    while t >= LANE:
        if dim % t == 0:
            return t
        t -= LANE
    return dim


MM_TM, MM_TN, MM_TK = 1408, 1536, 1408


def _matmul(a, b, ta=False, tb=False, out_dtype=F32, name="matmul", gather=None, exchange=None):
    carried = gather if gather is not None else exchange if exchange is not None else []
    nc = len(carried)
    if ta:
        K, M = a.shape
    else:
        M, K = a.shape
    if tb:
        N, K2 = b.shape
    else:
        K2, N = b.shape
    assert K == K2, (a.shape, b.shape, ta, tb)
    tm, tn, tk = _tile(M, MM_TM), _tile(N, MM_TN), _tile(K, MM_TK)
    nk = K // tk
    dims = ((0,) if ta else (1,), (1,) if tb else (0,))

    grid = (M // tm, N // tn, nk)

    def body(*refs):
        a_ref, b_ref = refs[:2]
        o_ref = refs[2 + nc]
        scratch = refs[3 + 2 * nc:]
        k = pl.program_id(2)
        if nc:
            make_plan = _gather_plan if gather is not None else _exchange_plan
            plan = make_plan(refs[2:2 + nc], refs[3 + nc:3 + 2 * nc], *scratch[-3:])
            at = lambda pos: ((pl.program_id(0) == pos[0]) & (pl.program_id(1) == pos[1]) & (k == pos[2]))

            @pl.when(at((0, 0, 0)))
            def _():
                _start(plan)

        part = _dot(a_ref[...].astype(BF16), b_ref[...].astype(BF16), dims)
        if nk == 1:
            o_ref[...] = part.astype(o_ref.dtype)
        else:
            acc_ref = scratch[0]

            @pl.when(k == 0)
            def _():
                acc_ref[...] = part

            @pl.when((k > 0) & (k < nk - 1))
            def _():
                acc_ref[...] += part

            @pl.when(k == nk - 1)
            def _():
                o_ref[...] = (acc_ref[...] + part).astype(o_ref.dtype)

        if nc:
            @pl.when(at((grid[0] - 1, grid[1] - 1, nk - 1)))
            def _():
                _finish(plan)

    a_spec = (pl.BlockSpec((tk, tm), lambda i, j, k: (k, i)) if ta
              else pl.BlockSpec((tm, tk), lambda i, j, k: (i, k)))
    b_spec = (pl.BlockSpec((tn, tk), lambda i, j, k: (j, k)) if tb
              else pl.BlockSpec((tk, tn), lambda i, j, k: (k, j)))
    if gather is not None:
        c_shapes, c_sems = _gather_shapes(carried), _gather_sems(nc)
    elif exchange is not None:
        c_shapes, c_sems = _exchange_shapes(carried), _exchange_sems(nc)
    else:
        c_shapes, c_sems = [], []
    res = pl.pallas_call(
        body, name=name, grid=grid,
        in_specs=[a_spec, b_spec] + _hbm_specs(nc),
        out_specs=[pl.BlockSpec((tm, tn), lambda i, j, k: (i, j))] + _hbm_specs(nc),
        out_shape=[jax.ShapeDtypeStruct((M, N), out_dtype)] + c_shapes,
        scratch_shapes=([pltpu.VMEM((tm, tn), F32)] if nk > 1 else []) + c_sems,
        compiler_params=_cparams(("arbitrary",) * 3 if nc else ("parallel", "parallel", "arbitrary")),
    )(a, b, *carried)
    return (res[0], res[1:]) if nc else res[0]


def _sds(shape, dtype=F32):
    return jax.ShapeDtypeStruct(shape, dtype)


def _norm_mod(x, nw, scale, shift):
    return _rms(x, nw) * (1.0 + scale) + shift


IN_PROJ_VMEM = 56 * 1024 * 1024


def _norm_in_proj(x, nw, scale, shift, w_in_pt, shards):
    T = x.shape[0]
    N = w_in_pt.shape[0]
    tm, tn = _tile(T, 1024), 3 * GW
    nm, nn = T // tm, N // tn
    nz = (GAB0 - 3 * GW) // HD
    ns = len(shards)

    def body(*refs):
        x_ref, nw_ref, sc_ref, sh_ref, w_ref = refs[:5]
        h_ref, o_ref, zs_ref = refs[5 + ns:8 + ns]
        plan = _gather_plan(refs[5:5 + ns], refs[8 + ns:8 + 2 * ns], *refs[8 + 2 * ns:])
        i, j = pl.program_id(0), pl.program_id(1)

        @pl.when((i == 0) & (j == 0))
        def _():
            _start(plan)

        @pl.when(j == 0)
        def _():
            for r0 in range(0, tm, ROWS_EPI):
                rows = pl.ds(r0, ROWS_EPI)
                h_ref[rows, :] = _norm_mod(x_ref[rows, :], nw_ref[...], sc_ref[...], sh_ref[...]).astype(BF16)

        o = _dot(h_ref[...], w_ref[...], _NT)
        o_ref[...] = o

        @pl.when(j == 1)
        def _():
            for p in range(nz // 2):
                zs_ref[2 * p], zs_ref[2 * p + 1] = _split_pair(o[:, p * LANE:(p + 1) * LANE])

        @pl.when((i == nm - 1) & (j == nn - 1))
        def _():
            _finish(plan)

    vec = pl.BlockSpec((1, D), lambda i, j: (0, 0))
    res = pl.pallas_call(
        body, name="norm1_in_proj", grid=(nm, nn),
        in_specs=[pl.BlockSpec((tm, D), lambda i, j: (i, 0)), vec, vec, vec,
                  pl.BlockSpec((tn, D), lambda i, j: (j, 0))] + _hbm_specs(ns),
        out_specs=[pl.BlockSpec((tm, D), lambda i, j: (i, 0)), pl.BlockSpec((tm, tn), lambda i, j: (i, j)),
                   pl.BlockSpec((nz, tm, HD), lambda i, j: (0, i, 0))] + _hbm_specs(ns),
        out_shape=[_sds((T, D), BF16), _sds((T, N)), _sds((nz, T, HD))] + _gather_shapes(shards),
        scratch_shapes=_gather_sems(ns),
        compiler_params=_cparams(("arbitrary", "arbitrary"), IN_PROJ_VMEM),
    )(x, nw, scale, shift, w_in_pt, *shards)
    return res[0], res[1], res[2], res[3:]


ROWS_TM = 512
ROWS_EPI = 256


def _matmul_rows(a, b, epi, tiled, consts, out_tiled, out_acc, name, pieces=()):
    T, K = a.shape
    tm, tk = _tile(T, ROWS_TM), _tile(K, MM_TK)
    nm, nk = T // tm, K // tk
    npc, nt, ncst, no, na = len(pieces), len(tiled), len(consts), len(out_tiled), len(out_acc)
    n_in = 2 + nt + ncst

    def body(*refs):
        a_ref, b_ref = refs[:2]
        t_refs, c_refs = refs[2:2 + nt], refs[2 + nt:n_in]
        o_refs = refs[n_in + npc:n_in + npc + no]
        acc_refs = refs[n_in + npc + no:n_in + npc + no + na]
        n_out = no + na + npc
        res_ref = refs[n_in + npc + n_out]
        plan = _exchange_plan(refs[n_in:n_in + npc], refs[n_in + npc + no + na:n_in + npc + n_out],
                              *refs[n_in + npc + n_out + 1:]) if npc else None
        i, k = pl.program_id(0), pl.program_id(1)

        @pl.when((i == 0) & (k == 0))
        def _():
            for r in acc_refs:
                r[...] = jnp.zeros_like(r)
            if npc:
                _start(plan)

        part = _dot(a_ref[...], b_ref[...], _NN)

        @pl.when(k == 0)
        def _():
            res_ref[...] = part

        @pl.when(k > 0)
        def _():
            res_ref[...] += part

        @pl.when(k == nk - 1)
        def _():
            for r0 in range(0, tm, ROWS_EPI):
                rows = pl.ds(r0, ROWS_EPI)
                outs = epi(res_ref[rows, :], *[r[rows, :] for r in t_refs], *[r[...] for r in c_refs])
                for r, v in zip(o_refs, outs[:no]):
                    r[rows, :] = v.astype(r.dtype)
                for r, v in zip(acc_refs, outs[no:]):
                    r[...] += v

        if npc:
            @pl.when((i == nm - 1) & (k == nk - 1))
            def _():
                _finish(plan)

    row = lambda w: pl.BlockSpec((tm, w), lambda i, k: (i, 0))
    whole = lambda s: pl.BlockSpec(s.shape, lambda i, k: (0, 0))
    res = pl.pallas_call(
        body, name=name, grid=(nm, nk),
        in_specs=[pl.BlockSpec((tm, tk), lambda i, k: (i, k)), pl.BlockSpec((tk, D), lambda i, k: (k, 0))]
                 + [row(t.shape[1]) for t in tiled] + [whole(c) for c in consts] + _hbm_specs(npc),
        out_specs=[row(s.shape[1]) for s in out_tiled] + [whole(s) for s in out_acc] + _hbm_specs(npc),
        out_shape=list(out_tiled) + list(out_acc) + (_exchange_shapes(pieces) if npc else []),
        scratch_shapes=[pltpu.VMEM((tm, D), F32)] + (_exchange_sems(npc) if npc else []),
        compiler_params=_cparams(("arbitrary", "arbitrary")),
    )(a, b, *tiled, *consts, *pieces)
    return res[:no + na], res[no + na:]


def _in_proj_dx_norm_bwd(dproj, w_in_pt, x, dres, nw, scale, shift, pieces):
    T = x.shape[0]

    def epi(dh, x, dres, nw, scale, shift):
        _, vjp = jax.vjp(_norm_mod, x, nw, scale, shift)
        dx, dnw, dsc, dsh = vjp(dh)
        return dx + dres, dnw, dsc, dsh

    return _matmul_rows(dproj, w_in_pt, epi, [x, dres], [nw, scale, shift], [_sds((T, D))], [_sds((1, D))] * 3,
                        "in_proj_dx_norm1_bwd", pieces)


def _out_proj_resid_norm(o_hm, w_out, x, gate1, nw, scale, shift):
    T = x.shape[0]
    nheads = o_hm.shape[0]
    tm = _tile(T, ROWS_TM)

    def body(o_ref, w_ref, x_ref, g_ref, nw_ref, sc_ref, sh_ref, cat_ref, mixed_ref, x1_ref, h2_ref):
        cat = jnp.concatenate([_merge_pair(o_ref[2 * p], o_ref[2 * p + 1]) for p in range(nheads // 2)], axis=1)
        cat_ref[...] = cat.astype(BF16)
        mixed_ref[...] = _dot(cat_ref[...], w_ref[...], _NN)
        for r0 in range(0, tm, ROWS_EPI):
            rows = pl.ds(r0, ROWS_EPI)
            x1, h2 = _resid_norm(x_ref[rows, :], mixed_ref[rows, :], g_ref[...], nw_ref[...], sc_ref[...], sh_ref[...])
            x1_ref[rows, :] = x1
            h2_ref[rows, :] = h2.astype(BF16)

    row = pl.BlockSpec((tm, D), lambda i: (i, 0))
    vec = pl.BlockSpec((1, D), lambda i: (0, 0))
    return pl.pallas_call(
        body, name="out_proj_resid_norm2", grid=(T // tm,),
        in_specs=[pl.BlockSpec((nheads, tm, HD), lambda i: (0, i, 0)), pl.BlockSpec((D, D), lambda i: (0, 0)), row,
                  vec, vec, vec, vec],
        out_specs=[row, row, row, row],
        out_shape=[_sds((T, D), BF16), _sds((T, D)), _sds((T, D)), _sds((T, D), BF16)],
        compiler_params=_cparams(("parallel",)),
    )(o_hm, w_out, x, gate1, nw, scale, shift)


def _ffn_up_dx_resid_bwd(dab, w_gut, x, mixed, dy, gate1, nw, scale, shift):
    T = x.shape[0]

    def epi(dh2, x, mixed, dy, gate1, nw, scale, shift):
        _, vjp = jax.vjp(_resid_norm, x, mixed, gate1, nw, scale, shift)
        return vjp((dy, dh2))

    outs, _ = _matmul_rows(dab, w_gut, epi, [x, mixed, dy], [gate1, nw, scale, shift],
                           [_sds((T, D)), _sds((T, D), BF16)], [_sds((1, D))] * 4, "ffn_up_dx_resid_norm2_bwd")
    return outs


def _ffn_down_loss(act, w_down, x1, target, gate2):
    T = x1.shape[0]

    def epi(ffn, x1, target, gate2):
        y = x1 + gate2 * ffn
        err = y - target
        loss = 0.5 * jnp.sum(jnp.sum(err * err, axis=1, keepdims=True), axis=0, keepdims=True) / D
        dy = err * (1.0 / D)
        return dy, gate2 * dy, jnp.sum(dy * ffn, axis=0, keepdims=True), jnp.broadcast_to(loss, (1, LANE))

    outs, _ = _matmul_rows(act, w_down, epi, [x1, target], [gate2], [_sds((T, D)), _sds((T, D), BF16)],
                           [_sds((1, D)), _sds((1, LANE))], "ffn_down_loss")
    return outs


def _resid_norm(x, mixed, gate1, nw, scale, shift):
    x1 = x + gate1 * mixed
    return x1, _norm_mod(x1, nw, scale, shift)


FFN_BLK = 256
FFN_TM = 2048


def _interleave_gate_up(gate_t, up_t):
    blocks = lambda t: t.reshape(DFF // FFN_BLK, 1, FFN_BLK, D)
    return jnp.concatenate([blocks(gate_t), blocks(up_t)], axis=1).reshape(2 * DFF, D)


def _split_gate_up(g):
    g = g.reshape(DFF // FFN_BLK, 2, FFN_BLK, D)
    return g[:, 0].reshape(DFF, D), g[:, 1].reshape(DFF, D)


def _ffn_up_act(h2, w_gut):
    T = h2.shape[0]
    tm = _tile(T, FFN_TM)

    def body(h_ref, w_ref, ab_ref, act_ref):
        ab = _dot(h_ref[...], w_ref[...], _NT)
        ab_ref[...] = ab
        act_ref[...] = (_silu(ab[:, :FFN_BLK]) * ab[:, FFN_BLK:]).astype(act_ref.dtype)

    return pl.pallas_call(
        body, name="ffn_up_act", grid=(T // tm, DFF // FFN_BLK),
        in_specs=[pl.BlockSpec((tm, D), lambda i, j: (i, 0)), pl.BlockSpec((2 * FFN_BLK, D), lambda i, j: (j, 0))],
        out_specs=[pl.BlockSpec((tm, 2 * FFN_BLK), lambda i, j: (i, j)), pl.BlockSpec((tm, FFN_BLK), lambda i, j: (i, j))],
        out_shape=[_sds((T, 2 * DFF)), _sds((T, DFF), BF16)],
        compiler_params=_cparams(("parallel", "parallel")),
    )(h2, w_gut)


def _ffn_down_dx_act(dffn, w_down, ab):
    T = dffn.shape[0]
    tm = _tile(T, FFN_TM)

    def body(d_ref, w_ref, ab_ref, o_ref):
        dact = _dot(d_ref[...], w_ref[...], _NT)
        a, b = ab_ref[:, :FFN_BLK], ab_ref[:, FFN_BLK:]
        s = _sigmoid(a)
        da = dact * b * (s * (1.0 + a * (1.0 - s)))
        db = dact * (a * s)
        o_ref[...] = jnp.concatenate([da, db], axis=1).astype(o_ref.dtype)

    return pl.pallas_call(
        body, name="ffn_down_dx_act", grid=(T // tm, DFF // FFN_BLK),
        in_specs=[pl.BlockSpec((tm, D), lambda i, j: (i, 0)), pl.BlockSpec((FFN_BLK, D), lambda i, j: (j, 0)),
                  pl.BlockSpec((tm, 2 * FFN_BLK), lambda i, j: (i, j))],
        out_specs=pl.BlockSpec((tm, 2 * FFN_BLK), lambda i, j: (i, j)),
        out_shape=_sds((T, 2 * DFF), BF16),
        compiler_params=_cparams(("parallel", "parallel")),
    )(dffn, w_down, ab)


def _round_bf16(x):
    return x.astype(BF16).astype(F32)


def _shift_down(x, s, rows):
    if s == 0:
        return x
    return jnp.where(rows >= s, pltpu.roll(x, s, 0), 0.0)


def _shift_up(x, s, rows, T):
    if s == 0:
        return x
    return jnp.where(rows < T - s, pltpu.roll(x, T - s, 0), 0.0)


def _conv_fwd(proj, conv_w):
    T = proj.shape[0]
    ncol = 3 * GW // LANE

    def body(x_ref, w_ref, o_ref):
        x = _round_bf16(x_ref[...])
        rows = lax.broadcasted_iota(jnp.int32, x.shape, 0)
        acc = jnp.zeros_like(x)
        for j in range(CONVW):
            acc = acc + _round_bf16(w_ref[pl.ds(j, 1), :]) * _shift_down(x, CONVW - 1 - j, rows)
        o_ref[0], o_ref[1] = _split_pair(_silu(acc))

    return pl.pallas_call(
        body, name="conv_fwd", grid=(ncol,),
        in_specs=[pl.BlockSpec((T, LANE), lambda j: (0, j)), pl.BlockSpec((CONVW, LANE), lambda j: (0, j))],
        out_specs=pl.BlockSpec((2, T, HD), lambda j: (j, 0, 0)),
        out_shape=_sds((3 * GH, T, HD)),
        compiler_params=_cparams(("parallel",)),
    )(proj, conv_w)


RELAYOUT_TM = 4096


def _split_pair(y):
    return y[:, :HD], pltpu.roll(y, HD, 1)[:, :HD]


def _merge_pair(a, b):
    return jnp.concatenate([a, b], axis=1)


def _merge_heads(hm, out_dtype, name, into=None, col_block0=0, head0=0, nheads=None):
    T = hm.shape[1]
    nheads = hm.shape[0] if nheads is None else nheads
    tm = _tile(T, RELAYOUT_TM)

    def body(*refs):
        h_ref, o_ref = refs[0], refs[-1]
        o_ref[...] = _merge_pair(h_ref[0], h_ref[1]).astype(o_ref.dtype)

    in_specs = [pl.BlockSpec((2, tm, HD), lambda j, i: (head0 // 2 + j, i, 0))]
    args = [hm]
    if into is None:
        out_shape = _sds((T, HD * nheads), out_dtype)
        aliases = {}
    else:
        out_shape = _sds(into.shape, into.dtype)
        in_specs.append(pl.BlockSpec(memory_space=pl.ANY))
        args.append(into)
        aliases = {1: 0}
    return pl.pallas_call(
        body, name=name, grid=(nheads // 2, T // tm),
        in_specs=in_specs,
        out_specs=pl.BlockSpec((tm, LANE), lambda j, i: (i, col_block0 + j)),
        out_shape=out_shape, input_output_aliases=aliases,
        compiler_params=_cparams(("parallel", "parallel")),
    )(*args)


def _matmul_nt_heads(a, b, name):
    T, K = a.shape
    N = b.shape[0]
    tm = _tile(T, 1024)

    def body(a_ref, b_ref, o_ref):
        res = _dot(a_ref[...], b_ref[...], _NT)
        for p in range(N // LANE):
            o_ref[2 * p], o_ref[2 * p + 1] = _split_pair(res[:, p * LANE:(p + 1) * LANE])

    return pl.pallas_call(
        body, name=name, grid=(T // tm,),
        in_specs=[pl.BlockSpec((tm, K), lambda i: (i, 0)), pl.BlockSpec((N, K), lambda i: (0, 0))],
        out_specs=pl.BlockSpec((N // HD, tm, HD), lambda i: (0, i, 0)),
        out_shape=_sds((N // HD, T, HD)),
        compiler_params=_cparams(("parallel",)),
    )(a, b)


def _conv_bwd(proj, conv_w, dqc):
    T = proj.shape[0]
    ncol = 3 * GW // LANE

    def body(x_ref, w_ref, d_ref, dx_ref, dw_ref):
        x = _round_bf16(x_ref[...])
        rows = lax.broadcasted_iota(jnp.int32, x.shape, 0)
        xs = [_shift_down(x, CONVW - 1 - j, rows) for j in range(CONVW)]
        w = [_round_bf16(w_ref[pl.ds(j, 1), :]) for j in range(CONVW)]
        pre = jnp.zeros_like(x)
        for j in range(CONVW):
            pre = pre + w[j] * xs[j]
        s = _sigmoid(pre)
        dpre = _round_bf16(_merge_pair(d_ref[0], d_ref[1]) * (s * (1.0 + pre * (1.0 - s))))
        dx = jnp.zeros_like(x)
        for j in range(CONVW):
            dx = dx + w[j] * _shift_up(dpre, CONVW - 1 - j, rows, T)
            dw_ref[pl.ds(j, 1), :] = jnp.sum(dpre * xs[j], axis=0, keepdims=True)
        dx_ref[...] = dx.astype(dx_ref.dtype)

    return pl.pallas_call(
        body, name="conv_bwd", grid=(ncol,),
        in_specs=[pl.BlockSpec((T, LANE), lambda j: (0, j)), pl.BlockSpec((CONVW, LANE), lambda j: (0, j)),
                  pl.BlockSpec((2, T, HD), lambda j: (j, 0, 0))],
        out_specs=[pl.BlockSpec((T, LANE), lambda j: (0, j)), pl.BlockSpec((CONVW, LANE), lambda j: (0, j))],
        out_shape=[_sds((T, NP), BF16), _sds((CONVW, 3 * GW))],
        compiler_params=_cparams(("parallel",)),
    )(proj, conv_w, dqc)


def _gdn_prep(kit, q, k, v, ga, gb, alog, dtb, t_inv=None):
    C = CHUNK
    ri = lax.broadcasted_iota(jnp.int32, (C, C), 0)
    ci = lax.broadcasted_iota(jnp.int32, (C, C), 1)
    causal = ri >= ci
    strict = ri > ci
    eye = (ri == ci).astype(F32)
    lower = causal.astype(F32)
    upper = (ri <= ci).astype(F32)

    a = ga + dtb
    softplus = jnp.maximum(a, 0.0) + jnp.log(1.0 + jnp.exp(-jnp.abs(a)))
    g_row = -jnp.exp(alog) * softplus
    beta_row = _sigmoid(gb)
    g_col = jnp.sum(eye * g_row, axis=2, keepdims=True)
    beta_col = jnp.sum(eye * beta_row, axis=2, keepdims=True)
    G_col = jnp.sum(lower * g_row, axis=2, keepdims=True)
    G_row = jnp.sum(upper * g_col, axis=1, keepdims=True)
    G_last = jnp.sum(g_row, axis=2, keepdims=True)
    decay = jnp.exp(jnp.where(causal, G_col - G_row, -1e30))

    qn = q * lax.rsqrt(jnp.sum(q * q, axis=-1, keepdims=True) + EPS) * (HD ** -0.5)
    kn = k * lax.rsqrt(jnp.sum(k * k, axis=-1, keepdims=True) + EPS)
    kb = kn * beta_col
    A = jnp.where(strict, kit.nt(kb, kn) * decay, 0.0)
    Tm = kit.inv(A, t_inv)
    eG = jnp.exp(G_col)
    u = kit.nn3(Tm, v * beta_col)
    w = kit.nn3(Tm, kb * eG)
    qk = jnp.where(causal, kit.nt(qn, kn) * decay, 0.0)
    q_dec = qn * eG
    k_dec = kn * jnp.exp(G_last - G_col)
    dec = jnp.exp(G_last)
    return u, w, qk, q_dec, k_dec, dec, Tm


def _gdn_out(o, z, nw):
    return _rms(o, nw) * _silu(z)


GDN_CB = 4


def _gdn_specs(T, blk):
    TB = GDN_CB * CHUNK
    seq = lambda grp: pl.BlockSpec((GH, TB, HD), lambda i, grp=grp: (grp, blk(i), 0))
    row = lambda grp: pl.BlockSpec((GH, GDN_CB, 1, CHUNK), lambda i, grp=grp: (grp, blk(i), 0, 0))
    per_head = pl.BlockSpec((GH, 1, CHUNK), lambda i: (0, 0, 0))
    whole = pl.BlockSpec((1, HD), lambda i: (0, 0))
    state = pl.BlockSpec((GH, GDN_CB, HD, HD), lambda i: (0, blk(i), 0, 0))
    return seq, row, per_head, whole, state


def _gdn_load(seq_refs, row_refs, head_refs):
    chunks = lambda r: jnp.concatenate([r[:, pl.ds(cb * CHUNK, CHUNK), :] for cb in range(GDN_CB)], axis=0)
    rows = lambda r: jnp.concatenate([r[:, cb] for cb in range(GDN_CB)], axis=0)
    heads = lambda r: jnp.concatenate([r[...]] * GDN_CB, axis=0)
    return [chunks(r) for r in seq_refs], [rows(r) for r in row_refs], [heads(r) for r in head_refs]


def _gdn_fwd(qkv_hm, zs_hm, gab, alog_b, dtb_b, nw, shards):
    T = qkv_hm.shape[1]
    N = T // CHUNK
    nblk = N // GDN_CB
    ns = len(shards)
    seq, row, per_head, whole, state = _gdn_specs(T, lambda i: i)
    kit = _Kit(False)

    def body(*refs):
        q_ref, k_ref, v_ref, z_ref, ga_ref, gb_ref, al_ref, dt_ref, nw_ref = refs[:9]
        o_ref, S_ref, T_ref = refs[9 + ns:12 + ns]
        S_scr = refs[12 + 2 * ns]
        plan = _gather_plan(refs[9:9 + ns], refs[12 + ns:12 + 2 * ns], *refs[13 + 2 * ns:])

        @pl.when(pl.program_id(0) == 0)
        def _():
            S_scr[...] = jnp.zeros_like(S_scr)
            _start(plan)

        (q, k, v, z), (ga, gb), (al, dt) = _gdn_load((q_ref, k_ref, v_ref, z_ref), (ga_ref, gb_ref), (al_ref, dt_ref))
        u, w, qk, q_dec, k_dec, dec, t_inv = _gdn_prep(kit, q, k, v, ga, gb, al, dt)
        S = S_scr[...]
        for cb in range(GDN_CB):
            hs = slice(cb * GH, (cb + 1) * GH)
            S_ref[:, cb] = S
            T_ref[:, cb] = t_inv[hs]
            v_new = u[hs] - kit.nn(w[hs], S)
            o = kit.nn(q_dec[hs], S) + kit.nn(qk[hs], v_new)
            S = S * dec[hs] + kit.tn(k_dec[hs], v_new)
            o_ref[:, pl.ds(cb * CHUNK, CHUNK), :] = _gdn_out(o, z[hs], nw_ref[...])
        S_scr[...] = S

        @pl.when(pl.program_id(0) == nblk - 1)
        def _():
            _finish(plan)

    res = pl.pallas_call(
        body, name="gdn_fwd", grid=(nblk,),
        in_specs=[seq(0), seq(1), seq(2), seq(0), row(0), row(1), per_head, per_head, whole] + _hbm_specs(ns),
        out_specs=[seq(0), state, state] + _hbm_specs(ns),
        out_shape=[_sds((GH + SQH, T, HD)), _sds((GH, N, HD, HD)), _sds((GH, N, CHUNK, CHUNK))]
                  + _gather_shapes(shards),
        scratch_shapes=[pltpu.VMEM((GH, HD, HD), F32)] + _gather_sems(ns),
        compiler_params=_cparams(("arbitrary",)),
    )(qkv_hm, qkv_hm, qkv_hm, zs_hm, gab, gab, alog_b, dtb_b, nw, *shards)
    return res[0], (res[1], res[2]), res[3:]


def _gdn_bwd(qkv_hm, zs_hm, gab, alog_b, dtb_b, nw, S_all, do, pieces):
    T = qkv_hm.shape[1]
    N = T // CHUNK
    nblk = N // GDN_CB
    npc = len(pieces)
    dkit, kit = _Kit(True), _Kit(False)
    rseq, rrow, per_head, whole, rstate = _gdn_specs(T, lambda i: nblk - 1 - i)

    def body(*refs):
        q_ref, k_ref, v_ref, z_ref, ga_ref, gb_ref, al_ref, dt_ref, nw_ref, S_ref, T_ref, do_ref = refs[:12]
        dqkv_ref, dz_ref, dga_ref, dgb_ref, dal_ref, ddt_ref, dnw_ref = refs[12 + npc:19 + npc]
        dS_scr = refs[19 + 2 * npc]
        plan = _exchange_plan(refs[12:12 + npc], refs[19 + npc:19 + 2 * npc], *refs[20 + 2 * npc:])

        @pl.when(pl.program_id(0) == 0)
        def _():
            dS_scr[...] = jnp.zeros_like(dS_scr)
            dal_ref[...] = jnp.zeros_like(dal_ref)
            ddt_ref[...] = jnp.zeros_like(ddt_ref)
            dnw_ref[...] = jnp.zeros_like(dnw_ref)
            _start(plan)

        (q, k, v, z, dout), (ga, gb), (al, dt) = _gdn_load((q_ref, k_ref, v_ref, z_ref, do_ref), (ga_ref, gb_ref),
                                                          (al_ref, dt_ref))
        S_in = jnp.concatenate([S_ref[:, cb] for cb in range(GDN_CB)], axis=0)
        t_inv = jnp.concatenate([T_ref[:, cb] for cb in range(GDN_CB)], axis=0)
        prep = lambda *a: _gdn_prep(dkit, *a, t_inv=t_inv)[:6]
        (u, w, qk, q_dec, k_dec, dec), prep_vjp = jax.vjp(prep, q, k, v, ga, gb, al, dt)
        v_new = u - kit.nn(w, S_in)
        o = kit.nn(q_dec, S_in) + kit.nn(qk, v_new)
        _, out_vjp = jax.vjp(_gdn_out, o, z, nw_ref[...])
        do, dz, dnw = out_vjp(dout)
        dvn_part = kit.tn(qk, do)
        dS_part = kit.tn(q_dec, do)
        dS = dS_scr[...]
        dS_out, dvn = [None] * GDN_CB, [None] * GDN_CB
        for cb in reversed(range(GDN_CB)):
            hs = slice(cb * GH, (cb + 1) * GH)
            dS_out[cb] = dS
            dvn[cb] = dvn_part[hs] + kit.nn(k_dec[hs], dS)
            dS = dS * dec[hs] + dS_part[hs] - kit.tn(w[hs], dvn[cb])
        dS_scr[...] = dS
        dS_out = jnp.concatenate(dS_out, axis=0)
        dvn = jnp.concatenate(dvn, axis=0)
        ddec = jnp.sum(jnp.sum(S_in * dS_out, axis=2, keepdims=True), axis=1, keepdims=True)
        cts = (dvn, -kit.nt(dvn, S_in), kit.nt(do, v_new), kit.nt(do, S_in), kit.nt(v_new, dS_out), ddec)
        dq, dk, dv, dga, dgb, dal, ddt = prep_vjp(cts)
        lanesum = lambda t: jnp.broadcast_to(jnp.sum(t, axis=2, keepdims=True), t.shape)
        for cb in range(GDN_CB):
            hs = slice(cb * GH, (cb + 1) * GH)
            sl = pl.ds(cb * CHUNK, CHUNK)
            dqkv_ref[pl.ds(0, GH), sl, :] = dq[hs]
            dqkv_ref[pl.ds(GH, GH), sl, :] = dk[hs]
            dqkv_ref[pl.ds(2 * GH, GH), sl, :] = dv[hs]
            dz_ref[:, sl, :] = dz[hs]
            dga_ref[:, cb] = dga[hs]
            dgb_ref[:, cb] = dgb[hs]
            dal_ref[...] += lanesum(dal[hs])
            ddt_ref[...] += lanesum(ddt[hs])
        dnw_ref[...] += dnw

        @pl.when(pl.program_id(0) == nblk - 1)
        def _():
            _finish(plan)

    res = pl.pallas_call(
        body, name="gdn_bwd", grid=(nblk,),
        in_specs=[rseq(0), rseq(1), rseq(2), rseq(0), rrow(0), rrow(1), per_head, per_head, whole, rstate, rstate,
                  rseq(0)] + _hbm_specs(npc),
        out_specs=[pl.BlockSpec((3 * GH, GDN_CB * CHUNK, HD), lambda i: (0, nblk - 1 - i, 0)), rseq(0), rrow(0),
                   rrow(0), per_head, per_head, whole] + _hbm_specs(npc),
        out_shape=[_sds((3 * GH, T, HD)), _sds((GH + 4 + SWA_GRAD_HEADS, T, HD))] + [_sds((GH, N, 1, CHUNK))] * 2
                  + [_sds((GH, 1, CHUNK))] * 2 + [_sds((1, HD))] + _exchange_shapes(pieces),
        scratch_shapes=[pltpu.VMEM((GH, HD, HD), F32)] + _exchange_sems(npc),
        compiler_params=_cparams(("arbitrary",), GDN_BWD_VMEM),
    )(qkv_hm, qkv_hm, qkv_hm, zs_hm, gab, gab, alog_b, dtb_b, nw, S_all[0], S_all[1], do, *pieces)
    return res[:7], res[7:]


def _swa_heads(kit, first, q, kp, kc, vp, vc, qnw, knw, sink, slope):
    W = WIN
    ri = lax.broadcasted_iota(jnp.int32, (W, W), 0)
    ci = lax.broadcasted_iota(jnp.int32, (W, W), 1)
    mask_c = ri >= ci
    mask_p = ci > ri + first * W
    dist_c = (ri - ci).astype(F32)
    dist_p = (ri - ci + W).astype(F32)
    kpn = _rms(kp, knw)
    kcn = _rms(kc, knw)
    qn = _rms(q, qnw)
    sc = jnp.where(mask_c, kit.nt(qn, kcn) * (HD ** -0.5) - slope * dist_c, -1e30)
    sp = jnp.where(mask_p, kit.nt(qn, kpn) * (HD ** -0.5) - slope * dist_p, -1e30)
    m = jnp.maximum(jnp.maximum(jnp.max(sc, axis=-1, keepdims=True), jnp.max(sp, axis=-1, keepdims=True)), sink)
    m = lax.stop_gradient(m)
    pc = jnp.exp(sc - m)
    pp = jnp.exp(sp - m)
    den = jnp.sum(pc, axis=-1, keepdims=True) + jnp.sum(pp, axis=-1, keepdims=True) + jnp.exp(sink - m)
    inv = 1.0 / den
    return kit.nn(pc * inv, vc) + kit.nn(pp * inv, vp)


def _swa_grads(kit, first, q, kp, kc, vp, vc, qnw, knw, sink, slope, do):
    W = WIN
    ri = lax.broadcasted_iota(jnp.int32, (W, W), 0)
    ci = lax.broadcasted_iota(jnp.int32, (W, W), 1)
    mask_c = ri >= ci
    mask_p = ci > ri + first * W
    dist_c = (ri - ci).astype(F32)
    dist_p = (ri - ci + W).astype(F32)
    scale = HD ** -0.5
    kpn, kp_vjp = jax.vjp(_rms, kp, knw)
    kcn, kc_vjp = jax.vjp(_rms, kc, knw)
    qn, q_vjp = jax.vjp(_rms, q, qnw)
    sc = jnp.where(mask_c, kit.nt(qn, kcn) * scale - slope * dist_c, -1e30)
    sp = jnp.where(mask_p, kit.nt(qn, kpn) * scale - slope * dist_p, -1e30)
    m = jnp.maximum(jnp.maximum(jnp.max(sc, axis=-1, keepdims=True), jnp.max(sp, axis=-1, keepdims=True)), sink)
    ec = jnp.exp(sc - m)
    ep = jnp.exp(sp - m)
    es = jnp.exp(sink - m)
    inv = 1.0 / (jnp.sum(ec, axis=-1, keepdims=True) + jnp.sum(ep, axis=-1, keepdims=True) + es)
    pc, pp = ec * inv, ep * inv
    dpc, dpp = kit.nt(do, vc), kit.nt(do, vp)
    delta = jnp.sum(dpc * pc, axis=-1, keepdims=True) + jnp.sum(dpp * pp, axis=-1, keepdims=True)
    dsc = pc * (dpc - delta) * scale
    dsp = pp * (dpp - delta) * scale
    dq, dqnw = q_vjp(kit.nn(dsc, kcn) + kit.nn(dsp, kpn))
    dkc, dknw_c = kc_vjp(kit.tn(dsc, qn))
    dkp, dknw_p = kp_vjp(kit.tn(dsp, qn))
    return dq, dkp, dkc, kit.tn(pp, do), kit.tn(pc, do), dqnw, dknw_c + dknw_p, -(es * inv) * delta


def _per_query_head(kv_ref):
    return jnp.concatenate([kv_ref[pl.ds(h // SGRP, 1)] for h in range(SQH)], axis=0)


def _per_kv_head(d):
    return jnp.concatenate([jnp.sum(d[g * SGRP:(g + 1) * SGRP], axis=0, keepdims=True) for g in range(SKVH)], axis=0)


def _swa_specs(blk):
    qspec = pl.BlockSpec((SQH, WIN, HD), lambda i: (1, blk(i), 0))
    cur = lambda grp: pl.BlockSpec((SKVH, WIN, HD), lambda i, grp=grp: (grp, blk(i), 0))
    prev = lambda grp: pl.BlockSpec((SKVH, WIN, HD), lambda i, grp=grp: (grp, jnp.maximum(blk(i) - 1, 0), 0))
    whole = pl.BlockSpec((1, HD), lambda i: (0, 0))
    col = pl.BlockSpec((SQH, WIN, 1), lambda i: (0, 0, 0))
    ospec = pl.BlockSpec((SQH, WIN, HD), lambda i: (0, blk(i), 0))
    return qspec, cur, prev, whole, col, ospec


def _swa_fwd(zs_hm, qnw, knw, sinks_col, slopes_col, o_buf, shards):
    T = zs_hm.shape[1]
    NB = T // WIN
    ns = len(shards)
    kit = _Kit(False)
    qspec, cur, prev, whole, col, _ = _swa_specs(lambda i: i)

    def body(*refs):
        q_ref, kp_ref, kc_ref, vp_ref, vc_ref, qnw_ref, knw_ref, s_ref, sl_ref = refs[:9]
        o_ref = refs[10 + ns]
        plan = _gather_plan(refs[10:10 + ns], refs[11 + ns:11 + 2 * ns], *refs[11 + 2 * ns:])

        @pl.when(pl.program_id(0) == 0)
        def _():
            _start(plan)

        first = (pl.program_id(0) == 0).astype(jnp.int32)
        o_ref[...] = _swa_heads(kit, first, q_ref[...], _per_query_head(kp_ref), _per_query_head(kc_ref),
                                _per_query_head(vp_ref), _per_query_head(vc_ref), qnw_ref[...], knw_ref[...],
                                s_ref[...], sl_ref[...])

        @pl.when(pl.program_id(0) == NB - 1)
        def _():
            _finish(plan)

    res = pl.pallas_call(
        body, name="swa_fwd", grid=(NB,),
        in_specs=[qspec, prev(8), cur(8), prev(9), cur(9), whole, whole, col, col] + _hbm_specs(1 + ns),
        out_specs=[pl.BlockSpec((SQH, WIN, HD), lambda i: (1, i, 0))] + _hbm_specs(ns),
        out_shape=[_sds(o_buf.shape)] + _gather_shapes(shards),
        input_output_aliases={9: 0},
        scratch_shapes=_gather_sems(ns),
        compiler_params=_cparams(("arbitrary",)),
    )(zs_hm, zs_hm, zs_hm, zs_hm, zs_hm, qnw, knw, sinks_col, slopes_col, o_buf, *shards)
    return res[0], res[1:]


SWA_GRAD_HEADS = SQH + 2 * SKVH


def _swa_bwd(zs_hm, qnw, knw, sinks_col, slopes_col, dmix_hm, d_buf):
    T = zs_hm.shape[1]
    NB = T // WIN
    kit = _Kit(False)
    qspec, cur, prev, whole, col, _ = _swa_specs(lambda i: NB - 1 - i)

    def body(q_ref, kp_ref, kc_ref, vp_ref, vc_ref, qnw_ref, knw_ref, s_ref, sl_ref, do_ref, buf_ref,
             d_ref, dqnw_ref, dknw_ref, ds_ref, ck_scr, cv_scr):
        dq_ref = d_ref.at[pl.ds(0, SQH)]
        dk_ref = d_ref.at[pl.ds(SQH, SKVH)]
        dv_ref = d_ref.at[pl.ds(SQH + SKVH, SKVH)]
        i = pl.program_id(0)
        first = (i == NB - 1).astype(jnp.int32)

        @pl.when(i == 0)
        def _():
            ck_scr[...] = jnp.zeros_like(ck_scr)
            cv_scr[...] = jnp.zeros_like(cv_scr)
            ds_ref[...] = jnp.zeros_like(ds_ref)
            dqnw_ref[...] = jnp.zeros_like(dqnw_ref)
            dknw_ref[...] = jnp.zeros_like(dknw_ref)

        dq, dkp, dkc, dvp, dvc, dqnw, dknw, dsink = _swa_grads(
            kit, first, q_ref[...], _per_query_head(kp_ref), _per_query_head(kc_ref), _per_query_head(vp_ref),
            _per_query_head(vc_ref), qnw_ref[...], knw_ref[...], s_ref[...], sl_ref[...], do_ref[...])
        dq_ref[...] = dq
        dk_ref[...] = _per_kv_head(dkc) + ck_scr[...]
        dv_ref[...] = _per_kv_head(dvc) + cv_scr[...]
        ck_scr[...] = _per_kv_head(dkp)
        cv_scr[...] = _per_kv_head(dvp)
        dqnw_ref[...] += dqnw
        dknw_ref[...] += dknw
        ds_ref[...] += jnp.broadcast_to(jnp.sum(dsink, axis=1, keepdims=True), dsink.shape)

    dospec = pl.BlockSpec((SQH, WIN, HD), lambda i: (1, NB - 1 - i, 0))
    dspec = pl.BlockSpec((SWA_GRAD_HEADS, WIN, HD), lambda i: (1, NB - 1 - i, 0))
    res = pl.pallas_call(
        body, name="swa_bwd", grid=(NB,),
        in_specs=[qspec, prev(8), cur(8), prev(9), cur(9), whole, whole, col, col, dospec] + _hbm_specs(1),
        out_specs=[dspec, whole, whole, col],
        out_shape=[_sds(d_buf.shape), _sds((1, HD)), _sds((1, HD)), _sds((SQH, WIN, 1))],
        input_output_aliases={10: 0},
        scratch_shapes=[pltpu.VMEM((SKVH, WIN, HD), F32), pltpu.VMEM((SKVH, WIN, HD), F32)],
        compiler_params=_cparams(("arbitrary",)),
    )(zs_hm, zs_hm, zs_hm, zs_hm, zs_hm, qnw, knw, sinks_col, slopes_col, dmix_hm, d_buf)
    return res


GAB0 = 3 * GW + 1280


W_IN_ROWS = PROJ // N_CHIP
W_IN_ROWS_PAD = 736


def _permute_w_in_t(w_in_t):
    return jnp.concatenate([w_in_t[:4 * GW], w_in_t[4 * GW + 2 * GH:], w_in_t[4 * GW:4 * GW + 2 * GH],
                            jnp.zeros((NP - PROJ, D), w_in_t.dtype)], axis=0)


def _w_in_grad_pieces(g_t):
    g = jnp.concatenate([g_t[:4 * GW], g_t[GAB0:GAB0 + 2 * GH], g_t[4 * GW:GAB0]], axis=0)
    g = jnp.pad(g.reshape(N_CHIP, W_IN_ROWS, D), ((0, 0), (0, W_IN_ROWS_PAD - W_IN_ROWS), (0, 0)))
    return g.reshape(N_CHIP, 2, W_IN_ROWS_PAD // 2, D)


def _pieces_by_rows(g):
    return g.reshape(N_CHIP, 2, g.shape[0] // (2 * N_CHIP), D)


def _local_step(x, target, mod, n1w, w_in_pt, conv_w, alog, dtb, gnw, qnw, knw, sinks, n2w, shards):
    sh_out, sh_gate, sh_up, sh_down = shards
    T = x.shape[0]
    N = T // CHUNK
    shift1, scale1, gate1, shift2, scale2, gate2 = [mod[:, i * D:(i + 1) * D] for i in range(6)]

    h, proj, zs_hm, (a_out,) = _norm_in_proj(x, n1w, scale1, shift1, w_in_pt, [sh_out])
    w_out = a_out.reshape(D, D)
    qkv_hm = _conv_fwd(proj, conv_w)
    gab = proj[:, GAB0:GAB0 + 2 * GH].T.reshape(2 * GH, N, 1, CHUNK)
    alog_b = jnp.broadcast_to(alog.reshape(GH, 1, 1), (GH, 1, CHUNK))
    dtb_b = jnp.broadcast_to(dtb.reshape(GH, 1, 1), (GH, 1, CHUNK))
    sinks_col = jnp.broadcast_to(sinks.reshape(SQH, 1, 1), (SQH, WIN, 1))
    o_hm, S_all, (a_gate, a_up) = _gdn_fwd(qkv_hm, zs_hm, gab, alog_b, dtb_b, gnw, [sh_gate, sh_up])
    w_gut = _interleave_gate_up(a_gate.reshape(DFF, D), a_up.reshape(DFF, D))
    slopes = 2.0 ** (-8.0 * (jnp.arange(SQH, dtype=F32) + 1.0) / SQH)
    slopes_col = jnp.broadcast_to(slopes.reshape(SQH, 1, 1), (SQH, WIN, 1))
    o_hm, (a_down,) = _swa_fwd(zs_hm, qnw, knw, sinks_col, slopes_col, o_hm, [sh_down])
    w_down = a_down.reshape(DFF, D)
    mixcat, mixed, x1, h2 = _out_proj_resid_norm(o_hm, w_out, x, gate1, n2w, scale2, shift2)
    ab, act = _ffn_up_act(h2, w_gut)
    dy, dffn, dgate2, loss = _ffn_down_loss(act, w_down, x1, target, gate2)

    dab = _ffn_down_dx_act(dffn, w_down, ab)
    g_w_down = _matmul(act, dffn, ta=True, out_dtype=BF16, name="ffn_down_dw")
    g_w_gut = _matmul(dab, h2, ta=True, out_dtype=BF16, name="ffn_up_dw")
    dx1, dmixed, dgate1, dn2w, dscale2, dshift2 = _ffn_up_dx_resid_bwd(dab, w_gut, x, mixed, dy, gate1, n2w, scale2,
                                                                       shift2)
    g_w_out = _matmul(mixcat, dmixed, ta=True, out_dtype=BF16, name="out_proj_dw")
    dmix_hm = _matmul_nt_heads(dmixed, w_out, "out_proj_dx")
    g_gate_t, g_up_t = _split_gate_up(g_w_gut)
    pieces = [_pieces_by_rows(g_w_out), _pieces_by_rows(g_gate_t), _pieces_by_rows(g_up_t),
              _pieces_by_rows(g_w_down)]
    (dqkv_hm, d_hm, dga, dgb, dalog, ddtb, dgnw), recv = _gdn_bwd(qkv_hm, zs_hm, gab, alog_b, dtb_b, gnw, S_all,
                                                                  dmix_hm, pieces)
    d_hm, dqnw, dknw, dsinks = _swa_bwd(zs_hm, qnw, knw, sinks_col, slopes_col, dmix_hm, d_hm)
    dproj, dconv = _conv_bwd(proj, conv_w, dqkv_hm)
    dproj = _merge_heads(d_hm, BF16, "merge_dz", into=dproj, col_block0=3 * GW // LANE, head0=0, nheads=GH)
    dproj = _merge_heads(d_hm, BF16, "merge_dswa", into=dproj, col_block0=4 * GW // LANE, head0=GH + 4,
                         nheads=SWA_GRAD_HEADS)
    dgab = jnp.concatenate([dga, dgb], axis=0).reshape(2 * GH, T).T.astype(BF16)
    dproj = lax.dynamic_update_slice(dproj, jnp.concatenate([dgab, jnp.zeros((T, NP - PROJ), BF16)], axis=1),
                                     (0, GAB0))
    g_w_in_pt = _matmul(dproj, h, ta=True, out_dtype=BF16, name="in_proj_dw")
    (grad_x, dn1w, dscale1, dshift1), recv_in = _in_proj_dx_norm_bwd(dproj, w_in_pt, x, dx1, n1w, scale1, shift1,
                                                                     [_w_in_grad_pieces(g_w_in_pt)])

    dmod = jnp.concatenate([dshift1, dscale1, dgate1, dshift2, dscale2, dgate2], axis=1)
    big = list(recv_in) + list(recv)
    small = dict(mod=dmod, norm1_w=dn1w, norm2_w=dn2w, conv_w=dconv, a_log=dalog[:, 0, 0], dt_bias=ddtb[:, 0, 0],
                 gdn_norm_w=dgnw, q_norm_w=dqnw, k_norm_w=dknw, sinks=dsinks[:, 0, 0])
    return loss, grad_x, big, small


def _adamw(w, g, m, v):
    m2 = ADAM_B1 * m + (1.0 - ADAM_B1) * g
    v2 = ADAM_B2 * v + (1.0 - ADAM_B2) * (g * g)
    m_hat = m2 / (1.0 - ADAM_B1 ** ADAM_STEP)
    v_hat = v2 / (1.0 - ADAM_B2 ** ADAM_STEP)
    delta = -ADAM_LR * (m_hat / (jnp.sqrt(v_hat) + ADAM_EPS) + ADAM_WD * w)
    return delta, m2, v2


def _reduce_adamw(recv, w, m, v, name):
    _, R, C = recv.shape
    tc = _tile(C, 256)

    def body(r_ref, w_ref, m_ref, v_ref, o_ref):
        g = r_ref[0].astype(F32)
        for s in range(1, N_DEV):
            g = g + r_ref[s].astype(F32)
        delta, m2, v2 = _adamw(w_ref[...], g, m_ref[...], v_ref[...])
        o_ref[0] = g
        o_ref[1] = delta
        o_ref[2] = m2
        o_ref[3] = v2

    col = pl.BlockSpec((R, tc), lambda j: (0, j))
    return pl.pallas_call(
        body, name=name, grid=(C // tc,),
        in_specs=[pl.BlockSpec((N_DEV, R, tc), lambda j: (0, 0, j)), col, col, col],
        out_specs=pl.BlockSpec((4, R, tc), lambda j: (0, 0, j)),
        out_shape=_sds((4, R, C)),
        compiler_params=_cparams(("parallel",)),
    )(recv, w, m, v)


def _adamw_call(g, w, m, v, name):
    def body(g_ref, w_ref, m_ref, v_ref, o_ref):
        delta, m2, v2 = _adamw(w_ref[...], g_ref[...], m_ref[...], v_ref[...])
        o_ref[0] = delta
        o_ref[1] = m2
        o_ref[2] = v2

    return pl.pallas_call(body, name=name, out_shape=_sds((3,) + g.shape))(g, w, m, v)


ADA_N = 6 * D // N_CHIP
KPAD = 128


def _w_ada_update(c8p, dm, w, m, v):
    tr = 256

    def body(c_ref, dm_ref, w_ref, m_ref, v_ref, g_ref, d_ref, m2_ref, v2_ref):
        g = _raw1(_silu(c_ref[...]), dm_ref[...], _TN)
        delta, m2, v2 = _adamw(w_ref[...], g, m_ref[...], v_ref[...])
        g_ref[...] = g
        d_ref[...] = delta
        m2_ref[...] = m2
        v2_ref[...] = v2

    blk = pl.BlockSpec((tr, ADA_N), lambda i: (i, 0))
    return pl.pallas_call(
        body, name="w_ada_update", grid=(D // tr,),
        in_specs=[pl.BlockSpec((KPAD, tr), lambda i: (0, i)), pl.BlockSpec((KPAD, ADA_N), lambda i: (0, 0)),
                  blk, blk, blk],
        out_specs=[blk] * 4, out_shape=[_sds((D, ADA_N))] * 4,
        compiler_params=_cparams(("parallel",)),
    )(c8p, dm, w, m, v)


def _me():
    return lax.axis_index("x"), lax.axis_index("y"), lax.axis_index("c")


def _peer(k, me):
    mx, my, mc = me
    return (1 - mx if k & 4 else mx, 1 - my if k & 2 else my, 1 - mc if k & 1 else mc)


def _lin(p):
    return 4 * p[0] + 2 * p[1] + p[2]


def _remote(src, dst, ssem, rsem, dev):
    return pltpu.make_async_remote_copy(src_ref=src, dst_ref=dst, send_sem=ssem, recv_sem=rsem,
                                        device_id=dev, device_id_type=MESH)


def _all_gather8(x, name):
    def body(x_ref, out_ref, send_sems, recv_sems):
        me = _me()
        out_ref[_lin(me)] = x_ref[...]
        sends = []
        for k in range(1, N_DEV):
            cp = _remote(x_ref, out_ref.at[_lin(me)], send_sems.at[k - 1], recv_sems.at[k - 1], _peer(k, me))
            cp.start()
            sends.append(cp)
        for k in range(1, N_DEV):
            p = _peer(k, me)
            _remote(x_ref, out_ref.at[_lin(p)], send_sems.at[k - 1], recv_sems.at[k - 1], p).wait_recv()
        for cp in sends:
            cp.wait_send()

    return pl.pallas_call(
        body, name=name,
        out_shape=_sds((N_DEV,) + x.shape, x.dtype),
        in_specs=[pl.BlockSpec(memory_space=pltpu.VMEM)],
        out_specs=pl.BlockSpec(memory_space=pltpu.VMEM),
        scratch_shapes=[pltpu.SemaphoreType.DMA((N_DEV - 1,)), pltpu.SemaphoreType.DMA((N_DEV - 1,))],
    )(x)


def _ag8_plan(src, out, send_sems, recv_sems):
    me = _me()
    sends, recvs = [], []
    for k in range(1, N_DEV):
        p = _peer(k, me)
        sends.append(_remote(src, out.at[_lin(me)], send_sems.at[k - 1], recv_sems.at[k - 1], p))
        recvs.append(_remote(src, out.at[_lin(p)], send_sems.at[k - 1], recv_sems.at[k - 1], p))
    return [], sends, recvs


def _prologue(c_row, conv_sh, w_ada, b_sh, w_in_sh):
    def body(c_ref, cv_ref, wa_ref, b_ref, win_ref, call_ref, cvall_ref, mods_ref, ain_ref, c16_scr, mp_scr,
             c_send, c_recv, cv_send, cv_recv, m_send, m_recv, w_send, w_recv, w_local):
        me = _lin(_me())
        w_plan = _gather_half_plan([win_ref], [ain_ref], w_send, w_recv, w_local)
        _start(w_plan)
        c_plan = _ag8_plan(c_ref, call_ref, c_send, c_recv)
        cv_plan = _ag8_plan(cv_ref, cvall_ref, cv_send, cv_recv)
        call_ref[me] = c_ref[...]
        cvall_ref[me] = cv_ref[...]
        _start(c_plan)
        _start(cv_plan)
        _finish(c_plan)
        c16_scr[...] = jnp.zeros_like(c16_scr)
        for d in range(N_DEV):
            c16_scr[pl.ds(d, 1), :] = call_ref[d]
        mp_scr[...] = _raw1(_silu(c16_scr[...]), wa_ref[...], _NN) + b_ref[...]
        mods_ref[me] = mp_scr[...]
        m_plan = _ag8_plan(mp_scr, mods_ref, m_send, m_recv)
        _start(m_plan)
        _finish(cv_plan)
        _finish(m_plan)
        _finish(w_plan)

    vmem = pl.BlockSpec(memory_space=pltpu.VMEM)
    sems = lambda n: pltpu.SemaphoreType.DMA((n,))
    return pl.pallas_call(
        body, name="prologue",
        in_specs=[vmem] * 4 + _hbm_specs(1), out_specs=[vmem] * 3 + _hbm_specs(1),
        out_shape=[_sds((N_DEV,) + c_row.shape), _sds((N_DEV,) + conv_sh.shape), _sds((N_DEV, 16, ADA_N)),
                   _sds((N_CHIP,) + w_in_sh.shape, w_in_sh.dtype)],
        scratch_shapes=[pltpu.VMEM((16, D), F32), pltpu.VMEM((16, ADA_N), F32)] + [sems(N_DEV - 1)] * 6
                       + _gather_sems(1),
        compiler_params=_cparams(),
    )(c_row, conv_sh, w_ada, b_sh, w_in_sh)


def _hbm_specs(n):
    return [pl.BlockSpec(memory_space=pl.ANY)] * n


def _gather_shapes(shards):
    return [_sds((N_CHIP,) + s.shape, s.dtype) for s in shards]


def _gather_sems(n):
    return [pltpu.SemaphoreType.DMA((3 * n,)), pltpu.SemaphoreType.DMA((3 * n,)), pltpu.SemaphoreType.DMA((n,))]


def _gather_plan(ins, outs, send_sems, recv_sems, local_sems):
    mx, my, mc = _me()
    chips = [(1 - mx, my), (mx, 1 - my), (1 - mx, 1 - my)]
    local, sends, recvs = [], [], []
    for a in range(len(ins)):
        local.append(pltpu.make_async_copy(ins[a], outs[a].at[2 * mx + my], local_sems.at[a]))
        for k, (px, py) in enumerate(chips):
            sems = (send_sems.at[3 * a + k], recv_sems.at[3 * a + k], (px, py, mc))
            sends.append(_remote(ins[a], outs[a].at[2 * mx + my], *sems))
            recvs.append(_remote(ins[a], outs[a].at[2 * px + py], *sems))
    return local, sends, recvs


def _gather_half_plan(ins, outs, send_sems, recv_sems, local_sems):
    mx, my, mc = _me()
    chips = [(1 - mx, my), (mx, 1 - my), (1 - mx, 1 - my)]
    local, sends, recvs = [], [], []
    for a in range(len(ins)):
        h = ins[a].shape[0] // 2
        mine = pl.ds(pl.multiple_of(mc * h, 16), h)
        local.append(pltpu.make_async_copy(ins[a], outs[a].at[2 * mx + my], local_sems.at[a]))
        for k, (px, py) in enumerate(chips):
            sems = (send_sems.at[3 * a + k], recv_sems.at[3 * a + k], (px, py, mc))
            sends.append(_remote(ins[a].at[mine], outs[a].at[2 * mx + my, mine], *sems))
            recvs.append(_remote(ins[a].at[mine], outs[a].at[2 * px + py, mine], *sems))
    return local, sends, recvs


def _sibling_fill(pieces):
    h = pieces.shape[1] // 2

    def body(p_ref, o_ref, send_sems, recv_sems):
        mx, my, mc = _me()
        sib = (mx, my, 1 - mc)
        chips = [(1 - mx, my), (mx, 1 - my), (1 - mx, 1 - my)]
        half = lambda c: pl.ds(pl.multiple_of(c * h, 16), h)
        o_ref[2 * mx + my] = p_ref[2 * mx + my]
        sends = []
        for k, (px, py) in enumerate(chips):
            j = 2 * px + py
            o_ref[j, half(mc), :] = p_ref[j, half(mc), :]
            cp = _remote(p_ref.at[j, half(mc)], o_ref.at[j, half(mc)], send_sems.at[k], recv_sems.at[k], sib)
            cp.start()
            sends.append(cp)
        for k, (px, py) in enumerate(chips):
            j = 2 * px + py
            _remote(p_ref.at[j, half(mc)], o_ref.at[j, half(1 - mc)], send_sems.at[k], recv_sems.at[k],
                    sib).wait_recv()
        for cp in sends:
            cp.wait_send()

    vmem = pl.BlockSpec(memory_space=pltpu.VMEM)
    return pl.pallas_call(
        body, name="sibling_fill", out_shape=_sds(pieces.shape, pieces.dtype),
        in_specs=[vmem], out_specs=vmem,
        scratch_shapes=[pltpu.SemaphoreType.DMA((N_CHIP - 1,)), pltpu.SemaphoreType.DMA((N_CHIP - 1,))],
        compiler_params=_cparams(),
    )(pieces)


def _start(plan):
    local, sends, _ = plan
    for cp in local + sends:
        cp.start()


def _finish(plan):
    local, sends, recvs = plan
    for cp in recvs:
        cp.wait_recv()
    for cp in sends:
        cp.wait_send()
    for cp in local:
        cp.wait()


def _exchange_shapes(pieces):
    return [_sds((N_DEV,) + p.shape[2:], p.dtype) for p in pieces]


def _exchange_sems(n):
    return [pltpu.SemaphoreType.DMA(((N_DEV - 1) * n,)), pltpu.SemaphoreType.DMA(((N_DEV - 1) * n,)),
            pltpu.SemaphoreType.DMA((n,))]


def _exchange_plan(ins, outs, send_sems, recv_sems, local_sems):
    me = _me()
    mx, my, mc = me
    local, sends, recvs = [], [], []
    for a in range(len(ins)):
        local.append(pltpu.make_async_copy(ins[a].at[2 * mx + my, mc], outs[a].at[_lin(me)], local_sems.at[a]))
        for k in range(1, N_DEV):
            p = _peer(k, me)
            s = (N_DEV - 1) * a + k - 1
            sends.append(_remote(ins[a].at[2 * p[0] + p[1], p[2]], outs[a].at[_lin(me)], send_sems.at[s],
                                 recv_sems.at[s], p))
            recvs.append(_remote(ins[a].at[2 * mx + my, mc], outs[a].at[_lin(p)], send_sems.at[s],
                                 recv_sems.at[s], p))
    return local, sends, recvs


REDUCE_VMEM = 56 * 1024 * 1024


def _reduce_swap(recvs):
    n = len(recvs)

    def body(*refs):
        r_refs, o_refs = refs[:n], refs[n:2 * n]
        send_sems, recv_sems = refs[2 * n:]
        mx, my, mc = _me()
        sib = (mx, my, 1 - mc)
        half = lambda a, c: o_refs[a].at[pl.ds(pl.multiple_of(c * recvs[a].shape[1], 8), recvs[a].shape[1])]
        sends = []
        for a in range(n):
            g = r_refs[a][0].astype(F32)
            for s in range(1, N_DEV):
                g = g + r_refs[a][s].astype(F32)
            half(a, mc)[...] = g
            cp = _remote(half(a, mc), half(a, mc), send_sems.at[a], recv_sems.at[a], sib)
            cp.start()
            sends.append(cp)
        for a in range(n):
            _remote(half(a, mc), half(a, 1 - mc), send_sems.at[a], recv_sems.at[a], sib).wait_recv()
        for cp in sends:
            cp.wait_send()

    vmem = pl.BlockSpec(memory_space=pltpu.VMEM)
    return pl.pallas_call(
        body, name="reduce_swap", out_shape=[_sds((2 * r.shape[1], r.shape[2])) for r in recvs],
        in_specs=[vmem] * n, out_specs=[vmem] * n,
        scratch_shapes=[pltpu.SemaphoreType.DMA((n,)), pltpu.SemaphoreType.DMA((n,))],
        compiler_params=_cparams(None, REDUCE_VMEM),
    )(*recvs)


def _adamw_big(g, w, m, v, name):
    rows, cols = g.shape
    tr = next((t for t in (256, 176, 128, 64, 8) if rows % t == 0), None)
    if tr is None:
        tc = _tile(cols, 256)
        blk, grid = pl.BlockSpec((rows, tc), lambda i: (0, i)), (cols // tc,)
    else:
        blk, grid = pl.BlockSpec((tr, cols), lambda i: (i, 0)), (rows // tr,)

    def body(g_ref, w_ref, m_ref, v_ref, go_ref, d_ref, m2_ref, v2_ref):
        g = g_ref[...]
        delta, m2, v2 = _adamw(w_ref[...], g, m_ref[...], v_ref[...])
        go_ref[...] = g
        d_ref[...] = delta
        m2_ref[...] = m2
        v2_ref[...] = v2

    return pl.pallas_call(
        body, name=name, grid=grid,
        in_specs=[blk] * 4, out_specs=[blk] * 4, out_shape=[_sds((rows, cols))] * 4,
        compiler_params=_cparams(("parallel",)),
    )(g, w, m, v)


SMALL_ORDER = (("mod", 6 * D), ("norm1_w", D), ("norm2_w", D), ("conv_w", CONVW * 3 * GW), ("a_log", GH),
               ("dt_bias", GH), ("gdn_norm_w", HD), ("q_norm_w", HD), ("k_norm_w", HD), ("sinks", SQH), ("loss", 1))
SMALL_R = 120


def _pack_small(d):
    parts = [d[k].reshape(-1).astype(F32) if k in d else jnp.zeros((n,), F32) for k, n in SMALL_ORDER]
    used = sum(n for _, n in SMALL_ORDER)
    parts.append(jnp.zeros((SMALL_R * LANE - used,), F32))
    return jnp.concatenate(parts).reshape(SMALL_R, LANE)


def _unpack_small(pk):
    flat = pk.reshape(-1)
    out, r = {}, 0
    for k, n in SMALL_ORDER:
        out[k] = flat[r:r + n]
        r += n
    return out


def kernel(x, c, w_ada, b_ada, norm1_w, w_in, conv_w, a_log, dt_bias, gdn_norm_w, q_norm_w, k_norm_w, sinks, w_out, norm2_w, w_gate, w_up, w_down, loss_target, m_w_ada, m_b_ada, m_norm1_w, m_w_in, m_conv_w, m_a_log, m_dt_bias, m_gdn_norm_w, m_q_norm_w, m_k_norm_w, m_sinks, m_w_out, m_norm2_w, m_w_gate, m_w_up, m_w_down, v_w_ada, v_b_ada, v_norm1_w, v_w_in, v_conv_w, v_a_log, v_dt_bias, v_gdn_norm_w, v_q_norm_w, v_k_norm_w, v_sinks, v_w_out, v_norm2_w, v_w_gate, v_w_up, v_w_down):
    mx, my, mc = _me()
    chip = 2 * mx + my
    dev = 4 * mx + 2 * my + mc
    T = x.shape[1]

    as_rows = lambda t, transposed: t[0].T if transposed else t[0]
    transposed = (True, False, True, True, False)
    big_w = [as_rows(t, tr) for t, tr in zip((w_in, w_out, w_gate, w_up, w_down), transposed)]
    shards = [t.astype(BF16) for t in big_w]

    b_sh = lax.dynamic_slice(b_ada, (0, chip * ADA_N), (1, ADA_N))
    w_in_sh = jnp.pad(shards[0], ((0, W_IN_ROWS_PAD - W_IN_ROWS), (0, 0)))
    c_all, conv_all, mods, a_in = _prologue(c, conv_w.reshape(CONVW, 3 * GW // N_CHIP), w_ada[0], b_sh, w_in_sh)
    c8 = c_all.reshape(N_DEV, D)
    conv_full = jnp.concatenate([conv_all[2 * j] for j in range(N_CHIP)], axis=1)
    mod = jnp.concatenate([lax.dynamic_slice(mods[2 * j], (dev, 0), (1, ADA_N)) for j in range(N_CHIP)], axis=1)
    w_in_pt = _permute_w_in_t(_sibling_fill(a_in)[:, :W_IN_ROWS].reshape(PROJ, D))

    loss, grad_x, big, small = _local_step(
        x[0], loss_target[0], mod, norm1_w, w_in_pt, conv_full, a_log, dt_bias, gdn_norm_w,
        q_norm_w, k_norm_w, sinks, norm2_w, shards[1:])

    small["loss"] = loss[:, :1]
    sg = _all_gather8(_pack_small(small), "gather_small_grads")
    rep = dict(mod=(b_ada, m_b_ada, v_b_ada), norm1_w=(norm1_w, m_norm1_w, v_norm1_w),
               norm2_w=(norm2_w, m_norm2_w, v_norm2_w), a_log=(a_log, m_a_log, v_a_log),
               dt_bias=(dt_bias, m_dt_bias, v_dt_bias), gdn_norm_w=(gdn_norm_w, m_gdn_norm_w, v_gdn_norm_w),
               q_norm_w=(q_norm_w, m_q_norm_w, v_q_norm_w), k_norm_w=(k_norm_w, m_k_norm_w, v_k_norm_w),
               sinks=(sinks, m_sinks, v_sinks))
    wmv = [_pack_small({k: t[i] for k, t in rep.items()}) for i in range(3)]
    sres = _reduce_adamw(sg, wmv[0], wmv[1], wmv[2], "small_reduce_adamw")
    s_g, s_d, s_m, s_v = [_unpack_small(sres[i]) for i in range(4)]
    loss_out = s_g["loss"][0]

    g_conv = lax.dynamic_slice(s_g["conv_w"].reshape(CONVW, 3 * GW), (0, chip * (3 * GW // N_CHIP)),
                               (CONVW, 3 * GW // N_CHIP))
    pad16 = lambda t: jnp.concatenate([t.reshape(12, LANE), jnp.zeros((4, LANE), F32)], axis=0)
    cres = _adamw_call(pad16(g_conv), pad16(conv_w), pad16(m_conv_w), pad16(v_conv_w), "conv_adamw")
    conv_out = [g_conv.reshape(conv_w.shape)] + [cres[i, :12].reshape(conv_w.shape) for i in range(3)]

    dmod8 = sg[:, :6 * D // LANE].reshape(N_DEV, 6 * D)
    dm = lax.dynamic_slice(dmod8, (0, chip * ADA_N), (N_DEV, ADA_N))
    zpad = lambda t: jnp.concatenate([t, jnp.zeros((KPAD - N_DEV, t.shape[1]), F32)], axis=0)
    ares = _w_ada_update(zpad(c8), zpad(dm), w_ada[0], m_w_ada[0], v_w_ada[0])

    names = ("w_in", "w_out", "w_gate", "w_up", "w_down")
    g_full = list(_reduce_swap(big))
    g_full[0] = g_full[0][:W_IN_ROWS]
    big_m = [as_rows(t, tr) for t, tr in zip((m_w_in, m_w_out, m_w_gate, m_w_up, m_w_down), transposed)]
    big_v = [as_rows(t, tr) for t, tr in zip((v_w_in, v_w_out, v_w_gate, v_w_up, v_w_down), transposed)]
    upd = [_adamw_big(g, w, m, v, "adamw_" + nm) for g, w, m, v, nm in zip(g_full, big_w, big_m, big_v, names)]
    back = lambda t, tr: (t.T if tr else t)[None]
    bg, bd, bm, bv = [[back(u[i], tr) for u, tr in zip(upd, transposed)] for i in range(4)]

    def group(a_i, small_d, conv_i, big_l):
        s = lambda k, ref: small_d[k].reshape(ref.shape)
        return [ares[a_i][None], s("mod", b_ada), s("norm1_w", norm1_w), big_l[0], conv_out[conv_i],
                s("a_log", a_log), s("dt_bias", dt_bias), s("gdn_norm_w", gdn_norm_w), s("q_norm_w", q_norm_w),
                s("k_norm_w", k_norm_w), s("sinks", sinks), big_l[1], s("norm2_w", norm2_w), big_l[2], big_l[3],
                big_l[4]]

    outs = [loss_out, grad_x[None]]
    outs += group(0, s_g, 0, bg) + group(1, s_d, 1, bd) + group(2, s_m, 2, bm) + group(3, s_v, 3, bv)
    return tuple(outs)
```

```python
import jax
import jax.numpy as jnp
from jax import lax
from jax.experimental import pallas as pl
from jax.experimental.pallas import tpu as pltpu

F32 = jnp.float32
BF16 = jnp.bfloat16
MESH = pl.DeviceIdType.MESH

D = 1024
HD = 64
GH = 8
GW = GH * HD
SQH = 8
SKVH = 2
SGRP = SQH // SKVH
WIN = 128
CONVW = 4
CHUNK = 64
DFF = 2816
PROJ = 2832
NP = 3072
EPS = 1e-6
N_DEV = 8
N_CHIP = 4

ADAM_LR = 0.001
ADAM_B1 = 0.9
ADAM_B2 = 0.999
ADAM_EPS = 1e-08
ADAM_WD = 0.01
ADAM_STEP = 10

VMEM_LIMIT = 48 * 1024 * 1024
GDN_BWD_VMEM = 58 * 1024 * 1024
LANE = 128


def _cparams(sem=None, vmem=VMEM_LIMIT):
    return pltpu.CompilerParams(dimension_semantics=sem, vmem_limit_bytes=vmem)


_NN = ((1,), (0,))
_NT = ((1,), (1,))
_TN = ((0,), (0,))


def _dot(a, b, dims):
    if a.ndim == 3:
        (ca,), (cb,) = dims
        return lax.dot_general(a, b, (((ca + 1,), (cb + 1,)), ((0,), (0,))), preferred_element_type=F32)
    return lax.dot_general(a, b, (dims, ((), ())), preferred_element_type=F32)


def _raw1(a, b, dims):
    return _dot(a.astype(BF16), b.astype(BF16), dims)


def _raw3(a, b, dims):
    ah = a.astype(BF16)
    al = (a - ah.astype(F32)).astype(BF16)
    bh = b.astype(BF16)
    bl = (b - bh.astype(F32)).astype(BF16)
    return _dot(ah, bh, dims) + (_dot(al, bh, dims) + _dot(ah, bl, dims))


def _make_diff_mm(raw):
    @jax.custom_vjp
    def nn(a, b):
        return raw(a, b, _NN)

    @jax.custom_vjp
    def nt(a, b):
        return raw(a, b, _NT)

    @jax.custom_vjp
    def tn(a, b):
        return raw(a, b, _TN)

    nn.defvjp(lambda a, b: (raw(a, b, _NN), (a, b)), lambda r, g: (nt(g, r[1]), tn(r[0], g)))
    nt.defvjp(lambda a, b: (raw(a, b, _NT), (a, b)), lambda r, g: (nn(g, r[1]), tn(g, r[0])))
    tn.defvjp(lambda a, b: (raw(a, b, _TN), (a, b)), lambda r, g: (nt(r[1], g), nn(r[0], g)))
    return nn, nt, tn


def _tri_inv_raw(a, nn3):
    n = a.shape[-1]
    ri = lax.broadcasted_iota(jnp.int32, (n, n), 0)
    ci = lax.broadcasted_iota(jnp.int32, (n, n), 1)
    t = (ri == ci).astype(F32)
    for lvl in range((n - 1).bit_length()):
        same_pair = (ri >> (lvl + 1)) == (ci >> (lvl + 1))
        lower_left = (((ri >> lvl) & 1) == 1) & (((ci >> lvl) & 1) == 0)
        y = jnp.where(same_pair & lower_left, a, 0.0)
        t = t - y if lvl == 0 else t - nn3(nn3(t, y), t)
    return t


class _Kit:
    def __init__(self, diff):
        if diff:
            self.nn, self.nt, self.tn = _make_diff_mm(_raw1)
            self.nn3, self.nt3, self.tn3 = _make_diff_mm(_raw3)
            nn3, nt3, tn3 = self.nn3, self.nt3, self.tn3

            @jax.custom_vjp
            def inv(a, t):
                return t

            def inv_fwd(a, t):
                return t, t

            def inv_bwd(t, g):
                return -tn3(t, nt3(g, t)), jnp.zeros_like(t)

            inv.defvjp(inv_fwd, inv_bwd)
            self.inv = inv
        else:
            self.nn = lambda a, b: _raw1(a, b, _NN)
            self.nt = lambda a, b: _raw1(a, b, _NT)
            self.tn = lambda a, b: _raw1(a, b, _TN)
            self.nn3 = lambda a, b: _raw3(a, b, _NN)
            self.nt3 = lambda a, b: _raw3(a, b, _NT)
            self.tn3 = lambda a, b: _raw3(a, b, _TN)
            self.inv = lambda a, t: _tri_inv_raw(a, self.nn3) if t is None else t


def _sigmoid(x):
    return 1.0 / (1.0 + jnp.exp(-x))


def _silu(x):
    return x * _sigmoid(x)


def _rms(x, w):
    return x * lax.rsqrt(jnp.mean(x * x, axis=-1, keepdims=True) + EPS) * w


def _tile(dim, target):
    t = (min(dim, target) // LANE) * LANE
    while t >= LANE:
        if dim % t == 0:
            return t
        t -= LANE
    return dim


MM_TM, MM_TN, MM_TK = 1408, 1536, 1408


def _matmul(a, b, ta=False, tb=False, out_dtype=F32, name="matmul", gather=None, exchange=None):
    carried = gather if gather is not None else exchange if exchange is not None else []
    nc = len(carried)
    if ta:
        K, M = a.shape
    else:
        M, K = a.shape
    if tb:
        N, K2 = b.shape
    else:
        K2, N = b.shape
    assert K == K2, (a.shape, b.shape, ta, tb)
    tm, tn, tk = _tile(M, MM_TM), _tile(N, MM_TN), _tile(K, MM_TK)
    nk = K // tk
    dims = ((0,) if ta else (1,), (1,) if tb else (0,))

    grid = (M // tm, N // tn, nk)

    def body(*refs):
        a_ref, b_ref = refs[:2]
        o_ref = refs[2 + nc]
        scratch = refs[3 + 2 * nc:]
        k = pl.program_id(2)
        if nc:
            make_plan = _gather_plan if gather is not None else _exchange_plan
            plan = make_plan(refs[2:2 + nc], refs[3 + nc:3 + 2 * nc], *scratch[-3:])
            at = lambda pos: ((pl.program_id(0) == pos[0]) & (pl.program_id(1) == pos[1]) & (k == pos[2]))

            @pl.when(at((0, 0, 0)))
            def _():
                _start(plan)

        part = _dot(a_ref[...].astype(BF16), b_ref[...].astype(BF16), dims)
        if nk == 1:
            o_ref[...] = part.astype(o_ref.dtype)
        else:
            acc_ref = scratch[0]

            @pl.when(k == 0)
            def _():
                acc_ref[...] = part

            @pl.when((k > 0) & (k < nk - 1))
            def _():
                acc_ref[...] += part

            @pl.when(k == nk - 1)
            def _():
                o_ref[...] = (acc_ref[...] + part).astype(o_ref.dtype)

        if nc:
            @pl.when(at((grid[0] - 1, grid[1] - 1, nk - 1)))
            def _():
                _finish(plan)

    a_spec = (pl.BlockSpec((tk, tm), lambda i, j, k: (k, i)) if ta
              else pl.BlockSpec((tm, tk), lambda i, j, k: (i, k)))
    b_spec = (pl.BlockSpec((tn, tk), lambda i, j, k: (j, k)) if tb
              else pl.BlockSpec((tk, tn), lambda i, j, k: (k, j)))
    if gather is not None:
        c_shapes, c_sems = _gather_shapes(carried), _gather_sems(nc)
    elif exchange is not None:
        c_shapes, c_sems = _exchange_shapes(carried), _exchange_sems(nc)
    else:
        c_shapes, c_sems = [], []
    res = pl.pallas_call(
        body, name=name, grid=grid,
        in_specs=[a_spec, b_spec] + _hbm_specs(nc),
        out_specs=[pl.BlockSpec((tm, tn), lambda i, j, k: (i, j))] + _hbm_specs(nc),
        out_shape=[jax.ShapeDtypeStruct((M, N), out_dtype)] + c_shapes,
        scratch_shapes=([pltpu.VMEM((tm, tn), F32)] if nk > 1 else []) + c_sems,
        compiler_params=_cparams(("arbitrary",) * 3 if nc else ("parallel", "parallel", "arbitrary")),
    )(a, b, *carried)
    return (res[0], res[1:]) if nc else res[0]


def _sds(shape, dtype=F32):
    return jax.ShapeDtypeStruct(shape, dtype)


def _norm_mod(x, nw, scale, shift):
    return _rms(x, nw) * (1.0 + scale) + shift


IN_PROJ_VMEM = 56 * 1024 * 1024


def _norm_in_proj(x, nw, scale, shift, w_in_pt, shards):
    T = x.shape[0]
    N = w_in_pt.shape[0]
    tm, tn = _tile(T, 1024), 3 * GW
    nm, nn = T // tm, N // tn
    nz = (GAB0 - 3 * GW) // HD
    ns = len(shards)

    def body(*refs):
        x_ref, nw_ref, sc_ref, sh_ref, w_ref = refs[:5]
        h_ref, o_ref, zs_ref = refs[5 + ns:8 + ns]
        plan = _gather_plan(refs[5:5 + ns], refs[8 + ns:8 + 2 * ns], *refs[8 + 2 * ns:])
        i, j = pl.program_id(0), pl.program_id(1)

        @pl.when((i == 0) & (j == 0))
        def _():
            _start(plan)

        @pl.when(j == 0)
        def _():
            for r0 in range(0, tm, ROWS_EPI):
                rows = pl.ds(r0, ROWS_EPI)
                h_ref[rows, :] = _norm_mod(x_ref[rows, :], nw_ref[...], sc_ref[...], sh_ref[...]).astype(BF16)

        o = _dot(h_ref[...], w_ref[...], _NT)
        o_ref[...] = o

        @pl.when(j == 1)
        def _():
            for p in range(nz // 2):
                zs_ref[2 * p], zs_ref[2 * p + 1] = _split_pair(o[:, p * LANE:(p + 1) * LANE])

        @pl.when((i == nm - 1) & (j == nn - 1))
        def _():
            _finish(plan)

    vec = pl.BlockSpec((1, D), lambda i, j: (0, 0))
    res = pl.pallas_call(
        body, name="norm1_in_proj", grid=(nm, nn),
        in_specs=[pl.BlockSpec((tm, D), lambda i, j: (i, 0)), vec, vec, vec,
                  pl.BlockSpec((tn, D), lambda i, j: (j, 0))] + _hbm_specs(ns),
        out_specs=[pl.BlockSpec((tm, D), lambda i, j: (i, 0)), pl.BlockSpec((tm, tn), lambda i, j: (i, j)),
                   pl.BlockSpec((nz, tm, HD), lambda i, j: (0, i, 0))] + _hbm_specs(ns),
        out_shape=[_sds((T, D), BF16), _sds((T, N)), _sds((nz, T, HD))] + _gather_shapes(shards),
        scratch_shapes=_gather_sems(ns),
        compiler_params=_cparams(("arbitrary", "arbitrary"), IN_PROJ_VMEM),
    )(x, nw, scale, shift, w_in_pt, *shards)
    return res[0], res[1], res[2], res[3:]


ROWS_TM = 512
ROWS_EPI = 256


def _matmul_rows(a, b, epi, tiled, consts, out_tiled, out_acc, name, pieces=()):
    T, K = a.shape
    tm, tk = _tile(T, ROWS_TM), _tile(K, MM_TK)
    nm, nk = T // tm, K // tk
    npc, nt, ncst, no, na = len(pieces), len(tiled), len(consts), len(out_tiled), len(out_acc)
    n_in = 2 + nt + ncst

    def body(*refs):
        a_ref, b_ref = refs[:2]
        t_refs, c_refs = refs[2:2 + nt], refs[2 + nt:n_in]
        o_refs = refs[n_in + npc:n_in + npc + no]
        acc_refs = refs[n_in + npc + no:n_in + npc + no + na]
        n_out = no + na + npc
        res_ref = refs[n_in + npc + n_out]
        plan = _exchange_plan(refs[n_in:n_in + npc], refs[n_in + npc + no + na:n_in + npc + n_out],
                              *refs[n_in + npc + n_out + 1:]) if npc else None
        i, k = pl.program_id(0), pl.program_id(1)

        @pl.when((i == 0) & (k == 0))
        def _():
            for r in acc_refs:
                r[...] = jnp.zeros_like(r)
            if npc:
                _start(plan)

        part = _dot(a_ref[...], b_ref[...], _NN)

        @pl.when(k == 0)
        def _():
            res_ref[...] = part

        @pl.when(k > 0)
        def _():
            res_ref[...] += part

        @pl.when(k == nk - 1)
        def _():
            for r0 in range(0, tm, ROWS_EPI):
                rows = pl.ds(r0, ROWS_EPI)
                outs = epi(res_ref[rows, :], *[r[rows, :] for r in t_refs], *[r[...] for r in c_refs])
                for r, v in zip(o_refs, outs[:no]):
                    r[rows, :] = v.astype(r.dtype)
                for r, v in zip(acc_refs, outs[no:]):
                    r[...] += v

        if npc:
            @pl.when((i == nm - 1) & (k == nk - 1))
            def _():
                _finish(plan)

    row = lambda w: pl.BlockSpec((tm, w), lambda i, k: (i, 0))
    whole = lambda s: pl.BlockSpec(s.shape, lambda i, k: (0, 0))
    res = pl.pallas_call(
        body, name=name, grid=(nm, nk),
        in_specs=[pl.BlockSpec((tm, tk), lambda i, k: (i, k)), pl.BlockSpec((tk, D), lambda i, k: (k, 0))]
                 + [row(t.shape[1]) for t in tiled] + [whole(c) for c in consts] + _hbm_specs(npc),
        out_specs=[row(s.shape[1]) for s in out_tiled] + [whole(s) for s in out_acc] + _hbm_specs(npc),
        out_shape=list(out_tiled) + list(out_acc) + (_exchange_shapes(pieces) if npc else []),
        scratch_shapes=[pltpu.VMEM((tm, D), F32)] + (_exchange_sems(npc) if npc else []),
        compiler_params=_cparams(("arbitrary", "arbitrary")),
    )(a, b, *tiled, *consts, *pieces)
    return res[:no + na], res[no + na:]


def _in_proj_dx_norm_bwd(dproj, w_in_pt, x, dres, nw, scale, shift, pieces):
    T = x.shape[0]

    def epi(dh, x, dres, nw, scale, shift):
        _, vjp = jax.vjp(_norm_mod, x, nw, scale, shift)
        dx, dnw, dsc, dsh = vjp(dh)
        return dx + dres, dnw, dsc, dsh

    return _matmul_rows(dproj, w_in_pt, epi, [x, dres], [nw, scale, shift], [_sds((T, D))], [_sds((1, D))] * 3,
                        "in_proj_dx_norm1_bwd", pieces)


def _out_proj_resid_norm(o_hm, w_out, x, gate1, nw, scale, shift):
    T = x.shape[0]
    nheads = o_hm.shape[0]
    tm = _tile(T, ROWS_TM)

    def body(o_ref, w_ref, x_ref, g_ref, nw_ref, sc_ref, sh_ref, cat_ref, mixed_ref, x1_ref, h2_ref):
        cat = jnp.concatenate([_merge_pair(o_ref[2 * p], o_ref[2 * p + 1]) for p in range(nheads // 2)], axis=1)
        cat_ref[...] = cat.astype(BF16)
        mixed_ref[...] = _dot(cat_ref[...], w_ref[...], _NN)
        for r0 in range(0, tm, ROWS_EPI):
            rows = pl.ds(r0, ROWS_EPI)
            x1, h2 = _resid_norm(x_ref[rows, :], mixed_ref[rows, :], g_ref[...], nw_ref[...], sc_ref[...], sh_ref[...])
            x1_ref[rows, :] = x1
            h2_ref[rows, :] = h2.astype(BF16)

    row = pl.BlockSpec((tm, D), lambda i: (i, 0))
    vec = pl.BlockSpec((1, D), lambda i: (0, 0))
    return pl.pallas_call(
        body, name="out_proj_resid_norm2", grid=(T // tm,),
        in_specs=[pl.BlockSpec((nheads, tm, HD), lambda i: (0, i, 0)), pl.BlockSpec((D, D), lambda i: (0, 0)), row,
                  vec, vec, vec, vec],
        out_specs=[row, row, row, row],
        out_shape=[_sds((T, D), BF16), _sds((T, D)), _sds((T, D)), _sds((T, D), BF16)],
        compiler_params=_cparams(("parallel",)),
    )(o_hm, w_out, x, gate1, nw, scale, shift)


def _ffn_up_dx_resid_bwd(dab, w_gut, x, mixed, dy, gate1, nw, scale, shift):
    T = x.shape[0]

    def epi(dh2, x, mixed, dy, gate1, nw, scale, shift):
        _, vjp = jax.vjp(_resid_norm, x, mixed, gate1, nw, scale, shift)
        return vjp((dy, dh2))

    outs, _ = _matmul_rows(dab, w_gut, epi, [x, mixed, dy], [gate1, nw, scale, shift],
                           [_sds((T, D)), _sds((T, D), BF16)], [_sds((1, D))] * 4, "ffn_up_dx_resid_norm2_bwd")
    return outs


def _ffn_down_loss(act, w_down, x1, target, gate2):
    T = x1.shape[0]

    def epi(ffn, x1, target, gate2):
        y = x1 + gate2 * ffn
        err = y - target
        loss = 0.5 * jnp.sum(jnp.sum(err * err, axis=1, keepdims=True), axis=0, keepdims=True) / D
        dy = err * (1.0 / D)
        return dy, gate2 * dy, jnp.sum(dy * ffn, axis=0, keepdims=True), jnp.broadcast_to(loss, (1, LANE))

    outs, _ = _matmul_rows(act, w_down, epi, [x1, target], [gate2], [_sds((T, D)), _sds((T, D), BF16)],
                           [_sds((1, D)), _sds((1, LANE))], "ffn_down_loss")
    return outs


def _resid_norm(x, mixed, gate1, nw, scale, shift):
    x1 = x + gate1 * mixed
    return x1, _norm_mod(x1, nw, scale, shift)


FFN_BLK = 256
FFN_TM = 2048


def _interleave_gate_up(gate_t, up_t):
    blocks = lambda t: t.reshape(DFF // FFN_BLK, 1, FFN_BLK, D)
    return jnp.concatenate([blocks(gate_t), blocks(up_t)], axis=1).reshape(2 * DFF, D)


def _split_gate_up(g):
    g = g.reshape(DFF // FFN_BLK, 2, FFN_BLK, D)
    return g[:, 0].reshape(DFF, D), g[:, 1].reshape(DFF, D)


def _ffn_up_act(h2, w_gut):
    T = h2.shape[0]
    tm = _tile(T, FFN_TM)

    def body(h_ref, w_ref, ab_ref, act_ref):
        ab = _dot(h_ref[...], w_ref[...], _NT)
        ab_ref[...] = ab
        act_ref[...] = (_silu(ab[:, :FFN_BLK]) * ab[:, FFN_BLK:]).astype(act_ref.dtype)

    return pl.pallas_call(
        body, name="ffn_up_act", grid=(T // tm, DFF // FFN_BLK),
        in_specs=[pl.BlockSpec((tm, D), lambda i, j: (i, 0)), pl.BlockSpec((2 * FFN_BLK, D), lambda i, j: (j, 0))],
        out_specs=[pl.BlockSpec((tm, 2 * FFN_BLK), lambda i, j: (i, j)), pl.BlockSpec((tm, FFN_BLK), lambda i, j: (i, j))],
        out_shape=[_sds((T, 2 * DFF)), _sds((T, DFF), BF16)],
        compiler_params=_cparams(("parallel", "parallel")),
    )(h2, w_gut)


def _ffn_down_dx_act(dffn, w_down, ab):
    T = dffn.shape[0]
    tm = _tile(T, FFN_TM)

    def body(d_ref, w_ref, ab_ref, o_ref):
        dact = _dot(d_ref[...], w_ref[...], _NT)
        a, b = ab_ref[:, :FFN_BLK], ab_ref[:, FFN_BLK:]
        s = _sigmoid(a)
        da = dact * b * (s * (1.0 + a * (1.0 - s)))
        db = dact * (a * s)
        o_ref[...] = jnp.concatenate([da, db], axis=1).astype(o_ref.dtype)

    return pl.pallas_call(
        body, name="ffn_down_dx_act", grid=(T // tm, DFF // FFN_BLK),
        in_specs=[pl.BlockSpec((tm, D), lambda i, j: (i, 0)), pl.BlockSpec((FFN_BLK, D), lambda i, j: (j, 0)),
                  pl.BlockSpec((tm, 2 * FFN_BLK), lambda i, j: (i, j))],
        out_specs=pl.BlockSpec((tm, 2 * FFN_BLK), lambda i, j: (i, j)),
        out_shape=_sds((T, 2 * DFF), BF16),
        compiler_params=_cparams(("parallel", "parallel")),
    )(dffn, w_down, ab)


def _round_bf16(x):
    return x.astype(BF16).astype(F32)


def _shift_down(x, s, rows):
    if s == 0:
        return x
    return jnp.where(rows >= s, pltpu.roll(x, s, 0), 0.0)


def _shift_up(x, s, rows, T):
    if s == 0:
        return x
    return jnp.where(rows < T - s, pltpu.roll(x, T - s, 0), 0.0)


def _conv_fwd(proj, conv_w):
    T = proj.shape[0]
    ncol = 3 * GW // LANE

    def body(x_ref, w_ref, o_ref):
        x = _round_bf16(x_ref[...])
        rows = lax.broadcasted_iota(jnp.int32, x.shape, 0)
        acc = jnp.zeros_like(x)
        for j in range(CONVW):
            acc = acc + _round_bf16(w_ref[pl.ds(j, 1), :]) * _shift_down(x, CONVW - 1 - j, rows)
        o_ref[0], o_ref[1] = _split_pair(_silu(acc))

    return pl.pallas_call(
        body, name="conv_fwd", grid=(ncol,),
        in_specs=[pl.BlockSpec((T, LANE), lambda j: (0, j)), pl.BlockSpec((CONVW, LANE), lambda j: (0, j))],
        out_specs=pl.BlockSpec((2, T, HD), lambda j: (j, 0, 0)),
        out_shape=_sds((3 * GH, T, HD)),
        compiler_params=_cparams(("parallel",)),
    )(proj, conv_w)


def _split_pair(y):
    return y[:, :HD], pltpu.roll(y, HD, 1)[:, :HD]


def _merge_pair(a, b):
    return jnp.concatenate([a, b], axis=1)


def _merge_all(heads):
    return jnp.concatenate([_merge_pair(heads[2 * p], heads[2 * p + 1]) for p in range(heads.shape[0] // 2)], axis=1)


def _matmul_nt_heads(a, b, name):
    T, K = a.shape
    N = b.shape[0]
    tm = _tile(T, 1024)

    def body(a_ref, b_ref, o_ref):
        res = _dot(a_ref[...], b_ref[...], _NT)
        for p in range(N // LANE):
            o_ref[2 * p], o_ref[2 * p + 1] = _split_pair(res[:, p * LANE:(p + 1) * LANE])

    return pl.pallas_call(
        body, name=name, grid=(T // tm,),
        in_specs=[pl.BlockSpec((tm, K), lambda i: (i, 0)), pl.BlockSpec((N, K), lambda i: (0, 0))],
        out_specs=pl.BlockSpec((N // HD, tm, HD), lambda i: (0, i, 0)),
        out_shape=_sds((N // HD, T, HD)),
        compiler_params=_cparams(("parallel",)),
    )(a, b)


def _conv_bwd(proj, conv_w, dqc, dproj):
    T = proj.shape[0]
    ncol = 3 * GW // LANE

    def body(x_ref, w_ref, d_ref, buf_ref, dx_ref, dw_ref):
        x = _round_bf16(x_ref[...])
        rows = lax.broadcasted_iota(jnp.int32, x.shape, 0)
        xs = [_shift_down(x, CONVW - 1 - j, rows) for j in range(CONVW)]
        w = [_round_bf16(w_ref[pl.ds(j, 1), :]) for j in range(CONVW)]
        pre = jnp.zeros_like(x)
        for j in range(CONVW):
            pre = pre + w[j] * xs[j]
        s = _sigmoid(pre)
        dpre = _round_bf16(_merge_pair(d_ref[0], d_ref[1]) * (s * (1.0 + pre * (1.0 - s))))
        dx = jnp.zeros_like(x)
        for j in range(CONVW):
            dx = dx + w[j] * _shift_up(dpre, CONVW - 1 - j, rows, T)
            dw_ref[pl.ds(j, 1), :] = jnp.sum(dpre * xs[j], axis=0, keepdims=True)
        dx_ref[...] = dx.astype(dx_ref.dtype)

    return pl.pallas_call(
        body, name="conv_bwd", grid=(ncol,),
        in_specs=[pl.BlockSpec((T, LANE), lambda j: (0, j)), pl.BlockSpec((CONVW, LANE), lambda j: (0, j)),
                  pl.BlockSpec((2, T, HD), lambda j: (j, 0, 0))] + _hbm_specs(1),
        out_specs=[pl.BlockSpec((T, LANE), lambda j: (0, j)), pl.BlockSpec((CONVW, LANE), lambda j: (0, j))],
        out_shape=[_sds(dproj.shape, dproj.dtype), _sds((CONVW, 3 * GW))],
        input_output_aliases={3: 0},
        compiler_params=_cparams(("parallel",)),
    )(proj, conv_w, dqc, dproj)


def _gdn_prep(kit, q, k, v, ga, gb, alog, dtb, t_inv=None):
    C = CHUNK
    ri = lax.broadcasted_iota(jnp.int32, (C, C), 0)
    ci = lax.broadcasted_iota(jnp.int32, (C, C), 1)
    causal = ri >= ci
    strict = ri > ci
    eye = (ri == ci).astype(F32)
    lower = causal.astype(F32)
    upper = (ri <= ci).astype(F32)

    a = ga + dtb
    softplus = jnp.maximum(a, 0.0) + jnp.log(1.0 + jnp.exp(-jnp.abs(a)))
    g_row = -jnp.exp(alog) * softplus
    beta_row = _sigmoid(gb)
    g_col = jnp.sum(eye * g_row, axis=2, keepdims=True)
    beta_col = jnp.sum(eye * beta_row, axis=2, keepdims=True)
    G_col = jnp.sum(lower * g_row, axis=2, keepdims=True)
    G_row = jnp.sum(upper * g_col, axis=1, keepdims=True)
    G_last = jnp.sum(g_row, axis=2, keepdims=True)
    decay = jnp.exp(jnp.where(causal, G_col - G_row, -1e30))

    qn = q * lax.rsqrt(jnp.sum(q * q, axis=-1, keepdims=True) + EPS) * (HD ** -0.5)
    kn = k * lax.rsqrt(jnp.sum(k * k, axis=-1, keepdims=True) + EPS)
    kb = kn * beta_col
    A = jnp.where(strict, kit.nt(kb, kn) * decay, 0.0)
    Tm = kit.inv(A, t_inv)
    eG = jnp.exp(G_col)
    u = kit.nn3(Tm, v * beta_col)
    w = kit.nn3(Tm, kb * eG)
    qk = jnp.where(causal, kit.nt(qn, kn) * decay, 0.0)
    q_dec = qn * eG
    k_dec = kn * jnp.exp(G_last - G_col)
    dec = jnp.exp(G_last)
    return u, w, qk, q_dec, k_dec, dec, Tm


def _gdn_out(o, z, nw):
    return _rms(o, nw) * _silu(z)


GDN_CB = 4


def _gdn_specs(T, blk):
    TB = GDN_CB * CHUNK
    seq = lambda grp: pl.BlockSpec((GH, TB, HD), lambda i, grp=grp: (grp, blk(i), 0))
    row = lambda grp: pl.BlockSpec((GH, GDN_CB, 1, CHUNK), lambda i, grp=grp: (grp, blk(i), 0, 0))
    per_head = pl.BlockSpec((GH, 1, CHUNK), lambda i: (0, 0, 0))
    whole = pl.BlockSpec((1, HD), lambda i: (0, 0))
    state = pl.BlockSpec((GH, GDN_CB, HD, HD), lambda i: (0, blk(i), 0, 0))
    return seq, row, per_head, whole, state


def _gdn_load(seq_refs, row_refs, head_refs):
    chunks = lambda r: jnp.concatenate([r[:, pl.ds(cb * CHUNK, CHUNK), :] for cb in range(GDN_CB)], axis=0)
    rows = lambda r: jnp.concatenate([r[:, cb] for cb in range(GDN_CB)], axis=0)
    heads = lambda r: jnp.concatenate([r[...]] * GDN_CB, axis=0)
    return [chunks(r) for r in seq_refs], [rows(r) for r in row_refs], [heads(r) for r in head_refs]


def _gdn_fwd(qkv_hm, zs_hm, gab, alog_b, dtb_b, nw, shards):
    T = qkv_hm.shape[1]
    N = T // CHUNK
    nblk = N // GDN_CB
    ns = len(shards)
    seq, row, per_head, whole, state = _gdn_specs(T, lambda i: i)
    kit = _Kit(False)

    def body(*refs):
        q_ref, k_ref, v_ref, z_ref, ga_ref, gb_ref, al_ref, dt_ref, nw_ref = refs[:9]
        o_ref, S_ref, T_ref = refs[9 + ns:12 + ns]
        S_scr = refs[12 + 2 * ns]
        plan = _gather_plan(refs[9:9 + ns], refs[12 + ns:12 + 2 * ns], *refs[13 + 2 * ns:])

        @pl.when(pl.program_id(0) == 0)
        def _():
            S_scr[...] = jnp.zeros_like(S_scr)
            _start(plan)

        (q, k, v, z), (ga, gb), (al, dt) = _gdn_load((q_ref, k_ref, v_ref, z_ref), (ga_ref, gb_ref), (al_ref, dt_ref))
        u, w, qk, q_dec, k_dec, dec, t_inv = _gdn_prep(kit, q, k, v, ga, gb, al, dt)
        S = S_scr[...]
        for cb in range(GDN_CB):
            hs = slice(cb * GH, (cb + 1) * GH)
            S_ref[:, cb] = S
            T_ref[:, cb] = t_inv[hs]
            v_new = u[hs] - kit.nn(w[hs], S)
            o = kit.nn(q_dec[hs], S) + kit.nn(qk[hs], v_new)
            S = S * dec[hs] + kit.tn(k_dec[hs], v_new)
            o_ref[:, pl.ds(cb * CHUNK, CHUNK), :] = _gdn_out(o, z[hs], nw_ref[...])
        S_scr[...] = S

        @pl.when(pl.program_id(0) == nblk - 1)
        def _():
            _finish(plan)

    res = pl.pallas_call(
        body, name="gdn_fwd", grid=(nblk,),
        in_specs=[seq(0), seq(1), seq(2), seq(0), row(0), row(1), per_head, per_head, whole] + _hbm_specs(ns),
        out_specs=[seq(0), state, state] + _hbm_specs(ns),
        out_shape=[_sds((GH + SQH, T, HD)), _sds((GH, N, HD, HD)), _sds((GH, N, CHUNK, CHUNK))]
                  + _gather_shapes(shards),
        scratch_shapes=[pltpu.VMEM((GH, HD, HD), F32)] + _gather_sems(ns),
        compiler_params=_cparams(("arbitrary",)),
    )(qkv_hm, qkv_hm, qkv_hm, zs_hm, gab, gab, alog_b, dtb_b, nw, *shards)
    return res[0], (res[1], res[2]), res[3:]


def _gdn_bwd(qkv_hm, zs_hm, gab, alog_b, dtb_b, nw, S_all, do, pieces):
    T = qkv_hm.shape[1]
    N = T // CHUNK
    nblk = N // GDN_CB
    npc = len(pieces)
    dkit, kit = _Kit(True), _Kit(False)
    rseq, rrow, per_head, whole, rstate = _gdn_specs(T, lambda i: nblk - 1 - i)

    def body(*refs):
        q_ref, k_ref, v_ref, z_ref, ga_ref, gb_ref, al_ref, dt_ref, nw_ref, S_ref, T_ref, do_ref = refs[:12]
        dqkv_ref, dz_ref, dga_ref, dgb_ref, dal_ref, ddt_ref, dnw_ref = refs[12 + npc:19 + npc]
        dS_scr = refs[19 + 2 * npc]
        plan = _exchange_plan(refs[12:12 + npc], refs[19 + npc:19 + 2 * npc], *refs[20 + 2 * npc:])

        @pl.when(pl.program_id(0) == 0)
        def _():
            dS_scr[...] = jnp.zeros_like(dS_scr)
            dal_ref[...] = jnp.zeros_like(dal_ref)
            ddt_ref[...] = jnp.zeros_like(ddt_ref)
            dnw_ref[...] = jnp.zeros_like(dnw_ref)
            _start(plan)

        (q, k, v, z, dout), (ga, gb), (al, dt) = _gdn_load((q_ref, k_ref, v_ref, z_ref, do_ref), (ga_ref, gb_ref),
                                                          (al_ref, dt_ref))
        S_in = jnp.concatenate([S_ref[:, cb] for cb in range(GDN_CB)], axis=0)
        t_inv = jnp.concatenate([T_ref[:, cb] for cb in range(GDN_CB)], axis=0)
        prep = lambda *a: _gdn_prep(dkit, *a, t_inv=t_inv)[:6]
        (u, w, qk, q_dec, k_dec, dec), prep_vjp = jax.vjp(prep, q, k, v, ga, gb, al, dt)
        v_new = u - kit.nn(w, S_in)
        o = kit.nn(q_dec, S_in) + kit.nn(qk, v_new)
        _, out_vjp = jax.vjp(_gdn_out, o, z, nw_ref[...])
        do, dz, dnw = out_vjp(dout)
        dvn_part = kit.tn(qk, do)
        dS_part = kit.tn(q_dec, do)
        dS = dS_scr[...]
        dS_out, dvn = [None] * GDN_CB, [None] * GDN_CB
        for cb in reversed(range(GDN_CB)):
            hs = slice(cb * GH, (cb + 1) * GH)
            dS_out[cb] = dS
            dvn[cb] = dvn_part[hs] + kit.nn(k_dec[hs], dS)
            dS = dS * dec[hs] + dS_part[hs] - kit.tn(w[hs], dvn[cb])
        dS_scr[...] = dS
        dS_out = jnp.concatenate(dS_out, axis=0)
        dvn = jnp.concatenate(dvn, axis=0)
        ddec = jnp.sum(jnp.sum(S_in * dS_out, axis=2, keepdims=True), axis=1, keepdims=True)
        cts = (dvn, -kit.nt(dvn, S_in), kit.nt(do, v_new), kit.nt(do, S_in), kit.nt(v_new, dS_out), ddec)
        dq, dk, dv, dga, dgb, dal, ddt = prep_vjp(cts)
        lanesum = lambda t: jnp.broadcast_to(jnp.sum(t, axis=2, keepdims=True), t.shape)
        for cb in range(GDN_CB):
            hs = slice(cb * GH, (cb + 1) * GH)
            sl = pl.ds(cb * CHUNK, CHUNK)
            dqkv_ref[pl.ds(0, GH), sl, :] = dq[hs]
            dqkv_ref[pl.ds(GH, GH), sl, :] = dk[hs]
            dqkv_ref[pl.ds(2 * GH, GH), sl, :] = dv[hs]
            dz_ref[sl, :] = _merge_all(dz[hs]).astype(BF16)
            dga_ref[:, cb] = dga[hs]
            dgb_ref[:, cb] = dgb[hs]
            dal_ref[...] += lanesum(dal[hs])
            ddt_ref[...] += lanesum(ddt[hs])
        dnw_ref[...] += dnw

        @pl.when(pl.program_id(0) == nblk - 1)
        def _():
            _finish(plan)

    res = pl.pallas_call(
        body, name="gdn_bwd", grid=(nblk,),
        in_specs=[rseq(0), rseq(1), rseq(2), rseq(0), rrow(0), rrow(1), per_head, per_head, whole, rstate, rstate,
                  rseq(0)] + _hbm_specs(npc),
        out_specs=[pl.BlockSpec((3 * GH, GDN_CB * CHUNK, HD), lambda i: (0, nblk - 1 - i, 0)),
                   pl.BlockSpec((GDN_CB * CHUNK, GW), lambda i: (nblk - 1 - i, 3)), rrow(0),
                   rrow(0), per_head, per_head, whole] + _hbm_specs(npc),
        out_shape=[_sds((3 * GH, T, HD)), _sds((T, NP), BF16)] + [_sds((GH, N, 1, CHUNK))] * 2
                  + [_sds((GH, 1, CHUNK))] * 2 + [_sds((1, HD))] + _exchange_shapes(pieces),
        scratch_shapes=[pltpu.VMEM((GH, HD, HD), F32)] + _exchange_sems(npc),
        compiler_params=_cparams(("arbitrary",), GDN_BWD_VMEM),
    )(qkv_hm, qkv_hm, qkv_hm, zs_hm, gab, gab, alog_b, dtb_b, nw, S_all[0], S_all[1], do, *pieces)
    return res[:7], res[7:]


def _swa_heads(kit, first, q, kp, kc, vp, vc, qnw, knw, sink, slope):
    W = WIN
    ri = lax.broadcasted_iota(jnp.int32, (W, W), 0)
    ci = lax.broadcasted_iota(jnp.int32, (W, W), 1)
    mask_c = ri >= ci
    mask_p = ci > ri + first * W
    dist_c = (ri - ci).astype(F32)
    dist_p = (ri - ci + W).astype(F32)
    kpn = _rms(kp, knw)
    kcn = _rms(kc, knw)
    qn = _rms(q, qnw)
    sc = jnp.where(mask_c, kit.nt(qn, kcn) * (HD ** -0.5) - slope * dist_c, -1e30)
    sp = jnp.where(mask_p, kit.nt(qn, kpn) * (HD ** -0.5) - slope * dist_p, -1e30)
    m = jnp.maximum(jnp.maximum(jnp.max(sc, axis=-1, keepdims=True), jnp.max(sp, axis=-1, keepdims=True)), sink)
    m = lax.stop_gradient(m)
    pc = jnp.exp(sc - m)
    pp = jnp.exp(sp - m)
    den = jnp.sum(pc, axis=-1, keepdims=True) + jnp.sum(pp, axis=-1, keepdims=True) + jnp.exp(sink - m)
    inv = 1.0 / den
    return kit.nn(pc * inv, vc) + kit.nn(pp * inv, vp)


def _swa_grads(kit, first, q, kp, kc, vp, vc, qnw, knw, sink, slope, do):
    W = WIN
    ri = lax.broadcasted_iota(jnp.int32, (W, W), 0)
    ci = lax.broadcasted_iota(jnp.int32, (W, W), 1)
    mask_c = ri >= ci
    mask_p = ci > ri + first * W
    dist_c = (ri - ci).astype(F32)
    dist_p = (ri - ci + W).astype(F32)
    scale = HD ** -0.5
    kpn, kp_vjp = jax.vjp(_rms, kp, knw)
    kcn, kc_vjp = jax.vjp(_rms, kc, knw)
    qn, q_vjp = jax.vjp(_rms, q, qnw)
    sc = jnp.where(mask_c, kit.nt(qn, kcn) * scale - slope * dist_c, -1e30)
    sp = jnp.where(mask_p, kit.nt(qn, kpn) * scale - slope * dist_p, -1e30)
    m = jnp.maximum(jnp.maximum(jnp.max(sc, axis=-1, keepdims=True), jnp.max(sp, axis=-1, keepdims=True)), sink)
    ec = jnp.exp(sc - m)
    ep = jnp.exp(sp - m)
    es = jnp.exp(sink - m)
    inv = 1.0 / (jnp.sum(ec, axis=-1, keepdims=True) + jnp.sum(ep, axis=-1, keepdims=True) + es)
    pc, pp = ec * inv, ep * inv
    dpc, dpp = kit.nt(do, vc), kit.nt(do, vp)
    delta = jnp.sum(dpc * pc, axis=-1, keepdims=True) + jnp.sum(dpp * pp, axis=-1, keepdims=True)
    dsc = pc * (dpc - delta) * scale
    dsp = pp * (dpp - delta) * scale
    dq, dqnw = q_vjp(kit.nn(dsc, kcn) + kit.nn(dsp, kpn))
    dkc, dknw_c = kc_vjp(kit.tn(dsc, qn))
    dkp, dknw_p = kp_vjp(kit.tn(dsp, qn))
    return dq, dkp, dkc, kit.tn(pp, do), kit.tn(pc, do), dqnw, dknw_c + dknw_p, -(es * inv) * delta


def _per_query_head(kv_ref):
    return jnp.concatenate([kv_ref[pl.ds(h // SGRP, 1)] for h in range(SQH)], axis=0)


def _per_kv_head(d):
    return jnp.concatenate([jnp.sum(d[g * SGRP:(g + 1) * SGRP], axis=0, keepdims=True) for g in range(SKVH)], axis=0)


def _swa_specs(blk):
    qspec = pl.BlockSpec((SQH, WIN, HD), lambda i: (1, blk(i), 0))
    cur = lambda grp: pl.BlockSpec((SKVH, WIN, HD), lambda i, grp=grp: (grp, blk(i), 0))
    prev = lambda grp: pl.BlockSpec((SKVH, WIN, HD), lambda i, grp=grp: (grp, jnp.maximum(blk(i) - 1, 0), 0))
    whole = pl.BlockSpec((1, HD), lambda i: (0, 0))
    col = pl.BlockSpec((SQH, WIN, 1), lambda i: (0, 0, 0))
    ospec = pl.BlockSpec((SQH, WIN, HD), lambda i: (0, blk(i), 0))
    return qspec, cur, prev, whole, col, ospec


def _swa_fwd(zs_hm, qnw, knw, sinks_col, slopes_col, o_buf, shards):
    T = zs_hm.shape[1]
    NB = T // WIN
    ns = len(shards)
    kit = _Kit(False)
    qspec, cur, prev, whole, col, _ = _swa_specs(lambda i: i)

    def body(*refs):
        q_ref, kp_ref, kc_ref, vp_ref, vc_ref, qnw_ref, knw_ref, s_ref, sl_ref = refs[:9]
        o_ref = refs[10 + ns]
        plan = _gather_plan(refs[10:10 + ns], refs[11 + ns:11 + 2 * ns], *refs[11 + 2 * ns:])

        @pl.when(pl.program_id(0) == 0)
        def _():
            _start(plan)

        first = (pl.program_id(0) == 0).astype(jnp.int32)
        o_ref[...] = _swa_heads(kit, first, q_ref[...], _per_query_head(kp_ref), _per_query_head(kc_ref),
                                _per_query_head(vp_ref), _per_query_head(vc_ref), qnw_ref[...], knw_ref[...],
                                s_ref[...], sl_ref[...])

        @pl.when(pl.program_id(0) == NB - 1)
        def _():
            _finish(plan)

    res = pl.pallas_call(
        body, name="swa_fwd", grid=(NB,),
        in_specs=[qspec, prev(8), cur(8), prev(9), cur(9), whole, whole, col, col] + _hbm_specs(1 + ns),
        out_specs=[pl.BlockSpec((SQH, WIN, HD), lambda i: (1, i, 0))] + _hbm_specs(ns),
        out_shape=[_sds(o_buf.shape)] + _gather_shapes(shards),
        input_output_aliases={9: 0},
        scratch_shapes=_gather_sems(ns),
        compiler_params=_cparams(("arbitrary",)),
    )(zs_hm, zs_hm, zs_hm, zs_hm, zs_hm, qnw, knw, sinks_col, slopes_col, o_buf, *shards)
    return res[0], res[1:]


def _swa_bwd(zs_hm, qnw, knw, sinks_col, slopes_col, dmix_hm, dproj):
    T = zs_hm.shape[1]
    NB = T // WIN
    kit = _Kit(False)
    qspec, cur, prev, whole, col, _ = _swa_specs(lambda i: NB - 1 - i)
    tail = NP - 4 * GW
    used = (SQH + 2 * SKVH) * HD

    def body(q_ref, kp_ref, kc_ref, vp_ref, vc_ref, qnw_ref, knw_ref, s_ref, sl_ref, do_ref, buf_ref,
             d_ref, dqnw_ref, dknw_ref, ds_ref, ck_scr, cv_scr):
        i = pl.program_id(0)
        first = (i == NB - 1).astype(jnp.int32)

        @pl.when(i == 0)
        def _():
            ck_scr[...] = jnp.zeros_like(ck_scr)
            cv_scr[...] = jnp.zeros_like(cv_scr)
            ds_ref[...] = jnp.zeros_like(ds_ref)
            dqnw_ref[...] = jnp.zeros_like(dqnw_ref)
            dknw_ref[...] = jnp.zeros_like(dknw_ref)

        dq, dkp, dkc, dvp, dvc, dqnw, dknw, dsink = _swa_grads(
            kit, first, q_ref[...], _per_query_head(kp_ref), _per_query_head(kc_ref), _per_query_head(vp_ref),
            _per_query_head(vc_ref), qnw_ref[...], knw_ref[...], s_ref[...], sl_ref[...], do_ref[...])
        dk = _per_kv_head(dkc) + ck_scr[...]
        dv = _per_kv_head(dvc) + cv_scr[...]
        d_ref[...] = jnp.concatenate([_merge_all(dq), _merge_all(dk), _merge_all(dv),
                                      jnp.zeros((WIN, tail - used), F32)], axis=1).astype(BF16)
        ck_scr[...] = _per_kv_head(dkp)
        cv_scr[...] = _per_kv_head(dvp)
        dqnw_ref[...] += dqnw
        dknw_ref[...] += dknw
        ds_ref[...] += jnp.broadcast_to(jnp.sum(dsink, axis=1, keepdims=True), dsink.shape)

    dospec = pl.BlockSpec((SQH, WIN, HD), lambda i: (1, NB - 1 - i, 0))
    dspec = pl.BlockSpec((WIN, tail), lambda i: (NB - 1 - i, 4 * GW // tail))
    res = pl.pallas_call(
        body, name="swa_bwd", grid=(NB,),
        in_specs=[qspec, prev(8), cur(8), prev(9), cur(9), whole, whole, col, col, dospec] + _hbm_specs(1),
        out_specs=[dspec, whole, whole, col],
        out_shape=[_sds(dproj.shape, dproj.dtype), _sds((1, HD)), _sds((1, HD)), _sds((SQH, WIN, 1))],
        input_output_aliases={10: 0},
        scratch_shapes=[pltpu.VMEM((SKVH, WIN, HD), F32), pltpu.VMEM((SKVH, WIN, HD), F32)],
        compiler_params=_cparams(("arbitrary",)),
    )(zs_hm, zs_hm, zs_hm, zs_hm, zs_hm, qnw, knw, sinks_col, slopes_col, dmix_hm, dproj)
    return res


GAB0 = 3 * GW + 1280


W_IN_ROWS = PROJ // N_CHIP
W_IN_ROWS_PAD = 736


def _permute_w_in_t(w_in_t):
    return jnp.concatenate([w_in_t[:4 * GW], w_in_t[4 * GW + 2 * GH:], w_in_t[4 * GW:4 * GW + 2 * GH],
                            jnp.zeros((NP - PROJ, D), w_in_t.dtype)], axis=0)


def _w_in_grad_pieces(g_t):
    g = jnp.concatenate([g_t[:4 * GW], g_t[GAB0:GAB0 + 2 * GH], g_t[4 * GW:GAB0]], axis=0)
    g = jnp.pad(g.reshape(N_CHIP, W_IN_ROWS, D), ((0, 0), (0, W_IN_ROWS_PAD - W_IN_ROWS), (0, 0)))
    return g.reshape(N_CHIP, 2, W_IN_ROWS_PAD // 2, D)


def _pieces_by_rows(g):
    return g.reshape(N_CHIP, 2, g.shape[0] // (2 * N_CHIP), D)


def _local_step(x, target, mod, n1w, w_in_pt, conv_w, alog, dtb, gnw, qnw, knw, sinks, n2w, shards):
    sh_out, sh_gate, sh_up, sh_down = shards
    T = x.shape[0]
    N = T // CHUNK
    shift1, scale1, gate1, shift2, scale2, gate2 = [mod[:, i * D:(i + 1) * D] for i in range(6)]

    h, proj, zs_hm, (a_out,) = _norm_in_proj(x, n1w, scale1, shift1, w_in_pt, [sh_out])
    w_out = a_out.reshape(D, D)
    qkv_hm = _conv_fwd(proj, conv_w)
    gab = proj[:, GAB0:GAB0 + 2 * GH].T.reshape(2 * GH, N, 1, CHUNK)
    alog_b = jnp.broadcast_to(alog.reshape(GH, 1, 1), (GH, 1, CHUNK))
    dtb_b = jnp.broadcast_to(dtb.reshape(GH, 1, 1), (GH, 1, CHUNK))
    sinks_col = jnp.broadcast_to(sinks.reshape(SQH, 1, 1), (SQH, WIN, 1))
    o_hm, S_all, (a_gate, a_up) = _gdn_fwd(qkv_hm, zs_hm, gab, alog_b, dtb_b, gnw, [sh_gate, sh_up])
    w_gut = _interleave_gate_up(a_gate.reshape(DFF, D), a_up.reshape(DFF, D))
    slopes = 2.0 ** (-8.0 * (jnp.arange(SQH, dtype=F32) + 1.0) / SQH)
    slopes_col = jnp.broadcast_to(slopes.reshape(SQH, 1, 1), (SQH, WIN, 1))
    o_hm, (a_down,) = _swa_fwd(zs_hm, qnw, knw, sinks_col, slopes_col, o_hm, [sh_down])
    w_down = a_down.reshape(DFF, D)
    mixcat, mixed, x1, h2 = _out_proj_resid_norm(o_hm, w_out, x, gate1, n2w, scale2, shift2)
    ab, act = _ffn_up_act(h2, w_gut)
    dy, dffn, dgate2, loss = _ffn_down_loss(act, w_down, x1, target, gate2)

    dab = _ffn_down_dx_act(dffn, w_down, ab)
    g_w_down = _matmul(act, dffn, ta=True, out_dtype=BF16, name="ffn_down_dw")
    g_w_gut = _matmul(dab, h2, ta=True, out_dtype=BF16, name="ffn_up_dw")
    dx1, dmixed, dgate1, dn2w, dscale2, dshift2 = _ffn_up_dx_resid_bwd(dab, w_gut, x, mixed, dy, gate1, n2w, scale2,
                                                                       shift2)
    g_w_out = _matmul(mixcat, dmixed, ta=True, out_dtype=BF16, name="out_proj_dw")
    dmix_hm = _matmul_nt_heads(dmixed, w_out, "out_proj_dx")
    g_gate_t, g_up_t = _split_gate_up(g_w_gut)
    pieces = [_pieces_by_rows(g_w_out), _pieces_by_rows(g_gate_t), _pieces_by_rows(g_up_t),
              _pieces_by_rows(g_w_down)]
    (dqkv_hm, dproj, dga, dgb, dalog, ddtb, dgnw), recv = _gdn_bwd(qkv_hm, zs_hm, gab, alog_b, dtb_b, gnw, S_all,
                                                                   dmix_hm, pieces)
    dproj, dqnw, dknw, dsinks = _swa_bwd(zs_hm, qnw, knw, sinks_col, slopes_col, dmix_hm, dproj)
    dproj, dconv = _conv_bwd(proj, conv_w, dqkv_hm, dproj)
    dgab = jnp.concatenate([dga, dgb], axis=0).reshape(2 * GH, T).T.astype(BF16)
    dproj = lax.dynamic_update_slice(dproj, jnp.concatenate([dgab, jnp.zeros((T, NP - PROJ), BF16)], axis=1),
                                     (0, GAB0))
    g_w_in_pt = _matmul(dproj, h, ta=True, out_dtype=BF16, name="in_proj_dw")
    (grad_x, dn1w, dscale1, dshift1), recv_in = _in_proj_dx_norm_bwd(dproj, w_in_pt, x, dx1, n1w, scale1, shift1,
                                                                     [_w_in_grad_pieces(g_w_in_pt)])

    dmod = jnp.concatenate([dshift1, dscale1, dgate1, dshift2, dscale2, dgate2], axis=1)
    big = list(recv_in) + list(recv)
    small = dict(mod=dmod, norm1_w=dn1w, norm2_w=dn2w, conv_w=dconv, a_log=dalog[:, 0, 0], dt_bias=ddtb[:, 0, 0],
                 gdn_norm_w=dgnw, q_norm_w=dqnw, k_norm_w=dknw, sinks=dsinks[:, 0, 0])
    return loss, grad_x, big, small


def _adamw(w, g, m, v):
    m2 = ADAM_B1 * m + (1.0 - ADAM_B1) * g
    v2 = ADAM_B2 * v + (1.0 - ADAM_B2) * (g * g)
    m_hat = m2 / (1.0 - ADAM_B1 ** ADAM_STEP)
    v_hat = v2 / (1.0 - ADAM_B2 ** ADAM_STEP)
    delta = -ADAM_LR * (m_hat / (jnp.sqrt(v_hat) + ADAM_EPS) + ADAM_WD * w)
    return delta, m2, v2


def _reduce_adamw(recv, w, m, v, name):
    _, R, C = recv.shape
    tc = _tile(C, 256)

    def body(r_ref, w_ref, m_ref, v_ref, o_ref):
        g = r_ref[0].astype(F32)
        for s in range(1, N_DEV):
            g = g + r_ref[s].astype(F32)
        delta, m2, v2 = _adamw(w_ref[...], g, m_ref[...], v_ref[...])
        o_ref[0] = g
        o_ref[1] = delta
        o_ref[2] = m2
        o_ref[3] = v2

    col = pl.BlockSpec((R, tc), lambda j: (0, j))
    return pl.pallas_call(
        body, name=name, grid=(C // tc,),
        in_specs=[pl.BlockSpec((N_DEV, R, tc), lambda j: (0, 0, j)), col, col, col],
        out_specs=pl.BlockSpec((4, R, tc), lambda j: (0, 0, j)),
        out_shape=_sds((4, R, C)),
        compiler_params=_cparams(("parallel",)),
    )(recv, w, m, v)


def _adamw_call(g, w, m, v, name):
    def body(g_ref, w_ref, m_ref, v_ref, o_ref):
        delta, m2, v2 = _adamw(w_ref[...], g_ref[...], m_ref[...], v_ref[...])
        o_ref[0] = delta
        o_ref[1] = m2
        o_ref[2] = v2

    return pl.pallas_call(body, name=name, out_shape=_sds((3,) + g.shape))(g, w, m, v)


ADA_N = 6 * D // N_CHIP
KPAD = 128


def _w_ada_update(c8p, dm, w, m, v):
    tr = 256

    def body(c_ref, dm_ref, w_ref, m_ref, v_ref, g_ref, d_ref, m2_ref, v2_ref):
        g = _raw1(_silu(c_ref[...]), dm_ref[...], _TN)
        delta, m2, v2 = _adamw(w_ref[...], g, m_ref[...], v_ref[...])
        g_ref[...] = g
        d_ref[...] = delta
        m2_ref[...] = m2
        v2_ref[...] = v2

    blk = pl.BlockSpec((tr, ADA_N), lambda i: (i, 0))
    return pl.pallas_call(
        body, name="w_ada_update", grid=(D // tr,),
        in_specs=[pl.BlockSpec((KPAD, tr), lambda i: (0, i)), pl.BlockSpec((KPAD, ADA_N), lambda i: (0, 0)),
                  blk, blk, blk],
        out_specs=[blk] * 4, out_shape=[_sds((D, ADA_N))] * 4,
        compiler_params=_cparams(("parallel",)),
    )(c8p, dm, w, m, v)


def _me():
    return lax.axis_index("x"), lax.axis_index("y"), lax.axis_index("c")


def _peer(k, me):
    mx, my, mc = me
    return (1 - mx if k & 4 else mx, 1 - my if k & 2 else my, 1 - mc if k & 1 else mc)


def _lin(p):
    return 4 * p[0] + 2 * p[1] + p[2]


def _remote(src, dst, ssem, rsem, dev):
    return pltpu.make_async_remote_copy(src_ref=src, dst_ref=dst, send_sem=ssem, recv_sem=rsem,
                                        device_id=dev, device_id_type=MESH)


def _all_gather8(x, name):
    def body(x_ref, out_ref, send_sems, recv_sems):
        me = _me()
        out_ref[_lin(me)] = x_ref[...]
        sends = []
        for k in range(1, N_DEV):
            cp = _remote(x_ref, out_ref.at[_lin(me)], send_sems.at[k - 1], recv_sems.at[k - 1], _peer(k, me))
            cp.start()
            sends.append(cp)
        for k in range(1, N_DEV):
            p = _peer(k, me)
            _remote(x_ref, out_ref.at[_lin(p)], send_sems.at[k - 1], recv_sems.at[k - 1], p).wait_recv()
        for cp in sends:
            cp.wait_send()

    return pl.pallas_call(
        body, name=name,
        out_shape=_sds((N_DEV,) + x.shape, x.dtype),
        in_specs=[pl.BlockSpec(memory_space=pltpu.VMEM)],
        out_specs=pl.BlockSpec(memory_space=pltpu.VMEM),
        scratch_shapes=[pltpu.SemaphoreType.DMA((N_DEV - 1,)), pltpu.SemaphoreType.DMA((N_DEV - 1,))],
    )(x)


def _ag8_plan(src, out, send_sems, recv_sems):
    me = _me()
    sends, recvs = [], []
    for k in range(1, N_DEV):
        p = _peer(k, me)
        sends.append(_remote(src, out.at[_lin(me)], send_sems.at[k - 1], recv_sems.at[k - 1], p))
        recvs.append(_remote(src, out.at[_lin(p)], send_sems.at[k - 1], recv_sems.at[k - 1], p))
    return [], sends, recvs


def _prologue(c_row, conv_sh, w_ada, b_sh, w_in_sh):
    def body(c_ref, cv_ref, wa_ref, b_ref, win_ref, call_ref, cvall_ref, mods_ref, ain_ref, c16_scr, mp_scr,
             c_send, c_recv, cv_send, cv_recv, m_send, m_recv, w_send, w_recv, w_local):
        me = _lin(_me())
        w_plan = _gather_half_plan([win_ref], [ain_ref], w_send, w_recv, w_local)
        _start(w_plan)
        c_plan = _ag8_plan(c_ref, call_ref, c_send, c_recv)
        cv_plan = _ag8_plan(cv_ref, cvall_ref, cv_send, cv_recv)
        call_ref[me] = c_ref[...]
        cvall_ref[me] = cv_ref[...]
        _start(c_plan)
        _start(cv_plan)
        _finish(c_plan)
        c16_scr[...] = jnp.zeros_like(c16_scr)
        for d in range(N_DEV):
            c16_scr[pl.ds(d, 1), :] = call_ref[d]
        mp_scr[...] = _raw1(_silu(c16_scr[...]), wa_ref[...], _NN) + b_ref[...]
        mods_ref[me] = mp_scr[...]
        m_plan = _ag8_plan(mp_scr, mods_ref, m_send, m_recv)
        _start(m_plan)
        _finish(cv_plan)
        _finish(m_plan)
        _finish(w_plan)

    vmem = pl.BlockSpec(memory_space=pltpu.VMEM)
    sems = lambda n: pltpu.SemaphoreType.DMA((n,))
    return pl.pallas_call(
        body, name="prologue",
        in_specs=[vmem] * 4 + _hbm_specs(1), out_specs=[vmem] * 3 + _hbm_specs(1),
        out_shape=[_sds((N_DEV,) + c_row.shape), _sds((N_DEV,) + conv_sh.shape), _sds((N_DEV, 16, ADA_N)),
                   _sds((N_CHIP,) + w_in_sh.shape, w_in_sh.dtype)],
        scratch_shapes=[pltpu.VMEM((16, D), F32), pltpu.VMEM((16, ADA_N), F32)] + [sems(N_DEV - 1)] * 6
                       + _gather_sems(1),
        compiler_params=_cparams(),
    )(c_row, conv_sh, w_ada, b_sh, w_in_sh)


def _hbm_specs(n):
    return [pl.BlockSpec(memory_space=pl.ANY)] * n


def _gather_shapes(shards):
    return [_sds((N_CHIP,) + s.shape, s.dtype) for s in shards]


def _gather_sems(n):
    return [pltpu.SemaphoreType.DMA((3 * n,)), pltpu.SemaphoreType.DMA((3 * n,)), pltpu.SemaphoreType.DMA((n,))]


def _gather_plan(ins, outs, send_sems, recv_sems, local_sems):
    mx, my, mc = _me()
    chips = [(1 - mx, my), (mx, 1 - my), (1 - mx, 1 - my)]
    local, sends, recvs = [], [], []
    for a in range(len(ins)):
        local.append(pltpu.make_async_copy(ins[a], outs[a].at[2 * mx + my], local_sems.at[a]))
        for k, (px, py) in enumerate(chips):
            sems = (send_sems.at[3 * a + k], recv_sems.at[3 * a + k], (px, py, mc))
            sends.append(_remote(ins[a], outs[a].at[2 * mx + my], *sems))
            recvs.append(_remote(ins[a], outs[a].at[2 * px + py], *sems))
    return local, sends, recvs


def _gather_half_plan(ins, outs, send_sems, recv_sems, local_sems):
    mx, my, mc = _me()
    chips = [(1 - mx, my), (mx, 1 - my), (1 - mx, 1 - my)]
    local, sends, recvs = [], [], []
    for a in range(len(ins)):
        h = ins[a].shape[0] // 2
        mine = pl.ds(pl.multiple_of(mc * h, 16), h)
        local.append(pltpu.make_async_copy(ins[a], outs[a].at[2 * mx + my], local_sems.at[a]))
        for k, (px, py) in enumerate(chips):
            sems = (send_sems.at[3 * a + k], recv_sems.at[3 * a + k], (px, py, mc))
            sends.append(_remote(ins[a].at[mine], outs[a].at[2 * mx + my, mine], *sems))
            recvs.append(_remote(ins[a].at[mine], outs[a].at[2 * px + py, mine], *sems))
    return local, sends, recvs


def _sibling_fill(pieces):
    h = pieces.shape[1] // 2

    def body(p_ref, o_ref, send_sems, recv_sems):
        mx, my, mc = _me()
        sib = (mx, my, 1 - mc)
        chips = [(1 - mx, my), (mx, 1 - my), (1 - mx, 1 - my)]
        half = lambda c: pl.ds(pl.multiple_of(c * h, 16), h)
        o_ref[2 * mx + my] = p_ref[2 * mx + my]
        sends = []
        for k, (px, py) in enumerate(chips):
            j = 2 * px + py
            o_ref[j, half(mc), :] = p_ref[j, half(mc), :]
            cp = _remote(p_ref.at[j, half(mc)], o_ref.at[j, half(mc)], send_sems.at[k], recv_sems.at[k], sib)
            cp.start()
            sends.append(cp)
        for k, (px, py) in enumerate(chips):
            j = 2 * px + py
            _remote(p_ref.at[j, half(mc)], o_ref.at[j, half(1 - mc)], send_sems.at[k], recv_sems.at[k],
                    sib).wait_recv()
        for cp in sends:
            cp.wait_send()

    vmem = pl.BlockSpec(memory_space=pltpu.VMEM)
    return pl.pallas_call(
        body, name="sibling_fill", out_shape=_sds(pieces.shape, pieces.dtype),
        in_specs=[vmem], out_specs=vmem,
        scratch_shapes=[pltpu.SemaphoreType.DMA((N_CHIP - 1,)), pltpu.SemaphoreType.DMA((N_CHIP - 1,))],
        compiler_params=_cparams(),
    )(pieces)


def _start(plan):
    local, sends, _ = plan
    for cp in local + sends:
        cp.start()


def _finish(plan):
    local, sends, recvs = plan
    for cp in recvs:
        cp.wait_recv()
    for cp in sends:
        cp.wait_send()
    for cp in local:
        cp.wait()


def _exchange_shapes(pieces):
    return [_sds((N_DEV,) + p.shape[2:], p.dtype) for p in pieces]


def _exchange_sems(n):
    return [pltpu.SemaphoreType.DMA(((N_DEV - 1) * n,)), pltpu.SemaphoreType.DMA(((N_DEV - 1) * n,)),
            pltpu.SemaphoreType.DMA((n,))]


def _exchange_plan(ins, outs, send_sems, recv_sems, local_sems):
    me = _me()
    mx, my, mc = me
    local, sends, recvs = [], [], []
    for a in range(len(ins)):
        local.append(pltpu.make_async_copy(ins[a].at[2 * mx + my, mc], outs[a].at[_lin(me)], local_sems.at[a]))
        for k in range(1, N_DEV):
            p = _peer(k, me)
            s = (N_DEV - 1) * a + k - 1
            sends.append(_remote(ins[a].at[2 * p[0] + p[1], p[2]], outs[a].at[_lin(me)], send_sems.at[s],
                                 recv_sems.at[s], p))
            recvs.append(_remote(ins[a].at[2 * mx + my, mc], outs[a].at[_lin(p)], send_sems.at[s],
                                 recv_sems.at[s], p))
    return local, sends, recvs


REDUCE_VMEM = 56 * 1024 * 1024


def _reduce_swap(recvs):
    n = len(recvs)

    def body(*refs):
        r_refs, o_refs = refs[:n], refs[n:2 * n]
        send_sems, recv_sems = refs[2 * n:]
        mx, my, mc = _me()
        sib = (mx, my, 1 - mc)
        half = lambda a, c: o_refs[a].at[pl.ds(pl.multiple_of(c * recvs[a].shape[1], 8), recvs[a].shape[1])]
        sends = []
        for a in range(n):
            g = r_refs[a][0].astype(F32)
            for s in range(1, N_DEV):
                g = g + r_refs[a][s].astype(F32)
            half(a, mc)[...] = g
            cp = _remote(half(a, mc), half(a, mc), send_sems.at[a], recv_sems.at[a], sib)
            cp.start()
            sends.append(cp)
        for a in range(n):
            _remote(half(a, mc), half(a, 1 - mc), send_sems.at[a], recv_sems.at[a], sib).wait_recv()
        for cp in sends:
            cp.wait_send()

    vmem = pl.BlockSpec(memory_space=pltpu.VMEM)
    return pl.pallas_call(
        body, name="reduce_swap", out_shape=[_sds((2 * r.shape[1], r.shape[2])) for r in recvs],
        in_specs=[vmem] * n, out_specs=[vmem] * n,
        scratch_shapes=[pltpu.SemaphoreType.DMA((n,)), pltpu.SemaphoreType.DMA((n,))],
        compiler_params=_cparams(None, REDUCE_VMEM),
    )(*recvs)


def _adamw_big(g, w, m, v, name):
    rows, cols = g.shape
    tr = next((t for t in (256, 176, 128, 64, 8) if rows % t == 0), None)
    if tr is None:
        tc = _tile(cols, 256)
        blk, grid = pl.BlockSpec((rows, tc), lambda i: (0, i)), (cols // tc,)
    else:
        blk, grid = pl.BlockSpec((tr, cols), lambda i: (i, 0)), (rows // tr,)

    def body(g_ref, w_ref, m_ref, v_ref, go_ref, d_ref, m2_ref, v2_ref):
        g = g_ref[...]
        delta, m2, v2 = _adamw(w_ref[...], g, m_ref[...], v_ref[...])
        go_ref[...] = g
        d_ref[...] = delta
        m2_ref[...] = m2
        v2_ref[...] = v2

    return pl.pallas_call(
        body, name=name, grid=grid,
        in_specs=[blk] * 4, out_specs=[blk] * 4, out_shape=[_sds((rows, cols))] * 4,
        compiler_params=_cparams(("parallel",)),
    )(g, w, m, v)


SMALL_ORDER = (("mod", 6 * D), ("norm1_w", D), ("norm2_w", D), ("conv_w", CONVW * 3 * GW), ("a_log", GH),
               ("dt_bias", GH), ("gdn_norm_w", HD), ("q_norm_w", HD), ("k_norm_w", HD), ("sinks", SQH), ("loss", 1))
SMALL_R = 120


def _pack_small(d):
    parts = [d[k].reshape(-1).astype(F32) if k in d else jnp.zeros((n,), F32) for k, n in SMALL_ORDER]
    used = sum(n for _, n in SMALL_ORDER)
    parts.append(jnp.zeros((SMALL_R * LANE - used,), F32))
    return jnp.concatenate(parts).reshape(SMALL_R, LANE)


def _unpack_small(pk):
    flat = pk.reshape(-1)
    out, r = {}, 0
    for k, n in SMALL_ORDER:
        out[k] = flat[r:r + n]
        r += n
    return out


def kernel(x, c, w_ada, b_ada, norm1_w, w_in, conv_w, a_log, dt_bias, gdn_norm_w, q_norm_w, k_norm_w, sinks, w_out, norm2_w, w_gate, w_up, w_down, loss_target, m_w_ada, m_b_ada, m_norm1_w, m_w_in, m_conv_w, m_a_log, m_dt_bias, m_gdn_norm_w, m_q_norm_w, m_k_norm_w, m_sinks, m_w_out, m_norm2_w, m_w_gate, m_w_up, m_w_down, v_w_ada, v_b_ada, v_norm1_w, v_w_in, v_conv_w, v_a_log, v_dt_bias, v_gdn_norm_w, v_q_norm_w, v_k_norm_w, v_sinks, v_w_out, v_norm2_w, v_w_gate, v_w_up, v_w_down):
    mx, my, mc = _me()
    chip = 2 * mx + my
    dev = 4 * mx + 2 * my + mc
    T = x.shape[1]

    as_rows = lambda t, transposed: t[0].T if transposed else t[0]
    transposed = (True, False, True, True, False)
    big_w = [as_rows(t, tr) for t, tr in zip((w_in, w_out, w_gate, w_up, w_down), transposed)]
    shards = [t.astype(BF16) for t in big_w]

    b_sh = lax.dynamic_slice(b_ada, (0, chip * ADA_N), (1, ADA_N))
    w_in_sh = jnp.pad(shards[0], ((0, W_IN_ROWS_PAD - W_IN_ROWS), (0, 0)))
    c_all, conv_all, mods, a_in = _prologue(c, conv_w.reshape(CONVW, 3 * GW // N_CHIP), w_ada[0], b_sh, w_in_sh)
    c8 = c_all.reshape(N_DEV, D)
    conv_full = jnp.concatenate([conv_all[2 * j] for j in range(N_CHIP)], axis=1)
    mod = jnp.concatenate([lax.dynamic_slice(mods[2 * j], (dev, 0), (1, ADA_N)) for j in range(N_CHIP)], axis=1)
    w_in_pt = _permute_w_in_t(_sibling_fill(a_in)[:, :W_IN_ROWS].reshape(PROJ, D))

    loss, grad_x, big, small = _local_step(
        x[0], loss_target[0], mod, norm1_w, w_in_pt, conv_full, a_log, dt_bias, gdn_norm_w,
        q_norm_w, k_norm_w, sinks, norm2_w, shards[1:])

    small["loss"] = loss[:, :1]
    sg = _all_gather8(_pack_small(small), "gather_small_grads")
    rep = dict(mod=(b_ada, m_b_ada, v_b_ada), norm1_w=(norm1_w, m_norm1_w, v_norm1_w),
               norm2_w=(norm2_w, m_norm2_w, v_norm2_w), a_log=(a_log, m_a_log, v_a_log),
               dt_bias=(dt_bias, m_dt_bias, v_dt_bias), gdn_norm_w=(gdn_norm_w, m_gdn_norm_w, v_gdn_norm_w),
               q_norm_w=(q_norm_w, m_q_norm_w, v_q_norm_w), k_norm_w=(k_norm_w, m_k_norm_w, v_k_norm_w),
               sinks=(sinks, m_sinks, v_sinks))
    wmv = [_pack_small({k: t[i] for k, t in rep.items()}) for i in range(3)]
    sres = _reduce_adamw(sg, wmv[0], wmv[1], wmv[2], "small_reduce_adamw")
    s_g, s_d, s_m, s_v = [_unpack_small(sres[i]) for i in range(4)]
    loss_out = s_g["loss"][0]

    g_conv = lax.dynamic_slice(s_g["conv_w"].reshape(CONVW, 3 * GW), (0, chip * (3 * GW // N_CHIP)),
                               (CONVW, 3 * GW // N_CHIP))
    pad16 = lambda t: jnp.concatenate([t.reshape(12, LANE), jnp.zeros((4, LANE), F32)], axis=0)
    cres = _adamw_call(pad16(g_conv), pad16(conv_w), pad16(m_conv_w), pad16(v_conv_w), "conv_adamw")
    conv_out = [g_conv.reshape(conv_w.shape)] + [cres[i, :12].reshape(conv_w.shape) for i in range(3)]

    dmod8 = sg[:, :6 * D // LANE].reshape(N_DEV, 6 * D)
    dm = lax.dynamic_slice(dmod8, (0, chip * ADA_N), (N_DEV, ADA_N))
    zpad = lambda t: jnp.concatenate([t, jnp.zeros((KPAD - N_DEV, t.shape[1]), F32)], axis=0)
    ares = _w_ada_update(zpad(c8), zpad(dm), w_ada[0], m_w_ada[0], v_w_ada[0])

    names = ("w_in", "w_out", "w_gate", "w_up", "w_down")
    g_full = list(_reduce_swap(big))
    g_full[0] = g_full[0][:W_IN_ROWS]
    big_m = [as_rows(t, tr) for t, tr in zip((m_w_in, m_w_out, m_w_gate, m_w_up, m_w_down), transposed)]
    big_v = [as_rows(t, tr) for t, tr in zip((v_w_in, v_w_out, v_w_gate, v_w_up, v_w_down), transposed)]
    upd = [_adamw_big(g, w, m, v, "adamw_" + nm) for g, w, m, v, nm in zip(g_full, big_w, big_m, big_v, names)]
    back = lambda t, tr: (t.T if tr else t)[None]
    bg, bd, bm, bv = [[back(u[i], tr) for u, tr in zip(upd, transposed)] for i in range(4)]

    def group(a_i, small_d, conv_i, big_l):
        s = lambda k, ref: small_d[k].reshape(ref.shape)
        return [ares[a_i][None], s("mod", b_ada), s("norm1_w", norm1_w), big_l[0], conv_out[conv_i],
                s("a_log", a_log), s("dt_bias", dt_bias), s("gdn_norm_w", gdn_norm_w), s("q_norm_w", q_norm_w),
                s("k_norm_w", k_norm_w), s("sinks", sinks), big_l[1], s("norm2_w", norm2_w), big_l[2], big_l[3],
                big_l[4]]

    outs = [loss_out, grad_x[None]]
    outs += group(0, s_g, 0, bg) + group(1, s_d, 1, bd) + group(2, s_m, 2, bm) + group(3, s_v, 3, bv)
    return tuple(outs)
```

```python
import jax
import jax.numpy as jnp
from jax import lax
from jax.experimental import pallas as pl
from jax.experimental.pallas import tpu as pltpu

F32 = jnp.float32
BF16 = jnp.bfloat16
MESH = pl.DeviceIdType.MESH

D = 1024
HD = 64
GH = 8
GW = GH * HD
SQH = 8
SKVH = 2
SGRP = SQH // SKVH
WIN = 128
CONVW = 4
CHUNK = 64
DFF = 2816
PROJ = 2832
NP = 3072
EPS = 1e-6
N_DEV = 8
N_CHIP = 4

ADAM_LR = 0.001
ADAM_B1 = 0.9
ADAM_B2 = 0.999
ADAM_EPS = 1e-08
ADAM_WD = 0.01
ADAM_STEP = 10

VMEM_LIMIT = 48 * 1024 * 1024
GDN_BWD_VMEM = 58 * 1024 * 1024
LANE = 128


def _cparams(sem=None, vmem=VMEM_LIMIT):
    return pltpu.CompilerParams(dimension_semantics=sem, vmem_limit_bytes=vmem)


_NN = ((1,), (0,))
_NT = ((1,), (1,))
_TN = ((0,), (0,))


def _dot(a, b, dims):
    if a.ndim == 3:
        (ca,), (cb,) = dims
        return lax.dot_general(a, b, (((ca + 1,), (cb + 1,)), ((0,), (0,))), preferred_element_type=F32)
    return lax.dot_general(a, b, (dims, ((), ())), preferred_element_type=F32)


def _raw1(a, b, dims):
    return _dot(a.astype(BF16), b.astype(BF16), dims)


def _raw3(a, b, dims):
    ah = a.astype(BF16)
    al = (a - ah.astype(F32)).astype(BF16)
    bh = b.astype(BF16)
    bl = (b - bh.astype(F32)).astype(BF16)
    return _dot(ah, bh, dims) + (_dot(al, bh, dims) + _dot(ah, bl, dims))


def _make_diff_mm(raw):
    @jax.custom_vjp
    def nn(a, b):
        return raw(a, b, _NN)

    @jax.custom_vjp
    def nt(a, b):
        return raw(a, b, _NT)

    @jax.custom_vjp
    def tn(a, b):
        return raw(a, b, _TN)

    nn.defvjp(lambda a, b: (raw(a, b, _NN), (a, b)), lambda r, g: (nt(g, r[1]), tn(r[0], g)))
    nt.defvjp(lambda a, b: (raw(a, b, _NT), (a, b)), lambda r, g: (nn(g, r[1]), tn(g, r[0])))
    tn.defvjp(lambda a, b: (raw(a, b, _TN), (a, b)), lambda r, g: (nt(r[1], g), nn(r[0], g)))
    return nn, nt, tn


def _tri_inv_raw(a, nn3):
    n = a.shape[-1]
    ri = lax.broadcasted_iota(jnp.int32, (n, n), 0)
    ci = lax.broadcasted_iota(jnp.int32, (n, n), 1)
    t = (ri == ci).astype(F32)
    for lvl in range((n - 1).bit_length()):
        same_pair = (ri >> (lvl + 1)) == (ci >> (lvl + 1))
        lower_left = (((ri >> lvl) & 1) == 1) & (((ci >> lvl) & 1) == 0)
        y = jnp.where(same_pair & lower_left, a, 0.0)
        t = t - y if lvl == 0 else t - nn3(nn3(t, y), t)
    return t


class _Kit:
    def __init__(self, diff):
        if diff:
            self.nn, self.nt, self.tn = _make_diff_mm(_raw1)
            self.nn3, self.nt3, self.tn3 = _make_diff_mm(_raw3)
            nn3, nt3, tn3 = self.nn3, self.nt3, self.tn3

            @jax.custom_vjp
            def inv(a, t):
                return t

            def inv_fwd(a, t):
                return t, t

            def inv_bwd(t, g):
                return -tn3(t, nt3(g, t)), jnp.zeros_like(t)

            inv.defvjp(inv_fwd, inv_bwd)
            self.inv = inv
        else:
            self.nn = lambda a, b: _raw1(a, b, _NN)
            self.nt = lambda a, b: _raw1(a, b, _NT)
            self.tn = lambda a, b: _raw1(a, b, _TN)
            self.nn3 = lambda a, b: _raw3(a, b, _NN)
            self.nt3 = lambda a, b: _raw3(a, b, _NT)
            self.tn3 = lambda a, b: _raw3(a, b, _TN)
            self.inv = lambda a, t: _tri_inv_raw(a, self.nn3) if t is None else t


def _sigmoid(x):
    return 1.0 / (1.0 + jnp.exp(-x))


def _silu(x):
    return x * _sigmoid(x)


def _rms(x, w):
    return x * lax.rsqrt(jnp.mean(x * x, axis=-1, keepdims=True) + EPS) * w


def _tile(dim, target):
    t = (min(dim, target) // LANE) * LANE
    while t >= LANE:
        if dim % t == 0:
            return t
        t -= LANE
    return dim


MM_TM, MM_TN, MM_TK = 1408, 1536, 1408


def _matmul(a, b, ta=False, tb=False, out_dtype=F32, name="matmul", gather=None, exchange=None):
    carried = gather if gather is not None else exchange if exchange is not None else []
    nc = len(carried)
    if ta:
        K, M = a.shape
    else:
        M, K = a.shape
    if tb:
        N, K2 = b.shape
    else:
        K2, N = b.shape
    assert K == K2, (a.shape, b.shape, ta, tb)
    tm, tn, tk = _tile(M, MM_TM), _tile(N, MM_TN), _tile(K, MM_TK)
    nk = K // tk
    dims = ((0,) if ta else (1,), (1,) if tb else (0,))

    grid = (M // tm, N // tn, nk)

    def body(*refs):
        a_ref, b_ref = refs[:2]
        o_ref = refs[2 + nc]
        scratch = refs[3 + 2 * nc:]
        k = pl.program_id(2)
        if nc:
            make_plan = _gather_plan if gather is not None else _exchange_plan
            plan = make_plan(refs[2:2 + nc], refs[3 + nc:3 + 2 * nc], *scratch[-3:])
            at = lambda pos: ((pl.program_id(0) == pos[0]) & (pl.program_id(1) == pos[1]) & (k == pos[2]))

            @pl.when(at((0, 0, 0)))
            def _():
                _start(plan)

        part = _dot(a_ref[...].astype(BF16), b_ref[...].astype(BF16), dims)
        if nk == 1:
            o_ref[...] = part.astype(o_ref.dtype)
        else:
            acc_ref = scratch[0]

            @pl.when(k == 0)
            def _():
                acc_ref[...] = part

            @pl.when((k > 0) & (k < nk - 1))
            def _():
                acc_ref[...] += part

            @pl.when(k == nk - 1)
            def _():
                o_ref[...] = (acc_ref[...] + part).astype(o_ref.dtype)

        if nc:
            @pl.when(at((grid[0] - 1, grid[1] - 1, nk - 1)))
            def _():
                _finish(plan)

    a_spec = (pl.BlockSpec((tk, tm), lambda i, j, k: (k, i)) if ta
              else pl.BlockSpec((tm, tk), lambda i, j, k: (i, k)))
    b_spec = (pl.BlockSpec((tn, tk), lambda i, j, k: (j, k)) if tb
              else pl.BlockSpec((tk, tn), lambda i, j, k: (k, j)))
    if gather is not None:
        c_shapes, c_sems = _gather_shapes(carried), _gather_sems(nc)
    elif exchange is not None:
        c_shapes, c_sems = _exchange_shapes(carried), _exchange_sems(nc)
    else:
        c_shapes, c_sems = [], []
    res = pl.pallas_call(
        body, name=name, grid=grid,
        in_specs=[a_spec, b_spec] + _hbm_specs(nc),
        out_specs=[pl.BlockSpec((tm, tn), lambda i, j, k: (i, j))] + _hbm_specs(nc),
        out_shape=[jax.ShapeDtypeStruct((M, N), out_dtype)] + c_shapes,
        scratch_shapes=([pltpu.VMEM((tm, tn), F32)] if nk > 1 else []) + c_sems,
        compiler_params=_cparams(("arbitrary",) * 3 if nc else ("parallel", "parallel", "arbitrary")),
    )(a, b, *carried)
    return (res[0], res[1:]) if nc else res[0]


def _sds(shape, dtype=F32):
    return jax.ShapeDtypeStruct(shape, dtype)


def _norm_mod(x, nw, scale, shift):
    return _rms(x, nw) * (1.0 + scale) + shift


IN_PROJ_VMEM = 56 * 1024 * 1024


def _norm_in_proj(x, nw, scale, shift, w_in_pt, shards):
    T = x.shape[0]
    N = w_in_pt.shape[0]
    tm, tn = _tile(T, 1024), 3 * GW
    nm, nn = T // tm, N // tn
    nz = (GAB0 - 3 * GW) // HD
    ns = len(shards)

    def body(*refs):
        x_ref, nw_ref, sc_ref, sh_ref, w_ref = refs[:5]
        h_ref, o_ref, zs_ref = refs[5 + ns:8 + ns]
        plan = _gather_plan(refs[5:5 + ns], refs[8 + ns:8 + 2 * ns], *refs[8 + 2 * ns:])
        i, j = pl.program_id(0), pl.program_id(1)

        @pl.when((i == 0) & (j == 0))
        def _():
            _start(plan)

        @pl.when(j == 0)
        def _():
            for r0 in range(0, tm, ROWS_EPI):
                rows = pl.ds(r0, ROWS_EPI)
                h_ref[rows, :] = _norm_mod(x_ref[rows, :], nw_ref[...], sc_ref[...], sh_ref[...]).astype(BF16)

        o = _dot(h_ref[...], w_ref[...], _NT)
        o_ref[...] = o

        @pl.when(j == 1)
        def _():
            for p in range(nz // 2):
                zs_ref[2 * p], zs_ref[2 * p + 1] = _split_pair(o[:, p * LANE:(p + 1) * LANE])

        @pl.when((i == nm - 1) & (j == nn - 1))
        def _():
            _finish(plan)

    vec = pl.BlockSpec((1, D), lambda i, j: (0, 0))
    res = pl.pallas_call(
        body, name="norm1_in_proj", grid=(nm, nn),
        in_specs=[pl.BlockSpec((tm, D), lambda i, j: (i, 0)), vec, vec, vec,
                  pl.BlockSpec((tn, D), lambda i, j: (j, 0))] + _hbm_specs(ns),
        out_specs=[pl.BlockSpec((tm, D), lambda i, j: (i, 0)), pl.BlockSpec((tm, tn), lambda i, j: (i, j)),
                   pl.BlockSpec((nz, tm, HD), lambda i, j: (0, i, 0))] + _hbm_specs(ns),
        out_shape=[_sds((T, D), BF16), _sds((T, N)), _sds((nz, T, HD))] + _gather_shapes(shards),
        scratch_shapes=_gather_sems(ns),
        compiler_params=_cparams(("arbitrary", "arbitrary"), IN_PROJ_VMEM),
    )(x, nw, scale, shift, w_in_pt, *shards)
    return res[0], res[1], res[2], res[3:]


ROWS_TM = 512
ROWS_EPI = 256


def _matmul_rows(a, b, epi, tiled, consts, out_tiled, out_acc, name, pieces=()):
    T, K = a.shape
    tm, tk = _tile(T, ROWS_TM), _tile(K, MM_TK)
    nm, nk = T // tm, K // tk
    npc, nt, ncst, no, na = len(pieces), len(tiled), len(consts), len(out_tiled), len(out_acc)
    n_in = 2 + nt + ncst

    def body(*refs):
        a_ref, b_ref = refs[:2]
        t_refs, c_refs = refs[2:2 + nt], refs[2 + nt:n_in]
        o_refs = refs[n_in + npc:n_in + npc + no]
        acc_refs = refs[n_in + npc + no:n_in + npc + no + na]
        n_out = no + na + npc
        res_ref = refs[n_in + npc + n_out]
        plan = _exchange_plan(refs[n_in:n_in + npc], refs[n_in + npc + no + na:n_in + npc + n_out],
                              *refs[n_in + npc + n_out + 1:]) if npc else None
        i, k = pl.program_id(0), pl.program_id(1)

        @pl.when((i == 0) & (k == 0))
        def _():
            for r in acc_refs:
                r[...] = jnp.zeros_like(r)
            if npc:
                _start(plan)

        part = _dot(a_ref[...], b_ref[...], _NN)

        @pl.when(k == 0)
        def _():
            res_ref[...] = part

        @pl.when(k > 0)
        def _():
            res_ref[...] += part

        @pl.when(k == nk - 1)
        def _():
            for r0 in range(0, tm, ROWS_EPI):
                rows = pl.ds(r0, ROWS_EPI)
                outs = epi(res_ref[rows, :], *[r[rows, :] for r in t_refs], *[r[...] for r in c_refs])
                for r, v in zip(o_refs, outs[:no]):
                    r[rows, :] = v.astype(r.dtype)
                for r, v in zip(acc_refs, outs[no:]):
                    r[...] += v

        if npc:
            @pl.when((i == nm - 1) & (k == nk - 1))
            def _():
                _finish(plan)

    row = lambda w: pl.BlockSpec((tm, w), lambda i, k: (i, 0))
    whole = lambda s: pl.BlockSpec(s.shape, lambda i, k: (0, 0))
    res = pl.pallas_call(
        body, name=name, grid=(nm, nk),
        in_specs=[pl.BlockSpec((tm, tk), lambda i, k: (i, k)), pl.BlockSpec((tk, D), lambda i, k: (k, 0))]
                 + [row(t.shape[1]) for t in tiled] + [whole(c) for c in consts] + _hbm_specs(npc),
        out_specs=[row(s.shape[1]) for s in out_tiled] + [whole(s) for s in out_acc] + _hbm_specs(npc),
        out_shape=list(out_tiled) + list(out_acc) + (_exchange_shapes(pieces) if npc else []),
        scratch_shapes=[pltpu.VMEM((tm, D), F32)] + (_exchange_sems(npc) if npc else []),
        compiler_params=_cparams(("arbitrary", "arbitrary")),
    )(a, b, *tiled, *consts, *pieces)
    return res[:no + na], res[no + na:]


def _in_proj_dx_norm_bwd(dproj, w_in_pt, x, dres, nw, scale, shift, pieces):
    T = x.shape[0]

    def epi(dh, x, dres, nw, scale, shift):
        _, vjp = jax.vjp(_norm_mod, x, nw, scale, shift)
        dx, dnw, dsc, dsh = vjp(dh)
        return dx + dres, dnw, dsc, dsh

    return _matmul_rows(dproj, w_in_pt, epi, [x, dres], [nw, scale, shift], [_sds((T, D))], [_sds((1, D))] * 3,
                        "in_proj_dx_norm1_bwd", pieces)


def _out_proj_resid_norm(o_hm, w_out, x, gate1, nw, scale, shift):
    T = x.shape[0]
    nheads = o_hm.shape[0]
    tm = _tile(T, ROWS_TM)

    def body(o_ref, w_ref, x_ref, g_ref, nw_ref, sc_ref, sh_ref, cat_ref, mixed_ref, x1_ref, h2_ref):
        cat = jnp.concatenate([_merge_pair(o_ref[2 * p], o_ref[2 * p + 1]) for p in range(nheads // 2)], axis=1)
        cat_ref[...] = cat.astype(BF16)
        mixed_ref[...] = _dot(cat_ref[...], w_ref[...], _NN)
        for r0 in range(0, tm, ROWS_EPI):
            rows = pl.ds(r0, ROWS_EPI)
            x1, h2 = _resid_norm(x_ref[rows, :], mixed_ref[rows, :], g_ref[...], nw_ref[...], sc_ref[...], sh_ref[...])
            x1_ref[rows, :] = x1
            h2_ref[rows, :] = h2.astype(BF16)

    row = pl.BlockSpec((tm, D), lambda i: (i, 0))
    vec = pl.BlockSpec((1, D), lambda i: (0, 0))
    return pl.pallas_call(
        body, name="out_proj_resid_norm2", grid=(T // tm,),
        in_specs=[pl.BlockSpec((nheads, tm, HD), lambda i: (0, i, 0)), pl.BlockSpec((D, D), lambda i: (0, 0)), row,
                  vec, vec, vec, vec],
        out_specs=[row, row, row, row],
        out_shape=[_sds((T, D), BF16), _sds((T, D)), _sds((T, D)), _sds((T, D), BF16)],
        compiler_params=_cparams(("parallel",)),
    )(o_hm, w_out, x, gate1, nw, scale, shift)


def _ffn_up_dx_resid_bwd(dab, w_gut, x, mixed, dy, gate1, nw, scale, shift):
    T = x.shape[0]

    def epi(dh2, x, mixed, dy, gate1, nw, scale, shift):
        _, vjp = jax.vjp(_resid_norm, x, mixed, gate1, nw, scale, shift)
        return vjp((dy, dh2))

    outs, _ = _matmul_rows(dab, w_gut, epi, [x, mixed, dy], [gate1, nw, scale, shift],
                           [_sds((T, D)), _sds((T, D), BF16)], [_sds((1, D))] * 4, "ffn_up_dx_resid_norm2_bwd")
    return outs


def _ffn_down_loss(act, w_down, x1, target, gate2):
    T = x1.shape[0]

    def epi(ffn, x1, target, gate2):
        y = x1 + gate2 * ffn
        err = y - target
        loss = 0.5 * jnp.sum(jnp.sum(err * err, axis=1, keepdims=True), axis=0, keepdims=True) / D
        dy = err * (1.0 / D)
        return dy, gate2 * dy, jnp.sum(dy * ffn, axis=0, keepdims=True), jnp.broadcast_to(loss, (1, LANE))

    outs, _ = _matmul_rows(act, w_down, epi, [x1, target], [gate2], [_sds((T, D)), _sds((T, D), BF16)],
                           [_sds((1, D)), _sds((1, LANE))], "ffn_down_loss")
    return outs


def _resid_norm(x, mixed, gate1, nw, scale, shift):
    x1 = x + gate1 * mixed
    return x1, _norm_mod(x1, nw, scale, shift)


FFN_BLK = 256
FFN_TM = 2048


def _interleave_gate_up(gate_t, up_t):
    blocks = lambda t: t.reshape(DFF // FFN_BLK, 1, FFN_BLK, D)
    return jnp.concatenate([blocks(gate_t), blocks(up_t)], axis=1).reshape(2 * DFF, D)


def _split_gate_up(g):
    g = g.reshape(DFF // FFN_BLK, 2, FFN_BLK, D)
    return g[:, 0].reshape(DFF, D), g[:, 1].reshape(DFF, D)


def _ffn_up_act(h2, w_gut):
    T = h2.shape[0]
    tm = _tile(T, FFN_TM)

    def body(h_ref, w_ref, ab_ref, act_ref):
        ab = _dot(h_ref[...], w_ref[...], _NT)
        ab_ref[...] = ab
        act_ref[...] = (_silu(ab[:, :FFN_BLK]) * ab[:, FFN_BLK:]).astype(act_ref.dtype)

    return pl.pallas_call(
        body, name="ffn_up_act", grid=(T // tm, DFF // FFN_BLK),
        in_specs=[pl.BlockSpec((tm, D), lambda i, j: (i, 0)), pl.BlockSpec((2 * FFN_BLK, D), lambda i, j: (j, 0))],
        out_specs=[pl.BlockSpec((tm, 2 * FFN_BLK), lambda i, j: (i, j)), pl.BlockSpec((tm, FFN_BLK), lambda i, j: (i, j))],
        out_shape=[_sds((T, 2 * DFF)), _sds((T, DFF), BF16)],
        compiler_params=_cparams(("parallel", "parallel")),
    )(h2, w_gut)


def _ffn_down_dx_act(dffn, w_down, ab):
    T = dffn.shape[0]
    tm = _tile(T, FFN_TM)

    def body(d_ref, w_ref, ab_ref, o_ref):
        dact = _dot(d_ref[...], w_ref[...], _NT)
        a, b = ab_ref[:, :FFN_BLK], ab_ref[:, FFN_BLK:]
        s = _sigmoid(a)
        da = dact * b * (s * (1.0 + a * (1.0 - s)))
        db = dact * (a * s)
        o_ref[...] = jnp.concatenate([da, db], axis=1).astype(o_ref.dtype)

    return pl.pallas_call(
        body, name="ffn_down_dx_act", grid=(T // tm, DFF // FFN_BLK),
        in_specs=[pl.BlockSpec((tm, D), lambda i, j: (i, 0)), pl.BlockSpec((FFN_BLK, D), lambda i, j: (j, 0)),
                  pl.BlockSpec((tm, 2 * FFN_BLK), lambda i, j: (i, j))],
        out_specs=pl.BlockSpec((tm, 2 * FFN_BLK), lambda i, j: (i, j)),
        out_shape=_sds((T, 2 * DFF), BF16),
        compiler_params=_cparams(("parallel", "parallel")),
    )(dffn, w_down, ab)


def _round_bf16(x):
    return x.astype(BF16).astype(F32)


def _shift_down(x, s, rows):
    if s == 0:
        return x
    return jnp.where(rows >= s, pltpu.roll(x, s, 0), 0.0)


def _shift_up(x, s, rows, T):
    if s == 0:
        return x
    return jnp.where(rows < T - s, pltpu.roll(x, T - s, 0), 0.0)


def _conv_fwd(proj, conv_w):
    T = proj.shape[0]
    ncol = 3 * GW // LANE

    def body(x_ref, w_ref, o_ref):
        x = _round_bf16(x_ref[...])
        rows = lax.broadcasted_iota(jnp.int32, x.shape, 0)
        acc = jnp.zeros_like(x)
        for j in range(CONVW):
            acc = acc + _round_bf16(w_ref[pl.ds(j, 1), :]) * _shift_down(x, CONVW - 1 - j, rows)
        o_ref[0], o_ref[1] = _split_pair(_silu(acc))

    return pl.pallas_call(
        body, name="conv_fwd", grid=(ncol,),
        in_specs=[pl.BlockSpec((T, LANE), lambda j: (0, j)), pl.BlockSpec((CONVW, LANE), lambda j: (0, j))],
        out_specs=pl.BlockSpec((2, T, HD), lambda j: (j, 0, 0)),
        out_shape=_sds((3 * GH, T, HD)),
        compiler_params=_cparams(("parallel",)),
    )(proj, conv_w)


def _split_pair(y):
    return y[:, :HD], pltpu.roll(y, HD, 1)[:, :HD]


def _merge_pair(a, b):
    return jnp.concatenate([a, b], axis=1)


def _merge_all(heads):
    return jnp.concatenate([_merge_pair(heads[2 * p], heads[2 * p + 1]) for p in range(heads.shape[0] // 2)], axis=1)


def _matmul_nt_heads(a, b, name):
    T, K = a.shape
    N = b.shape[0]
    tm = _tile(T, 1024)

    def body(a_ref, b_ref, o_ref):
        res = _dot(a_ref[...], b_ref[...], _NT)
        for p in range(N // LANE):
            o_ref[2 * p], o_ref[2 * p + 1] = _split_pair(res[:, p * LANE:(p + 1) * LANE])

    return pl.pallas_call(
        body, name=name, grid=(T // tm,),
        in_specs=[pl.BlockSpec((tm, K), lambda i: (i, 0)), pl.BlockSpec((N, K), lambda i: (0, 0))],
        out_specs=pl.BlockSpec((N // HD, tm, HD), lambda i: (0, i, 0)),
        out_shape=_sds((N // HD, T, HD)),
        compiler_params=_cparams(("parallel",)),
    )(a, b)


def _conv_bwd(proj, conv_w, dqc, dproj):
    T = proj.shape[0]
    ncol = 3 * GW // LANE

    def body(x_ref, w_ref, d_ref, buf_ref, dx_ref, dw_ref):
        x = _round_bf16(x_ref[...])
        rows = lax.broadcasted_iota(jnp.int32, x.shape, 0)
        xs = [_shift_down(x, CONVW - 1 - j, rows) for j in range(CONVW)]
        w = [_round_bf16(w_ref[pl.ds(j, 1), :]) for j in range(CONVW)]
        pre = jnp.zeros_like(x)
        for j in range(CONVW):
            pre = pre + w[j] * xs[j]
        s = _sigmoid(pre)
        dpre = _round_bf16(_merge_pair(d_ref[0], d_ref[1]) * (s * (1.0 + pre * (1.0 - s))))
        dx = jnp.zeros_like(x)
        for j in range(CONVW):
            dx = dx + w[j] * _shift_up(dpre, CONVW - 1 - j, rows, T)
            dw_ref[pl.ds(j, 1), :] = jnp.sum(dpre * xs[j], axis=0, keepdims=True)
        dx_ref[...] = dx.astype(dx_ref.dtype)

    return pl.pallas_call(
        body, name="conv_bwd", grid=(ncol,),
        in_specs=[pl.BlockSpec((T, LANE), lambda j: (0, j)), pl.BlockSpec((CONVW, LANE), lambda j: (0, j)),
                  pl.BlockSpec((2, T, HD), lambda j: (j, 0, 0))] + _hbm_specs(1),
        out_specs=[pl.BlockSpec((T, LANE), lambda j: (0, j)), pl.BlockSpec((CONVW, LANE), lambda j: (0, j))],
        out_shape=[_sds(dproj.shape, dproj.dtype), _sds((CONVW, 3 * GW))],
        input_output_aliases={3: 0},
        compiler_params=_cparams(("parallel",)),
    )(proj, conv_w, dqc, dproj)


def _gdn_prep(kit, q, k, v, ga, gb, alog, dtb, t_inv=None):
    C = CHUNK
    ri = lax.broadcasted_iota(jnp.int32, (C, C), 0)
    ci = lax.broadcasted_iota(jnp.int32, (C, C), 1)
    causal = ri >= ci
    strict = ri > ci
    eye = (ri == ci).astype(F32)
    lower = causal.astype(F32)
    upper = (ri <= ci).astype(F32)

    a = ga + dtb
    softplus = jnp.maximum(a, 0.0) + jnp.log(1.0 + jnp.exp(-jnp.abs(a)))
    g_row = -jnp.exp(alog) * softplus
    beta_row = _sigmoid(gb)
    g_col = jnp.sum(eye * g_row, axis=2, keepdims=True)
    beta_col = jnp.sum(eye * beta_row, axis=2, keepdims=True)
    G_col = jnp.sum(lower * g_row, axis=2, keepdims=True)
    G_row = jnp.sum(upper * g_col, axis=1, keepdims=True)
    G_last = jnp.sum(g_row, axis=2, keepdims=True)
    decay = jnp.exp(jnp.where(causal, G_col - G_row, -1e30))

    qn = q * lax.rsqrt(jnp.sum(q * q, axis=-1, keepdims=True) + EPS) * (HD ** -0.5)
    kn = k * lax.rsqrt(jnp.sum(k * k, axis=-1, keepdims=True) + EPS)
    kb = kn * beta_col
    A = jnp.where(strict, kit.nt(kb, kn) * decay, 0.0)
    Tm = kit.inv(A, t_inv)
    eG = jnp.exp(G_col)
    u = kit.nn3(Tm, v * beta_col)
    w = kit.nn3(Tm, kb * eG)
    qk = jnp.where(causal, kit.nt(qn, kn) * decay, 0.0)
    q_dec = qn * eG
    k_dec = kn * jnp.exp(G_last - G_col)
    dec = jnp.exp(G_last)
    return u, w, qk, q_dec, k_dec, dec, Tm


def _gdn_out(o, z, nw):
    return _rms(o, nw) * _silu(z)


GDN_CB = 4


def _gdn_specs(T, blk):
    TB = GDN_CB * CHUNK
    seq = lambda grp: pl.BlockSpec((GH, TB, HD), lambda i, grp=grp: (grp, blk(i), 0))
    row = lambda grp: pl.BlockSpec((GH, GDN_CB, 1, CHUNK), lambda i, grp=grp: (grp, blk(i), 0, 0))
    per_head = pl.BlockSpec((GH, 1, CHUNK), lambda i: (0, 0, 0))
    whole = pl.BlockSpec((1, HD), lambda i: (0, 0))
    state = pl.BlockSpec((GH, GDN_CB, HD, HD), lambda i: (0, blk(i), 0, 0))
    return seq, row, per_head, whole, state


def _gdn_load(seq_refs, row_refs, head_refs):
    chunks = lambda r: jnp.concatenate([r[:, pl.ds(cb * CHUNK, CHUNK), :] for cb in range(GDN_CB)], axis=0)
    rows = lambda r: jnp.concatenate([r[:, cb] for cb in range(GDN_CB)], axis=0)
    heads = lambda r: jnp.concatenate([r[...]] * GDN_CB, axis=0)
    return [chunks(r) for r in seq_refs], [rows(r) for r in row_refs], [heads(r) for r in head_refs]


def _gdn_fwd(qkv_hm, zs_hm, gab, alog_b, dtb_b, nw, shards):
    T = qkv_hm.shape[1]
    N = T // CHUNK
    nblk = N // GDN_CB
    ns = len(shards)
    seq, row, per_head, whole, state = _gdn_specs(T, lambda i: i)
    kit = _Kit(False)

    def body(*refs):
        q_ref, k_ref, v_ref, z_ref, ga_ref, gb_ref, al_ref, dt_ref, nw_ref = refs[:9]
        o_ref, S_ref, T_ref = refs[9 + ns:12 + ns]
        S_scr = refs[12 + 2 * ns]
        plan = _gather_plan(refs[9:9 + ns], refs[12 + ns:12 + 2 * ns], *refs[13 + 2 * ns:])

        @pl.when(pl.program_id(0) == 0)
        def _():
            S_scr[...] = jnp.zeros_like(S_scr)
            _start(plan)

        (q, k, v, z), (ga, gb), (al, dt) = _gdn_load((q_ref, k_ref, v_ref, z_ref), (ga_ref, gb_ref), (al_ref, dt_ref))
        u, w, qk, q_dec, k_dec, dec, t_inv = _gdn_prep(kit, q, k, v, ga, gb, al, dt)
        S = S_scr[...]
        for cb in range(GDN_CB):
            hs = slice(cb * GH, (cb + 1) * GH)
            S_ref[:, cb] = S
            T_ref[:, cb] = t_inv[hs]
            v_new = u[hs] - kit.nn(w[hs], S)
            o = kit.nn(q_dec[hs], S) + kit.nn(qk[hs], v_new)
            S = S * dec[hs] + kit.tn(k_dec[hs], v_new)
            o_ref[:, pl.ds(cb * CHUNK, CHUNK), :] = _gdn_out(o, z[hs], nw_ref[...])
        S_scr[...] = S

        @pl.when(pl.program_id(0) == nblk - 1)
        def _():
            _finish(plan)

    res = pl.pallas_call(
        body, name="gdn_fwd", grid=(nblk,),
        in_specs=[seq(0), seq(1), seq(2), seq(0), row(0), row(1), per_head, per_head, whole] + _hbm_specs(ns),
        out_specs=[seq(0), state, state] + _hbm_specs(ns),
        out_shape=[_sds((GH + SQH, T, HD)), _sds((GH, N, HD, HD)), _sds((GH, N, CHUNK, CHUNK))]
                  + _gather_shapes(shards),
        scratch_shapes=[pltpu.VMEM((GH, HD, HD), F32)] + _gather_sems(ns),
        compiler_params=_cparams(("arbitrary",)),
    )(qkv_hm, qkv_hm, qkv_hm, zs_hm, gab, gab, alog_b, dtb_b, nw, *shards)
    return res[0], (res[1], res[2]), res[3:]


def _gdn_bwd(qkv_hm, zs_hm, gab, alog_b, dtb_b, nw, S_all, do, pieces):
    T = qkv_hm.shape[1]
    N = T // CHUNK
    nblk = N // GDN_CB
    npc = len(pieces)
    dkit, kit = _Kit(True), _Kit(False)
    rseq, rrow, per_head, whole, rstate = _gdn_specs(T, lambda i: nblk - 1 - i)

    def body(*refs):
        q_ref, k_ref, v_ref, z_ref, ga_ref, gb_ref, al_ref, dt_ref, nw_ref, S_ref, T_ref, do_ref = refs[:12]
        dqkv_ref, dz_ref, dga_ref, dgb_ref, dal_ref, ddt_ref, dnw_ref = refs[12 + npc:19 + npc]
        dS_scr = refs[19 + 2 * npc]
        plan = _exchange_plan(refs[12:12 + npc], refs[19 + npc:19 + 2 * npc], *refs[20 + 2 * npc:])

        @pl.when(pl.program_id(0) == 0)
        def _():
            dS_scr[...] = jnp.zeros_like(dS_scr)
            dal_ref[...] = jnp.zeros_like(dal_ref)
            ddt_ref[...] = jnp.zeros_like(ddt_ref)
            dnw_ref[...] = jnp.zeros_like(dnw_ref)
            _start(plan)

        (q, k, v, z, dout), (ga, gb), (al, dt) = _gdn_load((q_ref, k_ref, v_ref, z_ref, do_ref), (ga_ref, gb_ref),
                                                          (al_ref, dt_ref))
        S_in = jnp.concatenate([S_ref[:, cb] for cb in range(GDN_CB)], axis=0)
        t_inv = jnp.concatenate([T_ref[:, cb] for cb in range(GDN_CB)], axis=0)
        prep = lambda *a: _gdn_prep(dkit, *a, t_inv=t_inv)[:6]
        (u, w, qk, q_dec, k_dec, dec), prep_vjp = jax.vjp(prep, q, k, v, ga, gb, al, dt)
        v_new = u - kit.nn(w, S_in)
        o = kit.nn(q_dec, S_in) + kit.nn(qk, v_new)
        _, out_vjp = jax.vjp(_gdn_out, o, z, nw_ref[...])
        do, dz, dnw = out_vjp(dout)
        dvn_part = kit.tn(qk, do)
        dS_part = kit.tn(q_dec, do)
        dS = dS_scr[...]
        dS_out, dvn = [None] * GDN_CB, [None] * GDN_CB
        for cb in reversed(range(GDN_CB)):
            hs = slice(cb * GH, (cb + 1) * GH)
            dS_out[cb] = dS
            dvn[cb] = dvn_part[hs] + kit.nn(k_dec[hs], dS)
            dS = dS * dec[hs] + dS_part[hs] - kit.tn(w[hs], dvn[cb])
        dS_scr[...] = dS
        dS_out = jnp.concatenate(dS_out, axis=0)
        dvn = jnp.concatenate(dvn, axis=0)
        ddec = jnp.sum(jnp.sum(S_in * dS_out, axis=2, keepdims=True), axis=1, keepdims=True)
        cts = (dvn, -kit.nt(dvn, S_in), kit.nt(do, v_new), kit.nt(do, S_in), kit.nt(v_new, dS_out), ddec)
        dq, dk, dv, dga, dgb, dal, ddt = prep_vjp(cts)
        lanesum = lambda t: jnp.broadcast_to(jnp.sum(t, axis=2, keepdims=True), t.shape)
        for cb in range(GDN_CB):
            hs = slice(cb * GH, (cb + 1) * GH)
            sl = pl.ds(cb * CHUNK, CHUNK)
            dqkv_ref[pl.ds(0, GH), sl, :] = dq[hs]
            dqkv_ref[pl.ds(GH, GH), sl, :] = dk[hs]
            dqkv_ref[pl.ds(2 * GH, GH), sl, :] = dv[hs]
            dz_ref[sl, :] = _merge_all(dz[hs]).astype(BF16)
            dga_ref[:, cb] = dga[hs]
            dgb_ref[:, cb] = dgb[hs]
            dal_ref[...] += lanesum(dal[hs])
            ddt_ref[...] += lanesum(ddt[hs])
        dnw_ref[...] += dnw

        @pl.when(pl.program_id(0) == nblk - 1)
        def _():
            _finish(plan)

    res = pl.pallas_call(
        body, name="gdn_bwd", grid=(nblk,),
        in_specs=[rseq(0), rseq(1), rseq(2), rseq(0), rrow(0), rrow(1), per_head, per_head, whole, rstate, rstate,
                  rseq(0)] + _hbm_specs(npc),
        out_specs=[pl.BlockSpec((3 * GH, GDN_CB * CHUNK, HD), lambda i: (0, nblk - 1 - i, 0)),
                   pl.BlockSpec((GDN_CB * CHUNK, GW), lambda i: (nblk - 1 - i, 3)), rrow(0),
                   rrow(0), per_head, per_head, whole] + _hbm_specs(npc),
        out_shape=[_sds((3 * GH, T, HD)), _sds((T, NP), BF16)] + [_sds((GH, N, 1, CHUNK))] * 2
                  + [_sds((GH, 1, CHUNK))] * 2 + [_sds((1, HD))] + _exchange_shapes(pieces),
        scratch_shapes=[pltpu.VMEM((GH, HD, HD), F32)] + _exchange_sems(npc),
        compiler_params=_cparams(("arbitrary",), GDN_BWD_VMEM),
    )(qkv_hm, qkv_hm, qkv_hm, zs_hm, gab, gab, alog_b, dtb_b, nw, S_all[0], S_all[1], do, *pieces)
    return res[:7], res[7:]


def _swa_heads(kit, first, q, kp, kc, vp, vc, qnw, knw, sink, slope):
    W = WIN
    ri = lax.broadcasted_iota(jnp.int32, (W, W), 0)
    ci = lax.broadcasted_iota(jnp.int32, (W, W), 1)
    mask_c = ri >= ci
    mask_p = ci > ri + first * W
    dist_c = (ri - ci).astype(F32)
    dist_p = (ri - ci + W).astype(F32)
    kpn = _rms(kp, knw)
    kcn = _rms(kc, knw)
    qn = _rms(q, qnw)
    sc = jnp.where(mask_c, kit.nt(qn, kcn) * (HD ** -0.5) - slope * dist_c, -1e30)
    sp = jnp.where(mask_p, kit.nt(qn, kpn) * (HD ** -0.5) - slope * dist_p, -1e30)
    m = jnp.maximum(jnp.maximum(jnp.max(sc, axis=-1, keepdims=True), jnp.max(sp, axis=-1, keepdims=True)), sink)
    m = lax.stop_gradient(m)
    pc = jnp.exp(sc - m)
    pp = jnp.exp(sp - m)
    den = jnp.sum(pc, axis=-1, keepdims=True) + jnp.sum(pp, axis=-1, keepdims=True) + jnp.exp(sink - m)
    inv = 1.0 / den
    return kit.nn(pc * inv, vc) + kit.nn(pp * inv, vp)


def _swa_grads(kit, first, q, kp, kc, vp, vc, qnw, knw, sink, slope, do):
    W = WIN
    ri = lax.broadcasted_iota(jnp.int32, (W, W), 0)
    ci = lax.broadcasted_iota(jnp.int32, (W, W), 1)
    mask_c = ri >= ci
    mask_p = ci > ri + first * W
    dist_c = (ri - ci).astype(F32)
    dist_p = (ri - ci + W).astype(F32)
    scale = HD ** -0.5
    kpn, kp_vjp = jax.vjp(_rms, kp, knw)
    kcn, kc_vjp = jax.vjp(_rms, kc, knw)
    qn, q_vjp = jax.vjp(_rms, q, qnw)
    sc = jnp.where(mask_c, kit.nt(qn, kcn) * scale - slope * dist_c, -1e30)
    sp = jnp.where(mask_p, kit.nt(qn, kpn) * scale - slope * dist_p, -1e30)
    m = jnp.maximum(jnp.maximum(jnp.max(sc, axis=-1, keepdims=True), jnp.max(sp, axis=-1, keepdims=True)), sink)
    ec = jnp.exp(sc - m)
    ep = jnp.exp(sp - m)
    es = jnp.exp(sink - m)
    inv = 1.0 / (jnp.sum(ec, axis=-1, keepdims=True) + jnp.sum(ep, axis=-1, keepdims=True) + es)
    pc, pp = ec * inv, ep * inv
    dpc, dpp = kit.nt(do, vc), kit.nt(do, vp)
    delta = jnp.sum(dpc * pc, axis=-1, keepdims=True) + jnp.sum(dpp * pp, axis=-1, keepdims=True)
    dsc = pc * (dpc - delta) * scale
    dsp = pp * (dpp - delta) * scale
    dq, dqnw = q_vjp(kit.nn(dsc, kcn) + kit.nn(dsp, kpn))
    dkc, dknw_c = kc_vjp(kit.tn(dsc, qn))
    dkp, dknw_p = kp_vjp(kit.tn(dsp, qn))
    return dq, dkp, dkc, kit.tn(pp, do), kit.tn(pc, do), dqnw, dknw_c + dknw_p, -(es * inv) * delta


def _per_query_head(kv_ref):
    return jnp.concatenate([kv_ref[pl.ds(h // SGRP, 1)] for h in range(SQH)], axis=0)


def _per_kv_head(d):
    return jnp.concatenate([jnp.sum(d[g * SGRP:(g + 1) * SGRP], axis=0, keepdims=True) for g in range(SKVH)], axis=0)


def _swa_specs(blk):
    qspec = pl.BlockSpec((SQH, WIN, HD), lambda i: (1, blk(i), 0))
    cur = lambda grp: pl.BlockSpec((SKVH, WIN, HD), lambda i, grp=grp: (grp, blk(i), 0))
    prev = lambda grp: pl.BlockSpec((SKVH, WIN, HD), lambda i, grp=grp: (grp, jnp.maximum(blk(i) - 1, 0), 0))
    whole = pl.BlockSpec((1, HD), lambda i: (0, 0))
    col = pl.BlockSpec((SQH, WIN, 1), lambda i: (0, 0, 0))
    ospec = pl.BlockSpec((SQH, WIN, HD), lambda i: (0, blk(i), 0))
    return qspec, cur, prev, whole, col, ospec


def _swa_fwd(zs_hm, qnw, knw, sinks_col, slopes_col, o_buf, shards):
    T = zs_hm.shape[1]
    NB = T // WIN
    ns = len(shards)
    kit = _Kit(False)
    qspec, cur, prev, whole, col, _ = _swa_specs(lambda i: i)

    def body(*refs):
        q_ref, kp_ref, kc_ref, vp_ref, vc_ref, qnw_ref, knw_ref, s_ref, sl_ref = refs[:9]
        o_ref = refs[10 + ns]
        plan = _gather_plan(refs[10:10 + ns], refs[11 + ns:11 + 2 * ns], *refs[11 + 2 * ns:])

        @pl.when(pl.program_id(0) == 0)
        def _():
            _start(plan)

        first = (pl.program_id(0) == 0).astype(jnp.int32)
        o_ref[...] = _swa_heads(kit, first, q_ref[...], _per_query_head(kp_ref), _per_query_head(kc_ref),
                                _per_query_head(vp_ref), _per_query_head(vc_ref), qnw_ref[...], knw_ref[...],
                                s_ref[...], sl_ref[...])

        @pl.when(pl.program_id(0) == NB - 1)
        def _():
            _finish(plan)

    res = pl.pallas_call(
        body, name="swa_fwd", grid=(NB,),
        in_specs=[qspec, prev(8), cur(8), prev(9), cur(9), whole, whole, col, col] + _hbm_specs(1 + ns),
        out_specs=[pl.BlockSpec((SQH, WIN, HD), lambda i: (1, i, 0))] + _hbm_specs(ns),
        out_shape=[_sds(o_buf.shape)] + _gather_shapes(shards),
        input_output_aliases={9: 0},
        scratch_shapes=_gather_sems(ns),
        compiler_params=_cparams(("arbitrary",)),
    )(zs_hm, zs_hm, zs_hm, zs_hm, zs_hm, qnw, knw, sinks_col, slopes_col, o_buf, *shards)
    return res[0], res[1:]


def _swa_bwd(zs_hm, qnw, knw, sinks_col, slopes_col, dmix_hm, dproj):
    T = zs_hm.shape[1]
    NB = T // WIN
    kit = _Kit(False)
    qspec, cur, prev, whole, col, _ = _swa_specs(lambda i: NB - 1 - i)
    tail = NP - 4 * GW
    used = (SQH + 2 * SKVH) * HD

    def body(q_ref, kp_ref, kc_ref, vp_ref, vc_ref, qnw_ref, knw_ref, s_ref, sl_ref, do_ref, buf_ref,
             d_ref, dqnw_ref, dknw_ref, ds_ref, ck_scr, cv_scr):
        i = pl.program_id(0)
        first = (i == NB - 1).astype(jnp.int32)

        @pl.when(i == 0)
        def _():
            ck_scr[...] = jnp.zeros_like(ck_scr)
            cv_scr[...] = jnp.zeros_like(cv_scr)
            ds_ref[...] = jnp.zeros_like(ds_ref)
            dqnw_ref[...] = jnp.zeros_like(dqnw_ref)
            dknw_ref[...] = jnp.zeros_like(dknw_ref)

        dq, dkp, dkc, dvp, dvc, dqnw, dknw, dsink = _swa_grads(
            kit, first, q_ref[...], _per_query_head(kp_ref), _per_query_head(kc_ref), _per_query_head(vp_ref),
            _per_query_head(vc_ref), qnw_ref[...], knw_ref[...], s_ref[...], sl_ref[...], do_ref[...])
        dk = _per_kv_head(dkc) + ck_scr[...]
        dv = _per_kv_head(dvc) + cv_scr[...]
        d_ref[...] = jnp.concatenate([_merge_all(dq), _merge_all(dk), _merge_all(dv),
                                      jnp.zeros((WIN, tail - used), F32)], axis=1).astype(BF16)
        ck_scr[...] = _per_kv_head(dkp)
        cv_scr[...] = _per_kv_head(dvp)
        dqnw_ref[...] += dqnw
        dknw_ref[...] += dknw
        ds_ref[...] += jnp.broadcast_to(jnp.sum(dsink, axis=1, keepdims=True), dsink.shape)

    dospec = pl.BlockSpec((SQH, WIN, HD), lambda i: (1, NB - 1 - i, 0))
    dspec = pl.BlockSpec((WIN, tail), lambda i: (NB - 1 - i, 4 * GW // tail))
    res = pl.pallas_call(
        body, name="swa_bwd", grid=(NB,),
        in_specs=[qspec, prev(8), cur(8), prev(9), cur(9), whole, whole, col, col, dospec] + _hbm_specs(1),
        out_specs=[dspec, whole, whole, col],
        out_shape=[_sds(dproj.shape, dproj.dtype), _sds((1, HD)), _sds((1, HD)), _sds((SQH, WIN, 1))],
        input_output_aliases={10: 0},
        scratch_shapes=[pltpu.VMEM((SKVH, WIN, HD), F32), pltpu.VMEM((SKVH, WIN, HD), F32)],
        compiler_params=_cparams(("arbitrary",)),
    )(zs_hm, zs_hm, zs_hm, zs_hm, zs_hm, qnw, knw, sinks_col, slopes_col, dmix_hm, dproj)
    return res


GAB0 = 3 * GW + 1280


W_IN_ROWS = PROJ // N_CHIP
W_IN_ROWS_PAD = 736


def _shard_rows(w_sh, lo, hi):
    out = []
    for j in range(N_CHIP):
        a, b = max(lo, j * W_IN_ROWS), min(hi, (j + 1) * W_IN_ROWS)
        if a < b:
            out.append(w_sh[j, a - j * W_IN_ROWS:b - j * W_IN_ROWS])
    return out


def _permute_w_in_t(w_sh):
    gab = 4 * GW + 2 * GH
    return jnp.concatenate(_shard_rows(w_sh, 0, 4 * GW) + _shard_rows(w_sh, gab, PROJ) + _shard_rows(w_sh, 4 * GW, gab)
                           + [jnp.zeros((NP - PROJ, D), w_sh.dtype)], axis=0)


def _w_in_grad_pieces(g_t):
    g = jnp.concatenate([g_t[:4 * GW], g_t[GAB0:GAB0 + 2 * GH], g_t[4 * GW:GAB0]], axis=0)
    pad = ((0, W_IN_ROWS_PAD - W_IN_ROWS), (0, 0))
    g = jnp.stack([jnp.pad(g[j * W_IN_ROWS:(j + 1) * W_IN_ROWS], pad) for j in range(N_CHIP)])
    return g.reshape(N_CHIP, 2, W_IN_ROWS_PAD // 2, D)


def _pieces_by_rows(g):
    return g.reshape(N_CHIP, 2, g.shape[0] // (2 * N_CHIP), D)


def _local_step(x, target, mod, n1w, w_in_pt, conv_w, alog, dtb, gnw, qnw, knw, sinks, n2w, shards):
    sh_out, sh_gate, sh_up, sh_down = shards
    T = x.shape[0]
    N = T // CHUNK
    shift1, scale1, gate1, shift2, scale2, gate2 = [mod[:, i * D:(i + 1) * D] for i in range(6)]

    h, proj, zs_hm, (a_out,) = _norm_in_proj(x, n1w, scale1, shift1, w_in_pt, [sh_out])
    w_out = a_out.reshape(D, D)
    qkv_hm = _conv_fwd(proj, conv_w)
    gab = proj[:, GAB0:GAB0 + 2 * GH].T.reshape(2 * GH, N, 1, CHUNK)
    alog_b = jnp.broadcast_to(alog.reshape(GH, 1, 1), (GH, 1, CHUNK))
    dtb_b = jnp.broadcast_to(dtb.reshape(GH, 1, 1), (GH, 1, CHUNK))
    sinks_col = jnp.broadcast_to(sinks.reshape(SQH, 1, 1), (SQH, WIN, 1))
    o_hm, S_all, (a_gate, a_up) = _gdn_fwd(qkv_hm, zs_hm, gab, alog_b, dtb_b, gnw, [sh_gate, sh_up])
    w_gut = _interleave_gate_up(a_gate.reshape(DFF, D), a_up.reshape(DFF, D))
    slopes = 2.0 ** (-8.0 * (jnp.arange(SQH, dtype=F32) + 1.0) / SQH)
    slopes_col = jnp.broadcast_to(slopes.reshape(SQH, 1, 1), (SQH, WIN, 1))
    o_hm, (a_down,) = _swa_fwd(zs_hm, qnw, knw, sinks_col, slopes_col, o_hm, [sh_down])
    w_down = a_down.reshape(DFF, D)
    mixcat, mixed, x1, h2 = _out_proj_resid_norm(o_hm, w_out, x, gate1, n2w, scale2, shift2)
    ab, act = _ffn_up_act(h2, w_gut)
    dy, dffn, dgate2, loss = _ffn_down_loss(act, w_down, x1, target, gate2)

    dab = _ffn_down_dx_act(dffn, w_down, ab)
    g_w_down = _matmul(act, dffn, ta=True, out_dtype=BF16, name="ffn_down_dw")
    g_w_gut = _matmul(dab, h2, ta=True, out_dtype=BF16, name="ffn_up_dw")
    dx1, dmixed, dgate1, dn2w, dscale2, dshift2 = _ffn_up_dx_resid_bwd(dab, w_gut, x, mixed, dy, gate1, n2w, scale2,
                                                                       shift2)
    g_w_out = _matmul(mixcat, dmixed, ta=True, out_dtype=BF16, name="out_proj_dw")
    dmix_hm = _matmul_nt_heads(dmixed, w_out, "out_proj_dx")
    g_gate_t, g_up_t = _split_gate_up(g_w_gut)
    pieces = [_pieces_by_rows(g_w_out), _pieces_by_rows(g_gate_t), _pieces_by_rows(g_up_t),
              _pieces_by_rows(g_w_down)]
    (dqkv_hm, dproj, dga, dgb, dalog, ddtb, dgnw), recv = _gdn_bwd(qkv_hm, zs_hm, gab, alog_b, dtb_b, gnw, S_all,
                                                                   dmix_hm, pieces)
    dproj, dqnw, dknw, dsinks = _swa_bwd(zs_hm, qnw, knw, sinks_col, slopes_col, dmix_hm, dproj)
    dproj, dconv = _conv_bwd(proj, conv_w, dqkv_hm, dproj)
    dgab = jnp.concatenate([dga, dgb], axis=0).reshape(2 * GH, T).T.astype(BF16)
    dproj = lax.dynamic_update_slice(dproj, jnp.concatenate([dgab, jnp.zeros((T, NP - PROJ), BF16)], axis=1),
                                     (0, GAB0))
    g_w_in_pt = _matmul(dproj, h, ta=True, out_dtype=BF16, name="in_proj_dw")
    (grad_x, dn1w, dscale1, dshift1), recv_in = _in_proj_dx_norm_bwd(dproj, w_in_pt, x, dx1, n1w, scale1, shift1,
                                                                     [_w_in_grad_pieces(g_w_in_pt)])

    dmod = jnp.concatenate([dshift1, dscale1, dgate1, dshift2, dscale2, dgate2], axis=1)
    big = list(recv_in) + list(recv)
    small = dict(mod=dmod, norm1_w=dn1w, norm2_w=dn2w, conv_w=dconv, a_log=dalog[:, 0, 0], dt_bias=ddtb[:, 0, 0],
                 gdn_norm_w=dgnw, q_norm_w=dqnw, k_norm_w=dknw, sinks=dsinks[:, 0, 0])
    return loss, grad_x, big, small


def _adamw(w, g, m, v):
    m2 = ADAM_B1 * m + (1.0 - ADAM_B1) * g
    v2 = ADAM_B2 * v + (1.0 - ADAM_B2) * (g * g)
    m_hat = m2 / (1.0 - ADAM_B1 ** ADAM_STEP)
    v_hat = v2 / (1.0 - ADAM_B2 ** ADAM_STEP)
    delta = -ADAM_LR * (m_hat / (jnp.sqrt(v_hat) + ADAM_EPS) + ADAM_WD * w)
    return delta, m2, v2


def _reduce_adamw(recv, w, m, v, name):
    _, R, C = recv.shape
    tc = _tile(C, 256)

    def body(r_ref, w_ref, m_ref, v_ref, o_ref):
        g = r_ref[0].astype(F32)
        for s in range(1, N_DEV):
            g = g + r_ref[s].astype(F32)
        delta, m2, v2 = _adamw(w_ref[...], g, m_ref[...], v_ref[...])
        o_ref[0] = g
        o_ref[1] = delta
        o_ref[2] = m2
        o_ref[3] = v2

    col = pl.BlockSpec((R, tc), lambda j: (0, j))
    return pl.pallas_call(
        body, name=name, grid=(C // tc,),
        in_specs=[pl.BlockSpec((N_DEV, R, tc), lambda j: (0, 0, j)), col, col, col],
        out_specs=pl.BlockSpec((4, R, tc), lambda j: (0, 0, j)),
        out_shape=_sds((4, R, C)),
        compiler_params=_cparams(("parallel",)),
    )(recv, w, m, v)


def _adamw_call(g, w, m, v, name):
    def body(g_ref, w_ref, m_ref, v_ref, o_ref):
        delta, m2, v2 = _adamw(w_ref[...], g_ref[...], m_ref[...], v_ref[...])
        o_ref[0] = delta
        o_ref[1] = m2
        o_ref[2] = v2

    return pl.pallas_call(body, name=name, out_shape=_sds((3,) + g.shape))(g, w, m, v)


ADA_N = 6 * D // N_CHIP
KPAD = 128


def _w_ada_update(c8p, dm, w, m, v):
    tr = 256

    def body(c_ref, dm_ref, w_ref, m_ref, v_ref, g_ref, d_ref, m2_ref, v2_ref):
        g = _raw1(_silu(c_ref[...]), dm_ref[...], _TN)
        delta, m2, v2 = _adamw(w_ref[...], g, m_ref[...], v_ref[...])
        g_ref[...] = g
        d_ref[...] = delta
        m2_ref[...] = m2
        v2_ref[...] = v2

    blk = pl.BlockSpec((tr, ADA_N), lambda i: (i, 0))
    return pl.pallas_call(
        body, name="w_ada_update", grid=(D // tr,),
        in_specs=[pl.BlockSpec((KPAD, tr), lambda i: (0, i)), pl.BlockSpec((KPAD, ADA_N), lambda i: (0, 0)),
                  blk, blk, blk],
        out_specs=[blk] * 4, out_shape=[_sds((D, ADA_N))] * 4,
        compiler_params=_cparams(("parallel",)),
    )(c8p, dm, w, m, v)


def _me():
    return lax.axis_index("x"), lax.axis_index("y"), lax.axis_index("c")


def _peer(k, me):
    mx, my, mc = me
    return (1 - mx if k & 4 else mx, 1 - my if k & 2 else my, 1 - mc if k & 1 else mc)


def _lin(p):
    return 4 * p[0] + 2 * p[1] + p[2]


def _remote(src, dst, ssem, rsem, dev):
    return pltpu.make_async_remote_copy(src_ref=src, dst_ref=dst, send_sem=ssem, recv_sem=rsem,
                                        device_id=dev, device_id_type=MESH)


def _all_gather8(x, name):
    def body(x_ref, out_ref, send_sems, recv_sems):
        me = _me()
        out_ref[_lin(me)] = x_ref[...]
        sends = []
        for k in range(1, N_DEV):
            cp = _remote(x_ref, out_ref.at[_lin(me)], send_sems.at[k - 1], recv_sems.at[k - 1], _peer(k, me))
            cp.start()
            sends.append(cp)
        for k in range(1, N_DEV):
            p = _peer(k, me)
            _remote(x_ref, out_ref.at[_lin(p)], send_sems.at[k - 1], recv_sems.at[k - 1], p).wait_recv()
        for cp in sends:
            cp.wait_send()

    return pl.pallas_call(
        body, name=name,
        out_shape=_sds((N_DEV,) + x.shape, x.dtype),
        in_specs=[pl.BlockSpec(memory_space=pltpu.VMEM)],
        out_specs=pl.BlockSpec(memory_space=pltpu.VMEM),
        scratch_shapes=[pltpu.SemaphoreType.DMA((N_DEV - 1,)), pltpu.SemaphoreType.DMA((N_DEV - 1,))],
    )(x)


def _ag8_plan(src, out, send_sems, recv_sems):
    me = _me()
    sends, recvs = [], []
    for k in range(1, N_DEV):
        p = _peer(k, me)
        sends.append(_remote(src, out.at[_lin(me)], send_sems.at[k - 1], recv_sems.at[k - 1], p))
        recvs.append(_remote(src, out.at[_lin(p)], send_sems.at[k - 1], recv_sems.at[k - 1], p))
    return [], sends, recvs


def _prologue(c_row, conv_sh, w_ada, b_sh, w_in_sh):
    def body(c_ref, cv_ref, wa_ref, b_ref, win_ref, call_ref, cvall_ref, mods_ref, ain_ref, c16_scr, mp_scr,
             c_send, c_recv, cv_send, cv_recv, m_send, m_recv, w_send, w_recv, w_local):
        me = _lin(_me())
        w_plan = _gather_half_plan([win_ref], [ain_ref], w_send, w_recv, w_local)
        _start(w_plan)
        c_plan = _ag8_plan(c_ref, call_ref, c_send, c_recv)
        cv_plan = _ag8_plan(cv_ref, cvall_ref, cv_send, cv_recv)
        call_ref[me] = c_ref[...]
        cvall_ref[me] = cv_ref[...]
        _start(c_plan)
        _start(cv_plan)
        _finish(c_plan)
        c16_scr[...] = jnp.zeros_like(c16_scr)
        for d in range(N_DEV):
            c16_scr[pl.ds(d, 1), :] = call_ref[d]
        mp_scr[...] = _raw1(_silu(c16_scr[...]), wa_ref[...], _NN) + b_ref[...]
        mods_ref[me] = mp_scr[...]
        m_plan = _ag8_plan(mp_scr, mods_ref, m_send, m_recv)
        _start(m_plan)
        _finish(cv_plan)
        _finish(m_plan)
        _finish(w_plan)

    vmem = pl.BlockSpec(memory_space=pltpu.VMEM)
    sems = lambda n: pltpu.SemaphoreType.DMA((n,))
    return pl.pallas_call(
        body, name="prologue",
        in_specs=[vmem] * 4 + _hbm_specs(1), out_specs=[vmem] * 3 + _hbm_specs(1),
        out_shape=[_sds((N_DEV,) + c_row.shape), _sds((N_DEV,) + conv_sh.shape), _sds((N_DEV, 16, ADA_N)),
                   _sds((N_CHIP,) + w_in_sh.shape, w_in_sh.dtype)],
        scratch_shapes=[pltpu.VMEM((16, D), F32), pltpu.VMEM((16, ADA_N), F32)] + [sems(N_DEV - 1)] * 6
                       + _gather_sems(1),
        compiler_params=_cparams(),
    )(c_row, conv_sh, w_ada, b_sh, w_in_sh)


def _hbm_specs(n):
    return [pl.BlockSpec(memory_space=pl.ANY)] * n


def _gather_shapes(shards):
    return [_sds((N_CHIP,) + s.shape, s.dtype) for s in shards]


def _gather_sems(n):
    return [pltpu.SemaphoreType.DMA((3 * n,)), pltpu.SemaphoreType.DMA((3 * n,)), pltpu.SemaphoreType.DMA((n,))]


def _gather_plan(ins, outs, send_sems, recv_sems, local_sems):
    mx, my, mc = _me()
    chips = [(1 - mx, my), (mx, 1 - my), (1 - mx, 1 - my)]
    local, sends, recvs = [], [], []
    for a in range(len(ins)):
        local.append(pltpu.make_async_copy(ins[a], outs[a].at[2 * mx + my], local_sems.at[a]))
        for k, (px, py) in enumerate(chips):
            sems = (send_sems.at[3 * a + k], recv_sems.at[3 * a + k], (px, py, mc))
            sends.append(_remote(ins[a], outs[a].at[2 * mx + my], *sems))
            recvs.append(_remote(ins[a], outs[a].at[2 * px + py], *sems))
    return local, sends, recvs


def _gather_half_plan(ins, outs, send_sems, recv_sems, local_sems):
    mx, my, mc = _me()
    chips = [(1 - mx, my), (mx, 1 - my), (1 - mx, 1 - my)]
    local, sends, recvs = [], [], []
    for a in range(len(ins)):
        h = ins[a].shape[0] // 2
        mine = pl.ds(pl.multiple_of(mc * h, 16), h)
        local.append(pltpu.make_async_copy(ins[a], outs[a].at[2 * mx + my], local_sems.at[a]))
        for k, (px, py) in enumerate(chips):
            sems = (send_sems.at[3 * a + k], recv_sems.at[3 * a + k], (px, py, mc))
            sends.append(_remote(ins[a].at[mine], outs[a].at[2 * mx + my, mine], *sems))
            recvs.append(_remote(ins[a].at[mine], outs[a].at[2 * px + py, mine], *sems))
    return local, sends, recvs


def _sibling_fill(pieces):
    h = pieces.shape[1] // 2

    def body(p_ref, o_ref, send_sems, recv_sems):
        mx, my, mc = _me()
        sib = (mx, my, 1 - mc)
        chips = [(1 - mx, my), (mx, 1 - my), (1 - mx, 1 - my)]
        half = lambda c: pl.ds(pl.multiple_of(c * h, 16), h)
        o_ref[2 * mx + my] = p_ref[2 * mx + my]
        sends = []
        for k, (px, py) in enumerate(chips):
            j = 2 * px + py
            o_ref[j, half(mc), :] = p_ref[j, half(mc), :]
            cp = _remote(p_ref.at[j, half(mc)], o_ref.at[j, half(mc)], send_sems.at[k], recv_sems.at[k], sib)
            cp.start()
            sends.append(cp)
        for k, (px, py) in enumerate(chips):
            j = 2 * px + py
            _remote(p_ref.at[j, half(mc)], o_ref.at[j, half(1 - mc)], send_sems.at[k], recv_sems.at[k],
                    sib).wait_recv()
        for cp in sends:
            cp.wait_send()

    vmem = pl.BlockSpec(memory_space=pltpu.VMEM)
    return pl.pallas_call(
        body, name="sibling_fill", out_shape=_sds(pieces.shape, pieces.dtype),
        in_specs=[vmem], out_specs=vmem,
        scratch_shapes=[pltpu.SemaphoreType.DMA((N_CHIP - 1,)), pltpu.SemaphoreType.DMA((N_CHIP - 1,))],
        compiler_params=_cparams(),
    )(pieces)


def _start(plan):
    local, sends, _ = plan
    for cp in local + sends:
        cp.start()


def _finish(plan):
    local, sends, recvs = plan
    for cp in recvs:
        cp.wait_recv()
    for cp in sends:
        cp.wait_send()
    for cp in local:
        cp.wait()


def _exchange_shapes(pieces):
    return [_sds((N_DEV,) + p.shape[2:], p.dtype) for p in pieces]


def _exchange_sems(n):
    return [pltpu.SemaphoreType.DMA(((N_DEV - 1) * n,)), pltpu.SemaphoreType.DMA(((N_DEV - 1) * n,)),
            pltpu.SemaphoreType.DMA((n,))]


def _exchange_plan(ins, outs, send_sems, recv_sems, local_sems):
    me = _me()
    mx, my, mc = me
    local, sends, recvs = [], [], []
    for a in range(len(ins)):
        local.append(pltpu.make_async_copy(ins[a].at[2 * mx + my, mc], outs[a].at[_lin(me)], local_sems.at[a]))
        for k in range(1, N_DEV):
            p = _peer(k, me)
            s = (N_DEV - 1) * a + k - 1
            sends.append(_remote(ins[a].at[2 * p[0] + p[1], p[2]], outs[a].at[_lin(me)], send_sems.at[s],
                                 recv_sems.at[s], p))
            recvs.append(_remote(ins[a].at[2 * mx + my, mc], outs[a].at[_lin(p)], send_sems.at[s],
                                 recv_sems.at[s], p))
    return local, sends, recvs


REDUCE_VMEM = 56 * 1024 * 1024


def _reduce_swap(recvs):
    n = len(recvs)

    def body(*refs):
        r_refs, o_refs = refs[:n], refs[n:2 * n]
        send_sems, recv_sems = refs[2 * n:]
        mx, my, mc = _me()
        sib = (mx, my, 1 - mc)
        half = lambda a, c: o_refs[a].at[pl.ds(pl.multiple_of(c * recvs[a].shape[1], 8), recvs[a].shape[1])]
        sends = []
        for a in range(n):
            g = r_refs[a][0].astype(F32)
            for s in range(1, N_DEV):
                g = g + r_refs[a][s].astype(F32)
            half(a, mc)[...] = g
            cp = _remote(half(a, mc), half(a, mc), send_sems.at[a], recv_sems.at[a], sib)
            cp.start()
            sends.append(cp)
        for a in range(n):
            _remote(half(a, mc), half(a, 1 - mc), send_sems.at[a], recv_sems.at[a], sib).wait_recv()
        for cp in sends:
            cp.wait_send()

    vmem = pl.BlockSpec(memory_space=pltpu.VMEM)
    return pl.pallas_call(
        body, name="reduce_swap", out_shape=[_sds((2 * r.shape[1], r.shape[2])) for r in recvs],
        in_specs=[vmem] * n, out_specs=[vmem] * n,
        scratch_shapes=[pltpu.SemaphoreType.DMA((n,)), pltpu.SemaphoreType.DMA((n,))],
        compiler_params=_cparams(None, REDUCE_VMEM),
    )(*recvs)


def _adamw_big(g, w, m, v, name):
    rows, cols = g.shape
    tr = next((t for t in (256, 176, 128, 64, 8) if rows % t == 0), None)
    if tr is None:
        tc = _tile(cols, 256)
        blk, grid = pl.BlockSpec((rows, tc), lambda i: (0, i)), (cols // tc,)
    else:
        blk, grid = pl.BlockSpec((tr, cols), lambda i: (i, 0)), (rows // tr,)

    def body(g_ref, w_ref, m_ref, v_ref, go_ref, d_ref, m2_ref, v2_ref):
        g = g_ref[...]
        delta, m2, v2 = _adamw(w_ref[...], g, m_ref[...], v_ref[...])
        go_ref[...] = g
        d_ref[...] = delta
        m2_ref[...] = m2
        v2_ref[...] = v2

    return pl.pallas_call(
        body, name=name, grid=grid,
        in_specs=[blk] * 4, out_specs=[blk] * 4, out_shape=[_sds((rows, cols))] * 4,
        compiler_params=_cparams(("parallel",)),
    )(g, w, m, v)


SMALL_ORDER = (("mod", 6 * D), ("norm1_w", D), ("norm2_w", D), ("conv_w", CONVW * 3 * GW), ("a_log", GH),
               ("dt_bias", GH), ("gdn_norm_w", HD), ("q_norm_w", HD), ("k_norm_w", HD), ("sinks", SQH), ("loss", 1))
SMALL_R = 120


def _pack_small(d):
    parts = [d[k].reshape(-1).astype(F32) if k in d else jnp.zeros((n,), F32) for k, n in SMALL_ORDER]
    used = sum(n for _, n in SMALL_ORDER)
    parts.append(jnp.zeros((SMALL_R * LANE - used,), F32))
    return jnp.concatenate(parts).reshape(SMALL_R, LANE)


def _unpack_small(pk):
    flat = pk.reshape(-1)
    out, r = {}, 0
    for k, n in SMALL_ORDER:
        out[k] = flat[r:r + n]
        r += n
    return out


def kernel(x, c, w_ada, b_ada, norm1_w, w_in, conv_w, a_log, dt_bias, gdn_norm_w, q_norm_w, k_norm_w, sinks, w_out, norm2_w, w_gate, w_up, w_down, loss_target, m_w_ada, m_b_ada, m_norm1_w, m_w_in, m_conv_w, m_a_log, m_dt_bias, m_gdn_norm_w, m_q_norm_w, m_k_norm_w, m_sinks, m_w_out, m_norm2_w, m_w_gate, m_w_up, m_w_down, v_w_ada, v_b_ada, v_norm1_w, v_w_in, v_conv_w, v_a_log, v_dt_bias, v_gdn_norm_w, v_q_norm_w, v_k_norm_w, v_sinks, v_w_out, v_norm2_w, v_w_gate, v_w_up, v_w_down):
    mx, my, mc = _me()
    chip = 2 * mx + my
    dev = 4 * mx + 2 * my + mc
    T = x.shape[1]

    as_rows = lambda t, transposed: t[0].T if transposed else t[0]
    transposed = (True, False, True, True, False)
    big_w = [as_rows(t, tr) for t, tr in zip((w_in, w_out, w_gate, w_up, w_down), transposed)]
    shards = [t.astype(BF16) for t in big_w]

    b_sh = lax.dynamic_slice(b_ada, (0, chip * ADA_N), (1, ADA_N))
    w_in_sh = jnp.pad(shards[0], ((0, W_IN_ROWS_PAD - W_IN_ROWS), (0, 0)))
    c_all, conv_all, mods, a_in = _prologue(c, conv_w.reshape(CONVW, 3 * GW // N_CHIP), w_ada[0], b_sh, w_in_sh)
    c8 = c_all.reshape(N_DEV, D)
    conv_full = jnp.concatenate([conv_all[2 * j] for j in range(N_CHIP)], axis=1)
    mod = jnp.concatenate([lax.dynamic_slice(mods[2 * j], (dev, 0), (1, ADA_N)) for j in range(N_CHIP)], axis=1)
    w_in_pt = _permute_w_in_t(_sibling_fill(a_in))

    loss, grad_x, big, small = _local_step(
        x[0], loss_target[0], mod, norm1_w, w_in_pt, conv_full, a_log, dt_bias, gdn_norm_w,
        q_norm_w, k_norm_w, sinks, norm2_w, shards[1:])

    small["loss"] = loss[:, :1]
    sg = _all_gather8(_pack_small(small), "gather_small_grads")
    rep = dict(mod=(b_ada, m_b_ada, v_b_ada), norm1_w=(norm1_w, m_norm1_w, v_norm1_w),
               norm2_w=(norm2_w, m_norm2_w, v_norm2_w), a_log=(a_log, m_a_log, v_a_log),
               dt_bias=(dt_bias, m_dt_bias, v_dt_bias), gdn_norm_w=(gdn_norm_w, m_gdn_norm_w, v_gdn_norm_w),
               q_norm_w=(q_norm_w, m_q_norm_w, v_q_norm_w), k_norm_w=(k_norm_w, m_k_norm_w, v_k_norm_w),
               sinks=(sinks, m_sinks, v_sinks))
    wmv = [_pack_small({k: t[i] for k, t in rep.items()}) for i in range(3)]
    sres = _reduce_adamw(sg, wmv[0], wmv[1], wmv[2], "small_reduce_adamw")
    s_g, s_d, s_m, s_v = [_unpack_small(sres[i]) for i in range(4)]
    loss_out = s_g["loss"][0]

    g_conv = lax.dynamic_slice(s_g["conv_w"].reshape(CONVW, 3 * GW), (0, chip * (3 * GW // N_CHIP)),
                               (CONVW, 3 * GW // N_CHIP))
    pad16 = lambda t: jnp.concatenate([t.reshape(12, LANE), jnp.zeros((4, LANE), F32)], axis=0)
    cres = _adamw_call(pad16(g_conv), pad16(conv_w), pad16(m_conv_w), pad16(v_conv_w), "conv_adamw")
    conv_out = [g_conv.reshape(conv_w.shape)] + [cres[i, :12].reshape(conv_w.shape) for i in range(3)]

    dmod8 = sg[:, :6 * D // LANE].reshape(N_DEV, 6 * D)
    dm = lax.dynamic_slice(dmod8, (0, chip * ADA_N), (N_DEV, ADA_N))
    zpad = lambda t: jnp.concatenate([t, jnp.zeros((KPAD - N_DEV, t.shape[1]), F32)], axis=0)
    ares = _w_ada_update(zpad(c8), zpad(dm), w_ada[0], m_w_ada[0], v_w_ada[0])

    names = ("w_in", "w_out", "w_gate", "w_up", "w_down")
    g_full = list(_reduce_swap(big))
    g_full[0] = g_full[0][:W_IN_ROWS]
    big_m = [as_rows(t, tr) for t, tr in zip((m_w_in, m_w_out, m_w_gate, m_w_up, m_w_down), transposed)]
    big_v = [as_rows(t, tr) for t, tr in zip((v_w_in, v_w_out, v_w_gate, v_w_up, v_w_down), transposed)]
    upd = [_adamw_big(g, w, m, v, "adamw_" + nm) for g, w, m, v, nm in zip(g_full, big_w, big_m, big_v, names)]
    back = lambda t, tr: (t.T if tr else t)[None]
    bg, bd, bm, bv = [[back(u[i], tr) for u, tr in zip(upd, transposed)] for i in range(4)]

    def group(a_i, small_d, conv_i, big_l):
        s = lambda k, ref: small_d[k].reshape(ref.shape)
        return [ares[a_i][None], s("mod", b_ada), s("norm1_w", norm1_w), big_l[0], conv_out[conv_i],
                s("a_log", a_log), s("dt_bias", dt_bias), s("gdn_norm_w", gdn_norm_w), s("q_norm_w", q_norm_w),
                s("k_norm_w", k_norm_w), s("sinks", sinks), big_l[1], s("norm2_w", norm2_w), big_l[2], big_l[3],
                big_l[4]]

    outs = [loss_out, grad_x[None]]
    outs += group(0, s_g, 0, bg) + group(1, s_d, 1, bd) + group(2, s_m, 2, bm) + group(3, s_v, 3, bv)
    return tuple(outs)
```

```python
import jax
import jax.numpy as jnp
from jax import lax
from jax.experimental import pallas as pl
from jax.experimental.pallas import tpu as pltpu

F32 = jnp.float32
BF16 = jnp.bfloat16
MESH = pl.DeviceIdType.MESH

D = 1024
HD = 64
GH = 8
GW = GH * HD
SQH = 8
SKVH = 2
SGRP = SQH // SKVH
WIN = 128
CONVW = 4
CHUNK = 64
DFF = 2816
PROJ = 2832
NP = 3072
EPS = 1e-6
N_DEV = 8
N_CHIP = 4

ADAM_LR = 0.001
ADAM_B1 = 0.9
ADAM_B2 = 0.999
ADAM_EPS = 1e-08
ADAM_WD = 0.01
ADAM_STEP = 10

VMEM_LIMIT = 48 * 1024 * 1024
GDN_BWD_VMEM = 58 * 1024 * 1024
LANE = 128


def _cparams(sem=None, vmem=VMEM_LIMIT):
    return pltpu.CompilerParams(dimension_semantics=sem, vmem_limit_bytes=vmem)


_NN = ((1,), (0,))
_NT = ((1,), (1,))
_TN = ((0,), (0,))


def _dot(a, b, dims):
    if a.ndim == 3:
        (ca,), (cb,) = dims
        return lax.dot_general(a, b, (((ca + 1,), (cb + 1,)), ((0,), (0,))), preferred_element_type=F32)
    return lax.dot_general(a, b, (dims, ((), ())), preferred_element_type=F32)


def _raw1(a, b, dims):
    return _dot(a.astype(BF16), b.astype(BF16), dims)


def _raw3(a, b, dims):
    ah = a.astype(BF16)
    al = (a - ah.astype(F32)).astype(BF16)
    bh = b.astype(BF16)
    bl = (b - bh.astype(F32)).astype(BF16)
    return _dot(ah, bh, dims) + (_dot(al, bh, dims) + _dot(ah, bl, dims))


def _make_diff_mm(raw):
    @jax.custom_vjp
    def nn(a, b):
        return raw(a, b, _NN)

    @jax.custom_vjp
    def nt(a, b):
        return raw(a, b, _NT)

    @jax.custom_vjp
    def tn(a, b):
        return raw(a, b, _TN)

    nn.defvjp(lambda a, b: (raw(a, b, _NN), (a, b)), lambda r, g: (nt(g, r[1]), tn(r[0], g)))
    nt.defvjp(lambda a, b: (raw(a, b, _NT), (a, b)), lambda r, g: (nn(g, r[1]), tn(g, r[0])))
    tn.defvjp(lambda a, b: (raw(a, b, _TN), (a, b)), lambda r, g: (nt(r[1], g), nn(r[0], g)))
    return nn, nt, tn


def _tri_inv_raw(a, nn3):
    n = a.shape[-1]
    ri = lax.broadcasted_iota(jnp.int32, (n, n), 0)
    ci = lax.broadcasted_iota(jnp.int32, (n, n), 1)
    t = (ri == ci).astype(F32)
    for lvl in range((n - 1).bit_length()):
        same_pair = (ri >> (lvl + 1)) == (ci >> (lvl + 1))
        lower_left = (((ri >> lvl) & 1) == 1) & (((ci >> lvl) & 1) == 0)
        y = jnp.where(same_pair & lower_left, a, 0.0)
        t = t - y if lvl == 0 else t - nn3(nn3(t, y), t)
    return t


class _Kit:
    def __init__(self, diff):
        if diff:
            self.nn, self.nt, self.tn = _make_diff_mm(_raw1)
            self.nn3, self.nt3, self.tn3 = _make_diff_mm(_raw3)
            nn3, nt3, tn3 = self.nn3, self.nt3, self.tn3

            @jax.custom_vjp
            def inv(a, t):
                return t

            def inv_fwd(a, t):
                return t, t

            def inv_bwd(t, g):
                return -tn3(t, nt3(g, t)), jnp.zeros_like(t)

            inv.defvjp(inv_fwd, inv_bwd)
            self.inv = inv
        else:
            self.nn = lambda a, b: _raw1(a, b, _NN)
            self.nt = lambda a, b: _raw1(a, b, _NT)
            self.tn = lambda a, b: _raw1(a, b, _TN)
            self.nn3 = lambda a, b: _raw3(a, b, _NN)
            self.nt3 = lambda a, b: _raw3(a, b, _NT)
            self.tn3 = lambda a, b: _raw3(a, b, _TN)
            self.inv = lambda a, t: _tri_inv_raw(a, self.nn3) if t is None else t


def _sigmoid(x):
    return 1.0 / (1.0 + jnp.exp(-x))


def _silu(x):
    return x * _sigmoid(x)


def _rms(x, w):
    return x * lax.rsqrt(jnp.mean(x * x, axis=-1, keepdims=True) + EPS) * w


def _tile(dim, target):
    t = (min(dim, target) // LANE) * LANE
    while t >= LANE:
        if dim % t == 0:
            return t
        t -= LANE
    return dim


MM_TM, MM_TN, MM_TK = 1408, 1536, 1408


def _matmul(a, b, ta=False, tb=False, out_dtype=F32, name="matmul", gather=None, exchange=None):
    carried = gather if gather is not None else exchange if exchange is not None else []
    nc = len(carried)
    if ta:
        K, M = a.shape
    else:
        M, K = a.shape
    if tb:
        N, K2 = b.shape
    else:
        K2, N = b.shape
    assert K == K2, (a.shape, b.shape, ta, tb)
    tm, tn, tk = _tile(M, MM_TM), _tile(N, MM_TN), _tile(K, MM_TK)
    nk = K // tk
    dims = ((0,) if ta else (1,), (1,) if tb else (0,))

    grid = (M // tm, N // tn, nk)

    def body(*refs):
        a_ref, b_ref = refs[:2]
        o_ref = refs[2 + nc]
        scratch = refs[3 + 2 * nc:]
        k = pl.program_id(2)
        if nc:
            make_plan = _gather_plan if gather is not None else _exchange_plan
            plan = make_plan(refs[2:2 + nc], refs[3 + nc:3 + 2 * nc], *scratch[-3:])
            at = lambda pos: ((pl.program_id(0) == pos[0]) & (pl.program_id(1) == pos[1]) & (k == pos[2]))

            @pl.when(at((0, 0, 0)))
            def _():
                _start(plan)

        part = _dot(a_ref[...].astype(BF16), b_ref[...].astype(BF16), dims)
        if nk == 1:
            o_ref[...] = part.astype(o_ref.dtype)
        else:
            acc_ref = scratch[0]

            @pl.when(k == 0)
            def _():
                acc_ref[...] = part

            @pl.when((k > 0) & (k < nk - 1))
            def _():
                acc_ref[...] += part

            @pl.when(k == nk - 1)
            def _():
                o_ref[...] = (acc_ref[...] + part).astype(o_ref.dtype)

        if nc:
            @pl.when(at((grid[0] - 1, grid[1] - 1, nk - 1)))
            def _():
                _finish(plan)

    a_spec = (pl.BlockSpec((tk, tm), lambda i, j, k: (k, i)) if ta
              else pl.BlockSpec((tm, tk), lambda i, j, k: (i, k)))
    b_spec = (pl.BlockSpec((tn, tk), lambda i, j, k: (j, k)) if tb
              else pl.BlockSpec((tk, tn), lambda i, j, k: (k, j)))
    if gather is not None:
        c_shapes, c_sems = _gather_shapes(carried), _gather_sems(nc)
    elif exchange is not None:
        c_shapes, c_sems = _exchange_shapes(carried), _exchange_sems(nc)
    else:
        c_shapes, c_sems = [], []
    res = pl.pallas_call(
        body, name=name, grid=grid,
        in_specs=[a_spec, b_spec] + _hbm_specs(nc),
        out_specs=[pl.BlockSpec((tm, tn), lambda i, j, k: (i, j))] + _hbm_specs(nc),
        out_shape=[jax.ShapeDtypeStruct((M, N), out_dtype)] + c_shapes,
        scratch_shapes=([pltpu.VMEM((tm, tn), F32)] if nk > 1 else []) + c_sems,
        compiler_params=_cparams(("arbitrary",) * 3 if nc else ("parallel", "parallel", "arbitrary")),
    )(a, b, *carried)
    return (res[0], res[1:]) if nc else res[0]


def _sds(shape, dtype=F32):
    return jax.ShapeDtypeStruct(shape, dtype)


def _norm_mod(x, nw, scale, shift):
    return _rms(x, nw) * (1.0 + scale) + shift


IN_PROJ_VMEM = 56 * 1024 * 1024


def _norm_in_proj(x, nw, scale, shift, w_in_pt, shards):
    T = x.shape[0]
    N = w_in_pt.shape[0]
    tm, tn = _tile(T, 1024), 3 * GW
    nm, nn = T // tm, N // tn
    nz = (GAB0 - 3 * GW) // HD
    ns = len(shards)

    def body(*refs):
        x_ref, nw_ref, sc_ref, sh_ref, w_ref = refs[:5]
        h_ref, o_ref, zs_ref = refs[5 + ns:8 + ns]
        plan = _gather_plan(refs[5:5 + ns], refs[8 + ns:8 + 2 * ns], *refs[8 + 2 * ns:])
        i, j = pl.program_id(0), pl.program_id(1)

        @pl.when((i == 0) & (j == 0))
        def _():
            _start(plan)

        @pl.when(j == 0)
        def _():
            for r0 in range(0, tm, ROWS_EPI):
                rows = pl.ds(r0, ROWS_EPI)
                h_ref[rows, :] = _norm_mod(x_ref[rows, :], nw_ref[...], sc_ref[...], sh_ref[...]).astype(BF16)

        o = _dot(h_ref[...], w_ref[...], _NT)
        o_ref[...] = o

        @pl.when(j == 1)
        def _():
            for p in range(nz // 2):
                zs_ref[2 * p], zs_ref[2 * p + 1] = _split_pair(o[:, p * LANE:(p + 1) * LANE])

        @pl.when((i == nm - 1) & (j == nn - 1))
        def _():
            _finish(plan)

    vec = pl.BlockSpec((1, D), lambda i, j: (0, 0))
    res = pl.pallas_call(
        body, name="norm1_in_proj", grid=(nm, nn),
        in_specs=[pl.BlockSpec((tm, D), lambda i, j: (i, 0)), vec, vec, vec,
                  pl.BlockSpec((tn, D), lambda i, j: (j, 0))] + _hbm_specs(ns),
        out_specs=[pl.BlockSpec((tm, D), lambda i, j: (i, 0)), pl.BlockSpec((tm, tn), lambda i, j: (i, j)),
                   pl.BlockSpec((nz, tm, HD), lambda i, j: (0, i, 0))] + _hbm_specs(ns),
        out_shape=[_sds((T, D), BF16), _sds((T, N)), _sds((nz, T, HD))] + _gather_shapes(shards),
        scratch_shapes=_gather_sems(ns),
        compiler_params=_cparams(("arbitrary", "arbitrary"), IN_PROJ_VMEM),
    )(x, nw, scale, shift, w_in_pt, *shards)
    return res[0], res[1], res[2], res[3:]


ROWS_TM = 512
ROWS_EPI = 256


def _matmul_rows(a, b, epi, tiled, consts, out_tiled, out_acc, name, pieces=()):
    T, K = a.shape
    tm, tk = _tile(T, ROWS_TM), _tile(K, MM_TK)
    nm, nk = T // tm, K // tk
    npc, nt, ncst, no, na = len(pieces), len(tiled), len(consts), len(out_tiled), len(out_acc)
    n_in = 2 + nt + ncst

    def body(*refs):
        a_ref, b_ref = refs[:2]
        t_refs, c_refs = refs[2:2 + nt], refs[2 + nt:n_in]
        o_refs = refs[n_in + npc:n_in + npc + no]
        acc_refs = refs[n_in + npc + no:n_in + npc + no + na]
        n_out = no + na + npc
        res_ref = refs[n_in + npc + n_out]
        plan = _exchange_plan(refs[n_in:n_in + npc], refs[n_in + npc + no + na:n_in + npc + n_out],
                              *refs[n_in + npc + n_out + 1:]) if npc else None
        i, k = pl.program_id(0), pl.program_id(1)

        @pl.when((i == 0) & (k == 0))
        def _():
            for r in acc_refs:
                r[...] = jnp.zeros_like(r)
            if npc:
                _start(plan)

        part = _dot(a_ref[...], b_ref[...], _NN)

        @pl.when(k == 0)
        def _():
            res_ref[...] = part

        @pl.when(k > 0)
        def _():
            res_ref[...] += part

        @pl.when(k == nk - 1)
        def _():
            for r0 in range(0, tm, ROWS_EPI):
                rows = pl.ds(r0, ROWS_EPI)
                outs = epi(res_ref[rows, :], *[r[rows, :] for r in t_refs], *[r[...] for r in c_refs])
                for r, v in zip(o_refs, outs[:no]):
                    r[rows, :] = v.astype(r.dtype)
                for r, v in zip(acc_refs, outs[no:]):
                    r[...] += v

        if npc:
            @pl.when((i == nm - 1) & (k == nk - 1))
            def _():
                _finish(plan)

    row = lambda w: pl.BlockSpec((tm, w), lambda i, k: (i, 0))
    whole = lambda s: pl.BlockSpec(s.shape, lambda i, k: (0, 0))
    res = pl.pallas_call(
        body, name=name, grid=(nm, nk),
        in_specs=[pl.BlockSpec((tm, tk), lambda i, k: (i, k)), pl.BlockSpec((tk, D), lambda i, k: (k, 0))]
                 + [row(t.shape[1]) for t in tiled] + [whole(c) for c in consts] + _hbm_specs(npc),
        out_specs=[row(s.shape[1]) for s in out_tiled] + [whole(s) for s in out_acc] + _hbm_specs(npc),
        out_shape=list(out_tiled) + list(out_acc) + (_exchange_shapes(pieces) if npc else []),
        scratch_shapes=[pltpu.VMEM((tm, D), F32)] + (_exchange_sems(npc) if npc else []),
        compiler_params=_cparams(("arbitrary", "arbitrary")),
    )(a, b, *tiled, *consts, *pieces)
    return res[:no + na], res[no + na:]


def _in_proj_dx_norm_bwd(dproj, w_in_pt, x, dres, nw, scale, shift, pieces):
    T = x.shape[0]

    def epi(dh, x, dres, nw, scale, shift):
        _, vjp = jax.vjp(_norm_mod, x, nw, scale, shift)
        dx, dnw, dsc, dsh = vjp(dh)
        return dx + dres, dnw, dsc, dsh

    return _matmul_rows(dproj, w_in_pt, epi, [x, dres], [nw, scale, shift], [_sds((T, D))], [_sds((1, D))] * 3,
                        "in_proj_dx_norm1_bwd", pieces)


def _out_proj_resid_norm(o_hm, w_out, x, gate1, nw, scale, shift):
    T = x.shape[0]
    nheads = o_hm.shape[0]
    tm = _tile(T, ROWS_TM)

    def body(o_ref, w_ref, x_ref, g_ref, nw_ref, sc_ref, sh_ref, cat_ref, mixed_ref, x1_ref, h2_ref):
        cat = jnp.concatenate([_merge_pair(o_ref[2 * p], o_ref[2 * p + 1]) for p in range(nheads // 2)], axis=1)
        cat_ref[...] = cat.astype(BF16)
        mixed_ref[...] = _dot(cat_ref[...], w_ref[...], _NN)
        for r0 in range(0, tm, ROWS_EPI):
            rows = pl.ds(r0, ROWS_EPI)
            x1, h2 = _resid_norm(x_ref[rows, :], mixed_ref[rows, :], g_ref[...], nw_ref[...], sc_ref[...], sh_ref[...])
            x1_ref[rows, :] = x1
            h2_ref[rows, :] = h2.astype(BF16)

    row = pl.BlockSpec((tm, D), lambda i: (i, 0))
    vec = pl.BlockSpec((1, D), lambda i: (0, 0))
    return pl.pallas_call(
        body, name="out_proj_resid_norm2", grid=(T // tm,),
        in_specs=[pl.BlockSpec((nheads, tm, HD), lambda i: (0, i, 0)), pl.BlockSpec((D, D), lambda i: (0, 0)), row,
                  vec, vec, vec, vec],
        out_specs=[row, row, row, row],
        out_shape=[_sds((T, D), BF16), _sds((T, D)), _sds((T, D)), _sds((T, D), BF16)],
        compiler_params=_cparams(("parallel",)),
    )(o_hm, w_out, x, gate1, nw, scale, shift)


def _ffn_up_dx_resid_bwd(dab, w_gut, x, mixed, dy, gate1, nw, scale, shift):
    T = x.shape[0]

    def epi(dh2, x, mixed, dy, gate1, nw, scale, shift):
        _, vjp = jax.vjp(_resid_norm, x, mixed, gate1, nw, scale, shift)
        return vjp((dy, dh2))

    outs, _ = _matmul_rows(dab, w_gut, epi, [x, mixed, dy], [gate1, nw, scale, shift],
                           [_sds((T, D)), _sds((T, D), BF16)], [_sds((1, D))] * 4, "ffn_up_dx_resid_norm2_bwd")
    return outs


def _ffn_down_loss(act, w_down, x1, target, gate2):
    T = x1.shape[0]

    def epi(ffn, x1, target, gate2):
        y = x1 + gate2 * ffn
        err = y - target
        loss = 0.5 * jnp.sum(jnp.sum(err * err, axis=1, keepdims=True), axis=0, keepdims=True) / D
        dy = err * (1.0 / D)
        return dy, gate2 * dy, jnp.sum(dy * ffn, axis=0, keepdims=True), jnp.broadcast_to(loss, (1, LANE))

    outs, _ = _matmul_rows(act, w_down, epi, [x1, target], [gate2], [_sds((T, D)), _sds((T, D), BF16)],
                           [_sds((1, D)), _sds((1, LANE))], "ffn_down_loss")
    return outs


def _resid_norm(x, mixed, gate1, nw, scale, shift):
    x1 = x + gate1 * mixed
    return x1, _norm_mod(x1, nw, scale, shift)


FFN_BLK = 256
FFN_TM = 2048
AB_SLOTS = 3


def _interleave_gate_up(gate_t, up_t):
    blocks = lambda t: t.reshape(DFF // FFN_BLK, 1, FFN_BLK, D)
    return jnp.concatenate([blocks(gate_t), blocks(up_t)], axis=1).reshape(2 * DFF, D)


def _split_gate_up(g):
    g = g.reshape(DFF // FFN_BLK, 2, FFN_BLK, D)
    return g[:, 0].reshape(DFF, D), g[:, 1].reshape(DFF, D)


def _ffn_up_act(h2, w_gut):
    T = h2.shape[0]
    tm = _tile(T, FFN_TM)

    def body(h_ref, w_ref, ab_ref, act_ref):
        ab = _dot(h_ref[...], w_ref[...], _NT)
        ab_ref[...] = ab
        act_ref[...] = (_silu(ab[:, :FFN_BLK]) * ab[:, FFN_BLK:]).astype(act_ref.dtype)

    return pl.pallas_call(
        body, name="ffn_up_act", grid=(T // tm, DFF // FFN_BLK),
        in_specs=[pl.BlockSpec((tm, D), lambda i, j: (i, 0)), pl.BlockSpec((2 * FFN_BLK, D), lambda i, j: (j, 0))],
        out_specs=[pl.BlockSpec((tm, 2 * FFN_BLK), lambda i, j: (i, j)), pl.BlockSpec((tm, FFN_BLK), lambda i, j: (i, j))],
        out_shape=[_sds((T, 2 * DFF)), _sds((T, DFF), BF16)],
        compiler_params=_cparams(("parallel", "parallel")),
    )(h2, w_gut)


def _ffn_down_dx_act(dffn, w_down, ab):
    T = dffn.shape[0]
    tm = _tile(T, FFN_TM)
    nj = DFF // FFN_BLK
    steps = (T // tm) * nj
    assert steps >= AB_SLOTS

    def body(d_ref, w_ref, ab_hbm, o_ref, buf, sem):
        step = pl.program_id(0) * nj + pl.program_id(1)

        def fetch(t):
            rows = pl.ds((t // nj) * tm, tm)
            cols = pl.ds((t % nj) * (2 * FFN_BLK), 2 * FFN_BLK)
            return pltpu.make_async_copy(ab_hbm.at[rows, cols], buf.at[t % AB_SLOTS], sem.at[t % AB_SLOTS])

        @pl.when(step == 0)
        def _():
            for t in range(AB_SLOTS - 1):
                fetch(t).start()

        @pl.when(step + AB_SLOTS - 1 < steps)
        def _():
            fetch(step + AB_SLOTS - 1).start()

        dact = _dot(d_ref[...], w_ref[...], _NT)
        fetch(step).wait()
        ab_ref = buf.at[step % AB_SLOTS]
        a, b = ab_ref[:, :FFN_BLK], ab_ref[:, FFN_BLK:]
        s = _sigmoid(a)
        da = dact * b * (s * (1.0 + a * (1.0 - s)))
        db = dact * (a * s)
        o_ref[...] = jnp.concatenate([da, db], axis=1).astype(o_ref.dtype)

    return pl.pallas_call(
        body, name="ffn_down_dx_act", grid=(T // tm, nj),
        in_specs=[pl.BlockSpec((tm, D), lambda i, j: (i, 0)), pl.BlockSpec((FFN_BLK, D), lambda i, j: (j, 0))]
        + _hbm_specs(1),
        out_specs=pl.BlockSpec((tm, 2 * FFN_BLK), lambda i, j: (i, j)),
        out_shape=_sds((T, 2 * DFF), BF16),
        scratch_shapes=[pltpu.VMEM((AB_SLOTS, tm, 2 * FFN_BLK), F32), pltpu.SemaphoreType.DMA((AB_SLOTS,))],
        compiler_params=_cparams(("arbitrary", "arbitrary")),
    )(dffn, w_down, ab)


def _round_bf16(x):
    return x.astype(BF16).astype(F32)


def _shift_down(x, s, rows):
    if s == 0:
        return x
    return jnp.where(rows >= s, pltpu.roll(x, s, 0), 0.0)


def _shift_up(x, s, rows, T):
    if s == 0:
        return x
    return jnp.where(rows < T - s, pltpu.roll(x, T - s, 0), 0.0)


def _conv_fwd(proj, conv_w):
    T = proj.shape[0]
    ncol = 3 * GW // LANE

    def body(x_ref, w_ref, o_ref):
        x = _round_bf16(x_ref[...])
        rows = lax.broadcasted_iota(jnp.int32, x.shape, 0)
        acc = jnp.zeros_like(x)
        for j in range(CONVW):
            acc = acc + _round_bf16(w_ref[pl.ds(j, 1), :]) * _shift_down(x, CONVW - 1 - j, rows)
        o_ref[0], o_ref[1] = _split_pair(_silu(acc))

    return pl.pallas_call(
        body, name="conv_fwd", grid=(ncol,),
        in_specs=[pl.BlockSpec((T, LANE), lambda j: (0, j)), pl.BlockSpec((CONVW, LANE), lambda j: (0, j))],
        out_specs=pl.BlockSpec((2, T, HD), lambda j: (j, 0, 0)),
        out_shape=_sds((3 * GH, T, HD)),
        compiler_params=_cparams(("parallel",)),
    )(proj, conv_w)


def _split_pair(y):
    return y[:, :HD], pltpu.roll(y, HD, 1)[:, :HD]


def _merge_pair(a, b):
    return jnp.concatenate([a, b], axis=1)


def _merge_all(heads):
    return jnp.concatenate([_merge_pair(heads[2 * p], heads[2 * p + 1]) for p in range(heads.shape[0] // 2)], axis=1)


def _matmul_nt_heads(a, b, name):
    T, K = a.shape
    N = b.shape[0]
    tm = _tile(T, 1024)

    def body(a_ref, b_ref, o_ref):
        res = _dot(a_ref[...], b_ref[...], _NT)
        for p in range(N // LANE):
            o_ref[2 * p], o_ref[2 * p + 1] = _split_pair(res[:, p * LANE:(p + 1) * LANE])

    return pl.pallas_call(
        body, name=name, grid=(T // tm,),
        in_specs=[pl.BlockSpec((tm, K), lambda i: (i, 0)), pl.BlockSpec((N, K), lambda i: (0, 0))],
        out_specs=pl.BlockSpec((N // HD, tm, HD), lambda i: (0, i, 0)),
        out_shape=_sds((N // HD, T, HD)),
        compiler_params=_cparams(("parallel",)),
    )(a, b)


def _conv_bwd(proj, conv_w, dqc, dproj):
    T = proj.shape[0]
    ncol = 3 * GW // LANE

    def body(x_ref, w_ref, d_ref, buf_ref, dx_ref, dw_ref):
        x = _round_bf16(x_ref[...])
        rows = lax.broadcasted_iota(jnp.int32, x.shape, 0)
        xs = [_shift_down(x, CONVW - 1 - j, rows) for j in range(CONVW)]
        w = [_round_bf16(w_ref[pl.ds(j, 1), :]) for j in range(CONVW)]
        pre = jnp.zeros_like(x)
        for j in range(CONVW):
            pre = pre + w[j] * xs[j]
        s = _sigmoid(pre)
        dpre = _round_bf16(_merge_pair(d_ref[0], d_ref[1]) * (s * (1.0 + pre * (1.0 - s))))
        dx = jnp.zeros_like(x)
        for j in range(CONVW):
            dx = dx + w[j] * _shift_up(dpre, CONVW - 1 - j, rows, T)
            dw_ref[pl.ds(j, 1), :] = jnp.sum(dpre * xs[j], axis=0, keepdims=True)
        dx_ref[...] = dx.astype(dx_ref.dtype)

    return pl.pallas_call(
        body, name="conv_bwd", grid=(ncol,),
        in_specs=[pl.BlockSpec((T, LANE), lambda j: (0, j)), pl.BlockSpec((CONVW, LANE), lambda j: (0, j)),
                  pl.BlockSpec((2, T, HD), lambda j: (j, 0, 0))] + _hbm_specs(1),
        out_specs=[pl.BlockSpec((T, LANE), lambda j: (0, j)), pl.BlockSpec((CONVW, LANE), lambda j: (0, j))],
        out_shape=[_sds(dproj.shape, dproj.dtype), _sds((CONVW, 3 * GW))],
        input_output_aliases={3: 0},
        compiler_params=_cparams(("parallel",)),
    )(proj, conv_w, dqc, dproj)


def _gdn_prep(kit, q, k, v, ga, gb, alog, dtb, t_inv=None):
    C = CHUNK
    ri = lax.broadcasted_iota(jnp.int32, (C, C), 0)
    ci = lax.broadcasted_iota(jnp.int32, (C, C), 1)
    causal = ri >= ci
    strict = ri > ci
    eye = (ri == ci).astype(F32)
    lower = causal.astype(F32)
    upper = (ri <= ci).astype(F32)

    a = ga + dtb
    softplus = jnp.maximum(a, 0.0) + jnp.log(1.0 + jnp.exp(-jnp.abs(a)))
    g_row = -jnp.exp(alog) * softplus
    beta_row = _sigmoid(gb)
    g_col = jnp.sum(eye * g_row, axis=2, keepdims=True)
    beta_col = jnp.sum(eye * beta_row, axis=2, keepdims=True)
    G_col = jnp.sum(lower * g_row, axis=2, keepdims=True)
    G_row = jnp.sum(upper * g_col, axis=1, keepdims=True)
    G_last = jnp.sum(g_row, axis=2, keepdims=True)
    decay = jnp.exp(jnp.where(causal, G_col - G_row, -1e30))

    qn = q * lax.rsqrt(jnp.sum(q * q, axis=-1, keepdims=True) + EPS) * (HD ** -0.5)
    kn = k * lax.rsqrt(jnp.sum(k * k, axis=-1, keepdims=True) + EPS)
    kb = kn * beta_col
    A = jnp.where(strict, kit.nt(kb, kn) * decay, 0.0)
    Tm = kit.inv(A, t_inv)
    eG = jnp.exp(G_col)
    u = kit.nn3(Tm, v * beta_col)
    w = kit.nn3(Tm, kb * eG)
    qk = jnp.where(causal, kit.nt(qn, kn) * decay, 0.0)
    q_dec = qn * eG
    k_dec = kn * jnp.exp(G_last - G_col)
    dec = jnp.exp(G_last)
    return u, w, qk, q_dec, k_dec, dec, Tm


def _gdn_out(o, z, nw):
    return _rms(o, nw) * _silu(z)


GDN_CB = 4


def _gdn_specs(T, blk):
    TB = GDN_CB * CHUNK
    seq = lambda grp: pl.BlockSpec((GH, TB, HD), lambda i, grp=grp: (grp, blk(i), 0))
    row = lambda grp: pl.BlockSpec((GH, GDN_CB, 1, CHUNK), lambda i, grp=grp: (grp, blk(i), 0, 0))
    per_head = pl.BlockSpec((GH, 1, CHUNK), lambda i: (0, 0, 0))
    whole = pl.BlockSpec((1, HD), lambda i: (0, 0))
    state = pl.BlockSpec((GH, GDN_CB, HD, HD), lambda i: (0, blk(i), 0, 0))
    return seq, row, per_head, whole, state


def _gdn_load(seq_refs, row_refs, head_refs):
    chunks = lambda r: jnp.concatenate([r[:, pl.ds(cb * CHUNK, CHUNK), :] for cb in range(GDN_CB)], axis=0)
    rows = lambda r: jnp.concatenate([r[:, cb] for cb in range(GDN_CB)], axis=0)
    heads = lambda r: jnp.concatenate([r[...]] * GDN_CB, axis=0)
    return [chunks(r) for r in seq_refs], [rows(r) for r in row_refs], [heads(r) for r in head_refs]


def _gdn_fwd(qkv_hm, zs_hm, gab, alog_b, dtb_b, nw, shards):
    T = qkv_hm.shape[1]
    N = T // CHUNK
    nblk = N // GDN_CB
    ns = len(shards)
    seq, row, per_head, whole, state = _gdn_specs(T, lambda i: i)
    kit = _Kit(False)

    def body(*refs):
        q_ref, k_ref, v_ref, z_ref, ga_ref, gb_ref, al_ref, dt_ref, nw_ref = refs[:9]
        o_ref, S_ref, T_ref = refs[9 + ns:12 + ns]
        S_scr = refs[12 + 2 * ns]
        plan = _gather_plan(refs[9:9 + ns], refs[12 + ns:12 + 2 * ns], *refs[13 + 2 * ns:])

        @pl.when(pl.program_id(0) == 0)
        def _():
            S_scr[...] = jnp.zeros_like(S_scr)
            _start(plan)

        (q, k, v, z), (ga, gb), (al, dt) = _gdn_load((q_ref, k_ref, v_ref, z_ref), (ga_ref, gb_ref), (al_ref, dt_ref))
        u, w, qk, q_dec, k_dec, dec, t_inv = _gdn_prep(kit, q, k, v, ga, gb, al, dt)
        S = S_scr[...]
        for cb in range(GDN_CB):
            hs = slice(cb * GH, (cb + 1) * GH)
            S_ref[:, cb] = S
            T_ref[:, cb] = t_inv[hs]
            v_new = u[hs] - kit.nn(w[hs], S)
            o = kit.nn(q_dec[hs], S) + kit.nn(qk[hs], v_new)
            S = S * dec[hs] + kit.tn(k_dec[hs], v_new)
            o_ref[:, pl.ds(cb * CHUNK, CHUNK), :] = _gdn_out(o, z[hs], nw_ref[...])
        S_scr[...] = S

        @pl.when(pl.program_id(0) == nblk - 1)
        def _():
            _finish(plan)

    res = pl.pallas_call(
        body, name="gdn_fwd", grid=(nblk,),
        in_specs=[seq(0), seq(1), seq(2), seq(0), row(0), row(1), per_head, per_head, whole] + _hbm_specs(ns),
        out_specs=[seq(0), state, state] + _hbm_specs(ns),
        out_shape=[_sds((GH + SQH, T, HD)), _sds((GH, N, HD, HD)), _sds((GH, N, CHUNK, CHUNK))]
                  + _gather_shapes(shards),
        scratch_shapes=[pltpu.VMEM((GH, HD, HD), F32)] + _gather_sems(ns),
        compiler_params=_cparams(("arbitrary",)),
    )(qkv_hm, qkv_hm, qkv_hm, zs_hm, gab, gab, alog_b, dtb_b, nw, *shards)
    return res[0], (res[1], res[2]), res[3:]


def _gdn_bwd(qkv_hm, zs_hm, gab, alog_b, dtb_b, nw, S_all, do, pieces):
    T = qkv_hm.shape[1]
    N = T // CHUNK
    nblk = N // GDN_CB
    npc = len(pieces)
    dkit, kit = _Kit(True), _Kit(False)
    rseq, rrow, per_head, whole, rstate = _gdn_specs(T, lambda i: nblk - 1 - i)

    def body(*refs):
        q_ref, k_ref, v_ref, z_ref, ga_ref, gb_ref, al_ref, dt_ref, nw_ref, S_ref, T_ref, do_ref = refs[:12]
        dqkv_ref, dz_ref, dga_ref, dgb_ref, dal_ref, ddt_ref, dnw_ref = refs[12 + npc:19 + npc]
        dS_scr = refs[19 + 2 * npc]
        plan = _exchange_plan(refs[12:12 + npc], refs[19 + npc:19 + 2 * npc], *refs[20 + 2 * npc:])

        @pl.when(pl.program_id(0) == 0)
        def _():
            dS_scr[...] = jnp.zeros_like(dS_scr)
            dal_ref[...] = jnp.zeros_like(dal_ref)
            ddt_ref[...] = jnp.zeros_like(ddt_ref)
            dnw_ref[...] = jnp.zeros_like(dnw_ref)
            _start(plan)

        (q, k, v, z, dout), (ga, gb), (al, dt) = _gdn_load((q_ref, k_ref, v_ref, z_ref, do_ref), (ga_ref, gb_ref),
                                                          (al_ref, dt_ref))
        S_in = jnp.concatenate([S_ref[:, cb] for cb in range(GDN_CB)], axis=0)
        t_inv = jnp.concatenate([T_ref[:, cb] for cb in range(GDN_CB)], axis=0)
        prep = lambda *a: _gdn_prep(dkit, *a, t_inv=t_inv)[:6]
        (u, w, qk, q_dec, k_dec, dec), prep_vjp = jax.vjp(prep, q, k, v, ga, gb, al, dt)
        v_new = u - kit.nn(w, S_in)
        o = kit.nn(q_dec, S_in) + kit.nn(qk, v_new)
        _, out_vjp = jax.vjp(_gdn_out, o, z, nw_ref[...])
        do, dz, dnw = out_vjp(dout)
        dvn_part = kit.tn(qk, do)
        dS_part = kit.tn(q_dec, do)
        dS = dS_scr[...]
        dS_out, dvn = [None] * GDN_CB, [None] * GDN_CB
        for cb in reversed(range(GDN_CB)):
            hs = slice(cb * GH, (cb + 1) * GH)
            dS_out[cb] = dS
            dvn[cb] = dvn_part[hs] + kit.nn(k_dec[hs], dS)
            dS = dS * dec[hs] + dS_part[hs] - kit.tn(w[hs], dvn[cb])
        dS_scr[...] = dS
        dS_out = jnp.concatenate(dS_out, axis=0)
        dvn = jnp.concatenate(dvn, axis=0)
        ddec = jnp.sum(jnp.sum(S_in * dS_out, axis=2, keepdims=True), axis=1, keepdims=True)
        cts = (dvn, -kit.nt(dvn, S_in), kit.nt(do, v_new), kit.nt(do, S_in), kit.nt(v_new, dS_out), ddec)
        dq, dk, dv, dga, dgb, dal, ddt = prep_vjp(cts)
        lanesum = lambda t: jnp.broadcast_to(jnp.sum(t, axis=2, keepdims=True), t.shape)
        for cb in range(GDN_CB):
            hs = slice(cb * GH, (cb + 1) * GH)
            sl = pl.ds(cb * CHUNK, CHUNK)
            dqkv_ref[pl.ds(0, GH), sl, :] = dq[hs]
            dqkv_ref[pl.ds(GH, GH), sl, :] = dk[hs]
            dqkv_ref[pl.ds(2 * GH, GH), sl, :] = dv[hs]
            dz_ref[sl, :] = _merge_all(dz[hs]).astype(BF16)
            dga_ref[:, cb] = dga[hs]
            dgb_ref[:, cb] = dgb[hs]
            dal_ref[...] += lanesum(dal[hs])
            ddt_ref[...] += lanesum(ddt[hs])
        dnw_ref[...] += dnw

        @pl.when(pl.program_id(0) == nblk - 1)
        def _():
            _finish(plan)

    res = pl.pallas_call(
        body, name="gdn_bwd", grid=(nblk,),
        in_specs=[rseq(0), rseq(1), rseq(2), rseq(0), rrow(0), rrow(1), per_head, per_head, whole, rstate, rstate,
                  rseq(0)] + _hbm_specs(npc),
        out_specs=[pl.BlockSpec((3 * GH, GDN_CB * CHUNK, HD), lambda i: (0, nblk - 1 - i, 0)),
                   pl.BlockSpec((GDN_CB * CHUNK, GW), lambda i: (nblk - 1 - i, 3)), rrow(0),
                   rrow(0), per_head, per_head, whole] + _hbm_specs(npc),
        out_shape=[_sds((3 * GH, T, HD)), _sds((T, NP), BF16)] + [_sds((GH, N, 1, CHUNK))] * 2
                  + [_sds((GH, 1, CHUNK))] * 2 + [_sds((1, HD))] + _exchange_shapes(pieces),
        scratch_shapes=[pltpu.VMEM((GH, HD, HD), F32)] + _exchange_sems(npc),
        compiler_params=_cparams(("arbitrary",), GDN_BWD_VMEM),
    )(qkv_hm, qkv_hm, qkv_hm, zs_hm, gab, gab, alog_b, dtb_b, nw, S_all[0], S_all[1], do, *pieces)
    return res[:7], res[7:]


def _swa_heads(kit, first, q, kp, kc, vp, vc, qnw, knw, sink, slope):
    W = WIN
    ri = lax.broadcasted_iota(jnp.int32, (W, W), 0)
    ci = lax.broadcasted_iota(jnp.int32, (W, W), 1)
    mask_c = ri >= ci
    mask_p = ci > ri + first * W
    dist_c = (ri - ci).astype(F32)
    dist_p = (ri - ci + W).astype(F32)
    kpn = _rms(kp, knw)
    kcn = _rms(kc, knw)
    qn = _rms(q, qnw)
    sc = jnp.where(mask_c, kit.nt(qn, kcn) * (HD ** -0.5) - slope * dist_c, -1e30)
    sp = jnp.where(mask_p, kit.nt(qn, kpn) * (HD ** -0.5) - slope * dist_p, -1e30)
    m = jnp.maximum(jnp.maximum(jnp.max(sc, axis=-1, keepdims=True), jnp.max(sp, axis=-1, keepdims=True)), sink)
    m = lax.stop_gradient(m)
    pc = jnp.exp(sc - m)
    pp = jnp.exp(sp - m)
    den = jnp.sum(pc, axis=-1, keepdims=True) + jnp.sum(pp, axis=-1, keepdims=True) + jnp.exp(sink - m)
    inv = 1.0 / den
    return kit.nn(pc * inv, vc) + kit.nn(pp * inv, vp)


def _swa_grads(kit, first, q, kp, kc, vp, vc, qnw, knw, sink, slope, do):
    W = WIN
    ri = lax.broadcasted_iota(jnp.int32, (W, W), 0)
    ci = lax.broadcasted_iota(jnp.int32, (W, W), 1)
    mask_c = ri >= ci
    mask_p = ci > ri + first * W
    dist_c = (ri - ci).astype(F32)
    dist_p = (ri - ci + W).astype(F32)
    scale = HD ** -0.5
    kpn, kp_vjp = jax.vjp(_rms, kp, knw)
    kcn, kc_vjp = jax.vjp(_rms, kc, knw)
    qn, q_vjp = jax.vjp(_rms, q, qnw)
    sc = jnp.where(mask_c, kit.nt(qn, kcn) * scale - slope * dist_c, -1e30)
    sp = jnp.where(mask_p, kit.nt(qn, kpn) * scale - slope * dist_p, -1e30)
    m = jnp.maximum(jnp.maximum(jnp.max(sc, axis=-1, keepdims=True), jnp.max(sp, axis=-1, keepdims=True)), sink)
    ec = jnp.exp(sc - m)
    ep = jnp.exp(sp - m)
    es = jnp.exp(sink - m)
    inv = 1.0 / (jnp.sum(ec, axis=-1, keepdims=True) + jnp.sum(ep, axis=-1, keepdims=True) + es)
    pc, pp = ec * inv, ep * inv
    dpc, dpp = kit.nt(do, vc), kit.nt(do, vp)
    delta = jnp.sum(dpc * pc, axis=-1, keepdims=True) + jnp.sum(dpp * pp, axis=-1, keepdims=True)
    dsc = pc * (dpc - delta) * scale
    dsp = pp * (dpp - delta) * scale
    dq, dqnw = q_vjp(kit.nn(dsc, kcn) + kit.nn(dsp, kpn))
    dkc, dknw_c = kc_vjp(kit.tn(dsc, qn))
    dkp, dknw_p = kp_vjp(kit.tn(dsp, qn))
    return dq, dkp, dkc, kit.tn(pp, do), kit.tn(pc, do), dqnw, dknw_c + dknw_p, -(es * inv) * delta


def _per_query_head(kv_ref):
    return jnp.concatenate([kv_ref[pl.ds(h // SGRP, 1)] for h in range(SQH)], axis=0)


def _per_kv_head(d):
    return jnp.concatenate([jnp.sum(d[g * SGRP:(g + 1) * SGRP], axis=0, keepdims=True) for g in range(SKVH)], axis=0)


def _swa_specs(blk):
    qspec = pl.BlockSpec((SQH, WIN, HD), lambda i: (1, blk(i), 0))
    cur = lambda grp: pl.BlockSpec((SKVH, WIN, HD), lambda i, grp=grp: (grp, blk(i), 0))
    prev = lambda grp: pl.BlockSpec((SKVH, WIN, HD), lambda i, grp=grp: (grp, jnp.maximum(blk(i) - 1, 0), 0))
    whole = pl.BlockSpec((1, HD), lambda i: (0, 0))
    col = pl.BlockSpec((SQH, WIN, 1), lambda i: (0, 0, 0))
    ospec = pl.BlockSpec((SQH, WIN, HD), lambda i: (0, blk(i), 0))
    return qspec, cur, prev, whole, col, ospec


def _swa_fwd(zs_hm, qnw, knw, sinks_col, slopes_col, o_buf, shards):
    T = zs_hm.shape[1]
    NB = T // WIN
    ns = len(shards)
    kit = _Kit(False)
    qspec, cur, prev, whole, col, _ = _swa_specs(lambda i: i)

    def body(*refs):
        q_ref, kp_ref, kc_ref, vp_ref, vc_ref, qnw_ref, knw_ref, s_ref, sl_ref = refs[:9]
        o_ref = refs[10 + ns]
        plan = _gather_plan(refs[10:10 + ns], refs[11 + ns:11 + 2 * ns], *refs[11 + 2 * ns:])

        @pl.when(pl.program_id(0) == 0)
        def _():
            _start(plan)

        first = (pl.program_id(0) == 0).astype(jnp.int32)
        o_ref[...] = _swa_heads(kit, first, q_ref[...], _per_query_head(kp_ref), _per_query_head(kc_ref),
                                _per_query_head(vp_ref), _per_query_head(vc_ref), qnw_ref[...], knw_ref[...],
                                s_ref[...], sl_ref[...])

        @pl.when(pl.program_id(0) == NB - 1)
        def _():
            _finish(plan)

    res = pl.pallas_call(
        body, name="swa_fwd", grid=(NB,),
        in_specs=[qspec, prev(8), cur(8), prev(9), cur(9), whole, whole, col, col] + _hbm_specs(1 + ns),
        out_specs=[pl.BlockSpec((SQH, WIN, HD), lambda i: (1, i, 0))] + _hbm_specs(ns),
        out_shape=[_sds(o_buf.shape)] + _gather_shapes(shards),
        input_output_aliases={9: 0},
        scratch_shapes=_gather_sems(ns),
        compiler_params=_cparams(("arbitrary",)),
    )(zs_hm, zs_hm, zs_hm, zs_hm, zs_hm, qnw, knw, sinks_col, slopes_col, o_buf, *shards)
    return res[0], res[1:]


def _swa_bwd(zs_hm, qnw, knw, sinks_col, slopes_col, dmix_hm, dproj):
    T = zs_hm.shape[1]
    NB = T // WIN
    kit = _Kit(False)
    qspec, cur, prev, whole, col, _ = _swa_specs(lambda i: NB - 1 - i)
    tail = NP - 4 * GW
    used = (SQH + 2 * SKVH) * HD

    def body(q_ref, kp_ref, kc_ref, vp_ref, vc_ref, qnw_ref, knw_ref, s_ref, sl_ref, do_ref, buf_ref,
             d_ref, dqnw_ref, dknw_ref, ds_ref, ck_scr, cv_scr):
        i = pl.program_id(0)
        first = (i == NB - 1).astype(jnp.int32)

        @pl.when(i == 0)
        def _():
            ck_scr[...] = jnp.zeros_like(ck_scr)
            cv_scr[...] = jnp.zeros_like(cv_scr)
            ds_ref[...] = jnp.zeros_like(ds_ref)
            dqnw_ref[...] = jnp.zeros_like(dqnw_ref)
            dknw_ref[...] = jnp.zeros_like(dknw_ref)

        dq, dkp, dkc, dvp, dvc, dqnw, dknw, dsink = _swa_grads(
            kit, first, q_ref[...], _per_query_head(kp_ref), _per_query_head(kc_ref), _per_query_head(vp_ref),
            _per_query_head(vc_ref), qnw_ref[...], knw_ref[...], s_ref[...], sl_ref[...], do_ref[...])
        dk = _per_kv_head(dkc) + ck_scr[...]
        dv = _per_kv_head(dvc) + cv_scr[...]
        d_ref[...] = jnp.concatenate([_merge_all(dq), _merge_all(dk), _merge_all(dv),
                                      jnp.zeros((WIN, tail - used), F32)], axis=1).astype(BF16)
        ck_scr[...] = _per_kv_head(dkp)
        cv_scr[...] = _per_kv_head(dvp)
        dqnw_ref[...] += dqnw
        dknw_ref[...] += dknw
        ds_ref[...] += jnp.broadcast_to(jnp.sum(dsink, axis=1, keepdims=True), dsink.shape)

    dospec = pl.BlockSpec((SQH, WIN, HD), lambda i: (1, NB - 1 - i, 0))
    dspec = pl.BlockSpec((WIN, tail), lambda i: (NB - 1 - i, 4 * GW // tail))
    res = pl.pallas_call(
        body, name="swa_bwd", grid=(NB,),
        in_specs=[qspec, prev(8), cur(8), prev(9), cur(9), whole, whole, col, col, dospec] + _hbm_specs(1),
        out_specs=[dspec, whole, whole, col],
        out_shape=[_sds(dproj.shape, dproj.dtype), _sds((1, HD)), _sds((1, HD)), _sds((SQH, WIN, 1))],
        input_output_aliases={10: 0},
        scratch_shapes=[pltpu.VMEM((SKVH, WIN, HD), F32), pltpu.VMEM((SKVH, WIN, HD), F32)],
        compiler_params=_cparams(("arbitrary",)),
    )(zs_hm, zs_hm, zs_hm, zs_hm, zs_hm, qnw, knw, sinks_col, slopes_col, dmix_hm, dproj)
    return res


GAB0 = 3 * GW + 1280


W_IN_ROWS = PROJ // N_CHIP
W_IN_ROWS_PAD = 736


def _shard_rows(w_sh, lo, hi):
    out = []
    for j in range(N_CHIP):
        a, b = max(lo, j * W_IN_ROWS), min(hi, (j + 1) * W_IN_ROWS)
        if a < b:
            out.append(w_sh[j, a - j * W_IN_ROWS:b - j * W_IN_ROWS])
    return out


def _permute_w_in_t(w_sh):
    gab = 4 * GW + 2 * GH
    return jnp.concatenate(_shard_rows(w_sh, 0, 4 * GW) + _shard_rows(w_sh, gab, PROJ) + _shard_rows(w_sh, 4 * GW, gab)
                           + [jnp.zeros((NP - PROJ, D), w_sh.dtype)], axis=0)


def _w_in_grad_pieces(g_t):
    g = jnp.concatenate([g_t[:4 * GW], g_t[GAB0:GAB0 + 2 * GH], g_t[4 * GW:GAB0]], axis=0)
    pad = ((0, W_IN_ROWS_PAD - W_IN_ROWS), (0, 0))
    g = jnp.stack([jnp.pad(g[j * W_IN_ROWS:(j + 1) * W_IN_ROWS], pad) for j in range(N_CHIP)])
    return g.reshape(N_CHIP, 2, W_IN_ROWS_PAD // 2, D)


def _pieces_by_rows(g):
    return g.reshape(N_CHIP, 2, g.shape[0] // (2 * N_CHIP), D)


def _local_step(x, target, mod, n1w, w_in_pt, conv_w, alog, dtb, gnw, qnw, knw, sinks, n2w, shards):
    sh_out, sh_gate, sh_up, sh_down = shards
    T = x.shape[0]
    N = T // CHUNK
    shift1, scale1, gate1, shift2, scale2, gate2 = [mod[:, i * D:(i + 1) * D] for i in range(6)]

    h, proj, zs_hm, (a_out,) = _norm_in_proj(x, n1w, scale1, shift1, w_in_pt, [sh_out])
    w_out = a_out.reshape(D, D)
    qkv_hm = _conv_fwd(proj, conv_w)
    gab = proj[:, GAB0:GAB0 + 2 * GH].T.reshape(2 * GH, N, 1, CHUNK)
    alog_b = jnp.broadcast_to(alog.reshape(GH, 1, 1), (GH, 1, CHUNK))
    dtb_b = jnp.broadcast_to(dtb.reshape(GH, 1, 1), (GH, 1, CHUNK))
    sinks_col = jnp.broadcast_to(sinks.reshape(SQH, 1, 1), (SQH, WIN, 1))
    o_hm, S_all, (a_gate, a_up) = _gdn_fwd(qkv_hm, zs_hm, gab, alog_b, dtb_b, gnw, [sh_gate, sh_up])
    w_gut = _interleave_gate_up(a_gate.reshape(DFF, D), a_up.reshape(DFF, D))
    slopes = 2.0 ** (-8.0 * (jnp.arange(SQH, dtype=F32) + 1.0) / SQH)
    slopes_col = jnp.broadcast_to(slopes.reshape(SQH, 1, 1), (SQH, WIN, 1))
    o_hm, (a_down,) = _swa_fwd(zs_hm, qnw, knw, sinks_col, slopes_col, o_hm, [sh_down])
    w_down = a_down.reshape(DFF, D)
    mixcat, mixed, x1, h2 = _out_proj_resid_norm(o_hm, w_out, x, gate1, n2w, scale2, shift2)
    ab, act = _ffn_up_act(h2, w_gut)
    dy, dffn, dgate2, loss = _ffn_down_loss(act, w_down, x1, target, gate2)

    dab = _ffn_down_dx_act(dffn, w_down, ab)
    g_w_down = _matmul(act, dffn, ta=True, out_dtype=BF16, name="ffn_down_dw")
    g_w_gut = _matmul(dab, h2, ta=True, out_dtype=BF16, name="ffn_up_dw")
    dx1, dmixed, dgate1, dn2w, dscale2, dshift2 = _ffn_up_dx_resid_bwd(dab, w_gut, x, mixed, dy, gate1, n2w, scale2,
                                                                       shift2)
    g_w_out = _matmul(mixcat, dmixed, ta=True, out_dtype=BF16, name="out_proj_dw")
    dmix_hm = _matmul_nt_heads(dmixed, w_out, "out_proj_dx")
    g_gate_t, g_up_t = _split_gate_up(g_w_gut)
    pieces = [_pieces_by_rows(g_w_out), _pieces_by_rows(g_gate_t), _pieces_by_rows(g_up_t),
              _pieces_by_rows(g_w_down)]
    (dqkv_hm, dproj, dga, dgb, dalog, ddtb, dgnw), recv = _gdn_bwd(qkv_hm, zs_hm, gab, alog_b, dtb_b, gnw, S_all,
                                                                   dmix_hm, pieces)
    dproj, dqnw, dknw, dsinks = _swa_bwd(zs_hm, qnw, knw, sinks_col, slopes_col, dmix_hm, dproj)
    dproj, dconv = _conv_bwd(proj, conv_w, dqkv_hm, dproj)
    dgab = jnp.concatenate([dga, dgb], axis=0).reshape(2 * GH, T).T.astype(BF16)
    dproj = lax.dynamic_update_slice(dproj, jnp.concatenate([dgab, jnp.zeros((T, NP - PROJ), BF16)], axis=1),
                                     (0, GAB0))
    g_w_in_pt = _matmul(dproj, h, ta=True, out_dtype=BF16, name="in_proj_dw")
    (grad_x, dn1w, dscale1, dshift1), recv_in = _in_proj_dx_norm_bwd(dproj, w_in_pt, x, dx1, n1w, scale1, shift1,
                                                                     [_w_in_grad_pieces(g_w_in_pt)])

    dmod = jnp.concatenate([dshift1, dscale1, dgate1, dshift2, dscale2, dgate2], axis=1)
    big = list(recv_in) + list(recv)
    small = dict(mod=dmod, norm1_w=dn1w, norm2_w=dn2w, conv_w=dconv, a_log=dalog[:, 0, 0], dt_bias=ddtb[:, 0, 0],
                 gdn_norm_w=dgnw, q_norm_w=dqnw, k_norm_w=dknw, sinks=dsinks[:, 0, 0])
    return loss, grad_x, big, small


def _adamw(w, g, m, v):
    m2 = ADAM_B1 * m + (1.0 - ADAM_B1) * g
    v2 = ADAM_B2 * v + (1.0 - ADAM_B2) * (g * g)
    m_hat = m2 / (1.0 - ADAM_B1 ** ADAM_STEP)
    v_hat = v2 / (1.0 - ADAM_B2 ** ADAM_STEP)
    delta = -ADAM_LR * (m_hat / (jnp.sqrt(v_hat) + ADAM_EPS) + ADAM_WD * w)
    return delta, m2, v2


def _reduce_adamw(recv, w, m, v, name):
    _, R, C = recv.shape
    tc = _tile(C, 256)

    def body(r_ref, w_ref, m_ref, v_ref, o_ref):
        g = r_ref[0].astype(F32)
        for s in range(1, N_DEV):
            g = g + r_ref[s].astype(F32)
        delta, m2, v2 = _adamw(w_ref[...], g, m_ref[...], v_ref[...])
        o_ref[0] = g
        o_ref[1] = delta
        o_ref[2] = m2
        o_ref[3] = v2

    col = pl.BlockSpec((R, tc), lambda j: (0, j))
    return pl.pallas_call(
        body, name=name, grid=(C // tc,),
        in_specs=[pl.BlockSpec((N_DEV, R, tc), lambda j: (0, 0, j)), col, col, col],
        out_specs=pl.BlockSpec((4, R, tc), lambda j: (0, 0, j)),
        out_shape=_sds((4, R, C)),
        compiler_params=_cparams(("parallel",)),
    )(recv, w, m, v)


def _adamw_call(g, w, m, v, name):
    def body(g_ref, w_ref, m_ref, v_ref, o_ref):
        delta, m2, v2 = _adamw(w_ref[...], g_ref[...], m_ref[...], v_ref[...])
        o_ref[0] = delta
        o_ref[1] = m2
        o_ref[2] = v2

    return pl.pallas_call(body, name=name, out_shape=_sds((3,) + g.shape))(g, w, m, v)


ADA_N = 6 * D // N_CHIP
KPAD = 128


def _w_ada_update(c8p, dm, w, m, v):
    tr = 256

    def body(c_ref, dm_ref, w_ref, m_ref, v_ref, g_ref, d_ref, m2_ref, v2_ref):
        g = _raw1(_silu(c_ref[...]), dm_ref[...], _TN)
        delta, m2, v2 = _adamw(w_ref[...], g, m_ref[...], v_ref[...])
        g_ref[...] = g
        d_ref[...] = delta
        m2_ref[...] = m2
        v2_ref[...] = v2

    blk = pl.BlockSpec((tr, ADA_N), lambda i: (i, 0))
    return pl.pallas_call(
        body, name="w_ada_update", grid=(D // tr,),
        in_specs=[pl.BlockSpec((KPAD, tr), lambda i: (0, i)), pl.BlockSpec((KPAD, ADA_N), lambda i: (0, 0)),
                  blk, blk, blk],
        out_specs=[blk] * 4, out_shape=[_sds((D, ADA_N))] * 4,
        compiler_params=_cparams(("parallel",)),
    )(c8p, dm, w, m, v)


def _me():
    return lax.axis_index("x"), lax.axis_index("y"), lax.axis_index("c")


def _peer(k, me):
    mx, my, mc = me
    return (1 - mx if k & 4 else mx, 1 - my if k & 2 else my, 1 - mc if k & 1 else mc)


def _lin(p):
    return 4 * p[0] + 2 * p[1] + p[2]


def _remote(src, dst, ssem, rsem, dev):
    return pltpu.make_async_remote_copy(src_ref=src, dst_ref=dst, send_sem=ssem, recv_sem=rsem,
                                        device_id=dev, device_id_type=MESH)


def _all_gather8(x, name):
    def body(x_ref, out_ref, send_sems, recv_sems):
        me = _me()
        out_ref[_lin(me)] = x_ref[...]
        sends = []
        for k in range(1, N_DEV):
            cp = _remote(x_ref, out_ref.at[_lin(me)], send_sems.at[k - 1], recv_sems.at[k - 1], _peer(k, me))
            cp.start()
            sends.append(cp)
        for k in range(1, N_DEV):
            p = _peer(k, me)
            _remote(x_ref, out_ref.at[_lin(p)], send_sems.at[k - 1], recv_sems.at[k - 1], p).wait_recv()
        for cp in sends:
            cp.wait_send()

    return pl.pallas_call(
        body, name=name,
        out_shape=_sds((N_DEV,) + x.shape, x.dtype),
        in_specs=[pl.BlockSpec(memory_space=pltpu.VMEM)],
        out_specs=pl.BlockSpec(memory_space=pltpu.VMEM),
        scratch_shapes=[pltpu.SemaphoreType.DMA((N_DEV - 1,)), pltpu.SemaphoreType.DMA((N_DEV - 1,))],
    )(x)


def _ag8_plan(src, out, send_sems, recv_sems):
    me = _me()
    sends, recvs = [], []
    for k in range(1, N_DEV):
        p = _peer(k, me)
        sends.append(_remote(src, out.at[_lin(me)], send_sems.at[k - 1], recv_sems.at[k - 1], p))
        recvs.append(_remote(src, out.at[_lin(p)], send_sems.at[k - 1], recv_sems.at[k - 1], p))
    return [], sends, recvs


def _prologue(c_row, conv_sh, w_ada, b_sh, w_in_sh):
    def body(c_ref, cv_ref, wa_ref, b_ref, win_ref, call_ref, cvall_ref, mods_ref, ain_ref, c16_scr, mp_scr,
             c_send, c_recv, cv_send, cv_recv, m_send, m_recv, w_send, w_recv, w_local):
        me = _lin(_me())
        w_plan = _gather_half_plan([win_ref], [ain_ref], w_send, w_recv, w_local)
        _start(w_plan)
        c_plan = _ag8_plan(c_ref, call_ref, c_send, c_recv)
        cv_plan = _ag8_plan(cv_ref, cvall_ref, cv_send, cv_recv)
        call_ref[me] = c_ref[...]
        cvall_ref[me] = cv_ref[...]
        _start(c_plan)
        _start(cv_plan)
        _finish(c_plan)
        c16_scr[...] = jnp.zeros_like(c16_scr)
        for d in range(N_DEV):
            c16_scr[pl.ds(d, 1), :] = call_ref[d]
        mp_scr[...] = _raw1(_silu(c16_scr[...]), wa_ref[...], _NN) + b_ref[...]
        mods_ref[me] = mp_scr[...]
        m_plan = _ag8_plan(mp_scr, mods_ref, m_send, m_recv)
        _start(m_plan)
        _finish(cv_plan)
        _finish(m_plan)
        _finish(w_plan)

    vmem = pl.BlockSpec(memory_space=pltpu.VMEM)
    sems = lambda n: pltpu.SemaphoreType.DMA((n,))
    return pl.pallas_call(
        body, name="prologue",
        in_specs=[vmem] * 4 + _hbm_specs(1), out_specs=[vmem] * 3 + _hbm_specs(1),
        out_shape=[_sds((N_DEV,) + c_row.shape), _sds((N_DEV,) + conv_sh.shape), _sds((N_DEV, 16, ADA_N)),
                   _sds((N_CHIP,) + w_in_sh.shape, w_in_sh.dtype)],
        scratch_shapes=[pltpu.VMEM((16, D), F32), pltpu.VMEM((16, ADA_N), F32)] + [sems(N_DEV - 1)] * 6
                       + _gather_sems(1),
        compiler_params=_cparams(),
    )(c_row, conv_sh, w_ada, b_sh, w_in_sh)


def _hbm_specs(n):
    return [pl.BlockSpec(memory_space=pl.ANY)] * n


def _gather_shapes(shards):
    return [_sds((N_CHIP,) + s.shape, s.dtype) for s in shards]


def _gather_sems(n):
    return [pltpu.SemaphoreType.DMA((3 * n,)), pltpu.SemaphoreType.DMA((3 * n,)), pltpu.SemaphoreType.DMA((n,))]


def _gather_plan(ins, outs, send_sems, recv_sems, local_sems):
    mx, my, mc = _me()
    chips = [(1 - mx, my), (mx, 1 - my), (1 - mx, 1 - my)]
    local, sends, recvs = [], [], []
    for a in range(len(ins)):
        local.append(pltpu.make_async_copy(ins[a], outs[a].at[2 * mx + my], local_sems.at[a]))
        for k, (px, py) in enumerate(chips):
            sems = (send_sems.at[3 * a + k], recv_sems.at[3 * a + k], (px, py, mc))
            sends.append(_remote(ins[a], outs[a].at[2 * mx + my], *sems))
            recvs.append(_remote(ins[a], outs[a].at[2 * px + py], *sems))
    return local, sends, recvs


def _gather_half_plan(ins, outs, send_sems, recv_sems, local_sems):
    mx, my, mc = _me()
    chips = [(1 - mx, my), (mx, 1 - my), (1 - mx, 1 - my)]
    local, sends, recvs = [], [], []
    for a in range(len(ins)):
        h = ins[a].shape[0] // 2
        mine = pl.ds(pl.multiple_of(mc * h, 16), h)
        local.append(pltpu.make_async_copy(ins[a], outs[a].at[2 * mx + my], local_sems.at[a]))
        for k, (px, py) in enumerate(chips):
            sems = (send_sems.at[3 * a + k], recv_sems.at[3 * a + k], (px, py, mc))
            sends.append(_remote(ins[a].at[mine], outs[a].at[2 * mx + my, mine], *sems))
            recvs.append(_remote(ins[a].at[mine], outs[a].at[2 * px + py, mine], *sems))
    return local, sends, recvs


def _sibling_fill(pieces):
    h = pieces.shape[1] // 2

    def body(p_ref, o_ref, send_sems, recv_sems):
        mx, my, mc = _me()
        sib = (mx, my, 1 - mc)
        chips = [(1 - mx, my), (mx, 1 - my), (1 - mx, 1 - my)]
        half = lambda c: pl.ds(pl.multiple_of(c * h, 16), h)
        o_ref[2 * mx + my] = p_ref[2 * mx + my]
        sends = []
        for k, (px, py) in enumerate(chips):
            j = 2 * px + py
            o_ref[j, half(mc), :] = p_ref[j, half(mc), :]
            cp = _remote(p_ref.at[j, half(mc)], o_ref.at[j, half(mc)], send_sems.at[k], recv_sems.at[k], sib)
            cp.start()
            sends.append(cp)
        for k, (px, py) in enumerate(chips):
            j = 2 * px + py
            _remote(p_ref.at[j, half(mc)], o_ref.at[j, half(1 - mc)], send_sems.at[k], recv_sems.at[k],
                    sib).wait_recv()
        for cp in sends:
            cp.wait_send()

    vmem = pl.BlockSpec(memory_space=pltpu.VMEM)
    return pl.pallas_call(
        body, name="sibling_fill", out_shape=_sds(pieces.shape, pieces.dtype),
        in_specs=[vmem], out_specs=vmem,
        scratch_shapes=[pltpu.SemaphoreType.DMA((N_CHIP - 1,)), pltpu.SemaphoreType.DMA((N_CHIP - 1,))],
        compiler_params=_cparams(),
    )(pieces)


def _start(plan):
    local, sends, _ = plan
    for cp in local + sends:
        cp.start()


def _finish(plan):
    local, sends, recvs = plan
    for cp in recvs:
        cp.wait_recv()
    for cp in sends:
        cp.wait_send()
    for cp in local:
        cp.wait()


def _exchange_shapes(pieces):
    return [_sds((N_DEV,) + p.shape[2:], p.dtype) for p in pieces]


def _exchange_sems(n):
    return [pltpu.SemaphoreType.DMA(((N_DEV - 1) * n,)), pltpu.SemaphoreType.DMA(((N_DEV - 1) * n,)),
            pltpu.SemaphoreType.DMA((n,))]


def _exchange_plan(ins, outs, send_sems, recv_sems, local_sems):
    me = _me()
    mx, my, mc = me
    local, sends, recvs = [], [], []
    for a in range(len(ins)):
        local.append(pltpu.make_async_copy(ins[a].at[2 * mx + my, mc], outs[a].at[_lin(me)], local_sems.at[a]))
        for k in range(1, N_DEV):
            p = _peer(k, me)
            s = (N_DEV - 1) * a + k - 1
            sends.append(_remote(ins[a].at[2 * p[0] + p[1], p[2]], outs[a].at[_lin(me)], send_sems.at[s],
                                 recv_sems.at[s], p))
            recvs.append(_remote(ins[a].at[2 * mx + my, mc], outs[a].at[_lin(p)], send_sems.at[s],
                                 recv_sems.at[s], p))
    return local, sends, recvs


REDUCE_VMEM = 56 * 1024 * 1024


def _reduce_swap(recvs):
    n = len(recvs)

    def body(*refs):
        r_refs, o_refs = refs[:n], refs[n:2 * n]
        send_sems, recv_sems = refs[2 * n:]
        mx, my, mc = _me()
        sib = (mx, my, 1 - mc)
        half = lambda a, c: o_refs[a].at[pl.ds(pl.multiple_of(c * recvs[a].shape[1], 8), recvs[a].shape[1])]
        sends = []
        for a in range(n):
            g = r_refs[a][0].astype(F32)
            for s in range(1, N_DEV):
                g = g + r_refs[a][s].astype(F32)
            half(a, mc)[...] = g
            cp = _remote(half(a, mc), half(a, mc), send_sems.at[a], recv_sems.at[a], sib)
            cp.start()
            sends.append(cp)
        for a in range(n):
            _remote(half(a, mc), half(a, 1 - mc), send_sems.at[a], recv_sems.at[a], sib).wait_recv()
        for cp in sends:
            cp.wait_send()

    vmem = pl.BlockSpec(memory_space=pltpu.VMEM)
    return pl.pallas_call(
        body, name="reduce_swap", out_shape=[_sds((2 * r.shape[1], r.shape[2])) for r in recvs],
        in_specs=[vmem] * n, out_specs=[vmem] * n,
        scratch_shapes=[pltpu.SemaphoreType.DMA((n,)), pltpu.SemaphoreType.DMA((n,))],
        compiler_params=_cparams(None, REDUCE_VMEM),
    )(*recvs)


def _adamw_big(g, w, m, v, name):
    rows, cols = g.shape
    tr = next((t for t in (256, 176, 128, 64, 8) if rows % t == 0), None)
    if tr is None:
        tc = _tile(cols, 256)
        blk, grid = pl.BlockSpec((rows, tc), lambda i: (0, i)), (cols // tc,)
    else:
        blk, grid = pl.BlockSpec((tr, cols), lambda i: (i, 0)), (rows // tr,)

    def body(g_ref, w_ref, m_ref, v_ref, go_ref, d_ref, m2_ref, v2_ref):
        g = g_ref[...]
        delta, m2, v2 = _adamw(w_ref[...], g, m_ref[...], v_ref[...])
        go_ref[...] = g
        d_ref[...] = delta
        m2_ref[...] = m2
        v2_ref[...] = v2

    return pl.pallas_call(
        body, name=name, grid=grid,
        in_specs=[blk] * 4, out_specs=[blk] * 4, out_shape=[_sds((rows, cols))] * 4,
        compiler_params=_cparams(("parallel",)),
    )(g, w, m, v)


SMALL_ORDER = (("mod", 6 * D), ("norm1_w", D), ("norm2_w", D), ("conv_w", CONVW * 3 * GW), ("a_log", GH),
               ("dt_bias", GH), ("gdn_norm_w", HD), ("q_norm_w", HD), ("k_norm_w", HD), ("sinks", SQH), ("loss", 1))
SMALL_R = 120


def _pack_small(d):
    parts = [d[k].reshape(-1).astype(F32) if k in d else jnp.zeros((n,), F32) for k, n in SMALL_ORDER]
    used = sum(n for _, n in SMALL_ORDER)
    parts.append(jnp.zeros((SMALL_R * LANE - used,), F32))
    return jnp.concatenate(parts).reshape(SMALL_R, LANE)


def _unpack_small(pk):
    flat = pk.reshape(-1)
    out, r = {}, 0
    for k, n in SMALL_ORDER:
        out[k] = flat[r:r + n]
        r += n
    return out


def kernel(x, c, w_ada, b_ada, norm1_w, w_in, conv_w, a_log, dt_bias, gdn_norm_w, q_norm_w, k_norm_w, sinks, w_out, norm2_w, w_gate, w_up, w_down, loss_target, m_w_ada, m_b_ada, m_norm1_w, m_w_in, m_conv_w, m_a_log, m_dt_bias, m_gdn_norm_w, m_q_norm_w, m_k_norm_w, m_sinks, m_w_out, m_norm2_w, m_w_gate, m_w_up, m_w_down, v_w_ada, v_b_ada, v_norm1_w, v_w_in, v_conv_w, v_a_log, v_dt_bias, v_gdn_norm_w, v_q_norm_w, v_k_norm_w, v_sinks, v_w_out, v_norm2_w, v_w_gate, v_w_up, v_w_down):
    mx, my, mc = _me()
    chip = 2 * mx + my
    dev = 4 * mx + 2 * my + mc
    T = x.shape[1]

    as_rows = lambda t, transposed: t[0].T if transposed else t[0]
    transposed = (True, False, True, True, False)
    big_w = [as_rows(t, tr) for t, tr in zip((w_in, w_out, w_gate, w_up, w_down), transposed)]
    shards = [t.astype(BF16) for t in big_w]

    b_sh = lax.dynamic_slice(b_ada, (0, chip * ADA_N), (1, ADA_N))
    w_in_sh = jnp.pad(shards[0], ((0, W_IN_ROWS_PAD - W_IN_ROWS), (0, 0)))
    c_all, conv_all, mods, a_in = _prologue(c, conv_w.reshape(CONVW, 3 * GW // N_CHIP), w_ada[0], b_sh, w_in_sh)
    c8 = c_all.reshape(N_DEV, D)
    conv_full = jnp.concatenate([conv_all[2 * j] for j in range(N_CHIP)], axis=1)
    mod = jnp.concatenate([lax.dynamic_slice(mods[2 * j], (dev, 0), (1, ADA_N)) for j in range(N_CHIP)], axis=1)
    w_in_pt = _permute_w_in_t(_sibling_fill(a_in))

    loss, grad_x, big, small = _local_step(
        x[0], loss_target[0], mod, norm1_w, w_in_pt, conv_full, a_log, dt_bias, gdn_norm_w,
        q_norm_w, k_norm_w, sinks, norm2_w, shards[1:])

    small["loss"] = loss[:, :1]
    sg = _all_gather8(_pack_small(small), "gather_small_grads")
    rep = dict(mod=(b_ada, m_b_ada, v_b_ada), norm1_w=(norm1_w, m_norm1_w, v_norm1_w),
               norm2_w=(norm2_w, m_norm2_w, v_norm2_w), a_log=(a_log, m_a_log, v_a_log),
               dt_bias=(dt_bias, m_dt_bias, v_dt_bias), gdn_norm_w=(gdn_norm_w, m_gdn_norm_w, v_gdn_norm_w),
               q_norm_w=(q_norm_w, m_q_norm_w, v_q_norm_w), k_norm_w=(k_norm_w, m_k_norm_w, v_k_norm_w),
               sinks=(sinks, m_sinks, v_sinks))
    wmv = [_pack_small({k: t[i] for k, t in rep.items()}) for i in range(3)]
    sres = _reduce_adamw(sg, wmv[0], wmv[1], wmv[2], "small_reduce_adamw")
    s_g, s_d, s_m, s_v = [_unpack_small(sres[i]) for i in range(4)]
    loss_out = s_g["loss"][0]

    g_conv = lax.dynamic_slice(s_g["conv_w"].reshape(CONVW, 3 * GW), (0, chip * (3 * GW // N_CHIP)),
                               (CONVW, 3 * GW // N_CHIP))
    pad16 = lambda t: jnp.concatenate([t.reshape(12, LANE), jnp.zeros((4, LANE), F32)], axis=0)
    cres = _adamw_call(pad16(g_conv), pad16(conv_w), pad16(m_conv_w), pad16(v_conv_w), "conv_adamw")
    conv_out = [g_conv.reshape(conv_w.shape)] + [cres[i, :12].reshape(conv_w.shape) for i in range(3)]

    dmod8 = sg[:, :6 * D // LANE].reshape(N_DEV, 6 * D)
    dm = lax.dynamic_slice(dmod8, (0, chip * ADA_N), (N_DEV, ADA_N))
    zpad = lambda t: jnp.concatenate([t, jnp.zeros((KPAD - N_DEV, t.shape[1]), F32)], axis=0)
    ares = _w_ada_update(zpad(c8), zpad(dm), w_ada[0], m_w_ada[0], v_w_ada[0])

    names = ("w_in", "w_out", "w_gate", "w_up", "w_down")
    g_full = list(_reduce_swap(big))
    g_full[0] = g_full[0][:W_IN_ROWS]
    big_m = [as_rows(t, tr) for t, tr in zip((m_w_in, m_w_out, m_w_gate, m_w_up, m_w_down), transposed)]
    big_v = [as_rows(t, tr) for t, tr in zip((v_w_in, v_w_out, v_w_gate, v_w_up, v_w_down), transposed)]
    upd = [_adamw_big(g, w, m, v, "adamw_" + nm) for g, w, m, v, nm in zip(g_full, big_w, big_m, big_v, names)]
    back = lambda t, tr: (t.T if tr else t)[None]
    bg, bd, bm, bv = [[back(u[i], tr) for u, tr in zip(upd, transposed)] for i in range(4)]

    def group(a_i, small_d, conv_i, big_l):
        s = lambda k, ref: small_d[k].reshape(ref.shape)
        return [ares[a_i][None], s("mod", b_ada), s("norm1_w", norm1_w), big_l[0], conv_out[conv_i],
                s("a_log", a_log), s("dt_bias", dt_bias), s("gdn_norm_w", gdn_norm_w), s("q_norm_w", q_norm_w),
                s("k_norm_w", k_norm_w), s("sinks", sinks), big_l[1], s("norm2_w", norm2_w), big_l[2], big_l[3],
                big_l[4]]

    outs = [loss_out, grad_x[None]]
    outs += group(0, s_g, 0, bg) + group(1, s_d, 1, bd) + group(2, s_m, 2, bm) + group(3, s_v, 3, bv)
    return tuple(outs)
```

```python
import jax
import jax.numpy as jnp
from jax import lax
from jax.experimental import pallas as pl
from jax.experimental.pallas import tpu as pltpu

F32 = jnp.float32
BF16 = jnp.bfloat16
MESH = pl.DeviceIdType.MESH

D = 1024
HD = 64
GH = 8
GW = GH * HD
SQH = 8
SKVH = 2
SGRP = SQH // SKVH
WIN = 128
CONVW = 4
CHUNK = 64
DFF = 2816
PROJ = 2832
NP = 3072
EPS = 1e-6
N_DEV = 8
N_CHIP = 4

ADAM_LR = 0.001
ADAM_B1 = 0.9
ADAM_B2 = 0.999
ADAM_EPS = 1e-08
ADAM_WD = 0.01
ADAM_STEP = 10

VMEM_LIMIT = 48 * 1024 * 1024
GDN_BWD_VMEM = 58 * 1024 * 1024
LANE = 128


def _cparams(sem=None, vmem=VMEM_LIMIT):
    return pltpu.CompilerParams(dimension_semantics=sem, vmem_limit_bytes=vmem)


_NN = ((1,), (0,))
_NT = ((1,), (1,))
_TN = ((0,), (0,))


def _dot(a, b, dims):
    if a.ndim == 3:
        (ca,), (cb,) = dims
        return lax.dot_general(a, b, (((ca + 1,), (cb + 1,)), ((0,), (0,))), preferred_element_type=F32)
    return lax.dot_general(a, b, (dims, ((), ())), preferred_element_type=F32)


def _raw1(a, b, dims):
    return _dot(a.astype(BF16), b.astype(BF16), dims)


def _raw3(a, b, dims):
    ah = a.astype(BF16)
    al = (a - ah.astype(F32)).astype(BF16)
    bh = b.astype(BF16)
    bl = (b - bh.astype(F32)).astype(BF16)
    return _dot(ah, bh, dims) + (_dot(al, bh, dims) + _dot(ah, bl, dims))


def _make_diff_mm(raw):
    @jax.custom_vjp
    def nn(a, b):
        return raw(a, b, _NN)

    @jax.custom_vjp
    def nt(a, b):
        return raw(a, b, _NT)

    @jax.custom_vjp
    def tn(a, b):
        return raw(a, b, _TN)

    nn.defvjp(lambda a, b: (raw(a, b, _NN), (a, b)), lambda r, g: (nt(g, r[1]), tn(r[0], g)))
    nt.defvjp(lambda a, b: (raw(a, b, _NT), (a, b)), lambda r, g: (nn(g, r[1]), tn(g, r[0])))
    tn.defvjp(lambda a, b: (raw(a, b, _TN), (a, b)), lambda r, g: (nt(r[1], g), nn(r[0], g)))
    return nn, nt, tn


def _tri_inv_raw(a, nn3):
    n = a.shape[-1]
    ri = lax.broadcasted_iota(jnp.int32, (n, n), 0)
    ci = lax.broadcasted_iota(jnp.int32, (n, n), 1)
    t = (ri == ci).astype(F32)
    for lvl in range((n - 1).bit_length()):
        same_pair = (ri >> (lvl + 1)) == (ci >> (lvl + 1))
        lower_left = (((ri >> lvl) & 1) == 1) & (((ci >> lvl) & 1) == 0)
        y = jnp.where(same_pair & lower_left, a, 0.0)
        t = t - y if lvl == 0 else t - nn3(nn3(t, y), t)
    return t


class _Kit:
    def __init__(self, diff):
        if diff:
            self.nn, self.nt, self.tn = _make_diff_mm(_raw1)
            self.nn3, self.nt3, self.tn3 = _make_diff_mm(_raw3)
            nn3, nt3, tn3 = self.nn3, self.nt3, self.tn3

            @jax.custom_vjp
            def inv(a, t):
                return t

            def inv_fwd(a, t):
                return t, t

            def inv_bwd(t, g):
                return -tn3(t, nt3(g, t)), jnp.zeros_like(t)

            inv.defvjp(inv_fwd, inv_bwd)
            self.inv = inv
        else:
            self.nn = lambda a, b: _raw1(a, b, _NN)
            self.nt = lambda a, b: _raw1(a, b, _NT)
            self.tn = lambda a, b: _raw1(a, b, _TN)
            self.nn3 = lambda a, b: _raw3(a, b, _NN)
            self.nt3 = lambda a, b: _raw3(a, b, _NT)
            self.tn3 = lambda a, b: _raw3(a, b, _TN)
            self.inv = lambda a, t: _tri_inv_raw(a, self.nn3) if t is None else t


def _sigmoid(x):
    return 1.0 / (1.0 + jnp.exp(-x))


def _silu(x):
    return x * _sigmoid(x)


def _rms(x, w):
    return x * lax.rsqrt(jnp.mean(x * x, axis=-1, keepdims=True) + EPS) * w


def _tile(dim, target):
    t = (min(dim, target) // LANE) * LANE
    while t >= LANE:
        if dim % t == 0:
            return t
        t -= LANE
    return dim


MM_TM, MM_TN, MM_TK = 1408, 1536, 1408


def _matmul(a, b, ta=False, tb=False, out_dtype=F32, name="matmul", gather=None, exchange=None):
    carried = gather if gather is not None else exchange if exchange is not None else []
    nc = len(carried)
    if ta:
        K, M = a.shape
    else:
        M, K = a.shape
    if tb:
        N, K2 = b.shape
    else:
        K2, N = b.shape
    assert K == K2, (a.shape, b.shape, ta, tb)
    tm, tn, tk = _tile(M, MM_TM), _tile(N, MM_TN), _tile(K, MM_TK)
    nk = K // tk
    dims = ((0,) if ta else (1,), (1,) if tb else (0,))

    grid = (M // tm, N // tn, nk)

    def body(*refs):
        a_ref, b_ref = refs[:2]
        o_ref = refs[2 + nc]
        scratch = refs[3 + 2 * nc:]
        k = pl.program_id(2)
        if nc:
            make_plan = _gather_plan if gather is not None else _exchange_plan
            plan = make_plan(refs[2:2 + nc], refs[3 + nc:3 + 2 * nc], *scratch[-3:])
            at = lambda pos: ((pl.program_id(0) == pos[0]) & (pl.program_id(1) == pos[1]) & (k == pos[2]))

            @pl.when(at((0, 0, 0)))
            def _():
                _start(plan)

        part = _dot(a_ref[...].astype(BF16), b_ref[...].astype(BF16), dims)
        if nk == 1:
            o_ref[...] = part.astype(o_ref.dtype)
        else:
            acc_ref = scratch[0]

            @pl.when(k == 0)
            def _():
                acc_ref[...] = part

            @pl.when((k > 0) & (k < nk - 1))
            def _():
                acc_ref[...] += part

            @pl.when(k == nk - 1)
            def _():
                o_ref[...] = (acc_ref[...] + part).astype(o_ref.dtype)

        if nc:
            @pl.when(at((grid[0] - 1, grid[1] - 1, nk - 1)))
            def _():
                _finish(plan)

    a_spec = (pl.BlockSpec((tk, tm), lambda i, j, k: (k, i)) if ta
              else pl.BlockSpec((tm, tk), lambda i, j, k: (i, k)))
    b_spec = (pl.BlockSpec((tn, tk), lambda i, j, k: (j, k)) if tb
              else pl.BlockSpec((tk, tn), lambda i, j, k: (k, j)))
    if gather is not None:
        c_shapes, c_sems = _gather_shapes(carried), _gather_sems(nc)
    elif exchange is not None:
        c_shapes, c_sems = _exchange_shapes(carried), _exchange_sems(nc)
    else:
        c_shapes, c_sems = [], []
    res = pl.pallas_call(
        body, name=name, grid=grid,
        in_specs=[a_spec, b_spec] + _hbm_specs(nc),
        out_specs=[pl.BlockSpec((tm, tn), lambda i, j, k: (i, j))] + _hbm_specs(nc),
        out_shape=[jax.ShapeDtypeStruct((M, N), out_dtype)] + c_shapes,
        scratch_shapes=([pltpu.VMEM((tm, tn), F32)] if nk > 1 else []) + c_sems,
        compiler_params=_cparams(("arbitrary",) * 3 if nc else ("parallel", "parallel", "arbitrary")),
    )(a, b, *carried)
    return (res[0], res[1:]) if nc else res[0]


def _sds(shape, dtype=F32):
    return jax.ShapeDtypeStruct(shape, dtype)


def _norm_mod(x, nw, scale, shift):
    return _rms(x, nw) * (1.0 + scale) + shift


IN_PROJ_VMEM = 56 * 1024 * 1024


def _norm_in_proj(x, nw, scale, shift, w_in_pt, shards):
    T = x.shape[0]
    N = w_in_pt.shape[0]
    tm, tn = _tile(T, 1024), 3 * GW
    nm, nn = T // tm, N // tn
    nz = (GAB0 - 3 * GW) // HD
    ns = len(shards)

    def body(*refs):
        x_ref, nw_ref, sc_ref, sh_ref, w_ref = refs[:5]
        h_ref, o_ref, zs_ref = refs[5 + ns:8 + ns]
        plan = _gather_plan(refs[5:5 + ns], refs[8 + ns:8 + 2 * ns], *refs[8 + 2 * ns:])
        i, j = pl.program_id(0), pl.program_id(1)

        @pl.when((i == 0) & (j == 0))
        def _():
            _start(plan)

        @pl.when(j == 0)
        def _():
            for r0 in range(0, tm, ROWS_EPI):
                rows = pl.ds(r0, ROWS_EPI)
                h_ref[rows, :] = _norm_mod(x_ref[rows, :], nw_ref[...], sc_ref[...], sh_ref[...]).astype(BF16)

        o = _dot(h_ref[...], w_ref[...], _NT)
        o_ref[...] = o

        @pl.when(j == 1)
        def _():
            for p in range(nz // 2):
                zs_ref[2 * p], zs_ref[2 * p + 1] = _split_pair(o[:, p * LANE:(p + 1) * LANE])

        @pl.when((i == nm - 1) & (j == nn - 1))
        def _():
            _finish(plan)

    vec = pl.BlockSpec((1, D), lambda i, j: (0, 0))
    res = pl.pallas_call(
        body, name="norm1_in_proj", grid=(nm, nn),
        in_specs=[pl.BlockSpec((tm, D), lambda i, j: (i, 0)), vec, vec, vec,
                  pl.BlockSpec((tn, D), lambda i, j: (j, 0))] + _hbm_specs(ns),
        out_specs=[pl.BlockSpec((tm, D), lambda i, j: (i, 0)), pl.BlockSpec((tm, tn), lambda i, j: (i, j)),
                   pl.BlockSpec((nz, tm, HD), lambda i, j: (0, i, 0))] + _hbm_specs(ns),
        out_shape=[_sds((T, D), BF16), _sds((T, N)), _sds((nz, T, HD))] + _gather_shapes(shards),
        scratch_shapes=_gather_sems(ns),
        compiler_params=_cparams(("arbitrary", "arbitrary"), IN_PROJ_VMEM),
    )(x, nw, scale, shift, w_in_pt, *shards)
    return res[0], res[1], res[2], res[3:]


ROWS_TM = 512
ROWS_EPI = 256


def _matmul_rows(a, b, epi, tiled, consts, out_tiled, out_acc, name, pieces=()):
    T, K = a.shape
    tm, tk = _tile(T, ROWS_TM), _tile(K, MM_TK)
    nm, nk = T // tm, K // tk
    npc, nt, ncst, no, na = len(pieces), len(tiled), len(consts), len(out_tiled), len(out_acc)
    n_in = 2 + nt + ncst

    def body(*refs):
        a_ref, b_ref = refs[:2]
        t_refs, c_refs = refs[2:2 + nt], refs[2 + nt:n_in]
        o_refs = refs[n_in + npc:n_in + npc + no]
        acc_refs = refs[n_in + npc + no:n_in + npc + no + na]
        n_out = no + na + npc
        res_ref = refs[n_in + npc + n_out]
        plan = _exchange_plan(refs[n_in:n_in + npc], refs[n_in + npc + no + na:n_in + npc + n_out],
                              *refs[n_in + npc + n_out + 1:]) if npc else None
        i, k = pl.program_id(0), pl.program_id(1)

        @pl.when((i == 0) & (k == 0))
        def _():
            for r in acc_refs:
                r[...] = jnp.zeros_like(r)
            if npc:
                _start(plan)

        part = _dot(a_ref[...], b_ref[pl.ds(pl.multiple_of(k * tk, tk), tk), :], _NN)

        @pl.when(k == 0)
        def _():
            res_ref[...] = part

        @pl.when(k > 0)
        def _():
            res_ref[...] += part

        @pl.when(k == nk - 1)
        def _():
            for r0 in range(0, tm, ROWS_EPI):
                rows = pl.ds(r0, ROWS_EPI)
                outs = epi(res_ref[rows, :], *[r[rows, :] for r in t_refs], *[r[...] for r in c_refs])
                for r, v in zip(o_refs, outs[:no]):
                    r[rows, :] = v.astype(r.dtype)
                for r, v in zip(acc_refs, outs[no:]):
                    r[...] += v

        if npc:
            @pl.when((i == nm - 1) & (k == nk - 1))
            def _():
                _finish(plan)

    row = lambda w: pl.BlockSpec((tm, w), lambda i, k: (i, 0))
    whole = lambda s: pl.BlockSpec(s.shape, lambda i, k: (0, 0))
    res = pl.pallas_call(
        body, name=name, grid=(nm, nk),
        in_specs=[pl.BlockSpec((tm, tk), lambda i, k: (i, k)),
                  pl.BlockSpec((K, D), lambda i, k: (0, 0), pipeline_mode=pl.Buffered(1))]
                 + [row(t.shape[1]) for t in tiled] + [whole(c) for c in consts] + _hbm_specs(npc),
        out_specs=[row(s.shape[1]) for s in out_tiled] + [whole(s) for s in out_acc] + _hbm_specs(npc),
        out_shape=list(out_tiled) + list(out_acc) + (_exchange_shapes(pieces) if npc else []),
        scratch_shapes=[pltpu.VMEM((tm, D), F32)] + (_exchange_sems(npc) if npc else []),
        compiler_params=_cparams(("arbitrary", "arbitrary")),
    )(a, b, *tiled, *consts, *pieces)
    return res[:no + na], res[no + na:]


def _in_proj_dx_norm_bwd(dproj, w_in_pt, x, dres, nw, scale, shift, pieces):
    T = x.shape[0]

    def epi(dh, x, dres, nw, scale, shift):
        _, vjp = jax.vjp(_norm_mod, x, nw, scale, shift)
        dx, dnw, dsc, dsh = vjp(dh)
        return dx + dres, dnw, dsc, dsh

    return _matmul_rows(dproj, w_in_pt, epi, [x, dres], [nw, scale, shift], [_sds((T, D))], [_sds((1, D))] * 3,
                        "in_proj_dx_norm1_bwd", pieces)


def _out_proj_resid_norm(o_hm, w_out, x, gate1, nw, scale, shift):
    T = x.shape[0]
    nheads = o_hm.shape[0]
    tm = _tile(T, ROWS_TM)

    def body(o_ref, w_ref, x_ref, g_ref, nw_ref, sc_ref, sh_ref, cat_ref, mixed_ref, x1_ref, h2_ref):
        cat = jnp.concatenate([_merge_pair(o_ref[2 * p], o_ref[2 * p + 1]) for p in range(nheads // 2)], axis=1)
        cat_ref[...] = cat.astype(BF16)
        mixed_ref[...] = _dot(cat_ref[...], w_ref[...], _NN)
        for r0 in range(0, tm, ROWS_EPI):
            rows = pl.ds(r0, ROWS_EPI)
            x1, h2 = _resid_norm(x_ref[rows, :], mixed_ref[rows, :], g_ref[...], nw_ref[...], sc_ref[...], sh_ref[...])
            x1_ref[rows, :] = x1
            h2_ref[rows, :] = h2.astype(BF16)

    row = pl.BlockSpec((tm, D), lambda i: (i, 0))
    vec = pl.BlockSpec((1, D), lambda i: (0, 0))
    return pl.pallas_call(
        body, name="out_proj_resid_norm2", grid=(T // tm,),
        in_specs=[pl.BlockSpec((nheads, tm, HD), lambda i: (0, i, 0)), pl.BlockSpec((D, D), lambda i: (0, 0)), row,
                  vec, vec, vec, vec],
        out_specs=[row, row, row, row],
        out_shape=[_sds((T, D), BF16), _sds((T, D)), _sds((T, D)), _sds((T, D), BF16)],
        compiler_params=_cparams(("parallel",)),
    )(o_hm, w_out, x, gate1, nw, scale, shift)


def _ffn_up_dx_resid_bwd(dab, w_gut, x, mixed, dy, gate1, nw, scale, shift):
    T = x.shape[0]

    def epi(dh2, x, mixed, dy, gate1, nw, scale, shift):
        _, vjp = jax.vjp(_resid_norm, x, mixed, gate1, nw, scale, shift)
        return vjp((dy, dh2))

    outs, _ = _matmul_rows(dab, w_gut, epi, [x, mixed, dy], [gate1, nw, scale, shift],
                           [_sds((T, D)), _sds((T, D), BF16)], [_sds((1, D))] * 4, "ffn_up_dx_resid_norm2_bwd")
    return outs


def _ffn_down_loss(act, w_down, x1, target, gate2):
    T = x1.shape[0]

    def epi(ffn, x1, target, gate2):
        y = x1 + gate2 * ffn
        err = y - target
        loss = 0.5 * jnp.sum(jnp.sum(err * err, axis=1, keepdims=True), axis=0, keepdims=True) / D
        dy = err * (1.0 / D)
        return dy, gate2 * dy, jnp.sum(dy * ffn, axis=0, keepdims=True), jnp.broadcast_to(loss, (1, LANE))

    outs, _ = _matmul_rows(act, w_down, epi, [x1, target], [gate2], [_sds((T, D)), _sds((T, D), BF16)],
                           [_sds((1, D)), _sds((1, LANE))], "ffn_down_loss")
    return outs


def _resid_norm(x, mixed, gate1, nw, scale, shift):
    x1 = x + gate1 * mixed
    return x1, _norm_mod(x1, nw, scale, shift)


FFN_BLK = 256
FFN_TM = 2048
AB_SLOTS = 3


def _interleave_gate_up(gate_t, up_t):
    blocks = lambda t: t.reshape(DFF // FFN_BLK, 1, FFN_BLK, D)
    return jnp.concatenate([blocks(gate_t), blocks(up_t)], axis=1).reshape(2 * DFF, D)


def _split_gate_up(g):
    g = g.reshape(DFF // FFN_BLK, 2, FFN_BLK, D)
    return g[:, 0].reshape(DFF, D), g[:, 1].reshape(DFF, D)


def _ffn_up_act(h2, w_gut):
    T = h2.shape[0]
    tm = _tile(T, FFN_TM)

    def body(h_ref, w_ref, ab_ref, act_ref):
        ab = _dot(h_ref[...], w_ref[...], _NT)
        ab_ref[...] = ab
        act_ref[...] = (_silu(ab[:, :FFN_BLK]) * ab[:, FFN_BLK:]).astype(act_ref.dtype)

    return pl.pallas_call(
        body, name="ffn_up_act", grid=(T // tm, DFF // FFN_BLK),
        in_specs=[pl.BlockSpec((tm, D), lambda i, j: (i, 0)), pl.BlockSpec((2 * FFN_BLK, D), lambda i, j: (j, 0))],
        out_specs=[pl.BlockSpec((tm, 2 * FFN_BLK), lambda i, j: (i, j)), pl.BlockSpec((tm, FFN_BLK), lambda i, j: (i, j))],
        out_shape=[_sds((T, 2 * DFF)), _sds((T, DFF), BF16)],
        compiler_params=_cparams(("parallel", "parallel")),
    )(h2, w_gut)


def _ffn_down_dx_act(dffn, w_down, ab):
    T = dffn.shape[0]
    tm = _tile(T, FFN_TM)
    nj = DFF // FFN_BLK
    steps = (T // tm) * nj
    assert steps >= AB_SLOTS

    def body(d_ref, w_ref, ab_hbm, o_ref, buf, sem):
        step = pl.program_id(0) * nj + pl.program_id(1)

        def fetch(t):
            rows = pl.ds((t // nj) * tm, tm)
            cols = pl.ds((t % nj) * (2 * FFN_BLK), 2 * FFN_BLK)
            return pltpu.make_async_copy(ab_hbm.at[rows, cols], buf.at[t % AB_SLOTS], sem.at[t % AB_SLOTS])

        @pl.when(step == 0)
        def _():
            for t in range(AB_SLOTS - 1):
                fetch(t).start()

        @pl.when(step + AB_SLOTS - 1 < steps)
        def _():
            fetch(step + AB_SLOTS - 1).start()

        dact = _dot(d_ref[...], w_ref[...], _NT)
        fetch(step).wait()
        ab_ref = buf.at[step % AB_SLOTS]
        a, b = ab_ref[:, :FFN_BLK], ab_ref[:, FFN_BLK:]
        s = _sigmoid(a)
        da = dact * b * (s * (1.0 + a * (1.0 - s)))
        db = dact * (a * s)
        o_ref[...] = jnp.concatenate([da, db], axis=1).astype(o_ref.dtype)

    return pl.pallas_call(
        body, name="ffn_down_dx_act", grid=(T // tm, nj),
        in_specs=[pl.BlockSpec((tm, D), lambda i, j: (i, 0)), pl.BlockSpec((FFN_BLK, D), lambda i, j: (j, 0))]
        + _hbm_specs(1),
        out_specs=pl.BlockSpec((tm, 2 * FFN_BLK), lambda i, j: (i, j)),
        out_shape=_sds((T, 2 * DFF), BF16),
        scratch_shapes=[pltpu.VMEM((AB_SLOTS, tm, 2 * FFN_BLK), F32), pltpu.SemaphoreType.DMA((AB_SLOTS,))],
        compiler_params=_cparams(("arbitrary", "arbitrary")),
    )(dffn, w_down, ab)


def _round_bf16(x):
    return x.astype(BF16).astype(F32)


def _shift_down(x, s, rows):
    if s == 0:
        return x
    return jnp.where(rows >= s, pltpu.roll(x, s, 0), 0.0)


def _shift_up(x, s, rows, T):
    if s == 0:
        return x
    return jnp.where(rows < T - s, pltpu.roll(x, T - s, 0), 0.0)


def _conv_fwd(proj, conv_w):
    T = proj.shape[0]
    ncol = 3 * GW // LANE

    def body(x_ref, w_ref, o_ref):
        x = _round_bf16(x_ref[...])
        rows = lax.broadcasted_iota(jnp.int32, x.shape, 0)
        acc = jnp.zeros_like(x)
        for j in range(CONVW):
            acc = acc + _round_bf16(w_ref[pl.ds(j, 1), :]) * _shift_down(x, CONVW - 1 - j, rows)
        o_ref[0], o_ref[1] = _split_pair(_silu(acc))

    return pl.pallas_call(
        body, name="conv_fwd", grid=(ncol,),
        in_specs=[pl.BlockSpec((T, LANE), lambda j: (0, j)), pl.BlockSpec((CONVW, LANE), lambda j: (0, j))],
        out_specs=pl.BlockSpec((2, T, HD), lambda j: (j, 0, 0)),
        out_shape=_sds((3 * GH, T, HD)),
        compiler_params=_cparams(("parallel",)),
    )(proj, conv_w)


def _split_pair(y):
    return y[:, :HD], pltpu.roll(y, HD, 1)[:, :HD]


def _merge_pair(a, b):
    return jnp.concatenate([a, b], axis=1)


def _merge_all(heads):
    return jnp.concatenate([_merge_pair(heads[2 * p], heads[2 * p + 1]) for p in range(heads.shape[0] // 2)], axis=1)


def _matmul_nt_heads(a, b, name):
    T, K = a.shape
    N = b.shape[0]
    tm = _tile(T, 1024)

    def body(a_ref, b_ref, o_ref):
        res = _dot(a_ref[...], b_ref[...], _NT)
        for p in range(N // LANE):
            o_ref[2 * p], o_ref[2 * p + 1] = _split_pair(res[:, p * LANE:(p + 1) * LANE])

    return pl.pallas_call(
        body, name=name, grid=(T // tm,),
        in_specs=[pl.BlockSpec((tm, K), lambda i: (i, 0)), pl.BlockSpec((N, K), lambda i: (0, 0))],
        out_specs=pl.BlockSpec((N // HD, tm, HD), lambda i: (0, i, 0)),
        out_shape=_sds((N // HD, T, HD)),
        compiler_params=_cparams(("parallel",)),
    )(a, b)


def _conv_bwd(proj, conv_w, dqc, dproj):
    T = proj.shape[0]
    ncol = 3 * GW // LANE

    def body(x_ref, w_ref, d_ref, buf_ref, dx_ref, dw_ref):
        x = _round_bf16(x_ref[...])
        rows = lax.broadcasted_iota(jnp.int32, x.shape, 0)
        xs = [_shift_down(x, CONVW - 1 - j, rows) for j in range(CONVW)]
        w = [_round_bf16(w_ref[pl.ds(j, 1), :]) for j in range(CONVW)]
        pre = jnp.zeros_like(x)
        for j in range(CONVW):
            pre = pre + w[j] * xs[j]
        s = _sigmoid(pre)
        dpre = _round_bf16(_merge_pair(d_ref[0], d_ref[1]) * (s * (1.0 + pre * (1.0 - s))))
        dx = jnp.zeros_like(x)
        for j in range(CONVW):
            dx = dx + w[j] * _shift_up(dpre, CONVW - 1 - j, rows, T)
            dw_ref[pl.ds(j, 1), :] = jnp.sum(dpre * xs[j], axis=0, keepdims=True)
        dx_ref[...] = dx.astype(dx_ref.dtype)

    return pl.pallas_call(
        body, name="conv_bwd", grid=(ncol,),
        in_specs=[pl.BlockSpec((T, LANE), lambda j: (0, j)), pl.BlockSpec((CONVW, LANE), lambda j: (0, j)),
                  pl.BlockSpec((2, T, HD), lambda j: (j, 0, 0))] + _hbm_specs(1),
        out_specs=[pl.BlockSpec((T, LANE), lambda j: (0, j)), pl.BlockSpec((CONVW, LANE), lambda j: (0, j))],
        out_shape=[_sds(dproj.shape, dproj.dtype), _sds((CONVW, 3 * GW))],
        input_output_aliases={3: 0},
        compiler_params=_cparams(("parallel",)),
    )(proj, conv_w, dqc, dproj)


def _gdn_prep(kit, q, k, v, ga, gb, alog, dtb, t_inv=None):
    C = CHUNK
    ri = lax.broadcasted_iota(jnp.int32, (C, C), 0)
    ci = lax.broadcasted_iota(jnp.int32, (C, C), 1)
    causal = ri >= ci
    strict = ri > ci
    eye = (ri == ci).astype(F32)
    lower = causal.astype(F32)
    upper = (ri <= ci).astype(F32)

    a = ga + dtb
    softplus = jnp.maximum(a, 0.0) + jnp.log(1.0 + jnp.exp(-jnp.abs(a)))
    g_row = -jnp.exp(alog) * softplus
    beta_row = _sigmoid(gb)
    g_col = jnp.sum(eye * g_row, axis=2, keepdims=True)
    beta_col = jnp.sum(eye * beta_row, axis=2, keepdims=True)
    G_col = jnp.sum(lower * g_row, axis=2, keepdims=True)
    G_row = jnp.sum(upper * g_col, axis=1, keepdims=True)
    G_last = jnp.sum(g_row, axis=2, keepdims=True)
    decay = jnp.exp(jnp.where(causal, G_col - G_row, -1e30))

    qn = q * lax.rsqrt(jnp.sum(q * q, axis=-1, keepdims=True) + EPS) * (HD ** -0.5)
    kn = k * lax.rsqrt(jnp.sum(k * k, axis=-1, keepdims=True) + EPS)
    kb = kn * beta_col
    A = jnp.where(strict, kit.nt(kb, kn) * decay, 0.0)
    Tm = kit.inv(A, t_inv)
    eG = jnp.exp(G_col)
    u = kit.nn3(Tm, v * beta_col)
    w = kit.nn3(Tm, kb * eG)
    qk = jnp.where(causal, kit.nt(qn, kn) * decay, 0.0)
    q_dec = qn * eG
    k_dec = kn * jnp.exp(G_last - G_col)
    dec = jnp.exp(G_last)
    return u, w, qk, q_dec, k_dec, dec, Tm


def _gdn_out(o, z, nw):
    return _rms(o, nw) * _silu(z)


GDN_CB = 4


def _gdn_specs(T, blk):
    TB = GDN_CB * CHUNK
    seq = lambda grp: pl.BlockSpec((GH, TB, HD), lambda i, grp=grp: (grp, blk(i), 0))
    row = lambda grp: pl.BlockSpec((GH, GDN_CB, 1, CHUNK), lambda i, grp=grp: (grp, blk(i), 0, 0))
    per_head = pl.BlockSpec((GH, 1, CHUNK), lambda i: (0, 0, 0))
    whole = pl.BlockSpec((1, HD), lambda i: (0, 0))
    state = pl.BlockSpec((GH, GDN_CB, HD, HD), lambda i: (0, blk(i), 0, 0))
    return seq, row, per_head, whole, state


def _gdn_load(seq_refs, row_refs, head_refs):
    chunks = lambda r: jnp.concatenate([r[:, pl.ds(cb * CHUNK, CHUNK), :] for cb in range(GDN_CB)], axis=0)
    rows = lambda r: jnp.concatenate([r[:, cb] for cb in range(GDN_CB)], axis=0)
    heads = lambda r: jnp.concatenate([r[...]] * GDN_CB, axis=0)
    return [chunks(r) for r in seq_refs], [rows(r) for r in row_refs], [heads(r) for r in head_refs]


def _gdn_fwd(qkv_hm, zs_hm, gab, alog_b, dtb_b, nw, shards):
    T = qkv_hm.shape[1]
    N = T // CHUNK
    nblk = N // GDN_CB
    ns = len(shards)
    seq, row, per_head, whole, state = _gdn_specs(T, lambda i: i)
    kit = _Kit(False)

    def body(*refs):
        q_ref, k_ref, v_ref, z_ref, ga_ref, gb_ref, al_ref, dt_ref, nw_ref = refs[:9]
        o_ref, S_ref, T_ref = refs[9 + ns:12 + ns]
        S_scr = refs[12 + 2 * ns]
        plan = _gather_plan(refs[9:9 + ns], refs[12 + ns:12 + 2 * ns], *refs[13 + 2 * ns:])

        @pl.when(pl.program_id(0) == 0)
        def _():
            S_scr[...] = jnp.zeros_like(S_scr)
            _start(plan)

        (q, k, v, z), (ga, gb), (al, dt) = _gdn_load((q_ref, k_ref, v_ref, z_ref), (ga_ref, gb_ref), (al_ref, dt_ref))
        u, w, qk, q_dec, k_dec, dec, t_inv = _gdn_prep(kit, q, k, v, ga, gb, al, dt)
        S = S_scr[...]
        for cb in range(GDN_CB):
            hs = slice(cb * GH, (cb + 1) * GH)
            S_ref[:, cb] = S
            T_ref[:, cb] = t_inv[hs]
            v_new = u[hs] - kit.nn(w[hs], S)
            o = kit.nn(q_dec[hs], S) + kit.nn(qk[hs], v_new)
            S = S * dec[hs] + kit.tn(k_dec[hs], v_new)
            o_ref[:, pl.ds(cb * CHUNK, CHUNK), :] = _gdn_out(o, z[hs], nw_ref[...])
        S_scr[...] = S

        @pl.when(pl.program_id(0) == nblk - 1)
        def _():
            _finish(plan)

    res = pl.pallas_call(
        body, name="gdn_fwd", grid=(nblk,),
        in_specs=[seq(0), seq(1), seq(2), seq(0), row(0), row(1), per_head, per_head, whole] + _hbm_specs(ns),
        out_specs=[seq(0), state, state] + _hbm_specs(ns),
        out_shape=[_sds((GH + SQH, T, HD)), _sds((GH, N, HD, HD)), _sds((GH, N, CHUNK, CHUNK))]
                  + _gather_shapes(shards),
        scratch_shapes=[pltpu.VMEM((GH, HD, HD), F32)] + _gather_sems(ns),
        compiler_params=_cparams(("arbitrary",)),
    )(qkv_hm, qkv_hm, qkv_hm, zs_hm, gab, gab, alog_b, dtb_b, nw, *shards)
    return res[0], (res[1], res[2]), res[3:]


def _gdn_bwd(qkv_hm, zs_hm, gab, alog_b, dtb_b, nw, S_all, do, pieces):
    T = qkv_hm.shape[1]
    N = T // CHUNK
    nblk = N // GDN_CB
    npc = len(pieces)
    dkit, kit = _Kit(True), _Kit(False)
    rseq, rrow, per_head, whole, rstate = _gdn_specs(T, lambda i: nblk - 1 - i)

    def body(*refs):
        q_ref, k_ref, v_ref, z_ref, ga_ref, gb_ref, al_ref, dt_ref, nw_ref, S_ref, T_ref, do_ref = refs[:12]
        dqkv_ref, dz_ref, dga_ref, dgb_ref, dal_ref, ddt_ref, dnw_ref = refs[12 + npc:19 + npc]
        dS_scr = refs[19 + 2 * npc]
        plan = _exchange_plan(refs[12:12 + npc], refs[19 + npc:19 + 2 * npc], *refs[20 + 2 * npc:])

        @pl.when(pl.program_id(0) == 0)
        def _():
            dS_scr[...] = jnp.zeros_like(dS_scr)
            dal_ref[...] = jnp.zeros_like(dal_ref)
            ddt_ref[...] = jnp.zeros_like(ddt_ref)
            dnw_ref[...] = jnp.zeros_like(dnw_ref)
            _start(plan)

        (q, k, v, z, dout), (ga, gb), (al, dt) = _gdn_load((q_ref, k_ref, v_ref, z_ref, do_ref), (ga_ref, gb_ref),
                                                          (al_ref, dt_ref))
        S_in = jnp.concatenate([S_ref[:, cb] for cb in range(GDN_CB)], axis=0)
        t_inv = jnp.concatenate([T_ref[:, cb] for cb in range(GDN_CB)], axis=0)
        prep = lambda *a: _gdn_prep(dkit, *a, t_inv=t_inv)[:6]
        (u, w, qk, q_dec, k_dec, dec), prep_vjp = jax.vjp(prep, q, k, v, ga, gb, al, dt)
        v_new = u - kit.nn(w, S_in)
        o = kit.nn(q_dec, S_in) + kit.nn(qk, v_new)
        _, out_vjp = jax.vjp(_gdn_out, o, z, nw_ref[...])
        do, dz, dnw = out_vjp(dout)
        dvn_part = kit.tn(qk, do)
        dS_part = kit.tn(q_dec, do)
        dS = dS_scr[...]
        dS_out, dvn = [None] * GDN_CB, [None] * GDN_CB
        for cb in reversed(range(GDN_CB)):
            hs = slice(cb * GH, (cb + 1) * GH)
            dS_out[cb] = dS
            dvn[cb] = dvn_part[hs] + kit.nn(k_dec[hs], dS)
            dS = dS * dec[hs] + dS_part[hs] - kit.tn(w[hs], dvn[cb])
        dS_scr[...] = dS
        dS_out = jnp.concatenate(dS_out, axis=0)
        dvn = jnp.concatenate(dvn, axis=0)
        ddec = jnp.sum(jnp.sum(S_in * dS_out, axis=2, keepdims=True), axis=1, keepdims=True)
        cts = (dvn, -kit.nt(dvn, S_in), kit.nt(do, v_new), kit.nt(do, S_in), kit.nt(v_new, dS_out), ddec)
        dq, dk, dv, dga, dgb, dal, ddt = prep_vjp(cts)
        lanesum = lambda t: jnp.broadcast_to(jnp.sum(t, axis=2, keepdims=True), t.shape)
        for cb in range(GDN_CB):
            hs = slice(cb * GH, (cb + 1) * GH)
            sl = pl.ds(cb * CHUNK, CHUNK)
            dqkv_ref[pl.ds(0, GH), sl, :] = dq[hs]
            dqkv_ref[pl.ds(GH, GH), sl, :] = dk[hs]
            dqkv_ref[pl.ds(2 * GH, GH), sl, :] = dv[hs]
            dz_ref[sl, :] = _merge_all(dz[hs]).astype(BF16)
            dga_ref[:, cb] = dga[hs]
            dgb_ref[:, cb] = dgb[hs]
            dal_ref[...] += lanesum(dal[hs])
            ddt_ref[...] += lanesum(ddt[hs])
        dnw_ref[...] += dnw

        @pl.when(pl.program_id(0) == nblk - 1)
        def _():
            _finish(plan)

    res = pl.pallas_call(
        body, name="gdn_bwd", grid=(nblk,),
        in_specs=[rseq(0), rseq(1), rseq(2), rseq(0), rrow(0), rrow(1), per_head, per_head, whole, rstate, rstate,
                  rseq(0)] + _hbm_specs(npc),
        out_specs=[pl.BlockSpec((3 * GH, GDN_CB * CHUNK, HD), lambda i: (0, nblk - 1 - i, 0)),
                   pl.BlockSpec((GDN_CB * CHUNK, GW), lambda i: (nblk - 1 - i, 3)), rrow(0),
                   rrow(0), per_head, per_head, whole] + _hbm_specs(npc),
        out_shape=[_sds((3 * GH, T, HD)), _sds((T, NP), BF16)] + [_sds((GH, N, 1, CHUNK))] * 2
                  + [_sds((GH, 1, CHUNK))] * 2 + [_sds((1, HD))] + _exchange_shapes(pieces),
        scratch_shapes=[pltpu.VMEM((GH, HD, HD), F32)] + _exchange_sems(npc),
        compiler_params=_cparams(("arbitrary",), GDN_BWD_VMEM),
    )(qkv_hm, qkv_hm, qkv_hm, zs_hm, gab, gab, alog_b, dtb_b, nw, S_all[0], S_all[1], do, *pieces)
    return res[:7], res[7:]


def _swa_heads(kit, first, q, kp, kc, vp, vc, qnw, knw, sink, slope):
    W = WIN
    ri = lax.broadcasted_iota(jnp.int32, (W, W), 0)
    ci = lax.broadcasted_iota(jnp.int32, (W, W), 1)
    mask_c = ri >= ci
    mask_p = ci > ri + first * W
    dist_c = (ri - ci).astype(F32)
    dist_p = (ri - ci + W).astype(F32)
    kpn = _rms(kp, knw)
    kcn = _rms(kc, knw)
    qn = _rms(q, qnw)
    sc = jnp.where(mask_c, kit.nt(qn, kcn) * (HD ** -0.5) - slope * dist_c, -1e30)
    sp = jnp.where(mask_p, kit.nt(qn, kpn) * (HD ** -0.5) - slope * dist_p, -1e30)
    m = jnp.maximum(jnp.maximum(jnp.max(sc, axis=-1, keepdims=True), jnp.max(sp, axis=-1, keepdims=True)), sink)
    m = lax.stop_gradient(m)
    pc = jnp.exp(sc - m)
    pp = jnp.exp(sp - m)
    den = jnp.sum(pc, axis=-1, keepdims=True) + jnp.sum(pp, axis=-1, keepdims=True) + jnp.exp(sink - m)
    inv = 1.0 / den
    return kit.nn(pc * inv, vc) + kit.nn(pp * inv, vp)


def _swa_grads(kit, first, q, kp, kc, vp, vc, qnw, knw, sink, slope, do):
    W = WIN
    ri = lax.broadcasted_iota(jnp.int32, (W, W), 0)
    ci = lax.broadcasted_iota(jnp.int32, (W, W), 1)
    mask_c = ri >= ci
    mask_p = ci > ri + first * W
    dist_c = (ri - ci).astype(F32)
    dist_p = (ri - ci + W).astype(F32)
    scale = HD ** -0.5
    kpn, kp_vjp = jax.vjp(_rms, kp, knw)
    kcn, kc_vjp = jax.vjp(_rms, kc, knw)
    qn, q_vjp = jax.vjp(_rms, q, qnw)
    sc = jnp.where(mask_c, kit.nt(qn, kcn) * scale - slope * dist_c, -1e30)
    sp = jnp.where(mask_p, kit.nt(qn, kpn) * scale - slope * dist_p, -1e30)
    m = jnp.maximum(jnp.maximum(jnp.max(sc, axis=-1, keepdims=True), jnp.max(sp, axis=-1, keepdims=True)), sink)
    ec = jnp.exp(sc - m)
    ep = jnp.exp(sp - m)
    es = jnp.exp(sink - m)
    inv = 1.0 / (jnp.sum(ec, axis=-1, keepdims=True) + jnp.sum(ep, axis=-1, keepdims=True) + es)
    pc, pp = ec * inv, ep * inv
    dpc, dpp = kit.nt(do, vc), kit.nt(do, vp)
    delta = jnp.sum(dpc * pc, axis=-1, keepdims=True) + jnp.sum(dpp * pp, axis=-1, keepdims=True)
    dsc = pc * (dpc - delta) * scale
    dsp = pp * (dpp - delta) * scale
    dq, dqnw = q_vjp(kit.nn(dsc, kcn) + kit.nn(dsp, kpn))
    dkc, dknw_c = kc_vjp(kit.tn(dsc, qn))
    dkp, dknw_p = kp_vjp(kit.tn(dsp, qn))
    return dq, dkp, dkc, kit.tn(pp, do), kit.tn(pc, do), dqnw, dknw_c + dknw_p, -(es * inv) * delta


def _per_query_head(kv_ref):
    return jnp.concatenate([kv_ref[pl.ds(h // SGRP, 1)] for h in range(SQH)], axis=0)


def _per_kv_head(d):
    return jnp.concatenate([jnp.sum(d[g * SGRP:(g + 1) * SGRP], axis=0, keepdims=True) for g in range(SKVH)], axis=0)


def _swa_specs(blk):
    qspec = pl.BlockSpec((SQH, WIN, HD), lambda i: (1, blk(i), 0))
    cur = lambda grp: pl.BlockSpec((SKVH, WIN, HD), lambda i, grp=grp: (grp, blk(i), 0))
    prev = lambda grp: pl.BlockSpec((SKVH, WIN, HD), lambda i, grp=grp: (grp, jnp.maximum(blk(i) - 1, 0), 0))
    whole = pl.BlockSpec((1, HD), lambda i: (0, 0))
    col = pl.BlockSpec((SQH, WIN, 1), lambda i: (0, 0, 0))
    ospec = pl.BlockSpec((SQH, WIN, HD), lambda i: (0, blk(i), 0))
    return qspec, cur, prev, whole, col, ospec


def _swa_fwd(zs_hm, qnw, knw, sinks_col, slopes_col, o_buf, shards):
    T = zs_hm.shape[1]
    NB = T // WIN
    ns = len(shards)
    kit = _Kit(False)
    qspec, cur, prev, whole, col, _ = _swa_specs(lambda i: i)

    def body(*refs):
        q_ref, kp_ref, kc_ref, vp_ref, vc_ref, qnw_ref, knw_ref, s_ref, sl_ref = refs[:9]
        o_ref = refs[10 + ns]
        plan = _gather_plan(refs[10:10 + ns], refs[11 + ns:11 + 2 * ns], *refs[11 + 2 * ns:])

        @pl.when(pl.program_id(0) == 0)
        def _():
            _start(plan)

        first = (pl.program_id(0) == 0).astype(jnp.int32)
        o_ref[...] = _swa_heads(kit, first, q_ref[...], _per_query_head(kp_ref), _per_query_head(kc_ref),
                                _per_query_head(vp_ref), _per_query_head(vc_ref), qnw_ref[...], knw_ref[...],
                                s_ref[...], sl_ref[...])

        @pl.when(pl.program_id(0) == NB - 1)
        def _():
            _finish(plan)

    res = pl.pallas_call(
        body, name="swa_fwd", grid=(NB,),
        in_specs=[qspec, prev(8), cur(8), prev(9), cur(9), whole, whole, col, col] + _hbm_specs(1 + ns),
        out_specs=[pl.BlockSpec((SQH, WIN, HD), lambda i: (1, i, 0))] + _hbm_specs(ns),
        out_shape=[_sds(o_buf.shape)] + _gather_shapes(shards),
        input_output_aliases={9: 0},
        scratch_shapes=_gather_sems(ns),
        compiler_params=_cparams(("arbitrary",)),
    )(zs_hm, zs_hm, zs_hm, zs_hm, zs_hm, qnw, knw, sinks_col, slopes_col, o_buf, *shards)
    return res[0], res[1:]


def _swa_bwd(zs_hm, qnw, knw, sinks_col, slopes_col, dmix_hm, dproj):
    T = zs_hm.shape[1]
    NB = T // WIN
    kit = _Kit(False)
    qspec, cur, prev, whole, col, _ = _swa_specs(lambda i: NB - 1 - i)
    tail = NP - 4 * GW
    used = (SQH + 2 * SKVH) * HD

    def body(q_ref, kp_ref, kc_ref, vp_ref, vc_ref, qnw_ref, knw_ref, s_ref, sl_ref, do_ref, buf_ref,
             d_ref, dqnw_ref, dknw_ref, ds_ref, ck_scr, cv_scr):
        i = pl.program_id(0)
        first = (i == NB - 1).astype(jnp.int32)

        @pl.when(i == 0)
        def _():
            ck_scr[...] = jnp.zeros_like(ck_scr)
            cv_scr[...] = jnp.zeros_like(cv_scr)
            ds_ref[...] = jnp.zeros_like(ds_ref)
            dqnw_ref[...] = jnp.zeros_like(dqnw_ref)
            dknw_ref[...] = jnp.zeros_like(dknw_ref)

        dq, dkp, dkc, dvp, dvc, dqnw, dknw, dsink = _swa_grads(
            kit, first, q_ref[...], _per_query_head(kp_ref), _per_query_head(kc_ref), _per_query_head(vp_ref),
            _per_query_head(vc_ref), qnw_ref[...], knw_ref[...], s_ref[...], sl_ref[...], do_ref[...])
        dk = _per_kv_head(dkc) + ck_scr[...]
        dv = _per_kv_head(dvc) + cv_scr[...]
        d_ref[...] = jnp.concatenate([_merge_all(dq), _merge_all(dk), _merge_all(dv),
                                      jnp.zeros((WIN, tail - used), F32)], axis=1).astype(BF16)
        ck_scr[...] = _per_kv_head(dkp)
        cv_scr[...] = _per_kv_head(dvp)
        dqnw_ref[...] += dqnw
        dknw_ref[...] += dknw
        ds_ref[...] += jnp.broadcast_to(jnp.sum(dsink, axis=1, keepdims=True), dsink.shape)

    dospec = pl.BlockSpec((SQH, WIN, HD), lambda i: (1, NB - 1 - i, 0))
    dspec = pl.BlockSpec((WIN, tail), lambda i: (NB - 1 - i, 4 * GW // tail))
    res = pl.pallas_call(
        body, name="swa_bwd", grid=(NB,),
        in_specs=[qspec, prev(8), cur(8), prev(9), cur(9), whole, whole, col, col, dospec] + _hbm_specs(1),
        out_specs=[dspec, whole, whole, col],
        out_shape=[_sds(dproj.shape, dproj.dtype), _sds((1, HD)), _sds((1, HD)), _sds((SQH, WIN, 1))],
        input_output_aliases={10: 0},
        scratch_shapes=[pltpu.VMEM((SKVH, WIN, HD), F32), pltpu.VMEM((SKVH, WIN, HD), F32)],
        compiler_params=_cparams(("arbitrary",)),
    )(zs_hm, zs_hm, zs_hm, zs_hm, zs_hm, qnw, knw, sinks_col, slopes_col, dmix_hm, dproj)
    return res


GAB0 = 3 * GW + 1280


W_IN_ROWS = PROJ // N_CHIP
W_IN_ROWS_PAD = 736


def _shard_rows(w_sh, lo, hi):
    out = []
    for j in range(N_CHIP):
        a, b = max(lo, j * W_IN_ROWS), min(hi, (j + 1) * W_IN_ROWS)
        if a < b:
            out.append(w_sh[j, a - j * W_IN_ROWS:b - j * W_IN_ROWS])
    return out


def _permute_w_in_t(w_sh):
    gab = 4 * GW + 2 * GH
    return jnp.concatenate(_shard_rows(w_sh, 0, 4 * GW) + _shard_rows(w_sh, gab, PROJ) + _shard_rows(w_sh, 4 * GW, gab)
                           + [jnp.zeros((NP - PROJ, D), w_sh.dtype)], axis=0)


def _w_in_grad_pieces(g_t):
    g = jnp.concatenate([g_t[:4 * GW], g_t[GAB0:GAB0 + 2 * GH], g_t[4 * GW:GAB0]], axis=0)
    pad = ((0, W_IN_ROWS_PAD - W_IN_ROWS), (0, 0))
    g = jnp.stack([jnp.pad(g[j * W_IN_ROWS:(j + 1) * W_IN_ROWS], pad) for j in range(N_CHIP)])
    return g.reshape(N_CHIP, 2, W_IN_ROWS_PAD // 2, D)


def _pieces_by_rows(g):
    return g.reshape(N_CHIP, 2, g.shape[0] // (2 * N_CHIP), D)


def _local_step(x, target, mod, n1w, w_in_pt, conv_w, alog, dtb, gnw, qnw, knw, sinks, n2w, shards):
    sh_out, sh_gate, sh_up, sh_down = shards
    T = x.shape[0]
    N = T // CHUNK
    shift1, scale1, gate1, shift2, scale2, gate2 = [mod[:, i * D:(i + 1) * D] for i in range(6)]

    h, proj, zs_hm, (a_out,) = _norm_in_proj(x, n1w, scale1, shift1, w_in_pt, [sh_out])
    w_out = a_out.reshape(D, D)
    qkv_hm = _conv_fwd(proj, conv_w)
    gab = proj[:, GAB0:GAB0 + 2 * GH].T.reshape(2 * GH, N, 1, CHUNK)
    alog_b = jnp.broadcast_to(alog.reshape(GH, 1, 1), (GH, 1, CHUNK))
    dtb_b = jnp.broadcast_to(dtb.reshape(GH, 1, 1), (GH, 1, CHUNK))
    sinks_col = jnp.broadcast_to(sinks.reshape(SQH, 1, 1), (SQH, WIN, 1))
    o_hm, S_all, (a_gate, a_up) = _gdn_fwd(qkv_hm, zs_hm, gab, alog_b, dtb_b, gnw, [sh_gate, sh_up])
    w_gut = _interleave_gate_up(a_gate.reshape(DFF, D), a_up.reshape(DFF, D))
    slopes = 2.0 ** (-8.0 * (jnp.arange(SQH, dtype=F32) + 1.0) / SQH)
    slopes_col = jnp.broadcast_to(slopes.reshape(SQH, 1, 1), (SQH, WIN, 1))
    o_hm, (a_down,) = _swa_fwd(zs_hm, qnw, knw, sinks_col, slopes_col, o_hm, [sh_down])
    w_down = a_down.reshape(DFF, D)
    mixcat, mixed, x1, h2 = _out_proj_resid_norm(o_hm, w_out, x, gate1, n2w, scale2, shift2)
    ab, act = _ffn_up_act(h2, w_gut)
    dy, dffn, dgate2, loss = _ffn_down_loss(act, w_down, x1, target, gate2)

    dab = _ffn_down_dx_act(dffn, w_down, ab)
    g_w_down = _matmul(act, dffn, ta=True, out_dtype=BF16, name="ffn_down_dw")
    g_w_gut = _matmul(dab, h2, ta=True, out_dtype=BF16, name="ffn_up_dw")
    dx1, dmixed, dgate1, dn2w, dscale2, dshift2 = _ffn_up_dx_resid_bwd(dab, w_gut, x, mixed, dy, gate1, n2w, scale2,
                                                                       shift2)
    g_w_out = _matmul(mixcat, dmixed, ta=True, out_dtype=BF16, name="out_proj_dw")
    dmix_hm = _matmul_nt_heads(dmixed, w_out, "out_proj_dx")
    g_gate_t, g_up_t = _split_gate_up(g_w_gut)
    pieces = [_pieces_by_rows(g_w_out), _pieces_by_rows(g_gate_t), _pieces_by_rows(g_up_t),
              _pieces_by_rows(g_w_down)]
    (dqkv_hm, dproj, dga, dgb, dalog, ddtb, dgnw), recv = _gdn_bwd(qkv_hm, zs_hm, gab, alog_b, dtb_b, gnw, S_all,
                                                                   dmix_hm, pieces)
    dproj, dqnw, dknw, dsinks = _swa_bwd(zs_hm, qnw, knw, sinks_col, slopes_col, dmix_hm, dproj)
    dproj, dconv = _conv_bwd(proj, conv_w, dqkv_hm, dproj)
    dgab = jnp.concatenate([dga, dgb], axis=0).reshape(2 * GH, T).T.astype(BF16)
    dproj = lax.dynamic_update_slice(dproj, jnp.concatenate([dgab, jnp.zeros((T, NP - PROJ), BF16)], axis=1),
                                     (0, GAB0))
    g_w_in_pt = _matmul(dproj, h, ta=True, out_dtype=BF16, name="in_proj_dw")
    (grad_x, dn1w, dscale1, dshift1), recv_in = _in_proj_dx_norm_bwd(dproj, w_in_pt, x, dx1, n1w, scale1, shift1,
                                                                     [_w_in_grad_pieces(g_w_in_pt)])

    dmod = jnp.concatenate([dshift1, dscale1, dgate1, dshift2, dscale2, dgate2], axis=1)
    big = list(recv_in) + list(recv)
    small = dict(mod=dmod, norm1_w=dn1w, norm2_w=dn2w, conv_w=dconv, a_log=dalog[:, 0, 0], dt_bias=ddtb[:, 0, 0],
                 gdn_norm_w=dgnw, q_norm_w=dqnw, k_norm_w=dknw, sinks=dsinks[:, 0, 0])
    return loss, grad_x, big, small


def _adamw(w, g, m, v):
    m2 = ADAM_B1 * m + (1.0 - ADAM_B1) * g
    v2 = ADAM_B2 * v + (1.0 - ADAM_B2) * (g * g)
    m_hat = m2 / (1.0 - ADAM_B1 ** ADAM_STEP)
    v_hat = v2 / (1.0 - ADAM_B2 ** ADAM_STEP)
    delta = -ADAM_LR * (m_hat / (jnp.sqrt(v_hat) + ADAM_EPS) + ADAM_WD * w)
    return delta, m2, v2


def _reduce_adamw(recv, w, m, v, name):
    _, R, C = recv.shape
    tc = _tile(C, 256)

    def body(r_ref, w_ref, m_ref, v_ref, o_ref):
        g = r_ref[0].astype(F32)
        for s in range(1, N_DEV):
            g = g + r_ref[s].astype(F32)
        delta, m2, v2 = _adamw(w_ref[...], g, m_ref[...], v_ref[...])
        o_ref[0] = g
        o_ref[1] = delta
        o_ref[2] = m2
        o_ref[3] = v2

    col = pl.BlockSpec((R, tc), lambda j: (0, j))
    return pl.pallas_call(
        body, name=name, grid=(C // tc,),
        in_specs=[pl.BlockSpec((N_DEV, R, tc), lambda j: (0, 0, j)), col, col, col],
        out_specs=pl.BlockSpec((4, R, tc), lambda j: (0, 0, j)),
        out_shape=_sds((4, R, C)),
        compiler_params=_cparams(("parallel",)),
    )(recv, w, m, v)


def _adamw_call(g, w, m, v, name):
    def body(g_ref, w_ref, m_ref, v_ref, o_ref):
        delta, m2, v2 = _adamw(w_ref[...], g_ref[...], m_ref[...], v_ref[...])
        o_ref[0] = delta
        o_ref[1] = m2
        o_ref[2] = v2

    return pl.pallas_call(body, name=name, out_shape=_sds((3,) + g.shape))(g, w, m, v)


ADA_N = 6 * D // N_CHIP
KPAD = 128


def _w_ada_update(c8p, dm, w, m, v):
    tr = 256

    def body(c_ref, dm_ref, w_ref, m_ref, v_ref, g_ref, d_ref, m2_ref, v2_ref):
        g = _raw1(_silu(c_ref[...]), dm_ref[...], _TN)
        delta, m2, v2 = _adamw(w_ref[...], g, m_ref[...], v_ref[...])
        g_ref[...] = g
        d_ref[...] = delta
        m2_ref[...] = m2
        v2_ref[...] = v2

    blk = pl.BlockSpec((tr, ADA_N), lambda i: (i, 0))
    return pl.pallas_call(
        body, name="w_ada_update", grid=(D // tr,),
        in_specs=[pl.BlockSpec((KPAD, tr), lambda i: (0, i)), pl.BlockSpec((KPAD, ADA_N), lambda i: (0, 0)),
                  blk, blk, blk],
        out_specs=[blk] * 4, out_shape=[_sds((D, ADA_N))] * 4,
        compiler_params=_cparams(("parallel",)),
    )(c8p, dm, w, m, v)


def _me():
    return lax.axis_index("x"), lax.axis_index("y"), lax.axis_index("c")


def _peer(k, me):
    mx, my, mc = me
    return (1 - mx if k & 4 else mx, 1 - my if k & 2 else my, 1 - mc if k & 1 else mc)


def _lin(p):
    return 4 * p[0] + 2 * p[1] + p[2]


def _remote(src, dst, ssem, rsem, dev):
    return pltpu.make_async_remote_copy(src_ref=src, dst_ref=dst, send_sem=ssem, recv_sem=rsem,
                                        device_id=dev, device_id_type=MESH)


def _all_gather8(x, name):
    def body(x_ref, out_ref, send_sems, recv_sems):
        me = _me()
        out_ref[_lin(me)] = x_ref[...]
        sends = []
        for k in range(1, N_DEV):
            cp = _remote(x_ref, out_ref.at[_lin(me)], send_sems.at[k - 1], recv_sems.at[k - 1], _peer(k, me))
            cp.start()
            sends.append(cp)
        for k in range(1, N_DEV):
            p = _peer(k, me)
            _remote(x_ref, out_ref.at[_lin(p)], send_sems.at[k - 1], recv_sems.at[k - 1], p).wait_recv()
        for cp in sends:
            cp.wait_send()

    return pl.pallas_call(
        body, name=name,
        out_shape=_sds((N_DEV,) + x.shape, x.dtype),
        in_specs=[pl.BlockSpec(memory_space=pltpu.VMEM)],
        out_specs=pl.BlockSpec(memory_space=pltpu.VMEM),
        scratch_shapes=[pltpu.SemaphoreType.DMA((N_DEV - 1,)), pltpu.SemaphoreType.DMA((N_DEV - 1,))],
    )(x)


def _ag8_plan(src, out, send_sems, recv_sems):
    me = _me()
    sends, recvs = [], []
    for k in range(1, N_DEV):
        p = _peer(k, me)
        sends.append(_remote(src, out.at[_lin(me)], send_sems.at[k - 1], recv_sems.at[k - 1], p))
        recvs.append(_remote(src, out.at[_lin(p)], send_sems.at[k - 1], recv_sems.at[k - 1], p))
    return [], sends, recvs


def _prologue(c_row, conv_sh, w_ada, b_sh, w_in_sh):
    def body(c_ref, cv_ref, wa_ref, b_ref, win_ref, call_ref, cvall_ref, mods_ref, ain_ref, c16_scr, mp_scr,
             c_send, c_recv, cv_send, cv_recv, m_send, m_recv, w_send, w_recv, w_local):
        me = _lin(_me())
        w_plan = _gather_half_plan([win_ref], [ain_ref], w_send, w_recv, w_local)
        _start(w_plan)
        c_plan = _ag8_plan(c_ref, call_ref, c_send, c_recv)
        cv_plan = _ag8_plan(cv_ref, cvall_ref, cv_send, cv_recv)
        call_ref[me] = c_ref[...]
        cvall_ref[me] = cv_ref[...]
        _start(c_plan)
        _start(cv_plan)
        _finish(c_plan)
        c16_scr[...] = jnp.zeros_like(c16_scr)
        for d in range(N_DEV):
            c16_scr[pl.ds(d, 1), :] = call_ref[d]
        mp_scr[...] = _raw1(_silu(c16_scr[...]), wa_ref[...], _NN) + b_ref[...]
        mods_ref[me] = mp_scr[...]
        m_plan = _ag8_plan(mp_scr, mods_ref, m_send, m_recv)
        _start(m_plan)
        _finish(cv_plan)
        _finish(m_plan)
        _finish(w_plan)

    vmem = pl.BlockSpec(memory_space=pltpu.VMEM)
    sems = lambda n: pltpu.SemaphoreType.DMA((n,))
    return pl.pallas_call(
        body, name="prologue",
        in_specs=[vmem] * 4 + _hbm_specs(1), out_specs=[vmem] * 3 + _hbm_specs(1),
        out_shape=[_sds((N_DEV,) + c_row.shape), _sds((N_DEV,) + conv_sh.shape), _sds((N_DEV, 16, ADA_N)),
                   _sds((N_CHIP,) + w_in_sh.shape, w_in_sh.dtype)],
        scratch_shapes=[pltpu.VMEM((16, D), F32), pltpu.VMEM((16, ADA_N), F32)] + [sems(N_DEV - 1)] * 6
                       + _gather_sems(1),
        compiler_params=_cparams(),
    )(c_row, conv_sh, w_ada, b_sh, w_in_sh)


def _hbm_specs(n):
    return [pl.BlockSpec(memory_space=pl.ANY)] * n


def _gather_shapes(shards):
    return [_sds((N_CHIP,) + s.shape, s.dtype) for s in shards]


def _gather_sems(n):
    return [pltpu.SemaphoreType.DMA((3 * n,)), pltpu.SemaphoreType.DMA((3 * n,)), pltpu.SemaphoreType.DMA((n,))]


def _gather_plan(ins, outs, send_sems, recv_sems, local_sems):
    mx, my, mc = _me()
    chips = [(1 - mx, my), (mx, 1 - my), (1 - mx, 1 - my)]
    local, sends, recvs = [], [], []
    for a in range(len(ins)):
        local.append(pltpu.make_async_copy(ins[a], outs[a].at[2 * mx + my], local_sems.at[a]))
        for k, (px, py) in enumerate(chips):
            sems = (send_sems.at[3 * a + k], recv_sems.at[3 * a + k], (px, py, mc))
            sends.append(_remote(ins[a], outs[a].at[2 * mx + my], *sems))
            recvs.append(_remote(ins[a], outs[a].at[2 * px + py], *sems))
    return local, sends, recvs


def _gather_half_plan(ins, outs, send_sems, recv_sems, local_sems):
    mx, my, mc = _me()
    chips = [(1 - mx, my), (mx, 1 - my), (1 - mx, 1 - my)]
    local, sends, recvs = [], [], []
    for a in range(len(ins)):
        h = ins[a].shape[0] // 2
        mine = pl.ds(pl.multiple_of(mc * h, 16), h)
        local.append(pltpu.make_async_copy(ins[a], outs[a].at[2 * mx + my], local_sems.at[a]))
        for k, (px, py) in enumerate(chips):
            sems = (send_sems.at[3 * a + k], recv_sems.at[3 * a + k], (px, py, mc))
            sends.append(_remote(ins[a].at[mine], outs[a].at[2 * mx + my, mine], *sems))
            recvs.append(_remote(ins[a].at[mine], outs[a].at[2 * px + py, mine], *sems))
    return local, sends, recvs


def _sibling_fill(pieces):
    h = pieces.shape[1] // 2

    def body(p_ref, o_ref, send_sems, recv_sems):
        mx, my, mc = _me()
        sib = (mx, my, 1 - mc)
        chips = [(1 - mx, my), (mx, 1 - my), (1 - mx, 1 - my)]
        half = lambda c: pl.ds(pl.multiple_of(c * h, 16), h)
        o_ref[2 * mx + my] = p_ref[2 * mx + my]
        sends = []
        for k, (px, py) in enumerate(chips):
            j = 2 * px + py
            o_ref[j, half(mc), :] = p_ref[j, half(mc), :]
            cp = _remote(p_ref.at[j, half(mc)], o_ref.at[j, half(mc)], send_sems.at[k], recv_sems.at[k], sib)
            cp.start()
            sends.append(cp)
        for k, (px, py) in enumerate(chips):
            j = 2 * px + py
            _remote(p_ref.at[j, half(mc)], o_ref.at[j, half(1 - mc)], send_sems.at[k], recv_sems.at[k],
                    sib).wait_recv()
        for cp in sends:
            cp.wait_send()

    vmem = pl.BlockSpec(memory_space=pltpu.VMEM)
    return pl.pallas_call(
        body, name="sibling_fill", out_shape=_sds(pieces.shape, pieces.dtype),
        in_specs=[vmem], out_specs=vmem,
        scratch_shapes=[pltpu.SemaphoreType.DMA((N_CHIP - 1,)), pltpu.SemaphoreType.DMA((N_CHIP - 1,))],
        compiler_params=_cparams(),
    )(pieces)


def _start(plan):
    local, sends, _ = plan
    for cp in local + sends:
        cp.start()


def _finish(plan):
    local, sends, recvs = plan
    for cp in recvs:
        cp.wait_recv()
    for cp in sends:
        cp.wait_send()
    for cp in local:
        cp.wait()


def _exchange_shapes(pieces):
    return [_sds((N_DEV,) + p.shape[2:], p.dtype) for p in pieces]


def _exchange_sems(n):
    return [pltpu.SemaphoreType.DMA(((N_DEV - 1) * n,)), pltpu.SemaphoreType.DMA(((N_DEV - 1) * n,)),
            pltpu.SemaphoreType.DMA((n,))]


def _exchange_plan(ins, outs, send_sems, recv_sems, local_sems):
    me = _me()
    mx, my, mc = me
    local, sends, recvs = [], [], []
    for a in range(len(ins)):
        local.append(pltpu.make_async_copy(ins[a].at[2 * mx + my, mc], outs[a].at[_lin(me)], local_sems.at[a]))
        for k in range(1, N_DEV):
            p = _peer(k, me)
            s = (N_DEV - 1) * a + k - 1
            sends.append(_remote(ins[a].at[2 * p[0] + p[1], p[2]], outs[a].at[_lin(me)], send_sems.at[s],
                                 recv_sems.at[s], p))
            recvs.append(_remote(ins[a].at[2 * mx + my, mc], outs[a].at[_lin(p)], send_sems.at[s],
                                 recv_sems.at[s], p))
    return local, sends, recvs


REDUCE_VMEM = 56 * 1024 * 1024


def _reduce_swap(recvs):
    n = len(recvs)

    def body(*refs):
        r_refs, o_refs = refs[:n], refs[n:2 * n]
        send_sems, recv_sems = refs[2 * n:]
        mx, my, mc = _me()
        sib = (mx, my, 1 - mc)
        half = lambda a, c: o_refs[a].at[pl.ds(pl.multiple_of(c * recvs[a].shape[1], 8), recvs[a].shape[1])]
        sends = []
        for a in range(n):
            g = r_refs[a][0].astype(F32)
            for s in range(1, N_DEV):
                g = g + r_refs[a][s].astype(F32)
            half(a, mc)[...] = g
            cp = _remote(half(a, mc), half(a, mc), send_sems.at[a], recv_sems.at[a], sib)
            cp.start()
            sends.append(cp)
        for a in range(n):
            _remote(half(a, mc), half(a, 1 - mc), send_sems.at[a], recv_sems.at[a], sib).wait_recv()
        for cp in sends:
            cp.wait_send()

    vmem = pl.BlockSpec(memory_space=pltpu.VMEM)
    return pl.pallas_call(
        body, name="reduce_swap", out_shape=[_sds((2 * r.shape[1], r.shape[2])) for r in recvs],
        in_specs=[vmem] * n, out_specs=[vmem] * n,
        scratch_shapes=[pltpu.SemaphoreType.DMA((n,)), pltpu.SemaphoreType.DMA((n,))],
        compiler_params=_cparams(None, REDUCE_VMEM),
    )(*recvs)


def _adamw_big(g, w, m, v, name):
    rows, cols = g.shape
    tr = next((t for t in (256, 176, 128, 64, 8) if rows % t == 0), None)
    if tr is None:
        tc = _tile(cols, 256)
        blk, grid = pl.BlockSpec((rows, tc), lambda i: (0, i)), (cols // tc,)
    else:
        blk, grid = pl.BlockSpec((tr, cols), lambda i: (i, 0)), (rows // tr,)

    def body(g_ref, w_ref, m_ref, v_ref, go_ref, d_ref, m2_ref, v2_ref):
        g = g_ref[...]
        delta, m2, v2 = _adamw(w_ref[...], g, m_ref[...], v_ref[...])
        go_ref[...] = g
        d_ref[...] = delta
        m2_ref[...] = m2
        v2_ref[...] = v2

    return pl.pallas_call(
        body, name=name, grid=grid,
        in_specs=[blk] * 4, out_specs=[blk] * 4, out_shape=[_sds((rows, cols))] * 4,
        compiler_params=_cparams(("parallel",)),
    )(g, w, m, v)


SMALL_ORDER = (("mod", 6 * D), ("norm1_w", D), ("norm2_w", D), ("conv_w", CONVW * 3 * GW), ("a_log", GH),
               ("dt_bias", GH), ("gdn_norm_w", HD), ("q_norm_w", HD), ("k_norm_w", HD), ("sinks", SQH), ("loss", 1))
SMALL_R = 120


def _pack_small(d):
    parts = [d[k].reshape(-1).astype(F32) if k in d else jnp.zeros((n,), F32) for k, n in SMALL_ORDER]
    used = sum(n for _, n in SMALL_ORDER)
    parts.append(jnp.zeros((SMALL_R * LANE - used,), F32))
    return jnp.concatenate(parts).reshape(SMALL_R, LANE)


def _unpack_small(pk):
    flat = pk.reshape(-1)
    out, r = {}, 0
    for k, n in SMALL_ORDER:
        out[k] = flat[r:r + n]
        r += n
    return out


def kernel(x, c, w_ada, b_ada, norm1_w, w_in, conv_w, a_log, dt_bias, gdn_norm_w, q_norm_w, k_norm_w, sinks, w_out, norm2_w, w_gate, w_up, w_down, loss_target, m_w_ada, m_b_ada, m_norm1_w, m_w_in, m_conv_w, m_a_log, m_dt_bias, m_gdn_norm_w, m_q_norm_w, m_k_norm_w, m_sinks, m_w_out, m_norm2_w, m_w_gate, m_w_up, m_w_down, v_w_ada, v_b_ada, v_norm1_w, v_w_in, v_conv_w, v_a_log, v_dt_bias, v_gdn_norm_w, v_q_norm_w, v_k_norm_w, v_sinks, v_w_out, v_norm2_w, v_w_gate, v_w_up, v_w_down):
    mx, my, mc = _me()
    chip = 2 * mx + my
    dev = 4 * mx + 2 * my + mc
    T = x.shape[1]

    as_rows = lambda t, transposed: t[0].T if transposed else t[0]
    transposed = (True, False, True, True, False)
    big_w = [as_rows(t, tr) for t, tr in zip((w_in, w_out, w_gate, w_up, w_down), transposed)]
    shards = [t.astype(BF16) for t in big_w]

    b_sh = lax.dynamic_slice(b_ada, (0, chip * ADA_N), (1, ADA_N))
    w_in_sh = jnp.pad(shards[0], ((0, W_IN_ROWS_PAD - W_IN_ROWS), (0, 0)))
    c_all, conv_all, mods, a_in = _prologue(c, conv_w.reshape(CONVW, 3 * GW // N_CHIP), w_ada[0], b_sh, w_in_sh)
    c8 = c_all.reshape(N_DEV, D)
    conv_full = jnp.concatenate([conv_all[2 * j] for j in range(N_CHIP)], axis=1)
    mod = jnp.concatenate([lax.dynamic_slice(mods[2 * j], (dev, 0), (1, ADA_N)) for j in range(N_CHIP)], axis=1)
    w_in_pt = _permute_w_in_t(_sibling_fill(a_in))

    loss, grad_x, big, small = _local_step(
        x[0], loss_target[0], mod, norm1_w, w_in_pt, conv_full, a_log, dt_bias, gdn_norm_w,
        q_norm_w, k_norm_w, sinks, norm2_w, shards[1:])

    small["loss"] = loss[:, :1]
    sg = _all_gather8(_pack_small(small), "gather_small_grads")
    rep = dict(mod=(b_ada, m_b_ada, v_b_ada), norm1_w=(norm1_w, m_norm1_w, v_norm1_w),
               norm2_w=(norm2_w, m_norm2_w, v_norm2_w), a_log=(a_log, m_a_log, v_a_log),
               dt_bias=(dt_bias, m_dt_bias, v_dt_bias), gdn_norm_w=(gdn_norm_w, m_gdn_norm_w, v_gdn_norm_w),
               q_norm_w=(q_norm_w, m_q_norm_w, v_q_norm_w), k_norm_w=(k_norm_w, m_k_norm_w, v_k_norm_w),
               sinks=(sinks, m_sinks, v_sinks))
    wmv = [_pack_small({k: t[i] for k, t in rep.items()}) for i in range(3)]
    sres = _reduce_adamw(sg, wmv[0], wmv[1], wmv[2], "small_reduce_adamw")
    s_g, s_d, s_m, s_v = [_unpack_small(sres[i]) for i in range(4)]
    loss_out = s_g["loss"][0]

    g_conv = lax.dynamic_slice(s_g["conv_w"].reshape(CONVW, 3 * GW), (0, chip * (3 * GW // N_CHIP)),
                               (CONVW, 3 * GW // N_CHIP))
    pad16 = lambda t: jnp.concatenate([t.reshape(12, LANE), jnp.zeros((4, LANE), F32)], axis=0)
    cres = _adamw_call(pad16(g_conv), pad16(conv_w), pad16(m_conv_w), pad16(v_conv_w), "conv_adamw")
    conv_out = [g_conv.reshape(conv_w.shape)] + [cres[i, :12].reshape(conv_w.shape) for i in range(3)]

    dmod8 = sg[:, :6 * D // LANE].reshape(N_DEV, 6 * D)
    dm = lax.dynamic_slice(dmod8, (0, chip * ADA_N), (N_DEV, ADA_N))
    zpad = lambda t: jnp.concatenate([t, jnp.zeros((KPAD - N_DEV, t.shape[1]), F32)], axis=0)
    ares = _w_ada_update(zpad(c8), zpad(dm), w_ada[0], m_w_ada[0], v_w_ada[0])

    names = ("w_in", "w_out", "w_gate", "w_up", "w_down")
    g_full = list(_reduce_swap(big))
    g_full[0] = g_full[0][:W_IN_ROWS]
    big_m = [as_rows(t, tr) for t, tr in zip((m_w_in, m_w_out, m_w_gate, m_w_up, m_w_down), transposed)]
    big_v = [as_rows(t, tr) for t, tr in zip((v_w_in, v_w_out, v_w_gate, v_w_up, v_w_down), transposed)]
    upd = [_adamw_big(g, w, m, v, "adamw_" + nm) for g, w, m, v, nm in zip(g_full, big_w, big_m, big_v, names)]
    back = lambda t, tr: (t.T if tr else t)[None]
    bg, bd, bm, bv = [[back(u[i], tr) for u, tr in zip(upd, transposed)] for i in range(4)]

    def group(a_i, small_d, conv_i, big_l):
        s = lambda k, ref: small_d[k].reshape(ref.shape)
        return [ares[a_i][None], s("mod", b_ada), s("norm1_w", norm1_w), big_l[0], conv_out[conv_i],
                s("a_log", a_log), s("dt_bias", dt_bias), s("gdn_norm_w", gdn_norm_w), s("q_norm_w", q_norm_w),
                s("k_norm_w", k_norm_w), s("sinks", sinks), big_l[1], s("norm2_w", norm2_w), big_l[2], big_l[3],
                big_l[4]]

    outs = [loss_out, grad_x[None]]
    outs += group(0, s_g, 0, bg) + group(1, s_d, 1, bd) + group(2, s_m, 2, bm) + group(3, s_v, 3, bv)
    return tuple(outs)
```

```python
import jax
import jax.numpy as jnp
from jax import lax
from jax.experimental import pallas as pl
from jax.experimental.pallas import tpu as pltpu

F32 = jnp.float32
BF16 = jnp.bfloat16
MESH = pl.DeviceIdType.MESH

D = 1024
HD = 64
GH = 8
GW = GH * HD
SQH = 8
SKVH = 2
SGRP = SQH // SKVH
WIN = 128
CONVW = 4
CHUNK = 64
DFF = 2816
PROJ = 2832
NP = 3072
EPS = 1e-6
N_DEV = 8
N_CHIP = 4

ADAM_LR = 0.001
ADAM_B1 = 0.9
ADAM_B2 = 0.999
ADAM_EPS = 1e-08
ADAM_WD = 0.01
ADAM_STEP = 10

VMEM_LIMIT = 48 * 1024 * 1024
GDN_BWD_VMEM = 58 * 1024 * 1024
LANE = 128


def _cparams(sem=None, vmem=VMEM_LIMIT):
    return pltpu.CompilerParams(dimension_semantics=sem, vmem_limit_bytes=vmem)


_NN = ((1,), (0,))
_NT = ((1,), (1,))
_TN = ((0,), (0,))


def _dot(a, b, dims):
    if a.ndim == 3:
        (ca,), (cb,) = dims
        return lax.dot_general(a, b, (((ca + 1,), (cb + 1,)), ((0,), (0,))), preferred_element_type=F32)
    return lax.dot_general(a, b, (dims, ((), ())), preferred_element_type=F32)


def _raw1(a, b, dims):
    return _dot(a.astype(BF16), b.astype(BF16), dims)


def _raw3(a, b, dims):
    ah = a.astype(BF16)
    al = (a - ah.astype(F32)).astype(BF16)
    bh = b.astype(BF16)
    bl = (b - bh.astype(F32)).astype(BF16)
    return _dot(ah, bh, dims) + (_dot(al, bh, dims) + _dot(ah, bl, dims))


def _make_diff_mm(raw):
    @jax.custom_vjp
    def nn(a, b):
        return raw(a, b, _NN)

    @jax.custom_vjp
    def nt(a, b):
        return raw(a, b, _NT)

    @jax.custom_vjp
    def tn(a, b):
        return raw(a, b, _TN)

    nn.defvjp(lambda a, b: (raw(a, b, _NN), (a, b)), lambda r, g: (nt(g, r[1]), tn(r[0], g)))
    nt.defvjp(lambda a, b: (raw(a, b, _NT), (a, b)), lambda r, g: (nn(g, r[1]), tn(g, r[0])))
    tn.defvjp(lambda a, b: (raw(a, b, _TN), (a, b)), lambda r, g: (nt(r[1], g), nn(r[0], g)))
    return nn, nt, tn


def _tri_inv_raw(a, nn3):
    n = a.shape[-1]
    ri = lax.broadcasted_iota(jnp.int32, (n, n), 0)
    ci = lax.broadcasted_iota(jnp.int32, (n, n), 1)
    t = (ri == ci).astype(F32)
    for lvl in range((n - 1).bit_length()):
        same_pair = (ri >> (lvl + 1)) == (ci >> (lvl + 1))
        lower_left = (((ri >> lvl) & 1) == 1) & (((ci >> lvl) & 1) == 0)
        y = jnp.where(same_pair & lower_left, a, 0.0)
        t = t - y if lvl == 0 else t - nn3(nn3(t, y), t)
    return t


class _Kit:
    def __init__(self, diff):
        if diff:
            self.nn, self.nt, self.tn = _make_diff_mm(_raw1)
            self.nn3, self.nt3, self.tn3 = _make_diff_mm(_raw3)
            nn3, nt3, tn3 = self.nn3, self.nt3, self.tn3

            @jax.custom_vjp
            def inv(a, t):
                return t

            def inv_fwd(a, t):
                return t, t

            def inv_bwd(t, g):
                return -tn3(t, nt3(g, t)), jnp.zeros_like(t)

            inv.defvjp(inv_fwd, inv_bwd)
            self.inv = inv
        else:
            self.nn = lambda a, b: _raw1(a, b, _NN)
            self.nt = lambda a, b: _raw1(a, b, _NT)
            self.tn = lambda a, b: _raw1(a, b, _TN)
            self.nn3 = lambda a, b: _raw3(a, b, _NN)
            self.nt3 = lambda a, b: _raw3(a, b, _NT)
            self.tn3 = lambda a, b: _raw3(a, b, _TN)
            self.inv = lambda a, t: _tri_inv_raw(a, self.nn3) if t is None else t


def _sigmoid(x):
    return 1.0 / (1.0 + jnp.exp(-x))


def _silu(x):
    return x * _sigmoid(x)


def _rms(x, w):
    return x * lax.rsqrt(jnp.mean(x * x, axis=-1, keepdims=True) + EPS) * w


def _tile(dim, target):
    t = (min(dim, target) // LANE) * LANE
    while t >= LANE:
        if dim % t == 0:
            return t
        t -= LANE
    return dim


MM_TM, MM_TN, MM_TK = 1408, 1536, 1408


def _matmul(a, b, ta=False, tb=False, out_dtype=F32, name="matmul", gather=None, exchange=None):
    carried = gather if gather is not None else exchange if exchange is not None else []
    nc = len(carried)
    if ta:
        K, M = a.shape
    else:
        M, K = a.shape
    if tb:
        N, K2 = b.shape
    else:
        K2, N = b.shape
    assert K == K2, (a.shape, b.shape, ta, tb)
    tm, tn, tk = _tile(M, MM_TM), _tile(N, MM_TN), _tile(K, MM_TK)
    nk = K // tk
    dims = ((0,) if ta else (1,), (1,) if tb else (0,))

    grid = (M // tm, N // tn, nk)

    def body(*refs):
        a_ref, b_ref = refs[:2]
        o_ref = refs[2 + nc]
        scratch = refs[3 + 2 * nc:]
        k = pl.program_id(2)
        if nc:
            make_plan = _gather_plan if gather is not None else _exchange_plan
            plan = make_plan(refs[2:2 + nc], refs[3 + nc:3 + 2 * nc], *scratch[-3:])
            at = lambda pos: ((pl.program_id(0) == pos[0]) & (pl.program_id(1) == pos[1]) & (k == pos[2]))

            @pl.when(at((0, 0, 0)))
            def _():
                _start(plan)

        part = _dot(a_ref[...].astype(BF16), b_ref[...].astype(BF16), dims)
        if nk == 1:
            o_ref[...] = part.astype(o_ref.dtype)
        else:
            acc_ref = scratch[0]

            @pl.when(k == 0)
            def _():
                acc_ref[...] = part

            @pl.when((k > 0) & (k < nk - 1))
            def _():
                acc_ref[...] += part

            @pl.when(k == nk - 1)
            def _():
                o_ref[...] = (acc_ref[...] + part).astype(o_ref.dtype)

        if nc:
            @pl.when(at((grid[0] - 1, grid[1] - 1, nk - 1)))
            def _():
                _finish(plan)

    a_spec = (pl.BlockSpec((tk, tm), lambda i, j, k: (k, i)) if ta
              else pl.BlockSpec((tm, tk), lambda i, j, k: (i, k)))
    b_spec = (pl.BlockSpec((tn, tk), lambda i, j, k: (j, k)) if tb
              else pl.BlockSpec((tk, tn), lambda i, j, k: (k, j)))
    if gather is not None:
        c_shapes, c_sems = _gather_shapes(carried), _gather_sems(nc)
    elif exchange is not None:
        c_shapes, c_sems = _exchange_shapes(carried), _exchange_sems(nc)
    else:
        c_shapes, c_sems = [], []
    res = pl.pallas_call(
        body, name=name, grid=grid,
        in_specs=[a_spec, b_spec] + _hbm_specs(nc),
        out_specs=[pl.BlockSpec((tm, tn), lambda i, j, k: (i, j))] + _hbm_specs(nc),
        out_shape=[jax.ShapeDtypeStruct((M, N), out_dtype)] + c_shapes,
        scratch_shapes=([pltpu.VMEM((tm, tn), F32)] if nk > 1 else []) + c_sems,
        compiler_params=_cparams(("arbitrary",) * 3 if nc else ("parallel", "parallel", "arbitrary")),
    )(a, b, *carried)
    return (res[0], res[1:]) if nc else res[0]


def _sds(shape, dtype=F32):
    return jax.ShapeDtypeStruct(shape, dtype)


def _norm_mod(x, nw, scale, shift):
    return _rms(x, nw) * (1.0 + scale) + shift


IN_PROJ_VMEM = 56 * 1024 * 1024


def _norm_in_proj(x, nw, scale, shift, w_in_pt, shards):
    T = x.shape[0]
    N = w_in_pt.shape[0]
    tm, tn = _tile(T, 1024), 3 * GW
    nm, nn = T // tm, N // tn
    nz = (GAB0 - 3 * GW) // HD
    ns = len(shards)

    def body(*refs):
        x_ref, nw_ref, sc_ref, sh_ref, w_ref = refs[:5]
        h_ref, o_ref, zs_ref = refs[5 + ns:8 + ns]
        plan = _gather_plan(refs[5:5 + ns], refs[8 + ns:8 + 2 * ns], *refs[8 + 2 * ns:])
        i, j = pl.program_id(0), pl.program_id(1)

        @pl.when((i == 0) & (j == 0))
        def _():
            _start(plan)

        @pl.when(j == 0)
        def _():
            for r0 in range(0, tm, ROWS_EPI):
                rows = pl.ds(r0, ROWS_EPI)
                h_ref[rows, :] = _norm_mod(x_ref[rows, :], nw_ref[...], sc_ref[...], sh_ref[...]).astype(BF16)

        o = _dot(h_ref[...], w_ref[pl.ds(pl.multiple_of(j * tn, tn), tn), :], _NT)
        o_ref[...] = o

        @pl.when(j == 1)
        def _():
            for p in range(nz // 2):
                zs_ref[2 * p], zs_ref[2 * p + 1] = _split_pair(o[:, p * LANE:(p + 1) * LANE])

        @pl.when((i == nm - 1) & (j == nn - 1))
        def _():
            _finish(plan)

    vec = pl.BlockSpec((1, D), lambda i, j: (0, 0))
    res = pl.pallas_call(
        body, name="norm1_in_proj", grid=(nm, nn),
        in_specs=[pl.BlockSpec((tm, D), lambda i, j: (i, 0)), vec, vec, vec,
                  pl.BlockSpec((N, D), lambda i, j: (0, 0), pipeline_mode=pl.Buffered(1))] + _hbm_specs(ns),
        out_specs=[pl.BlockSpec((tm, D), lambda i, j: (i, 0)), pl.BlockSpec((tm, tn), lambda i, j: (i, j)),
                   pl.BlockSpec((nz, tm, HD), lambda i, j: (0, i, 0))] + _hbm_specs(ns),
        out_shape=[_sds((T, D), BF16), _sds((T, N)), _sds((nz, T, HD))] + _gather_shapes(shards),
        scratch_shapes=_gather_sems(ns),
        compiler_params=_cparams(("arbitrary", "arbitrary"), IN_PROJ_VMEM),
    )(x, nw, scale, shift, w_in_pt, *shards)
    return res[0], res[1], res[2], res[3:]


ROWS_TM = 512
ROWS_EPI = 256


def _matmul_rows(a, b, epi, tiled, consts, out_tiled, out_acc, name, pieces=()):
    T, K = a.shape
    tm, tk = _tile(T, ROWS_TM), _tile(K, MM_TK)
    nm, nk = T // tm, K // tk
    npc, nt, ncst, no, na = len(pieces), len(tiled), len(consts), len(out_tiled), len(out_acc)
    n_in = 2 + nt + ncst

    def body(*refs):
        a_ref, b_ref = refs[:2]
        t_refs, c_refs = refs[2:2 + nt], refs[2 + nt:n_in]
        o_refs = refs[n_in + npc:n_in + npc + no]
        acc_refs = refs[n_in + npc + no:n_in + npc + no + na]
        n_out = no + na + npc
        res_ref = refs[n_in + npc + n_out]
        plan = _exchange_plan(refs[n_in:n_in + npc], refs[n_in + npc + no + na:n_in + npc + n_out],
                              *refs[n_in + npc + n_out + 1:]) if npc else None
        i, k = pl.program_id(0), pl.program_id(1)

        @pl.when((i == 0) & (k == 0))
        def _():
            for r in acc_refs:
                r[...] = jnp.zeros_like(r)
            if npc:
                _start(plan)

        part = _dot(a_ref[...], b_ref[pl.ds(pl.multiple_of(k * tk, tk), tk), :], _NN)

        @pl.when(k == 0)
        def _():
            res_ref[...] = part

        @pl.when(k > 0)
        def _():
            res_ref[...] += part

        @pl.when(k == nk - 1)
        def _():
            for r0 in range(0, tm, ROWS_EPI):
                rows = pl.ds(r0, ROWS_EPI)
                outs = epi(res_ref[rows, :], *[r[rows, :] for r in t_refs], *[r[...] for r in c_refs])
                for r, v in zip(o_refs, outs[:no]):
                    r[rows, :] = v.astype(r.dtype)
                for r, v in zip(acc_refs, outs[no:]):
                    r[...] += v

        if npc:
            @pl.when((i == nm - 1) & (k == nk - 1))
            def _():
                _finish(plan)

    row = lambda w: pl.BlockSpec((tm, w), lambda i, k: (i, 0))
    whole = lambda s: pl.BlockSpec(s.shape, lambda i, k: (0, 0))
    res = pl.pallas_call(
        body, name=name, grid=(nm, nk),
        in_specs=[pl.BlockSpec((tm, tk), lambda i, k: (i, k)),
                  pl.BlockSpec((K, D), lambda i, k: (0, 0), pipeline_mode=pl.Buffered(1))]
                 + [row(t.shape[1]) for t in tiled] + [whole(c) for c in consts] + _hbm_specs(npc),
        out_specs=[row(s.shape[1]) for s in out_tiled] + [whole(s) for s in out_acc] + _hbm_specs(npc),
        out_shape=list(out_tiled) + list(out_acc) + (_exchange_shapes(pieces) if npc else []),
        scratch_shapes=[pltpu.VMEM((tm, D), F32)] + (_exchange_sems(npc) if npc else []),
        compiler_params=_cparams(("arbitrary", "arbitrary")),
    )(a, b, *tiled, *consts, *pieces)
    return res[:no + na], res[no + na:]


def _in_proj_dx_norm_bwd(dproj, w_in_pt, x, dres, nw, scale, shift, pieces):
    T = x.shape[0]

    def epi(dh, x, dres, nw, scale, shift):
        _, vjp = jax.vjp(_norm_mod, x, nw, scale, shift)
        dx, dnw, dsc, dsh = vjp(dh)
        return dx + dres, dnw, dsc, dsh

    return _matmul_rows(dproj, w_in_pt, epi, [x, dres], [nw, scale, shift], [_sds((T, D))], [_sds((1, D))] * 3,
                        "in_proj_dx_norm1_bwd", pieces)


def _out_proj_resid_norm(o_hm, w_out, x, gate1, nw, scale, shift):
    T = x.shape[0]
    nheads = o_hm.shape[0]
    tm = _tile(T, ROWS_TM)

    def body(o_ref, w_ref, x_ref, g_ref, nw_ref, sc_ref, sh_ref, cat_ref, mixed_ref, x1_ref, h2_ref):
        cat = jnp.concatenate([_merge_pair(o_ref[2 * p], o_ref[2 * p + 1]) for p in range(nheads // 2)], axis=1)
        cat_ref[...] = cat.astype(BF16)
        mixed_ref[...] = _dot(cat_ref[...], w_ref[...], _NN)
        for r0 in range(0, tm, ROWS_EPI):
            rows = pl.ds(r0, ROWS_EPI)
            x1, h2 = _resid_norm(x_ref[rows, :], mixed_ref[rows, :], g_ref[...], nw_ref[...], sc_ref[...], sh_ref[...])
            x1_ref[rows, :] = x1
            h2_ref[rows, :] = h2.astype(BF16)

    row = pl.BlockSpec((tm, D), lambda i: (i, 0))
    vec = pl.BlockSpec((1, D), lambda i: (0, 0))
    return pl.pallas_call(
        body, name="out_proj_resid_norm2", grid=(T // tm,),
        in_specs=[pl.BlockSpec((nheads, tm, HD), lambda i: (0, i, 0)), pl.BlockSpec((D, D), lambda i: (0, 0)), row,
                  vec, vec, vec, vec],
        out_specs=[row, row, row, row],
        out_shape=[_sds((T, D), BF16), _sds((T, D)), _sds((T, D)), _sds((T, D), BF16)],
        compiler_params=_cparams(("parallel",)),
    )(o_hm, w_out, x, gate1, nw, scale, shift)


def _ffn_up_dx_resid_bwd(dab, w_gut, x, mixed, dy, gate1, nw, scale, shift):
    T = x.shape[0]

    def epi(dh2, x, mixed, dy, gate1, nw, scale, shift):
        _, vjp = jax.vjp(_resid_norm, x, mixed, gate1, nw, scale, shift)
        return vjp((dy, dh2))

    outs, _ = _matmul_rows(dab, w_gut, epi, [x, mixed, dy], [gate1, nw, scale, shift],
                           [_sds((T, D)), _sds((T, D), BF16)], [_sds((1, D))] * 4, "ffn_up_dx_resid_norm2_bwd")
    return outs


def _ffn_down_loss(act, w_down, x1, target, gate2):
    T = x1.shape[0]

    def epi(ffn, x1, target, gate2):
        y = x1 + gate2 * ffn
        err = y - target
        loss = 0.5 * jnp.sum(jnp.sum(err * err, axis=1, keepdims=True), axis=0, keepdims=True) / D
        dy = err * (1.0 / D)
        return dy, gate2 * dy, jnp.sum(dy * ffn, axis=0, keepdims=True), jnp.broadcast_to(loss, (1, LANE))

    outs, _ = _matmul_rows(act, w_down, epi, [x1, target], [gate2], [_sds((T, D)), _sds((T, D), BF16)],
                           [_sds((1, D)), _sds((1, LANE))], "ffn_down_loss")
    return outs


def _resid_norm(x, mixed, gate1, nw, scale, shift):
    x1 = x + gate1 * mixed
    return x1, _norm_mod(x1, nw, scale, shift)


FFN_BLK = 256
FFN_TM = 2048
AB_SLOTS = 3


def _interleave_gate_up(gate_t, up_t):
    blocks = lambda t: t.reshape(DFF // FFN_BLK, 1, FFN_BLK, D)
    return jnp.concatenate([blocks(gate_t), blocks(up_t)], axis=1).reshape(2 * DFF, D)


def _split_gate_up(g):
    g = g.reshape(DFF // FFN_BLK, 2, FFN_BLK, D)
    return g[:, 0].reshape(DFF, D), g[:, 1].reshape(DFF, D)


def _ffn_up_act(h2, w_gut):
    T = h2.shape[0]
    tm = _tile(T, FFN_TM)

    def body(h_ref, w_ref, ab_ref, act_ref):
        ab = _dot(h_ref[...], w_ref[...], _NT)
        ab_ref[...] = ab
        act_ref[...] = (_silu(ab[:, :FFN_BLK]) * ab[:, FFN_BLK:]).astype(act_ref.dtype)

    return pl.pallas_call(
        body, name="ffn_up_act", grid=(T // tm, DFF // FFN_BLK),
        in_specs=[pl.BlockSpec((tm, D), lambda i, j: (i, 0)), pl.BlockSpec((2 * FFN_BLK, D), lambda i, j: (j, 0))],
        out_specs=[pl.BlockSpec((tm, 2 * FFN_BLK), lambda i, j: (i, j)), pl.BlockSpec((tm, FFN_BLK), lambda i, j: (i, j))],
        out_shape=[_sds((T, 2 * DFF)), _sds((T, DFF), BF16)],
        compiler_params=_cparams(("parallel", "parallel")),
    )(h2, w_gut)


def _ffn_down_dx_act(dffn, w_down, ab):
    T = dffn.shape[0]
    tm = _tile(T, FFN_TM)
    nj = DFF // FFN_BLK
    steps = (T // tm) * nj
    assert steps >= AB_SLOTS

    def body(d_ref, w_ref, ab_hbm, o_ref, buf, sem):
        step = pl.program_id(0) * nj + pl.program_id(1)

        def fetch(t):
            rows = pl.ds((t // nj) * tm, tm)
            cols = pl.ds((t % nj) * (2 * FFN_BLK), 2 * FFN_BLK)
            return pltpu.make_async_copy(ab_hbm.at[rows, cols], buf.at[t % AB_SLOTS], sem.at[t % AB_SLOTS])

        @pl.when(step == 0)
        def _():
            for t in range(AB_SLOTS - 1):
                fetch(t).start()

        @pl.when(step + AB_SLOTS - 1 < steps)
        def _():
            fetch(step + AB_SLOTS - 1).start()

        dact = _dot(d_ref[...], w_ref[...], _NT)
        fetch(step).wait()
        ab_ref = buf.at[step % AB_SLOTS]
        a, b = ab_ref[:, :FFN_BLK], ab_ref[:, FFN_BLK:]
        s = _sigmoid(a)
        da = dact * b * (s * (1.0 + a * (1.0 - s)))
        db = dact * (a * s)
        o_ref[...] = jnp.concatenate([da, db], axis=1).astype(o_ref.dtype)

    return pl.pallas_call(
        body, name="ffn_down_dx_act", grid=(T // tm, nj),
        in_specs=[pl.BlockSpec((tm, D), lambda i, j: (i, 0)), pl.BlockSpec((FFN_BLK, D), lambda i, j: (j, 0))]
        + _hbm_specs(1),
        out_specs=pl.BlockSpec((tm, 2 * FFN_BLK), lambda i, j: (i, j)),
        out_shape=_sds((T, 2 * DFF), BF16),
        scratch_shapes=[pltpu.VMEM((AB_SLOTS, tm, 2 * FFN_BLK), F32), pltpu.SemaphoreType.DMA((AB_SLOTS,))],
        compiler_params=_cparams(("arbitrary", "arbitrary")),
    )(dffn, w_down, ab)


def _round_bf16(x):
    return x.astype(BF16).astype(F32)


def _shift_down(x, s, rows):
    if s == 0:
        return x
    return jnp.where(rows >= s, pltpu.roll(x, s, 0), 0.0)


def _shift_up(x, s, rows, T):
    if s == 0:
        return x
    return jnp.where(rows < T - s, pltpu.roll(x, T - s, 0), 0.0)


def _conv_fwd(proj, conv_w):
    T = proj.shape[0]
    ncol = 3 * GW // LANE

    def body(x_ref, w_ref, o_ref):
        x = _round_bf16(x_ref[...])
        rows = lax.broadcasted_iota(jnp.int32, x.shape, 0)
        acc = jnp.zeros_like(x)
        for j in range(CONVW):
            acc = acc + _round_bf16(w_ref[pl.ds(j, 1), :]) * _shift_down(x, CONVW - 1 - j, rows)
        o_ref[0], o_ref[1] = _split_pair(_silu(acc))

    return pl.pallas_call(
        body, name="conv_fwd", grid=(ncol,),
        in_specs=[pl.BlockSpec((T, LANE), lambda j: (0, j)), pl.BlockSpec((CONVW, LANE), lambda j: (0, j))],
        out_specs=pl.BlockSpec((2, T, HD), lambda j: (j, 0, 0)),
        out_shape=_sds((3 * GH, T, HD)),
        compiler_params=_cparams(("parallel",)),
    )(proj, conv_w)


def _split_pair(y):
    return y[:, :HD], pltpu.roll(y, HD, 1)[:, :HD]


def _merge_pair(a, b):
    return jnp.concatenate([a, b], axis=1)


def _merge_all(heads):
    return jnp.concatenate([_merge_pair(heads[2 * p], heads[2 * p + 1]) for p in range(heads.shape[0] // 2)], axis=1)


def _matmul_nt_heads(a, b, name):
    T, K = a.shape
    N = b.shape[0]
    tm = _tile(T, 1024)

    def body(a_ref, b_ref, o_ref):
        res = _dot(a_ref[...], b_ref[...], _NT)
        for p in range(N // LANE):
            o_ref[2 * p], o_ref[2 * p + 1] = _split_pair(res[:, p * LANE:(p + 1) * LANE])

    return pl.pallas_call(
        body, name=name, grid=(T // tm,),
        in_specs=[pl.BlockSpec((tm, K), lambda i: (i, 0)), pl.BlockSpec((N, K), lambda i: (0, 0))],
        out_specs=pl.BlockSpec((N // HD, tm, HD), lambda i: (0, i, 0)),
        out_shape=_sds((N // HD, T, HD)),
        compiler_params=_cparams(("parallel",)),
    )(a, b)


def _conv_bwd(proj, conv_w, dqc, dproj):
    T = proj.shape[0]
    ncol = 3 * GW // LANE

    def body(x_ref, w_ref, d_ref, buf_ref, dx_ref, dw_ref):
        x = _round_bf16(x_ref[...])
        rows = lax.broadcasted_iota(jnp.int32, x.shape, 0)
        xs = [_shift_down(x, CONVW - 1 - j, rows) for j in range(CONVW)]
        w = [_round_bf16(w_ref[pl.ds(j, 1), :]) for j in range(CONVW)]
        pre = jnp.zeros_like(x)
        for j in range(CONVW):
            pre = pre + w[j] * xs[j]
        s = _sigmoid(pre)
        dpre = _round_bf16(_merge_pair(d_ref[0], d_ref[1]) * (s * (1.0 + pre * (1.0 - s))))
        dx = jnp.zeros_like(x)
        for j in range(CONVW):
            dx = dx + w[j] * _shift_up(dpre, CONVW - 1 - j, rows, T)
            dw_ref[pl.ds(j, 1), :] = jnp.sum(dpre * xs[j], axis=0, keepdims=True)
        dx_ref[...] = dx.astype(dx_ref.dtype)

    return pl.pallas_call(
        body, name="conv_bwd", grid=(ncol,),
        in_specs=[pl.BlockSpec((T, LANE), lambda j: (0, j)), pl.BlockSpec((CONVW, LANE), lambda j: (0, j)),
                  pl.BlockSpec((2, T, HD), lambda j: (j, 0, 0))] + _hbm_specs(1),
        out_specs=[pl.BlockSpec((T, LANE), lambda j: (0, j)), pl.BlockSpec((CONVW, LANE), lambda j: (0, j))],
        out_shape=[_sds(dproj.shape, dproj.dtype), _sds((CONVW, 3 * GW))],
        input_output_aliases={3: 0},
        compiler_params=_cparams(("parallel",)),
    )(proj, conv_w, dqc, dproj)


def _gdn_prep(kit, q, k, v, ga, gb, alog, dtb, t_inv=None):
    C = CHUNK
    ri = lax.broadcasted_iota(jnp.int32, (C, C), 0)
    ci = lax.broadcasted_iota(jnp.int32, (C, C), 1)
    causal = ri >= ci
    strict = ri > ci
    eye = (ri == ci).astype(F32)
    lower = causal.astype(F32)
    upper = (ri <= ci).astype(F32)

    a = ga + dtb
    softplus = jnp.maximum(a, 0.0) + jnp.log(1.0 + jnp.exp(-jnp.abs(a)))
    g_row = -jnp.exp(alog) * softplus
    beta_row = _sigmoid(gb)
    g_col = jnp.sum(eye * g_row, axis=2, keepdims=True)
    beta_col = jnp.sum(eye * beta_row, axis=2, keepdims=True)
    G_col = jnp.sum(lower * g_row, axis=2, keepdims=True)
    G_row = jnp.sum(upper * g_col, axis=1, keepdims=True)
    G_last = jnp.sum(g_row, axis=2, keepdims=True)
    decay = jnp.exp(jnp.where(causal, G_col - G_row, -1e30))

    qn = q * lax.rsqrt(jnp.sum(q * q, axis=-1, keepdims=True) + EPS) * (HD ** -0.5)
    kn = k * lax.rsqrt(jnp.sum(k * k, axis=-1, keepdims=True) + EPS)
    kb = kn * beta_col
    A = jnp.where(strict, kit.nt(kb, kn) * decay, 0.0)
    Tm = kit.inv(A, t_inv)
    eG = jnp.exp(G_col)
    u = kit.nn3(Tm, v * beta_col)
    w = kit.nn3(Tm, kb * eG)
    qk = jnp.where(causal, kit.nt(qn, kn) * decay, 0.0)
    q_dec = qn * eG
    k_dec = kn * jnp.exp(G_last - G_col)
    dec = jnp.exp(G_last)
    return u, w, qk, q_dec, k_dec, dec, Tm


def _gdn_out(o, z, nw):
    return _rms(o, nw) * _silu(z)


GDN_CB = 4


def _gdn_specs(T, blk):
    TB = GDN_CB * CHUNK
    seq = lambda grp: pl.BlockSpec((GH, TB, HD), lambda i, grp=grp: (grp, blk(i), 0))
    row = lambda grp: pl.BlockSpec((GH, GDN_CB, 1, CHUNK), lambda i, grp=grp: (grp, blk(i), 0, 0))
    per_head = pl.BlockSpec((GH, 1, CHUNK), lambda i: (0, 0, 0))
    whole = pl.BlockSpec((1, HD), lambda i: (0, 0))
    state = pl.BlockSpec((GH, GDN_CB, HD, HD), lambda i: (0, blk(i), 0, 0))
    return seq, row, per_head, whole, state


def _gdn_load(seq_refs, row_refs, head_refs):
    chunks = lambda r: jnp.concatenate([r[:, pl.ds(cb * CHUNK, CHUNK), :] for cb in range(GDN_CB)], axis=0)
    rows = lambda r: jnp.concatenate([r[:, cb] for cb in range(GDN_CB)], axis=0)
    heads = lambda r: jnp.concatenate([r[...]] * GDN_CB, axis=0)
    return [chunks(r) for r in seq_refs], [rows(r) for r in row_refs], [heads(r) for r in head_refs]


def _gdn_fwd(qkv_hm, zs_hm, gab, alog_b, dtb_b, nw, shards):
    T = qkv_hm.shape[1]
    N = T // CHUNK
    nblk = N // GDN_CB
    ns = len(shards)
    seq, row, per_head, whole, state = _gdn_specs(T, lambda i: i)
    kit = _Kit(False)

    def body(*refs):
        q_ref, k_ref, v_ref, z_ref, ga_ref, gb_ref, al_ref, dt_ref, nw_ref = refs[:9]
        o_ref, S_ref, T_ref = refs[9 + ns:12 + ns]
        S_scr = refs[12 + 2 * ns]
        plan = _gather_plan(refs[9:9 + ns], refs[12 + ns:12 + 2 * ns], *refs[13 + 2 * ns:])

        @pl.when(pl.program_id(0) == 0)
        def _():
            S_scr[...] = jnp.zeros_like(S_scr)
            _start(plan)

        (q, k, v, z), (ga, gb), (al, dt) = _gdn_load((q_ref, k_ref, v_ref, z_ref), (ga_ref, gb_ref), (al_ref, dt_ref))
        u, w, qk, q_dec, k_dec, dec, t_inv = _gdn_prep(kit, q, k, v, ga, gb, al, dt)
        S = S_scr[...]
        for cb in range(GDN_CB):
            hs = slice(cb * GH, (cb + 1) * GH)
            S_ref[:, cb] = S
            T_ref[:, cb] = t_inv[hs]
            v_new = u[hs] - kit.nn(w[hs], S)
            o = kit.nn(q_dec[hs], S) + kit.nn(qk[hs], v_new)
            S = S * dec[hs] + kit.tn(k_dec[hs], v_new)
            o_ref[:, pl.ds(cb * CHUNK, CHUNK), :] = _gdn_out(o, z[hs], nw_ref[...])
        S_scr[...] = S

        @pl.when(pl.program_id(0) == nblk - 1)
        def _():
            _finish(plan)

    res = pl.pallas_call(
        body, name="gdn_fwd", grid=(nblk,),
        in_specs=[seq(0), seq(1), seq(2), seq(0), row(0), row(1), per_head, per_head, whole] + _hbm_specs(ns),
        out_specs=[seq(0), state, state] + _hbm_specs(ns),
        out_shape=[_sds((GH + SQH, T, HD)), _sds((GH, N, HD, HD)), _sds((GH, N, CHUNK, CHUNK))]
                  + _gather_shapes(shards),
        scratch_shapes=[pltpu.VMEM((GH, HD, HD), F32)] + _gather_sems(ns),
        compiler_params=_cparams(("arbitrary",)),
    )(qkv_hm, qkv_hm, qkv_hm, zs_hm, gab, gab, alog_b, dtb_b, nw, *shards)
    return res[0], (res[1], res[2]), res[3:]


def _gdn_bwd(qkv_hm, zs_hm, gab, alog_b, dtb_b, nw, S_all, do, pieces):
    T = qkv_hm.shape[1]
    N = T // CHUNK
    nblk = N // GDN_CB
    npc = len(pieces)
    dkit, kit = _Kit(True), _Kit(False)
    rseq, rrow, per_head, whole, rstate = _gdn_specs(T, lambda i: nblk - 1 - i)

    def body(*refs):
        q_ref, k_ref, v_ref, z_ref, ga_ref, gb_ref, al_ref, dt_ref, nw_ref, S_ref, T_ref, do_ref = refs[:12]
        dqkv_ref, dz_ref, dga_ref, dgb_ref, dal_ref, ddt_ref, dnw_ref = refs[12 + npc:19 + npc]
        dS_scr = refs[19 + 2 * npc]
        plan = _exchange_plan(refs[12:12 + npc], refs[19 + npc:19 + 2 * npc], *refs[20 + 2 * npc:])

        @pl.when(pl.program_id(0) == 0)
        def _():
            dS_scr[...] = jnp.zeros_like(dS_scr)
            dal_ref[...] = jnp.zeros_like(dal_ref)
            ddt_ref[...] = jnp.zeros_like(ddt_ref)
            dnw_ref[...] = jnp.zeros_like(dnw_ref)
            _start(plan)

        (q, k, v, z, dout), (ga, gb), (al, dt) = _gdn_load((q_ref, k_ref, v_ref, z_ref, do_ref), (ga_ref, gb_ref),
                                                          (al_ref, dt_ref))
        S_in = jnp.concatenate([S_ref[:, cb] for cb in range(GDN_CB)], axis=0)
        t_inv = jnp.concatenate([T_ref[:, cb] for cb in range(GDN_CB)], axis=0)
        prep = lambda *a: _gdn_prep(dkit, *a, t_inv=t_inv)[:6]
        (u, w, qk, q_dec, k_dec, dec), prep_vjp = jax.vjp(prep, q, k, v, ga, gb, al, dt)
        v_new = u - kit.nn(w, S_in)
        o = kit.nn(q_dec, S_in) + kit.nn(qk, v_new)
        _, out_vjp = jax.vjp(_gdn_out, o, z, nw_ref[...])
        do, dz, dnw = out_vjp(dout)
        dvn_part = kit.tn(qk, do)
        dS_part = kit.tn(q_dec, do)
        dS = dS_scr[...]
        dS_out, dvn = [None] * GDN_CB, [None] * GDN_CB
        for cb in reversed(range(GDN_CB)):
            hs = slice(cb * GH, (cb + 1) * GH)
            dS_out[cb] = dS
            dvn[cb] = dvn_part[hs] + kit.nn(k_dec[hs], dS)
            dS = dS * dec[hs] + dS_part[hs] - kit.tn(w[hs], dvn[cb])
        dS_scr[...] = dS
        dS_out = jnp.concatenate(dS_out, axis=0)
        dvn = jnp.concatenate(dvn, axis=0)
        ddec = jnp.sum(jnp.sum(S_in * dS_out, axis=2, keepdims=True), axis=1, keepdims=True)
        cts = (dvn, -kit.nt(dvn, S_in), kit.nt(do, v_new), kit.nt(do, S_in), kit.nt(v_new, dS_out), ddec)
        dq, dk, dv, dga, dgb, dal, ddt = prep_vjp(cts)
        lanesum = lambda t: jnp.broadcast_to(jnp.sum(t, axis=2, keepdims=True), t.shape)
        for cb in range(GDN_CB):
            hs = slice(cb * GH, (cb + 1) * GH)
            sl = pl.ds(cb * CHUNK, CHUNK)
            dqkv_ref[pl.ds(0, GH), sl, :] = dq[hs]
            dqkv_ref[pl.ds(GH, GH), sl, :] = dk[hs]
            dqkv_ref[pl.ds(2 * GH, GH), sl, :] = dv[hs]
            dz_ref[sl, :] = _merge_all(dz[hs]).astype(BF16)
            dga_ref[:, cb] = dga[hs]
            dgb_ref[:, cb] = dgb[hs]
            dal_ref[...] += lanesum(dal[hs])
            ddt_ref[...] += lanesum(ddt[hs])
        dnw_ref[...] += dnw

        @pl.when(pl.program_id(0) == nblk - 1)
        def _():
            _finish(plan)

    res = pl.pallas_call(
        body, name="gdn_bwd", grid=(nblk,),
        in_specs=[rseq(0), rseq(1), rseq(2), rseq(0), rrow(0), rrow(1), per_head, per_head, whole, rstate, rstate,
                  rseq(0)] + _hbm_specs(npc),
        out_specs=[pl.BlockSpec((3 * GH, GDN_CB * CHUNK, HD), lambda i: (0, nblk - 1 - i, 0)),
                   pl.BlockSpec((GDN_CB * CHUNK, GW), lambda i: (nblk - 1 - i, 3)), rrow(0),
                   rrow(0), per_head, per_head, whole] + _hbm_specs(npc),
        out_shape=[_sds((3 * GH, T, HD)), _sds((T, NP), BF16)] + [_sds((GH, N, 1, CHUNK))] * 2
                  + [_sds((GH, 1, CHUNK))] * 2 + [_sds((1, HD))] + _exchange_shapes(pieces),
        scratch_shapes=[pltpu.VMEM((GH, HD, HD), F32)] + _exchange_sems(npc),
        compiler_params=_cparams(("arbitrary",), GDN_BWD_VMEM),
    )(qkv_hm, qkv_hm, qkv_hm, zs_hm, gab, gab, alog_b, dtb_b, nw, S_all[0], S_all[1], do, *pieces)
    return res[:7], res[7:]


def _swa_heads(kit, first, q, kp, kc, vp, vc, qnw, knw, sink, slope):
    W = WIN
    ri = lax.broadcasted_iota(jnp.int32, (W, W), 0)
    ci = lax.broadcasted_iota(jnp.int32, (W, W), 1)
    mask_c = ri >= ci
    mask_p = ci > ri + first * W
    dist_c = (ri - ci).astype(F32)
    dist_p = (ri - ci + W).astype(F32)
    kpn = _rms(kp, knw)
    kcn = _rms(kc, knw)
    qn = _rms(q, qnw)
    sc = jnp.where(mask_c, kit.nt(qn, kcn) * (HD ** -0.5) - slope * dist_c, -1e30)
    sp = jnp.where(mask_p, kit.nt(qn, kpn) * (HD ** -0.5) - slope * dist_p, -1e30)
    m = jnp.maximum(jnp.maximum(jnp.max(sc, axis=-1, keepdims=True), jnp.max(sp, axis=-1, keepdims=True)), sink)
    m = lax.stop_gradient(m)
    pc = jnp.exp(sc - m)
    pp = jnp.exp(sp - m)
    den = jnp.sum(pc, axis=-1, keepdims=True) + jnp.sum(pp, axis=-1, keepdims=True) + jnp.exp(sink - m)
    inv = 1.0 / den
    return kit.nn(pc * inv, vc) + kit.nn(pp * inv, vp)


def _swa_grads(kit, first, q, kp, kc, vp, vc, qnw, knw, sink, slope, do):
    W = WIN
    ri = lax.broadcasted_iota(jnp.int32, (W, W), 0)
    ci = lax.broadcasted_iota(jnp.int32, (W, W), 1)
    mask_c = ri >= ci
    mask_p = ci > ri + first * W
    dist_c = (ri - ci).astype(F32)
    dist_p = (ri - ci + W).astype(F32)
    scale = HD ** -0.5
    kpn, kp_vjp = jax.vjp(_rms, kp, knw)
    kcn, kc_vjp = jax.vjp(_rms, kc, knw)
    qn, q_vjp = jax.vjp(_rms, q, qnw)
    sc = jnp.where(mask_c, kit.nt(qn, kcn) * scale - slope * dist_c, -1e30)
    sp = jnp.where(mask_p, kit.nt(qn, kpn) * scale - slope * dist_p, -1e30)
    m = jnp.maximum(jnp.maximum(jnp.max(sc, axis=-1, keepdims=True), jnp.max(sp, axis=-1, keepdims=True)), sink)
    ec = jnp.exp(sc - m)
    ep = jnp.exp(sp - m)
    es = jnp.exp(sink - m)
    inv = 1.0 / (jnp.sum(ec, axis=-1, keepdims=True) + jnp.sum(ep, axis=-1, keepdims=True) + es)
    pc, pp = ec * inv, ep * inv
    dpc, dpp = kit.nt(do, vc), kit.nt(do, vp)
    delta = jnp.sum(dpc * pc, axis=-1, keepdims=True) + jnp.sum(dpp * pp, axis=-1, keepdims=True)
    dsc = pc * (dpc - delta) * scale
    dsp = pp * (dpp - delta) * scale
    dq, dqnw = q_vjp(kit.nn(dsc, kcn) + kit.nn(dsp, kpn))
    dkc, dknw_c = kc_vjp(kit.tn(dsc, qn))
    dkp, dknw_p = kp_vjp(kit.tn(dsp, qn))
    return dq, dkp, dkc, kit.tn(pp, do), kit.tn(pc, do), dqnw, dknw_c + dknw_p, -(es * inv) * delta


def _per_query_head(kv_ref):
    return jnp.concatenate([kv_ref[pl.ds(h // SGRP, 1)] for h in range(SQH)], axis=0)


def _per_kv_head(d):
    return jnp.concatenate([jnp.sum(d[g * SGRP:(g + 1) * SGRP], axis=0, keepdims=True) for g in range(SKVH)], axis=0)


def _swa_specs(blk):
    qspec = pl.BlockSpec((SQH, WIN, HD), lambda i: (1, blk(i), 0))
    cur = lambda grp: pl.BlockSpec((SKVH, WIN, HD), lambda i, grp=grp: (grp, blk(i), 0))
    prev = lambda grp: pl.BlockSpec((SKVH, WIN, HD), lambda i, grp=grp: (grp, jnp.maximum(blk(i) - 1, 0), 0))
    whole = pl.BlockSpec((1, HD), lambda i: (0, 0))
    col = pl.BlockSpec((SQH, WIN, 1), lambda i: (0, 0, 0))
    ospec = pl.BlockSpec((SQH, WIN, HD), lambda i: (0, blk(i), 0))
    return qspec, cur, prev, whole, col, ospec


def _swa_fwd(zs_hm, qnw, knw, sinks_col, slopes_col, o_buf, shards):
    T = zs_hm.shape[1]
    NB = T // WIN
    ns = len(shards)
    kit = _Kit(False)
    qspec, cur, prev, whole, col, _ = _swa_specs(lambda i: i)

    def body(*refs):
        q_ref, kp_ref, kc_ref, vp_ref, vc_ref, qnw_ref, knw_ref, s_ref, sl_ref = refs[:9]
        o_ref = refs[10 + ns]
        plan = _gather_plan(refs[10:10 + ns], refs[11 + ns:11 + 2 * ns], *refs[11 + 2 * ns:])

        @pl.when(pl.program_id(0) == 0)
        def _():
            _start(plan)

        first = (pl.program_id(0) == 0).astype(jnp.int32)
        o_ref[...] = _swa_heads(kit, first, q_ref[...], _per_query_head(kp_ref), _per_query_head(kc_ref),
                                _per_query_head(vp_ref), _per_query_head(vc_ref), qnw_ref[...], knw_ref[...],
                                s_ref[...], sl_ref[...])

        @pl.when(pl.program_id(0) == NB - 1)
        def _():
            _finish(plan)

    res = pl.pallas_call(
        body, name="swa_fwd", grid=(NB,),
        in_specs=[qspec, prev(8), cur(8), prev(9), cur(9), whole, whole, col, col] + _hbm_specs(1 + ns),
        out_specs=[pl.BlockSpec((SQH, WIN, HD), lambda i: (1, i, 0))] + _hbm_specs(ns),
        out_shape=[_sds(o_buf.shape)] + _gather_shapes(shards),
        input_output_aliases={9: 0},
        scratch_shapes=_gather_sems(ns),
        compiler_params=_cparams(("arbitrary",)),
    )(zs_hm, zs_hm, zs_hm, zs_hm, zs_hm, qnw, knw, sinks_col, slopes_col, o_buf, *shards)
    return res[0], res[1:]


def _swa_bwd(zs_hm, qnw, knw, sinks_col, slopes_col, dmix_hm, dproj):
    T = zs_hm.shape[1]
    NB = T // WIN
    kit = _Kit(False)
    qspec, cur, prev, whole, col, _ = _swa_specs(lambda i: NB - 1 - i)
    tail = NP - 4 * GW
    used = (SQH + 2 * SKVH) * HD

    def body(q_ref, kp_ref, kc_ref, vp_ref, vc_ref, qnw_ref, knw_ref, s_ref, sl_ref, do_ref, buf_ref,
             d_ref, dqnw_ref, dknw_ref, ds_ref, ck_scr, cv_scr):
        i = pl.program_id(0)
        first = (i == NB - 1).astype(jnp.int32)

        @pl.when(i == 0)
        def _():
            ck_scr[...] = jnp.zeros_like(ck_scr)
            cv_scr[...] = jnp.zeros_like(cv_scr)
            ds_ref[...] = jnp.zeros_like(ds_ref)
            dqnw_ref[...] = jnp.zeros_like(dqnw_ref)
            dknw_ref[...] = jnp.zeros_like(dknw_ref)

        dq, dkp, dkc, dvp, dvc, dqnw, dknw, dsink = _swa_grads(
            kit, first, q_ref[...], _per_query_head(kp_ref), _per_query_head(kc_ref), _per_query_head(vp_ref),
            _per_query_head(vc_ref), qnw_ref[...], knw_ref[...], s_ref[...], sl_ref[...], do_ref[...])
        dk = _per_kv_head(dkc) + ck_scr[...]
        dv = _per_kv_head(dvc) + cv_scr[...]
        d_ref[...] = jnp.concatenate([_merge_all(dq), _merge_all(dk), _merge_all(dv),
                                      jnp.zeros((WIN, tail - used), F32)], axis=1).astype(BF16)
        ck_scr[...] = _per_kv_head(dkp)
        cv_scr[...] = _per_kv_head(dvp)
        dqnw_ref[...] += dqnw
        dknw_ref[...] += dknw
        ds_ref[...] += jnp.broadcast_to(jnp.sum(dsink, axis=1, keepdims=True), dsink.shape)

    dospec = pl.BlockSpec((SQH, WIN, HD), lambda i: (1, NB - 1 - i, 0))
    dspec = pl.BlockSpec((WIN, tail), lambda i: (NB - 1 - i, 4 * GW // tail))
    res = pl.pallas_call(
        body, name="swa_bwd", grid=(NB,),
        in_specs=[qspec, prev(8), cur(8), prev(9), cur(9), whole, whole, col, col, dospec] + _hbm_specs(1),
        out_specs=[dspec, whole, whole, col],
        out_shape=[_sds(dproj.shape, dproj.dtype), _sds((1, HD)), _sds((1, HD)), _sds((SQH, WIN, 1))],
        input_output_aliases={10: 0},
        scratch_shapes=[pltpu.VMEM((SKVH, WIN, HD), F32), pltpu.VMEM((SKVH, WIN, HD), F32)],
        compiler_params=_cparams(("arbitrary",)),
    )(zs_hm, zs_hm, zs_hm, zs_hm, zs_hm, qnw, knw, sinks_col, slopes_col, dmix_hm, dproj)
    return res


GAB0 = 3 * GW + 1280


W_IN_ROWS = PROJ // N_CHIP
W_IN_ROWS_PAD = 736


def _shard_rows(w_sh, lo, hi):
    out = []
    for j in range(N_CHIP):
        a, b = max(lo, j * W_IN_ROWS), min(hi, (j + 1) * W_IN_ROWS)
        if a < b:
            out.append(w_sh[j, a - j * W_IN_ROWS:b - j * W_IN_ROWS])
    return out


def _permute_w_in_t(w_sh):
    gab = 4 * GW + 2 * GH
    return jnp.concatenate(_shard_rows(w_sh, 0, 4 * GW) + _shard_rows(w_sh, gab, PROJ) + _shard_rows(w_sh, 4 * GW, gab)
                           + [jnp.zeros((NP - PROJ, D), w_sh.dtype)], axis=0)


def _w_in_grad_pieces(g_t):
    g = jnp.concatenate([g_t[:4 * GW], g_t[GAB0:GAB0 + 2 * GH], g_t[4 * GW:GAB0]], axis=0)
    pad = ((0, W_IN_ROWS_PAD - W_IN_ROWS), (0, 0))
    g = jnp.stack([jnp.pad(g[j * W_IN_ROWS:(j + 1) * W_IN_ROWS], pad) for j in range(N_CHIP)])
    return g.reshape(N_CHIP, 2, W_IN_ROWS_PAD // 2, D)


def _pieces_by_rows(g):
    return g.reshape(N_CHIP, 2, g.shape[0] // (2 * N_CHIP), D)


def _local_step(x, target, mod, n1w, w_in_pt, conv_w, alog, dtb, gnw, qnw, knw, sinks, n2w, shards):
    sh_out, sh_gate, sh_up, sh_down = shards
    T = x.shape[0]
    N = T // CHUNK
    shift1, scale1, gate1, shift2, scale2, gate2 = [mod[:, i * D:(i + 1) * D] for i in range(6)]

    h, proj, zs_hm, (a_out,) = _norm_in_proj(x, n1w, scale1, shift1, w_in_pt, [sh_out])
    w_out = a_out.reshape(D, D)
    qkv_hm = _conv_fwd(proj, conv_w)
    gab = proj[:, GAB0:GAB0 + 2 * GH].T.reshape(2 * GH, N, 1, CHUNK)
    alog_b = jnp.broadcast_to(alog.reshape(GH, 1, 1), (GH, 1, CHUNK))
    dtb_b = jnp.broadcast_to(dtb.reshape(GH, 1, 1), (GH, 1, CHUNK))
    sinks_col = jnp.broadcast_to(sinks.reshape(SQH, 1, 1), (SQH, WIN, 1))
    o_hm, S_all, (a_gate, a_up) = _gdn_fwd(qkv_hm, zs_hm, gab, alog_b, dtb_b, gnw, [sh_gate, sh_up])
    w_gut = _interleave_gate_up(a_gate.reshape(DFF, D), a_up.reshape(DFF, D))
    slopes = 2.0 ** (-8.0 * (jnp.arange(SQH, dtype=F32) + 1.0) / SQH)
    slopes_col = jnp.broadcast_to(slopes.reshape(SQH, 1, 1), (SQH, WIN, 1))
    o_hm, (a_down,) = _swa_fwd(zs_hm, qnw, knw, sinks_col, slopes_col, o_hm, [sh_down])
    w_down = a_down.reshape(DFF, D)
    mixcat, mixed, x1, h2 = _out_proj_resid_norm(o_hm, w_out, x, gate1, n2w, scale2, shift2)
    ab, act = _ffn_up_act(h2, w_gut)
    dy, dffn, dgate2, loss = _ffn_down_loss(act, w_down, x1, target, gate2)

    dab = _ffn_down_dx_act(dffn, w_down, ab)
    g_w_down = _matmul(act, dffn, ta=True, out_dtype=BF16, name="ffn_down_dw")
    g_w_gut = _matmul(dab, h2, ta=True, out_dtype=BF16, name="ffn_up_dw")
    dx1, dmixed, dgate1, dn2w, dscale2, dshift2 = _ffn_up_dx_resid_bwd(dab, w_gut, x, mixed, dy, gate1, n2w, scale2,
                                                                       shift2)
    g_w_out = _matmul(mixcat, dmixed, ta=True, out_dtype=BF16, name="out_proj_dw")
    dmix_hm = _matmul_nt_heads(dmixed, w_out, "out_proj_dx")
    g_gate_t, g_up_t = _split_gate_up(g_w_gut)
    pieces = [_pieces_by_rows(g_w_out), _pieces_by_rows(g_gate_t), _pieces_by_rows(g_up_t),
              _pieces_by_rows(g_w_down)]
    (dqkv_hm, dproj, dga, dgb, dalog, ddtb, dgnw), recv = _gdn_bwd(qkv_hm, zs_hm, gab, alog_b, dtb_b, gnw, S_all,
                                                                   dmix_hm, pieces)
    dproj, dqnw, dknw, dsinks = _swa_bwd(zs_hm, qnw, knw, sinks_col, slopes_col, dmix_hm, dproj)
    dproj, dconv = _conv_bwd(proj, conv_w, dqkv_hm, dproj)
    dgab = jnp.concatenate([dga, dgb], axis=0).reshape(2 * GH, T).T.astype(BF16)
    dproj = lax.dynamic_update_slice(dproj, jnp.concatenate([dgab, jnp.zeros((T, NP - PROJ), BF16)], axis=1),
                                     (0, GAB0))
    g_w_in_pt = _matmul(dproj, h, ta=True, out_dtype=BF16, name="in_proj_dw")
    (grad_x, dn1w, dscale1, dshift1), recv_in = _in_proj_dx_norm_bwd(dproj, w_in_pt, x, dx1, n1w, scale1, shift1,
                                                                     [_w_in_grad_pieces(g_w_in_pt)])

    dmod = jnp.concatenate([dshift1, dscale1, dgate1, dshift2, dscale2, dgate2], axis=1)
    big = list(recv_in) + list(recv)
    small = dict(mod=dmod, norm1_w=dn1w, norm2_w=dn2w, conv_w=dconv, a_log=dalog[:, 0, 0], dt_bias=ddtb[:, 0, 0],
                 gdn_norm_w=dgnw, q_norm_w=dqnw, k_norm_w=dknw, sinks=dsinks[:, 0, 0])
    return loss, grad_x, big, small


def _adamw(w, g, m, v):
    m2 = ADAM_B1 * m + (1.0 - ADAM_B1) * g
    v2 = ADAM_B2 * v + (1.0 - ADAM_B2) * (g * g)
    m_hat = m2 / (1.0 - ADAM_B1 ** ADAM_STEP)
    v_hat = v2 / (1.0 - ADAM_B2 ** ADAM_STEP)
    delta = -ADAM_LR * (m_hat / (jnp.sqrt(v_hat) + ADAM_EPS) + ADAM_WD * w)
    return delta, m2, v2


def _reduce_adamw(recv, w, m, v, name):
    _, R, C = recv.shape
    tc = _tile(C, 256)

    def body(r_ref, w_ref, m_ref, v_ref, o_ref):
        g = r_ref[0].astype(F32)
        for s in range(1, N_DEV):
            g = g + r_ref[s].astype(F32)
        delta, m2, v2 = _adamw(w_ref[...], g, m_ref[...], v_ref[...])
        o_ref[0] = g
        o_ref[1] = delta
        o_ref[2] = m2
        o_ref[3] = v2

    col = pl.BlockSpec((R, tc), lambda j: (0, j))
    return pl.pallas_call(
        body, name=name, grid=(C // tc,),
        in_specs=[pl.BlockSpec((N_DEV, R, tc), lambda j: (0, 0, j)), col, col, col],
        out_specs=pl.BlockSpec((4, R, tc), lambda j: (0, 0, j)),
        out_shape=_sds((4, R, C)),
        compiler_params=_cparams(("parallel",)),
    )(recv, w, m, v)


def _adamw_call(g, w, m, v, name):
    def body(g_ref, w_ref, m_ref, v_ref, o_ref):
        delta, m2, v2 = _adamw(w_ref[...], g_ref[...], m_ref[...], v_ref[...])
        o_ref[0] = delta
        o_ref[1] = m2
        o_ref[2] = v2

    return pl.pallas_call(body, name=name, out_shape=_sds((3,) + g.shape))(g, w, m, v)


ADA_N = 6 * D // N_CHIP
KPAD = 128


def _w_ada_update(c8p, dm, w, m, v):
    tr = 256

    def body(c_ref, dm_ref, w_ref, m_ref, v_ref, g_ref, d_ref, m2_ref, v2_ref):
        g = _raw1(_silu(c_ref[...]), dm_ref[...], _TN)
        delta, m2, v2 = _adamw(w_ref[...], g, m_ref[...], v_ref[...])
        g_ref[...] = g
        d_ref[...] = delta
        m2_ref[...] = m2
        v2_ref[...] = v2

    blk = pl.BlockSpec((tr, ADA_N), lambda i: (i, 0))
    return pl.pallas_call(
        body, name="w_ada_update", grid=(D // tr,),
        in_specs=[pl.BlockSpec((KPAD, tr), lambda i: (0, i)), pl.BlockSpec((KPAD, ADA_N), lambda i: (0, 0)),
                  blk, blk, blk],
        out_specs=[blk] * 4, out_shape=[_sds((D, ADA_N))] * 4,
        compiler_params=_cparams(("parallel",)),
    )(c8p, dm, w, m, v)


def _me():
    return lax.axis_index("x"), lax.axis_index("y"), lax.axis_index("c")


def _peer(k, me):
    mx, my, mc = me
    return (1 - mx if k & 4 else mx, 1 - my if k & 2 else my, 1 - mc if k & 1 else mc)


def _lin(p):
    return 4 * p[0] + 2 * p[1] + p[2]


def _remote(src, dst, ssem, rsem, dev):
    return pltpu.make_async_remote_copy(src_ref=src, dst_ref=dst, send_sem=ssem, recv_sem=rsem,
                                        device_id=dev, device_id_type=MESH)


def _all_gather8(x, name):
    def body(x_ref, out_ref, send_sems, recv_sems):
        me = _me()
        out_ref[_lin(me)] = x_ref[...]
        sends = []
        for k in range(1, N_DEV):
            cp = _remote(x_ref, out_ref.at[_lin(me)], send_sems.at[k - 1], recv_sems.at[k - 1], _peer(k, me))
            cp.start()
            sends.append(cp)
        for k in range(1, N_DEV):
            p = _peer(k, me)
            _remote(x_ref, out_ref.at[_lin(p)], send_sems.at[k - 1], recv_sems.at[k - 1], p).wait_recv()
        for cp in sends:
            cp.wait_send()

    return pl.pallas_call(
        body, name=name,
        out_shape=_sds((N_DEV,) + x.shape, x.dtype),
        in_specs=[pl.BlockSpec(memory_space=pltpu.VMEM)],
        out_specs=pl.BlockSpec(memory_space=pltpu.VMEM),
        scratch_shapes=[pltpu.SemaphoreType.DMA((N_DEV - 1,)), pltpu.SemaphoreType.DMA((N_DEV - 1,))],
    )(x)


def _ag8_plan(src, out, send_sems, recv_sems):
    me = _me()
    sends, recvs = [], []
    for k in range(1, N_DEV):
        p = _peer(k, me)
        sends.append(_remote(src, out.at[_lin(me)], send_sems.at[k - 1], recv_sems.at[k - 1], p))
        recvs.append(_remote(src, out.at[_lin(p)], send_sems.at[k - 1], recv_sems.at[k - 1], p))
    return [], sends, recvs


def _prologue(c_row, conv_sh, w_ada, b_sh, w_in_sh):
    def body(c_ref, cv_ref, wa_ref, b_ref, win_ref, call_ref, cvall_ref, mods_ref, ain_ref, c16_scr, mp_scr,
             c_send, c_recv, cv_send, cv_recv, m_send, m_recv, w_send, w_recv, w_local):
        me = _lin(_me())
        w_plan = _gather_half_plan([win_ref], [ain_ref], w_send, w_recv, w_local)
        _start(w_plan)
        c_plan = _ag8_plan(c_ref, call_ref, c_send, c_recv)
        cv_plan = _ag8_plan(cv_ref, cvall_ref, cv_send, cv_recv)
        call_ref[me] = c_ref[...]
        cvall_ref[me] = cv_ref[...]
        _start(c_plan)
        _start(cv_plan)
        _finish(c_plan)
        c16_scr[...] = jnp.zeros_like(c16_scr)
        for d in range(N_DEV):
            c16_scr[pl.ds(d, 1), :] = call_ref[d]
        mp_scr[...] = _raw1(_silu(c16_scr[...]), wa_ref[...], _NN) + b_ref[...]
        mods_ref[me] = mp_scr[...]
        m_plan = _ag8_plan(mp_scr, mods_ref, m_send, m_recv)
        _start(m_plan)
        _finish(cv_plan)
        _finish(m_plan)
        _finish(w_plan)

    vmem = pl.BlockSpec(memory_space=pltpu.VMEM)
    sems = lambda n: pltpu.SemaphoreType.DMA((n,))
    return pl.pallas_call(
        body, name="prologue",
        in_specs=[vmem] * 4 + _hbm_specs(1), out_specs=[vmem] * 3 + _hbm_specs(1),
        out_shape=[_sds((N_DEV,) + c_row.shape), _sds((N_DEV,) + conv_sh.shape), _sds((N_DEV, 16, ADA_N)),
                   _sds((N_CHIP,) + w_in_sh.shape, w_in_sh.dtype)],
        scratch_shapes=[pltpu.VMEM((16, D), F32), pltpu.VMEM((16, ADA_N), F32)] + [sems(N_DEV - 1)] * 6
                       + _gather_sems(1),
        compiler_params=_cparams(),
    )(c_row, conv_sh, w_ada, b_sh, w_in_sh)


def _hbm_specs(n):
    return [pl.BlockSpec(memory_space=pl.ANY)] * n


def _gather_shapes(shards):
    return [_sds((N_CHIP,) + s.shape, s.dtype) for s in shards]


def _gather_sems(n):
    return [pltpu.SemaphoreType.DMA((3 * n,)), pltpu.SemaphoreType.DMA((3 * n,)), pltpu.SemaphoreType.DMA((n,))]


def _gather_plan(ins, outs, send_sems, recv_sems, local_sems):
    mx, my, mc = _me()
    chips = [(1 - mx, my), (mx, 1 - my), (1 - mx, 1 - my)]
    local, sends, recvs = [], [], []
    for a in range(len(ins)):
        local.append(pltpu.make_async_copy(ins[a], outs[a].at[2 * mx + my], local_sems.at[a]))
        for k, (px, py) in enumerate(chips):
            sems = (send_sems.at[3 * a + k], recv_sems.at[3 * a + k], (px, py, mc))
            sends.append(_remote(ins[a], outs[a].at[2 * mx + my], *sems))
            recvs.append(_remote(ins[a], outs[a].at[2 * px + py], *sems))
    return local, sends, recvs


def _gather_half_plan(ins, outs, send_sems, recv_sems, local_sems):
    mx, my, mc = _me()
    chips = [(1 - mx, my), (mx, 1 - my), (1 - mx, 1 - my)]
    local, sends, recvs = [], [], []
    for a in range(len(ins)):
        h = ins[a].shape[0] // 2
        mine = pl.ds(pl.multiple_of(mc * h, 16), h)
        local.append(pltpu.make_async_copy(ins[a], outs[a].at[2 * mx + my], local_sems.at[a]))
        for k, (px, py) in enumerate(chips):
            sems = (send_sems.at[3 * a + k], recv_sems.at[3 * a + k], (px, py, mc))
            sends.append(_remote(ins[a].at[mine], outs[a].at[2 * mx + my, mine], *sems))
            recvs.append(_remote(ins[a].at[mine], outs[a].at[2 * px + py, mine], *sems))
    return local, sends, recvs


def _sibling_fill(pieces):
    h = pieces.shape[1] // 2

    def body(p_ref, o_ref, send_sems, recv_sems):
        mx, my, mc = _me()
        sib = (mx, my, 1 - mc)
        chips = [(1 - mx, my), (mx, 1 - my), (1 - mx, 1 - my)]
        half = lambda c: pl.ds(pl.multiple_of(c * h, 16), h)
        o_ref[2 * mx + my] = p_ref[2 * mx + my]
        sends = []
        for k, (px, py) in enumerate(chips):
            j = 2 * px + py
            o_ref[j, half(mc), :] = p_ref[j, half(mc), :]
            cp = _remote(p_ref.at[j, half(mc)], o_ref.at[j, half(mc)], send_sems.at[k], recv_sems.at[k], sib)
            cp.start()
            sends.append(cp)
        for k, (px, py) in enumerate(chips):
            j = 2 * px + py
            _remote(p_ref.at[j, half(mc)], o_ref.at[j, half(1 - mc)], send_sems.at[k], recv_sems.at[k],
                    sib).wait_recv()
        for cp in sends:
            cp.wait_send()

    vmem = pl.BlockSpec(memory_space=pltpu.VMEM)
    return pl.pallas_call(
        body, name="sibling_fill", out_shape=_sds(pieces.shape, pieces.dtype),
        in_specs=[vmem], out_specs=vmem,
        scratch_shapes=[pltpu.SemaphoreType.DMA((N_CHIP - 1,)), pltpu.SemaphoreType.DMA((N_CHIP - 1,))],
        compiler_params=_cparams(),
    )(pieces)


def _start(plan):
    local, sends, _ = plan
    for cp in local + sends:
        cp.start()


def _finish(plan):
    local, sends, recvs = plan
    for cp in recvs:
        cp.wait_recv()
    for cp in sends:
        cp.wait_send()
    for cp in local:
        cp.wait()


def _exchange_shapes(pieces):
    return [_sds((N_DEV,) + p.shape[2:], p.dtype) for p in pieces]


def _exchange_sems(n):
    return [pltpu.SemaphoreType.DMA(((N_DEV - 1) * n,)), pltpu.SemaphoreType.DMA(((N_DEV - 1) * n,)),
            pltpu.SemaphoreType.DMA((n,))]


def _exchange_plan(ins, outs, send_sems, recv_sems, local_sems):
    me = _me()
    mx, my, mc = me
    local, sends, recvs = [], [], []
    for a in range(len(ins)):
        local.append(pltpu.make_async_copy(ins[a].at[2 * mx + my, mc], outs[a].at[_lin(me)], local_sems.at[a]))
        for k in range(1, N_DEV):
            p = _peer(k, me)
            s = (N_DEV - 1) * a + k - 1
            sends.append(_remote(ins[a].at[2 * p[0] + p[1], p[2]], outs[a].at[_lin(me)], send_sems.at[s],
                                 recv_sems.at[s], p))
            recvs.append(_remote(ins[a].at[2 * mx + my, mc], outs[a].at[_lin(p)], send_sems.at[s],
                                 recv_sems.at[s], p))
    return local, sends, recvs


REDUCE_VMEM = 56 * 1024 * 1024


def _reduce_swap(recvs):
    n = len(recvs)

    def body(*refs):
        r_refs, o_refs = refs[:n], refs[n:2 * n]
        send_sems, recv_sems = refs[2 * n:]
        mx, my, mc = _me()
        sib = (mx, my, 1 - mc)
        half = lambda a, c: o_refs[a].at[pl.ds(pl.multiple_of(c * recvs[a].shape[1], 8), recvs[a].shape[1])]
        sends = []
        for a in range(n):
            g = r_refs[a][0].astype(F32)
            for s in range(1, N_DEV):
                g = g + r_refs[a][s].astype(F32)
            half(a, mc)[...] = g
            cp = _remote(half(a, mc), half(a, mc), send_sems.at[a], recv_sems.at[a], sib)
            cp.start()
            sends.append(cp)
        for a in range(n):
            _remote(half(a, mc), half(a, 1 - mc), send_sems.at[a], recv_sems.at[a], sib).wait_recv()
        for cp in sends:
            cp.wait_send()

    vmem = pl.BlockSpec(memory_space=pltpu.VMEM)
    return pl.pallas_call(
        body, name="reduce_swap", out_shape=[_sds((2 * r.shape[1], r.shape[2])) for r in recvs],
        in_specs=[vmem] * n, out_specs=[vmem] * n,
        scratch_shapes=[pltpu.SemaphoreType.DMA((n,)), pltpu.SemaphoreType.DMA((n,))],
        compiler_params=_cparams(None, REDUCE_VMEM),
    )(*recvs)


def _adamw_big(g, w, m, v, name):
    rows, cols = g.shape
    tr = next((t for t in (256, 176, 128, 64, 8) if rows % t == 0), None)
    if tr is None:
        tc = _tile(cols, 256)
        blk, grid = pl.BlockSpec((rows, tc), lambda i: (0, i)), (cols // tc,)
    else:
        blk, grid = pl.BlockSpec((tr, cols), lambda i: (i, 0)), (rows // tr,)

    def body(g_ref, w_ref, m_ref, v_ref, go_ref, d_ref, m2_ref, v2_ref):
        g = g_ref[...]
        delta, m2, v2 = _adamw(w_ref[...], g, m_ref[...], v_ref[...])
        go_ref[...] = g
        d_ref[...] = delta
        m2_ref[...] = m2
        v2_ref[...] = v2

    return pl.pallas_call(
        body, name=name, grid=grid,
        in_specs=[blk] * 4, out_specs=[blk] * 4, out_shape=[_sds((rows, cols))] * 4,
        compiler_params=_cparams(("parallel",)),
    )(g, w, m, v)


SMALL_ORDER = (("mod", 6 * D), ("norm1_w", D), ("norm2_w", D), ("conv_w", CONVW * 3 * GW), ("a_log", GH),
               ("dt_bias", GH), ("gdn_norm_w", HD), ("q_norm_w", HD), ("k_norm_w", HD), ("sinks", SQH), ("loss", 1))
SMALL_R = 120


def _pack_small(d):
    parts = [d[k].reshape(-1).astype(F32) if k in d else jnp.zeros((n,), F32) for k, n in SMALL_ORDER]
    used = sum(n for _, n in SMALL_ORDER)
    parts.append(jnp.zeros((SMALL_R * LANE - used,), F32))
    return jnp.concatenate(parts).reshape(SMALL_R, LANE)


def _unpack_small(pk):
    flat = pk.reshape(-1)
    out, r = {}, 0
    for k, n in SMALL_ORDER:
        out[k] = flat[r:r + n]
        r += n
    return out


def kernel(x, c, w_ada, b_ada, norm1_w, w_in, conv_w, a_log, dt_bias, gdn_norm_w, q_norm_w, k_norm_w, sinks, w_out, norm2_w, w_gate, w_up, w_down, loss_target, m_w_ada, m_b_ada, m_norm1_w, m_w_in, m_conv_w, m_a_log, m_dt_bias, m_gdn_norm_w, m_q_norm_w, m_k_norm_w, m_sinks, m_w_out, m_norm2_w, m_w_gate, m_w_up, m_w_down, v_w_ada, v_b_ada, v_norm1_w, v_w_in, v_conv_w, v_a_log, v_dt_bias, v_gdn_norm_w, v_q_norm_w, v_k_norm_w, v_sinks, v_w_out, v_norm2_w, v_w_gate, v_w_up, v_w_down):
    mx, my, mc = _me()
    chip = 2 * mx + my
    dev = 4 * mx + 2 * my + mc
    T = x.shape[1]

    as_rows = lambda t, transposed: t[0].T if transposed else t[0]
    transposed = (True, False, True, True, False)
    big_w = [as_rows(t, tr) for t, tr in zip((w_in, w_out, w_gate, w_up, w_down), transposed)]
    shards = [t.astype(BF16) for t in big_w]

    b_sh = lax.dynamic_slice(b_ada, (0, chip * ADA_N), (1, ADA_N))
    w_in_sh = jnp.pad(shards[0], ((0, W_IN_ROWS_PAD - W_IN_ROWS), (0, 0)))
    c_all, conv_all, mods, a_in = _prologue(c, conv_w.reshape(CONVW, 3 * GW // N_CHIP), w_ada[0], b_sh, w_in_sh)
    c8 = c_all.reshape(N_DEV, D)
    conv_full = jnp.concatenate([conv_all[2 * j] for j in range(N_CHIP)], axis=1)
    mod = jnp.concatenate([lax.dynamic_slice(mods[2 * j], (dev, 0), (1, ADA_N)) for j in range(N_CHIP)], axis=1)
    w_in_pt = _permute_w_in_t(_sibling_fill(a_in))

    loss, grad_x, big, small = _local_step(
        x[0], loss_target[0], mod, norm1_w, w_in_pt, conv_full, a_log, dt_bias, gdn_norm_w,
        q_norm_w, k_norm_w, sinks, norm2_w, shards[1:])

    small["loss"] = loss[:, :1]
    sg = _all_gather8(_pack_small(small), "gather_small_grads")
    rep = dict(mod=(b_ada, m_b_ada, v_b_ada), norm1_w=(norm1_w, m_norm1_w, v_norm1_w),
               norm2_w=(norm2_w, m_norm2_w, v_norm2_w), a_log=(a_log, m_a_log, v_a_log),
               dt_bias=(dt_bias, m_dt_bias, v_dt_bias), gdn_norm_w=(gdn_norm_w, m_gdn_norm_w, v_gdn_norm_w),
               q_norm_w=(q_norm_w, m_q_norm_w, v_q_norm_w), k_norm_w=(k_norm_w, m_k_norm_w, v_k_norm_w),
               sinks=(sinks, m_sinks, v_sinks))
    wmv = [_pack_small({k: t[i] for k, t in rep.items()}) for i in range(3)]
    sres = _reduce_adamw(sg, wmv[0], wmv[1], wmv[2], "small_reduce_adamw")
    s_g, s_d, s_m, s_v = [_unpack_small(sres[i]) for i in range(4)]
    loss_out = s_g["loss"][0]

    g_conv = lax.dynamic_slice(s_g["conv_w"].reshape(CONVW, 3 * GW), (0, chip * (3 * GW // N_CHIP)),
                               (CONVW, 3 * GW // N_CHIP))
    pad16 = lambda t: jnp.concatenate([t.reshape(12, LANE), jnp.zeros((4, LANE), F32)], axis=0)
    cres = _adamw_call(pad16(g_conv), pad16(conv_w), pad16(m_conv_w), pad16(v_conv_w), "conv_adamw")
    conv_out = [g_conv.reshape(conv_w.shape)] + [cres[i, :12].reshape(conv_w.shape) for i in range(3)]

    dmod8 = sg[:, :6 * D // LANE].reshape(N_DEV, 6 * D)
    dm = lax.dynamic_slice(dmod8, (0, chip * ADA_N), (N_DEV, ADA_N))
    zpad = lambda t: jnp.concatenate([t, jnp.zeros((KPAD - N_DEV, t.shape[1]), F32)], axis=0)
    ares = _w_ada_update(zpad(c8), zpad(dm), w_ada[0], m_w_ada[0], v_w_ada[0])

    names = ("w_in", "w_out", "w_gate", "w_up", "w_down")
    g_full = list(_reduce_swap(big))
    g_full[0] = g_full[0][:W_IN_ROWS]
    big_m = [as_rows(t, tr) for t, tr in zip((m_w_in, m_w_out, m_w_gate, m_w_up, m_w_down), transposed)]
    big_v = [as_rows(t, tr) for t, tr in zip((v_w_in, v_w_out, v_w_gate, v_w_up, v_w_down), transposed)]
    upd = [_adamw_big(g, w, m, v, "adamw_" + nm) for g, w, m, v, nm in zip(g_full, big_w, big_m, big_v, names)]
    back = lambda t, tr: (t.T if tr else t)[None]
    bg, bd, bm, bv = [[back(u[i], tr) for u, tr in zip(upd, transposed)] for i in range(4)]

    def group(a_i, small_d, conv_i, big_l):
        s = lambda k, ref: small_d[k].reshape(ref.shape)
        return [ares[a_i][None], s("mod", b_ada), s("norm1_w", norm1_w), big_l[0], conv_out[conv_i],
                s("a_log", a_log), s("dt_bias", dt_bias), s("gdn_norm_w", gdn_norm_w), s("q_norm_w", q_norm_w),
                s("k_norm_w", k_norm_w), s("sinks", sinks), big_l[1], s("norm2_w", norm2_w), big_l[2], big_l[3],
                big_l[4]]

    outs = [loss_out, grad_x[None]]
    outs += group(0, s_g, 0, bg) + group(1, s_d, 1, bd) + group(2, s_m, 2, bm) + group(3, s_v, 3, bv)
    return tuple(outs)
```
